```python
import jax
import jax.numpy as jnp
from jax import lax
import numpy as np

D_MODEL = 1024
BATCH = 8
SEQ = 8192
DEPTH = 1

ATT_GROUPS = ((128, 1), (512, 4), (2048, 16))
ATT_HEADS_PER_GROUP = 4
ATT_HEADS = ATT_HEADS_PER_GROUP * len(ATT_GROUPS)
ATT_HEAD_DIM = 128
ATT_BLOCK = 128
ATT_W = ATT_HEADS * ATT_HEAD_DIM
ATT_OUT_W = ATT_HEADS_PER_GROUP * ATT_HEAD_DIM
RET_HEADS = 4
RET_QK_DIM = D_MODEL // RET_HEADS
RET_V_DIM = 2 * D_MODEL // RET_HEADS
RET_QK_W = RET_HEADS * RET_QK_DIM
RET_V_W = RET_HEADS * RET_V_DIM
RET_CHUNK = 128
D_FF = 4 * D_MODEL
IN_SPLITS = (ATT_W, ATT_W, ATT_W, RET_QK_W, RET_QK_W, RET_V_W, RET_V_W, D_MODEL, D_MODEL)
IN_W = sum(IN_SPLITS)
EPS = 1e-6

kernel_name = 'hybrid_dilated_attention_retention_block'


def rmsnorm(x, g):
    xf = x.astype(jnp.float32)
    y = xf * lax.rsqrt(jnp.mean(xf * xf, axis=-1, keepdims=True) + EPS)
    return (y * g.astype(jnp.float32)).astype(x.dtype)


def alibi_slopes(n_heads):
    return jnp.asarray(2.0 ** (-8.0 * np.arange(1, n_heads + 1, dtype=np.float32) / n_heads), dtype=jnp.float32)


def dilated_group(q, k, v, slopes, window, dilation):
    B, S, H, Dh = q.shape
    span = dilation * ATT_BLOCK
    s_pad = -(-S // span) * span
    L = s_pad // dilation
    nb = L // ATT_BLOCK
    reach = window // dilation

    def to_blocks(t):
        t = jnp.pad(t, ((0, 0), (0, s_pad - S), (0, 0), (0, 0)))
        t = t.reshape(B, L, dilation, H, Dh).transpose(0, 2, 3, 1, 4)
        return t.reshape(B, dilation, H, nb, ATT_BLOCK, Dh)

    def with_prev(t):
        prev = jnp.pad(t, ((0, 0), (0, 0), (0, 0), (1, 0), (0, 0), (0, 0)))[:, :, :, :-1]
        return jnp.concatenate([prev, t], axis=4)

    qb = to_blocks(q)
    kb = with_prev(to_blocks(k))
    vb = with_prev(to_blocks(v))
    s = jnp.einsum('brhnqd,brhnkd->brhnqk', qb, kb, preferred_element_type=jnp.float32) * (Dh ** -0.5)
    qi = jnp.arange(ATT_BLOCK)[:, None]
    kj = jnp.arange(2 * ATT_BLOCK)[None, :]
    dist = ATT_BLOCK + qi - kj
    blk = jnp.arange(nb)[:, None, None]
    valid = (dist >= 0) & (dist <= reach) & ((blk - 1) * ATT_BLOCK + kj >= 0)
    bias = -slopes[:, None, None] * (dist * dilation).astype(jnp.float32)
    s = jnp.where(valid[None, None, None], s + bias[None, None, :, None], -jnp.inf)
    m = jnp.max(s, axis=-1)
    p = jnp.exp(s - m[..., None])
    den = jnp.sum(p, axis=-1)
    o = jnp.einsum('brhnqk,brhnkd->brhnqd', p, vb.astype(jnp.float32)) / den[..., None]
    lse = m + jnp.log(den)
    o = o.reshape(B, dilation, H, L, Dh).transpose(0, 3, 1, 2, 4).reshape(B, s_pad, H, Dh)[:, :S]
    lse = lse.reshape(B, dilation, H, L).transpose(0, 3, 1, 2).reshape(B, s_pad, H)[:, :S]
    return o, lse


def retention(q, k, v):
    B, S, H, dk = q.shape
    dv = v.shape[-1]
    C = RET_CHUNK
    N = S // C
    log_g = jnp.log(1.0 - 2.0 ** (-5.0 - jnp.arange(H, dtype=jnp.float32)))
    idx = jnp.arange(C, dtype=jnp.float32)
    diff = idx[:, None] - idx[None, :]
    decay = jnp.where(diff >= 0, jnp.exp(log_g[:, None, None] * jnp.maximum(diff, 0.0)), 0.0)
    xi = jnp.exp(log_g[None, :] * (idx[:, None] + 1.0))
    zeta = jnp.exp(log_g[None, :] * (C - 1.0 - idx[:, None]))
    g_chunk = jnp.exp(log_g * C)
    qc = q.astype(jnp.float32).reshape(B, N, C, H, dk)
    kc = (k.astype(jnp.float32) * (dk ** -0.5)).reshape(B, N, C, H, dk)
    vc = v.astype(jnp.float32).reshape(B, N, C, H, dv)
    s = jnp.einsum('bnqhd,bnkhd->bnhqk', qc, kc) * decay[None, None]
    inner = jnp.einsum('bnhqk,bnkhv->bnqhv', s, vc)
    kz = kc * zeta[None, None, :, :, None]

    def step(state, xs):
        q_i, kz_i, v_i = xs
        cross = jnp.einsum('bqhd,bhdv->bqhv', q_i, state)
        state = state * g_chunk[None, :, None, None] + jnp.einsum('bkhd,bkhv->bhdv', kz_i, v_i)
        return state, cross

    state0 = jnp.zeros((B, H, dk, dv), jnp.float32)
    _, cross = lax.scan(step, state0, (qc.transpose(1, 0, 2, 3, 4), kz.transpose(1, 0, 2, 3, 4), vc.transpose(1, 0, 2, 3, 4)))
    cross = cross.transpose(1, 0, 2, 3, 4) * xi[None, None, :, :, None]
    return (inner + cross).reshape(B, S, H, dv)


def head_groupnorm(o, g, b):
    B, S, H, dv = o.shape
    mu = jnp.mean(o, axis=-1, keepdims=True)
    var = jnp.mean(jnp.square(o - mu), axis=-1, keepdims=True)
    y = ((o - mu) * lax.rsqrt(var + EPS)).reshape(B, S, H * dv)
    return y * g.astype(jnp.float32) + b.astype(jnp.float32)


def _fwd_setup_inputs(seed: int = 0) -> dict:
    key = jax.random.key(seed)
    ks = jax.random.split(key, 14)

    def nrm(k, shape, scale):
        return scale * jax.random.normal(k, shape, jnp.float32)

    return {
        'x': nrm(ks[0], (BATCH, SEQ, D_MODEL), 1.0),
        'norm1_g': 1.0 + nrm(ks[1], (DEPTH, D_MODEL), 0.02),
        'w_in': nrm(ks[2], (DEPTH, D_MODEL, IN_W), D_MODEL ** -0.5),
        'q_norm_g': 1.0 + nrm(ks[3], (DEPTH, ATT_HEADS, ATT_HEAD_DIM), 0.02),
        'k_norm_g': 1.0 + nrm(ks[4], (DEPTH, ATT_HEADS, ATT_HEAD_DIM), 0.02),
        'ret_gn_g': 1.0 + nrm(ks[5], (DEPTH, RET_V_W), 0.02),
        'ret_gn_b': nrm(ks[6], (DEPTH, RET_V_W), 0.02),
        'w_proj_a': nrm(ks[7], (DEPTH, ATT_OUT_W, D_MODEL), ATT_OUT_W ** -0.5),
        'w_proj_b': nrm(ks[8], (DEPTH, RET_V_W, D_MODEL), RET_V_W ** -0.5),
        'w_out': nrm(ks[9], (DEPTH, D_MODEL, D_MODEL), D_MODEL ** -0.5),
        'norm2_g': 1.0 + nrm(ks[10], (DEPTH, D_MODEL), 0.02),
        'w_up': nrm(ks[11], (DEPTH, D_MODEL, D_FF), D_MODEL ** -0.5),
        'w_down': nrm(ks[12], (DEPTH, D_FF, D_MODEL), D_FF ** -0.5),
    }


def _fwd_reference(x, norm1_g, w_in, q_norm_g, k_norm_g, ret_gn_g, ret_gn_b, w_proj_a, w_proj_b, w_out, norm2_g, w_up, w_down):
    B, S, _ = x.shape
    slopes = alibi_slopes(ATT_HEADS)
    bounds = np.cumsum((0,) + IN_SPLITS).tolist()
    for l in range(DEPTH):
        xn = rmsnorm(x, norm1_g[l])
        wl = w_in[l]
        qa, ka, va, qr, kr, vr, gr, gate_a, gate_b = [xn @ wl[:, bounds[i]:bounds[i + 1]] for i in range(len(IN_SPLITS))]
        qa = rmsnorm(qa.reshape(B, S, ATT_HEADS, ATT_HEAD_DIM), q_norm_g[l])
        ka = rmsnorm(ka.reshape(B, S, ATT_HEADS, ATT_HEAD_DIM), k_norm_g[l])
        va = va.reshape(B, S, ATT_HEADS, ATT_HEAD_DIM)
        outs, lses = [], []
        for gi, (window, dilation) in enumerate(ATT_GROUPS):
            hs = slice(gi * ATT_HEADS_PER_GROUP, (gi + 1) * ATT_HEADS_PER_GROUP)
            o, lse = dilated_group(qa[:, :, hs], ka[:, :, hs], va[:, :, hs], slopes[hs], window, dilation)
            outs.append(o)
            lses.append(lse)
        alpha = jax.nn.softmax(jnp.stack(lses, axis=0), axis=0)
        o_a = jnp.sum(alpha[..., None] * jnp.stack(outs, axis=0), axis=0).reshape(B, S, ATT_OUT_W).astype(x.dtype)
        o_r = retention(qr.reshape(B, S, RET_HEADS, RET_QK_DIM), kr.reshape(B, S, RET_HEADS, RET_QK_DIM),
                        vr.reshape(B, S, RET_HEADS, RET_V_DIM))
        o_r = (head_groupnorm(o_r, ret_gn_g[l], ret_gn_b[l]) * jax.nn.silu(gr.astype(jnp.float32))).astype(x.dtype)
        y = jax.nn.sigmoid(gate_a) * (o_a @ w_proj_a[l]) + jax.nn.sigmoid(gate_b) * (o_r @ w_proj_b[l])
        x = x + y @ w_out[l]
        xn2 = rmsnorm(x, norm2_g[l])
        x = x + jnp.square(jax.nn.relu(xn2 @ w_up[l])) @ w_down[l]
    return x


import jax as _jax
import jax.numpy as _jnp

TWIN_FORMAT = 'train_step'
FWD_PARAMS = ['x', 'norm1_g', 'w_in', 'q_norm_g', 'k_norm_g', 'ret_gn_g', 'ret_gn_b', 'w_proj_a', 'w_proj_b', 'w_out', 'norm2_g', 'w_up', 'w_down']
TWIN_WEIGHTS = ['norm1_g', 'w_in', 'q_norm_g', 'k_norm_g', 'ret_gn_g', 'ret_gn_b', 'w_proj_a', 'w_proj_b', 'w_out', 'norm2_g', 'w_up', 'w_down']
TWIN_DIFF_INPUT = 'x'
TWIN_INPUTS = ['x', 'norm1_g', 'w_in', 'q_norm_g', 'k_norm_g', 'ret_gn_g', 'ret_gn_b', 'w_proj_a', 'w_proj_b', 'w_out', 'norm2_g', 'w_up', 'w_down', 'loss_target', 'm_norm1_g', 'm_w_in', 'm_q_norm_g', 'm_k_norm_g', 'm_ret_gn_g', 'm_ret_gn_b', 'm_w_proj_a', 'm_w_proj_b', 'm_w_out', 'm_norm2_g', 'm_w_up', 'm_w_down', 'v_norm1_g', 'v_w_in', 'v_q_norm_g', 'v_k_norm_g', 'v_ret_gn_g', 'v_ret_gn_b', 'v_w_proj_a', 'v_w_proj_b', 'v_w_out', 'v_norm2_g', 'v_w_up', 'v_w_down']
TWIN_OUTPUTS = ['loss', 'grad_x', 'grad_norm1_g', 'grad_w_in', 'grad_q_norm_g', 'grad_k_norm_g', 'grad_ret_gn_g', 'grad_ret_gn_b', 'grad_w_proj_a', 'grad_w_proj_b', 'grad_w_out', 'grad_norm2_g', 'grad_w_up', 'grad_w_down', 'delta_norm1_g', 'delta_w_in', 'delta_q_norm_g', 'delta_k_norm_g', 'delta_ret_gn_g', 'delta_ret_gn_b', 'delta_w_proj_a', 'delta_w_proj_b', 'delta_w_out', 'delta_norm2_g', 'delta_w_up', 'delta_w_down', 'new_m_norm1_g', 'new_m_w_in', 'new_m_q_norm_g', 'new_m_k_norm_g', 'new_m_ret_gn_g', 'new_m_ret_gn_b', 'new_m_w_proj_a', 'new_m_w_proj_b', 'new_m_w_out', 'new_m_norm2_g', 'new_m_w_up', 'new_m_w_down', 'new_v_norm1_g', 'new_v_w_in', 'new_v_q_norm_g', 'new_v_k_norm_g', 'new_v_ret_gn_g', 'new_v_ret_gn_b', 'new_v_w_proj_a', 'new_v_w_proj_b', 'new_v_w_out', 'new_v_norm2_g', 'new_v_w_up', 'new_v_w_down']
TWIN_LEAF_KINDS = {'loss': 'loss', 'grad_x': 'grad_x', 'grad_norm1_g': 'grad_w', 'grad_w_in': 'grad_w', 'grad_q_norm_g': 'grad_w', 'grad_k_norm_g': 'grad_w', 'grad_ret_gn_g': 'grad_w', 'grad_ret_gn_b': 'grad_w', 'grad_w_proj_a': 'grad_w', 'grad_w_proj_b': 'grad_w', 'grad_w_out': 'grad_w', 'grad_norm2_g': 'grad_w', 'grad_w_up': 'grad_w', 'grad_w_down': 'grad_w', 'delta_norm1_g': 'delta_w', 'delta_w_in': 'delta_w', 'delta_q_norm_g': 'delta_w', 'delta_k_norm_g': 'delta_w', 'delta_ret_gn_g': 'delta_w', 'delta_ret_gn_b': 'delta_w', 'delta_w_proj_a': 'delta_w', 'delta_w_proj_b': 'delta_w', 'delta_w_out': 'delta_w', 'delta_norm2_g': 'delta_w', 'delta_w_up': 'delta_w', 'delta_w_down': 'delta_w', 'new_m_norm1_g': 'new_m', 'new_m_w_in': 'new_m', 'new_m_q_norm_g': 'new_m', 'new_m_k_norm_g': 'new_m', 'new_m_ret_gn_g': 'new_m', 'new_m_ret_gn_b': 'new_m', 'new_m_w_proj_a': 'new_m', 'new_m_w_proj_b': 'new_m', 'new_m_w_out': 'new_m', 'new_m_norm2_g': 'new_m', 'new_m_w_up': 'new_m', 'new_m_w_down': 'new_m', 'new_v_norm1_g': 'new_v', 'new_v_w_in': 'new_v', 'new_v_q_norm_g': 'new_v', 'new_v_k_norm_g': 'new_v', 'new_v_ret_gn_g': 'new_v', 'new_v_ret_gn_b': 'new_v', 'new_v_w_proj_a': 'new_v', 'new_v_w_proj_b': 'new_v', 'new_v_w_out': 'new_v', 'new_v_norm2_g': 'new_v', 'new_v_w_up': 'new_v', 'new_v_w_down': 'new_v'}


def _forward(args):
    return _fwd_reference(*[args[k] for k in FWD_PARAMS])


def _output_shape():
    def fwd():
        inp = _fwd_setup_inputs(0)
        return _fwd_reference(*[inp[k] for k in FWD_PARAMS])
    out = _jax.eval_shape(fwd)
    return out.shape, out.dtype

N_MICROBATCH = 1
ADAM_LR = 0.001
ADAM_B1 = 0.9
ADAM_B2 = 0.999
ADAM_EPS = 1e-08
ADAM_WD = 0.01
ADAM_STEP = 10
PER_EXAMPLE_BATCH_AXIS = {'x': 0, 'loss_target': 0}
SHARED_INPUTS = []
_WEIGHT_DTYPES = {'norm1_g': _jnp.float32, 'w_in': _jnp.float32, 'q_norm_g': _jnp.float32, 'k_norm_g': _jnp.float32, 'ret_gn_g': _jnp.float32, 'ret_gn_b': _jnp.float32, 'w_proj_a': _jnp.float32, 'w_proj_b': _jnp.float32, 'w_out': _jnp.float32, 'norm2_g': _jnp.float32, 'w_up': _jnp.float32, 'w_down': _jnp.float32}
MOMENT_SCALE = {'norm1_g': 8.948101e+00, 'w_in': 2.247789e-01, 'q_norm_g': 3.978857e-01, 'k_norm_g': 3.967346e-01, 'ret_gn_g': 3.094473e+00, 'ret_gn_b': 4.423789e+00, 'w_proj_a': 2.081769e-01, 'w_proj_b': 4.046254e-01, 'w_out': 5.404239e-01, 'norm2_g': 1.926925e+02, 'w_up': 1.421478e+00, 'w_down': 1.622033e+01}


def _to_microbatches(a, axis):
    t = _jnp.moveaxis(a, axis, 0)
    t = t.reshape((N_MICROBATCH, t.shape[0] // N_MICROBATCH) + t.shape[1:])
    return _jnp.moveaxis(t, 1, axis + 1)


def setup_inputs(seed: int = 0) -> dict:
    inp = _fwd_setup_inputs(seed)
    key = _jax.random.fold_in(_jax.random.key(seed), 7919)
    shape, _ = _output_shape()
    out = dict(inp)
    out["loss_target"] = _jax.random.normal(_jax.random.fold_in(key, 0), shape, _jnp.float32)
    for i, name in enumerate(TWIN_WEIGHTS):
        w = inp[name].astype(_jnp.float32)
        if MOMENT_SCALE is None:
            s = _jnp.sqrt(_jnp.mean(_jnp.square(w)) + 1e-30)
        else:
            s = MOMENT_SCALE[name]
        km, kv = _jax.random.split(_jax.random.fold_in(key, i + 1))
        out[name] = w
        out["m_" + name] = s * _jax.random.normal(km, w.shape, _jnp.float32)
        out["v_" + name] = (s * s) * _jax.random.uniform(kv, w.shape, _jnp.float32, 0.5, 1.5)
    if N_MICROBATCH > 1:
        for name, axis in PER_EXAMPLE_BATCH_AXIS.items():
            out[name] = _to_microbatches(out[name], axis)
    return {'x': out['x'], 'norm1_g': out['norm1_g'], 'w_in': out['w_in'], 'q_norm_g': out['q_norm_g'], 'k_norm_g': out['k_norm_g'], 'ret_gn_g': out['ret_gn_g'], 'ret_gn_b': out['ret_gn_b'], 'w_proj_a': out['w_proj_a'], 'w_proj_b': out['w_proj_b'], 'w_out': out['w_out'], 'norm2_g': out['norm2_g'], 'w_up': out['w_up'], 'w_down': out['w_down'], 'loss_target': out['loss_target'], 'm_norm1_g': out['m_norm1_g'], 'm_w_in': out['m_w_in'], 'm_q_norm_g': out['m_q_norm_g'], 'm_k_norm_g': out['m_k_norm_g'], 'm_ret_gn_g': out['m_ret_gn_g'], 'm_ret_gn_b': out['m_ret_gn_b'], 'm_w_proj_a': out['m_w_proj_a'], 'm_w_proj_b': out['m_w_proj_b'], 'm_w_out': out['m_w_out'], 'm_norm2_g': out['m_norm2_g'], 'm_w_up': out['m_w_up'], 'm_w_down': out['m_w_down'], 'v_norm1_g': out['v_norm1_g'], 'v_w_in': out['v_w_in'], 'v_q_norm_g': out['v_q_norm_g'], 'v_k_norm_g': out['v_k_norm_g'], 'v_ret_gn_g': out['v_ret_gn_g'], 'v_ret_gn_b': out['v_ret_gn_b'], 'v_w_proj_a': out['v_w_proj_a'], 'v_w_proj_b': out['v_w_proj_b'], 'v_w_out': out['v_w_out'], 'v_norm2_g': out['v_norm2_g'], 'v_w_up': out['v_w_up'], 'v_w_down': out['v_w_down']}


def _loss(weights, diff, rest, loss_target):
    with _jax.named_scope("forward"):
        args = {**rest, TWIN_DIFF_INPUT: diff, **{k: w.astype(_WEIGHT_DTYPES[k]) for k, w in weights.items()}}
        y = _forward(args)
    with _jax.named_scope("loss_head"):
        err = _jnp.square(y.astype(_jnp.float32) - loss_target)
        return 0.5 * _jnp.sum(_jnp.mean(err, axis=-1)) if err.ndim else 0.5 * err


def _adamw(w, g, m, v):
    m = ADAM_B1 * m + (1.0 - ADAM_B1) * g
    v = ADAM_B2 * v + (1.0 - ADAM_B2) * _jnp.square(g)
    m_hat = m / (1.0 - ADAM_B1 ** ADAM_STEP)
    v_hat = v / (1.0 - ADAM_B2 ** ADAM_STEP)
    delta = -ADAM_LR * (m_hat / (_jnp.sqrt(v_hat) + ADAM_EPS) + ADAM_WD * w)
    return delta, m, v


def reference(x, norm1_g, w_in, q_norm_g, k_norm_g, ret_gn_g, ret_gn_b, w_proj_a, w_proj_b, w_out, norm2_g, w_up, w_down, loss_target, m_norm1_g, m_w_in, m_q_norm_g, m_k_norm_g, m_ret_gn_g, m_ret_gn_b, m_w_proj_a, m_w_proj_b, m_w_out, m_norm2_g, m_w_up, m_w_down, v_norm1_g, v_w_in, v_q_norm_g, v_k_norm_g, v_ret_gn_g, v_ret_gn_b, v_w_proj_a, v_w_proj_b, v_w_out, v_norm2_g, v_w_up, v_w_down):
    given = dict(x=x, norm1_g=norm1_g, w_in=w_in, q_norm_g=q_norm_g, k_norm_g=k_norm_g, ret_gn_g=ret_gn_g, ret_gn_b=ret_gn_b, w_proj_a=w_proj_a, w_proj_b=w_proj_b, w_out=w_out, norm2_g=norm2_g, w_up=w_up, w_down=w_down, loss_target=loss_target, m_norm1_g=m_norm1_g, m_w_in=m_w_in, m_q_norm_g=m_q_norm_g, m_k_norm_g=m_k_norm_g, m_ret_gn_g=m_ret_gn_g, m_ret_gn_b=m_ret_gn_b, m_w_proj_a=m_w_proj_a, m_w_proj_b=m_w_proj_b, m_w_out=m_w_out, m_norm2_g=m_norm2_g, m_w_up=m_w_up, m_w_down=m_w_down, v_norm1_g=v_norm1_g, v_w_in=v_w_in, v_q_norm_g=v_q_norm_g, v_k_norm_g=v_k_norm_g, v_ret_gn_g=v_ret_gn_g, v_ret_gn_b=v_ret_gn_b, v_w_proj_a=v_w_proj_a, v_w_proj_b=v_w_proj_b, v_w_out=v_w_out, v_norm2_g=v_norm2_g, v_w_up=v_w_up, v_w_down=v_w_down)
    weights = {n: given[n] for n in TWIN_WEIGHTS}
    shared = {n: given[n] for n in SHARED_INPUTS}
    per_example = {n: given[n] for n in ['x']}
    grad_fn = _jax.value_and_grad(_loss, argnums=(0, 1))

    def one_microbatch(ex, loss_target):
        ex = dict(ex)
        diff = ex.pop(TWIN_DIFF_INPUT)
        return grad_fn(weights, diff, {**shared, **ex}, loss_target)

    if N_MICROBATCH == 1:
        loss, (grad_w, grad_x) = one_microbatch(per_example, given["loss_target"])
    else:
        def body(carry, xs):
            loss_sum, grad_sum = carry
            l_k, (gw_k, gx_k) = one_microbatch(xs[0], xs[1])
            with _jax.named_scope("update"):
                return (loss_sum + l_k, _jax.tree.map(_jnp.add, grad_sum, gw_k)), gx_k

        init = (_jnp.zeros((), _jnp.float32), _jax.tree.map(_jnp.zeros_like, weights))
        (loss, grad_w), grad_x = _jax.lax.scan(body, init, (per_example, given["loss_target"]))
    with _jax.named_scope("update"):
        delta_w, new_m, new_v = {}, {}, {}
        for n in TWIN_WEIGHTS:
            delta_w[n], new_m[n], new_v[n] = _adamw(weights[n], grad_w[n], given["m_" + n], given["v_" + n])
    return (loss, grad_x, *[grad_w[n] for n in TWIN_WEIGHTS], *[delta_w[n] for n in TWIN_WEIGHTS],
            *[new_m[n] for n in TWIN_WEIGHTS], *[new_v[n] for n in TWIN_WEIGHTS])
```

```python
import functools
import math

import jax
import jax.numpy as jnp
from jax import lax
from jax.experimental import pallas as pl
from jax.experimental.pallas import tpu as pltpu

CDT = jnp.bfloat16
F32 = jnp.float32
EPS = 1e-6

ATT_GROUPS = ((128, 1), (512, 4), (2048, 16))
ATT_HPG = 4
ATT_HEADS = 12
HD = 128
BLK = 128
ATT_W = ATT_HEADS * HD
GW = ATT_HPG * HD
RET_HEADS = 4

ADAM_LR = 0.001
ADAM_B1 = 0.9
ADAM_B2 = 0.999
ADAM_EPS = 1e-08
ADAM_WD = 0.01
ADAM_STEP = 10

VMEM_LIMIT_BYTES = 48 * 1024 * 1024
MESH = pl.DeviceIdType.MESH
HBM_SPEC = pl.BlockSpec(memory_space=pltpu.HBM)
VMEM_SPEC = pl.BlockSpec(memory_space=pltpu.VMEM)


def _params(n_axes):
    return pltpu.CompilerParams(dimension_semantics=("arbitrary",) * n_axes,
                                vmem_limit_bytes=VMEM_LIMIT_BYTES)


def _dot_nn(a, b):
    return jnp.dot(a, b, preferred_element_type=F32)


def _dot_nt(a, b):
    return lax.dot_general(a, b, (((1,), (1,)), ((), ())), preferred_element_type=F32)


def _dot_tn(a, b):
    return lax.dot_general(a, b, (((0,), (0,)), ((), ())), preferred_element_type=F32)


def _sigmoid(v):
    return 1.0 / (1.0 + jnp.exp(-v))


def _matmul(name, mode, a, b, *, tm, tn, tk, extras=(), outs, epilogue):
    if mode == "nn":
        (M, K), (K2, N) = a.shape, b.shape
    elif mode == "nt":
        (M, K), (N, K2) = a.shape, b.shape
    else:
        (K, M), (K2, N) = a.shape, b.shape
    assert K == K2 and M % tm == 0 and N % tn == 0 and K % tk == 0, (name, a.shape, b.shape)
    ni, nj, nk = M // tm, N // tn, K // tk
    if mode == "tn":
        a_spec = pl.BlockSpec((tk, tm), lambda i, j, k: (k, i))
    else:
        a_spec = pl.BlockSpec((tm, tk), lambda i, j, k: (i, k))
    if mode == "nt":
        b_spec = pl.BlockSpec((tn, tk), lambda i, j, k: (j, k))
    else:
        b_spec = pl.BlockSpec((tk, tn), lambda i, j, k: (k, j))
    dot = {"nn": _dot_nn, "nt": _dot_nt, "tn": _dot_tn}[mode]
    n_ex, n_out = len(extras), len(outs)

    def body(*refs):
        a_ref, b_ref = refs[0], refs[1]
        ex = refs[2:2 + n_ex]
        out = refs[2 + n_ex:2 + n_ex + n_out]
        acc = refs[-1]
        i = pl.program_id(0)
        k = pl.program_id(2)

        @pl.when(k == 0)
        def _():
            acc[...] = jnp.zeros_like(acc)

        acc[...] += dot(a_ref[...].astype(CDT), b_ref[...].astype(CDT))

        @pl.when(k == nk - 1)
        def _():
            epilogue(acc[...], ex, out, i)

    res = pl.pallas_call(
        body, name=name, grid=(ni, nj, nk),
        in_specs=[a_spec, b_spec] + [s for _, s in extras],
        out_specs=[s for _, s in outs],
        out_shape=[o for o, _ in outs],
        scratch_shapes=[pltpu.VMEM((tm, tn), F32)],
        compiler_params=_params(3),
    )(a, b, *[e for e, _ in extras])
    return res


def _mn(tm, tn, col_off=0):
    return pl.BlockSpec((tm, tn), lambda i, j, k: (i, j + col_off))


def _row(tn):
    return pl.BlockSpec((1, tn), lambda i, j, k: (0, j))


def _ep_store(acc, ex, out, i):
    out[0][...] = acc.astype(out[0].dtype)


def _ep_resid(acc, ex, out, i):
    out[0][...] = ex[0][...] + acc


def _ep_up(acc, ex, out, i):
    out[0][...] = acc.astype(out[0].dtype)
    r = jnp.maximum(acc, 0.0)
    out[1][...] = (r * r).astype(out[1].dtype)


def _ep_down_loss(acc, ex, out, i, inv_d):
    diff = (ex[0][...] + acc) - ex[1][...]
    dx2 = diff * inv_d
    out[0][...] = dx2
    out[1][...] = dx2.astype(out[1].dtype)

    @pl.when(i == 0)
    def _():
        out[2][...] = jnp.zeros_like(out[2])

    out[2][...] += jnp.sum(diff * diff, axis=0, keepdims=True)


def _ep_dh(acc, ex, out, i):
    h = ex[0][...].astype(F32)
    out[0][...] = (acc * (2.0 * jnp.maximum(h, 0.0))).astype(out[0].dtype)


def _ep_rms_bwd(acc, ex, out, i):
    x = ex[0][...]
    g = ex[1][...]
    rstd = lax.rsqrt(jnp.mean(x * x, axis=-1, keepdims=True) + EPS)
    xh = x * rstd
    dxh = acc * g
    dx = ex[2][...] + rstd * (dxh - xh * jnp.mean(dxh * xh, axis=-1, keepdims=True))
    out[0][...] = dx
    out[1][...] = dx.astype(out[1].dtype)

    @pl.when(i == 0)
    def _():
        out[2][...] = jnp.zeros_like(out[2])

    out[2][...] += jnp.sum(acc * xh, axis=0, keepdims=True)


def _ep_gates(acc, ex, out, i):
    sa = _sigmoid(ex[0][...].astype(F32))
    sb = _sigmoid(ex[1][...].astype(F32))
    dpa = acc * sa
    dpb = acc * sb
    out[0][...] = dpa.astype(out[0].dtype)
    out[1][...] = dpb.astype(out[1].dtype)
    out[2][...] = (dpa * ex[2][...].astype(F32) * (1.0 - sa)).astype(out[2].dtype)
    out[3][...] = (dpb * ex[3][...].astype(F32) * (1.0 - sb)).astype(out[3].dtype)


def _sds(shape, dtype):
    return jax.ShapeDtypeStruct(shape, dtype)


def _rms_fwd(name, x, g, tm=512):
    S, D = x.shape

    def body(x_ref, g_ref, o_ref):
        xv = x_ref[...]
        rstd = lax.rsqrt(jnp.mean(xv * xv, axis=-1, keepdims=True) + EPS)
        o_ref[...] = (xv * rstd * g_ref[...]).astype(o_ref.dtype)

    return pl.pallas_call(
        body, name=name, grid=(S // tm,),
        in_specs=[pl.BlockSpec((tm, D), lambda i: (i, 0)), pl.BlockSpec((1, D), lambda i: (0, 0))],
        out_specs=pl.BlockSpec((tm, D), lambda i: (i, 0)),
        out_shape=_sds((S, D), CDT), compiler_params=_params(1))(x, g)


def _qknorm_fwd(proj, gqk, tm=512):
    S = proj.shape[0]
    W = 2 * ATT_W

    def body(p_ref, g_ref, o_ref):
        for hd in range(2 * ATT_HEADS):
            sl = slice(hd * HD, (hd + 1) * HD)
            v = p_ref[:, sl].astype(F32)
            rstd = lax.rsqrt(jnp.mean(v * v, axis=-1, keepdims=True) + EPS)
            o_ref[:, sl] = (v * rstd * g_ref[:, sl]).astype(o_ref.dtype)

    return pl.pallas_call(
        body, name="qknorm_fwd", grid=(S // tm,),
        in_specs=[pl.BlockSpec((tm, W), lambda i: (i, 0)), pl.BlockSpec((1, W), lambda i: (0, 0))],
        out_specs=pl.BlockSpec((tm, W), lambda i: (i, 0)),
        out_shape=_sds((S, W), CDT), compiler_params=_params(1))(proj, gqk)


def _qknorm_bwd(proj, gqk, dqs, dks, dvs, tm=256):
    S = proj.shape[0]
    W = 2 * ATT_W

    def body(p_ref, g_ref, *refs):
        dq_refs, dk_refs, dv_refs = refs[0:3], refs[3:6], refs[6:9]
        o_ref, dg_ref = refs[9], refs[10]
        i = pl.program_id(0)

        @pl.when(i == 0)
        def _():
            dg_ref[...] = jnp.zeros_like(dg_ref)

        for hd in range(2 * ATT_HEADS):
            sl = slice(hd * HD, (hd + 1) * HD)
            head = hd % ATT_HEADS
            grp, slot = head // ATT_HPG, head % ATT_HPG
            src = (dq_refs if hd < ATT_HEADS else dk_refs)[grp]
            dn = src[:, slot * HD:(slot + 1) * HD]
            v = p_ref[:, sl].astype(F32)
            rstd = lax.rsqrt(jnp.mean(v * v, axis=-1, keepdims=True) + EPS)
            vh = v * rstd
            dg_ref[:, sl] += jnp.sum(dn * vh, axis=0, keepdims=True)
            dvh = dn * g_ref[:, sl]
            o_ref[:, sl] = (rstd * (dvh - vh * jnp.mean(dvh * vh, axis=-1, keepdims=True))).astype(o_ref.dtype)
        for grp in range(3):
            o_ref[:, W + grp * GW:W + (grp + 1) * GW] = dv_refs[grp][...].astype(o_ref.dtype)

    g512 = pl.BlockSpec((tm, GW), lambda i: (i, 0))
    return pl.pallas_call(
        body, name="qknorm_bwd", grid=(S // tm,),
        in_specs=[pl.BlockSpec((tm, W), lambda i: (i, 0)), pl.BlockSpec((1, W), lambda i: (0, 0))] + [g512] * 9,
        out_specs=[pl.BlockSpec((tm, 3 * ATT_W), lambda i: (i, 0)), pl.BlockSpec((1, W), lambda i: (0, 0))],
        out_shape=[_sds((S, 3 * ATT_W), CDT), _sds((1, W), F32)],
        compiler_params=_params(1))(proj, gqk, *dqs, *dks, *dvs)


def _att_mask(n):
    qi = lax.broadcasted_iota(jnp.int32, (BLK, 2 * BLK), 0)
    kj = lax.broadcasted_iota(jnp.int32, (BLK, 2 * BLK), 1)
    dist = BLK + qi - kj
    valid = (dist >= 0) & (dist <= BLK) & ((kj >= BLK) | (n > 0))
    return valid, dist.astype(F32)


def _att_slopes(grp):
    return [2.0 ** (-8.0 * (grp * ATT_HPG + hh + 1) / ATT_HEADS) for hh in range(ATT_HPG)]


def _att_in_specs(grp, nb, in_w):
    qpb = 2 * ATT_W // GW
    ppb = in_w // GW
    last = nb - 1

    def cur(n):
        return jnp.minimum(n, last)

    def prev(n):
        return jnp.maximum(jnp.minimum(n, last) - 1, 0)

    blk = (BLK, GW)
    return [
        pl.BlockSpec(blk, lambda r, n: (cur(n), r * qpb + grp)),
        pl.BlockSpec(blk, lambda r, n: (prev(n), r * qpb + 3 + grp)),
        pl.BlockSpec(blk, lambda r, n: (cur(n), r * qpb + 3 + grp)),
        pl.BlockSpec(blk, lambda r, n: (prev(n), r * ppb + 6 + grp)),
        pl.BlockSpec(blk, lambda r, n: (cur(n), r * ppb + 6 + grp)),
    ]


def _att_fwd(grp, qkn, proj):
    S, in_w = proj.shape
    _, d = ATT_GROUPS[grp]
    L = S // d
    nb = L // BLK
    slopes = _att_slopes(grp)
    scale = HD ** -0.5

    def body(q_ref, kp_ref, kc_ref, vp_ref, vc_ref, o_ref, l_ref):
        n = pl.program_id(1)
        valid, distf = _att_mask(n)
        for hh in range(ATT_HPG):
            sl = slice(hh * HD, (hh + 1) * HD)
            k = jnp.concatenate([kp_ref[:, sl], kc_ref[:, sl]], axis=0)
            v = jnp.concatenate([vp_ref[:, sl], vc_ref[:, sl]], axis=0)
            s = _dot_nt(q_ref[:, sl], k) * scale + (-slopes[hh] * d) * distf
            s = jnp.where(valid, s, -1e30)
            m = jnp.max(s, axis=-1, keepdims=True)
            p = jnp.exp(s - m)
            den = jnp.sum(p, axis=-1, keepdims=True)
            o_ref[:, sl] = _dot_nn(p.astype(CDT), v) / den
            l_ref[:, sl] = jnp.broadcast_to(m + jnp.log(den), (BLK, HD))

    out_spec = pl.BlockSpec((BLK, GW), lambda r, n: (n, r))
    o, l = pl.pallas_call(
        body, name="att_fwd_g%d" % grp, grid=(d, nb),
        in_specs=_att_in_specs(grp, nb, in_w),
        out_specs=[out_spec, out_spec],
        out_shape=[_sds((L, d * GW), F32), _sds((L, d * GW), F32)],
        compiler_params=_params(2),
    )(qkn.reshape(L, d * 2 * ATT_W), qkn.reshape(L, d * 2 * ATT_W), qkn.reshape(L, d * 2 * ATT_W),
      proj.reshape(L, d * in_w), proj.reshape(L, d * in_w))
    return o.reshape(S, GW), l.reshape(S, GW)


def _att_bwd(grp, qkn, proj, lse, do_g, c_g):
    S, in_w = proj.shape
    _, d = ATT_GROUPS[grp]
    L = S // d
    nb = L // BLK
    slopes = _att_slopes(grp)
    scale = HD ** -0.5
    last = nb - 1

    def body(q_ref, kp_ref, kc_ref, vp_ref, vc_ref, l_ref, do_ref, c_ref, dq_ref, dk_ref, dv_ref, ck, cv):
        n = pl.program_id(1)

        @pl.when(n == 0)
        def _():
            ck[...] = jnp.zeros_like(ck)
            cv[...] = jnp.zeros_like(cv)

        @pl.when(n < nb)
        def _():
            valid, distf = _att_mask(n)
            for hh in range(ATT_HPG):
                sl = slice(hh * HD, (hh + 1) * HD)
                q = q_ref[:, sl]
                k = jnp.concatenate([kp_ref[:, sl], kc_ref[:, sl]], axis=0)
                v = jnp.concatenate([vp_ref[:, sl], vc_ref[:, sl]], axis=0)
                do = do_ref[:, sl]
                s = _dot_nt(q, k) * scale + (-slopes[hh] * d) * distf
                p = jnp.where(valid, jnp.exp(s - l_ref[:, sl][:, 0:1]), 0.0)
                dp = _dot_nt(do, v)
                ds = (p * (dp + c_ref[:, sl][:, 0:1]) * scale).astype(CDT)
                dq_ref[:, sl] = _dot_nn(ds, k)
                dk = _dot_tn(ds, q)
                dv = _dot_tn(p.astype(CDT), do)
                dk_ref[:, sl] = ck[:, sl] + dk[0:BLK]
                dv_ref[:, sl] = cv[:, sl] + dv[0:BLK]
                ck[:, sl] = dk[BLK:2 * BLK]
                cv[:, sl] = dv[BLK:2 * BLK]

        @pl.when(n == nb)
        def _():
            dk_ref[...] = ck[...]
            dv_ref[...] = cv[...]

    blk = (BLK, GW)
    at_q = pl.BlockSpec(blk, lambda r, n: (jnp.minimum(n, last), r))
    behind = pl.BlockSpec(blk, lambda r, n: (jnp.maximum(n - 1, 0), r))
    qv = qkn.reshape(L, d * 2 * ATT_W)
    pv = proj.reshape(L, d * in_w)
    dq, dk, dv = pl.pallas_call(
        body, name="att_bwd_g%d" % grp, grid=(d, nb + 1),
        in_specs=_att_in_specs(grp, nb, in_w) + [at_q, at_q, at_q],
        out_specs=[at_q, behind, behind],
        out_shape=[_sds((L, d * GW), F32)] * 3,
        scratch_shapes=[pltpu.VMEM(blk, F32), pltpu.VMEM(blk, F32)],
        compiler_params=_params(2),
    )(qv, qv, qv, pv, pv, lse.reshape(L, d * GW), do_g.reshape(L, d * GW), c_g.reshape(L, d * GW))
    return dq.reshape(S, GW), dk.reshape(S, GW), dv.reshape(S, GW)


def _head_sum_bcast(v):
    parts = []
    for j in range(ATT_HPG):
        sj = jnp.sum(v[:, j * HD:(j + 1) * HD], axis=-1, keepdims=True)
        parts.append(jnp.broadcast_to(sj, (v.shape[0], HD)))
    return jnp.concatenate(parts, axis=1)


def _mix_alpha(l_refs):
    l0, l1, l2 = l_refs[0][...], l_refs[1][...], l_refs[2][...]
    mx = jnp.maximum(jnp.maximum(l0, l1), l2)
    e = [jnp.exp(l0 - mx), jnp.exp(l1 - mx), jnp.exp(l2 - mx)]
    tot = e[0] + e[1] + e[2]
    return [ei / tot for ei in e]


def _mix_fwd(os_, ls_, tm=512):
    S = os_[0].shape[0]

    def body(o0, o1, o2, l0, l1, l2, out):
        al = _mix_alpha((l0, l1, l2))
        out[...] = (al[0] * o0[...] + al[1] * o1[...] + al[2] * o2[...]).astype(out.dtype)

    spec = pl.BlockSpec((tm, GW), lambda i: (i, 0))
    return pl.pallas_call(
        body, name="mix_fwd", grid=(S // tm,), in_specs=[spec] * 6, out_specs=spec,
        out_shape=_sds((S, GW), CDT), compiler_params=_params(1))(*os_, *ls_)


def _mix_bwd(os_, ls_, do_a, tm=512):
    S = os_[0].shape[0]

    def body(o0, o1, o2, l0, l1, l2, d_ref, g0, g1, g2, c0, c1, c2):
        al = _mix_alpha((l0, l1, l2))
        dv = d_ref[...]
        o_a = al[0] * o0[...] + al[1] * o1[...] + al[2] * o2[...]
        dsum = _head_sum_bcast(dv * o_a)
        for a, g_ref, c_ref in zip(al, (g0, g1, g2), (c0, c1, c2)):
            g_ref[...] = (a * dv).astype(g_ref.dtype)
            c_ref[...] = -(a * dsum)

    spec = pl.BlockSpec((tm, GW), lambda i: (i, 0))
    res = pl.pallas_call(
        body, name="mix_bwd", grid=(S // tm,), in_specs=[spec] * 7, out_specs=[spec] * 6,
        out_shape=[_sds((S, GW), CDT)] * 3 + [_sds((S, GW), F32)] * 3,
        compiler_params=_params(1))(*os_, *ls_, do_a)
    return res[:3], res[3:]


def _ret_tables(dk):
    H, C = RET_HEADS, BLK
    log_g = jnp.log(1.0 - 2.0 ** (-5.0 - jnp.arange(H, dtype=F32)))
    idx = jnp.arange(C, dtype=F32)
    diff = idx[:, None] - idx[None, :]
    decay = jnp.where(diff >= 0, jnp.exp(log_g[:, None, None] * jnp.maximum(diff, 0.0)), 0.0)
    xi = jnp.exp(log_g[:, None] * (idx[None, :] + 1.0))
    zeta = jnp.exp(log_g[:, None] * (C - 1.0 - idx[None, :])) * (dk ** -0.5)
    g_chunk = jnp.exp(log_g * C)
    bc = lambda t: jnp.broadcast_to(t[:, :, None], (H, C, C))
    return decay, bc(xi), bc(zeta), jnp.broadcast_to(g_chunk[:, None, None], (H, 8, C))


def _gn_fwd(o, g, b):
    mu = jnp.mean(o, axis=-1, keepdims=True)
    xc = o - mu
    rstd = lax.rsqrt(jnp.mean(xc * xc, axis=-1, keepdims=True) + EPS)
    yh = xc * rstd
    return yh, rstd, yh * g + b


def _ret_specs(dk, dv, order):
    qk_w, v_w = RET_HEADS * dk, RET_HEADS * dv
    off_q = 3 * ATT_W
    off_k, off_v, off_g = off_q + qk_w, off_q + 2 * qk_w, off_q + 2 * qk_w + v_w
    assert off_q % dk == 0 and off_k % dk == 0 and off_v % dv == 0 and off_g % dv == 0
    tab = pl.BlockSpec((None, BLK, BLK), lambda h, i: (h, 0, 0))
    return [
        pl.BlockSpec((BLK, dk), lambda h, i: (order(i), off_q // dk + h)),
        pl.BlockSpec((BLK, dk), lambda h, i: (order(i), off_k // dk + h)),
        pl.BlockSpec((BLK, dv), lambda h, i: (order(i), off_v // dv + h)),
        pl.BlockSpec((BLK, dv), lambda h, i: (order(i), off_g // dv + h)),
        tab, tab, tab, pl.BlockSpec((None, 8, BLK), lambda h, i: (h, 0, 0)),
    ]


def _ret_fwd(proj, gn_g, gn_b, dk, dv):
    S = proj.shape[0]
    N = S // BLK
    H = RET_HEADS
    kscale = dk ** -0.5

    def body(q_ref, k_ref, v_ref, gr_ref, dec_ref, xi_ref, zeta_ref, gc_ref, g_ref, b_ref,
             opre_ref, or_ref, st_ref, state):
        n = pl.program_id(1)

        @pl.when(n == 0)
        def _():
            state[...] = jnp.zeros_like(state)

        q, k, v = q_ref[...], k_ref[...], v_ref[...]
        s = _dot_nt(q, k) * kscale * dec_ref[...]
        st = state[...]
        st_c = st.astype(CDT)
        st_ref[...] = st_c
        o = _dot_nn(s.astype(CDT), v) + _dot_nn(q, st_c) * xi_ref[:, 0:1]
        kz = (k.astype(F32) * zeta_ref[:, 0:1]).astype(CDT)
        state[...] = st * gc_ref[0:1, 0:1] + _dot_tn(kz, v)
        opre_ref[...] = o
        _, _, y = _gn_fwd(o, g_ref[...], b_ref[...])
        gr = gr_ref[...].astype(F32)
        or_ref[...] = (y * (gr * _sigmoid(gr))).astype(or_ref.dtype)

    v_w = H * dv
    row = pl.BlockSpec((1, dv), lambda h, i: (0, h))
    tile = pl.BlockSpec((BLK, dv), lambda h, i: (i, h))
    return pl.pallas_call(
        body, name="ret_fwd", grid=(H, N),
        in_specs=_ret_specs(dk, dv, lambda i: i) + [row, row],
        out_specs=[tile, tile, pl.BlockSpec((None, None, dk, dv), lambda h, i: (i, h, 0, 0))],
        out_shape=[_sds((S, v_w), F32), _sds((S, v_w), CDT), _sds((N, H, dk, dv), CDT)],
        scratch_shapes=[pltpu.VMEM((dk, dv), F32)],
        compiler_params=_params(2),
    )(proj, proj, proj, proj, *_ret_tables(dk), gn_g, gn_b)


def _ret_bwd(proj, gn_g, gn_b, o_pre, states, d_or, dk, dv):
    S = proj.shape[0]
    N = S // BLK
    H = RET_HEADS
    qk_w, v_w = H * dk, H * dv
    kscale = dk ** -0.5

    def body(q_ref, k_ref, v_ref, gr_ref, dec_ref, xi_ref, zeta_ref, gc_ref, g_ref, b_ref, opre_ref, st_ref, dor_ref,
             dq_ref, dk_ref, dv_ref, dgr_ref, dg_ref, db_ref, dstate):
        i = pl.program_id(1)

        @pl.when(i == 0)
        def _():
            dstate[...] = jnp.zeros_like(dstate)
            dg_ref[...] = jnp.zeros_like(dg_ref)
            db_ref[...] = jnp.zeros_like(db_ref)

        q, k, v = q_ref[...], k_ref[...], v_ref[...]
        decay, xi, zeta_s = dec_ref[...], xi_ref[:, 0:1], zeta_ref[:, 0:1]
        gr = gr_ref[...].astype(F32)
        sg = _sigmoid(gr)
        gain = g_ref[...]
        yh, rstd, y = _gn_fwd(opre_ref[...], gain, b_ref[...])
        d_or_v = dor_ref[...]
        dy = d_or_v * (gr * sg)
        dgr_ref[...] = (d_or_v * y * (sg * (1.0 + gr * (1.0 - sg)))).astype(dgr_ref.dtype)
        dg_ref[...] += jnp.sum(dy * yh, axis=0, keepdims=True)
        db_ref[...] += jnp.sum(dy, axis=0, keepdims=True)
        dyh = dy * gain
        do = rstd * (dyh - jnp.mean(dyh, axis=-1, keepdims=True) - yh * jnp.mean(dyh * yh, axis=-1, keepdims=True))
        do_c = do.astype(CDT)
        dox = (do * xi).astype(CDT)
        a_c = (_dot_nt(q, k) * kscale * decay).astype(CDT)
        g_c = (_dot_nt(do_c, v) * decay).astype(CDT)
        dsn = dstate[...]
        dsn_c = dsn.astype(CDT)
        kz = (k.astype(F32) * zeta_s).astype(CDT)
        dq_ref[...] = (_dot_nn(g_c, k) * kscale + _dot_nt(dox, st_ref[...])).astype(dq_ref.dtype)
        dk_ref[...] = (_dot_tn(g_c, q) * kscale + _dot_nt(v, dsn_c) * zeta_s).astype(dk_ref.dtype)
        dv_ref[...] = (_dot_tn(a_c, do_c) + _dot_nn(kz, dsn_c)).astype(dv_ref.dtype)
        dstate[...] = dsn * gc_ref[0:1, 0:1] + _dot_tn(q, dox)

    rev = lambda i: N - 1 - i
    row = pl.BlockSpec((1, dv), lambda h, i: (0, h))
    vt = pl.BlockSpec((BLK, dv), lambda h, i: (rev(i), h))
    qt = pl.BlockSpec((BLK, dk), lambda h, i: (rev(i), h))
    return pl.pallas_call(
        body, name="ret_bwd", grid=(H, N),
        in_specs=_ret_specs(dk, dv, rev) + [row, row, vt,
                 pl.BlockSpec((None, None, dk, dv), lambda h, i: (rev(i), h, 0, 0)), vt],
        out_specs=[qt, qt, vt, vt, row, row],
        out_shape=[_sds((S, qk_w), CDT), _sds((S, qk_w), CDT), _sds((S, v_w), CDT), _sds((S, v_w), CDT),
                   _sds((1, v_w), F32), _sds((1, v_w), F32)],
        scratch_shapes=[pltpu.VMEM((dk, dv), F32)],
        compiler_params=_params(2),
    )(proj, proj, proj, proj, *_ret_tables(dk), gn_g, gn_b, o_pre, states, d_or)


def _merge_fwd(o_a, o_r, wa, wb, proj, d_model, tm=512, tn=256):
    S, in_w = proj.shape
    off_a, off_b = in_w - 2 * d_model, in_w - d_model
    assert off_a % tn == 0 and off_b % tn == 0

    def body(oa_ref, or_ref, wa_ref, wb_ref, ga_ref, gb_ref, y_ref, pa_ref, pb_ref):
        pa = _dot_nn(oa_ref[...], wa_ref[...])
        pb = _dot_nn(or_ref[...], wb_ref[...])
        y = _sigmoid(ga_ref[...].astype(F32)) * pa + _sigmoid(gb_ref[...].astype(F32)) * pb
        y_ref[...] = y.astype(y_ref.dtype)
        pa_ref[...] = pa.astype(pa_ref.dtype)
        pb_ref[...] = pb.astype(pb_ref.dtype)

    ka, kb = o_a.shape[1], o_r.shape[1]
    out = pl.BlockSpec((tm, tn), lambda i, j: (i, j))
    return pl.pallas_call(
        body, name="merge_fwd", grid=(S // tm, d_model // tn),
        in_specs=[pl.BlockSpec((tm, ka), lambda i, j: (i, 0)), pl.BlockSpec((tm, kb), lambda i, j: (i, 0)),
                  pl.BlockSpec((ka, tn), lambda i, j: (0, j)), pl.BlockSpec((kb, tn), lambda i, j: (0, j)),
                  pl.BlockSpec((tm, tn), lambda i, j: (i, off_a // tn + j)),
                  pl.BlockSpec((tm, tn), lambda i, j: (i, off_b // tn + j))],
        out_specs=[out, out, out], out_shape=[_sds((S, d_model), CDT)] * 3,
        compiler_params=_params(2))(o_a, o_r, wa, wb, proj, proj)


def _local_step(x, target, w, small):
    S, D = x.shape
    in_w = w["w_in"].shape[1]
    d_ff = w["w_up"].shape[1]
    ret_v_w = w["w_proj_b"].shape[0]
    dv = ret_v_w // RET_HEADS
    dk = (in_w - 3 * ATT_W - 2 * ret_v_w - 2 * D) // (2 * RET_HEADS)
    gqk = jnp.concatenate([small["q_norm_g"].reshape(1, ATT_W), small["k_norm_g"].reshape(1, ATT_W)], axis=1)
    g1, g2 = small["norm1_g"], small["norm2_g"]
    gn_g, gn_b = small["ret_gn_g"], small["ret_gn_b"]

    xn = _rms_fwd("rms1_fwd", x, g1)
    (proj,) = _matmul("in_proj", "nn", xn, w["w_in"], tm=1024, tn=640, tk=D,
                      outs=[(_sds((S, in_w), CDT), _mn(1024, 640))], epilogue=_ep_store)
    qkn = _qknorm_fwd(proj, gqk)
    att = [_att_fwd(g, qkn, proj) for g in range(3)]
    os_, ls_ = [a[0] for a in att], [a[1] for a in att]
    o_a = _mix_fwd(os_, ls_)
    o_pre, o_r, states = _ret_fwd(proj, gn_g, gn_b, dk, dv)
    y, pa, pb = _merge_fwd(o_a, o_r, w["w_proj_a"], w["w_proj_b"], proj, D)
    (x1,) = _matmul("out_proj", "nn", y, w["w_out"], tm=512, tn=D, tk=D,
                    extras=[(x, _mn(512, D))], outs=[(_sds((S, D), F32), _mn(512, D))], epilogue=_ep_resid)
    xn2 = _rms_fwd("rms2_fwd", x1, g2)
    hid, act = _matmul("mlp_up", "nn", xn2, w["w_up"], tm=1024, tn=512, tk=D,
                       outs=[(_sds((S, d_ff), CDT), _mn(1024, 512))] * 2, epilogue=_ep_up)
    dx2, dx2c, loss_row = _matmul(
        "mlp_down_loss", "nn", act, w["w_down"], tm=512, tn=D, tk=1024,
        extras=[(x1, _mn(512, D)), (target, _mn(512, D))],
        outs=[(_sds((S, D), F32), _mn(512, D)), (_sds((S, D), CDT), _mn(512, D)), (_sds((1, D), F32), _row(D))],
        epilogue=functools.partial(_ep_down_loss, inv_d=1.0 / D))
    loss = 0.5 * jnp.sum(loss_row) / D

    (dh,) = _matmul("d_hidden", "nt", dx2c, w["w_down"], tm=1024, tn=512, tk=D,
                    extras=[(hid, _mn(1024, 512))], outs=[(_sds((S, d_ff), CDT), _mn(1024, 512))], epilogue=_ep_dh)
    (gw_down,) = _matmul("dw_down", "tn", act, dx2c, tm=1024, tn=D, tk=1024,
                         outs=[(_sds((d_ff, D), F32), _mn(1024, D))], epilogue=_ep_store)
    (gw_up,) = _matmul("dw_up", "tn", xn2, dh, tm=D, tn=1024, tk=1024,
                       outs=[(_sds((D, d_ff), F32), _mn(D, 1024))], epilogue=_ep_store)
    dx1, dx1c, dg2 = _matmul(
        "d_x1", "nt", dh, w["w_up"], tm=512, tn=D, tk=1024,
        extras=[(x1, _mn(512, D)), (g2, _row(D)), (dx2, _mn(512, D))],
        outs=[(_sds((S, D), F32), _mn(512, D)), (_sds((S, D), CDT), _mn(512, D)), (_sds((1, D), F32), _row(D))],
        epilogue=_ep_rms_bwd)

    gt = 512
    assert (in_w - 2 * D) % gt == 0
    off_a, off_b = (in_w - 2 * D) // gt, (in_w - D) // gt
    dpa, dpb, dga, dgb = _matmul(
        "d_gates", "nt", dx1c, w["w_out"], tm=512, tn=gt, tk=D,
        extras=[(proj, _mn(512, gt, off_a)), (proj, _mn(512, gt, off_b)), (pa, _mn(512, gt)), (pb, _mn(512, gt))],
        outs=[(_sds((S, D), CDT), _mn(512, gt))] * 4, epilogue=_ep_gates)
    (gw_out,) = _matmul("dw_out", "tn", y, dx1c, tm=D, tn=D, tk=1024,
                        outs=[(_sds((D, D), F32), _mn(D, D))], epilogue=_ep_store)
    (do_a,) = _matmul("d_o_a", "nt", dpa, w["w_proj_a"], tm=1024, tn=GW, tk=D,
                      outs=[(_sds((S, GW), F32), _mn(1024, GW))], epilogue=_ep_store)
    (d_or,) = _matmul("d_o_r", "nt", dpb, w["w_proj_b"], tm=1024, tn=512, tk=D,
                      outs=[(_sds((S, ret_v_w), F32), _mn(1024, 512))], epilogue=_ep_store)
    (gw_pa,) = _matmul("dw_proj_a", "tn", o_a, dpa, tm=GW, tn=D, tk=1024,
                       outs=[(_sds((GW, D), F32), _mn(GW, D))], epilogue=_ep_store)
    (gw_pb,) = _matmul("dw_proj_b", "tn", o_r, dpb, tm=1024, tn=D, tk=1024,
                       outs=[(_sds((ret_v_w, D), F32), _mn(1024, D))], epilogue=_ep_store)

    dq_r, dk_r, dv_r, dgr, dgn_g, dgn_b = _ret_bwd(proj, gn_g, gn_b, o_pre, states, d_or, dk, dv)
    do_gs, c_gs = _mix_bwd(os_, ls_, do_a)
    datt_parts = [_att_bwd(g, qkn, proj, ls_[g], do_gs[g], c_gs[g]) for g in range(3)]
    d_att, dgqk = _qknorm_bwd(proj, gqk, [p[0] for p in datt_parts], [p[1] for p in datt_parts],
                              [p[2] for p in datt_parts])
    dproj = jnp.concatenate([d_att, dq_r, dk_r, dv_r, dgr, dga, dgb], axis=1)

    (gw_in,) = _matmul("dw_in", "tn", xn, dproj, tm=D, tn=640, tk=1024,
                       outs=[(_sds((D, in_w), F32), _mn(D, 640))], epilogue=_ep_store)
    grad_x, _, dg1 = _matmul(
        "d_x", "nt", dproj, w["w_in"], tm=512, tn=D, tk=1280,
        extras=[(x, _mn(512, D)), (g1, _row(D)), (dx1, _mn(512, D))],
        outs=[(_sds((S, D), F32), _mn(512, D)), (_sds((S, D), CDT), _mn(512, D)), (_sds((1, D), F32), _row(D))],
        epilogue=_ep_rms_bwd)

    big = {"w_in": gw_in, "w_proj_a": gw_pa, "w_proj_b": gw_pb, "w_out": gw_out, "w_up": gw_up, "w_down": gw_down}
    smallg = {"norm1_g": dg1, "q_norm_g": dgqk[:, :ATT_W], "k_norm_g": dgqk[:, ATT_W:],
              "ret_gn_g": dgn_g, "ret_gn_b": dgn_b, "norm2_g": dg2}
    return loss, grad_x, big, smallg


N_CHIPS = 4
N_DEV = 8


def _place():
    x, y, c = lax.axis_index("x"), lax.axis_index("y"), lax.axis_index("c")
    return x, y, c


def _other_chips(x, y):
    out = []
    for fx, fy in ((1, 0), (0, 1), (1, 1)):
        px = 1 - x if fx else x
        py = 1 - y if fy else y
        out.append(((px, py), 2 * px + py))
    return out


def _gather_weights(shards):
    n = len(shards)

    def body(*refs):
        ins, outs = refs[:n], refs[n:2 * n]
        send, recv, loc = refs[2 * n:]
        x, y, c = _place()
        me = 2 * x + y
        peers = _other_chips(x, y)
        local = [pltpu.make_async_copy(ins[w], outs[w].at[me], loc.at[w]) for w in range(n)]
        for cp in local:
            cp.start()
        sends = []
        for w in range(n):
            for j, ((px, py), _) in enumerate(peers):
                cp = pltpu.make_async_remote_copy(
                    src_ref=ins[w], dst_ref=outs[w].at[me], send_sem=send.at[w * 3 + j], recv_sem=recv.at[w * 3 + j],
                    device_id=(px, py, c), device_id_type=MESH)
                cp.start()
                sends.append(cp)
        for w in range(n):
            for j, ((px, py), pidx) in enumerate(peers):
                pltpu.make_async_remote_copy(
                    src_ref=ins[w], dst_ref=outs[w].at[pidx], send_sem=send.at[w * 3 + j], recv_sem=recv.at[w * 3 + j],
                    device_id=(px, py, c), device_id_type=MESH).wait_recv()
        for cp in sends:
            cp.wait_send()
        for cp in local:
            cp.wait()

    return pl.pallas_call(
        body, name="gather_weights",
        in_specs=[HBM_SPEC] * n, out_specs=[HBM_SPEC] * n,
        out_shape=[_sds((N_CHIPS,) + s.shape, s.dtype) for s in shards],
        scratch_shapes=[pltpu.SemaphoreType.DMA((3 * n,)), pltpu.SemaphoreType.DMA((3 * n,)),
                        pltpu.SemaphoreType.DMA((n,))],
    )(*shards)


def _pair_exchange(grads):
    n = len(grads)

    def body(*refs):
        ins, mine, got = refs[:n], refs[n:2 * n], refs[2 * n:3 * n]
        send, recv, loc = refs[3 * n:]
        x, y, c = _place()
        local, sends = [], []
        for w in range(n):
            half = ins[w].shape[1] // 2
            keep = ins[w].at[:, pl.ds(c * half, half), :]
            give = ins[w].at[:, pl.ds((1 - c) * half, half), :]
            lc = pltpu.make_async_copy(keep, mine[w], loc.at[w])
            lc.start()
            local.append(lc)
            cp = pltpu.make_async_remote_copy(src_ref=give, dst_ref=got[w], send_sem=send.at[w], recv_sem=recv.at[w],
                                              device_id=(x, y, 1 - c), device_id_type=MESH)
            cp.start()
            sends.append(cp)
        for cp in sends:
            cp.wait_recv()
        for cp in sends:
            cp.wait_send()
        for lc in local:
            lc.wait()

    halves = [_sds((g.shape[0], g.shape[1] // 2, g.shape[2]), g.dtype) for g in grads]
    res = pl.pallas_call(
        body, name="pair_exchange", in_specs=[HBM_SPEC] * n, out_specs=[HBM_SPEC] * (2 * n),
        out_shape=halves + halves,
        scratch_shapes=[pltpu.SemaphoreType.DMA((n,)), pltpu.SemaphoreType.DMA((n,)), pltpu.SemaphoreType.DMA((n,))],
    )(*grads)
    return res[:n], res[n:]


def _chip_scatter(parts):
    n = len(parts)

    def body(*refs):
        ins, outs = refs[:n], refs[n:2 * n]
        send, recv, loc = refs[2 * n:]
        x, y, c = _place()
        me = 2 * x + y
        peers = _other_chips(x, y)
        local, sends = [], []
        for w in range(n):
            lc = pltpu.make_async_copy(ins[w].at[me], outs[w].at[me], loc.at[w])
            lc.start()
            local.append(lc)
            for j, ((px, py), pidx) in enumerate(peers):
                cp = pltpu.make_async_remote_copy(
                    src_ref=ins[w].at[pidx], dst_ref=outs[w].at[me], send_sem=send.at[w * 3 + j],
                    recv_sem=recv.at[w * 3 + j], device_id=(px, py, c), device_id_type=MESH)
                cp.start()
                sends.append(cp)
        for w in range(n):
            for j, ((px, py), pidx) in enumerate(peers):
                pltpu.make_async_remote_copy(
                    src_ref=ins[w].at[pidx], dst_ref=outs[w].at[pidx], send_sem=send.at[w * 3 + j],
                    recv_sem=recv.at[w * 3 + j], device_id=(px, py, c), device_id_type=MESH).wait_recv()
        for cp in sends:
            cp.wait_send()
        for lc in local:
            lc.wait()

    return pl.pallas_call(
        body, name="chip_scatter", in_specs=[HBM_SPEC] * n, out_specs=[HBM_SPEC] * n,
        out_shape=[_sds(p.shape, p.dtype) for p in parts],
        scratch_shapes=[pltpu.SemaphoreType.DMA((3 * n,)), pltpu.SemaphoreType.DMA((3 * n,)),
                        pltpu.SemaphoreType.DMA((n,))],
    )(*parts)


def _pair_share(halves):
    n = len(halves)

    def body(*refs):
        ins, outs = refs[:n], refs[n:2 * n]
        send, recv, loc = refs[2 * n:]
        x, y, c = _place()
        local, sends = [], []
        for w in range(n):
            half = ins[w].shape[0]
            dst = outs[w].at[pl.ds(c * half, half), :]
            lc = pltpu.make_async_copy(ins[w], dst, loc.at[w])
            lc.start()
            local.append(lc)
            cp = pltpu.make_async_remote_copy(src_ref=ins[w], dst_ref=dst, send_sem=send.at[w], recv_sem=recv.at[w],
                                              device_id=(x, y, 1 - c), device_id_type=MESH)
            cp.start()
            sends.append(cp)
        for w in range(n):
            half = ins[w].shape[0]
            pltpu.make_async_remote_copy(
                src_ref=ins[w], dst_ref=outs[w].at[pl.ds((1 - c) * half, half), :], send_sem=send.at[w],
                recv_sem=recv.at[w], device_id=(x, y, 1 - c), device_id_type=MESH).wait_recv()
        for cp in sends:
            cp.wait_send()
        for lc in local:
            lc.wait()

    return pl.pallas_call(
        body, name="pair_share", in_specs=[HBM_SPEC] * n, out_specs=[HBM_SPEC] * n,
        out_shape=[_sds((2 * h.shape[0], h.shape[1]), h.dtype) for h in halves],
        scratch_shapes=[pltpu.SemaphoreType.DMA((n,)), pltpu.SemaphoreType.DMA((n,)), pltpu.SemaphoreType.DMA((n,))],
    )(*halves)


def _all_reduce_small(v):
    r, cdim = v.shape

    def body(v_ref, o_ref, buf, send, recv):
        x, y, c = _place()
        me = 4 * x + 2 * y + c
        buf[me] = v_ref[...]
        sends = []
        for m in range(1, N_DEV):
            px = 1 - x if m & 4 else x
            py = 1 - y if m & 2 else y
            pc = 1 - c if m & 1 else c
            cp = pltpu.make_async_remote_copy(src_ref=v_ref, dst_ref=buf.at[me], send_sem=send.at[m - 1],
                                              recv_sem=recv.at[m - 1], device_id=(px, py, pc), device_id_type=MESH)
            cp.start()
            sends.append((cp, 4 * px + 2 * py + pc))
        for m, (cp, pidx) in enumerate(sends):
            pltpu.make_async_remote_copy(src_ref=v_ref, dst_ref=buf.at[pidx], send_sem=send.at[m], recv_sem=recv.at[m],
                                         device_id=(x, y, c), device_id_type=MESH).wait_recv()
        for cp, _ in sends:
            cp.wait_send()
        tot = buf[0]
        for k in range(1, N_DEV):
            tot = tot + buf[k]
        o_ref[...] = tot

    return pl.pallas_call(
        body, name="all_reduce_small", in_specs=[VMEM_SPEC], out_specs=VMEM_SPEC,
        out_shape=_sds((r, cdim), F32),
        scratch_shapes=[pltpu.VMEM((N_DEV, r, cdim), F32), pltpu.SemaphoreType.DMA((N_DEV - 1,)),
                        pltpu.SemaphoreType.DMA((N_DEV - 1,))],
    )(v)


def _row_tile(rows, cols, budget_elems=1 << 18):
    if rows % 8:
        return rows
    t = max(8, (budget_elems // cols) // 8 * 8)
    while rows % t:
        t -= 8
    return t


def _add2(name, a, b):
    k, R, C = a.shape
    tr = _row_tile(R, C)

    def body(a_ref, b_ref, o_ref):
        o_ref[...] = a_ref[...] + b_ref[...]

    spec = pl.BlockSpec((None, tr, C), lambda s, i: (s, i, 0))
    return pl.pallas_call(body, name=name, grid=(k, R // tr), in_specs=[spec, spec], out_specs=spec,
                          out_shape=_sds(a.shape, a.dtype), compiler_params=_params(2))(a, b)


def _sum4(name, a):
    _, R, C = a.shape
    tr = _row_tile(R, C)

    def body(a_ref, o_ref):
        o_ref[...] = ((a_ref[0] + a_ref[1]) + a_ref[2]) + a_ref[3]

    return pl.pallas_call(
        body, name=name, grid=(R // tr,), in_specs=[pl.BlockSpec((N_CHIPS, tr, C), lambda i: (0, i, 0))],
        out_specs=pl.BlockSpec((tr, C), lambda i: (i, 0)), out_shape=_sds((R, C), a.dtype),
        compiler_params=_params(1))(a)


def _adamw(name, w, g, m, v):
    R, C = w.shape
    tr = _row_tile(R, C, 1 << 17)

    def body(w_ref, g_ref, m_ref, v_ref, d_ref, nm_ref, nv_ref):
        gv = g_ref[...]
        nm = ADAM_B1 * m_ref[...] + (1.0 - ADAM_B1) * gv
        nv = ADAM_B2 * v_ref[...] + (1.0 - ADAM_B2) * (gv * gv)
        m_hat = nm / (1.0 - ADAM_B1 ** ADAM_STEP)
        v_hat = nv / (1.0 - ADAM_B2 ** ADAM_STEP)
        d_ref[...] = -ADAM_LR * (m_hat / (jnp.sqrt(v_hat) + ADAM_EPS) + ADAM_WD * w_ref[...])
        nm_ref[...] = nm
        nv_ref[...] = nv

    spec = pl.BlockSpec((tr, C), lambda i: (i, 0))
    return pl.pallas_call(body, name=name, grid=(R // tr,), in_specs=[spec] * 4, out_specs=[spec] * 3,
                          out_shape=[_sds((R, C), F32)] * 3, compiler_params=_params(1))(w, g, m, v)


BIG = ("w_in", "w_proj_a", "w_proj_b", "w_out", "w_up", "w_down")
COL_SHARDED = ("w_in", "w_proj_a", "w_up")
SMALL = ("norm1_g", "q_norm_g", "k_norm_g", "ret_gn_g", "ret_gn_b", "norm2_g")
ALL_W = ("norm1_g", "w_in", "q_norm_g", "k_norm_g", "ret_gn_g", "ret_gn_b", "w_proj_a", "w_proj_b", "w_out",
         "norm2_g", "w_up", "w_down")
LANES = 128


def _to_full(name, gathered):
    k, r, c = gathered.shape
    if name in COL_SHARDED:
        return gathered.transpose(1, 0, 2).reshape(r, k * c)
    return gathered.reshape(k * r, c)


def _to_shard_major(name, full):
    if name in COL_SHARDED:
        r, c4 = full.shape
        return full.reshape(r, N_CHIPS, c4 // N_CHIPS).transpose(1, 0, 2)
    r4, c = full.shape
    return full.reshape(N_CHIPS, r4 // N_CHIPS, c)


def kernel(x, norm1_g, w_in, q_norm_g, k_norm_g, ret_gn_g, ret_gn_b, w_proj_a, w_proj_b, w_out, norm2_g, w_up, w_down, loss_target, m_norm1_g, m_w_in, m_q_norm_g, m_k_norm_g, m_ret_gn_g, m_ret_gn_b, m_w_proj_a, m_w_proj_b, m_w_out, m_norm2_g, m_w_up, m_w_down, v_norm1_g, v_w_in, v_q_norm_g, v_k_norm_g, v_ret_gn_g, v_ret_gn_b, v_w_proj_a, v_w_proj_b, v_w_out, v_norm2_g, v_w_up, v_w_down):
    weights = dict(norm1_g=norm1_g, w_in=w_in, q_norm_g=q_norm_g, k_norm_g=k_norm_g, ret_gn_g=ret_gn_g,
                   ret_gn_b=ret_gn_b, w_proj_a=w_proj_a, w_proj_b=w_proj_b, w_out=w_out, norm2_g=norm2_g,
                   w_up=w_up, w_down=w_down)
    moments_m = dict(norm1_g=m_norm1_g, w_in=m_w_in, q_norm_g=m_q_norm_g, k_norm_g=m_k_norm_g, ret_gn_g=m_ret_gn_g,
                     ret_gn_b=m_ret_gn_b, w_proj_a=m_w_proj_a, w_proj_b=m_w_proj_b, w_out=m_w_out,
                     norm2_g=m_norm2_g, w_up=m_w_up, w_down=m_w_down)
    moments_v = dict(norm1_g=v_norm1_g, w_in=v_w_in, q_norm_g=v_q_norm_g, k_norm_g=v_k_norm_g, ret_gn_g=v_ret_gn_g,
                     ret_gn_b=v_ret_gn_b, w_proj_a=v_w_proj_a, w_proj_b=v_w_proj_b, w_out=v_w_out,
                     norm2_g=v_norm2_g, w_up=v_w_up, w_down=v_w_down)

    shards = [weights[n][0].astype(CDT) for n in BIG]
    gathered = _gather_weights(shards)
    full = {n: _to_full(n, g) for n, g in zip(BIG, gathered)}
    small = {n: weights[n].reshape(1, -1) for n in SMALL}

    loss, grad_x, big_g, small_g = _local_step(x[0], loss_target[0], full, small)
    loss = lax.psum(loss, ("x", "y", "c"))

    sm = [_to_shard_major(n, big_g[n]) for n in BIG]
    mine, theirs = _pair_exchange(sm)
    chip_part = [_add2("pair_add_%s" % n, a, b) for n, a, b in zip(BIG, mine, theirs)]
    by_chip = _chip_scatter(chip_part)
    halves = [_sum4("chip_sum_%s" % n, a) for n, a in zip(BIG, by_chip)]
    grads = dict(zip(BIG, _pair_share(halves)))

    packed = jnp.concatenate([small_g[n].reshape(1, -1) for n in SMALL], axis=1)
    sizes = [small_g[n].size for n in SMALL]
    red = _all_reduce_small(packed.reshape(-1, LANES)).reshape(1, -1)
    off = 0
    for n, sz in zip(SMALL, sizes):
        grads[n] = red[:, off:off + sz]
        off += sz

    out_g, out_d, out_m, out_v = {}, {}, {}, {}
    for n in ALL_W:
        shape = weights[n].shape
        two_d = (shape[-2], shape[-1]) if n in BIG else (1, weights[n].size)
        g2 = grads[n].reshape(two_d)
        d, nm, nv = _adamw("adamw_%s" % n, weights[n].reshape(two_d), g2, moments_m[n].reshape(two_d),
                           moments_v[n].reshape(two_d))
        out_g[n], out_d[n], out_m[n], out_v[n] = (t.reshape(shape) for t in (g2, d, nm, nv))

    return (loss, grad_x[None], *[out_g[n] for n in ALL_W], *[out_d[n] for n in ALL_W],
            *[out_m[n] for n in ALL_W], *[out_v[n] for n in ALL_W])
```

```python
import functools
import math

import jax
import jax.numpy as jnp
from jax import lax
from jax.experimental import pallas as pl
from jax.experimental.pallas import tpu as pltpu

CDT = jnp.bfloat16
F32 = jnp.float32
EPS = 1e-6

ATT_GROUPS = ((128, 1), (512, 4), (2048, 16))
ATT_HPG = 4
ATT_HEADS = 12
HD = 128
BLK = 128
ATT_W = ATT_HEADS * HD
GW = ATT_HPG * HD
RET_HEADS = 4

ADAM_LR = 0.001
ADAM_B1 = 0.9
ADAM_B2 = 0.999
ADAM_EPS = 1e-08
ADAM_WD = 0.01
ADAM_STEP = 10

VMEM_LIMIT_BYTES = 48 * 1024 * 1024
MESH = pl.DeviceIdType.MESH
HBM_SPEC = pl.BlockSpec(memory_space=pltpu.HBM)
VMEM_SPEC = pl.BlockSpec(memory_space=pltpu.VMEM)


def _params(n_axes):
    return pltpu.CompilerParams(dimension_semantics=("arbitrary",) * n_axes,
                                vmem_limit_bytes=VMEM_LIMIT_BYTES)


def _dot_nn(a, b):
    return jnp.dot(a, b, preferred_element_type=F32)


def _dot_nt(a, b):
    return lax.dot_general(a, b, (((1,), (1,)), ((), ())), preferred_element_type=F32)


def _dot_tn(a, b):
    return lax.dot_general(a, b, (((0,), (0,)), ((), ())), preferred_element_type=F32)


def _sigmoid(v):
    return 1.0 / (1.0 + jnp.exp(-v))


def _matmul(name, mode, a, b, *, tm, tn, tk, extras=(), outs, epilogue):
    if mode == "nn":
        (M, K), (K2, N) = a.shape, b.shape
    elif mode == "nt":
        (M, K), (N, K2) = a.shape, b.shape
    else:
        (K, M), (K2, N) = a.shape, b.shape
    assert K == K2 and M % tm == 0 and N % tn == 0 and K % tk == 0, (name, a.shape, b.shape)
    ni, nj, nk = M // tm, N // tn, K // tk
    if mode == "tn":
        a_spec = pl.BlockSpec((tk, tm), lambda i, j, k: (k, i))
    else:
        a_spec = pl.BlockSpec((tm, tk), lambda i, j, k: (i, k))
    if mode == "nt":
        b_spec = pl.BlockSpec((tn, tk), lambda i, j, k: (j, k))
    else:
        b_spec = pl.BlockSpec((tk, tn), lambda i, j, k: (k, j))
    dot = {"nn": _dot_nn, "nt": _dot_nt, "tn": _dot_tn}[mode]
    n_ex, n_out = len(extras), len(outs)

    def body(*refs):
        a_ref, b_ref = refs[0], refs[1]
        ex = refs[2:2 + n_ex]
        out = refs[2 + n_ex:2 + n_ex + n_out]
        acc = refs[-1]
        i = pl.program_id(0)
        k = pl.program_id(2)

        @pl.when(k == 0)
        def _():
            acc[...] = jnp.zeros_like(acc)

        acc[...] += dot(a_ref[...].astype(CDT), b_ref[...].astype(CDT))

        @pl.when(k == nk - 1)
        def _():
            epilogue(acc[...], ex, out, i)

    res = pl.pallas_call(
        body, name=name, grid=(ni, nj, nk),
        in_specs=[a_spec, b_spec] + [s for _, s in extras],
        out_specs=[s for _, s in outs],
        out_shape=[o for o, _ in outs],
        scratch_shapes=[pltpu.VMEM((tm, tn), F32)],
        compiler_params=_params(3),
    )(a, b, *[e for e, _ in extras])
    return res


def _mn(tm, tn, col_off=0):
    return pl.BlockSpec((tm, tn), lambda i, j, k: (i, j + col_off))


def _row(tn):
    return pl.BlockSpec((1, tn), lambda i, j, k: (0, j))


def _ep_store(acc, ex, out, i):
    out[0][...] = acc.astype(out[0].dtype)


def _ep_resid(acc, ex, out, i):
    out[0][...] = ex[0][...] + acc


def _ep_up(acc, ex, out, i):
    out[0][...] = acc.astype(out[0].dtype)
    r = jnp.maximum(acc, 0.0)
    out[1][...] = (r * r).astype(out[1].dtype)


def _ep_down_loss(acc, ex, out, i, inv_d):
    diff = (ex[0][...] + acc) - ex[1][...]
    dx2 = diff * inv_d
    out[0][...] = dx2
    out[1][...] = dx2.astype(out[1].dtype)

    @pl.when(i == 0)
    def _():
        out[2][...] = jnp.zeros_like(out[2])

    out[2][...] += jnp.sum(diff * diff, axis=0, keepdims=True)


def _ep_dh(acc, ex, out, i):
    h = ex[0][...].astype(F32)
    out[0][...] = (acc * (2.0 * jnp.maximum(h, 0.0))).astype(out[0].dtype)


def _ep_rms_bwd(acc, ex, out, i):
    x = ex[0][...]
    g = ex[1][...]
    rstd = lax.rsqrt(jnp.mean(x * x, axis=-1, keepdims=True) + EPS)
    xh = x * rstd
    dxh = acc * g
    dx = ex[2][...] + rstd * (dxh - xh * jnp.mean(dxh * xh, axis=-1, keepdims=True))
    out[0][...] = dx
    out[1][...] = dx.astype(out[1].dtype)

    @pl.when(i == 0)
    def _():
        out[2][...] = jnp.zeros_like(out[2])

    out[2][...] += jnp.sum(acc * xh, axis=0, keepdims=True)


def _ep_gates(acc, ex, out, i):
    sa = _sigmoid(ex[0][...].astype(F32))
    sb = _sigmoid(ex[1][...].astype(F32))
    dpa = acc * sa
    dpb = acc * sb
    out[0][...] = dpa.astype(out[0].dtype)
    out[1][...] = dpb.astype(out[1].dtype)
    out[2][...] = (dpa * ex[2][...].astype(F32) * (1.0 - sa)).astype(out[2].dtype)
    out[3][...] = (dpb * ex[3][...].astype(F32) * (1.0 - sb)).astype(out[3].dtype)


def _sds(shape, dtype):
    return jax.ShapeDtypeStruct(shape, dtype)


def _rms_fwd(name, x, g, tm=512):
    S, D = x.shape

    def body(x_ref, g_ref, o_ref):
        xv = x_ref[...]
        rstd = lax.rsqrt(jnp.mean(xv * xv, axis=-1, keepdims=True) + EPS)
        o_ref[...] = (xv * rstd * g_ref[...]).astype(o_ref.dtype)

    return pl.pallas_call(
        body, name=name, grid=(S // tm,),
        in_specs=[pl.BlockSpec((tm, D), lambda i: (i, 0)), pl.BlockSpec((1, D), lambda i: (0, 0))],
        out_specs=pl.BlockSpec((tm, D), lambda i: (i, 0)),
        out_shape=_sds((S, D), CDT), compiler_params=_params(1))(x, g)


def _qknorm_fwd(proj, gqk, tm=512):
    S = proj.shape[0]
    W = 2 * ATT_W

    def body(p_ref, g_ref, o_ref):
        for hd in range(2 * ATT_HEADS):
            sl = slice(hd * HD, (hd + 1) * HD)
            v = p_ref[:, sl].astype(F32)
            rstd = lax.rsqrt(jnp.mean(v * v, axis=-1, keepdims=True) + EPS)
            o_ref[:, sl] = (v * rstd * g_ref[:, sl]).astype(o_ref.dtype)

    return pl.pallas_call(
        body, name="qknorm_fwd", grid=(S // tm,),
        in_specs=[pl.BlockSpec((tm, W), lambda i: (i, 0)), pl.BlockSpec((1, W), lambda i: (0, 0))],
        out_specs=pl.BlockSpec((tm, W), lambda i: (i, 0)),
        out_shape=_sds((S, W), CDT), compiler_params=_params(1))(proj, gqk)


def _qknorm_bwd(proj, gqk, dqs, dks, dvs, tm=256):
    S = proj.shape[0]
    W = 2 * ATT_W

    def body(p_ref, g_ref, *refs):
        dq_refs, dk_refs, dv_refs = refs[0:3], refs[3:6], refs[6:9]
        o_ref, dg_ref = refs[9], refs[10]
        i = pl.program_id(0)

        @pl.when(i == 0)
        def _():
            dg_ref[...] = jnp.zeros_like(dg_ref)

        for hd in range(2 * ATT_HEADS):
            sl = slice(hd * HD, (hd + 1) * HD)
            head = hd % ATT_HEADS
            grp, slot = head // ATT_HPG, head % ATT_HPG
            src = (dq_refs if hd < ATT_HEADS else dk_refs)[grp]
            dn = src[:, slot * HD:(slot + 1) * HD]
            v = p_ref[:, sl].astype(F32)
            rstd = lax.rsqrt(jnp.mean(v * v, axis=-1, keepdims=True) + EPS)
            vh = v * rstd
            dg_ref[:, sl] += jnp.sum(dn * vh, axis=0, keepdims=True)
            dvh = dn * g_ref[:, sl]
            o_ref[:, sl] = (rstd * (dvh - vh * jnp.mean(dvh * vh, axis=-1, keepdims=True))).astype(o_ref.dtype)
        for grp in range(3):
            o_ref[:, W + grp * GW:W + (grp + 1) * GW] = dv_refs[grp][...].astype(o_ref.dtype)

    g512 = pl.BlockSpec((tm, GW), lambda i: (i, 0))
    return pl.pallas_call(
        body, name="qknorm_bwd", grid=(S // tm,),
        in_specs=[pl.BlockSpec((tm, W), lambda i: (i, 0)), pl.BlockSpec((1, W), lambda i: (0, 0))] + [g512] * 9,
        out_specs=[pl.BlockSpec((tm, 3 * ATT_W), lambda i: (i, 0)), pl.BlockSpec((1, W), lambda i: (0, 0))],
        out_shape=[_sds((S, 3 * ATT_W), CDT), _sds((1, W), F32)],
        compiler_params=_params(1))(proj, gqk, *dqs, *dks, *dvs)


def _att_mask(n):
    qi = lax.broadcasted_iota(jnp.int32, (BLK, 2 * BLK), 0)
    kj = lax.broadcasted_iota(jnp.int32, (BLK, 2 * BLK), 1)
    dist = BLK + qi - kj
    valid = (dist >= 0) & (dist <= BLK) & ((kj >= BLK) | (n > 0))
    return valid, dist.astype(F32)


def _att_slopes(grp):
    return [2.0 ** (-8.0 * (grp * ATT_HPG + hh + 1) / ATT_HEADS) for hh in range(ATT_HPG)]


def _att_operands(grp, qkn, proj):
    S = proj.shape[0]
    _, d = ATT_GROUPS[grp]
    L = S // d
    last = L // BLK - 1
    if d == 1:
        arrays = [qkn, qkn, qkn, proj, proj]
        cols = [lambda r: grp, lambda r: 3 + grp, lambda r: 3 + grp, lambda r: 6 + grp, lambda r: 6 + grp]
    else:
        gs = slice(grp * GW, (grp + 1) * GW)
        qkv = jnp.concatenate([qkn[:, gs], qkn[:, ATT_W:][:, gs], proj[:, 2 * ATT_W:][:, gs]], axis=1)
        arrays = [qkv.reshape(L, d * 3 * GW)] * 5
        cols = [lambda r: 3 * r, lambda r: 3 * r + 1, lambda r: 3 * r + 1, lambda r: 3 * r + 2, lambda r: 3 * r + 2]

    def cur(n):
        return jnp.minimum(n, last)

    def prev(n):
        return jnp.maximum(jnp.minimum(n, last) - 1, 0)

    rows = [cur, prev, cur, prev, cur]
    specs = [pl.BlockSpec((BLK, GW), functools.partial(lambda r, n, rf, cf: (rf(n), cf(r)), rf=rf, cf=cf))
             for rf, cf in zip(rows, cols)]
    return arrays, specs


def _att_fwd(grp, S, operands):
    _, d = ATT_GROUPS[grp]
    L = S // d
    nb = L // BLK
    slopes = _att_slopes(grp)
    scale = HD ** -0.5
    arrays, specs = operands

    def body(q_ref, kp_ref, kc_ref, vp_ref, vc_ref, o_ref, l_ref):
        n = pl.program_id(1)
        valid, distf = _att_mask(n)
        for hh in range(ATT_HPG):
            sl = slice(hh * HD, (hh + 1) * HD)
            k = jnp.concatenate([kp_ref[:, sl], kc_ref[:, sl]], axis=0)
            v = jnp.concatenate([vp_ref[:, sl], vc_ref[:, sl]], axis=0)
            s = _dot_nt(q_ref[:, sl], k) * scale + (-slopes[hh] * d) * distf
            s = jnp.where(valid, s, -1e30)
            m = jnp.max(s, axis=-1, keepdims=True)
            p = jnp.exp(s - m)
            den = jnp.sum(p, axis=-1, keepdims=True)
            o_ref[:, sl] = _dot_nn(p.astype(CDT), v) / den
            l_ref[:, sl] = jnp.broadcast_to(m + jnp.log(den), (BLK, HD))

    out_spec = pl.BlockSpec((BLK, GW), lambda r, n: (n, r))
    o, l = pl.pallas_call(
        body, name="att_fwd_g%d" % grp, grid=(d, nb),
        in_specs=specs,
        out_specs=[out_spec, out_spec],
        out_shape=[_sds((L, d * GW), F32), _sds((L, d * GW), F32)],
        compiler_params=_params(2),
    )(*arrays)
    return o.reshape(S, GW), l.reshape(S, GW)


def _att_bwd(grp, S, operands, lse, do_g, c_g):
    arrays, specs = operands
    _, d = ATT_GROUPS[grp]
    L = S // d
    nb = L // BLK
    slopes = _att_slopes(grp)
    scale = HD ** -0.5
    last = nb - 1

    def body(q_ref, kp_ref, kc_ref, vp_ref, vc_ref, l_ref, do_ref, c_ref, dq_ref, dk_ref, dv_ref, ck, cv):
        n = pl.program_id(1)

        @pl.when(n == 0)
        def _():
            ck[...] = jnp.zeros_like(ck)
            cv[...] = jnp.zeros_like(cv)

        @pl.when(n < nb)
        def _():
            valid, distf = _att_mask(n)
            for hh in range(ATT_HPG):
                sl = slice(hh * HD, (hh + 1) * HD)
                q = q_ref[:, sl]
                k = jnp.concatenate([kp_ref[:, sl], kc_ref[:, sl]], axis=0)
                v = jnp.concatenate([vp_ref[:, sl], vc_ref[:, sl]], axis=0)
                do = do_ref[:, sl]
                s = _dot_nt(q, k) * scale + (-slopes[hh] * d) * distf
                p = jnp.where(valid, jnp.exp(s - l_ref[:, sl][:, 0:1]), 0.0)
                dp = _dot_nt(do, v)
                ds = (p * (dp + c_ref[:, sl][:, 0:1]) * scale).astype(CDT)
                dq_ref[:, sl] = _dot_nn(ds, k)
                dk = _dot_tn(ds, q)
                dv = _dot_tn(p.astype(CDT), do)
                dk_ref[:, sl] = ck[:, sl] + dk[0:BLK]
                dv_ref[:, sl] = cv[:, sl] + dv[0:BLK]
                ck[:, sl] = dk[BLK:2 * BLK]
                cv[:, sl] = dv[BLK:2 * BLK]

        @pl.when(n == nb)
        def _():
            dk_ref[...] = ck[...]
            dv_ref[...] = cv[...]

    blk = (BLK, GW)
    at_q = pl.BlockSpec(blk, lambda r, n: (jnp.minimum(n, last), r))
    behind = pl.BlockSpec(blk, lambda r, n: (jnp.maximum(n - 1, 0), r))
    dq, dk, dv = pl.pallas_call(
        body, name="att_bwd_g%d" % grp, grid=(d, nb + 1),
        in_specs=specs + [at_q, at_q, at_q],
        out_specs=[at_q, behind, behind],
        out_shape=[_sds((L, d * GW), F32)] * 3,
        scratch_shapes=[pltpu.VMEM(blk, F32), pltpu.VMEM(blk, F32)],
        compiler_params=_params(2),
    )(*arrays, lse.reshape(L, d * GW), do_g.reshape(L, d * GW), c_g.reshape(L, d * GW))
    return dq.reshape(S, GW), dk.reshape(S, GW), dv.reshape(S, GW)


def _head_sum_bcast(v):
    parts = []
    for j in range(ATT_HPG):
        sj = jnp.sum(v[:, j * HD:(j + 1) * HD], axis=-1, keepdims=True)
        parts.append(jnp.broadcast_to(sj, (v.shape[0], HD)))
    return jnp.concatenate(parts, axis=1)


def _mix_alpha(l_refs):
    l0, l1, l2 = l_refs[0][...], l_refs[1][...], l_refs[2][...]
    mx = jnp.maximum(jnp.maximum(l0, l1), l2)
    e = [jnp.exp(l0 - mx), jnp.exp(l1 - mx), jnp.exp(l2 - mx)]
    tot = e[0] + e[1] + e[2]
    return [ei / tot for ei in e]


def _mix_fwd(os_, ls_, tm=512):
    S = os_[0].shape[0]

    def body(o0, o1, o2, l0, l1, l2, out):
        al = _mix_alpha((l0, l1, l2))
        out[...] = (al[0] * o0[...] + al[1] * o1[...] + al[2] * o2[...]).astype(out.dtype)

    spec = pl.BlockSpec((tm, GW), lambda i: (i, 0))
    return pl.pallas_call(
        body, name="mix_fwd", grid=(S // tm,), in_specs=[spec] * 6, out_specs=spec,
        out_shape=_sds((S, GW), CDT), compiler_params=_params(1))(*os_, *ls_)


def _mix_bwd(os_, ls_, do_a, tm=512):
    S = os_[0].shape[0]

    def body(o0, o1, o2, l0, l1, l2, d_ref, g0, g1, g2, c0, c1, c2):
        al = _mix_alpha((l0, l1, l2))
        dv = d_ref[...]
        o_a = al[0] * o0[...] + al[1] * o1[...] + al[2] * o2[...]
        dsum = _head_sum_bcast(dv * o_a)
        for a, g_ref, c_ref in zip(al, (g0, g1, g2), (c0, c1, c2)):
            g_ref[...] = (a * dv).astype(g_ref.dtype)
            c_ref[...] = -(a * dsum)

    spec = pl.BlockSpec((tm, GW), lambda i: (i, 0))
    res = pl.pallas_call(
        body, name="mix_bwd", grid=(S // tm,), in_specs=[spec] * 7, out_specs=[spec] * 6,
        out_shape=[_sds((S, GW), CDT)] * 3 + [_sds((S, GW), F32)] * 3,
        compiler_params=_params(1))(*os_, *ls_, do_a)
    return res[:3], res[3:]


def _ret_tables(dk):
    H, C = RET_HEADS, BLK
    log_g = jnp.log(1.0 - 2.0 ** (-5.0 - jnp.arange(H, dtype=F32)))
    idx = jnp.arange(C, dtype=F32)
    diff = idx[:, None] - idx[None, :]
    decay = jnp.where(diff >= 0, jnp.exp(log_g[:, None, None] * jnp.maximum(diff, 0.0)), 0.0)
    xi = jnp.exp(log_g[:, None] * (idx[None, :] + 1.0))
    zeta = jnp.exp(log_g[:, None] * (C - 1.0 - idx[None, :])) * (dk ** -0.5)
    g_chunk = jnp.exp(log_g * C)
    bc = lambda t: jnp.broadcast_to(t[:, :, None], (H, C, C))
    return decay, bc(xi), bc(zeta), jnp.broadcast_to(g_chunk[:, None, None], (H, 8, C))


def _gn_fwd(o, g, b):
    mu = jnp.mean(o, axis=-1, keepdims=True)
    xc = o - mu
    rstd = lax.rsqrt(jnp.mean(xc * xc, axis=-1, keepdims=True) + EPS)
    yh = xc * rstd
    return yh, rstd, yh * g + b


def _ret_specs(dk, dv, order):
    qk_w, v_w = RET_HEADS * dk, RET_HEADS * dv
    off_q = 3 * ATT_W
    off_k, off_v, off_g = off_q + qk_w, off_q + 2 * qk_w, off_q + 2 * qk_w + v_w
    assert off_q % dk == 0 and off_k % dk == 0 and off_v % dv == 0 and off_g % dv == 0
    tab = pl.BlockSpec((None, BLK, BLK), lambda h, i: (h, 0, 0))
    return [
        pl.BlockSpec((BLK, dk), lambda h, i: (order(i), off_q // dk + h)),
        pl.BlockSpec((BLK, dk), lambda h, i: (order(i), off_k // dk + h)),
        pl.BlockSpec((BLK, dv), lambda h, i: (order(i), off_v // dv + h)),
        pl.BlockSpec((BLK, dv), lambda h, i: (order(i), off_g // dv + h)),
        tab, tab, tab, pl.BlockSpec((None, 8, BLK), lambda h, i: (h, 0, 0)),
    ]


def _ret_fwd(proj, gn_g, gn_b, dk, dv):
    S = proj.shape[0]
    N = S // BLK
    H = RET_HEADS
    kscale = dk ** -0.5

    def body(q_ref, k_ref, v_ref, gr_ref, dec_ref, xi_ref, zeta_ref, gc_ref, g_ref, b_ref,
             opre_ref, or_ref, st_ref, state):
        n = pl.program_id(1)

        @pl.when(n == 0)
        def _():
            state[...] = jnp.zeros_like(state)

        q, k, v = q_ref[...], k_ref[...], v_ref[...]
        s = _dot_nt(q, k) * kscale * dec_ref[...]
        st = state[...]
        st_c = st.astype(CDT)
        st_ref[...] = st_c
        o = _dot_nn(s.astype(CDT), v) + _dot_nn(q, st_c) * xi_ref[:, 0:1]
        kz = (k.astype(F32) * zeta_ref[:, 0:1]).astype(CDT)
        state[...] = st * gc_ref[0:1, 0:1] + _dot_tn(kz, v)
        opre_ref[...] = o
        _, _, y = _gn_fwd(o, g_ref[...], b_ref[...])
        gr = gr_ref[...].astype(F32)
        or_ref[...] = (y * (gr * _sigmoid(gr))).astype(or_ref.dtype)

    v_w = H * dv
    row = pl.BlockSpec((1, dv), lambda h, i: (0, h))
    tile = pl.BlockSpec((BLK, dv), lambda h, i: (i, h))
    return pl.pallas_call(
        body, name="ret_fwd", grid=(H, N),
        in_specs=_ret_specs(dk, dv, lambda i: i) + [row, row],
        out_specs=[tile, tile, pl.BlockSpec((None, None, dk, dv), lambda h, i: (i, h, 0, 0))],
        out_shape=[_sds((S, v_w), F32), _sds((S, v_w), CDT), _sds((N, H, dk, dv), CDT)],
        scratch_shapes=[pltpu.VMEM((dk, dv), F32)],
        compiler_params=_params(2),
    )(proj, proj, proj, proj, *_ret_tables(dk), gn_g, gn_b)


def _ret_bwd(proj, gn_g, gn_b, o_pre, states, d_or, dk, dv):
    S = proj.shape[0]
    N = S // BLK
    H = RET_HEADS
    qk_w, v_w = H * dk, H * dv
    kscale = dk ** -0.5

    def body(q_ref, k_ref, v_ref, gr_ref, dec_ref, xi_ref, zeta_ref, gc_ref, g_ref, b_ref, opre_ref, st_ref, dor_ref,
             dq_ref, dk_ref, dv_ref, dgr_ref, dg_ref, db_ref, dstate):
        i = pl.program_id(1)

        @pl.when(i == 0)
        def _():
            dstate[...] = jnp.zeros_like(dstate)
            dg_ref[...] = jnp.zeros_like(dg_ref)
            db_ref[...] = jnp.zeros_like(db_ref)

        q, k, v = q_ref[...], k_ref[...], v_ref[...]
        decay, xi, zeta_s = dec_ref[...], xi_ref[:, 0:1], zeta_ref[:, 0:1]
        gr = gr_ref[...].astype(F32)
        sg = _sigmoid(gr)
        gain = g_ref[...]
        yh, rstd, y = _gn_fwd(opre_ref[...], gain, b_ref[...])
        d_or_v = dor_ref[...]
        dy = d_or_v * (gr * sg)
        dgr_ref[...] = (d_or_v * y * (sg * (1.0 + gr * (1.0 - sg)))).astype(dgr_ref.dtype)
        dg_ref[...] += jnp.sum(dy * yh, axis=0, keepdims=True)
        db_ref[...] += jnp.sum(dy, axis=0, keepdims=True)
        dyh = dy * gain
        do = rstd * (dyh - jnp.mean(dyh, axis=-1, keepdims=True) - yh * jnp.mean(dyh * yh, axis=-1, keepdims=True))
        do_c = do.astype(CDT)
        dox = (do * xi).astype(CDT)
        a_c = (_dot_nt(q, k) * kscale * decay).astype(CDT)
        g_c = (_dot_nt(do_c, v) * decay).astype(CDT)
        dsn = dstate[...]
        dsn_c = dsn.astype(CDT)
        kz = (k.astype(F32) * zeta_s).astype(CDT)
        dq_ref[...] = (_dot_nn(g_c, k) * kscale + _dot_nt(dox, st_ref[...])).astype(dq_ref.dtype)
        dk_ref[...] = (_dot_tn(g_c, q) * kscale + _dot_nt(v, dsn_c) * zeta_s).astype(dk_ref.dtype)
        dv_ref[...] = (_dot_tn(a_c, do_c) + _dot_nn(kz, dsn_c)).astype(dv_ref.dtype)
        dstate[...] = dsn * gc_ref[0:1, 0:1] + _dot_tn(q, dox)

    rev = lambda i: N - 1 - i
    row = pl.BlockSpec((1, dv), lambda h, i: (0, h))
    vt = pl.BlockSpec((BLK, dv), lambda h, i: (rev(i), h))
    qt = pl.BlockSpec((BLK, dk), lambda h, i: (rev(i), h))
    return pl.pallas_call(
        body, name="ret_bwd", grid=(H, N),
        in_specs=_ret_specs(dk, dv, rev) + [row, row, vt,
                 pl.BlockSpec((None, None, dk, dv), lambda h, i: (rev(i), h, 0, 0)), vt],
        out_specs=[qt, qt, vt, vt, row, row],
        out_shape=[_sds((S, qk_w), CDT), _sds((S, qk_w), CDT), _sds((S, v_w), CDT), _sds((S, v_w), CDT),
                   _sds((1, v_w), F32), _sds((1, v_w), F32)],
        scratch_shapes=[pltpu.VMEM((dk, dv), F32)],
        compiler_params=_params(2),
    )(proj, proj, proj, proj, *_ret_tables(dk), gn_g, gn_b, o_pre, states, d_or)


def _merge_fwd(o_a, o_r, wa, wb, proj, d_model, tm=512, tn=256):
    S, in_w = proj.shape
    off_a, off_b = in_w - 2 * d_model, in_w - d_model
    assert off_a % tn == 0 and off_b % tn == 0

    def body(oa_ref, or_ref, wa_ref, wb_ref, ga_ref, gb_ref, y_ref, pa_ref, pb_ref):
        pa = _dot_nn(oa_ref[...], wa_ref[...])
        pb = _dot_nn(or_ref[...], wb_ref[...])
        y = _sigmoid(ga_ref[...].astype(F32)) * pa + _sigmoid(gb_ref[...].astype(F32)) * pb
        y_ref[...] = y.astype(y_ref.dtype)
        pa_ref[...] = pa.astype(pa_ref.dtype)
        pb_ref[...] = pb.astype(pb_ref.dtype)

    ka, kb = o_a.shape[1], o_r.shape[1]
    out = pl.BlockSpec((tm, tn), lambda i, j: (i, j))
    return pl.pallas_call(
        body, name="merge_fwd", grid=(S // tm, d_model // tn),
        in_specs=[pl.BlockSpec((tm, ka), lambda i, j: (i, 0)), pl.BlockSpec((tm, kb), lambda i, j: (i, 0)),
                  pl.BlockSpec((ka, tn), lambda i, j: (0, j)), pl.BlockSpec((kb, tn), lambda i, j: (0, j)),
                  pl.BlockSpec((tm, tn), lambda i, j: (i, off_a // tn + j)),
                  pl.BlockSpec((tm, tn), lambda i, j: (i, off_b // tn + j))],
        out_specs=[out, out, out], out_shape=[_sds((S, d_model), CDT)] * 3,
        compiler_params=_params(2))(o_a, o_r, wa, wb, proj, proj)


def _local_step(x, target, w, small):
    S, D = x.shape
    in_w = w["w_in"].shape[1]
    d_ff = w["w_up"].shape[1]
    ret_v_w = w["w_proj_b"].shape[0]
    dv = ret_v_w // RET_HEADS
    dk = (in_w - 3 * ATT_W - 2 * ret_v_w - 2 * D) // (2 * RET_HEADS)
    gqk = jnp.concatenate([small["q_norm_g"].reshape(1, ATT_W), small["k_norm_g"].reshape(1, ATT_W)], axis=1)
    g1, g2 = small["norm1_g"], small["norm2_g"]
    gn_g, gn_b = small["ret_gn_g"], small["ret_gn_b"]

    xn = _rms_fwd("rms1_fwd", x, g1)
    (proj,) = _matmul("in_proj", "nn", xn, w["w_in"], tm=1024, tn=640, tk=D,
                      outs=[(_sds((S, in_w), CDT), _mn(1024, 640))], epilogue=_ep_store)
    qkn = _qknorm_fwd(proj, gqk)
    att_ops = [_att_operands(g, qkn, proj) for g in range(3)]
    att = [_att_fwd(g, S, att_ops[g]) for g in range(3)]
    os_, ls_ = [a[0] for a in att], [a[1] for a in att]
    o_a = _mix_fwd(os_, ls_)
    o_pre, o_r, states = _ret_fwd(proj, gn_g, gn_b, dk, dv)
    y, pa, pb = _merge_fwd(o_a, o_r, w["w_proj_a"], w["w_proj_b"], proj, D)
    (x1,) = _matmul("out_proj", "nn", y, w["w_out"], tm=512, tn=D, tk=D,
                    extras=[(x, _mn(512, D))], outs=[(_sds((S, D), F32), _mn(512, D))], epilogue=_ep_resid)
    xn2 = _rms_fwd("rms2_fwd", x1, g2)
    hid, act = _matmul("mlp_up", "nn", xn2, w["w_up"], tm=1024, tn=512, tk=D,
                       outs=[(_sds((S, d_ff), CDT), _mn(1024, 512))] * 2, epilogue=_ep_up)
    dx2, dx2c, loss_row = _matmul(
        "mlp_down_loss", "nn", act, w["w_down"], tm=512, tn=D, tk=1024,
        extras=[(x1, _mn(512, D)), (target, _mn(512, D))],
        outs=[(_sds((S, D), F32), _mn(512, D)), (_sds((S, D), CDT), _mn(512, D)), (_sds((1, D), F32), _row(D))],
        epilogue=functools.partial(_ep_down_loss, inv_d=1.0 / D))
    loss = 0.5 * jnp.sum(loss_row) / D

    (dh,) = _matmul("d_hidden", "nt", dx2c, w["w_down"], tm=1024, tn=512, tk=D,
                    extras=[(hid, _mn(1024, 512))], outs=[(_sds((S, d_ff), CDT), _mn(1024, 512))], epilogue=_ep_dh)
    (gw_down,) = _matmul("dw_down", "tn", act, dx2c, tm=1024, tn=D, tk=1024,
                         outs=[(_sds((d_ff, D), F32), _mn(1024, D))], epilogue=_ep_store)
    (gw_up,) = _matmul("dw_up", "tn", xn2, dh, tm=D, tn=1024, tk=1024,
                       outs=[(_sds((D, d_ff), F32), _mn(D, 1024))], epilogue=_ep_store)
    dx1, dx1c, dg2 = _matmul(
        "d_x1", "nt", dh, w["w_up"], tm=512, tn=D, tk=1024,
        extras=[(x1, _mn(512, D)), (g2, _row(D)), (dx2, _mn(512, D))],
        outs=[(_sds((S, D), F32), _mn(512, D)), (_sds((S, D), CDT), _mn(512, D)), (_sds((1, D), F32), _row(D))],
        epilogue=_ep_rms_bwd)

    gt = 512
    assert (in_w - 2 * D) % gt == 0
    off_a, off_b = (in_w - 2 * D) // gt, (in_w - D) // gt
    dpa, dpb, dga, dgb = _matmul(
        "d_gates", "nt", dx1c, w["w_out"], tm=512, tn=gt, tk=D,
        extras=[(proj, _mn(512, gt, off_a)), (proj, _mn(512, gt, off_b)), (pa, _mn(512, gt)), (pb, _mn(512, gt))],
        outs=[(_sds((S, D), CDT), _mn(512, gt))] * 4, epilogue=_ep_gates)
    (gw_out,) = _matmul("dw_out", "tn", y, dx1c, tm=D, tn=D, tk=1024,
                        outs=[(_sds((D, D), F32), _mn(D, D))], epilogue=_ep_store)
    (do_a,) = _matmul("d_o_a", "nt", dpa, w["w_proj_a"], tm=1024, tn=GW, tk=D,
                      outs=[(_sds((S, GW), F32), _mn(1024, GW))], epilogue=_ep_store)
    (d_or,) = _matmul("d_o_r", "nt", dpb, w["w_proj_b"], tm=1024, tn=512, tk=D,
                      outs=[(_sds((S, ret_v_w), F32), _mn(1024, 512))], epilogue=_ep_store)
    (gw_pa,) = _matmul("dw_proj_a", "tn", o_a, dpa, tm=GW, tn=D, tk=1024,
                       outs=[(_sds((GW, D), F32), _mn(GW, D))], epilogue=_ep_store)
    (gw_pb,) = _matmul("dw_proj_b", "tn", o_r, dpb, tm=1024, tn=D, tk=1024,
                       outs=[(_sds((ret_v_w, D), F32), _mn(1024, D))], epilogue=_ep_store)

    dq_r, dk_r, dv_r, dgr, dgn_g, dgn_b = _ret_bwd(proj, gn_g, gn_b, o_pre, states, d_or, dk, dv)
    do_gs, c_gs = _mix_bwd(os_, ls_, do_a)
    datt_parts = [_att_bwd(g, S, att_ops[g], ls_[g], do_gs[g], c_gs[g]) for g in range(3)]
    d_att, dgqk = _qknorm_bwd(proj, gqk, [p[0] for p in datt_parts], [p[1] for p in datt_parts],
                              [p[2] for p in datt_parts])
    dproj = jnp.concatenate([d_att, dq_r, dk_r, dv_r, dgr, dga, dgb], axis=1)

    (gw_in,) = _matmul("dw_in", "tn", xn, dproj, tm=D, tn=640, tk=1024,
                       outs=[(_sds((D, in_w), F32), _mn(D, 640))], epilogue=_ep_store)
    grad_x, _, dg1 = _matmul(
        "d_x", "nt", dproj, w["w_in"], tm=512, tn=D, tk=1280,
        extras=[(x, _mn(512, D)), (g1, _row(D)), (dx1, _mn(512, D))],
        outs=[(_sds((S, D), F32), _mn(512, D)), (_sds((S, D), CDT), _mn(512, D)), (_sds((1, D), F32), _row(D))],
        epilogue=_ep_rms_bwd)

    big = {"w_in": gw_in, "w_proj_a": gw_pa, "w_proj_b": gw_pb, "w_out": gw_out, "w_up": gw_up, "w_down": gw_down}
    smallg = {"norm1_g": dg1, "q_norm_g": dgqk[:, :ATT_W], "k_norm_g": dgqk[:, ATT_W:],
              "ret_gn_g": dgn_g, "ret_gn_b": dgn_b, "norm2_g": dg2}
    return loss, grad_x, big, smallg


N_CHIPS = 4
N_DEV = 8


def _place():
    x, y, c = lax.axis_index("x"), lax.axis_index("y"), lax.axis_index("c")
    return x, y, c


def _other_chips(x, y):
    out = []
    for fx, fy in ((1, 0), (0, 1), (1, 1)):
        px = 1 - x if fx else x
        py = 1 - y if fy else y
        out.append(((px, py), 2 * px + py))
    return out


def _gather_weights(shards):
    n = len(shards)

    def body(*refs):
        ins, outs = refs[:n], refs[n:2 * n]
        send, recv, loc = refs[2 * n:]
        x, y, c = _place()
        me = 2 * x + y
        peers = _other_chips(x, y)
        local = [pltpu.make_async_copy(ins[w], outs[w].at[me], loc.at[w]) for w in range(n)]
        for cp in local:
            cp.start()
        sends = []
        for w in range(n):
            for j, ((px, py), _) in enumerate(peers):
                cp = pltpu.make_async_remote_copy(
                    src_ref=ins[w], dst_ref=outs[w].at[me], send_sem=send.at[w * 3 + j], recv_sem=recv.at[w * 3 + j],
                    device_id=(px, py, c), device_id_type=MESH)
                cp.start()
                sends.append(cp)
        for w in range(n):
            for j, ((px, py), pidx) in enumerate(peers):
                pltpu.make_async_remote_copy(
                    src_ref=ins[w], dst_ref=outs[w].at[pidx], send_sem=send.at[w * 3 + j], recv_sem=recv.at[w * 3 + j],
                    device_id=(px, py, c), device_id_type=MESH).wait_recv()
        for cp in sends:
            cp.wait_send()
        for cp in local:
            cp.wait()

    return pl.pallas_call(
        body, name="gather_weights",
        in_specs=[HBM_SPEC] * n, out_specs=[HBM_SPEC] * n,
        out_shape=[_sds((N_CHIPS,) + s.shape, s.dtype) for s in shards],
        scratch_shapes=[pltpu.SemaphoreType.DMA((3 * n,)), pltpu.SemaphoreType.DMA((3 * n,)),
                        pltpu.SemaphoreType.DMA((n,))],
    )(*shards)


PAIR_TILE_ELEMS = 1 << 19


def _pair_reduce(name, g, core):
    k, R, C = g.shape
    half = R // 2
    tr = _row_tile(half, C, PAIR_TILE_ELEMS)
    nt = half // tr

    def body(c_ref, mine_ref, give_ref, out_ref, slot, send, recv):
        b = (pl.program_id(0) * nt + pl.program_id(1)) % 2
        x, y, c = _place()
        cp = pltpu.make_async_remote_copy(src_ref=give_ref, dst_ref=slot.at[b], send_sem=send.at[b],
                                          recv_sem=recv.at[b], device_id=(x, y, 1 - c), device_id_type=MESH)
        cp.start()
        cp.wait_recv()
        out_ref[...] = mine_ref[...] + slot[b]
        cp.wait_send()

    blk = (tr, C)
    grid_spec = pltpu.PrefetchScalarGridSpec(
        num_scalar_prefetch=1, grid=(k, nt),
        in_specs=[pl.BlockSpec(blk, lambda s, i, c: ((2 * s + c[0]) * nt + i, 0)),
                  pl.BlockSpec(blk, lambda s, i, c: ((2 * s + 1 - c[0]) * nt + i, 0))],
        out_specs=pl.BlockSpec(blk, lambda s, i, c: (s * nt + i, 0)),
        scratch_shapes=[pltpu.VMEM((2, tr, C), F32), pltpu.SemaphoreType.DMA((2,)), pltpu.SemaphoreType.DMA((2,))])
    g2 = g.reshape(k * R, C)
    out = pl.pallas_call(body, name=name, grid_spec=grid_spec, out_shape=_sds((k * half, C), F32),
                         compiler_params=_params(2))(core, g2, g2)
    return out.reshape(k, half, C)


def _chip_scatter(parts):
    n = len(parts)

    def body(*refs):
        ins, outs = refs[:n], refs[n:2 * n]
        send, recv, loc = refs[2 * n:]
        x, y, c = _place()
        me = 2 * x + y
        peers = _other_chips(x, y)
        local, sends = [], []
        for w in range(n):
            lc = pltpu.make_async_copy(ins[w].at[me], outs[w].at[me], loc.at[w])
            lc.start()
            local.append(lc)
            for j, ((px, py), pidx) in enumerate(peers):
                cp = pltpu.make_async_remote_copy(
                    src_ref=ins[w].at[pidx], dst_ref=outs[w].at[me], send_sem=send.at[w * 3 + j],
                    recv_sem=recv.at[w * 3 + j], device_id=(px, py, c), device_id_type=MESH)
                cp.start()
                sends.append(cp)
        for w in range(n):
            for j, ((px, py), pidx) in enumerate(peers):
                pltpu.make_async_remote_copy(
                    src_ref=ins[w].at[pidx], dst_ref=outs[w].at[pidx], send_sem=send.at[w * 3 + j],
                    recv_sem=recv.at[w * 3 + j], device_id=(px, py, c), device_id_type=MESH).wait_recv()
        for cp in sends:
            cp.wait_send()
        for lc in local:
            lc.wait()

    return pl.pallas_call(
        body, name="chip_scatter", in_specs=[HBM_SPEC] * n, out_specs=[HBM_SPEC] * n,
        out_shape=[_sds(p.shape, p.dtype) for p in parts],
        scratch_shapes=[pltpu.SemaphoreType.DMA((3 * n,)), pltpu.SemaphoreType.DMA((3 * n,)),
                        pltpu.SemaphoreType.DMA((n,))],
    )(*parts)


def _pair_share(name, f, core):
    half, C = f.shape
    tr = _row_tile(half, C, PAIR_TILE_ELEMS)
    nt = half // tr

    def body(c_ref, f_ref, out_ref, slot, send, recv):
        p = pl.program_id(1)
        b = pl.program_id(0) % 2
        x, y, c = _place()
        cp = pltpu.make_async_remote_copy(src_ref=f_ref, dst_ref=slot.at[b], send_sem=send.at[b],
                                          recv_sem=recv.at[b], device_id=(x, y, 1 - c), device_id_type=MESH)

        @pl.when(p == 0)
        def _():
            cp.start()
            out_ref[...] = f_ref[...]

        @pl.when(p == 1)
        def _():
            cp.wait_recv()
            out_ref[...] = slot[b]
            cp.wait_send()

    grid_spec = pltpu.PrefetchScalarGridSpec(
        num_scalar_prefetch=1, grid=(nt, 2),
        in_specs=[pl.BlockSpec((tr, C), lambda i, p, c: (i, 0))],
        out_specs=pl.BlockSpec((tr, C), lambda i, p, c: (jnp.where(p == 0, c[0], 1 - c[0]) * nt + i, 0)),
        scratch_shapes=[pltpu.VMEM((2, tr, C), F32), pltpu.SemaphoreType.DMA((2,)), pltpu.SemaphoreType.DMA((2,))])
    return pl.pallas_call(body, name=name, grid_spec=grid_spec, out_shape=_sds((2 * half, C), F32),
                          compiler_params=_params(2))(core, f)


def _all_reduce_small(v):
    r, cdim = v.shape

    def body(v_ref, o_ref, buf, send, recv):
        x, y, c = _place()
        me = 4 * x + 2 * y + c
        buf[me] = v_ref[...]
        sends = []
        for m in range(1, N_DEV):
            px = 1 - x if m & 4 else x
            py = 1 - y if m & 2 else y
            pc = 1 - c if m & 1 else c
            cp = pltpu.make_async_remote_copy(src_ref=v_ref, dst_ref=buf.at[me], send_sem=send.at[m - 1],
                                              recv_sem=recv.at[m - 1], device_id=(px, py, pc), device_id_type=MESH)
            cp.start()
            sends.append((cp, 4 * px + 2 * py + pc))
        for m, (cp, pidx) in enumerate(sends):
            pltpu.make_async_remote_copy(src_ref=v_ref, dst_ref=buf.at[pidx], send_sem=send.at[m], recv_sem=recv.at[m],
                                         device_id=(x, y, c), device_id_type=MESH).wait_recv()
        for cp, _ in sends:
            cp.wait_send()
        tot = buf[0]
        for k in range(1, N_DEV):
            tot = tot + buf[k]
        o_ref[...] = tot

    return pl.pallas_call(
        body, name="all_reduce_small", in_specs=[VMEM_SPEC], out_specs=VMEM_SPEC,
        out_shape=_sds((r, cdim), F32),
        scratch_shapes=[pltpu.VMEM((N_DEV, r, cdim), F32), pltpu.SemaphoreType.DMA((N_DEV - 1,)),
                        pltpu.SemaphoreType.DMA((N_DEV - 1,))],
    )(v)


def _row_tile(rows, cols, budget_elems=1 << 18):
    if rows % 8:
        return rows
    t = max(8, (budget_elems // cols) // 8 * 8)
    while rows % t:
        t -= 8
    return t


def _sum4(name, a):
    _, R, C = a.shape
    tr = _row_tile(R, C)

    def body(a_ref, o_ref):
        o_ref[...] = ((a_ref[0] + a_ref[1]) + a_ref[2]) + a_ref[3]

    return pl.pallas_call(
        body, name=name, grid=(R // tr,), in_specs=[pl.BlockSpec((N_CHIPS, tr, C), lambda i: (0, i, 0))],
        out_specs=pl.BlockSpec((tr, C), lambda i: (i, 0)), out_shape=_sds((R, C), a.dtype),
        compiler_params=_params(1))(a)


def _adamw(name, w, g, m, v):
    R, C = w.shape
    tr = _row_tile(R, C, 1 << 17)

    def body(w_ref, g_ref, m_ref, v_ref, d_ref, nm_ref, nv_ref):
        gv = g_ref[...]
        nm = ADAM_B1 * m_ref[...] + (1.0 - ADAM_B1) * gv
        nv = ADAM_B2 * v_ref[...] + (1.0 - ADAM_B2) * (gv * gv)
        m_hat = nm / (1.0 - ADAM_B1 ** ADAM_STEP)
        v_hat = nv / (1.0 - ADAM_B2 ** ADAM_STEP)
        d_ref[...] = -ADAM_LR * (m_hat / (jnp.sqrt(v_hat) + ADAM_EPS) + ADAM_WD * w_ref[...])
        nm_ref[...] = nm
        nv_ref[...] = nv

    spec = pl.BlockSpec((tr, C), lambda i: (i, 0))
    return pl.pallas_call(body, name=name, grid=(R // tr,), in_specs=[spec] * 4, out_specs=[spec] * 3,
                          out_shape=[_sds((R, C), F32)] * 3, compiler_params=_params(1))(w, g, m, v)


BIG = ("w_in", "w_proj_a", "w_proj_b", "w_out", "w_up", "w_down")
COL_SHARDED = ("w_in", "w_proj_a", "w_up")
SMALL = ("norm1_g", "q_norm_g", "k_norm_g", "ret_gn_g", "ret_gn_b", "norm2_g")
ALL_W = ("norm1_g", "w_in", "q_norm_g", "k_norm_g", "ret_gn_g", "ret_gn_b", "w_proj_a", "w_proj_b", "w_out",
         "norm2_g", "w_up", "w_down")
LANES = 128


def _to_full(name, gathered):
    k, r, c = gathered.shape
    if name in COL_SHARDED:
        return gathered.transpose(1, 0, 2).reshape(r, k * c)
    return gathered.reshape(k * r, c)


def _to_shard_major(name, full):
    if name in COL_SHARDED:
        r, c4 = full.shape
        return full.reshape(r, N_CHIPS, c4 // N_CHIPS).transpose(1, 0, 2)
    r4, c = full.shape
    return full.reshape(N_CHIPS, r4 // N_CHIPS, c)


def kernel(x, norm1_g, w_in, q_norm_g, k_norm_g, ret_gn_g, ret_gn_b, w_proj_a, w_proj_b, w_out, norm2_g, w_up, w_down, loss_target, m_norm1_g, m_w_in, m_q_norm_g, m_k_norm_g, m_ret_gn_g, m_ret_gn_b, m_w_proj_a, m_w_proj_b, m_w_out, m_norm2_g, m_w_up, m_w_down, v_norm1_g, v_w_in, v_q_norm_g, v_k_norm_g, v_ret_gn_g, v_ret_gn_b, v_w_proj_a, v_w_proj_b, v_w_out, v_norm2_g, v_w_up, v_w_down):
    weights = dict(norm1_g=norm1_g, w_in=w_in, q_norm_g=q_norm_g, k_norm_g=k_norm_g, ret_gn_g=ret_gn_g,
                   ret_gn_b=ret_gn_b, w_proj_a=w_proj_a, w_proj_b=w_proj_b, w_out=w_out, norm2_g=norm2_g,
                   w_up=w_up, w_down=w_down)
    moments_m = dict(norm1_g=m_norm1_g, w_in=m_w_in, q_norm_g=m_q_norm_g, k_norm_g=m_k_norm_g, ret_gn_g=m_ret_gn_g,
                     ret_gn_b=m_ret_gn_b, w_proj_a=m_w_proj_a, w_proj_b=m_w_proj_b, w_out=m_w_out,
                     norm2_g=m_norm2_g, w_up=m_w_up, w_down=m_w_down)
    moments_v = dict(norm1_g=v_norm1_g, w_in=v_w_in, q_norm_g=v_q_norm_g, k_norm_g=v_k_norm_g, ret_gn_g=v_ret_gn_g,
                     ret_gn_b=v_ret_gn_b, w_proj_a=v_w_proj_a, w_proj_b=v_w_proj_b, w_out=v_w_out,
                     norm2_g=v_norm2_g, w_up=v_w_up, w_down=v_w_down)

    shards = [weights[n][0].astype(CDT) for n in BIG]
    gathered = _gather_weights(shards)
    full = {n: _to_full(n, g) for n, g in zip(BIG, gathered)}
    small = {n: weights[n].reshape(1, -1) for n in SMALL}

    loss, grad_x, big_g, small_g = _local_step(x[0], loss_target[0], full, small)
    loss = lax.psum(loss, ("x", "y", "c"))

    core = lax.axis_index("c").astype(jnp.int32).reshape(1)
    chip_part = [_pair_reduce("pair_reduce_%s" % n, _to_shard_major(n, big_g[n]), core) for n in BIG]
    by_chip = _chip_scatter(chip_part)
    halves = [_sum4("chip_sum_%s" % n, a) for n, a in zip(BIG, by_chip)]
    grads = {n: _pair_share("pair_share_%s" % n, h, core) for n, h in zip(BIG, halves)}

    packed = jnp.concatenate([small_g[n].reshape(1, -1) for n in SMALL], axis=1)
    sizes = [small_g[n].size for n in SMALL]
    red = _all_reduce_small(packed.reshape(-1, LANES)).reshape(1, -1)
    off = 0
    for n, sz in zip(SMALL, sizes):
        grads[n] = red[:, off:off + sz]
        off += sz

    out_g, out_d, out_m, out_v = {}, {}, {}, {}
    for n in ALL_W:
        shape = weights[n].shape
        two_d = (shape[-2], shape[-1]) if n in BIG else (1, weights[n].size)
        g2 = grads[n].reshape(two_d)
        d, nm, nv = _adamw("adamw_%s" % n, weights[n].reshape(two_d), g2, moments_m[n].reshape(two_d),
                           moments_v[n].reshape(two_d))
        out_g[n], out_d[n], out_m[n], out_v[n] = (t.reshape(shape) for t in (g2, d, nm, nv))

    return (loss, grad_x[None], *[out_g[n] for n in ALL_W], *[out_d[n] for n in ALL_W],
            *[out_m[n] for n in ALL_W], *[out_v[n] for n in ALL_W])
```

```python
import functools
import math

import jax
import jax.numpy as jnp
from jax import lax
from jax.experimental import pallas as pl
from jax.experimental.pallas import tpu as pltpu

CDT = jnp.bfloat16
F32 = jnp.float32
EPS = 1e-6

ATT_GROUPS = ((128, 1), (512, 4), (2048, 16))
ATT_HPG = 4
ATT_HEADS = 12
HD = 128
BLK = 128
ATT_W = ATT_HEADS * HD
GW = ATT_HPG * HD
RET_HEADS = 4

ADAM_LR = 0.001
ADAM_B1 = 0.9
ADAM_B2 = 0.999
ADAM_EPS = 1e-08
ADAM_WD = 0.01
ADAM_STEP = 10

VMEM_LIMIT_BYTES = 48 * 1024 * 1024
MESH = pl.DeviceIdType.MESH
HBM_SPEC = pl.BlockSpec(memory_space=pltpu.HBM)
VMEM_SPEC = pl.BlockSpec(memory_space=pltpu.VMEM)


def _params(n_axes):
    return pltpu.CompilerParams(dimension_semantics=("arbitrary",) * n_axes,
                                vmem_limit_bytes=VMEM_LIMIT_BYTES)


def _dot_nn(a, b):
    return jnp.dot(a, b, preferred_element_type=F32)


def _dot_nt(a, b):
    return lax.dot_general(a, b, (((1,), (1,)), ((), ())), preferred_element_type=F32)


def _dot_tn(a, b):
    return lax.dot_general(a, b, (((0,), (0,)), ((), ())), preferred_element_type=F32)


def _sigmoid(v):
    return 1.0 / (1.0 + jnp.exp(-v))


def _matmul(name, mode, a, b, *, tm, tn, tk, extras=(), outs, epilogue):
    if mode == "nn":
        (M, K), (K2, N) = a.shape, b.shape
    elif mode == "nt":
        (M, K), (N, K2) = a.shape, b.shape
    else:
        (K, M), (K2, N) = a.shape, b.shape
    assert K == K2 and M % tm == 0 and N % tn == 0 and K % tk == 0, (name, a.shape, b.shape)
    ni, nj, nk = M // tm, N // tn, K // tk
    if mode == "tn":
        a_spec = pl.BlockSpec((tk, tm), lambda i, j, k: (k, i))
    else:
        a_spec = pl.BlockSpec((tm, tk), lambda i, j, k: (i, k))
    if mode == "nt":
        b_spec = pl.BlockSpec((tn, tk), lambda i, j, k: (j, k))
    else:
        b_spec = pl.BlockSpec((tk, tn), lambda i, j, k: (k, j))
    dot = {"nn": _dot_nn, "nt": _dot_nt, "tn": _dot_tn}[mode]
    n_ex, n_out = len(extras), len(outs)

    def body(*refs):
        a_ref, b_ref = refs[0], refs[1]
        ex = refs[2:2 + n_ex]
        out = refs[2 + n_ex:2 + n_ex + n_out]
        acc = refs[-1]
        i = pl.program_id(0)
        k = pl.program_id(2)

        @pl.when(k == 0)
        def _():
            acc[...] = jnp.zeros_like(acc)

        acc[...] += dot(a_ref[...].astype(CDT), b_ref[...].astype(CDT))

        @pl.when(k == nk - 1)
        def _():
            epilogue(acc[...], ex, out, i)

    res = pl.pallas_call(
        body, name=name, grid=(ni, nj, nk),
        in_specs=[a_spec, b_spec] + [s for _, s in extras],
        out_specs=[s for _, s in outs],
        out_shape=[o for o, _ in outs],
        scratch_shapes=[pltpu.VMEM((tm, tn), F32)],
        compiler_params=_params(3),
    )(a, b, *[e for e, _ in extras])
    return res


def _mn(tm, tn, col_off=0):
    return pl.BlockSpec((tm, tn), lambda i, j, k: (i, j + col_off))


def _row(tn):
    return pl.BlockSpec((1, tn), lambda i, j, k: (0, j))


def _ep_store(acc, ex, out, i):
    out[0][...] = acc.astype(out[0].dtype)


def _ep_resid(acc, ex, out, i):
    out[0][...] = ex[0][...] + acc


def _ep_up(acc, ex, out, i):
    out[0][...] = acc.astype(out[0].dtype)
    r = jnp.maximum(acc, 0.0)
    out[1][...] = (r * r).astype(out[1].dtype)


def _ep_down_loss(acc, ex, out, i, inv_d):
    diff = (ex[0][...] + acc) - ex[1][...]
    dx2 = diff * inv_d
    out[0][...] = dx2
    out[1][...] = dx2.astype(out[1].dtype)

    @pl.when(i == 0)
    def _():
        out[2][...] = jnp.zeros_like(out[2])

    out[2][...] += jnp.sum(diff * diff, axis=0, keepdims=True)


def _ep_dh(acc, ex, out, i):
    h = ex[0][...].astype(F32)
    out[0][...] = (acc * (2.0 * jnp.maximum(h, 0.0))).astype(out[0].dtype)


def _ep_rms_bwd(acc, ex, out, i):
    x = ex[0][...]
    g = ex[1][...]
    rstd = lax.rsqrt(jnp.mean(x * x, axis=-1, keepdims=True) + EPS)
    xh = x * rstd
    dxh = acc * g
    dx = ex[2][...] + rstd * (dxh - xh * jnp.mean(dxh * xh, axis=-1, keepdims=True))
    out[0][...] = dx
    out[1][...] = dx.astype(out[1].dtype)

    @pl.when(i == 0)
    def _():
        out[2][...] = jnp.zeros_like(out[2])

    out[2][...] += jnp.sum(acc * xh, axis=0, keepdims=True)


def _ep_gates(acc, ex, out, i):
    sa = _sigmoid(ex[0][...].astype(F32))
    sb = _sigmoid(ex[1][...].astype(F32))
    dpa = acc * sa
    dpb = acc * sb
    out[0][...] = dpa.astype(out[0].dtype)
    out[1][...] = dpb.astype(out[1].dtype)
    out[2][...] = (dpa * ex[2][...].astype(F32) * (1.0 - sa)).astype(out[2].dtype)
    out[3][...] = (dpb * ex[3][...].astype(F32) * (1.0 - sb)).astype(out[3].dtype)


def _sds(shape, dtype):
    return jax.ShapeDtypeStruct(shape, dtype)


def _rms_fwd(name, x, g, tm=512):
    S, D = x.shape

    def body(x_ref, g_ref, o_ref):
        xv = x_ref[...]
        rstd = lax.rsqrt(jnp.mean(xv * xv, axis=-1, keepdims=True) + EPS)
        o_ref[...] = (xv * rstd * g_ref[...]).astype(o_ref.dtype)

    return pl.pallas_call(
        body, name=name, grid=(S // tm,),
        in_specs=[pl.BlockSpec((tm, D), lambda i: (i, 0)), pl.BlockSpec((1, D), lambda i: (0, 0))],
        out_specs=pl.BlockSpec((tm, D), lambda i: (i, 0)),
        out_shape=_sds((S, D), CDT), compiler_params=_params(1))(x, g)


def _qknorm_fwd(proj, gqk, tm=512):
    S = proj.shape[0]
    W = 2 * ATT_W

    def body(p_ref, g_ref, o_ref):
        for hd in range(2 * ATT_HEADS):
            sl = slice(hd * HD, (hd + 1) * HD)
            v = p_ref[:, sl].astype(F32)
            rstd = lax.rsqrt(jnp.mean(v * v, axis=-1, keepdims=True) + EPS)
            o_ref[:, sl] = (v * rstd * g_ref[:, sl]).astype(o_ref.dtype)

    return pl.pallas_call(
        body, name="qknorm_fwd", grid=(S // tm,),
        in_specs=[pl.BlockSpec((tm, W), lambda i: (i, 0)), pl.BlockSpec((1, W), lambda i: (0, 0))],
        out_specs=pl.BlockSpec((tm, W), lambda i: (i, 0)),
        out_shape=_sds((S, W), CDT), compiler_params=_params(1))(proj, gqk)


def _qknorm_bwd(proj, gqk, dqs, dks, dvs, tm=256):
    S = proj.shape[0]
    W = 2 * ATT_W

    def body(p_ref, g_ref, *refs):
        dq_refs, dk_refs, dv_refs = refs[0:3], refs[3:6], refs[6:9]
        o_ref, dg_ref = refs[9], refs[10]
        i = pl.program_id(0)

        @pl.when(i == 0)
        def _():
            dg_ref[...] = jnp.zeros_like(dg_ref)

        for hd in range(2 * ATT_HEADS):
            sl = slice(hd * HD, (hd + 1) * HD)
            head = hd % ATT_HEADS
            grp, slot = head // ATT_HPG, head % ATT_HPG
            src = (dq_refs if hd < ATT_HEADS else dk_refs)[grp]
            dn = src[:, slot * HD:(slot + 1) * HD]
            v = p_ref[:, sl].astype(F32)
            rstd = lax.rsqrt(jnp.mean(v * v, axis=-1, keepdims=True) + EPS)
            vh = v * rstd
            dg_ref[:, sl] += jnp.sum(dn * vh, axis=0, keepdims=True)
            dvh = dn * g_ref[:, sl]
            o_ref[:, sl] = (rstd * (dvh - vh * jnp.mean(dvh * vh, axis=-1, keepdims=True))).astype(o_ref.dtype)
        for grp in range(3):
            o_ref[:, W + grp * GW:W + (grp + 1) * GW] = dv_refs[grp][...].astype(o_ref.dtype)

    g512 = pl.BlockSpec((tm, GW), lambda i: (i, 0))
    return pl.pallas_call(
        body, name="qknorm_bwd", grid=(S // tm,),
        in_specs=[pl.BlockSpec((tm, W), lambda i: (i, 0)), pl.BlockSpec((1, W), lambda i: (0, 0))] + [g512] * 9,
        out_specs=[pl.BlockSpec((tm, 3 * ATT_W), lambda i: (i, 0)), pl.BlockSpec((1, W), lambda i: (0, 0))],
        out_shape=[_sds((S, 3 * ATT_W), CDT), _sds((1, W), F32)],
        compiler_params=_params(1))(proj, gqk, *dqs, *dks, *dvs)


def _att_mask(n):
    qi = lax.broadcasted_iota(jnp.int32, (BLK, 2 * BLK), 0)
    kj = lax.broadcasted_iota(jnp.int32, (BLK, 2 * BLK), 1)
    dist = BLK + qi - kj
    valid = (dist >= 0) & (dist <= BLK) & ((kj >= BLK) | (n > 0))
    return valid, dist.astype(F32)


def _att_slopes(grp):
    return [2.0 ** (-8.0 * (grp * ATT_HPG + hh + 1) / ATT_HEADS) for hh in range(ATT_HPG)]


def _att_operands(grp, qkn, proj):
    S = proj.shape[0]
    _, d = ATT_GROUPS[grp]
    L = S // d
    last = L // BLK - 1
    if d == 1:
        arrays = [qkn, qkn, qkn, proj, proj]
        cols = [lambda r: grp, lambda r: 3 + grp, lambda r: 3 + grp, lambda r: 6 + grp, lambda r: 6 + grp]
    else:
        gs = slice(grp * GW, (grp + 1) * GW)
        qkv = jnp.concatenate([qkn[:, gs], qkn[:, ATT_W:][:, gs], proj[:, 2 * ATT_W:][:, gs]], axis=1)
        arrays = [qkv.reshape(L, d * 3 * GW)] * 5
        cols = [lambda r: 3 * r, lambda r: 3 * r + 1, lambda r: 3 * r + 1, lambda r: 3 * r + 2, lambda r: 3 * r + 2]

    def cur(n):
        return jnp.minimum(n, last)

    def prev(n):
        return jnp.maximum(jnp.minimum(n, last) - 1, 0)

    rows = [cur, prev, cur, prev, cur]
    specs = [pl.BlockSpec((BLK, GW), functools.partial(lambda r, n, rf, cf: (rf(n), cf(r)), rf=rf, cf=cf))
             for rf, cf in zip(rows, cols)]
    return arrays, specs


def _att_fwd(grp, S, operands):
    _, d = ATT_GROUPS[grp]
    L = S // d
    nb = L // BLK
    slopes = _att_slopes(grp)
    scale = HD ** -0.5
    arrays, specs = operands

    def body(q_ref, kp_ref, kc_ref, vp_ref, vc_ref, o_ref, l_ref):
        n = pl.program_id(1)
        valid, distf = _att_mask(n)
        for hh in range(ATT_HPG):
            sl = slice(hh * HD, (hh + 1) * HD)
            k = jnp.concatenate([kp_ref[:, sl], kc_ref[:, sl]], axis=0)
            v = jnp.concatenate([vp_ref[:, sl], vc_ref[:, sl]], axis=0)
            s = _dot_nt(q_ref[:, sl], k) * scale + (-slopes[hh] * d) * distf
            s = jnp.where(valid, s, -1e30)
            m = jnp.max(s, axis=-1, keepdims=True)
            p = jnp.exp(s - m)
            den = jnp.sum(p, axis=-1, keepdims=True)
            o_ref[:, sl] = _dot_nn(p.astype(CDT), v) / den
            l_ref[:, sl] = jnp.broadcast_to(m + jnp.log(den), (BLK, HD))

    out_spec = pl.BlockSpec((BLK, GW), lambda r, n: (n, r))
    o, l = pl.pallas_call(
        body, name="att_fwd_g%d" % grp, grid=(d, nb),
        in_specs=specs,
        out_specs=[out_spec, out_spec],
        out_shape=[_sds((L, d * GW), F32), _sds((L, d * GW), F32)],
        compiler_params=_params(2),
    )(*arrays)
    return o.reshape(S, GW), l.reshape(S, GW)


def _att_bwd(grp, S, operands, lse, do_g, c_g):
    arrays, specs = operands
    _, d = ATT_GROUPS[grp]
    L = S // d
    nb = L // BLK
    slopes = _att_slopes(grp)
    scale = HD ** -0.5
    last = nb - 1

    def body(q_ref, kp_ref, kc_ref, vp_ref, vc_ref, l_ref, do_ref, c_ref, dq_ref, dk_ref, dv_ref, ck, cv):
        n = pl.program_id(1)

        @pl.when(n == 0)
        def _():
            ck[...] = jnp.zeros_like(ck)
            cv[...] = jnp.zeros_like(cv)

        @pl.when(n < nb)
        def _():
            valid, distf = _att_mask(n)
            for hh in range(ATT_HPG):
                sl = slice(hh * HD, (hh + 1) * HD)
                q = q_ref[:, sl]
                k = jnp.concatenate([kp_ref[:, sl], kc_ref[:, sl]], axis=0)
                v = jnp.concatenate([vp_ref[:, sl], vc_ref[:, sl]], axis=0)
                do = do_ref[:, sl]
                s = _dot_nt(q, k) * scale + (-slopes[hh] * d) * distf
                p = jnp.where(valid, jnp.exp(s - l_ref[:, sl][:, 0:1]), 0.0)
                dp = _dot_nt(do, v)
                ds = (p * (dp + c_ref[:, sl][:, 0:1]) * scale).astype(CDT)
                dq_ref[:, sl] = _dot_nn(ds, k)
                dk = _dot_tn(ds, q)
                dv = _dot_tn(p.astype(CDT), do)
                dk_ref[:, sl] = ck[:, sl] + dk[0:BLK]
                dv_ref[:, sl] = cv[:, sl] + dv[0:BLK]
                ck[:, sl] = dk[BLK:2 * BLK]
                cv[:, sl] = dv[BLK:2 * BLK]

        @pl.when(n == nb)
        def _():
            dk_ref[...] = ck[...]
            dv_ref[...] = cv[...]

    blk = (BLK, GW)
    at_q = pl.BlockSpec(blk, lambda r, n: (jnp.minimum(n, last), r))
    behind = pl.BlockSpec(blk, lambda r, n: (jnp.maximum(n - 1, 0), r))
    dq, dk, dv = pl.pallas_call(
        body, name="att_bwd_g%d" % grp, grid=(d, nb + 1),
        in_specs=specs + [at_q, at_q, at_q],
        out_specs=[at_q, behind, behind],
        out_shape=[_sds((L, d * GW), F32)] * 3,
        scratch_shapes=[pltpu.VMEM(blk, F32), pltpu.VMEM(blk, F32)],
        compiler_params=_params(2),
    )(*arrays, lse.reshape(L, d * GW), do_g.reshape(L, d * GW), c_g.reshape(L, d * GW))
    return dq.reshape(S, GW), dk.reshape(S, GW), dv.reshape(S, GW)


def _head_sum_bcast(v):
    parts = []
    for j in range(ATT_HPG):
        sj = jnp.sum(v[:, j * HD:(j + 1) * HD], axis=-1, keepdims=True)
        parts.append(jnp.broadcast_to(sj, (v.shape[0], HD)))
    return jnp.concatenate(parts, axis=1)


def _mix_alpha(l_refs):
    l0, l1, l2 = l_refs[0][...], l_refs[1][...], l_refs[2][...]
    mx = jnp.maximum(jnp.maximum(l0, l1), l2)
    e = [jnp.exp(l0 - mx), jnp.exp(l1 - mx), jnp.exp(l2 - mx)]
    tot = e[0] + e[1] + e[2]
    return [ei / tot for ei in e]


def _mix_fwd(os_, ls_, tm=512):
    S = os_[0].shape[0]

    def body(o0, o1, o2, l0, l1, l2, out):
        al = _mix_alpha((l0, l1, l2))
        out[...] = (al[0] * o0[...] + al[1] * o1[...] + al[2] * o2[...]).astype(out.dtype)

    spec = pl.BlockSpec((tm, GW), lambda i: (i, 0))
    return pl.pallas_call(
        body, name="mix_fwd", grid=(S // tm,), in_specs=[spec] * 6, out_specs=spec,
        out_shape=_sds((S, GW), CDT), compiler_params=_params(1))(*os_, *ls_)


def _mix_bwd(os_, ls_, do_a, tm=512):
    S = os_[0].shape[0]

    def body(o0, o1, o2, l0, l1, l2, d_ref, g0, g1, g2, c0, c1, c2):
        al = _mix_alpha((l0, l1, l2))
        dv = d_ref[...]
        o_a = al[0] * o0[...] + al[1] * o1[...] + al[2] * o2[...]
        dsum = _head_sum_bcast(dv * o_a)
        for a, g_ref, c_ref in zip(al, (g0, g1, g2), (c0, c1, c2)):
            g_ref[...] = (a * dv).astype(g_ref.dtype)
            c_ref[...] = -(a * dsum)

    spec = pl.BlockSpec((tm, GW), lambda i: (i, 0))
    res = pl.pallas_call(
        body, name="mix_bwd", grid=(S // tm,), in_specs=[spec] * 7, out_specs=[spec] * 6,
        out_shape=[_sds((S, GW), CDT)] * 3 + [_sds((S, GW), F32)] * 3,
        compiler_params=_params(1))(*os_, *ls_, do_a)
    return res[:3], res[3:]


def _ret_tables(dk):
    H, C = RET_HEADS, BLK
    log_g = jnp.log(1.0 - 2.0 ** (-5.0 - jnp.arange(H, dtype=F32)))
    idx = jnp.arange(C, dtype=F32)
    diff = idx[:, None] - idx[None, :]
    decay = jnp.where(diff >= 0, jnp.exp(log_g[:, None, None] * jnp.maximum(diff, 0.0)), 0.0)
    xi = jnp.exp(log_g[:, None] * (idx[None, :] + 1.0))
    zeta = jnp.exp(log_g[:, None] * (C - 1.0 - idx[None, :])) * (dk ** -0.5)
    g_chunk = jnp.exp(log_g * C)
    bc = lambda t: jnp.broadcast_to(t[:, :, None], (H, C, C))
    return decay, bc(xi), bc(zeta), jnp.broadcast_to(g_chunk[:, None, None], (H, 8, C))


def _gn_fwd(o, g, b):
    mu = jnp.mean(o, axis=-1, keepdims=True)
    xc = o - mu
    rstd = lax.rsqrt(jnp.mean(xc * xc, axis=-1, keepdims=True) + EPS)
    yh = xc * rstd
    return yh, rstd, yh * g + b


def _ret_specs(dk, dv, order):
    qk_w, v_w = RET_HEADS * dk, RET_HEADS * dv
    off_q = 3 * ATT_W
    off_k, off_v, off_g = off_q + qk_w, off_q + 2 * qk_w, off_q + 2 * qk_w + v_w
    assert off_q % dk == 0 and off_k % dk == 0 and off_v % dv == 0 and off_g % dv == 0
    tab = pl.BlockSpec((None, BLK, BLK), lambda h, i: (h, 0, 0))
    return [
        pl.BlockSpec((BLK, dk), lambda h, i: (order(i), off_q // dk + h)),
        pl.BlockSpec((BLK, dk), lambda h, i: (order(i), off_k // dk + h)),
        pl.BlockSpec((BLK, dv), lambda h, i: (order(i), off_v // dv + h)),
        pl.BlockSpec((BLK, dv), lambda h, i: (order(i), off_g // dv + h)),
        tab, tab, tab, pl.BlockSpec((None, 8, BLK), lambda h, i: (h, 0, 0)),
    ]


def _ret_fwd(proj, gn_g, gn_b, dk, dv):
    S = proj.shape[0]
    N = S // BLK
    H = RET_HEADS
    kscale = dk ** -0.5

    def body(q_ref, k_ref, v_ref, gr_ref, dec_ref, xi_ref, zeta_ref, gc_ref, g_ref, b_ref,
             opre_ref, or_ref, st_ref, state):
        n = pl.program_id(1)

        @pl.when(n == 0)
        def _():
            state[...] = jnp.zeros_like(state)

        q, k, v = q_ref[...], k_ref[...], v_ref[...]
        s = _dot_nt(q, k) * kscale * dec_ref[...]
        st = state[...]
        st_c = st.astype(CDT)
        st_ref[...] = st_c
        o = _dot_nn(s.astype(CDT), v) + _dot_nn(q, st_c) * xi_ref[:, 0:1]
        kz = (k.astype(F32) * zeta_ref[:, 0:1]).astype(CDT)
        state[...] = st * gc_ref[0:1, 0:1] + _dot_tn(kz, v)
        opre_ref[...] = o
        _, _, y = _gn_fwd(o, g_ref[...], b_ref[...])
        gr = gr_ref[...].astype(F32)
        or_ref[...] = (y * (gr * _sigmoid(gr))).astype(or_ref.dtype)

    v_w = H * dv
    row = pl.BlockSpec((1, dv), lambda h, i: (0, h))
    tile = pl.BlockSpec((BLK, dv), lambda h, i: (i, h))
    return pl.pallas_call(
        body, name="ret_fwd", grid=(H, N),
        in_specs=_ret_specs(dk, dv, lambda i: i) + [row, row],
        out_specs=[tile, tile, pl.BlockSpec((None, None, dk, dv), lambda h, i: (i, h, 0, 0))],
        out_shape=[_sds((S, v_w), F32), _sds((S, v_w), CDT), _sds((N, H, dk, dv), CDT)],
        scratch_shapes=[pltpu.VMEM((dk, dv), F32)],
        compiler_params=_params(2),
    )(proj, proj, proj, proj, *_ret_tables(dk), gn_g, gn_b)


def _ret_bwd(proj, gn_g, gn_b, o_pre, states, d_or, dk, dv):
    S = proj.shape[0]
    N = S // BLK
    H = RET_HEADS
    qk_w, v_w = H * dk, H * dv
    kscale = dk ** -0.5

    def body(q_ref, k_ref, v_ref, gr_ref, dec_ref, xi_ref, zeta_ref, gc_ref, g_ref, b_ref, opre_ref, st_ref, dor_ref,
             dq_ref, dk_ref, dv_ref, dgr_ref, dg_ref, db_ref, dstate):
        i = pl.program_id(1)

        @pl.when(i == 0)
        def _():
            dstate[...] = jnp.zeros_like(dstate)
            dg_ref[...] = jnp.zeros_like(dg_ref)
            db_ref[...] = jnp.zeros_like(db_ref)

        q, k, v = q_ref[...], k_ref[...], v_ref[...]
        decay, xi, zeta_s = dec_ref[...], xi_ref[:, 0:1], zeta_ref[:, 0:1]
        gr = gr_ref[...].astype(F32)
        sg = _sigmoid(gr)
        gain = g_ref[...]
        yh, rstd, y = _gn_fwd(opre_ref[...], gain, b_ref[...])
        d_or_v = dor_ref[...]
        dy = d_or_v * (gr * sg)
        dgr_ref[...] = (d_or_v * y * (sg * (1.0 + gr * (1.0 - sg)))).astype(dgr_ref.dtype)
        dg_ref[...] += jnp.sum(dy * yh, axis=0, keepdims=True)
        db_ref[...] += jnp.sum(dy, axis=0, keepdims=True)
        dyh = dy * gain
        do = rstd * (dyh - jnp.mean(dyh, axis=-1, keepdims=True) - yh * jnp.mean(dyh * yh, axis=-1, keepdims=True))
        do_c = do.astype(CDT)
        dox = (do * xi).astype(CDT)
        a_c = (_dot_nt(q, k) * kscale * decay).astype(CDT)
        g_c = (_dot_nt(do_c, v) * decay).astype(CDT)
        dsn = dstate[...]
        dsn_c = dsn.astype(CDT)
        kz = (k.astype(F32) * zeta_s).astype(CDT)
        dq_ref[...] = (_dot_nn(g_c, k) * kscale + _dot_nt(dox, st_ref[...])).astype(dq_ref.dtype)
        dk_ref[...] = (_dot_tn(g_c, q) * kscale + _dot_nt(v, dsn_c) * zeta_s).astype(dk_ref.dtype)
        dv_ref[...] = (_dot_tn(a_c, do_c) + _dot_nn(kz, dsn_c)).astype(dv_ref.dtype)
        dstate[...] = dsn * gc_ref[0:1, 0:1] + _dot_tn(q, dox)

    rev = lambda i: N - 1 - i
    row = pl.BlockSpec((1, dv), lambda h, i: (0, h))
    vt = pl.BlockSpec((BLK, dv), lambda h, i: (rev(i), h))
    qt = pl.BlockSpec((BLK, dk), lambda h, i: (rev(i), h))
    return pl.pallas_call(
        body, name="ret_bwd", grid=(H, N),
        in_specs=_ret_specs(dk, dv, rev) + [row, row, vt,
                 pl.BlockSpec((None, None, dk, dv), lambda h, i: (rev(i), h, 0, 0)), vt],
        out_specs=[qt, qt, vt, vt, row, row],
        out_shape=[_sds((S, qk_w), CDT), _sds((S, qk_w), CDT), _sds((S, v_w), CDT), _sds((S, v_w), CDT),
                   _sds((1, v_w), F32), _sds((1, v_w), F32)],
        scratch_shapes=[pltpu.VMEM((dk, dv), F32)],
        compiler_params=_params(2),
    )(proj, proj, proj, proj, *_ret_tables(dk), gn_g, gn_b, o_pre, states, d_or)


def _merge_fwd(o_a, o_r, wa, wb, proj, d_model, tm=512, tn=256):
    S, in_w = proj.shape
    off_a, off_b = in_w - 2 * d_model, in_w - d_model
    assert off_a % tn == 0 and off_b % tn == 0

    def body(oa_ref, or_ref, wa_ref, wb_ref, ga_ref, gb_ref, y_ref, pa_ref, pb_ref):
        pa = _dot_nn(oa_ref[...], wa_ref[...])
        pb = _dot_nn(or_ref[...], wb_ref[...])
        y = _sigmoid(ga_ref[...].astype(F32)) * pa + _sigmoid(gb_ref[...].astype(F32)) * pb
        y_ref[...] = y.astype(y_ref.dtype)
        pa_ref[...] = pa.astype(pa_ref.dtype)
        pb_ref[...] = pb.astype(pb_ref.dtype)

    ka, kb = o_a.shape[1], o_r.shape[1]
    out = pl.BlockSpec((tm, tn), lambda i, j: (i, j))
    return pl.pallas_call(
        body, name="merge_fwd", grid=(S // tm, d_model // tn),
        in_specs=[pl.BlockSpec((tm, ka), lambda i, j: (i, 0)), pl.BlockSpec((tm, kb), lambda i, j: (i, 0)),
                  pl.BlockSpec((ka, tn), lambda i, j: (0, j)), pl.BlockSpec((kb, tn), lambda i, j: (0, j)),
                  pl.BlockSpec((tm, tn), lambda i, j: (i, off_a // tn + j)),
                  pl.BlockSpec((tm, tn), lambda i, j: (i, off_b // tn + j))],
        out_specs=[out, out, out], out_shape=[_sds((S, d_model), CDT)] * 3,
        compiler_params=_params(2))(o_a, o_r, wa, wb, proj, proj)


def _local_step(x, target, w, small):
    S, D = x.shape
    in_w = w["w_in"].shape[1]
    d_ff = w["w_up"].shape[1]
    ret_v_w = w["w_proj_b"].shape[0]
    dv = ret_v_w // RET_HEADS
    dk = (in_w - 3 * ATT_W - 2 * ret_v_w - 2 * D) // (2 * RET_HEADS)
    gqk = jnp.concatenate([small["q_norm_g"].reshape(1, ATT_W), small["k_norm_g"].reshape(1, ATT_W)], axis=1)
    g1, g2 = small["norm1_g"], small["norm2_g"]
    gn_g, gn_b = small["ret_gn_g"], small["ret_gn_b"]

    xn = _rms_fwd("rms1_fwd", x, g1)
    (proj,) = _matmul("in_proj", "nn", xn, w["w_in"], tm=1024, tn=640, tk=D,
                      outs=[(_sds((S, in_w), CDT), _mn(1024, 640))], epilogue=_ep_store)
    qkn = _qknorm_fwd(proj, gqk)
    att_ops = [_att_operands(g, qkn, proj) for g in range(3)]
    att = [_att_fwd(g, S, att_ops[g]) for g in range(3)]
    os_, ls_ = [a[0] for a in att], [a[1] for a in att]
    o_a = _mix_fwd(os_, ls_)
    o_pre, o_r, states = _ret_fwd(proj, gn_g, gn_b, dk, dv)
    y, pa, pb = _merge_fwd(o_a, o_r, w["w_proj_a"], w["w_proj_b"], proj, D)
    (x1,) = _matmul("out_proj", "nn", y, w["w_out"], tm=512, tn=D, tk=D,
                    extras=[(x, _mn(512, D))], outs=[(_sds((S, D), F32), _mn(512, D))], epilogue=_ep_resid)
    xn2 = _rms_fwd("rms2_fwd", x1, g2)
    hid, act = _matmul("mlp_up", "nn", xn2, w["w_up"], tm=1024, tn=512, tk=D,
                       outs=[(_sds((S, d_ff), CDT), _mn(1024, 512))] * 2, epilogue=_ep_up)
    dx2, dx2c, loss_row = _matmul(
        "mlp_down_loss", "nn", act, w["w_down"], tm=512, tn=D, tk=1024,
        extras=[(x1, _mn(512, D)), (target, _mn(512, D))],
        outs=[(_sds((S, D), F32), _mn(512, D)), (_sds((S, D), CDT), _mn(512, D)), (_sds((1, D), F32), _row(D))],
        epilogue=functools.partial(_ep_down_loss, inv_d=1.0 / D))
    loss = 0.5 * jnp.sum(loss_row) / D

    (dh,) = _matmul("d_hidden", "nt", dx2c, w["w_down"], tm=1024, tn=512, tk=D,
                    extras=[(hid, _mn(1024, 512))], outs=[(_sds((S, d_ff), CDT), _mn(1024, 512))], epilogue=_ep_dh)
    (gw_down,) = _matmul("dw_down", "tn", act, dx2c, tm=1024, tn=D, tk=1024,
                         outs=[(_sds((d_ff, D), F32), _mn(1024, D))], epilogue=_ep_store)
    (gw_up,) = _matmul("dw_up", "tn", xn2, dh, tm=D, tn=1024, tk=1024,
                       outs=[(_sds((D, d_ff), F32), _mn(D, 1024))], epilogue=_ep_store)
    dx1, dx1c, dg2 = _matmul(
        "d_x1", "nt", dh, w["w_up"], tm=512, tn=D, tk=1024,
        extras=[(x1, _mn(512, D)), (g2, _row(D)), (dx2, _mn(512, D))],
        outs=[(_sds((S, D), F32), _mn(512, D)), (_sds((S, D), CDT), _mn(512, D)), (_sds((1, D), F32), _row(D))],
        epilogue=_ep_rms_bwd)

    gt = 512
    assert (in_w - 2 * D) % gt == 0
    off_a, off_b = (in_w - 2 * D) // gt, (in_w - D) // gt
    dpa, dpb, dga, dgb = _matmul(
        "d_gates", "nt", dx1c, w["w_out"], tm=512, tn=gt, tk=D,
        extras=[(proj, _mn(512, gt, off_a)), (proj, _mn(512, gt, off_b)), (pa, _mn(512, gt)), (pb, _mn(512, gt))],
        outs=[(_sds((S, D), CDT), _mn(512, gt))] * 4, epilogue=_ep_gates)
    (gw_out,) = _matmul("dw_out", "tn", y, dx1c, tm=D, tn=D, tk=1024,
                        outs=[(_sds((D, D), F32), _mn(D, D))], epilogue=_ep_store)
    (do_a,) = _matmul("d_o_a", "nt", dpa, w["w_proj_a"], tm=1024, tn=GW, tk=D,
                      outs=[(_sds((S, GW), F32), _mn(1024, GW))], epilogue=_ep_store)
    (d_or,) = _matmul("d_o_r", "nt", dpb, w["w_proj_b"], tm=1024, tn=512, tk=D,
                      outs=[(_sds((S, ret_v_w), F32), _mn(1024, 512))], epilogue=_ep_store)
    (gw_pa,) = _matmul("dw_proj_a", "tn", o_a, dpa, tm=GW, tn=D, tk=1024,
                       outs=[(_sds((GW, D), F32), _mn(GW, D))], epilogue=_ep_store)
    (gw_pb,) = _matmul("dw_proj_b", "tn", o_r, dpb, tm=1024, tn=D, tk=1024,
                       outs=[(_sds((ret_v_w, D), F32), _mn(1024, D))], epilogue=_ep_store)

    dq_r, dk_r, dv_r, dgr, dgn_g, dgn_b = _ret_bwd(proj, gn_g, gn_b, o_pre, states, d_or, dk, dv)
    do_gs, c_gs = _mix_bwd(os_, ls_, do_a)
    datt_parts = [_att_bwd(g, S, att_ops[g], ls_[g], do_gs[g], c_gs[g]) for g in range(3)]
    d_att, dgqk = _qknorm_bwd(proj, gqk, [p[0] for p in datt_parts], [p[1] for p in datt_parts],
                              [p[2] for p in datt_parts])
    dproj = jnp.concatenate([d_att, dq_r, dk_r, dv_r, dgr, dga, dgb], axis=1)

    (gw_in,) = _matmul("dw_in", "tn", xn, dproj, tm=D, tn=640, tk=1024,
                       outs=[(_sds((D, in_w), F32), _mn(D, 640))], epilogue=_ep_store)
    grad_x, _, dg1 = _matmul(
        "d_x", "nt", dproj, w["w_in"], tm=512, tn=D, tk=1280,
        extras=[(x, _mn(512, D)), (g1, _row(D)), (dx1, _mn(512, D))],
        outs=[(_sds((S, D), F32), _mn(512, D)), (_sds((S, D), CDT), _mn(512, D)), (_sds((1, D), F32), _row(D))],
        epilogue=_ep_rms_bwd)

    big = {"w_in": gw_in, "w_proj_a": gw_pa, "w_proj_b": gw_pb, "w_out": gw_out, "w_up": gw_up, "w_down": gw_down}
    smallg = {"norm1_g": dg1, "q_norm_g": dgqk[:, :ATT_W], "k_norm_g": dgqk[:, ATT_W:],
              "ret_gn_g": dgn_g, "ret_gn_b": dgn_b, "norm2_g": dg2}
    return loss, grad_x, big, smallg


N_CHIPS = 4
N_DEV = 8


def _place():
    x, y, c = lax.axis_index("x"), lax.axis_index("y"), lax.axis_index("c")
    return x, y, c


def _other_chips(x, y):
    out = []
    for fx, fy in ((1, 0), (0, 1), (1, 1)):
        px = 1 - x if fx else x
        py = 1 - y if fy else y
        out.append(((px, py), 2 * px + py))
    return out


def _gather_weights(shards):
    n = len(shards)

    def body(*refs):
        ins, outs = refs[:n], refs[n:2 * n]
        send, recv, loc = refs[2 * n:]
        x, y, c = _place()
        me = 2 * x + y
        peers = _other_chips(x, y)
        local = [pltpu.make_async_copy(ins[w], outs[w].at[me], loc.at[w]) for w in range(n)]
        for cp in local:
            cp.start()

        def rows(w):
            half = ins[w].shape[0] // 2
            return pl.ds(c * half, half)

        sends = []
        for w in range(n):
            for j, ((px, py), _) in enumerate(peers):
                cp = pltpu.make_async_remote_copy(
                    src_ref=ins[w].at[rows(w), :], dst_ref=outs[w].at[me, rows(w), :], send_sem=send.at[w * 3 + j],
                    recv_sem=recv.at[w * 3 + j], device_id=(px, py, c), device_id_type=MESH)
                cp.start()
                sends.append(cp)
        for w in range(n):
            for j, ((px, py), pidx) in enumerate(peers):
                pltpu.make_async_remote_copy(
                    src_ref=ins[w].at[rows(w), :], dst_ref=outs[w].at[pidx, rows(w), :], send_sem=send.at[w * 3 + j],
                    recv_sem=recv.at[w * 3 + j], device_id=(px, py, c), device_id_type=MESH).wait_recv()
        for cp in sends:
            cp.wait_send()
        for cp in local:
            cp.wait()

    return pl.pallas_call(
        body, name="gather_weights",
        in_specs=[HBM_SPEC] * n, out_specs=[HBM_SPEC] * n,
        out_shape=[_sds((N_CHIPS,) + s.shape, s.dtype) for s in shards],
        scratch_shapes=[pltpu.SemaphoreType.DMA((3 * n,)), pltpu.SemaphoreType.DMA((3 * n,)),
                        pltpu.SemaphoreType.DMA((n,))],
    )(*shards)


PAIR_TILE_ELEMS = 1 << 19


def _pair_fill(name, gathered, core, others):
    k, r, C = gathered.shape
    half = r // 2
    tr = _row_tile(half, C, PAIR_TILE_ELEMS, mult=16)
    nt = half // tr

    def body(c_ref, o_ref, in_ref, out_ref, slot, send, recv):
        b = (pl.program_id(0) * nt + pl.program_id(1)) % 2
        x, y, c = _place()
        cp = pltpu.make_async_remote_copy(src_ref=in_ref, dst_ref=slot.at[b], send_sem=send.at[b],
                                          recv_sem=recv.at[b], device_id=(x, y, 1 - c), device_id_type=MESH)
        cp.start()
        cp.wait_recv()
        out_ref[...] = slot[b]
        cp.wait_send()

    grid_spec = pltpu.PrefetchScalarGridSpec(
        num_scalar_prefetch=2, grid=(N_CHIPS - 1, nt),
        in_specs=[pl.BlockSpec((tr, C), lambda j, i, c, o: ((2 * o[j] + c[0]) * nt + i, 0))],
        out_specs=pl.BlockSpec((tr, C), lambda j, i, c, o: ((2 * o[j] + 1 - c[0]) * nt + i, 0)),
        scratch_shapes=[pltpu.VMEM((2, tr, C), gathered.dtype), pltpu.SemaphoreType.DMA((2,)),
                        pltpu.SemaphoreType.DMA((2,))])
    out = pl.pallas_call(body, name=name, grid_spec=grid_spec, out_shape=_sds((k * r, C), gathered.dtype),
                         input_output_aliases={2: 0}, compiler_params=_params(2))(
                             core, others, gathered.reshape(k * r, C))
    return out.reshape(k, r, C)


def _pair_reduce(name, g, core):
    k, R, C = g.shape
    half = R // 2
    tr = _row_tile(half, C, PAIR_TILE_ELEMS)
    nt = half // tr

    def body(c_ref, mine_ref, give_ref, out_ref, wire_ref, slot, send, recv):
        b = (pl.program_id(0) * nt + pl.program_id(1)) % 2
        x, y, c = _place()
        cp = pltpu.make_async_remote_copy(src_ref=give_ref, dst_ref=slot.at[b], send_sem=send.at[b],
                                          recv_sem=recv.at[b], device_id=(x, y, 1 - c), device_id_type=MESH)
        cp.start()
        cp.wait_recv()
        tot = mine_ref[...] + slot[b]
        out_ref[...] = tot
        wire_ref[...] = tot.astype(wire_ref.dtype)
        cp.wait_send()

    blk = (tr, C)
    out_spec = pl.BlockSpec(blk, lambda s, i, c: (s * nt + i, 0))
    grid_spec = pltpu.PrefetchScalarGridSpec(
        num_scalar_prefetch=1, grid=(k, nt),
        in_specs=[pl.BlockSpec(blk, lambda s, i, c: ((2 * s + c[0]) * nt + i, 0)),
                  pl.BlockSpec(blk, lambda s, i, c: ((2 * s + 1 - c[0]) * nt + i, 0))],
        out_specs=[out_spec, out_spec],
        scratch_shapes=[pltpu.VMEM((2, tr, C), F32), pltpu.SemaphoreType.DMA((2,)), pltpu.SemaphoreType.DMA((2,))])
    g2 = g.reshape(k * R, C)
    out, wire = pl.pallas_call(body, name=name, grid_spec=grid_spec,
                               out_shape=[_sds((k * half, C), F32), _sds((k * half, C), CDT)],
                               compiler_params=_params(2))(core, g2, g2)
    return out, wire.reshape(k, half, C)


def _chip_scatter(parts):
    n = len(parts)

    def body(*refs):
        ins, outs = refs[:n], refs[n:2 * n]
        send, recv, loc = refs[2 * n:]
        x, y, c = _place()
        me = 2 * x + y
        peers = _other_chips(x, y)
        local, sends = [], []
        for w in range(n):
            lc = pltpu.make_async_copy(ins[w].at[me], outs[w].at[me], loc.at[w])
            lc.start()
            local.append(lc)
            for j, ((px, py), pidx) in enumerate(peers):
                cp = pltpu.make_async_remote_copy(
                    src_ref=ins[w].at[pidx], dst_ref=outs[w].at[me], send_sem=send.at[w * 3 + j],
                    recv_sem=recv.at[w * 3 + j], device_id=(px, py, c), device_id_type=MESH)
                cp.start()
                sends.append(cp)
        for w in range(n):
            for j, ((px, py), pidx) in enumerate(peers):
                pltpu.make_async_remote_copy(
                    src_ref=ins[w].at[pidx], dst_ref=outs[w].at[pidx], send_sem=send.at[w * 3 + j],
                    recv_sem=recv.at[w * 3 + j], device_id=(px, py, c), device_id_type=MESH).wait_recv()
        for cp in sends:
            cp.wait_send()
        for lc in local:
            lc.wait()

    return pl.pallas_call(
        body, name="chip_scatter", in_specs=[HBM_SPEC] * n, out_specs=[HBM_SPEC] * n,
        out_shape=[_sds(p.shape, p.dtype) for p in parts],
        scratch_shapes=[pltpu.SemaphoreType.DMA((3 * n,)), pltpu.SemaphoreType.DMA((3 * n,)),
                        pltpu.SemaphoreType.DMA((n,))],
    )(*parts)


def _pair_share(name, f, core):
    half, C = f.shape
    tr = _row_tile(half, C, PAIR_TILE_ELEMS)
    nt = half // tr

    def body(c_ref, f_ref, out_ref, slot, send, recv):
        p = pl.program_id(1)
        b = pl.program_id(0) % 2
        x, y, c = _place()
        cp = pltpu.make_async_remote_copy(src_ref=f_ref, dst_ref=slot.at[b], send_sem=send.at[b],
                                          recv_sem=recv.at[b], device_id=(x, y, 1 - c), device_id_type=MESH)

        @pl.when(p == 0)
        def _():
            cp.start()
            out_ref[...] = f_ref[...]

        @pl.when(p == 1)
        def _():
            cp.wait_recv()
            out_ref[...] = slot[b]
            cp.wait_send()

    grid_spec = pltpu.PrefetchScalarGridSpec(
        num_scalar_prefetch=1, grid=(nt, 2),
        in_specs=[pl.BlockSpec((tr, C), lambda i, p, c: (i, 0))],
        out_specs=pl.BlockSpec((tr, C), lambda i, p, c: (jnp.where(p == 0, c[0], 1 - c[0]) * nt + i, 0)),
        scratch_shapes=[pltpu.VMEM((2, tr, C), F32), pltpu.SemaphoreType.DMA((2,)), pltpu.SemaphoreType.DMA((2,))])
    return pl.pallas_call(body, name=name, grid_spec=grid_spec, out_shape=_sds((2 * half, C), F32),
                          compiler_params=_params(2))(core, f)


def _all_reduce_small(v):
    r, cdim = v.shape

    def body(v_ref, o_ref, buf, send, recv):
        x, y, c = _place()
        me = 4 * x + 2 * y + c
        buf[me] = v_ref[...]
        sends = []
        for m in range(1, N_DEV):
            px = 1 - x if m & 4 else x
            py = 1 - y if m & 2 else y
            pc = 1 - c if m & 1 else c
            cp = pltpu.make_async_remote_copy(src_ref=v_ref, dst_ref=buf.at[me], send_sem=send.at[m - 1],
                                              recv_sem=recv.at[m - 1], device_id=(px, py, pc), device_id_type=MESH)
            cp.start()
            sends.append((cp, 4 * px + 2 * py + pc))
        for m, (cp, pidx) in enumerate(sends):
            pltpu.make_async_remote_copy(src_ref=v_ref, dst_ref=buf.at[pidx], send_sem=send.at[m], recv_sem=recv.at[m],
                                         device_id=(x, y, c), device_id_type=MESH).wait_recv()
        for cp, _ in sends:
            cp.wait_send()
        tot = buf[0]
        for k in range(1, N_DEV):
            tot = tot + buf[k]
        o_ref[...] = tot

    return pl.pallas_call(
        body, name="all_reduce_small", in_specs=[VMEM_SPEC], out_specs=VMEM_SPEC,
        out_shape=_sds((r, cdim), F32),
        scratch_shapes=[pltpu.VMEM((N_DEV, r, cdim), F32), pltpu.SemaphoreType.DMA((N_DEV - 1,)),
                        pltpu.SemaphoreType.DMA((N_DEV - 1,))],
    )(v)


def _row_tile(rows, cols, budget_elems=1 << 18, mult=8):
    if rows % mult:
        return rows
    t = max(mult, (budget_elems // cols) // mult * mult)
    while rows % t:
        t -= mult
    return t


def _sum4(name, own, by_chip, chip):
    _, R, C = by_chip.shape
    tr = _row_tile(R, C, mult=16)
    nt = R // tr

    def body(chip_ref, own_ref, by_ref, o_ref):
        tot = None
        for k in range(N_CHIPS):
            term = jnp.where(chip_ref[0] == k, own_ref[...], by_ref[k].astype(F32))
            tot = term if tot is None else tot + term
        o_ref[...] = tot

    grid_spec = pltpu.PrefetchScalarGridSpec(
        num_scalar_prefetch=1, grid=(nt,),
        in_specs=[pl.BlockSpec((tr, C), lambda i, chip: (chip[0] * nt + i, 0)),
                  pl.BlockSpec((N_CHIPS, tr, C), lambda i, chip: (0, i, 0))],
        out_specs=pl.BlockSpec((tr, C), lambda i, chip: (i, 0)))
    return pl.pallas_call(body, name=name, grid_spec=grid_spec, out_shape=_sds((R, C), F32),
                          compiler_params=_params(1))(chip, own, by_chip)


def _adamw(name, w, g, m, v):
    R, C = w.shape
    tr = _row_tile(R, C, 1 << 17)

    def body(w_ref, g_ref, m_ref, v_ref, d_ref, nm_ref, nv_ref):
        gv = g_ref[...]
        nm = ADAM_B1 * m_ref[...] + (1.0 - ADAM_B1) * gv
        nv = ADAM_B2 * v_ref[...] + (1.0 - ADAM_B2) * (gv * gv)
        m_hat = nm / (1.0 - ADAM_B1 ** ADAM_STEP)
        v_hat = nv / (1.0 - ADAM_B2 ** ADAM_STEP)
        d_ref[...] = -ADAM_LR * (m_hat / (jnp.sqrt(v_hat) + ADAM_EPS) + ADAM_WD * w_ref[...])
        nm_ref[...] = nm
        nv_ref[...] = nv

    spec = pl.BlockSpec((tr, C), lambda i: (i, 0))
    return pl.pallas_call(body, name=name, grid=(R // tr,), in_specs=[spec] * 4, out_specs=[spec] * 3,
                          out_shape=[_sds((R, C), F32)] * 3, compiler_params=_params(1))(w, g, m, v)


BIG = ("w_in", "w_proj_a", "w_proj_b", "w_out", "w_up", "w_down")
COL_SHARDED = ("w_in", "w_proj_a", "w_up")
SMALL = ("norm1_g", "q_norm_g", "k_norm_g", "ret_gn_g", "ret_gn_b", "norm2_g")
ALL_W = ("norm1_g", "w_in", "q_norm_g", "k_norm_g", "ret_gn_g", "ret_gn_b", "w_proj_a", "w_proj_b", "w_out",
         "norm2_g", "w_up", "w_down")
LANES = 128


def _to_full(name, gathered):
    k, r, c = gathered.shape
    if name in COL_SHARDED:
        return gathered.transpose(1, 0, 2).reshape(r, k * c)
    return gathered.reshape(k * r, c)


def _to_shard_major(name, full):
    if name in COL_SHARDED:
        r, c4 = full.shape
        return full.reshape(r, N_CHIPS, c4 // N_CHIPS).transpose(1, 0, 2)
    r4, c = full.shape
    return full.reshape(N_CHIPS, r4 // N_CHIPS, c)


def kernel(x, norm1_g, w_in, q_norm_g, k_norm_g, ret_gn_g, ret_gn_b, w_proj_a, w_proj_b, w_out, norm2_g, w_up, w_down, loss_target, m_norm1_g, m_w_in, m_q_norm_g, m_k_norm_g, m_ret_gn_g, m_ret_gn_b, m_w_proj_a, m_w_proj_b, m_w_out, m_norm2_g, m_w_up, m_w_down, v_norm1_g, v_w_in, v_q_norm_g, v_k_norm_g, v_ret_gn_g, v_ret_gn_b, v_w_proj_a, v_w_proj_b, v_w_out, v_norm2_g, v_w_up, v_w_down):
    weights = dict(norm1_g=norm1_g, w_in=w_in, q_norm_g=q_norm_g, k_norm_g=k_norm_g, ret_gn_g=ret_gn_g,
                   ret_gn_b=ret_gn_b, w_proj_a=w_proj_a, w_proj_b=w_proj_b, w_out=w_out, norm2_g=norm2_g,
                   w_up=w_up, w_down=w_down)
    moments_m = dict(norm1_g=m_norm1_g, w_in=m_w_in, q_norm_g=m_q_norm_g, k_norm_g=m_k_norm_g, ret_gn_g=m_ret_gn_g,
                     ret_gn_b=m_ret_gn_b, w_proj_a=m_w_proj_a, w_proj_b=m_w_proj_b, w_out=m_w_out,
                     norm2_g=m_norm2_g, w_up=m_w_up, w_down=m_w_down)
    moments_v = dict(norm1_g=v_norm1_g, w_in=v_w_in, q_norm_g=v_q_norm_g, k_norm_g=v_k_norm_g, ret_gn_g=v_ret_gn_g,
                     ret_gn_b=v_ret_gn_b, w_proj_a=v_w_proj_a, w_proj_b=v_w_proj_b, w_out=v_w_out,
                     norm2_g=v_norm2_g, w_up=v_w_up, w_down=v_w_down)

    mx, my = lax.axis_index("x"), lax.axis_index("y")
    core = lax.axis_index("c").astype(jnp.int32).reshape(1)
    chip = (2 * mx + my).astype(jnp.int32).reshape(1)
    others = jnp.stack([2 * (1 - mx) + my, 2 * mx + 1 - my, 2 * (1 - mx) + 1 - my]).astype(jnp.int32)
    shards = [weights[n][0].astype(CDT) for n in BIG]
    gathered = [_pair_fill("pair_fill_%s" % n, g, core, others) for n, g in zip(BIG, _gather_weights(shards))]
    full = {n: _to_full(n, g) for n, g in zip(BIG, gathered)}
    small = {n: weights[n].reshape(1, -1) for n in SMALL}

    loss, grad_x, big_g, small_g = _local_step(x[0], loss_target[0], full, small)
    loss = lax.psum(loss, ("x", "y", "c"))

    reduced = [_pair_reduce("pair_reduce_%s" % n, _to_shard_major(n, big_g[n]), core) for n in BIG]
    by_chip = _chip_scatter([wire for _, wire in reduced])
    halves = [_sum4("chip_sum_%s" % n, own, got, chip) for n, (own, _), got in zip(BIG, reduced, by_chip)]
    grads = {n: _pair_share("pair_share_%s" % n, h, core) for n, h in zip(BIG, halves)}

    packed = jnp.concatenate([small_g[n].reshape(1, -1) for n in SMALL], axis=1)
    sizes = [small_g[n].size for n in SMALL]
    red = _all_reduce_small(packed.reshape(-1, LANES)).reshape(1, -1)
    off = 0
    for n, sz in zip(SMALL, sizes):
        grads[n] = red[:, off:off + sz]
        off += sz

    out_g, out_d, out_m, out_v = {}, {}, {}, {}
    for n in ALL_W:
        shape = weights[n].shape
        two_d = (shape[-2], shape[-1]) if n in BIG else (1, weights[n].size)
        g2 = grads[n].reshape(two_d)
        d, nm, nv = _adamw("adamw_%s" % n, weights[n].reshape(two_d), g2, moments_m[n].reshape(two_d),
                           moments_v[n].reshape(two_d))
        out_g[n], out_d[n], out_m[n], out_v[n] = (t.reshape(shape) for t in (g2, d, nm, nv))

    return (loss, grad_x[None], *[out_g[n] for n in ALL_W], *[out_d[n] for n in ALL_W],
            *[out_m[n] for n in ALL_W], *[out_v[n] for n in ALL_W])
```

```python
import functools
import math

import jax
import jax.numpy as jnp
from jax import lax
from jax.experimental import pallas as pl
from jax.experimental.pallas import tpu as pltpu

CDT = jnp.bfloat16
F32 = jnp.float32
EPS = 1e-6

ATT_GROUPS = ((128, 1), (512, 4), (2048, 16))
ATT_HPG = 4
ATT_HEADS = 12
HD = 128
BLK = 128
ATT_W = ATT_HEADS * HD
GW = ATT_HPG * HD
RET_HEADS = 4

ADAM_LR = 0.001
ADAM_B1 = 0.9
ADAM_B2 = 0.999
ADAM_EPS = 1e-08
ADAM_WD = 0.01
ADAM_STEP = 10

VMEM_LIMIT_BYTES = 48 * 1024 * 1024
MESH = pl.DeviceIdType.MESH
HBM_SPEC = pl.BlockSpec(memory_space=pltpu.HBM)
VMEM_SPEC = pl.BlockSpec(memory_space=pltpu.VMEM)


def _params(n_axes):
    return pltpu.CompilerParams(dimension_semantics=("arbitrary",) * n_axes,
                                vmem_limit_bytes=VMEM_LIMIT_BYTES)


def _dot_nn(a, b):
    return jnp.dot(a, b, preferred_element_type=F32)


def _dot_nt(a, b):
    return lax.dot_general(a, b, (((1,), (1,)), ((), ())), preferred_element_type=F32)


def _dot_tn(a, b):
    return lax.dot_general(a, b, (((0,), (0,)), ((), ())), preferred_element_type=F32)


def _sigmoid(v):
    return 1.0 / (1.0 + jnp.exp(-v))


def _matmul(name, mode, a, b, *, tm, tn, tk, extras=(), outs, epilogue, deps=()):
    deps = [d for d in deps if d is not None]
    if mode == "nn":
        (M, K), (K2, N) = a.shape, b.shape
    elif mode == "nt":
        (M, K), (N, K2) = a.shape, b.shape
    else:
        (K, M), (K2, N) = a.shape, b.shape
    assert K == K2 and M % tm == 0 and N % tn == 0 and K % tk == 0, (name, a.shape, b.shape)
    ni, nj, nk = M // tm, N // tn, K // tk
    if mode == "tn":
        a_spec = pl.BlockSpec((tk, tm), lambda i, j, k: (k, i))
    else:
        a_spec = pl.BlockSpec((tm, tk), lambda i, j, k: (i, k))
    if mode == "nt":
        b_spec = pl.BlockSpec((tn, tk), lambda i, j, k: (j, k))
    else:
        b_spec = pl.BlockSpec((tk, tn), lambda i, j, k: (k, j))
    dot = {"nn": _dot_nn, "nt": _dot_nt, "tn": _dot_tn}[mode]
    n_ex, n_out, n_dep = len(extras), len(outs), len(deps)

    def body(*refs):
        a_ref, b_ref = refs[0], refs[1]
        ex = refs[2:2 + n_ex]
        out = refs[2 + n_ex + n_dep:2 + n_ex + n_dep + n_out]
        acc = refs[-1]
        i = pl.program_id(0)
        k = pl.program_id(2)

        @pl.when(k == 0)
        def _():
            acc[...] = jnp.zeros_like(acc)

        acc[...] += dot(a_ref[...].astype(CDT), b_ref[...].astype(CDT))

        @pl.when(k == nk - 1)
        def _():
            epilogue(acc[...], ex, out, i)

    res = pl.pallas_call(
        body, name=name, grid=(ni, nj, nk),
        in_specs=[a_spec, b_spec] + [s for _, s in extras] + [pl.BlockSpec(memory_space=pl.ANY)] * n_dep,
        out_specs=[s for _, s in outs],
        out_shape=[o for o, _ in outs],
        scratch_shapes=[pltpu.VMEM((tm, tn), F32)],
        compiler_params=_params(3),
    )(a, b, *[e for e, _ in extras], *deps)
    return res


def _mn(tm, tn, col_off=0):
    return pl.BlockSpec((tm, tn), lambda i, j, k: (i, j + col_off))


def _row(tn):
    return pl.BlockSpec((1, tn), lambda i, j, k: (0, j))


def _ep_store(acc, ex, out, i):
    out[0][...] = acc.astype(out[0].dtype)


def _ep_resid(acc, ex, out, i):
    out[0][...] = ex[0][...] + acc


def _ep_up(acc, ex, out, i):
    out[0][...] = acc.astype(out[0].dtype)
    r = jnp.maximum(acc, 0.0)
    out[1][...] = (r * r).astype(out[1].dtype)


def _ep_down_loss(acc, ex, out, i, inv_d):
    diff = (ex[0][...] + acc) - ex[1][...]
    dx2 = diff * inv_d
    out[0][...] = dx2
    out[1][...] = dx2.astype(out[1].dtype)

    @pl.when(i == 0)
    def _():
        out[2][...] = jnp.zeros_like(out[2])

    out[2][...] += jnp.sum(diff * diff, axis=0, keepdims=True)


def _ep_dh(acc, ex, out, i):
    h = ex[0][...].astype(F32)
    out[0][...] = (acc * (2.0 * jnp.maximum(h, 0.0))).astype(out[0].dtype)


def _ep_rms_bwd(acc, ex, out, i):
    x = ex[0][...]
    g = ex[1][...]
    rstd = lax.rsqrt(jnp.mean(x * x, axis=-1, keepdims=True) + EPS)
    xh = x * rstd
    dxh = acc * g
    dx = ex[2][...] + rstd * (dxh - xh * jnp.mean(dxh * xh, axis=-1, keepdims=True))
    out[0][...] = dx
    out[1][...] = dx.astype(out[1].dtype)

    @pl.when(i == 0)
    def _():
        out[2][...] = jnp.zeros_like(out[2])

    out[2][...] += jnp.sum(acc * xh, axis=0, keepdims=True)


def _ep_gates(acc, ex, out, i):
    sa = _sigmoid(ex[0][...].astype(F32))
    sb = _sigmoid(ex[1][...].astype(F32))
    dpa = acc * sa
    dpb = acc * sb
    out[0][...] = dpa.astype(out[0].dtype)
    out[1][...] = dpb.astype(out[1].dtype)
    out[2][...] = (dpa * ex[2][...].astype(F32) * (1.0 - sa)).astype(out[2].dtype)
    out[3][...] = (dpb * ex[3][...].astype(F32) * (1.0 - sb)).astype(out[3].dtype)


def _sds(shape, dtype):
    return jax.ShapeDtypeStruct(shape, dtype)


def _rms_fwd(name, x, g, tm=512):
    S, D = x.shape

    def body(x_ref, g_ref, o_ref):
        xv = x_ref[...]
        rstd = lax.rsqrt(jnp.mean(xv * xv, axis=-1, keepdims=True) + EPS)
        o_ref[...] = (xv * rstd * g_ref[...]).astype(o_ref.dtype)

    return pl.pallas_call(
        body, name=name, grid=(S // tm,),
        in_specs=[pl.BlockSpec((tm, D), lambda i: (i, 0)), pl.BlockSpec((1, D), lambda i: (0, 0))],
        out_specs=pl.BlockSpec((tm, D), lambda i: (i, 0)),
        out_shape=_sds((S, D), CDT), compiler_params=_params(1))(x, g)


def _qknorm_fwd(proj, gqk, tm=512):
    S = proj.shape[0]
    W = 2 * ATT_W

    def body(p_ref, g_ref, o_ref):
        for hd in range(2 * ATT_HEADS):
            sl = slice(hd * HD, (hd + 1) * HD)
            v = p_ref[:, sl].astype(F32)
            rstd = lax.rsqrt(jnp.mean(v * v, axis=-1, keepdims=True) + EPS)
            o_ref[:, sl] = (v * rstd * g_ref[:, sl]).astype(o_ref.dtype)

    return pl.pallas_call(
        body, name="qknorm_fwd", grid=(S // tm,),
        in_specs=[pl.BlockSpec((tm, W), lambda i: (i, 0)), pl.BlockSpec((1, W), lambda i: (0, 0))],
        out_specs=pl.BlockSpec((tm, W), lambda i: (i, 0)),
        out_shape=_sds((S, W), CDT), compiler_params=_params(1))(proj, gqk)


def _qknorm_bwd(proj, gqk, dqs, dks, dvs, tm=256):
    S = proj.shape[0]
    W = 2 * ATT_W

    def body(p_ref, g_ref, *refs):
        dq_refs, dk_refs, dv_refs = refs[0:3], refs[3:6], refs[6:9]
        o_ref, dg_ref = refs[9], refs[10]
        i = pl.program_id(0)

        @pl.when(i == 0)
        def _():
            dg_ref[...] = jnp.zeros_like(dg_ref)

        for hd in range(2 * ATT_HEADS):
            sl = slice(hd * HD, (hd + 1) * HD)
            head = hd % ATT_HEADS
            grp, slot = head // ATT_HPG, head % ATT_HPG
            src = (dq_refs if hd < ATT_HEADS else dk_refs)[grp]
            dn = src[:, slot * HD:(slot + 1) * HD]
            v = p_ref[:, sl].astype(F32)
            rstd = lax.rsqrt(jnp.mean(v * v, axis=-1, keepdims=True) + EPS)
            vh = v * rstd
            dg_ref[:, sl] += jnp.sum(dn * vh, axis=0, keepdims=True)
            dvh = dn * g_ref[:, sl]
            o_ref[:, sl] = (rstd * (dvh - vh * jnp.mean(dvh * vh, axis=-1, keepdims=True))).astype(o_ref.dtype)
        for grp in range(3):
            o_ref[:, W + grp * GW:W + (grp + 1) * GW] = dv_refs[grp][...].astype(o_ref.dtype)

    g512 = pl.BlockSpec((tm, GW), lambda i: (i, 0))
    return pl.pallas_call(
        body, name="qknorm_bwd", grid=(S // tm,),
        in_specs=[pl.BlockSpec((tm, W), lambda i: (i, 0)), pl.BlockSpec((1, W), lambda i: (0, 0))] + [g512] * 9,
        out_specs=[pl.BlockSpec((tm, 3 * ATT_W), lambda i: (i, 0)), pl.BlockSpec((1, W), lambda i: (0, 0))],
        out_shape=[_sds((S, 3 * ATT_W), CDT), _sds((1, W), F32)],
        compiler_params=_params(1))(proj, gqk, *dqs, *dks, *dvs)


def _att_mask(n):
    qi = lax.broadcasted_iota(jnp.int32, (BLK, 2 * BLK), 0)
    kj = lax.broadcasted_iota(jnp.int32, (BLK, 2 * BLK), 1)
    dist = BLK + qi - kj
    valid = (dist >= 0) & (dist <= BLK) & ((kj >= BLK) | (n > 0))
    return valid, dist.astype(F32)


def _att_slopes(grp):
    return [2.0 ** (-8.0 * (grp * ATT_HPG + hh + 1) / ATT_HEADS) for hh in range(ATT_HPG)]


def _att_operands(grp, qkn, proj):
    S = proj.shape[0]
    _, d = ATT_GROUPS[grp]
    L = S // d
    last = L // BLK - 1
    if d == 1:
        arrays = [qkn, qkn, qkn, proj, proj]
        cols = [lambda r: grp, lambda r: 3 + grp, lambda r: 3 + grp, lambda r: 6 + grp, lambda r: 6 + grp]
    else:
        gs = slice(grp * GW, (grp + 1) * GW)
        qkv = jnp.concatenate([qkn[:, gs], qkn[:, ATT_W:][:, gs], proj[:, 2 * ATT_W:][:, gs]], axis=1)
        arrays = [qkv.reshape(L, d * 3 * GW)] * 5
        cols = [lambda r: 3 * r, lambda r: 3 * r + 1, lambda r: 3 * r + 1, lambda r: 3 * r + 2, lambda r: 3 * r + 2]

    def cur(n):
        return jnp.minimum(n, last)

    def prev(n):
        return jnp.maximum(jnp.minimum(n, last) - 1, 0)

    rows = [cur, prev, cur, prev, cur]
    specs = [pl.BlockSpec((BLK, GW), functools.partial(lambda r, n, rf, cf: (rf(n), cf(r)), rf=rf, cf=cf))
             for rf, cf in zip(rows, cols)]
    return arrays, specs


def _att_fwd(grp, S, operands):
    _, d = ATT_GROUPS[grp]
    L = S // d
    nb = L // BLK
    slopes = _att_slopes(grp)
    scale = HD ** -0.5
    arrays, specs = operands

    def body(q_ref, kp_ref, kc_ref, vp_ref, vc_ref, o_ref, l_ref):
        n = pl.program_id(1)
        valid, distf = _att_mask(n)
        for hh in range(ATT_HPG):
            sl = slice(hh * HD, (hh + 1) * HD)
            k = jnp.concatenate([kp_ref[:, sl], kc_ref[:, sl]], axis=0)
            v = jnp.concatenate([vp_ref[:, sl], vc_ref[:, sl]], axis=0)
            s = _dot_nt(q_ref[:, sl], k) * scale + (-slopes[hh] * d) * distf
            s = jnp.where(valid, s, -1e30)
            m = jnp.max(s, axis=-1, keepdims=True)
            p = jnp.exp(s - m)
            den = jnp.sum(p, axis=-1, keepdims=True)
            o_ref[:, sl] = _dot_nn(p.astype(CDT), v) / den
            l_ref[:, sl] = jnp.broadcast_to(m + jnp.log(den), (BLK, HD))

    out_spec = pl.BlockSpec((BLK, GW), lambda r, n: (n, r))
    o, l = pl.pallas_call(
        body, name="att_fwd_g%d" % grp, grid=(d, nb),
        in_specs=specs,
        out_specs=[out_spec, out_spec],
        out_shape=[_sds((L, d * GW), F32), _sds((L, d * GW), F32)],
        compiler_params=_params(2),
    )(*arrays)
    return o.reshape(S, GW), l.reshape(S, GW)


def _att_bwd(grp, S, operands, lse, do_g, c_g):
    arrays, specs = operands
    _, d = ATT_GROUPS[grp]
    L = S // d
    nb = L // BLK
    slopes = _att_slopes(grp)
    scale = HD ** -0.5
    last = nb - 1

    def body(q_ref, kp_ref, kc_ref, vp_ref, vc_ref, l_ref, do_ref, c_ref, dq_ref, dk_ref, dv_ref, ck, cv):
        n = pl.program_id(1)

        @pl.when(n == 0)
        def _():
            ck[...] = jnp.zeros_like(ck)
            cv[...] = jnp.zeros_like(cv)

        @pl.when(n < nb)
        def _():
            valid, distf = _att_mask(n)
            for hh in range(ATT_HPG):
                sl = slice(hh * HD, (hh + 1) * HD)
                q = q_ref[:, sl]
                k = jnp.concatenate([kp_ref[:, sl], kc_ref[:, sl]], axis=0)
                v = jnp.concatenate([vp_ref[:, sl], vc_ref[:, sl]], axis=0)
                do = do_ref[:, sl]
                s = _dot_nt(q, k) * scale + (-slopes[hh] * d) * distf
                p = jnp.where(valid, jnp.exp(s - l_ref[:, sl][:, 0:1]), 0.0)
                dp = _dot_nt(do, v)
                ds = (p * (dp + c_ref[:, sl][:, 0:1]) * scale).astype(CDT)
                dq_ref[:, sl] = _dot_nn(ds, k)
                dk = _dot_tn(ds, q)
                dv = _dot_tn(p.astype(CDT), do)
                dk_ref[:, sl] = ck[:, sl] + dk[0:BLK]
                dv_ref[:, sl] = cv[:, sl] + dv[0:BLK]
                ck[:, sl] = dk[BLK:2 * BLK]
                cv[:, sl] = dv[BLK:2 * BLK]

        @pl.when(n == nb)
        def _():
            dk_ref[...] = ck[...]
            dv_ref[...] = cv[...]

    blk = (BLK, GW)
    at_q = pl.BlockSpec(blk, lambda r, n: (jnp.minimum(n, last), r))
    behind = pl.BlockSpec(blk, lambda r, n: (jnp.maximum(n - 1, 0), r))
    dq, dk, dv = pl.pallas_call(
        body, name="att_bwd_g%d" % grp, grid=(d, nb + 1),
        in_specs=specs + [at_q, at_q, at_q],
        out_specs=[at_q, behind, behind],
        out_shape=[_sds((L, d * GW), F32)] * 3,
        scratch_shapes=[pltpu.VMEM(blk, F32), pltpu.VMEM(blk, F32)],
        compiler_params=_params(2),
    )(*arrays, lse.reshape(L, d * GW), do_g.reshape(L, d * GW), c_g.reshape(L, d * GW))
    return dq.reshape(S, GW), dk.reshape(S, GW), dv.reshape(S, GW)


def _head_sum_bcast(v):
    parts = []
    for j in range(ATT_HPG):
        sj = jnp.sum(v[:, j * HD:(j + 1) * HD], axis=-1, keepdims=True)
        parts.append(jnp.broadcast_to(sj, (v.shape[0], HD)))
    return jnp.concatenate(parts, axis=1)


def _mix_alpha(l_refs):
    l0, l1, l2 = l_refs[0][...], l_refs[1][...], l_refs[2][...]
    mx = jnp.maximum(jnp.maximum(l0, l1), l2)
    e = [jnp.exp(l0 - mx), jnp.exp(l1 - mx), jnp.exp(l2 - mx)]
    tot = e[0] + e[1] + e[2]
    return [ei / tot for ei in e]


def _mix_fwd(os_, ls_, tm=512):
    S = os_[0].shape[0]

    def body(o0, o1, o2, l0, l1, l2, out):
        al = _mix_alpha((l0, l1, l2))
        out[...] = (al[0] * o0[...] + al[1] * o1[...] + al[2] * o2[...]).astype(out.dtype)

    spec = pl.BlockSpec((tm, GW), lambda i: (i, 0))
    return pl.pallas_call(
        body, name="mix_fwd", grid=(S // tm,), in_specs=[spec] * 6, out_specs=spec,
        out_shape=_sds((S, GW), CDT), compiler_params=_params(1))(*os_, *ls_)


def _mix_bwd(os_, ls_, do_a, tm=512):
    S = os_[0].shape[0]

    def body(o0, o1, o2, l0, l1, l2, d_ref, g0, g1, g2, c0, c1, c2):
        al = _mix_alpha((l0, l1, l2))
        dv = d_ref[...]
        o_a = al[0] * o0[...] + al[1] * o1[...] + al[2] * o2[...]
        dsum = _head_sum_bcast(dv * o_a)
        for a, g_ref, c_ref in zip(al, (g0, g1, g2), (c0, c1, c2)):
            g_ref[...] = (a * dv).astype(g_ref.dtype)
            c_ref[...] = -(a * dsum)

    spec = pl.BlockSpec((tm, GW), lambda i: (i, 0))
    res = pl.pallas_call(
        body, name="mix_bwd", grid=(S // tm,), in_specs=[spec] * 7, out_specs=[spec] * 6,
        out_shape=[_sds((S, GW), CDT)] * 3 + [_sds((S, GW), F32)] * 3,
        compiler_params=_params(1))(*os_, *ls_, do_a)
    return res[:3], res[3:]


def _ret_tables(dk):
    H, C = RET_HEADS, BLK
    log_g = jnp.log(1.0 - 2.0 ** (-5.0 - jnp.arange(H, dtype=F32)))
    idx = jnp.arange(C, dtype=F32)
    diff = idx[:, None] - idx[None, :]
    decay = jnp.where(diff >= 0, jnp.exp(log_g[:, None, None] * jnp.maximum(diff, 0.0)), 0.0)
    xi = jnp.exp(log_g[:, None] * (idx[None, :] + 1.0))
    zeta = jnp.exp(log_g[:, None] * (C - 1.0 - idx[None, :])) * (dk ** -0.5)
    g_chunk = jnp.exp(log_g * C)
    bc = lambda t: jnp.broadcast_to(t[:, :, None], (H, C, C))
    return decay, bc(xi), bc(zeta), jnp.broadcast_to(g_chunk[:, None, None], (H, 8, C))


def _gn_fwd(o, g, b):
    mu = jnp.mean(o, axis=-1, keepdims=True)
    xc = o - mu
    rstd = lax.rsqrt(jnp.mean(xc * xc, axis=-1, keepdims=True) + EPS)
    yh = xc * rstd
    return yh, rstd, yh * g + b


def _ret_specs(dk, dv, order):
    qk_w, v_w = RET_HEADS * dk, RET_HEADS * dv
    off_q = 3 * ATT_W
    off_k, off_v, off_g = off_q + qk_w, off_q + 2 * qk_w, off_q + 2 * qk_w + v_w
    assert off_q % dk == 0 and off_k % dk == 0 and off_v % dv == 0 and off_g % dv == 0
    tab = pl.BlockSpec((None, BLK, BLK), lambda h, i: (h, 0, 0))
    return [
        pl.BlockSpec((BLK, dk), lambda h, i: (order(i), off_q // dk + h)),
        pl.BlockSpec((BLK, dk), lambda h, i: (order(i), off_k // dk + h)),
        pl.BlockSpec((BLK, dv), lambda h, i: (order(i), off_v // dv + h)),
        pl.BlockSpec((BLK, dv), lambda h, i: (order(i), off_g // dv + h)),
        tab, tab, tab, pl.BlockSpec((None, 8, BLK), lambda h, i: (h, 0, 0)),
    ]


def _ret_fwd(proj, gn_g, gn_b, dk, dv):
    S = proj.shape[0]
    N = S // BLK
    H = RET_HEADS
    kscale = dk ** -0.5

    def body(q_ref, k_ref, v_ref, gr_ref, dec_ref, xi_ref, zeta_ref, gc_ref, g_ref, b_ref,
             opre_ref, or_ref, st_ref, state):
        n = pl.program_id(1)

        @pl.when(n == 0)
        def _():
            state[...] = jnp.zeros_like(state)

        q, k, v = q_ref[...], k_ref[...], v_ref[...]
        s = _dot_nt(q, k) * kscale * dec_ref[...]
        st = state[...]
        st_c = st.astype(CDT)
        st_ref[...] = st_c
        o = _dot_nn(s.astype(CDT), v) + _dot_nn(q, st_c) * xi_ref[:, 0:1]
        kz = (k.astype(F32) * zeta_ref[:, 0:1]).astype(CDT)
        state[...] = st * gc_ref[0:1, 0:1] + _dot_tn(kz, v)
        opre_ref[...] = o
        _, _, y = _gn_fwd(o, g_ref[...], b_ref[...])
        gr = gr_ref[...].astype(F32)
        or_ref[...] = (y * (gr * _sigmoid(gr))).astype(or_ref.dtype)

    v_w = H * dv
    row = pl.BlockSpec((1, dv), lambda h, i: (0, h))
    tile = pl.BlockSpec((BLK, dv), lambda h, i: (i, h))
    return pl.pallas_call(
        body, name="ret_fwd", grid=(H, N),
        in_specs=_ret_specs(dk, dv, lambda i: i) + [row, row],
        out_specs=[tile, tile, pl.BlockSpec((None, None, dk, dv), lambda h, i: (i, h, 0, 0))],
        out_shape=[_sds((S, v_w), F32), _sds((S, v_w), CDT), _sds((N, H, dk, dv), CDT)],
        scratch_shapes=[pltpu.VMEM((dk, dv), F32)],
        compiler_params=_params(2),
    )(proj, proj, proj, proj, *_ret_tables(dk), gn_g, gn_b)


def _ret_bwd(proj, gn_g, gn_b, o_pre, states, d_or, dk, dv):
    S = proj.shape[0]
    N = S // BLK
    H = RET_HEADS
    qk_w, v_w = H * dk, H * dv
    kscale = dk ** -0.5

    def body(q_ref, k_ref, v_ref, gr_ref, dec_ref, xi_ref, zeta_ref, gc_ref, g_ref, b_ref, opre_ref, st_ref, dor_ref,
             dq_ref, dk_ref, dv_ref, dgr_ref, dg_ref, db_ref, dstate):
        i = pl.program_id(1)

        @pl.when(i == 0)
        def _():
            dstate[...] = jnp.zeros_like(dstate)
            dg_ref[...] = jnp.zeros_like(dg_ref)
            db_ref[...] = jnp.zeros_like(db_ref)

        q, k, v = q_ref[...], k_ref[...], v_ref[...]
        decay, xi, zeta_s = dec_ref[...], xi_ref[:, 0:1], zeta_ref[:, 0:1]
        gr = gr_ref[...].astype(F32)
        sg = _sigmoid(gr)
        gain = g_ref[...]
        yh, rstd, y = _gn_fwd(opre_ref[...], gain, b_ref[...])
        d_or_v = dor_ref[...]
        dy = d_or_v * (gr * sg)
        dgr_ref[...] = (d_or_v * y * (sg * (1.0 + gr * (1.0 - sg)))).astype(dgr_ref.dtype)
        dg_ref[...] += jnp.sum(dy * yh, axis=0, keepdims=True)
        db_ref[...] += jnp.sum(dy, axis=0, keepdims=True)
        dyh = dy * gain
        do = rstd * (dyh - jnp.mean(dyh, axis=-1, keepdims=True) - yh * jnp.mean(dyh * yh, axis=-1, keepdims=True))
        do_c = do.astype(CDT)
        dox = (do * xi).astype(CDT)
        a_c = (_dot_nt(q, k) * kscale * decay).astype(CDT)
        g_c = (_dot_nt(do_c, v) * decay).astype(CDT)
        dsn = dstate[...]
        dsn_c = dsn.astype(CDT)
        kz = (k.astype(F32) * zeta_s).astype(CDT)
        dq_ref[...] = (_dot_nn(g_c, k) * kscale + _dot_nt(dox, st_ref[...])).astype(dq_ref.dtype)
        dk_ref[...] = (_dot_tn(g_c, q) * kscale + _dot_nt(v, dsn_c) * zeta_s).astype(dk_ref.dtype)
        dv_ref[...] = (_dot_tn(a_c, do_c) + _dot_nn(kz, dsn_c)).astype(dv_ref.dtype)
        dstate[...] = dsn * gc_ref[0:1, 0:1] + _dot_tn(q, dox)

    rev = lambda i: N - 1 - i
    row = pl.BlockSpec((1, dv), lambda h, i: (0, h))
    vt = pl.BlockSpec((BLK, dv), lambda h, i: (rev(i), h))
    qt = pl.BlockSpec((BLK, dk), lambda h, i: (rev(i), h))
    return pl.pallas_call(
        body, name="ret_bwd", grid=(H, N),
        in_specs=_ret_specs(dk, dv, rev) + [row, row, vt,
                 pl.BlockSpec((None, None, dk, dv), lambda h, i: (rev(i), h, 0, 0)), vt],
        out_specs=[qt, qt, vt, vt, row, row],
        out_shape=[_sds((S, qk_w), CDT), _sds((S, qk_w), CDT), _sds((S, v_w), CDT), _sds((S, v_w), CDT),
                   _sds((1, v_w), F32), _sds((1, v_w), F32)],
        scratch_shapes=[pltpu.VMEM((dk, dv), F32)],
        compiler_params=_params(2),
    )(proj, proj, proj, proj, *_ret_tables(dk), gn_g, gn_b, o_pre, states, d_or)


def _merge_fwd(o_a, o_r, wa, wb, proj, d_model, tm=512, tn=256):
    S, in_w = proj.shape
    off_a, off_b = in_w - 2 * d_model, in_w - d_model
    assert off_a % tn == 0 and off_b % tn == 0

    def body(oa_ref, or_ref, wa_ref, wb_ref, ga_ref, gb_ref, y_ref, pa_ref, pb_ref):
        pa = _dot_nn(oa_ref[...], wa_ref[...])
        pb = _dot_nn(or_ref[...], wb_ref[...])
        y = _sigmoid(ga_ref[...].astype(F32)) * pa + _sigmoid(gb_ref[...].astype(F32)) * pb
        y_ref[...] = y.astype(y_ref.dtype)
        pa_ref[...] = pa.astype(pa_ref.dtype)
        pb_ref[...] = pb.astype(pb_ref.dtype)

    ka, kb = o_a.shape[1], o_r.shape[1]
    out = pl.BlockSpec((tm, tn), lambda i, j: (i, j))
    return pl.pallas_call(
        body, name="merge_fwd", grid=(S // tm, d_model // tn),
        in_specs=[pl.BlockSpec((tm, ka), lambda i, j: (i, 0)), pl.BlockSpec((tm, kb), lambda i, j: (i, 0)),
                  pl.BlockSpec((ka, tn), lambda i, j: (0, j)), pl.BlockSpec((kb, tn), lambda i, j: (0, j)),
                  pl.BlockSpec((tm, tn), lambda i, j: (i, off_a // tn + j)),
                  pl.BlockSpec((tm, tn), lambda i, j: (i, off_b // tn + j))],
        out_specs=[out, out, out], out_shape=[_sds((S, d_model), CDT)] * 3,
        compiler_params=_params(2))(o_a, o_r, wa, wb, proj, proj)


def _local_step(x, target, w_in, small, late_weights, on_grads, after_w_in=None):
    S, D = x.shape
    in_w = w_in.shape[1]
    d_ff = 4 * D
    ret_v_w = 2 * D
    dv = ret_v_w // RET_HEADS
    dk = (in_w - 3 * ATT_W - 2 * ret_v_w - 2 * D) // (2 * RET_HEADS)
    gqk = jnp.concatenate([small["q_norm_g"].reshape(1, ATT_W), small["k_norm_g"].reshape(1, ATT_W)], axis=1)
    g1, g2 = small["norm1_g"], small["norm2_g"]
    gn_g, gn_b = small["ret_gn_g"], small["ret_gn_b"]

    xn = _rms_fwd("rms1_fwd", x, g1)
    (proj,) = _matmul("in_proj", "nn", xn, w_in, tm=1024, tn=640, tk=D,
                      outs=[(_sds((S, in_w), CDT), _mn(1024, 640))], epilogue=_ep_store, deps=[after_w_in])
    qkn = _qknorm_fwd(proj, gqk)
    att_ops = [_att_operands(g, qkn, proj) for g in range(3)]
    att = [_att_fwd(g, S, att_ops[g]) for g in range(3)]
    os_, ls_ = [a[0] for a in att], [a[1] for a in att]
    o_a = _mix_fwd(os_, ls_)
    o_pre, o_r, states = _ret_fwd(proj, gn_g, gn_b, dk, dv)
    w = late_weights(o_r)
    y, pa, pb = _merge_fwd(o_a, o_r, w["w_proj_a"], w["w_proj_b"], proj, D)
    (x1,) = _matmul("out_proj", "nn", y, w["w_out"], tm=512, tn=D, tk=D,
                    extras=[(x, _mn(512, D))], outs=[(_sds((S, D), F32), _mn(512, D))], epilogue=_ep_resid)
    xn2 = _rms_fwd("rms2_fwd", x1, g2)
    hid, act = _matmul("mlp_up", "nn", xn2, w["w_up"], tm=1024, tn=512, tk=D,
                       outs=[(_sds((S, d_ff), CDT), _mn(1024, 512))] * 2, epilogue=_ep_up)
    dx2, dx2c, loss_row = _matmul(
        "mlp_down_loss", "nn", act, w["w_down"], tm=512, tn=D, tk=1024,
        extras=[(x1, _mn(512, D)), (target, _mn(512, D))],
        outs=[(_sds((S, D), F32), _mn(512, D)), (_sds((S, D), CDT), _mn(512, D)), (_sds((1, D), F32), _row(D))],
        epilogue=functools.partial(_ep_down_loss, inv_d=1.0 / D))
    loss = 0.5 * jnp.sum(loss_row) / D

    (dh,) = _matmul("d_hidden", "nt", dx2c, w["w_down"], tm=1024, tn=512, tk=D,
                    extras=[(hid, _mn(1024, 512))], outs=[(_sds((S, d_ff), CDT), _mn(1024, 512))], epilogue=_ep_dh)
    (gw_down,) = _matmul("dw_down", "tn", act, dx2c, tm=1024, tn=D, tk=1024,
                         outs=[(_sds((d_ff, D), F32), _mn(1024, D))], epilogue=_ep_store)
    (gw_up,) = _matmul("dw_up", "tn", xn2, dh, tm=D, tn=1024, tk=1024,
                       outs=[(_sds((D, d_ff), F32), _mn(D, 1024))], epilogue=_ep_store)
    tok = on_grads({"w_down": gw_down, "w_up": gw_up})
    dx1, dx1c, dg2 = _matmul(
        "d_x1", "nt", dh, w["w_up"], tm=512, tn=D, tk=1024,
        extras=[(x1, _mn(512, D)), (g2, _row(D)), (dx2, _mn(512, D))],
        outs=[(_sds((S, D), F32), _mn(512, D)), (_sds((S, D), CDT), _mn(512, D)), (_sds((1, D), F32), _row(D))],
        epilogue=_ep_rms_bwd, deps=[tok])

    gt = 512
    assert (in_w - 2 * D) % gt == 0
    off_a, off_b = (in_w - 2 * D) // gt, (in_w - D) // gt
    dpa, dpb, dga, dgb = _matmul(
        "d_gates", "nt", dx1c, w["w_out"], tm=512, tn=gt, tk=D,
        extras=[(proj, _mn(512, gt, off_a)), (proj, _mn(512, gt, off_b)), (pa, _mn(512, gt)), (pb, _mn(512, gt))],
        outs=[(_sds((S, D), CDT), _mn(512, gt))] * 4, epilogue=_ep_gates)
    (gw_out,) = _matmul("dw_out", "tn", y, dx1c, tm=D, tn=D, tk=1024,
                        outs=[(_sds((D, D), F32), _mn(D, D))], epilogue=_ep_store)
    (gw_pa,) = _matmul("dw_proj_a", "tn", o_a, dpa, tm=GW, tn=D, tk=1024,
                       outs=[(_sds((GW, D), F32), _mn(GW, D))], epilogue=_ep_store)
    (gw_pb,) = _matmul("dw_proj_b", "tn", o_r, dpb, tm=1024, tn=D, tk=1024,
                       outs=[(_sds((ret_v_w, D), F32), _mn(1024, D))], epilogue=_ep_store)
    (do_a,) = _matmul("d_o_a", "nt", dpa, w["w_proj_a"], tm=1024, tn=GW, tk=D,
                      outs=[(_sds((S, GW), F32), _mn(1024, GW))], epilogue=_ep_store)
    tok = on_grads({"w_out": gw_out, "w_proj_a": gw_pa, "w_proj_b": gw_pb})
    (d_or,) = _matmul("d_o_r", "nt", dpb, w["w_proj_b"], tm=1024, tn=512, tk=D,
                      outs=[(_sds((S, ret_v_w), F32), _mn(1024, 512))], epilogue=_ep_store, deps=[tok])

    dq_r, dk_r, dv_r, dgr, dgn_g, dgn_b = _ret_bwd(proj, gn_g, gn_b, o_pre, states, d_or, dk, dv)
    do_gs, c_gs = _mix_bwd(os_, ls_, do_a)
    datt_parts = [_att_bwd(g, S, att_ops[g], ls_[g], do_gs[g], c_gs[g]) for g in range(3)]
    d_att, dgqk = _qknorm_bwd(proj, gqk, [p[0] for p in datt_parts], [p[1] for p in datt_parts],
                              [p[2] for p in datt_parts])
    dproj = jnp.concatenate([d_att, dq_r, dk_r, dv_r, dgr, dga, dgb], axis=1)

    (gw_in,) = _matmul("dw_in", "tn", xn, dproj, tm=D, tn=640, tk=1024,
                       outs=[(_sds((D, in_w), F32), _mn(D, 640))], epilogue=_ep_store)
    tok = on_grads({"w_in": gw_in})
    grad_x, _, dg1 = _matmul(
        "d_x", "nt", dproj, w_in, tm=512, tn=D, tk=1280,
        extras=[(x, _mn(512, D)), (g1, _row(D)), (dx1, _mn(512, D))],
        outs=[(_sds((S, D), F32), _mn(512, D)), (_sds((S, D), CDT), _mn(512, D)), (_sds((1, D), F32), _row(D))],
        epilogue=_ep_rms_bwd, deps=[tok])

    smallg = {"norm1_g": dg1, "q_norm_g": dgqk[:, :ATT_W], "k_norm_g": dgqk[:, ATT_W:],
              "ret_gn_g": dgn_g, "ret_gn_b": dgn_b, "norm2_g": dg2}
    return loss, grad_x, smallg


N_CHIPS = 4
N_DEV = 8


def _place():
    x, y, c = lax.axis_index("x"), lax.axis_index("y"), lax.axis_index("c")
    return x, y, c


def _other_chips(x, y):
    out = []
    for fx, fy in ((1, 0), (0, 1), (1, 1)):
        px = 1 - x if fx else x
        py = 1 - y if fy else y
        out.append(((px, py), 2 * px + py))
    return out


def _gather_weights(shards):
    n = len(shards)

    def body(*refs):
        ins, outs = refs[:n], refs[n:2 * n]
        send, recv, loc = refs[2 * n:]
        x, y, c = _place()
        me = 2 * x + y
        peers = _other_chips(x, y)
        local = [pltpu.make_async_copy(ins[w], outs[w].at[me], loc.at[w]) for w in range(n)]
        for cp in local:
            cp.start()

        def rows(w):
            half = ins[w].shape[0] // 2
            return pl.ds(c * half, half)

        sends = []
        for w in range(n):
            for j, ((px, py), _) in enumerate(peers):
                cp = pltpu.make_async_remote_copy(
                    src_ref=ins[w].at[rows(w), :], dst_ref=outs[w].at[me, rows(w), :], send_sem=send.at[w * 3 + j],
                    recv_sem=recv.at[w * 3 + j], device_id=(px, py, c), device_id_type=MESH)
                cp.start()
                sends.append(cp)
        for w in range(n):
            for j, ((px, py), pidx) in enumerate(peers):
                pltpu.make_async_remote_copy(
                    src_ref=ins[w].at[rows(w), :], dst_ref=outs[w].at[pidx, rows(w), :], send_sem=send.at[w * 3 + j],
                    recv_sem=recv.at[w * 3 + j], device_id=(px, py, c), device_id_type=MESH).wait_recv()
        for cp in sends:
            cp.wait_send()
        for cp in local:
            cp.wait()

    return pl.pallas_call(
        body, name="gather_weights",
        in_specs=[HBM_SPEC] * n, out_specs=[HBM_SPEC] * n,
        out_shape=[_sds((N_CHIPS,) + s.shape, s.dtype) for s in shards],
        scratch_shapes=[pltpu.SemaphoreType.DMA((3 * n,)), pltpu.SemaphoreType.DMA((3 * n,)),
                        pltpu.SemaphoreType.DMA((n,))],
    )(*shards)


SEM_SPEC = pl.BlockSpec(memory_space=pltpu.SEMAPHORE)
ANY_SPEC = pl.BlockSpec(memory_space=pl.ANY)
EFFECT = pltpu.SideEffectType.DATAFLOW_SIDE_EFFECTING


def _ici_copies(kind, srcs, lands, send, recv):
    x, y, c = _place()
    me = 2 * x + y
    out = []
    for w, (s, l) in enumerate(zip(srcs, lands)):
        for j, ((px, py), pidx) in enumerate(_other_chips(x, y)):
            if kind == "gather":
                half = s.shape[0] // 2
                rows = pl.ds(c * half, half)
                src, dst_there, dst_here = s.at[rows, :], l.at[me, rows, :], l.at[pidx, rows, :]
            else:
                src, dst_there, dst_here = s.at[pidx], l.at[me], l.at[pidx]
            out.append((src, dst_there, dst_here, send.at[3 * w + j], recv.at[3 * w + j], (px, py, c)))
    return out


def _exchange_start(name, kind, srcs, land_shapes):
    n = len(srcs)

    def body(*refs):
        src_refs, land_refs = refs[:n], refs[n:2 * n]
        send, recv = refs[2 * n], refs[2 * n + 1]
        token = refs[-1]
        for src, dst, _, ss, rs, dev in _ici_copies(kind, src_refs, land_refs, send, recv):
            pltpu.make_async_remote_copy(src_ref=src, dst_ref=dst, send_sem=ss, recv_sem=rs, device_id=dev,
                                         device_id_type=MESH).start()
        token[...] = jnp.zeros_like(token)

    thru = [pltpu.HBM(s.shape, s.dtype) for s in srcs] + [pltpu.HBM(shape, dtype) for shape, dtype in land_shapes]
    res = pl.pallas_call(
        body, name=name,
        out_shape=(pltpu.SemaphoreType.DMA((3 * n,)), pltpu.SemaphoreType.DMA((3 * n,)), *thru, _sds((8, LANES), F32)),
        in_specs=[HBM_SPEC] * (2 * n), out_specs=(SEM_SPEC, SEM_SPEC, *[HBM_SPEC] * (2 * n), VMEM_SPEC),
        input_output_aliases={i: 2 + i for i in range(2 * n)},
        compiler_params=pltpu.CompilerParams(has_side_effects=EFFECT),
    )(*[pltpu.with_memory_space_constraint(s, pltpu.HBM) for s in srcs],
      *[pltpu.with_memory_space_constraint(lax.empty(shape, dtype), pltpu.HBM) for shape, dtype in land_shapes])
    return res[0], res[1], list(res[2:2 + n]), list(res[2 + n:2 + 2 * n]), res[-1]


def _exchange_wait(name, kind, send, recv, srcs, lands, after):
    n = len(srcs)

    def body(*refs):
        src_refs, land_refs = refs[:n], refs[n:2 * n]
        send_ref, recv_ref = refs[2 * n], refs[2 * n + 1]
        for src, _, dst, ss, rs, dev in _ici_copies(kind, src_refs, land_refs, send_ref, recv_ref):
            cp = pltpu.make_async_remote_copy(src_ref=src, dst_ref=dst, send_sem=ss, recv_sem=rs, device_id=dev,
                                              device_id_type=MESH)
            cp.wait_send()
            cp.wait_recv()

    thru = [pltpu.HBM(t.shape, t.dtype) for t in list(srcs) + list(lands)]
    res = pl.pallas_call(
        body, name=name, out_shape=thru,
        in_specs=[HBM_SPEC] * (2 * n) + [SEM_SPEC, SEM_SPEC, ANY_SPEC], out_specs=[HBM_SPEC] * (2 * n),
        input_output_aliases={i: i for i in range(2 * n)},
        compiler_params=pltpu.CompilerParams(has_side_effects=EFFECT),
    )(*srcs, *lands, send, recv, after)
    return list(res[:n]), list(res[n:])


PAIR_TILE_ELEMS = 1 << 19


def _pair_fill(name, gathered, core, others):
    k, r, C = gathered.shape
    half = r // 2
    tr = _row_tile(half, C, PAIR_TILE_ELEMS, mult=16)
    nt = half // tr

    def body(c_ref, o_ref, in_ref, out_ref, slot, send, recv):
        b = (pl.program_id(0) * nt + pl.program_id(1)) % 2
        x, y, c = _place()
        cp = pltpu.make_async_remote_copy(src_ref=in_ref, dst_ref=slot.at[b], send_sem=send.at[b],
                                          recv_sem=recv.at[b], device_id=(x, y, 1 - c), device_id_type=MESH)
        cp.start()
        cp.wait_recv()
        out_ref[...] = slot[b]
        cp.wait_send()

    grid_spec = pltpu.PrefetchScalarGridSpec(
        num_scalar_prefetch=2, grid=(N_CHIPS - 1, nt),
        in_specs=[pl.BlockSpec((tr, C), lambda j, i, c, o: ((2 * o[j] + c[0]) * nt + i, 0))],
        out_specs=pl.BlockSpec((tr, C), lambda j, i, c, o: ((2 * o[j] + 1 - c[0]) * nt + i, 0)),
        scratch_shapes=[pltpu.VMEM((2, tr, C), gathered.dtype), pltpu.SemaphoreType.DMA((2,)),
                        pltpu.SemaphoreType.DMA((2,))])
    out = pl.pallas_call(body, name=name, grid_spec=grid_spec, out_shape=_sds((k * r, C), gathered.dtype),
                         input_output_aliases={2: 0}, compiler_params=_params(2))(
                             core, others, gathered.reshape(k * r, C))
    return out.reshape(k, r, C)


def _pair_reduce(name, g, core):
    k, R, C = g.shape
    half = R // 2
    tr = _row_tile(half, C, PAIR_TILE_ELEMS)
    nt = half // tr

    def body(c_ref, mine_ref, give_ref, out_ref, wire_ref, slot, send, recv):
        b = (pl.program_id(0) * nt + pl.program_id(1)) % 2
        x, y, c = _place()
        cp = pltpu.make_async_remote_copy(src_ref=give_ref, dst_ref=slot.at[b], send_sem=send.at[b],
                                          recv_sem=recv.at[b], device_id=(x, y, 1 - c), device_id_type=MESH)
        cp.start()
        cp.wait_recv()
        tot = mine_ref[...] + slot[b]
        out_ref[...] = tot
        wire_ref[...] = tot.astype(wire_ref.dtype)
        cp.wait_send()

    blk = (tr, C)
    out_spec = pl.BlockSpec(blk, lambda s, i, c: (s * nt + i, 0))
    grid_spec = pltpu.PrefetchScalarGridSpec(
        num_scalar_prefetch=1, grid=(k, nt),
        in_specs=[pl.BlockSpec(blk, lambda s, i, c: ((2 * s + c[0]) * nt + i, 0)),
                  pl.BlockSpec(blk, lambda s, i, c: ((2 * s + 1 - c[0]) * nt + i, 0))],
        out_specs=[out_spec, out_spec],
        scratch_shapes=[pltpu.VMEM((2, tr, C), F32), pltpu.SemaphoreType.DMA((2,)), pltpu.SemaphoreType.DMA((2,))])
    g2 = g.reshape(k * R, C)
    out, wire = pl.pallas_call(body, name=name, grid_spec=grid_spec,
                               out_shape=[_sds((k * half, C), F32), _sds((k * half, C), CDT)],
                               compiler_params=_params(2))(core, g2, g2)
    return out, wire.reshape(k, half, C)


def _pair_share(name, f, core):
    half, C = f.shape
    tr = _row_tile(half, C, PAIR_TILE_ELEMS)
    nt = half // tr

    def body(c_ref, f_ref, out_ref, slot, send, recv):
        p = pl.program_id(1)
        b = pl.program_id(0) % 2
        x, y, c = _place()
        cp = pltpu.make_async_remote_copy(src_ref=f_ref, dst_ref=slot.at[b], send_sem=send.at[b],
                                          recv_sem=recv.at[b], device_id=(x, y, 1 - c), device_id_type=MESH)

        @pl.when(p == 0)
        def _():
            cp.start()
            out_ref[...] = f_ref[...]

        @pl.when(p == 1)
        def _():
            cp.wait_recv()
            out_ref[...] = slot[b]
            cp.wait_send()

    grid_spec = pltpu.PrefetchScalarGridSpec(
        num_scalar_prefetch=1, grid=(nt, 2),
        in_specs=[pl.BlockSpec((tr, C), lambda i, p, c: (i, 0))],
        out_specs=pl.BlockSpec((tr, C), lambda i, p, c: (jnp.where(p == 0, c[0], 1 - c[0]) * nt + i, 0)),
        scratch_shapes=[pltpu.VMEM((2, tr, C), F32), pltpu.SemaphoreType.DMA((2,)), pltpu.SemaphoreType.DMA((2,))])
    return pl.pallas_call(body, name=name, grid_spec=grid_spec, out_shape=_sds((2 * half, C), F32),
                          compiler_params=_params(2))(core, f)


def _all_reduce_small(v):
    r, cdim = v.shape

    def body(v_ref, o_ref, buf, send, recv):
        x, y, c = _place()
        me = 4 * x + 2 * y + c
        buf[me] = v_ref[...]
        sends = []
        for m in range(1, N_DEV):
            px = 1 - x if m & 4 else x
            py = 1 - y if m & 2 else y
            pc = 1 - c if m & 1 else c
            cp = pltpu.make_async_remote_copy(src_ref=v_ref, dst_ref=buf.at[me], send_sem=send.at[m - 1],
                                              recv_sem=recv.at[m - 1], device_id=(px, py, pc), device_id_type=MESH)
            cp.start()
            sends.append((cp, 4 * px + 2 * py + pc))
        for m, (cp, pidx) in enumerate(sends):
            pltpu.make_async_remote_copy(src_ref=v_ref, dst_ref=buf.at[pidx], send_sem=send.at[m], recv_sem=recv.at[m],
                                         device_id=(x, y, c), device_id_type=MESH).wait_recv()
        for cp, _ in sends:
            cp.wait_send()
        tot = buf[0]
        for k in range(1, N_DEV):
            tot = tot + buf[k]
        o_ref[...] = tot

    return pl.pallas_call(
        body, name="all_reduce_small", in_specs=[VMEM_SPEC], out_specs=VMEM_SPEC,
        out_shape=_sds((r, cdim), F32),
        scratch_shapes=[pltpu.VMEM((N_DEV, r, cdim), F32), pltpu.SemaphoreType.DMA((N_DEV - 1,)),
                        pltpu.SemaphoreType.DMA((N_DEV - 1,))],
    )(v)


def _row_tile(rows, cols, budget_elems=1 << 18, mult=8):
    if rows % mult:
        return rows
    t = max(mult, (budget_elems // cols) // mult * mult)
    while rows % t:
        t -= mult
    return t


def _sum4(name, own, by_chip, chip, others):
    k, R, C = by_chip.shape
    tr = _row_tile(R, C, mult=16)
    nt = R // tr

    def body(chip_ref, others_ref, own_ref, a_ref, b_ref, c_ref, o_ref):
        o_ref[...] = ((own_ref[...] + a_ref[...].astype(F32)) + b_ref[...].astype(F32)) + c_ref[...].astype(F32)

    def from_chip(j):
        return pl.BlockSpec((tr, C), lambda i, chip, oth: (oth[j] * nt + i, 0))

    grid_spec = pltpu.PrefetchScalarGridSpec(
        num_scalar_prefetch=2, grid=(nt,),
        in_specs=[pl.BlockSpec((tr, C), lambda i, chip, oth: (chip[0] * nt + i, 0)),
                  from_chip(0), from_chip(1), from_chip(2)],
        out_specs=pl.BlockSpec((tr, C), lambda i, chip, oth: (i, 0)))
    by2 = by_chip.reshape(k * R, C)
    return pl.pallas_call(body, name=name, grid_spec=grid_spec, out_shape=_sds((R, C), F32),
                          compiler_params=_params(1))(chip, others, own, by2, by2, by2)


def _adamw(name, w, g, m, v):
    R, C = w.shape
    tr = _row_tile(R, C, 1 << 17)

    def body(w_ref, g_ref, m_ref, v_ref, d_ref, nm_ref, nv_ref):
        gv = g_ref[...]
        nm = ADAM_B1 * m_ref[...] + (1.0 - ADAM_B1) * gv
        nv = ADAM_B2 * v_ref[...] + (1.0 - ADAM_B2) * (gv * gv)
        m_hat = nm / (1.0 - ADAM_B1 ** ADAM_STEP)
        v_hat = nv / (1.0 - ADAM_B2 ** ADAM_STEP)
        d_ref[...] = -ADAM_LR * (m_hat / (jnp.sqrt(v_hat) + ADAM_EPS) + ADAM_WD * w_ref[...])
        nm_ref[...] = nm
        nv_ref[...] = nv

    spec = pl.BlockSpec((tr, C), lambda i: (i, 0))
    return pl.pallas_call(body, name=name, grid=(R // tr,), in_specs=[spec] * 4, out_specs=[spec] * 3,
                          out_shape=[_sds((R, C), F32)] * 3, compiler_params=_params(1))(w, g, m, v)


BIG = ("w_in", "w_proj_a", "w_proj_b", "w_out", "w_up", "w_down")
COL_SHARDED = ("w_in", "w_proj_a", "w_up")
SMALL = ("norm1_g", "q_norm_g", "k_norm_g", "ret_gn_g", "ret_gn_b", "norm2_g")
ALL_W = ("norm1_g", "w_in", "q_norm_g", "k_norm_g", "ret_gn_g", "ret_gn_b", "w_proj_a", "w_proj_b", "w_out",
         "norm2_g", "w_up", "w_down")
LANES = 128


def _to_full(name, gathered):
    k, r, c = gathered.shape
    if name in COL_SHARDED:
        return gathered.transpose(1, 0, 2).reshape(r, k * c)
    return gathered.reshape(k * r, c)


def _to_shard_major(name, full):
    if name in COL_SHARDED:
        r, c4 = full.shape
        return full.reshape(r, N_CHIPS, c4 // N_CHIPS).transpose(1, 0, 2)
    r4, c = full.shape
    return full.reshape(N_CHIPS, r4 // N_CHIPS, c)


def kernel(x, norm1_g, w_in, q_norm_g, k_norm_g, ret_gn_g, ret_gn_b, w_proj_a, w_proj_b, w_out, norm2_g, w_up, w_down, loss_target, m_norm1_g, m_w_in, m_q_norm_g, m_k_norm_g, m_ret_gn_g, m_ret_gn_b, m_w_proj_a, m_w_proj_b, m_w_out, m_norm2_g, m_w_up, m_w_down, v_norm1_g, v_w_in, v_q_norm_g, v_k_norm_g, v_ret_gn_g, v_ret_gn_b, v_w_proj_a, v_w_proj_b, v_w_out, v_norm2_g, v_w_up, v_w_down):
    weights = dict(norm1_g=norm1_g, w_in=w_in, q_norm_g=q_norm_g, k_norm_g=k_norm_g, ret_gn_g=ret_gn_g,
                   ret_gn_b=ret_gn_b, w_proj_a=w_proj_a, w_proj_b=w_proj_b, w_out=w_out, norm2_g=norm2_g,
                   w_up=w_up, w_down=w_down)
    moments_m = dict(norm1_g=m_norm1_g, w_in=m_w_in, q_norm_g=m_q_norm_g, k_norm_g=m_k_norm_g, ret_gn_g=m_ret_gn_g,
                     ret_gn_b=m_ret_gn_b, w_proj_a=m_w_proj_a, w_proj_b=m_w_proj_b, w_out=m_w_out,
                     norm2_g=m_norm2_g, w_up=m_w_up, w_down=m_w_down)
    moments_v = dict(norm1_g=v_norm1_g, w_in=v_w_in, q_norm_g=v_q_norm_g, k_norm_g=v_k_norm_g, ret_gn_g=v_ret_gn_g,
                     ret_gn_b=v_ret_gn_b, w_proj_a=v_w_proj_a, w_proj_b=v_w_proj_b, w_out=v_w_out,
                     norm2_g=v_norm2_g, w_up=v_w_up, w_down=v_w_down)

    mx, my = lax.axis_index("x"), lax.axis_index("y")
    core = lax.axis_index("c").astype(jnp.int32).reshape(1)
    chip = (2 * mx + my).astype(jnp.int32).reshape(1)
    others = jnp.stack([2 * (1 - mx) + my, 2 * mx + 1 - my, 2 * (1 - mx) + 1 - my]).astype(jnp.int32)
    shards = {n: weights[n][0].astype(CDT) for n in BIG}
    (g_in,) = _gather_weights([shards["w_in"]])
    w_in_full = _to_full("w_in", _pair_fill("pair_fill_w_in", g_in, core, others))
    late = [n for n in BIG if n != "w_in"]
    l_send, l_recv, l_srcs, l_lands, l_token = _exchange_start(
        "gather_late_start", "gather", [shards[n] for n in late], [((N_CHIPS,) + shards[n].shape, CDT) for n in late])

    def late_weights(after):
        srcs, lands = _exchange_wait("gather_late_wait", "gather", l_send, l_recv, l_srcs, l_lands, after)
        out = {}
        for n, mine, land in zip(late, srcs, lands):
            g = _pair_fill("pair_fill_%s" % n, land, core, others)
            out[n] = _to_full(n, lax.dynamic_update_index_in_dim(g, mine, chip[0], 0))
        return out

    pending = []

    def on_grads(group):
        names = list(group)
        red = [_pair_reduce("pair_reduce_%s" % n, _to_shard_major(n, group[n]), core) for n in names]
        wires = [wire for _, wire in red]
        send, recv, srcs, lands, token = _exchange_start(
            "scatter_start_%s" % names[0], "scatter", wires, [(wire.shape, wire.dtype) for wire in wires])
        pending.append((names, [own for own, _ in red], send, recv, srcs, lands))
        return token

    small = {n: weights[n].reshape(1, -1) for n in SMALL}

    loss, grad_x, small_g = _local_step(x[0], loss_target[0], w_in_full, small, late_weights, on_grads, l_token)
    loss = lax.psum(loss, ("x", "y", "c"))

    grads = {}
    for names, owns, send, recv, srcs, lands in pending:
        _, got = _exchange_wait("scatter_wait_%s" % names[0], "scatter", send, recv, srcs, lands, grad_x)
        for n, own, by_chip in zip(names, owns, got):
            half = _sum4("chip_sum_%s" % n, own, by_chip, chip, others)
            grads[n] = _pair_share("pair_share_%s" % n, half, core)

    packed = jnp.concatenate([small_g[n].reshape(1, -1) for n in SMALL], axis=1)
    sizes = [small_g[n].size for n in SMALL]
    red = _all_reduce_small(packed.reshape(-1, LANES)).reshape(1, -1)
    off = 0
    for n, sz in zip(SMALL, sizes):
        grads[n] = red[:, off:off + sz]
        off += sz

    out_g, out_d, out_m, out_v = {}, {}, {}, {}
    for n in ALL_W:
        shape = weights[n].shape
        two_d = (shape[-2], shape[-1]) if n in BIG else (1, weights[n].size)
        g2 = grads[n].reshape(two_d)
        d, nm, nv = _adamw("adamw_%s" % n, weights[n].reshape(two_d), g2, moments_m[n].reshape(two_d),
                           moments_v[n].reshape(two_d))
        out_g[n], out_d[n], out_m[n], out_v[n] = (t.reshape(shape) for t in (g2, d, nm, nv))

    return (loss, grad_x[None], *[out_g[n] for n in ALL_W], *[out_d[n] for n in ALL_W],
            *[out_m[n] for n in ALL_W], *[out_v[n] for n in ALL_W])
```

```python
import functools
import math

import jax
import jax.numpy as jnp
from jax import lax
from jax.experimental import pallas as pl
from jax.experimental.pallas import tpu as pltpu

CDT = jnp.bfloat16
F32 = jnp.float32
EPS = 1e-6

ATT_GROUPS = ((128, 1), (512, 4), (2048, 16))
ATT_HPG = 4
ATT_HEADS = 12
HD = 128
BLK = 128
ATT_W = ATT_HEADS * HD
GW = ATT_HPG * HD
RET_HEADS = 4

ADAM_LR = 0.001
ADAM_B1 = 0.9
ADAM_B2 = 0.999
ADAM_EPS = 1e-08
ADAM_WD = 0.01
ADAM_STEP = 10

VMEM_LIMIT_BYTES = 48 * 1024 * 1024
MESH = pl.DeviceIdType.MESH
HBM_SPEC = pl.BlockSpec(memory_space=pltpu.HBM)
VMEM_SPEC = pl.BlockSpec(memory_space=pltpu.VMEM)


def _params(n_axes):
    return pltpu.CompilerParams(dimension_semantics=("arbitrary",) * n_axes,
                                vmem_limit_bytes=VMEM_LIMIT_BYTES)


def _dot_nn(a, b):
    return jnp.dot(a, b, preferred_element_type=F32)


def _dot_nt(a, b):
    return lax.dot_general(a, b, (((1,), (1,)), ((), ())), preferred_element_type=F32)


def _dot_tn(a, b):
    return lax.dot_general(a, b, (((0,), (0,)), ((), ())), preferred_element_type=F32)


def _sigmoid(v):
    return 1.0 / (1.0 + jnp.exp(-v))


def _matmul(name, mode, a, b, *, tm, tn, tk, extras=(), outs, epilogue, deps=()):
    deps = [d for d in deps if d is not None]
    if mode == "nn":
        (M, K), (K2, N) = a.shape, b.shape
    elif mode == "nt":
        (M, K), (N, K2) = a.shape, b.shape
    else:
        (K, M), (K2, N) = a.shape, b.shape
    assert K == K2 and M % tm == 0 and N % tn == 0 and K % tk == 0, (name, a.shape, b.shape)
    ni, nj, nk = M // tm, N // tn, K // tk
    if mode == "tn":
        a_spec = pl.BlockSpec((tk, tm), lambda i, j, k: (k, i))
    else:
        a_spec = pl.BlockSpec((tm, tk), lambda i, j, k: (i, k))
    if mode == "nt":
        b_spec = pl.BlockSpec((tn, tk), lambda i, j, k: (j, k))
    else:
        b_spec = pl.BlockSpec((tk, tn), lambda i, j, k: (k, j))
    dot = {"nn": _dot_nn, "nt": _dot_nt, "tn": _dot_tn}[mode]
    n_ex, n_out, n_dep = len(extras), len(outs), len(deps)

    def body(*refs):
        a_ref, b_ref = refs[0], refs[1]
        ex = refs[2:2 + n_ex]
        out = refs[2 + n_ex + n_dep:2 + n_ex + n_dep + n_out]
        acc = refs[-1]
        i = pl.program_id(0)
        k = pl.program_id(2)

        @pl.when(k == 0)
        def _():
            acc[...] = jnp.zeros_like(acc)

        acc[...] += dot(a_ref[...].astype(CDT), b_ref[...].astype(CDT))

        @pl.when(k == nk - 1)
        def _():
            epilogue(acc[...], ex, out, i)

    res = pl.pallas_call(
        body, name=name, grid=(ni, nj, nk),
        in_specs=[a_spec, b_spec] + [s for _, s in extras] + [pl.BlockSpec(memory_space=pl.ANY)] * n_dep,
        out_specs=[s for _, s in outs],
        out_shape=[o for o, _ in outs],
        scratch_shapes=[pltpu.VMEM((tm, tn), F32)],
        compiler_params=_params(3),
    )(a, b, *[e for e, _ in extras], *deps)
    return res


def _mn(tm, tn, col_off=0):
    return pl.BlockSpec((tm, tn), lambda i, j, k: (i, j + col_off))


def _row(tn):
    return pl.BlockSpec((1, tn), lambda i, j, k: (0, j))


def _ep_store(acc, ex, out, i):
    out[0][...] = acc.astype(out[0].dtype)


def _ep_resid(acc, ex, out, i):
    out[0][...] = ex[0][...] + acc


def _ep_up(acc, ex, out, i):
    out[0][...] = acc.astype(out[0].dtype)
    r = jnp.maximum(acc, 0.0)
    out[1][...] = (r * r).astype(out[1].dtype)


def _ep_down_loss(acc, ex, out, i, inv_d):
    diff = (ex[0][...] + acc) - ex[1][...]
    dx2 = diff * inv_d
    out[0][...] = dx2
    out[1][...] = dx2.astype(out[1].dtype)

    @pl.when(i == 0)
    def _():
        out[2][...] = jnp.zeros_like(out[2])

    out[2][...] += jnp.sum(diff * diff, axis=0, keepdims=True)


def _ep_dh(acc, ex, out, i):
    h = ex[0][...].astype(F32)
    out[0][...] = (acc * (2.0 * jnp.maximum(h, 0.0))).astype(out[0].dtype)


def _ep_rms_bwd(acc, ex, out, i):
    x = ex[0][...]
    g = ex[1][...]
    rstd = lax.rsqrt(jnp.mean(x * x, axis=-1, keepdims=True) + EPS)
    xh = x * rstd
    dxh = acc * g
    dx = ex[2][...] + rstd * (dxh - xh * jnp.mean(dxh * xh, axis=-1, keepdims=True))
    out[0][...] = dx
    out[1][...] = dx.astype(out[1].dtype)

    @pl.when(i == 0)
    def _():
        out[2][...] = jnp.zeros_like(out[2])

    out[2][...] += jnp.sum(acc * xh, axis=0, keepdims=True)


def _ep_gates(acc, ex, out, i):
    sa = _sigmoid(ex[0][...].astype(F32))
    sb = _sigmoid(ex[1][...].astype(F32))
    dpa = acc * sa
    dpb = acc * sb
    out[0][...] = dpa.astype(out[0].dtype)
    out[1][...] = dpb.astype(out[1].dtype)
    out[2][...] = (dpa * ex[2][...].astype(F32) * (1.0 - sa)).astype(out[2].dtype)
    out[3][...] = (dpb * ex[3][...].astype(F32) * (1.0 - sb)).astype(out[3].dtype)


def _sds(shape, dtype):
    return jax.ShapeDtypeStruct(shape, dtype)


def _rms_fwd(name, x, g, tm=512):
    S, D = x.shape

    def body(x_ref, g_ref, o_ref):
        xv = x_ref[...]
        rstd = lax.rsqrt(jnp.mean(xv * xv, axis=-1, keepdims=True) + EPS)
        o_ref[...] = (xv * rstd * g_ref[...]).astype(o_ref.dtype)

    return pl.pallas_call(
        body, name=name, grid=(S // tm,),
        in_specs=[pl.BlockSpec((tm, D), lambda i: (i, 0)), pl.BlockSpec((1, D), lambda i: (0, 0))],
        out_specs=pl.BlockSpec((tm, D), lambda i: (i, 0)),
        out_shape=_sds((S, D), CDT), compiler_params=_params(1))(x, g)


def _rm_shape(S, d, width):
    return (S, width) if d == 1 else (d, S // d, width)


def _rm_spec(tm, d, width):
    if d == 1:
        return pl.BlockSpec((tm, width), lambda i: (i, 0))
    return pl.BlockSpec((d, tm // d, width), lambda i: (0, i, 0))


def _rm_put(dst_ref, cols, buf_ref, d):
    if d == 1:
        dst_ref[:, cols] = buf_ref[...].astype(dst_ref.dtype)
        return
    m = buf_ref.shape[0] // d
    for r in range(d):
        dst_ref[r, :, cols] = buf_ref[pl.ds(r, m, stride=d), :].astype(dst_ref.dtype)


def _rm_reader(buf_ref, src_ref, d):
    if d == 1:
        return lambda s: src_ref[:, s * HD:(s + 1) * HD].astype(F32)
    m = buf_ref.shape[1] // d
    for s in range(buf_ref.shape[0]):
        for r in range(d):
            buf_ref.at[s][pl.ds(r, m, stride=d), :] = src_ref[r, :, s * HD:(s + 1) * HD].astype(F32)
    return lambda s: buf_ref[s]


def _qknorm_fwd(proj, gqk, tm=512):
    S = proj.shape[0]
    W = 2 * ATT_W
    dil = [d for _, d in ATT_GROUPS]

    def body(p_ref, g_ref, o0, o1, o2, buf):
        outs = (o0, o1, o2)
        for hd in range(3 * ATT_HEADS):
            which, head = hd // ATT_HEADS, hd % ATT_HEADS
            grp, slot = head // ATT_HPG, head % ATT_HPG
            v = p_ref[:, hd * HD:(hd + 1) * HD].astype(F32)
            if which < 2:
                rstd = lax.rsqrt(jnp.mean(v * v, axis=-1, keepdims=True) + EPS)
                v = v * rstd * g_ref[:, hd * HD:(hd + 1) * HD]
            buf[...] = v
            _rm_put(outs[grp], slice(which * GW + slot * HD, which * GW + (slot + 1) * HD), buf, dil[grp])

    return pl.pallas_call(
        body, name="qknorm_fwd", grid=(S // tm,),
        in_specs=[pl.BlockSpec((tm, 3 * ATT_W), lambda i: (i, 0)), pl.BlockSpec((1, W), lambda i: (0, 0))],
        out_specs=[_rm_spec(tm, d, 3 * GW) for d in dil],
        out_shape=[_sds(_rm_shape(S, d, 3 * GW), CDT) for d in dil],
        scratch_shapes=[pltpu.VMEM((tm, HD), F32)],
        compiler_params=_params(1))(proj, gqk)


def _qknorm_bwd(proj, gqk, dqs, dks, dvs, tm=256):
    S = proj.shape[0]
    W = 2 * ATT_W
    dil = [d for _, d in ATT_GROUPS]

    def body(p_ref, g_ref, *refs):
        ins = refs[0:9]
        o_ref, dg_ref = refs[9], refs[10]
        bufs = refs[11:20]
        i = pl.program_id(0)

        @pl.when(i == 0)
        def _():
            dg_ref[...] = jnp.zeros_like(dg_ref)

        nat = [_rm_reader(bufs[j], ins[j], dil[j % 3]) for j in range(9)]
        dq_get, dk_get, dv_get = nat[0:3], nat[3:6], nat[6:9]
        for hd in range(2 * ATT_HEADS):
            sl = slice(hd * HD, (hd + 1) * HD)
            head = hd % ATT_HEADS
            grp, slot = head // ATT_HPG, head % ATT_HPG
            dn = (dq_get if hd < ATT_HEADS else dk_get)[grp](slot)
            v = p_ref[:, sl].astype(F32)
            rstd = lax.rsqrt(jnp.mean(v * v, axis=-1, keepdims=True) + EPS)
            vh = v * rstd
            dg_ref[:, sl] += jnp.sum(dn * vh, axis=0, keepdims=True)
            dvh = dn * g_ref[:, sl]
            o_ref[:, sl] = (rstd * (dvh - vh * jnp.mean(dvh * vh, axis=-1, keepdims=True))).astype(o_ref.dtype)
        for head in range(ATT_HEADS):
            grp, slot = head // ATT_HPG, head % ATT_HPG
            o_ref[:, W + head * HD:W + (head + 1) * HD] = dv_get[grp](slot).astype(o_ref.dtype)

    return pl.pallas_call(
        body, name="qknorm_bwd", grid=(S // tm,),
        in_specs=[pl.BlockSpec((tm, W), lambda i: (i, 0)), pl.BlockSpec((1, W), lambda i: (0, 0))]
        + [_rm_spec(tm, d, GW) for d in dil] * 3,
        out_specs=[pl.BlockSpec((tm, 3 * ATT_W), lambda i: (i, 0)), pl.BlockSpec((1, W), lambda i: (0, 0))],
        out_shape=[_sds((S, 3 * ATT_W), CDT), _sds((1, W), F32)],
        scratch_shapes=[pltpu.VMEM((ATT_HPG, tm, HD), F32)] * 9,
        compiler_params=_params(1))(proj, gqk, *dqs, *dks, *dvs)


def _att_mask(n):
    qi = lax.broadcasted_iota(jnp.int32, (BLK, 2 * BLK), 0)
    kj = lax.broadcasted_iota(jnp.int32, (BLK, 2 * BLK), 1)
    dist = BLK + qi - kj
    valid = (dist >= 0) & (dist <= BLK) & ((kj >= BLK) | (n > 0))
    return valid, dist.astype(F32)


def _att_slopes(grp):
    return [2.0 ** (-8.0 * (grp * ATT_HPG + hh + 1) / ATT_HEADS) for hh in range(ATT_HPG)]


def _att_spec(d, row_fn, col=0):
    if d == 1:
        return pl.BlockSpec((BLK, GW), lambda r, n: (row_fn(n), col))
    return pl.BlockSpec((None, BLK, GW), lambda r, n: (r, row_fn(n), col))


def _att_qkv_specs(d, nb):
    last = nb - 1

    def cur(n):
        return jnp.minimum(n, last)

    def prev(n):
        return jnp.maximum(jnp.minimum(n, last) - 1, 0)

    return [_att_spec(d, cur, 0), _att_spec(d, prev, 1), _att_spec(d, cur, 1), _att_spec(d, prev, 2),
            _att_spec(d, cur, 2)]


def _att_fwd(grp, S, qkv):
    _, d = ATT_GROUPS[grp]
    L = S // d
    nb = L // BLK
    slopes = _att_slopes(grp)
    scale = HD ** -0.5

    def body(q_ref, kp_ref, kc_ref, vp_ref, vc_ref, o_ref, l_ref):
        n = pl.program_id(1)
        valid, distf = _att_mask(n)
        for hh in range(ATT_HPG):
            sl = slice(hh * HD, (hh + 1) * HD)
            k = jnp.concatenate([kp_ref[:, sl], kc_ref[:, sl]], axis=0)
            v = jnp.concatenate([vp_ref[:, sl], vc_ref[:, sl]], axis=0)
            s = _dot_nt(q_ref[:, sl], k) * scale + (-slopes[hh] * d) * distf
            s = jnp.where(valid, s, -1e30)
            m = jnp.max(s, axis=-1, keepdims=True)
            p = jnp.exp(s - m)
            den = jnp.sum(p, axis=-1, keepdims=True)
            o_ref[:, sl] = _dot_nn(p.astype(CDT), v) / den
            l_ref[:, sl] = jnp.broadcast_to(m + jnp.log(den), (BLK, HD))

    out_spec = _att_spec(d, lambda n: n)
    return pl.pallas_call(
        body, name="att_fwd_g%d" % grp, grid=(d, nb),
        in_specs=_att_qkv_specs(d, nb),
        out_specs=[out_spec, out_spec],
        out_shape=[_sds(_rm_shape(S, d, GW), F32)] * 2,
        compiler_params=_params(2),
    )(qkv, qkv, qkv, qkv, qkv)


def _att_bwd(grp, S, qkv, lse, do_g, c_g):
    _, d = ATT_GROUPS[grp]
    L = S // d
    nb = L // BLK
    slopes = _att_slopes(grp)
    scale = HD ** -0.5
    last = nb - 1

    def body(q_ref, kp_ref, kc_ref, vp_ref, vc_ref, l_ref, do_ref, c_ref, dq_ref, dk_ref, dv_ref, ck, cv):
        n = pl.program_id(1)

        @pl.when(n == 0)
        def _():
            ck[...] = jnp.zeros_like(ck)
            cv[...] = jnp.zeros_like(cv)

        @pl.when(n < nb)
        def _():
            valid, distf = _att_mask(n)
            for hh in range(ATT_HPG):
                sl = slice(hh * HD, (hh + 1) * HD)
                q = q_ref[:, sl]
                k = jnp.concatenate([kp_ref[:, sl], kc_ref[:, sl]], axis=0)
                v = jnp.concatenate([vp_ref[:, sl], vc_ref[:, sl]], axis=0)
                do = do_ref[:, sl]
                s = _dot_nt(q, k) * scale + (-slopes[hh] * d) * distf
                p = jnp.where(valid, jnp.exp(s - l_ref[:, sl][:, 0:1]), 0.0)
                dp = _dot_nt(do, v)
                ds = (p * (dp + c_ref[:, sl][:, 0:1]) * scale).astype(CDT)
                dq_ref[:, sl] = _dot_nn(ds, k)
                dk = _dot_tn(ds, q)
                dv = _dot_tn(p.astype(CDT), do)
                dk_ref[:, sl] = ck[:, sl] + dk[0:BLK]
                dv_ref[:, sl] = cv[:, sl] + dv[0:BLK]
                ck[:, sl] = dk[BLK:2 * BLK]
                cv[:, sl] = dv[BLK:2 * BLK]

        @pl.when(n == nb)
        def _():
            dk_ref[...] = ck[...]
            dv_ref[...] = cv[...]

    blk = (BLK, GW)
    at_q = _att_spec(d, lambda n: jnp.minimum(n, last))
    behind = _att_spec(d, lambda n: jnp.maximum(n - 1, 0))
    return pl.pallas_call(
        body, name="att_bwd_g%d" % grp, grid=(d, nb + 1),
        in_specs=_att_qkv_specs(d, nb) + [at_q, at_q, at_q],
        out_specs=[at_q, behind, behind],
        out_shape=[_sds(_rm_shape(S, d, GW), F32)] * 3,
        scratch_shapes=[pltpu.VMEM(blk, F32), pltpu.VMEM(blk, F32)],
        compiler_params=_params(2),
    )(qkv, qkv, qkv, qkv, qkv, lse, do_g, c_g)


def _mix_alpha(l0, l1, l2):
    mx = jnp.maximum(jnp.maximum(l0, l1), l2)
    e = [jnp.exp(l0 - mx), jnp.exp(l1 - mx), jnp.exp(l2 - mx)]
    tot = e[0] + e[1] + e[2]
    return [ei / tot for ei in e]


def _mix_fwd(S, os_, ls_, tm=512):
    dil = [d for _, d in ATT_GROUPS]

    def body(*refs):
        out, bufs = refs[6], refs[7:13]
        get = [_rm_reader(bufs[j], refs[j], dil[j % 3]) for j in range(6)]
        for s in range(ATT_HPG):
            al = _mix_alpha(*[get[3 + g](s) for g in range(3)])
            mixed = al[0] * get[0](s) + al[1] * get[1](s) + al[2] * get[2](s)
            out[:, s * HD:(s + 1) * HD] = mixed.astype(out.dtype)

    specs = [_rm_spec(tm, d, GW) for d in dil]
    return pl.pallas_call(
        body, name="mix_fwd", grid=(S // tm,), in_specs=specs * 2, out_specs=pl.BlockSpec((tm, GW), lambda i: (i, 0)),
        out_shape=_sds((S, GW), CDT), scratch_shapes=[pltpu.VMEM((ATT_HPG, tm, HD), F32)] * 6,
        compiler_params=_params(1))(*os_, *ls_)


def _mix_bwd(S, os_, ls_, do_a, tm=512):
    dil = [d for _, d in ATT_GROUPS]

    def body(*refs):
        d_ref, outs, bufs, tmp = refs[6], refs[7:13], refs[13:19], refs[19]
        get = [_rm_reader(bufs[j], refs[j], dil[j % 3]) for j in range(6)]
        for s in range(ATT_HPG):
            cols = slice(s * HD, (s + 1) * HD)
            al = _mix_alpha(*[get[3 + g](s) for g in range(3)])
            dv = d_ref[:, cols]
            o_a = al[0] * get[0](s) + al[1] * get[1](s) + al[2] * get[2](s)
            dsum = jnp.sum(dv * o_a, axis=-1, keepdims=True)
            for g in range(3):
                tmp[...] = al[g] * dv
                _rm_put(outs[g], cols, tmp, dil[g])
                tmp[...] = -(al[g] * dsum)
                _rm_put(outs[3 + g], cols, tmp, dil[g])

    specs = [_rm_spec(tm, d, GW) for d in dil]
    res = pl.pallas_call(
        body, name="mix_bwd", grid=(S // tm,), in_specs=specs * 2 + [pl.BlockSpec((tm, GW), lambda i: (i, 0))],
        out_specs=specs * 2,
        out_shape=[_sds(_rm_shape(S, d, GW), CDT) for d in dil] + [_sds(_rm_shape(S, d, GW), F32) for d in dil],
        scratch_shapes=[pltpu.VMEM((ATT_HPG, tm, HD), F32)] * 6 + [pltpu.VMEM((tm, HD), F32)],
        compiler_params=_params(1))(*os_, *ls_, do_a)
    return res[:3], res[3:]


def _ret_tables(dk):
    H, C = RET_HEADS, BLK
    log_g = jnp.log(1.0 - 2.0 ** (-5.0 - jnp.arange(H, dtype=F32)))
    idx = jnp.arange(C, dtype=F32)
    diff = idx[:, None] - idx[None, :]
    decay = jnp.where(diff >= 0, jnp.exp(log_g[:, None, None] * jnp.maximum(diff, 0.0)), 0.0)
    xi = jnp.exp(log_g[:, None] * (idx[None, :] + 1.0))
    zeta = jnp.exp(log_g[:, None] * (C - 1.0 - idx[None, :])) * (dk ** -0.5)
    g_chunk = jnp.exp(log_g * C)
    bc = lambda t: jnp.broadcast_to(t[:, :, None], (H, C, C))
    return decay, bc(xi), bc(zeta), jnp.broadcast_to(g_chunk[:, None, None], (H, 8, C))


def _gn_fwd(o, g, b):
    mu = jnp.mean(o, axis=-1, keepdims=True)
    xc = o - mu
    rstd = lax.rsqrt(jnp.mean(xc * xc, axis=-1, keepdims=True) + EPS)
    yh = xc * rstd
    return yh, rstd, yh * g + b


def _ret_specs(dk, dv, order):
    qk_w, v_w = RET_HEADS * dk, RET_HEADS * dv
    off_q = 3 * ATT_W
    off_k, off_v, off_g = off_q + qk_w, off_q + 2 * qk_w, off_q + 2 * qk_w + v_w
    assert off_q % dk == 0 and off_k % dk == 0 and off_v % dv == 0 and off_g % dv == 0
    tab = pl.BlockSpec((None, BLK, BLK), lambda h, i: (h, 0, 0))
    return [
        pl.BlockSpec((BLK, dk), lambda h, i: (order(i), off_q // dk + h)),
        pl.BlockSpec((BLK, dk), lambda h, i: (order(i), off_k // dk + h)),
        pl.BlockSpec((BLK, dv), lambda h, i: (order(i), off_v // dv + h)),
        pl.BlockSpec((BLK, dv), lambda h, i: (order(i), off_g // dv + h)),
        tab, tab, tab, pl.BlockSpec((None, 8, BLK), lambda h, i: (h, 0, 0)),
    ]


def _ret_fwd(proj, gn_g, gn_b, dk, dv):
    S = proj.shape[0]
    N = S // BLK
    H = RET_HEADS
    kscale = dk ** -0.5

    def body(q_ref, k_ref, v_ref, gr_ref, dec_ref, xi_ref, zeta_ref, gc_ref, g_ref, b_ref,
             opre_ref, or_ref, st_ref, state):
        n = pl.program_id(1)

        @pl.when(n == 0)
        def _():
            state[...] = jnp.zeros_like(state)

        q, k, v = q_ref[...], k_ref[...], v_ref[...]
        s = _dot_nt(q, k) * kscale * dec_ref[...]
        st = state[...]
        st_c = st.astype(CDT)
        st_ref[...] = st_c
        o = _dot_nn(s.astype(CDT), v) + _dot_nn(q, st_c) * xi_ref[:, 0:1]
        kz = (k.astype(F32) * zeta_ref[:, 0:1]).astype(CDT)
        state[...] = st * gc_ref[0:1, 0:1] + _dot_tn(kz, v)
        opre_ref[...] = o
        _, _, y = _gn_fwd(o, g_ref[...], b_ref[...])
        gr = gr_ref[...].astype(F32)
        or_ref[...] = (y * (gr * _sigmoid(gr))).astype(or_ref.dtype)

    v_w = H * dv
    row = pl.BlockSpec((1, dv), lambda h, i: (0, h))
    tile = pl.BlockSpec((BLK, dv), lambda h, i: (i, h))
    return pl.pallas_call(
        body, name="ret_fwd", grid=(H, N),
        in_specs=_ret_specs(dk, dv, lambda i: i) + [row, row],
        out_specs=[tile, tile, pl.BlockSpec((None, None, dk, dv), lambda h, i: (i, h, 0, 0))],
        out_shape=[_sds((S, v_w), F32), _sds((S, v_w), CDT), _sds((N, H, dk, dv), CDT)],
        scratch_shapes=[pltpu.VMEM((dk, dv), F32)],
        compiler_params=_params(2),
    )(proj, proj, proj, proj, *_ret_tables(dk), gn_g, gn_b)


def _ret_bwd(proj, gn_g, gn_b, o_pre, states, d_or, dk, dv):
    S = proj.shape[0]
    N = S // BLK
    H = RET_HEADS
    qk_w, v_w = H * dk, H * dv
    kscale = dk ** -0.5

    def body(q_ref, k_ref, v_ref, gr_ref, dec_ref, xi_ref, zeta_ref, gc_ref, g_ref, b_ref, opre_ref, st_ref, dor_ref,
             dq_ref, dk_ref, dv_ref, dgr_ref, dg_ref, db_ref, dstate):
        i = pl.program_id(1)

        @pl.when(i == 0)
        def _():
            dstate[...] = jnp.zeros_like(dstate)
            dg_ref[...] = jnp.zeros_like(dg_ref)
            db_ref[...] = jnp.zeros_like(db_ref)

        q, k, v = q_ref[...], k_ref[...], v_ref[...]
        decay, xi, zeta_s = dec_ref[...], xi_ref[:, 0:1], zeta_ref[:, 0:1]
        gr = gr_ref[...].astype(F32)
        sg = _sigmoid(gr)
        gain = g_ref[...]
        yh, rstd, y = _gn_fwd(opre_ref[...], gain, b_ref[...])
        d_or_v = dor_ref[...]
        dy = d_or_v * (gr * sg)
        dgr_ref[...] = (d_or_v * y * (sg * (1.0 + gr * (1.0 - sg)))).astype(dgr_ref.dtype)
        dg_ref[...] += jnp.sum(dy * yh, axis=0, keepdims=True)
        db_ref[...] += jnp.sum(dy, axis=0, keepdims=True)
        dyh = dy * gain
        do = rstd * (dyh - jnp.mean(dyh, axis=-1, keepdims=True) - yh * jnp.mean(dyh * yh, axis=-1, keepdims=True))
        do_c = do.astype(CDT)
        dox = (do * xi).astype(CDT)
        a_c = (_dot_nt(q, k) * kscale * decay).astype(CDT)
        g_c = (_dot_nt(do_c, v) * decay).astype(CDT)
        dsn = dstate[...]
        dsn_c = dsn.astype(CDT)
        kz = (k.astype(F32) * zeta_s).astype(CDT)
        dq_ref[...] = (_dot_nn(g_c, k) * kscale + _dot_nt(dox, st_ref[...])).astype(dq_ref.dtype)
        dk_ref[...] = (_dot_tn(g_c, q) * kscale + _dot_nt(v, dsn_c) * zeta_s).astype(dk_ref.dtype)
        dv_ref[...] = (_dot_tn(a_c, do_c) + _dot_nn(kz, dsn_c)).astype(dv_ref.dtype)
        dstate[...] = dsn * gc_ref[0:1, 0:1] + _dot_tn(q, dox)

    rev = lambda i: N - 1 - i
    row = pl.BlockSpec((1, dv), lambda h, i: (0, h))
    vt = pl.BlockSpec((BLK, dv), lambda h, i: (rev(i), h))
    qt = pl.BlockSpec((BLK, dk), lambda h, i: (rev(i), h))
    return pl.pallas_call(
        body, name="ret_bwd", grid=(H, N),
        in_specs=_ret_specs(dk, dv, rev) + [row, row, vt,
                 pl.BlockSpec((None, None, dk, dv), lambda h, i: (rev(i), h, 0, 0)), vt],
        out_specs=[qt, qt, vt, vt, row, row],
        out_shape=[_sds((S, qk_w), CDT), _sds((S, qk_w), CDT), _sds((S, v_w), CDT), _sds((S, v_w), CDT),
                   _sds((1, v_w), F32), _sds((1, v_w), F32)],
        scratch_shapes=[pltpu.VMEM((dk, dv), F32)],
        compiler_params=_params(2),
    )(proj, proj, proj, proj, *_ret_tables(dk), gn_g, gn_b, o_pre, states, d_or)


def _merge_fwd(o_a, o_r, wa, wb, proj, d_model, tm=512, tn=256):
    S, in_w = proj.shape
    off_a, off_b = in_w - 2 * d_model, in_w - d_model
    assert off_a % tn == 0 and off_b % tn == 0

    def body(oa_ref, or_ref, wa_ref, wb_ref, ga_ref, gb_ref, y_ref, pa_ref, pb_ref):
        pa = _dot_nn(oa_ref[...], wa_ref[...])
        pb = _dot_nn(or_ref[...], wb_ref[...])
        y = _sigmoid(ga_ref[...].astype(F32)) * pa + _sigmoid(gb_ref[...].astype(F32)) * pb
        y_ref[...] = y.astype(y_ref.dtype)
        pa_ref[...] = pa.astype(pa_ref.dtype)
        pb_ref[...] = pb.astype(pb_ref.dtype)

    ka, kb = o_a.shape[1], o_r.shape[1]
    out = pl.BlockSpec((tm, tn), lambda i, j: (i, j))
    return pl.pallas_call(
        body, name="merge_fwd", grid=(S // tm, d_model // tn),
        in_specs=[pl.BlockSpec((tm, ka), lambda i, j: (i, 0)), pl.BlockSpec((tm, kb), lambda i, j: (i, 0)),
                  pl.BlockSpec((ka, tn), lambda i, j: (0, j)), pl.BlockSpec((kb, tn), lambda i, j: (0, j)),
                  pl.BlockSpec((tm, tn), lambda i, j: (i, off_a // tn + j)),
                  pl.BlockSpec((tm, tn), lambda i, j: (i, off_b // tn + j))],
        out_specs=[out, out, out], out_shape=[_sds((S, d_model), CDT)] * 3,
        compiler_params=_params(2))(o_a, o_r, wa, wb, proj, proj)


def _local_step(x, target, w_in, small, late_weights, on_grads, after_w_in=None):
    S, D = x.shape
    in_w = w_in.shape[1]
    d_ff = 4 * D
    ret_v_w = 2 * D
    dv = ret_v_w // RET_HEADS
    dk = (in_w - 3 * ATT_W - 2 * ret_v_w - 2 * D) // (2 * RET_HEADS)
    gqk = jnp.concatenate([small["q_norm_g"].reshape(1, ATT_W), small["k_norm_g"].reshape(1, ATT_W)], axis=1)
    g1, g2 = small["norm1_g"], small["norm2_g"]
    gn_g, gn_b = small["ret_gn_g"], small["ret_gn_b"]

    xn = _rms_fwd("rms1_fwd", x, g1)
    (proj,) = _matmul("in_proj", "nn", xn, w_in, tm=1024, tn=640, tk=D,
                      outs=[(_sds((S, in_w), CDT), _mn(1024, 640))], epilogue=_ep_store, deps=[after_w_in])
    qkv = _qknorm_fwd(proj, gqk)
    att = [_att_fwd(g, S, qkv[g]) for g in range(3)]
    os_, ls_ = [a[0] for a in att], [a[1] for a in att]
    o_a = _mix_fwd(S, os_, ls_)
    o_pre, o_r, states = _ret_fwd(proj, gn_g, gn_b, dk, dv)
    w = late_weights(o_r)
    y, pa, pb = _merge_fwd(o_a, o_r, w["w_proj_a"], w["w_proj_b"], proj, D)
    (x1,) = _matmul("out_proj", "nn", y, w["w_out"], tm=512, tn=D, tk=D,
                    extras=[(x, _mn(512, D))], outs=[(_sds((S, D), F32), _mn(512, D))], epilogue=_ep_resid)
    xn2 = _rms_fwd("rms2_fwd", x1, g2)
    hid, act = _matmul("mlp_up", "nn", xn2, w["w_up"], tm=1024, tn=512, tk=D,
                       outs=[(_sds((S, d_ff), CDT), _mn(1024, 512))] * 2, epilogue=_ep_up)
    dx2, dx2c, loss_row = _matmul(
        "mlp_down_loss", "nn", act, w["w_down"], tm=512, tn=D, tk=1024,
        extras=[(x1, _mn(512, D)), (target, _mn(512, D))],
        outs=[(_sds((S, D), F32), _mn(512, D)), (_sds((S, D), CDT), _mn(512, D)), (_sds((1, D), F32), _row(D))],
        epilogue=functools.partial(_ep_down_loss, inv_d=1.0 / D))
    loss = 0.5 * jnp.sum(loss_row) / D

    (dh,) = _matmul("d_hidden", "nt", dx2c, w["w_down"], tm=1024, tn=512, tk=D,
                    extras=[(hid, _mn(1024, 512))], outs=[(_sds((S, d_ff), CDT), _mn(1024, 512))], epilogue=_ep_dh)
    (gw_down,) = _matmul("dw_down", "tn", act, dx2c, tm=1024, tn=D, tk=1024,
                         outs=[(_sds((d_ff, D), F32), _mn(1024, D))], epilogue=_ep_store)
    (gw_up,) = _matmul("dw_up", "tn", xn2, dh, tm=D, tn=1024, tk=1024,
                       outs=[(_sds((D, d_ff), F32), _mn(D, 1024))], epilogue=_ep_store)
    tok = on_grads({"w_down": gw_down, "w_up": gw_up})
    dx1, dx1c, dg2 = _matmul(
        "d_x1", "nt", dh, w["w_up"], tm=512, tn=D, tk=1024,
        extras=[(x1, _mn(512, D)), (g2, _row(D)), (dx2, _mn(512, D))],
        outs=[(_sds((S, D), F32), _mn(512, D)), (_sds((S, D), CDT), _mn(512, D)), (_sds((1, D), F32), _row(D))],
        epilogue=_ep_rms_bwd, deps=[tok])

    gt = 512
    assert (in_w - 2 * D) % gt == 0
    off_a, off_b = (in_w - 2 * D) // gt, (in_w - D) // gt
    dpa, dpb, dga, dgb = _matmul(
        "d_gates", "nt", dx1c, w["w_out"], tm=512, tn=gt, tk=D,
        extras=[(proj, _mn(512, gt, off_a)), (proj, _mn(512, gt, off_b)), (pa, _mn(512, gt)), (pb, _mn(512, gt))],
        outs=[(_sds((S, D), CDT), _mn(512, gt))] * 4, epilogue=_ep_gates)
    (gw_out,) = _matmul("dw_out", "tn", y, dx1c, tm=D, tn=D, tk=1024,
                        outs=[(_sds((D, D), F32), _mn(D, D))], epilogue=_ep_store)
    (gw_pa,) = _matmul("dw_proj_a", "tn", o_a, dpa, tm=GW, tn=D, tk=1024,
                       outs=[(_sds((GW, D), F32), _mn(GW, D))], epilogue=_ep_store)
    (gw_pb,) = _matmul("dw_proj_b", "tn", o_r, dpb, tm=1024, tn=D, tk=1024,
                       outs=[(_sds((ret_v_w, D), F32), _mn(1024, D))], epilogue=_ep_store)
    (do_a,) = _matmul("d_o_a", "nt", dpa, w["w_proj_a"], tm=1024, tn=GW, tk=D,
                      outs=[(_sds((S, GW), F32), _mn(1024, GW))], epilogue=_ep_store)
    tok = on_grads({"w_out": gw_out, "w_proj_a": gw_pa, "w_proj_b": gw_pb})
    (d_or,) = _matmul("d_o_r", "nt", dpb, w["w_proj_b"], tm=1024, tn=512, tk=D,
                      outs=[(_sds((S, ret_v_w), F32), _mn(1024, 512))], epilogue=_ep_store, deps=[tok])

    dq_r, dk_r, dv_r, dgr, dgn_g, dgn_b = _ret_bwd(proj, gn_g, gn_b, o_pre, states, d_or, dk, dv)
    do_gs, c_gs = _mix_bwd(S, os_, ls_, do_a)
    datt_parts = [_att_bwd(g, S, qkv[g], ls_[g], do_gs[g], c_gs[g]) for g in range(3)]
    d_att, dgqk = _qknorm_bwd(proj, gqk, [p[0] for p in datt_parts], [p[1] for p in datt_parts],
                              [p[2] for p in datt_parts])
    dproj = jnp.concatenate([d_att, dq_r, dk_r, dv_r, dgr, dga, dgb], axis=1)

    (gw_in,) = _matmul("dw_in", "tn", xn, dproj, tm=D, tn=640, tk=1024,
                       outs=[(_sds((D, in_w), F32), _mn(D, 640))], epilogue=_ep_store)
    tok = on_grads({"w_in": gw_in})
    grad_x, _, dg1 = _matmul(
        "d_x", "nt", dproj, w_in, tm=512, tn=D, tk=1280,
        extras=[(x, _mn(512, D)), (g1, _row(D)), (dx1, _mn(512, D))],
        outs=[(_sds((S, D), F32), _mn(512, D)), (_sds((S, D), CDT), _mn(512, D)), (_sds((1, D), F32), _row(D))],
        epilogue=_ep_rms_bwd, deps=[tok])

    smallg = {"norm1_g": dg1, "q_norm_g": dgqk[:, :ATT_W], "k_norm_g": dgqk[:, ATT_W:],
              "ret_gn_g": dgn_g, "ret_gn_b": dgn_b, "norm2_g": dg2}
    return loss, grad_x, smallg


N_CHIPS = 4
N_DEV = 8


def _place():
    x, y, c = lax.axis_index("x"), lax.axis_index("y"), lax.axis_index("c")
    return x, y, c


def _other_chips(x, y):
    out = []
    for fx, fy in ((1, 0), (0, 1), (1, 1)):
        px = 1 - x if fx else x
        py = 1 - y if fy else y
        out.append(((px, py), 2 * px + py))
    return out


def _gather_weights(shards):
    n = len(shards)

    def body(*refs):
        ins, outs = refs[:n], refs[n:2 * n]
        send, recv = refs[2 * n:]
        x, y, c = _place()
        me = 2 * x + y
        peers = _other_chips(x, y)

        def rows(w):
            half = ins[w].shape[0] // 2
            return pl.ds(c * half, half)

        sends = []
        for w in range(n):
            for j, ((px, py), _) in enumerate(peers):
                cp = pltpu.make_async_remote_copy(
                    src_ref=ins[w].at[rows(w), :], dst_ref=outs[w].at[me, rows(w), :], send_sem=send.at[w * 3 + j],
                    recv_sem=recv.at[w * 3 + j], device_id=(px, py, c), device_id_type=MESH)
                cp.start()
                sends.append(cp)
        for w in range(n):
            for j, ((px, py), pidx) in enumerate(peers):
                pltpu.make_async_remote_copy(
                    src_ref=ins[w].at[rows(w), :], dst_ref=outs[w].at[pidx, rows(w), :], send_sem=send.at[w * 3 + j],
                    recv_sem=recv.at[w * 3 + j], device_id=(px, py, c), device_id_type=MESH).wait_recv()
        for cp in sends:
            cp.wait_send()

    return pl.pallas_call(
        body, name="gather_weights",
        in_specs=[HBM_SPEC] * n, out_specs=[HBM_SPEC] * n,
        out_shape=[_sds((N_CHIPS,) + s.shape, s.dtype) for s in shards],
        scratch_shapes=[pltpu.SemaphoreType.DMA((3 * n,)), pltpu.SemaphoreType.DMA((3 * n,))],
    )(*shards)


SEM_SPEC = pl.BlockSpec(memory_space=pltpu.SEMAPHORE)
ANY_SPEC = pl.BlockSpec(memory_space=pl.ANY)
EFFECT = pltpu.SideEffectType.DATAFLOW_SIDE_EFFECTING


def _ici_copies(kind, srcs, lands, send, recv):
    x, y, c = _place()
    me = 2 * x + y
    out = []
    for w, (s, l) in enumerate(zip(srcs, lands)):
        for j, ((px, py), pidx) in enumerate(_other_chips(x, y)):
            if kind == "gather":
                half = s.shape[0] // 2
                rows = pl.ds(c * half, half)
                src, dst_there, dst_here = s.at[rows, :], l.at[me, rows, :], l.at[pidx, rows, :]
            else:
                src, dst_there, dst_here = s.at[pidx], l.at[me], l.at[pidx]
            out.append((src, dst_there, dst_here, send.at[3 * w + j], recv.at[3 * w + j], (px, py, c)))
    return out


def _exchange_start(name, kind, srcs, land_shapes):
    n = len(srcs)

    def body(*refs):
        src_refs, land_refs = refs[:n], refs[n:2 * n]
        send, recv = refs[2 * n], refs[2 * n + 1]
        token = refs[-1]
        for src, dst, _, ss, rs, dev in _ici_copies(kind, src_refs, land_refs, send, recv):
            pltpu.make_async_remote_copy(src_ref=src, dst_ref=dst, send_sem=ss, recv_sem=rs, device_id=dev,
                                         device_id_type=MESH).start()
        token[...] = jnp.zeros_like(token)

    thru = [pltpu.HBM(s.shape, s.dtype) for s in srcs] + [pltpu.HBM(shape, dtype) for shape, dtype in land_shapes]
    res = pl.pallas_call(
        body, name=name,
        out_shape=(pltpu.SemaphoreType.DMA((3 * n,)), pltpu.SemaphoreType.DMA((3 * n,)), *thru, _sds((8, LANES), F32)),
        in_specs=[HBM_SPEC] * (2 * n), out_specs=(SEM_SPEC, SEM_SPEC, *[HBM_SPEC] * (2 * n), VMEM_SPEC),
        input_output_aliases={i: 2 + i for i in range(2 * n)},
        compiler_params=pltpu.CompilerParams(has_side_effects=EFFECT),
    )(*[pltpu.with_memory_space_constraint(s, pltpu.HBM) for s in srcs],
      *[pltpu.with_memory_space_constraint(lax.empty(shape, dtype), pltpu.HBM) for shape, dtype in land_shapes])
    return res[0], res[1], list(res[2:2 + n]), list(res[2 + n:2 + 2 * n]), res[-1]


def _exchange_wait(name, kind, send, recv, srcs, lands, after):
    n = len(srcs)

    def body(*refs):
        src_refs, land_refs = refs[:n], refs[n:2 * n]
        send_ref, recv_ref = refs[2 * n], refs[2 * n + 1]
        for src, _, dst, ss, rs, dev in _ici_copies(kind, src_refs, land_refs, send_ref, recv_ref):
            cp = pltpu.make_async_remote_copy(src_ref=src, dst_ref=dst, send_sem=ss, recv_sem=rs, device_id=dev,
                                              device_id_type=MESH)
            cp.wait_send()
            cp.wait_recv()

    thru = [pltpu.HBM(t.shape, t.dtype) for t in list(srcs) + list(lands)]
    res = pl.pallas_call(
        body, name=name, out_shape=thru,
        in_specs=[HBM_SPEC] * (2 * n) + [SEM_SPEC, SEM_SPEC, ANY_SPEC], out_specs=[HBM_SPEC] * (2 * n),
        input_output_aliases={i: i for i in range(2 * n)},
        compiler_params=pltpu.CompilerParams(has_side_effects=EFFECT),
    )(*srcs, *lands, send, recv, after)
    return list(res[:n]), list(res[n:])


PAIR_TILE_ELEMS = 1 << 19


def _pair_fill(name, gathered, mine, core, others, chip):
    k, r, C = gathered.shape
    half = r // 2
    tr = _row_tile(half, C, PAIR_TILE_ELEMS, mult=16)
    nt = half // tr
    n_far = N_CHIPS - 1

    def body(c_ref, o_ref, chip_ref, in_ref, mine_ref, out_ref, slot, send, recv):
        j = pl.program_id(0)
        b = (j * nt + pl.program_id(1)) % 2
        x, y, c = _place()
        cp = pltpu.make_async_remote_copy(src_ref=in_ref, dst_ref=slot.at[b], send_sem=send.at[b],
                                          recv_sem=recv.at[b], device_id=(x, y, 1 - c), device_id_type=MESH)

        @pl.when(j < n_far)
        def _():
            cp.start()
            cp.wait_recv()
            out_ref[...] = slot[b]
            cp.wait_send()

        @pl.when(j >= n_far)
        def _():
            out_ref[...] = mine_ref[...]

    def far(j):
        return jnp.minimum(j, n_far - 1)

    grid_spec = pltpu.PrefetchScalarGridSpec(
        num_scalar_prefetch=3, grid=(n_far + 2, nt),
        in_specs=[pl.BlockSpec((tr, C), lambda j, i, c, o, m: (
                      (2 * o[far(j)] + c[0]) * nt + jnp.where(j < n_far, i, nt - 1), 0)),
                  pl.BlockSpec((tr, C), lambda j, i, c, o, m: (jnp.where(j < n_far, 0, (j - n_far) * nt + i), 0))],
        out_specs=pl.BlockSpec((tr, C), lambda j, i, c, o, m: (
            jnp.where(j < n_far, 2 * o[far(j)] + 1 - c[0], 2 * m[0] + j - n_far) * nt + i, 0)),
        scratch_shapes=[pltpu.VMEM((2, tr, C), gathered.dtype), pltpu.SemaphoreType.DMA((2,)),
                        pltpu.SemaphoreType.DMA((2,))])
    out = pl.pallas_call(body, name=name, grid_spec=grid_spec, out_shape=_sds((k * r, C), gathered.dtype),
                         input_output_aliases={3: 0}, compiler_params=_params(2))(
                             core, others, chip, gathered.reshape(k * r, C), mine)
    return out.reshape(k, r, C)


def _pair_reduce(name, g, core):
    k, R, C = g.shape
    half = R // 2
    tr = _row_tile(half, C, PAIR_TILE_ELEMS)
    nt = half // tr

    def body(c_ref, mine_ref, give_ref, out_ref, wire_ref, slot, send, recv):
        b = (pl.program_id(0) * nt + pl.program_id(1)) % 2
        x, y, c = _place()
        cp = pltpu.make_async_remote_copy(src_ref=give_ref, dst_ref=slot.at[b], send_sem=send.at[b],
                                          recv_sem=recv.at[b], device_id=(x, y, 1 - c), device_id_type=MESH)
        cp.start()
        cp.wait_recv()
        tot = mine_ref[...] + slot[b]
        out_ref[...] = tot
        wire_ref[...] = tot.astype(wire_ref.dtype)
        cp.wait_send()

    blk = (tr, C)
    out_spec = pl.BlockSpec(blk, lambda s, i, c: (s * nt + i, 0))
    grid_spec = pltpu.PrefetchScalarGridSpec(
        num_scalar_prefetch=1, grid=(k, nt),
        in_specs=[pl.BlockSpec(blk, lambda s, i, c: ((2 * s + c[0]) * nt + i, 0)),
                  pl.BlockSpec(blk, lambda s, i, c: ((2 * s + 1 - c[0]) * nt + i, 0))],
        out_specs=[out_spec, out_spec],
        scratch_shapes=[pltpu.VMEM((2, tr, C), F32), pltpu.SemaphoreType.DMA((2,)), pltpu.SemaphoreType.DMA((2,))])
    g2 = g.reshape(k * R, C)
    out, wire = pl.pallas_call(body, name=name, grid_spec=grid_spec,
                               out_shape=[_sds((k * half, C), F32), _sds((k * half, C), CDT)],
                               compiler_params=_params(2))(core, g2, g2)
    return out, wire.reshape(k, half, C)


def _pair_share(name, f, core):
    half, C = f.shape
    tr = _row_tile(half, C, PAIR_TILE_ELEMS)
    nt = half // tr

    def body(c_ref, f_ref, out_ref, slot, send, recv):
        p = pl.program_id(1)
        b = pl.program_id(0) % 2
        x, y, c = _place()
        cp = pltpu.make_async_remote_copy(src_ref=f_ref, dst_ref=slot.at[b], send_sem=send.at[b],
                                          recv_sem=recv.at[b], device_id=(x, y, 1 - c), device_id_type=MESH)

        @pl.when(p == 0)
        def _():
            cp.start()
            out_ref[...] = f_ref[...]

        @pl.when(p == 1)
        def _():
            cp.wait_recv()
            out_ref[...] = slot[b]
            cp.wait_send()

    grid_spec = pltpu.PrefetchScalarGridSpec(
        num_scalar_prefetch=1, grid=(nt, 2),
        in_specs=[pl.BlockSpec((tr, C), lambda i, p, c: (i, 0))],
        out_specs=pl.BlockSpec((tr, C), lambda i, p, c: (jnp.where(p == 0, c[0], 1 - c[0]) * nt + i, 0)),
        scratch_shapes=[pltpu.VMEM((2, tr, C), F32), pltpu.SemaphoreType.DMA((2,)), pltpu.SemaphoreType.DMA((2,))])
    return pl.pallas_call(body, name=name, grid_spec=grid_spec, out_shape=_sds((2 * half, C), F32),
                          compiler_params=_params(2))(core, f)


def _all_reduce_small(v):
    r, cdim = v.shape

    def body(v_ref, o_ref, buf, send, recv):
        x, y, c = _place()
        me = 4 * x + 2 * y + c
        buf[me] = v_ref[...]
        sends = []
        for m in range(1, N_DEV):
            px = 1 - x if m & 4 else x
            py = 1 - y if m & 2 else y
            pc = 1 - c if m & 1 else c
            cp = pltpu.make_async_remote_copy(src_ref=v_ref, dst_ref=buf.at[me], send_sem=send.at[m - 1],
                                              recv_sem=recv.at[m - 1], device_id=(px, py, pc), device_id_type=MESH)
            cp.start()
            sends.append((cp, 4 * px + 2 * py + pc))
        for m, (cp, pidx) in enumerate(sends):
            pltpu.make_async_remote_copy(src_ref=v_ref, dst_ref=buf.at[pidx], send_sem=send.at[m], recv_sem=recv.at[m],
                                         device_id=(x, y, c), device_id_type=MESH).wait_recv()
        for cp, _ in sends:
            cp.wait_send()
        tot = buf[0]
        for k in range(1, N_DEV):
            tot = tot + buf[k]
        o_ref[...] = tot

    return pl.pallas_call(
        body, name="all_reduce_small", in_specs=[VMEM_SPEC], out_specs=VMEM_SPEC,
        out_shape=_sds((r, cdim), F32),
        scratch_shapes=[pltpu.VMEM((N_DEV, r, cdim), F32), pltpu.SemaphoreType.DMA((N_DEV - 1,)),
                        pltpu.SemaphoreType.DMA((N_DEV - 1,))],
    )(v)


def _row_tile(rows, cols, budget_elems=1 << 18, mult=8):
    if rows % mult:
        return rows
    t = max(mult, (budget_elems // cols) // mult * mult)
    while rows % t:
        t -= mult
    return t


def _sum4(name, own, by_chip, chip, others):
    k, R, C = by_chip.shape
    tr = _row_tile(R, C, mult=16)
    nt = R // tr

    def body(chip_ref, others_ref, own_ref, a_ref, b_ref, c_ref, o_ref):
        o_ref[...] = ((own_ref[...] + a_ref[...].astype(F32)) + b_ref[...].astype(F32)) + c_ref[...].astype(F32)

    def from_chip(j):
        return pl.BlockSpec((tr, C), lambda i, chip, oth: (oth[j] * nt + i, 0))

    grid_spec = pltpu.PrefetchScalarGridSpec(
        num_scalar_prefetch=2, grid=(nt,),
        in_specs=[pl.BlockSpec((tr, C), lambda i, chip, oth: (chip[0] * nt + i, 0)),
                  from_chip(0), from_chip(1), from_chip(2)],
        out_specs=pl.BlockSpec((tr, C), lambda i, chip, oth: (i, 0)))
    by2 = by_chip.reshape(k * R, C)
    return pl.pallas_call(body, name=name, grid_spec=grid_spec, out_shape=_sds((R, C), F32),
                          compiler_params=_params(1))(chip, others, own, by2, by2, by2)


def _adamw(name, w, g, m, v):
    R, C = w.shape
    tr = _row_tile(R, C, 1 << 17)

    def body(w_ref, g_ref, m_ref, v_ref, d_ref, nm_ref, nv_ref):
        gv = g_ref[...]
        nm = ADAM_B1 * m_ref[...] + (1.0 - ADAM_B1) * gv
        nv = ADAM_B2 * v_ref[...] + (1.0 - ADAM_B2) * (gv * gv)
        m_hat = nm / (1.0 - ADAM_B1 ** ADAM_STEP)
        v_hat = nv / (1.0 - ADAM_B2 ** ADAM_STEP)
        d_ref[...] = -ADAM_LR * (m_hat / (jnp.sqrt(v_hat) + ADAM_EPS) + ADAM_WD * w_ref[...])
        nm_ref[...] = nm
        nv_ref[...] = nv

    spec = pl.BlockSpec((tr, C), lambda i: (i, 0))
    return pl.pallas_call(body, name=name, grid=(R // tr,), in_specs=[spec] * 4, out_specs=[spec] * 3,
                          out_shape=[_sds((R, C), F32)] * 3, compiler_params=_params(1))(w, g, m, v)


BIG = ("w_in", "w_proj_a", "w_proj_b", "w_out", "w_up", "w_down")
COL_SHARDED = ("w_in", "w_proj_a", "w_up")
SMALL = ("norm1_g", "q_norm_g", "k_norm_g", "ret_gn_g", "ret_gn_b", "norm2_g")
ALL_W = ("norm1_g", "w_in", "q_norm_g", "k_norm_g", "ret_gn_g", "ret_gn_b", "w_proj_a", "w_proj_b", "w_out",
         "norm2_g", "w_up", "w_down")
LANES = 128


def _to_full(name, gathered):
    k, r, c = gathered.shape
    if name in COL_SHARDED:
        return gathered.transpose(1, 0, 2).reshape(r, k * c)
    return gathered.reshape(k * r, c)


def _to_shard_major(name, full):
    if name in COL_SHARDED:
        r, c4 = full.shape
        return full.reshape(r, N_CHIPS, c4 // N_CHIPS).transpose(1, 0, 2)
    r4, c = full.shape
    return full.reshape(N_CHIPS, r4 // N_CHIPS, c)


def kernel(x, norm1_g, w_in, q_norm_g, k_norm_g, ret_gn_g, ret_gn_b, w_proj_a, w_proj_b, w_out, norm2_g, w_up, w_down, loss_target, m_norm1_g, m_w_in, m_q_norm_g, m_k_norm_g, m_ret_gn_g, m_ret_gn_b, m_w_proj_a, m_w_proj_b, m_w_out, m_norm2_g, m_w_up, m_w_down, v_norm1_g, v_w_in, v_q_norm_g, v_k_norm_g, v_ret_gn_g, v_ret_gn_b, v_w_proj_a, v_w_proj_b, v_w_out, v_norm2_g, v_w_up, v_w_down):
    weights = dict(norm1_g=norm1_g, w_in=w_in, q_norm_g=q_norm_g, k_norm_g=k_norm_g, ret_gn_g=ret_gn_g,
                   ret_gn_b=ret_gn_b, w_proj_a=w_proj_a, w_proj_b=w_proj_b, w_out=w_out, norm2_g=norm2_g,
                   w_up=w_up, w_down=w_down)
    moments_m = dict(norm1_g=m_norm1_g, w_in=m_w_in, q_norm_g=m_q_norm_g, k_norm_g=m_k_norm_g, ret_gn_g=m_ret_gn_g,
                     ret_gn_b=m_ret_gn_b, w_proj_a=m_w_proj_a, w_proj_b=m_w_proj_b, w_out=m_w_out,
                     norm2_g=m_norm2_g, w_up=m_w_up, w_down=m_w_down)
    moments_v = dict(norm1_g=v_norm1_g, w_in=v_w_in, q_norm_g=v_q_norm_g, k_norm_g=v_k_norm_g, ret_gn_g=v_ret_gn_g,
                     ret_gn_b=v_ret_gn_b, w_proj_a=v_w_proj_a, w_proj_b=v_w_proj_b, w_out=v_w_out,
                     norm2_g=v_norm2_g, w_up=v_w_up, w_down=v_w_down)

    mx, my = lax.axis_index("x"), lax.axis_index("y")
    core = lax.axis_index("c").astype(jnp.int32).reshape(1)
    chip = (2 * mx + my).astype(jnp.int32).reshape(1)
    others = jnp.stack([2 * (1 - mx) + my, 2 * mx + 1 - my, 2 * (1 - mx) + 1 - my]).astype(jnp.int32)
    shards = {n: weights[n][0].astype(CDT) for n in BIG}
    (g_in,) = _gather_weights([shards["w_in"]])
    w_in_full = _to_full("w_in", _pair_fill("pair_fill_w_in", g_in, shards["w_in"], core, others, chip))
    late = [n for n in BIG if n != "w_in"]
    l_send, l_recv, l_srcs, l_lands, l_token = _exchange_start(
        "gather_late_start", "gather", [shards[n] for n in late], [((N_CHIPS,) + shards[n].shape, CDT) for n in late])

    def late_weights(after):
        srcs, lands = _exchange_wait("gather_late_wait", "gather", l_send, l_recv, l_srcs, l_lands, after)
        out = {}
        for n, mine, land in zip(late, srcs, lands):
            out[n] = _to_full(n, _pair_fill("pair_fill_%s" % n, land, mine, core, others, chip))
        return out

    pending = []

    def on_grads(group):
        names = list(group)
        red = [_pair_reduce("pair_reduce_%s" % n, _to_shard_major(n, group[n]), core) for n in names]
        wires = [wire for _, wire in red]
        send, recv, srcs, lands, token = _exchange_start(
            "scatter_start_%s" % names[0], "scatter", wires, [(wire.shape, wire.dtype) for wire in wires])
        pending.append((names, [own for own, _ in red], send, recv, srcs, lands))
        return token

    small = {n: weights[n].reshape(1, -1) for n in SMALL}

    loss, grad_x, small_g = _local_step(x[0], loss_target[0], w_in_full, small, late_weights, on_grads, l_token)
    loss = lax.psum(loss, ("x", "y", "c"))

    grads = {}
    for names, owns, send, recv, srcs, lands in pending:
        _, got = _exchange_wait("scatter_wait_%s" % names[0], "scatter", send, recv, srcs, lands, grad_x)
        for n, own, by_chip in zip(names, owns, got):
            half = _sum4("chip_sum_%s" % n, own, by_chip, chip, others)
            grads[n] = _pair_share("pair_share_%s" % n, half, core)

    packed = jnp.concatenate([small_g[n].reshape(1, -1) for n in SMALL], axis=1)
    sizes = [small_g[n].size for n in SMALL]
    red = _all_reduce_small(packed.reshape(-1, LANES)).reshape(1, -1)
    off = 0
    for n, sz in zip(SMALL, sizes):
        grads[n] = red[:, off:off + sz]
        off += sz

    out_g, out_d, out_m, out_v = {}, {}, {}, {}
    for n in ALL_W:
        shape = weights[n].shape
        two_d = (shape[-2], shape[-1]) if n in BIG else (1, weights[n].size)
        g2 = grads[n].reshape(two_d)
        d, nm, nv = _adamw("adamw_%s" % n, weights[n].reshape(two_d), g2, moments_m[n].reshape(two_d),
                           moments_v[n].reshape(two_d))
        out_g[n], out_d[n], out_m[n], out_v[n] = (t.reshape(shape) for t in (g2, d, nm, nv))

    return (loss, grad_x[None], *[out_g[n] for n in ALL_W], *[out_d[n] for n in ALL_W],
            *[out_m[n] for n in ALL_W], *[out_v[n] for n in ALL_W])
```

```python
import functools
import math

import jax
import jax.numpy as jnp
from jax import lax
from jax.experimental import pallas as pl
from jax.experimental.pallas import tpu as pltpu

CDT = jnp.bfloat16
F32 = jnp.float32
EPS = 1e-6

ATT_GROUPS = ((128, 1), (512, 4), (2048, 16))
ATT_HPG = 4
ATT_HEADS = 12
HD = 128
BLK = 128
ATT_W = ATT_HEADS * HD
GW = ATT_HPG * HD
RET_HEADS = 4

ADAM_LR = 0.001
ADAM_B1 = 0.9
ADAM_B2 = 0.999
ADAM_EPS = 1e-08
ADAM_WD = 0.01
ADAM_STEP = 10

VMEM_LIMIT_BYTES = 48 * 1024 * 1024
MESH = pl.DeviceIdType.MESH
HBM_SPEC = pl.BlockSpec(memory_space=pltpu.HBM)
VMEM_SPEC = pl.BlockSpec(memory_space=pltpu.VMEM)


def _params(n_axes):
    return pltpu.CompilerParams(dimension_semantics=("arbitrary",) * n_axes,
                                vmem_limit_bytes=VMEM_LIMIT_BYTES)


def _dot_nn(a, b):
    return jnp.dot(a, b, preferred_element_type=F32)


def _dot_nt(a, b):
    return lax.dot_general(a, b, (((1,), (1,)), ((), ())), preferred_element_type=F32)


def _dot_tn(a, b):
    return lax.dot_general(a, b, (((0,), (0,)), ((), ())), preferred_element_type=F32)


def _sigmoid(v):
    return 1.0 / (1.0 + jnp.exp(-v))


def _matmul(name, mode, a, b, *, tm, tn, tk, extras=(), outs, epilogue, deps=()):
    deps = [d for d in deps if d is not None]
    if mode == "nn":
        (M, K), (K2, N) = a.shape, b.shape
    elif mode == "nt":
        (M, K), (N, K2) = a.shape, b.shape
    else:
        (K, M), (K2, N) = a.shape, b.shape
    assert K == K2 and M % tm == 0 and N % tn == 0 and K % tk == 0, (name, a.shape, b.shape)
    ni, nj, nk = M // tm, N // tn, K // tk
    if mode == "tn":
        a_spec = pl.BlockSpec((tk, tm), lambda i, j, k: (k, i))
    else:
        a_spec = pl.BlockSpec((tm, tk), lambda i, j, k: (i, k))
    if mode == "nt":
        b_spec = pl.BlockSpec((tn, tk), lambda i, j, k: (j, k))
    else:
        b_spec = pl.BlockSpec((tk, tn), lambda i, j, k: (k, j))
    dot = {"nn": _dot_nn, "nt": _dot_nt, "tn": _dot_tn}[mode]
    n_ex, n_out, n_dep = len(extras), len(outs), len(deps)

    def body(*refs):
        a_ref, b_ref = refs[0], refs[1]
        ex = refs[2:2 + n_ex]
        out = refs[2 + n_ex + n_dep:2 + n_ex + n_dep + n_out]
        acc = refs[-1] if nk > 1 else None
        i = pl.program_id(0)
        k = pl.program_id(2)
        prod = dot(a_ref[...].astype(CDT), b_ref[...].astype(CDT))
        if nk == 1:
            epilogue(prod, ex, out, i)
            return

        @pl.when(k == 0)
        def _():
            acc[...] = prod

        @pl.when((k > 0) & (k < nk - 1))
        def _():
            acc[...] += prod

        @pl.when(k == nk - 1)
        def _():
            epilogue(acc[...] + prod, ex, out, i)

    res = pl.pallas_call(
        body, name=name, grid=(ni, nj, nk),
        in_specs=[a_spec, b_spec] + [s for _, s in extras] + [pl.BlockSpec(memory_space=pl.ANY)] * n_dep,
        out_specs=[s for _, s in outs],
        out_shape=[o for o, _ in outs],
        scratch_shapes=[pltpu.VMEM((tm, tn), F32)] if nk > 1 else [],
        compiler_params=_params(3),
    )(a, b, *[e for e, _ in extras], *deps)
    return res


def _mn(tm, tn, col_off=0):
    return pl.BlockSpec((tm, tn), lambda i, j, k: (i, j + col_off))


def _row(tn):
    return pl.BlockSpec((1, tn), lambda i, j, k: (0, j))


def _ep_store(acc, ex, out, i):
    out[0][...] = acc.astype(out[0].dtype)


def _ep_resid(acc, ex, out, i):
    out[0][...] = ex[0][...] + acc


def _ep_up(acc, ex, out, i):
    out[0][...] = acc.astype(out[0].dtype)
    r = jnp.maximum(acc, 0.0)
    out[1][...] = (r * r).astype(out[1].dtype)


def _ep_down_loss(acc, ex, out, i, inv_d):
    diff = (ex[0][...] + acc) - ex[1][...]
    dx2 = diff * inv_d
    out[0][...] = dx2
    out[1][...] = dx2.astype(out[1].dtype)

    @pl.when(i == 0)
    def _():
        out[2][...] = jnp.zeros_like(out[2])

    out[2][...] += jnp.sum(diff * diff, axis=0, keepdims=True)


def _ep_dh(acc, ex, out, i):
    h = ex[0][...].astype(F32)
    out[0][...] = (acc * (2.0 * jnp.maximum(h, 0.0))).astype(out[0].dtype)


def _ep_rms_bwd(acc, ex, out, i):
    x = ex[0][...]
    g = ex[1][...]
    rstd = lax.rsqrt(jnp.mean(x * x, axis=-1, keepdims=True) + EPS)
    xh = x * rstd
    dxh = acc * g
    dx = ex[2][...] + rstd * (dxh - xh * jnp.mean(dxh * xh, axis=-1, keepdims=True))
    out[0][...] = dx
    out[1][...] = dx.astype(out[1].dtype)

    @pl.when(i == 0)
    def _():
        out[2][...] = jnp.zeros_like(out[2])

    out[2][...] += jnp.sum(acc * xh, axis=0, keepdims=True)


def _ep_gates(acc, ex, out, i):
    sa = _sigmoid(ex[0][...].astype(F32))
    sb = _sigmoid(ex[1][...].astype(F32))
    dpa = acc * sa
    dpb = acc * sb
    out[0][...] = dpa.astype(out[0].dtype)
    out[1][...] = dpb.astype(out[1].dtype)
    out[2][...] = (dpa * ex[2][...].astype(F32) * (1.0 - sa)).astype(out[2].dtype)
    out[3][...] = (dpb * ex[3][...].astype(F32) * (1.0 - sb)).astype(out[3].dtype)


def _sds(shape, dtype):
    return jax.ShapeDtypeStruct(shape, dtype)


def _rms_fwd(name, x, g, tm=512):
    S, D = x.shape

    def body(x_ref, g_ref, o_ref):
        xv = x_ref[...]
        rstd = lax.rsqrt(jnp.mean(xv * xv, axis=-1, keepdims=True) + EPS)
        o_ref[...] = (xv * rstd * g_ref[...]).astype(o_ref.dtype)

    return pl.pallas_call(
        body, name=name, grid=(S // tm,),
        in_specs=[pl.BlockSpec((tm, D), lambda i: (i, 0)), pl.BlockSpec((1, D), lambda i: (0, 0))],
        out_specs=pl.BlockSpec((tm, D), lambda i: (i, 0)),
        out_shape=_sds((S, D), CDT), compiler_params=_params(1))(x, g)


def _rm_shape(S, d, width):
    return (S, width) if d == 1 else (d, S // d, width)


def _rm_spec(tm, d, width):
    if d == 1:
        return pl.BlockSpec((tm, width), lambda i: (i, 0))
    return pl.BlockSpec((d, tm // d, width), lambda i: (0, i, 0))


def _rm_put(dst_ref, cols, buf_ref, d):
    if d == 1:
        dst_ref[:, cols] = buf_ref[...].astype(dst_ref.dtype)
        return
    m = buf_ref.shape[0] // d
    for r in range(d):
        dst_ref[r, :, cols] = buf_ref[pl.ds(r, m, stride=d), :].astype(dst_ref.dtype)


def _rm_reader(buf_ref, src_ref, d):
    if d == 1:
        return lambda s: src_ref[:, s * HD:(s + 1) * HD].astype(F32)
    m = buf_ref.shape[1] // d
    for s in range(buf_ref.shape[0]):
        for r in range(d):
            buf_ref.at[s][pl.ds(r, m, stride=d), :] = src_ref[r, :, s * HD:(s + 1) * HD].astype(F32)
    return lambda s: buf_ref[s]


def _qknorm_fwd(proj, gqk, tm=512):
    S = proj.shape[0]
    W = 2 * ATT_W
    dil = [d for _, d in ATT_GROUPS]

    def body(p_ref, g_ref, o0, o1, o2, buf):
        outs = (o0, o1, o2)
        for hd in range(3 * ATT_HEADS):
            which, head = hd // ATT_HEADS, hd % ATT_HEADS
            grp, slot = head // ATT_HPG, head % ATT_HPG
            v = p_ref[:, hd * HD:(hd + 1) * HD].astype(F32)
            if which < 2:
                rstd = lax.rsqrt(jnp.mean(v * v, axis=-1, keepdims=True) + EPS)
                v = v * rstd * g_ref[:, hd * HD:(hd + 1) * HD]
            buf[...] = v
            _rm_put(outs[grp], slice(which * GW + slot * HD, which * GW + (slot + 1) * HD), buf, dil[grp])

    return pl.pallas_call(
        body, name="qknorm_fwd", grid=(S // tm,),
        in_specs=[pl.BlockSpec((tm, 3 * ATT_W), lambda i: (i, 0)), pl.BlockSpec((1, W), lambda i: (0, 0))],
        out_specs=[_rm_spec(tm, d, 3 * GW) for d in dil],
        out_shape=[_sds(_rm_shape(S, d, 3 * GW), CDT) for d in dil],
        scratch_shapes=[pltpu.VMEM((tm, HD), F32)],
        compiler_params=_params(1))(proj, gqk)


def _qknorm_bwd(proj, gqk, dqs, dks, dvs, tm=256):
    S = proj.shape[0]
    W = 2 * ATT_W
    dil = [d for _, d in ATT_GROUPS]

    def body(p_ref, g_ref, *refs):
        ins = refs[0:9]
        o_ref, dg_ref = refs[9], refs[10]
        bufs = refs[11:20]
        i = pl.program_id(0)

        @pl.when(i == 0)
        def _():
            dg_ref[...] = jnp.zeros_like(dg_ref)

        nat = [_rm_reader(bufs[j], ins[j], dil[j % 3]) for j in range(9)]
        dq_get, dk_get, dv_get = nat[0:3], nat[3:6], nat[6:9]
        for hd in range(2 * ATT_HEADS):
            sl = slice(hd * HD, (hd + 1) * HD)
            head = hd % ATT_HEADS
            grp, slot = head // ATT_HPG, head % ATT_HPG
            dn = (dq_get if hd < ATT_HEADS else dk_get)[grp](slot)
            v = p_ref[:, sl].astype(F32)
            rstd = lax.rsqrt(jnp.mean(v * v, axis=-1, keepdims=True) + EPS)
            vh = v * rstd
            dg_ref[:, sl] += jnp.sum(dn * vh, axis=0, keepdims=True)
            dvh = dn * g_ref[:, sl]
            o_ref[:, sl] = (rstd * (dvh - vh * jnp.mean(dvh * vh, axis=-1, keepdims=True))).astype(o_ref.dtype)
        for head in range(ATT_HEADS):
            grp, slot = head // ATT_HPG, head % ATT_HPG
            o_ref[:, W + head * HD:W + (head + 1) * HD] = dv_get[grp](slot).astype(o_ref.dtype)

    return pl.pallas_call(
        body, name="qknorm_bwd", grid=(S // tm,),
        in_specs=[pl.BlockSpec((tm, W), lambda i: (i, 0)), pl.BlockSpec((1, W), lambda i: (0, 0))]
        + [_rm_spec(tm, d, GW) for d in dil] * 3,
        out_specs=[pl.BlockSpec((tm, 3 * ATT_W), lambda i: (i, 0)), pl.BlockSpec((1, W), lambda i: (0, 0))],
        out_shape=[_sds((S, 3 * ATT_W), CDT), _sds((1, W), F32)],
        scratch_shapes=[pltpu.VMEM((ATT_HPG, tm, HD), F32)] * 9,
        compiler_params=_params(1))(proj, gqk, *dqs, *dks, *dvs)


def _att_mask(n):
    qi = lax.broadcasted_iota(jnp.int32, (BLK, 2 * BLK), 0)
    kj = lax.broadcasted_iota(jnp.int32, (BLK, 2 * BLK), 1)
    dist = BLK + qi - kj
    valid = (dist >= 0) & (dist <= BLK) & ((kj >= BLK) | (n > 0))
    return valid, dist.astype(F32)


def _att_slopes(grp):
    return [2.0 ** (-8.0 * (grp * ATT_HPG + hh + 1) / ATT_HEADS) for hh in range(ATT_HPG)]


def _att_spec(d, row_fn, col=0):
    if d == 1:
        return pl.BlockSpec((BLK, GW), lambda r, n: (row_fn(n), col))
    return pl.BlockSpec((None, BLK, GW), lambda r, n: (r, row_fn(n), col))


def _att_qkv_specs(d, nb):
    last = nb - 1

    def cur(n):
        return jnp.minimum(n, last)

    def prev(n):
        return jnp.maximum(jnp.minimum(n, last) - 1, 0)

    return [_att_spec(d, cur, 0), _att_spec(d, prev, 1), _att_spec(d, cur, 1), _att_spec(d, prev, 2),
            _att_spec(d, cur, 2)]


def _att_fwd(grp, S, qkv):
    _, d = ATT_GROUPS[grp]
    L = S // d
    nb = L // BLK
    slopes = _att_slopes(grp)
    scale = HD ** -0.5

    def body(q_ref, kp_ref, kc_ref, vp_ref, vc_ref, o_ref, l_ref):
        n = pl.program_id(1)
        valid, distf = _att_mask(n)
        for hh in range(ATT_HPG):
            sl = slice(hh * HD, (hh + 1) * HD)
            k = jnp.concatenate([kp_ref[:, sl], kc_ref[:, sl]], axis=0)
            v = jnp.concatenate([vp_ref[:, sl], vc_ref[:, sl]], axis=0)
            s = _dot_nt(q_ref[:, sl], k) * scale + (-slopes[hh] * d) * distf
            s = jnp.where(valid, s, -1e30)
            m = jnp.max(s, axis=-1, keepdims=True)
            p = jnp.exp(s - m)
            den = jnp.sum(p, axis=-1, keepdims=True)
            o_ref[:, sl] = _dot_nn(p.astype(CDT), v) / den
            l_ref[:, sl] = jnp.broadcast_to(m + jnp.log(den), (BLK, HD))

    out_spec = _att_spec(d, lambda n: n)
    return pl.pallas_call(
        body, name="att_fwd_g%d" % grp, grid=(d, nb),
        in_specs=_att_qkv_specs(d, nb),
        out_specs=[out_spec, out_spec],
        out_shape=[_sds(_rm_shape(S, d, GW), F32)] * 2,
        compiler_params=_params(2),
    )(qkv, qkv, qkv, qkv, qkv)


def _att_bwd(grp, S, qkv, lse, do_g, c_g):
    _, d = ATT_GROUPS[grp]
    L = S // d
    nb = L // BLK
    slopes = _att_slopes(grp)
    scale = HD ** -0.5
    last = nb - 1

    def body(q_ref, kp_ref, kc_ref, vp_ref, vc_ref, l_ref, do_ref, c_ref, dq_ref, dk_ref, dv_ref, ck, cv):
        n = pl.program_id(1)

        @pl.when(n == 0)
        def _():
            ck[...] = jnp.zeros_like(ck)
            cv[...] = jnp.zeros_like(cv)

        @pl.when(n < nb)
        def _():
            valid, distf = _att_mask(n)
            for hh in range(ATT_HPG):
                sl = slice(hh * HD, (hh + 1) * HD)
                q = q_ref[:, sl]
                k = jnp.concatenate([kp_ref[:, sl], kc_ref[:, sl]], axis=0)
                v = jnp.concatenate([vp_ref[:, sl], vc_ref[:, sl]], axis=0)
                do = do_ref[:, sl]
                s = _dot_nt(q, k) * scale + (-slopes[hh] * d) * distf
                p = jnp.where(valid, jnp.exp(s - l_ref[:, sl][:, 0:1]), 0.0)
                dp = _dot_nt(do, v)
                ds = (p * (dp + c_ref[:, sl][:, 0:1]) * scale).astype(CDT)
                dq_ref[:, sl] = _dot_nn(ds, k)
                dk = _dot_tn(ds, q)
                dv = _dot_tn(p.astype(CDT), do)
                dk_ref[:, sl] = ck[:, sl] + dk[0:BLK]
                dv_ref[:, sl] = cv[:, sl] + dv[0:BLK]
                ck[:, sl] = dk[BLK:2 * BLK]
                cv[:, sl] = dv[BLK:2 * BLK]

        @pl.when(n == nb)
        def _():
            dk_ref[...] = ck[...]
            dv_ref[...] = cv[...]

    blk = (BLK, GW)
    at_q = _att_spec(d, lambda n: jnp.minimum(n, last))
    behind = _att_spec(d, lambda n: jnp.maximum(n - 1, 0))
    return pl.pallas_call(
        body, name="att_bwd_g%d" % grp, grid=(d, nb + 1),
        in_specs=_att_qkv_specs(d, nb) + [at_q, at_q, at_q],
        out_specs=[at_q, behind, behind],
        out_shape=[_sds(_rm_shape(S, d, GW), F32)] * 3,
        scratch_shapes=[pltpu.VMEM(blk, F32), pltpu.VMEM(blk, F32)],
        compiler_params=_params(2),
    )(qkv, qkv, qkv, qkv, qkv, lse, do_g, c_g)


def _mix_alpha(l0, l1, l2):
    mx = jnp.maximum(jnp.maximum(l0, l1), l2)
    e = [jnp.exp(l0 - mx), jnp.exp(l1 - mx), jnp.exp(l2 - mx)]
    tot = e[0] + e[1] + e[2]
    return [ei / tot for ei in e]


def _mix_fwd(S, os_, ls_, tm=512):
    dil = [d for _, d in ATT_GROUPS]

    def body(*refs):
        out, bufs = refs[6], refs[7:13]
        get = [_rm_reader(bufs[j], refs[j], dil[j % 3]) for j in range(6)]
        for s in range(ATT_HPG):
            al = _mix_alpha(*[get[3 + g](s) for g in range(3)])
            mixed = al[0] * get[0](s) + al[1] * get[1](s) + al[2] * get[2](s)
            out[:, s * HD:(s + 1) * HD] = mixed.astype(out.dtype)

    specs = [_rm_spec(tm, d, GW) for d in dil]
    return pl.pallas_call(
        body, name="mix_fwd", grid=(S // tm,), in_specs=specs * 2, out_specs=pl.BlockSpec((tm, GW), lambda i: (i, 0)),
        out_shape=_sds((S, GW), CDT), scratch_shapes=[pltpu.VMEM((ATT_HPG, tm, HD), F32)] * 6,
        compiler_params=_params(1))(*os_, *ls_)


def _mix_bwd(S, os_, ls_, do_a, tm=512):
    dil = [d for _, d in ATT_GROUPS]

    def body(*refs):
        d_ref, outs, bufs, tmp = refs[6], refs[7:13], refs[13:19], refs[19]
        get = [_rm_reader(bufs[j], refs[j], dil[j % 3]) for j in range(6)]
        for s in range(ATT_HPG):
            cols = slice(s * HD, (s + 1) * HD)
            al = _mix_alpha(*[get[3 + g](s) for g in range(3)])
            dv = d_ref[:, cols]
            o_a = al[0] * get[0](s) + al[1] * get[1](s) + al[2] * get[2](s)
            dsum = jnp.sum(dv * o_a, axis=-1, keepdims=True)
            for g in range(3):
                tmp[...] = al[g] * dv
                _rm_put(outs[g], cols, tmp, dil[g])
                tmp[...] = -(al[g] * dsum)
                _rm_put(outs[3 + g], cols, tmp, dil[g])

    specs = [_rm_spec(tm, d, GW) for d in dil]
    res = pl.pallas_call(
        body, name="mix_bwd", grid=(S // tm,), in_specs=specs * 2 + [pl.BlockSpec((tm, GW), lambda i: (i, 0))],
        out_specs=specs * 2,
        out_shape=[_sds(_rm_shape(S, d, GW), CDT) for d in dil] + [_sds(_rm_shape(S, d, GW), F32) for d in dil],
        scratch_shapes=[pltpu.VMEM((ATT_HPG, tm, HD), F32)] * 6 + [pltpu.VMEM((tm, HD), F32)],
        compiler_params=_params(1))(*os_, *ls_, do_a)
    return res[:3], res[3:]


def _ret_tables(dk):
    H, C = RET_HEADS, BLK
    log_g = jnp.log(1.0 - 2.0 ** (-5.0 - jnp.arange(H, dtype=F32)))
    idx = jnp.arange(C, dtype=F32)
    diff = idx[:, None] - idx[None, :]
    decay = jnp.where(diff >= 0, jnp.exp(log_g[:, None, None] * jnp.maximum(diff, 0.0)), 0.0)
    xi = jnp.exp(log_g[:, None] * (idx[None, :] + 1.0))
    zeta = jnp.exp(log_g[:, None] * (C - 1.0 - idx[None, :])) * (dk ** -0.5)
    g_chunk = jnp.exp(log_g * C)
    bc = lambda t: jnp.broadcast_to(t[:, :, None], (H, C, C))
    return decay, bc(xi), bc(zeta), jnp.broadcast_to(g_chunk[:, None, None], (H, 8, C))


def _gn_fwd(o, g, b):
    mu = jnp.mean(o, axis=-1, keepdims=True)
    xc = o - mu
    rstd = lax.rsqrt(jnp.mean(xc * xc, axis=-1, keepdims=True) + EPS)
    yh = xc * rstd
    return yh, rstd, yh * g + b


def _ret_specs(dk, dv, order):
    H = RET_HEADS
    qk_w, v_w = H * dk, H * dv
    off_q = 3 * ATT_W
    off_k, off_v, off_g = off_q + qk_w, off_q + 2 * qk_w, off_q + 2 * qk_w + v_w
    assert 2 * dk == dv and all(off % dv == 0 for off in (off_q, off_k, off_v, off_g))

    def col(off, j):
        return pl.BlockSpec((BLK, dv), lambda i: (order(i), off // dv + j))

    tab = pl.BlockSpec((H, BLK, BLK), lambda i: (0, 0, 0))
    return ([col(off_q, j) for j in range(H // 2)] + [col(off_k, j) for j in range(H // 2)]
            + [col(off_v, j) for j in range(H)] + [col(off_g, j) for j in range(H)]
            + [tab, tab, tab, pl.BlockSpec((H, 8, BLK), lambda i: (0, 0, 0))])


def _ret_heads(refs, dk):
    H = RET_HEADS
    q_refs, k_refs = refs[0:H // 2], refs[H // 2:H]
    v_refs, gr_refs = refs[H:2 * H], refs[2 * H:3 * H]

    def head(h):
        cols = slice((h % 2) * dk, (h % 2 + 1) * dk)
        return q_refs[h // 2][:, cols], k_refs[h // 2][:, cols], v_refs[h][...], gr_refs[h][...]

    return head, refs[3 * H:3 * H + 4]


def _ret_fwd(proj, gn_g, gn_b, dk, dv):
    S = proj.shape[0]
    N = S // BLK
    H = RET_HEADS
    kscale = dk ** -0.5
    n_in = 3 * H + 4

    def body(*refs):
        head, (dec_ref, xi_ref, zeta_ref, gc_ref) = _ret_heads(refs, dk)
        g_ref, b_ref, opre_ref, or_ref, st_ref, state = refs[n_in:n_in + 6]
        n = pl.program_id(0)

        @pl.when(n == 0)
        def _():
            state[...] = jnp.zeros_like(state)

        for h in range(H):
            vs = slice(h * dv, (h + 1) * dv)
            q, k, v, gr = head(h)
            s = _dot_nt(q, k) * kscale * dec_ref[h]
            st = state[h]
            st_c = st.astype(CDT)
            st_ref[h] = st_c
            o = _dot_nn(s.astype(CDT), v) + _dot_nn(q, st_c) * xi_ref[h][:, 0:1]
            kz = (k.astype(F32) * zeta_ref[h][:, 0:1]).astype(CDT)
            state[h] = st * gc_ref[h][0:1, 0:1] + _dot_tn(kz, v)
            opre_ref[:, vs] = o
            _, _, y = _gn_fwd(o, g_ref[:, vs], b_ref[:, vs])
            gr = gr.astype(F32)
            or_ref[:, vs] = (y * (gr * _sigmoid(gr))).astype(or_ref.dtype)

    v_w = H * dv
    row = pl.BlockSpec((1, v_w), lambda i: (0, 0))
    tile = pl.BlockSpec((BLK, v_w), lambda i: (i, 0))
    return pl.pallas_call(
        body, name="ret_fwd", grid=(N,),
        in_specs=_ret_specs(dk, dv, lambda i: i) + [row, row],
        out_specs=[tile, tile, pl.BlockSpec((None, H, dk, dv), lambda i: (i, 0, 0, 0))],
        out_shape=[_sds((S, v_w), F32), _sds((S, v_w), CDT), _sds((N, H, dk, dv), CDT)],
        scratch_shapes=[pltpu.VMEM((H, dk, dv), F32)],
        compiler_params=_params(1),
    )(*[proj] * (3 * H), *_ret_tables(dk), gn_g, gn_b)


def _ret_bwd(proj, gn_g, gn_b, o_pre, states, d_or, dk, dv):
    S = proj.shape[0]
    N = S // BLK
    H = RET_HEADS
    qk_w, v_w = H * dk, H * dv
    kscale = dk ** -0.5
    n_in = 3 * H + 4

    def body(*refs):
        head, (dec_ref, xi_ref, zeta_ref, gc_ref) = _ret_heads(refs, dk)
        g_ref, b_ref, opre_ref, st_ref, dor_ref, out_ref, dg_ref, db_ref, dstate = refs[n_in:n_in + 9]
        i = pl.program_id(0)

        @pl.when(i == 0)
        def _():
            dstate[...] = jnp.zeros_like(dstate)
            dg_ref[...] = jnp.zeros_like(dg_ref)
            db_ref[...] = jnp.zeros_like(db_ref)

        for h in range(H):
            vs = slice(h * dv, (h + 1) * dv)
            q, k, v, gr = head(h)
            decay, xi, zeta_s = dec_ref[h], xi_ref[h][:, 0:1], zeta_ref[h][:, 0:1]
            gr = gr.astype(F32)
            sg = _sigmoid(gr)
            gain = g_ref[:, vs]
            yh, rstd, y = _gn_fwd(opre_ref[:, vs], gain, b_ref[:, vs])
            d_or_v = dor_ref[:, vs]
            dy = d_or_v * (gr * sg)
            out_ref[:, 2 * qk_w + v_w + h * dv:2 * qk_w + v_w + (h + 1) * dv] = (
                d_or_v * y * (sg * (1.0 + gr * (1.0 - sg)))).astype(out_ref.dtype)
            dg_ref[:, vs] += jnp.sum(dy * yh, axis=0, keepdims=True)
            db_ref[:, vs] += jnp.sum(dy, axis=0, keepdims=True)
            dyh = dy * gain
            do = rstd * (dyh - jnp.mean(dyh, axis=-1, keepdims=True)
                         - yh * jnp.mean(dyh * yh, axis=-1, keepdims=True))
            do_c = do.astype(CDT)
            dox = (do * xi).astype(CDT)
            a_c = (_dot_nt(q, k) * kscale * decay).astype(CDT)
            g_c = (_dot_nt(do_c, v) * decay).astype(CDT)
            dsn = dstate[h]
            dsn_c = dsn.astype(CDT)
            kz = (k.astype(F32) * zeta_s).astype(CDT)
            dq = _dot_nn(g_c, k) * kscale + _dot_nt(dox, st_ref[h])
            dkk = _dot_tn(g_c, q) * kscale + _dot_nt(v, dsn_c) * zeta_s
            dvv = _dot_tn(a_c, do_c) + _dot_nn(kz, dsn_c)
            dstate[h] = dsn * gc_ref[h][0:1, 0:1] + _dot_tn(q, dox)
            out_ref[:, h * dk:(h + 1) * dk] = dq.astype(out_ref.dtype)
            out_ref[:, qk_w + h * dk:qk_w + (h + 1) * dk] = dkk.astype(out_ref.dtype)
            out_ref[:, 2 * qk_w + h * dv:2 * qk_w + (h + 1) * dv] = dvv.astype(out_ref.dtype)

    rev = lambda i: N - 1 - i
    row = pl.BlockSpec((1, v_w), lambda i: (0, 0))
    tile = pl.BlockSpec((BLK, v_w), lambda i: (rev(i), 0))
    out_w = 2 * qk_w + 2 * v_w
    return pl.pallas_call(
        body, name="ret_bwd", grid=(N,),
        in_specs=_ret_specs(dk, dv, rev) + [row, row, tile,
                 pl.BlockSpec((None, H, dk, dv), lambda i: (rev(i), 0, 0, 0)), tile],
        out_specs=[pl.BlockSpec((BLK, out_w), lambda i: (rev(i), 0)), row, row],
        out_shape=[_sds((S, out_w), CDT), _sds((1, v_w), F32), _sds((1, v_w), F32)],
        scratch_shapes=[pltpu.VMEM((H, dk, dv), F32)],
        compiler_params=_params(1),
    )(*[proj] * (3 * H), *_ret_tables(dk), gn_g, gn_b, o_pre, states, d_or)


def _merge_fwd(o_a, o_r, wa, wb, proj, d_model, tm=512, tn=512):
    S, in_w = proj.shape
    off_a, off_b = in_w - 2 * d_model, in_w - d_model
    assert off_a % tn == 0 and off_b % tn == 0

    def body(oa_ref, or_ref, wa_ref, wb_ref, ga_ref, gb_ref, y_ref, pa_ref, pb_ref):
        pa = _dot_nn(oa_ref[...], wa_ref[...])
        pb = _dot_nn(or_ref[...], wb_ref[...])
        y = _sigmoid(ga_ref[...].astype(F32)) * pa + _sigmoid(gb_ref[...].astype(F32)) * pb
        y_ref[...] = y.astype(y_ref.dtype)
        pa_ref[...] = pa.astype(pa_ref.dtype)
        pb_ref[...] = pb.astype(pb_ref.dtype)

    ka, kb = o_a.shape[1], o_r.shape[1]
    out = pl.BlockSpec((tm, tn), lambda i, j: (i, j))
    return pl.pallas_call(
        body, name="merge_fwd", grid=(S // tm, d_model // tn),
        in_specs=[pl.BlockSpec((tm, ka), lambda i, j: (i, 0)), pl.BlockSpec((tm, kb), lambda i, j: (i, 0)),
                  pl.BlockSpec((ka, tn), lambda i, j: (0, j)), pl.BlockSpec((kb, tn), lambda i, j: (0, j)),
                  pl.BlockSpec((tm, tn), lambda i, j: (i, off_a // tn + j)),
                  pl.BlockSpec((tm, tn), lambda i, j: (i, off_b // tn + j))],
        out_specs=[out, out, out], out_shape=[_sds((S, d_model), CDT)] * 3,
        compiler_params=_params(2))(o_a, o_r, wa, wb, proj, proj)


def _local_step(x, target, w_in, small, late_weights, on_grads, after_w_in=None):
    S, D = x.shape
    in_w = w_in.shape[1]
    d_ff = 4 * D
    ret_v_w = 2 * D
    dv = ret_v_w // RET_HEADS
    dk = (in_w - 3 * ATT_W - 2 * ret_v_w - 2 * D) // (2 * RET_HEADS)
    gqk = jnp.concatenate([small["q_norm_g"].reshape(1, ATT_W), small["k_norm_g"].reshape(1, ATT_W)], axis=1)
    g1, g2 = small["norm1_g"], small["norm2_g"]
    gn_g, gn_b = small["ret_gn_g"], small["ret_gn_b"]

    xn = _rms_fwd("rms1_fwd", x, g1)
    (proj,) = _matmul("in_proj", "nn", xn, w_in, tm=1024, tn=640, tk=D,
                      outs=[(_sds((S, in_w), CDT), _mn(1024, 640))], epilogue=_ep_store, deps=[after_w_in])
    qkv = _qknorm_fwd(proj, gqk)
    att = [_att_fwd(g, S, qkv[g]) for g in range(3)]
    os_, ls_ = [a[0] for a in att], [a[1] for a in att]
    o_a = _mix_fwd(S, os_, ls_)
    o_pre, o_r, states = _ret_fwd(proj, gn_g, gn_b, dk, dv)
    w = late_weights(o_r)
    y, pa, pb = _merge_fwd(o_a, o_r, w["w_proj_a"], w["w_proj_b"], proj, D)
    (x1,) = _matmul("out_proj", "nn", y, w["w_out"], tm=512, tn=D, tk=D,
                    extras=[(x, _mn(512, D))], outs=[(_sds((S, D), F32), _mn(512, D))], epilogue=_ep_resid)
    xn2 = _rms_fwd("rms2_fwd", x1, g2)
    hid, act = _matmul("mlp_up", "nn", xn2, w["w_up"], tm=1024, tn=512, tk=D,
                       outs=[(_sds((S, d_ff), CDT), _mn(1024, 512))] * 2, epilogue=_ep_up)
    dx2, dx2c, loss_row = _matmul(
        "mlp_down_loss", "nn", act, w["w_down"], tm=512, tn=D, tk=2048,
        extras=[(x1, _mn(512, D)), (target, _mn(512, D))],
        outs=[(_sds((S, D), F32), _mn(512, D)), (_sds((S, D), CDT), _mn(512, D)), (_sds((1, D), F32), _row(D))],
        epilogue=functools.partial(_ep_down_loss, inv_d=1.0 / D))
    loss = 0.5 * jnp.sum(loss_row) / D

    (dh,) = _matmul("d_hidden", "nt", dx2c, w["w_down"], tm=1024, tn=512, tk=D,
                    extras=[(hid, _mn(1024, 512))], outs=[(_sds((S, d_ff), CDT), _mn(1024, 512))], epilogue=_ep_dh)
    (gw_down,) = _matmul("dw_down", "tn", act, dx2c, tm=1024, tn=D, tk=1024,
                         outs=[(_sds((d_ff, D), F32), _mn(1024, D))], epilogue=_ep_store)
    (gw_up,) = _matmul("dw_up", "tn", xn2, dh, tm=D, tn=1024, tk=1024,
                       outs=[(_sds((D, d_ff), F32), _mn(D, 1024))], epilogue=_ep_store)
    tok = on_grads({"w_down": gw_down, "w_up": gw_up})
    dx1, dx1c, dg2 = _matmul(
        "d_x1", "nt", dh, w["w_up"], tm=512, tn=D, tk=2048,
        extras=[(x1, _mn(512, D)), (g2, _row(D)), (dx2, _mn(512, D))],
        outs=[(_sds((S, D), F32), _mn(512, D)), (_sds((S, D), CDT), _mn(512, D)), (_sds((1, D), F32), _row(D))],
        epilogue=_ep_rms_bwd, deps=[tok])

    gt = 512
    assert (in_w - 2 * D) % gt == 0
    off_a, off_b = (in_w - 2 * D) // gt, (in_w - D) // gt
    dpa, dpb, dga, dgb = _matmul(
        "d_gates", "nt", dx1c, w["w_out"], tm=512, tn=gt, tk=D,
        extras=[(proj, _mn(512, gt, off_a)), (proj, _mn(512, gt, off_b)), (pa, _mn(512, gt)), (pb, _mn(512, gt))],
        outs=[(_sds((S, D), CDT), _mn(512, gt))] * 4, epilogue=_ep_gates)
    (gw_out,) = _matmul("dw_out", "tn", y, dx1c, tm=D, tn=D, tk=1024,
                        outs=[(_sds((D, D), F32), _mn(D, D))], epilogue=_ep_store)
    (gw_pa,) = _matmul("dw_proj_a", "tn", o_a, dpa, tm=GW, tn=D, tk=1024,
                       outs=[(_sds((GW, D), F32), _mn(GW, D))], epilogue=_ep_store)
    (gw_pb,) = _matmul("dw_proj_b", "tn", o_r, dpb, tm=1024, tn=D, tk=1024,
                       outs=[(_sds((ret_v_w, D), F32), _mn(1024, D))], epilogue=_ep_store)
    (do_a,) = _matmul("d_o_a", "nt", dpa, w["w_proj_a"], tm=1024, tn=GW, tk=D,
                      outs=[(_sds((S, GW), F32), _mn(1024, GW))], epilogue=_ep_store)
    tok = on_grads({"w_out": gw_out, "w_proj_a": gw_pa, "w_proj_b": gw_pb})
    (d_or,) = _matmul("d_o_r", "nt", dpb, w["w_proj_b"], tm=1024, tn=512, tk=D,
                      outs=[(_sds((S, ret_v_w), F32), _mn(1024, 512))], epilogue=_ep_store, deps=[tok])

    d_ret, dgn_g, dgn_b = _ret_bwd(proj, gn_g, gn_b, o_pre, states, d_or, dk, dv)
    do_gs, c_gs = _mix_bwd(S, os_, ls_, do_a)
    datt_parts = [_att_bwd(g, S, qkv[g], ls_[g], do_gs[g], c_gs[g]) for g in range(3)]
    d_att, dgqk = _qknorm_bwd(proj, gqk, [p[0] for p in datt_parts], [p[1] for p in datt_parts],
                              [p[2] for p in datt_parts])
    dproj = jnp.concatenate([d_att, d_ret, dga, dgb], axis=1)

    (gw_in,) = _matmul("dw_in", "tn", xn, dproj, tm=D, tn=640, tk=1024,
                       outs=[(_sds((D, in_w), F32), _mn(D, 640))], epilogue=_ep_store)
    tok = on_grads({"w_in": gw_in})
    grad_x, _, dg1 = _matmul(
        "d_x", "nt", dproj, w_in, tm=512, tn=D, tk=1280,
        extras=[(x, _mn(512, D)), (g1, _row(D)), (dx1, _mn(512, D))],
        outs=[(_sds((S, D), F32), _mn(512, D)), (_sds((S, D), CDT), _mn(512, D)), (_sds((1, D), F32), _row(D))],
        epilogue=_ep_rms_bwd, deps=[tok])

    smallg = {"norm1_g": dg1, "q_norm_g": dgqk[:, :ATT_W], "k_norm_g": dgqk[:, ATT_W:],
              "ret_gn_g": dgn_g, "ret_gn_b": dgn_b, "norm2_g": dg2}
    return loss, grad_x, smallg


N_CHIPS = 4
N_DEV = 8


def _place():
    x, y, c = lax.axis_index("x"), lax.axis_index("y"), lax.axis_index("c")
    return x, y, c


def _other_chips(x, y):
    out = []
    for fx, fy in ((1, 0), (0, 1), (1, 1)):
        px = 1 - x if fx else x
        py = 1 - y if fy else y
        out.append(((px, py), 2 * px + py))
    return out


def _gather_weights(shards):
    n = len(shards)

    def body(*refs):
        ins, outs = refs[:n], refs[n:2 * n]
        send, recv = refs[2 * n:]
        x, y, c = _place()
        me = 2 * x + y
        peers = _other_chips(x, y)

        def rows(w):
            half = ins[w].shape[0] // 2
            return pl.ds(c * half, half)

        sends = []
        for w in range(n):
            for j, ((px, py), _) in enumerate(peers):
                cp = pltpu.make_async_remote_copy(
                    src_ref=ins[w].at[rows(w), :], dst_ref=outs[w].at[me, rows(w), :], send_sem=send.at[w * 3 + j],
                    recv_sem=recv.at[w * 3 + j], device_id=(px, py, c), device_id_type=MESH)
                cp.start()
                sends.append(cp)
        for w in range(n):
            for j, ((px, py), pidx) in enumerate(peers):
                pltpu.make_async_remote_copy(
                    src_ref=ins[w].at[rows(w), :], dst_ref=outs[w].at[pidx, rows(w), :], send_sem=send.at[w * 3 + j],
                    recv_sem=recv.at[w * 3 + j], device_id=(px, py, c), device_id_type=MESH).wait_recv()
        for cp in sends:
            cp.wait_send()

    return pl.pallas_call(
        body, name="gather_weights",
        in_specs=[HBM_SPEC] * n, out_specs=[HBM_SPEC] * n,
        out_shape=[_sds((N_CHIPS,) + s.shape, s.dtype) for s in shards],
        scratch_shapes=[pltpu.SemaphoreType.DMA((3 * n,)), pltpu.SemaphoreType.DMA((3 * n,))],
    )(*shards)


SEM_SPEC = pl.BlockSpec(memory_space=pltpu.SEMAPHORE)
ANY_SPEC = pl.BlockSpec(memory_space=pl.ANY)
EFFECT = pltpu.SideEffectType.DATAFLOW_SIDE_EFFECTING


def _ici_copies(kind, srcs, lands, send, recv):
    x, y, c = _place()
    me = 2 * x + y
    out = []
    for w, (s, l) in enumerate(zip(srcs, lands)):
        for j, ((px, py), pidx) in enumerate(_other_chips(x, y)):
            if kind == "gather":
                half = s.shape[0] // 2
                rows = pl.ds(c * half, half)
                src, dst_there, dst_here = s.at[rows, :], l.at[me, rows, :], l.at[pidx, rows, :]
            else:
                src, dst_there, dst_here = s.at[pidx], l.at[me], l.at[pidx]
            out.append((src, dst_there, dst_here, send.at[3 * w + j], recv.at[3 * w + j], (px, py, c)))
    return out


def _exchange_start(name, kind, srcs, land_shapes):
    n = len(srcs)

    def body(*refs):
        src_refs, land_refs = refs[:n], refs[n:2 * n]
        send, recv = refs[2 * n], refs[2 * n + 1]
        token = refs[-1]
        for src, dst, _, ss, rs, dev in _ici_copies(kind, src_refs, land_refs, send, recv):
            pltpu.make_async_remote_copy(src_ref=src, dst_ref=dst, send_sem=ss, recv_sem=rs, device_id=dev,
                                         device_id_type=MESH).start()
        token[...] = jnp.zeros_like(token)

    thru = [pltpu.HBM(s.shape, s.dtype) for s in srcs] + [pltpu.HBM(shape, dtype) for shape, dtype in land_shapes]
    res = pl.pallas_call(
        body, name=name,
        out_shape=(pltpu.SemaphoreType.DMA((3 * n,)), pltpu.SemaphoreType.DMA((3 * n,)), *thru, _sds((8, LANES), F32)),
        in_specs=[HBM_SPEC] * (2 * n), out_specs=(SEM_SPEC, SEM_SPEC, *[HBM_SPEC] * (2 * n), VMEM_SPEC),
        input_output_aliases={i: 2 + i for i in range(2 * n)},
        compiler_params=pltpu.CompilerParams(has_side_effects=EFFECT),
    )(*[pltpu.with_memory_space_constraint(s, pltpu.HBM) for s in srcs],
      *[pltpu.with_memory_space_constraint(lax.empty(shape, dtype), pltpu.HBM) for shape, dtype in land_shapes])
    return res[0], res[1], list(res[2:2 + n]), list(res[2 + n:2 + 2 * n]), res[-1]


def _exchange_wait(name, kind, send, recv, srcs, lands, after):
    n = len(srcs)

    def body(*refs):
        src_refs, land_refs = refs[:n], refs[n:2 * n]
        send_ref, recv_ref = refs[2 * n], refs[2 * n + 1]
        for src, _, dst, ss, rs, dev in _ici_copies(kind, src_refs, land_refs, send_ref, recv_ref):
            cp = pltpu.make_async_remote_copy(src_ref=src, dst_ref=dst, send_sem=ss, recv_sem=rs, device_id=dev,
                                              device_id_type=MESH)
            cp.wait_send()
            cp.wait_recv()

    thru = [pltpu.HBM(t.shape, t.dtype) for t in list(srcs) + list(lands)]
    res = pl.pallas_call(
        body, name=name, out_shape=thru,
        in_specs=[HBM_SPEC] * (2 * n) + [SEM_SPEC, SEM_SPEC, ANY_SPEC], out_specs=[HBM_SPEC] * (2 * n),
        input_output_aliases={i: i for i in range(2 * n)},
        compiler_params=pltpu.CompilerParams(has_side_effects=EFFECT),
    )(*srcs, *lands, send, recv, after)
    return list(res[:n]), list(res[n:])


PAIR_TILE_ELEMS = 1 << 19


def _pair_fill(name, gathered, mine, core, others, chip):
    k, r, C = gathered.shape
    half = r // 2
    tr = _row_tile(half, C, PAIR_TILE_ELEMS, mult=16)
    nt = half // tr
    n_far = N_CHIPS - 1

    def body(c_ref, o_ref, chip_ref, in_ref, mine_ref, out_ref, slot, send, recv):
        j = pl.program_id(0)
        b = (j * nt + pl.program_id(1)) % 2
        x, y, c = _place()
        cp = pltpu.make_async_remote_copy(src_ref=in_ref, dst_ref=slot.at[b], send_sem=send.at[b],
                                          recv_sem=recv.at[b], device_id=(x, y, 1 - c), device_id_type=MESH)

        @pl.when(j < n_far)
        def _():
            cp.start()
            cp.wait_recv()
            out_ref[...] = slot[b]
            cp.wait_send()

        @pl.when(j >= n_far)
        def _():
            out_ref[...] = mine_ref[...]

    def far(j):
        return jnp.minimum(j, n_far - 1)

    grid_spec = pltpu.PrefetchScalarGridSpec(
        num_scalar_prefetch=3, grid=(n_far + 2, nt),
        in_specs=[pl.BlockSpec((tr, C), lambda j, i, c, o, m: (
                      (2 * o[far(j)] + c[0]) * nt + jnp.where(j < n_far, i, nt - 1), 0)),
                  pl.BlockSpec((tr, C), lambda j, i, c, o, m: (jnp.where(j < n_far, 0, (j - n_far) * nt + i), 0))],
        out_specs=pl.BlockSpec((tr, C), lambda j, i, c, o, m: (
            jnp.where(j < n_far, 2 * o[far(j)] + 1 - c[0], 2 * m[0] + j - n_far) * nt + i, 0)),
        scratch_shapes=[pltpu.VMEM((2, tr, C), gathered.dtype), pltpu.SemaphoreType.DMA((2,)),
                        pltpu.SemaphoreType.DMA((2,))])
    out = pl.pallas_call(body, name=name, grid_spec=grid_spec, out_shape=_sds((k * r, C), gathered.dtype),
                         input_output_aliases={3: 0}, compiler_params=_params(2))(
                             core, others, chip, gathered.reshape(k * r, C), mine)
    return out.reshape(k, r, C)


def _pair_reduce(name, g, core):
    k, R, C = g.shape
    half = R // 2
    tr = _row_tile(half, C, PAIR_TILE_ELEMS)
    nt = half // tr

    def body(c_ref, mine_ref, give_ref, out_ref, wire_ref, slot, send, recv):
        b = (pl.program_id(0) * nt + pl.program_id(1)) % 2
        x, y, c = _place()
        cp = pltpu.make_async_remote_copy(src_ref=give_ref, dst_ref=slot.at[b], send_sem=send.at[b],
                                          recv_sem=recv.at[b], device_id=(x, y, 1 - c), device_id_type=MESH)
        cp.start()
        cp.wait_recv()
        tot = mine_ref[...] + slot[b]
        out_ref[...] = tot
        wire_ref[...] = tot.astype(wire_ref.dtype)
        cp.wait_send()

    blk = (tr, C)
    out_spec = pl.BlockSpec(blk, lambda s, i, c: (s * nt + i, 0))
    grid_spec = pltpu.PrefetchScalarGridSpec(
        num_scalar_prefetch=1, grid=(k, nt),
        in_specs=[pl.BlockSpec(blk, lambda s, i, c: ((2 * s + c[0]) * nt + i, 0)),
                  pl.BlockSpec(blk, lambda s, i, c: ((2 * s + 1 - c[0]) * nt + i, 0))],
        out_specs=[out_spec, out_spec],
        scratch_shapes=[pltpu.VMEM((2, tr, C), F32), pltpu.SemaphoreType.DMA((2,)), pltpu.SemaphoreType.DMA((2,))])
    g2 = g.reshape(k * R, C)
    out, wire = pl.pallas_call(body, name=name, grid_spec=grid_spec,
                               out_shape=[_sds((k * half, C), F32), _sds((k * half, C), CDT)],
                               compiler_params=_params(2))(core, g2, g2)
    return out, wire.reshape(k, half, C)


def _pair_share(name, f, core):
    half, C = f.shape
    tr = _row_tile(half, C, PAIR_TILE_ELEMS)
    nt = half // tr

    def body(c_ref, f_ref, out_ref, slot, send, recv):
        p = pl.program_id(1)
        b = pl.program_id(0) % 2
        x, y, c = _place()
        cp = pltpu.make_async_remote_copy(src_ref=f_ref, dst_ref=slot.at[b], send_sem=send.at[b],
                                          recv_sem=recv.at[b], device_id=(x, y, 1 - c), device_id_type=MESH)

        @pl.when(p == 0)
        def _():
            cp.start()
            out_ref[...] = f_ref[...]

        @pl.when(p == 1)
        def _():
            cp.wait_recv()
            out_ref[...] = slot[b]
            cp.wait_send()

    grid_spec = pltpu.PrefetchScalarGridSpec(
        num_scalar_prefetch=1, grid=(nt, 2),
        in_specs=[pl.BlockSpec((tr, C), lambda i, p, c: (i, 0))],
        out_specs=pl.BlockSpec((tr, C), lambda i, p, c: (jnp.where(p == 0, c[0], 1 - c[0]) * nt + i, 0)),
        scratch_shapes=[pltpu.VMEM((2, tr, C), F32), pltpu.SemaphoreType.DMA((2,)), pltpu.SemaphoreType.DMA((2,))])
    return pl.pallas_call(body, name=name, grid_spec=grid_spec, out_shape=_sds((2 * half, C), F32),
                          compiler_params=_params(2))(core, f)


def _all_reduce_small(v):
    r, cdim = v.shape

    def body(v_ref, o_ref, buf, send, recv):
        x, y, c = _place()
        me = 4 * x + 2 * y + c
        buf[me] = v_ref[...]
        sends = []
        for m in range(1, N_DEV):
            px = 1 - x if m & 4 else x
            py = 1 - y if m & 2 else y
            pc = 1 - c if m & 1 else c
            cp = pltpu.make_async_remote_copy(src_ref=v_ref, dst_ref=buf.at[me], send_sem=send.at[m - 1],
                                              recv_sem=recv.at[m - 1], device_id=(px, py, pc), device_id_type=MESH)
            cp.start()
            sends.append((cp, 4 * px + 2 * py + pc))
        for m, (cp, pidx) in enumerate(sends):
            pltpu.make_async_remote_copy(src_ref=v_ref, dst_ref=buf.at[pidx], send_sem=send.at[m], recv_sem=recv.at[m],
                                         device_id=(x, y, c), device_id_type=MESH).wait_recv()
        for cp, _ in sends:
            cp.wait_send()
        tot = buf[0]
        for k in range(1, N_DEV):
            tot = tot + buf[k]
        o_ref[...] = tot

    return pl.pallas_call(
        body, name="all_reduce_small", in_specs=[VMEM_SPEC], out_specs=VMEM_SPEC,
        out_shape=_sds((r, cdim), F32),
        scratch_shapes=[pltpu.VMEM((N_DEV, r, cdim), F32), pltpu.SemaphoreType.DMA((N_DEV - 1,)),
                        pltpu.SemaphoreType.DMA((N_DEV - 1,))],
    )(v)


def _row_tile(rows, cols, budget_elems=1 << 18, mult=8):
    if rows % mult:
        return rows
    t = max(mult, (budget_elems // cols) // mult * mult)
    while rows % t:
        t -= mult
    return t


def _sum4(name, own, by_chip, chip, others):
    k, R, C = by_chip.shape
    tr = _row_tile(R, C, mult=16)
    nt = R // tr

    def body(chip_ref, others_ref, own_ref, a_ref, b_ref, c_ref, o_ref):
        o_ref[...] = ((own_ref[...] + a_ref[...].astype(F32)) + b_ref[...].astype(F32)) + c_ref[...].astype(F32)

    def from_chip(j):
        return pl.BlockSpec((tr, C), lambda i, chip, oth: (oth[j] * nt + i, 0))

    grid_spec = pltpu.PrefetchScalarGridSpec(
        num_scalar_prefetch=2, grid=(nt,),
        in_specs=[pl.BlockSpec((tr, C), lambda i, chip, oth: (chip[0] * nt + i, 0)),
                  from_chip(0), from_chip(1), from_chip(2)],
        out_specs=pl.BlockSpec((tr, C), lambda i, chip, oth: (i, 0)))
    by2 = by_chip.reshape(k * R, C)
    return pl.pallas_call(body, name=name, grid_spec=grid_spec, out_shape=_sds((R, C), F32),
                          compiler_params=_params(1))(chip, others, own, by2, by2, by2)


def _adamw(name, w, g, m, v):
    R, C = w.shape
    tr = _row_tile(R, C, 1 << 17)

    def body(w_ref, g_ref, m_ref, v_ref, d_ref, nm_ref, nv_ref):
        gv = g_ref[...]
        nm = ADAM_B1 * m_ref[...] + (1.0 - ADAM_B1) * gv
        nv = ADAM_B2 * v_ref[...] + (1.0 - ADAM_B2) * (gv * gv)
        m_hat = nm / (1.0 - ADAM_B1 ** ADAM_STEP)
        v_hat = nv / (1.0 - ADAM_B2 ** ADAM_STEP)
        d_ref[...] = -ADAM_LR * (m_hat / (jnp.sqrt(v_hat) + ADAM_EPS) + ADAM_WD * w_ref[...])
        nm_ref[...] = nm
        nv_ref[...] = nv

    spec = pl.BlockSpec((tr, C), lambda i: (i, 0))
    return pl.pallas_call(body, name=name, grid=(R // tr,), in_specs=[spec] * 4, out_specs=[spec] * 3,
                          out_shape=[_sds((R, C), F32)] * 3, compiler_params=_params(1))(w, g, m, v)


BIG = ("w_in", "w_proj_a", "w_proj_b", "w_out", "w_up", "w_down")
COL_SHARDED = ("w_in", "w_proj_a", "w_up")
SMALL = ("norm1_g", "q_norm_g", "k_norm_g", "ret_gn_g", "ret_gn_b", "norm2_g")
ALL_W = ("norm1_g", "w_in", "q_norm_g", "k_norm_g", "ret_gn_g", "ret_gn_b", "w_proj_a", "w_proj_b", "w_out",
         "norm2_g", "w_up", "w_down")
LANES = 128


def _to_full(name, gathered):
    k, r, c = gathered.shape
    if name in COL_SHARDED:
        return gathered.transpose(1, 0, 2).reshape(r, k * c)
    return gathered.reshape(k * r, c)


def _to_shard_major(name, full):
    if name in COL_SHARDED:
        r, c4 = full.shape
        return full.reshape(r, N_CHIPS, c4 // N_CHIPS).transpose(1, 0, 2)
    r4, c = full.shape
    return full.reshape(N_CHIPS, r4 // N_CHIPS, c)


def kernel(x, norm1_g, w_in, q_norm_g, k_norm_g, ret_gn_g, ret_gn_b, w_proj_a, w_proj_b, w_out, norm2_g, w_up, w_down, loss_target, m_norm1_g, m_w_in, m_q_norm_g, m_k_norm_g, m_ret_gn_g, m_ret_gn_b, m_w_proj_a, m_w_proj_b, m_w_out, m_norm2_g, m_w_up, m_w_down, v_norm1_g, v_w_in, v_q_norm_g, v_k_norm_g, v_ret_gn_g, v_ret_gn_b, v_w_proj_a, v_w_proj_b, v_w_out, v_norm2_g, v_w_up, v_w_down):
    weights = dict(norm1_g=norm1_g, w_in=w_in, q_norm_g=q_norm_g, k_norm_g=k_norm_g, ret_gn_g=ret_gn_g,
                   ret_gn_b=ret_gn_b, w_proj_a=w_proj_a, w_proj_b=w_proj_b, w_out=w_out, norm2_g=norm2_g,
                   w_up=w_up, w_down=w_down)
    moments_m = dict(norm1_g=m_norm1_g, w_in=m_w_in, q_norm_g=m_q_norm_g, k_norm_g=m_k_norm_g, ret_gn_g=m_ret_gn_g,
                     ret_gn_b=m_ret_gn_b, w_proj_a=m_w_proj_a, w_proj_b=m_w_proj_b, w_out=m_w_out,
                     norm2_g=m_norm2_g, w_up=m_w_up, w_down=m_w_down)
    moments_v = dict(norm1_g=v_norm1_g, w_in=v_w_in, q_norm_g=v_q_norm_g, k_norm_g=v_k_norm_g, ret_gn_g=v_ret_gn_g,
                     ret_gn_b=v_ret_gn_b, w_proj_a=v_w_proj_a, w_proj_b=v_w_proj_b, w_out=v_w_out,
                     norm2_g=v_norm2_g, w_up=v_w_up, w_down=v_w_down)

    mx, my = lax.axis_index("x"), lax.axis_index("y")
    core = lax.axis_index("c").astype(jnp.int32).reshape(1)
    chip = (2 * mx + my).astype(jnp.int32).reshape(1)
    others = jnp.stack([2 * (1 - mx) + my, 2 * mx + 1 - my, 2 * (1 - mx) + 1 - my]).astype(jnp.int32)
    shards = {n: weights[n][0].astype(CDT) for n in BIG}
    (g_in,) = _gather_weights([shards["w_in"]])
    w_in_full = _to_full("w_in", _pair_fill("pair_fill_w_in", g_in, shards["w_in"], core, others, chip))
    late = [n for n in BIG if n != "w_in"]
    l_send, l_recv, l_srcs, l_lands, l_token = _exchange_start(
        "gather_late_start", "gather", [shards[n] for n in late], [((N_CHIPS,) + shards[n].shape, CDT) for n in late])

    def late_weights(after):
        srcs, lands = _exchange_wait("gather_late_wait", "gather", l_send, l_recv, l_srcs, l_lands, after)
        out = {}
        for n, mine, land in zip(late, srcs, lands):
            out[n] = _to_full(n, _pair_fill("pair_fill_%s" % n, land, mine, core, others, chip))
        return out

    pending = []

    def on_grads(group):
        names = list(group)
        red = [_pair_reduce("pair_reduce_%s" % n, _to_shard_major(n, group[n]), core) for n in names]
        wires = [wire for _, wire in red]
        send, recv, srcs, lands, token = _exchange_start(
            "scatter_start_%s" % names[0], "scatter", wires, [(wire.shape, wire.dtype) for wire in wires])
        pending.append((names, [own for own, _ in red], send, recv, srcs, lands))
        return token

    small = {n: weights[n].reshape(1, -1) for n in SMALL}

    loss, grad_x, small_g = _local_step(x[0], loss_target[0], w_in_full, small, late_weights, on_grads, l_token)
    loss = lax.psum(loss, ("x", "y", "c"))

    grads = {}
    for names, owns, send, recv, srcs, lands in pending:
        _, got = _exchange_wait("scatter_wait_%s" % names[0], "scatter", send, recv, srcs, lands, grad_x)
        for n, own, by_chip in zip(names, owns, got):
            half = _sum4("chip_sum_%s" % n, own, by_chip, chip, others)
            grads[n] = _pair_share("pair_share_%s" % n, half, core)

    packed = jnp.concatenate([small_g[n].reshape(1, -1) for n in SMALL], axis=1)
    sizes = [small_g[n].size for n in SMALL]
    red = _all_reduce_small(packed.reshape(-1, LANES)).reshape(1, -1)
    off = 0
    for n, sz in zip(SMALL, sizes):
        grads[n] = red[:, off:off + sz]
        off += sz

    out_g, out_d, out_m, out_v = {}, {}, {}, {}
    for n in ALL_W:
        shape = weights[n].shape
        two_d = (shape[-2], shape[-1]) if n in BIG else (1, weights[n].size)
        g2 = grads[n].reshape(two_d)
        d, nm, nv = _adamw("adamw_%s" % n, weights[n].reshape(two_d), g2, moments_m[n].reshape(two_d),
                           moments_v[n].reshape(two_d))
        out_g[n], out_d[n], out_m[n], out_v[n] = (t.reshape(shape) for t in (g2, d, nm, nv))

    return (loss, grad_x[None], *[out_g[n] for n in ALL_W], *[out_d[n] for n in ALL_W],
            *[out_m[n] for n in ALL_W], *[out_v[n] for n in ALL_W])
```

```python
import functools
import math

import jax
import jax.numpy as jnp
from jax import lax
from jax.experimental import pallas as pl
from jax.experimental.pallas import tpu as pltpu

CDT = jnp.bfloat16
F32 = jnp.float32
EPS = 1e-6

ATT_GROUPS = ((128, 1), (512, 4), (2048, 16))
ATT_HPG = 4
ATT_HEADS = 12
HD = 128
BLK = 128
ATT_W = ATT_HEADS * HD
GW = ATT_HPG * HD
RET_HEADS = 4

ADAM_LR = 0.001
ADAM_B1 = 0.9
ADAM_B2 = 0.999
ADAM_EPS = 1e-08
ADAM_WD = 0.01
ADAM_STEP = 10

VMEM_LIMIT_BYTES = 48 * 1024 * 1024
MXU_DIM = 256
MESH = pl.DeviceIdType.MESH
HBM_SPEC = pl.BlockSpec(memory_space=pltpu.HBM)
VMEM_SPEC = pl.BlockSpec(memory_space=pltpu.VMEM)


def _params(n_axes):
    return pltpu.CompilerParams(dimension_semantics=("arbitrary",) * n_axes,
                                vmem_limit_bytes=VMEM_LIMIT_BYTES)


def _dot_nn(a, b):
    return jnp.dot(a, b, preferred_element_type=F32)


def _dot_nt(a, b):
    return lax.dot_general(a, b, (((1,), (1,)), ((), ())), preferred_element_type=F32)


def _dot_tn(a, b):
    return lax.dot_general(a, b, (((0,), (0,)), ((), ())), preferred_element_type=F32)


def _sigmoid(v):
    return 1.0 / (1.0 + jnp.exp(-v))


def _matmul(name, mode, a, b, *, tm, tn, tk, extras=(), outs, epilogue, deps=(), b_spec=None, n_cols=None,
            prefetch=(), alias_dep_to_out=None, j_outer=False):
    deps = [d for d in deps if d is not None]
    if mode == "tn":
        K, M = a.shape
    else:
        M, K = a.shape
    if b_spec is None:
        (N, K2) = b.shape if mode == "nt" else b.shape[::-1]
        assert K == K2, (name, a.shape, b.shape)
        if mode == "nt":
            b_spec = pl.BlockSpec((tn, tk), lambda i, j, k, *p: (j, k))
        else:
            b_spec = pl.BlockSpec((tk, tn), lambda i, j, k, *p: (k, j))
    else:
        N = n_cols
    assert M % tm == 0 and N % tn == 0 and K % tk == 0, (name, a.shape, b.shape)
    ni, nj, nk = M // tm, N // tn, K // tk
    if mode == "tn":
        a_spec = pl.BlockSpec((tk, tm), lambda i, j, k, *p: (k, i))
    else:
        a_spec = pl.BlockSpec((tm, tk), lambda i, j, k, *p: (i, k))
    dot = {"nn": _dot_nn, "nt": _dot_nt, "tn": _dot_tn}[mode]
    n_ex, n_out, n_dep, n_pre = len(extras), len(outs), len(deps), len(prefetch)
    grid = (ni, nj, nk)
    if j_outer:
        grid = (nj, ni, nk)

        def swapped(spec):
            return pl.BlockSpec(spec.block_shape, lambda j, i, k, *p: spec.index_map(i, j, k, *p))

        a_spec, b_spec = swapped(a_spec), swapped(b_spec)
        extras = [(e, swapped(s)) for e, s in extras]
        outs = [(o, swapped(s)) for o, s in outs]

    def body(*refs):
        refs = refs[n_pre:]
        a_ref, b_ref = refs[0], refs[1]
        ex = refs[2:2 + n_ex]
        out = refs[2 + n_ex + n_dep:2 + n_ex + n_dep + n_out]
        acc = refs[-1] if nk > 1 else None
        i = pl.program_id(1 if j_outer else 0)
        k = pl.program_id(2)
        if nk == 1:
            epilogue(dot(a_ref[...].astype(CDT), b_ref[...].astype(CDT)), ex, out, i)
            return

        @pl.when(k == 0)
        def _():
            acc[...] = jnp.zeros_like(acc)

        acc[...] += dot(a_ref[...].astype(CDT), b_ref[...].astype(CDT))

        @pl.when(k == nk - 1)
        def _():
            epilogue(acc[...], ex, out, i)

    grid_spec = pltpu.PrefetchScalarGridSpec(
        num_scalar_prefetch=n_pre, grid=grid,
        in_specs=[a_spec, b_spec] + [s for _, s in extras] + [pl.BlockSpec(memory_space=pl.ANY)] * n_dep,
        out_specs=[s for _, s in outs],
        scratch_shapes=[pltpu.VMEM((tm, tn), F32)] if nk > 1 else [])
    aliases = {}
    if alias_dep_to_out is not None:
        aliases = {n_pre + 2 + n_ex + alias_dep_to_out[0]: alias_dep_to_out[1]}
    res = pl.pallas_call(
        body, name=name, grid_spec=grid_spec, out_shape=[o for o, _ in outs], input_output_aliases=aliases,
        compiler_params=_params(3),
    )(*prefetch, a, b, *[e for e, _ in extras], *deps)
    return res


def _mn(tm, tn, col_off=0):
    return pl.BlockSpec((tm, tn), lambda i, j, k, *p: (i, j + col_off))


def _row(tn):
    return pl.BlockSpec((1, tn), lambda i, j, k, *p: (0, j))


def _ep_store(acc, ex, out, i):
    out[0][...] = acc.astype(out[0].dtype)


def _ep_resid(acc, ex, out, i):
    out[0][...] = ex[0][...] + acc


def _ep_up(acc, ex, out, i):
    out[0][...] = acc.astype(out[0].dtype)
    r = jnp.maximum(acc, 0.0)
    out[1][...] = (r * r).astype(out[1].dtype)


def _ep_down_loss(acc, ex, out, i, inv_d):
    diff = (ex[0][...] + acc) - ex[1][...]
    dx2 = diff * inv_d
    out[0][...] = dx2
    out[1][...] = dx2.astype(out[1].dtype)

    @pl.when(i == 0)
    def _():
        out[2][...] = jnp.zeros_like(out[2])

    out[2][...] += jnp.sum(diff * diff, axis=0, keepdims=True)


def _ep_dh(acc, ex, out, i):
    h = ex[0][...].astype(F32)
    out[0][...] = (acc * (2.0 * jnp.maximum(h, 0.0))).astype(out[0].dtype)


def _ep_rms_bwd(acc, ex, out, i):
    x = ex[0][...]
    g = ex[1][...]
    rstd = lax.rsqrt(jnp.mean(x * x, axis=-1, keepdims=True) + EPS)
    xh = x * rstd
    dxh = acc * g
    dx = ex[2][...] + rstd * (dxh - xh * jnp.mean(dxh * xh, axis=-1, keepdims=True))
    out[0][...] = dx
    out[1][...] = dx.astype(out[1].dtype)

    @pl.when(i == 0)
    def _():
        out[2][...] = jnp.zeros_like(out[2])

    out[2][...] += jnp.sum(acc * xh, axis=0, keepdims=True)


def _ep_gates(acc, ex, out, i):
    sa = _sigmoid(ex[0][...].astype(F32))
    sb = _sigmoid(ex[1][...].astype(F32))
    dpa = acc * sa
    dpb = acc * sb
    out[0][...] = dpa.astype(out[0].dtype)
    out[1][...] = dpb.astype(out[1].dtype)
    out[2][...] = (dpa * ex[2][...].astype(F32) * (1.0 - sa)).astype(out[2].dtype)
    out[3][...] = (dpb * ex[3][...].astype(F32) * (1.0 - sb)).astype(out[3].dtype)


def _sds(shape, dtype):
    return jax.ShapeDtypeStruct(shape, dtype)


def _rms_fwd(name, x, g, tm=512):
    S, D = x.shape

    def body(x_ref, g_ref, o_ref):
        xv = x_ref[...]
        rstd = lax.rsqrt(jnp.mean(xv * xv, axis=-1, keepdims=True) + EPS)
        o_ref[...] = (xv * rstd * g_ref[...]).astype(o_ref.dtype)

    return pl.pallas_call(
        body, name=name, grid=(S // tm,),
        in_specs=[pl.BlockSpec((tm, D), lambda i: (i, 0)), pl.BlockSpec((1, D), lambda i: (0, 0))],
        out_specs=pl.BlockSpec((tm, D), lambda i: (i, 0)),
        out_shape=_sds((S, D), CDT), compiler_params=_params(1))(x, g)


def _rm_shape(S, d, width):
    return (S, width) if d == 1 else (d, S // d, width)


def _rm_spec(tm, d, width):
    if d == 1:
        return pl.BlockSpec((tm, width), lambda i: (i, 0))
    return pl.BlockSpec((d, tm // d, width), lambda i: (0, i, 0))


def _rm_put(dst_ref, cols, buf_ref, d):
    if d == 1:
        dst_ref[:, cols] = buf_ref[...].astype(dst_ref.dtype)
        return
    m = buf_ref.shape[0] // d
    for r in range(d):
        dst_ref[r, :, cols] = buf_ref[pl.ds(r, m, stride=d), :].astype(dst_ref.dtype)


def _rm_reader(buf_ref, src_ref, d):
    if d == 1:
        return lambda s: src_ref[:, s * HD:(s + 1) * HD].astype(F32)
    m = buf_ref.shape[1] // d
    for s in range(buf_ref.shape[0]):
        for r in range(d):
            buf_ref.at[s][pl.ds(r, m, stride=d), :] = src_ref[r, :, s * HD:(s + 1) * HD].astype(F32)
    return lambda s: buf_ref[s]


def _qknorm_fwd(proj, gqk, tm=512):
    S = proj.shape[0]
    W = 2 * ATT_W
    dil = [d for _, d in ATT_GROUPS]

    def body(p_ref, g_ref, o0, o1, o2, buf):
        outs = (o0, o1, o2)
        for hd in range(3 * ATT_HEADS):
            which, head = hd // ATT_HEADS, hd % ATT_HEADS
            grp, slot = head // ATT_HPG, head % ATT_HPG
            v = p_ref[:, hd * HD:(hd + 1) * HD].astype(F32)
            if which < 2:
                rstd = lax.rsqrt(jnp.mean(v * v, axis=-1, keepdims=True) + EPS)
                v = v * rstd * g_ref[:, hd * HD:(hd + 1) * HD]
            buf[...] = v
            _rm_put(outs[grp], slice(which * GW + slot * HD, which * GW + (slot + 1) * HD), buf, dil[grp])

    return pl.pallas_call(
        body, name="qknorm_fwd", grid=(S // tm,),
        in_specs=[pl.BlockSpec((tm, 3 * ATT_W), lambda i: (i, 0)), pl.BlockSpec((1, W), lambda i: (0, 0))],
        out_specs=[_rm_spec(tm, d, 3 * GW) for d in dil],
        out_shape=[_sds(_rm_shape(S, d, 3 * GW), CDT) for d in dil],
        scratch_shapes=[pltpu.VMEM((tm, HD), F32)],
        compiler_params=_params(1))(proj, gqk)


def _qknorm_bwd(proj, gqk, dqs, dks, dvs, tm=256):
    S = proj.shape[0]
    W = 2 * ATT_W
    dil = [d for _, d in ATT_GROUPS]

    def body(p_ref, g_ref, *refs):
        ins = refs[0:9]
        o_ref, dg_ref = refs[9], refs[10]
        bufs = refs[11:20]
        i = pl.program_id(0)

        @pl.when(i == 0)
        def _():
            dg_ref[...] = jnp.zeros_like(dg_ref)

        nat = [_rm_reader(bufs[j], ins[j], dil[j % 3]) for j in range(9)]
        dq_get, dk_get, dv_get = nat[0:3], nat[3:6], nat[6:9]
        for hd in range(2 * ATT_HEADS):
            sl = slice(hd * HD, (hd + 1) * HD)
            head = hd % ATT_HEADS
            grp, slot = head // ATT_HPG, head % ATT_HPG
            dn = (dq_get if hd < ATT_HEADS else dk_get)[grp](slot)
            v = p_ref[:, sl].astype(F32)
            rstd = lax.rsqrt(jnp.mean(v * v, axis=-1, keepdims=True) + EPS)
            vh = v * rstd
            dg_ref[:, sl] += jnp.sum(dn * vh, axis=0, keepdims=True)
            dvh = dn * g_ref[:, sl]
            o_ref[:, sl] = (rstd * (dvh - vh * jnp.mean(dvh * vh, axis=-1, keepdims=True))).astype(o_ref.dtype)
        for head in range(ATT_HEADS):
            grp, slot = head // ATT_HPG, head % ATT_HPG
            o_ref[:, W + head * HD:W + (head + 1) * HD] = dv_get[grp](slot).astype(o_ref.dtype)

    return pl.pallas_call(
        body, name="qknorm_bwd", grid=(S // tm,),
        in_specs=[pl.BlockSpec((tm, W), lambda i: (i, 0)), pl.BlockSpec((1, W), lambda i: (0, 0))]
        + [_rm_spec(tm, d, GW) for d in dil] * 3,
        out_specs=[pl.BlockSpec((tm, 3 * ATT_W), lambda i: (i, 0)), pl.BlockSpec((1, W), lambda i: (0, 0))],
        out_shape=[_sds((S, 3 * ATT_W), CDT), _sds((1, W), F32)],
        scratch_shapes=[pltpu.VMEM((ATT_HPG, tm, HD), F32)] * 9,
        compiler_params=_params(1))(proj, gqk, *dqs, *dks, *dvs)


def _att_mask(n):
    qi = lax.broadcasted_iota(jnp.int32, (BLK, 2 * BLK), 0)
    kj = lax.broadcasted_iota(jnp.int32, (BLK, 2 * BLK), 1)
    dist = BLK + qi - kj
    valid = (dist >= 0) & (dist <= BLK) & ((kj >= BLK) | (n > 0))
    return valid, dist.astype(F32)


def _att_slopes(grp):
    return [2.0 ** (-8.0 * (grp * ATT_HPG + hh + 1) / ATT_HEADS) for hh in range(ATT_HPG)]


def _att_spec(d, row_fn, col=0):
    if d == 1:
        return pl.BlockSpec((BLK, GW), lambda r, n: (row_fn(n), col))
    return pl.BlockSpec((None, BLK, GW), lambda r, n: (r, row_fn(n), col))


def _att_qkv_specs(d, nb):
    last = nb - 1

    def cur(n):
        return jnp.minimum(n, last)

    def prev(n):
        return jnp.maximum(jnp.minimum(n, last) - 1, 0)

    return [_att_spec(d, cur, 0), _att_spec(d, prev, 1), _att_spec(d, cur, 1), _att_spec(d, prev, 2),
            _att_spec(d, cur, 2)]


def _att_fwd(grp, S, qkv):
    _, d = ATT_GROUPS[grp]
    L = S // d
    nb = L // BLK
    slopes = _att_slopes(grp)
    scale = HD ** -0.5

    def body(q_ref, kp_ref, kc_ref, vp_ref, vc_ref, o_ref, l_ref):
        n = pl.program_id(1)
        valid, distf = _att_mask(n)
        for hh in range(ATT_HPG):
            sl = slice(hh * HD, (hh + 1) * HD)
            k = jnp.concatenate([kp_ref[:, sl], kc_ref[:, sl]], axis=0)
            v = jnp.concatenate([vp_ref[:, sl], vc_ref[:, sl]], axis=0)
            s = _dot_nt(q_ref[:, sl], k) * scale + (-slopes[hh] * d) * distf
            s = jnp.where(valid, s, -1e30)
            m = jnp.max(s, axis=-1, keepdims=True)
            p = jnp.exp(s - m)
            den = jnp.sum(p, axis=-1, keepdims=True)
            o_ref[:, sl] = _dot_nn(p.astype(CDT), v) / den
            l_ref[:, sl] = jnp.broadcast_to(m + jnp.log(den), (BLK, HD))

    out_spec = _att_spec(d, lambda n: n)
    return pl.pallas_call(
        body, name="att_fwd_g%d" % grp, grid=(d, nb),
        in_specs=_att_qkv_specs(d, nb),
        out_specs=[out_spec, out_spec],
        out_shape=[_sds(_rm_shape(S, d, GW), F32)] * 2,
        compiler_params=_params(2),
    )(qkv, qkv, qkv, qkv, qkv)


def _att_bwd(grp, S, qkv, lse, do_g, c_g):
    _, d = ATT_GROUPS[grp]
    L = S // d
    nb = L // BLK
    slopes = _att_slopes(grp)
    scale = HD ** -0.5
    last = nb - 1

    def body(q_ref, kp_ref, kc_ref, vp_ref, vc_ref, l_ref, do_ref, c_ref, dq_ref, dk_ref, dv_ref, ck, cv):
        n = pl.program_id(1)

        @pl.when(n == 0)
        def _():
            ck[...] = jnp.zeros_like(ck)
            cv[...] = jnp.zeros_like(cv)

        @pl.when(n < nb)
        def _():
            valid, distf = _att_mask(n)
            for hh in range(ATT_HPG):
                sl = slice(hh * HD, (hh + 1) * HD)
                q = q_ref[:, sl]
                k = jnp.concatenate([kp_ref[:, sl], kc_ref[:, sl]], axis=0)
                v = jnp.concatenate([vp_ref[:, sl], vc_ref[:, sl]], axis=0)
                do = do_ref[:, sl]
                s = _dot_nt(q, k) * scale + (-slopes[hh] * d) * distf
                p = jnp.where(valid, jnp.exp(s - l_ref[:, sl][:, 0:1]), 0.0)
                dp = _dot_nt(do, v)
                ds = (p * (dp + c_ref[:, sl][:, 0:1]) * scale).astype(CDT)
                dq_ref[:, sl] = _dot_nn(ds, k)
                dk = _dot_tn(ds, q)
                dv = _dot_tn(p.astype(CDT), do)
                dk_ref[:, sl] = ck[:, sl] + dk[0:BLK]
                dv_ref[:, sl] = cv[:, sl] + dv[0:BLK]
                ck[:, sl] = dk[BLK:2 * BLK]
                cv[:, sl] = dv[BLK:2 * BLK]

        @pl.when(n == nb)
        def _():
            dk_ref[...] = ck[...]
            dv_ref[...] = cv[...]

    blk = (BLK, GW)
    at_q = _att_spec(d, lambda n: jnp.minimum(n, last))
    behind = _att_spec(d, lambda n: jnp.maximum(n - 1, 0))
    return pl.pallas_call(
        body, name="att_bwd_g%d" % grp, grid=(d, nb + 1),
        in_specs=_att_qkv_specs(d, nb) + [at_q, at_q, at_q],
        out_specs=[at_q, behind, behind],
        out_shape=[_sds(_rm_shape(S, d, GW), F32)] * 3,
        scratch_shapes=[pltpu.VMEM(blk, F32), pltpu.VMEM(blk, F32)],
        compiler_params=_params(2),
    )(qkv, qkv, qkv, qkv, qkv, lse, do_g, c_g)


def _mix_alpha(l0, l1, l2):
    mx = jnp.maximum(jnp.maximum(l0, l1), l2)
    e = [jnp.exp(l0 - mx), jnp.exp(l1 - mx), jnp.exp(l2 - mx)]
    tot = e[0] + e[1] + e[2]
    return [ei / tot for ei in e]


def _mix_fwd(S, os_, ls_, tm=512):
    dil = [d for _, d in ATT_GROUPS]

    def body(*refs):
        out, bufs = refs[6], refs[7:13]
        get = [_rm_reader(bufs[j], refs[j], dil[j % 3]) for j in range(6)]
        for s in range(ATT_HPG):
            al = _mix_alpha(*[get[3 + g](s) for g in range(3)])
            mixed = al[0] * get[0](s) + al[1] * get[1](s) + al[2] * get[2](s)
            out[:, s * HD:(s + 1) * HD] = mixed.astype(out.dtype)

    specs = [_rm_spec(tm, d, GW) for d in dil]
    return pl.pallas_call(
        body, name="mix_fwd", grid=(S // tm,), in_specs=specs * 2, out_specs=pl.BlockSpec((tm, GW), lambda i: (i, 0)),
        out_shape=_sds((S, GW), CDT), scratch_shapes=[pltpu.VMEM((ATT_HPG, tm, HD), F32)] * 6,
        compiler_params=_params(1))(*os_, *ls_)


def _mix_bwd(S, os_, ls_, do_a, tm=512):
    dil = [d for _, d in ATT_GROUPS]

    def body(*refs):
        d_ref, outs, bufs, tmp = refs[6], refs[7:13], refs[13:19], refs[19]
        get = [_rm_reader(bufs[j], refs[j], dil[j % 3]) for j in range(6)]
        for s in range(ATT_HPG):
            cols = slice(s * HD, (s + 1) * HD)
            al = _mix_alpha(*[get[3 + g](s) for g in range(3)])
            dv = d_ref[:, cols]
            o_a = al[0] * get[0](s) + al[1] * get[1](s) + al[2] * get[2](s)
            dsum = jnp.sum(dv * o_a, axis=-1, keepdims=True)
            for g in range(3):
                tmp[...] = al[g] * dv
                _rm_put(outs[g], cols, tmp, dil[g])
                tmp[...] = -(al[g] * dsum)
                _rm_put(outs[3 + g], cols, tmp, dil[g])

    specs = [_rm_spec(tm, d, GW) for d in dil]
    res = pl.pallas_call(
        body, name="mix_bwd", grid=(S // tm,), in_specs=specs * 2 + [pl.BlockSpec((tm, GW), lambda i: (i, 0))],
        out_specs=specs * 2,
        out_shape=[_sds(_rm_shape(S, d, GW), CDT) for d in dil] + [_sds(_rm_shape(S, d, GW), F32) for d in dil],
        scratch_shapes=[pltpu.VMEM((ATT_HPG, tm, HD), F32)] * 6 + [pltpu.VMEM((tm, HD), F32)],
        compiler_params=_params(1))(*os_, *ls_, do_a)
    return res[:3], res[3:]


def _ret_tables(dk):
    H, C = RET_HEADS, BLK
    log_g = jnp.log(1.0 - 2.0 ** (-5.0 - jnp.arange(H, dtype=F32)))
    idx = jnp.arange(C, dtype=F32)
    diff = idx[:, None] - idx[None, :]
    decay = jnp.where(diff >= 0, jnp.exp(log_g[:, None, None] * jnp.maximum(diff, 0.0)), 0.0)
    xi = jnp.exp(log_g[:, None] * (idx[None, :] + 1.0))
    zeta = jnp.exp(log_g[:, None] * (C - 1.0 - idx[None, :])) * (dk ** -0.5)
    g_chunk = jnp.exp(log_g * C)
    bc = lambda t: jnp.broadcast_to(t[:, :, None], (H, C, C))
    return decay, bc(xi), bc(zeta), jnp.broadcast_to(g_chunk[:, None, None], (H, 8, C))


def _gn_fwd(o, g, b):
    mu = jnp.mean(o, axis=-1, keepdims=True)
    xc = o - mu
    rstd = lax.rsqrt(jnp.mean(xc * xc, axis=-1, keepdims=True) + EPS)
    yh = xc * rstd
    return yh, rstd, yh * g + b


def _ret_specs(dk, dv, order):
    H = RET_HEADS
    qk_w, v_w = H * dk, H * dv
    off_q = 3 * ATT_W
    off_k, off_v, off_g = off_q + qk_w, off_q + 2 * qk_w, off_q + 2 * qk_w + v_w
    assert 2 * dk == dv and all(off % dv == 0 for off in (off_q, off_k, off_v, off_g))

    def col(off, j):
        return pl.BlockSpec((BLK, dv), lambda i: (order(i), off // dv + j))

    tab = pl.BlockSpec((H, BLK, BLK), lambda i: (0, 0, 0))
    return ([col(off_q, j) for j in range(H // 2)] + [col(off_k, j) for j in range(H // 2)]
            + [col(off_v, j) for j in range(H)] + [col(off_g, j) for j in range(H)]
            + [tab, tab, tab, pl.BlockSpec((H, 8, BLK), lambda i: (0, 0, 0))])


def _ret_heads(refs, dk):
    H = RET_HEADS
    q_refs, k_refs = refs[0:H // 2], refs[H // 2:H]
    v_refs, gr_refs = refs[H:2 * H], refs[2 * H:3 * H]

    def head(h):
        cols = slice((h % 2) * dk, (h % 2 + 1) * dk)
        return q_refs[h // 2][:, cols], k_refs[h // 2][:, cols], v_refs[h][...], gr_refs[h][...]

    return head, refs[3 * H:3 * H + 4]


def _ret_fwd(proj, gn_g, gn_b, dk, dv):
    S = proj.shape[0]
    N = S // BLK
    H = RET_HEADS
    kscale = dk ** -0.5
    n_in = 3 * H + 4

    def body(*refs):
        head, (dec_ref, xi_ref, zeta_ref, gc_ref) = _ret_heads(refs, dk)
        g_ref, b_ref, opre_ref, or_ref, st_ref, state = refs[n_in:n_in + 6]
        n = pl.program_id(0)

        @pl.when(n == 0)
        def _():
            state[...] = jnp.zeros_like(state)

        for h in range(H):
            vs = slice(h * dv, (h + 1) * dv)
            q, k, v, gr = head(h)
            s = _dot_nt(q, k) * kscale * dec_ref[h]
            st = state[h]
            st_c = st.astype(CDT)
            st_ref[h] = st_c
            o = _dot_nn(s.astype(CDT), v) + _dot_nn(q, st_c) * xi_ref[h][:, 0:1]
            kz = (k.astype(F32) * zeta_ref[h][:, 0:1]).astype(CDT)
            state[h] = st * gc_ref[h][0:1, 0:1] + _dot_tn(kz, v)
            opre_ref[:, vs] = o
            _, _, y = _gn_fwd(o, g_ref[:, vs], b_ref[:, vs])
            gr = gr.astype(F32)
            or_ref[:, vs] = (y * (gr * _sigmoid(gr))).astype(or_ref.dtype)

    v_w = H * dv
    row = pl.BlockSpec((1, v_w), lambda i: (0, 0))
    tile = pl.BlockSpec((BLK, v_w), lambda i: (i, 0))
    return pl.pallas_call(
        body, name="ret_fwd", grid=(N,),
        in_specs=_ret_specs(dk, dv, lambda i: i) + [row, row],
        out_specs=[tile, tile, pl.BlockSpec((None, H, dk, dv), lambda i: (i, 0, 0, 0))],
        out_shape=[_sds((S, v_w), F32), _sds((S, v_w), CDT), _sds((N, H, dk, dv), CDT)],
        scratch_shapes=[pltpu.VMEM((H, dk, dv), F32)],
        compiler_params=_params(1),
    )(*[proj] * (3 * H), *_ret_tables(dk), gn_g, gn_b)


def _ret_bwd(proj, gn_g, gn_b, o_pre, states, d_or, dk, dv):
    S = proj.shape[0]
    N = S // BLK
    H = RET_HEADS
    qk_w, v_w = H * dk, H * dv
    kscale = dk ** -0.5
    n_in = 3 * H + 4

    def body(*refs):
        head, (dec_ref, xi_ref, zeta_ref, gc_ref) = _ret_heads(refs, dk)
        g_ref, b_ref, opre_ref, st_ref, dor_ref, out_ref, dg_ref, db_ref, dstate = refs[n_in:n_in + 9]
        i = pl.program_id(0)

        @pl.when(i == 0)
        def _():
            dstate[...] = jnp.zeros_like(dstate)
            dg_ref[...] = jnp.zeros_like(dg_ref)
            db_ref[...] = jnp.zeros_like(db_ref)

        for h in range(H):
            vs = slice(h * dv, (h + 1) * dv)
            q, k, v, gr = head(h)
            decay, xi, zeta_s = dec_ref[h], xi_ref[h][:, 0:1], zeta_ref[h][:, 0:1]
            gr = gr.astype(F32)
            sg = _sigmoid(gr)
            gain = g_ref[:, vs]
            yh, rstd, y = _gn_fwd(opre_ref[:, vs], gain, b_ref[:, vs])
            d_or_v = dor_ref[:, vs]
            dy = d_or_v * (gr * sg)
            out_ref[:, 2 * qk_w + v_w + h * dv:2 * qk_w + v_w + (h + 1) * dv] = (
                d_or_v * y * (sg * (1.0 + gr * (1.0 - sg)))).astype(out_ref.dtype)
            dg_ref[:, vs] += jnp.sum(dy * yh, axis=0, keepdims=True)
            db_ref[:, vs] += jnp.sum(dy, axis=0, keepdims=True)
            dyh = dy * gain
            do = rstd * (dyh - jnp.mean(dyh, axis=-1, keepdims=True)
                         - yh * jnp.mean(dyh * yh, axis=-1, keepdims=True))
            do_c = do.astype(CDT)
            dox = (do * xi).astype(CDT)
            a_c = (_dot_nt(q, k) * kscale * decay).astype(CDT)
            g_c = (_dot_nt(do_c, v) * decay).astype(CDT)
            dsn = dstate[h]
            dsn_c = dsn.astype(CDT)
            kz = (k.astype(F32) * zeta_s).astype(CDT)
            dq = _dot_nn(g_c, k) * kscale + _dot_nt(dox, st_ref[h])
            dkk = _dot_tn(g_c, q) * kscale + _dot_nt(v, dsn_c) * zeta_s
            dvv = _dot_tn(a_c, do_c) + _dot_nn(kz, dsn_c)
            dstate[h] = dsn * gc_ref[h][0:1, 0:1] + _dot_tn(q, dox)
            out_ref[:, h * dk:(h + 1) * dk] = dq.astype(out_ref.dtype)
            out_ref[:, qk_w + h * dk:qk_w + (h + 1) * dk] = dkk.astype(out_ref.dtype)
            out_ref[:, 2 * qk_w + h * dv:2 * qk_w + (h + 1) * dv] = dvv.astype(out_ref.dtype)

    rev = lambda i: N - 1 - i
    row = pl.BlockSpec((1, v_w), lambda i: (0, 0))
    tile = pl.BlockSpec((BLK, v_w), lambda i: (rev(i), 0))
    out_w = 2 * qk_w + 2 * v_w
    return pl.pallas_call(
        body, name="ret_bwd", grid=(N,),
        in_specs=_ret_specs(dk, dv, rev) + [row, row, tile,
                 pl.BlockSpec((None, H, dk, dv), lambda i: (rev(i), 0, 0, 0)), tile],
        out_specs=[pl.BlockSpec((BLK, out_w), lambda i: (rev(i), 0)), row, row],
        out_shape=[_sds((S, out_w), CDT), _sds((1, v_w), F32), _sds((1, v_w), F32)],
        scratch_shapes=[pltpu.VMEM((H, dk, dv), F32)],
        compiler_params=_params(1),
    )(*[proj] * (3 * H), *_ret_tables(dk), gn_g, gn_b, o_pre, states, d_or)


def _merge_fwd(o_a, o_r, wa, wb, proj, d_model, tm=512, tn=512):
    S, in_w = proj.shape
    off_a, off_b = in_w - 2 * d_model, in_w - d_model
    assert off_a % tn == 0 and off_b % tn == 0

    def body(oa_ref, or_ref, wa_ref, wb_ref, ga_ref, gb_ref, y_ref, pa_ref, pb_ref):
        pa = _dot_nn(oa_ref[...], wa_ref[...])
        pb = _dot_nn(or_ref[...], wb_ref[...])
        y = _sigmoid(ga_ref[...].astype(F32)) * pa + _sigmoid(gb_ref[...].astype(F32)) * pb
        y_ref[...] = y.astype(y_ref.dtype)
        pa_ref[...] = pa.astype(pa_ref.dtype)
        pb_ref[...] = pb.astype(pb_ref.dtype)

    ka, kb = o_a.shape[1], o_r.shape[1]
    out = pl.BlockSpec((tm, tn), lambda i, j: (i, j))
    return pl.pallas_call(
        body, name="merge_fwd", grid=(S // tm, d_model // tn),
        in_specs=[pl.BlockSpec((tm, ka), lambda i, j: (i, 0)), pl.BlockSpec((tm, kb), lambda i, j: (i, 0)),
                  pl.BlockSpec((ka, tn), lambda i, j: (0, j)), pl.BlockSpec((kb, tn), lambda i, j: (0, j)),
                  pl.BlockSpec((tm, tn), lambda i, j: (i, off_a // tn + j)),
                  pl.BlockSpec((tm, tn), lambda i, j: (i, off_b // tn + j))],
        out_specs=[out, out, out], out_shape=[_sds((S, d_model), CDT)] * 3,
        compiler_params=_params(2))(o_a, o_r, wa, wb, proj, proj)


def _local_step(x, target, w_in_mine, chip, others, far_w_in, small, late_weights, on_grads, deps0=()):
    S, D = x.shape
    ns_in = w_in_mine.shape[1]
    in_w = N_CHIPS * ns_in
    d_ff = 4 * D
    ret_v_w = 2 * D
    dv = ret_v_w // RET_HEADS
    dk = (in_w - 3 * ATT_W - 2 * ret_v_w - 2 * D) // (2 * RET_HEADS)
    gqk = jnp.concatenate([small["q_norm_g"].reshape(1, ATT_W), small["k_norm_g"].reshape(1, ATT_W)], axis=1)
    g1, g2 = small["norm1_g"], small["norm2_g"]
    gn_g, gn_b = small["ret_gn_g"], small["ret_gn_b"]

    xn = _rms_fwd("rms1_fwd", x, g1)
    proj_sds = _sds((S, in_w), CDT)
    (proj_mine,) = _matmul(
        "in_proj_mine", "nn", xn, w_in_mine, tm=512, tn=ns_in, tk=D, prefetch=[chip],
        outs=[(proj_sds, pl.BlockSpec((512, ns_in), lambda i, j, k, c: (i, c[0])))], epilogue=_ep_store, deps=deps0)
    w_in = far_w_in(proj_mine)
    (proj,) = _matmul(
        "in_proj_far", "nn", xn, w_in, tm=512, tn=ns_in, tk=D, prefetch=[others], n_cols=(N_CHIPS - 1) * ns_in,
        b_spec=pl.BlockSpec((None, D, ns_in), lambda i, j, k, o: (o[j], 0, 0)),
        outs=[(proj_sds, pl.BlockSpec((512, ns_in), lambda i, j, k, o: (i, o[j])))], epilogue=_ep_store,
        deps=[proj_mine], alias_dep_to_out=(0, 0), j_outer=True)
    qkv = _qknorm_fwd(proj, gqk)
    att = [_att_fwd(g, S, qkv[g]) for g in range(3)]
    os_, ls_ = [a[0] for a in att], [a[1] for a in att]
    o_a = _mix_fwd(S, os_, ls_)
    o_pre, o_r, states = _ret_fwd(proj, gn_g, gn_b, dk, dv)
    w = late_weights(o_r)
    y, pa, pb = _merge_fwd(o_a, o_r, w["w_proj_a"], w["w_proj_b"], proj, D)
    (x1,) = _matmul("out_proj", "nn", y, w["w_out"], tm=512, tn=D, tk=D,
                    extras=[(x, _mn(512, D))], outs=[(_sds((S, D), F32), _mn(512, D))], epilogue=_ep_resid)
    xn2 = _rms_fwd("rms2_fwd", x1, g2)
    hid, act = _matmul("mlp_up", "nn", xn2, w["w_up"], tm=1024, tn=512, tk=D,
                       outs=[(_sds((S, d_ff), CDT), _mn(1024, 512))] * 2, epilogue=_ep_up)
    dx2, dx2c, loss_row = _matmul(
        "mlp_down_loss", "nn", act, w["w_down"], tm=512, tn=D, tk=2048,
        extras=[(x1, _mn(512, D)), (target, _mn(512, D))],
        outs=[(_sds((S, D), F32), _mn(512, D)), (_sds((S, D), CDT), _mn(512, D)), (_sds((1, D), F32), _row(D))],
        epilogue=functools.partial(_ep_down_loss, inv_d=1.0 / D))
    loss = 0.5 * jnp.sum(loss_row) / D

    (dh,) = _matmul("d_hidden", "nt", dx2c, w["w_down"], tm=1024, tn=512, tk=D,
                    extras=[(hid, _mn(1024, 512))], outs=[(_sds((S, d_ff), CDT), _mn(1024, 512))], epilogue=_ep_dh)
    (gw_down,) = _matmul("dw_down", "tn", act, dx2c, tm=1024, tn=D, tk=1024,
                         outs=[(_sds((d_ff, D), F32), _mn(1024, D))], epilogue=_ep_store)
    (gw_up,) = _matmul("dw_up", "tn", xn2, dh, tm=D, tn=1024, tk=1024,
                       outs=[(_sds((D, d_ff), F32), _mn(D, 1024))], epilogue=_ep_store)
    tok = on_grads({"w_down": gw_down, "w_up": gw_up})
    dx1, dx1c, dg2 = _matmul(
        "d_x1", "nt", dh, w["w_up"], tm=512, tn=D, tk=2048,
        extras=[(x1, _mn(512, D)), (g2, _row(D)), (dx2, _mn(512, D))],
        outs=[(_sds((S, D), F32), _mn(512, D)), (_sds((S, D), CDT), _mn(512, D)), (_sds((1, D), F32), _row(D))],
        epilogue=_ep_rms_bwd, deps=[tok])

    gt = 512
    assert (in_w - 2 * D) % gt == 0
    off_a, off_b = (in_w - 2 * D) // gt, (in_w - D) // gt
    dpa, dpb, dga, dgb = _matmul(
        "d_gates", "nt", dx1c, w["w_out"], tm=512, tn=gt, tk=D,
        extras=[(proj, _mn(512, gt, off_a)), (proj, _mn(512, gt, off_b)), (pa, _mn(512, gt)), (pb, _mn(512, gt))],
        outs=[(_sds((S, D), CDT), _mn(512, gt))] * 4, epilogue=_ep_gates)
    (gw_out,) = _matmul("dw_out", "tn", y, dx1c, tm=D, tn=D, tk=1024,
                        outs=[(_sds((D, D), F32), _mn(D, D))], epilogue=_ep_store)
    (gw_pa,) = _matmul("dw_proj_a", "tn", o_a, dpa, tm=GW, tn=D, tk=1024,
                       outs=[(_sds((GW, D), F32), _mn(GW, D))], epilogue=_ep_store)
    (gw_pb,) = _matmul("dw_proj_b", "tn", o_r, dpb, tm=1024, tn=D, tk=1024,
                       outs=[(_sds((ret_v_w, D), F32), _mn(1024, D))], epilogue=_ep_store)
    (do_a,) = _matmul("d_o_a", "nt", dpa, w["w_proj_a"], tm=1024, tn=GW, tk=D,
                      outs=[(_sds((S, GW), F32), _mn(1024, GW))], epilogue=_ep_store)
    tok = on_grads({"w_out": gw_out, "w_proj_a": gw_pa, "w_proj_b": gw_pb})
    (d_or,) = _matmul("d_o_r", "nt", dpb, w["w_proj_b"], tm=1024, tn=512, tk=D,
                      outs=[(_sds((S, ret_v_w), F32), _mn(1024, 512))], epilogue=_ep_store, deps=[tok])

    d_ret, dgn_g, dgn_b = _ret_bwd(proj, gn_g, gn_b, o_pre, states, d_or, dk, dv)
    do_gs, c_gs = _mix_bwd(S, os_, ls_, do_a)
    datt_parts = [_att_bwd(g, S, qkv[g], ls_[g], do_gs[g], c_gs[g]) for g in range(3)]
    d_att, dgqk = _qknorm_bwd(proj, gqk, [p[0] for p in datt_parts], [p[1] for p in datt_parts],
                              [p[2] for p in datt_parts])
    dproj = jnp.concatenate([d_att, d_ret, dga, dgb], axis=1)

    (gw_in,) = _matmul(
        "dw_in", "tn", xn, dproj, tm=512, tn=ns_in, tk=1024,
        outs=[(_sds((N_CHIPS, D, ns_in), F32), pl.BlockSpec((None, 512, ns_in), lambda i, j, k: (j, i, 0)))],
        epilogue=_ep_store)
    tok = on_grads({"w_in": gw_in})
    grad_x, _, dg1 = _matmul(
        "d_x", "nt", dproj, w_in, tm=512, tn=D, tk=ns_in, n_cols=D,
        b_spec=pl.BlockSpec((None, D, ns_in), lambda i, j, k: (k, 0, 0)),
        extras=[(x, _mn(512, D)), (g1, _row(D)), (dx1, _mn(512, D))],
        outs=[(_sds((S, D), F32), _mn(512, D)), (_sds((S, D), CDT), _mn(512, D)), (_sds((1, D), F32), _row(D))],
        epilogue=_ep_rms_bwd, deps=[tok])

    smallg = {"norm1_g": dg1, "q_norm_g": dgqk[:, :ATT_W], "k_norm_g": dgqk[:, ATT_W:],
              "ret_gn_g": dgn_g, "ret_gn_b": dgn_b, "norm2_g": dg2}
    return loss, grad_x, smallg


N_CHIPS = 4
N_DEV = 8


def _place():
    x, y, c = lax.axis_index("x"), lax.axis_index("y"), lax.axis_index("c")
    return x, y, c


def _other_chips(x, y):
    out = []
    for fx, fy in ((1, 0), (0, 1), (1, 1)):
        px = 1 - x if fx else x
        py = 1 - y if fy else y
        out.append(((px, py), 2 * px + py))
    return out


SEM_SPEC = pl.BlockSpec(memory_space=pltpu.SEMAPHORE)
ANY_SPEC = pl.BlockSpec(memory_space=pl.ANY)
EFFECT = pltpu.SideEffectType.DATAFLOW_SIDE_EFFECTING


def _ici_copies(kind, srcs, lands, send, recv):
    x, y, c = _place()
    me = 2 * x + y
    out = []
    for w, (s, l) in enumerate(zip(srcs, lands)):
        for j, ((px, py), pidx) in enumerate(_other_chips(x, y)):
            if kind == "gather":
                half = s.shape[0] // 2
                rows = pl.ds(c * half, half)
                src, dst_there, dst_here = s.at[rows, :], l.at[me, rows, :], l.at[pidx, rows, :]
            else:
                src, dst_there, dst_here = s.at[pidx], l.at[me], l.at[pidx]
            out.append((src, dst_there, dst_here, send.at[3 * w + j], recv.at[3 * w + j], (px, py, c)))
    return out


def _exchange_start(name, kind, srcs, land_shapes):
    n = len(srcs)

    def body(*refs):
        src_refs, land_refs = refs[:n], refs[n:2 * n]
        send, recv = refs[2 * n], refs[2 * n + 1]
        token = refs[-1]
        for src, dst, _, ss, rs, dev in _ici_copies(kind, src_refs, land_refs, send, recv):
            pltpu.make_async_remote_copy(src_ref=src, dst_ref=dst, send_sem=ss, recv_sem=rs, device_id=dev,
                                         device_id_type=MESH).start()
        token[...] = jnp.zeros_like(token)

    thru = [pltpu.HBM(s.shape, s.dtype) for s in srcs] + [pltpu.HBM(shape, dtype) for shape, dtype in land_shapes]
    res = pl.pallas_call(
        body, name=name,
        out_shape=(pltpu.SemaphoreType.DMA((3 * n,)), pltpu.SemaphoreType.DMA((3 * n,)), *thru, _sds((8, LANES), F32)),
        in_specs=[HBM_SPEC] * (2 * n), out_specs=(SEM_SPEC, SEM_SPEC, *[HBM_SPEC] * (2 * n), VMEM_SPEC),
        input_output_aliases={i: 2 + i for i in range(2 * n)},
        compiler_params=pltpu.CompilerParams(has_side_effects=EFFECT),
    )(*[pltpu.with_memory_space_constraint(s, pltpu.HBM) for s in srcs],
      *[pltpu.with_memory_space_constraint(lax.empty(shape, dtype), pltpu.HBM) for shape, dtype in land_shapes])
    return res[0], res[1], list(res[2:2 + n]), list(res[2 + n:2 + 2 * n]), res[-1]


def _exchange_wait(name, kind, send, recv, srcs, lands, after):
    n = len(srcs)

    def body(*refs):
        src_refs, land_refs = refs[:n], refs[n:2 * n]
        send_ref, recv_ref = refs[2 * n], refs[2 * n + 1]
        for src, _, dst, ss, rs, dev in _ici_copies(kind, src_refs, land_refs, send_ref, recv_ref):
            cp = pltpu.make_async_remote_copy(src_ref=src, dst_ref=dst, send_sem=ss, recv_sem=rs, device_id=dev,
                                              device_id_type=MESH)
            cp.wait_send()
            cp.wait_recv()

    thru = [pltpu.HBM(t.shape, t.dtype) for t in list(srcs) + list(lands)]
    res = pl.pallas_call(
        body, name=name, out_shape=thru,
        in_specs=[HBM_SPEC] * (2 * n) + [SEM_SPEC, SEM_SPEC, ANY_SPEC], out_specs=[HBM_SPEC] * (2 * n),
        input_output_aliases={i: i for i in range(2 * n)},
        compiler_params=pltpu.CompilerParams(has_side_effects=EFFECT),
    )(*srcs, *lands, send, recv, after)
    return list(res[:n]), list(res[n:])


PAIR_TILE_ELEMS = 1 << 19


def _pair_fill(name, gathered, mine, core, others, chip):
    k, r, C = gathered.shape
    half = r // 2
    tr = _row_tile(half, C, PAIR_TILE_ELEMS, mult=16)
    nt = half // tr
    n_far = N_CHIPS - 1

    def body(c_ref, o_ref, chip_ref, in_ref, mine_ref, out_ref, slot, send, recv):
        j = pl.program_id(0)
        b = (j * nt + pl.program_id(1)) % 2
        x, y, c = _place()
        cp = pltpu.make_async_remote_copy(src_ref=in_ref, dst_ref=slot.at[b], send_sem=send.at[b],
                                          recv_sem=recv.at[b], device_id=(x, y, 1 - c), device_id_type=MESH)

        @pl.when(j < n_far)
        def _():
            cp.start()
            cp.wait_recv()
            out_ref[...] = slot[b]
            cp.wait_send()

        @pl.when(j >= n_far)
        def _():
            out_ref[...] = mine_ref[...]

    def far(j):
        return jnp.minimum(j, n_far - 1)

    grid_spec = pltpu.PrefetchScalarGridSpec(
        num_scalar_prefetch=3, grid=(n_far + 2, nt),
        in_specs=[pl.BlockSpec((tr, C), lambda j, i, c, o, m: (
                      (2 * o[far(j)] + c[0]) * nt + jnp.where(j < n_far, i, nt - 1), 0)),
                  pl.BlockSpec((tr, C), lambda j, i, c, o, m: (jnp.where(j < n_far, 0, (j - n_far) * nt + i), 0))],
        out_specs=pl.BlockSpec((tr, C), lambda j, i, c, o, m: (
            jnp.where(j < n_far, 2 * o[far(j)] + 1 - c[0], 2 * m[0] + j - n_far) * nt + i, 0)),
        scratch_shapes=[pltpu.VMEM((2, tr, C), gathered.dtype), pltpu.SemaphoreType.DMA((2,)),
                        pltpu.SemaphoreType.DMA((2,))])
    out = pl.pallas_call(body, name=name, grid_spec=grid_spec, out_shape=_sds((k * r, C), gathered.dtype),
                         input_output_aliases={3: 0}, compiler_params=_params(2))(
                             core, others, chip, gathered.reshape(k * r, C), mine)
    return out.reshape(k, r, C)


def _pair_reduce(name, g, core):
    k, R, C = g.shape
    half = R // 2
    tr = _row_tile(half, C, PAIR_TILE_ELEMS)
    nt = half // tr

    def body(c_ref, mine_ref, give_ref, out_ref, wire_ref, slot, send, recv):
        b = (pl.program_id(0) * nt + pl.program_id(1)) % 2
        x, y, c = _place()
        cp = pltpu.make_async_remote_copy(src_ref=give_ref, dst_ref=slot.at[b], send_sem=send.at[b],
                                          recv_sem=recv.at[b], device_id=(x, y, 1 - c), device_id_type=MESH)
        cp.start()
        cp.wait_recv()
        tot = mine_ref[...] + slot[b]
        out_ref[...] = tot
        wire_ref[...] = tot.astype(wire_ref.dtype)
        cp.wait_send()

    blk = (tr, C)
    out_spec = pl.BlockSpec(blk, lambda s, i, c: (s * nt + i, 0))
    grid_spec = pltpu.PrefetchScalarGridSpec(
        num_scalar_prefetch=1, grid=(k, nt),
        in_specs=[pl.BlockSpec(blk, lambda s, i, c: ((2 * s + c[0]) * nt + i, 0)),
                  pl.BlockSpec(blk, lambda s, i, c: ((2 * s + 1 - c[0]) * nt + i, 0))],
        out_specs=[out_spec, out_spec],
        scratch_shapes=[pltpu.VMEM((2, tr, C), F32), pltpu.SemaphoreType.DMA((2,)), pltpu.SemaphoreType.DMA((2,))])
    g2 = g.reshape(k * R, C)
    out, wire = pl.pallas_call(body, name=name, grid_spec=grid_spec,
                               out_shape=[_sds((k * half, C), F32), _sds((k * half, C), CDT)],
                               compiler_params=_params(2))(core, g2, g2)
    return out, wire.reshape(k, half, C)


def _pair_share(name, f, core):
    half, C = f.shape
    tr = _row_tile(half, C, PAIR_TILE_ELEMS)
    nt = half // tr

    def body(c_ref, f_ref, out_ref, slot, send, recv):
        p = pl.program_id(1)
        b = pl.program_id(0) % 2
        x, y, c = _place()
        cp = pltpu.make_async_remote_copy(src_ref=f_ref, dst_ref=slot.at[b], send_sem=send.at[b],
                                          recv_sem=recv.at[b], device_id=(x, y, 1 - c), device_id_type=MESH)

        @pl.when(p == 0)
        def _():
            cp.start()
            out_ref[...] = f_ref[...]

        @pl.when(p == 1)
        def _():
            cp.wait_recv()
            out_ref[...] = slot[b]
            cp.wait_send()

    grid_spec = pltpu.PrefetchScalarGridSpec(
        num_scalar_prefetch=1, grid=(nt, 2),
        in_specs=[pl.BlockSpec((tr, C), lambda i, p, c: (i, 0))],
        out_specs=pl.BlockSpec((tr, C), lambda i, p, c: (jnp.where(p == 0, c[0], 1 - c[0]) * nt + i, 0)),
        scratch_shapes=[pltpu.VMEM((2, tr, C), F32), pltpu.SemaphoreType.DMA((2,)), pltpu.SemaphoreType.DMA((2,))])
    return pl.pallas_call(body, name=name, grid_spec=grid_spec, out_shape=_sds((2 * half, C), F32),
                          compiler_params=_params(2))(core, f)


def _all_reduce_small(v):
    r, cdim = v.shape

    def body(v_ref, o_ref, buf, send, recv):
        x, y, c = _place()
        me = 4 * x + 2 * y + c
        buf[me] = v_ref[...]
        sends = []
        for m in range(1, N_DEV):
            px = 1 - x if m & 4 else x
            py = 1 - y if m & 2 else y
            pc = 1 - c if m & 1 else c
            cp = pltpu.make_async_remote_copy(src_ref=v_ref, dst_ref=buf.at[me], send_sem=send.at[m - 1],
                                              recv_sem=recv.at[m - 1], device_id=(px, py, pc), device_id_type=MESH)
            cp.start()
            sends.append((cp, 4 * px + 2 * py + pc))
        for m, (cp, pidx) in enumerate(sends):
            pltpu.make_async_remote_copy(src_ref=v_ref, dst_ref=buf.at[pidx], send_sem=send.at[m], recv_sem=recv.at[m],
                                         device_id=(x, y, c), device_id_type=MESH).wait_recv()
        for cp, _ in sends:
            cp.wait_send()
        tot = buf[0]
        for k in range(1, N_DEV):
            tot = tot + buf[k]
        o_ref[...] = tot

    return pl.pallas_call(
        body, name="all_reduce_small", in_specs=[VMEM_SPEC], out_specs=VMEM_SPEC,
        out_shape=_sds((r, cdim), F32),
        scratch_shapes=[pltpu.VMEM((N_DEV, r, cdim), F32), pltpu.SemaphoreType.DMA((N_DEV - 1,)),
                        pltpu.SemaphoreType.DMA((N_DEV - 1,))],
    )(v)


def _row_tile(rows, cols, budget_elems=1 << 18, mult=8):
    if rows % mult:
        return rows
    t = max(mult, (budget_elems // cols) // mult * mult)
    while rows % t:
        t -= mult
    return t


def _sum4(name, own, by_chip, chip, others):
    k, R, C = by_chip.shape
    tr = _row_tile(R, C, mult=16)
    nt = R // tr

    def body(chip_ref, others_ref, own_ref, a_ref, b_ref, c_ref, o_ref):
        o_ref[...] = ((own_ref[...] + a_ref[...].astype(F32)) + b_ref[...].astype(F32)) + c_ref[...].astype(F32)

    def from_chip(j):
        return pl.BlockSpec((tr, C), lambda i, chip, oth: (oth[j] * nt + i, 0))

    grid_spec = pltpu.PrefetchScalarGridSpec(
        num_scalar_prefetch=2, grid=(nt,),
        in_specs=[pl.BlockSpec((tr, C), lambda i, chip, oth: (chip[0] * nt + i, 0)),
                  from_chip(0), from_chip(1), from_chip(2)],
        out_specs=pl.BlockSpec((tr, C), lambda i, chip, oth: (i, 0)))
    by2 = by_chip.reshape(k * R, C)
    return pl.pallas_call(body, name=name, grid_spec=grid_spec, out_shape=_sds((R, C), F32),
                          compiler_params=_params(1))(chip, others, own, by2, by2, by2)


def _adamw(name, w, g, m, v):
    R, C = w.shape
    tr = _row_tile(R, C, 1 << 17)

    def body(w_ref, g_ref, m_ref, v_ref, d_ref, nm_ref, nv_ref):
        gv = g_ref[...]
        nm = ADAM_B1 * m_ref[...] + (1.0 - ADAM_B1) * gv
        nv = ADAM_B2 * v_ref[...] + (1.0 - ADAM_B2) * (gv * gv)
        m_hat = nm / (1.0 - ADAM_B1 ** ADAM_STEP)
        v_hat = nv / (1.0 - ADAM_B2 ** ADAM_STEP)
        d_ref[...] = -ADAM_LR * (m_hat / (jnp.sqrt(v_hat) + ADAM_EPS) + ADAM_WD * w_ref[...])
        nm_ref[...] = nm
        nv_ref[...] = nv

    spec = pl.BlockSpec((tr, C), lambda i: (i, 0))
    return pl.pallas_call(body, name=name, grid=(R // tr,), in_specs=[spec] * 4, out_specs=[spec] * 3,
                          out_shape=[_sds((R, C), F32)] * 3, compiler_params=_params(1))(w, g, m, v)


BIG = ("w_in", "w_proj_a", "w_proj_b", "w_out", "w_up", "w_down")
COL_SHARDED = ("w_in", "w_proj_a", "w_up")
SMALL = ("norm1_g", "q_norm_g", "k_norm_g", "ret_gn_g", "ret_gn_b", "norm2_g")
ALL_W = ("norm1_g", "w_in", "q_norm_g", "k_norm_g", "ret_gn_g", "ret_gn_b", "w_proj_a", "w_proj_b", "w_out",
         "norm2_g", "w_up", "w_down")
LANES = 128


def _to_full(name, gathered):
    k, r, c = gathered.shape
    if name in COL_SHARDED:
        return gathered.transpose(1, 0, 2).reshape(r, k * c)
    return gathered.reshape(k * r, c)


def _to_shard_major(name, full):
    if name in COL_SHARDED:
        r, c4 = full.shape
        return full.reshape(r, N_CHIPS, c4 // N_CHIPS).transpose(1, 0, 2)
    r4, c = full.shape
    return full.reshape(N_CHIPS, r4 // N_CHIPS, c)


def kernel(x, norm1_g, w_in, q_norm_g, k_norm_g, ret_gn_g, ret_gn_b, w_proj_a, w_proj_b, w_out, norm2_g, w_up, w_down, loss_target, m_norm1_g, m_w_in, m_q_norm_g, m_k_norm_g, m_ret_gn_g, m_ret_gn_b, m_w_proj_a, m_w_proj_b, m_w_out, m_norm2_g, m_w_up, m_w_down, v_norm1_g, v_w_in, v_q_norm_g, v_k_norm_g, v_ret_gn_g, v_ret_gn_b, v_w_proj_a, v_w_proj_b, v_w_out, v_norm2_g, v_w_up, v_w_down):
    weights = dict(norm1_g=norm1_g, w_in=w_in, q_norm_g=q_norm_g, k_norm_g=k_norm_g, ret_gn_g=ret_gn_g,
                   ret_gn_b=ret_gn_b, w_proj_a=w_proj_a, w_proj_b=w_proj_b, w_out=w_out, norm2_g=norm2_g,
                   w_up=w_up, w_down=w_down)
    moments_m = dict(norm1_g=m_norm1_g, w_in=m_w_in, q_norm_g=m_q_norm_g, k_norm_g=m_k_norm_g, ret_gn_g=m_ret_gn_g,
                     ret_gn_b=m_ret_gn_b, w_proj_a=m_w_proj_a, w_proj_b=m_w_proj_b, w_out=m_w_out,
                     norm2_g=m_norm2_g, w_up=m_w_up, w_down=m_w_down)
    moments_v = dict(norm1_g=v_norm1_g, w_in=v_w_in, q_norm_g=v_q_norm_g, k_norm_g=v_k_norm_g, ret_gn_g=v_ret_gn_g,
                     ret_gn_b=v_ret_gn_b, w_proj_a=v_w_proj_a, w_proj_b=v_w_proj_b, w_out=v_w_out,
                     norm2_g=v_norm2_g, w_up=v_w_up, w_down=v_w_down)

    mx, my = lax.axis_index("x"), lax.axis_index("y")
    core = lax.axis_index("c").astype(jnp.int32).reshape(1)
    chip = (2 * mx + my).astype(jnp.int32).reshape(1)
    others = jnp.stack([2 * (1 - mx) + my, 2 * mx + 1 - my, 2 * (1 - mx) + 1 - my]).astype(jnp.int32)
    shards = {n: weights[n][0].astype(CDT) for n in BIG}
    def start_gather(name, names):
        return _exchange_start(name, "gather", [shards[n] for n in names],
                               [((N_CHIPS,) + shards[n].shape, CDT) for n in names])

    i_send, i_recv, i_srcs, i_lands, i_token = start_gather("gather_w_in_start", ["w_in"])
    late = [n for n in BIG if n != "w_in"]
    l_send, l_recv, l_srcs, l_lands, l_token = start_gather("gather_late_start", late)

    def far_w_in(after):
        srcs, lands = _exchange_wait("gather_w_in_wait", "gather", i_send, i_recv, i_srcs, i_lands, after)
        return _pair_fill("pair_fill_w_in", lands[0], srcs[0], core, others, chip)

    def late_weights(after):
        srcs, lands = _exchange_wait("gather_late_wait", "gather", l_send, l_recv, l_srcs, l_lands, after)
        out = {}
        for n, mine, land in zip(late, srcs, lands):
            out[n] = _to_full(n, _pair_fill("pair_fill_%s" % n, land, mine, core, others, chip))
        return out

    pending = []

    def on_grads(group):
        names = list(group)
        red = [_pair_reduce("pair_reduce_%s" % n, g if g.ndim == 3 else _to_shard_major(n, g), core)
               for n, g in group.items()]
        wires = [wire for _, wire in red]
        send, recv, srcs, lands, token = _exchange_start(
            "scatter_start_%s" % names[0], "scatter", wires, [(wire.shape, wire.dtype) for wire in wires])
        pending.append((names, [own for own, _ in red], send, recv, srcs, lands))
        return token

    small = {n: weights[n].reshape(1, -1) for n in SMALL}

    loss, grad_x, small_g = _local_step(x[0], loss_target[0], i_srcs[0], chip, others, far_w_in, small,
                                        late_weights, on_grads, deps0=[i_token, l_token])
    loss = lax.psum(loss, ("x", "y", "c"))

    grads = {}
    for names, owns, send, recv, srcs, lands in pending:
        _, got = _exchange_wait("scatter_wait_%s" % names[0], "scatter", send, recv, srcs, lands, grad_x)
        for n, own, by_chip in zip(names, owns, got):
            half = _sum4("chip_sum_%s" % n, own, by_chip, chip, others)
            grads[n] = _pair_share("pair_share_%s" % n, half, core)

    packed = jnp.concatenate([small_g[n].reshape(1, -1) for n in SMALL], axis=1)
    sizes = [small_g[n].size for n in SMALL]
    red = _all_reduce_small(packed.reshape(-1, LANES)).reshape(1, -1)
    off = 0
    for n, sz in zip(SMALL, sizes):
        grads[n] = red[:, off:off + sz]
        off += sz

    out_g, out_d, out_m, out_v = {}, {}, {}, {}
    for n in ALL_W:
        shape = weights[n].shape
        two_d = (shape[-2], shape[-1]) if n in BIG else (1, weights[n].size)
        g2 = grads[n].reshape(two_d)
        d, nm, nv = _adamw("adamw_%s" % n, weights[n].reshape(two_d), g2, moments_m[n].reshape(two_d),
                           moments_v[n].reshape(two_d))
        out_g[n], out_d[n], out_m[n], out_v[n] = (t.reshape(shape) for t in (g2, d, nm, nv))

    return (loss, grad_x[None], *[out_g[n] for n in ALL_W], *[out_d[n] for n in ALL_W],
            *[out_m[n] for n in ALL_W], *[out_v[n] for n in ALL_W])
```

```python
import functools
import math

import jax
import jax.numpy as jnp
from jax import lax
from jax.experimental import pallas as pl
from jax.experimental.pallas import tpu as pltpu

CDT = jnp.bfloat16
F32 = jnp.float32
EPS = 1e-6

ATT_GROUPS = ((128, 1), (512, 4), (2048, 16))
ATT_HPG = 4
ATT_HEADS = 12
HD = 128
BLK = 128
ATT_W = ATT_HEADS * HD
GW = ATT_HPG * HD
RET_HEADS = 4

ADAM_LR = 0.001
ADAM_B1 = 0.9
ADAM_B2 = 0.999
ADAM_EPS = 1e-08
ADAM_WD = 0.01
ADAM_STEP = 10

VMEM_LIMIT_BYTES = 56 * 1024 * 1024
MXU_DIM = 256
MESH = pl.DeviceIdType.MESH
HBM_SPEC = pl.BlockSpec(memory_space=pltpu.HBM)
VMEM_SPEC = pl.BlockSpec(memory_space=pltpu.VMEM)


def _params(n_axes):
    return pltpu.CompilerParams(dimension_semantics=("arbitrary",) * n_axes,
                                vmem_limit_bytes=VMEM_LIMIT_BYTES)


def _dot_nn(a, b):
    return jnp.dot(a, b, preferred_element_type=F32)


def _dot_nt(a, b):
    return lax.dot_general(a, b, (((1,), (1,)), ((), ())), preferred_element_type=F32)


def _dot_tn(a, b):
    return lax.dot_general(a, b, (((0,), (0,)), ((), ())), preferred_element_type=F32)


def _sigmoid(v):
    return 1.0 / (1.0 + jnp.exp(-v))


def _matmul(name, mode, a, b, *, tm, tn, tk, extras=(), outs, epilogue, deps=(), b_spec=None, n_cols=None,
            prefetch=(), alias_dep_to_out=None, j_outer=False):
    deps = [d for d in deps if d is not None]
    if mode == "tn":
        K, M = a.shape
    else:
        M, K = a.shape
    if b_spec is None:
        (N, K2) = b.shape if mode == "nt" else b.shape[::-1]
        assert K == K2, (name, a.shape, b.shape)
        if mode == "nt":
            b_spec = pl.BlockSpec((tn, tk), lambda i, j, k, *p: (j, k))
        else:
            b_spec = pl.BlockSpec((tk, tn), lambda i, j, k, *p: (k, j))
    else:
        N = n_cols
    assert M % tm == 0 and N % tn == 0 and K % tk == 0, (name, a.shape, b.shape)
    ni, nj, nk = M // tm, N // tn, K // tk
    if mode == "tn":
        a_spec = pl.BlockSpec((tk, tm), lambda i, j, k, *p: (k, i))
    else:
        a_spec = pl.BlockSpec((tm, tk), lambda i, j, k, *p: (i, k))
    dot = {"nn": _dot_nn, "nt": _dot_nt, "tn": _dot_tn}[mode]
    n_ex, n_out, n_dep, n_pre = len(extras), len(outs), len(deps), len(prefetch)
    grid = (ni, nj, nk)
    if j_outer:
        grid = (nj, ni, nk)

        def swapped(spec):
            return pl.BlockSpec(spec.block_shape, lambda j, i, k, *p: spec.index_map(i, j, k, *p))

        a_spec, b_spec = swapped(a_spec), swapped(b_spec)
        extras = [(e, swapped(s)) for e, s in extras]
        outs = [(o, swapped(s)) for o, s in outs]

    def body(*refs):
        refs = refs[n_pre:]
        a_ref, b_ref = refs[0], refs[1]
        ex = refs[2:2 + n_ex]
        out = refs[2 + n_ex + n_dep:2 + n_ex + n_dep + n_out]
        acc = refs[-1] if nk > 1 else None
        i = pl.program_id(1 if j_outer else 0)
        k = pl.program_id(2)
        if nk == 1:
            epilogue(dot(a_ref[...].astype(CDT), b_ref[...].astype(CDT)), ex, out, i)
            return

        @pl.when(k == 0)
        def _():
            acc[...] = jnp.zeros_like(acc)

        acc[...] += dot(a_ref[...].astype(CDT), b_ref[...].astype(CDT))

        @pl.when(k == nk - 1)
        def _():
            epilogue(acc[...], ex, out, i)

    grid_spec = pltpu.PrefetchScalarGridSpec(
        num_scalar_prefetch=n_pre, grid=grid,
        in_specs=[a_spec, b_spec] + [s for _, s in extras] + [pl.BlockSpec(memory_space=pl.ANY)] * n_dep,
        out_specs=[s for _, s in outs],
        scratch_shapes=[pltpu.VMEM((tm, tn), F32)] if nk > 1 else [])
    aliases = {}
    if alias_dep_to_out is not None:
        aliases = {n_pre + 2 + n_ex + alias_dep_to_out[0]: alias_dep_to_out[1]}
    res = pl.pallas_call(
        body, name=name, grid_spec=grid_spec, out_shape=[o for o, _ in outs], input_output_aliases=aliases,
        compiler_params=_params(3),
    )(*prefetch, a, b, *[e for e, _ in extras], *deps)
    return res


def _mn(tm, tn, col_off=0):
    return pl.BlockSpec((tm, tn), lambda i, j, k, *p: (i, j + col_off))


def _row(tn):
    return pl.BlockSpec((1, tn), lambda i, j, k, *p: (0, j))


def _ep_store(acc, ex, out, i):
    out[0][...] = acc.astype(out[0].dtype)


def _ep_resid(acc, ex, out, i):
    out[0][...] = ex[0][...] + acc


def _ep_up(acc, ex, out, i):
    out[0][...] = acc.astype(out[0].dtype)
    r = jnp.maximum(acc, 0.0)
    out[1][...] = (r * r).astype(out[1].dtype)


def _ep_down_loss(acc, ex, out, i, inv_d):
    diff = (ex[0][...] + acc) - ex[1][...]
    dx2 = diff * inv_d
    out[0][...] = dx2
    out[1][...] = dx2.astype(out[1].dtype)

    @pl.when(i == 0)
    def _():
        out[2][...] = jnp.zeros_like(out[2])

    out[2][...] += jnp.sum(diff * diff, axis=0, keepdims=True)


def _ep_dh(acc, ex, out, i):
    h = ex[0][...].astype(F32)
    out[0][...] = (acc * (2.0 * jnp.maximum(h, 0.0))).astype(out[0].dtype)


def _ep_rms_bwd(acc, ex, out, i):
    x = ex[0][...]
    g = ex[1][...]
    rstd = lax.rsqrt(jnp.mean(x * x, axis=-1, keepdims=True) + EPS)
    xh = x * rstd
    dxh = acc * g
    dx = ex[2][...] + rstd * (dxh - xh * jnp.mean(dxh * xh, axis=-1, keepdims=True))
    out[0][...] = dx
    out[1][...] = dx.astype(out[1].dtype)

    @pl.when(i == 0)
    def _():
        out[2][...] = jnp.zeros_like(out[2])

    out[2][...] += jnp.sum(acc * xh, axis=0, keepdims=True)


def _ep_gates(acc, ex, out, i):
    sa = _sigmoid(ex[0][...].astype(F32))
    sb = _sigmoid(ex[1][...].astype(F32))
    dpa = acc * sa
    dpb = acc * sb
    out[0][...] = dpa.astype(out[0].dtype)
    out[1][...] = dpb.astype(out[1].dtype)
    out[2][...] = (dpa * ex[2][...].astype(F32) * (1.0 - sa)).astype(out[2].dtype)
    out[3][...] = (dpb * ex[3][...].astype(F32) * (1.0 - sb)).astype(out[3].dtype)


def _sds(shape, dtype):
    return jax.ShapeDtypeStruct(shape, dtype)


def _rms_fwd(name, x, g, tm=512):
    S, D = x.shape

    def body(x_ref, g_ref, o_ref):
        xv = x_ref[...]
        rstd = lax.rsqrt(jnp.mean(xv * xv, axis=-1, keepdims=True) + EPS)
        o_ref[...] = (xv * rstd * g_ref[...]).astype(o_ref.dtype)

    return pl.pallas_call(
        body, name=name, grid=(S // tm,),
        in_specs=[pl.BlockSpec((tm, D), lambda i: (i, 0)), pl.BlockSpec((1, D), lambda i: (0, 0))],
        out_specs=pl.BlockSpec((tm, D), lambda i: (i, 0)),
        out_shape=_sds((S, D), CDT), compiler_params=_params(1))(x, g)


def _rm_shape(S, d, width):
    return (S, width) if d == 1 else (d, S // d, width)


def _rm_spec(tm, d, width):
    if d == 1:
        return pl.BlockSpec((tm, width), lambda i: (i, 0))
    return pl.BlockSpec((d, tm // d, width), lambda i: (0, i, 0))


def _rm_put(dst_ref, cols, buf_ref, d):
    if d == 1:
        dst_ref[:, cols] = buf_ref[...].astype(dst_ref.dtype)
        return
    m = buf_ref.shape[0] // d
    for r in range(d):
        dst_ref[r, :, cols] = buf_ref[pl.ds(r, m, stride=d), :].astype(dst_ref.dtype)


def _rm_reader(buf_ref, src_ref, d):
    if d == 1:
        return lambda s: src_ref[:, s * HD:(s + 1) * HD].astype(F32)
    m = buf_ref.shape[1] // d
    for s in range(buf_ref.shape[0]):
        for r in range(d):
            buf_ref.at[s][pl.ds(r, m, stride=d), :] = src_ref[r, :, s * HD:(s + 1) * HD].astype(F32)
    return lambda s: buf_ref[s]


def _qknorm_fwd(proj, gqk, tm=512):
    S = proj.shape[0]
    W = 2 * ATT_W
    dil = [d for _, d in ATT_GROUPS]

    def body(p_ref, g_ref, o0, o1, o2, buf):
        outs = (o0, o1, o2)
        for hd in range(3 * ATT_HEADS):
            which, head = hd // ATT_HEADS, hd % ATT_HEADS
            grp, slot = head // ATT_HPG, head % ATT_HPG
            v = p_ref[:, hd * HD:(hd + 1) * HD].astype(F32)
            if which < 2:
                rstd = lax.rsqrt(jnp.mean(v * v, axis=-1, keepdims=True) + EPS)
                v = v * rstd * g_ref[:, hd * HD:(hd + 1) * HD]
            buf[...] = v
            _rm_put(outs[grp], slice(which * GW + slot * HD, which * GW + (slot + 1) * HD), buf, dil[grp])

    return pl.pallas_call(
        body, name="qknorm_fwd", grid=(S // tm,),
        in_specs=[pl.BlockSpec((tm, 3 * ATT_W), lambda i: (i, 0)), pl.BlockSpec((1, W), lambda i: (0, 0))],
        out_specs=[_rm_spec(tm, d, 3 * GW) for d in dil],
        out_shape=[_sds(_rm_shape(S, d, 3 * GW), CDT) for d in dil],
        scratch_shapes=[pltpu.VMEM((tm, HD), F32)],
        compiler_params=_params(1))(proj, gqk)


def _qknorm_bwd(proj, gqk, dqs, dks, dvs, tm=256):
    S = proj.shape[0]
    W = 2 * ATT_W
    dil = [d for _, d in ATT_GROUPS]

    def body(p_ref, g_ref, *refs):
        ins = refs[0:9]
        o_ref, dg_ref = refs[9], refs[10]
        bufs = refs[11:20]
        i = pl.program_id(0)

        @pl.when(i == 0)
        def _():
            dg_ref[...] = jnp.zeros_like(dg_ref)

        nat = [_rm_reader(bufs[j], ins[j], dil[j % 3]) for j in range(9)]
        dq_get, dk_get, dv_get = nat[0:3], nat[3:6], nat[6:9]
        for hd in range(2 * ATT_HEADS):
            sl = slice(hd * HD, (hd + 1) * HD)
            head = hd % ATT_HEADS
            grp, slot = head // ATT_HPG, head % ATT_HPG
            dn = (dq_get if hd < ATT_HEADS else dk_get)[grp](slot)
            v = p_ref[:, sl].astype(F32)
            rstd = lax.rsqrt(jnp.mean(v * v, axis=-1, keepdims=True) + EPS)
            vh = v * rstd
            dg_ref[:, sl] += jnp.sum(dn * vh, axis=0, keepdims=True)
            dvh = dn * g_ref[:, sl]
            o_ref[:, sl] = (rstd * (dvh - vh * jnp.mean(dvh * vh, axis=-1, keepdims=True))).astype(o_ref.dtype)
        for head in range(ATT_HEADS):
            grp, slot = head // ATT_HPG, head % ATT_HPG
            o_ref[:, W + head * HD:W + (head + 1) * HD] = dv_get[grp](slot).astype(o_ref.dtype)

    return pl.pallas_call(
        body, name="qknorm_bwd", grid=(S // tm,),
        in_specs=[pl.BlockSpec((tm, W), lambda i: (i, 0)), pl.BlockSpec((1, W), lambda i: (0, 0))]
        + [_rm_spec(tm, d, GW) for d in dil] * 3,
        out_specs=[pl.BlockSpec((tm, 3 * ATT_W), lambda i: (i, 0)), pl.BlockSpec((1, W), lambda i: (0, 0))],
        out_shape=[_sds((S, 3 * ATT_W), CDT), _sds((1, W), F32)],
        scratch_shapes=[pltpu.VMEM((ATT_HPG, tm, HD), F32)] * 9,
        compiler_params=_params(1))(proj, gqk, *dqs, *dks, *dvs)


def _att_mask(n):
    qi = lax.broadcasted_iota(jnp.int32, (BLK, 2 * BLK), 0)
    kj = lax.broadcasted_iota(jnp.int32, (BLK, 2 * BLK), 1)
    dist = BLK + qi - kj
    valid = (dist >= 0) & (dist <= BLK) & ((kj >= BLK) | (n > 0))
    return valid, dist.astype(F32)


def _att_slopes(grp):
    return [2.0 ** (-8.0 * (grp * ATT_HPG + hh + 1) / ATT_HEADS) for hh in range(ATT_HPG)]


def _att_spec(d, row_fn, col=0):
    if d == 1:
        return pl.BlockSpec((BLK, GW), lambda r, n: (row_fn(n), col))
    return pl.BlockSpec((None, BLK, GW), lambda r, n: (r, row_fn(n), col))


def _att_qkv_specs(d, nb):
    last = nb - 1

    def cur(n):
        return jnp.minimum(n, last)

    def prev(n):
        return jnp.maximum(jnp.minimum(n, last) - 1, 0)

    return [_att_spec(d, cur, 0), _att_spec(d, prev, 1), _att_spec(d, cur, 1), _att_spec(d, prev, 2),
            _att_spec(d, cur, 2)]


def _att_fwd(grp, S, qkv):
    _, d = ATT_GROUPS[grp]
    L = S // d
    nb = L // BLK
    slopes = _att_slopes(grp)
    scale = HD ** -0.5

    def body(q_ref, kp_ref, kc_ref, vp_ref, vc_ref, o_ref, l_ref):
        n = pl.program_id(1)
        valid, distf = _att_mask(n)
        for hh in range(ATT_HPG):
            sl = slice(hh * HD, (hh + 1) * HD)
            k = jnp.concatenate([kp_ref[:, sl], kc_ref[:, sl]], axis=0)
            v = jnp.concatenate([vp_ref[:, sl], vc_ref[:, sl]], axis=0)
            s = _dot_nt(q_ref[:, sl], k) * scale + (-slopes[hh] * d) * distf
            s = jnp.where(valid, s, -1e30)
            m = jnp.max(s, axis=-1, keepdims=True)
            p = jnp.exp(s - m)
            den = jnp.sum(p, axis=-1, keepdims=True)
            o_ref[:, sl] = _dot_nn(p.astype(CDT), v) / den
            l_ref[:, sl] = jnp.broadcast_to(m + jnp.log(den), (BLK, HD))

    out_spec = _att_spec(d, lambda n: n)
    return pl.pallas_call(
        body, name="att_fwd_g%d" % grp, grid=(d, nb),
        in_specs=_att_qkv_specs(d, nb),
        out_specs=[out_spec, out_spec],
        out_shape=[_sds(_rm_shape(S, d, GW), F32)] * 2,
        compiler_params=_params(2),
    )(qkv, qkv, qkv, qkv, qkv)


def _att_bwd(grp, S, qkv, lse, do_g, c_g):
    _, d = ATT_GROUPS[grp]
    L = S // d
    nb = L // BLK
    slopes = _att_slopes(grp)
    scale = HD ** -0.5
    last = nb - 1

    def body(q_ref, kp_ref, kc_ref, vp_ref, vc_ref, l_ref, do_ref, c_ref, dq_ref, dk_ref, dv_ref, ck, cv):
        n = pl.program_id(1)

        @pl.when(n == 0)
        def _():
            ck[...] = jnp.zeros_like(ck)
            cv[...] = jnp.zeros_like(cv)

        @pl.when(n < nb)
        def _():
            valid, distf = _att_mask(n)
            for hh in range(ATT_HPG):
                sl = slice(hh * HD, (hh + 1) * HD)
                q = q_ref[:, sl]
                k = jnp.concatenate([kp_ref[:, sl], kc_ref[:, sl]], axis=0)
                v = jnp.concatenate([vp_ref[:, sl], vc_ref[:, sl]], axis=0)
                do = do_ref[:, sl]
                s = _dot_nt(q, k) * scale + (-slopes[hh] * d) * distf
                p = jnp.where(valid, jnp.exp(s - l_ref[:, sl][:, 0:1]), 0.0)
                dp = _dot_nt(do, v)
                ds = (p * (dp + c_ref[:, sl][:, 0:1]) * scale).astype(CDT)
                dq_ref[:, sl] = _dot_nn(ds, k)
                dk = _dot_tn(ds, q)
                dv = _dot_tn(p.astype(CDT), do)
                dk_ref[:, sl] = ck[:, sl] + dk[0:BLK]
                dv_ref[:, sl] = cv[:, sl] + dv[0:BLK]
                ck[:, sl] = dk[BLK:2 * BLK]
                cv[:, sl] = dv[BLK:2 * BLK]

        @pl.when(n == nb)
        def _():
            dk_ref[...] = ck[...]
            dv_ref[...] = cv[...]

    blk = (BLK, GW)
    at_q = _att_spec(d, lambda n: jnp.minimum(n, last))
    behind = _att_spec(d, lambda n: jnp.maximum(n - 1, 0))
    return pl.pallas_call(
        body, name="att_bwd_g%d" % grp, grid=(d, nb + 1),
        in_specs=_att_qkv_specs(d, nb) + [at_q, at_q, at_q],
        out_specs=[at_q, behind, behind],
        out_shape=[_sds(_rm_shape(S, d, GW), F32)] * 3,
        scratch_shapes=[pltpu.VMEM(blk, F32), pltpu.VMEM(blk, F32)],
        compiler_params=_params(2),
    )(qkv, qkv, qkv, qkv, qkv, lse, do_g, c_g)


def _mix_alpha(l0, l1, l2):
    mx = jnp.maximum(jnp.maximum(l0, l1), l2)
    e = [jnp.exp(l0 - mx), jnp.exp(l1 - mx), jnp.exp(l2 - mx)]
    tot = e[0] + e[1] + e[2]
    return [ei / tot for ei in e]


def _mix_fwd(S, os_, ls_, tm=512):
    dil = [d for _, d in ATT_GROUPS]

    def body(*refs):
        out, bufs = refs[6], refs[7:13]
        get = [_rm_reader(bufs[j], refs[j], dil[j % 3]) for j in range(6)]
        for s in range(ATT_HPG):
            al = _mix_alpha(*[get[3 + g](s) for g in range(3)])
            mixed = al[0] * get[0](s) + al[1] * get[1](s) + al[2] * get[2](s)
            out[:, s * HD:(s + 1) * HD] = mixed.astype(out.dtype)

    specs = [_rm_spec(tm, d, GW) for d in dil]
    return pl.pallas_call(
        body, name="mix_fwd", grid=(S // tm,), in_specs=specs * 2, out_specs=pl.BlockSpec((tm, GW), lambda i: (i, 0)),
        out_shape=_sds((S, GW), CDT), scratch_shapes=[pltpu.VMEM((ATT_HPG, tm, HD), F32)] * 6,
        compiler_params=_params(1))(*os_, *ls_)


def _mix_bwd(S, os_, ls_, do_a, tm=512):
    dil = [d for _, d in ATT_GROUPS]

    def body(*refs):
        d_ref, outs, bufs, tmp = refs[6], refs[7:13], refs[13:19], refs[19]
        get = [_rm_reader(bufs[j], refs[j], dil[j % 3]) for j in range(6)]
        for s in range(ATT_HPG):
            cols = slice(s * HD, (s + 1) * HD)
            al = _mix_alpha(*[get[3 + g](s) for g in range(3)])
            dv = d_ref[:, cols]
            o_a = al[0] * get[0](s) + al[1] * get[1](s) + al[2] * get[2](s)
            dsum = jnp.sum(dv * o_a, axis=-1, keepdims=True)
            for g in range(3):
                tmp[...] = al[g] * dv
                _rm_put(outs[g], cols, tmp, dil[g])
                tmp[...] = -(al[g] * dsum)
                _rm_put(outs[3 + g], cols, tmp, dil[g])

    specs = [_rm_spec(tm, d, GW) for d in dil]
    res = pl.pallas_call(
        body, name="mix_bwd", grid=(S // tm,), in_specs=specs * 2 + [pl.BlockSpec((tm, GW), lambda i: (i, 0))],
        out_specs=specs * 2,
        out_shape=[_sds(_rm_shape(S, d, GW), CDT) for d in dil] + [_sds(_rm_shape(S, d, GW), F32) for d in dil],
        scratch_shapes=[pltpu.VMEM((ATT_HPG, tm, HD), F32)] * 6 + [pltpu.VMEM((tm, HD), F32)],
        compiler_params=_params(1))(*os_, *ls_, do_a)
    return res[:3], res[3:]


def _ret_tables(dk):
    H, C = RET_HEADS, BLK
    log_g = jnp.log(1.0 - 2.0 ** (-5.0 - jnp.arange(H, dtype=F32)))
    idx = jnp.arange(C, dtype=F32)
    diff = idx[:, None] - idx[None, :]
    decay = jnp.where(diff >= 0, jnp.exp(log_g[:, None, None] * jnp.maximum(diff, 0.0)), 0.0)
    xi = jnp.exp(log_g[:, None] * (idx[None, :] + 1.0))
    zeta = jnp.exp(log_g[:, None] * (C - 1.0 - idx[None, :])) * (dk ** -0.5)
    g_chunk = jnp.exp(log_g * C)
    bc = lambda t: jnp.broadcast_to(t[:, :, None], (H, C, C))
    return decay, bc(xi), bc(zeta), jnp.broadcast_to(g_chunk[:, None, None], (H, 8, C))


def _gn_fwd(o, g, b):
    mu = jnp.mean(o, axis=-1, keepdims=True)
    xc = o - mu
    rstd = lax.rsqrt(jnp.mean(xc * xc, axis=-1, keepdims=True) + EPS)
    yh = xc * rstd
    return yh, rstd, yh * g + b


def _ret_specs(dk, dv, order):
    H = RET_HEADS
    qk_w, v_w = H * dk, H * dv
    off_q = 3 * ATT_W
    off_k, off_v, off_g = off_q + qk_w, off_q + 2 * qk_w, off_q + 2 * qk_w + v_w
    assert 2 * dk == dv and all(off % dv == 0 for off in (off_q, off_k, off_v, off_g))

    def col(off, j):
        return pl.BlockSpec((BLK, dv), lambda i: (order(i), off // dv + j))

    tab = pl.BlockSpec((H, BLK, BLK), lambda i: (0, 0, 0))
    return ([col(off_q, j) for j in range(H // 2)] + [col(off_k, j) for j in range(H // 2)]
            + [col(off_v, j) for j in range(H)] + [col(off_g, j) for j in range(H)]
            + [tab, tab, tab, pl.BlockSpec((H, 8, BLK), lambda i: (0, 0, 0))])


def _ret_heads(refs, dk):
    H = RET_HEADS
    q_refs, k_refs = refs[0:H // 2], refs[H // 2:H]
    v_refs, gr_refs = refs[H:2 * H], refs[2 * H:3 * H]

    def head(h):
        cols = slice((h % 2) * dk, (h % 2 + 1) * dk)
        return q_refs[h // 2][:, cols], k_refs[h // 2][:, cols], v_refs[h][...], gr_refs[h][...]

    return head, refs[3 * H:3 * H + 4]


def _ret_fwd(proj, gn_g, gn_b, dk, dv):
    S = proj.shape[0]
    N = S // BLK
    H = RET_HEADS
    kscale = dk ** -0.5
    n_in = 3 * H + 4

    def body(*refs):
        head, (dec_ref, xi_ref, zeta_ref, gc_ref) = _ret_heads(refs, dk)
        g_ref, b_ref, opre_ref, or_ref, st_ref, state = refs[n_in:n_in + 6]
        n = pl.program_id(0)

        @pl.when(n == 0)
        def _():
            state[...] = jnp.zeros_like(state)

        for h in range(H):
            vs = slice(h * dv, (h + 1) * dv)
            q, k, v, gr = head(h)
            s = _dot_nt(q, k) * kscale * dec_ref[h]
            st = state[h]
            st_c = st.astype(CDT)
            st_ref[h] = st_c
            o = _dot_nn(s.astype(CDT), v) + _dot_nn(q, st_c) * xi_ref[h][:, 0:1]
            kz = (k.astype(F32) * zeta_ref[h][:, 0:1]).astype(CDT)
            state[h] = st * gc_ref[h][0:1, 0:1] + _dot_tn(kz, v)
            opre_ref[:, vs] = o
            _, _, y = _gn_fwd(o, g_ref[:, vs], b_ref[:, vs])
            gr = gr.astype(F32)
            or_ref[:, vs] = (y * (gr * _sigmoid(gr))).astype(or_ref.dtype)

    v_w = H * dv
    row = pl.BlockSpec((1, v_w), lambda i: (0, 0))
    tile = pl.BlockSpec((BLK, v_w), lambda i: (i, 0))
    return pl.pallas_call(
        body, name="ret_fwd", grid=(N,),
        in_specs=_ret_specs(dk, dv, lambda i: i) + [row, row],
        out_specs=[tile, tile, pl.BlockSpec((None, H, dk, dv), lambda i: (i, 0, 0, 0))],
        out_shape=[_sds((S, v_w), F32), _sds((S, v_w), CDT), _sds((N, H, dk, dv), CDT)],
        scratch_shapes=[pltpu.VMEM((H, dk, dv), F32)],
        compiler_params=_params(1),
    )(*[proj] * (3 * H), *_ret_tables(dk), gn_g, gn_b)


def _ret_bwd(proj, gn_g, gn_b, o_pre, states, d_or, dk, dv):
    S = proj.shape[0]
    N = S // BLK
    H = RET_HEADS
    qk_w, v_w = H * dk, H * dv
    kscale = dk ** -0.5
    n_in = 3 * H + 4

    def body(*refs):
        head, (dec_ref, xi_ref, zeta_ref, gc_ref) = _ret_heads(refs, dk)
        g_ref, b_ref, opre_ref, st_ref, dor_ref, out_ref, dg_ref, db_ref, dstate = refs[n_in:n_in + 9]
        i = pl.program_id(0)

        @pl.when(i == 0)
        def _():
            dstate[...] = jnp.zeros_like(dstate)
            dg_ref[...] = jnp.zeros_like(dg_ref)
            db_ref[...] = jnp.zeros_like(db_ref)

        for h in range(H):
            vs = slice(h * dv, (h + 1) * dv)
            q, k, v, gr = head(h)
            decay, xi, zeta_s = dec_ref[h], xi_ref[h][:, 0:1], zeta_ref[h][:, 0:1]
            gr = gr.astype(F32)
            sg = _sigmoid(gr)
            gain = g_ref[:, vs]
            yh, rstd, y = _gn_fwd(opre_ref[:, vs], gain, b_ref[:, vs])
            d_or_v = dor_ref[:, vs]
            dy = d_or_v * (gr * sg)
            out_ref[:, 2 * qk_w + v_w + h * dv:2 * qk_w + v_w + (h + 1) * dv] = (
                d_or_v * y * (sg * (1.0 + gr * (1.0 - sg)))).astype(out_ref.dtype)
            dg_ref[:, vs] += jnp.sum(dy * yh, axis=0, keepdims=True)
            db_ref[:, vs] += jnp.sum(dy, axis=0, keepdims=True)
            dyh = dy * gain
            do = rstd * (dyh - jnp.mean(dyh, axis=-1, keepdims=True)
                         - yh * jnp.mean(dyh * yh, axis=-1, keepdims=True))
            do_c = do.astype(CDT)
            dox = (do * xi).astype(CDT)
            a_c = (_dot_nt(q, k) * kscale * decay).astype(CDT)
            g_c = (_dot_nt(do_c, v) * decay).astype(CDT)
            dsn = dstate[h]
            dsn_c = dsn.astype(CDT)
            kz = (k.astype(F32) * zeta_s).astype(CDT)
            dq = _dot_nn(g_c, k) * kscale + _dot_nt(dox, st_ref[h])
            dkk = _dot_tn(g_c, q) * kscale + _dot_nt(v, dsn_c) * zeta_s
            dvv = _dot_tn(a_c, do_c) + _dot_nn(kz, dsn_c)
            dstate[h] = dsn * gc_ref[h][0:1, 0:1] + _dot_tn(q, dox)
            out_ref[:, h * dk:(h + 1) * dk] = dq.astype(out_ref.dtype)
            out_ref[:, qk_w + h * dk:qk_w + (h + 1) * dk] = dkk.astype(out_ref.dtype)
            out_ref[:, 2 * qk_w + h * dv:2 * qk_w + (h + 1) * dv] = dvv.astype(out_ref.dtype)

    rev = lambda i: N - 1 - i
    row = pl.BlockSpec((1, v_w), lambda i: (0, 0))
    tile = pl.BlockSpec((BLK, v_w), lambda i: (rev(i), 0))
    out_w = 2 * qk_w + 2 * v_w
    return pl.pallas_call(
        body, name="ret_bwd", grid=(N,),
        in_specs=_ret_specs(dk, dv, rev) + [row, row, tile,
                 pl.BlockSpec((None, H, dk, dv), lambda i: (rev(i), 0, 0, 0)), tile],
        out_specs=[pl.BlockSpec((BLK, out_w), lambda i: (rev(i), 0)), row, row],
        out_shape=[_sds((S, out_w), CDT), _sds((1, v_w), F32), _sds((1, v_w), F32)],
        scratch_shapes=[pltpu.VMEM((H, dk, dv), F32)],
        compiler_params=_params(1),
    )(*[proj] * (3 * H), *_ret_tables(dk), gn_g, gn_b, o_pre, states, d_or)


def _merge_fwd(o_a, o_r, wa, wb, proj, d_model, tm=512, tn=512):
    S, in_w = proj.shape
    off_a, off_b = in_w - 2 * d_model, in_w - d_model
    assert off_a % tn == 0 and off_b % tn == 0

    def body(oa_ref, or_ref, wa_ref, wb_ref, ga_ref, gb_ref, y_ref, pa_ref, pb_ref):
        pa = _dot_nn(oa_ref[...], wa_ref[...])
        pb = _dot_nn(or_ref[...], wb_ref[...])
        y = _sigmoid(ga_ref[...].astype(F32)) * pa + _sigmoid(gb_ref[...].astype(F32)) * pb
        y_ref[...] = y.astype(y_ref.dtype)
        pa_ref[...] = pa.astype(pa_ref.dtype)
        pb_ref[...] = pb.astype(pb_ref.dtype)

    ka, kb = o_a.shape[1], o_r.shape[1]
    out = pl.BlockSpec((tm, tn), lambda i, j: (i, j))
    return pl.pallas_call(
        body, name="merge_fwd", grid=(S // tm, d_model // tn),
        in_specs=[pl.BlockSpec((tm, ka), lambda i, j: (i, 0)), pl.BlockSpec((tm, kb), lambda i, j: (i, 0)),
                  pl.BlockSpec((ka, tn), lambda i, j: (0, j)), pl.BlockSpec((kb, tn), lambda i, j: (0, j)),
                  pl.BlockSpec((tm, tn), lambda i, j: (i, off_a // tn + j)),
                  pl.BlockSpec((tm, tn), lambda i, j: (i, off_b // tn + j))],
        out_specs=[out, out, out], out_shape=[_sds((S, d_model), CDT)] * 3,
        compiler_params=_params(2))(o_a, o_r, wa, wb, proj, proj)


def _local_step(x, target, w_in_mine, chip, others, far_w_in, small, late_weights, on_grads, deps0=()):
    S, D = x.shape
    ns_in = w_in_mine.shape[1]
    in_w = N_CHIPS * ns_in
    d_ff = 4 * D
    ret_v_w = 2 * D
    dv = ret_v_w // RET_HEADS
    dk = (in_w - 3 * ATT_W - 2 * ret_v_w - 2 * D) // (2 * RET_HEADS)
    gqk = jnp.concatenate([small["q_norm_g"].reshape(1, ATT_W), small["k_norm_g"].reshape(1, ATT_W)], axis=1)
    g1, g2 = small["norm1_g"], small["norm2_g"]
    gn_g, gn_b = small["ret_gn_g"], small["ret_gn_b"]

    xn = _rms_fwd("rms1_fwd", x, g1)
    proj_sds = _sds((S, in_w), CDT)
    (proj_mine,) = _matmul(
        "in_proj_mine", "nn", xn, w_in_mine, tm=512, tn=ns_in, tk=D, prefetch=[chip],
        outs=[(proj_sds, pl.BlockSpec((512, ns_in), lambda i, j, k, c: (i, c[0])))], epilogue=_ep_store, deps=deps0)
    w_in = far_w_in(proj_mine)
    (proj,) = _matmul(
        "in_proj_far", "nn", xn, w_in, tm=512, tn=ns_in, tk=D, prefetch=[others], n_cols=(N_CHIPS - 1) * ns_in,
        b_spec=pl.BlockSpec((None, D, ns_in), lambda i, j, k, o: (o[j], 0, 0)),
        outs=[(proj_sds, pl.BlockSpec((512, ns_in), lambda i, j, k, o: (i, o[j])))], epilogue=_ep_store,
        deps=[proj_mine], alias_dep_to_out=(0, 0), j_outer=True)
    qkv = _qknorm_fwd(proj, gqk)
    att = [_att_fwd(g, S, qkv[g]) for g in range(3)]
    os_, ls_ = [a[0] for a in att], [a[1] for a in att]
    o_a = _mix_fwd(S, os_, ls_)
    o_pre, o_r, states = _ret_fwd(proj, gn_g, gn_b, dk, dv)
    w = late_weights(o_r)
    y, pa, pb = _merge_fwd(o_a, o_r, w["w_proj_a"], w["w_proj_b"], proj, D)
    (x1,) = _matmul("out_proj", "nn", y, w["w_out"], tm=512, tn=D, tk=D,
                    extras=[(x, _mn(512, D))], outs=[(_sds((S, D), F32), _mn(512, D))], epilogue=_ep_resid)
    xn2 = _rms_fwd("rms2_fwd", x1, g2)
    hid, act = _matmul("mlp_up", "nn", xn2, w["w_up"], tm=512, tn=2048, tk=D, j_outer=True,
                       outs=[(_sds((S, d_ff), CDT), _mn(512, 2048))] * 2, epilogue=_ep_up)
    dx2, dx2c, loss_row = _matmul(
        "mlp_down_loss", "nn", act, w["w_down"], tm=512, tn=D, tk=d_ff,
        extras=[(x1, _mn(512, D)), (target, _mn(512, D))],
        outs=[(_sds((S, D), F32), _mn(512, D)), (_sds((S, D), CDT), _mn(512, D)), (_sds((1, D), F32), _row(D))],
        epilogue=functools.partial(_ep_down_loss, inv_d=1.0 / D))
    loss = 0.5 * jnp.sum(loss_row) / D

    (dh,) = _matmul("d_hidden", "nt", dx2c, w["w_down"], tm=512, tn=2048, tk=D, j_outer=True,
                    extras=[(hid, _mn(512, 2048))], outs=[(_sds((S, d_ff), CDT), _mn(512, 2048))], epilogue=_ep_dh)
    (gw_down,) = _matmul("dw_down", "tn", act, dx2c, tm=1024, tn=D, tk=1024,
                         outs=[(_sds((d_ff, D), F32), _mn(1024, D))], epilogue=_ep_store)
    (gw_up,) = _matmul("dw_up", "tn", xn2, dh, tm=D, tn=1024, tk=1024,
                       outs=[(_sds((D, d_ff), F32), _mn(D, 1024))], epilogue=_ep_store)
    tok = on_grads({"w_down": gw_down, "w_up": gw_up})
    dx1, dx1c, dg2 = _matmul(
        "d_x1", "nt", dh, w["w_up"], tm=512, tn=D, tk=d_ff,
        extras=[(x1, _mn(512, D)), (g2, _row(D)), (dx2, _mn(512, D))],
        outs=[(_sds((S, D), F32), _mn(512, D)), (_sds((S, D), CDT), _mn(512, D)), (_sds((1, D), F32), _row(D))],
        epilogue=_ep_rms_bwd, deps=[tok])

    gt = 512
    assert (in_w - 2 * D) % gt == 0
    off_a, off_b = (in_w - 2 * D) // gt, (in_w - D) // gt
    dpa, dpb, dga, dgb = _matmul(
        "d_gates", "nt", dx1c, w["w_out"], tm=512, tn=gt, tk=D,
        extras=[(proj, _mn(512, gt, off_a)), (proj, _mn(512, gt, off_b)), (pa, _mn(512, gt)), (pb, _mn(512, gt))],
        outs=[(_sds((S, D), CDT), _mn(512, gt))] * 4, epilogue=_ep_gates)
    (gw_out,) = _matmul("dw_out", "tn", y, dx1c, tm=D, tn=D, tk=1024,
                        outs=[(_sds((D, D), F32), _mn(D, D))], epilogue=_ep_store)
    (gw_pa,) = _matmul("dw_proj_a", "tn", o_a, dpa, tm=GW, tn=D, tk=1024,
                       outs=[(_sds((GW, D), F32), _mn(GW, D))], epilogue=_ep_store)
    (gw_pb,) = _matmul("dw_proj_b", "tn", o_r, dpb, tm=1024, tn=D, tk=1024,
                       outs=[(_sds((ret_v_w, D), F32), _mn(1024, D))], epilogue=_ep_store)
    (do_a,) = _matmul("d_o_a", "nt", dpa, w["w_proj_a"], tm=1024, tn=GW, tk=D,
                      outs=[(_sds((S, GW), F32), _mn(1024, GW))], epilogue=_ep_store)
    tok = on_grads({"w_out": gw_out, "w_proj_a": gw_pa, "w_proj_b": gw_pb})
    (d_or,) = _matmul("d_o_r", "nt", dpb, w["w_proj_b"], tm=512, tn=ret_v_w, tk=D,
                      outs=[(_sds((S, ret_v_w), F32), _mn(512, ret_v_w))], epilogue=_ep_store, deps=[tok])

    d_ret, dgn_g, dgn_b = _ret_bwd(proj, gn_g, gn_b, o_pre, states, d_or, dk, dv)
    do_gs, c_gs = _mix_bwd(S, os_, ls_, do_a)
    datt_parts = [_att_bwd(g, S, qkv[g], ls_[g], do_gs[g], c_gs[g]) for g in range(3)]
    d_att, dgqk = _qknorm_bwd(proj, gqk, [p[0] for p in datt_parts], [p[1] for p in datt_parts],
                              [p[2] for p in datt_parts])
    dproj = jnp.concatenate([d_att, d_ret, dga, dgb], axis=1)

    (gw_in,) = _matmul(
        "dw_in", "tn", xn, dproj, tm=512, tn=ns_in, tk=1024,
        outs=[(_sds((N_CHIPS, D, ns_in), F32), pl.BlockSpec((None, 512, ns_in), lambda i, j, k: (j, i, 0)))],
        epilogue=_ep_store)
    tok = on_grads({"w_in": gw_in})
    grad_x, _, dg1 = _matmul(
        "d_x", "nt", dproj, w_in, tm=512, tn=D, tk=ns_in, n_cols=D,
        b_spec=pl.BlockSpec((None, D, ns_in), lambda i, j, k: (k, 0, 0)),
        extras=[(x, _mn(512, D)), (g1, _row(D)), (dx1, _mn(512, D))],
        outs=[(_sds((S, D), F32), _mn(512, D)), (_sds((S, D), CDT), _mn(512, D)), (_sds((1, D), F32), _row(D))],
        epilogue=_ep_rms_bwd, deps=[tok])

    smallg = {"norm1_g": dg1, "q_norm_g": dgqk[:, :ATT_W], "k_norm_g": dgqk[:, ATT_W:],
              "ret_gn_g": dgn_g, "ret_gn_b": dgn_b, "norm2_g": dg2}
    return loss, grad_x, smallg


N_CHIPS = 4
N_DEV = 8


def _place():
    x, y, c = lax.axis_index("x"), lax.axis_index("y"), lax.axis_index("c")
    return x, y, c


def _other_chips(x, y):
    out = []
    for fx, fy in ((1, 0), (0, 1), (1, 1)):
        px = 1 - x if fx else x
        py = 1 - y if fy else y
        out.append(((px, py), 2 * px + py))
    return out


SEM_SPEC = pl.BlockSpec(memory_space=pltpu.SEMAPHORE)
ANY_SPEC = pl.BlockSpec(memory_space=pl.ANY)
EFFECT = pltpu.SideEffectType.DATAFLOW_SIDE_EFFECTING


def _ici_copies(kind, srcs, lands, send, recv):
    x, y, c = _place()
    me = 2 * x + y
    out = []
    for w, (s, l) in enumerate(zip(srcs, lands)):
        for j, ((px, py), pidx) in enumerate(_other_chips(x, y)):
            if kind == "gather":
                half = s.shape[0] // 2
                rows = pl.ds(c * half, half)
                src, dst_there, dst_here = s.at[rows, :], l.at[me, rows, :], l.at[pidx, rows, :]
            else:
                src, dst_there, dst_here = s.at[pidx], l.at[me], l.at[pidx]
            out.append((src, dst_there, dst_here, send.at[3 * w + j], recv.at[3 * w + j], (px, py, c)))
    return out


def _exchange_start(name, kind, srcs, land_shapes):
    n = len(srcs)

    def body(*refs):
        src_refs, land_refs = refs[:n], refs[n:2 * n]
        send, recv = refs[2 * n], refs[2 * n + 1]
        token = refs[-1]
        for src, dst, _, ss, rs, dev in _ici_copies(kind, src_refs, land_refs, send, recv):
            pltpu.make_async_remote_copy(src_ref=src, dst_ref=dst, send_sem=ss, recv_sem=rs, device_id=dev,
                                         device_id_type=MESH).start()
        token[...] = jnp.zeros_like(token)

    thru = [pltpu.HBM(s.shape, s.dtype) for s in srcs] + [pltpu.HBM(shape, dtype) for shape, dtype in land_shapes]
    res = pl.pallas_call(
        body, name=name,
        out_shape=(pltpu.SemaphoreType.DMA((3 * n,)), pltpu.SemaphoreType.DMA((3 * n,)), *thru, _sds((8, LANES), F32)),
        in_specs=[HBM_SPEC] * (2 * n), out_specs=(SEM_SPEC, SEM_SPEC, *[HBM_SPEC] * (2 * n), VMEM_SPEC),
        input_output_aliases={i: 2 + i for i in range(2 * n)},
        compiler_params=pltpu.CompilerParams(has_side_effects=EFFECT),
    )(*[pltpu.with_memory_space_constraint(s, pltpu.HBM) for s in srcs],
      *[pltpu.with_memory_space_constraint(lax.empty(shape, dtype), pltpu.HBM) for shape, dtype in land_shapes])
    return res[0], res[1], list(res[2:2 + n]), list(res[2 + n:2 + 2 * n]), res[-1]


def _exchange_wait(name, kind, send, recv, srcs, lands, after):
    n = len(srcs)

    def body(*refs):
        src_refs, land_refs = refs[:n], refs[n:2 * n]
        send_ref, recv_ref = refs[2 * n], refs[2 * n + 1]
        for src, _, dst, ss, rs, dev in _ici_copies(kind, src_refs, land_refs, send_ref, recv_ref):
            cp = pltpu.make_async_remote_copy(src_ref=src, dst_ref=dst, send_sem=ss, recv_sem=rs, device_id=dev,
                                              device_id_type=MESH)
            cp.wait_send()
            cp.wait_recv()

    thru = [pltpu.HBM(t.shape, t.dtype) for t in list(srcs) + list(lands)]
    res = pl.pallas_call(
        body, name=name, out_shape=thru,
        in_specs=[HBM_SPEC] * (2 * n) + [SEM_SPEC, SEM_SPEC, ANY_SPEC], out_specs=[HBM_SPEC] * (2 * n),
        input_output_aliases={i: i for i in range(2 * n)},
        compiler_params=pltpu.CompilerParams(has_side_effects=EFFECT),
    )(*srcs, *lands, send, recv, after)
    return list(res[:n]), list(res[n:])


PAIR_TILE_ELEMS = 1 << 19


def _pair_fill(name, gathered, mine, core, others, chip):
    k, r, C = gathered.shape
    half = r // 2
    tr = _row_tile(half, C, PAIR_TILE_ELEMS, mult=16)
    nt = half // tr
    n_far = N_CHIPS - 1

    def body(c_ref, o_ref, chip_ref, in_ref, mine_ref, out_ref, slot, send, recv):
        j = pl.program_id(0)
        b = (j * nt + pl.program_id(1)) % 2
        x, y, c = _place()
        cp = pltpu.make_async_remote_copy(src_ref=in_ref, dst_ref=slot.at[b], send_sem=send.at[b],
                                          recv_sem=recv.at[b], device_id=(x, y, 1 - c), device_id_type=MESH)

        @pl.when(j < n_far)
        def _():
            cp.start()
            cp.wait_recv()
            out_ref[...] = slot[b]
            cp.wait_send()

        @pl.when(j >= n_far)
        def _():
            out_ref[...] = mine_ref[...]

    def far(j):
        return jnp.minimum(j, n_far - 1)

    grid_spec = pltpu.PrefetchScalarGridSpec(
        num_scalar_prefetch=3, grid=(n_far + 2, nt),
        in_specs=[pl.BlockSpec((tr, C), lambda j, i, c, o, m: (
                      (2 * o[far(j)] + c[0]) * nt + jnp.where(j < n_far, i, nt - 1), 0)),
                  pl.BlockSpec((tr, C), lambda j, i, c, o, m: (jnp.where(j < n_far, 0, (j - n_far) * nt + i), 0))],
        out_specs=pl.BlockSpec((tr, C), lambda j, i, c, o, m: (
            jnp.where(j < n_far, 2 * o[far(j)] + 1 - c[0], 2 * m[0] + j - n_far) * nt + i, 0)),
        scratch_shapes=[pltpu.VMEM((2, tr, C), gathered.dtype), pltpu.SemaphoreType.DMA((2,)),
                        pltpu.SemaphoreType.DMA((2,))])
    out = pl.pallas_call(body, name=name, grid_spec=grid_spec, out_shape=_sds((k * r, C), gathered.dtype),
                         input_output_aliases={3: 0}, compiler_params=_params(2))(
                             core, others, chip, gathered.reshape(k * r, C), mine)
    return out.reshape(k, r, C)


def _pair_reduce(name, g, core):
    k, R, C = g.shape
    half = R // 2
    tr = _row_tile(half, C, PAIR_TILE_ELEMS)
    nt = half // tr

    def body(c_ref, mine_ref, give_ref, out_ref, wire_ref, slot, send, recv):
        b = (pl.program_id(0) * nt + pl.program_id(1)) % 2
        x, y, c = _place()
        cp = pltpu.make_async_remote_copy(src_ref=give_ref, dst_ref=slot.at[b], send_sem=send.at[b],
                                          recv_sem=recv.at[b], device_id=(x, y, 1 - c), device_id_type=MESH)
        cp.start()
        cp.wait_recv()
        tot = mine_ref[...] + slot[b]
        out_ref[...] = tot
        wire_ref[...] = tot.astype(wire_ref.dtype)
        cp.wait_send()

    blk = (tr, C)
    out_spec = pl.BlockSpec(blk, lambda s, i, c: (s * nt + i, 0))
    grid_spec = pltpu.PrefetchScalarGridSpec(
        num_scalar_prefetch=1, grid=(k, nt),
        in_specs=[pl.BlockSpec(blk, lambda s, i, c: ((2 * s + c[0]) * nt + i, 0)),
                  pl.BlockSpec(blk, lambda s, i, c: ((2 * s + 1 - c[0]) * nt + i, 0))],
        out_specs=[out_spec, out_spec],
        scratch_shapes=[pltpu.VMEM((2, tr, C), F32), pltpu.SemaphoreType.DMA((2,)), pltpu.SemaphoreType.DMA((2,))])
    g2 = g.reshape(k * R, C)
    out, wire = pl.pallas_call(body, name=name, grid_spec=grid_spec,
                               out_shape=[_sds((k * half, C), F32), _sds((k * half, C), CDT)],
                               compiler_params=_params(2))(core, g2, g2)
    return out, wire.reshape(k, half, C)


def _pair_share(name, f, core):
    half, C = f.shape
    tr = _row_tile(half, C, PAIR_TILE_ELEMS)
    nt = half // tr

    def body(c_ref, f_ref, out_ref, slot, send, recv):
        p = pl.program_id(1)
        b = pl.program_id(0) % 2
        x, y, c = _place()
        cp = pltpu.make_async_remote_copy(src_ref=f_ref, dst_ref=slot.at[b], send_sem=send.at[b],
                                          recv_sem=recv.at[b], device_id=(x, y, 1 - c), device_id_type=MESH)

        @pl.when(p == 0)
        def _():
            cp.start()
            out_ref[...] = f_ref[...]

        @pl.when(p == 1)
        def _():
            cp.wait_recv()
            out_ref[...] = slot[b]
            cp.wait_send()

    grid_spec = pltpu.PrefetchScalarGridSpec(
        num_scalar_prefetch=1, grid=(nt, 2),
        in_specs=[pl.BlockSpec((tr, C), lambda i, p, c: (i, 0))],
        out_specs=pl.BlockSpec((tr, C), lambda i, p, c: (jnp.where(p == 0, c[0], 1 - c[0]) * nt + i, 0)),
        scratch_shapes=[pltpu.VMEM((2, tr, C), F32), pltpu.SemaphoreType.DMA((2,)), pltpu.SemaphoreType.DMA((2,))])
    return pl.pallas_call(body, name=name, grid_spec=grid_spec, out_shape=_sds((2 * half, C), F32),
                          compiler_params=_params(2))(core, f)


def _all_reduce_small(v):
    r, cdim = v.shape

    def body(v_ref, o_ref, buf, send, recv):
        x, y, c = _place()
        me = 4 * x + 2 * y + c
        buf[me] = v_ref[...]
        sends = []
        for m in range(1, N_DEV):
            px = 1 - x if m & 4 else x
            py = 1 - y if m & 2 else y
            pc = 1 - c if m & 1 else c
            cp = pltpu.make_async_remote_copy(src_ref=v_ref, dst_ref=buf.at[me], send_sem=send.at[m - 1],
                                              recv_sem=recv.at[m - 1], device_id=(px, py, pc), device_id_type=MESH)
            cp.start()
            sends.append((cp, 4 * px + 2 * py + pc))
        for m, (cp, pidx) in enumerate(sends):
            pltpu.make_async_remote_copy(src_ref=v_ref, dst_ref=buf.at[pidx], send_sem=send.at[m], recv_sem=recv.at[m],
                                         device_id=(x, y, c), device_id_type=MESH).wait_recv()
        for cp, _ in sends:
            cp.wait_send()
        tot = buf[0]
        for k in range(1, N_DEV):
            tot = tot + buf[k]
        o_ref[...] = tot

    return pl.pallas_call(
        body, name="all_reduce_small", in_specs=[VMEM_SPEC], out_specs=VMEM_SPEC,
        out_shape=_sds((r, cdim), F32),
        scratch_shapes=[pltpu.VMEM((N_DEV, r, cdim), F32), pltpu.SemaphoreType.DMA((N_DEV - 1,)),
                        pltpu.SemaphoreType.DMA((N_DEV - 1,))],
    )(v)


def _row_tile(rows, cols, budget_elems=1 << 18, mult=8):
    if rows % mult:
        return rows
    t = max(mult, (budget_elems // cols) // mult * mult)
    while rows % t:
        t -= mult
    return t


def _sum4(name, own, by_chip, chip, others):
    k, R, C = by_chip.shape
    tr = _row_tile(R, C, mult=16)
    nt = R // tr

    def body(chip_ref, others_ref, own_ref, a_ref, b_ref, c_ref, o_ref):
        o_ref[...] = ((own_ref[...] + a_ref[...].astype(F32)) + b_ref[...].astype(F32)) + c_ref[...].astype(F32)

    def from_chip(j):
        return pl.BlockSpec((tr, C), lambda i, chip, oth: (oth[j] * nt + i, 0))

    grid_spec = pltpu.PrefetchScalarGridSpec(
        num_scalar_prefetch=2, grid=(nt,),
        in_specs=[pl.BlockSpec((tr, C), lambda i, chip, oth: (chip[0] * nt + i, 0)),
                  from_chip(0), from_chip(1), from_chip(2)],
        out_specs=pl.BlockSpec((tr, C), lambda i, chip, oth: (i, 0)))
    by2 = by_chip.reshape(k * R, C)
    return pl.pallas_call(body, name=name, grid_spec=grid_spec, out_shape=_sds((R, C), F32),
                          compiler_params=_params(1))(chip, others, own, by2, by2, by2)


def _adamw(name, w, g, m, v):
    R, C = w.shape
    tr = _row_tile(R, C, 1 << 17)

    def body(w_ref, g_ref, m_ref, v_ref, d_ref, nm_ref, nv_ref):
        gv = g_ref[...]
        nm = ADAM_B1 * m_ref[...] + (1.0 - ADAM_B1) * gv
        nv = ADAM_B2 * v_ref[...] + (1.0 - ADAM_B2) * (gv * gv)
        m_hat = nm / (1.0 - ADAM_B1 ** ADAM_STEP)
        v_hat = nv / (1.0 - ADAM_B2 ** ADAM_STEP)
        d_ref[...] = -ADAM_LR * (m_hat / (jnp.sqrt(v_hat) + ADAM_EPS) + ADAM_WD * w_ref[...])
        nm_ref[...] = nm
        nv_ref[...] = nv

    spec = pl.BlockSpec((tr, C), lambda i: (i, 0))
    return pl.pallas_call(body, name=name, grid=(R // tr,), in_specs=[spec] * 4, out_specs=[spec] * 3,
                          out_shape=[_sds((R, C), F32)] * 3, compiler_params=_params(1))(w, g, m, v)


BIG = ("w_in", "w_proj_a", "w_proj_b", "w_out", "w_up", "w_down")
COL_SHARDED = ("w_in", "w_proj_a", "w_up")
SMALL = ("norm1_g", "q_norm_g", "k_norm_g", "ret_gn_g", "ret_gn_b", "norm2_g")
ALL_W = ("norm1_g", "w_in", "q_norm_g", "k_norm_g", "ret_gn_g", "ret_gn_b", "w_proj_a", "w_proj_b", "w_out",
         "norm2_g", "w_up", "w_down")
LANES = 128


def _to_full(name, gathered):
    k, r, c = gathered.shape
    if name in COL_SHARDED:
        return gathered.transpose(1, 0, 2).reshape(r, k * c)
    return gathered.reshape(k * r, c)


def _to_shard_major(name, full):
    if name in COL_SHARDED:
        r, c4 = full.shape
        return full.reshape(r, N_CHIPS, c4 // N_CHIPS).transpose(1, 0, 2)
    r4, c = full.shape
    return full.reshape(N_CHIPS, r4 // N_CHIPS, c)


def kernel(x, norm1_g, w_in, q_norm_g, k_norm_g, ret_gn_g, ret_gn_b, w_proj_a, w_proj_b, w_out, norm2_g, w_up, w_down, loss_target, m_norm1_g, m_w_in, m_q_norm_g, m_k_norm_g, m_ret_gn_g, m_ret_gn_b, m_w_proj_a, m_w_proj_b, m_w_out, m_norm2_g, m_w_up, m_w_down, v_norm1_g, v_w_in, v_q_norm_g, v_k_norm_g, v_ret_gn_g, v_ret_gn_b, v_w_proj_a, v_w_proj_b, v_w_out, v_norm2_g, v_w_up, v_w_down):
    weights = dict(norm1_g=norm1_g, w_in=w_in, q_norm_g=q_norm_g, k_norm_g=k_norm_g, ret_gn_g=ret_gn_g,
                   ret_gn_b=ret_gn_b, w_proj_a=w_proj_a, w_proj_b=w_proj_b, w_out=w_out, norm2_g=norm2_g,
                   w_up=w_up, w_down=w_down)
    moments_m = dict(norm1_g=m_norm1_g, w_in=m_w_in, q_norm_g=m_q_norm_g, k_norm_g=m_k_norm_g, ret_gn_g=m_ret_gn_g,
                     ret_gn_b=m_ret_gn_b, w_proj_a=m_w_proj_a, w_proj_b=m_w_proj_b, w_out=m_w_out,
                     norm2_g=m_norm2_g, w_up=m_w_up, w_down=m_w_down)
    moments_v = dict(norm1_g=v_norm1_g, w_in=v_w_in, q_norm_g=v_q_norm_g, k_norm_g=v_k_norm_g, ret_gn_g=v_ret_gn_g,
                     ret_gn_b=v_ret_gn_b, w_proj_a=v_w_proj_a, w_proj_b=v_w_proj_b, w_out=v_w_out,
                     norm2_g=v_norm2_g, w_up=v_w_up, w_down=v_w_down)

    mx, my = lax.axis_index("x"), lax.axis_index("y")
    core = lax.axis_index("c").astype(jnp.int32).reshape(1)
    chip = (2 * mx + my).astype(jnp.int32).reshape(1)
    others = jnp.stack([2 * (1 - mx) + my, 2 * mx + 1 - my, 2 * (1 - mx) + 1 - my]).astype(jnp.int32)
    shards = {n: weights[n][0].astype(CDT) for n in BIG}
    def start_gather(name, names):
        return _exchange_start(name, "gather", [shards[n] for n in names],
                               [((N_CHIPS,) + shards[n].shape, CDT) for n in names])

    i_send, i_recv, i_srcs, i_lands, i_token = start_gather("gather_w_in_start", ["w_in"])
    late = [n for n in BIG if n != "w_in"]
    l_send, l_recv, l_srcs, l_lands, l_token = start_gather("gather_late_start", late)

    def far_w_in(after):
        srcs, lands = _exchange_wait("gather_w_in_wait", "gather", i_send, i_recv, i_srcs, i_lands, after)
        return _pair_fill("pair_fill_w_in", lands[0], srcs[0], core, others, chip)

    def late_weights(after):
        srcs, lands = _exchange_wait("gather_late_wait", "gather", l_send, l_recv, l_srcs, l_lands, after)
        out = {}
        for n, mine, land in zip(late, srcs, lands):
            out[n] = _to_full(n, _pair_fill("pair_fill_%s" % n, land, mine, core, others, chip))
        return out

    pending = []

    def on_grads(group):
        names = list(group)
        red = [_pair_reduce("pair_reduce_%s" % n, g if g.ndim == 3 else _to_shard_major(n, g), core)
               for n, g in group.items()]
        wires = [wire for _, wire in red]
        send, recv, srcs, lands, token = _exchange_start(
            "scatter_start_%s" % names[0], "scatter", wires, [(wire.shape, wire.dtype) for wire in wires])
        pending.append((names, [own for own, _ in red], send, recv, srcs, lands))
        return token

    small = {n: weights[n].reshape(1, -1) for n in SMALL}

    loss, grad_x, small_g = _local_step(x[0], loss_target[0], i_srcs[0], chip, others, far_w_in, small,
                                        late_weights, on_grads, deps0=[i_token, l_token])
    loss = lax.psum(loss, ("x", "y", "c"))

    grads = {}
    for names, owns, send, recv, srcs, lands in pending:
        _, got = _exchange_wait("scatter_wait_%s" % names[0], "scatter", send, recv, srcs, lands, grad_x)
        for n, own, by_chip in zip(names, owns, got):
            half = _sum4("chip_sum_%s" % n, own, by_chip, chip, others)
            grads[n] = _pair_share("pair_share_%s" % n, half, core)

    packed = jnp.concatenate([small_g[n].reshape(1, -1) for n in SMALL], axis=1)
    sizes = [small_g[n].size for n in SMALL]
    red = _all_reduce_small(packed.reshape(-1, LANES)).reshape(1, -1)
    off = 0
    for n, sz in zip(SMALL, sizes):
        grads[n] = red[:, off:off + sz]
        off += sz

    out_g, out_d, out_m, out_v = {}, {}, {}, {}
    for n in ALL_W:
        shape = weights[n].shape
        two_d = (shape[-2], shape[-1]) if n in BIG else (1, weights[n].size)
        g2 = grads[n].reshape(two_d)
        d, nm, nv = _adamw("adamw_%s" % n, weights[n].reshape(two_d), g2, moments_m[n].reshape(two_d),
                           moments_v[n].reshape(two_d))
        out_g[n], out_d[n], out_m[n], out_v[n] = (t.reshape(shape) for t in (g2, d, nm, nv))

    return (loss, grad_x[None], *[out_g[n] for n in ALL_W], *[out_d[n] for n in ALL_W],
            *[out_m[n] for n in ALL_W], *[out_v[n] for n in ALL_W])
```

```python
import functools
import math

import jax
import jax.numpy as jnp
from jax import lax
from jax.experimental import pallas as pl
from jax.experimental.pallas import tpu as pltpu

CDT = jnp.bfloat16
F32 = jnp.float32
EPS = 1e-6

ATT_GROUPS = ((128, 1), (512, 4), (2048, 16))
ATT_HPG = 4
ATT_HEADS = 12
HD = 128
BLK = 128
ATT_W = ATT_HEADS * HD
GW = ATT_HPG * HD
RET_HEADS = 4

ADAM_LR = 0.001
ADAM_B1 = 0.9
ADAM_B2 = 0.999
ADAM_EPS = 1e-08
ADAM_WD = 0.01
ADAM_STEP = 10

VMEM_LIMIT_BYTES = 56 * 1024 * 1024
MXU_DIM = 256
MESH = pl.DeviceIdType.MESH
HBM_SPEC = pl.BlockSpec(memory_space=pltpu.HBM)
VMEM_SPEC = pl.BlockSpec(memory_space=pltpu.VMEM)


def _params(n_axes):
    return pltpu.CompilerParams(dimension_semantics=("arbitrary",) * n_axes,
                                vmem_limit_bytes=VMEM_LIMIT_BYTES)


def _dot_nn(a, b):
    return jnp.dot(a, b, preferred_element_type=F32)


def _dot_nt(a, b):
    return lax.dot_general(a, b, (((1,), (1,)), ((), ())), preferred_element_type=F32)


def _dot_tn(a, b):
    return lax.dot_general(a, b, (((0,), (0,)), ((), ())), preferred_element_type=F32)


def _sigmoid(v):
    return 1.0 / (1.0 + jnp.exp(-v))


def _matmul(name, mode, a, b, *, tm, tn, tk, extras=(), outs, epilogue, deps=(), b_spec=None, n_cols=None,
            prefetch=(), alias_dep_to_out=None, j_outer=False):
    deps = [d for d in deps if d is not None]
    if mode == "tn":
        K, M = a.shape
    else:
        M, K = a.shape
    if b_spec is None:
        (N, K2) = b.shape if mode == "nt" else b.shape[::-1]
        assert K == K2, (name, a.shape, b.shape)
        if mode == "nt":
            b_spec = pl.BlockSpec((tn, tk), lambda i, j, k, *p: (j, k))
        else:
            b_spec = pl.BlockSpec((tk, tn), lambda i, j, k, *p: (k, j))
    else:
        N = n_cols
    assert M % tm == 0 and N % tn == 0 and K % tk == 0, (name, a.shape, b.shape)
    ni, nj, nk = M // tm, N // tn, K // tk
    if mode == "tn":
        a_spec = pl.BlockSpec((tk, tm), lambda i, j, k, *p: (k, i))
    else:
        a_spec = pl.BlockSpec((tm, tk), lambda i, j, k, *p: (i, k))
    dot = {"nn": _dot_nn, "nt": _dot_nt, "tn": _dot_tn}[mode]
    n_ex, n_out, n_dep, n_pre = len(extras), len(outs), len(deps), len(prefetch)
    grid = (ni, nj, nk)
    if j_outer:
        grid = (nj, ni, nk)

        def swapped(spec):
            return pl.BlockSpec(spec.block_shape, lambda j, i, k, *p: spec.index_map(i, j, k, *p))

        a_spec, b_spec = swapped(a_spec), swapped(b_spec)
        extras = [(e, swapped(s)) for e, s in extras]
        outs = [(o, swapped(s)) for o, s in outs]

    def body(*refs):
        refs = refs[n_pre:]
        a_ref, b_ref = refs[0], refs[1]
        ex = refs[2:2 + n_ex]
        out = refs[2 + n_ex + n_dep:2 + n_ex + n_dep + n_out]
        acc = refs[-1] if nk > 1 else None
        i = pl.program_id(1 if j_outer else 0)
        k = pl.program_id(2)
        if nk == 1:
            epilogue(dot(a_ref[...].astype(CDT), b_ref[...].astype(CDT)), ex, out, i)
            return

        @pl.when(k == 0)
        def _():
            acc[...] = jnp.zeros_like(acc)

        acc[...] += dot(a_ref[...].astype(CDT), b_ref[...].astype(CDT))

        @pl.when(k == nk - 1)
        def _():
            epilogue(acc[...], ex, out, i)

    grid_spec = pltpu.PrefetchScalarGridSpec(
        num_scalar_prefetch=n_pre, grid=grid,
        in_specs=[a_spec, b_spec] + [s for _, s in extras] + [pl.BlockSpec(memory_space=pl.ANY)] * n_dep,
        out_specs=[s for _, s in outs],
        scratch_shapes=[pltpu.VMEM((tm, tn), F32)] if nk > 1 else [])
    aliases = {}
    if alias_dep_to_out is not None:
        aliases = {n_pre + 2 + n_ex + alias_dep_to_out[0]: alias_dep_to_out[1]}
    res = pl.pallas_call(
        body, name=name, grid_spec=grid_spec, out_shape=[o for o, _ in outs], input_output_aliases=aliases,
        compiler_params=_params(3),
    )(*prefetch, a, b, *[e for e, _ in extras], *deps)
    return res


def _mn(tm, tn, col_off=0):
    return pl.BlockSpec((tm, tn), lambda i, j, k, *p: (i, j + col_off))


def _row(tn):
    return pl.BlockSpec((1, tn), lambda i, j, k, *p: (0, j))


def _ep_store(acc, ex, out, i):
    out[0][...] = acc.astype(out[0].dtype)


def _ep_resid(acc, ex, out, i):
    out[0][...] = ex[0][...] + acc


def _ep_up(acc, ex, out, i):
    out[0][...] = acc.astype(out[0].dtype)
    r = jnp.maximum(acc, 0.0)
    out[1][...] = (r * r).astype(out[1].dtype)


def _ep_down_loss(acc, ex, out, i, inv_d):
    diff = (ex[0][...] + acc) - ex[1][...]
    dx2 = diff * inv_d
    out[0][...] = dx2
    out[1][...] = dx2.astype(out[1].dtype)

    @pl.when(i == 0)
    def _():
        out[2][...] = jnp.zeros_like(out[2])

    out[2][...] += jnp.sum(diff * diff, axis=0, keepdims=True)


def _ep_dh(acc, ex, out, i):
    h = ex[0][...].astype(F32)
    out[0][...] = (acc * (2.0 * jnp.maximum(h, 0.0))).astype(out[0].dtype)


def _ep_rms_bwd(acc, ex, out, i):
    x = ex[0][...]
    g = ex[1][...]
    rstd = lax.rsqrt(jnp.mean(x * x, axis=-1, keepdims=True) + EPS)
    xh = x * rstd
    dxh = acc * g
    dx = ex[2][...] + rstd * (dxh - xh * jnp.mean(dxh * xh, axis=-1, keepdims=True))
    out[0][...] = dx
    out[1][...] = dx.astype(out[1].dtype)

    @pl.when(i == 0)
    def _():
        out[2][...] = jnp.zeros_like(out[2])

    out[2][...] += jnp.sum(acc * xh, axis=0, keepdims=True)


def _ep_gates(acc, ex, out, i):
    sa = _sigmoid(ex[0][...].astype(F32))
    sb = _sigmoid(ex[1][...].astype(F32))
    dpa = acc * sa
    dpb = acc * sb
    out[0][...] = dpa.astype(out[0].dtype)
    out[1][...] = dpb.astype(out[1].dtype)
    out[2][...] = (dpa * ex[2][...].astype(F32) * (1.0 - sa)).astype(out[2].dtype)
    out[3][...] = (dpb * ex[3][...].astype(F32) * (1.0 - sb)).astype(out[3].dtype)


def _sds(shape, dtype):
    return jax.ShapeDtypeStruct(shape, dtype)


def _rms_fwd(name, x, g, tm=512):
    S, D = x.shape

    def body(x_ref, g_ref, o_ref):
        xv = x_ref[...]
        rstd = lax.rsqrt(jnp.mean(xv * xv, axis=-1, keepdims=True) + EPS)
        o_ref[...] = (xv * rstd * g_ref[...]).astype(o_ref.dtype)

    return pl.pallas_call(
        body, name=name, grid=(S // tm,),
        in_specs=[pl.BlockSpec((tm, D), lambda i: (i, 0)), pl.BlockSpec((1, D), lambda i: (0, 0))],
        out_specs=pl.BlockSpec((tm, D), lambda i: (i, 0)),
        out_shape=_sds((S, D), CDT), compiler_params=_params(1))(x, g)


def _rm_shape(S, d, width):
    return (S, width) if d == 1 else (d, S // d, width)


def _rm_spec(tm, d, width):
    if d == 1:
        return pl.BlockSpec((tm, width), lambda i: (i, 0))
    return pl.BlockSpec((d, tm // d, width), lambda i: (0, i, 0))


def _rm_put(dst_ref, cols, buf_ref, d):
    if d == 1:
        dst_ref[:, cols] = buf_ref[...].astype(dst_ref.dtype)
        return
    m = buf_ref.shape[0] // d
    for r in range(d):
        dst_ref[r, :, cols] = buf_ref[pl.ds(r, m, stride=d), :].astype(dst_ref.dtype)


def _rm_reader(buf_ref, src_ref, d):
    if d == 1:
        return lambda s: src_ref[:, s * HD:(s + 1) * HD].astype(F32)
    m = buf_ref.shape[1] // d
    for s in range(buf_ref.shape[0]):
        for r in range(d):
            buf_ref.at[s][pl.ds(r, m, stride=d), :] = src_ref[r, :, s * HD:(s + 1) * HD].astype(F32)
    return lambda s: buf_ref[s]


def _qknorm_fwd(proj, gqk, tm=512):
    S = proj.shape[0]
    W = 2 * ATT_W
    dil = [d for _, d in ATT_GROUPS]

    def body(p_ref, g_ref, o0, o1, o2, buf):
        outs = (o0, o1, o2)
        for hd in range(3 * ATT_HEADS):
            which, head = hd // ATT_HEADS, hd % ATT_HEADS
            grp, slot = head // ATT_HPG, head % ATT_HPG
            v = p_ref[:, hd * HD:(hd + 1) * HD].astype(F32)
            if which < 2:
                rstd = lax.rsqrt(jnp.mean(v * v, axis=-1, keepdims=True) + EPS)
                v = v * rstd * g_ref[:, hd * HD:(hd + 1) * HD]
            buf[...] = v
            _rm_put(outs[grp], slice(which * GW + slot * HD, which * GW + (slot + 1) * HD), buf, dil[grp])

    return pl.pallas_call(
        body, name="qknorm_fwd", grid=(S // tm,),
        in_specs=[pl.BlockSpec((tm, 3 * ATT_W), lambda i: (i, 0)), pl.BlockSpec((1, W), lambda i: (0, 0))],
        out_specs=[_rm_spec(tm, d, 3 * GW) for d in dil],
        out_shape=[_sds(_rm_shape(S, d, 3 * GW), CDT) for d in dil],
        scratch_shapes=[pltpu.VMEM((tm, HD), F32)],
        compiler_params=_params(1))(proj, gqk)


def _qknorm_bwd(proj, gqk, dqs, dks, dvs, dproj, tm=256):
    S = proj.shape[0]
    W = 2 * ATT_W
    dil = [d for _, d in ATT_GROUPS]

    def body(p_ref, g_ref, *refs):
        ins = refs[0:9]
        o_ref, dg_ref = refs[10], refs[11]
        bufs = refs[12:21]
        i = pl.program_id(0)

        @pl.when(i == 0)
        def _():
            dg_ref[...] = jnp.zeros_like(dg_ref)

        nat = [_rm_reader(bufs[j], ins[j], dil[j % 3]) for j in range(9)]
        dq_get, dk_get, dv_get = nat[0:3], nat[3:6], nat[6:9]
        for hd in range(2 * ATT_HEADS):
            sl = slice(hd * HD, (hd + 1) * HD)
            head = hd % ATT_HEADS
            grp, slot = head // ATT_HPG, head % ATT_HPG
            dn = (dq_get if hd < ATT_HEADS else dk_get)[grp](slot)
            v = p_ref[:, sl].astype(F32)
            rstd = lax.rsqrt(jnp.mean(v * v, axis=-1, keepdims=True) + EPS)
            vh = v * rstd
            dg_ref[:, sl] += jnp.sum(dn * vh, axis=0, keepdims=True)
            dvh = dn * g_ref[:, sl]
            o_ref[:, sl] = (rstd * (dvh - vh * jnp.mean(dvh * vh, axis=-1, keepdims=True))).astype(o_ref.dtype)
        for head in range(ATT_HEADS):
            grp, slot = head // ATT_HPG, head % ATT_HPG
            o_ref[:, W + head * HD:W + (head + 1) * HD] = dv_get[grp](slot).astype(o_ref.dtype)

    return pl.pallas_call(
        body, name="qknorm_bwd", grid=(S // tm,),
        in_specs=[pl.BlockSpec((tm, W), lambda i: (i, 0)), pl.BlockSpec((1, W), lambda i: (0, 0))]
        + [_rm_spec(tm, d, GW) for d in dil] * 3 + [pl.BlockSpec(memory_space=pl.ANY)],
        out_specs=[pl.BlockSpec((tm, 3 * ATT_W), lambda i: (i, 0)), pl.BlockSpec((1, W), lambda i: (0, 0))],
        out_shape=[_sds(dproj.shape, dproj.dtype), _sds((1, W), F32)],
        scratch_shapes=[pltpu.VMEM((ATT_HPG, tm, HD), F32)] * 9,
        input_output_aliases={11: 0},
        compiler_params=_params(1))(proj, gqk, *dqs, *dks, *dvs, dproj)


def _att_mask(n):
    qi = lax.broadcasted_iota(jnp.int32, (BLK, 2 * BLK), 0)
    kj = lax.broadcasted_iota(jnp.int32, (BLK, 2 * BLK), 1)
    dist = BLK + qi - kj
    valid = (dist >= 0) & (dist <= BLK) & ((kj >= BLK) | (n > 0))
    return valid, dist.astype(F32)


def _att_slopes(grp):
    return [2.0 ** (-8.0 * (grp * ATT_HPG + hh + 1) / ATT_HEADS) for hh in range(ATT_HPG)]


def _att_spec(d, row_fn, col=0):
    if d == 1:
        return pl.BlockSpec((BLK, GW), lambda r, n: (row_fn(n), col))
    return pl.BlockSpec((None, BLK, GW), lambda r, n: (r, row_fn(n), col))


def _att_qkv_specs(d, nb):
    last = nb - 1

    def cur(n):
        return jnp.minimum(n, last)

    def prev(n):
        return jnp.maximum(jnp.minimum(n, last) - 1, 0)

    return [_att_spec(d, cur, 0), _att_spec(d, prev, 1), _att_spec(d, cur, 1), _att_spec(d, prev, 2),
            _att_spec(d, cur, 2)]


def _att_fwd(grp, S, qkv):
    _, d = ATT_GROUPS[grp]
    L = S // d
    nb = L // BLK
    slopes = _att_slopes(grp)
    scale = HD ** -0.5

    def body(q_ref, kp_ref, kc_ref, vp_ref, vc_ref, o_ref, l_ref):
        n = pl.program_id(1)
        valid, distf = _att_mask(n)
        for hh in range(ATT_HPG):
            sl = slice(hh * HD, (hh + 1) * HD)
            k = jnp.concatenate([kp_ref[:, sl], kc_ref[:, sl]], axis=0)
            v = jnp.concatenate([vp_ref[:, sl], vc_ref[:, sl]], axis=0)
            s = _dot_nt(q_ref[:, sl], k) * scale + (-slopes[hh] * d) * distf
            s = jnp.where(valid, s, -1e30)
            m = jnp.max(s, axis=-1, keepdims=True)
            p = jnp.exp(s - m)
            den = jnp.sum(p, axis=-1, keepdims=True)
            o_ref[:, sl] = _dot_nn(p.astype(CDT), v) / den
            l_ref[:, sl] = jnp.broadcast_to(m + jnp.log(den), (BLK, HD))

    out_spec = _att_spec(d, lambda n: n)
    return pl.pallas_call(
        body, name="att_fwd_g%d" % grp, grid=(d, nb),
        in_specs=_att_qkv_specs(d, nb),
        out_specs=[out_spec, out_spec],
        out_shape=[_sds(_rm_shape(S, d, GW), F32)] * 2,
        compiler_params=_params(2),
    )(qkv, qkv, qkv, qkv, qkv)


def _att_bwd(grp, S, qkv, lse, do_g, c_g):
    _, d = ATT_GROUPS[grp]
    L = S // d
    nb = L // BLK
    slopes = _att_slopes(grp)
    scale = HD ** -0.5
    last = nb - 1

    def body(q_ref, kp_ref, kc_ref, vp_ref, vc_ref, l_ref, do_ref, c_ref, dq_ref, dk_ref, dv_ref, ck, cv):
        n = pl.program_id(1)

        @pl.when(n == 0)
        def _():
            ck[...] = jnp.zeros_like(ck)
            cv[...] = jnp.zeros_like(cv)

        @pl.when(n < nb)
        def _():
            valid, distf = _att_mask(n)
            for hh in range(ATT_HPG):
                sl = slice(hh * HD, (hh + 1) * HD)
                q = q_ref[:, sl]
                k = jnp.concatenate([kp_ref[:, sl], kc_ref[:, sl]], axis=0)
                v = jnp.concatenate([vp_ref[:, sl], vc_ref[:, sl]], axis=0)
                do = do_ref[:, sl]
                s = _dot_nt(q, k) * scale + (-slopes[hh] * d) * distf
                p = jnp.where(valid, jnp.exp(s - l_ref[:, sl][:, 0:1]), 0.0)
                dp = _dot_nt(do, v)
                ds = (p * (dp + c_ref[:, sl][:, 0:1]) * scale).astype(CDT)
                dq_ref[:, sl] = _dot_nn(ds, k)
                dk = _dot_tn(ds, q)
                dv = _dot_tn(p.astype(CDT), do)
                dk_ref[:, sl] = ck[:, sl] + dk[0:BLK]
                dv_ref[:, sl] = cv[:, sl] + dv[0:BLK]
                ck[:, sl] = dk[BLK:2 * BLK]
                cv[:, sl] = dv[BLK:2 * BLK]

        @pl.when(n == nb)
        def _():
            dk_ref[...] = ck[...]
            dv_ref[...] = cv[...]

    blk = (BLK, GW)
    at_q = _att_spec(d, lambda n: jnp.minimum(n, last))
    behind = _att_spec(d, lambda n: jnp.maximum(n - 1, 0))
    return pl.pallas_call(
        body, name="att_bwd_g%d" % grp, grid=(d, nb + 1),
        in_specs=_att_qkv_specs(d, nb) + [at_q, at_q, at_q],
        out_specs=[at_q, behind, behind],
        out_shape=[_sds(_rm_shape(S, d, GW), F32)] * 3,
        scratch_shapes=[pltpu.VMEM(blk, F32), pltpu.VMEM(blk, F32)],
        compiler_params=_params(2),
    )(qkv, qkv, qkv, qkv, qkv, lse, do_g, c_g)


def _mix_alpha(l0, l1, l2):
    mx = jnp.maximum(jnp.maximum(l0, l1), l2)
    e = [jnp.exp(l0 - mx), jnp.exp(l1 - mx), jnp.exp(l2 - mx)]
    tot = e[0] + e[1] + e[2]
    return [ei / tot for ei in e]


def _mix_fwd(S, os_, ls_, tm=512):
    dil = [d for _, d in ATT_GROUPS]

    def body(*refs):
        out, bufs = refs[6], refs[7:13]
        get = [_rm_reader(bufs[j], refs[j], dil[j % 3]) for j in range(6)]
        for s in range(ATT_HPG):
            al = _mix_alpha(*[get[3 + g](s) for g in range(3)])
            mixed = al[0] * get[0](s) + al[1] * get[1](s) + al[2] * get[2](s)
            out[:, s * HD:(s + 1) * HD] = mixed.astype(out.dtype)

    specs = [_rm_spec(tm, d, GW) for d in dil]
    return pl.pallas_call(
        body, name="mix_fwd", grid=(S // tm,), in_specs=specs * 2, out_specs=pl.BlockSpec((tm, GW), lambda i: (i, 0)),
        out_shape=_sds((S, GW), CDT), scratch_shapes=[pltpu.VMEM((ATT_HPG, tm, HD), F32)] * 6,
        compiler_params=_params(1))(*os_, *ls_)


def _mix_bwd(S, os_, ls_, do_a, tm=512):
    dil = [d for _, d in ATT_GROUPS]

    def body(*refs):
        d_ref, outs, bufs, tmp = refs[6], refs[7:13], refs[13:19], refs[19]
        get = [_rm_reader(bufs[j], refs[j], dil[j % 3]) for j in range(6)]
        for s in range(ATT_HPG):
            cols = slice(s * HD, (s + 1) * HD)
            al = _mix_alpha(*[get[3 + g](s) for g in range(3)])
            dv = d_ref[:, cols]
            o_a = al[0] * get[0](s) + al[1] * get[1](s) + al[2] * get[2](s)
            dsum = jnp.sum(dv * o_a, axis=-1, keepdims=True)
            for g in range(3):
                tmp[...] = al[g] * dv
                _rm_put(outs[g], cols, tmp, dil[g])
                tmp[...] = -(al[g] * dsum)
                _rm_put(outs[3 + g], cols, tmp, dil[g])

    specs = [_rm_spec(tm, d, GW) for d in dil]
    res = pl.pallas_call(
        body, name="mix_bwd", grid=(S // tm,), in_specs=specs * 2 + [pl.BlockSpec((tm, GW), lambda i: (i, 0))],
        out_specs=specs * 2,
        out_shape=[_sds(_rm_shape(S, d, GW), CDT) for d in dil] + [_sds(_rm_shape(S, d, GW), F32) for d in dil],
        scratch_shapes=[pltpu.VMEM((ATT_HPG, tm, HD), F32)] * 6 + [pltpu.VMEM((tm, HD), F32)],
        compiler_params=_params(1))(*os_, *ls_, do_a)
    return res[:3], res[3:]


def _ret_tables(dk):
    H, C = RET_HEADS, BLK
    log_g = jnp.log(1.0 - 2.0 ** (-5.0 - jnp.arange(H, dtype=F32)))
    idx = jnp.arange(C, dtype=F32)
    diff = idx[:, None] - idx[None, :]
    decay = jnp.where(diff >= 0, jnp.exp(log_g[:, None, None] * jnp.maximum(diff, 0.0)), 0.0)
    xi = jnp.exp(log_g[:, None] * (idx[None, :] + 1.0))
    zeta = jnp.exp(log_g[:, None] * (C - 1.0 - idx[None, :])) * (dk ** -0.5)
    g_chunk = jnp.exp(log_g * C)
    bc = lambda t: jnp.broadcast_to(t[:, :, None], (H, C, C))
    return decay, bc(xi), bc(zeta), jnp.broadcast_to(g_chunk[:, None, None], (H, 8, C))


def _gn_fwd(o, g, b):
    mu = jnp.mean(o, axis=-1, keepdims=True)
    xc = o - mu
    rstd = lax.rsqrt(jnp.mean(xc * xc, axis=-1, keepdims=True) + EPS)
    yh = xc * rstd
    return yh, rstd, yh * g + b


def _ret_specs(dk, dv, order):
    H = RET_HEADS
    qk_w, v_w = H * dk, H * dv
    off_q = 3 * ATT_W
    off_k, off_v, off_g = off_q + qk_w, off_q + 2 * qk_w, off_q + 2 * qk_w + v_w
    assert 2 * dk == dv and all(off % dv == 0 for off in (off_q, off_k, off_v, off_g))

    def col(off, j):
        return pl.BlockSpec((BLK, dv), lambda i: (order(i), off // dv + j))

    tab = pl.BlockSpec((H, BLK, BLK), lambda i: (0, 0, 0))
    return ([col(off_q, j) for j in range(H // 2)] + [col(off_k, j) for j in range(H // 2)]
            + [col(off_v, j) for j in range(H)] + [col(off_g, j) for j in range(H)]
            + [tab, tab, tab, pl.BlockSpec((H, 8, BLK), lambda i: (0, 0, 0))])


def _ret_heads(refs, dk):
    H = RET_HEADS
    q_refs, k_refs = refs[0:H // 2], refs[H // 2:H]
    v_refs, gr_refs = refs[H:2 * H], refs[2 * H:3 * H]

    def head(h):
        cols = slice((h % 2) * dk, (h % 2 + 1) * dk)
        return q_refs[h // 2][:, cols], k_refs[h // 2][:, cols], v_refs[h][...], gr_refs[h][...]

    return head, refs[3 * H:3 * H + 4]


def _ret_fwd(proj, gn_g, gn_b, dk, dv):
    S = proj.shape[0]
    N = S // BLK
    H = RET_HEADS
    kscale = dk ** -0.5
    n_in = 3 * H + 4

    def body(*refs):
        head, (dec_ref, xi_ref, zeta_ref, gc_ref) = _ret_heads(refs, dk)
        g_ref, b_ref, opre_ref, or_ref, st_ref, state = refs[n_in:n_in + 6]
        n = pl.program_id(0)

        @pl.when(n == 0)
        def _():
            state[...] = jnp.zeros_like(state)

        for h in range(H):
            vs = slice(h * dv, (h + 1) * dv)
            q, k, v, gr = head(h)
            s = _dot_nt(q, k) * kscale * dec_ref[h]
            st = state[h]
            st_c = st.astype(CDT)
            st_ref[h] = st_c
            o = _dot_nn(s.astype(CDT), v) + _dot_nn(q, st_c) * xi_ref[h][:, 0:1]
            kz = (k.astype(F32) * zeta_ref[h][:, 0:1]).astype(CDT)
            state[h] = st * gc_ref[h][0:1, 0:1] + _dot_tn(kz, v)
            opre_ref[:, vs] = o
            _, _, y = _gn_fwd(o, g_ref[:, vs], b_ref[:, vs])
            gr = gr.astype(F32)
            or_ref[:, vs] = (y * (gr * _sigmoid(gr))).astype(or_ref.dtype)

    v_w = H * dv
    row = pl.BlockSpec((1, v_w), lambda i: (0, 0))
    tile = pl.BlockSpec((BLK, v_w), lambda i: (i, 0))
    return pl.pallas_call(
        body, name="ret_fwd", grid=(N,),
        in_specs=_ret_specs(dk, dv, lambda i: i) + [row, row],
        out_specs=[tile, tile, pl.BlockSpec((None, H, dk, dv), lambda i: (i, 0, 0, 0))],
        out_shape=[_sds((S, v_w), F32), _sds((S, v_w), CDT), _sds((N, H, dk, dv), CDT)],
        scratch_shapes=[pltpu.VMEM((H, dk, dv), F32)],
        compiler_params=_params(1),
    )(*[proj] * (3 * H), *_ret_tables(dk), gn_g, gn_b)


def _ret_bwd(proj, gn_g, gn_b, o_pre, states, d_or, dga, dgb, dk, dv):
    S, in_w = proj.shape
    N = S // BLK
    H = RET_HEADS
    qk_w, v_w = H * dk, H * dv
    kscale = dk ** -0.5
    n_in = 3 * H + 4
    out_w = 2 * qk_w + 2 * v_w
    gate_w = dga.shape[1]
    col0 = 3 * ATT_W
    assert col0 + out_w + 2 * gate_w == in_w
    rev = lambda i: N - 1 - i

    def body(*refs):
        head, (dec_ref, xi_ref, zeta_ref, gc_ref) = _ret_heads(refs, dk)
        (g_ref, b_ref, opre_ref, st_ref, dor_ref, dga_ref, dgb_ref, dproj_ref, dg_ref, db_ref, dstate, stage,
         sem) = refs[n_in:n_in + 13]
        i = pl.program_id(0)
        slot = i % 2
        out_ref = stage.at[slot]

        def out_copy(s, step):
            rows = pl.ds(pl.multiple_of(rev(step) * BLK, BLK), BLK)
            return pltpu.make_async_copy(stage.at[s], dproj_ref.at[rows, pl.ds(col0, in_w - col0)], sem.at[s])

        @pl.when(i >= 2)
        def _():
            out_copy(slot, i - 2).wait()

        @pl.when(i == 0)
        def _():
            dstate[...] = jnp.zeros_like(dstate)
            dg_ref[...] = jnp.zeros_like(dg_ref)
            db_ref[...] = jnp.zeros_like(db_ref)

        out_ref[:, out_w:out_w + gate_w] = dga_ref[...]
        out_ref[:, out_w + gate_w:out_w + 2 * gate_w] = dgb_ref[...]
        for h in range(H):
            vs = slice(h * dv, (h + 1) * dv)
            q, k, v, gr = head(h)
            decay, xi, zeta_s = dec_ref[h], xi_ref[h][:, 0:1], zeta_ref[h][:, 0:1]
            gr = gr.astype(F32)
            sg = _sigmoid(gr)
            gain = g_ref[:, vs]
            yh, rstd, y = _gn_fwd(opre_ref[:, vs], gain, b_ref[:, vs])
            d_or_v = dor_ref[:, vs]
            dy = d_or_v * (gr * sg)
            out_ref[:, 2 * qk_w + v_w + h * dv:2 * qk_w + v_w + (h + 1) * dv] = (
                d_or_v * y * (sg * (1.0 + gr * (1.0 - sg)))).astype(out_ref.dtype)
            dg_ref[:, vs] += jnp.sum(dy * yh, axis=0, keepdims=True)
            db_ref[:, vs] += jnp.sum(dy, axis=0, keepdims=True)
            dyh = dy * gain
            do = rstd * (dyh - jnp.mean(dyh, axis=-1, keepdims=True)
                         - yh * jnp.mean(dyh * yh, axis=-1, keepdims=True))
            do_c = do.astype(CDT)
            dox = (do * xi).astype(CDT)
            a_c = (_dot_nt(q, k) * kscale * decay).astype(CDT)
            g_c = (_dot_nt(do_c, v) * decay).astype(CDT)
            dsn = dstate[h]
            dsn_c = dsn.astype(CDT)
            kz = (k.astype(F32) * zeta_s).astype(CDT)
            dq = _dot_nn(g_c, k) * kscale + _dot_nt(dox, st_ref[h])
            dkk = _dot_tn(g_c, q) * kscale + _dot_nt(v, dsn_c) * zeta_s
            dvv = _dot_tn(a_c, do_c) + _dot_nn(kz, dsn_c)
            dstate[h] = dsn * gc_ref[h][0:1, 0:1] + _dot_tn(q, dox)
            out_ref[:, h * dk:(h + 1) * dk] = dq.astype(out_ref.dtype)
            out_ref[:, qk_w + h * dk:qk_w + (h + 1) * dk] = dkk.astype(out_ref.dtype)
            out_ref[:, 2 * qk_w + h * dv:2 * qk_w + (h + 1) * dv] = dvv.astype(out_ref.dtype)

        cp = out_copy(slot, i)
        cp.start()

        @pl.when(i == N - 1)
        def _():
            cp.wait()
            if N >= 2:
                out_copy(1 - slot, i - 1).wait()

    row = pl.BlockSpec((1, v_w), lambda i: (0, 0))
    tile = pl.BlockSpec((BLK, v_w), lambda i: (rev(i), 0))
    gate = pl.BlockSpec((BLK, gate_w), lambda i: (rev(i), 0))
    return pl.pallas_call(
        body, name="ret_bwd", grid=(N,),
        in_specs=_ret_specs(dk, dv, rev) + [row, row, tile,
                 pl.BlockSpec((None, H, dk, dv), lambda i: (rev(i), 0, 0, 0)), tile, gate, gate],
        out_specs=[pl.BlockSpec(memory_space=pl.ANY), row, row],
        out_shape=[_sds((S, in_w), CDT), _sds((1, v_w), F32), _sds((1, v_w), F32)],
        scratch_shapes=[pltpu.VMEM((H, dk, dv), F32), pltpu.VMEM((2, BLK, in_w - col0), CDT),
                        pltpu.SemaphoreType.DMA((2,))],
        compiler_params=_params(1),
    )(*[proj] * (3 * H), *_ret_tables(dk), gn_g, gn_b, o_pre, states, d_or, dga, dgb)


def _merge_fwd(o_a, o_r, wa, wb, proj, d_model, tm=512, tn=512):
    S, in_w = proj.shape
    off_a, off_b = in_w - 2 * d_model, in_w - d_model
    assert off_a % tn == 0 and off_b % tn == 0

    def body(oa_ref, or_ref, wa_ref, wb_ref, ga_ref, gb_ref, y_ref, pa_ref, pb_ref):
        pa = _dot_nn(oa_ref[...], wa_ref[...])
        pb = _dot_nn(or_ref[...], wb_ref[...])
        y = _sigmoid(ga_ref[...].astype(F32)) * pa + _sigmoid(gb_ref[...].astype(F32)) * pb
        y_ref[...] = y.astype(y_ref.dtype)
        pa_ref[...] = pa.astype(pa_ref.dtype)
        pb_ref[...] = pb.astype(pb_ref.dtype)

    ka, kb = o_a.shape[1], o_r.shape[1]
    out = pl.BlockSpec((tm, tn), lambda i, j: (i, j))
    return pl.pallas_call(
        body, name="merge_fwd", grid=(S // tm, d_model // tn),
        in_specs=[pl.BlockSpec((tm, ka), lambda i, j: (i, 0)), pl.BlockSpec((tm, kb), lambda i, j: (i, 0)),
                  pl.BlockSpec((ka, tn), lambda i, j: (0, j)), pl.BlockSpec((kb, tn), lambda i, j: (0, j)),
                  pl.BlockSpec((tm, tn), lambda i, j: (i, off_a // tn + j)),
                  pl.BlockSpec((tm, tn), lambda i, j: (i, off_b // tn + j))],
        out_specs=[out, out, out], out_shape=[_sds((S, d_model), CDT)] * 3,
        compiler_params=_params(2))(o_a, o_r, wa, wb, proj, proj)


def _local_step(x, target, w_in_mine, chip, others, far_w_in, small, late_weights, on_grads, deps0=()):
    S, D = x.shape
    ns_in = w_in_mine.shape[1]
    in_w = N_CHIPS * ns_in
    d_ff = 4 * D
    ret_v_w = 2 * D
    dv = ret_v_w // RET_HEADS
    dk = (in_w - 3 * ATT_W - 2 * ret_v_w - 2 * D) // (2 * RET_HEADS)
    gqk = jnp.concatenate([small["q_norm_g"].reshape(1, ATT_W), small["k_norm_g"].reshape(1, ATT_W)], axis=1)
    g1, g2 = small["norm1_g"], small["norm2_g"]
    gn_g, gn_b = small["ret_gn_g"], small["ret_gn_b"]

    xn = _rms_fwd("rms1_fwd", x, g1)
    proj_sds = _sds((S, in_w), CDT)
    (proj_mine,) = _matmul(
        "in_proj_mine", "nn", xn, w_in_mine, tm=512, tn=ns_in, tk=D, prefetch=[chip],
        outs=[(proj_sds, pl.BlockSpec((512, ns_in), lambda i, j, k, c: (i, c[0])))], epilogue=_ep_store, deps=deps0)
    w_in = far_w_in(proj_mine)
    (proj,) = _matmul(
        "in_proj_far", "nn", xn, w_in, tm=512, tn=ns_in, tk=D, prefetch=[others], n_cols=(N_CHIPS - 1) * ns_in,
        b_spec=pl.BlockSpec((None, D, ns_in), lambda i, j, k, o: (o[j], 0, 0)),
        outs=[(proj_sds, pl.BlockSpec((512, ns_in), lambda i, j, k, o: (i, o[j])))], epilogue=_ep_store,
        deps=[proj_mine], alias_dep_to_out=(0, 0), j_outer=True)
    qkv = _qknorm_fwd(proj, gqk)
    att = [_att_fwd(g, S, qkv[g]) for g in range(3)]
    os_, ls_ = [a[0] for a in att], [a[1] for a in att]
    o_a = _mix_fwd(S, os_, ls_)
    o_pre, o_r, states = _ret_fwd(proj, gn_g, gn_b, dk, dv)
    w = late_weights(o_r)
    y, pa, pb = _merge_fwd(o_a, o_r, w["w_proj_a"], w["w_proj_b"], proj, D)
    (x1,) = _matmul("out_proj", "nn", y, w["w_out"], tm=512, tn=D, tk=D,
                    extras=[(x, _mn(512, D))], outs=[(_sds((S, D), F32), _mn(512, D))], epilogue=_ep_resid)
    xn2 = _rms_fwd("rms2_fwd", x1, g2)
    hid, act = _matmul("mlp_up", "nn", xn2, w["w_up"], tm=512, tn=2048, tk=D, j_outer=True,
                       outs=[(_sds((S, d_ff), CDT), _mn(512, 2048))] * 2, epilogue=_ep_up)
    dx2, dx2c, loss_row = _matmul(
        "mlp_down_loss", "nn", act, w["w_down"], tm=512, tn=D, tk=d_ff,
        extras=[(x1, _mn(512, D)), (target, _mn(512, D))],
        outs=[(_sds((S, D), F32), _mn(512, D)), (_sds((S, D), CDT), _mn(512, D)), (_sds((1, D), F32), _row(D))],
        epilogue=functools.partial(_ep_down_loss, inv_d=1.0 / D))
    loss = 0.5 * jnp.sum(loss_row) / D

    (dh,) = _matmul("d_hidden", "nt", dx2c, w["w_down"], tm=512, tn=2048, tk=D, j_outer=True,
                    extras=[(hid, _mn(512, 2048))], outs=[(_sds((S, d_ff), CDT), _mn(512, 2048))], epilogue=_ep_dh)
    (gw_down,) = _matmul("dw_down", "tn", act, dx2c, tm=1024, tn=D, tk=1024,
                         outs=[(_sds((d_ff, D), F32), _mn(1024, D))], epilogue=_ep_store)
    (gw_up,) = _matmul("dw_up", "tn", xn2, dh, tm=D, tn=1024, tk=1024,
                       outs=[(_sds((D, d_ff), F32), _mn(D, 1024))], epilogue=_ep_store)
    tok = on_grads({"w_down": gw_down, "w_up": gw_up})
    dx1, dx1c, dg2 = _matmul(
        "d_x1", "nt", dh, w["w_up"], tm=512, tn=D, tk=d_ff,
        extras=[(x1, _mn(512, D)), (g2, _row(D)), (dx2, _mn(512, D))],
        outs=[(_sds((S, D), F32), _mn(512, D)), (_sds((S, D), CDT), _mn(512, D)), (_sds((1, D), F32), _row(D))],
        epilogue=_ep_rms_bwd, deps=[tok])

    gt = 512
    assert (in_w - 2 * D) % gt == 0
    off_a, off_b = (in_w - 2 * D) // gt, (in_w - D) // gt
    dpa, dpb, dga, dgb = _matmul(
        "d_gates", "nt", dx1c, w["w_out"], tm=512, tn=gt, tk=D,
        extras=[(proj, _mn(512, gt, off_a)), (proj, _mn(512, gt, off_b)), (pa, _mn(512, gt)), (pb, _mn(512, gt))],
        outs=[(_sds((S, D), CDT), _mn(512, gt))] * 4, epilogue=_ep_gates)
    (gw_out,) = _matmul("dw_out", "tn", y, dx1c, tm=D, tn=D, tk=1024,
                        outs=[(_sds((D, D), F32), _mn(D, D))], epilogue=_ep_store)
    (gw_pa,) = _matmul("dw_proj_a", "tn", o_a, dpa, tm=GW, tn=D, tk=1024,
                       outs=[(_sds((GW, D), F32), _mn(GW, D))], epilogue=_ep_store)
    (gw_pb,) = _matmul("dw_proj_b", "tn", o_r, dpb, tm=1024, tn=D, tk=1024,
                       outs=[(_sds((ret_v_w, D), F32), _mn(1024, D))], epilogue=_ep_store)
    (do_a,) = _matmul("d_o_a", "nt", dpa, w["w_proj_a"], tm=1024, tn=GW, tk=D,
                      outs=[(_sds((S, GW), F32), _mn(1024, GW))], epilogue=_ep_store)
    tok = on_grads({"w_out": gw_out, "w_proj_a": gw_pa, "w_proj_b": gw_pb})
    (d_or,) = _matmul("d_o_r", "nt", dpb, w["w_proj_b"], tm=512, tn=ret_v_w, tk=D,
                      outs=[(_sds((S, ret_v_w), F32), _mn(512, ret_v_w))], epilogue=_ep_store, deps=[tok])

    dproj, dgn_g, dgn_b = _ret_bwd(proj, gn_g, gn_b, o_pre, states, d_or, dga, dgb, dk, dv)
    do_gs, c_gs = _mix_bwd(S, os_, ls_, do_a)
    datt_parts = [_att_bwd(g, S, qkv[g], ls_[g], do_gs[g], c_gs[g]) for g in range(3)]
    dproj, dgqk = _qknorm_bwd(proj, gqk, [p[0] for p in datt_parts], [p[1] for p in datt_parts],
                              [p[2] for p in datt_parts], dproj)

    (gw_in,) = _matmul(
        "dw_in", "tn", xn, dproj, tm=512, tn=ns_in, tk=1024,
        outs=[(_sds((N_CHIPS, D, ns_in), F32), pl.BlockSpec((None, 512, ns_in), lambda i, j, k: (j, i, 0)))],
        epilogue=_ep_store)
    tok = on_grads({"w_in": gw_in})
    grad_x, _, dg1 = _matmul(
        "d_x", "nt", dproj, w_in, tm=512, tn=D, tk=ns_in, n_cols=D,
        b_spec=pl.BlockSpec((None, D, ns_in), lambda i, j, k: (k, 0, 0)),
        extras=[(x, _mn(512, D)), (g1, _row(D)), (dx1, _mn(512, D))],
        outs=[(_sds((S, D), F32), _mn(512, D)), (_sds((S, D), CDT), _mn(512, D)), (_sds((1, D), F32), _row(D))],
        epilogue=_ep_rms_bwd, deps=[tok])

    smallg = {"norm1_g": dg1, "q_norm_g": dgqk[:, :ATT_W], "k_norm_g": dgqk[:, ATT_W:],
              "ret_gn_g": dgn_g, "ret_gn_b": dgn_b, "norm2_g": dg2}
    return loss, grad_x, smallg


N_CHIPS = 4
N_DEV = 8


def _place():
    x, y, c = lax.axis_index("x"), lax.axis_index("y"), lax.axis_index("c")
    return x, y, c


def _other_chips(x, y):
    out = []
    for fx, fy in ((1, 0), (0, 1), (1, 1)):
        px = 1 - x if fx else x
        py = 1 - y if fy else y
        out.append(((px, py), 2 * px + py))
    return out


SEM_SPEC = pl.BlockSpec(memory_space=pltpu.SEMAPHORE)
ANY_SPEC = pl.BlockSpec(memory_space=pl.ANY)
EFFECT = pltpu.SideEffectType.DATAFLOW_SIDE_EFFECTING


def _ici_copies(kind, srcs, lands, send, recv):
    x, y, c = _place()
    me = 2 * x + y
    out = []
    for w, (s, l) in enumerate(zip(srcs, lands)):
        for j, ((px, py), pidx) in enumerate(_other_chips(x, y)):
            if kind == "gather":
                half = s.shape[0] // 2
                rows = pl.ds(c * half, half)
                src, dst_there, dst_here = s.at[rows, :], l.at[me, rows, :], l.at[pidx, rows, :]
            else:
                src, dst_there, dst_here = s.at[pidx], l.at[me], l.at[pidx]
            out.append((src, dst_there, dst_here, send.at[3 * w + j], recv.at[3 * w + j], (px, py, c)))
    return out


def _exchange_start(name, kind, srcs, land_shapes):
    n = len(srcs)

    def body(*refs):
        src_refs, land_refs = refs[:n], refs[n:2 * n]
        send, recv = refs[2 * n], refs[2 * n + 1]
        token = refs[-1]
        for src, dst, _, ss, rs, dev in _ici_copies(kind, src_refs, land_refs, send, recv):
            pltpu.make_async_remote_copy(src_ref=src, dst_ref=dst, send_sem=ss, recv_sem=rs, device_id=dev,
                                         device_id_type=MESH).start()
        token[...] = jnp.zeros_like(token)

    thru = [pltpu.HBM(s.shape, s.dtype) for s in srcs] + [pltpu.HBM(shape, dtype) for shape, dtype in land_shapes]
    res = pl.pallas_call(
        body, name=name,
        out_shape=(pltpu.SemaphoreType.DMA((3 * n,)), pltpu.SemaphoreType.DMA((3 * n,)), *thru, _sds((8, LANES), F32)),
        in_specs=[HBM_SPEC] * (2 * n), out_specs=(SEM_SPEC, SEM_SPEC, *[HBM_SPEC] * (2 * n), VMEM_SPEC),
        input_output_aliases={i: 2 + i for i in range(2 * n)},
        compiler_params=pltpu.CompilerParams(has_side_effects=EFFECT),
    )(*[pltpu.with_memory_space_constraint(s, pltpu.HBM) for s in srcs],
      *[pltpu.with_memory_space_constraint(lax.empty(shape, dtype), pltpu.HBM) for shape, dtype in land_shapes])
    return res[0], res[1], list(res[2:2 + n]), list(res[2 + n:2 + 2 * n]), res[-1]


def _exchange_wait(name, kind, send, recv, srcs, lands, after):
    n = len(srcs)

    def body(*refs):
        src_refs, land_refs = refs[:n], refs[n:2 * n]
        send_ref, recv_ref = refs[2 * n], refs[2 * n + 1]
        for src, _, dst, ss, rs, dev in _ici_copies(kind, src_refs, land_refs, send_ref, recv_ref):
            cp = pltpu.make_async_remote_copy(src_ref=src, dst_ref=dst, send_sem=ss, recv_sem=rs, device_id=dev,
                                              device_id_type=MESH)
            cp.wait_send()
            cp.wait_recv()

    thru = [pltpu.HBM(t.shape, t.dtype) for t in list(srcs) + list(lands)]
    res = pl.pallas_call(
        body, name=name, out_shape=thru,
        in_specs=[HBM_SPEC] * (2 * n) + [SEM_SPEC, SEM_SPEC, ANY_SPEC], out_specs=[HBM_SPEC] * (2 * n),
        input_output_aliases={i: i for i in range(2 * n)},
        compiler_params=pltpu.CompilerParams(has_side_effects=EFFECT),
    )(*srcs, *lands, send, recv, after)
    return list(res[:n]), list(res[n:])


PAIR_TILE_ELEMS = 1 << 19


def _pair_fill(name, gathered, mine, core, others, chip):
    k, r, C = gathered.shape
    half = r // 2
    tr = _row_tile(half, C, PAIR_TILE_ELEMS, mult=16)
    nt = half // tr
    n_far = N_CHIPS - 1

    def body(c_ref, o_ref, chip_ref, in_ref, mine_ref, out_ref, slot, send, recv):
        j = pl.program_id(0)
        b = (j * nt + pl.program_id(1)) % 2
        x, y, c = _place()
        cp = pltpu.make_async_remote_copy(src_ref=in_ref, dst_ref=slot.at[b], send_sem=send.at[b],
                                          recv_sem=recv.at[b], device_id=(x, y, 1 - c), device_id_type=MESH)

        @pl.when(j < n_far)
        def _():
            cp.start()
            cp.wait_recv()
            out_ref[...] = slot[b]
            cp.wait_send()

        @pl.when(j >= n_far)
        def _():
            out_ref[...] = mine_ref[...]

    def far(j):
        return jnp.minimum(j, n_far - 1)

    grid_spec = pltpu.PrefetchScalarGridSpec(
        num_scalar_prefetch=3, grid=(n_far + 2, nt),
        in_specs=[pl.BlockSpec((tr, C), lambda j, i, c, o, m: (
                      (2 * o[far(j)] + c[0]) * nt + jnp.where(j < n_far, i, nt - 1), 0)),
                  pl.BlockSpec((tr, C), lambda j, i, c, o, m: (jnp.where(j < n_far, 0, (j - n_far) * nt + i), 0))],
        out_specs=pl.BlockSpec((tr, C), lambda j, i, c, o, m: (
            jnp.where(j < n_far, 2 * o[far(j)] + 1 - c[0], 2 * m[0] + j - n_far) * nt + i, 0)),
        scratch_shapes=[pltpu.VMEM((2, tr, C), gathered.dtype), pltpu.SemaphoreType.DMA((2,)),
                        pltpu.SemaphoreType.DMA((2,))])
    out = pl.pallas_call(body, name=name, grid_spec=grid_spec, out_shape=_sds((k * r, C), gathered.dtype),
                         input_output_aliases={3: 0}, compiler_params=_params(2))(
                             core, others, chip, gathered.reshape(k * r, C), mine)
    return out.reshape(k, r, C)


def _pair_reduce(name, g, core):
    k, R, C = g.shape
    half = R // 2
    tr = _row_tile(half, C, PAIR_TILE_ELEMS)
    nt = half // tr

    def body(c_ref, mine_ref, give_ref, out_ref, wire_ref, slot, send, recv):
        b = (pl.program_id(0) * nt + pl.program_id(1)) % 2
        x, y, c = _place()
        cp = pltpu.make_async_remote_copy(src_ref=give_ref, dst_ref=slot.at[b], send_sem=send.at[b],
                                          recv_sem=recv.at[b], device_id=(x, y, 1 - c), device_id_type=MESH)
        cp.start()
        cp.wait_recv()
        tot = mine_ref[...] + slot[b]
        out_ref[...] = tot
        wire_ref[...] = tot.astype(wire_ref.dtype)
        cp.wait_send()

    blk = (tr, C)
    out_spec = pl.BlockSpec(blk, lambda s, i, c: (s * nt + i, 0))
    grid_spec = pltpu.PrefetchScalarGridSpec(
        num_scalar_prefetch=1, grid=(k, nt),
        in_specs=[pl.BlockSpec(blk, lambda s, i, c: ((2 * s + c[0]) * nt + i, 0)),
                  pl.BlockSpec(blk, lambda s, i, c: ((2 * s + 1 - c[0]) * nt + i, 0))],
        out_specs=[out_spec, out_spec],
        scratch_shapes=[pltpu.VMEM((2, tr, C), F32), pltpu.SemaphoreType.DMA((2,)), pltpu.SemaphoreType.DMA((2,))])
    g2 = g.reshape(k * R, C)
    out, wire = pl.pallas_call(body, name=name, grid_spec=grid_spec,
                               out_shape=[_sds((k * half, C), F32), _sds((k * half, C), CDT)],
                               compiler_params=_params(2))(core, g2, g2)
    return out, wire.reshape(k, half, C)


def _pair_share(name, f, core):
    half, C = f.shape
    tr = _row_tile(half, C, PAIR_TILE_ELEMS)
    nt = half // tr

    def body(c_ref, f_ref, out_ref, slot, send, recv):
        p = pl.program_id(1)
        b = pl.program_id(0) % 2
        x, y, c = _place()
        cp = pltpu.make_async_remote_copy(src_ref=f_ref, dst_ref=slot.at[b], send_sem=send.at[b],
                                          recv_sem=recv.at[b], device_id=(x, y, 1 - c), device_id_type=MESH)

        @pl.when(p == 0)
        def _():
            cp.start()
            out_ref[...] = f_ref[...]

        @pl.when(p == 1)
        def _():
            cp.wait_recv()
            out_ref[...] = slot[b]
            cp.wait_send()

    grid_spec = pltpu.PrefetchScalarGridSpec(
        num_scalar_prefetch=1, grid=(nt, 2),
        in_specs=[pl.BlockSpec((tr, C), lambda i, p, c: (i, 0))],
        out_specs=pl.BlockSpec((tr, C), lambda i, p, c: (jnp.where(p == 0, c[0], 1 - c[0]) * nt + i, 0)),
        scratch_shapes=[pltpu.VMEM((2, tr, C), F32), pltpu.SemaphoreType.DMA((2,)), pltpu.SemaphoreType.DMA((2,))])
    return pl.pallas_call(body, name=name, grid_spec=grid_spec, out_shape=_sds((2 * half, C), F32),
                          compiler_params=_params(2))(core, f)


def _all_reduce_small(v):
    r, cdim = v.shape

    def body(v_ref, o_ref, buf, send, recv):
        x, y, c = _place()
        me = 4 * x + 2 * y + c
        buf[me] = v_ref[...]
        sends = []
        for m in range(1, N_DEV):
            px = 1 - x if m & 4 else x
            py = 1 - y if m & 2 else y
            pc = 1 - c if m & 1 else c
            cp = pltpu.make_async_remote_copy(src_ref=v_ref, dst_ref=buf.at[me], send_sem=send.at[m - 1],
                                              recv_sem=recv.at[m - 1], device_id=(px, py, pc), device_id_type=MESH)
            cp.start()
            sends.append((cp, 4 * px + 2 * py + pc))
        for m, (cp, pidx) in enumerate(sends):
            pltpu.make_async_remote_copy(src_ref=v_ref, dst_ref=buf.at[pidx], send_sem=send.at[m], recv_sem=recv.at[m],
                                         device_id=(x, y, c), device_id_type=MESH).wait_recv()
        for cp, _ in sends:
            cp.wait_send()
        tot = buf[0]
        for k in range(1, N_DEV):
            tot = tot + buf[k]
        o_ref[...] = tot

    return pl.pallas_call(
        body, name="all_reduce_small", in_specs=[VMEM_SPEC], out_specs=VMEM_SPEC,
        out_shape=_sds((r, cdim), F32),
        scratch_shapes=[pltpu.VMEM((N_DEV, r, cdim), F32), pltpu.SemaphoreType.DMA((N_DEV - 1,)),
                        pltpu.SemaphoreType.DMA((N_DEV - 1,))],
    )(v)


def _row_tile(rows, cols, budget_elems=1 << 18, mult=8):
    if rows % mult:
        return rows
    t = max(mult, (budget_elems // cols) // mult * mult)
    while rows % t:
        t -= mult
    return t


def _sum4(name, own, by_chip, chip, others):
    k, R, C = by_chip.shape
    tr = _row_tile(R, C, mult=16)
    nt = R // tr

    def body(chip_ref, others_ref, own_ref, a_ref, b_ref, c_ref, o_ref):
        o_ref[...] = ((own_ref[...] + a_ref[...].astype(F32)) + b_ref[...].astype(F32)) + c_ref[...].astype(F32)

    def from_chip(j):
        return pl.BlockSpec((tr, C), lambda i, chip, oth: (oth[j] * nt + i, 0))

    grid_spec = pltpu.PrefetchScalarGridSpec(
        num_scalar_prefetch=2, grid=(nt,),
        in_specs=[pl.BlockSpec((tr, C), lambda i, chip, oth: (chip[0] * nt + i, 0)),
                  from_chip(0), from_chip(1), from_chip(2)],
        out_specs=pl.BlockSpec((tr, C), lambda i, chip, oth: (i, 0)))
    by2 = by_chip.reshape(k * R, C)
    return pl.pallas_call(body, name=name, grid_spec=grid_spec, out_shape=_sds((R, C), F32),
                          compiler_params=_params(1))(chip, others, own, by2, by2, by2)


def _adamw(name, w, g, m, v):
    R, C = w.shape
    tr = _row_tile(R, C, 1 << 17)

    def body(w_ref, g_ref, m_ref, v_ref, d_ref, nm_ref, nv_ref):
        gv = g_ref[...]
        nm = ADAM_B1 * m_ref[...] + (1.0 - ADAM_B1) * gv
        nv = ADAM_B2 * v_ref[...] + (1.0 - ADAM_B2) * (gv * gv)
        m_hat = nm / (1.0 - ADAM_B1 ** ADAM_STEP)
        v_hat = nv / (1.0 - ADAM_B2 ** ADAM_STEP)
        d_ref[...] = -ADAM_LR * (m_hat / (jnp.sqrt(v_hat) + ADAM_EPS) + ADAM_WD * w_ref[...])
        nm_ref[...] = nm
        nv_ref[...] = nv

    spec = pl.BlockSpec((tr, C), lambda i: (i, 0))
    return pl.pallas_call(body, name=name, grid=(R // tr,), in_specs=[spec] * 4, out_specs=[spec] * 3,
                          out_shape=[_sds((R, C), F32)] * 3, compiler_params=_params(1))(w, g, m, v)


BIG = ("w_in", "w_proj_a", "w_proj_b", "w_out", "w_up", "w_down")
COL_SHARDED = ("w_in", "w_proj_a", "w_up")
SMALL = ("norm1_g", "q_norm_g", "k_norm_g", "ret_gn_g", "ret_gn_b", "norm2_g")
ALL_W = ("norm1_g", "w_in", "q_norm_g", "k_norm_g", "ret_gn_g", "ret_gn_b", "w_proj_a", "w_proj_b", "w_out",
         "norm2_g", "w_up", "w_down")
LANES = 128


def _to_full(name, gathered):
    k, r, c = gathered.shape
    if name in COL_SHARDED:
        return gathered.transpose(1, 0, 2).reshape(r, k * c)
    return gathered.reshape(k * r, c)


def _to_shard_major(name, full):
    if name in COL_SHARDED:
        r, c4 = full.shape
        return full.reshape(r, N_CHIPS, c4 // N_CHIPS).transpose(1, 0, 2)
    r4, c = full.shape
    return full.reshape(N_CHIPS, r4 // N_CHIPS, c)


def kernel(x, norm1_g, w_in, q_norm_g, k_norm_g, ret_gn_g, ret_gn_b, w_proj_a, w_proj_b, w_out, norm2_g, w_up, w_down, loss_target, m_norm1_g, m_w_in, m_q_norm_g, m_k_norm_g, m_ret_gn_g, m_ret_gn_b, m_w_proj_a, m_w_proj_b, m_w_out, m_norm2_g, m_w_up, m_w_down, v_norm1_g, v_w_in, v_q_norm_g, v_k_norm_g, v_ret_gn_g, v_ret_gn_b, v_w_proj_a, v_w_proj_b, v_w_out, v_norm2_g, v_w_up, v_w_down):
    weights = dict(norm1_g=norm1_g, w_in=w_in, q_norm_g=q_norm_g, k_norm_g=k_norm_g, ret_gn_g=ret_gn_g,
                   ret_gn_b=ret_gn_b, w_proj_a=w_proj_a, w_proj_b=w_proj_b, w_out=w_out, norm2_g=norm2_g,
                   w_up=w_up, w_down=w_down)
    moments_m = dict(norm1_g=m_norm1_g, w_in=m_w_in, q_norm_g=m_q_norm_g, k_norm_g=m_k_norm_g, ret_gn_g=m_ret_gn_g,
                     ret_gn_b=m_ret_gn_b, w_proj_a=m_w_proj_a, w_proj_b=m_w_proj_b, w_out=m_w_out,
                     norm2_g=m_norm2_g, w_up=m_w_up, w_down=m_w_down)
    moments_v = dict(norm1_g=v_norm1_g, w_in=v_w_in, q_norm_g=v_q_norm_g, k_norm_g=v_k_norm_g, ret_gn_g=v_ret_gn_g,
                     ret_gn_b=v_ret_gn_b, w_proj_a=v_w_proj_a, w_proj_b=v_w_proj_b, w_out=v_w_out,
                     norm2_g=v_norm2_g, w_up=v_w_up, w_down=v_w_down)

    mx, my = lax.axis_index("x"), lax.axis_index("y")
    core = lax.axis_index("c").astype(jnp.int32).reshape(1)
    chip = (2 * mx + my).astype(jnp.int32).reshape(1)
    others = jnp.stack([2 * (1 - mx) + my, 2 * mx + 1 - my, 2 * (1 - mx) + 1 - my]).astype(jnp.int32)
    shards = {n: weights[n][0].astype(CDT) for n in BIG}
    def start_gather(name, names):
        return _exchange_start(name, "gather", [shards[n] for n in names],
                               [((N_CHIPS,) + shards[n].shape, CDT) for n in names])

    i_send, i_recv, i_srcs, i_lands, i_token = start_gather("gather_w_in_start", ["w_in"])
    late = [n for n in BIG if n != "w_in"]
    l_send, l_recv, l_srcs, l_lands, l_token = start_gather("gather_late_start", late)

    def far_w_in(after):
        srcs, lands = _exchange_wait("gather_w_in_wait", "gather", i_send, i_recv, i_srcs, i_lands, after)
        return _pair_fill("pair_fill_w_in", lands[0], srcs[0], core, others, chip)

    def late_weights(after):
        srcs, lands = _exchange_wait("gather_late_wait", "gather", l_send, l_recv, l_srcs, l_lands, after)
        out = {}
        for n, mine, land in zip(late, srcs, lands):
            out[n] = _to_full(n, _pair_fill("pair_fill_%s" % n, land, mine, core, others, chip))
        return out

    pending = []

    def on_grads(group):
        names = list(group)
        red = [_pair_reduce("pair_reduce_%s" % n, g if g.ndim == 3 else _to_shard_major(n, g), core)
               for n, g in group.items()]
        wires = [wire for _, wire in red]
        send, recv, srcs, lands, token = _exchange_start(
            "scatter_start_%s" % names[0], "scatter", wires, [(wire.shape, wire.dtype) for wire in wires])
        pending.append((names, [own for own, _ in red], send, recv, srcs, lands))
        return token

    small = {n: weights[n].reshape(1, -1) for n in SMALL}

    loss, grad_x, small_g = _local_step(x[0], loss_target[0], i_srcs[0], chip, others, far_w_in, small,
                                        late_weights, on_grads, deps0=[i_token, l_token])
    loss = lax.psum(loss, ("x", "y", "c"))

    grads = {}
    for names, owns, send, recv, srcs, lands in pending:
        _, got = _exchange_wait("scatter_wait_%s" % names[0], "scatter", send, recv, srcs, lands, grad_x)
        for n, own, by_chip in zip(names, owns, got):
            half = _sum4("chip_sum_%s" % n, own, by_chip, chip, others)
            grads[n] = _pair_share("pair_share_%s" % n, half, core)

    packed = jnp.concatenate([small_g[n].reshape(1, -1) for n in SMALL], axis=1)
    sizes = [small_g[n].size for n in SMALL]
    red = _all_reduce_small(packed.reshape(-1, LANES)).reshape(1, -1)
    off = 0
    for n, sz in zip(SMALL, sizes):
        grads[n] = red[:, off:off + sz]
        off += sz

    out_g, out_d, out_m, out_v = {}, {}, {}, {}
    for n in ALL_W:
        shape = weights[n].shape
        two_d = (shape[-2], shape[-1]) if n in BIG else (1, weights[n].size)
        g2 = grads[n].reshape(two_d)
        d, nm, nv = _adamw("adamw_%s" % n, weights[n].reshape(two_d), g2, moments_m[n].reshape(two_d),
                           moments_v[n].reshape(two_d))
        out_g[n], out_d[n], out_m[n], out_v[n] = (t.reshape(shape) for t in (g2, d, nm, nv))

    return (loss, grad_x[None], *[out_g[n] for n in ALL_W], *[out_d[n] for n in ALL_W],
            *[out_m[n] for n in ALL_W], *[out_v[n] for n in ALL_W])
```

```python
import functools
import math

import jax
import jax.numpy as jnp
from jax import lax
from jax.experimental import pallas as pl
from jax.experimental.pallas import tpu as pltpu

CDT = jnp.bfloat16
F32 = jnp.float32
EPS = 1e-6

ATT_GROUPS = ((128, 1), (512, 4), (2048, 16))
ATT_HPG = 4
ATT_HEADS = 12
HD = 128
BLK = 128
ATT_W = ATT_HEADS * HD
GW = ATT_HPG * HD
RET_HEADS = 4

ADAM_LR = 0.001
ADAM_B1 = 0.9
ADAM_B2 = 0.999
ADAM_EPS = 1e-08
ADAM_WD = 0.01
ADAM_STEP = 10

VMEM_LIMIT_BYTES = 56 * 1024 * 1024
MXU_DIM = 256
MESH = pl.DeviceIdType.MESH
HBM_SPEC = pl.BlockSpec(memory_space=pltpu.HBM)
VMEM_SPEC = pl.BlockSpec(memory_space=pltpu.VMEM)


def _params(n_axes):
    return pltpu.CompilerParams(dimension_semantics=("arbitrary",) * n_axes,
                                vmem_limit_bytes=VMEM_LIMIT_BYTES)


def _dot_nn(a, b):
    return jnp.dot(a, b, preferred_element_type=F32)


def _dot_nt(a, b):
    return lax.dot_general(a, b, (((1,), (1,)), ((), ())), preferred_element_type=F32)


def _dot_tn(a, b):
    return lax.dot_general(a, b, (((0,), (0,)), ((), ())), preferred_element_type=F32)


def _sigmoid(v):
    return 1.0 / (1.0 + jnp.exp(-v))


def _matmul(name, mode, a, b, *, tm, tn, tk, extras=(), outs, epilogue, deps=(), b_spec=None, n_cols=None,
            prefetch=(), alias_dep_to_out=None, j_outer=False):
    deps = [d for d in deps if d is not None]
    if mode == "tn":
        K, M = a.shape
    else:
        M, K = a.shape
    if b_spec is None:
        (N, K2) = b.shape if mode == "nt" else b.shape[::-1]
        assert K == K2, (name, a.shape, b.shape)
        if mode == "nt":
            b_spec = pl.BlockSpec((tn, tk), lambda i, j, k, *p: (j, k))
        else:
            b_spec = pl.BlockSpec((tk, tn), lambda i, j, k, *p: (k, j))
    else:
        N = n_cols
    assert M % tm == 0 and N % tn == 0 and K % tk == 0, (name, a.shape, b.shape)
    ni, nj, nk = M // tm, N // tn, K // tk
    if mode == "tn":
        a_spec = pl.BlockSpec((tk, tm), lambda i, j, k, *p: (k, i))
    else:
        a_spec = pl.BlockSpec((tm, tk), lambda i, j, k, *p: (i, k))
    dot = {"nn": _dot_nn, "nt": _dot_nt, "tn": _dot_tn}[mode]
    n_ex, n_out, n_dep, n_pre = len(extras), len(outs), len(deps), len(prefetch)
    grid = (ni, nj, nk)
    if j_outer:
        grid = (nj, ni, nk)

        def swapped(spec):
            return pl.BlockSpec(spec.block_shape, lambda j, i, k, *p: spec.index_map(i, j, k, *p))

        a_spec, b_spec = swapped(a_spec), swapped(b_spec)
        extras = [(e, swapped(s)) for e, s in extras]
        outs = [(o, swapped(s)) for o, s in outs]

    def body(*refs):
        refs = refs[n_pre:]
        a_ref, b_ref = refs[0], refs[1]
        ex = refs[2:2 + n_ex]
        out = refs[2 + n_ex + n_dep:2 + n_ex + n_dep + n_out]
        acc = refs[-1] if nk > 1 else None
        i = pl.program_id(1 if j_outer else 0)
        k = pl.program_id(2)
        if nk == 1:
            epilogue(dot(a_ref[...].astype(CDT), b_ref[...].astype(CDT)), ex, out, i)
            return

        @pl.when(k == 0)
        def _():
            acc[...] = jnp.zeros_like(acc)

        acc[...] += dot(a_ref[...].astype(CDT), b_ref[...].astype(CDT))

        @pl.when(k == nk - 1)
        def _():
            epilogue(acc[...], ex, out, i)

    grid_spec = pltpu.PrefetchScalarGridSpec(
        num_scalar_prefetch=n_pre, grid=grid,
        in_specs=[a_spec, b_spec] + [s for _, s in extras] + [pl.BlockSpec(memory_space=pl.ANY)] * n_dep,
        out_specs=[s for _, s in outs],
        scratch_shapes=[pltpu.VMEM((tm, tn), F32)] if nk > 1 else [])
    aliases = {}
    if alias_dep_to_out is not None:
        aliases = {n_pre + 2 + n_ex + alias_dep_to_out[0]: alias_dep_to_out[1]}
    res = pl.pallas_call(
        body, name=name, grid_spec=grid_spec, out_shape=[o for o, _ in outs], input_output_aliases=aliases,
        compiler_params=_params(3),
    )(*prefetch, a, b, *[e for e, _ in extras], *deps)
    return res


def _mn(tm, tn, col_off=0):
    return pl.BlockSpec((tm, tn), lambda i, j, k, *p: (i, j + col_off))


def _row(tn):
    return pl.BlockSpec((1, tn), lambda i, j, k, *p: (0, j))


def _ep_store(acc, ex, out, i):
    out[0][...] = acc.astype(out[0].dtype)


def _ep_resid(acc, ex, out, i):
    out[0][...] = ex[0][...] + acc


def _ep_up(acc, ex, out, i):
    out[0][...] = acc.astype(out[0].dtype)
    r = jnp.maximum(acc, 0.0)
    out[1][...] = (r * r).astype(out[1].dtype)


def _ep_down_loss(acc, ex, out, i, inv_d):
    diff = (ex[0][...] + acc) - ex[1][...]
    dx2 = diff * inv_d
    out[0][...] = dx2
    out[1][...] = dx2.astype(out[1].dtype)

    @pl.when(i == 0)
    def _():
        out[2][...] = jnp.zeros_like(out[2])

    out[2][...] += jnp.sum(diff * diff, axis=0, keepdims=True)


def _ep_dh(acc, ex, out, i):
    h = ex[0][...].astype(F32)
    out[0][...] = (acc * (2.0 * jnp.maximum(h, 0.0))).astype(out[0].dtype)


def _ep_rms_bwd(acc, ex, out, i):
    x = ex[0][...]
    g = ex[1][...]
    rstd = lax.rsqrt(jnp.mean(x * x, axis=-1, keepdims=True) + EPS)
    xh = x * rstd
    dxh = acc * g
    dx = ex[2][...] + rstd * (dxh - xh * jnp.mean(dxh * xh, axis=-1, keepdims=True))
    out[0][...] = dx
    out[1][...] = dx.astype(out[1].dtype)

    @pl.when(i == 0)
    def _():
        out[2][...] = jnp.zeros_like(out[2])

    out[2][...] += jnp.sum(acc * xh, axis=0, keepdims=True)


def _ep_gates(acc, ex, out, i):
    sa = _sigmoid(ex[0][...].astype(F32))
    sb = _sigmoid(ex[1][...].astype(F32))
    dpa = acc * sa
    dpb = acc * sb
    out[0][...] = dpa.astype(out[0].dtype)
    out[1][...] = dpb.astype(out[1].dtype)
    out[2][...] = (dpa * ex[2][...].astype(F32) * (1.0 - sa)).astype(out[2].dtype)
    out[3][...] = (dpb * ex[3][...].astype(F32) * (1.0 - sb)).astype(out[3].dtype)


def _sds(shape, dtype):
    return jax.ShapeDtypeStruct(shape, dtype)


def _rms_fwd(name, x, g, tm=512):
    S, D = x.shape

    def body(x_ref, g_ref, o_ref):
        xv = x_ref[...]
        rstd = lax.rsqrt(jnp.mean(xv * xv, axis=-1, keepdims=True) + EPS)
        o_ref[...] = (xv * rstd * g_ref[...]).astype(o_ref.dtype)

    return pl.pallas_call(
        body, name=name, grid=(S // tm,),
        in_specs=[pl.BlockSpec((tm, D), lambda i: (i, 0)), pl.BlockSpec((1, D), lambda i: (0, 0))],
        out_specs=pl.BlockSpec((tm, D), lambda i: (i, 0)),
        out_shape=_sds((S, D), CDT), compiler_params=_params(1))(x, g)


def _rm_shape(S, d, width):
    return (S, width) if d == 1 else (d, S // d, width)


def _rm_spec(tm, d, width):
    if d == 1:
        return pl.BlockSpec((tm, width), lambda i: (i, 0))
    return pl.BlockSpec((d, tm // d, width), lambda i: (0, i, 0))


def _rm_put(dst_ref, cols, buf_ref, d):
    if d == 1:
        dst_ref[:, cols] = buf_ref[...].astype(dst_ref.dtype)
        return
    m = buf_ref.shape[0] // d
    for r in range(d):
        dst_ref[r, :, cols] = buf_ref[pl.ds(r, m, stride=d), :].astype(dst_ref.dtype)


def _rm_reader(buf_ref, src_ref, d):
    if d == 1:
        return lambda s: src_ref[:, s * HD:(s + 1) * HD].astype(F32)
    m = buf_ref.shape[1] // d
    for s in range(buf_ref.shape[0]):
        for r in range(d):
            buf_ref.at[s][pl.ds(r, m, stride=d), :] = src_ref[r, :, s * HD:(s + 1) * HD].astype(F32)
    return lambda s: buf_ref[s]


def _qknorm_fwd(proj, gqk, tm=512):
    S = proj.shape[0]
    W = 2 * ATT_W
    dil = [d for _, d in ATT_GROUPS]

    def body(p_ref, g_ref, o0, o1, o2, buf):
        outs = (o0, o1, o2)
        for hd in range(3 * ATT_HEADS):
            which, head = hd // ATT_HEADS, hd % ATT_HEADS
            grp, slot = head // ATT_HPG, head % ATT_HPG
            v = p_ref[:, hd * HD:(hd + 1) * HD].astype(F32)
            if which < 2:
                rstd = lax.rsqrt(jnp.mean(v * v, axis=-1, keepdims=True) + EPS)
                v = v * rstd * g_ref[:, hd * HD:(hd + 1) * HD]
            buf[...] = v
            _rm_put(outs[grp], slice(which * GW + slot * HD, which * GW + (slot + 1) * HD), buf, dil[grp])

    return pl.pallas_call(
        body, name="qknorm_fwd", grid=(S // tm,),
        in_specs=[pl.BlockSpec((tm, 3 * ATT_W), lambda i: (i, 0)), pl.BlockSpec((1, W), lambda i: (0, 0))],
        out_specs=[_rm_spec(tm, d, 3 * GW) for d in dil],
        out_shape=[_sds(_rm_shape(S, d, 3 * GW), CDT) for d in dil],
        scratch_shapes=[pltpu.VMEM((tm, HD), F32)],
        compiler_params=_params(1))(proj, gqk)


def _qknorm_bwd(proj, gqk, dqs, dks, dvs, dproj, tm=256):
    S = proj.shape[0]
    W = 2 * ATT_W
    dil = [d for _, d in ATT_GROUPS]

    def body(p_ref, g_ref, *refs):
        ins = refs[0:9]
        o_ref, dg_ref = refs[10], refs[11]
        bufs = refs[12:21]
        i = pl.program_id(0)

        @pl.when(i == 0)
        def _():
            dg_ref[...] = jnp.zeros_like(dg_ref)

        nat = [_rm_reader(bufs[j], ins[j], dil[j % 3]) for j in range(9)]
        dq_get, dk_get, dv_get = nat[0:3], nat[3:6], nat[6:9]
        for hd in range(2 * ATT_HEADS):
            sl = slice(hd * HD, (hd + 1) * HD)
            head = hd % ATT_HEADS
            grp, slot = head // ATT_HPG, head % ATT_HPG
            dn = (dq_get if hd < ATT_HEADS else dk_get)[grp](slot)
            v = p_ref[:, sl].astype(F32)
            rstd = lax.rsqrt(jnp.mean(v * v, axis=-1, keepdims=True) + EPS)
            vh = v * rstd
            dg_ref[:, sl] += jnp.sum(dn * vh, axis=0, keepdims=True)
            dvh = dn * g_ref[:, sl]
            o_ref[:, sl] = (rstd * (dvh - vh * jnp.mean(dvh * vh, axis=-1, keepdims=True))).astype(o_ref.dtype)
        for head in range(ATT_HEADS):
            grp, slot = head // ATT_HPG, head % ATT_HPG
            o_ref[:, W + head * HD:W + (head + 1) * HD] = dv_get[grp](slot).astype(o_ref.dtype)

    return pl.pallas_call(
        body, name="qknorm_bwd", grid=(S // tm,),
        in_specs=[pl.BlockSpec((tm, W), lambda i: (i, 0)), pl.BlockSpec((1, W), lambda i: (0, 0))]
        + [_rm_spec(tm, d, GW) for d in dil] * 3 + [pl.BlockSpec(memory_space=pl.ANY)],
        out_specs=[pl.BlockSpec((tm, 3 * ATT_W), lambda i: (i, 0)), pl.BlockSpec((1, W), lambda i: (0, 0))],
        out_shape=[_sds(dproj.shape, dproj.dtype), _sds((1, W), F32)],
        scratch_shapes=[pltpu.VMEM((ATT_HPG, tm, HD), F32)] * 9,
        input_output_aliases={11: 0},
        compiler_params=_params(1))(proj, gqk, *dqs, *dks, *dvs, dproj)


def _att_mask(n):
    qi = lax.broadcasted_iota(jnp.int32, (BLK, 2 * BLK), 0)
    kj = lax.broadcasted_iota(jnp.int32, (BLK, 2 * BLK), 1)
    dist = BLK + qi - kj
    valid = (dist >= 0) & (dist <= BLK) & ((kj >= BLK) | (n > 0))
    return valid, dist.astype(F32)


def _att_slopes(grp):
    return [2.0 ** (-8.0 * (grp * ATT_HPG + hh + 1) / ATT_HEADS) for hh in range(ATT_HPG)]


def _att_spec(d, row_fn, col=0):
    if d == 1:
        return pl.BlockSpec((BLK, GW), lambda r, n: (row_fn(n), col))
    return pl.BlockSpec((None, BLK, GW), lambda r, n: (r, row_fn(n), col))


def _att_qkv_specs(d, nb):
    last = nb - 1

    def cur(n):
        return jnp.minimum(n, last)

    def prev(n):
        return jnp.maximum(jnp.minimum(n, last) - 1, 0)

    return [_att_spec(d, cur, 0), _att_spec(d, prev, 1), _att_spec(d, cur, 1), _att_spec(d, prev, 2),
            _att_spec(d, cur, 2)]


def _att_fwd(grp, S, qkv):
    _, d = ATT_GROUPS[grp]
    L = S // d
    nb = L // BLK
    slopes = _att_slopes(grp)
    scale = HD ** -0.5

    def body(q_ref, kp_ref, kc_ref, vp_ref, vc_ref, o_ref, l_ref, s_buf, p_buf, den_buf):
        n = pl.program_id(1)
        valid, distf = _att_mask(n)
        heads = [slice(hh * HD, (hh + 1) * HD) for hh in range(ATT_HPG)]
        for hh, sl in enumerate(heads):
            k = jnp.concatenate([kp_ref[:, sl], kc_ref[:, sl]], axis=0)
            s_buf[hh] = _dot_nt(q_ref[:, sl], k)
        for hh, sl in enumerate(heads):
            s = s_buf[hh] * scale + (-slopes[hh] * d) * distf
            s = jnp.where(valid, s, -1e30)
            m = jnp.max(s, axis=-1, keepdims=True)
            p = jnp.exp(s - m)
            den = jnp.sum(p, axis=-1, keepdims=True)
            p_buf[hh] = p.astype(CDT)
            den_buf[hh] = jnp.broadcast_to(den, (BLK, HD))
            l_ref[:, sl] = jnp.broadcast_to(m + jnp.log(den), (BLK, HD))
        for hh, sl in enumerate(heads):
            v = jnp.concatenate([vp_ref[:, sl], vc_ref[:, sl]], axis=0)
            o_ref[:, sl] = _dot_nn(p_buf[hh], v) / den_buf[hh]

    out_spec = _att_spec(d, lambda n: n)
    return pl.pallas_call(
        body, name="att_fwd_g%d" % grp, grid=(d, nb),
        in_specs=_att_qkv_specs(d, nb),
        out_specs=[out_spec, out_spec],
        out_shape=[_sds(_rm_shape(S, d, GW), F32)] * 2,
        scratch_shapes=[pltpu.VMEM((ATT_HPG, BLK, 2 * BLK), F32), pltpu.VMEM((ATT_HPG, BLK, 2 * BLK), CDT),
                        pltpu.VMEM((ATT_HPG, BLK, HD), F32)],
        compiler_params=_params(2),
    )(qkv, qkv, qkv, qkv, qkv)


def _att_bwd(grp, S, qkv, lse, do_g, c_g):
    _, d = ATT_GROUPS[grp]
    L = S // d
    nb = L // BLK
    slopes = _att_slopes(grp)
    scale = HD ** -0.5
    last = nb - 1

    def body(q_ref, kp_ref, kc_ref, vp_ref, vc_ref, l_ref, do_ref, c_ref, dq_ref, dk_ref, dv_ref, ck, cv,
             s_buf, dp_buf, p_buf, ds_buf):
        n = pl.program_id(1)

        @pl.when(n == 0)
        def _():
            ck[...] = jnp.zeros_like(ck)
            cv[...] = jnp.zeros_like(cv)

        @pl.when(n < nb)
        def _():
            valid, distf = _att_mask(n)
            heads = [slice(hh * HD, (hh + 1) * HD) for hh in range(ATT_HPG)]
            for hh, sl in enumerate(heads):
                k = jnp.concatenate([kp_ref[:, sl], kc_ref[:, sl]], axis=0)
                v = jnp.concatenate([vp_ref[:, sl], vc_ref[:, sl]], axis=0)
                s_buf[hh] = _dot_nt(q_ref[:, sl], k)
                dp_buf[hh] = _dot_nt(do_ref[:, sl], v)
            for hh, sl in enumerate(heads):
                s = s_buf[hh] * scale + (-slopes[hh] * d) * distf
                p = jnp.where(valid, jnp.exp(s - l_ref[:, sl][:, 0:1]), 0.0)
                p_buf[hh] = p.astype(CDT)
                ds_buf[hh] = (p * (dp_buf[hh] + c_ref[:, sl][:, 0:1]) * scale).astype(CDT)
            for hh, sl in enumerate(heads):
                k = jnp.concatenate([kp_ref[:, sl], kc_ref[:, sl]], axis=0)
                ds = ds_buf[hh]
                dq_ref[:, sl] = _dot_nn(ds, k)
                dk = _dot_tn(ds, q_ref[:, sl])
                dv = _dot_tn(p_buf[hh], do_ref[:, sl])
                dk_ref[:, sl] = ck[:, sl] + dk[0:BLK]
                dv_ref[:, sl] = cv[:, sl] + dv[0:BLK]
                ck[:, sl] = dk[BLK:2 * BLK]
                cv[:, sl] = dv[BLK:2 * BLK]

        @pl.when(n == nb)
        def _():
            dk_ref[...] = ck[...]
            dv_ref[...] = cv[...]

    blk = (BLK, GW)
    at_q = _att_spec(d, lambda n: jnp.minimum(n, last))
    behind = _att_spec(d, lambda n: jnp.maximum(n - 1, 0))
    return pl.pallas_call(
        body, name="att_bwd_g%d" % grp, grid=(d, nb + 1),
        in_specs=_att_qkv_specs(d, nb) + [at_q, at_q, at_q],
        out_specs=[at_q, behind, behind],
        out_shape=[_sds(_rm_shape(S, d, GW), F32)] * 3,
        scratch_shapes=[pltpu.VMEM(blk, F32), pltpu.VMEM(blk, F32),
                        pltpu.VMEM((ATT_HPG, BLK, 2 * BLK), F32), pltpu.VMEM((ATT_HPG, BLK, 2 * BLK), F32),
                        pltpu.VMEM((ATT_HPG, BLK, 2 * BLK), CDT), pltpu.VMEM((ATT_HPG, BLK, 2 * BLK), CDT)],
        compiler_params=_params(2),
    )(qkv, qkv, qkv, qkv, qkv, lse, do_g, c_g)


def _mix_alpha(l0, l1, l2):
    mx = jnp.maximum(jnp.maximum(l0, l1), l2)
    e = [jnp.exp(l0 - mx), jnp.exp(l1 - mx), jnp.exp(l2 - mx)]
    tot = e[0] + e[1] + e[2]
    return [ei / tot for ei in e]


def _mix_fwd(S, os_, ls_, tm=512):
    dil = [d for _, d in ATT_GROUPS]

    def body(*refs):
        out, bufs = refs[6], refs[7:13]
        get = [_rm_reader(bufs[j], refs[j], dil[j % 3]) for j in range(6)]
        for s in range(ATT_HPG):
            al = _mix_alpha(*[get[3 + g](s) for g in range(3)])
            mixed = al[0] * get[0](s) + al[1] * get[1](s) + al[2] * get[2](s)
            out[:, s * HD:(s + 1) * HD] = mixed.astype(out.dtype)

    specs = [_rm_spec(tm, d, GW) for d in dil]
    return pl.pallas_call(
        body, name="mix_fwd", grid=(S // tm,), in_specs=specs * 2, out_specs=pl.BlockSpec((tm, GW), lambda i: (i, 0)),
        out_shape=_sds((S, GW), CDT), scratch_shapes=[pltpu.VMEM((ATT_HPG, tm, HD), F32)] * 6,
        compiler_params=_params(1))(*os_, *ls_)


def _mix_bwd(S, os_, ls_, do_a, tm=512):
    dil = [d for _, d in ATT_GROUPS]

    def body(*refs):
        d_ref, outs, bufs, tmp = refs[6], refs[7:13], refs[13:19], refs[19]
        get = [_rm_reader(bufs[j], refs[j], dil[j % 3]) for j in range(6)]
        for s in range(ATT_HPG):
            cols = slice(s * HD, (s + 1) * HD)
            al = _mix_alpha(*[get[3 + g](s) for g in range(3)])
            dv = d_ref[:, cols]
            o_a = al[0] * get[0](s) + al[1] * get[1](s) + al[2] * get[2](s)
            dsum = jnp.sum(dv * o_a, axis=-1, keepdims=True)
            for g in range(3):
                tmp[...] = al[g] * dv
                _rm_put(outs[g], cols, tmp, dil[g])
                tmp[...] = -(al[g] * dsum)
                _rm_put(outs[3 + g], cols, tmp, dil[g])

    specs = [_rm_spec(tm, d, GW) for d in dil]
    res = pl.pallas_call(
        body, name="mix_bwd", grid=(S // tm,), in_specs=specs * 2 + [pl.BlockSpec((tm, GW), lambda i: (i, 0))],
        out_specs=specs * 2,
        out_shape=[_sds(_rm_shape(S, d, GW), CDT) for d in dil] + [_sds(_rm_shape(S, d, GW), F32) for d in dil],
        scratch_shapes=[pltpu.VMEM((ATT_HPG, tm, HD), F32)] * 6 + [pltpu.VMEM((tm, HD), F32)],
        compiler_params=_params(1))(*os_, *ls_, do_a)
    return res[:3], res[3:]


def _ret_tables(dk):
    H, C = RET_HEADS, BLK
    log_g = jnp.log(1.0 - 2.0 ** (-5.0 - jnp.arange(H, dtype=F32)))
    idx = jnp.arange(C, dtype=F32)
    diff = idx[:, None] - idx[None, :]
    decay = jnp.where(diff >= 0, jnp.exp(log_g[:, None, None] * jnp.maximum(diff, 0.0)), 0.0)
    xi = jnp.exp(log_g[:, None] * (idx[None, :] + 1.0))
    zeta = jnp.exp(log_g[:, None] * (C - 1.0 - idx[None, :])) * (dk ** -0.5)
    g_chunk = jnp.exp(log_g * C)
    bc = lambda t: jnp.broadcast_to(t[:, :, None], (H, C, C))
    return decay, bc(xi), bc(zeta), jnp.broadcast_to(g_chunk[:, None, None], (H, 8, C))


def _gn_fwd(o, g, b):
    mu = jnp.mean(o, axis=-1, keepdims=True)
    xc = o - mu
    rstd = lax.rsqrt(jnp.mean(xc * xc, axis=-1, keepdims=True) + EPS)
    yh = xc * rstd
    return yh, rstd, yh * g + b


def _ret_specs(dk, dv, order):
    H = RET_HEADS
    qk_w, v_w = H * dk, H * dv
    off_q = 3 * ATT_W
    off_k, off_v, off_g = off_q + qk_w, off_q + 2 * qk_w, off_q + 2 * qk_w + v_w
    assert 2 * dk == dv and all(off % dv == 0 for off in (off_q, off_k, off_v, off_g))

    def col(off, j):
        return pl.BlockSpec((BLK, dv), lambda i: (order(i), off // dv + j))

    tab = pl.BlockSpec((H, BLK, BLK), lambda i: (0, 0, 0))
    return ([col(off_q, j) for j in range(H // 2)] + [col(off_k, j) for j in range(H // 2)]
            + [col(off_v, j) for j in range(H)] + [col(off_g, j) for j in range(H)]
            + [tab, tab, tab, pl.BlockSpec((H, 8, BLK), lambda i: (0, 0, 0))])


def _ret_heads(refs, dk):
    H = RET_HEADS
    q_refs, k_refs = refs[0:H // 2], refs[H // 2:H]
    v_refs, gr_refs = refs[H:2 * H], refs[2 * H:3 * H]

    def head(h):
        cols = slice((h % 2) * dk, (h % 2 + 1) * dk)
        return q_refs[h // 2][:, cols], k_refs[h // 2][:, cols], v_refs[h][...], gr_refs[h][...]

    return head, refs[3 * H:3 * H + 4]


def _ret_fwd(proj, gn_g, gn_b, dk, dv):
    S = proj.shape[0]
    N = S // BLK
    H = RET_HEADS
    kscale = dk ** -0.5
    n_in = 3 * H + 4

    def body(*refs):
        head, (dec_ref, xi_ref, zeta_ref, gc_ref) = _ret_heads(refs, dk)
        g_ref, b_ref, opre_ref, or_ref, st_ref, state = refs[n_in:n_in + 6]
        n = pl.program_id(0)

        @pl.when(n == 0)
        def _():
            state[...] = jnp.zeros_like(state)

        for h in range(H):
            vs = slice(h * dv, (h + 1) * dv)
            q, k, v, gr = head(h)
            s = _dot_nt(q, k) * kscale * dec_ref[h]
            st = state[h]
            st_c = st.astype(CDT)
            st_ref[h] = st_c
            o = _dot_nn(s.astype(CDT), v) + _dot_nn(q, st_c) * xi_ref[h][:, 0:1]
            kz = (k.astype(F32) * zeta_ref[h][:, 0:1]).astype(CDT)
            state[h] = st * gc_ref[h][0:1, 0:1] + _dot_tn(kz, v)
            opre_ref[:, vs] = o
            _, _, y = _gn_fwd(o, g_ref[:, vs], b_ref[:, vs])
            gr = gr.astype(F32)
            or_ref[:, vs] = (y * (gr * _sigmoid(gr))).astype(or_ref.dtype)

    v_w = H * dv
    row = pl.BlockSpec((1, v_w), lambda i: (0, 0))
    tile = pl.BlockSpec((BLK, v_w), lambda i: (i, 0))
    return pl.pallas_call(
        body, name="ret_fwd", grid=(N,),
        in_specs=_ret_specs(dk, dv, lambda i: i) + [row, row],
        out_specs=[tile, tile, pl.BlockSpec((None, H, dk, dv), lambda i: (i, 0, 0, 0))],
        out_shape=[_sds((S, v_w), F32), _sds((S, v_w), CDT), _sds((N, H, dk, dv), CDT)],
        scratch_shapes=[pltpu.VMEM((H, dk, dv), F32)],
        compiler_params=_params(1),
    )(*[proj] * (3 * H), *_ret_tables(dk), gn_g, gn_b)


def _ret_bwd(proj, gn_g, gn_b, o_pre, states, d_or, dga, dgb, dk, dv):
    S, in_w = proj.shape
    N = S // BLK
    H = RET_HEADS
    qk_w, v_w = H * dk, H * dv
    kscale = dk ** -0.5
    n_in = 3 * H + 4
    out_w = 2 * qk_w + 2 * v_w
    gate_w = dga.shape[1]
    col0 = 3 * ATT_W
    assert col0 + out_w + 2 * gate_w == in_w
    rev = lambda i: N - 1 - i

    def body(*refs):
        head, (dec_ref, xi_ref, zeta_ref, gc_ref) = _ret_heads(refs, dk)
        (g_ref, b_ref, opre_ref, st_ref, dor_ref, dga_ref, dgb_ref, dproj_ref, dg_ref, db_ref, dstate, stage,
         sem) = refs[n_in:n_in + 13]
        i = pl.program_id(0)
        slot = i % 2
        out_ref = stage.at[slot]

        def out_copy(s, step):
            rows = pl.ds(pl.multiple_of(rev(step) * BLK, BLK), BLK)
            return pltpu.make_async_copy(stage.at[s], dproj_ref.at[rows, pl.ds(col0, in_w - col0)], sem.at[s])

        @pl.when(i >= 2)
        def _():
            out_copy(slot, i - 2).wait()

        @pl.when(i == 0)
        def _():
            dstate[...] = jnp.zeros_like(dstate)
            dg_ref[...] = jnp.zeros_like(dg_ref)
            db_ref[...] = jnp.zeros_like(db_ref)

        out_ref[:, out_w:out_w + gate_w] = dga_ref[...]
        out_ref[:, out_w + gate_w:out_w + 2 * gate_w] = dgb_ref[...]
        for h in range(H):
            vs = slice(h * dv, (h + 1) * dv)
            q, k, v, gr = head(h)
            decay, xi, zeta_s = dec_ref[h], xi_ref[h][:, 0:1], zeta_ref[h][:, 0:1]
            gr = gr.astype(F32)
            sg = _sigmoid(gr)
            gain = g_ref[:, vs]
            yh, rstd, y = _gn_fwd(opre_ref[:, vs], gain, b_ref[:, vs])
            d_or_v = dor_ref[:, vs]
            dy = d_or_v * (gr * sg)
            out_ref[:, 2 * qk_w + v_w + h * dv:2 * qk_w + v_w + (h + 1) * dv] = (
                d_or_v * y * (sg * (1.0 + gr * (1.0 - sg)))).astype(out_ref.dtype)
            dg_ref[:, vs] += jnp.sum(dy * yh, axis=0, keepdims=True)
            db_ref[:, vs] += jnp.sum(dy, axis=0, keepdims=True)
            dyh = dy * gain
            do = rstd * (dyh - jnp.mean(dyh, axis=-1, keepdims=True)
                         - yh * jnp.mean(dyh * yh, axis=-1, keepdims=True))
            do_c = do.astype(CDT)
            dox = (do * xi).astype(CDT)
            a_c = (_dot_nt(q, k) * kscale * decay).astype(CDT)
            g_c = (_dot_nt(do_c, v) * decay).astype(CDT)
            dsn = dstate[h]
            dsn_c = dsn.astype(CDT)
            kz = (k.astype(F32) * zeta_s).astype(CDT)
            dq = _dot_nn(g_c, k) * kscale + _dot_nt(dox, st_ref[h])
            dkk = _dot_tn(g_c, q) * kscale + _dot_nt(v, dsn_c) * zeta_s
            dvv = _dot_tn(a_c, do_c) + _dot_nn(kz, dsn_c)
            dstate[h] = dsn * gc_ref[h][0:1, 0:1] + _dot_tn(q, dox)
            out_ref[:, h * dk:(h + 1) * dk] = dq.astype(out_ref.dtype)
            out_ref[:, qk_w + h * dk:qk_w + (h + 1) * dk] = dkk.astype(out_ref.dtype)
            out_ref[:, 2 * qk_w + h * dv:2 * qk_w + (h + 1) * dv] = dvv.astype(out_ref.dtype)

        cp = out_copy(slot, i)
        cp.start()

        @pl.when(i == N - 1)
        def _():
            cp.wait()
            if N >= 2:
                out_copy(1 - slot, i - 1).wait()

    row = pl.BlockSpec((1, v_w), lambda i: (0, 0))
    tile = pl.BlockSpec((BLK, v_w), lambda i: (rev(i), 0))
    gate = pl.BlockSpec((BLK, gate_w), lambda i: (rev(i), 0))
    return pl.pallas_call(
        body, name="ret_bwd", grid=(N,),
        in_specs=_ret_specs(dk, dv, rev) + [row, row, tile,
                 pl.BlockSpec((None, H, dk, dv), lambda i: (rev(i), 0, 0, 0)), tile, gate, gate],
        out_specs=[pl.BlockSpec(memory_space=pl.ANY), row, row],
        out_shape=[_sds((S, in_w), CDT), _sds((1, v_w), F32), _sds((1, v_w), F32)],
        scratch_shapes=[pltpu.VMEM((H, dk, dv), F32), pltpu.VMEM((2, BLK, in_w - col0), CDT),
                        pltpu.SemaphoreType.DMA((2,))],
        compiler_params=_params(1),
    )(*[proj] * (3 * H), *_ret_tables(dk), gn_g, gn_b, o_pre, states, d_or, dga, dgb)


def _merge_fwd(o_a, o_r, wa, wb, proj, d_model, tm=512, tn=512):
    S, in_w = proj.shape
    off_a, off_b = in_w - 2 * d_model, in_w - d_model
    assert off_a % tn == 0 and off_b % tn == 0

    def body(oa_ref, or_ref, wa_ref, wb_ref, ga_ref, gb_ref, y_ref, pa_ref, pb_ref):
        pa = _dot_nn(oa_ref[...], wa_ref[...])
        pb = _dot_nn(or_ref[...], wb_ref[...])
        y = _sigmoid(ga_ref[...].astype(F32)) * pa + _sigmoid(gb_ref[...].astype(F32)) * pb
        y_ref[...] = y.astype(y_ref.dtype)
        pa_ref[...] = pa.astype(pa_ref.dtype)
        pb_ref[...] = pb.astype(pb_ref.dtype)

    ka, kb = o_a.shape[1], o_r.shape[1]
    out = pl.BlockSpec((tm, tn), lambda i, j: (i, j))
    return pl.pallas_call(
        body, name="merge_fwd", grid=(S // tm, d_model // tn),
        in_specs=[pl.BlockSpec((tm, ka), lambda i, j: (i, 0)), pl.BlockSpec((tm, kb), lambda i, j: (i, 0)),
                  pl.BlockSpec((ka, tn), lambda i, j: (0, j)), pl.BlockSpec((kb, tn), lambda i, j: (0, j)),
                  pl.BlockSpec((tm, tn), lambda i, j: (i, off_a // tn + j)),
                  pl.BlockSpec((tm, tn), lambda i, j: (i, off_b // tn + j))],
        out_specs=[out, out, out], out_shape=[_sds((S, d_model), CDT)] * 3,
        compiler_params=_params(2))(o_a, o_r, wa, wb, proj, proj)


def _local_step(x, target, w_in_mine, chip, others, far_w_in, small, late_weights, on_grads, deps0=()):
    S, D = x.shape
    ns_in = w_in_mine.shape[1]
    in_w = N_CHIPS * ns_in
    d_ff = 4 * D
    ret_v_w = 2 * D
    dv = ret_v_w // RET_HEADS
    dk = (in_w - 3 * ATT_W - 2 * ret_v_w - 2 * D) // (2 * RET_HEADS)
    gqk = jnp.concatenate([small["q_norm_g"].reshape(1, ATT_W), small["k_norm_g"].reshape(1, ATT_W)], axis=1)
    g1, g2 = small["norm1_g"], small["norm2_g"]
    gn_g, gn_b = small["ret_gn_g"], small["ret_gn_b"]

    xn = _rms_fwd("rms1_fwd", x, g1)
    proj_sds = _sds((S, in_w), CDT)
    (proj_mine,) = _matmul(
        "in_proj_mine", "nn", xn, w_in_mine, tm=512, tn=ns_in, tk=D, prefetch=[chip],
        outs=[(proj_sds, pl.BlockSpec((512, ns_in), lambda i, j, k, c: (i, c[0])))], epilogue=_ep_store, deps=deps0)
    w_in = far_w_in(proj_mine)
    (proj,) = _matmul(
        "in_proj_far", "nn", xn, w_in, tm=512, tn=ns_in, tk=D, prefetch=[others], n_cols=(N_CHIPS - 1) * ns_in,
        b_spec=pl.BlockSpec((None, D, ns_in), lambda i, j, k, o: (o[j], 0, 0)),
        outs=[(proj_sds, pl.BlockSpec((512, ns_in), lambda i, j, k, o: (i, o[j])))], epilogue=_ep_store,
        deps=[proj_mine], alias_dep_to_out=(0, 0), j_outer=True)
    qkv = _qknorm_fwd(proj, gqk)
    att = [_att_fwd(g, S, qkv[g]) for g in range(3)]
    os_, ls_ = [a[0] for a in att], [a[1] for a in att]
    o_a = _mix_fwd(S, os_, ls_)
    o_pre, o_r, states = _ret_fwd(proj, gn_g, gn_b, dk, dv)
    w = late_weights(o_r)
    y, pa, pb = _merge_fwd(o_a, o_r, w["w_proj_a"], w["w_proj_b"], proj, D)
    (x1,) = _matmul("out_proj", "nn", y, w["w_out"], tm=512, tn=D, tk=D,
                    extras=[(x, _mn(512, D))], outs=[(_sds((S, D), F32), _mn(512, D))], epilogue=_ep_resid)
    xn2 = _rms_fwd("rms2_fwd", x1, g2)
    hid, act = _matmul("mlp_up", "nn", xn2, w["w_up"], tm=512, tn=2048, tk=D, j_outer=True,
                       outs=[(_sds((S, d_ff), CDT), _mn(512, 2048))] * 2, epilogue=_ep_up)
    dx2, dx2c, loss_row = _matmul(
        "mlp_down_loss", "nn", act, w["w_down"], tm=512, tn=D, tk=d_ff,
        extras=[(x1, _mn(512, D)), (target, _mn(512, D))],
        outs=[(_sds((S, D), F32), _mn(512, D)), (_sds((S, D), CDT), _mn(512, D)), (_sds((1, D), F32), _row(D))],
        epilogue=functools.partial(_ep_down_loss, inv_d=1.0 / D))
    loss = 0.5 * jnp.sum(loss_row) / D

    (dh,) = _matmul("d_hidden", "nt", dx2c, w["w_down"], tm=512, tn=2048, tk=D, j_outer=True,
                    extras=[(hid, _mn(512, 2048))], outs=[(_sds((S, d_ff), CDT), _mn(512, 2048))], epilogue=_ep_dh)
    (gw_down,) = _matmul("dw_down", "tn", act, dx2c, tm=1024, tn=D, tk=1024,
                         outs=[(_sds((d_ff, D), F32), _mn(1024, D))], epilogue=_ep_store)
    (gw_up,) = _matmul("dw_up", "tn", xn2, dh, tm=D, tn=1024, tk=1024,
                       outs=[(_sds((D, d_ff), F32), _mn(D, 1024))], epilogue=_ep_store)
    tok = on_grads({"w_down": gw_down, "w_up": gw_up})
    dx1, dx1c, dg2 = _matmul(
        "d_x1", "nt", dh, w["w_up"], tm=512, tn=D, tk=d_ff,
        extras=[(x1, _mn(512, D)), (g2, _row(D)), (dx2, _mn(512, D))],
        outs=[(_sds((S, D), F32), _mn(512, D)), (_sds((S, D), CDT), _mn(512, D)), (_sds((1, D), F32), _row(D))],
        epilogue=_ep_rms_bwd, deps=[tok])

    gt = 512
    assert (in_w - 2 * D) % gt == 0
    off_a, off_b = (in_w - 2 * D) // gt, (in_w - D) // gt
    dpa, dpb, dga, dgb = _matmul(
        "d_gates", "nt", dx1c, w["w_out"], tm=512, tn=gt, tk=D,
        extras=[(proj, _mn(512, gt, off_a)), (proj, _mn(512, gt, off_b)), (pa, _mn(512, gt)), (pb, _mn(512, gt))],
        outs=[(_sds((S, D), CDT), _mn(512, gt))] * 4, epilogue=_ep_gates)
    (gw_out,) = _matmul("dw_out", "tn", y, dx1c, tm=D, tn=D, tk=1024,
                        outs=[(_sds((D, D), F32), _mn(D, D))], epilogue=_ep_store)
    (gw_pa,) = _matmul("dw_proj_a", "tn", o_a, dpa, tm=GW, tn=D, tk=1024,
                       outs=[(_sds((GW, D), F32), _mn(GW, D))], epilogue=_ep_store)
    (gw_pb,) = _matmul("dw_proj_b", "tn", o_r, dpb, tm=1024, tn=D, tk=1024,
                       outs=[(_sds((ret_v_w, D), F32), _mn(1024, D))], epilogue=_ep_store)
    (do_a,) = _matmul("d_o_a", "nt", dpa, w["w_proj_a"], tm=1024, tn=GW, tk=D,
                      outs=[(_sds((S, GW), F32), _mn(1024, GW))], epilogue=_ep_store)
    tok = on_grads({"w_out": gw_out, "w_proj_a": gw_pa, "w_proj_b": gw_pb})
    (d_or,) = _matmul("d_o_r", "nt", dpb, w["w_proj_b"], tm=512, tn=ret_v_w, tk=D,
                      outs=[(_sds((S, ret_v_w), F32), _mn(512, ret_v_w))], epilogue=_ep_store, deps=[tok])

    dproj, dgn_g, dgn_b = _ret_bwd(proj, gn_g, gn_b, o_pre, states, d_or, dga, dgb, dk, dv)
    do_gs, c_gs = _mix_bwd(S, os_, ls_, do_a)
    datt_parts = [_att_bwd(g, S, qkv[g], ls_[g], do_gs[g], c_gs[g]) for g in range(3)]
    dproj, dgqk = _qknorm_bwd(proj, gqk, [p[0] for p in datt_parts], [p[1] for p in datt_parts],
                              [p[2] for p in datt_parts], dproj)

    (gw_in,) = _matmul(
        "dw_in", "tn", xn, dproj, tm=512, tn=ns_in, tk=1024,
        outs=[(_sds((N_CHIPS, D, ns_in), F32), pl.BlockSpec((None, 512, ns_in), lambda i, j, k: (j, i, 0)))],
        epilogue=_ep_store)
    tok = on_grads({"w_in": gw_in})
    grad_x, _, dg1 = _matmul(
        "d_x", "nt", dproj, w_in, tm=512, tn=D, tk=ns_in, n_cols=D,
        b_spec=pl.BlockSpec((None, D, ns_in), lambda i, j, k: (k, 0, 0)),
        extras=[(x, _mn(512, D)), (g1, _row(D)), (dx1, _mn(512, D))],
        outs=[(_sds((S, D), F32), _mn(512, D)), (_sds((S, D), CDT), _mn(512, D)), (_sds((1, D), F32), _row(D))],
        epilogue=_ep_rms_bwd, deps=[tok])

    smallg = {"norm1_g": dg1, "q_norm_g": dgqk[:, :ATT_W], "k_norm_g": dgqk[:, ATT_W:],
              "ret_gn_g": dgn_g, "ret_gn_b": dgn_b, "norm2_g": dg2}
    return loss, grad_x, smallg


N_CHIPS = 4
N_DEV = 8


def _place():
    x, y, c = lax.axis_index("x"), lax.axis_index("y"), lax.axis_index("c")
    return x, y, c


def _other_chips(x, y):
    out = []
    for fx, fy in ((1, 0), (0, 1), (1, 1)):
        px = 1 - x if fx else x
        py = 1 - y if fy else y
        out.append(((px, py), 2 * px + py))
    return out


SEM_SPEC = pl.BlockSpec(memory_space=pltpu.SEMAPHORE)
ANY_SPEC = pl.BlockSpec(memory_space=pl.ANY)
EFFECT = pltpu.SideEffectType.DATAFLOW_SIDE_EFFECTING


def _ici_copies(kind, srcs, lands, send, recv):
    x, y, c = _place()
    me = 2 * x + y
    out = []
    for w, (s, l) in enumerate(zip(srcs, lands)):
        for j, ((px, py), pidx) in enumerate(_other_chips(x, y)):
            if kind == "gather":
                half = s.shape[0] // 2
                rows = pl.ds(c * half, half)
                src, dst_there, dst_here = s.at[rows, :], l.at[me, rows, :], l.at[pidx, rows, :]
            else:
                src, dst_there, dst_here = s.at[pidx], l.at[me], l.at[pidx]
            out.append((src, dst_there, dst_here, send.at[3 * w + j], recv.at[3 * w + j], (px, py, c)))
    return out


def _exchange_start(name, kind, srcs, land_shapes):
    n = len(srcs)

    def body(*refs):
        src_refs, land_refs = refs[:n], refs[n:2 * n]
        send, recv = refs[2 * n], refs[2 * n + 1]
        token = refs[-1]
        for src, dst, _, ss, rs, dev in _ici_copies(kind, src_refs, land_refs, send, recv):
            pltpu.make_async_remote_copy(src_ref=src, dst_ref=dst, send_sem=ss, recv_sem=rs, device_id=dev,
                                         device_id_type=MESH).start()
        token[...] = jnp.zeros_like(token)

    thru = [pltpu.HBM(s.shape, s.dtype) for s in srcs] + [pltpu.HBM(shape, dtype) for shape, dtype in land_shapes]
    res = pl.pallas_call(
        body, name=name,
        out_shape=(pltpu.SemaphoreType.DMA((3 * n,)), pltpu.SemaphoreType.DMA((3 * n,)), *thru, _sds((8, LANES), F32)),
        in_specs=[HBM_SPEC] * (2 * n), out_specs=(SEM_SPEC, SEM_SPEC, *[HBM_SPEC] * (2 * n), VMEM_SPEC),
        input_output_aliases={i: 2 + i for i in range(2 * n)},
        compiler_params=pltpu.CompilerParams(has_side_effects=EFFECT),
    )(*[pltpu.with_memory_space_constraint(s, pltpu.HBM) for s in srcs],
      *[pltpu.with_memory_space_constraint(lax.empty(shape, dtype), pltpu.HBM) for shape, dtype in land_shapes])
    return res[0], res[1], list(res[2:2 + n]), list(res[2 + n:2 + 2 * n]), res[-1]


def _exchange_wait(name, kind, send, recv, srcs, lands, after):
    n = len(srcs)

    def body(*refs):
        src_refs, land_refs = refs[:n], refs[n:2 * n]
        send_ref, recv_ref = refs[2 * n], refs[2 * n + 1]
        for src, _, dst, ss, rs, dev in _ici_copies(kind, src_refs, land_refs, send_ref, recv_ref):
            cp = pltpu.make_async_remote_copy(src_ref=src, dst_ref=dst, send_sem=ss, recv_sem=rs, device_id=dev,
                                              device_id_type=MESH)
            cp.wait_send()
            cp.wait_recv()

    thru = [pltpu.HBM(t.shape, t.dtype) for t in list(srcs) + list(lands)]
    res = pl.pallas_call(
        body, name=name, out_shape=thru,
        in_specs=[HBM_SPEC] * (2 * n) + [SEM_SPEC, SEM_SPEC, ANY_SPEC], out_specs=[HBM_SPEC] * (2 * n),
        input_output_aliases={i: i for i in range(2 * n)},
        compiler_params=pltpu.CompilerParams(has_side_effects=EFFECT),
    )(*srcs, *lands, send, recv, after)
    return list(res[:n]), list(res[n:])


PAIR_TILE_ELEMS = 1 << 19


def _pair_fill(name, gathered, mine, core, others, chip):
    k, r, C = gathered.shape
    half = r // 2
    tr = _row_tile(half, C, PAIR_TILE_ELEMS, mult=16)
    nt = half // tr
    n_far = N_CHIPS - 1

    def body(c_ref, o_ref, chip_ref, in_ref, mine_ref, out_ref, slot, send, recv):
        j = pl.program_id(0)
        b = (j * nt + pl.program_id(1)) % 2
        x, y, c = _place()
        cp = pltpu.make_async_remote_copy(src_ref=in_ref, dst_ref=slot.at[b], send_sem=send.at[b],
                                          recv_sem=recv.at[b], device_id=(x, y, 1 - c), device_id_type=MESH)

        @pl.when(j < n_far)
        def _():
            cp.start()
            cp.wait_recv()
            out_ref[...] = slot[b]
            cp.wait_send()

        @pl.when(j >= n_far)
        def _():
            out_ref[...] = mine_ref[...]

    def far(j):
        return jnp.minimum(j, n_far - 1)

    grid_spec = pltpu.PrefetchScalarGridSpec(
        num_scalar_prefetch=3, grid=(n_far + 2, nt),
        in_specs=[pl.BlockSpec((tr, C), lambda j, i, c, o, m: (
                      (2 * o[far(j)] + c[0]) * nt + jnp.where(j < n_far, i, nt - 1), 0)),
                  pl.BlockSpec((tr, C), lambda j, i, c, o, m: (jnp.where(j < n_far, 0, (j - n_far) * nt + i), 0))],
        out_specs=pl.BlockSpec((tr, C), lambda j, i, c, o, m: (
            jnp.where(j < n_far, 2 * o[far(j)] + 1 - c[0], 2 * m[0] + j - n_far) * nt + i, 0)),
        scratch_shapes=[pltpu.VMEM((2, tr, C), gathered.dtype), pltpu.SemaphoreType.DMA((2,)),
                        pltpu.SemaphoreType.DMA((2,))])
    out = pl.pallas_call(body, name=name, grid_spec=grid_spec, out_shape=_sds((k * r, C), gathered.dtype),
                         input_output_aliases={3: 0}, compiler_params=_params(2))(
                             core, others, chip, gathered.reshape(k * r, C), mine)
    return out.reshape(k, r, C)


def _pair_reduce(name, g, core):
    k, R, C = g.shape
    half = R // 2
    tr = _row_tile(half, C, PAIR_TILE_ELEMS, mult=16)
    nt = half // tr

    def body(c_ref, mine_ref, give_ref, out_ref, wire_ref, stage, slot, send, recv):
        b = (pl.program_id(0) * nt + pl.program_id(1)) % 2
        x, y, c = _place()
        stage[b] = give_ref[...].astype(stage.dtype)
        cp = pltpu.make_async_remote_copy(src_ref=stage.at[b], dst_ref=slot.at[b], send_sem=send.at[b],
                                          recv_sem=recv.at[b], device_id=(x, y, 1 - c), device_id_type=MESH)
        cp.start()
        cp.wait_recv()
        tot = mine_ref[...] + slot[b].astype(F32)
        out_ref[...] = tot
        wire_ref[...] = tot.astype(wire_ref.dtype)
        cp.wait_send()

    blk = (tr, C)
    out_spec = pl.BlockSpec(blk, lambda s, i, c: (s * nt + i, 0))
    grid_spec = pltpu.PrefetchScalarGridSpec(
        num_scalar_prefetch=1, grid=(k, nt),
        in_specs=[pl.BlockSpec(blk, lambda s, i, c: ((2 * s + c[0]) * nt + i, 0)),
                  pl.BlockSpec(blk, lambda s, i, c: ((2 * s + 1 - c[0]) * nt + i, 0))],
        out_specs=[out_spec, out_spec],
        scratch_shapes=[pltpu.VMEM((2, tr, C), CDT), pltpu.VMEM((2, tr, C), CDT), pltpu.SemaphoreType.DMA((2,)),
                        pltpu.SemaphoreType.DMA((2,))])
    g2 = g.reshape(k * R, C)
    out, wire = pl.pallas_call(body, name=name, grid_spec=grid_spec,
                               out_shape=[_sds((k * half, C), F32), _sds((k * half, C), CDT)],
                               compiler_params=_params(2))(core, g2, g2)
    return out, wire.reshape(k, half, C)


def _pair_share(name, f, core):
    half, C = f.shape
    tr = _row_tile(half, C, PAIR_TILE_ELEMS)
    nt = half // tr

    def body(c_ref, f_ref, out_ref, slot, send, recv):
        p = pl.program_id(1)
        b = pl.program_id(0) % 2
        x, y, c = _place()
        cp = pltpu.make_async_remote_copy(src_ref=f_ref, dst_ref=slot.at[b], send_sem=send.at[b],
                                          recv_sem=recv.at[b], device_id=(x, y, 1 - c), device_id_type=MESH)

        @pl.when(p == 0)
        def _():
            cp.start()
            out_ref[...] = f_ref[...]

        @pl.when(p == 1)
        def _():
            cp.wait_recv()
            out_ref[...] = slot[b]
            cp.wait_send()

    grid_spec = pltpu.PrefetchScalarGridSpec(
        num_scalar_prefetch=1, grid=(nt, 2),
        in_specs=[pl.BlockSpec((tr, C), lambda i, p, c: (i, 0))],
        out_specs=pl.BlockSpec((tr, C), lambda i, p, c: (jnp.where(p == 0, c[0], 1 - c[0]) * nt + i, 0)),
        scratch_shapes=[pltpu.VMEM((2, tr, C), F32), pltpu.SemaphoreType.DMA((2,)), pltpu.SemaphoreType.DMA((2,))])
    return pl.pallas_call(body, name=name, grid_spec=grid_spec, out_shape=_sds((2 * half, C), F32),
                          compiler_params=_params(2))(core, f)


def _all_reduce_small(v):
    r, cdim = v.shape

    def body(v_ref, o_ref, buf, send, recv):
        x, y, c = _place()
        me = 4 * x + 2 * y + c
        buf[me] = v_ref[...]
        sends = []
        for m in range(1, N_DEV):
            px = 1 - x if m & 4 else x
            py = 1 - y if m & 2 else y
            pc = 1 - c if m & 1 else c
            cp = pltpu.make_async_remote_copy(src_ref=v_ref, dst_ref=buf.at[me], send_sem=send.at[m - 1],
                                              recv_sem=recv.at[m - 1], device_id=(px, py, pc), device_id_type=MESH)
            cp.start()
            sends.append((cp, 4 * px + 2 * py + pc))
        for m, (cp, pidx) in enumerate(sends):
            pltpu.make_async_remote_copy(src_ref=v_ref, dst_ref=buf.at[pidx], send_sem=send.at[m], recv_sem=recv.at[m],
                                         device_id=(x, y, c), device_id_type=MESH).wait_recv()
        for cp, _ in sends:
            cp.wait_send()
        tot = buf[0]
        for k in range(1, N_DEV):
            tot = tot + buf[k]
        o_ref[...] = tot

    return pl.pallas_call(
        body, name="all_reduce_small", in_specs=[VMEM_SPEC], out_specs=VMEM_SPEC,
        out_shape=_sds((r, cdim), F32),
        scratch_shapes=[pltpu.VMEM((N_DEV, r, cdim), F32), pltpu.SemaphoreType.DMA((N_DEV - 1,)),
                        pltpu.SemaphoreType.DMA((N_DEV - 1,))],
    )(v)


def _row_tile(rows, cols, budget_elems=1 << 18, mult=8):
    if rows % mult:
        return rows
    t = max(mult, (budget_elems // cols) // mult * mult)
    while rows % t:
        t -= mult
    return t


def _sum4(name, own, by_chip, chip, others):
    k, R, C = by_chip.shape
    tr = _row_tile(R, C, mult=16)
    nt = R // tr

    def body(chip_ref, others_ref, own_ref, a_ref, b_ref, c_ref, o_ref):
        o_ref[...] = ((own_ref[...] + a_ref[...].astype(F32)) + b_ref[...].astype(F32)) + c_ref[...].astype(F32)

    def from_chip(j):
        return pl.BlockSpec((tr, C), lambda i, chip, oth: (oth[j] * nt + i, 0))

    grid_spec = pltpu.PrefetchScalarGridSpec(
        num_scalar_prefetch=2, grid=(nt,),
        in_specs=[pl.BlockSpec((tr, C), lambda i, chip, oth: (chip[0] * nt + i, 0)),
                  from_chip(0), from_chip(1), from_chip(2)],
        out_specs=pl.BlockSpec((tr, C), lambda i, chip, oth: (i, 0)))
    by2 = by_chip.reshape(k * R, C)
    return pl.pallas_call(body, name=name, grid_spec=grid_spec, out_shape=_sds((R, C), F32),
                          compiler_params=_params(1))(chip, others, own, by2, by2, by2)


def _adamw(name, w, g, m, v):
    R, C = w.shape
    tr = _row_tile(R, C, 1 << 17)

    def body(w_ref, g_ref, m_ref, v_ref, d_ref, nm_ref, nv_ref):
        gv = g_ref[...]
        nm = ADAM_B1 * m_ref[...] + (1.0 - ADAM_B1) * gv
        nv = ADAM_B2 * v_ref[...] + (1.0 - ADAM_B2) * (gv * gv)
        m_hat = nm / (1.0 - ADAM_B1 ** ADAM_STEP)
        v_hat = nv / (1.0 - ADAM_B2 ** ADAM_STEP)
        d_ref[...] = -ADAM_LR * (m_hat / (jnp.sqrt(v_hat) + ADAM_EPS) + ADAM_WD * w_ref[...])
        nm_ref[...] = nm
        nv_ref[...] = nv

    spec = pl.BlockSpec((tr, C), lambda i: (i, 0))
    return pl.pallas_call(body, name=name, grid=(R // tr,), in_specs=[spec] * 4, out_specs=[spec] * 3,
                          out_shape=[_sds((R, C), F32)] * 3, compiler_params=_params(1))(w, g, m, v)


BIG = ("w_in", "w_proj_a", "w_proj_b", "w_out", "w_up", "w_down")
COL_SHARDED = ("w_in", "w_proj_a", "w_up")
SMALL = ("norm1_g", "q_norm_g", "k_norm_g", "ret_gn_g", "ret_gn_b", "norm2_g")
ALL_W = ("norm1_g", "w_in", "q_norm_g", "k_norm_g", "ret_gn_g", "ret_gn_b", "w_proj_a", "w_proj_b", "w_out",
         "norm2_g", "w_up", "w_down")
LANES = 128


def _to_full(name, gathered):
    k, r, c = gathered.shape
    if name in COL_SHARDED:
        return gathered.transpose(1, 0, 2).reshape(r, k * c)
    return gathered.reshape(k * r, c)


def _to_shard_major(name, full):
    if name in COL_SHARDED:
        r, c4 = full.shape
        return full.reshape(r, N_CHIPS, c4 // N_CHIPS).transpose(1, 0, 2)
    r4, c = full.shape
    return full.reshape(N_CHIPS, r4 // N_CHIPS, c)


def kernel(x, norm1_g, w_in, q_norm_g, k_norm_g, ret_gn_g, ret_gn_b, w_proj_a, w_proj_b, w_out, norm2_g, w_up, w_down, loss_target, m_norm1_g, m_w_in, m_q_norm_g, m_k_norm_g, m_ret_gn_g, m_ret_gn_b, m_w_proj_a, m_w_proj_b, m_w_out, m_norm2_g, m_w_up, m_w_down, v_norm1_g, v_w_in, v_q_norm_g, v_k_norm_g, v_ret_gn_g, v_ret_gn_b, v_w_proj_a, v_w_proj_b, v_w_out, v_norm2_g, v_w_up, v_w_down):
    weights = dict(norm1_g=norm1_g, w_in=w_in, q_norm_g=q_norm_g, k_norm_g=k_norm_g, ret_gn_g=ret_gn_g,
                   ret_gn_b=ret_gn_b, w_proj_a=w_proj_a, w_proj_b=w_proj_b, w_out=w_out, norm2_g=norm2_g,
                   w_up=w_up, w_down=w_down)
    moments_m = dict(norm1_g=m_norm1_g, w_in=m_w_in, q_norm_g=m_q_norm_g, k_norm_g=m_k_norm_g, ret_gn_g=m_ret_gn_g,
                     ret_gn_b=m_ret_gn_b, w_proj_a=m_w_proj_a, w_proj_b=m_w_proj_b, w_out=m_w_out,
                     norm2_g=m_norm2_g, w_up=m_w_up, w_down=m_w_down)
    moments_v = dict(norm1_g=v_norm1_g, w_in=v_w_in, q_norm_g=v_q_norm_g, k_norm_g=v_k_norm_g, ret_gn_g=v_ret_gn_g,
                     ret_gn_b=v_ret_gn_b, w_proj_a=v_w_proj_a, w_proj_b=v_w_proj_b, w_out=v_w_out,
                     norm2_g=v_norm2_g, w_up=v_w_up, w_down=v_w_down)

    mx, my = lax.axis_index("x"), lax.axis_index("y")
    core = lax.axis_index("c").astype(jnp.int32).reshape(1)
    chip = (2 * mx + my).astype(jnp.int32).reshape(1)
    others = jnp.stack([2 * (1 - mx) + my, 2 * mx + 1 - my, 2 * (1 - mx) + 1 - my]).astype(jnp.int32)
    shards = {n: weights[n][0].astype(CDT) for n in BIG}
    def start_gather(name, names):
        return _exchange_start(name, "gather", [shards[n] for n in names],
                               [((N_CHIPS,) + shards[n].shape, CDT) for n in names])

    i_send, i_recv, i_srcs, i_lands, i_token = start_gather("gather_w_in_start", ["w_in"])
    late = [n for n in BIG if n != "w_in"]
    l_send, l_recv, l_srcs, l_lands, l_token = start_gather("gather_late_start", late)

    def far_w_in(after):
        srcs, lands = _exchange_wait("gather_w_in_wait", "gather", i_send, i_recv, i_srcs, i_lands, after)
        return _pair_fill("pair_fill_w_in", lands[0], srcs[0], core, others, chip)

    def late_weights(after):
        srcs, lands = _exchange_wait("gather_late_wait", "gather", l_send, l_recv, l_srcs, l_lands, after)
        out = {}
        for n, mine, land in zip(late, srcs, lands):
            out[n] = _to_full(n, _pair_fill("pair_fill_%s" % n, land, mine, core, others, chip))
        return out

    pending = []

    def on_grads(group):
        names = list(group)
        red = [_pair_reduce("pair_reduce_%s" % n, g if g.ndim == 3 else _to_shard_major(n, g), core)
               for n, g in group.items()]
        wires = [wire for _, wire in red]
        send, recv, srcs, lands, token = _exchange_start(
            "scatter_start_%s" % names[0], "scatter", wires, [(wire.shape, wire.dtype) for wire in wires])
        pending.append((names, [own for own, _ in red], send, recv, srcs, lands))
        return token

    small = {n: weights[n].reshape(1, -1) for n in SMALL}

    loss, grad_x, small_g = _local_step(x[0], loss_target[0], i_srcs[0], chip, others, far_w_in, small,
                                        late_weights, on_grads, deps0=[i_token, l_token])
    loss = lax.psum(loss, ("x", "y", "c"))

    grads = {}
    for names, owns, send, recv, srcs, lands in pending:
        _, got = _exchange_wait("scatter_wait_%s" % names[0], "scatter", send, recv, srcs, lands, grad_x)
        for n, own, by_chip in zip(names, owns, got):
            half = _sum4("chip_sum_%s" % n, own, by_chip, chip, others)
            grads[n] = _pair_share("pair_share_%s" % n, half, core)

    packed = jnp.concatenate([small_g[n].reshape(1, -1) for n in SMALL], axis=1)
    sizes = [small_g[n].size for n in SMALL]
    red = _all_reduce_small(packed.reshape(-1, LANES)).reshape(1, -1)
    off = 0
    for n, sz in zip(SMALL, sizes):
        grads[n] = red[:, off:off + sz]
        off += sz

    out_g, out_d, out_m, out_v = {}, {}, {}, {}
    for n in ALL_W:
        shape = weights[n].shape
        two_d = (shape[-2], shape[-1]) if n in BIG else (1, weights[n].size)
        g2 = grads[n].reshape(two_d)
        d, nm, nv = _adamw("adamw_%s" % n, weights[n].reshape(two_d), g2, moments_m[n].reshape(two_d),
                           moments_v[n].reshape(two_d))
        out_g[n], out_d[n], out_m[n], out_v[n] = (t.reshape(shape) for t in (g2, d, nm, nv))

    return (loss, grad_x[None], *[out_g[n] for n in ALL_W], *[out_d[n] for n in ALL_W],
            *[out_m[n] for n in ALL_W], *[out_v[n] for n in ALL_W])
```

```python
import functools
import math

import jax
import jax.numpy as jnp
from jax import lax
from jax.experimental import pallas as pl
from jax.experimental.pallas import tpu as pltpu

CDT = jnp.bfloat16
F32 = jnp.float32
EPS = 1e-6

ATT_GROUPS = ((128, 1), (512, 4), (2048, 16))
ATT_HPG = 4
ATT_HEADS = 12
HD = 128
BLK = 128
ATT_W = ATT_HEADS * HD
GW = ATT_HPG * HD
RET_HEADS = 4

ADAM_LR = 0.001
ADAM_B1 = 0.9
ADAM_B2 = 0.999
ADAM_EPS = 1e-08
ADAM_WD = 0.01
ADAM_STEP = 10

VMEM_LIMIT_BYTES = 56 * 1024 * 1024
MXU_DIM = 256
MESH = pl.DeviceIdType.MESH
HBM_SPEC = pl.BlockSpec(memory_space=pltpu.HBM)
VMEM_SPEC = pl.BlockSpec(memory_space=pltpu.VMEM)


def _params(n_axes):
    return pltpu.CompilerParams(dimension_semantics=("arbitrary",) * n_axes,
                                vmem_limit_bytes=VMEM_LIMIT_BYTES)


def _dot_nn(a, b):
    return jnp.dot(a, b, preferred_element_type=F32)


def _dot_nt(a, b):
    return lax.dot_general(a, b, (((1,), (1,)), ((), ())), preferred_element_type=F32)


def _dot_tn(a, b):
    return lax.dot_general(a, b, (((0,), (0,)), ((), ())), preferred_element_type=F32)


def _sigmoid(v):
    return 1.0 / (1.0 + jnp.exp(-v))


def _matmul(name, mode, a, b, *, tm, tn, tk, extras=(), outs, epilogue, deps=(), b_spec=None, n_cols=None,
            prefetch=(), alias_dep_to_out=None, j_outer=False):
    deps = [d for d in deps if d is not None]
    if mode == "tn":
        K, M = a.shape
    else:
        M, K = a.shape
    if b_spec is None:
        (N, K2) = b.shape if mode == "nt" else b.shape[::-1]
        assert K == K2, (name, a.shape, b.shape)
        if mode == "nt":
            b_spec = pl.BlockSpec((tn, tk), lambda i, j, k, *p: (j, k))
        else:
            b_spec = pl.BlockSpec((tk, tn), lambda i, j, k, *p: (k, j))
    else:
        N = n_cols
    assert M % tm == 0 and N % tn == 0 and K % tk == 0, (name, a.shape, b.shape)
    ni, nj, nk = M // tm, N // tn, K // tk
    if mode == "tn":
        a_spec = pl.BlockSpec((tk, tm), lambda i, j, k, *p: (k, i))
    else:
        a_spec = pl.BlockSpec((tm, tk), lambda i, j, k, *p: (i, k))
    dot = {"nn": _dot_nn, "nt": _dot_nt, "tn": _dot_tn}[mode]
    n_ex, n_out, n_dep, n_pre = len(extras), len(outs), len(deps), len(prefetch)
    grid = (ni, nj, nk)
    if j_outer:
        grid = (nj, ni, nk)

        def swapped(spec):
            return pl.BlockSpec(spec.block_shape, lambda j, i, k, *p: spec.index_map(i, j, k, *p))

        a_spec, b_spec = swapped(a_spec), swapped(b_spec)
        extras = [(e, swapped(s)) for e, s in extras]
        outs = [(o, swapped(s)) for o, s in outs]

    def body(*refs):
        refs = refs[n_pre:]
        a_ref, b_ref = refs[0], refs[1]
        ex = refs[2:2 + n_ex]
        out = refs[2 + n_ex + n_dep:2 + n_ex + n_dep + n_out]
        acc = refs[-1] if nk > 1 else None
        i = pl.program_id(1 if j_outer else 0)
        k = pl.program_id(2)
        if nk == 1:
            epilogue(dot(a_ref[...].astype(CDT), b_ref[...].astype(CDT)), ex, out, i)
            return

        @pl.when(k == 0)
        def _():
            acc[...] = jnp.zeros_like(acc)

        acc[...] += dot(a_ref[...].astype(CDT), b_ref[...].astype(CDT))

        @pl.when(k == nk - 1)
        def _():
            epilogue(acc[...], ex, out, i)

    grid_spec = pltpu.PrefetchScalarGridSpec(
        num_scalar_prefetch=n_pre, grid=grid,
        in_specs=[a_spec, b_spec] + [s for _, s in extras] + [pl.BlockSpec(memory_space=pl.ANY)] * n_dep,
        out_specs=[s for _, s in outs],
        scratch_shapes=[pltpu.VMEM((tm, tn), F32)] if nk > 1 else [])
    aliases = {}
    if alias_dep_to_out is not None:
        aliases = {n_pre + 2 + n_ex + alias_dep_to_out[0]: alias_dep_to_out[1]}
    res = pl.pallas_call(
        body, name=name, grid_spec=grid_spec, out_shape=[o for o, _ in outs], input_output_aliases=aliases,
        compiler_params=_params(3),
    )(*prefetch, a, b, *[e for e, _ in extras], *deps)
    return res


def _mn(tm, tn, col_off=0):
    return pl.BlockSpec((tm, tn), lambda i, j, k, *p: (i, j + col_off))


def _row(tn):
    return pl.BlockSpec((1, tn), lambda i, j, k, *p: (0, j))


def _ep_store(acc, ex, out, i):
    out[0][...] = acc.astype(out[0].dtype)


def _ep_resid(acc, ex, out, i):
    out[0][...] = ex[0][...] + acc


def _ep_up(acc, ex, out, i):
    out[0][...] = acc.astype(out[0].dtype)
    r = jnp.maximum(acc, 0.0)
    out[1][...] = (r * r).astype(out[1].dtype)


def _ep_down_loss(acc, ex, out, i, inv_d):
    diff = (ex[0][...] + acc) - ex[1][...]
    dx2 = diff * inv_d
    out[0][...] = dx2
    out[1][...] = dx2.astype(out[1].dtype)

    @pl.when(i == 0)
    def _():
        out[2][...] = jnp.zeros_like(out[2])

    out[2][...] += jnp.sum(diff * diff, axis=0, keepdims=True)


def _ep_dh(acc, ex, out, i):
    h = ex[0][...].astype(F32)
    out[0][...] = (acc * (2.0 * jnp.maximum(h, 0.0))).astype(out[0].dtype)


def _ep_rms_bwd(acc, ex, out, i):
    x = ex[0][...]
    g = ex[1][...]
    rstd = lax.rsqrt(jnp.mean(x * x, axis=-1, keepdims=True) + EPS)
    xh = x * rstd
    dxh = acc * g
    dx = ex[2][...] + rstd * (dxh - xh * jnp.mean(dxh * xh, axis=-1, keepdims=True))
    out[0][...] = dx
    out[1][...] = dx.astype(out[1].dtype)

    @pl.when(i == 0)
    def _():
        out[2][...] = jnp.zeros_like(out[2])

    out[2][...] += jnp.sum(acc * xh, axis=0, keepdims=True)


def _ep_gates(acc, ex, out, i):
    sa = _sigmoid(ex[0][...].astype(F32))
    sb = _sigmoid(ex[1][...].astype(F32))
    dpa = acc * sa
    dpb = acc * sb
    out[0][...] = dpa.astype(out[0].dtype)
    out[1][...] = dpb.astype(out[1].dtype)
    out[2][...] = (dpa * ex[2][...].astype(F32) * (1.0 - sa)).astype(out[2].dtype)
    out[3][...] = (dpb * ex[3][...].astype(F32) * (1.0 - sb)).astype(out[3].dtype)


def _sds(shape, dtype):
    return jax.ShapeDtypeStruct(shape, dtype)


def _rms_fwd(name, x, g, tm=512):
    S, D = x.shape

    def body(x_ref, g_ref, o_ref):
        xv = x_ref[...]
        rstd = lax.rsqrt(jnp.mean(xv * xv, axis=-1, keepdims=True) + EPS)
        o_ref[...] = (xv * rstd * g_ref[...]).astype(o_ref.dtype)

    return pl.pallas_call(
        body, name=name, grid=(S // tm,),
        in_specs=[pl.BlockSpec((tm, D), lambda i: (i, 0)), pl.BlockSpec((1, D), lambda i: (0, 0))],
        out_specs=pl.BlockSpec((tm, D), lambda i: (i, 0)),
        out_shape=_sds((S, D), CDT), compiler_params=_params(1))(x, g)


def _rm_shape(S, d, width):
    return (S, width) if d == 1 else (d, S // d, width)


def _rm_spec(tm, d, width):
    if d == 1:
        return pl.BlockSpec((tm, width), lambda i: (i, 0))
    return pl.BlockSpec((d, tm // d, width), lambda i: (0, i, 0))


def _rm_put(dst_ref, cols, buf_ref, d):
    if d == 1:
        dst_ref[:, cols] = buf_ref[...].astype(dst_ref.dtype)
        return
    m = buf_ref.shape[0] // d
    for r in range(d):
        dst_ref[r, :, cols] = buf_ref[pl.ds(r, m, stride=d), :].astype(dst_ref.dtype)


def _rm_reader(buf_ref, src_ref, d):
    if d == 1:
        return lambda s: src_ref[:, s * HD:(s + 1) * HD].astype(F32)
    m = buf_ref.shape[1] // d
    for s in range(buf_ref.shape[0]):
        for r in range(d):
            buf_ref.at[s][pl.ds(r, m, stride=d), :] = src_ref[r, :, s * HD:(s + 1) * HD].astype(F32)
    return lambda s: buf_ref[s]


def _qknorm_fwd(proj, gqk, tm=512):
    S = proj.shape[0]
    W = 2 * ATT_W
    dil = [d for _, d in ATT_GROUPS]

    def body(p_ref, g_ref, o0, o1, o2, buf):
        outs = (o0, o1, o2)
        for hd in range(3 * ATT_HEADS):
            which, head = hd // ATT_HEADS, hd % ATT_HEADS
            grp, slot = head // ATT_HPG, head % ATT_HPG
            v = p_ref[:, hd * HD:(hd + 1) * HD].astype(F32)
            if which < 2:
                rstd = lax.rsqrt(jnp.mean(v * v, axis=-1, keepdims=True) + EPS)
                v = v * rstd * g_ref[:, hd * HD:(hd + 1) * HD]
            buf[...] = v
            _rm_put(outs[grp], slice(which * GW + slot * HD, which * GW + (slot + 1) * HD), buf, dil[grp])

    return pl.pallas_call(
        body, name="qknorm_fwd", grid=(S // tm,),
        in_specs=[pl.BlockSpec((tm, 3 * ATT_W), lambda i: (i, 0)), pl.BlockSpec((1, W), lambda i: (0, 0))],
        out_specs=[_rm_spec(tm, d, 3 * GW) for d in dil],
        out_shape=[_sds(_rm_shape(S, d, 3 * GW), CDT) for d in dil],
        scratch_shapes=[pltpu.VMEM((tm, HD), F32)],
        compiler_params=_params(1))(proj, gqk)


def _qknorm_bwd(proj, gqk, dqs, dks, dvs, dproj, tm=256):
    S = proj.shape[0]
    W = 2 * ATT_W
    dil = [d for _, d in ATT_GROUPS]

    def body(p_ref, g_ref, *refs):
        ins = refs[0:9]
        o_ref, dg_ref = refs[10], refs[11]
        bufs = refs[12:21]
        i = pl.program_id(0)

        @pl.when(i == 0)
        def _():
            dg_ref[...] = jnp.zeros_like(dg_ref)

        nat = [_rm_reader(bufs[j], ins[j], dil[j % 3]) for j in range(9)]
        dq_get, dk_get, dv_get = nat[0:3], nat[3:6], nat[6:9]
        for hd in range(2 * ATT_HEADS):
            sl = slice(hd * HD, (hd + 1) * HD)
            head = hd % ATT_HEADS
            grp, slot = head // ATT_HPG, head % ATT_HPG
            dn = (dq_get if hd < ATT_HEADS else dk_get)[grp](slot)
            v = p_ref[:, sl].astype(F32)
            rstd = lax.rsqrt(jnp.mean(v * v, axis=-1, keepdims=True) + EPS)
            vh = v * rstd
            dg_ref[:, sl] += jnp.sum(dn * vh, axis=0, keepdims=True)
            dvh = dn * g_ref[:, sl]
            o_ref[:, sl] = (rstd * (dvh - vh * jnp.mean(dvh * vh, axis=-1, keepdims=True))).astype(o_ref.dtype)
        for head in range(ATT_HEADS):
            grp, slot = head // ATT_HPG, head % ATT_HPG
            o_ref[:, W + head * HD:W + (head + 1) * HD] = dv_get[grp](slot).astype(o_ref.dtype)

    return pl.pallas_call(
        body, name="qknorm_bwd", grid=(S // tm,),
        in_specs=[pl.BlockSpec((tm, W), lambda i: (i, 0)), pl.BlockSpec((1, W), lambda i: (0, 0))]
        + [_rm_spec(tm, d, GW) for d in dil] * 3 + [pl.BlockSpec(memory_space=pl.ANY)],
        out_specs=[pl.BlockSpec((tm, 3 * ATT_W), lambda i: (i, 0)), pl.BlockSpec((1, W), lambda i: (0, 0))],
        out_shape=[_sds(dproj.shape, dproj.dtype), _sds((1, W), F32)],
        scratch_shapes=[pltpu.VMEM((ATT_HPG, tm, HD), F32)] * 9,
        input_output_aliases={11: 0},
        compiler_params=_params(1))(proj, gqk, *dqs, *dks, *dvs, dproj)


def _att_mask(n):
    qi = lax.broadcasted_iota(jnp.int32, (BLK, 2 * BLK), 0)
    kj = lax.broadcasted_iota(jnp.int32, (BLK, 2 * BLK), 1)
    dist = BLK + qi - kj
    valid = (dist >= 0) & (dist <= BLK) & ((kj >= BLK) | (n > 0))
    return valid, dist.astype(F32)


def _att_slopes(grp):
    return [2.0 ** (-8.0 * (grp * ATT_HPG + hh + 1) / ATT_HEADS) for hh in range(ATT_HPG)]


def _att_spec(d, row_fn, col=0):
    if d == 1:
        return pl.BlockSpec((BLK, GW), lambda r, n: (row_fn(n), col))
    return pl.BlockSpec((None, BLK, GW), lambda r, n: (r, row_fn(n), col))


def _att_qkv_specs(d, nb):
    last = nb - 1

    def cur(n):
        return jnp.minimum(n, last)

    def prev(n):
        return jnp.maximum(jnp.minimum(n, last) - 1, 0)

    return [_att_spec(d, cur, 0), _att_spec(d, prev, 1), _att_spec(d, cur, 1), _att_spec(d, prev, 2),
            _att_spec(d, cur, 2)]


def _att_fwd(grp, S, qkv):
    _, d = ATT_GROUPS[grp]
    L = S // d
    nb = L // BLK
    slopes = _att_slopes(grp)
    scale = HD ** -0.5

    def body(q_ref, kp_ref, kc_ref, vp_ref, vc_ref, o_ref, l_ref, s_buf, p_buf, den_buf):
        n = pl.program_id(1)
        valid, distf = _att_mask(n)
        heads = [slice(hh * HD, (hh + 1) * HD) for hh in range(ATT_HPG)]
        for hh, sl in enumerate(heads):
            k = jnp.concatenate([kp_ref[:, sl], kc_ref[:, sl]], axis=0)
            s_buf[hh] = _dot_nt(q_ref[:, sl], k)
        for hh, sl in enumerate(heads):
            s = s_buf[hh] * scale + (-slopes[hh] * d) * distf
            s = jnp.where(valid, s, -1e30)
            m = jnp.max(s, axis=-1, keepdims=True)
            p = jnp.exp(s - m)
            den = jnp.sum(p, axis=-1, keepdims=True)
            p_buf[hh] = p.astype(CDT)
            den_buf[hh] = jnp.broadcast_to(den, (BLK, HD))
            l_ref[:, sl] = jnp.broadcast_to(m + jnp.log(den), (BLK, HD))
        for hh, sl in enumerate(heads):
            v = jnp.concatenate([vp_ref[:, sl], vc_ref[:, sl]], axis=0)
            o_ref[:, sl] = _dot_nn(p_buf[hh], v) / den_buf[hh]

    out_spec = _att_spec(d, lambda n: n)
    return pl.pallas_call(
        body, name="att_fwd_g%d" % grp, grid=(d, nb),
        in_specs=_att_qkv_specs(d, nb),
        out_specs=[out_spec, out_spec],
        out_shape=[_sds(_rm_shape(S, d, GW), F32)] * 2,
        scratch_shapes=[pltpu.VMEM((ATT_HPG, BLK, 2 * BLK), F32), pltpu.VMEM((ATT_HPG, BLK, 2 * BLK), CDT),
                        pltpu.VMEM((ATT_HPG, BLK, HD), F32)],
        compiler_params=_params(2),
    )(qkv, qkv, qkv, qkv, qkv)


def _att_bwd(grp, S, qkv, lse, do_g, c_g):
    _, d = ATT_GROUPS[grp]
    L = S // d
    nb = L // BLK
    slopes = _att_slopes(grp)
    scale = HD ** -0.5
    last = nb - 1

    def body(q_ref, kp_ref, kc_ref, vp_ref, vc_ref, l_ref, do_ref, c_ref, dq_ref, dk_ref, dv_ref, ck, cv,
             s_buf, dp_buf, p_buf, ds_buf):
        n = pl.program_id(1)

        @pl.when(n == 0)
        def _():
            ck[...] = jnp.zeros_like(ck)
            cv[...] = jnp.zeros_like(cv)

        @pl.when(n < nb)
        def _():
            valid, distf = _att_mask(n)
            heads = [slice(hh * HD, (hh + 1) * HD) for hh in range(ATT_HPG)]
            for hh, sl in enumerate(heads):
                k = jnp.concatenate([kp_ref[:, sl], kc_ref[:, sl]], axis=0)
                v = jnp.concatenate([vp_ref[:, sl], vc_ref[:, sl]], axis=0)
                s_buf[hh] = _dot_nt(q_ref[:, sl], k)
                dp_buf[hh] = _dot_nt(do_ref[:, sl], v)
            for hh, sl in enumerate(heads):
                s = s_buf[hh] * scale + (-slopes[hh] * d) * distf
                p = jnp.where(valid, jnp.exp(s - l_ref[:, sl][:, 0:1]), 0.0)
                p_buf[hh] = p.astype(CDT)
                ds_buf[hh] = (p * (dp_buf[hh] + c_ref[:, sl][:, 0:1]) * scale).astype(CDT)
            for hh, sl in enumerate(heads):
                k = jnp.concatenate([kp_ref[:, sl], kc_ref[:, sl]], axis=0)
                ds = ds_buf[hh]
                dq_ref[:, sl] = _dot_nn(ds, k)
                dk = _dot_tn(ds, q_ref[:, sl])
                dv = _dot_tn(p_buf[hh], do_ref[:, sl])
                dk_ref[:, sl] = ck[:, sl] + dk[0:BLK]
                dv_ref[:, sl] = cv[:, sl] + dv[0:BLK]
                ck[:, sl] = dk[BLK:2 * BLK]
                cv[:, sl] = dv[BLK:2 * BLK]

        @pl.when(n == nb)
        def _():
            dk_ref[...] = ck[...]
            dv_ref[...] = cv[...]

    blk = (BLK, GW)
    at_q = _att_spec(d, lambda n: jnp.minimum(n, last))
    behind = _att_spec(d, lambda n: jnp.maximum(n - 1, 0))
    return pl.pallas_call(
        body, name="att_bwd_g%d" % grp, grid=(d, nb + 1),
        in_specs=_att_qkv_specs(d, nb) + [at_q, at_q, at_q],
        out_specs=[at_q, behind, behind],
        out_shape=[_sds(_rm_shape(S, d, GW), F32)] * 3,
        scratch_shapes=[pltpu.VMEM(blk, F32), pltpu.VMEM(blk, F32),
                        pltpu.VMEM((ATT_HPG, BLK, 2 * BLK), F32), pltpu.VMEM((ATT_HPG, BLK, 2 * BLK), F32),
                        pltpu.VMEM((ATT_HPG, BLK, 2 * BLK), CDT), pltpu.VMEM((ATT_HPG, BLK, 2 * BLK), CDT)],
        compiler_params=_params(2),
    )(qkv, qkv, qkv, qkv, qkv, lse, do_g, c_g)


def _mix_alpha(l0, l1, l2):
    mx = jnp.maximum(jnp.maximum(l0, l1), l2)
    e = [jnp.exp(l0 - mx), jnp.exp(l1 - mx), jnp.exp(l2 - mx)]
    tot = e[0] + e[1] + e[2]
    return [ei / tot for ei in e]


def _mix_fwd(S, os_, ls_, tm=512):
    dil = [d for _, d in ATT_GROUPS]

    def body(*refs):
        out, bufs = refs[6], refs[7:13]
        get = [_rm_reader(bufs[j], refs[j], dil[j % 3]) for j in range(6)]
        for s in range(ATT_HPG):
            al = _mix_alpha(*[get[3 + g](s) for g in range(3)])
            mixed = al[0] * get[0](s) + al[1] * get[1](s) + al[2] * get[2](s)
            out[:, s * HD:(s + 1) * HD] = mixed.astype(out.dtype)

    specs = [_rm_spec(tm, d, GW) for d in dil]
    return pl.pallas_call(
        body, name="mix_fwd", grid=(S // tm,), in_specs=specs * 2, out_specs=pl.BlockSpec((tm, GW), lambda i: (i, 0)),
        out_shape=_sds((S, GW), CDT), scratch_shapes=[pltpu.VMEM((ATT_HPG, tm, HD), F32)] * 6,
        compiler_params=_params(1))(*os_, *ls_)


def _mix_bwd(S, os_, ls_, do_a, tm=512):
    dil = [d for _, d in ATT_GROUPS]

    def body(*refs):
        d_ref, outs, bufs, tmp = refs[6], refs[7:13], refs[13:19], refs[19]
        get = [_rm_reader(bufs[j], refs[j], dil[j % 3]) for j in range(6)]
        for s in range(ATT_HPG):
            cols = slice(s * HD, (s + 1) * HD)
            al = _mix_alpha(*[get[3 + g](s) for g in range(3)])
            dv = d_ref[:, cols]
            o_a = al[0] * get[0](s) + al[1] * get[1](s) + al[2] * get[2](s)
            dsum = jnp.sum(dv * o_a, axis=-1, keepdims=True)
            for g in range(3):
                tmp[...] = al[g] * dv
                _rm_put(outs[g], cols, tmp, dil[g])
                tmp[...] = -(al[g] * dsum)
                _rm_put(outs[3 + g], cols, tmp, dil[g])

    specs = [_rm_spec(tm, d, GW) for d in dil]
    res = pl.pallas_call(
        body, name="mix_bwd", grid=(S // tm,), in_specs=specs * 2 + [pl.BlockSpec((tm, GW), lambda i: (i, 0))],
        out_specs=specs * 2,
        out_shape=[_sds(_rm_shape(S, d, GW), CDT) for d in dil] + [_sds(_rm_shape(S, d, GW), F32) for d in dil],
        scratch_shapes=[pltpu.VMEM((ATT_HPG, tm, HD), F32)] * 6 + [pltpu.VMEM((tm, HD), F32)],
        compiler_params=_params(1))(*os_, *ls_, do_a)
    return res[:3], res[3:]


def _ret_tables(dk):
    H, C = RET_HEADS, BLK
    log_g = jnp.log(1.0 - 2.0 ** (-5.0 - jnp.arange(H, dtype=F32)))
    idx = jnp.arange(C, dtype=F32)
    diff = idx[:, None] - idx[None, :]
    decay = jnp.where(diff >= 0, jnp.exp(log_g[:, None, None] * jnp.maximum(diff, 0.0)), 0.0)
    xi = jnp.exp(log_g[:, None] * (idx[None, :] + 1.0))
    zeta = jnp.exp(log_g[:, None] * (C - 1.0 - idx[None, :])) * (dk ** -0.5)
    g_chunk = jnp.exp(log_g * C)
    bc = lambda t: jnp.broadcast_to(t[:, :, None], (H, C, C))
    return decay, bc(xi), bc(zeta), jnp.broadcast_to(g_chunk[:, None, None], (H, 8, C))


def _gn_fwd(o, g, b):
    mu = jnp.mean(o, axis=-1, keepdims=True)
    xc = o - mu
    rstd = lax.rsqrt(jnp.mean(xc * xc, axis=-1, keepdims=True) + EPS)
    yh = xc * rstd
    return yh, rstd, yh * g + b


def _ret_specs(dk, dv, order):
    H = RET_HEADS
    qk_w, v_w = H * dk, H * dv
    off_q = 3 * ATT_W
    off_k, off_v, off_g = off_q + qk_w, off_q + 2 * qk_w, off_q + 2 * qk_w + v_w
    assert 2 * dk == dv and all(off % dv == 0 for off in (off_q, off_k, off_v, off_g))

    def col(off, j):
        return pl.BlockSpec((BLK, dv), lambda i: (order(i), off // dv + j))

    tab = pl.BlockSpec((H, BLK, BLK), lambda i: (0, 0, 0))
    return ([col(off_q, j) for j in range(H // 2)] + [col(off_k, j) for j in range(H // 2)]
            + [col(off_v, j) for j in range(H)] + [col(off_g, j) for j in range(H)]
            + [tab, tab, tab, pl.BlockSpec((H, 8, BLK), lambda i: (0, 0, 0))])


def _ret_heads(refs, dk):
    H = RET_HEADS
    q_refs, k_refs = refs[0:H // 2], refs[H // 2:H]
    v_refs, gr_refs = refs[H:2 * H], refs[2 * H:3 * H]

    def head(h):
        cols = slice((h % 2) * dk, (h % 2 + 1) * dk)
        return q_refs[h // 2][:, cols], k_refs[h // 2][:, cols], v_refs[h][...], gr_refs[h][...]

    return head, refs[3 * H:3 * H + 4]


def _ret_fwd(proj, gn_g, gn_b, dk, dv):
    S = proj.shape[0]
    N = S // BLK
    H = RET_HEADS
    kscale = dk ** -0.5
    n_in = 3 * H + 4

    def body(*refs):
        head, (dec_ref, xi_ref, zeta_ref, gc_ref) = _ret_heads(refs, dk)
        g_ref, b_ref, opre_ref, or_ref, st_ref, state, s_buf, cross_buf = refs[n_in:n_in + 8]
        n = pl.program_id(0)

        @pl.when(n == 0)
        def _():
            state[...] = jnp.zeros_like(state)

        for h in range(H):
            q, k, v, _ = head(h)
            s_buf[h] = _dot_nt(q, k)
            st = state[h]
            st_c = st.astype(CDT)
            st_ref[h] = st_c
            cross_buf[h] = _dot_nn(q, st_c)
            kz = (k.astype(F32) * zeta_ref[h][:, 0:1]).astype(CDT)
            state[h] = st * gc_ref[h][0:1, 0:1] + _dot_tn(kz, v)
        for h in range(H):
            vs = slice(h * dv, (h + 1) * dv)
            _, _, v, gr = head(h)
            s = s_buf[h] * kscale * dec_ref[h]
            o = _dot_nn(s.astype(CDT), v) + cross_buf[h] * xi_ref[h][:, 0:1]
            opre_ref[:, vs] = o
            _, _, y = _gn_fwd(o, g_ref[:, vs], b_ref[:, vs])
            gr = gr.astype(F32)
            or_ref[:, vs] = (y * (gr * _sigmoid(gr))).astype(or_ref.dtype)

    v_w = H * dv
    row = pl.BlockSpec((1, v_w), lambda i: (0, 0))
    tile = pl.BlockSpec((BLK, v_w), lambda i: (i, 0))
    return pl.pallas_call(
        body, name="ret_fwd", grid=(N,),
        in_specs=_ret_specs(dk, dv, lambda i: i) + [row, row],
        out_specs=[tile, tile, pl.BlockSpec((None, H, dk, dv), lambda i: (i, 0, 0, 0))],
        out_shape=[_sds((S, v_w), F32), _sds((S, v_w), CDT), _sds((N, H, dk, dv), CDT)],
        scratch_shapes=[pltpu.VMEM((H, dk, dv), F32), pltpu.VMEM((H, BLK, BLK), F32), pltpu.VMEM((H, BLK, dv), F32)],
        compiler_params=_params(1),
    )(*[proj] * (3 * H), *_ret_tables(dk), gn_g, gn_b)


def _ret_bwd(proj, gn_g, gn_b, o_pre, states, d_or, dga, dgb, dk, dv):
    S, in_w = proj.shape
    N = S // BLK
    H = RET_HEADS
    qk_w, v_w = H * dk, H * dv
    kscale = dk ** -0.5
    n_in = 3 * H + 4
    out_w = 2 * qk_w + 2 * v_w
    gate_w = dga.shape[1]
    col0 = 3 * ATT_W
    assert col0 + out_w + 2 * gate_w == in_w
    rev = lambda i: N - 1 - i

    def body(*refs):
        head, (dec_ref, xi_ref, zeta_ref, gc_ref) = _ret_heads(refs, dk)
        (g_ref, b_ref, opre_ref, st_ref, dor_ref, dga_ref, dgb_ref, dproj_ref, dg_ref, db_ref, dstate, stage,
         sem, do_buf, dox_buf, a_buf, g_buf, dq_buf, dk_buf, dv_buf) = refs[n_in:n_in + 20]
        i = pl.program_id(0)
        slot = i % 2
        out_ref = stage.at[slot]

        def out_copy(s, step):
            rows = pl.ds(pl.multiple_of(rev(step) * BLK, BLK), BLK)
            return pltpu.make_async_copy(stage.at[s], dproj_ref.at[rows, pl.ds(col0, in_w - col0)], sem.at[s])

        @pl.when(i >= 2)
        def _():
            out_copy(slot, i - 2).wait()

        @pl.when(i == 0)
        def _():
            dstate[...] = jnp.zeros_like(dstate)
            dg_ref[...] = jnp.zeros_like(dg_ref)
            db_ref[...] = jnp.zeros_like(db_ref)

        out_ref[:, out_w:out_w + gate_w] = dga_ref[...]
        out_ref[:, out_w + gate_w:out_w + 2 * gate_w] = dgb_ref[...]
        for h in range(H):
            vs = slice(h * dv, (h + 1) * dv)
            _, _, _, gr = head(h)
            gr = gr.astype(F32)
            sg = _sigmoid(gr)
            gain = g_ref[:, vs]
            yh, rstd, y = _gn_fwd(opre_ref[:, vs], gain, b_ref[:, vs])
            d_or_v = dor_ref[:, vs]
            dy = d_or_v * (gr * sg)
            out_ref[:, 2 * qk_w + v_w + h * dv:2 * qk_w + v_w + (h + 1) * dv] = (
                d_or_v * y * (sg * (1.0 + gr * (1.0 - sg)))).astype(out_ref.dtype)
            dg_ref[:, vs] += jnp.sum(dy * yh, axis=0, keepdims=True)
            db_ref[:, vs] += jnp.sum(dy, axis=0, keepdims=True)
            dyh = dy * gain
            do = rstd * (dyh - jnp.mean(dyh, axis=-1, keepdims=True)
                         - yh * jnp.mean(dyh * yh, axis=-1, keepdims=True))
            do_buf[h] = do.astype(CDT)
            dox_buf[h] = (do * xi_ref[h][:, 0:1]).astype(CDT)
        for h in range(H):
            q, k, v, _ = head(h)
            dox = dox_buf[h]
            a_buf[h] = _dot_nt(q, k)
            g_buf[h] = _dot_nt(do_buf[h], v)
            dsn = dstate[h]
            dsn_c = dsn.astype(CDT)
            kz = (k.astype(F32) * zeta_ref[h][:, 0:1]).astype(CDT)
            dq_buf[h] = _dot_nt(dox, st_ref[h])
            dk_buf[h] = _dot_nt(v, dsn_c)
            dv_buf[h] = _dot_nn(kz, dsn_c)
            dstate[h] = dsn * gc_ref[h][0:1, 0:1] + _dot_tn(q, dox)
        for h in range(H):
            q, k, _, _ = head(h)
            decay = dec_ref[h]
            a_c = (a_buf[h] * kscale * decay).astype(CDT)
            g_c = (g_buf[h] * decay).astype(CDT)
            dq = _dot_nn(g_c, k) * kscale + dq_buf[h]
            dkk = _dot_tn(g_c, q) * kscale + dk_buf[h] * zeta_ref[h][:, 0:1]
            dvv = _dot_tn(a_c, do_buf[h]) + dv_buf[h]
            out_ref[:, h * dk:(h + 1) * dk] = dq.astype(out_ref.dtype)
            out_ref[:, qk_w + h * dk:qk_w + (h + 1) * dk] = dkk.astype(out_ref.dtype)
            out_ref[:, 2 * qk_w + h * dv:2 * qk_w + (h + 1) * dv] = dvv.astype(out_ref.dtype)

        cp = out_copy(slot, i)
        cp.start()

        @pl.when(i == N - 1)
        def _():
            cp.wait()
            if N >= 2:
                out_copy(1 - slot, i - 1).wait()

    row = pl.BlockSpec((1, v_w), lambda i: (0, 0))
    tile = pl.BlockSpec((BLK, v_w), lambda i: (rev(i), 0))
    gate = pl.BlockSpec((BLK, gate_w), lambda i: (rev(i), 0))
    return pl.pallas_call(
        body, name="ret_bwd", grid=(N,),
        in_specs=_ret_specs(dk, dv, rev) + [row, row, tile,
                 pl.BlockSpec((None, H, dk, dv), lambda i: (rev(i), 0, 0, 0)), tile, gate, gate],
        out_specs=[pl.BlockSpec(memory_space=pl.ANY), row, row],
        out_shape=[_sds((S, in_w), CDT), _sds((1, v_w), F32), _sds((1, v_w), F32)],
        scratch_shapes=[pltpu.VMEM((H, dk, dv), F32), pltpu.VMEM((2, BLK, in_w - col0), CDT),
                        pltpu.SemaphoreType.DMA((2,)),
                        pltpu.VMEM((H, BLK, dv), CDT), pltpu.VMEM((H, BLK, dv), CDT),
                        pltpu.VMEM((H, BLK, BLK), F32), pltpu.VMEM((H, BLK, BLK), F32),
                        pltpu.VMEM((H, BLK, dk), F32), pltpu.VMEM((H, BLK, dk), F32), pltpu.VMEM((H, BLK, dv), F32)],
        compiler_params=_params(1),
    )(*[proj] * (3 * H), *_ret_tables(dk), gn_g, gn_b, o_pre, states, d_or, dga, dgb)


def _merge_fwd(o_a, o_r, wa, wb, proj, d_model, tm=512, tn=512):
    S, in_w = proj.shape
    off_a, off_b = in_w - 2 * d_model, in_w - d_model
    assert off_a % tn == 0 and off_b % tn == 0

    def body(oa_ref, or_ref, wa_ref, wb_ref, ga_ref, gb_ref, y_ref, pa_ref, pb_ref):
        pa = _dot_nn(oa_ref[...], wa_ref[...])
        pb = _dot_nn(or_ref[...], wb_ref[...])
        y = _sigmoid(ga_ref[...].astype(F32)) * pa + _sigmoid(gb_ref[...].astype(F32)) * pb
        y_ref[...] = y.astype(y_ref.dtype)
        pa_ref[...] = pa.astype(pa_ref.dtype)
        pb_ref[...] = pb.astype(pb_ref.dtype)

    ka, kb = o_a.shape[1], o_r.shape[1]
    out = pl.BlockSpec((tm, tn), lambda i, j: (i, j))
    return pl.pallas_call(
        body, name="merge_fwd", grid=(S // tm, d_model // tn),
        in_specs=[pl.BlockSpec((tm, ka), lambda i, j: (i, 0)), pl.BlockSpec((tm, kb), lambda i, j: (i, 0)),
                  pl.BlockSpec((ka, tn), lambda i, j: (0, j)), pl.BlockSpec((kb, tn), lambda i, j: (0, j)),
                  pl.BlockSpec((tm, tn), lambda i, j: (i, off_a // tn + j)),
                  pl.BlockSpec((tm, tn), lambda i, j: (i, off_b // tn + j))],
        out_specs=[out, out, out], out_shape=[_sds((S, d_model), CDT)] * 3,
        compiler_params=_params(2))(o_a, o_r, wa, wb, proj, proj)


def _local_step(x, target, w_in_mine, chip, others, far_w_in, small, late_weights, on_grads, deps0=()):
    S, D = x.shape
    ns_in = w_in_mine.shape[1]
    in_w = N_CHIPS * ns_in
    d_ff = 4 * D
    ret_v_w = 2 * D
    dv = ret_v_w // RET_HEADS
    dk = (in_w - 3 * ATT_W - 2 * ret_v_w - 2 * D) // (2 * RET_HEADS)
    gqk = jnp.concatenate([small["q_norm_g"].reshape(1, ATT_W), small["k_norm_g"].reshape(1, ATT_W)], axis=1)
    g1, g2 = small["norm1_g"], small["norm2_g"]
    gn_g, gn_b = small["ret_gn_g"], small["ret_gn_b"]

    xn = _rms_fwd("rms1_fwd", x, g1)
    proj_sds = _sds((S, in_w), CDT)
    (proj_mine,) = _matmul(
        "in_proj_mine", "nn", xn, w_in_mine, tm=512, tn=ns_in, tk=D, prefetch=[chip],
        outs=[(proj_sds, pl.BlockSpec((512, ns_in), lambda i, j, k, c: (i, c[0])))], epilogue=_ep_store, deps=deps0)
    w_in = far_w_in(proj_mine)
    (proj,) = _matmul(
        "in_proj_far", "nn", xn, w_in, tm=512, tn=ns_in, tk=D, prefetch=[others], n_cols=(N_CHIPS - 1) * ns_in,
        b_spec=pl.BlockSpec((None, D, ns_in), lambda i, j, k, o: (o[j], 0, 0)),
        outs=[(proj_sds, pl.BlockSpec((512, ns_in), lambda i, j, k, o: (i, o[j])))], epilogue=_ep_store,
        deps=[proj_mine], alias_dep_to_out=(0, 0), j_outer=True)
    qkv = _qknorm_fwd(proj, gqk)
    att = [_att_fwd(g, S, qkv[g]) for g in range(3)]
    os_, ls_ = [a[0] for a in att], [a[1] for a in att]
    o_a = _mix_fwd(S, os_, ls_)
    o_pre, o_r, states = _ret_fwd(proj, gn_g, gn_b, dk, dv)
    w = late_weights(o_r)
    y, pa, pb = _merge_fwd(o_a, o_r, w["w_proj_a"], w["w_proj_b"], proj, D)
    (x1,) = _matmul("out_proj", "nn", y, w["w_out"], tm=512, tn=D, tk=D,
                    extras=[(x, _mn(512, D))], outs=[(_sds((S, D), F32), _mn(512, D))], epilogue=_ep_resid)
    xn2 = _rms_fwd("rms2_fwd", x1, g2)
    hid, act = _matmul("mlp_up", "nn", xn2, w["w_up"], tm=512, tn=2048, tk=D, j_outer=True,
                       outs=[(_sds((S, d_ff), CDT), _mn(512, 2048))] * 2, epilogue=_ep_up)
    dx2, dx2c, loss_row = _matmul(
        "mlp_down_loss", "nn", act, w["w_down"], tm=512, tn=D, tk=d_ff,
        extras=[(x1, _mn(512, D)), (target, _mn(512, D))],
        outs=[(_sds((S, D), F32), _mn(512, D)), (_sds((S, D), CDT), _mn(512, D)), (_sds((1, D), F32), _row(D))],
        epilogue=functools.partial(_ep_down_loss, inv_d=1.0 / D))
    loss = 0.5 * jnp.sum(loss_row) / D

    (dh,) = _matmul("d_hidden", "nt", dx2c, w["w_down"], tm=512, tn=2048, tk=D, j_outer=True,
                    extras=[(hid, _mn(512, 2048))], outs=[(_sds((S, d_ff), CDT), _mn(512, 2048))], epilogue=_ep_dh)
    (gw_down,) = _matmul("dw_down", "tn", act, dx2c, tm=1024, tn=D, tk=1024,
                         outs=[(_sds((d_ff, D), F32), _mn(1024, D))], epilogue=_ep_store)
    (gw_up,) = _matmul("dw_up", "tn", xn2, dh, tm=D, tn=1024, tk=1024,
                       outs=[(_sds((D, d_ff), F32), _mn(D, 1024))], epilogue=_ep_store)
    tok = on_grads({"w_down": gw_down, "w_up": gw_up})
    dx1, dx1c, dg2 = _matmul(
        "d_x1", "nt", dh, w["w_up"], tm=512, tn=D, tk=d_ff,
        extras=[(x1, _mn(512, D)), (g2, _row(D)), (dx2, _mn(512, D))],
        outs=[(_sds((S, D), F32), _mn(512, D)), (_sds((S, D), CDT), _mn(512, D)), (_sds((1, D), F32), _row(D))],
        epilogue=_ep_rms_bwd, deps=[tok])

    gt = 512
    assert (in_w - 2 * D) % gt == 0
    off_a, off_b = (in_w - 2 * D) // gt, (in_w - D) // gt
    dpa, dpb, dga, dgb = _matmul(
        "d_gates", "nt", dx1c, w["w_out"], tm=512, tn=gt, tk=D,
        extras=[(proj, _mn(512, gt, off_a)), (proj, _mn(512, gt, off_b)), (pa, _mn(512, gt)), (pb, _mn(512, gt))],
        outs=[(_sds((S, D), CDT), _mn(512, gt))] * 4, epilogue=_ep_gates)
    (gw_out,) = _matmul("dw_out", "tn", y, dx1c, tm=D, tn=D, tk=1024,
                        outs=[(_sds((D, D), F32), _mn(D, D))], epilogue=_ep_store)
    (gw_pa,) = _matmul("dw_proj_a", "tn", o_a, dpa, tm=GW, tn=D, tk=1024,
                       outs=[(_sds((GW, D), F32), _mn(GW, D))], epilogue=_ep_store)
    (gw_pb,) = _matmul("dw_proj_b", "tn", o_r, dpb, tm=1024, tn=D, tk=1024,
                       outs=[(_sds((ret_v_w, D), F32), _mn(1024, D))], epilogue=_ep_store)
    (do_a,) = _matmul("d_o_a", "nt", dpa, w["w_proj_a"], tm=1024, tn=GW, tk=D,
                      outs=[(_sds((S, GW), F32), _mn(1024, GW))], epilogue=_ep_store)
    tok = on_grads({"w_out": gw_out, "w_proj_a": gw_pa, "w_proj_b": gw_pb})
    (d_or,) = _matmul("d_o_r", "nt", dpb, w["w_proj_b"], tm=512, tn=ret_v_w, tk=D,
                      outs=[(_sds((S, ret_v_w), F32), _mn(512, ret_v_w))], epilogue=_ep_store, deps=[tok])

    dproj, dgn_g, dgn_b = _ret_bwd(proj, gn_g, gn_b, o_pre, states, d_or, dga, dgb, dk, dv)
    do_gs, c_gs = _mix_bwd(S, os_, ls_, do_a)
    datt_parts = [_att_bwd(g, S, qkv[g], ls_[g], do_gs[g], c_gs[g]) for g in range(3)]
    dproj, dgqk = _qknorm_bwd(proj, gqk, [p[0] for p in datt_parts], [p[1] for p in datt_parts],
                              [p[2] for p in datt_parts], dproj)

    (gw_in,) = _matmul(
        "dw_in", "tn", xn, dproj, tm=512, tn=ns_in, tk=1024,
        outs=[(_sds((N_CHIPS, D, ns_in), F32), pl.BlockSpec((None, 512, ns_in), lambda i, j, k: (j, i, 0)))],
        epilogue=_ep_store)
    tok = on_grads({"w_in": gw_in})
    grad_x, _, dg1 = _matmul(
        "d_x", "nt", dproj, w_in, tm=512, tn=D, tk=ns_in, n_cols=D,
        b_spec=pl.BlockSpec((None, D, ns_in), lambda i, j, k: (k, 0, 0)),
        extras=[(x, _mn(512, D)), (g1, _row(D)), (dx1, _mn(512, D))],
        outs=[(_sds((S, D), F32), _mn(512, D)), (_sds((S, D), CDT), _mn(512, D)), (_sds((1, D), F32), _row(D))],
        epilogue=_ep_rms_bwd, deps=[tok])

    smallg = {"norm1_g": dg1, "q_norm_g": dgqk[:, :ATT_W], "k_norm_g": dgqk[:, ATT_W:],
              "ret_gn_g": dgn_g, "ret_gn_b": dgn_b, "norm2_g": dg2}
    return loss, grad_x, smallg


N_CHIPS = 4
N_DEV = 8


def _place():
    x, y, c = lax.axis_index("x"), lax.axis_index("y"), lax.axis_index("c")
    return x, y, c


def _other_chips(x, y):
    out = []
    for fx, fy in ((1, 0), (0, 1), (1, 1)):
        px = 1 - x if fx else x
        py = 1 - y if fy else y
        out.append(((px, py), 2 * px + py))
    return out


SEM_SPEC = pl.BlockSpec(memory_space=pltpu.SEMAPHORE)
ANY_SPEC = pl.BlockSpec(memory_space=pl.ANY)
EFFECT = pltpu.SideEffectType.DATAFLOW_SIDE_EFFECTING


def _ici_copies(kind, srcs, lands, send, recv):
    x, y, c = _place()
    me = 2 * x + y
    out = []
    for w, (s, l) in enumerate(zip(srcs, lands)):
        for j, ((px, py), pidx) in enumerate(_other_chips(x, y)):
            if kind == "gather":
                half = s.shape[0] // 2
                rows = pl.ds(c * half, half)
                src, dst_there, dst_here = s.at[rows, :], l.at[me, rows, :], l.at[pidx, rows, :]
            else:
                src, dst_there, dst_here = s.at[pidx], l.at[me], l.at[pidx]
            out.append((src, dst_there, dst_here, send.at[3 * w + j], recv.at[3 * w + j], (px, py, c)))
    return out


def _exchange_start(name, kind, srcs, land_shapes):
    n = len(srcs)

    def body(*refs):
        src_refs, land_refs = refs[:n], refs[n:2 * n]
        send, recv = refs[2 * n], refs[2 * n + 1]
        token = refs[-1]
        for src, dst, _, ss, rs, dev in _ici_copies(kind, src_refs, land_refs, send, recv):
            pltpu.make_async_remote_copy(src_ref=src, dst_ref=dst, send_sem=ss, recv_sem=rs, device_id=dev,
                                         device_id_type=MESH).start()
        token[...] = jnp.zeros_like(token)

    thru = [pltpu.HBM(s.shape, s.dtype) for s in srcs] + [pltpu.HBM(shape, dtype) for shape, dtype in land_shapes]
    res = pl.pallas_call(
        body, name=name,
        out_shape=(pltpu.SemaphoreType.DMA((3 * n,)), pltpu.SemaphoreType.DMA((3 * n,)), *thru, _sds((8, LANES), F32)),
        in_specs=[HBM_SPEC] * (2 * n), out_specs=(SEM_SPEC, SEM_SPEC, *[HBM_SPEC] * (2 * n), VMEM_SPEC),
        input_output_aliases={i: 2 + i for i in range(2 * n)},
        compiler_params=pltpu.CompilerParams(has_side_effects=EFFECT),
    )(*[pltpu.with_memory_space_constraint(s, pltpu.HBM) for s in srcs],
      *[pltpu.with_memory_space_constraint(lax.empty(shape, dtype), pltpu.HBM) for shape, dtype in land_shapes])
    return res[0], res[1], list(res[2:2 + n]), list(res[2 + n:2 + 2 * n]), res[-1]


def _exchange_wait(name, kind, send, recv, srcs, lands, after):
    n = len(srcs)

    def body(*refs):
        src_refs, land_refs = refs[:n], refs[n:2 * n]
        send_ref, recv_ref = refs[2 * n], refs[2 * n + 1]
        for src, _, dst, ss, rs, dev in _ici_copies(kind, src_refs, land_refs, send_ref, recv_ref):
            cp = pltpu.make_async_remote_copy(src_ref=src, dst_ref=dst, send_sem=ss, recv_sem=rs, device_id=dev,
                                              device_id_type=MESH)
            cp.wait_send()
            cp.wait_recv()

    thru = [pltpu.HBM(t.shape, t.dtype) for t in list(srcs) + list(lands)]
    res = pl.pallas_call(
        body, name=name, out_shape=thru,
        in_specs=[HBM_SPEC] * (2 * n) + [SEM_SPEC, SEM_SPEC, ANY_SPEC], out_specs=[HBM_SPEC] * (2 * n),
        input_output_aliases={i: i for i in range(2 * n)},
        compiler_params=pltpu.CompilerParams(has_side_effects=EFFECT),
    )(*srcs, *lands, send, recv, after)
    return list(res[:n]), list(res[n:])


PAIR_TILE_ELEMS = 1 << 19


def _pair_fill(name, gathered, mine, core, others, chip):
    k, r, C = gathered.shape
    half = r // 2
    tr = _row_tile(half, C, PAIR_TILE_ELEMS, mult=16)
    nt = half // tr
    n_far = N_CHIPS - 1

    def body(c_ref, o_ref, chip_ref, in_ref, mine_ref, out_ref, slot, send, recv):
        j = pl.program_id(0)
        b = (j * nt + pl.program_id(1)) % 2
        x, y, c = _place()
        cp = pltpu.make_async_remote_copy(src_ref=in_ref, dst_ref=slot.at[b], send_sem=send.at[b],
                                          recv_sem=recv.at[b], device_id=(x, y, 1 - c), device_id_type=MESH)

        @pl.when(j < n_far)
        def _():
            cp.start()
            cp.wait_recv()
            out_ref[...] = slot[b]
            cp.wait_send()

        @pl.when(j >= n_far)
        def _():
            out_ref[...] = mine_ref[...]

    def far(j):
        return jnp.minimum(j, n_far - 1)

    grid_spec = pltpu.PrefetchScalarGridSpec(
        num_scalar_prefetch=3, grid=(n_far + 2, nt),
        in_specs=[pl.BlockSpec((tr, C), lambda j, i, c, o, m: (
                      (2 * o[far(j)] + c[0]) * nt + jnp.where(j < n_far, i, nt - 1), 0)),
                  pl.BlockSpec((tr, C), lambda j, i, c, o, m: (jnp.where(j < n_far, 0, (j - n_far) * nt + i), 0))],
        out_specs=pl.BlockSpec((tr, C), lambda j, i, c, o, m: (
            jnp.where(j < n_far, 2 * o[far(j)] + 1 - c[0], 2 * m[0] + j - n_far) * nt + i, 0)),
        scratch_shapes=[pltpu.VMEM((2, tr, C), gathered.dtype), pltpu.SemaphoreType.DMA((2,)),
                        pltpu.SemaphoreType.DMA((2,))])
    out = pl.pallas_call(body, name=name, grid_spec=grid_spec, out_shape=_sds((k * r, C), gathered.dtype),
                         input_output_aliases={3: 0}, compiler_params=_params(2))(
                             core, others, chip, gathered.reshape(k * r, C), mine)
    return out.reshape(k, r, C)


def _pair_reduce(name, g, core):
    k, R, C = g.shape
    half = R // 2
    tr = _row_tile(half, C, PAIR_TILE_ELEMS, mult=16)
    nt = half // tr

    def body(c_ref, mine_ref, give_ref, out_ref, wire_ref, stage, slot, send, recv):
        b = (pl.program_id(0) * nt + pl.program_id(1)) % 2
        x, y, c = _place()
        stage[b] = give_ref[...].astype(stage.dtype)
        cp = pltpu.make_async_remote_copy(src_ref=stage.at[b], dst_ref=slot.at[b], send_sem=send.at[b],
                                          recv_sem=recv.at[b], device_id=(x, y, 1 - c), device_id_type=MESH)
        cp.start()
        cp.wait_recv()
        tot = mine_ref[...] + slot[b].astype(F32)
        out_ref[...] = tot
        wire_ref[...] = tot.astype(wire_ref.dtype)
        cp.wait_send()

    blk = (tr, C)
    out_spec = pl.BlockSpec(blk, lambda s, i, c: (s * nt + i, 0))
    grid_spec = pltpu.PrefetchScalarGridSpec(
        num_scalar_prefetch=1, grid=(k, nt),
        in_specs=[pl.BlockSpec(blk, lambda s, i, c: ((2 * s + c[0]) * nt + i, 0)),
                  pl.BlockSpec(blk, lambda s, i, c: ((2 * s + 1 - c[0]) * nt + i, 0))],
        out_specs=[out_spec, out_spec],
        scratch_shapes=[pltpu.VMEM((2, tr, C), CDT), pltpu.VMEM((2, tr, C), CDT), pltpu.SemaphoreType.DMA((2,)),
                        pltpu.SemaphoreType.DMA((2,))])
    g2 = g.reshape(k * R, C)
    out, wire = pl.pallas_call(body, name=name, grid_spec=grid_spec,
                               out_shape=[_sds((k * half, C), F32), _sds((k * half, C), CDT)],
                               compiler_params=_params(2))(core, g2, g2)
    return out, wire.reshape(k, half, C)


def _pair_share(name, f, core):
    half, C = f.shape
    tr = _row_tile(half, C, PAIR_TILE_ELEMS)
    nt = half // tr

    def body(c_ref, f_ref, out_ref, slot, send, recv):
        p = pl.program_id(1)
        b = pl.program_id(0) % 2
        x, y, c = _place()
        cp = pltpu.make_async_remote_copy(src_ref=f_ref, dst_ref=slot.at[b], send_sem=send.at[b],
                                          recv_sem=recv.at[b], device_id=(x, y, 1 - c), device_id_type=MESH)

        @pl.when(p == 0)
        def _():
            cp.start()
            out_ref[...] = f_ref[...]

        @pl.when(p == 1)
        def _():
            cp.wait_recv()
            out_ref[...] = slot[b]
            cp.wait_send()

    grid_spec = pltpu.PrefetchScalarGridSpec(
        num_scalar_prefetch=1, grid=(nt, 2),
        in_specs=[pl.BlockSpec((tr, C), lambda i, p, c: (i, 0))],
        out_specs=pl.BlockSpec((tr, C), lambda i, p, c: (jnp.where(p == 0, c[0], 1 - c[0]) * nt + i, 0)),
        scratch_shapes=[pltpu.VMEM((2, tr, C), F32), pltpu.SemaphoreType.DMA((2,)), pltpu.SemaphoreType.DMA((2,))])
    return pl.pallas_call(body, name=name, grid_spec=grid_spec, out_shape=_sds((2 * half, C), F32),
                          compiler_params=_params(2))(core, f)


def _all_reduce_small(v):
    r, cdim = v.shape

    def body(v_ref, o_ref, buf, send, recv):
        x, y, c = _place()
        me = 4 * x + 2 * y + c
        buf[me] = v_ref[...]
        sends = []
        for m in range(1, N_DEV):
            px = 1 - x if m & 4 else x
            py = 1 - y if m & 2 else y
            pc = 1 - c if m & 1 else c
            cp = pltpu.make_async_remote_copy(src_ref=v_ref, dst_ref=buf.at[me], send_sem=send.at[m - 1],
                                              recv_sem=recv.at[m - 1], device_id=(px, py, pc), device_id_type=MESH)
            cp.start()
            sends.append((cp, 4 * px + 2 * py + pc))
        for m, (cp, pidx) in enumerate(sends):
            pltpu.make_async_remote_copy(src_ref=v_ref, dst_ref=buf.at[pidx], send_sem=send.at[m], recv_sem=recv.at[m],
                                         device_id=(x, y, c), device_id_type=MESH).wait_recv()
        for cp, _ in sends:
            cp.wait_send()
        tot = buf[0]
        for k in range(1, N_DEV):
            tot = tot + buf[k]
        o_ref[...] = tot

    return pl.pallas_call(
        body, name="all_reduce_small", in_specs=[VMEM_SPEC], out_specs=VMEM_SPEC,
        out_shape=_sds((r, cdim), F32),
        scratch_shapes=[pltpu.VMEM((N_DEV, r, cdim), F32), pltpu.SemaphoreType.DMA((N_DEV - 1,)),
                        pltpu.SemaphoreType.DMA((N_DEV - 1,))],
    )(v)


def _row_tile(rows, cols, budget_elems=1 << 18, mult=8):
    if rows % mult:
        return rows
    t = max(mult, (budget_elems // cols) // mult * mult)
    while rows % t:
        t -= mult
    return t


def _sum4(name, own, by_chip, chip, others):
    k, R, C = by_chip.shape
    tr = _row_tile(R, C, mult=16)
    nt = R // tr

    def body(chip_ref, others_ref, own_ref, a_ref, b_ref, c_ref, o_ref):
        o_ref[...] = ((own_ref[...] + a_ref[...].astype(F32)) + b_ref[...].astype(F32)) + c_ref[...].astype(F32)

    def from_chip(j):
        return pl.BlockSpec((tr, C), lambda i, chip, oth: (oth[j] * nt + i, 0))

    grid_spec = pltpu.PrefetchScalarGridSpec(
        num_scalar_prefetch=2, grid=(nt,),
        in_specs=[pl.BlockSpec((tr, C), lambda i, chip, oth: (chip[0] * nt + i, 0)),
                  from_chip(0), from_chip(1), from_chip(2)],
        out_specs=pl.BlockSpec((tr, C), lambda i, chip, oth: (i, 0)))
    by2 = by_chip.reshape(k * R, C)
    return pl.pallas_call(body, name=name, grid_spec=grid_spec, out_shape=_sds((R, C), F32),
                          compiler_params=_params(1))(chip, others, own, by2, by2, by2)


def _adamw(name, w, g, m, v):
    R, C = w.shape
    tr = _row_tile(R, C, 1 << 17)

    def body(w_ref, g_ref, m_ref, v_ref, d_ref, nm_ref, nv_ref):
        gv = g_ref[...]
        nm = ADAM_B1 * m_ref[...] + (1.0 - ADAM_B1) * gv
        nv = ADAM_B2 * v_ref[...] + (1.0 - ADAM_B2) * (gv * gv)
        m_hat = nm / (1.0 - ADAM_B1 ** ADAM_STEP)
        v_hat = nv / (1.0 - ADAM_B2 ** ADAM_STEP)
        d_ref[...] = -ADAM_LR * (m_hat / (jnp.sqrt(v_hat) + ADAM_EPS) + ADAM_WD * w_ref[...])
        nm_ref[...] = nm
        nv_ref[...] = nv

    spec = pl.BlockSpec((tr, C), lambda i: (i, 0))
    return pl.pallas_call(body, name=name, grid=(R // tr,), in_specs=[spec] * 4, out_specs=[spec] * 3,
                          out_shape=[_sds((R, C), F32)] * 3, compiler_params=_params(1))(w, g, m, v)


BIG = ("w_in", "w_proj_a", "w_proj_b", "w_out", "w_up", "w_down")
COL_SHARDED = ("w_in", "w_proj_a", "w_up")
SMALL = ("norm1_g", "q_norm_g", "k_norm_g", "ret_gn_g", "ret_gn_b", "norm2_g")
ALL_W = ("norm1_g", "w_in", "q_norm_g", "k_norm_g", "ret_gn_g", "ret_gn_b", "w_proj_a", "w_proj_b", "w_out",
         "norm2_g", "w_up", "w_down")
LANES = 128


def _to_full(name, gathered):
    k, r, c = gathered.shape
    if name in COL_SHARDED:
        return gathered.transpose(1, 0, 2).reshape(r, k * c)
    return gathered.reshape(k * r, c)


def _to_shard_major(name, full):
    if name in COL_SHARDED:
        r, c4 = full.shape
        return full.reshape(r, N_CHIPS, c4 // N_CHIPS).transpose(1, 0, 2)
    r4, c = full.shape
    return full.reshape(N_CHIPS, r4 // N_CHIPS, c)


def kernel(x, norm1_g, w_in, q_norm_g, k_norm_g, ret_gn_g, ret_gn_b, w_proj_a, w_proj_b, w_out, norm2_g, w_up, w_down, loss_target, m_norm1_g, m_w_in, m_q_norm_g, m_k_norm_g, m_ret_gn_g, m_ret_gn_b, m_w_proj_a, m_w_proj_b, m_w_out, m_norm2_g, m_w_up, m_w_down, v_norm1_g, v_w_in, v_q_norm_g, v_k_norm_g, v_ret_gn_g, v_ret_gn_b, v_w_proj_a, v_w_proj_b, v_w_out, v_norm2_g, v_w_up, v_w_down):
    weights = dict(norm1_g=norm1_g, w_in=w_in, q_norm_g=q_norm_g, k_norm_g=k_norm_g, ret_gn_g=ret_gn_g,
                   ret_gn_b=ret_gn_b, w_proj_a=w_proj_a, w_proj_b=w_proj_b, w_out=w_out, norm2_g=norm2_g,
                   w_up=w_up, w_down=w_down)
    moments_m = dict(norm1_g=m_norm1_g, w_in=m_w_in, q_norm_g=m_q_norm_g, k_norm_g=m_k_norm_g, ret_gn_g=m_ret_gn_g,
                     ret_gn_b=m_ret_gn_b, w_proj_a=m_w_proj_a, w_proj_b=m_w_proj_b, w_out=m_w_out,
                     norm2_g=m_norm2_g, w_up=m_w_up, w_down=m_w_down)
    moments_v = dict(norm1_g=v_norm1_g, w_in=v_w_in, q_norm_g=v_q_norm_g, k_norm_g=v_k_norm_g, ret_gn_g=v_ret_gn_g,
                     ret_gn_b=v_ret_gn_b, w_proj_a=v_w_proj_a, w_proj_b=v_w_proj_b, w_out=v_w_out,
                     norm2_g=v_norm2_g, w_up=v_w_up, w_down=v_w_down)

    mx, my = lax.axis_index("x"), lax.axis_index("y")
    core = lax.axis_index("c").astype(jnp.int32).reshape(1)
    chip = (2 * mx + my).astype(jnp.int32).reshape(1)
    others = jnp.stack([2 * (1 - mx) + my, 2 * mx + 1 - my, 2 * (1 - mx) + 1 - my]).astype(jnp.int32)
    shards = {n: weights[n][0].astype(CDT) for n in BIG}
    def start_gather(name, names):
        return _exchange_start(name, "gather", [shards[n] for n in names],
                               [((N_CHIPS,) + shards[n].shape, CDT) for n in names])

    i_send, i_recv, i_srcs, i_lands, i_token = start_gather("gather_w_in_start", ["w_in"])
    late = [n for n in BIG if n != "w_in"]
    l_send, l_recv, l_srcs, l_lands, l_token = start_gather("gather_late_start", late)

    def far_w_in(after):
        srcs, lands = _exchange_wait("gather_w_in_wait", "gather", i_send, i_recv, i_srcs, i_lands, after)
        return _pair_fill("pair_fill_w_in", lands[0], srcs[0], core, others, chip)

    def late_weights(after):
        srcs, lands = _exchange_wait("gather_late_wait", "gather", l_send, l_recv, l_srcs, l_lands, after)
        out = {}
        for n, mine, land in zip(late, srcs, lands):
            out[n] = _to_full(n, _pair_fill("pair_fill_%s" % n, land, mine, core, others, chip))
        return out

    pending = []

    def on_grads(group):
        names = list(group)
        red = [_pair_reduce("pair_reduce_%s" % n, g if g.ndim == 3 else _to_shard_major(n, g), core)
               for n, g in group.items()]
        wires = [wire for _, wire in red]
        send, recv, srcs, lands, token = _exchange_start(
            "scatter_start_%s" % names[0], "scatter", wires, [(wire.shape, wire.dtype) for wire in wires])
        pending.append((names, [own for own, _ in red], send, recv, srcs, lands))
        return token

    small = {n: weights[n].reshape(1, -1) for n in SMALL}

    loss, grad_x, small_g = _local_step(x[0], loss_target[0], i_srcs[0], chip, others, far_w_in, small,
                                        late_weights, on_grads, deps0=[i_token, l_token])
    loss = lax.psum(loss, ("x", "y", "c"))

    grads = {}
    for names, owns, send, recv, srcs, lands in pending:
        _, got = _exchange_wait("scatter_wait_%s" % names[0], "scatter", send, recv, srcs, lands, grad_x)
        for n, own, by_chip in zip(names, owns, got):
            half = _sum4("chip_sum_%s" % n, own, by_chip, chip, others)
            grads[n] = _pair_share("pair_share_%s" % n, half, core)

    packed = jnp.concatenate([small_g[n].reshape(1, -1) for n in SMALL], axis=1)
    sizes = [small_g[n].size for n in SMALL]
    red = _all_reduce_small(packed.reshape(-1, LANES)).reshape(1, -1)
    off = 0
    for n, sz in zip(SMALL, sizes):
        grads[n] = red[:, off:off + sz]
        off += sz

    out_g, out_d, out_m, out_v = {}, {}, {}, {}
    for n in ALL_W:
        shape = weights[n].shape
        two_d = (shape[-2], shape[-1]) if n in BIG else (1, weights[n].size)
        g2 = grads[n].reshape(two_d)
        d, nm, nv = _adamw("adamw_%s" % n, weights[n].reshape(two_d), g2, moments_m[n].reshape(two_d),
                           moments_v[n].reshape(two_d))
        out_g[n], out_d[n], out_m[n], out_v[n] = (t.reshape(shape) for t in (g2, d, nm, nv))

    return (loss, grad_x[None], *[out_g[n] for n in ALL_W], *[out_d[n] for n in ALL_W],
            *[out_m[n] for n in ALL_W], *[out_v[n] for n in ALL_W])
```

```python
import functools
import math

import jax
import jax.numpy as jnp
from jax import lax
from jax.experimental import pallas as pl
from jax.experimental.pallas import tpu as pltpu

CDT = jnp.bfloat16
F32 = jnp.float32
EPS = 1e-6

ATT_GROUPS = ((128, 1), (512, 4), (2048, 16))
ATT_HPG = 4
ATT_HEADS = 12
HD = 128
BLK = 128
ATT_W = ATT_HEADS * HD
GW = ATT_HPG * HD
RET_HEADS = 4

ADAM_LR = 0.001
ADAM_B1 = 0.9
ADAM_B2 = 0.999
ADAM_EPS = 1e-08
ADAM_WD = 0.01
ADAM_STEP = 10

VMEM_LIMIT_BYTES = 56 * 1024 * 1024
MXU_DIM = 256
MESH = pl.DeviceIdType.MESH
HBM_SPEC = pl.BlockSpec(memory_space=pltpu.HBM)
VMEM_SPEC = pl.BlockSpec(memory_space=pltpu.VMEM)


def _params(n_axes):
    return pltpu.CompilerParams(dimension_semantics=("arbitrary",) * n_axes,
                                vmem_limit_bytes=VMEM_LIMIT_BYTES)


def _dot_nn(a, b):
    return jnp.dot(a, b, preferred_element_type=F32)


def _dot_nt(a, b):
    return lax.dot_general(a, b, (((1,), (1,)), ((), ())), preferred_element_type=F32)


def _dot_tn(a, b):
    return lax.dot_general(a, b, (((0,), (0,)), ((), ())), preferred_element_type=F32)


def _sigmoid(v):
    return 1.0 / (1.0 + jnp.exp(-v))


def _matmul(name, mode, a, b, *, tm, tn, tk, extras=(), outs, epilogue, deps=(), b_spec=None, n_cols=None,
            prefetch=(), alias_dep_to_out=None, j_outer=False):
    deps = [d for d in deps if d is not None]
    if mode == "tn":
        K, M = a.shape
    else:
        M, K = a.shape
    if b_spec is None:
        (N, K2) = b.shape if mode == "nt" else b.shape[::-1]
        assert K == K2, (name, a.shape, b.shape)
        if mode == "nt":
            b_spec = pl.BlockSpec((tn, tk), lambda i, j, k, *p: (j, k))
        else:
            b_spec = pl.BlockSpec((tk, tn), lambda i, j, k, *p: (k, j))
    else:
        N = n_cols
    assert M % tm == 0 and N % tn == 0 and K % tk == 0, (name, a.shape, b.shape)
    ni, nj, nk = M // tm, N // tn, K // tk
    if mode == "tn":
        a_spec = pl.BlockSpec((tk, tm), lambda i, j, k, *p: (k, i))
    else:
        a_spec = pl.BlockSpec((tm, tk), lambda i, j, k, *p: (i, k))
    dot = {"nn": _dot_nn, "nt": _dot_nt, "tn": _dot_tn}[mode]
    n_ex, n_out, n_dep, n_pre = len(extras), len(outs), len(deps), len(prefetch)
    grid = (ni, nj, nk)
    if j_outer:
        grid = (nj, ni, nk)

        def swapped(spec):
            return pl.BlockSpec(spec.block_shape, lambda j, i, k, *p: spec.index_map(i, j, k, *p))

        a_spec, b_spec = swapped(a_spec), swapped(b_spec)
        extras = [(e, swapped(s)) for e, s in extras]
        outs = [(o, swapped(s)) for o, s in outs]

    def body(*refs):
        refs = refs[n_pre:]
        a_ref, b_ref = refs[0], refs[1]
        ex = refs[2:2 + n_ex]
        out = refs[2 + n_ex + n_dep:2 + n_ex + n_dep + n_out]
        acc = refs[-1] if nk > 1 else None
        i = pl.program_id(1 if j_outer else 0)
        k = pl.program_id(2)
        if nk == 1:
            epilogue(dot(a_ref[...].astype(CDT), b_ref[...].astype(CDT)), ex, out, i)
            return

        @pl.when(k == 0)
        def _():
            acc[...] = jnp.zeros_like(acc)

        acc[...] += dot(a_ref[...].astype(CDT), b_ref[...].astype(CDT))

        @pl.when(k == nk - 1)
        def _():
            epilogue(acc[...], ex, out, i)

    grid_spec = pltpu.PrefetchScalarGridSpec(
        num_scalar_prefetch=n_pre, grid=grid,
        in_specs=[a_spec, b_spec] + [s for _, s in extras] + [pl.BlockSpec(memory_space=pl.ANY)] * n_dep,
        out_specs=[s for _, s in outs],
        scratch_shapes=[pltpu.VMEM((tm, tn), F32)] if nk > 1 else [])
    aliases = {}
    if alias_dep_to_out is not None:
        aliases = {n_pre + 2 + n_ex + alias_dep_to_out[0]: alias_dep_to_out[1]}
    res = pl.pallas_call(
        body, name=name, grid_spec=grid_spec, out_shape=[o for o, _ in outs], input_output_aliases=aliases,
        compiler_params=_params(3),
    )(*prefetch, a, b, *[e for e, _ in extras], *deps)
    return res


def _mn(tm, tn, col_off=0):
    return pl.BlockSpec((tm, tn), lambda i, j, k, *p: (i, j + col_off))


def _row(tn):
    return pl.BlockSpec((1, tn), lambda i, j, k, *p: (0, j))


def _ep_store(acc, ex, out, i):
    out[0][...] = acc.astype(out[0].dtype)


def _ep_resid(acc, ex, out, i):
    out[0][...] = ex[0][...] + acc


def _ep_up(acc, ex, out, i):
    out[0][...] = acc.astype(out[0].dtype)
    r = jnp.maximum(acc, 0.0)
    out[1][...] = (r * r).astype(out[1].dtype)


def _ep_down_loss(acc, ex, out, i, inv_d):
    diff = (ex[0][...] + acc) - ex[1][...]
    dx2 = diff * inv_d
    out[0][...] = dx2
    out[1][...] = dx2.astype(out[1].dtype)

    @pl.when(i == 0)
    def _():
        out[2][...] = jnp.zeros_like(out[2])

    out[2][...] += jnp.sum(diff * diff, axis=0, keepdims=True)


def _ep_dh(acc, ex, out, i):
    h = ex[0][...].astype(F32)
    out[0][...] = (acc * (2.0 * jnp.maximum(h, 0.0))).astype(out[0].dtype)


def _ep_rms_bwd(acc, ex, out, i):
    x = ex[0][...]
    g = ex[1][...]
    rstd = lax.rsqrt(jnp.mean(x * x, axis=-1, keepdims=True) + EPS)
    xh = x * rstd
    dxh = acc * g
    dx = ex[2][...] + rstd * (dxh - xh * jnp.mean(dxh * xh, axis=-1, keepdims=True))
    out[0][...] = dx
    out[1][...] = dx.astype(out[1].dtype)

    @pl.when(i == 0)
    def _():
        out[2][...] = jnp.zeros_like(out[2])

    out[2][...] += jnp.sum(acc * xh, axis=0, keepdims=True)


def _ep_gates(acc, ex, out, i):
    sa = _sigmoid(ex[0][...].astype(F32))
    sb = _sigmoid(ex[1][...].astype(F32))
    dpa = acc * sa
    dpb = acc * sb
    out[0][...] = dpa.astype(out[0].dtype)
    out[1][...] = dpb.astype(out[1].dtype)
    out[2][...] = (dpa * ex[2][...].astype(F32) * (1.0 - sa)).astype(out[2].dtype)
    out[3][...] = (dpb * ex[3][...].astype(F32) * (1.0 - sb)).astype(out[3].dtype)


def _sds(shape, dtype):
    return jax.ShapeDtypeStruct(shape, dtype)


def _rms_fwd(name, x, g, tm=512):
    S, D = x.shape

    def body(x_ref, g_ref, o_ref):
        xv = x_ref[...]
        rstd = lax.rsqrt(jnp.mean(xv * xv, axis=-1, keepdims=True) + EPS)
        o_ref[...] = (xv * rstd * g_ref[...]).astype(o_ref.dtype)

    return pl.pallas_call(
        body, name=name, grid=(S // tm,),
        in_specs=[pl.BlockSpec((tm, D), lambda i: (i, 0)), pl.BlockSpec((1, D), lambda i: (0, 0))],
        out_specs=pl.BlockSpec((tm, D), lambda i: (i, 0)),
        out_shape=_sds((S, D), CDT), compiler_params=_params(1))(x, g)


def _rm_shape(S, d, width):
    return (S, width) if d == 1 else (d, S // d, width)


def _rm_spec(tm, d, width):
    if d == 1:
        return pl.BlockSpec((tm, width), lambda i: (i, 0))
    return pl.BlockSpec((d, tm // d, width), lambda i: (0, i, 0))


def _rm_put(dst_ref, cols, buf_ref, d):
    if d == 1:
        dst_ref[:, cols] = buf_ref[...].astype(dst_ref.dtype)
        return
    m = buf_ref.shape[0] // d
    for r in range(d):
        dst_ref[r, :, cols] = buf_ref[pl.ds(r, m, stride=d), :].astype(dst_ref.dtype)


def _rm_reader(buf_ref, src_ref, d):
    if d == 1:
        return lambda s: src_ref[:, s * HD:(s + 1) * HD].astype(F32)
    m = buf_ref.shape[1] // d
    for s in range(buf_ref.shape[0]):
        for r in range(d):
            buf_ref.at[s][pl.ds(r, m, stride=d), :] = src_ref[r, :, s * HD:(s + 1) * HD].astype(F32)
    return lambda s: buf_ref[s]


def _qknorm_fwd(proj, gqk, tm=512):
    S = proj.shape[0]
    W = 2 * ATT_W
    dil = [d for _, d in ATT_GROUPS]

    def body(p_ref, g_ref, o0, o1, o2, buf):
        outs = (o0, o1, o2)
        for hd in range(3 * ATT_HEADS):
            which, head = hd // ATT_HEADS, hd % ATT_HEADS
            grp, slot = head // ATT_HPG, head % ATT_HPG
            v = p_ref[:, hd * HD:(hd + 1) * HD].astype(F32)
            if which < 2:
                rstd = lax.rsqrt(jnp.mean(v * v, axis=-1, keepdims=True) + EPS)
                v = v * rstd * g_ref[:, hd * HD:(hd + 1) * HD]
            buf[...] = v
            _rm_put(outs[grp], slice(which * GW + slot * HD, which * GW + (slot + 1) * HD), buf, dil[grp])

    return pl.pallas_call(
        body, name="qknorm_fwd", grid=(S // tm,),
        in_specs=[pl.BlockSpec((tm, 3 * ATT_W), lambda i: (i, 0)), pl.BlockSpec((1, W), lambda i: (0, 0))],
        out_specs=[_rm_spec(tm, d, 3 * GW) for d in dil],
        out_shape=[_sds(_rm_shape(S, d, 3 * GW), CDT) for d in dil],
        scratch_shapes=[pltpu.VMEM((tm, HD), F32)],
        compiler_params=_params(1))(proj, gqk)


def _qknorm_bwd(proj, gqk, dqs, dks, dvs, dproj, tm=256):
    S = proj.shape[0]
    W = 2 * ATT_W
    dil = [d for _, d in ATT_GROUPS]

    def body(p_ref, g_ref, *refs):
        ins = refs[0:9]
        o_ref, dg_ref = refs[10], refs[11]
        bufs = refs[12:21]
        i = pl.program_id(0)

        @pl.when(i == 0)
        def _():
            dg_ref[...] = jnp.zeros_like(dg_ref)

        nat = [_rm_reader(bufs[j], ins[j], dil[j % 3]) for j in range(9)]
        dq_get, dk_get, dv_get = nat[0:3], nat[3:6], nat[6:9]
        for hd in range(2 * ATT_HEADS):
            sl = slice(hd * HD, (hd + 1) * HD)
            head = hd % ATT_HEADS
            grp, slot = head // ATT_HPG, head % ATT_HPG
            dn = (dq_get if hd < ATT_HEADS else dk_get)[grp](slot)
            v = p_ref[:, sl].astype(F32)
            rstd = lax.rsqrt(jnp.mean(v * v, axis=-1, keepdims=True) + EPS)
            vh = v * rstd
            dg_ref[:, sl] += jnp.sum(dn * vh, axis=0, keepdims=True)
            dvh = dn * g_ref[:, sl]
            o_ref[:, sl] = (rstd * (dvh - vh * jnp.mean(dvh * vh, axis=-1, keepdims=True))).astype(o_ref.dtype)
        for head in range(ATT_HEADS):
            grp, slot = head // ATT_HPG, head % ATT_HPG
            o_ref[:, W + head * HD:W + (head + 1) * HD] = dv_get[grp](slot).astype(o_ref.dtype)

    return pl.pallas_call(
        body, name="qknorm_bwd", grid=(S // tm,),
        in_specs=[pl.BlockSpec((tm, W), lambda i: (i, 0)), pl.BlockSpec((1, W), lambda i: (0, 0))]
        + [_rm_spec(tm, d, GW) for d in dil] * 3 + [pl.BlockSpec(memory_space=pl.ANY)],
        out_specs=[pl.BlockSpec((tm, 3 * ATT_W), lambda i: (i, 0)), pl.BlockSpec((1, W), lambda i: (0, 0))],
        out_shape=[_sds(dproj.shape, dproj.dtype), _sds((1, W), F32)],
        scratch_shapes=[pltpu.VMEM((ATT_HPG, tm, HD), F32)] * 9,
        input_output_aliases={11: 0},
        compiler_params=_params(1))(proj, gqk, *dqs, *dks, *dvs, dproj)


def _att_mask(n):
    qi = lax.broadcasted_iota(jnp.int32, (BLK, 2 * BLK), 0)
    kj = lax.broadcasted_iota(jnp.int32, (BLK, 2 * BLK), 1)
    dist = BLK + qi - kj
    valid = (dist >= 0) & (dist <= BLK) & ((kj >= BLK) | (n > 0))
    return valid, dist.astype(F32)


def _att_slopes(grp):
    return [2.0 ** (-8.0 * (grp * ATT_HPG + hh + 1) / ATT_HEADS) for hh in range(ATT_HPG)]


def _att_spec(d, row_fn, col=0):
    if d == 1:
        return pl.BlockSpec((BLK, GW), lambda r, n: (row_fn(n), col))
    return pl.BlockSpec((None, BLK, GW), lambda r, n: (r, row_fn(n), col))


def _att_qkv_specs(d, nb):
    last = nb - 1

    def cur(n):
        return jnp.minimum(n, last)

    def prev(n):
        return jnp.maximum(jnp.minimum(n, last) - 1, 0)

    return [_att_spec(d, cur, 0), _att_spec(d, prev, 1), _att_spec(d, cur, 1), _att_spec(d, prev, 2),
            _att_spec(d, cur, 2)]


def _att_fwd(grp, S, qkv):
    _, d = ATT_GROUPS[grp]
    L = S // d
    nb = L // BLK
    slopes = _att_slopes(grp)
    scale = HD ** -0.5

    def body(q_ref, kp_ref, kc_ref, vp_ref, vc_ref, o_ref, l_ref, s_buf, p_buf, den_buf):
        n = pl.program_id(1)
        valid, distf = _att_mask(n)
        heads = [slice(hh * HD, (hh + 1) * HD) for hh in range(ATT_HPG)]
        for hh, sl in enumerate(heads):
            k = jnp.concatenate([kp_ref[:, sl], kc_ref[:, sl]], axis=0)
            s_buf[hh] = _dot_nt(q_ref[:, sl], k)
        for hh, sl in enumerate(heads):
            s = s_buf[hh] * scale + (-slopes[hh] * d) * distf
            s = jnp.where(valid, s, -1e30)
            m = jnp.max(s, axis=-1, keepdims=True)
            p = jnp.exp(s - m)
            den = jnp.sum(p, axis=-1, keepdims=True)
            p_buf[hh] = p.astype(CDT)
            den_buf[hh] = jnp.broadcast_to(den, (BLK, HD))
            l_ref[:, sl] = jnp.broadcast_to(m + jnp.log(den), (BLK, HD))
        for hh, sl in enumerate(heads):
            v = jnp.concatenate([vp_ref[:, sl], vc_ref[:, sl]], axis=0)
            o_ref[:, sl] = _dot_nn(p_buf[hh], v) / den_buf[hh]

    out_spec = _att_spec(d, lambda n: n)
    return pl.pallas_call(
        body, name="att_fwd_g%d" % grp, grid=(d, nb),
        in_specs=_att_qkv_specs(d, nb),
        out_specs=[out_spec, out_spec],
        out_shape=[_sds(_rm_shape(S, d, GW), F32)] * 2,
        scratch_shapes=[pltpu.VMEM((ATT_HPG, BLK, 2 * BLK), F32), pltpu.VMEM((ATT_HPG, BLK, 2 * BLK), CDT),
                        pltpu.VMEM((ATT_HPG, BLK, HD), F32)],
        compiler_params=_params(2),
    )(qkv, qkv, qkv, qkv, qkv)


def _att_bwd(grp, S, qkv, lse, do_g, c_g):
    _, d = ATT_GROUPS[grp]
    L = S // d
    nb = L // BLK
    slopes = _att_slopes(grp)
    scale = HD ** -0.5
    last = nb - 1

    def body(q_ref, kp_ref, kc_ref, vp_ref, vc_ref, l_ref, do_ref, c_ref, dq_ref, dk_ref, dv_ref, ck, cv,
             s_buf, dp_buf, p_buf, ds_buf):
        n = pl.program_id(1)

        @pl.when(n == 0)
        def _():
            ck[...] = jnp.zeros_like(ck)
            cv[...] = jnp.zeros_like(cv)

        @pl.when(n < nb)
        def _():
            valid, distf = _att_mask(n)
            heads = [slice(hh * HD, (hh + 1) * HD) for hh in range(ATT_HPG)]
            for hh, sl in enumerate(heads):
                k = jnp.concatenate([kp_ref[:, sl], kc_ref[:, sl]], axis=0)
                v = jnp.concatenate([vp_ref[:, sl], vc_ref[:, sl]], axis=0)
                s_buf[hh] = _dot_nt(q_ref[:, sl], k)
                dp_buf[hh] = _dot_nt(do_ref[:, sl], v)
            for hh, sl in enumerate(heads):
                s = s_buf[hh] * scale + (-slopes[hh] * d) * distf
                p = jnp.where(valid, jnp.exp(s - l_ref[:, sl][:, 0:1]), 0.0)
                p_buf[hh] = p.astype(CDT)
                ds_buf[hh] = (p * (dp_buf[hh] + c_ref[:, sl][:, 0:1]) * scale).astype(CDT)
            for hh, sl in enumerate(heads):
                k = jnp.concatenate([kp_ref[:, sl], kc_ref[:, sl]], axis=0)
                ds = ds_buf[hh]
                dq_ref[:, sl] = _dot_nn(ds, k)
                dk = _dot_tn(ds, q_ref[:, sl])
                dv = _dot_tn(p_buf[hh], do_ref[:, sl])
                dk_ref[:, sl] = ck[:, sl] + dk[0:BLK]
                dv_ref[:, sl] = cv[:, sl] + dv[0:BLK]
                ck[:, sl] = dk[BLK:2 * BLK]
                cv[:, sl] = dv[BLK:2 * BLK]

        @pl.when(n == nb)
        def _():
            dk_ref[...] = ck[...]
            dv_ref[...] = cv[...]

    blk = (BLK, GW)
    at_q = _att_spec(d, lambda n: jnp.minimum(n, last))
    behind = _att_spec(d, lambda n: jnp.maximum(n - 1, 0))
    return pl.pallas_call(
        body, name="att_bwd_g%d" % grp, grid=(d, nb + 1),
        in_specs=_att_qkv_specs(d, nb) + [at_q, at_q, at_q],
        out_specs=[at_q, behind, behind],
        out_shape=[_sds(_rm_shape(S, d, GW), F32)] * 3,
        scratch_shapes=[pltpu.VMEM(blk, F32), pltpu.VMEM(blk, F32),
                        pltpu.VMEM((ATT_HPG, BLK, 2 * BLK), F32), pltpu.VMEM((ATT_HPG, BLK, 2 * BLK), F32),
                        pltpu.VMEM((ATT_HPG, BLK, 2 * BLK), CDT), pltpu.VMEM((ATT_HPG, BLK, 2 * BLK), CDT)],
        compiler_params=_params(2),
    )(qkv, qkv, qkv, qkv, qkv, lse, do_g, c_g)


def _mix_alpha(l0, l1, l2):
    mx = jnp.maximum(jnp.maximum(l0, l1), l2)
    e = [jnp.exp(l0 - mx), jnp.exp(l1 - mx), jnp.exp(l2 - mx)]
    tot = e[0] + e[1] + e[2]
    return [ei / tot for ei in e]


def _mix_fwd(S, os_, ls_, tm=512):
    dil = [d for _, d in ATT_GROUPS]

    def body(*refs):
        out, bufs = refs[6], refs[7:13]
        get = [_rm_reader(bufs[j], refs[j], dil[j % 3]) for j in range(6)]
        for s in range(ATT_HPG):
            al = _mix_alpha(*[get[3 + g](s) for g in range(3)])
            mixed = al[0] * get[0](s) + al[1] * get[1](s) + al[2] * get[2](s)
            out[:, s * HD:(s + 1) * HD] = mixed.astype(out.dtype)

    specs = [_rm_spec(tm, d, GW) for d in dil]
    return pl.pallas_call(
        body, name="mix_fwd", grid=(S // tm,), in_specs=specs * 2, out_specs=pl.BlockSpec((tm, GW), lambda i: (i, 0)),
        out_shape=_sds((S, GW), CDT), scratch_shapes=[pltpu.VMEM((ATT_HPG, tm, HD), F32)] * 6,
        compiler_params=_params(1))(*os_, *ls_)


def _mix_bwd(S, os_, ls_, do_a, tm=512):
    dil = [d for _, d in ATT_GROUPS]

    def body(*refs):
        d_ref, outs, bufs, tmp = refs[6], refs[7:13], refs[13:19], refs[19]
        get = [_rm_reader(bufs[j], refs[j], dil[j % 3]) for j in range(6)]
        for s in range(ATT_HPG):
            cols = slice(s * HD, (s + 1) * HD)
            al = _mix_alpha(*[get[3 + g](s) for g in range(3)])
            dv = d_ref[:, cols]
            o_a = al[0] * get[0](s) + al[1] * get[1](s) + al[2] * get[2](s)
            dsum = jnp.sum(dv * o_a, axis=-1, keepdims=True)
            for g in range(3):
                tmp[...] = al[g] * dv
                _rm_put(outs[g], cols, tmp, dil[g])
                tmp[...] = -(al[g] * dsum)
                _rm_put(outs[3 + g], cols, tmp, dil[g])

    specs = [_rm_spec(tm, d, GW) for d in dil]
    res = pl.pallas_call(
        body, name="mix_bwd", grid=(S // tm,), in_specs=specs * 2 + [pl.BlockSpec((tm, GW), lambda i: (i, 0))],
        out_specs=specs * 2,
        out_shape=[_sds(_rm_shape(S, d, GW), CDT) for d in dil] + [_sds(_rm_shape(S, d, GW), F32) for d in dil],
        scratch_shapes=[pltpu.VMEM((ATT_HPG, tm, HD), F32)] * 6 + [pltpu.VMEM((tm, HD), F32)],
        compiler_params=_params(1))(*os_, *ls_, do_a)
    return res[:3], res[3:]


def _ret_tables(dk):
    H, C = RET_HEADS, BLK
    log_g = jnp.log(1.0 - 2.0 ** (-5.0 - jnp.arange(H, dtype=F32)))
    idx = jnp.arange(C, dtype=F32)
    diff = idx[:, None] - idx[None, :]
    decay = jnp.where(diff >= 0, jnp.exp(log_g[:, None, None] * jnp.maximum(diff, 0.0)), 0.0)
    xi = jnp.exp(log_g[:, None] * (idx[None, :] + 1.0))
    zeta = jnp.exp(log_g[:, None] * (C - 1.0 - idx[None, :])) * (dk ** -0.5)
    g_chunk = jnp.exp(log_g * C)
    bc = lambda t: jnp.broadcast_to(t[:, :, None], (H, C, C))
    return decay, bc(xi), bc(zeta), jnp.broadcast_to(g_chunk[:, None, None], (H, 8, C))


def _gn_fwd(o, g, b):
    mu = jnp.mean(o, axis=-1, keepdims=True)
    xc = o - mu
    rstd = lax.rsqrt(jnp.mean(xc * xc, axis=-1, keepdims=True) + EPS)
    yh = xc * rstd
    return yh, rstd, yh * g + b


def _ret_specs(dk, dv, order):
    H = RET_HEADS
    qk_w, v_w = H * dk, H * dv
    off_q = 3 * ATT_W
    off_k, off_v, off_g = off_q + qk_w, off_q + 2 * qk_w, off_q + 2 * qk_w + v_w
    assert 2 * dk == dv and all(off % dv == 0 for off in (off_q, off_k, off_v, off_g))

    def col(off, j):
        return pl.BlockSpec((BLK, dv), lambda i: (order(i), off // dv + j))

    tab = pl.BlockSpec((H, BLK, BLK), lambda i: (0, 0, 0))
    return ([col(off_q, j) for j in range(H // 2)] + [col(off_k, j) for j in range(H // 2)]
            + [col(off_v, j) for j in range(H)] + [col(off_g, j) for j in range(H)]
            + [tab, tab, tab, pl.BlockSpec((H, 8, BLK), lambda i: (0, 0, 0))])


def _ret_heads(refs, dk):
    H = RET_HEADS
    q_refs, k_refs = refs[0:H // 2], refs[H // 2:H]
    v_refs, gr_refs = refs[H:2 * H], refs[2 * H:3 * H]

    def head(h):
        cols = slice((h % 2) * dk, (h % 2 + 1) * dk)
        return q_refs[h // 2][:, cols], k_refs[h // 2][:, cols], v_refs[h][...], gr_refs[h][...]

    return head, refs[3 * H:3 * H + 4]


def _ret_fwd(proj, gn_g, gn_b, dk, dv):
    S = proj.shape[0]
    N = S // BLK
    H = RET_HEADS
    kscale = dk ** -0.5
    n_in = 3 * H + 4

    def body(*refs):
        head, (dec_ref, xi_ref, zeta_ref, gc_ref) = _ret_heads(refs, dk)
        g_ref, b_ref, opre_ref, or_ref, st_ref, state, s_buf, cross_buf = refs[n_in:n_in + 8]
        n = pl.program_id(0)

        @pl.when(n == 0)
        def _():
            state[...] = jnp.zeros_like(state)

        for h in range(H):
            q, k, v, _ = head(h)
            s_buf[h] = _dot_nt(q, k)
            st = state[h]
            st_c = st.astype(CDT)
            st_ref[h] = st_c
            cross_buf[h] = _dot_nn(q, st_c)
            kz = (k.astype(F32) * zeta_ref[h][:, 0:1]).astype(CDT)
            state[h] = st * gc_ref[h][0:1, 0:1] + _dot_tn(kz, v)
        for h in range(H):
            vs = slice(h * dv, (h + 1) * dv)
            _, _, v, gr = head(h)
            s = s_buf[h] * kscale * dec_ref[h]
            o = _dot_nn(s.astype(CDT), v) + cross_buf[h] * xi_ref[h][:, 0:1]
            opre_ref[:, vs] = o
            _, _, y = _gn_fwd(o, g_ref[:, vs], b_ref[:, vs])
            gr = gr.astype(F32)
            or_ref[:, vs] = (y * (gr * _sigmoid(gr))).astype(or_ref.dtype)

    v_w = H * dv
    row = pl.BlockSpec((1, v_w), lambda i: (0, 0))
    tile = pl.BlockSpec((BLK, v_w), lambda i: (i, 0))
    return pl.pallas_call(
        body, name="ret_fwd", grid=(N,),
        in_specs=_ret_specs(dk, dv, lambda i: i) + [row, row],
        out_specs=[tile, tile, pl.BlockSpec((None, H, dk, dv), lambda i: (i, 0, 0, 0))],
        out_shape=[_sds((S, v_w), F32), _sds((S, v_w), CDT), _sds((N, H, dk, dv), CDT)],
        scratch_shapes=[pltpu.VMEM((H, dk, dv), F32), pltpu.VMEM((H, BLK, BLK), F32), pltpu.VMEM((H, BLK, dv), F32)],
        compiler_params=_params(1),
    )(*[proj] * (3 * H), *_ret_tables(dk), gn_g, gn_b)


def _ret_bwd(proj, gn_g, gn_b, o_pre, states, d_or, dga, dgb, dk, dv):
    S, in_w = proj.shape
    N = S // BLK
    H = RET_HEADS
    qk_w, v_w = H * dk, H * dv
    kscale = dk ** -0.5
    n_in = 3 * H + 4
    out_w = 2 * qk_w + 2 * v_w
    gate_w = dga.shape[1]
    col0 = 3 * ATT_W
    assert col0 + out_w + 2 * gate_w == in_w
    rev = lambda i: N - 1 - i

    def body(*refs):
        head, (dec_ref, xi_ref, zeta_ref, gc_ref) = _ret_heads(refs, dk)
        (g_ref, b_ref, opre_ref, st_ref, dor_ref, dga_ref, dgb_ref, dproj_ref, dg_ref, db_ref, dstate, stage,
         sem, do_buf, dox_buf, a_buf, g_buf, dq_buf, dk_buf, dv_buf) = refs[n_in:n_in + 20]
        i = pl.program_id(0)
        slot = i % 2
        out_ref = stage.at[slot]

        def out_copy(s, step):
            rows = pl.ds(pl.multiple_of(rev(step) * BLK, BLK), BLK)
            return pltpu.make_async_copy(stage.at[s], dproj_ref.at[rows, pl.ds(col0, in_w - col0)], sem.at[s])

        @pl.when(i >= 2)
        def _():
            out_copy(slot, i - 2).wait()

        @pl.when(i == 0)
        def _():
            dstate[...] = jnp.zeros_like(dstate)
            dg_ref[...] = jnp.zeros_like(dg_ref)
            db_ref[...] = jnp.zeros_like(db_ref)

        out_ref[:, out_w:out_w + gate_w] = dga_ref[...]
        out_ref[:, out_w + gate_w:out_w + 2 * gate_w] = dgb_ref[...]
        for h in range(H):
            vs = slice(h * dv, (h + 1) * dv)
            _, _, _, gr = head(h)
            gr = gr.astype(F32)
            sg = _sigmoid(gr)
            gain = g_ref[:, vs]
            yh, rstd, y = _gn_fwd(opre_ref[:, vs], gain, b_ref[:, vs])
            d_or_v = dor_ref[:, vs]
            dy = d_or_v * (gr * sg)
            out_ref[:, 2 * qk_w + v_w + h * dv:2 * qk_w + v_w + (h + 1) * dv] = (
                d_or_v * y * (sg * (1.0 + gr * (1.0 - sg)))).astype(out_ref.dtype)
            dg_ref[:, vs] += jnp.sum(dy * yh, axis=0, keepdims=True)
            db_ref[:, vs] += jnp.sum(dy, axis=0, keepdims=True)
            dyh = dy * gain
            do = rstd * (dyh - jnp.mean(dyh, axis=-1, keepdims=True)
                         - yh * jnp.mean(dyh * yh, axis=-1, keepdims=True))
            do_buf[h] = do.astype(CDT)
            dox_buf[h] = (do * xi_ref[h][:, 0:1]).astype(CDT)
        for h in range(H):
            q, k, v, _ = head(h)
            dox = dox_buf[h]
            a_buf[h] = _dot_nt(q, k)
            g_buf[h] = _dot_nt(do_buf[h], v)
            dsn = dstate[h]
            dsn_c = dsn.astype(CDT)
            kz = (k.astype(F32) * zeta_ref[h][:, 0:1]).astype(CDT)
            dq_buf[h] = _dot_nt(dox, st_ref[h])
            dk_buf[h] = _dot_nt(v, dsn_c)
            dv_buf[h] = _dot_nn(kz, dsn_c)
            dstate[h] = dsn * gc_ref[h][0:1, 0:1] + _dot_tn(q, dox)
        for h in range(H):
            q, k, _, _ = head(h)
            decay = dec_ref[h]
            a_c = (a_buf[h] * kscale * decay).astype(CDT)
            g_c = (g_buf[h] * decay).astype(CDT)
            dq = _dot_nn(g_c, k) * kscale + dq_buf[h]
            dkk = _dot_tn(g_c, q) * kscale + dk_buf[h] * zeta_ref[h][:, 0:1]
            dvv = _dot_tn(a_c, do_buf[h]) + dv_buf[h]
            out_ref[:, h * dk:(h + 1) * dk] = dq.astype(out_ref.dtype)
            out_ref[:, qk_w + h * dk:qk_w + (h + 1) * dk] = dkk.astype(out_ref.dtype)
            out_ref[:, 2 * qk_w + h * dv:2 * qk_w + (h + 1) * dv] = dvv.astype(out_ref.dtype)

        cp = out_copy(slot, i)
        cp.start()

        @pl.when(i == N - 1)
        def _():
            cp.wait()
            if N >= 2:
                out_copy(1 - slot, i - 1).wait()

    row = pl.BlockSpec((1, v_w), lambda i: (0, 0))
    tile = pl.BlockSpec((BLK, v_w), lambda i: (rev(i), 0))
    gate = pl.BlockSpec((BLK, gate_w), lambda i: (rev(i), 0))
    return pl.pallas_call(
        body, name="ret_bwd", grid=(N,),
        in_specs=_ret_specs(dk, dv, rev) + [row, row, tile,
                 pl.BlockSpec((None, H, dk, dv), lambda i: (rev(i), 0, 0, 0)), tile, gate, gate],
        out_specs=[pl.BlockSpec(memory_space=pl.ANY), row, row],
        out_shape=[_sds((S, in_w), CDT), _sds((1, v_w), F32), _sds((1, v_w), F32)],
        scratch_shapes=[pltpu.VMEM((H, dk, dv), F32), pltpu.VMEM((2, BLK, in_w - col0), CDT),
                        pltpu.SemaphoreType.DMA((2,)),
                        pltpu.VMEM((H, BLK, dv), CDT), pltpu.VMEM((H, BLK, dv), CDT),
                        pltpu.VMEM((H, BLK, BLK), F32), pltpu.VMEM((H, BLK, BLK), F32),
                        pltpu.VMEM((H, BLK, dk), F32), pltpu.VMEM((H, BLK, dk), F32), pltpu.VMEM((H, BLK, dv), F32)],
        compiler_params=_params(1),
    )(*[proj] * (3 * H), *_ret_tables(dk), gn_g, gn_b, o_pre, states, d_or, dga, dgb)


def _merge_fwd(o_a, o_r, wa, wb, proj, d_model, tm=512, tn=512):
    S, in_w = proj.shape
    off_a, off_b = in_w - 2 * d_model, in_w - d_model
    assert off_a % tn == 0 and off_b % tn == 0

    def body(oa_ref, or_ref, wa_ref, wb_ref, ga_ref, gb_ref, y_ref, pa_ref, pb_ref):
        pa = _dot_nn(oa_ref[...], wa_ref[...])
        pb = _dot_nn(or_ref[...], wb_ref[...])
        y = _sigmoid(ga_ref[...].astype(F32)) * pa + _sigmoid(gb_ref[...].astype(F32)) * pb
        y_ref[...] = y.astype(y_ref.dtype)
        pa_ref[...] = pa.astype(pa_ref.dtype)
        pb_ref[...] = pb.astype(pb_ref.dtype)

    ka, kb = o_a.shape[1], o_r.shape[1]
    out = pl.BlockSpec((tm, tn), lambda i, j: (i, j))
    return pl.pallas_call(
        body, name="merge_fwd", grid=(S // tm, d_model // tn),
        in_specs=[pl.BlockSpec((tm, ka), lambda i, j: (i, 0)), pl.BlockSpec((tm, kb), lambda i, j: (i, 0)),
                  pl.BlockSpec((ka, tn), lambda i, j: (0, j)), pl.BlockSpec((kb, tn), lambda i, j: (0, j)),
                  pl.BlockSpec((tm, tn), lambda i, j: (i, off_a // tn + j)),
                  pl.BlockSpec((tm, tn), lambda i, j: (i, off_b // tn + j))],
        out_specs=[out, out, out], out_shape=[_sds((S, d_model), CDT)] * 3,
        compiler_params=_params(2))(o_a, o_r, wa, wb, proj, proj)


def _local_step(x, target, w_in_mine, chip, others, far_w_in, small, late_weights, on_grads, deps0=()):
    S, D = x.shape
    ns_in = w_in_mine.shape[1]
    in_w = N_CHIPS * ns_in
    d_ff = 4 * D
    ret_v_w = 2 * D
    dv = ret_v_w // RET_HEADS
    dk = (in_w - 3 * ATT_W - 2 * ret_v_w - 2 * D) // (2 * RET_HEADS)
    gqk = jnp.concatenate([small["q_norm_g"].reshape(1, ATT_W), small["k_norm_g"].reshape(1, ATT_W)], axis=1)
    g1, g2 = small["norm1_g"], small["norm2_g"]
    gn_g, gn_b = small["ret_gn_g"], small["ret_gn_b"]

    xn = _rms_fwd("rms1_fwd", x, g1)
    proj_sds = _sds((S, in_w), CDT)
    (proj_mine,) = _matmul(
        "in_proj_mine", "nn", xn, w_in_mine, tm=512, tn=ns_in, tk=D, prefetch=[chip],
        outs=[(proj_sds, pl.BlockSpec((512, ns_in), lambda i, j, k, c: (i, c[0])))], epilogue=_ep_store, deps=deps0)
    w_in = far_w_in(proj_mine)
    (proj,) = _matmul(
        "in_proj_far", "nn", xn, w_in, tm=512, tn=ns_in, tk=D, prefetch=[others], n_cols=(N_CHIPS - 1) * ns_in,
        b_spec=pl.BlockSpec((None, D, ns_in), lambda i, j, k, o: (o[j], 0, 0)),
        outs=[(proj_sds, pl.BlockSpec((512, ns_in), lambda i, j, k, o: (i, o[j])))], epilogue=_ep_store,
        deps=[proj_mine], alias_dep_to_out=(0, 0), j_outer=True)
    qkv = _qknorm_fwd(proj, gqk)
    att = [_att_fwd(g, S, qkv[g]) for g in range(3)]
    os_, ls_ = [a[0] for a in att], [a[1] for a in att]
    o_a = _mix_fwd(S, os_, ls_)
    o_pre, o_r, states = _ret_fwd(proj, gn_g, gn_b, dk, dv)
    w = late_weights(o_r)
    y, pa, pb = _merge_fwd(o_a, o_r, w["w_proj_a"], w["w_proj_b"], proj, D)
    (x1,) = _matmul("out_proj", "nn", y, w["w_out"], tm=512, tn=D, tk=D,
                    extras=[(x, _mn(512, D))], outs=[(_sds((S, D), F32), _mn(512, D))], epilogue=_ep_resid)
    xn2 = _rms_fwd("rms2_fwd", x1, g2)
    hid, act = _matmul("mlp_up", "nn", xn2, w["w_up"], tm=512, tn=2048, tk=D, j_outer=True,
                       outs=[(_sds((S, d_ff), CDT), _mn(512, 2048))] * 2, epilogue=_ep_up)
    dx2, dx2c, loss_row = _matmul(
        "mlp_down_loss", "nn", act, w["w_down"], tm=512, tn=D, tk=d_ff,
        extras=[(x1, _mn(512, D)), (target, _mn(512, D))],
        outs=[(_sds((S, D), F32), _mn(512, D)), (_sds((S, D), CDT), _mn(512, D)), (_sds((1, D), F32), _row(D))],
        epilogue=functools.partial(_ep_down_loss, inv_d=1.0 / D))
    loss = 0.5 * jnp.sum(loss_row) / D

    (dh,) = _matmul("d_hidden", "nt", dx2c, w["w_down"], tm=512, tn=2048, tk=D, j_outer=True,
                    extras=[(hid, _mn(512, 2048))], outs=[(_sds((S, d_ff), CDT), _mn(512, 2048))], epilogue=_ep_dh)
    (gw_down,) = _matmul("dw_down", "tn", act, dx2c, tm=1024, tn=D, tk=1024,
                         outs=[(_sds((d_ff, D), F32), _mn(1024, D))], epilogue=_ep_store)
    (gw_up,) = _matmul("dw_up", "tn", xn2, dh, tm=D, tn=1024, tk=1024,
                       outs=[(_sds((D, d_ff), F32), _mn(D, 1024))], epilogue=_ep_store)
    tok = on_grads({"w_down": gw_down, "w_up": gw_up})
    dx1, dx1c, dg2 = _matmul(
        "d_x1", "nt", dh, w["w_up"], tm=512, tn=D, tk=d_ff,
        extras=[(x1, _mn(512, D)), (g2, _row(D)), (dx2, _mn(512, D))],
        outs=[(_sds((S, D), F32), _mn(512, D)), (_sds((S, D), CDT), _mn(512, D)), (_sds((1, D), F32), _row(D))],
        epilogue=_ep_rms_bwd, deps=[tok])

    gt = 512
    assert (in_w - 2 * D) % gt == 0
    off_a, off_b = (in_w - 2 * D) // gt, (in_w - D) // gt
    dpa, dpb, dga, dgb = _matmul(
        "d_gates", "nt", dx1c, w["w_out"], tm=512, tn=gt, tk=D,
        extras=[(proj, _mn(512, gt, off_a)), (proj, _mn(512, gt, off_b)), (pa, _mn(512, gt)), (pb, _mn(512, gt))],
        outs=[(_sds((S, D), CDT), _mn(512, gt))] * 4, epilogue=_ep_gates)
    (gw_out,) = _matmul("dw_out", "tn", y, dx1c, tm=D, tn=D, tk=1024,
                        outs=[(_sds((D, D), F32), _mn(D, D))], epilogue=_ep_store)
    (gw_pa,) = _matmul("dw_proj_a", "tn", o_a, dpa, tm=GW, tn=D, tk=1024,
                       outs=[(_sds((GW, D), F32), _mn(GW, D))], epilogue=_ep_store)
    (gw_pb,) = _matmul("dw_proj_b", "tn", o_r, dpb, tm=1024, tn=D, tk=1024,
                       outs=[(_sds((ret_v_w, D), F32), _mn(1024, D))], epilogue=_ep_store)
    (do_a,) = _matmul("d_o_a", "nt", dpa, w["w_proj_a"], tm=1024, tn=GW, tk=D,
                      outs=[(_sds((S, GW), F32), _mn(1024, GW))], epilogue=_ep_store)
    tok = on_grads({"w_out": gw_out, "w_proj_a": gw_pa, "w_proj_b": gw_pb})
    (d_or,) = _matmul("d_o_r", "nt", dpb, w["w_proj_b"], tm=512, tn=ret_v_w, tk=D,
                      outs=[(_sds((S, ret_v_w), F32), _mn(512, ret_v_w))], epilogue=_ep_store, deps=[tok])

    dproj, dgn_g, dgn_b = _ret_bwd(proj, gn_g, gn_b, o_pre, states, d_or, dga, dgb, dk, dv)
    do_gs, c_gs = _mix_bwd(S, os_, ls_, do_a)
    datt_parts = [_att_bwd(g, S, qkv[g], ls_[g], do_gs[g], c_gs[g]) for g in range(3)]
    dproj, dgqk = _qknorm_bwd(proj, gqk, [p[0] for p in datt_parts], [p[1] for p in datt_parts],
                              [p[2] for p in datt_parts], dproj)

    (gw_in,) = _matmul(
        "dw_in", "tn", xn, dproj, tm=512, tn=ns_in, tk=1024,
        outs=[(_sds((N_CHIPS, D, ns_in), F32), pl.BlockSpec((None, 512, ns_in), lambda i, j, k: (j, i, 0)))],
        epilogue=_ep_store)
    tok = on_grads({"w_in": gw_in})
    grad_x, _, dg1 = _matmul(
        "d_x", "nt", dproj, w_in, tm=512, tn=D, tk=ns_in, n_cols=D,
        b_spec=pl.BlockSpec((None, D, ns_in), lambda i, j, k: (k, 0, 0)),
        extras=[(x, _mn(512, D)), (g1, _row(D)), (dx1, _mn(512, D))],
        outs=[(_sds((S, D), F32), _mn(512, D)), (_sds((S, D), CDT), _mn(512, D)), (_sds((1, D), F32), _row(D))],
        epilogue=_ep_rms_bwd, deps=[tok])

    smallg = {"norm1_g": dg1, "q_norm_g": dgqk[:, :ATT_W], "k_norm_g": dgqk[:, ATT_W:],
              "ret_gn_g": dgn_g, "ret_gn_b": dgn_b, "norm2_g": dg2}
    return loss, grad_x, smallg


N_CHIPS = 4
N_DEV = 8


def _place():
    x, y, c = lax.axis_index("x"), lax.axis_index("y"), lax.axis_index("c")
    return x, y, c


def _other_chips(x, y):
    out = []
    for fx, fy in ((1, 0), (0, 1), (1, 1)):
        px = 1 - x if fx else x
        py = 1 - y if fy else y
        out.append(((px, py), 2 * px + py))
    return out


SEM_SPEC = pl.BlockSpec(memory_space=pltpu.SEMAPHORE)
ANY_SPEC = pl.BlockSpec(memory_space=pl.ANY)
EFFECT = pltpu.SideEffectType.DATAFLOW_SIDE_EFFECTING


def _ici_copies(kind, srcs, lands, send, recv):
    x, y, c = _place()
    me = 2 * x + y
    out = []
    for w, (s, l) in enumerate(zip(srcs, lands)):
        for j, ((px, py), pidx) in enumerate(_other_chips(x, y)):
            if kind == "gather":
                half = s.shape[0] // 2
                rows = pl.ds(c * half, half)
                src, dst_there, dst_here = s.at[rows, :], l.at[me, rows, :], l.at[pidx, rows, :]
            else:
                src, dst_there, dst_here = s.at[pidx], l.at[me], l.at[pidx]
            out.append((src, dst_there, dst_here, send.at[3 * w + j], recv.at[3 * w + j], (px, py, c)))
    return out


def _exchange_start(name, kind, srcs, land_shapes):
    n = len(srcs)

    def body(*refs):
        src_refs, land_refs = refs[:n], refs[n:2 * n]
        send, recv = refs[2 * n], refs[2 * n + 1]
        token = refs[-1]
        for src, dst, _, ss, rs, dev in _ici_copies(kind, src_refs, land_refs, send, recv):
            pltpu.make_async_remote_copy(src_ref=src, dst_ref=dst, send_sem=ss, recv_sem=rs, device_id=dev,
                                         device_id_type=MESH).start()
        token[...] = jnp.zeros_like(token)

    thru = [pltpu.HBM(s.shape, s.dtype) for s in srcs] + [pltpu.HBM(shape, dtype) for shape, dtype in land_shapes]
    res = pl.pallas_call(
        body, name=name,
        out_shape=(pltpu.SemaphoreType.DMA((3 * n,)), pltpu.SemaphoreType.DMA((3 * n,)), *thru, _sds((8, LANES), F32)),
        in_specs=[HBM_SPEC] * (2 * n), out_specs=(SEM_SPEC, SEM_SPEC, *[HBM_SPEC] * (2 * n), VMEM_SPEC),
        input_output_aliases={i: 2 + i for i in range(2 * n)},
        compiler_params=pltpu.CompilerParams(has_side_effects=EFFECT),
    )(*[pltpu.with_memory_space_constraint(s, pltpu.HBM) for s in srcs],
      *[pltpu.with_memory_space_constraint(lax.empty(shape, dtype), pltpu.HBM) for shape, dtype in land_shapes])
    return res[0], res[1], list(res[2:2 + n]), list(res[2 + n:2 + 2 * n]), res[-1]


def _exchange_wait(name, kind, send, recv, srcs, lands, after):
    n = len(srcs)

    def body(*refs):
        src_refs, land_refs = refs[:n], refs[n:2 * n]
        send_ref, recv_ref = refs[2 * n], refs[2 * n + 1]
        for src, _, dst, ss, rs, dev in _ici_copies(kind, src_refs, land_refs, send_ref, recv_ref):
            cp = pltpu.make_async_remote_copy(src_ref=src, dst_ref=dst, send_sem=ss, recv_sem=rs, device_id=dev,
                                              device_id_type=MESH)
            cp.wait_send()
            cp.wait_recv()

    thru = [pltpu.HBM(t.shape, t.dtype) for t in list(srcs) + list(lands)]
    res = pl.pallas_call(
        body, name=name, out_shape=thru,
        in_specs=[HBM_SPEC] * (2 * n) + [SEM_SPEC, SEM_SPEC, ANY_SPEC], out_specs=[HBM_SPEC] * (2 * n),
        input_output_aliases={i: i for i in range(2 * n)},
        compiler_params=pltpu.CompilerParams(has_side_effects=EFFECT),
    )(*srcs, *lands, send, recv, after)
    return list(res[:n]), list(res[n:])


PAIR_TILE_ELEMS = 1 << 19


def _pair_fill(name, gathered, mine, core, others, chip):
    k, r, C = gathered.shape
    half = r // 2
    tr = _row_tile(half, C, PAIR_TILE_ELEMS, mult=16)
    nt = half // tr
    n_far = N_CHIPS - 1

    def body(c_ref, o_ref, chip_ref, in_ref, mine_ref, out_ref, slot, send, recv):
        j = pl.program_id(0)
        b = (j * nt + pl.program_id(1)) % 2
        x, y, c = _place()
        cp = pltpu.make_async_remote_copy(src_ref=in_ref, dst_ref=slot.at[b], send_sem=send.at[b],
                                          recv_sem=recv.at[b], device_id=(x, y, 1 - c), device_id_type=MESH)

        @pl.when(j < n_far)
        def _():
            cp.start()
            cp.wait_recv()
            out_ref[...] = slot[b]
            cp.wait_send()

        @pl.when(j >= n_far)
        def _():
            out_ref[...] = mine_ref[...]

    def far(j):
        return jnp.minimum(j, n_far - 1)

    grid_spec = pltpu.PrefetchScalarGridSpec(
        num_scalar_prefetch=3, grid=(n_far + 2, nt),
        in_specs=[pl.BlockSpec((tr, C), lambda j, i, c, o, m: (
                      (2 * o[far(j)] + c[0]) * nt + jnp.where(j < n_far, i, nt - 1), 0)),
                  pl.BlockSpec((tr, C), lambda j, i, c, o, m: (jnp.where(j < n_far, 0, (j - n_far) * nt + i), 0))],
        out_specs=pl.BlockSpec((tr, C), lambda j, i, c, o, m: (
            jnp.where(j < n_far, 2 * o[far(j)] + 1 - c[0], 2 * m[0] + j - n_far) * nt + i, 0)),
        scratch_shapes=[pltpu.VMEM((2, tr, C), gathered.dtype), pltpu.SemaphoreType.DMA((2,)),
                        pltpu.SemaphoreType.DMA((2,))])
    out = pl.pallas_call(body, name=name, grid_spec=grid_spec, out_shape=_sds((k * r, C), gathered.dtype),
                         input_output_aliases={3: 0}, compiler_params=_params(2))(
                             core, others, chip, gathered.reshape(k * r, C), mine)
    return out.reshape(k, r, C)


def _pair_reduce(name, g, core):
    k, R, C = g.shape
    half = R // 2
    tr = _row_tile(half, C, PAIR_TILE_ELEMS, mult=16)
    nt = half // tr

    def body(c_ref, mine_ref, give_ref, out_ref, wire_ref, stage, slot, send, recv):
        b = (pl.program_id(0) * nt + pl.program_id(1)) % 2
        x, y, c = _place()
        stage[b] = give_ref[...].astype(stage.dtype)
        cp = pltpu.make_async_remote_copy(src_ref=stage.at[b], dst_ref=slot.at[b], send_sem=send.at[b],
                                          recv_sem=recv.at[b], device_id=(x, y, 1 - c), device_id_type=MESH)
        cp.start()
        cp.wait_recv()
        tot = mine_ref[...] + slot[b].astype(F32)
        out_ref[...] = tot
        wire_ref[...] = tot.astype(wire_ref.dtype)
        cp.wait_send()

    blk = (tr, C)
    out_spec = pl.BlockSpec(blk, lambda s, i, c: (s * nt + i, 0))
    grid_spec = pltpu.PrefetchScalarGridSpec(
        num_scalar_prefetch=1, grid=(k, nt),
        in_specs=[pl.BlockSpec(blk, lambda s, i, c: ((2 * s + c[0]) * nt + i, 0)),
                  pl.BlockSpec(blk, lambda s, i, c: ((2 * s + 1 - c[0]) * nt + i, 0))],
        out_specs=[out_spec, out_spec],
        scratch_shapes=[pltpu.VMEM((2, tr, C), CDT), pltpu.VMEM((2, tr, C), CDT), pltpu.SemaphoreType.DMA((2,)),
                        pltpu.SemaphoreType.DMA((2,))])
    g2 = g.reshape(k * R, C)
    out, wire = pl.pallas_call(body, name=name, grid_spec=grid_spec,
                               out_shape=[_sds((k * half, C), F32), _sds((k * half, C), CDT)],
                               compiler_params=_params(2))(core, g2, g2)
    return out, wire.reshape(k, half, C)


def _all_reduce_small(v):
    r, cdim = v.shape

    def body(v_ref, o_ref, buf, send, recv):
        x, y, c = _place()
        me = 4 * x + 2 * y + c
        buf[me] = v_ref[...]
        sends = []
        for m in range(1, N_DEV):
            px = 1 - x if m & 4 else x
            py = 1 - y if m & 2 else y
            pc = 1 - c if m & 1 else c
            cp = pltpu.make_async_remote_copy(src_ref=v_ref, dst_ref=buf.at[me], send_sem=send.at[m - 1],
                                              recv_sem=recv.at[m - 1], device_id=(px, py, pc), device_id_type=MESH)
            cp.start()
            sends.append((cp, 4 * px + 2 * py + pc))
        for m, (cp, pidx) in enumerate(sends):
            pltpu.make_async_remote_copy(src_ref=v_ref, dst_ref=buf.at[pidx], send_sem=send.at[m], recv_sem=recv.at[m],
                                         device_id=(x, y, c), device_id_type=MESH).wait_recv()
        for cp, _ in sends:
            cp.wait_send()
        tot = buf[0]
        for k in range(1, N_DEV):
            tot = tot + buf[k]
        o_ref[...] = tot

    return pl.pallas_call(
        body, name="all_reduce_small", in_specs=[VMEM_SPEC], out_specs=VMEM_SPEC,
        out_shape=_sds((r, cdim), F32),
        scratch_shapes=[pltpu.VMEM((N_DEV, r, cdim), F32), pltpu.SemaphoreType.DMA((N_DEV - 1,)),
                        pltpu.SemaphoreType.DMA((N_DEV - 1,))],
    )(v)


def _row_tile(rows, cols, budget_elems=1 << 18, mult=8):
    if rows % mult:
        return rows
    t = max(mult, (budget_elems // cols) // mult * mult)
    while rows % t:
        t -= mult
    return t


def _adamw_update(w, g, m, v):
    nm = ADAM_B1 * m + (1.0 - ADAM_B1) * g
    nv = ADAM_B2 * v + (1.0 - ADAM_B2) * (g * g)
    m_hat = nm / (1.0 - ADAM_B1 ** ADAM_STEP)
    v_hat = nv / (1.0 - ADAM_B2 ** ADAM_STEP)
    return -ADAM_LR * (m_hat / (jnp.sqrt(v_hat) + ADAM_EPS) + ADAM_WD * w), nm, nv


def _adamw(name, w, g, m, v):
    R, C = w.shape
    tr = _row_tile(R, C, 1 << 17)

    def body(w_ref, g_ref, m_ref, v_ref, d_ref, nm_ref, nv_ref):
        d_ref[...], nm_ref[...], nv_ref[...] = _adamw_update(w_ref[...], g_ref[...], m_ref[...], v_ref[...])

    spec = pl.BlockSpec((tr, C), lambda i: (i, 0))
    return pl.pallas_call(body, name=name, grid=(R // tr,), in_specs=[spec] * 4, out_specs=[spec] * 3,
                          out_shape=[_sds((R, C), F32)] * 3, compiler_params=_params(1))(w, g, m, v)


def _finish_shard(name, own, by_chip, w, m, v, chip, others, core):
    k, half, C = by_chip.shape
    tr = _row_tile(half, C, 1 << 17, mult=16)
    nt = half // tr

    def body(chip_ref, oth_ref, c_ref, own_ref, a_ref, b_ref, cc_ref, w_ref, m_ref, v_ref,
             g_out, d_out, nm_out, nv_out, mine, slot, send, recv):
        p = pl.program_id(1)
        b = pl.program_id(0) % 2
        x, y, c = _place()
        cp = pltpu.make_async_remote_copy(src_ref=mine.at[b], dst_ref=slot.at[b], send_sem=send.at[b],
                                          recv_sem=recv.at[b], device_id=(x, y, 1 - c), device_id_type=MESH)

        def update(g):
            g_out[...] = g
            d_out[...], nm_out[...], nv_out[...] = _adamw_update(w_ref[...], g, m_ref[...], v_ref[...])

        @pl.when(p == 0)
        def _():
            tot = ((own_ref[...] + a_ref[...].astype(F32)) + b_ref[...].astype(F32)) + cc_ref[...].astype(F32)
            mine[b] = tot
            cp.start()
            update(tot)

        @pl.when(p == 1)
        def _():
            cp.wait_recv()
            update(slot[b])
            cp.wait_send()

    def piece(j):
        return pl.BlockSpec((tr, C), lambda i, p, chip, oth, c: (oth[j] * nt + i, 0))

    def rows():
        return pl.BlockSpec((tr, C), lambda i, p, chip, oth, c: (jnp.where(p == 0, c[0], 1 - c[0]) * nt + i, 0))

    grid_spec = pltpu.PrefetchScalarGridSpec(
        num_scalar_prefetch=3, grid=(nt, 2),
        in_specs=[pl.BlockSpec((tr, C), lambda i, p, chip, oth, c: (chip[0] * nt + i, 0)),
                  piece(0), piece(1), piece(2), rows(), rows(), rows()],
        out_specs=[rows(), rows(), rows(), rows()],
        scratch_shapes=[pltpu.VMEM((2, tr, C), F32), pltpu.VMEM((2, tr, C), F32), pltpu.SemaphoreType.DMA((2,)),
                        pltpu.SemaphoreType.DMA((2,))])
    by2 = by_chip.reshape(k * half, C)
    return pl.pallas_call(body, name=name, grid_spec=grid_spec, out_shape=[_sds((2 * half, C), F32)] * 4,
                          compiler_params=_params(2))(chip, others, core, own, by2, by2, by2, w, m, v)


BIG = ("w_in", "w_proj_a", "w_proj_b", "w_out", "w_up", "w_down")
COL_SHARDED = ("w_in", "w_proj_a", "w_up")
SMALL = ("norm1_g", "q_norm_g", "k_norm_g", "ret_gn_g", "ret_gn_b", "norm2_g")
ALL_W = ("norm1_g", "w_in", "q_norm_g", "k_norm_g", "ret_gn_g", "ret_gn_b", "w_proj_a", "w_proj_b", "w_out",
         "norm2_g", "w_up", "w_down")
LANES = 128


def _to_full(name, gathered):
    k, r, c = gathered.shape
    if name in COL_SHARDED:
        return gathered.transpose(1, 0, 2).reshape(r, k * c)
    return gathered.reshape(k * r, c)


def _to_shard_major(name, full):
    if name in COL_SHARDED:
        r, c4 = full.shape
        return full.reshape(r, N_CHIPS, c4 // N_CHIPS).transpose(1, 0, 2)
    r4, c = full.shape
    return full.reshape(N_CHIPS, r4 // N_CHIPS, c)


def kernel(x, norm1_g, w_in, q_norm_g, k_norm_g, ret_gn_g, ret_gn_b, w_proj_a, w_proj_b, w_out, norm2_g, w_up, w_down, loss_target, m_norm1_g, m_w_in, m_q_norm_g, m_k_norm_g, m_ret_gn_g, m_ret_gn_b, m_w_proj_a, m_w_proj_b, m_w_out, m_norm2_g, m_w_up, m_w_down, v_norm1_g, v_w_in, v_q_norm_g, v_k_norm_g, v_ret_gn_g, v_ret_gn_b, v_w_proj_a, v_w_proj_b, v_w_out, v_norm2_g, v_w_up, v_w_down):
    weights = dict(norm1_g=norm1_g, w_in=w_in, q_norm_g=q_norm_g, k_norm_g=k_norm_g, ret_gn_g=ret_gn_g,
                   ret_gn_b=ret_gn_b, w_proj_a=w_proj_a, w_proj_b=w_proj_b, w_out=w_out, norm2_g=norm2_g,
                   w_up=w_up, w_down=w_down)
    moments_m = dict(norm1_g=m_norm1_g, w_in=m_w_in, q_norm_g=m_q_norm_g, k_norm_g=m_k_norm_g, ret_gn_g=m_ret_gn_g,
                     ret_gn_b=m_ret_gn_b, w_proj_a=m_w_proj_a, w_proj_b=m_w_proj_b, w_out=m_w_out,
                     norm2_g=m_norm2_g, w_up=m_w_up, w_down=m_w_down)
    moments_v = dict(norm1_g=v_norm1_g, w_in=v_w_in, q_norm_g=v_q_norm_g, k_norm_g=v_k_norm_g, ret_gn_g=v_ret_gn_g,
                     ret_gn_b=v_ret_gn_b, w_proj_a=v_w_proj_a, w_proj_b=v_w_proj_b, w_out=v_w_out,
                     norm2_g=v_norm2_g, w_up=v_w_up, w_down=v_w_down)

    mx, my = lax.axis_index("x"), lax.axis_index("y")
    core = lax.axis_index("c").astype(jnp.int32).reshape(1)
    chip = (2 * mx + my).astype(jnp.int32).reshape(1)
    others = jnp.stack([2 * (1 - mx) + my, 2 * mx + 1 - my, 2 * (1 - mx) + 1 - my]).astype(jnp.int32)
    shards = {n: weights[n][0].astype(CDT) for n in BIG}
    def start_gather(name, names):
        return _exchange_start(name, "gather", [shards[n] for n in names],
                               [((N_CHIPS,) + shards[n].shape, CDT) for n in names])

    i_send, i_recv, i_srcs, i_lands, i_token = start_gather("gather_w_in_start", ["w_in"])
    late = [n for n in BIG if n != "w_in"]
    l_send, l_recv, l_srcs, l_lands, l_token = start_gather("gather_late_start", late)

    def far_w_in(after):
        srcs, lands = _exchange_wait("gather_w_in_wait", "gather", i_send, i_recv, i_srcs, i_lands, after)
        return _pair_fill("pair_fill_w_in", lands[0], srcs[0], core, others, chip)

    def late_weights(after):
        srcs, lands = _exchange_wait("gather_late_wait", "gather", l_send, l_recv, l_srcs, l_lands, after)
        out = {}
        for n, mine, land in zip(late, srcs, lands):
            out[n] = _to_full(n, _pair_fill("pair_fill_%s" % n, land, mine, core, others, chip))
        return out

    pending = []

    def on_grads(group):
        names = list(group)
        red = [_pair_reduce("pair_reduce_%s" % n, g if g.ndim == 3 else _to_shard_major(n, g), core)
               for n, g in group.items()]
        wires = [wire for _, wire in red]
        send, recv, srcs, lands, token = _exchange_start(
            "scatter_start_%s" % names[0], "scatter", wires, [(wire.shape, wire.dtype) for wire in wires])
        pending.append((names, [own for own, _ in red], send, recv, srcs, lands))
        return token

    small = {n: weights[n].reshape(1, -1) for n in SMALL}

    loss, grad_x, small_g = _local_step(x[0], loss_target[0], i_srcs[0], chip, others, far_w_in, small,
                                        late_weights, on_grads, deps0=[i_token, l_token])
    loss = lax.psum(loss, ("x", "y", "c"))

    out_g, out_d, out_m, out_v = {}, {}, {}, {}
    for names, owns, send, recv, srcs, lands in pending:
        _, got = _exchange_wait("scatter_wait_%s" % names[0], "scatter", send, recv, srcs, lands, grad_x)
        for n, own, by_chip in zip(names, owns, got):
            shape = weights[n].shape
            res = _finish_shard("finish_%s" % n, own, by_chip, weights[n][0], moments_m[n][0], moments_v[n][0],
                                chip, others, core)
            out_g[n], out_d[n], out_m[n], out_v[n] = (t.reshape(shape) for t in res)

    packed = jnp.concatenate([small_g[n].reshape(1, -1) for n in SMALL], axis=1)
    red = _all_reduce_small(packed.reshape(-1, LANES)).reshape(1, -1)
    off = 0
    for n in SMALL:
        shape = weights[n].shape
        row = (1, weights[n].size)
        g2 = red[:, off:off + row[1]]
        off += row[1]
        d, nm, nv = _adamw("adamw_%s" % n, weights[n].reshape(row), g2, moments_m[n].reshape(row),
                           moments_v[n].reshape(row))
        out_g[n], out_d[n], out_m[n], out_v[n] = (t.reshape(shape) for t in (g2, d, nm, nv))

    return (loss, grad_x[None], *[out_g[n] for n in ALL_W], *[out_d[n] for n in ALL_W],
            *[out_m[n] for n in ALL_W], *[out_v[n] for n in ALL_W])
```

```python
import functools
import math

import jax
import jax.numpy as jnp
from jax import lax
from jax.experimental import pallas as pl
from jax.experimental.pallas import tpu as pltpu

CDT = jnp.bfloat16
F32 = jnp.float32
EPS = 1e-6

ATT_GROUPS = ((128, 1), (512, 4), (2048, 16))
ATT_HPG = 4
ATT_HEADS = 12
HD = 128
BLK = 128
ATT_W = ATT_HEADS * HD
GW = ATT_HPG * HD
RET_HEADS = 4

ADAM_LR = 0.001
ADAM_B1 = 0.9
ADAM_B2 = 0.999
ADAM_EPS = 1e-08
ADAM_WD = 0.01
ADAM_STEP = 10

VMEM_LIMIT_BYTES = 56 * 1024 * 1024
MXU_DIM = 256
MESH = pl.DeviceIdType.MESH
HBM_SPEC = pl.BlockSpec(memory_space=pltpu.HBM)
VMEM_SPEC = pl.BlockSpec(memory_space=pltpu.VMEM)


def _params(n_axes):
    return pltpu.CompilerParams(dimension_semantics=("arbitrary",) * n_axes,
                                vmem_limit_bytes=VMEM_LIMIT_BYTES)


def _dot_nn(a, b):
    return jnp.dot(a, b, preferred_element_type=F32)


def _dot_nt(a, b):
    return lax.dot_general(a, b, (((1,), (1,)), ((), ())), preferred_element_type=F32)


def _dot_tn(a, b):
    return lax.dot_general(a, b, (((0,), (0,)), ((), ())), preferred_element_type=F32)


def _sigmoid(v):
    return 1.0 / (1.0 + jnp.exp(-v))


def _matmul(name, mode, a, b, *, tm, tn, tk, extras=(), outs, epilogue, deps=(), b_spec=None, n_cols=None,
            prefetch=(), alias_dep_to_out=None, j_outer=False):
    deps = [d for d in deps if d is not None]
    if mode == "tn":
        K, M = a.shape
    else:
        M, K = a.shape
    if b_spec is None:
        (N, K2) = b.shape if mode == "nt" else b.shape[::-1]
        assert K == K2, (name, a.shape, b.shape)
        if mode == "nt":
            b_spec = pl.BlockSpec((tn, tk), lambda i, j, k, *p: (j, k))
        else:
            b_spec = pl.BlockSpec((tk, tn), lambda i, j, k, *p: (k, j))
    else:
        N = n_cols
    assert M % tm == 0 and N % tn == 0 and K % tk == 0, (name, a.shape, b.shape)
    ni, nj, nk = M // tm, N // tn, K // tk
    if mode == "tn":
        a_spec = pl.BlockSpec((tk, tm), lambda i, j, k, *p: (k, i))
    else:
        a_spec = pl.BlockSpec((tm, tk), lambda i, j, k, *p: (i, k))
    dot = {"nn": _dot_nn, "nt": _dot_nt, "tn": _dot_tn}[mode]
    n_ex, n_out, n_dep, n_pre = len(extras), len(outs), len(deps), len(prefetch)
    grid = (ni, nj, nk)
    if j_outer:
        grid = (nj, ni, nk)

        def swapped(spec):
            return pl.BlockSpec(spec.block_shape, lambda j, i, k, *p: spec.index_map(i, j, k, *p))

        a_spec, b_spec = swapped(a_spec), swapped(b_spec)
        extras = [(e, swapped(s)) for e, s in extras]
        outs = [(o, swapped(s)) for o, s in outs]

    def body(*refs):
        refs = refs[n_pre:]
        a_ref, b_ref = refs[0], refs[1]
        ex = refs[2:2 + n_ex]
        out = refs[2 + n_ex + n_dep:2 + n_ex + n_dep + n_out]
        acc = refs[-1] if nk > 1 else None
        i = pl.program_id(1 if j_outer else 0)
        k = pl.program_id(2)
        if nk == 1:
            epilogue(dot(a_ref[...].astype(CDT), b_ref[...].astype(CDT)), ex, out, i)
            return

        @pl.when(k == 0)
        def _():
            acc[...] = jnp.zeros_like(acc)

        acc[...] += dot(a_ref[...].astype(CDT), b_ref[...].astype(CDT))

        @pl.when(k == nk - 1)
        def _():
            epilogue(acc[...], ex, out, i)

    grid_spec = pltpu.PrefetchScalarGridSpec(
        num_scalar_prefetch=n_pre, grid=grid,
        in_specs=[a_spec, b_spec] + [s for _, s in extras] + [pl.BlockSpec(memory_space=pl.ANY)] * n_dep,
        out_specs=[s for _, s in outs],
        scratch_shapes=[pltpu.VMEM((tm, tn), F32)] if nk > 1 else [])
    aliases = {}
    if alias_dep_to_out is not None:
        aliases = {n_pre + 2 + n_ex + alias_dep_to_out[0]: alias_dep_to_out[1]}
    res = pl.pallas_call(
        body, name=name, grid_spec=grid_spec, out_shape=[o for o, _ in outs], input_output_aliases=aliases,
        compiler_params=_params(3),
    )(*prefetch, a, b, *[e for e, _ in extras], *deps)
    return res


def _mn(tm, tn, col_off=0):
    return pl.BlockSpec((tm, tn), lambda i, j, k, *p: (i, j + col_off))


def _row(tn):
    return pl.BlockSpec((1, tn), lambda i, j, k, *p: (0, j))


def _ep_store(acc, ex, out, i):
    out[0][...] = acc.astype(out[0].dtype)


def _ep_resid(acc, ex, out, i):
    out[0][...] = ex[0][...] + acc


def _ep_up(acc, ex, out, i):
    out[0][...] = acc.astype(out[0].dtype)
    r = jnp.maximum(acc, 0.0)
    out[1][...] = (r * r).astype(out[1].dtype)


def _ep_down_loss(acc, ex, out, i, inv_d):
    diff = (ex[0][...] + acc) - ex[1][...]
    dx2 = diff * inv_d
    out[0][...] = dx2
    out[1][...] = dx2.astype(out[1].dtype)

    @pl.when(i == 0)
    def _():
        out[2][...] = jnp.zeros_like(out[2])

    out[2][...] += jnp.sum(diff * diff, axis=0, keepdims=True)


def _ep_dh(acc, ex, out, i):
    h = ex[0][...].astype(F32)
    out[0][...] = (acc * (2.0 * jnp.maximum(h, 0.0))).astype(out[0].dtype)


def _ep_rms_bwd(acc, ex, out, i):
    x = ex[0][...]
    g = ex[1][...]
    rstd = lax.rsqrt(jnp.mean(x * x, axis=-1, keepdims=True) + EPS)
    xh = x * rstd
    dxh = acc * g
    dx = ex[2][...] + rstd * (dxh - xh * jnp.mean(dxh * xh, axis=-1, keepdims=True))
    out[0][...] = dx
    out[1][...] = dx.astype(out[1].dtype)

    @pl.when(i == 0)
    def _():
        out[2][...] = jnp.zeros_like(out[2])

    out[2][...] += jnp.sum(acc * xh, axis=0, keepdims=True)


def _ep_gates(acc, ex, out, i):
    sa = _sigmoid(ex[0][...].astype(F32))
    sb = _sigmoid(ex[1][...].astype(F32))
    dpa = acc * sa
    dpb = acc * sb
    out[0][...] = dpa.astype(out[0].dtype)
    out[1][...] = dpb.astype(out[1].dtype)
    out[2][...] = (dpa * ex[2][...].astype(F32) * (1.0 - sa)).astype(out[2].dtype)
    out[3][...] = (dpb * ex[3][...].astype(F32) * (1.0 - sb)).astype(out[3].dtype)


def _sds(shape, dtype):
    return jax.ShapeDtypeStruct(shape, dtype)


def _rms_fwd(name, x, g, tm=512):
    S, D = x.shape

    def body(x_ref, g_ref, o_ref):
        xv = x_ref[...]
        rstd = lax.rsqrt(jnp.mean(xv * xv, axis=-1, keepdims=True) + EPS)
        o_ref[...] = (xv * rstd * g_ref[...]).astype(o_ref.dtype)

    return pl.pallas_call(
        body, name=name, grid=(S // tm,),
        in_specs=[pl.BlockSpec((tm, D), lambda i: (i, 0)), pl.BlockSpec((1, D), lambda i: (0, 0))],
        out_specs=pl.BlockSpec((tm, D), lambda i: (i, 0)),
        out_shape=_sds((S, D), CDT), compiler_params=_params(1))(x, g)


def _rm_shape(S, d, width):
    return (S, width) if d == 1 else (d, S // d, width)


def _rm_spec(tm, d, width):
    if d == 1:
        return pl.BlockSpec((tm, width), lambda i: (i, 0))
    return pl.BlockSpec((d, tm // d, width), lambda i: (0, i, 0))


def _rm_put(dst_ref, cols, buf_ref, d):
    if d == 1:
        dst_ref[:, cols] = buf_ref[...].astype(dst_ref.dtype)
        return
    m = buf_ref.shape[0] // d
    for r in range(d):
        dst_ref[r, :, cols] = buf_ref[pl.ds(r, m, stride=d), :].astype(dst_ref.dtype)


def _rm_reader(buf_ref, src_ref, d):
    if d == 1:
        return lambda s: src_ref[:, s * HD:(s + 1) * HD].astype(F32)
    m = buf_ref.shape[1] // d
    for s in range(buf_ref.shape[0]):
        for r in range(d):
            buf_ref.at[s][pl.ds(r, m, stride=d), :] = src_ref[r, :, s * HD:(s + 1) * HD].astype(F32)
    return lambda s: buf_ref[s]


def _qknorm_fwd(proj, gqk, tm=512):
    S = proj.shape[0]
    W = 2 * ATT_W
    dil = [d for _, d in ATT_GROUPS]

    def body(p_ref, g_ref, o0, o1, o2, buf):
        outs = (o0, o1, o2)
        for hd in range(3 * ATT_HEADS):
            which, head = hd // ATT_HEADS, hd % ATT_HEADS
            grp, slot = head // ATT_HPG, head % ATT_HPG
            v = p_ref[:, hd * HD:(hd + 1) * HD].astype(F32)
            if which < 2:
                rstd = lax.rsqrt(jnp.mean(v * v, axis=-1, keepdims=True) + EPS)
                v = v * rstd * g_ref[:, hd * HD:(hd + 1) * HD]
            buf[...] = v
            _rm_put(outs[grp], slice(which * GW + slot * HD, which * GW + (slot + 1) * HD), buf, dil[grp])

    return pl.pallas_call(
        body, name="qknorm_fwd", grid=(S // tm,),
        in_specs=[pl.BlockSpec((tm, 3 * ATT_W), lambda i: (i, 0)), pl.BlockSpec((1, W), lambda i: (0, 0))],
        out_specs=[_rm_spec(tm, d, 3 * GW) for d in dil],
        out_shape=[_sds(_rm_shape(S, d, 3 * GW), CDT) for d in dil],
        scratch_shapes=[pltpu.VMEM((tm, HD), F32)],
        compiler_params=_params(1))(proj, gqk)


def _qknorm_bwd(proj, gqk, dqs, dks, dvs, dproj, tm=256):
    S = proj.shape[0]
    W = 2 * ATT_W
    dil = [d for _, d in ATT_GROUPS]

    def body(p_ref, g_ref, *refs):
        ins = refs[0:9]
        o_ref, dg_ref = refs[10], refs[11]
        bufs = refs[12:21]
        i = pl.program_id(0)

        @pl.when(i == 0)
        def _():
            dg_ref[...] = jnp.zeros_like(dg_ref)

        nat = [_rm_reader(bufs[j], ins[j], dil[j % 3]) for j in range(9)]
        dq_get, dk_get, dv_get = nat[0:3], nat[3:6], nat[6:9]
        for hd in range(2 * ATT_HEADS):
            sl = slice(hd * HD, (hd + 1) * HD)
            head = hd % ATT_HEADS
            grp, slot = head // ATT_HPG, head % ATT_HPG
            dn = (dq_get if hd < ATT_HEADS else dk_get)[grp](slot)
            v = p_ref[:, sl].astype(F32)
            rstd = lax.rsqrt(jnp.mean(v * v, axis=-1, keepdims=True) + EPS)
            vh = v * rstd
            dg_ref[:, sl] += jnp.sum(dn * vh, axis=0, keepdims=True)
            dvh = dn * g_ref[:, sl]
            o_ref[:, sl] = (rstd * (dvh - vh * jnp.mean(dvh * vh, axis=-1, keepdims=True))).astype(o_ref.dtype)
        for head in range(ATT_HEADS):
            grp, slot = head // ATT_HPG, head % ATT_HPG
            o_ref[:, W + head * HD:W + (head + 1) * HD] = dv_get[grp](slot).astype(o_ref.dtype)

    return pl.pallas_call(
        body, name="qknorm_bwd", grid=(S // tm,),
        in_specs=[pl.BlockSpec((tm, W), lambda i: (i, 0)), pl.BlockSpec((1, W), lambda i: (0, 0))]
        + [_rm_spec(tm, d, GW) for d in dil] * 3 + [pl.BlockSpec(memory_space=pl.ANY)],
        out_specs=[pl.BlockSpec((tm, 3 * ATT_W), lambda i: (i, 0)), pl.BlockSpec((1, W), lambda i: (0, 0))],
        out_shape=[_sds(dproj.shape, dproj.dtype), _sds((1, W), F32)],
        scratch_shapes=[pltpu.VMEM((ATT_HPG, tm, HD), F32)] * 9,
        input_output_aliases={11: 0},
        compiler_params=_params(1))(proj, gqk, *dqs, *dks, *dvs, dproj)


def _att_mask(n):
    qi = lax.broadcasted_iota(jnp.int32, (BLK, 2 * BLK), 0)
    kj = lax.broadcasted_iota(jnp.int32, (BLK, 2 * BLK), 1)
    dist = BLK + qi - kj
    valid = (dist >= 0) & (dist <= BLK) & ((kj >= BLK) | (n > 0))
    return valid, dist.astype(F32)


def _att_slopes(grp):
    return [2.0 ** (-8.0 * (grp * ATT_HPG + hh + 1) / ATT_HEADS) for hh in range(ATT_HPG)]


def _att_spec(d, row_fn, col=0):
    if d == 1:
        return pl.BlockSpec((BLK, GW), lambda r, n: (row_fn(n), col))
    return pl.BlockSpec((None, BLK, GW), lambda r, n: (r, row_fn(n), col))


def _att_qkv_specs(d, nb):
    last = nb - 1

    def cur(n):
        return jnp.minimum(n, last)

    def prev(n):
        return jnp.maximum(jnp.minimum(n, last) - 1, 0)

    return [_att_spec(d, cur, 0), _att_spec(d, prev, 1), _att_spec(d, cur, 1), _att_spec(d, prev, 2),
            _att_spec(d, cur, 2)]


def _att_fwd(grp, S, qkv):
    _, d = ATT_GROUPS[grp]
    L = S // d
    nb = L // BLK
    slopes = _att_slopes(grp)
    scale = HD ** -0.5

    def body(q_ref, kp_ref, kc_ref, vp_ref, vc_ref, o_ref, l_ref, s_buf, p_buf, den_buf):
        n = pl.program_id(1)
        valid, distf = _att_mask(n)
        heads = [slice(hh * HD, (hh + 1) * HD) for hh in range(ATT_HPG)]
        for hh, sl in enumerate(heads):
            k = jnp.concatenate([kp_ref[:, sl], kc_ref[:, sl]], axis=0)
            s_buf[hh] = _dot_nt(q_ref[:, sl], k)
        for hh, sl in enumerate(heads):
            s = s_buf[hh] * scale + (-slopes[hh] * d) * distf
            s = jnp.where(valid, s, -1e30)
            m = jnp.max(s, axis=-1, keepdims=True)
            p = jnp.exp(s - m)
            den = jnp.sum(p, axis=-1, keepdims=True)
            p_buf[hh] = p.astype(CDT)
            den_buf[hh] = jnp.broadcast_to(den, (BLK, HD))
            l_ref[:, sl] = jnp.broadcast_to(m + jnp.log(den), (BLK, HD))
        for hh, sl in enumerate(heads):
            v = jnp.concatenate([vp_ref[:, sl], vc_ref[:, sl]], axis=0)
            o_ref[:, sl] = _dot_nn(p_buf[hh], v) / den_buf[hh]

    out_spec = _att_spec(d, lambda n: n)
    return pl.pallas_call(
        body, name="att_fwd_g%d" % grp, grid=(d, nb),
        in_specs=_att_qkv_specs(d, nb),
        out_specs=[out_spec, out_spec],
        out_shape=[_sds(_rm_shape(S, d, GW), F32)] * 2,
        scratch_shapes=[pltpu.VMEM((ATT_HPG, BLK, 2 * BLK), F32), pltpu.VMEM((ATT_HPG, BLK, 2 * BLK), CDT),
                        pltpu.VMEM((ATT_HPG, BLK, HD), F32)],
        compiler_params=_params(2),
    )(qkv, qkv, qkv, qkv, qkv)


def _att_bwd(grp, S, qkv, lse, do_g, c_g):
    _, d = ATT_GROUPS[grp]
    L = S // d
    nb = L // BLK
    slopes = _att_slopes(grp)
    scale = HD ** -0.5
    last = nb - 1

    def body(q_ref, kp_ref, kc_ref, vp_ref, vc_ref, l_ref, do_ref, c_ref, dq_ref, dk_ref, dv_ref, ck, cv,
             s_buf, dp_buf, p_buf, ds_buf):
        n = pl.program_id(1)

        @pl.when(n == 0)
        def _():
            ck[...] = jnp.zeros_like(ck)
            cv[...] = jnp.zeros_like(cv)

        @pl.when(n < nb)
        def _():
            valid, distf = _att_mask(n)
            heads = [slice(hh * HD, (hh + 1) * HD) for hh in range(ATT_HPG)]
            for hh, sl in enumerate(heads):
                k = jnp.concatenate([kp_ref[:, sl], kc_ref[:, sl]], axis=0)
                v = jnp.concatenate([vp_ref[:, sl], vc_ref[:, sl]], axis=0)
                s_buf[hh] = _dot_nt(q_ref[:, sl], k)
                dp_buf[hh] = _dot_nt(do_ref[:, sl], v)
            for hh, sl in enumerate(heads):
                s = s_buf[hh] * scale + (-slopes[hh] * d) * distf
                p = jnp.where(valid, jnp.exp(s - l_ref[:, sl][:, 0:1]), 0.0)
                p_buf[hh] = p.astype(CDT)
                ds_buf[hh] = (p * (dp_buf[hh] + c_ref[:, sl][:, 0:1]) * scale).astype(CDT)
            for hh, sl in enumerate(heads):
                k = jnp.concatenate([kp_ref[:, sl], kc_ref[:, sl]], axis=0)
                ds = ds_buf[hh]
                dq_ref[:, sl] = _dot_nn(ds, k)
                dk = _dot_tn(ds, q_ref[:, sl])
                dv = _dot_tn(p_buf[hh], do_ref[:, sl])
                dk_ref[:, sl] = ck[:, sl] + dk[0:BLK]
                dv_ref[:, sl] = cv[:, sl] + dv[0:BLK]
                ck[:, sl] = dk[BLK:2 * BLK]
                cv[:, sl] = dv[BLK:2 * BLK]

        @pl.when(n == nb)
        def _():
            dk_ref[...] = ck[...]
            dv_ref[...] = cv[...]

    blk = (BLK, GW)
    at_q = _att_spec(d, lambda n: jnp.minimum(n, last))
    behind = _att_spec(d, lambda n: jnp.maximum(n - 1, 0))
    return pl.pallas_call(
        body, name="att_bwd_g%d" % grp, grid=(d, nb + 1),
        in_specs=_att_qkv_specs(d, nb) + [at_q, at_q, at_q],
        out_specs=[at_q, behind, behind],
        out_shape=[_sds(_rm_shape(S, d, GW), F32)] * 3,
        scratch_shapes=[pltpu.VMEM(blk, F32), pltpu.VMEM(blk, F32),
                        pltpu.VMEM((ATT_HPG, BLK, 2 * BLK), F32), pltpu.VMEM((ATT_HPG, BLK, 2 * BLK), F32),
                        pltpu.VMEM((ATT_HPG, BLK, 2 * BLK), CDT), pltpu.VMEM((ATT_HPG, BLK, 2 * BLK), CDT)],
        compiler_params=_params(2),
    )(qkv, qkv, qkv, qkv, qkv, lse, do_g, c_g)


def _mix_alpha(l0, l1, l2):
    mx = jnp.maximum(jnp.maximum(l0, l1), l2)
    e = [jnp.exp(l0 - mx), jnp.exp(l1 - mx), jnp.exp(l2 - mx)]
    tot = e[0] + e[1] + e[2]
    return [ei / tot for ei in e]


def _mix_fwd(S, os_, ls_, tm=512):
    dil = [d for _, d in ATT_GROUPS]

    def body(*refs):
        out, bufs = refs[6], refs[7:13]
        get = [_rm_reader(bufs[j], refs[j], dil[j % 3]) for j in range(6)]
        for s in range(ATT_HPG):
            al = _mix_alpha(*[get[3 + g](s) for g in range(3)])
            mixed = al[0] * get[0](s) + al[1] * get[1](s) + al[2] * get[2](s)
            out[:, s * HD:(s + 1) * HD] = mixed.astype(out.dtype)

    specs = [_rm_spec(tm, d, GW) for d in dil]
    return pl.pallas_call(
        body, name="mix_fwd", grid=(S // tm,), in_specs=specs * 2, out_specs=pl.BlockSpec((tm, GW), lambda i: (i, 0)),
        out_shape=_sds((S, GW), CDT), scratch_shapes=[pltpu.VMEM((ATT_HPG, tm, HD), F32)] * 6,
        compiler_params=_params(1))(*os_, *ls_)


def _mix_bwd(S, os_, ls_, do_a, tm=512):
    dil = [d for _, d in ATT_GROUPS]

    def body(*refs):
        d_ref, outs, bufs, tmp = refs[6], refs[7:13], refs[13:19], refs[19]
        get = [_rm_reader(bufs[j], refs[j], dil[j % 3]) for j in range(6)]
        for s in range(ATT_HPG):
            cols = slice(s * HD, (s + 1) * HD)
            al = _mix_alpha(*[get[3 + g](s) for g in range(3)])
            dv = d_ref[:, cols]
            o_a = al[0] * get[0](s) + al[1] * get[1](s) + al[2] * get[2](s)
            dsum = jnp.sum(dv * o_a, axis=-1, keepdims=True)
            for g in range(3):
                tmp[...] = al[g] * dv
                _rm_put(outs[g], cols, tmp, dil[g])
                tmp[...] = -(al[g] * dsum)
                _rm_put(outs[3 + g], cols, tmp, dil[g])

    specs = [_rm_spec(tm, d, GW) for d in dil]
    res = pl.pallas_call(
        body, name="mix_bwd", grid=(S // tm,), in_specs=specs * 2 + [pl.BlockSpec((tm, GW), lambda i: (i, 0))],
        out_specs=specs * 2,
        out_shape=[_sds(_rm_shape(S, d, GW), CDT) for d in dil] + [_sds(_rm_shape(S, d, GW), F32) for d in dil],
        scratch_shapes=[pltpu.VMEM((ATT_HPG, tm, HD), F32)] * 6 + [pltpu.VMEM((tm, HD), F32)],
        compiler_params=_params(1))(*os_, *ls_, do_a)
    return res[:3], res[3:]


def _ret_tables(dk):
    H, C = RET_HEADS, BLK
    log_g = jnp.log(1.0 - 2.0 ** (-5.0 - jnp.arange(H, dtype=F32)))
    idx = jnp.arange(C, dtype=F32)
    diff = idx[:, None] - idx[None, :]
    decay = jnp.where(diff >= 0, jnp.exp(log_g[:, None, None] * jnp.maximum(diff, 0.0)), 0.0)
    xi = jnp.exp(log_g[:, None] * (idx[None, :] + 1.0))
    zeta = jnp.exp(log_g[:, None] * (C - 1.0 - idx[None, :])) * (dk ** -0.5)
    g_chunk = jnp.exp(log_g * C)
    bc = lambda t: jnp.broadcast_to(t[:, :, None], (H, C, C))
    return decay, bc(xi), bc(zeta), jnp.broadcast_to(g_chunk[:, None, None], (H, 8, C))


def _gn_fwd(o, g, b):
    mu = jnp.mean(o, axis=-1, keepdims=True)
    xc = o - mu
    rstd = lax.rsqrt(jnp.mean(xc * xc, axis=-1, keepdims=True) + EPS)
    yh = xc * rstd
    return yh, rstd, yh * g + b


def _ret_specs(dk, dv, order):
    H = RET_HEADS
    qk_w, v_w = H * dk, H * dv
    off_q = 3 * ATT_W
    off_k, off_v, off_g = off_q + qk_w, off_q + 2 * qk_w, off_q + 2 * qk_w + v_w
    assert 2 * dk == dv and all(off % dv == 0 for off in (off_q, off_k, off_v, off_g))

    def col(off, j):
        return pl.BlockSpec((BLK, dv), lambda i: (order(i), off // dv + j))

    tab = pl.BlockSpec((H, BLK, BLK), lambda i: (0, 0, 0))
    return ([col(off_q, j) for j in range(H // 2)] + [col(off_k, j) for j in range(H // 2)]
            + [col(off_v, j) for j in range(H)] + [col(off_g, j) for j in range(H)]
            + [tab, tab, tab, pl.BlockSpec((H, 8, BLK), lambda i: (0, 0, 0))])


def _ret_heads(refs, dk):
    H = RET_HEADS
    q_refs, k_refs = refs[0:H // 2], refs[H // 2:H]
    v_refs, gr_refs = refs[H:2 * H], refs[2 * H:3 * H]

    def head(h):
        cols = slice((h % 2) * dk, (h % 2 + 1) * dk)
        return q_refs[h // 2][:, cols], k_refs[h // 2][:, cols], v_refs[h][...], gr_refs[h][...]

    return head, refs[3 * H:3 * H + 4]


def _ret_fwd(proj, gn_g, gn_b, dk, dv):
    S = proj.shape[0]
    N = S // BLK
    H = RET_HEADS
    kscale = dk ** -0.5
    n_in = 3 * H + 4

    def body(*refs):
        head, (dec_ref, xi_ref, zeta_ref, gc_ref) = _ret_heads(refs, dk)
        g_ref, b_ref, opre_ref, or_ref, st_ref, state, s_buf, cross_buf = refs[n_in:n_in + 8]
        n = pl.program_id(0)

        @pl.when(n == 0)
        def _():
            state[...] = jnp.zeros_like(state)

        for h in range(H):
            q, k, v, _ = head(h)
            s_buf[h] = _dot_nt(q, k)
            st = state[h]
            st_c = st.astype(CDT)
            st_ref[h] = st_c
            cross_buf[h] = _dot_nn(q, st_c)
            kz = (k.astype(F32) * zeta_ref[h][:, 0:1]).astype(CDT)
            state[h] = st * gc_ref[h][0:1, 0:1] + _dot_tn(kz, v)
        for h in range(H):
            vs = slice(h * dv, (h + 1) * dv)
            _, _, v, gr = head(h)
            s = s_buf[h] * kscale * dec_ref[h]
            o = _dot_nn(s.astype(CDT), v) + cross_buf[h] * xi_ref[h][:, 0:1]
            opre_ref[:, vs] = o
            _, _, y = _gn_fwd(o, g_ref[:, vs], b_ref[:, vs])
            gr = gr.astype(F32)
            or_ref[:, vs] = (y * (gr * _sigmoid(gr))).astype(or_ref.dtype)

    v_w = H * dv
    row = pl.BlockSpec((1, v_w), lambda i: (0, 0))
    tile = pl.BlockSpec((BLK, v_w), lambda i: (i, 0))
    return pl.pallas_call(
        body, name="ret_fwd", grid=(N,),
        in_specs=_ret_specs(dk, dv, lambda i: i) + [row, row],
        out_specs=[tile, tile, pl.BlockSpec((None, H, dk, dv), lambda i: (i, 0, 0, 0))],
        out_shape=[_sds((S, v_w), F32), _sds((S, v_w), CDT), _sds((N, H, dk, dv), CDT)],
        scratch_shapes=[pltpu.VMEM((H, dk, dv), F32), pltpu.VMEM((H, BLK, BLK), F32), pltpu.VMEM((H, BLK, dv), F32)],
        compiler_params=_params(1),
    )(*[proj] * (3 * H), *_ret_tables(dk), gn_g, gn_b)


def _ret_bwd(proj, gn_g, gn_b, o_pre, states, d_or, dga, dgb, dk, dv):
    S, in_w = proj.shape
    N = S // BLK
    H = RET_HEADS
    qk_w, v_w = H * dk, H * dv
    kscale = dk ** -0.5
    n_in = 3 * H + 4
    out_w = 2 * qk_w + 2 * v_w
    gate_w = dga.shape[1]
    col0 = 3 * ATT_W
    assert col0 + out_w + 2 * gate_w == in_w
    rev = lambda i: N - 1 - i

    def body(*refs):
        head, (dec_ref, xi_ref, zeta_ref, gc_ref) = _ret_heads(refs, dk)
        (g_ref, b_ref, opre_ref, st_ref, dor_ref, dga_ref, dgb_ref, dproj_ref, dg_ref, db_ref, dstate, stage,
         sem, do_buf, dox_buf, a_buf, g_buf, dq_buf, dk_buf, dv_buf) = refs[n_in:n_in + 20]
        i = pl.program_id(0)
        slot = i % 2
        out_ref = stage.at[slot]

        def out_copy(s, step):
            rows = pl.ds(pl.multiple_of(rev(step) * BLK, BLK), BLK)
            return pltpu.make_async_copy(stage.at[s], dproj_ref.at[rows, pl.ds(col0, in_w - col0)], sem.at[s])

        @pl.when(i >= 2)
        def _():
            out_copy(slot, i - 2).wait()

        @pl.when(i == 0)
        def _():
            dstate[...] = jnp.zeros_like(dstate)
            dg_ref[...] = jnp.zeros_like(dg_ref)
            db_ref[...] = jnp.zeros_like(db_ref)

        out_ref[:, out_w:out_w + gate_w] = dga_ref[...]
        out_ref[:, out_w + gate_w:out_w + 2 * gate_w] = dgb_ref[...]
        for h in range(H):
            vs = slice(h * dv, (h + 1) * dv)
            _, _, _, gr = head(h)
            gr = gr.astype(F32)
            sg = _sigmoid(gr)
            gain = g_ref[:, vs]
            yh, rstd, y = _gn_fwd(opre_ref[:, vs], gain, b_ref[:, vs])
            d_or_v = dor_ref[:, vs]
            dy = d_or_v * (gr * sg)
            out_ref[:, 2 * qk_w + v_w + h * dv:2 * qk_w + v_w + (h + 1) * dv] = (
                d_or_v * y * (sg * (1.0 + gr * (1.0 - sg)))).astype(out_ref.dtype)
            dg_ref[:, vs] += jnp.sum(dy * yh, axis=0, keepdims=True)
            db_ref[:, vs] += jnp.sum(dy, axis=0, keepdims=True)
            dyh = dy * gain
            do = rstd * (dyh - jnp.mean(dyh, axis=-1, keepdims=True)
                         - yh * jnp.mean(dyh * yh, axis=-1, keepdims=True))
            do_buf[h] = do.astype(CDT)
            dox_buf[h] = (do * xi_ref[h][:, 0:1]).astype(CDT)
        for h in range(H):
            q, k, v, _ = head(h)
            dox = dox_buf[h]
            a_buf[h] = _dot_nt(q, k)
            g_buf[h] = _dot_nt(do_buf[h], v)
            dsn = dstate[h]
            dsn_c = dsn.astype(CDT)
            kz = (k.astype(F32) * zeta_ref[h][:, 0:1]).astype(CDT)
            dq_buf[h] = _dot_nt(dox, st_ref[h])
            dk_buf[h] = _dot_nt(v, dsn_c)
            dv_buf[h] = _dot_nn(kz, dsn_c)
            dstate[h] = dsn * gc_ref[h][0:1, 0:1] + _dot_tn(q, dox)
        for h in range(H):
            q, k, _, _ = head(h)
            decay = dec_ref[h]
            a_c = (a_buf[h] * kscale * decay).astype(CDT)
            g_c = (g_buf[h] * decay).astype(CDT)
            dq = _dot_nn(g_c, k) * kscale + dq_buf[h]
            dkk = _dot_tn(g_c, q) * kscale + dk_buf[h] * zeta_ref[h][:, 0:1]
            dvv = _dot_tn(a_c, do_buf[h]) + dv_buf[h]
            out_ref[:, h * dk:(h + 1) * dk] = dq.astype(out_ref.dtype)
            out_ref[:, qk_w + h * dk:qk_w + (h + 1) * dk] = dkk.astype(out_ref.dtype)
            out_ref[:, 2 * qk_w + h * dv:2 * qk_w + (h + 1) * dv] = dvv.astype(out_ref.dtype)

        cp = out_copy(slot, i)
        cp.start()

        @pl.when(i == N - 1)
        def _():
            cp.wait()
            if N >= 2:
                out_copy(1 - slot, i - 1).wait()

    row = pl.BlockSpec((1, v_w), lambda i: (0, 0))
    tile = pl.BlockSpec((BLK, v_w), lambda i: (rev(i), 0))
    gate = pl.BlockSpec((BLK, gate_w), lambda i: (rev(i), 0))
    return pl.pallas_call(
        body, name="ret_bwd", grid=(N,),
        in_specs=_ret_specs(dk, dv, rev) + [row, row, tile,
                 pl.BlockSpec((None, H, dk, dv), lambda i: (rev(i), 0, 0, 0)), tile, gate, gate],
        out_specs=[pl.BlockSpec(memory_space=pl.ANY), row, row],
        out_shape=[_sds((S, in_w), CDT), _sds((1, v_w), F32), _sds((1, v_w), F32)],
        scratch_shapes=[pltpu.VMEM((H, dk, dv), F32), pltpu.VMEM((2, BLK, in_w - col0), CDT),
                        pltpu.SemaphoreType.DMA((2,)),
                        pltpu.VMEM((H, BLK, dv), CDT), pltpu.VMEM((H, BLK, dv), CDT),
                        pltpu.VMEM((H, BLK, BLK), F32), pltpu.VMEM((H, BLK, BLK), F32),
                        pltpu.VMEM((H, BLK, dk), F32), pltpu.VMEM((H, BLK, dk), F32), pltpu.VMEM((H, BLK, dv), F32)],
        compiler_params=_params(1),
    )(*[proj] * (3 * H), *_ret_tables(dk), gn_g, gn_b, o_pre, states, d_or, dga, dgb)


def _merge_fwd(o_a, o_r, wa, wb, proj, d_model, tm=512, tn=512):
    S, in_w = proj.shape
    off_a, off_b = in_w - 2 * d_model, in_w - d_model
    assert off_a % tn == 0 and off_b % tn == 0

    def body(oa_ref, or_ref, wa_ref, wb_ref, ga_ref, gb_ref, y_ref, pa_ref, pb_ref):
        pa = _dot_nn(oa_ref[...], wa_ref[...])
        pb = _dot_nn(or_ref[...], wb_ref[...])
        y = _sigmoid(ga_ref[...].astype(F32)) * pa + _sigmoid(gb_ref[...].astype(F32)) * pb
        y_ref[...] = y.astype(y_ref.dtype)
        pa_ref[...] = pa.astype(pa_ref.dtype)
        pb_ref[...] = pb.astype(pb_ref.dtype)

    ka, kb = o_a.shape[1], o_r.shape[1]
    out = pl.BlockSpec((tm, tn), lambda i, j: (i, j))
    return pl.pallas_call(
        body, name="merge_fwd", grid=(S // tm, d_model // tn),
        in_specs=[pl.BlockSpec((tm, ka), lambda i, j: (i, 0)), pl.BlockSpec((tm, kb), lambda i, j: (i, 0)),
                  pl.BlockSpec((ka, tn), lambda i, j: (0, j)), pl.BlockSpec((kb, tn), lambda i, j: (0, j)),
                  pl.BlockSpec((tm, tn), lambda i, j: (i, off_a // tn + j)),
                  pl.BlockSpec((tm, tn), lambda i, j: (i, off_b // tn + j))],
        out_specs=[out, out, out], out_shape=[_sds((S, d_model), CDT)] * 3,
        compiler_params=_params(2))(o_a, o_r, wa, wb, proj, proj)


def _local_step(x, target, w_in_mine, chip, others, far_w_in, small, late_weights, on_grads, deps0=()):
    S, D = x.shape
    ns_in = w_in_mine.shape[1]
    in_w = N_CHIPS * ns_in
    d_ff = 4 * D
    ret_v_w = 2 * D
    dv = ret_v_w // RET_HEADS
    dk = (in_w - 3 * ATT_W - 2 * ret_v_w - 2 * D) // (2 * RET_HEADS)
    gqk = jnp.concatenate([small["q_norm_g"].reshape(1, ATT_W), small["k_norm_g"].reshape(1, ATT_W)], axis=1)
    g1, g2 = small["norm1_g"], small["norm2_g"]
    gn_g, gn_b = small["ret_gn_g"], small["ret_gn_b"]

    xn = _rms_fwd("rms1_fwd", x, g1)
    proj_sds = _sds((S, in_w), CDT)
    (proj_mine,) = _matmul(
        "in_proj_mine", "nn", xn, w_in_mine, tm=512, tn=ns_in, tk=D, prefetch=[chip],
        outs=[(proj_sds, pl.BlockSpec((512, ns_in), lambda i, j, k, c: (i, c[0])))], epilogue=_ep_store, deps=deps0)
    w_in = far_w_in(proj_mine)
    (proj,) = _matmul(
        "in_proj_far", "nn", xn, w_in, tm=512, tn=ns_in, tk=D, prefetch=[others], n_cols=(N_CHIPS - 1) * ns_in,
        b_spec=pl.BlockSpec((None, D, ns_in), lambda i, j, k, o: (o[j], 0, 0)),
        outs=[(proj_sds, pl.BlockSpec((512, ns_in), lambda i, j, k, o: (i, o[j])))], epilogue=_ep_store,
        deps=[proj_mine], alias_dep_to_out=(0, 0), j_outer=True)
    qkv = _qknorm_fwd(proj, gqk)
    att = [_att_fwd(g, S, qkv[g]) for g in range(3)]
    os_, ls_ = [a[0] for a in att], [a[1] for a in att]
    o_a = _mix_fwd(S, os_, ls_)
    o_pre, o_r, states = _ret_fwd(proj, gn_g, gn_b, dk, dv)
    w = late_weights(o_r)
    y, pa, pb = _merge_fwd(o_a, o_r, w["w_proj_a"], w["w_proj_b"], proj, D)
    (x1,) = _matmul("out_proj", "nn", y, w["w_out"], tm=512, tn=D, tk=D,
                    extras=[(x, _mn(512, D))], outs=[(_sds((S, D), F32), _mn(512, D))], epilogue=_ep_resid)
    xn2 = _rms_fwd("rms2_fwd", x1, g2)
    hid, act = _matmul("mlp_up", "nn", xn2, w["w_up"], tm=512, tn=2048, tk=D, j_outer=True,
                       outs=[(_sds((S, d_ff), CDT), _mn(512, 2048))] * 2, epilogue=_ep_up)
    dx2, dx2c, loss_row = _matmul(
        "mlp_down_loss", "nn", act, w["w_down"], tm=512, tn=D, tk=d_ff,
        extras=[(x1, _mn(512, D)), (target, _mn(512, D))],
        outs=[(_sds((S, D), F32), _mn(512, D)), (_sds((S, D), CDT), _mn(512, D)), (_sds((1, D), F32), _row(D))],
        epilogue=functools.partial(_ep_down_loss, inv_d=1.0 / D))
    loss = 0.5 * jnp.sum(loss_row) / D

    (dh,) = _matmul("d_hidden", "nt", dx2c, w["w_down"], tm=512, tn=2048, tk=D, j_outer=True,
                    extras=[(hid, _mn(512, 2048))], outs=[(_sds((S, d_ff), CDT), _mn(512, 2048))], epilogue=_ep_dh)
    (gw_down,) = _matmul("dw_down", "tn", act, dx2c, tm=1024, tn=D, tk=1024,
                         outs=[(_sds((d_ff, D), F32), _mn(1024, D))], epilogue=_ep_store)
    (gw_up,) = _matmul("dw_up", "tn", xn2, dh, tm=D, tn=1024, tk=1024,
                       outs=[(_sds((D, d_ff), F32), _mn(D, 1024))], epilogue=_ep_store)
    tok = on_grads({"w_down": gw_down, "w_up": gw_up})
    dx1, dx1c, dg2 = _matmul(
        "d_x1", "nt", dh, w["w_up"], tm=512, tn=D, tk=d_ff,
        extras=[(x1, _mn(512, D)), (g2, _row(D)), (dx2, _mn(512, D))],
        outs=[(_sds((S, D), F32), _mn(512, D)), (_sds((S, D), CDT), _mn(512, D)), (_sds((1, D), F32), _row(D))],
        epilogue=_ep_rms_bwd, deps=[tok])

    gt = 512
    assert (in_w - 2 * D) % gt == 0
    off_a, off_b = (in_w - 2 * D) // gt, (in_w - D) // gt
    dpa, dpb, dga, dgb = _matmul(
        "d_gates", "nt", dx1c, w["w_out"], tm=512, tn=gt, tk=D,
        extras=[(proj, _mn(512, gt, off_a)), (proj, _mn(512, gt, off_b)), (pa, _mn(512, gt)), (pb, _mn(512, gt))],
        outs=[(_sds((S, D), CDT), _mn(512, gt))] * 4, epilogue=_ep_gates)
    (gw_out,) = _matmul("dw_out", "tn", y, dx1c, tm=D, tn=D, tk=1024,
                        outs=[(_sds((D, D), F32), _mn(D, D))], epilogue=_ep_store)
    (gw_pa,) = _matmul("dw_proj_a", "tn", o_a, dpa, tm=GW, tn=D, tk=1024,
                       outs=[(_sds((GW, D), F32), _mn(GW, D))], epilogue=_ep_store)
    (gw_pb,) = _matmul("dw_proj_b", "tn", o_r, dpb, tm=1024, tn=D, tk=1024,
                       outs=[(_sds((ret_v_w, D), F32), _mn(1024, D))], epilogue=_ep_store)
    (do_a,) = _matmul("d_o_a", "nt", dpa, w["w_proj_a"], tm=1024, tn=GW, tk=D,
                      outs=[(_sds((S, GW), F32), _mn(1024, GW))], epilogue=_ep_store)
    tok = on_grads({"w_out": gw_out, "w_proj_a": gw_pa, "w_proj_b": gw_pb})
    (d_or,) = _matmul("d_o_r", "nt", dpb, w["w_proj_b"], tm=512, tn=ret_v_w, tk=D,
                      outs=[(_sds((S, ret_v_w), F32), _mn(512, ret_v_w))], epilogue=_ep_store, deps=[tok])

    dproj, dgn_g, dgn_b = _ret_bwd(proj, gn_g, gn_b, o_pre, states, d_or, dga, dgb, dk, dv)
    do_gs, c_gs = _mix_bwd(S, os_, ls_, do_a)
    datt_parts = [_att_bwd(g, S, qkv[g], ls_[g], do_gs[g], c_gs[g]) for g in range(3)]
    dproj, dgqk = _qknorm_bwd(proj, gqk, [p[0] for p in datt_parts], [p[1] for p in datt_parts],
                              [p[2] for p in datt_parts], dproj)

    (gw_in,) = _matmul(
        "dw_in", "tn", xn, dproj, tm=512, tn=ns_in, tk=1024,
        outs=[(_sds((N_CHIPS, D, ns_in), F32), pl.BlockSpec((None, 512, ns_in), lambda i, j, k: (j, i, 0)))],
        epilogue=_ep_store)
    tok = on_grads({"w_in": gw_in})
    grad_x, _, dg1 = _matmul(
        "d_x", "nt", dproj, w_in, tm=512, tn=D, tk=ns_in, n_cols=D,
        b_spec=pl.BlockSpec((None, D, ns_in), lambda i, j, k: (k, 0, 0)),
        extras=[(x, _mn(512, D)), (g1, _row(D)), (dx1, _mn(512, D))],
        outs=[(_sds((S, D), F32), _mn(512, D)), (_sds((S, D), CDT), _mn(512, D)), (_sds((1, D), F32), _row(D))],
        epilogue=_ep_rms_bwd, deps=[tok])

    smallg = {"norm1_g": dg1, "q_norm_g": dgqk[:, :ATT_W], "k_norm_g": dgqk[:, ATT_W:],
              "ret_gn_g": dgn_g, "ret_gn_b": dgn_b, "norm2_g": dg2}
    return loss, grad_x, smallg


N_CHIPS = 4
N_DEV = 8


def _place():
    x, y, c = lax.axis_index("x"), lax.axis_index("y"), lax.axis_index("c")
    return x, y, c


def _other_chips(x, y):
    out = []
    for fx, fy in ((1, 0), (0, 1), (1, 1)):
        px = 1 - x if fx else x
        py = 1 - y if fy else y
        out.append(((px, py), 2 * px + py))
    return out


SEM_SPEC = pl.BlockSpec(memory_space=pltpu.SEMAPHORE)
ANY_SPEC = pl.BlockSpec(memory_space=pl.ANY)
EFFECT = pltpu.SideEffectType.DATAFLOW_SIDE_EFFECTING


def _ici_copies(kind, srcs, lands, send, recv):
    x, y, c = _place()
    me = 2 * x + y
    out = []
    for w, (s, l) in enumerate(zip(srcs, lands)):
        for j, ((px, py), pidx) in enumerate(_other_chips(x, y)):
            if kind == "gather":
                half = s.shape[0] // 2
                rows = pl.ds(c * half, half)
                src, dst_there, dst_here = s.at[rows, :], l.at[me, rows, :], l.at[pidx, rows, :]
            else:
                src, dst_there, dst_here = s.at[pidx], l.at[me], l.at[pidx]
            out.append((src, dst_there, dst_here, send.at[3 * w + j], recv.at[3 * w + j], (px, py, c)))
    return out


def _exchange_start(name, kind, srcs, land_shapes):
    n = len(srcs)

    def body(*refs):
        src_refs, land_refs = refs[:n], refs[n:2 * n]
        send, recv = refs[2 * n], refs[2 * n + 1]
        token = refs[-1]
        for src, dst, _, ss, rs, dev in _ici_copies(kind, src_refs, land_refs, send, recv):
            pltpu.make_async_remote_copy(src_ref=src, dst_ref=dst, send_sem=ss, recv_sem=rs, device_id=dev,
                                         device_id_type=MESH).start()
        token[...] = jnp.zeros_like(token)

    thru = [pltpu.HBM(s.shape, s.dtype) for s in srcs] + [pltpu.HBM(shape, dtype) for shape, dtype in land_shapes]
    res = pl.pallas_call(
        body, name=name,
        out_shape=(pltpu.SemaphoreType.DMA((3 * n,)), pltpu.SemaphoreType.DMA((3 * n,)), *thru, _sds((8, LANES), F32)),
        in_specs=[HBM_SPEC] * (2 * n), out_specs=(SEM_SPEC, SEM_SPEC, *[HBM_SPEC] * (2 * n), VMEM_SPEC),
        input_output_aliases={i: 2 + i for i in range(2 * n)},
        compiler_params=pltpu.CompilerParams(has_side_effects=EFFECT),
    )(*[pltpu.with_memory_space_constraint(s, pltpu.HBM) for s in srcs],
      *[pltpu.with_memory_space_constraint(lax.empty(shape, dtype), pltpu.HBM) for shape, dtype in land_shapes])
    return res[0], res[1], list(res[2:2 + n]), list(res[2 + n:2 + 2 * n]), res[-1]


def _exchange_wait(name, kind, send, recv, srcs, lands, after):
    n = len(srcs)

    def body(*refs):
        src_refs, land_refs = refs[:n], refs[n:2 * n]
        send_ref, recv_ref = refs[2 * n], refs[2 * n + 1]
        for src, _, dst, ss, rs, dev in _ici_copies(kind, src_refs, land_refs, send_ref, recv_ref):
            cp = pltpu.make_async_remote_copy(src_ref=src, dst_ref=dst, send_sem=ss, recv_sem=rs, device_id=dev,
                                              device_id_type=MESH)
            cp.wait_send()
            cp.wait_recv()

    thru = [pltpu.HBM(t.shape, t.dtype) for t in list(srcs) + list(lands)]
    res = pl.pallas_call(
        body, name=name, out_shape=thru,
        in_specs=[HBM_SPEC] * (2 * n) + [SEM_SPEC, SEM_SPEC, ANY_SPEC], out_specs=[HBM_SPEC] * (2 * n),
        input_output_aliases={i: i for i in range(2 * n)},
        compiler_params=pltpu.CompilerParams(has_side_effects=EFFECT),
    )(*srcs, *lands, send, recv, after)
    return list(res[:n]), list(res[n:])


PAIR_TILE_ELEMS = 1 << 19


def _pair_fill(name, gathered, mine, core, others, chip):
    k, r, C = gathered.shape
    half = r // 2
    tr = _row_tile(half, C, PAIR_TILE_ELEMS, mult=16)
    nt = half // tr
    n_far = N_CHIPS - 1

    def body(c_ref, o_ref, chip_ref, in_ref, mine_ref, out_ref, slot, send, recv):
        j = pl.program_id(0)
        b = (j * nt + pl.program_id(1)) % 2
        x, y, c = _place()
        cp = pltpu.make_async_remote_copy(src_ref=in_ref, dst_ref=slot.at[b], send_sem=send.at[b],
                                          recv_sem=recv.at[b], device_id=(x, y, 1 - c), device_id_type=MESH)

        @pl.when(j < n_far)
        def _():
            cp.start()
            cp.wait_recv()
            out_ref[...] = slot[b]
            cp.wait_send()

        @pl.when(j >= n_far)
        def _():
            out_ref[...] = mine_ref[...]

    def far(j):
        return jnp.minimum(j, n_far - 1)

    grid_spec = pltpu.PrefetchScalarGridSpec(
        num_scalar_prefetch=3, grid=(n_far + 2, nt),
        in_specs=[pl.BlockSpec((tr, C), lambda j, i, c, o, m: (
                      (2 * o[far(j)] + c[0]) * nt + jnp.where(j < n_far, i, nt - 1), 0)),
                  pl.BlockSpec((tr, C), lambda j, i, c, o, m: (jnp.where(j < n_far, 0, (j - n_far) * nt + i), 0))],
        out_specs=pl.BlockSpec((tr, C), lambda j, i, c, o, m: (
            jnp.where(j < n_far, 2 * o[far(j)] + 1 - c[0], 2 * m[0] + j - n_far) * nt + i, 0)),
        scratch_shapes=[pltpu.VMEM((2, tr, C), gathered.dtype), pltpu.SemaphoreType.DMA((2,)),
                        pltpu.SemaphoreType.DMA((2,))])
    out = pl.pallas_call(body, name=name, grid_spec=grid_spec, out_shape=_sds((k * r, C), gathered.dtype),
                         input_output_aliases={3: 0}, compiler_params=_params(2))(
                             core, others, chip, gathered.reshape(k * r, C), mine)
    return out.reshape(k, r, C)


def _pair_reduce(name, g, core):
    k, R, C = g.shape
    half = R // 2
    tr = _row_tile(half, C, PAIR_TILE_ELEMS, mult=16)
    nt = half // tr

    def body(c_ref, mine_ref, give_ref, out_ref, wire_ref, stage, slot, send, recv):
        b = (pl.program_id(0) * nt + pl.program_id(1)) % 2
        x, y, c = _place()
        stage[b] = give_ref[...].astype(stage.dtype)
        cp = pltpu.make_async_remote_copy(src_ref=stage.at[b], dst_ref=slot.at[b], send_sem=send.at[b],
                                          recv_sem=recv.at[b], device_id=(x, y, 1 - c), device_id_type=MESH)
        cp.start()
        cp.wait_recv()
        tot = mine_ref[...] + slot[b].astype(F32)
        out_ref[...] = tot
        wire_ref[...] = tot.astype(wire_ref.dtype)
        cp.wait_send()

    blk = (tr, C)
    out_spec = pl.BlockSpec(blk, lambda s, i, c: (s * nt + i, 0))
    grid_spec = pltpu.PrefetchScalarGridSpec(
        num_scalar_prefetch=1, grid=(k, nt),
        in_specs=[pl.BlockSpec(blk, lambda s, i, c: ((2 * s + c[0]) * nt + i, 0)),
                  pl.BlockSpec(blk, lambda s, i, c: ((2 * s + 1 - c[0]) * nt + i, 0))],
        out_specs=[out_spec, out_spec],
        scratch_shapes=[pltpu.VMEM((2, tr, C), CDT), pltpu.VMEM((2, tr, C), CDT), pltpu.SemaphoreType.DMA((2,)),
                        pltpu.SemaphoreType.DMA((2,))])
    g2 = g.reshape(k * R, C)
    out, wire = pl.pallas_call(body, name=name, grid_spec=grid_spec,
                               out_shape=[_sds((k * half, C), F32), _sds((k * half, C), CDT)],
                               compiler_params=_params(2))(core, g2, g2)
    return out, wire.reshape(k, half, C)


def _all_reduce_small(v):
    r, cdim = v.shape

    def body(v_ref, o_ref, buf, send, recv):
        x, y, c = _place()
        me = 4 * x + 2 * y + c
        buf[me] = v_ref[...]
        sends = []
        for m in range(1, N_DEV):
            px = 1 - x if m & 4 else x
            py = 1 - y if m & 2 else y
            pc = 1 - c if m & 1 else c
            cp = pltpu.make_async_remote_copy(src_ref=v_ref, dst_ref=buf.at[me], send_sem=send.at[m - 1],
                                              recv_sem=recv.at[m - 1], device_id=(px, py, pc), device_id_type=MESH)
            cp.start()
            sends.append((cp, 4 * px + 2 * py + pc))
        for m, (cp, pidx) in enumerate(sends):
            pltpu.make_async_remote_copy(src_ref=v_ref, dst_ref=buf.at[pidx], send_sem=send.at[m], recv_sem=recv.at[m],
                                         device_id=(x, y, c), device_id_type=MESH).wait_recv()
        for cp, _ in sends:
            cp.wait_send()
        tot = buf[0]
        for k in range(1, N_DEV):
            tot = tot + buf[k]
        o_ref[...] = tot

    return pl.pallas_call(
        body, name="all_reduce_small", in_specs=[VMEM_SPEC], out_specs=VMEM_SPEC,
        out_shape=_sds((r, cdim), F32),
        scratch_shapes=[pltpu.VMEM((N_DEV, r, cdim), F32), pltpu.SemaphoreType.DMA((N_DEV - 1,)),
                        pltpu.SemaphoreType.DMA((N_DEV - 1,))],
    )(v)


def _row_tile(rows, cols, budget_elems=1 << 18, mult=8):
    if rows % mult:
        return rows
    t = max(mult, (budget_elems // cols) // mult * mult)
    while rows % t:
        t -= mult
    return t


def _adamw_update(w, g, m, v):
    nm = ADAM_B1 * m + (1.0 - ADAM_B1) * g
    nv = ADAM_B2 * v + (1.0 - ADAM_B2) * (g * g)
    m_hat = nm / (1.0 - ADAM_B1 ** ADAM_STEP)
    v_hat = nv / (1.0 - ADAM_B2 ** ADAM_STEP)
    return -ADAM_LR * (m_hat / (jnp.sqrt(v_hat) + ADAM_EPS) + ADAM_WD * w), nm, nv


def _adamw(name, w, g, m, v):
    R, C = w.shape
    tr = _row_tile(R, C, 1 << 17)

    def body(w_ref, g_ref, m_ref, v_ref, d_ref, nm_ref, nv_ref):
        d_ref[...], nm_ref[...], nv_ref[...] = _adamw_update(w_ref[...], g_ref[...], m_ref[...], v_ref[...])

    spec = pl.BlockSpec((tr, C), lambda i: (i, 0))
    return pl.pallas_call(body, name=name, grid=(R // tr,), in_specs=[spec] * 4, out_specs=[spec] * 3,
                          out_shape=[_sds((R, C), F32)] * 3, compiler_params=_params(1))(w, g, m, v)


def _sum_share(name, own, by_chip, chip, others, core):
    k, half, C = by_chip.shape
    tr = _row_tile(half, C, mult=16)
    nt = half // tr

    def body(chip_ref, oth_ref, c_ref, own_ref, a_ref, b_ref, cc_ref, g_out, mine, slot, send, recv):
        p = pl.program_id(1)
        b = pl.program_id(0) % 2
        x, y, c = _place()
        cp = pltpu.make_async_remote_copy(src_ref=mine.at[b], dst_ref=slot.at[b], send_sem=send.at[b],
                                          recv_sem=recv.at[b], device_id=(x, y, 1 - c), device_id_type=MESH)

        @pl.when(p == 0)
        def _():
            tot = ((own_ref[...] + a_ref[...].astype(F32)) + b_ref[...].astype(F32)) + cc_ref[...].astype(F32)
            mine[b] = tot
            cp.start()
            g_out[...] = tot

        @pl.when(p == 1)
        def _():
            cp.wait_recv()
            g_out[...] = slot[b]
            cp.wait_send()

    def piece(j):
        return pl.BlockSpec((tr, C), lambda i, p, chip, oth, c: (oth[j] * nt + i, 0))

    grid_spec = pltpu.PrefetchScalarGridSpec(
        num_scalar_prefetch=3, grid=(nt, 2),
        in_specs=[pl.BlockSpec((tr, C), lambda i, p, chip, oth, c: (chip[0] * nt + i, 0)),
                  piece(0), piece(1), piece(2)],
        out_specs=pl.BlockSpec((tr, C), lambda i, p, chip, oth, c: (
            jnp.where(p == 0, c[0], 1 - c[0]) * nt + i, 0)),
        scratch_shapes=[pltpu.VMEM((2, tr, C), F32), pltpu.VMEM((2, tr, C), F32), pltpu.SemaphoreType.DMA((2,)),
                        pltpu.SemaphoreType.DMA((2,))])
    by2 = by_chip.reshape(k * half, C)
    return pl.pallas_call(body, name=name, grid_spec=grid_spec, out_shape=_sds((2 * half, C), F32),
                          compiler_params=_params(2))(chip, others, core, own, by2, by2, by2)


BIG = ("w_in", "w_proj_a", "w_proj_b", "w_out", "w_up", "w_down")
COL_SHARDED = ("w_in", "w_proj_a", "w_up")
SMALL = ("norm1_g", "q_norm_g", "k_norm_g", "ret_gn_g", "ret_gn_b", "norm2_g")
ALL_W = ("norm1_g", "w_in", "q_norm_g", "k_norm_g", "ret_gn_g", "ret_gn_b", "w_proj_a", "w_proj_b", "w_out",
         "norm2_g", "w_up", "w_down")
LANES = 128


def _to_full(name, gathered):
    k, r, c = gathered.shape
    if name in COL_SHARDED:
        return gathered.transpose(1, 0, 2).reshape(r, k * c)
    return gathered.reshape(k * r, c)


def _to_shard_major(name, full):
    if name in COL_SHARDED:
        r, c4 = full.shape
        return full.reshape(r, N_CHIPS, c4 // N_CHIPS).transpose(1, 0, 2)
    r4, c = full.shape
    return full.reshape(N_CHIPS, r4 // N_CHIPS, c)


def kernel(x, norm1_g, w_in, q_norm_g, k_norm_g, ret_gn_g, ret_gn_b, w_proj_a, w_proj_b, w_out, norm2_g, w_up, w_down, loss_target, m_norm1_g, m_w_in, m_q_norm_g, m_k_norm_g, m_ret_gn_g, m_ret_gn_b, m_w_proj_a, m_w_proj_b, m_w_out, m_norm2_g, m_w_up, m_w_down, v_norm1_g, v_w_in, v_q_norm_g, v_k_norm_g, v_ret_gn_g, v_ret_gn_b, v_w_proj_a, v_w_proj_b, v_w_out, v_norm2_g, v_w_up, v_w_down):
    weights = dict(norm1_g=norm1_g, w_in=w_in, q_norm_g=q_norm_g, k_norm_g=k_norm_g, ret_gn_g=ret_gn_g,
                   ret_gn_b=ret_gn_b, w_proj_a=w_proj_a, w_proj_b=w_proj_b, w_out=w_out, norm2_g=norm2_g,
                   w_up=w_up, w_down=w_down)
    moments_m = dict(norm1_g=m_norm1_g, w_in=m_w_in, q_norm_g=m_q_norm_g, k_norm_g=m_k_norm_g, ret_gn_g=m_ret_gn_g,
                     ret_gn_b=m_ret_gn_b, w_proj_a=m_w_proj_a, w_proj_b=m_w_proj_b, w_out=m_w_out,
                     norm2_g=m_norm2_g, w_up=m_w_up, w_down=m_w_down)
    moments_v = dict(norm1_g=v_norm1_g, w_in=v_w_in, q_norm_g=v_q_norm_g, k_norm_g=v_k_norm_g, ret_gn_g=v_ret_gn_g,
                     ret_gn_b=v_ret_gn_b, w_proj_a=v_w_proj_a, w_proj_b=v_w_proj_b, w_out=v_w_out,
                     norm2_g=v_norm2_g, w_up=v_w_up, w_down=v_w_down)

    mx, my = lax.axis_index("x"), lax.axis_index("y")
    core = lax.axis_index("c").astype(jnp.int32).reshape(1)
    chip = (2 * mx + my).astype(jnp.int32).reshape(1)
    others = jnp.stack([2 * (1 - mx) + my, 2 * mx + 1 - my, 2 * (1 - mx) + 1 - my]).astype(jnp.int32)
    shards = {n: weights[n][0].astype(CDT) for n in BIG}
    def start_gather(name, names):
        return _exchange_start(name, "gather", [shards[n] for n in names],
                               [((N_CHIPS,) + shards[n].shape, CDT) for n in names])

    i_send, i_recv, i_srcs, i_lands, i_token = start_gather("gather_w_in_start", ["w_in"])
    late = [n for n in BIG if n != "w_in"]
    l_send, l_recv, l_srcs, l_lands, l_token = start_gather("gather_late_start", late)

    def far_w_in(after):
        srcs, lands = _exchange_wait("gather_w_in_wait", "gather", i_send, i_recv, i_srcs, i_lands, after)
        return _pair_fill("pair_fill_w_in", lands[0], srcs[0], core, others, chip)

    def late_weights(after):
        srcs, lands = _exchange_wait("gather_late_wait", "gather", l_send, l_recv, l_srcs, l_lands, after)
        out = {}
        for n, mine, land in zip(late, srcs, lands):
            out[n] = _to_full(n, _pair_fill("pair_fill_%s" % n, land, mine, core, others, chip))
        return out

    pending = []

    def on_grads(group):
        names = list(group)
        red = [_pair_reduce("pair_reduce_%s" % n, g if g.ndim == 3 else _to_shard_major(n, g), core)
               for n, g in group.items()]
        wires = [wire for _, wire in red]
        send, recv, srcs, lands, token = _exchange_start(
            "scatter_start_%s" % names[0], "scatter", wires, [(wire.shape, wire.dtype) for wire in wires])
        pending.append((names, [own for own, _ in red], send, recv, srcs, lands))
        return token

    small = {n: weights[n].reshape(1, -1) for n in SMALL}

    loss, grad_x, small_g = _local_step(x[0], loss_target[0], i_srcs[0], chip, others, far_w_in, small,
                                        late_weights, on_grads, deps0=[i_token, l_token])
    loss = lax.psum(loss, ("x", "y", "c"))

    out_g, out_d, out_m, out_v = {}, {}, {}, {}
    for names, owns, send, recv, srcs, lands in pending:
        _, got = _exchange_wait("scatter_wait_%s" % names[0], "scatter", send, recv, srcs, lands, grad_x)
        for n, own, by_chip in zip(names, owns, got):
            shape = weights[n].shape
            g2 = _sum_share("sum_share_%s" % n, own, by_chip, chip, others, core)
            d, nm, nv = _adamw("adamw_%s" % n, weights[n][0], g2, moments_m[n][0], moments_v[n][0])
            out_g[n], out_d[n], out_m[n], out_v[n] = (t.reshape(shape) for t in (g2, d, nm, nv))

    packed = jnp.concatenate([small_g[n].reshape(1, -1) for n in SMALL], axis=1)
    red = _all_reduce_small(packed.reshape(-1, LANES)).reshape(1, -1)
    off = 0
    for n in SMALL:
        shape = weights[n].shape
        row = (1, weights[n].size)
        g2 = red[:, off:off + row[1]]
        off += row[1]
        d, nm, nv = _adamw("adamw_%s" % n, weights[n].reshape(row), g2, moments_m[n].reshape(row),
                           moments_v[n].reshape(row))
        out_g[n], out_d[n], out_m[n], out_v[n] = (t.reshape(shape) for t in (g2, d, nm, nv))

    return (loss, grad_x[None], *[out_g[n] for n in ALL_W], *[out_d[n] for n in ALL_W],
            *[out_m[n] for n in ALL_W], *[out_v[n] for n in ALL_W])
```

```python
import functools
import math

import jax
import jax.numpy as jnp
from jax import lax
from jax.experimental import pallas as pl
from jax.experimental.pallas import tpu as pltpu

CDT = jnp.bfloat16
F32 = jnp.float32
EPS = 1e-6

ATT_GROUPS = ((128, 1), (512, 4), (2048, 16))
ATT_HPG = 4
ATT_HEADS = 12
HD = 128
BLK = 128
ATT_W = ATT_HEADS * HD
GW = ATT_HPG * HD
RET_HEADS = 4

ADAM_LR = 0.001
ADAM_B1 = 0.9
ADAM_B2 = 0.999
ADAM_EPS = 1e-08
ADAM_WD = 0.01
ADAM_STEP = 10

VMEM_LIMIT_BYTES = 56 * 1024 * 1024
MXU_DIM = 256
MESH = pl.DeviceIdType.MESH
HBM_SPEC = pl.BlockSpec(memory_space=pltpu.HBM)
VMEM_SPEC = pl.BlockSpec(memory_space=pltpu.VMEM)


def _params(n_axes):
    return pltpu.CompilerParams(dimension_semantics=("arbitrary",) * n_axes,
                                vmem_limit_bytes=VMEM_LIMIT_BYTES)


def _dot_nn(a, b):
    return jnp.dot(a, b, preferred_element_type=F32)


def _dot_nt(a, b):
    return lax.dot_general(a, b, (((1,), (1,)), ((), ())), preferred_element_type=F32)


def _dot_tn(a, b):
    return lax.dot_general(a, b, (((0,), (0,)), ((), ())), preferred_element_type=F32)


def _sigmoid(v):
    return 1.0 / (1.0 + jnp.exp(-v))


def _matmul(name, mode, a, b, *, tm, tn, tk, extras=(), outs, epilogue, deps=(), b_spec=None, n_cols=None,
            prefetch=(), alias_dep_to_out=None, j_outer=False):
    deps = [d for d in deps if d is not None]
    if mode == "tn":
        K, M = a.shape
    else:
        M, K = a.shape
    if b_spec is None:
        (N, K2) = b.shape if mode == "nt" else b.shape[::-1]
        assert K == K2, (name, a.shape, b.shape)
        if mode == "nt":
            b_spec = pl.BlockSpec((tn, tk), lambda i, j, k, *p: (j, k))
        else:
            b_spec = pl.BlockSpec((tk, tn), lambda i, j, k, *p: (k, j))
    else:
        N = n_cols
    assert M % tm == 0 and N % tn == 0 and K % tk == 0, (name, a.shape, b.shape)
    ni, nj, nk = M // tm, N // tn, K // tk
    if mode == "tn":
        a_spec = pl.BlockSpec((tk, tm), lambda i, j, k, *p: (k, i))
    else:
        a_spec = pl.BlockSpec((tm, tk), lambda i, j, k, *p: (i, k))
    dot = {"nn": _dot_nn, "nt": _dot_nt, "tn": _dot_tn}[mode]
    n_ex, n_out, n_dep, n_pre = len(extras), len(outs), len(deps), len(prefetch)
    grid = (ni, nj, nk)
    if j_outer:
        grid = (nj, ni, nk)

        def swapped(spec):
            return pl.BlockSpec(spec.block_shape, lambda j, i, k, *p: spec.index_map(i, j, k, *p))

        a_spec, b_spec = swapped(a_spec), swapped(b_spec)
        extras = [(e, swapped(s)) for e, s in extras]
        outs = [(o, swapped(s)) for o, s in outs]

    def body(*refs):
        refs = refs[n_pre:]
        a_ref, b_ref = refs[0], refs[1]
        ex = refs[2:2 + n_ex]
        out = refs[2 + n_ex + n_dep:2 + n_ex + n_dep + n_out]
        acc = refs[-1] if nk > 1 else None
        i = pl.program_id(1 if j_outer else 0)
        k = pl.program_id(2)
        if nk == 1:
            epilogue(dot(a_ref[...].astype(CDT), b_ref[...].astype(CDT)), ex, out, i)
            return

        @pl.when(k == 0)
        def _():
            acc[...] = jnp.zeros_like(acc)

        acc[...] += dot(a_ref[...].astype(CDT), b_ref[...].astype(CDT))

        @pl.when(k == nk - 1)
        def _():
            epilogue(acc[...], ex, out, i)

    grid_spec = pltpu.PrefetchScalarGridSpec(
        num_scalar_prefetch=n_pre, grid=grid,
        in_specs=[a_spec, b_spec] + [s for _, s in extras] + [pl.BlockSpec(memory_space=pl.ANY)] * n_dep,
        out_specs=[s for _, s in outs],
        scratch_shapes=[pltpu.VMEM((tm, tn), F32)] if nk > 1 else [])
    aliases = {}
    if alias_dep_to_out is not None:
        aliases = {n_pre + 2 + n_ex + alias_dep_to_out[0]: alias_dep_to_out[1]}
    res = pl.pallas_call(
        body, name=name, grid_spec=grid_spec, out_shape=[o for o, _ in outs], input_output_aliases=aliases,
        compiler_params=_params(3),
    )(*prefetch, a, b, *[e for e, _ in extras], *deps)
    return res


def _mn(tm, tn, col_off=0):
    return pl.BlockSpec((tm, tn), lambda i, j, k, *p: (i, j + col_off))


def _row(tn):
    return pl.BlockSpec((1, tn), lambda i, j, k, *p: (0, j))


def _ep_store(acc, ex, out, i):
    out[0][...] = acc.astype(out[0].dtype)


def _ep_resid(acc, ex, out, i):
    out[0][...] = ex[0][...] + acc


def _ep_up(acc, ex, out, i):
    out[0][...] = acc.astype(out[0].dtype)
    r = jnp.maximum(acc, 0.0)
    out[1][...] = (r * r).astype(out[1].dtype)


def _ep_down_loss(acc, ex, out, i, inv_d):
    diff = (ex[0][...] + acc) - ex[1][...]
    dx2 = diff * inv_d
    out[0][...] = dx2
    out[1][...] = dx2.astype(out[1].dtype)

    @pl.when(i == 0)
    def _():
        out[2][...] = jnp.zeros_like(out[2])

    out[2][...] += jnp.sum(diff * diff, axis=0, keepdims=True)


def _ep_dh(acc, ex, out, i):
    h = ex[0][...].astype(F32)
    out[0][...] = (acc * (2.0 * jnp.maximum(h, 0.0))).astype(out[0].dtype)


def _ep_rms_bwd(acc, ex, out, i):
    x = ex[0][...]
    g = ex[1][...]
    rstd = lax.rsqrt(jnp.mean(x * x, axis=-1, keepdims=True) + EPS)
    xh = x * rstd
    dxh = acc * g
    dx = ex[2][...] + rstd * (dxh - xh * jnp.mean(dxh * xh, axis=-1, keepdims=True))
    out[0][...] = dx
    out[1][...] = dx.astype(out[1].dtype)

    @pl.when(i == 0)
    def _():
        out[2][...] = jnp.zeros_like(out[2])

    out[2][...] += jnp.sum(acc * xh, axis=0, keepdims=True)


def _ep_gates(acc, ex, out, i):
    sa = _sigmoid(ex[0][...].astype(F32))
    sb = _sigmoid(ex[1][...].astype(F32))
    dpa = acc * sa
    dpb = acc * sb
    out[0][...] = dpa.astype(out[0].dtype)
    out[1][...] = dpb.astype(out[1].dtype)
    out[2][...] = (dpa * ex[2][...].astype(F32) * (1.0 - sa)).astype(out[2].dtype)
    out[3][...] = (dpb * ex[3][...].astype(F32) * (1.0 - sb)).astype(out[3].dtype)


def _sds(shape, dtype):
    return jax.ShapeDtypeStruct(shape, dtype)


def _rms_fwd(name, x, g, tm=512):
    S, D = x.shape

    def body(x_ref, g_ref, o_ref):
        xv = x_ref[...]
        rstd = lax.rsqrt(jnp.mean(xv * xv, axis=-1, keepdims=True) + EPS)
        o_ref[...] = (xv * rstd * g_ref[...]).astype(o_ref.dtype)

    return pl.pallas_call(
        body, name=name, grid=(S // tm,),
        in_specs=[pl.BlockSpec((tm, D), lambda i: (i, 0)), pl.BlockSpec((1, D), lambda i: (0, 0))],
        out_specs=pl.BlockSpec((tm, D), lambda i: (i, 0)),
        out_shape=_sds((S, D), CDT), compiler_params=_params(1))(x, g)


def _rm_shape(S, d, width):
    return (S, width) if d == 1 else (d, S // d, width)


def _rm_spec(tm, d, width):
    if d == 1:
        return pl.BlockSpec((tm, width), lambda i: (i, 0))
    return pl.BlockSpec((d, tm // d, width), lambda i: (0, i, 0))


def _rm_put(dst_ref, cols, buf_ref, d):
    if d == 1:
        dst_ref[:, cols] = buf_ref[...].astype(dst_ref.dtype)
        return
    m = buf_ref.shape[0] // d
    for r in range(d):
        dst_ref[r, :, cols] = buf_ref[pl.ds(r, m, stride=d), :].astype(dst_ref.dtype)


def _rm_reader(buf_ref, src_ref, d):
    if d == 1:
        return lambda s, rows: src_ref[rows, s * HD:(s + 1) * HD].astype(F32)
    m = buf_ref.shape[1] // d
    for s in range(buf_ref.shape[0]):
        for r in range(d):
            buf_ref.at[s][pl.ds(r, m, stride=d), :] = src_ref[r, :, s * HD:(s + 1) * HD].astype(F32)
    return lambda s, rows: buf_ref.at[s][rows, :]


ROW_CHUNK = 64


def _for_row_chunks(tm, fn):
    def step(c, carry):
        fn(pl.ds(pl.multiple_of(c * ROW_CHUNK, ROW_CHUNK), ROW_CHUNK))
        return carry

    lax.fori_loop(0, tm // ROW_CHUNK, step, 0)


def _qknorm_fwd(proj, gqk, tm=512):
    S = proj.shape[0]
    W = 2 * ATT_W
    dil = [d for _, d in ATT_GROUPS]

    def body(p_ref, g_ref, o0, o1, o2, buf):
        outs = (o0, o1, o2)
        for hd in range(3 * ATT_HEADS):
            which, head = hd // ATT_HEADS, hd % ATT_HEADS
            grp, slot = head // ATT_HPG, head % ATT_HPG
            cols = slice(hd * HD, (hd + 1) * HD)

            def chunk(rows, which=which, cols=cols):
                v = p_ref[rows, cols].astype(F32)
                if which < 2:
                    rstd = lax.rsqrt(jnp.mean(v * v, axis=-1, keepdims=True) + EPS)
                    v = v * rstd * g_ref[:, cols]
                buf[rows, :] = v

            _for_row_chunks(tm, chunk)
            _rm_put(outs[grp], slice(which * GW + slot * HD, which * GW + (slot + 1) * HD), buf, dil[grp])

    return pl.pallas_call(
        body, name="qknorm_fwd", grid=(S // tm,),
        in_specs=[pl.BlockSpec((tm, 3 * ATT_W), lambda i: (i, 0)), pl.BlockSpec((1, W), lambda i: (0, 0))],
        out_specs=[_rm_spec(tm, d, 3 * GW) for d in dil],
        out_shape=[_sds(_rm_shape(S, d, 3 * GW), CDT) for d in dil],
        scratch_shapes=[pltpu.VMEM((tm, HD), F32)],
        compiler_params=_params(1))(proj, gqk)


def _qknorm_bwd(proj, gqk, dqs, dks, dvs, dproj, tm=256):
    S = proj.shape[0]
    W = 2 * ATT_W
    dil = [d for _, d in ATT_GROUPS]

    def body(p_ref, g_ref, *refs):
        ins = refs[0:9]
        o_ref, dg_ref = refs[10], refs[11]
        bufs = refs[12:21]
        i = pl.program_id(0)

        @pl.when(i == 0)
        def _():
            dg_ref[...] = jnp.zeros_like(dg_ref)

        nat = [_rm_reader(bufs[j], ins[j], dil[j % 3]) for j in range(9)]
        dq_get, dk_get, dv_get = nat[0:3], nat[3:6], nat[6:9]
        for hd in range(2 * ATT_HEADS):
            sl = slice(hd * HD, (hd + 1) * HD)
            head = hd % ATT_HEADS
            grp, slot = head // ATT_HPG, head % ATT_HPG
            get = (dq_get if hd < ATT_HEADS else dk_get)[grp]

            def chunk(rows, sl=sl, slot=slot, get=get):
                dn = get(slot, rows)
                v = p_ref[rows, sl].astype(F32)
                rstd = lax.rsqrt(jnp.mean(v * v, axis=-1, keepdims=True) + EPS)
                vh = v * rstd
                dg_ref[:, sl] += jnp.sum(dn * vh, axis=0, keepdims=True)
                dvh = dn * g_ref[:, sl]
                o_ref[rows, sl] = (rstd * (dvh - vh * jnp.mean(dvh * vh, axis=-1, keepdims=True))).astype(o_ref.dtype)

            _for_row_chunks(tm, chunk)
        for head in range(ATT_HEADS):
            grp, slot = head // ATT_HPG, head % ATT_HPG
            o_ref[:, W + head * HD:W + (head + 1) * HD] = dv_get[grp](slot, slice(None)).astype(o_ref.dtype)

    return pl.pallas_call(
        body, name="qknorm_bwd", grid=(S // tm,),
        in_specs=[pl.BlockSpec((tm, W), lambda i: (i, 0)), pl.BlockSpec((1, W), lambda i: (0, 0))]
        + [_rm_spec(tm, d, GW) for d in dil] * 3 + [pl.BlockSpec(memory_space=pl.ANY)],
        out_specs=[pl.BlockSpec((tm, 3 * ATT_W), lambda i: (i, 0)), pl.BlockSpec((1, W), lambda i: (0, 0))],
        out_shape=[_sds(dproj.shape, dproj.dtype), _sds((1, W), F32)],
        scratch_shapes=[pltpu.VMEM((ATT_HPG, tm, HD), F32)] * 9,
        input_output_aliases={11: 0},
        compiler_params=_params(1))(proj, gqk, *dqs, *dks, *dvs, dproj)


def _att_mask(n):
    qi = lax.broadcasted_iota(jnp.int32, (BLK, 2 * BLK), 0)
    kj = lax.broadcasted_iota(jnp.int32, (BLK, 2 * BLK), 1)
    dist = BLK + qi - kj
    valid = (dist >= 0) & (dist <= BLK) & ((kj >= BLK) | (n > 0))
    return valid, dist.astype(F32)


def _att_slopes(grp):
    return [2.0 ** (-8.0 * (grp * ATT_HPG + hh + 1) / ATT_HEADS) for hh in range(ATT_HPG)]


def _att_spec(d, row_fn, col=0):
    if d == 1:
        return pl.BlockSpec((BLK, GW), lambda r, n: (row_fn(n), col))
    return pl.BlockSpec((None, BLK, GW), lambda r, n: (r, row_fn(n), col))


def _att_qkv_specs(d, nb):
    last = nb - 1

    def cur(n):
        return jnp.minimum(n, last)

    def prev(n):
        return jnp.maximum(jnp.minimum(n, last) - 1, 0)

    return [_att_spec(d, cur, 0), _att_spec(d, prev, 1), _att_spec(d, cur, 1), _att_spec(d, prev, 2),
            _att_spec(d, cur, 2)]


def _att_fwd(grp, S, qkv):
    _, d = ATT_GROUPS[grp]
    L = S // d
    nb = L // BLK
    slopes = _att_slopes(grp)
    scale = HD ** -0.5

    def body(q_ref, kp_ref, kc_ref, vp_ref, vc_ref, o_ref, l_ref, s_buf, p_buf, den_buf):
        n = pl.program_id(1)
        valid, distf = _att_mask(n)
        heads = [slice(hh * HD, (hh + 1) * HD) for hh in range(ATT_HPG)]
        for hh, sl in enumerate(heads):
            k = jnp.concatenate([kp_ref[:, sl], kc_ref[:, sl]], axis=0)
            s_buf[hh] = _dot_nt(q_ref[:, sl], k)
        for hh, sl in enumerate(heads):
            s = s_buf[hh] * scale + (-slopes[hh] * d) * distf
            s = jnp.where(valid, s, -1e30)
            m = jnp.max(s, axis=-1, keepdims=True)
            p = jnp.exp(s - m)
            den = jnp.sum(p, axis=-1, keepdims=True)
            p_buf[hh] = p.astype(CDT)
            den_buf[hh] = jnp.broadcast_to(den, (BLK, HD))
            l_ref[:, sl] = jnp.broadcast_to(m + jnp.log(den), (BLK, HD))
        for hh, sl in enumerate(heads):
            v = jnp.concatenate([vp_ref[:, sl], vc_ref[:, sl]], axis=0)
            o_ref[:, sl] = _dot_nn(p_buf[hh], v) / den_buf[hh]

    out_spec = _att_spec(d, lambda n: n)
    return pl.pallas_call(
        body, name="att_fwd_g%d" % grp, grid=(d, nb),
        in_specs=_att_qkv_specs(d, nb),
        out_specs=[out_spec, out_spec],
        out_shape=[_sds(_rm_shape(S, d, GW), F32)] * 2,
        scratch_shapes=[pltpu.VMEM((ATT_HPG, BLK, 2 * BLK), F32), pltpu.VMEM((ATT_HPG, BLK, 2 * BLK), CDT),
                        pltpu.VMEM((ATT_HPG, BLK, HD), F32)],
        compiler_params=_params(2),
    )(qkv, qkv, qkv, qkv, qkv)


def _att_bwd(grp, S, qkv, lse, do_g, c_g):
    _, d = ATT_GROUPS[grp]
    L = S // d
    nb = L // BLK
    slopes = _att_slopes(grp)
    scale = HD ** -0.5
    last = nb - 1

    def body(q_ref, kp_ref, kc_ref, vp_ref, vc_ref, l_ref, do_ref, c_ref, dq_ref, dk_ref, dv_ref, ck, cv,
             s_buf, dp_buf, p_buf, ds_buf):
        n = pl.program_id(1)

        @pl.when(n == 0)
        def _():
            ck[...] = jnp.zeros_like(ck)
            cv[...] = jnp.zeros_like(cv)

        @pl.when(n < nb)
        def _():
            valid, distf = _att_mask(n)
            heads = [slice(hh * HD, (hh + 1) * HD) for hh in range(ATT_HPG)]
            for hh, sl in enumerate(heads):
                k = jnp.concatenate([kp_ref[:, sl], kc_ref[:, sl]], axis=0)
                v = jnp.concatenate([vp_ref[:, sl], vc_ref[:, sl]], axis=0)
                s_buf[hh] = _dot_nt(q_ref[:, sl], k)
                dp_buf[hh] = _dot_nt(do_ref[:, sl], v)
            for hh, sl in enumerate(heads):
                s = s_buf[hh] * scale + (-slopes[hh] * d) * distf
                p = jnp.where(valid, jnp.exp(s - l_ref[:, sl][:, 0:1]), 0.0)
                p_buf[hh] = p.astype(CDT)
                ds_buf[hh] = (p * (dp_buf[hh] + c_ref[:, sl][:, 0:1]) * scale).astype(CDT)
            for hh, sl in enumerate(heads):
                k = jnp.concatenate([kp_ref[:, sl], kc_ref[:, sl]], axis=0)
                ds = ds_buf[hh]
                dq_ref[:, sl] = _dot_nn(ds, k)
                dk = _dot_tn(ds, q_ref[:, sl])
                dv = _dot_tn(p_buf[hh], do_ref[:, sl])
                dk_ref[:, sl] = ck[:, sl] + dk[0:BLK]
                dv_ref[:, sl] = cv[:, sl] + dv[0:BLK]
                ck[:, sl] = dk[BLK:2 * BLK]
                cv[:, sl] = dv[BLK:2 * BLK]

        @pl.when(n == nb)
        def _():
            dk_ref[...] = ck[...]
            dv_ref[...] = cv[...]

    blk = (BLK, GW)
    at_q = _att_spec(d, lambda n: jnp.minimum(n, last))
    behind = _att_spec(d, lambda n: jnp.maximum(n - 1, 0))
    return pl.pallas_call(
        body, name="att_bwd_g%d" % grp, grid=(d, nb + 1),
        in_specs=_att_qkv_specs(d, nb) + [at_q, at_q, at_q],
        out_specs=[at_q, behind, behind],
        out_shape=[_sds(_rm_shape(S, d, GW), F32)] * 3,
        scratch_shapes=[pltpu.VMEM(blk, F32), pltpu.VMEM(blk, F32),
                        pltpu.VMEM((ATT_HPG, BLK, 2 * BLK), F32), pltpu.VMEM((ATT_HPG, BLK, 2 * BLK), F32),
                        pltpu.VMEM((ATT_HPG, BLK, 2 * BLK), CDT), pltpu.VMEM((ATT_HPG, BLK, 2 * BLK), CDT)],
        compiler_params=_params(2),
    )(qkv, qkv, qkv, qkv, qkv, lse, do_g, c_g)


def _mix_alpha(l0, l1, l2):
    mx = jnp.maximum(jnp.maximum(l0, l1), l2)
    e = [jnp.exp(l0 - mx), jnp.exp(l1 - mx), jnp.exp(l2 - mx)]
    tot = e[0] + e[1] + e[2]
    return [ei / tot for ei in e]


def _mix_fwd(S, os_, ls_, tm=512):
    dil = [d for _, d in ATT_GROUPS]

    def body(*refs):
        out, bufs = refs[6], refs[7:13]
        get = [_rm_reader(bufs[j], refs[j], dil[j % 3]) for j in range(6)]
        for s in range(ATT_HPG):
            def chunk(rows, s=s):
                al = _mix_alpha(*[get[3 + g](s, rows) for g in range(3)])
                mixed = al[0] * get[0](s, rows) + al[1] * get[1](s, rows) + al[2] * get[2](s, rows)
                out[rows, s * HD:(s + 1) * HD] = mixed.astype(out.dtype)

            _for_row_chunks(tm, chunk)

    specs = [_rm_spec(tm, d, GW) for d in dil]
    return pl.pallas_call(
        body, name="mix_fwd", grid=(S // tm,), in_specs=specs * 2, out_specs=pl.BlockSpec((tm, GW), lambda i: (i, 0)),
        out_shape=_sds((S, GW), CDT), scratch_shapes=[pltpu.VMEM((ATT_HPG, tm, HD), F32)] * 6,
        compiler_params=_params(1))(*os_, *ls_)


def _mix_bwd(S, os_, ls_, do_a, tm=512):
    dil = [d for _, d in ATT_GROUPS]

    def body(*refs):
        d_ref, outs, bufs, tmps = refs[6], refs[7:13], refs[13:19], refs[19:25]
        get = [_rm_reader(bufs[j], refs[j], dil[j % 3]) for j in range(6)]
        for s in range(ATT_HPG):
            cols = slice(s * HD, (s + 1) * HD)

            def chunk(rows, s=s, cols=cols):
                al = _mix_alpha(*[get[3 + g](s, rows) for g in range(3)])
                dv = d_ref[rows, cols]
                o_a = al[0] * get[0](s, rows) + al[1] * get[1](s, rows) + al[2] * get[2](s, rows)
                dsum = jnp.sum(dv * o_a, axis=-1, keepdims=True)
                for g in range(3):
                    tmps[g][rows, :] = al[g] * dv
                    tmps[3 + g][rows, :] = -(al[g] * dsum)

            _for_row_chunks(tm, chunk)
            for j in range(6):
                _rm_put(outs[j], cols, tmps[j], dil[j % 3])

    specs = [_rm_spec(tm, d, GW) for d in dil]
    res = pl.pallas_call(
        body, name="mix_bwd", grid=(S // tm,), in_specs=specs * 2 + [pl.BlockSpec((tm, GW), lambda i: (i, 0))],
        out_specs=specs * 2,
        out_shape=[_sds(_rm_shape(S, d, GW), CDT) for d in dil] + [_sds(_rm_shape(S, d, GW), F32) for d in dil],
        scratch_shapes=[pltpu.VMEM((ATT_HPG, tm, HD), F32)] * 6 + [pltpu.VMEM((tm, HD), F32)] * 6,
        compiler_params=_params(1))(*os_, *ls_, do_a)
    return res[:3], res[3:]


def _ret_tables(dk):
    H, C = RET_HEADS, BLK
    log_g = jnp.log(1.0 - 2.0 ** (-5.0 - jnp.arange(H, dtype=F32)))
    idx = jnp.arange(C, dtype=F32)
    diff = idx[:, None] - idx[None, :]
    decay = jnp.where(diff >= 0, jnp.exp(log_g[:, None, None] * jnp.maximum(diff, 0.0)), 0.0)
    xi = jnp.exp(log_g[:, None] * (idx[None, :] + 1.0))
    zeta = jnp.exp(log_g[:, None] * (C - 1.0 - idx[None, :])) * (dk ** -0.5)
    g_chunk = jnp.exp(log_g * C)
    bc = lambda t: jnp.broadcast_to(t[:, :, None], (H, C, C))
    return decay, bc(xi), bc(zeta), jnp.broadcast_to(g_chunk[:, None, None], (H, 8, C))


def _gn_fwd(o, g, b):
    mu = jnp.mean(o, axis=-1, keepdims=True)
    xc = o - mu
    rstd = lax.rsqrt(jnp.mean(xc * xc, axis=-1, keepdims=True) + EPS)
    yh = xc * rstd
    return yh, rstd, yh * g + b


def _ret_specs(dk, dv, order):
    H = RET_HEADS
    qk_w, v_w = H * dk, H * dv
    off_q = 3 * ATT_W
    off_k, off_v, off_g = off_q + qk_w, off_q + 2 * qk_w, off_q + 2 * qk_w + v_w
    assert 2 * dk == dv and all(off % dv == 0 for off in (off_q, off_k, off_v, off_g))

    def col(off, j):
        return pl.BlockSpec((BLK, dv), lambda i: (order(i), off // dv + j))

    tab = pl.BlockSpec((H, BLK, BLK), lambda i: (0, 0, 0))
    return ([col(off_q, j) for j in range(H // 2)] + [col(off_k, j) for j in range(H // 2)]
            + [col(off_v, j) for j in range(H)] + [col(off_g, j) for j in range(H)]
            + [tab, tab, tab, pl.BlockSpec((H, 8, BLK), lambda i: (0, 0, 0))])


def _ret_heads(refs, dk):
    H = RET_HEADS
    q_refs, k_refs = refs[0:H // 2], refs[H // 2:H]
    v_refs, gr_refs = refs[H:2 * H], refs[2 * H:3 * H]

    def head(h):
        cols = slice((h % 2) * dk, (h % 2 + 1) * dk)
        return q_refs[h // 2][:, cols], k_refs[h // 2][:, cols], v_refs[h][...], gr_refs[h][...]

    return head, refs[3 * H:3 * H + 4]


def _ret_fwd(proj, gn_g, gn_b, dk, dv):
    S = proj.shape[0]
    N = S // BLK
    H = RET_HEADS
    kscale = dk ** -0.5
    n_in = 3 * H + 4

    def body(*refs):
        head, (dec_ref, xi_ref, zeta_ref, gc_ref) = _ret_heads(refs, dk)
        g_ref, b_ref, opre_ref, or_ref, st_ref, state, s_buf, cross_buf = refs[n_in:n_in + 8]
        n = pl.program_id(0)

        @pl.when(n == 0)
        def _():
            state[...] = jnp.zeros_like(state)

        for h in range(H):
            q, k, v, _ = head(h)
            s_buf[h] = _dot_nt(q, k)
            st = state[h]
            st_c = st.astype(CDT)
            st_ref[h] = st_c
            cross_buf[h] = _dot_nn(q, st_c)
            kz = (k.astype(F32) * zeta_ref[h][:, 0:1]).astype(CDT)
            state[h] = st * gc_ref[h][0:1, 0:1] + _dot_tn(kz, v)
        for h in range(H):
            vs = slice(h * dv, (h + 1) * dv)
            _, _, v, gr = head(h)
            s = s_buf[h] * kscale * dec_ref[h]
            o = _dot_nn(s.astype(CDT), v) + cross_buf[h] * xi_ref[h][:, 0:1]
            opre_ref[:, vs] = o
            _, _, y = _gn_fwd(o, g_ref[:, vs], b_ref[:, vs])
            gr = gr.astype(F32)
            or_ref[:, vs] = (y * (gr * _sigmoid(gr))).astype(or_ref.dtype)

    v_w = H * dv
    row = pl.BlockSpec((1, v_w), lambda i: (0, 0))
    tile = pl.BlockSpec((BLK, v_w), lambda i: (i, 0))
    return pl.pallas_call(
        body, name="ret_fwd", grid=(N,),
        in_specs=_ret_specs(dk, dv, lambda i: i) + [row, row],
        out_specs=[tile, tile, pl.BlockSpec((None, H, dk, dv), lambda i: (i, 0, 0, 0))],
        out_shape=[_sds((S, v_w), F32), _sds((S, v_w), CDT), _sds((N, H, dk, dv), CDT)],
        scratch_shapes=[pltpu.VMEM((H, dk, dv), F32), pltpu.VMEM((H, BLK, BLK), F32), pltpu.VMEM((H, BLK, dv), F32)],
        compiler_params=_params(1),
    )(*[proj] * (3 * H), *_ret_tables(dk), gn_g, gn_b)


def _ret_bwd(proj, gn_g, gn_b, o_pre, states, d_or, dga, dgb, dk, dv):
    S, in_w = proj.shape
    N = S // BLK
    H = RET_HEADS
    qk_w, v_w = H * dk, H * dv
    kscale = dk ** -0.5
    n_in = 3 * H + 4
    out_w = 2 * qk_w + 2 * v_w
    gate_w = dga.shape[1]
    col0 = 3 * ATT_W
    assert col0 + out_w + 2 * gate_w == in_w
    rev = lambda i: N - 1 - i

    def body(*refs):
        head, (dec_ref, xi_ref, zeta_ref, gc_ref) = _ret_heads(refs, dk)
        (g_ref, b_ref, opre_ref, st_ref, dor_ref, dga_ref, dgb_ref, dproj_ref, dg_ref, db_ref, dstate, stage,
         sem, do_buf, dox_buf, a_buf, g_buf, dq_buf, dk_buf, dv_buf) = refs[n_in:n_in + 20]
        i = pl.program_id(0)
        slot = i % 2
        out_ref = stage.at[slot]

        def out_copy(s, step):
            rows = pl.ds(pl.multiple_of(rev(step) * BLK, BLK), BLK)
            return pltpu.make_async_copy(stage.at[s], dproj_ref.at[rows, pl.ds(col0, in_w - col0)], sem.at[s])

        @pl.when(i >= 2)
        def _():
            out_copy(slot, i - 2).wait()

        @pl.when(i == 0)
        def _():
            dstate[...] = jnp.zeros_like(dstate)
            dg_ref[...] = jnp.zeros_like(dg_ref)
            db_ref[...] = jnp.zeros_like(db_ref)

        out_ref[:, out_w:out_w + gate_w] = dga_ref[...]
        out_ref[:, out_w + gate_w:out_w + 2 * gate_w] = dgb_ref[...]
        for h in range(H):
            vs = slice(h * dv, (h + 1) * dv)
            _, _, _, gr = head(h)
            gr = gr.astype(F32)
            sg = _sigmoid(gr)
            gain = g_ref[:, vs]
            yh, rstd, y = _gn_fwd(opre_ref[:, vs], gain, b_ref[:, vs])
            d_or_v = dor_ref[:, vs]
            dy = d_or_v * (gr * sg)
            out_ref[:, 2 * qk_w + v_w + h * dv:2 * qk_w + v_w + (h + 1) * dv] = (
                d_or_v * y * (sg * (1.0 + gr * (1.0 - sg)))).astype(out_ref.dtype)
            dg_ref[:, vs] += jnp.sum(dy * yh, axis=0, keepdims=True)
            db_ref[:, vs] += jnp.sum(dy, axis=0, keepdims=True)
            dyh = dy * gain
            do = rstd * (dyh - jnp.mean(dyh, axis=-1, keepdims=True)
                         - yh * jnp.mean(dyh * yh, axis=-1, keepdims=True))
            do_buf[h] = do.astype(CDT)
            dox_buf[h] = (do * xi_ref[h][:, 0:1]).astype(CDT)
        for h in range(H):
            q, k, v, _ = head(h)
            dox = dox_buf[h]
            a_buf[h] = _dot_nt(q, k)
            g_buf[h] = _dot_nt(do_buf[h], v)
            dsn = dstate[h]
            dsn_c = dsn.astype(CDT)
            kz = (k.astype(F32) * zeta_ref[h][:, 0:1]).astype(CDT)
            dq_buf[h] = _dot_nt(dox, st_ref[h])
            dk_buf[h] = _dot_nt(v, dsn_c)
            dv_buf[h] = _dot_nn(kz, dsn_c)
            dstate[h] = dsn * gc_ref[h][0:1, 0:1] + _dot_tn(q, dox)
        for h in range(H):
            q, k, _, _ = head(h)
            decay = dec_ref[h]
            a_c = (a_buf[h] * kscale * decay).astype(CDT)
            g_c = (g_buf[h] * decay).astype(CDT)
            dq = _dot_nn(g_c, k) * kscale + dq_buf[h]
            dkk = _dot_tn(g_c, q) * kscale + dk_buf[h] * zeta_ref[h][:, 0:1]
            dvv = _dot_tn(a_c, do_buf[h]) + dv_buf[h]
            out_ref[:, h * dk:(h + 1) * dk] = dq.astype(out_ref.dtype)
            out_ref[:, qk_w + h * dk:qk_w + (h + 1) * dk] = dkk.astype(out_ref.dtype)
            out_ref[:, 2 * qk_w + h * dv:2 * qk_w + (h + 1) * dv] = dvv.astype(out_ref.dtype)

        cp = out_copy(slot, i)
        cp.start()

        @pl.when(i == N - 1)
        def _():
            cp.wait()
            if N >= 2:
                out_copy(1 - slot, i - 1).wait()

    row = pl.BlockSpec((1, v_w), lambda i: (0, 0))
    tile = pl.BlockSpec((BLK, v_w), lambda i: (rev(i), 0))
    gate = pl.BlockSpec((BLK, gate_w), lambda i: (rev(i), 0))
    return pl.pallas_call(
        body, name="ret_bwd", grid=(N,),
        in_specs=_ret_specs(dk, dv, rev) + [row, row, tile,
                 pl.BlockSpec((None, H, dk, dv), lambda i: (rev(i), 0, 0, 0)), tile, gate, gate],
        out_specs=[pl.BlockSpec(memory_space=pl.ANY), row, row],
        out_shape=[_sds((S, in_w), CDT), _sds((1, v_w), F32), _sds((1, v_w), F32)],
        scratch_shapes=[pltpu.VMEM((H, dk, dv), F32), pltpu.VMEM((2, BLK, in_w - col0), CDT),
                        pltpu.SemaphoreType.DMA((2,)),
                        pltpu.VMEM((H, BLK, dv), CDT), pltpu.VMEM((H, BLK, dv), CDT),
                        pltpu.VMEM((H, BLK, BLK), F32), pltpu.VMEM((H, BLK, BLK), F32),
                        pltpu.VMEM((H, BLK, dk), F32), pltpu.VMEM((H, BLK, dk), F32), pltpu.VMEM((H, BLK, dv), F32)],
        compiler_params=_params(1),
    )(*[proj] * (3 * H), *_ret_tables(dk), gn_g, gn_b, o_pre, states, d_or, dga, dgb)


def _merge_fwd(o_a, o_r, wa, wb, proj, d_model, tm=512, tn=512):
    S, in_w = proj.shape
    off_a, off_b = in_w - 2 * d_model, in_w - d_model
    assert off_a % tn == 0 and off_b % tn == 0

    def body(oa_ref, or_ref, wa_ref, wb_ref, ga_ref, gb_ref, y_ref, pa_ref, pb_ref):
        pa = _dot_nn(oa_ref[...], wa_ref[...])
        pb = _dot_nn(or_ref[...], wb_ref[...])
        y = _sigmoid(ga_ref[...].astype(F32)) * pa + _sigmoid(gb_ref[...].astype(F32)) * pb
        y_ref[...] = y.astype(y_ref.dtype)
        pa_ref[...] = pa.astype(pa_ref.dtype)
        pb_ref[...] = pb.astype(pb_ref.dtype)

    ka, kb = o_a.shape[1], o_r.shape[1]
    out = pl.BlockSpec((tm, tn), lambda i, j: (i, j))
    return pl.pallas_call(
        body, name="merge_fwd", grid=(S // tm, d_model // tn),
        in_specs=[pl.BlockSpec((tm, ka), lambda i, j: (i, 0)), pl.BlockSpec((tm, kb), lambda i, j: (i, 0)),
                  pl.BlockSpec((ka, tn), lambda i, j: (0, j)), pl.BlockSpec((kb, tn), lambda i, j: (0, j)),
                  pl.BlockSpec((tm, tn), lambda i, j: (i, off_a // tn + j)),
                  pl.BlockSpec((tm, tn), lambda i, j: (i, off_b // tn + j))],
        out_specs=[out, out, out], out_shape=[_sds((S, d_model), CDT)] * 3,
        compiler_params=_params(2))(o_a, o_r, wa, wb, proj, proj)


def _local_step(x, target, w_in_mine, chip, others, far_w_in, small, late_weights, on_grads, deps0=()):
    S, D = x.shape
    ns_in = w_in_mine.shape[1]
    in_w = N_CHIPS * ns_in
    d_ff = 4 * D
    ret_v_w = 2 * D
    dv = ret_v_w // RET_HEADS
    dk = (in_w - 3 * ATT_W - 2 * ret_v_w - 2 * D) // (2 * RET_HEADS)
    gqk = jnp.concatenate([small["q_norm_g"].reshape(1, ATT_W), small["k_norm_g"].reshape(1, ATT_W)], axis=1)
    g1, g2 = small["norm1_g"], small["norm2_g"]
    gn_g, gn_b = small["ret_gn_g"], small["ret_gn_b"]

    xn = _rms_fwd("rms1_fwd", x, g1)
    proj_sds = _sds((S, in_w), CDT)
    (proj_mine,) = _matmul(
        "in_proj_mine", "nn", xn, w_in_mine, tm=512, tn=ns_in, tk=D, prefetch=[chip],
        outs=[(proj_sds, pl.BlockSpec((512, ns_in), lambda i, j, k, c: (i, c[0])))], epilogue=_ep_store, deps=deps0)
    w_in = far_w_in(proj_mine)
    (proj,) = _matmul(
        "in_proj_far", "nn", xn, w_in, tm=512, tn=ns_in, tk=D, prefetch=[others], n_cols=(N_CHIPS - 1) * ns_in,
        b_spec=pl.BlockSpec((None, D, ns_in), lambda i, j, k, o: (o[j], 0, 0)),
        outs=[(proj_sds, pl.BlockSpec((512, ns_in), lambda i, j, k, o: (i, o[j])))], epilogue=_ep_store,
        deps=[proj_mine], alias_dep_to_out=(0, 0), j_outer=True)
    qkv = _qknorm_fwd(proj, gqk)
    att = [_att_fwd(g, S, qkv[g]) for g in range(3)]
    os_, ls_ = [a[0] for a in att], [a[1] for a in att]
    o_a = _mix_fwd(S, os_, ls_)
    o_pre, o_r, states = _ret_fwd(proj, gn_g, gn_b, dk, dv)
    w = late_weights(o_r)
    y, pa, pb = _merge_fwd(o_a, o_r, w["w_proj_a"], w["w_proj_b"], proj, D)
    (x1,) = _matmul("out_proj", "nn", y, w["w_out"], tm=512, tn=D, tk=D,
                    extras=[(x, _mn(512, D))], outs=[(_sds((S, D), F32), _mn(512, D))], epilogue=_ep_resid)
    xn2 = _rms_fwd("rms2_fwd", x1, g2)
    hid, act = _matmul("mlp_up", "nn", xn2, w["w_up"], tm=512, tn=2048, tk=D, j_outer=True,
                       outs=[(_sds((S, d_ff), CDT), _mn(512, 2048))] * 2, epilogue=_ep_up)
    dx2, dx2c, loss_row = _matmul(
        "mlp_down_loss", "nn", act, w["w_down"], tm=512, tn=D, tk=d_ff,
        extras=[(x1, _mn(512, D)), (target, _mn(512, D))],
        outs=[(_sds((S, D), F32), _mn(512, D)), (_sds((S, D), CDT), _mn(512, D)), (_sds((1, D), F32), _row(D))],
        epilogue=functools.partial(_ep_down_loss, inv_d=1.0 / D))
    loss = 0.5 * jnp.sum(loss_row) / D

    (dh,) = _matmul("d_hidden", "nt", dx2c, w["w_down"], tm=512, tn=2048, tk=D, j_outer=True,
                    extras=[(hid, _mn(512, 2048))], outs=[(_sds((S, d_ff), CDT), _mn(512, 2048))], epilogue=_ep_dh)
    (gw_down,) = _matmul("dw_down", "tn", act, dx2c, tm=1024, tn=D, tk=1024,
                         outs=[(_sds((d_ff, D), F32), _mn(1024, D))], epilogue=_ep_store)
    (gw_up,) = _matmul("dw_up", "tn", xn2, dh, tm=D, tn=1024, tk=1024,
                       outs=[(_sds((D, d_ff), F32), _mn(D, 1024))], epilogue=_ep_store)
    tok = on_grads({"w_down": gw_down, "w_up": gw_up})
    dx1, dx1c, dg2 = _matmul(
        "d_x1", "nt", dh, w["w_up"], tm=512, tn=D, tk=d_ff,
        extras=[(x1, _mn(512, D)), (g2, _row(D)), (dx2, _mn(512, D))],
        outs=[(_sds((S, D), F32), _mn(512, D)), (_sds((S, D), CDT), _mn(512, D)), (_sds((1, D), F32), _row(D))],
        epilogue=_ep_rms_bwd, deps=[tok])

    gt = 512
    assert (in_w - 2 * D) % gt == 0
    off_a, off_b = (in_w - 2 * D) // gt, (in_w - D) // gt
    dpa, dpb, dga, dgb = _matmul(
        "d_gates", "nt", dx1c, w["w_out"], tm=512, tn=gt, tk=D,
        extras=[(proj, _mn(512, gt, off_a)), (proj, _mn(512, gt, off_b)), (pa, _mn(512, gt)), (pb, _mn(512, gt))],
        outs=[(_sds((S, D), CDT), _mn(512, gt))] * 4, epilogue=_ep_gates)
    (gw_out,) = _matmul("dw_out", "tn", y, dx1c, tm=D, tn=D, tk=1024,
                        outs=[(_sds((D, D), F32), _mn(D, D))], epilogue=_ep_store)
    (gw_pa,) = _matmul("dw_proj_a", "tn", o_a, dpa, tm=GW, tn=D, tk=1024,
                       outs=[(_sds((GW, D), F32), _mn(GW, D))], epilogue=_ep_store)
    (gw_pb,) = _matmul("dw_proj_b", "tn", o_r, dpb, tm=1024, tn=D, tk=1024,
                       outs=[(_sds((ret_v_w, D), F32), _mn(1024, D))], epilogue=_ep_store)
    (do_a,) = _matmul("d_o_a", "nt", dpa, w["w_proj_a"], tm=1024, tn=GW, tk=D,
                      outs=[(_sds((S, GW), F32), _mn(1024, GW))], epilogue=_ep_store)
    tok = on_grads({"w_out": gw_out, "w_proj_a": gw_pa, "w_proj_b": gw_pb})
    (d_or,) = _matmul("d_o_r", "nt", dpb, w["w_proj_b"], tm=512, tn=ret_v_w, tk=D,
                      outs=[(_sds((S, ret_v_w), F32), _mn(512, ret_v_w))], epilogue=_ep_store, deps=[tok])

    dproj, dgn_g, dgn_b = _ret_bwd(proj, gn_g, gn_b, o_pre, states, d_or, dga, dgb, dk, dv)
    do_gs, c_gs = _mix_bwd(S, os_, ls_, do_a)
    datt_parts = [_att_bwd(g, S, qkv[g], ls_[g], do_gs[g], c_gs[g]) for g in range(3)]
    dproj, dgqk = _qknorm_bwd(proj, gqk, [p[0] for p in datt_parts], [p[1] for p in datt_parts],
                              [p[2] for p in datt_parts], dproj)

    (gw_in,) = _matmul(
        "dw_in", "tn", xn, dproj, tm=512, tn=ns_in, tk=1024,
        outs=[(_sds((N_CHIPS, D, ns_in), F32), pl.BlockSpec((None, 512, ns_in), lambda i, j, k: (j, i, 0)))],
        epilogue=_ep_store)
    tok = on_grads({"w_in": gw_in})
    grad_x, _, dg1 = _matmul(
        "d_x", "nt", dproj, w_in, tm=512, tn=D, tk=ns_in, n_cols=D,
        b_spec=pl.BlockSpec((None, D, ns_in), lambda i, j, k: (k, 0, 0)),
        extras=[(x, _mn(512, D)), (g1, _row(D)), (dx1, _mn(512, D))],
        outs=[(_sds((S, D), F32), _mn(512, D)), (_sds((S, D), CDT), _mn(512, D)), (_sds((1, D), F32), _row(D))],
        epilogue=_ep_rms_bwd, deps=[tok])

    smallg = {"norm1_g": dg1, "q_norm_g": dgqk[:, :ATT_W], "k_norm_g": dgqk[:, ATT_W:],
              "ret_gn_g": dgn_g, "ret_gn_b": dgn_b, "norm2_g": dg2}
    return loss, grad_x, smallg


N_CHIPS = 4
N_DEV = 8


def _place():
    x, y, c = lax.axis_index("x"), lax.axis_index("y"), lax.axis_index("c")
    return x, y, c


def _other_chips(x, y):
    out = []
    for fx, fy in ((1, 0), (0, 1), (1, 1)):
        px = 1 - x if fx else x
        py = 1 - y if fy else y
        out.append(((px, py), 2 * px + py))
    return out


SEM_SPEC = pl.BlockSpec(memory_space=pltpu.SEMAPHORE)
ANY_SPEC = pl.BlockSpec(memory_space=pl.ANY)
EFFECT = pltpu.SideEffectType.DATAFLOW_SIDE_EFFECTING


def _ici_copies(kind, srcs, lands, send, recv):
    x, y, c = _place()
    me = 2 * x + y
    out = []
    for w, (s, l) in enumerate(zip(srcs, lands)):
        for j, ((px, py), pidx) in enumerate(_other_chips(x, y)):
            if kind == "gather":
                half = s.shape[0] // 2
                rows = pl.ds(c * half, half)
                src, dst_there, dst_here = s.at[rows, :], l.at[me, rows, :], l.at[pidx, rows, :]
            else:
                src, dst_there, dst_here = s.at[pidx], l.at[me], l.at[pidx]
            out.append((src, dst_there, dst_here, send.at[3 * w + j], recv.at[3 * w + j], (px, py, c)))
    return out


def _exchange_start(name, kind, srcs, land_shapes):
    n = len(srcs)

    def body(*refs):
        src_refs, land_refs = refs[:n], refs[n:2 * n]
        send, recv = refs[2 * n], refs[2 * n + 1]
        token = refs[-1]
        for src, dst, _, ss, rs, dev in _ici_copies(kind, src_refs, land_refs, send, recv):
            pltpu.make_async_remote_copy(src_ref=src, dst_ref=dst, send_sem=ss, recv_sem=rs, device_id=dev,
                                         device_id_type=MESH).start()
        token[...] = jnp.zeros_like(token)

    thru = [pltpu.HBM(s.shape, s.dtype) for s in srcs] + [pltpu.HBM(shape, dtype) for shape, dtype in land_shapes]
    res = pl.pallas_call(
        body, name=name,
        out_shape=(pltpu.SemaphoreType.DMA((3 * n,)), pltpu.SemaphoreType.DMA((3 * n,)), *thru, _sds((8, LANES), F32)),
        in_specs=[HBM_SPEC] * (2 * n), out_specs=(SEM_SPEC, SEM_SPEC, *[HBM_SPEC] * (2 * n), VMEM_SPEC),
        input_output_aliases={i: 2 + i for i in range(2 * n)},
        compiler_params=pltpu.CompilerParams(has_side_effects=EFFECT),
    )(*[pltpu.with_memory_space_constraint(s, pltpu.HBM) for s in srcs],
      *[pltpu.with_memory_space_constraint(lax.empty(shape, dtype), pltpu.HBM) for shape, dtype in land_shapes])
    return res[0], res[1], list(res[2:2 + n]), list(res[2 + n:2 + 2 * n]), res[-1]


def _exchange_wait(name, kind, send, recv, srcs, lands, after):
    n = len(srcs)

    def body(*refs):
        src_refs, land_refs = refs[:n], refs[n:2 * n]
        send_ref, recv_ref = refs[2 * n], refs[2 * n + 1]
        for src, _, dst, ss, rs, dev in _ici_copies(kind, src_refs, land_refs, send_ref, recv_ref):
            cp = pltpu.make_async_remote_copy(src_ref=src, dst_ref=dst, send_sem=ss, recv_sem=rs, device_id=dev,
                                              device_id_type=MESH)
            cp.wait_send()
            cp.wait_recv()

    thru = [pltpu.HBM(t.shape, t.dtype) for t in list(srcs) + list(lands)]
    res = pl.pallas_call(
        body, name=name, out_shape=thru,
        in_specs=[HBM_SPEC] * (2 * n) + [SEM_SPEC, SEM_SPEC, ANY_SPEC], out_specs=[HBM_SPEC] * (2 * n),
        input_output_aliases={i: i for i in range(2 * n)},
        compiler_params=pltpu.CompilerParams(has_side_effects=EFFECT),
    )(*srcs, *lands, send, recv, after)
    return list(res[:n]), list(res[n:])


PAIR_TILE_ELEMS = 1 << 19


def _pair_fill(name, gathered, mine, core, others, chip):
    k, r, C = gathered.shape
    half = r // 2
    tr = _row_tile(half, C, PAIR_TILE_ELEMS, mult=16)
    nt = half // tr
    n_far = N_CHIPS - 1

    def body(c_ref, o_ref, chip_ref, in_ref, mine_ref, out_ref, slot, send, recv):
        j = pl.program_id(0)
        b = (j * nt + pl.program_id(1)) % 2
        x, y, c = _place()
        cp = pltpu.make_async_remote_copy(src_ref=in_ref, dst_ref=slot.at[b], send_sem=send.at[b],
                                          recv_sem=recv.at[b], device_id=(x, y, 1 - c), device_id_type=MESH)

        @pl.when(j < n_far)
        def _():
            cp.start()
            cp.wait_recv()
            out_ref[...] = slot[b]
            cp.wait_send()

        @pl.when(j >= n_far)
        def _():
            out_ref[...] = mine_ref[...]

    def far(j):
        return jnp.minimum(j, n_far - 1)

    grid_spec = pltpu.PrefetchScalarGridSpec(
        num_scalar_prefetch=3, grid=(n_far + 2, nt),
        in_specs=[pl.BlockSpec((tr, C), lambda j, i, c, o, m: (
                      (2 * o[far(j)] + c[0]) * nt + jnp.where(j < n_far, i, nt - 1), 0)),
                  pl.BlockSpec((tr, C), lambda j, i, c, o, m: (jnp.where(j < n_far, 0, (j - n_far) * nt + i), 0))],
        out_specs=pl.BlockSpec((tr, C), lambda j, i, c, o, m: (
            jnp.where(j < n_far, 2 * o[far(j)] + 1 - c[0], 2 * m[0] + j - n_far) * nt + i, 0)),
        scratch_shapes=[pltpu.VMEM((2, tr, C), gathered.dtype), pltpu.SemaphoreType.DMA((2,)),
                        pltpu.SemaphoreType.DMA((2,))])
    out = pl.pallas_call(body, name=name, grid_spec=grid_spec, out_shape=_sds((k * r, C), gathered.dtype),
                         input_output_aliases={3: 0}, compiler_params=_params(2))(
                             core, others, chip, gathered.reshape(k * r, C), mine)
    return out.reshape(k, r, C)


def _pair_reduce(name, g, core):
    k, R, C = g.shape
    half = R // 2
    tr = _row_tile(half, C, PAIR_TILE_ELEMS, mult=16)
    nt = half // tr

    def body(c_ref, mine_ref, give_ref, out_ref, wire_ref, stage, slot, send, recv):
        b = (pl.program_id(0) * nt + pl.program_id(1)) % 2
        x, y, c = _place()
        stage[b] = give_ref[...].astype(stage.dtype)
        cp = pltpu.make_async_remote_copy(src_ref=stage.at[b], dst_ref=slot.at[b], send_sem=send.at[b],
                                          recv_sem=recv.at[b], device_id=(x, y, 1 - c), device_id_type=MESH)
        cp.start()
        cp.wait_recv()
        tot = mine_ref[...] + slot[b].astype(F32)
        out_ref[...] = tot
        wire_ref[...] = tot.astype(wire_ref.dtype)
        cp.wait_send()

    blk = (tr, C)
    out_spec = pl.BlockSpec(blk, lambda s, i, c: (s * nt + i, 0))
    grid_spec = pltpu.PrefetchScalarGridSpec(
        num_scalar_prefetch=1, grid=(k, nt),
        in_specs=[pl.BlockSpec(blk, lambda s, i, c: ((2 * s + c[0]) * nt + i, 0)),
                  pl.BlockSpec(blk, lambda s, i, c: ((2 * s + 1 - c[0]) * nt + i, 0))],
        out_specs=[out_spec, out_spec],
        scratch_shapes=[pltpu.VMEM((2, tr, C), CDT), pltpu.VMEM((2, tr, C), CDT), pltpu.SemaphoreType.DMA((2,)),
                        pltpu.SemaphoreType.DMA((2,))])
    g2 = g.reshape(k * R, C)
    out, wire = pl.pallas_call(body, name=name, grid_spec=grid_spec,
                               out_shape=[_sds((k * half, C), F32), _sds((k * half, C), CDT)],
                               compiler_params=_params(2))(core, g2, g2)
    return out, wire.reshape(k, half, C)


def _all_reduce_small(v):
    r, cdim = v.shape

    def body(v_ref, o_ref, buf, send, recv):
        x, y, c = _place()
        me = 4 * x + 2 * y + c
        buf[me] = v_ref[...]
        sends = []
        for m in range(1, N_DEV):
            px = 1 - x if m & 4 else x
            py = 1 - y if m & 2 else y
            pc = 1 - c if m & 1 else c
            cp = pltpu.make_async_remote_copy(src_ref=v_ref, dst_ref=buf.at[me], send_sem=send.at[m - 1],
                                              recv_sem=recv.at[m - 1], device_id=(px, py, pc), device_id_type=MESH)
            cp.start()
            sends.append((cp, 4 * px + 2 * py + pc))
        for m, (cp, pidx) in enumerate(sends):
            pltpu.make_async_remote_copy(src_ref=v_ref, dst_ref=buf.at[pidx], send_sem=send.at[m], recv_sem=recv.at[m],
                                         device_id=(x, y, c), device_id_type=MESH).wait_recv()
        for cp, _ in sends:
            cp.wait_send()
        tot = buf[0]
        for k in range(1, N_DEV):
            tot = tot + buf[k]
        o_ref[...] = tot

    return pl.pallas_call(
        body, name="all_reduce_small", in_specs=[VMEM_SPEC], out_specs=VMEM_SPEC,
        out_shape=_sds((r, cdim), F32),
        scratch_shapes=[pltpu.VMEM((N_DEV, r, cdim), F32), pltpu.SemaphoreType.DMA((N_DEV - 1,)),
                        pltpu.SemaphoreType.DMA((N_DEV - 1,))],
    )(v)


def _row_tile(rows, cols, budget_elems=1 << 18, mult=8):
    if rows % mult:
        return rows
    t = max(mult, (budget_elems // cols) // mult * mult)
    while rows % t:
        t -= mult
    return t


def _adamw_update(w, g, m, v):
    nm = ADAM_B1 * m + (1.0 - ADAM_B1) * g
    nv = ADAM_B2 * v + (1.0 - ADAM_B2) * (g * g)
    m_hat = nm / (1.0 - ADAM_B1 ** ADAM_STEP)
    v_hat = nv / (1.0 - ADAM_B2 ** ADAM_STEP)
    return -ADAM_LR * (m_hat / (jnp.sqrt(v_hat) + ADAM_EPS) + ADAM_WD * w), nm, nv


def _adamw(name, w, g, m, v):
    R, C = w.shape
    tr = _row_tile(R, C, 1 << 17)

    def body(w_ref, g_ref, m_ref, v_ref, d_ref, nm_ref, nv_ref):
        d_ref[...], nm_ref[...], nv_ref[...] = _adamw_update(w_ref[...], g_ref[...], m_ref[...], v_ref[...])

    spec = pl.BlockSpec((tr, C), lambda i: (i, 0))
    return pl.pallas_call(body, name=name, grid=(R // tr,), in_specs=[spec] * 4, out_specs=[spec] * 3,
                          out_shape=[_sds((R, C), F32)] * 3, compiler_params=_params(1))(w, g, m, v)


def _sum_share(name, own, by_chip, chip, others, core):
    k, half, C = by_chip.shape
    tr = _row_tile(half, C, mult=16)
    nt = half // tr

    def body(chip_ref, oth_ref, c_ref, own_ref, a_ref, b_ref, cc_ref, g_out, mine, slot, send, recv):
        p = pl.program_id(1)
        b = pl.program_id(0) % 2
        x, y, c = _place()
        cp = pltpu.make_async_remote_copy(src_ref=mine.at[b], dst_ref=slot.at[b], send_sem=send.at[b],
                                          recv_sem=recv.at[b], device_id=(x, y, 1 - c), device_id_type=MESH)

        @pl.when(p == 0)
        def _():
            tot = ((own_ref[...] + a_ref[...].astype(F32)) + b_ref[...].astype(F32)) + cc_ref[...].astype(F32)
            mine[b] = tot
            cp.start()
            g_out[...] = tot

        @pl.when(p == 1)
        def _():
            cp.wait_recv()
            g_out[...] = slot[b]
            cp.wait_send()

    def piece(j):
        return pl.BlockSpec((tr, C), lambda i, p, chip, oth, c: (oth[j] * nt + i, 0))

    grid_spec = pltpu.PrefetchScalarGridSpec(
        num_scalar_prefetch=3, grid=(nt, 2),
        in_specs=[pl.BlockSpec((tr, C), lambda i, p, chip, oth, c: (chip[0] * nt + i, 0)),
                  piece(0), piece(1), piece(2)],
        out_specs=pl.BlockSpec((tr, C), lambda i, p, chip, oth, c: (
            jnp.where(p == 0, c[0], 1 - c[0]) * nt + i, 0)),
        scratch_shapes=[pltpu.VMEM((2, tr, C), F32), pltpu.VMEM((2, tr, C), F32), pltpu.SemaphoreType.DMA((2,)),
                        pltpu.SemaphoreType.DMA((2,))])
    by2 = by_chip.reshape(k * half, C)
    return pl.pallas_call(body, name=name, grid_spec=grid_spec, out_shape=_sds((2 * half, C), F32),
                          compiler_params=_params(2))(chip, others, core, own, by2, by2, by2)


BIG = ("w_in", "w_proj_a", "w_proj_b", "w_out", "w_up", "w_down")
COL_SHARDED = ("w_in", "w_proj_a", "w_up")
SMALL = ("norm1_g", "q_norm_g", "k_norm_g", "ret_gn_g", "ret_gn_b", "norm2_g")
ALL_W = ("norm1_g", "w_in", "q_norm_g", "k_norm_g", "ret_gn_g", "ret_gn_b", "w_proj_a", "w_proj_b", "w_out",
         "norm2_g", "w_up", "w_down")
LANES = 128


def _to_full(name, gathered):
    k, r, c = gathered.shape
    if name in COL_SHARDED:
        return gathered.transpose(1, 0, 2).reshape(r, k * c)
    return gathered.reshape(k * r, c)


def _to_shard_major(name, full):
    if name in COL_SHARDED:
        r, c4 = full.shape
        return full.reshape(r, N_CHIPS, c4 // N_CHIPS).transpose(1, 0, 2)
    r4, c = full.shape
    return full.reshape(N_CHIPS, r4 // N_CHIPS, c)


def kernel(x, norm1_g, w_in, q_norm_g, k_norm_g, ret_gn_g, ret_gn_b, w_proj_a, w_proj_b, w_out, norm2_g, w_up, w_down, loss_target, m_norm1_g, m_w_in, m_q_norm_g, m_k_norm_g, m_ret_gn_g, m_ret_gn_b, m_w_proj_a, m_w_proj_b, m_w_out, m_norm2_g, m_w_up, m_w_down, v_norm1_g, v_w_in, v_q_norm_g, v_k_norm_g, v_ret_gn_g, v_ret_gn_b, v_w_proj_a, v_w_proj_b, v_w_out, v_norm2_g, v_w_up, v_w_down):
    weights = dict(norm1_g=norm1_g, w_in=w_in, q_norm_g=q_norm_g, k_norm_g=k_norm_g, ret_gn_g=ret_gn_g,
                   ret_gn_b=ret_gn_b, w_proj_a=w_proj_a, w_proj_b=w_proj_b, w_out=w_out, norm2_g=norm2_g,
                   w_up=w_up, w_down=w_down)
    moments_m = dict(norm1_g=m_norm1_g, w_in=m_w_in, q_norm_g=m_q_norm_g, k_norm_g=m_k_norm_g, ret_gn_g=m_ret_gn_g,
                     ret_gn_b=m_ret_gn_b, w_proj_a=m_w_proj_a, w_proj_b=m_w_proj_b, w_out=m_w_out,
                     norm2_g=m_norm2_g, w_up=m_w_up, w_down=m_w_down)
    moments_v = dict(norm1_g=v_norm1_g, w_in=v_w_in, q_norm_g=v_q_norm_g, k_norm_g=v_k_norm_g, ret_gn_g=v_ret_gn_g,
                     ret_gn_b=v_ret_gn_b, w_proj_a=v_w_proj_a, w_proj_b=v_w_proj_b, w_out=v_w_out,
                     norm2_g=v_norm2_g, w_up=v_w_up, w_down=v_w_down)

    mx, my = lax.axis_index("x"), lax.axis_index("y")
    core = lax.axis_index("c").astype(jnp.int32).reshape(1)
    chip = (2 * mx + my).astype(jnp.int32).reshape(1)
    others = jnp.stack([2 * (1 - mx) + my, 2 * mx + 1 - my, 2 * (1 - mx) + 1 - my]).astype(jnp.int32)
    shards = {n: weights[n][0].astype(CDT) for n in BIG}
    def start_gather(name, names):
        return _exchange_start(name, "gather", [shards[n] for n in names],
                               [((N_CHIPS,) + shards[n].shape, CDT) for n in names])

    i_send, i_recv, i_srcs, i_lands, i_token = start_gather("gather_w_in_start", ["w_in"])
    late = [n for n in BIG if n != "w_in"]
    l_send, l_recv, l_srcs, l_lands, l_token = start_gather("gather_late_start", late)

    def far_w_in(after):
        srcs, lands = _exchange_wait("gather_w_in_wait", "gather", i_send, i_recv, i_srcs, i_lands, after)
        return _pair_fill("pair_fill_w_in", lands[0], srcs[0], core, others, chip)

    def late_weights(after):
        srcs, lands = _exchange_wait("gather_late_wait", "gather", l_send, l_recv, l_srcs, l_lands, after)
        out = {}
        for n, mine, land in zip(late, srcs, lands):
            out[n] = _to_full(n, _pair_fill("pair_fill_%s" % n, land, mine, core, others, chip))
        return out

    pending = []

    def on_grads(group):
        names = list(group)
        red = [_pair_reduce("pair_reduce_%s" % n, g if g.ndim == 3 else _to_shard_major(n, g), core)
               for n, g in group.items()]
        wires = [wire for _, wire in red]
        send, recv, srcs, lands, token = _exchange_start(
            "scatter_start_%s" % names[0], "scatter", wires, [(wire.shape, wire.dtype) for wire in wires])
        pending.append((names, [own for own, _ in red], send, recv, srcs, lands))
        return token

    small = {n: weights[n].reshape(1, -1) for n in SMALL}

    loss, grad_x, small_g = _local_step(x[0], loss_target[0], i_srcs[0], chip, others, far_w_in, small,
                                        late_weights, on_grads, deps0=[i_token, l_token])
    loss = lax.psum(loss, ("x", "y", "c"))

    out_g, out_d, out_m, out_v = {}, {}, {}, {}
    for names, owns, send, recv, srcs, lands in pending:
        _, got = _exchange_wait("scatter_wait_%s" % names[0], "scatter", send, recv, srcs, lands, grad_x)
        for n, own, by_chip in zip(names, owns, got):
            shape = weights[n].shape
            g2 = _sum_share("sum_share_%s" % n, own, by_chip, chip, others, core)
            d, nm, nv = _adamw("adamw_%s" % n, weights[n][0], g2, moments_m[n][0], moments_v[n][0])
            out_g[n], out_d[n], out_m[n], out_v[n] = (t.reshape(shape) for t in (g2, d, nm, nv))

    packed = jnp.concatenate([small_g[n].reshape(1, -1) for n in SMALL], axis=1)
    red = _all_reduce_small(packed.reshape(-1, LANES)).reshape(1, -1)
    off = 0
    for n in SMALL:
        shape = weights[n].shape
        row = (1, weights[n].size)
        g2 = red[:, off:off + row[1]]
        off += row[1]
        d, nm, nv = _adamw("adamw_%s" % n, weights[n].reshape(row), g2, moments_m[n].reshape(row),
                           moments_v[n].reshape(row))
        out_g[n], out_d[n], out_m[n], out_v[n] = (t.reshape(shape) for t in (g2, d, nm, nv))

    return (loss, grad_x[None], *[out_g[n] for n in ALL_W], *[out_d[n] for n in ALL_W],
            *[out_m[n] for n in ALL_W], *[out_v[n] for n in ALL_W])
```

```python
import functools
import math

import jax
import jax.numpy as jnp
from jax import lax
from jax.experimental import pallas as pl
from jax.experimental.pallas import tpu as pltpu

CDT = jnp.bfloat16
F32 = jnp.float32
EPS = 1e-6

ATT_GROUPS = ((128, 1), (512, 4), (2048, 16))
ATT_HPG = 4
ATT_HEADS = 12
HD = 128
BLK = 128
ATT_W = ATT_HEADS * HD
GW = ATT_HPG * HD
RET_HEADS = 4

ADAM_LR = 0.001
ADAM_B1 = 0.9
ADAM_B2 = 0.999
ADAM_EPS = 1e-08
ADAM_WD = 0.01
ADAM_STEP = 10

VMEM_LIMIT_BYTES = 56 * 1024 * 1024
MXU_DIM = 256
MESH = pl.DeviceIdType.MESH
HBM_SPEC = pl.BlockSpec(memory_space=pltpu.HBM)
VMEM_SPEC = pl.BlockSpec(memory_space=pltpu.VMEM)


def _params(n_axes):
    return pltpu.CompilerParams(dimension_semantics=("arbitrary",) * n_axes,
                                vmem_limit_bytes=VMEM_LIMIT_BYTES)


def _dot_nn(a, b):
    return jnp.dot(a, b, preferred_element_type=F32)


def _dot_nt(a, b):
    return lax.dot_general(a, b, (((1,), (1,)), ((), ())), preferred_element_type=F32)


def _dot_tn(a, b):
    return lax.dot_general(a, b, (((0,), (0,)), ((), ())), preferred_element_type=F32)


def _sigmoid(v):
    return 1.0 / (1.0 + jnp.exp(-v))


def _matmul(name, mode, a, b, *, tm, tn, tk, extras=(), outs, epilogue, deps=(), b_spec=None, n_cols=None,
            prefetch=(), alias_dep_to_out=None, j_outer=False):
    deps = [d for d in deps if d is not None]
    if mode == "tn":
        K, M = a.shape
    else:
        M, K = a.shape
    if b_spec is None:
        (N, K2) = b.shape if mode == "nt" else b.shape[::-1]
        assert K == K2, (name, a.shape, b.shape)
        if mode == "nt":
            b_spec = pl.BlockSpec((tn, tk), lambda i, j, k, *p: (j, k))
        else:
            b_spec = pl.BlockSpec((tk, tn), lambda i, j, k, *p: (k, j))
    else:
        N = n_cols
    assert M % tm == 0 and N % tn == 0 and K % tk == 0, (name, a.shape, b.shape)
    ni, nj, nk = M // tm, N // tn, K // tk
    if mode == "tn":
        a_spec = pl.BlockSpec((tk, tm), lambda i, j, k, *p: (k, i))
    else:
        a_spec = pl.BlockSpec((tm, tk), lambda i, j, k, *p: (i, k))
    dot = {"nn": _dot_nn, "nt": _dot_nt, "tn": _dot_tn}[mode]
    n_ex, n_out, n_dep, n_pre = len(extras), len(outs), len(deps), len(prefetch)
    grid = (ni, nj, nk)
    if j_outer:
        grid = (nj, ni, nk)

        def swapped(spec):
            return pl.BlockSpec(spec.block_shape, lambda j, i, k, *p: spec.index_map(i, j, k, *p))

        a_spec, b_spec = swapped(a_spec), swapped(b_spec)
        extras = [(e, swapped(s)) for e, s in extras]
        outs = [(o, swapped(s)) for o, s in outs]

    def body(*refs):
        refs = refs[n_pre:]
        a_ref, b_ref = refs[0], refs[1]
        ex = refs[2:2 + n_ex]
        out = refs[2 + n_ex + n_dep:2 + n_ex + n_dep + n_out]
        acc = refs[-1] if nk > 1 else None
        i = pl.program_id(1 if j_outer else 0)
        k = pl.program_id(2)
        if nk == 1:
            epilogue(dot(a_ref[...].astype(CDT), b_ref[...].astype(CDT)), ex, out, i)
            return

        @pl.when(k == 0)
        def _():
            acc[...] = jnp.zeros_like(acc)

        acc[...] += dot(a_ref[...].astype(CDT), b_ref[...].astype(CDT))

        @pl.when(k == nk - 1)
        def _():
            epilogue(acc[...], ex, out, i)

    grid_spec = pltpu.PrefetchScalarGridSpec(
        num_scalar_prefetch=n_pre, grid=grid,
        in_specs=[a_spec, b_spec] + [s for _, s in extras] + [pl.BlockSpec(memory_space=pl.ANY)] * n_dep,
        out_specs=[s for _, s in outs],
        scratch_shapes=[pltpu.VMEM((tm, tn), F32)] if nk > 1 else [])
    aliases = {}
    if alias_dep_to_out is not None:
        aliases = {n_pre + 2 + n_ex + alias_dep_to_out[0]: alias_dep_to_out[1]}
    res = pl.pallas_call(
        body, name=name, grid_spec=grid_spec, out_shape=[o for o, _ in outs], input_output_aliases=aliases,
        compiler_params=_params(3),
    )(*prefetch, a, b, *[e for e, _ in extras], *deps)
    return res


def _mn(tm, tn, col_off=0):
    return pl.BlockSpec((tm, tn), lambda i, j, k, *p: (i, j + col_off))


def _row(tn):
    return pl.BlockSpec((1, tn), lambda i, j, k, *p: (0, j))


def _ep_store(acc, ex, out, i):
    out[0][...] = acc.astype(out[0].dtype)


def _ep_resid(acc, ex, out, i):
    out[0][...] = ex[0][...] + acc


def _ep_up(acc, ex, out, i):
    out[0][...] = acc.astype(out[0].dtype)
    r = jnp.maximum(acc, 0.0)
    out[1][...] = (r * r).astype(out[1].dtype)


def _ep_down_loss(acc, ex, out, i, inv_d):
    diff = (ex[0][...] + acc) - ex[1][...]
    dx2 = diff * inv_d
    out[0][...] = dx2
    out[1][...] = dx2.astype(out[1].dtype)

    @pl.when(i == 0)
    def _():
        out[2][...] = jnp.zeros_like(out[2])

    out[2][...] += jnp.sum(diff * diff, axis=0, keepdims=True)


def _ep_dh(acc, ex, out, i):
    h = ex[0][...].astype(F32)
    out[0][...] = (acc * (2.0 * jnp.maximum(h, 0.0))).astype(out[0].dtype)


def _ep_rms_bwd(acc, ex, out, i):
    x = ex[0][...]
    g = ex[1][...]
    rstd = lax.rsqrt(jnp.mean(x * x, axis=-1, keepdims=True) + EPS)
    xh = x * rstd
    dxh = acc * g
    dx = ex[2][...] + rstd * (dxh - xh * jnp.mean(dxh * xh, axis=-1, keepdims=True))
    out[0][...] = dx
    for copy in out[1:-1]:
        copy[...] = dx.astype(copy.dtype)
    dg = out[-1]

    @pl.when(i == 0)
    def _():
        dg[...] = jnp.zeros_like(dg)

    dg[...] += jnp.sum(acc * xh, axis=0, keepdims=True)


def _ep_gates(acc, ex, out, i):
    sa = _sigmoid(ex[0][...].astype(F32))
    sb = _sigmoid(ex[1][...].astype(F32))
    dpa = acc * sa
    dpb = acc * sb
    out[0][...] = dpa.astype(out[0].dtype)
    out[1][...] = dpb.astype(out[1].dtype)
    out[2][...] = (dpa * ex[2][...].astype(F32) * (1.0 - sa)).astype(out[2].dtype)
    out[3][...] = (dpb * ex[3][...].astype(F32) * (1.0 - sb)).astype(out[3].dtype)


def _sds(shape, dtype):
    return jax.ShapeDtypeStruct(shape, dtype)


def _rms_fwd(name, x, g, tm=512):
    S, D = x.shape

    def body(x_ref, g_ref, o_ref):
        xv = x_ref[...]
        rstd = lax.rsqrt(jnp.mean(xv * xv, axis=-1, keepdims=True) + EPS)
        o_ref[...] = (xv * rstd * g_ref[...]).astype(o_ref.dtype)

    return pl.pallas_call(
        body, name=name, grid=(S // tm,),
        in_specs=[pl.BlockSpec((tm, D), lambda i: (i, 0)), pl.BlockSpec((1, D), lambda i: (0, 0))],
        out_specs=pl.BlockSpec((tm, D), lambda i: (i, 0)),
        out_shape=_sds((S, D), CDT), compiler_params=_params(1))(x, g)


def _rm_shape(S, d, width):
    return (S, width) if d == 1 else (d, S // d, width)


def _rm_spec(tm, d, width):
    if d == 1:
        return pl.BlockSpec((tm, width), lambda i: (i, 0))
    return pl.BlockSpec((d, tm // d, width), lambda i: (0, i, 0))


def _rm_put(dst_ref, cols, buf_ref, d):
    if d == 1:
        dst_ref[:, cols] = buf_ref[...].astype(dst_ref.dtype)
        return
    m = buf_ref.shape[0] // d
    for r in range(d):
        dst_ref[r, :, cols] = buf_ref[pl.ds(r, m, stride=d), :].astype(dst_ref.dtype)


def _rm_reader(buf_ref, src_ref, d):
    if d == 1:
        return lambda s, rows: src_ref[rows, s * HD:(s + 1) * HD].astype(F32)
    m = buf_ref.shape[1] // d
    for s in range(buf_ref.shape[0]):
        for r in range(d):
            buf_ref.at[s][pl.ds(r, m, stride=d), :] = src_ref[r, :, s * HD:(s + 1) * HD].astype(F32)
    return lambda s, rows: buf_ref.at[s][rows, :]


def _qknorm_fwd(proj, gqk, tm=256):
    S = proj.shape[0]
    W = 2 * ATT_W
    dil = [d for _, d in ATT_GROUPS]

    def body(p_ref, g_ref, o0, o1, o2, buf):
        outs = (o0, o1, o2)
        for hd in range(3 * ATT_HEADS):
            which, head = hd // ATT_HEADS, hd % ATT_HEADS
            grp, slot = head // ATT_HPG, head % ATT_HPG
            cols = slice(hd * HD, (hd + 1) * HD)

            def chunk(rows, which=which, cols=cols):
                v = p_ref[rows, cols].astype(F32)
                if which < 2:
                    rstd = lax.rsqrt(jnp.mean(v * v, axis=-1, keepdims=True) + EPS)
                    v = v * rstd * g_ref[:, cols]
                buf[rows, :] = v

            chunk(slice(None))
            _rm_put(outs[grp], slice(which * GW + slot * HD, which * GW + (slot + 1) * HD), buf, dil[grp])

    return pl.pallas_call(
        body, name="qknorm_fwd", grid=(S // tm,),
        in_specs=[pl.BlockSpec((tm, 3 * ATT_W), lambda i: (i, 0)), pl.BlockSpec((1, W), lambda i: (0, 0))],
        out_specs=[_rm_spec(tm, d, 3 * GW) for d in dil],
        out_shape=[_sds(_rm_shape(S, d, 3 * GW), CDT) for d in dil],
        scratch_shapes=[pltpu.VMEM((tm, HD), F32)],
        compiler_params=_params(1))(proj, gqk)


def _qknorm_bwd(proj, gqk, dqs, dks, dvs, dproj, tm=256):
    S = proj.shape[0]
    W = 2 * ATT_W
    dil = [d for _, d in ATT_GROUPS]

    def body(p_ref, g_ref, *refs):
        ins = refs[0:9]
        o_ref, dg_ref = refs[10], refs[11]
        bufs = refs[12:21]
        i = pl.program_id(0)

        @pl.when(i == 0)
        def _():
            dg_ref[...] = jnp.zeros_like(dg_ref)

        nat = [_rm_reader(bufs[j], ins[j], dil[j % 3]) for j in range(9)]
        dq_get, dk_get, dv_get = nat[0:3], nat[3:6], nat[6:9]
        for hd in range(2 * ATT_HEADS):
            sl = slice(hd * HD, (hd + 1) * HD)
            head = hd % ATT_HEADS
            grp, slot = head // ATT_HPG, head % ATT_HPG
            get = (dq_get if hd < ATT_HEADS else dk_get)[grp]

            def chunk(rows, sl=sl, slot=slot, get=get):
                dn = get(slot, rows)
                v = p_ref[rows, sl].astype(F32)
                rstd = lax.rsqrt(jnp.mean(v * v, axis=-1, keepdims=True) + EPS)
                vh = v * rstd
                dg_ref[:, sl] += jnp.sum(dn * vh, axis=0, keepdims=True)
                dvh = dn * g_ref[:, sl]
                o_ref[rows, sl] = (rstd * (dvh - vh * jnp.mean(dvh * vh, axis=-1, keepdims=True))).astype(o_ref.dtype)

            chunk(slice(None))
        for head in range(ATT_HEADS):
            grp, slot = head // ATT_HPG, head % ATT_HPG
            o_ref[:, W + head * HD:W + (head + 1) * HD] = dv_get[grp](slot, slice(None)).astype(o_ref.dtype)

    return pl.pallas_call(
        body, name="qknorm_bwd", grid=(S // tm,),
        in_specs=[pl.BlockSpec((tm, W), lambda i: (i, 0)), pl.BlockSpec((1, W), lambda i: (0, 0))]
        + [_rm_spec(tm, d, GW) for d in dil] * 3 + [pl.BlockSpec(memory_space=pl.ANY)],
        out_specs=[pl.BlockSpec((tm, 3 * ATT_W), lambda i: (i, 0)), pl.BlockSpec((1, W), lambda i: (0, 0))],
        out_shape=[_sds(dproj.shape, dproj.dtype), _sds((1, W), F32)],
        scratch_shapes=[pltpu.VMEM((ATT_HPG, tm, HD), F32)] * 9,
        input_output_aliases={11: 0},
        compiler_params=_params(1))(proj, gqk, *dqs, *dks, *dvs, dproj)


def _att_mask(n):
    qi = lax.broadcasted_iota(jnp.int32, (BLK, 2 * BLK), 0)
    kj = lax.broadcasted_iota(jnp.int32, (BLK, 2 * BLK), 1)
    dist = BLK + qi - kj
    valid = (dist >= 0) & (dist <= BLK) & ((kj >= BLK) | (n > 0))
    return valid, dist.astype(F32)


def _att_slopes(grp):
    return [2.0 ** (-8.0 * (grp * ATT_HPG + hh + 1) / ATT_HEADS) for hh in range(ATT_HPG)]


def _att_spec(d, row_fn, col=0):
    if d == 1:
        return pl.BlockSpec((BLK, GW), lambda r, n: (row_fn(n), col))
    return pl.BlockSpec((None, BLK, GW), lambda r, n: (r, row_fn(n), col))


def _att_qkv_specs(d, nb):
    last = nb - 1

    def cur(n):
        return jnp.minimum(n, last)

    def prev(n):
        return jnp.maximum(jnp.minimum(n, last) - 1, 0)

    return [_att_spec(d, cur, 0), _att_spec(d, prev, 1), _att_spec(d, cur, 1), _att_spec(d, prev, 2),
            _att_spec(d, cur, 2)]


def _att_fwd(grp, S, qkv):
    _, d = ATT_GROUPS[grp]
    L = S // d
    nb = L // BLK
    slopes = _att_slopes(grp)
    scale = HD ** -0.5

    def body(q_ref, kp_ref, kc_ref, vp_ref, vc_ref, o_ref, l_ref, s_buf, p_buf, den_buf):
        n = pl.program_id(1)
        valid, distf = _att_mask(n)
        heads = [slice(hh * HD, (hh + 1) * HD) for hh in range(ATT_HPG)]
        for hh, sl in enumerate(heads):
            k = jnp.concatenate([kp_ref[:, sl], kc_ref[:, sl]], axis=0)
            s_buf[hh] = _dot_nt(q_ref[:, sl], k)
        for hh, sl in enumerate(heads):
            s = s_buf[hh] * scale + (-slopes[hh] * d) * distf
            s = jnp.where(valid, s, -1e30)
            m = jnp.max(s, axis=-1, keepdims=True)
            p = jnp.exp(s - m)
            den = jnp.sum(p, axis=-1, keepdims=True)
            p_buf[hh] = p.astype(CDT)
            den_buf[hh] = jnp.broadcast_to(den, (BLK, HD))
            l_ref[:, sl] = jnp.broadcast_to(m + jnp.log(den), (BLK, HD))
        for hh, sl in enumerate(heads):
            v = jnp.concatenate([vp_ref[:, sl], vc_ref[:, sl]], axis=0)
            o_ref[:, sl] = _dot_nn(p_buf[hh], v) / den_buf[hh]

    out_spec = _att_spec(d, lambda n: n)
    return pl.pallas_call(
        body, name="att_fwd_g%d" % grp, grid=(d, nb),
        in_specs=_att_qkv_specs(d, nb),
        out_specs=[out_spec, out_spec],
        out_shape=[_sds(_rm_shape(S, d, GW), F32)] * 2,
        scratch_shapes=[pltpu.VMEM((ATT_HPG, BLK, 2 * BLK), F32), pltpu.VMEM((ATT_HPG, BLK, 2 * BLK), CDT),
                        pltpu.VMEM((ATT_HPG, BLK, HD), F32)],
        compiler_params=_params(2),
    )(qkv, qkv, qkv, qkv, qkv)


def _att_bwd(grp, S, qkv, lse, do_g, c_g):
    _, d = ATT_GROUPS[grp]
    L = S // d
    nb = L // BLK
    slopes = _att_slopes(grp)
    scale = HD ** -0.5
    last = nb - 1

    def body(q_ref, kp_ref, kc_ref, vp_ref, vc_ref, l_ref, do_ref, c_ref, dq_ref, dk_ref, dv_ref, ck, cv,
             s_buf, dp_buf, p_buf, ds_buf):
        n = pl.program_id(1)

        @pl.when(n == 0)
        def _():
            ck[...] = jnp.zeros_like(ck)
            cv[...] = jnp.zeros_like(cv)

        @pl.when(n < nb)
        def _():
            valid, distf = _att_mask(n)
            heads = [slice(hh * HD, (hh + 1) * HD) for hh in range(ATT_HPG)]
            for hh, sl in enumerate(heads):
                k = jnp.concatenate([kp_ref[:, sl], kc_ref[:, sl]], axis=0)
                v = jnp.concatenate([vp_ref[:, sl], vc_ref[:, sl]], axis=0)
                s_buf[hh] = _dot_nt(q_ref[:, sl], k)
                dp_buf[hh] = _dot_nt(do_ref[:, sl], v)
            for hh, sl in enumerate(heads):
                s = s_buf[hh] * scale + (-slopes[hh] * d) * distf
                p = jnp.where(valid, jnp.exp(s - l_ref[:, sl][:, 0:1]), 0.0)
                p_buf[hh] = p.astype(CDT)
                ds_buf[hh] = (p * (dp_buf[hh] + c_ref[:, sl][:, 0:1]) * scale).astype(CDT)
            for hh, sl in enumerate(heads):
                k = jnp.concatenate([kp_ref[:, sl], kc_ref[:, sl]], axis=0)
                ds = ds_buf[hh]
                dq_ref[:, sl] = _dot_nn(ds, k)
                dk = _dot_tn(ds, q_ref[:, sl])
                dv = _dot_tn(p_buf[hh], do_ref[:, sl])
                dk_ref[:, sl] = ck[:, sl] + dk[0:BLK]
                dv_ref[:, sl] = cv[:, sl] + dv[0:BLK]
                ck[:, sl] = dk[BLK:2 * BLK]
                cv[:, sl] = dv[BLK:2 * BLK]

        @pl.when(n == nb)
        def _():
            dk_ref[...] = ck[...]
            dv_ref[...] = cv[...]

    blk = (BLK, GW)
    at_q = _att_spec(d, lambda n: jnp.minimum(n, last))
    behind = _att_spec(d, lambda n: jnp.maximum(n - 1, 0))
    return pl.pallas_call(
        body, name="att_bwd_g%d" % grp, grid=(d, nb + 1),
        in_specs=_att_qkv_specs(d, nb) + [at_q, at_q, at_q],
        out_specs=[at_q, behind, behind],
        out_shape=[_sds(_rm_shape(S, d, GW), F32)] * 3,
        scratch_shapes=[pltpu.VMEM(blk, F32), pltpu.VMEM(blk, F32),
                        pltpu.VMEM((ATT_HPG, BLK, 2 * BLK), F32), pltpu.VMEM((ATT_HPG, BLK, 2 * BLK), F32),
                        pltpu.VMEM((ATT_HPG, BLK, 2 * BLK), CDT), pltpu.VMEM((ATT_HPG, BLK, 2 * BLK), CDT)],
        compiler_params=_params(2),
    )(qkv, qkv, qkv, qkv, qkv, lse, do_g, c_g)


def _mix_alpha(l0, l1, l2):
    mx = jnp.maximum(jnp.maximum(l0, l1), l2)
    e = [jnp.exp(l0 - mx), jnp.exp(l1 - mx), jnp.exp(l2 - mx)]
    tot = e[0] + e[1] + e[2]
    return [ei / tot for ei in e]


def _mix_fwd(S, os_, ls_, tm=256):
    dil = [d for _, d in ATT_GROUPS]

    def body(*refs):
        out, bufs = refs[6], refs[7:13]
        get = [_rm_reader(bufs[j], refs[j], dil[j % 3]) for j in range(6)]
        for s in range(ATT_HPG):
            def chunk(rows, s=s):
                al = _mix_alpha(*[get[3 + g](s, rows) for g in range(3)])
                mixed = al[0] * get[0](s, rows) + al[1] * get[1](s, rows) + al[2] * get[2](s, rows)
                out[rows, s * HD:(s + 1) * HD] = mixed.astype(out.dtype)

            chunk(slice(None))

    specs = [_rm_spec(tm, d, GW) for d in dil]
    return pl.pallas_call(
        body, name="mix_fwd", grid=(S // tm,), in_specs=specs * 2, out_specs=pl.BlockSpec((tm, GW), lambda i: (i, 0)),
        out_shape=_sds((S, GW), CDT), scratch_shapes=[pltpu.VMEM((ATT_HPG, tm, HD), F32)] * 6,
        compiler_params=_params(1))(*os_, *ls_)


def _mix_bwd(S, os_, ls_, do_a, tm=256):
    dil = [d for _, d in ATT_GROUPS]

    def body(*refs):
        d_ref, outs, bufs, tmps = refs[6], refs[7:13], refs[13:19], refs[19:25]
        get = [_rm_reader(bufs[j], refs[j], dil[j % 3]) for j in range(6)]
        for s in range(ATT_HPG):
            cols = slice(s * HD, (s + 1) * HD)

            def chunk(rows, s=s, cols=cols):
                al = _mix_alpha(*[get[3 + g](s, rows) for g in range(3)])
                dv = d_ref[rows, cols]
                o_a = al[0] * get[0](s, rows) + al[1] * get[1](s, rows) + al[2] * get[2](s, rows)
                dsum = jnp.sum(dv * o_a, axis=-1, keepdims=True)
                for g in range(3):
                    tmps[g][rows, :] = al[g] * dv
                    tmps[3 + g][rows, :] = -(al[g] * dsum)

            chunk(slice(None))
            for j in range(6):
                _rm_put(outs[j], cols, tmps[j], dil[j % 3])

    specs = [_rm_spec(tm, d, GW) for d in dil]
    res = pl.pallas_call(
        body, name="mix_bwd", grid=(S // tm,), in_specs=specs * 2 + [pl.BlockSpec((tm, GW), lambda i: (i, 0))],
        out_specs=specs * 2,
        out_shape=[_sds(_rm_shape(S, d, GW), CDT) for d in dil] + [_sds(_rm_shape(S, d, GW), F32) for d in dil],
        scratch_shapes=[pltpu.VMEM((ATT_HPG, tm, HD), F32)] * 6 + [pltpu.VMEM((tm, HD), F32)] * 6,
        compiler_params=_params(1))(*os_, *ls_, do_a)
    return res[:3], res[3:]


def _ret_tables(dk):
    H, C = RET_HEADS, BLK
    log_g = jnp.log(1.0 - 2.0 ** (-5.0 - jnp.arange(H, dtype=F32)))
    idx = jnp.arange(C, dtype=F32)
    diff = idx[:, None] - idx[None, :]
    decay = jnp.where(diff >= 0, jnp.exp(log_g[:, None, None] * jnp.maximum(diff, 0.0)), 0.0)
    xi = jnp.exp(log_g[:, None] * (idx[None, :] + 1.0))
    zeta = jnp.exp(log_g[:, None] * (C - 1.0 - idx[None, :])) * (dk ** -0.5)
    g_chunk = jnp.exp(log_g * C)
    bc = lambda t: jnp.broadcast_to(t[:, :, None], (H, C, C))
    return decay, bc(xi), bc(zeta), jnp.broadcast_to(g_chunk[:, None, None], (H, 8, C))


def _gn_fwd(o, g, b):
    mu = jnp.mean(o, axis=-1, keepdims=True)
    xc = o - mu
    rstd = lax.rsqrt(jnp.mean(xc * xc, axis=-1, keepdims=True) + EPS)
    yh = xc * rstd
    return yh, rstd, yh * g + b


def _ret_specs(dk, dv, order):
    H = RET_HEADS
    qk_w, v_w = H * dk, H * dv
    off_q = 3 * ATT_W
    off_k, off_v, off_g = off_q + qk_w, off_q + 2 * qk_w, off_q + 2 * qk_w + v_w
    assert 2 * dk == dv and all(off % dv == 0 for off in (off_q, off_k, off_v, off_g))

    def col(off, j):
        return pl.BlockSpec((BLK, dv), lambda i: (order(i), off // dv + j))

    tab = pl.BlockSpec((H, BLK, BLK), lambda i: (0, 0, 0))
    return ([col(off_q, j) for j in range(H // 2)] + [col(off_k, j) for j in range(H // 2)]
            + [col(off_v, j) for j in range(H)] + [col(off_g, j) for j in range(H)]
            + [tab, tab, tab, pl.BlockSpec((H, 8, BLK), lambda i: (0, 0, 0))])


def _ret_heads(refs, dk):
    H = RET_HEADS
    q_refs, k_refs = refs[0:H // 2], refs[H // 2:H]
    v_refs, gr_refs = refs[H:2 * H], refs[2 * H:3 * H]

    def head(h):
        cols = slice((h % 2) * dk, (h % 2 + 1) * dk)
        return q_refs[h // 2][:, cols], k_refs[h // 2][:, cols], v_refs[h][...], gr_refs[h][...]

    return head, refs[3 * H:3 * H + 4]


def _ret_fwd(proj, gn_g, gn_b, dk, dv):
    S = proj.shape[0]
    N = S // BLK
    H = RET_HEADS
    kscale = dk ** -0.5
    n_in = 3 * H + 4

    def body(*refs):
        head, (dec_ref, xi_ref, zeta_ref, gc_ref) = _ret_heads(refs, dk)
        g_ref, b_ref, opre_ref, or_ref, st_ref, state, s_buf, cross_buf = refs[n_in:n_in + 8]
        n = pl.program_id(0)

        @pl.when(n == 0)
        def _():
            state[...] = jnp.zeros_like(state)

        for h in range(H):
            q, k, v, _ = head(h)
            s_buf[h] = _dot_nt(q, k)
            st = state[h]
            st_c = st.astype(CDT)
            st_ref[h] = st_c
            cross_buf[h] = _dot_nn(q, st_c)
            kz = (k.astype(F32) * zeta_ref[h][:, 0:1]).astype(CDT)
            state[h] = st * gc_ref[h][0:1, 0:1] + _dot_tn(kz, v)
        for h in range(H):
            vs = slice(h * dv, (h + 1) * dv)
            _, _, v, gr = head(h)
            s = s_buf[h] * kscale * dec_ref[h]
            o = _dot_nn(s.astype(CDT), v) + cross_buf[h] * xi_ref[h][:, 0:1]
            opre_ref[:, vs] = o
            _, _, y = _gn_fwd(o, g_ref[:, vs], b_ref[:, vs])
            gr = gr.astype(F32)
            or_ref[:, vs] = (y * (gr * _sigmoid(gr))).astype(or_ref.dtype)

    v_w = H * dv
    row = pl.BlockSpec((1, v_w), lambda i: (0, 0))
    tile = pl.BlockSpec((BLK, v_w), lambda i: (i, 0))
    return pl.pallas_call(
        body, name="ret_fwd", grid=(N,),
        in_specs=_ret_specs(dk, dv, lambda i: i) + [row, row],
        out_specs=[tile, tile, pl.BlockSpec((None, H, dk, dv), lambda i: (i, 0, 0, 0))],
        out_shape=[_sds((S, v_w), F32), _sds((S, v_w), CDT), _sds((N, H, dk, dv), CDT)],
        scratch_shapes=[pltpu.VMEM((H, dk, dv), F32), pltpu.VMEM((H, BLK, BLK), F32), pltpu.VMEM((H, BLK, dv), F32)],
        compiler_params=_params(1),
    )(*[proj] * (3 * H), *_ret_tables(dk), gn_g, gn_b)


def _ret_bwd(proj, gn_g, gn_b, o_pre, states, d_or, dga, dgb, dk, dv):
    S, in_w = proj.shape
    N = S // BLK
    H = RET_HEADS
    qk_w, v_w = H * dk, H * dv
    kscale = dk ** -0.5
    n_in = 3 * H + 4
    out_w = 2 * qk_w + 2 * v_w
    gate_w = dga.shape[1]
    col0 = 3 * ATT_W
    assert col0 + out_w + 2 * gate_w == in_w
    rev = lambda i: N - 1 - i

    def body(*refs):
        head, (dec_ref, xi_ref, zeta_ref, gc_ref) = _ret_heads(refs, dk)
        (g_ref, b_ref, opre_ref, st_ref, dor_ref, dga_ref, dgb_ref, dproj_ref, dg_ref, db_ref, dstate, stage,
         sem, do_buf, dox_buf, a_buf, g_buf, dq_buf, dk_buf, dv_buf) = refs[n_in:n_in + 20]
        i = pl.program_id(0)
        slot = i % 2
        out_ref = stage.at[slot]

        def out_copy(s, step):
            rows = pl.ds(pl.multiple_of(rev(step) * BLK, BLK), BLK)
            return pltpu.make_async_copy(stage.at[s], dproj_ref.at[rows, pl.ds(col0, in_w - col0)], sem.at[s])

        @pl.when(i >= 2)
        def _():
            out_copy(slot, i - 2).wait()

        @pl.when(i == 0)
        def _():
            dstate[...] = jnp.zeros_like(dstate)
            dg_ref[...] = jnp.zeros_like(dg_ref)
            db_ref[...] = jnp.zeros_like(db_ref)

        out_ref[:, out_w:out_w + gate_w] = dga_ref[...]
        out_ref[:, out_w + gate_w:out_w + 2 * gate_w] = dgb_ref[...]
        for h in range(H):
            vs = slice(h * dv, (h + 1) * dv)
            _, _, _, gr = head(h)
            gr = gr.astype(F32)
            sg = _sigmoid(gr)
            gain = g_ref[:, vs]
            yh, rstd, y = _gn_fwd(opre_ref[:, vs], gain, b_ref[:, vs])
            d_or_v = dor_ref[:, vs]
            dy = d_or_v * (gr * sg)
            out_ref[:, 2 * qk_w + v_w + h * dv:2 * qk_w + v_w + (h + 1) * dv] = (
                d_or_v * y * (sg * (1.0 + gr * (1.0 - sg)))).astype(out_ref.dtype)
            dg_ref[:, vs] += jnp.sum(dy * yh, axis=0, keepdims=True)
            db_ref[:, vs] += jnp.sum(dy, axis=0, keepdims=True)
            dyh = dy * gain
            do = rstd * (dyh - jnp.mean(dyh, axis=-1, keepdims=True)
                         - yh * jnp.mean(dyh * yh, axis=-1, keepdims=True))
            do_buf[h] = do.astype(CDT)
            dox_buf[h] = (do * xi_ref[h][:, 0:1]).astype(CDT)
        for h in range(H):
            q, k, v, _ = head(h)
            dox = dox_buf[h]
            a_buf[h] = _dot_nt(q, k)
            g_buf[h] = _dot_nt(do_buf[h], v)
            dsn = dstate[h]
            dsn_c = dsn.astype(CDT)
            kz = (k.astype(F32) * zeta_ref[h][:, 0:1]).astype(CDT)
            dq_buf[h] = _dot_nt(dox, st_ref[h])
            dk_buf[h] = _dot_nt(v, dsn_c)
            dv_buf[h] = _dot_nn(kz, dsn_c)
            dstate[h] = dsn * gc_ref[h][0:1, 0:1] + _dot_tn(q, dox)
        for h in range(H):
            q, k, _, _ = head(h)
            decay = dec_ref[h]
            a_c = (a_buf[h] * kscale * decay).astype(CDT)
            g_c = (g_buf[h] * decay).astype(CDT)
            dq = _dot_nn(g_c, k) * kscale + dq_buf[h]
            dkk = _dot_tn(g_c, q) * kscale + dk_buf[h] * zeta_ref[h][:, 0:1]
            dvv = _dot_tn(a_c, do_buf[h]) + dv_buf[h]
            out_ref[:, h * dk:(h + 1) * dk] = dq.astype(out_ref.dtype)
            out_ref[:, qk_w + h * dk:qk_w + (h + 1) * dk] = dkk.astype(out_ref.dtype)
            out_ref[:, 2 * qk_w + h * dv:2 * qk_w + (h + 1) * dv] = dvv.astype(out_ref.dtype)

        cp = out_copy(slot, i)
        cp.start()

        @pl.when(i == N - 1)
        def _():
            cp.wait()
            if N >= 2:
                out_copy(1 - slot, i - 1).wait()

    row = pl.BlockSpec((1, v_w), lambda i: (0, 0))
    tile = pl.BlockSpec((BLK, v_w), lambda i: (rev(i), 0))
    gate = pl.BlockSpec((BLK, gate_w), lambda i: (rev(i), 0))
    return pl.pallas_call(
        body, name="ret_bwd", grid=(N,),
        in_specs=_ret_specs(dk, dv, rev) + [row, row, tile,
                 pl.BlockSpec((None, H, dk, dv), lambda i: (rev(i), 0, 0, 0)), tile, gate, gate],
        out_specs=[pl.BlockSpec(memory_space=pl.ANY), row, row],
        out_shape=[_sds((S, in_w), CDT), _sds((1, v_w), F32), _sds((1, v_w), F32)],
        scratch_shapes=[pltpu.VMEM((H, dk, dv), F32), pltpu.VMEM((2, BLK, in_w - col0), CDT),
                        pltpu.SemaphoreType.DMA((2,)),
                        pltpu.VMEM((H, BLK, dv), CDT), pltpu.VMEM((H, BLK, dv), CDT),
                        pltpu.VMEM((H, BLK, BLK), F32), pltpu.VMEM((H, BLK, BLK), F32),
                        pltpu.VMEM((H, BLK, dk), F32), pltpu.VMEM((H, BLK, dk), F32), pltpu.VMEM((H, BLK, dv), F32)],
        compiler_params=_params(1),
    )(*[proj] * (3 * H), *_ret_tables(dk), gn_g, gn_b, o_pre, states, d_or, dga, dgb)


def _merge_fwd(o_a, o_r, wa, wb, proj, d_model, tm=512, tn=512):
    S, in_w = proj.shape
    off_a, off_b = in_w - 2 * d_model, in_w - d_model
    assert off_a % tn == 0 and off_b % tn == 0

    def body(oa_ref, or_ref, wa_ref, wb_ref, ga_ref, gb_ref, y_ref, pa_ref, pb_ref):
        pa = _dot_nn(oa_ref[...], wa_ref[...])
        pb = _dot_nn(or_ref[...], wb_ref[...])
        y = _sigmoid(ga_ref[...].astype(F32)) * pa + _sigmoid(gb_ref[...].astype(F32)) * pb
        y_ref[...] = y.astype(y_ref.dtype)
        pa_ref[...] = pa.astype(pa_ref.dtype)
        pb_ref[...] = pb.astype(pb_ref.dtype)

    ka, kb = o_a.shape[1], o_r.shape[1]
    out = pl.BlockSpec((tm, tn), lambda i, j: (i, j))
    return pl.pallas_call(
        body, name="merge_fwd", grid=(S // tm, d_model // tn),
        in_specs=[pl.BlockSpec((tm, ka), lambda i, j: (i, 0)), pl.BlockSpec((tm, kb), lambda i, j: (i, 0)),
                  pl.BlockSpec((ka, tn), lambda i, j: (0, j)), pl.BlockSpec((kb, tn), lambda i, j: (0, j)),
                  pl.BlockSpec((tm, tn), lambda i, j: (i, off_a // tn + j)),
                  pl.BlockSpec((tm, tn), lambda i, j: (i, off_b // tn + j))],
        out_specs=[out, out, out], out_shape=[_sds((S, d_model), CDT)] * 3,
        compiler_params=_params(2))(o_a, o_r, wa, wb, proj, proj)


def _local_step(x, target, w_in_mine, chip, others, far_w_in, small, late_weights, on_grads, deps0=()):
    S, D = x.shape
    ns_in = w_in_mine.shape[1]
    in_w = N_CHIPS * ns_in
    d_ff = 4 * D
    ret_v_w = 2 * D
    dv = ret_v_w // RET_HEADS
    dk = (in_w - 3 * ATT_W - 2 * ret_v_w - 2 * D) // (2 * RET_HEADS)
    gqk = jnp.concatenate([small["q_norm_g"].reshape(1, ATT_W), small["k_norm_g"].reshape(1, ATT_W)], axis=1)
    g1, g2 = small["norm1_g"], small["norm2_g"]
    gn_g, gn_b = small["ret_gn_g"], small["ret_gn_b"]

    xn = _rms_fwd("rms1_fwd", x, g1)
    proj_sds = _sds((S, in_w), CDT)
    (proj_mine,) = _matmul(
        "in_proj_mine", "nn", xn, w_in_mine, tm=512, tn=ns_in, tk=D, prefetch=[chip],
        outs=[(proj_sds, pl.BlockSpec((512, ns_in), lambda i, j, k, c: (i, c[0])))], epilogue=_ep_store, deps=deps0)
    w_in = far_w_in(proj_mine)
    (proj,) = _matmul(
        "in_proj_far", "nn", xn, w_in, tm=512, tn=ns_in, tk=D, prefetch=[others], n_cols=(N_CHIPS - 1) * ns_in,
        b_spec=pl.BlockSpec((None, D, ns_in), lambda i, j, k, o: (o[j], 0, 0)),
        outs=[(proj_sds, pl.BlockSpec((512, ns_in), lambda i, j, k, o: (i, o[j])))], epilogue=_ep_store,
        deps=[proj_mine], alias_dep_to_out=(0, 0), j_outer=True)
    qkv = _qknorm_fwd(proj, gqk)
    att = [_att_fwd(g, S, qkv[g]) for g in range(3)]
    os_, ls_ = [a[0] for a in att], [a[1] for a in att]
    o_a = _mix_fwd(S, os_, ls_)
    o_pre, o_r, states = _ret_fwd(proj, gn_g, gn_b, dk, dv)
    w = late_weights(o_r)
    y, pa, pb = _merge_fwd(o_a, o_r, w["w_proj_a"], w["w_proj_b"], proj, D)
    (x1,) = _matmul("out_proj", "nn", y, w["w_out"], tm=512, tn=D, tk=D,
                    extras=[(x, _mn(512, D))], outs=[(_sds((S, D), F32), _mn(512, D))], epilogue=_ep_resid)
    xn2 = _rms_fwd("rms2_fwd", x1, g2)
    hid, act = _matmul("mlp_up", "nn", xn2, w["w_up"], tm=512, tn=2048, tk=D, j_outer=True,
                       outs=[(_sds((S, d_ff), CDT), _mn(512, 2048))] * 2, epilogue=_ep_up)
    dx2, dx2c, loss_row = _matmul(
        "mlp_down_loss", "nn", act, w["w_down"], tm=512, tn=D, tk=d_ff,
        extras=[(x1, _mn(512, D)), (target, _mn(512, D))],
        outs=[(_sds((S, D), F32), _mn(512, D)), (_sds((S, D), CDT), _mn(512, D)), (_sds((1, D), F32), _row(D))],
        epilogue=functools.partial(_ep_down_loss, inv_d=1.0 / D))
    loss = 0.5 * jnp.sum(loss_row) / D

    (dh,) = _matmul("d_hidden", "nt", dx2c, w["w_down"], tm=512, tn=2048, tk=D, j_outer=True,
                    extras=[(hid, _mn(512, 2048))], outs=[(_sds((S, d_ff), CDT), _mn(512, 2048))], epilogue=_ep_dh)
    (gw_down,) = _matmul("dw_down", "tn", act, dx2c, tm=1024, tn=D, tk=1024,
                         outs=[(_sds((d_ff, D), F32), _mn(1024, D))], epilogue=_ep_store)
    (gw_up,) = _matmul("dw_up", "tn", xn2, dh, tm=D, tn=1024, tk=1024,
                       outs=[(_sds((D, d_ff), F32), _mn(D, 1024))], epilogue=_ep_store)
    tok = on_grads({"w_down": gw_down, "w_up": gw_up})
    dx1, dx1c, dg2 = _matmul(
        "d_x1", "nt", dh, w["w_up"], tm=512, tn=D, tk=d_ff,
        extras=[(x1, _mn(512, D)), (g2, _row(D)), (dx2, _mn(512, D))],
        outs=[(_sds((S, D), F32), _mn(512, D)), (_sds((S, D), CDT), _mn(512, D)), (_sds((1, D), F32), _row(D))],
        epilogue=_ep_rms_bwd, deps=[tok])

    gt = 512
    assert (in_w - 2 * D) % gt == 0
    off_a, off_b = (in_w - 2 * D) // gt, (in_w - D) // gt
    dpa, dpb, dga, dgb = _matmul(
        "d_gates", "nt", dx1c, w["w_out"], tm=512, tn=gt, tk=D,
        extras=[(proj, _mn(512, gt, off_a)), (proj, _mn(512, gt, off_b)), (pa, _mn(512, gt)), (pb, _mn(512, gt))],
        outs=[(_sds((S, D), CDT), _mn(512, gt))] * 4, epilogue=_ep_gates)
    (gw_out,) = _matmul("dw_out", "tn", y, dx1c, tm=D, tn=D, tk=1024,
                        outs=[(_sds((D, D), F32), _mn(D, D))], epilogue=_ep_store)
    (gw_pa,) = _matmul("dw_proj_a", "tn", o_a, dpa, tm=GW, tn=D, tk=1024,
                       outs=[(_sds((GW, D), F32), _mn(GW, D))], epilogue=_ep_store)
    (gw_pb,) = _matmul("dw_proj_b", "tn", o_r, dpb, tm=1024, tn=D, tk=1024,
                       outs=[(_sds((ret_v_w, D), F32), _mn(1024, D))], epilogue=_ep_store)
    (do_a,) = _matmul("d_o_a", "nt", dpa, w["w_proj_a"], tm=1024, tn=GW, tk=D,
                      outs=[(_sds((S, GW), F32), _mn(1024, GW))], epilogue=_ep_store)
    tok = on_grads({"w_out": gw_out, "w_proj_a": gw_pa, "w_proj_b": gw_pb})
    (d_or,) = _matmul("d_o_r", "nt", dpb, w["w_proj_b"], tm=512, tn=ret_v_w, tk=D,
                      outs=[(_sds((S, ret_v_w), F32), _mn(512, ret_v_w))], epilogue=_ep_store, deps=[tok])

    dproj, dgn_g, dgn_b = _ret_bwd(proj, gn_g, gn_b, o_pre, states, d_or, dga, dgb, dk, dv)
    do_gs, c_gs = _mix_bwd(S, os_, ls_, do_a)
    datt_parts = [_att_bwd(g, S, qkv[g], ls_[g], do_gs[g], c_gs[g]) for g in range(3)]
    dproj, dgqk = _qknorm_bwd(proj, gqk, [p[0] for p in datt_parts], [p[1] for p in datt_parts],
                              [p[2] for p in datt_parts], dproj)

    (gw_in,) = _matmul(
        "dw_in", "tn", xn, dproj, tm=512, tn=ns_in, tk=1024,
        outs=[(_sds((N_CHIPS, D, ns_in), F32), pl.BlockSpec((None, 512, ns_in), lambda i, j, k: (j, i, 0)))],
        epilogue=_ep_store)
    tok = on_grads({"w_in": gw_in})
    grad_x, dg1 = _matmul(
        "d_x", "nt", dproj, w_in, tm=512, tn=D, tk=ns_in, n_cols=D,
        b_spec=pl.BlockSpec((None, D, ns_in), lambda i, j, k: (k, 0, 0)),
        extras=[(x, _mn(512, D)), (g1, _row(D)), (dx1, _mn(512, D))],
        outs=[(_sds((S, D), F32), _mn(512, D)), (_sds((1, D), F32), _row(D))],
        epilogue=_ep_rms_bwd, deps=[tok])

    smallg = {"norm1_g": dg1, "q_norm_g": dgqk[:, :ATT_W], "k_norm_g": dgqk[:, ATT_W:],
              "ret_gn_g": dgn_g, "ret_gn_b": dgn_b, "norm2_g": dg2}
    return loss, grad_x, smallg


N_CHIPS = 4
N_DEV = 8


def _place():
    x, y, c = lax.axis_index("x"), lax.axis_index("y"), lax.axis_index("c")
    return x, y, c


def _other_chips(x, y):
    out = []
    for fx, fy in ((1, 0), (0, 1), (1, 1)):
        px = 1 - x if fx else x
        py = 1 - y if fy else y
        out.append(((px, py), 2 * px + py))
    return out


SEM_SPEC = pl.BlockSpec(memory_space=pltpu.SEMAPHORE)
ANY_SPEC = pl.BlockSpec(memory_space=pl.ANY)
EFFECT = pltpu.SideEffectType.DATAFLOW_SIDE_EFFECTING


def _ici_copies(kind, srcs, lands, send, recv):
    x, y, c = _place()
    me = 2 * x + y
    out = []
    for w, (s, l) in enumerate(zip(srcs, lands)):
        for j, ((px, py), pidx) in enumerate(_other_chips(x, y)):
            if kind == "gather":
                half = s.shape[0] // 2
                rows = pl.ds(c * half, half)
                src, dst_there, dst_here = s.at[rows, :], l.at[me, rows, :], l.at[pidx, rows, :]
            else:
                src, dst_there, dst_here = s.at[pidx], l.at[me], l.at[pidx]
            out.append((src, dst_there, dst_here, send.at[3 * w + j], recv.at[3 * w + j], (px, py, c)))
    return out


def _exchange_start(name, kind, srcs, land_shapes):
    n = len(srcs)

    def body(*refs):
        src_refs, land_refs = refs[:n], refs[n:2 * n]
        send, recv = refs[2 * n], refs[2 * n + 1]
        token = refs[-1]
        for src, dst, _, ss, rs, dev in _ici_copies(kind, src_refs, land_refs, send, recv):
            pltpu.make_async_remote_copy(src_ref=src, dst_ref=dst, send_sem=ss, recv_sem=rs, device_id=dev,
                                         device_id_type=MESH).start()
        token[...] = jnp.zeros_like(token)

    thru = [pltpu.HBM(s.shape, s.dtype) for s in srcs] + [pltpu.HBM(shape, dtype) for shape, dtype in land_shapes]
    res = pl.pallas_call(
        body, name=name,
        out_shape=(pltpu.SemaphoreType.DMA((3 * n,)), pltpu.SemaphoreType.DMA((3 * n,)), *thru, _sds((8, LANES), F32)),
        in_specs=[HBM_SPEC] * (2 * n), out_specs=(SEM_SPEC, SEM_SPEC, *[HBM_SPEC] * (2 * n), VMEM_SPEC),
        input_output_aliases={i: 2 + i for i in range(2 * n)},
        compiler_params=pltpu.CompilerParams(has_side_effects=EFFECT),
    )(*[pltpu.with_memory_space_constraint(s, pltpu.HBM) for s in srcs],
      *[pltpu.with_memory_space_constraint(lax.empty(shape, dtype), pltpu.HBM) for shape, dtype in land_shapes])
    return res[0], res[1], list(res[2:2 + n]), list(res[2 + n:2 + 2 * n]), res[-1]


def _exchange_wait(name, kind, send, recv, srcs, lands, after):
    n = len(srcs)

    def body(*refs):
        src_refs, land_refs = refs[:n], refs[n:2 * n]
        send_ref, recv_ref = refs[2 * n], refs[2 * n + 1]
        for src, _, dst, ss, rs, dev in _ici_copies(kind, src_refs, land_refs, send_ref, recv_ref):
            cp = pltpu.make_async_remote_copy(src_ref=src, dst_ref=dst, send_sem=ss, recv_sem=rs, device_id=dev,
                                              device_id_type=MESH)
            cp.wait_send()
            cp.wait_recv()

    thru = [pltpu.HBM(t.shape, t.dtype) for t in list(srcs) + list(lands)]
    res = pl.pallas_call(
        body, name=name, out_shape=thru,
        in_specs=[HBM_SPEC] * (2 * n) + [SEM_SPEC, SEM_SPEC, ANY_SPEC], out_specs=[HBM_SPEC] * (2 * n),
        input_output_aliases={i: i for i in range(2 * n)},
        compiler_params=pltpu.CompilerParams(has_side_effects=EFFECT),
    )(*srcs, *lands, send, recv, after)
    return list(res[:n]), list(res[n:])


PAIR_TILE_ELEMS = 1 << 19


def _pair_fill(name, gathered, mine, core, others, chip):
    k, r, C = gathered.shape
    half = r // 2
    tr = _row_tile(half, C, PAIR_TILE_ELEMS, mult=16)
    nt = half // tr
    n_far = N_CHIPS - 1

    def body(c_ref, o_ref, chip_ref, in_ref, mine_ref, out_ref, slot, send, recv):
        j = pl.program_id(0)
        b = (j * nt + pl.program_id(1)) % 2
        x, y, c = _place()
        cp = pltpu.make_async_remote_copy(src_ref=in_ref, dst_ref=slot.at[b], send_sem=send.at[b],
                                          recv_sem=recv.at[b], device_id=(x, y, 1 - c), device_id_type=MESH)

        @pl.when(j < n_far)
        def _():
            cp.start()
            cp.wait_recv()
            out_ref[...] = slot[b]
            cp.wait_send()

        @pl.when(j >= n_far)
        def _():
            out_ref[...] = mine_ref[...]

    def far(j):
        return jnp.minimum(j, n_far - 1)

    grid_spec = pltpu.PrefetchScalarGridSpec(
        num_scalar_prefetch=3, grid=(n_far + 2, nt),
        in_specs=[pl.BlockSpec((tr, C), lambda j, i, c, o, m: (
                      (2 * o[far(j)] + c[0]) * nt + jnp.where(j < n_far, i, nt - 1), 0)),
                  pl.BlockSpec((tr, C), lambda j, i, c, o, m: (jnp.where(j < n_far, 0, (j - n_far) * nt + i), 0))],
        out_specs=pl.BlockSpec((tr, C), lambda j, i, c, o, m: (
            jnp.where(j < n_far, 2 * o[far(j)] + 1 - c[0], 2 * m[0] + j - n_far) * nt + i, 0)),
        scratch_shapes=[pltpu.VMEM((2, tr, C), gathered.dtype), pltpu.SemaphoreType.DMA((2,)),
                        pltpu.SemaphoreType.DMA((2,))])
    out = pl.pallas_call(body, name=name, grid_spec=grid_spec, out_shape=_sds((k * r, C), gathered.dtype),
                         input_output_aliases={3: 0}, compiler_params=_params(2))(
                             core, others, chip, gathered.reshape(k * r, C), mine)
    return out.reshape(k, r, C)


def _pair_reduce(name, g, core):
    k, R, C = g.shape
    half = R // 2
    tr = _row_tile(half, C, PAIR_TILE_ELEMS, mult=16)
    nt = half // tr

    def body(c_ref, mine_ref, give_ref, out_ref, wire_ref, stage, slot, send, recv):
        b = (pl.program_id(0) * nt + pl.program_id(1)) % 2
        x, y, c = _place()
        stage[b] = give_ref[...].astype(stage.dtype)
        cp = pltpu.make_async_remote_copy(src_ref=stage.at[b], dst_ref=slot.at[b], send_sem=send.at[b],
                                          recv_sem=recv.at[b], device_id=(x, y, 1 - c), device_id_type=MESH)
        cp.start()
        cp.wait_recv()
        tot = mine_ref[...] + slot[b].astype(F32)
        out_ref[...] = tot
        wire_ref[...] = tot.astype(wire_ref.dtype)
        cp.wait_send()

    blk = (tr, C)
    out_spec = pl.BlockSpec(blk, lambda s, i, c: (s * nt + i, 0))
    grid_spec = pltpu.PrefetchScalarGridSpec(
        num_scalar_prefetch=1, grid=(k, nt),
        in_specs=[pl.BlockSpec(blk, lambda s, i, c: ((2 * s + c[0]) * nt + i, 0)),
                  pl.BlockSpec(blk, lambda s, i, c: ((2 * s + 1 - c[0]) * nt + i, 0))],
        out_specs=[out_spec, out_spec],
        scratch_shapes=[pltpu.VMEM((2, tr, C), CDT), pltpu.VMEM((2, tr, C), CDT), pltpu.SemaphoreType.DMA((2,)),
                        pltpu.SemaphoreType.DMA((2,))])
    g2 = g.reshape(k * R, C)
    out, wire = pl.pallas_call(body, name=name, grid_spec=grid_spec,
                               out_shape=[_sds((k * half, C), F32), _sds((k * half, C), CDT)],
                               compiler_params=_params(2))(core, g2, g2)
    return out, wire.reshape(k, half, C)


def _all_reduce_small(v):
    r, cdim = v.shape

    def body(v_ref, o_ref, buf, send, recv):
        x, y, c = _place()
        me = 4 * x + 2 * y + c
        buf[me] = v_ref[...]
        sends = []
        for m in range(1, N_DEV):
            px = 1 - x if m & 4 else x
            py = 1 - y if m & 2 else y
            pc = 1 - c if m & 1 else c
            cp = pltpu.make_async_remote_copy(src_ref=v_ref, dst_ref=buf.at[me], send_sem=send.at[m - 1],
                                              recv_sem=recv.at[m - 1], device_id=(px, py, pc), device_id_type=MESH)
            cp.start()
            sends.append((cp, 4 * px + 2 * py + pc))
        for m, (cp, pidx) in enumerate(sends):
            pltpu.make_async_remote_copy(src_ref=v_ref, dst_ref=buf.at[pidx], send_sem=send.at[m], recv_sem=recv.at[m],
                                         device_id=(x, y, c), device_id_type=MESH).wait_recv()
        for cp, _ in sends:
            cp.wait_send()
        tot = buf[0]
        for k in range(1, N_DEV):
            tot = tot + buf[k]
        o_ref[...] = tot

    return pl.pallas_call(
        body, name="all_reduce_small", in_specs=[VMEM_SPEC], out_specs=VMEM_SPEC,
        out_shape=_sds((r, cdim), F32),
        scratch_shapes=[pltpu.VMEM((N_DEV, r, cdim), F32), pltpu.SemaphoreType.DMA((N_DEV - 1,)),
                        pltpu.SemaphoreType.DMA((N_DEV - 1,))],
    )(v)


def _row_tile(rows, cols, budget_elems=1 << 18, mult=8):
    if rows % mult:
        return rows
    t = max(mult, (budget_elems // cols) // mult * mult)
    while rows % t:
        t -= mult
    return t


def _adamw_update(w, g, m, v):
    nm = ADAM_B1 * m + (1.0 - ADAM_B1) * g
    nv = ADAM_B2 * v + (1.0 - ADAM_B2) * (g * g)
    m_hat = nm / (1.0 - ADAM_B1 ** ADAM_STEP)
    v_hat = nv / (1.0 - ADAM_B2 ** ADAM_STEP)
    return -ADAM_LR * (m_hat / (jnp.sqrt(v_hat) + ADAM_EPS) + ADAM_WD * w), nm, nv


def _adamw(name, w, g, m, v):
    R, C = w.shape
    tr = _row_tile(R, C, 1 << 17)

    def body(w_ref, g_ref, m_ref, v_ref, d_ref, nm_ref, nv_ref):
        d_ref[...], nm_ref[...], nv_ref[...] = _adamw_update(w_ref[...], g_ref[...], m_ref[...], v_ref[...])

    spec = pl.BlockSpec((tr, C), lambda i: (i, 0))
    return pl.pallas_call(body, name=name, grid=(R // tr,), in_specs=[spec] * 4, out_specs=[spec] * 3,
                          out_shape=[_sds((R, C), F32)] * 3, compiler_params=_params(1))(w, g, m, v)


def _sum_share(name, own, by_chip, chip, others, core):
    k, half, C = by_chip.shape
    tr = _row_tile(half, C, mult=16)
    nt = half // tr

    def body(chip_ref, oth_ref, c_ref, own_ref, a_ref, b_ref, cc_ref, g_out, mine, slot, send, recv):
        p = pl.program_id(1)
        b = pl.program_id(0) % 2
        x, y, c = _place()
        cp = pltpu.make_async_remote_copy(src_ref=mine.at[b], dst_ref=slot.at[b], send_sem=send.at[b],
                                          recv_sem=recv.at[b], device_id=(x, y, 1 - c), device_id_type=MESH)

        @pl.when(p == 0)
        def _():
            tot = ((own_ref[...] + a_ref[...].astype(F32)) + b_ref[...].astype(F32)) + cc_ref[...].astype(F32)
            mine[b] = tot
            cp.start()
            g_out[...] = tot

        @pl.when(p == 1)
        def _():
            cp.wait_recv()
            g_out[...] = slot[b]
            cp.wait_send()

    def piece(j):
        return pl.BlockSpec((tr, C), lambda i, p, chip, oth, c: (oth[j] * nt + i, 0))

    grid_spec = pltpu.PrefetchScalarGridSpec(
        num_scalar_prefetch=3, grid=(nt, 2),
        in_specs=[pl.BlockSpec((tr, C), lambda i, p, chip, oth, c: (chip[0] * nt + i, 0)),
                  piece(0), piece(1), piece(2)],
        out_specs=pl.BlockSpec((tr, C), lambda i, p, chip, oth, c: (
            jnp.where(p == 0, c[0], 1 - c[0]) * nt + i, 0)),
        scratch_shapes=[pltpu.VMEM((2, tr, C), F32), pltpu.VMEM((2, tr, C), F32), pltpu.SemaphoreType.DMA((2,)),
                        pltpu.SemaphoreType.DMA((2,))])
    by2 = by_chip.reshape(k * half, C)
    return pl.pallas_call(body, name=name, grid_spec=grid_spec, out_shape=_sds((2 * half, C), F32),
                          compiler_params=_params(2))(chip, others, core, own, by2, by2, by2)


BIG = ("w_in", "w_proj_a", "w_proj_b", "w_out", "w_up", "w_down")
COL_SHARDED = ("w_in", "w_proj_a", "w_up")
SMALL = ("norm1_g", "q_norm_g", "k_norm_g", "ret_gn_g", "ret_gn_b", "norm2_g")
ALL_W = ("norm1_g", "w_in", "q_norm_g", "k_norm_g", "ret_gn_g", "ret_gn_b", "w_proj_a", "w_proj_b", "w_out",
         "norm2_g", "w_up", "w_down")
LANES = 128


def _to_full(name, gathered):
    k, r, c = gathered.shape
    if name in COL_SHARDED:
        return gathered.transpose(1, 0, 2).reshape(r, k * c)
    return gathered.reshape(k * r, c)


def _to_shard_major(name, full):
    if name in COL_SHARDED:
        r, c4 = full.shape
        return full.reshape(r, N_CHIPS, c4 // N_CHIPS).transpose(1, 0, 2)
    r4, c = full.shape
    return full.reshape(N_CHIPS, r4 // N_CHIPS, c)


def kernel(x, norm1_g, w_in, q_norm_g, k_norm_g, ret_gn_g, ret_gn_b, w_proj_a, w_proj_b, w_out, norm2_g, w_up, w_down, loss_target, m_norm1_g, m_w_in, m_q_norm_g, m_k_norm_g, m_ret_gn_g, m_ret_gn_b, m_w_proj_a, m_w_proj_b, m_w_out, m_norm2_g, m_w_up, m_w_down, v_norm1_g, v_w_in, v_q_norm_g, v_k_norm_g, v_ret_gn_g, v_ret_gn_b, v_w_proj_a, v_w_proj_b, v_w_out, v_norm2_g, v_w_up, v_w_down):
    weights = dict(norm1_g=norm1_g, w_in=w_in, q_norm_g=q_norm_g, k_norm_g=k_norm_g, ret_gn_g=ret_gn_g,
                   ret_gn_b=ret_gn_b, w_proj_a=w_proj_a, w_proj_b=w_proj_b, w_out=w_out, norm2_g=norm2_g,
                   w_up=w_up, w_down=w_down)
    moments_m = dict(norm1_g=m_norm1_g, w_in=m_w_in, q_norm_g=m_q_norm_g, k_norm_g=m_k_norm_g, ret_gn_g=m_ret_gn_g,
                     ret_gn_b=m_ret_gn_b, w_proj_a=m_w_proj_a, w_proj_b=m_w_proj_b, w_out=m_w_out,
                     norm2_g=m_norm2_g, w_up=m_w_up, w_down=m_w_down)
    moments_v = dict(norm1_g=v_norm1_g, w_in=v_w_in, q_norm_g=v_q_norm_g, k_norm_g=v_k_norm_g, ret_gn_g=v_ret_gn_g,
                     ret_gn_b=v_ret_gn_b, w_proj_a=v_w_proj_a, w_proj_b=v_w_proj_b, w_out=v_w_out,
                     norm2_g=v_norm2_g, w_up=v_w_up, w_down=v_w_down)

    mx, my = lax.axis_index("x"), lax.axis_index("y")
    core = lax.axis_index("c").astype(jnp.int32).reshape(1)
    chip = (2 * mx + my).astype(jnp.int32).reshape(1)
    others = jnp.stack([2 * (1 - mx) + my, 2 * mx + 1 - my, 2 * (1 - mx) + 1 - my]).astype(jnp.int32)
    shards = {n: weights[n][0].astype(CDT) for n in BIG}
    def start_gather(name, names):
        return _exchange_start(name, "gather", [shards[n] for n in names],
                               [((N_CHIPS,) + shards[n].shape, CDT) for n in names])

    i_send, i_recv, i_srcs, i_lands, i_token = start_gather("gather_w_in_start", ["w_in"])
    late = [n for n in BIG if n != "w_in"]
    l_send, l_recv, l_srcs, l_lands, l_token = start_gather("gather_late_start", late)

    def far_w_in(after):
        srcs, lands = _exchange_wait("gather_w_in_wait", "gather", i_send, i_recv, i_srcs, i_lands, after)
        return _pair_fill("pair_fill_w_in", lands[0], srcs[0], core, others, chip)

    def late_weights(after):
        srcs, lands = _exchange_wait("gather_late_wait", "gather", l_send, l_recv, l_srcs, l_lands, after)
        out = {}
        for n, mine, land in zip(late, srcs, lands):
            out[n] = _to_full(n, _pair_fill("pair_fill_%s" % n, land, mine, core, others, chip))
        return out

    pending = []

    def on_grads(group):
        names = list(group)
        red = [_pair_reduce("pair_reduce_%s" % n, g if g.ndim == 3 else _to_shard_major(n, g), core)
               for n, g in group.items()]
        wires = [wire for _, wire in red]
        send, recv, srcs, lands, token = _exchange_start(
            "scatter_start_%s" % names[0], "scatter", wires, [(wire.shape, wire.dtype) for wire in wires])
        pending.append((names, [own for own, _ in red], send, recv, srcs, lands))
        return token

    small = {n: weights[n].reshape(1, -1) for n in SMALL}

    loss, grad_x, small_g = _local_step(x[0], loss_target[0], i_srcs[0], chip, others, far_w_in, small,
                                        late_weights, on_grads, deps0=[i_token, l_token])
    loss = lax.psum(loss, ("x", "y", "c"))

    out_g, out_d, out_m, out_v = {}, {}, {}, {}
    for names, owns, send, recv, srcs, lands in pending:
        _, got = _exchange_wait("scatter_wait_%s" % names[0], "scatter", send, recv, srcs, lands, grad_x)
        for n, own, by_chip in zip(names, owns, got):
            shape = weights[n].shape
            g2 = _sum_share("sum_share_%s" % n, own, by_chip, chip, others, core)
            d, nm, nv = _adamw("adamw_%s" % n, weights[n][0], g2, moments_m[n][0], moments_v[n][0])
            out_g[n], out_d[n], out_m[n], out_v[n] = (t.reshape(shape) for t in (g2, d, nm, nv))

    packed = jnp.concatenate([small_g[n].reshape(1, -1) for n in SMALL], axis=1)
    red = _all_reduce_small(packed.reshape(-1, LANES)).reshape(1, -1)
    off = 0
    for n in SMALL:
        shape = weights[n].shape
        row = (1, weights[n].size)
        g2 = red[:, off:off + row[1]]
        off += row[1]
        d, nm, nv = _adamw("adamw_%s" % n, weights[n].reshape(row), g2, moments_m[n].reshape(row),
                           moments_v[n].reshape(row))
        out_g[n], out_d[n], out_m[n], out_v[n] = (t.reshape(shape) for t in (g2, d, nm, nv))

    return (loss, grad_x[None], *[out_g[n] for n in ALL_W], *[out_d[n] for n in ALL_W],
            *[out_m[n] for n in ALL_W], *[out_v[n] for n in ALL_W])
```

```python
import functools
import math

import jax
import jax.numpy as jnp
from jax import lax
from jax.experimental import pallas as pl
from jax.experimental.pallas import tpu as pltpu

CDT = jnp.bfloat16
F32 = jnp.float32
EPS = 1e-6

ATT_GROUPS = ((128, 1), (512, 4), (2048, 16))
ATT_HPG = 4
ATT_HEADS = 12
HD = 128
BLK = 128
ATT_W = ATT_HEADS * HD
GW = ATT_HPG * HD
RET_HEADS = 4

ADAM_LR = 0.001
ADAM_B1 = 0.9
ADAM_B2 = 0.999
ADAM_EPS = 1e-08
ADAM_WD = 0.01
ADAM_STEP = 10

VMEM_LIMIT_BYTES = 56 * 1024 * 1024
MXU_DIM = 256
MESH = pl.DeviceIdType.MESH
HBM_SPEC = pl.BlockSpec(memory_space=pltpu.HBM)
VMEM_SPEC = pl.BlockSpec(memory_space=pltpu.VMEM)


def _params(n_axes, collective_id=None):
    return pltpu.CompilerParams(dimension_semantics=("arbitrary",) * n_axes,
                                vmem_limit_bytes=VMEM_LIMIT_BYTES, collective_id=collective_id)


PAIR_FILL_ID, PAIR_REDUCE_ID, SUM_SHARE_ID = 1, 2, 3


def _sibling_barrier(first_step):
    @pl.when(first_step)
    def _():
        sem = pltpu.get_barrier_semaphore()
        x, y, c = lax.axis_index("x"), lax.axis_index("y"), lax.axis_index("c")
        pl.semaphore_signal(sem, inc=1, device_id=(x, y, 1 - c), device_id_type=pl.DeviceIdType.MESH)
        pl.semaphore_wait(sem, 1)


def _dot_nn(a, b):
    return jnp.dot(a, b, preferred_element_type=F32)


def _dot_nt(a, b):
    return lax.dot_general(a, b, (((1,), (1,)), ((), ())), preferred_element_type=F32)


def _dot_tn(a, b):
    return lax.dot_general(a, b, (((0,), (0,)), ((), ())), preferred_element_type=F32)


def _sigmoid(v):
    return 1.0 / (1.0 + jnp.exp(-v))


def _matmul(name, mode, a, b, *, tm, tn, tk, extras=(), outs, epilogue, deps=(), b_spec=None, n_cols=None,
            prefetch=(), alias_dep_to_out=None, j_outer=False):
    deps = [d for d in deps if d is not None]
    if mode == "tn":
        K, M = a.shape
    else:
        M, K = a.shape
    if b_spec is None:
        (N, K2) = b.shape if mode == "nt" else b.shape[::-1]
        assert K == K2, (name, a.shape, b.shape)
        if mode == "nt":
            b_spec = pl.BlockSpec((tn, tk), lambda i, j, k, *p: (j, k))
        else:
            b_spec = pl.BlockSpec((tk, tn), lambda i, j, k, *p: (k, j))
    else:
        N = n_cols
    assert M % tm == 0 and N % tn == 0 and K % tk == 0, (name, a.shape, b.shape)
    ni, nj, nk = M // tm, N // tn, K // tk
    if mode == "tn":
        a_spec = pl.BlockSpec((tk, tm), lambda i, j, k, *p: (k, i))
    else:
        a_spec = pl.BlockSpec((tm, tk), lambda i, j, k, *p: (i, k))
    dot = {"nn": _dot_nn, "nt": _dot_nt, "tn": _dot_tn}[mode]
    n_ex, n_out, n_dep, n_pre = len(extras), len(outs), len(deps), len(prefetch)
    grid = (ni, nj, nk)
    if j_outer:
        grid = (nj, ni, nk)

        def swapped(spec):
            return pl.BlockSpec(spec.block_shape, lambda j, i, k, *p: spec.index_map(i, j, k, *p))

        a_spec, b_spec = swapped(a_spec), swapped(b_spec)
        extras = [(e, swapped(s)) for e, s in extras]
        outs = [(o, swapped(s)) for o, s in outs]

    def body(*refs):
        refs = refs[n_pre:]
        a_ref, b_ref = refs[0], refs[1]
        ex = refs[2:2 + n_ex]
        out = refs[2 + n_ex + n_dep:2 + n_ex + n_dep + n_out]
        acc = refs[-1] if nk > 1 else None
        i = pl.program_id(1 if j_outer else 0)
        k = pl.program_id(2)
        if nk == 1:
            epilogue(dot(a_ref[...].astype(CDT), b_ref[...].astype(CDT)), ex, out, i)
            return

        @pl.when(k == 0)
        def _():
            acc[...] = jnp.zeros_like(acc)

        acc[...] += dot(a_ref[...].astype(CDT), b_ref[...].astype(CDT))

        @pl.when(k == nk - 1)
        def _():
            epilogue(acc[...], ex, out, i)

    grid_spec = pltpu.PrefetchScalarGridSpec(
        num_scalar_prefetch=n_pre, grid=grid,
        in_specs=[a_spec, b_spec] + [s for _, s in extras] + [pl.BlockSpec(memory_space=pl.ANY)] * n_dep,
        out_specs=[s for _, s in outs],
        scratch_shapes=[pltpu.VMEM((tm, tn), F32)] if nk > 1 else [])
    aliases = {}
    if alias_dep_to_out is not None:
        aliases = {n_pre + 2 + n_ex + alias_dep_to_out[0]: alias_dep_to_out[1]}
    res = pl.pallas_call(
        body, name=name, grid_spec=grid_spec, out_shape=[o for o, _ in outs], input_output_aliases=aliases,
        compiler_params=_params(3),
    )(*prefetch, a, b, *[e for e, _ in extras], *deps)
    return res


def _mn(tm, tn, col_off=0):
    return pl.BlockSpec((tm, tn), lambda i, j, k, *p: (i, j + col_off))


def _row(tn):
    return pl.BlockSpec((1, tn), lambda i, j, k, *p: (0, j))


def _ep_store(acc, ex, out, i):
    out[0][...] = acc.astype(out[0].dtype)


def _ep_resid(acc, ex, out, i):
    out[0][...] = ex[0][...] + acc


def _ep_up(acc, ex, out, i):
    out[0][...] = acc.astype(out[0].dtype)
    r = jnp.maximum(acc, 0.0)
    out[1][...] = (r * r).astype(out[1].dtype)


def _ep_down_loss(acc, ex, out, i, inv_d):
    diff = (ex[0][...] + acc) - ex[1][...]
    dx2 = diff * inv_d
    out[0][...] = dx2
    out[1][...] = dx2.astype(out[1].dtype)

    @pl.when(i == 0)
    def _():
        out[2][...] = jnp.zeros_like(out[2])

    out[2][...] += jnp.sum(diff * diff, axis=0, keepdims=True)


def _ep_dh(acc, ex, out, i):
    h = ex[0][...].astype(F32)
    out[0][...] = (acc * (2.0 * jnp.maximum(h, 0.0))).astype(out[0].dtype)


def _ep_rms_bwd(acc, ex, out, i):
    x = ex[0][...]
    g = ex[1][...]
    rstd = lax.rsqrt(jnp.mean(x * x, axis=-1, keepdims=True) + EPS)
    xh = x * rstd
    dxh = acc * g
    dx = ex[2][...] + rstd * (dxh - xh * jnp.mean(dxh * xh, axis=-1, keepdims=True))
    out[0][...] = dx
    for copy in out[1:-1]:
        copy[...] = dx.astype(copy.dtype)
    dg = out[-1]

    @pl.when(i == 0)
    def _():
        dg[...] = jnp.zeros_like(dg)

    dg[...] += jnp.sum(acc * xh, axis=0, keepdims=True)


def _ep_gates(acc, ex, out, i):
    sa = _sigmoid(ex[0][...].astype(F32))
    sb = _sigmoid(ex[1][...].astype(F32))
    dpa = acc * sa
    dpb = acc * sb
    out[0][...] = dpa.astype(out[0].dtype)
    out[1][...] = dpb.astype(out[1].dtype)
    out[2][...] = (dpa * ex[2][...].astype(F32) * (1.0 - sa)).astype(out[2].dtype)
    out[3][...] = (dpb * ex[3][...].astype(F32) * (1.0 - sb)).astype(out[3].dtype)


def _sds(shape, dtype):
    return jax.ShapeDtypeStruct(shape, dtype)


def _rms_fwd(name, x, g, tm=512):
    S, D = x.shape

    def body(x_ref, g_ref, o_ref):
        xv = x_ref[...]
        rstd = lax.rsqrt(jnp.mean(xv * xv, axis=-1, keepdims=True) + EPS)
        o_ref[...] = (xv * rstd * g_ref[...]).astype(o_ref.dtype)

    return pl.pallas_call(
        body, name=name, grid=(S // tm,),
        in_specs=[pl.BlockSpec((tm, D), lambda i: (i, 0)), pl.BlockSpec((1, D), lambda i: (0, 0))],
        out_specs=pl.BlockSpec((tm, D), lambda i: (i, 0)),
        out_shape=_sds((S, D), CDT), compiler_params=_params(1))(x, g)


def _rm_shape(S, d, width):
    return (S, width) if d == 1 else (d, S // d, width)


def _rm_spec(tm, d, width):
    if d == 1:
        return pl.BlockSpec((tm, width), lambda i: (i, 0))
    return pl.BlockSpec((d, tm // d, width), lambda i: (0, i, 0))


def _rm_put(dst_ref, cols, buf_ref, d):
    if d == 1:
        dst_ref[:, cols] = buf_ref[...].astype(dst_ref.dtype)
        return
    m = buf_ref.shape[0] // d
    for r in range(d):
        dst_ref[r, :, cols] = buf_ref[pl.ds(r, m, stride=d), :].astype(dst_ref.dtype)


def _rm_reader(buf_ref, src_ref, d):
    if d == 1:
        return lambda s, rows: src_ref[rows, s * HD:(s + 1) * HD].astype(F32)
    m = buf_ref.shape[1] // d
    for s in range(buf_ref.shape[0]):
        for r in range(d):
            buf_ref.at[s][pl.ds(r, m, stride=d), :] = src_ref[r, :, s * HD:(s + 1) * HD].astype(F32)
    return lambda s, rows: buf_ref.at[s][rows, :]


def _qknorm_fwd(proj, gqk, tm=512):
    S = proj.shape[0]
    W = 2 * ATT_W
    dil = [d for _, d in ATT_GROUPS]

    def body(p_ref, g_ref, o0, o1, o2, buf):
        outs = (o0, o1, o2)
        for hd in range(3 * ATT_HEADS):
            which, head = hd // ATT_HEADS, hd % ATT_HEADS
            grp, slot = head // ATT_HPG, head % ATT_HPG
            cols = slice(hd * HD, (hd + 1) * HD)

            def chunk(rows, which=which, cols=cols):
                v = p_ref[rows, cols].astype(F32)
                if which < 2:
                    rstd = lax.rsqrt(jnp.mean(v * v, axis=-1, keepdims=True) + EPS)
                    v = v * rstd * g_ref[:, cols]
                buf[rows, :] = v

            chunk(slice(None))
            _rm_put(outs[grp], slice(which * GW + slot * HD, which * GW + (slot + 1) * HD), buf, dil[grp])

    return pl.pallas_call(
        body, name="qknorm_fwd", grid=(S // tm,),
        in_specs=[pl.BlockSpec((tm, 3 * ATT_W), lambda i: (i, 0)), pl.BlockSpec((1, W), lambda i: (0, 0))],
        out_specs=[_rm_spec(tm, d, 3 * GW) for d in dil],
        out_shape=[_sds(_rm_shape(S, d, 3 * GW), CDT) for d in dil],
        scratch_shapes=[pltpu.VMEM((tm, HD), F32)],
        compiler_params=_params(1))(proj, gqk)


def _qknorm_bwd(proj, gqk, dqs, dks, dvs, dproj, tm=256):
    S = proj.shape[0]
    W = 2 * ATT_W
    dil = [d for _, d in ATT_GROUPS]

    def body(p_ref, g_ref, *refs):
        ins = refs[0:9]
        o_ref, dg_ref = refs[10], refs[11]
        bufs = refs[12:21]
        i = pl.program_id(0)

        @pl.when(i == 0)
        def _():
            dg_ref[...] = jnp.zeros_like(dg_ref)

        nat = [_rm_reader(bufs[j], ins[j], dil[j % 3]) for j in range(9)]
        dq_get, dk_get, dv_get = nat[0:3], nat[3:6], nat[6:9]
        for hd in range(2 * ATT_HEADS):
            sl = slice(hd * HD, (hd + 1) * HD)
            head = hd % ATT_HEADS
            grp, slot = head // ATT_HPG, head % ATT_HPG
            get = (dq_get if hd < ATT_HEADS else dk_get)[grp]

            def chunk(rows, sl=sl, slot=slot, get=get):
                dn = get(slot, rows)
                v = p_ref[rows, sl].astype(F32)
                rstd = lax.rsqrt(jnp.mean(v * v, axis=-1, keepdims=True) + EPS)
                vh = v * rstd
                dg_ref[:, sl] += jnp.sum(dn * vh, axis=0, keepdims=True)
                dvh = dn * g_ref[:, sl]
                o_ref[rows, sl] = (rstd * (dvh - vh * jnp.mean(dvh * vh, axis=-1, keepdims=True))).astype(o_ref.dtype)

            chunk(slice(None))
        for head in range(ATT_HEADS):
            grp, slot = head // ATT_HPG, head % ATT_HPG
            o_ref[:, W + head * HD:W + (head + 1) * HD] = dv_get[grp](slot, slice(None)).astype(o_ref.dtype)

    return pl.pallas_call(
        body, name="qknorm_bwd", grid=(S // tm,),
        in_specs=[pl.BlockSpec((tm, W), lambda i: (i, 0)), pl.BlockSpec((1, W), lambda i: (0, 0))]
        + [_rm_spec(tm, d, GW) for d in dil] * 3 + [pl.BlockSpec(memory_space=pl.ANY)],
        out_specs=[pl.BlockSpec((tm, 3 * ATT_W), lambda i: (i, 0)), pl.BlockSpec((1, W), lambda i: (0, 0))],
        out_shape=[_sds(dproj.shape, dproj.dtype), _sds((1, W), F32)],
        scratch_shapes=[pltpu.VMEM((ATT_HPG, tm, HD), F32)] * 9,
        input_output_aliases={11: 0},
        compiler_params=_params(1))(proj, gqk, *dqs, *dks, *dvs, dproj)


def _att_mask(n):
    qi = lax.broadcasted_iota(jnp.int32, (BLK, 2 * BLK), 0)
    kj = lax.broadcasted_iota(jnp.int32, (BLK, 2 * BLK), 1)
    dist = BLK + qi - kj
    valid = (dist >= 0) & (dist <= BLK) & ((kj >= BLK) | (n > 0))
    return valid, dist.astype(F32)


def _att_slopes(grp):
    return [2.0 ** (-8.0 * (grp * ATT_HPG + hh + 1) / ATT_HEADS) for hh in range(ATT_HPG)]


def _att_spec(d, row_fn, col=0):
    if d == 1:
        return pl.BlockSpec((BLK, GW), lambda r, n: (row_fn(n), col))
    return pl.BlockSpec((None, BLK, GW), lambda r, n: (r, row_fn(n), col))


def _att_qkv_specs(d, nb):
    last = nb - 1

    def cur(n):
        return jnp.minimum(n, last)

    def prev(n):
        return jnp.maximum(jnp.minimum(n, last) - 1, 0)

    return [_att_spec(d, cur, 0), _att_spec(d, prev, 1), _att_spec(d, cur, 1), _att_spec(d, prev, 2),
            _att_spec(d, cur, 2)]


def _att_fwd(grp, S, qkv):
    _, d = ATT_GROUPS[grp]
    L = S // d
    nb = L // BLK
    slopes = _att_slopes(grp)
    scale = HD ** -0.5

    def body(q_ref, kp_ref, kc_ref, vp_ref, vc_ref, o_ref, l_ref, s_buf, p_buf, den_buf):
        n = pl.program_id(1)
        valid, distf = _att_mask(n)
        heads = [slice(hh * HD, (hh + 1) * HD) for hh in range(ATT_HPG)]
        for hh, sl in enumerate(heads):
            k = jnp.concatenate([kp_ref[:, sl], kc_ref[:, sl]], axis=0)
            s_buf[hh] = _dot_nt(q_ref[:, sl], k)
        for hh, sl in enumerate(heads):
            s = s_buf[hh] * scale + (-slopes[hh] * d) * distf
            s = jnp.where(valid, s, -1e30)
            m = jnp.max(s, axis=-1, keepdims=True)
            p = jnp.exp(s - m)
            den = jnp.sum(p, axis=-1, keepdims=True)
            p_buf[hh] = p.astype(CDT)
            den_buf[hh] = jnp.broadcast_to(den, (BLK, HD))
            l_ref[:, sl] = jnp.broadcast_to(m + jnp.log(den), (BLK, HD))
        for hh, sl in enumerate(heads):
            v = jnp.concatenate([vp_ref[:, sl], vc_ref[:, sl]], axis=0)
            o_ref[:, sl] = _dot_nn(p_buf[hh], v) / den_buf[hh]

    out_spec = _att_spec(d, lambda n: n)
    return pl.pallas_call(
        body, name="att_fwd_g%d" % grp, grid=(d, nb),
        in_specs=_att_qkv_specs(d, nb),
        out_specs=[out_spec, out_spec],
        out_shape=[_sds(_rm_shape(S, d, GW), F32)] * 2,
        scratch_shapes=[pltpu.VMEM((ATT_HPG, BLK, 2 * BLK), F32), pltpu.VMEM((ATT_HPG, BLK, 2 * BLK), CDT),
                        pltpu.VMEM((ATT_HPG, BLK, HD), F32)],
        compiler_params=_params(2),
    )(qkv, qkv, qkv, qkv, qkv)


def _att_bwd(grp, S, qkv, lse, do_g, c_g):
    _, d = ATT_GROUPS[grp]
    L = S // d
    nb = L // BLK
    slopes = _att_slopes(grp)
    scale = HD ** -0.5
    last = nb - 1

    def body(q_ref, kp_ref, kc_ref, vp_ref, vc_ref, l_ref, do_ref, c_ref, dq_ref, dk_ref, dv_ref, ck, cv,
             s_buf, dp_buf, p_buf, ds_buf):
        n = pl.program_id(1)

        @pl.when(n == 0)
        def _():
            ck[...] = jnp.zeros_like(ck)
            cv[...] = jnp.zeros_like(cv)

        @pl.when(n < nb)
        def _():
            valid, distf = _att_mask(n)
            heads = [slice(hh * HD, (hh + 1) * HD) for hh in range(ATT_HPG)]
            for hh, sl in enumerate(heads):
                k = jnp.concatenate([kp_ref[:, sl], kc_ref[:, sl]], axis=0)
                v = jnp.concatenate([vp_ref[:, sl], vc_ref[:, sl]], axis=0)
                s_buf[hh] = _dot_nt(q_ref[:, sl], k)
                dp_buf[hh] = _dot_nt(do_ref[:, sl], v)
            for hh, sl in enumerate(heads):
                s = s_buf[hh] * scale + (-slopes[hh] * d) * distf
                p = jnp.where(valid, jnp.exp(s - l_ref[:, sl][:, 0:1]), 0.0)
                p_buf[hh] = p.astype(CDT)
                ds_buf[hh] = (p * (dp_buf[hh] + c_ref[:, sl][:, 0:1]) * scale).astype(CDT)
            for hh, sl in enumerate(heads):
                k = jnp.concatenate([kp_ref[:, sl], kc_ref[:, sl]], axis=0)
                ds = ds_buf[hh]
                dq_ref[:, sl] = _dot_nn(ds, k)
                dk = _dot_tn(ds, q_ref[:, sl])
                dv = _dot_tn(p_buf[hh], do_ref[:, sl])
                dk_ref[:, sl] = ck[:, sl] + dk[0:BLK]
                dv_ref[:, sl] = cv[:, sl] + dv[0:BLK]
                ck[:, sl] = dk[BLK:2 * BLK]
                cv[:, sl] = dv[BLK:2 * BLK]

        @pl.when(n == nb)
        def _():
            dk_ref[...] = ck[...]
            dv_ref[...] = cv[...]

    blk = (BLK, GW)
    at_q = _att_spec(d, lambda n: jnp.minimum(n, last))
    behind = _att_spec(d, lambda n: jnp.maximum(n - 1, 0))
    return pl.pallas_call(
        body, name="att_bwd_g%d" % grp, grid=(d, nb + 1),
        in_specs=_att_qkv_specs(d, nb) + [at_q, at_q, at_q],
        out_specs=[at_q, behind, behind],
        out_shape=[_sds(_rm_shape(S, d, GW), F32)] * 3,
        scratch_shapes=[pltpu.VMEM(blk, F32), pltpu.VMEM(blk, F32),
                        pltpu.VMEM((ATT_HPG, BLK, 2 * BLK), F32), pltpu.VMEM((ATT_HPG, BLK, 2 * BLK), F32),
                        pltpu.VMEM((ATT_HPG, BLK, 2 * BLK), CDT), pltpu.VMEM((ATT_HPG, BLK, 2 * BLK), CDT)],
        compiler_params=_params(2),
    )(qkv, qkv, qkv, qkv, qkv, lse, do_g, c_g)


def _mix_alpha(l0, l1, l2):
    mx = jnp.maximum(jnp.maximum(l0, l1), l2)
    e = [jnp.exp(l0 - mx), jnp.exp(l1 - mx), jnp.exp(l2 - mx)]
    tot = e[0] + e[1] + e[2]
    return [ei / tot for ei in e]


def _mix_fwd(S, os_, ls_, tm=512):
    dil = [d for _, d in ATT_GROUPS]

    def body(*refs):
        out, bufs = refs[6], refs[7:13]
        get = [_rm_reader(bufs[j], refs[j], dil[j % 3]) for j in range(6)]
        for s in range(ATT_HPG):
            def chunk(rows, s=s):
                al = _mix_alpha(*[get[3 + g](s, rows) for g in range(3)])
                mixed = al[0] * get[0](s, rows) + al[1] * get[1](s, rows) + al[2] * get[2](s, rows)
                out[rows, s * HD:(s + 1) * HD] = mixed.astype(out.dtype)

            chunk(slice(None))

    specs = [_rm_spec(tm, d, GW) for d in dil]
    return pl.pallas_call(
        body, name="mix_fwd", grid=(S // tm,), in_specs=specs * 2, out_specs=pl.BlockSpec((tm, GW), lambda i: (i, 0)),
        out_shape=_sds((S, GW), CDT), scratch_shapes=[pltpu.VMEM((ATT_HPG, tm, HD), F32)] * 6,
        compiler_params=_params(1))(*os_, *ls_)


def _mix_bwd(S, os_, ls_, do_a, tm=512):
    dil = [d for _, d in ATT_GROUPS]

    def body(*refs):
        d_ref, outs, bufs, tmps = refs[6], refs[7:13], refs[13:19], refs[19:25]
        get = [_rm_reader(bufs[j], refs[j], dil[j % 3]) for j in range(6)]
        for s in range(ATT_HPG):
            cols = slice(s * HD, (s + 1) * HD)

            def chunk(rows, s=s, cols=cols):
                al = _mix_alpha(*[get[3 + g](s, rows) for g in range(3)])
                dv = d_ref[rows, cols]
                o_a = al[0] * get[0](s, rows) + al[1] * get[1](s, rows) + al[2] * get[2](s, rows)
                dsum = jnp.sum(dv * o_a, axis=-1, keepdims=True)
                for g in range(3):
                    tmps[g][rows, :] = al[g] * dv
                    tmps[3 + g][rows, :] = -(al[g] * dsum)

            chunk(slice(None))
            for j in range(6):
                _rm_put(outs[j], cols, tmps[j], dil[j % 3])

    specs = [_rm_spec(tm, d, GW) for d in dil]
    res = pl.pallas_call(
        body, name="mix_bwd", grid=(S // tm,), in_specs=specs * 2 + [pl.BlockSpec((tm, GW), lambda i: (i, 0))],
        out_specs=specs * 2,
        out_shape=[_sds(_rm_shape(S, d, GW), CDT) for d in dil] + [_sds(_rm_shape(S, d, GW), F32) for d in dil],
        scratch_shapes=[pltpu.VMEM((ATT_HPG, tm, HD), F32)] * 6 + [pltpu.VMEM((tm, HD), F32)] * 6,
        compiler_params=_params(1))(*os_, *ls_, do_a)
    return res[:3], res[3:]


def _ret_tables(dk):
    H, C = RET_HEADS, BLK
    log_g = jnp.log(1.0 - 2.0 ** (-5.0 - jnp.arange(H, dtype=F32)))
    idx = jnp.arange(C, dtype=F32)
    diff = idx[:, None] - idx[None, :]
    decay = jnp.where(diff >= 0, jnp.exp(log_g[:, None, None] * jnp.maximum(diff, 0.0)), 0.0)
    xi = jnp.exp(log_g[:, None] * (idx[None, :] + 1.0))
    zeta = jnp.exp(log_g[:, None] * (C - 1.0 - idx[None, :])) * (dk ** -0.5)
    g_chunk = jnp.exp(log_g * C)
    bc = lambda t: jnp.broadcast_to(t[:, :, None], (H, C, C))
    return decay, bc(xi), bc(zeta), jnp.broadcast_to(g_chunk[:, None, None], (H, 8, C))


def _gn_fwd(o, g, b):
    mu = jnp.mean(o, axis=-1, keepdims=True)
    xc = o - mu
    rstd = lax.rsqrt(jnp.mean(xc * xc, axis=-1, keepdims=True) + EPS)
    yh = xc * rstd
    return yh, rstd, yh * g + b


def _ret_specs(dk, dv, order):
    H = RET_HEADS
    qk_w, v_w = H * dk, H * dv
    off_q = 3 * ATT_W
    off_k, off_v, off_g = off_q + qk_w, off_q + 2 * qk_w, off_q + 2 * qk_w + v_w
    assert 2 * dk == dv and all(off % dv == 0 for off in (off_q, off_k, off_v, off_g))

    def col(off, j):
        return pl.BlockSpec((BLK, dv), lambda i: (order(i), off // dv + j))

    tab = pl.BlockSpec((H, BLK, BLK), lambda i: (0, 0, 0))
    return ([col(off_q, j) for j in range(H // 2)] + [col(off_k, j) for j in range(H // 2)]
            + [col(off_v, j) for j in range(H)] + [col(off_g, j) for j in range(H)]
            + [tab, tab, tab, pl.BlockSpec((H, 8, BLK), lambda i: (0, 0, 0))])


def _ret_heads(refs, dk):
    H = RET_HEADS
    q_refs, k_refs = refs[0:H // 2], refs[H // 2:H]
    v_refs, gr_refs = refs[H:2 * H], refs[2 * H:3 * H]

    def head(h):
        cols = slice((h % 2) * dk, (h % 2 + 1) * dk)
        return q_refs[h // 2][:, cols], k_refs[h // 2][:, cols], v_refs[h][...], gr_refs[h][...]

    return head, refs[3 * H:3 * H + 4]


def _ret_fwd(proj, gn_g, gn_b, dk, dv):
    S = proj.shape[0]
    N = S // BLK
    H = RET_HEADS
    kscale = dk ** -0.5
    n_in = 3 * H + 4

    def body(*refs):
        head, (dec_ref, xi_ref, zeta_ref, gc_ref) = _ret_heads(refs, dk)
        g_ref, b_ref, opre_ref, or_ref, st_ref, state, s_buf, cross_buf = refs[n_in:n_in + 8]
        n = pl.program_id(0)

        @pl.when(n == 0)
        def _():
            state[...] = jnp.zeros_like(state)

        for h in range(H):
            q, k, v, _ = head(h)
            s_buf[h] = _dot_nt(q, k)
            st = state[h]
            st_c = st.astype(CDT)
            st_ref[h] = st_c
            cross_buf[h] = _dot_nn(q, st_c)
            kz = (k.astype(F32) * zeta_ref[h][:, 0:1]).astype(CDT)
            state[h] = st * gc_ref[h][0:1, 0:1] + _dot_tn(kz, v)
        for h in range(H):
            vs = slice(h * dv, (h + 1) * dv)
            _, _, v, gr = head(h)
            s = s_buf[h] * kscale * dec_ref[h]
            o = _dot_nn(s.astype(CDT), v) + cross_buf[h] * xi_ref[h][:, 0:1]
            opre_ref[:, vs] = o
            _, _, y = _gn_fwd(o, g_ref[:, vs], b_ref[:, vs])
            gr = gr.astype(F32)
            or_ref[:, vs] = (y * (gr * _sigmoid(gr))).astype(or_ref.dtype)

    v_w = H * dv
    row = pl.BlockSpec((1, v_w), lambda i: (0, 0))
    tile = pl.BlockSpec((BLK, v_w), lambda i: (i, 0))
    return pl.pallas_call(
        body, name="ret_fwd", grid=(N,),
        in_specs=_ret_specs(dk, dv, lambda i: i) + [row, row],
        out_specs=[tile, tile, pl.BlockSpec((None, H, dk, dv), lambda i: (i, 0, 0, 0))],
        out_shape=[_sds((S, v_w), F32), _sds((S, v_w), CDT), _sds((N, H, dk, dv), CDT)],
        scratch_shapes=[pltpu.VMEM((H, dk, dv), F32), pltpu.VMEM((H, BLK, BLK), F32), pltpu.VMEM((H, BLK, dv), F32)],
        compiler_params=_params(1),
    )(*[proj] * (3 * H), *_ret_tables(dk), gn_g, gn_b)


def _ret_bwd(proj, gn_g, gn_b, o_pre, states, d_or, dga, dgb, dk, dv):
    S, in_w = proj.shape
    N = S // BLK
    H = RET_HEADS
    qk_w, v_w = H * dk, H * dv
    kscale = dk ** -0.5
    n_in = 3 * H + 4
    out_w = 2 * qk_w + 2 * v_w
    gate_w = dga.shape[1]
    col0 = 3 * ATT_W
    assert col0 + out_w + 2 * gate_w == in_w
    rev = lambda i: N - 1 - i

    def body(*refs):
        head, (dec_ref, xi_ref, zeta_ref, gc_ref) = _ret_heads(refs, dk)
        (g_ref, b_ref, opre_ref, st_ref, dor_ref, dga_ref, dgb_ref, dproj_ref, dg_ref, db_ref, dstate, stage,
         sem, do_buf, dox_buf, a_buf, g_buf, dq_buf, dk_buf, dv_buf) = refs[n_in:n_in + 20]
        i = pl.program_id(0)
        slot = i % 2
        out_ref = stage.at[slot]

        def out_copy(s, step):
            rows = pl.ds(pl.multiple_of(rev(step) * BLK, BLK), BLK)
            return pltpu.make_async_copy(stage.at[s], dproj_ref.at[rows, pl.ds(col0, in_w - col0)], sem.at[s])

        @pl.when(i >= 2)
        def _():
            out_copy(slot, i - 2).wait()

        @pl.when(i == 0)
        def _():
            dstate[...] = jnp.zeros_like(dstate)
            dg_ref[...] = jnp.zeros_like(dg_ref)
            db_ref[...] = jnp.zeros_like(db_ref)

        out_ref[:, out_w:out_w + gate_w] = dga_ref[...]
        out_ref[:, out_w + gate_w:out_w + 2 * gate_w] = dgb_ref[...]
        for h in range(H):
            vs = slice(h * dv, (h + 1) * dv)
            _, _, _, gr = head(h)
            gr = gr.astype(F32)
            sg = _sigmoid(gr)
            gain = g_ref[:, vs]
            yh, rstd, y = _gn_fwd(opre_ref[:, vs], gain, b_ref[:, vs])
            d_or_v = dor_ref[:, vs]
            dy = d_or_v * (gr * sg)
            out_ref[:, 2 * qk_w + v_w + h * dv:2 * qk_w + v_w + (h + 1) * dv] = (
                d_or_v * y * (sg * (1.0 + gr * (1.0 - sg)))).astype(out_ref.dtype)
            dg_ref[:, vs] += jnp.sum(dy * yh, axis=0, keepdims=True)
            db_ref[:, vs] += jnp.sum(dy, axis=0, keepdims=True)
            dyh = dy * gain
            do = rstd * (dyh - jnp.mean(dyh, axis=-1, keepdims=True)
                         - yh * jnp.mean(dyh * yh, axis=-1, keepdims=True))
            do_buf[h] = do.astype(CDT)
            dox_buf[h] = (do * xi_ref[h][:, 0:1]).astype(CDT)
        for h in range(H):
            q, k, v, _ = head(h)
            dox = dox_buf[h]
            a_buf[h] = _dot_nt(q, k)
            g_buf[h] = _dot_nt(do_buf[h], v)
            dsn = dstate[h]
            dsn_c = dsn.astype(CDT)
            kz = (k.astype(F32) * zeta_ref[h][:, 0:1]).astype(CDT)
            dq_buf[h] = _dot_nt(dox, st_ref[h])
            dk_buf[h] = _dot_nt(v, dsn_c)
            dv_buf[h] = _dot_nn(kz, dsn_c)
            dstate[h] = dsn * gc_ref[h][0:1, 0:1] + _dot_tn(q, dox)
        for h in range(H):
            q, k, _, _ = head(h)
            decay = dec_ref[h]
            a_c = (a_buf[h] * kscale * decay).astype(CDT)
            g_c = (g_buf[h] * decay).astype(CDT)
            dq = _dot_nn(g_c, k) * kscale + dq_buf[h]
            dkk = _dot_tn(g_c, q) * kscale + dk_buf[h] * zeta_ref[h][:, 0:1]
            dvv = _dot_tn(a_c, do_buf[h]) + dv_buf[h]
            out_ref[:, h * dk:(h + 1) * dk] = dq.astype(out_ref.dtype)
            out_ref[:, qk_w + h * dk:qk_w + (h + 1) * dk] = dkk.astype(out_ref.dtype)
            out_ref[:, 2 * qk_w + h * dv:2 * qk_w + (h + 1) * dv] = dvv.astype(out_ref.dtype)

        cp = out_copy(slot, i)
        cp.start()

        @pl.when(i == N - 1)
        def _():
            cp.wait()
            if N >= 2:
                out_copy(1 - slot, i - 1).wait()

    row = pl.BlockSpec((1, v_w), lambda i: (0, 0))
    tile = pl.BlockSpec((BLK, v_w), lambda i: (rev(i), 0))
    gate = pl.BlockSpec((BLK, gate_w), lambda i: (rev(i), 0))
    return pl.pallas_call(
        body, name="ret_bwd", grid=(N,),
        in_specs=_ret_specs(dk, dv, rev) + [row, row, tile,
                 pl.BlockSpec((None, H, dk, dv), lambda i: (rev(i), 0, 0, 0)), tile, gate, gate],
        out_specs=[pl.BlockSpec(memory_space=pl.ANY), row, row],
        out_shape=[_sds((S, in_w), CDT), _sds((1, v_w), F32), _sds((1, v_w), F32)],
        scratch_shapes=[pltpu.VMEM((H, dk, dv), F32), pltpu.VMEM((2, BLK, in_w - col0), CDT),
                        pltpu.SemaphoreType.DMA((2,)),
                        pltpu.VMEM((H, BLK, dv), CDT), pltpu.VMEM((H, BLK, dv), CDT),
                        pltpu.VMEM((H, BLK, BLK), F32), pltpu.VMEM((H, BLK, BLK), F32),
                        pltpu.VMEM((H, BLK, dk), F32), pltpu.VMEM((H, BLK, dk), F32), pltpu.VMEM((H, BLK, dv), F32)],
        compiler_params=_params(1),
    )(*[proj] * (3 * H), *_ret_tables(dk), gn_g, gn_b, o_pre, states, d_or, dga, dgb)


def _merge_fwd(o_a, o_r, wa, wb, proj, d_model, tm=512, tn=512):
    S, in_w = proj.shape
    off_a, off_b = in_w - 2 * d_model, in_w - d_model
    assert off_a % tn == 0 and off_b % tn == 0

    def body(oa_ref, or_ref, wa_ref, wb_ref, ga_ref, gb_ref, y_ref, pa_ref, pb_ref):
        pa = _dot_nn(oa_ref[...], wa_ref[...])
        pb = _dot_nn(or_ref[...], wb_ref[...])
        y = _sigmoid(ga_ref[...].astype(F32)) * pa + _sigmoid(gb_ref[...].astype(F32)) * pb
        y_ref[...] = y.astype(y_ref.dtype)
        pa_ref[...] = pa.astype(pa_ref.dtype)
        pb_ref[...] = pb.astype(pb_ref.dtype)

    ka, kb = o_a.shape[1], o_r.shape[1]
    out = pl.BlockSpec((tm, tn), lambda i, j: (i, j))
    return pl.pallas_call(
        body, name="merge_fwd", grid=(S // tm, d_model // tn),
        in_specs=[pl.BlockSpec((tm, ka), lambda i, j: (i, 0)), pl.BlockSpec((tm, kb), lambda i, j: (i, 0)),
                  pl.BlockSpec((ka, tn), lambda i, j: (0, j)), pl.BlockSpec((kb, tn), lambda i, j: (0, j)),
                  pl.BlockSpec((tm, tn), lambda i, j: (i, off_a // tn + j)),
                  pl.BlockSpec((tm, tn), lambda i, j: (i, off_b // tn + j))],
        out_specs=[out, out, out], out_shape=[_sds((S, d_model), CDT)] * 3,
        compiler_params=_params(2))(o_a, o_r, wa, wb, proj, proj)


def _local_step(x, target, w_in_mine, chip, others, far_w_in, small, late_weights, on_grads, deps0=()):
    S, D = x.shape
    ns_in = w_in_mine.shape[1]
    in_w = N_CHIPS * ns_in
    d_ff = 4 * D
    ret_v_w = 2 * D
    dv = ret_v_w // RET_HEADS
    dk = (in_w - 3 * ATT_W - 2 * ret_v_w - 2 * D) // (2 * RET_HEADS)
    gqk = jnp.concatenate([small["q_norm_g"].reshape(1, ATT_W), small["k_norm_g"].reshape(1, ATT_W)], axis=1)
    g1, g2 = small["norm1_g"], small["norm2_g"]
    gn_g, gn_b = small["ret_gn_g"], small["ret_gn_b"]

    xn = _rms_fwd("rms1_fwd", x, g1)
    proj_sds = _sds((S, in_w), CDT)
    (proj_mine,) = _matmul(
        "in_proj_mine", "nn", xn, w_in_mine, tm=512, tn=ns_in, tk=D, prefetch=[chip],
        outs=[(proj_sds, pl.BlockSpec((512, ns_in), lambda i, j, k, c: (i, c[0])))], epilogue=_ep_store, deps=deps0)
    w_in = far_w_in(proj_mine)
    (proj,) = _matmul(
        "in_proj_far", "nn", xn, w_in, tm=512, tn=ns_in, tk=D, prefetch=[others], n_cols=(N_CHIPS - 1) * ns_in,
        b_spec=pl.BlockSpec((None, D, ns_in), lambda i, j, k, o: (o[j], 0, 0)),
        outs=[(proj_sds, pl.BlockSpec((512, ns_in), lambda i, j, k, o: (i, o[j])))], epilogue=_ep_store,
        deps=[proj_mine], alias_dep_to_out=(0, 0), j_outer=True)
    qkv = _qknorm_fwd(proj, gqk)
    att = [_att_fwd(g, S, qkv[g]) for g in range(3)]
    os_, ls_ = [a[0] for a in att], [a[1] for a in att]
    o_a = _mix_fwd(S, os_, ls_)
    o_pre, o_r, states = _ret_fwd(proj, gn_g, gn_b, dk, dv)
    w = late_weights(o_r)
    y, pa, pb = _merge_fwd(o_a, o_r, w["w_proj_a"], w["w_proj_b"], proj, D)
    (x1,) = _matmul("out_proj", "nn", y, w["w_out"], tm=512, tn=D, tk=D,
                    extras=[(x, _mn(512, D))], outs=[(_sds((S, D), F32), _mn(512, D))], epilogue=_ep_resid)
    xn2 = _rms_fwd("rms2_fwd", x1, g2)
    hid, act = _matmul("mlp_up", "nn", xn2, w["w_up"], tm=512, tn=2048, tk=D, j_outer=True,
                       outs=[(_sds((S, d_ff), CDT), _mn(512, 2048))] * 2, epilogue=_ep_up)
    dx2, dx2c, loss_row = _matmul(
        "mlp_down_loss", "nn", act, w["w_down"], tm=512, tn=D, tk=d_ff,
        extras=[(x1, _mn(512, D)), (target, _mn(512, D))],
        outs=[(_sds((S, D), F32), _mn(512, D)), (_sds((S, D), CDT), _mn(512, D)), (_sds((1, D), F32), _row(D))],
        epilogue=functools.partial(_ep_down_loss, inv_d=1.0 / D))
    loss = 0.5 * jnp.sum(loss_row) / D

    (dh,) = _matmul("d_hidden", "nt", dx2c, w["w_down"], tm=512, tn=2048, tk=D, j_outer=True,
                    extras=[(hid, _mn(512, 2048))], outs=[(_sds((S, d_ff), CDT), _mn(512, 2048))], epilogue=_ep_dh)
    (gw_down,) = _matmul("dw_down", "tn", act, dx2c, tm=1024, tn=D, tk=1024,
                         outs=[(_sds((d_ff, D), F32), _mn(1024, D))], epilogue=_ep_store)
    (gw_up,) = _matmul("dw_up", "tn", xn2, dh, tm=D, tn=1024, tk=1024,
                       outs=[(_sds((D, d_ff), F32), _mn(D, 1024))], epilogue=_ep_store)
    tok = on_grads({"w_down": gw_down, "w_up": gw_up})
    dx1, dx1c, dg2 = _matmul(
        "d_x1", "nt", dh, w["w_up"], tm=512, tn=D, tk=d_ff,
        extras=[(x1, _mn(512, D)), (g2, _row(D)), (dx2, _mn(512, D))],
        outs=[(_sds((S, D), F32), _mn(512, D)), (_sds((S, D), CDT), _mn(512, D)), (_sds((1, D), F32), _row(D))],
        epilogue=_ep_rms_bwd, deps=[tok])

    gt = 512
    assert (in_w - 2 * D) % gt == 0
    off_a, off_b = (in_w - 2 * D) // gt, (in_w - D) // gt
    dpa, dpb, dga, dgb = _matmul(
        "d_gates", "nt", dx1c, w["w_out"], tm=512, tn=gt, tk=D,
        extras=[(proj, _mn(512, gt, off_a)), (proj, _mn(512, gt, off_b)), (pa, _mn(512, gt)), (pb, _mn(512, gt))],
        outs=[(_sds((S, D), CDT), _mn(512, gt))] * 4, epilogue=_ep_gates)
    (gw_out,) = _matmul("dw_out", "tn", y, dx1c, tm=D, tn=D, tk=1024,
                        outs=[(_sds((D, D), F32), _mn(D, D))], epilogue=_ep_store)
    (gw_pa,) = _matmul("dw_proj_a", "tn", o_a, dpa, tm=GW, tn=D, tk=1024,
                       outs=[(_sds((GW, D), F32), _mn(GW, D))], epilogue=_ep_store)
    (gw_pb,) = _matmul("dw_proj_b", "tn", o_r, dpb, tm=1024, tn=D, tk=1024,
                       outs=[(_sds((ret_v_w, D), F32), _mn(1024, D))], epilogue=_ep_store)
    (do_a,) = _matmul("d_o_a", "nt", dpa, w["w_proj_a"], tm=1024, tn=GW, tk=D,
                      outs=[(_sds((S, GW), F32), _mn(1024, GW))], epilogue=_ep_store)
    tok = on_grads({"w_out": gw_out, "w_proj_a": gw_pa, "w_proj_b": gw_pb})
    (d_or,) = _matmul("d_o_r", "nt", dpb, w["w_proj_b"], tm=512, tn=ret_v_w, tk=D,
                      outs=[(_sds((S, ret_v_w), F32), _mn(512, ret_v_w))], epilogue=_ep_store, deps=[tok])

    dproj, dgn_g, dgn_b = _ret_bwd(proj, gn_g, gn_b, o_pre, states, d_or, dga, dgb, dk, dv)
    do_gs, c_gs = _mix_bwd(S, os_, ls_, do_a)
    datt_parts = [_att_bwd(g, S, qkv[g], ls_[g], do_gs[g], c_gs[g]) for g in range(3)]
    dproj, dgqk = _qknorm_bwd(proj, gqk, [p[0] for p in datt_parts], [p[1] for p in datt_parts],
                              [p[2] for p in datt_parts], dproj)

    (gw_in,) = _matmul(
        "dw_in", "tn", xn, dproj, tm=512, tn=ns_in, tk=1024,
        outs=[(_sds((N_CHIPS, D, ns_in), F32), pl.BlockSpec((None, 512, ns_in), lambda i, j, k: (j, i, 0)))],
        epilogue=_ep_store)
    tok = on_grads({"w_in": gw_in})
    grad_x, dg1 = _matmul(
        "d_x", "nt", dproj, w_in, tm=512, tn=D, tk=ns_in, n_cols=D,
        b_spec=pl.BlockSpec((None, D, ns_in), lambda i, j, k: (k, 0, 0)),
        extras=[(x, _mn(512, D)), (g1, _row(D)), (dx1, _mn(512, D))],
        outs=[(_sds((S, D), F32), _mn(512, D)), (_sds((1, D), F32), _row(D))],
        epilogue=_ep_rms_bwd, deps=[tok])

    smallg = {"norm1_g": dg1, "q_norm_g": dgqk[:, :ATT_W], "k_norm_g": dgqk[:, ATT_W:],
              "ret_gn_g": dgn_g, "ret_gn_b": dgn_b, "norm2_g": dg2}
    return loss, grad_x, smallg


N_CHIPS = 4
N_DEV = 8


def _place():
    x, y, c = lax.axis_index("x"), lax.axis_index("y"), lax.axis_index("c")
    return x, y, c


def _other_chips(x, y):
    out = []
    for fx, fy in ((1, 0), (0, 1), (1, 1)):
        px = 1 - x if fx else x
        py = 1 - y if fy else y
        out.append(((px, py), 2 * px + py))
    return out


SEM_SPEC = pl.BlockSpec(memory_space=pltpu.SEMAPHORE)
ANY_SPEC = pl.BlockSpec(memory_space=pl.ANY)
EFFECT = pltpu.SideEffectType.DATAFLOW_SIDE_EFFECTING


def _ici_copies(kind, srcs, lands, send, recv):
    x, y, c = _place()
    me = 2 * x + y
    out = []
    for w, (s, l) in enumerate(zip(srcs, lands)):
        for j, ((px, py), pidx) in enumerate(_other_chips(x, y)):
            if kind == "gather":
                half = s.shape[0] // 2
                rows = pl.ds(c * half, half)
                src, dst_there, dst_here = s.at[rows, :], l.at[me, rows, :], l.at[pidx, rows, :]
            else:
                src, dst_there, dst_here = s.at[pidx], l.at[me], l.at[pidx]
            out.append((src, dst_there, dst_here, send.at[3 * w + j], recv.at[3 * w + j], (px, py, c)))
    return out


def _exchange_start(name, kind, srcs, land_shapes):
    n = len(srcs)

    def body(*refs):
        src_refs, land_refs = refs[:n], refs[n:2 * n]
        send, recv = refs[2 * n], refs[2 * n + 1]
        token = refs[-1]
        for src, dst, _, ss, rs, dev in _ici_copies(kind, src_refs, land_refs, send, recv):
            pltpu.make_async_remote_copy(src_ref=src, dst_ref=dst, send_sem=ss, recv_sem=rs, device_id=dev,
                                         device_id_type=MESH).start()
        token[...] = jnp.zeros_like(token)

    thru = [pltpu.HBM(s.shape, s.dtype) for s in srcs] + [pltpu.HBM(shape, dtype) for shape, dtype in land_shapes]
    res = pl.pallas_call(
        body, name=name,
        out_shape=(pltpu.SemaphoreType.DMA((3 * n,)), pltpu.SemaphoreType.DMA((3 * n,)), *thru, _sds((8, LANES), F32)),
        in_specs=[HBM_SPEC] * (2 * n), out_specs=(SEM_SPEC, SEM_SPEC, *[HBM_SPEC] * (2 * n), VMEM_SPEC),
        input_output_aliases={i: 2 + i for i in range(2 * n)},
        compiler_params=pltpu.CompilerParams(has_side_effects=EFFECT),
    )(*[pltpu.with_memory_space_constraint(s, pltpu.HBM) for s in srcs],
      *[pltpu.with_memory_space_constraint(lax.empty(shape, dtype), pltpu.HBM) for shape, dtype in land_shapes])
    return res[0], res[1], list(res[2:2 + n]), list(res[2 + n:2 + 2 * n]), res[-1]


def _exchange_wait(name, kind, send, recv, srcs, lands, after):
    n = len(srcs)

    def body(*refs):
        src_refs, land_refs = refs[:n], refs[n:2 * n]
        send_ref, recv_ref = refs[2 * n], refs[2 * n + 1]
        for src, _, dst, ss, rs, dev in _ici_copies(kind, src_refs, land_refs, send_ref, recv_ref):
            cp = pltpu.make_async_remote_copy(src_ref=src, dst_ref=dst, send_sem=ss, recv_sem=rs, device_id=dev,
                                              device_id_type=MESH)
            cp.wait_send()
            cp.wait_recv()

    thru = [pltpu.HBM(t.shape, t.dtype) for t in list(srcs) + list(lands)]
    res = pl.pallas_call(
        body, name=name, out_shape=thru,
        in_specs=[HBM_SPEC] * (2 * n) + [SEM_SPEC, SEM_SPEC, ANY_SPEC], out_specs=[HBM_SPEC] * (2 * n),
        input_output_aliases={i: i for i in range(2 * n)},
        compiler_params=pltpu.CompilerParams(has_side_effects=EFFECT),
    )(*srcs, *lands, send, recv, after)
    return list(res[:n]), list(res[n:])


PAIR_TILE_ELEMS = 1 << 19


def _pair_fill(name, gathered, mine, core, others, chip):
    k, r, C = gathered.shape
    half = r // 2
    tr = _row_tile(half, C, PAIR_TILE_ELEMS, mult=16)
    nt = half // tr
    n_far = N_CHIPS - 1

    def body(c_ref, o_ref, chip_ref, in_ref, mine_ref, out_ref, slot, send, recv):
        j = pl.program_id(0)
        _sibling_barrier((j == 0) & (pl.program_id(1) == 0))
        b = (j * nt + pl.program_id(1)) % 2
        x, y, c = _place()
        cp = pltpu.make_async_remote_copy(src_ref=in_ref, dst_ref=slot.at[b], send_sem=send.at[b],
                                          recv_sem=recv.at[b], device_id=(x, y, 1 - c), device_id_type=MESH)

        @pl.when(j < n_far)
        def _():
            cp.start()
            cp.wait_recv()
            out_ref[...] = slot[b]
            cp.wait_send()

        @pl.when(j >= n_far)
        def _():
            out_ref[...] = mine_ref[...]

    def far(j):
        return jnp.minimum(j, n_far - 1)

    grid_spec = pltpu.PrefetchScalarGridSpec(
        num_scalar_prefetch=3, grid=(n_far + 2, nt),
        in_specs=[pl.BlockSpec((tr, C), lambda j, i, c, o, m: (
                      (2 * o[far(j)] + c[0]) * nt + jnp.where(j < n_far, i, nt - 1), 0)),
                  pl.BlockSpec((tr, C), lambda j, i, c, o, m: (jnp.where(j < n_far, 0, (j - n_far) * nt + i), 0))],
        out_specs=pl.BlockSpec((tr, C), lambda j, i, c, o, m: (
            jnp.where(j < n_far, 2 * o[far(j)] + 1 - c[0], 2 * m[0] + j - n_far) * nt + i, 0)),
        scratch_shapes=[pltpu.VMEM((2, tr, C), gathered.dtype), pltpu.SemaphoreType.DMA((2,)),
                        pltpu.SemaphoreType.DMA((2,))])
    out = pl.pallas_call(body, name=name, grid_spec=grid_spec, out_shape=_sds((k * r, C), gathered.dtype),
                         input_output_aliases={3: 0}, compiler_params=_params(2, PAIR_FILL_ID))(
                             core, others, chip, gathered.reshape(k * r, C), mine)
    return out.reshape(k, r, C)


def _pair_reduce(name, g, core):
    k, R, C = g.shape
    half = R // 2
    tr = _row_tile(half, C, PAIR_TILE_ELEMS, mult=16)
    nt = half // tr

    def body(c_ref, mine_ref, give_ref, out_ref, wire_ref, stage, slot, send, recv):
        _sibling_barrier((pl.program_id(0) == 0) & (pl.program_id(1) == 0))
        b = (pl.program_id(0) * nt + pl.program_id(1)) % 2
        x, y, c = _place()
        stage[b] = give_ref[...].astype(stage.dtype)
        cp = pltpu.make_async_remote_copy(src_ref=stage.at[b], dst_ref=slot.at[b], send_sem=send.at[b],
                                          recv_sem=recv.at[b], device_id=(x, y, 1 - c), device_id_type=MESH)
        cp.start()
        cp.wait_recv()
        tot = mine_ref[...] + slot[b].astype(F32)
        out_ref[...] = tot
        wire_ref[...] = tot.astype(wire_ref.dtype)
        cp.wait_send()

    blk = (tr, C)
    out_spec = pl.BlockSpec(blk, lambda s, i, c: (s * nt + i, 0))
    grid_spec = pltpu.PrefetchScalarGridSpec(
        num_scalar_prefetch=1, grid=(k, nt),
        in_specs=[pl.BlockSpec(blk, lambda s, i, c: ((2 * s + c[0]) * nt + i, 0)),
                  pl.BlockSpec(blk, lambda s, i, c: ((2 * s + 1 - c[0]) * nt + i, 0))],
        out_specs=[out_spec, out_spec],
        scratch_shapes=[pltpu.VMEM((2, tr, C), CDT), pltpu.VMEM((2, tr, C), CDT), pltpu.SemaphoreType.DMA((2,)),
                        pltpu.SemaphoreType.DMA((2,))])
    g2 = g.reshape(k * R, C)
    out, wire = pl.pallas_call(body, name=name, grid_spec=grid_spec,
                               out_shape=[_sds((k * half, C), F32), _sds((k * half, C), CDT)],
                               compiler_params=_params(2, PAIR_REDUCE_ID))(core, g2, g2)
    return out, wire.reshape(k, half, C)


def _all_reduce_small(v):
    r, cdim = v.shape

    def body(v_ref, o_ref, buf, send, recv):
        x, y, c = _place()
        me = 4 * x + 2 * y + c
        buf[me] = v_ref[...]
        sends = []
        for m in range(1, N_DEV):
            px = 1 - x if m & 4 else x
            py = 1 - y if m & 2 else y
            pc = 1 - c if m & 1 else c
            cp = pltpu.make_async_remote_copy(src_ref=v_ref, dst_ref=buf.at[me], send_sem=send.at[m - 1],
                                              recv_sem=recv.at[m - 1], device_id=(px, py, pc), device_id_type=MESH)
            cp.start()
            sends.append((cp, 4 * px + 2 * py + pc))
        for m, (cp, pidx) in enumerate(sends):
            pltpu.make_async_remote_copy(src_ref=v_ref, dst_ref=buf.at[pidx], send_sem=send.at[m], recv_sem=recv.at[m],
                                         device_id=(x, y, c), device_id_type=MESH).wait_recv()
        for cp, _ in sends:
            cp.wait_send()
        tot = buf[0]
        for k in range(1, N_DEV):
            tot = tot + buf[k]
        o_ref[...] = tot

    return pl.pallas_call(
        body, name="all_reduce_small", in_specs=[VMEM_SPEC], out_specs=VMEM_SPEC,
        out_shape=_sds((r, cdim), F32),
        scratch_shapes=[pltpu.VMEM((N_DEV, r, cdim), F32), pltpu.SemaphoreType.DMA((N_DEV - 1,)),
                        pltpu.SemaphoreType.DMA((N_DEV - 1,))],
    )(v)


def _row_tile(rows, cols, budget_elems=1 << 18, mult=8):
    if rows % mult:
        return rows
    t = max(mult, (budget_elems // cols) // mult * mult)
    while rows % t:
        t -= mult
    return t


def _adamw_update(w, g, m, v):
    nm = ADAM_B1 * m + (1.0 - ADAM_B1) * g
    nv = ADAM_B2 * v + (1.0 - ADAM_B2) * (g * g)
    m_hat = nm / (1.0 - ADAM_B1 ** ADAM_STEP)
    v_hat = nv / (1.0 - ADAM_B2 ** ADAM_STEP)
    return -ADAM_LR * (m_hat / (jnp.sqrt(v_hat) + ADAM_EPS) + ADAM_WD * w), nm, nv


def _adamw(name, w, g, m, v):
    R, C = w.shape
    tr = _row_tile(R, C, 1 << 17)

    def body(w_ref, g_ref, m_ref, v_ref, d_ref, nm_ref, nv_ref):
        d_ref[...], nm_ref[...], nv_ref[...] = _adamw_update(w_ref[...], g_ref[...], m_ref[...], v_ref[...])

    spec = pl.BlockSpec((tr, C), lambda i: (i, 0))
    return pl.pallas_call(body, name=name, grid=(R // tr,), in_specs=[spec] * 4, out_specs=[spec] * 3,
                          out_shape=[_sds((R, C), F32)] * 3, compiler_params=_params(1))(w, g, m, v)


def _sum_share(name, own, by_chip, chip, others, core):
    k, half, C = by_chip.shape
    tr = _row_tile(half, C, mult=16)
    nt = half // tr

    def body(chip_ref, oth_ref, c_ref, own_ref, a_ref, b_ref, cc_ref, g_out, mine, slot, send, recv):
        p = pl.program_id(1)
        _sibling_barrier((pl.program_id(0) == 0) & (p == 0))
        b = pl.program_id(0) % 2
        x, y, c = _place()
        cp = pltpu.make_async_remote_copy(src_ref=mine.at[b], dst_ref=slot.at[b], send_sem=send.at[b],
                                          recv_sem=recv.at[b], device_id=(x, y, 1 - c), device_id_type=MESH)

        @pl.when(p == 0)
        def _():
            tot = ((own_ref[...] + a_ref[...].astype(F32)) + b_ref[...].astype(F32)) + cc_ref[...].astype(F32)
            mine[b] = tot
            cp.start()
            g_out[...] = tot

        @pl.when(p == 1)
        def _():
            cp.wait_recv()
            g_out[...] = slot[b]
            cp.wait_send()

    def piece(j):
        return pl.BlockSpec((tr, C), lambda i, p, chip, oth, c: (oth[j] * nt + i, 0))

    grid_spec = pltpu.PrefetchScalarGridSpec(
        num_scalar_prefetch=3, grid=(nt, 2),
        in_specs=[pl.BlockSpec((tr, C), lambda i, p, chip, oth, c: (chip[0] * nt + i, 0)),
                  piece(0), piece(1), piece(2)],
        out_specs=pl.BlockSpec((tr, C), lambda i, p, chip, oth, c: (
            jnp.where(p == 0, c[0], 1 - c[0]) * nt + i, 0)),
        scratch_shapes=[pltpu.VMEM((2, tr, C), F32), pltpu.VMEM((2, tr, C), F32), pltpu.SemaphoreType.DMA((2,)),
                        pltpu.SemaphoreType.DMA((2,))])
    by2 = by_chip.reshape(k * half, C)
    return pl.pallas_call(body, name=name, grid_spec=grid_spec, out_shape=_sds((2 * half, C), F32),
                          compiler_params=_params(2, SUM_SHARE_ID))(chip, others, core, own, by2, by2, by2)


BIG = ("w_in", "w_proj_a", "w_proj_b", "w_out", "w_up", "w_down")
COL_SHARDED = ("w_in", "w_proj_a", "w_up")
SMALL = ("norm1_g", "q_norm_g", "k_norm_g", "ret_gn_g", "ret_gn_b", "norm2_g")
ALL_W = ("norm1_g", "w_in", "q_norm_g", "k_norm_g", "ret_gn_g", "ret_gn_b", "w_proj_a", "w_proj_b", "w_out",
         "norm2_g", "w_up", "w_down")
LANES = 128


def _to_full(name, gathered):
    k, r, c = gathered.shape
    if name in COL_SHARDED:
        return gathered.transpose(1, 0, 2).reshape(r, k * c)
    return gathered.reshape(k * r, c)


def _to_shard_major(name, full):
    if name in COL_SHARDED:
        r, c4 = full.shape
        return full.reshape(r, N_CHIPS, c4 // N_CHIPS).transpose(1, 0, 2)
    r4, c = full.shape
    return full.reshape(N_CHIPS, r4 // N_CHIPS, c)


def kernel(x, norm1_g, w_in, q_norm_g, k_norm_g, ret_gn_g, ret_gn_b, w_proj_a, w_proj_b, w_out, norm2_g, w_up, w_down, loss_target, m_norm1_g, m_w_in, m_q_norm_g, m_k_norm_g, m_ret_gn_g, m_ret_gn_b, m_w_proj_a, m_w_proj_b, m_w_out, m_norm2_g, m_w_up, m_w_down, v_norm1_g, v_w_in, v_q_norm_g, v_k_norm_g, v_ret_gn_g, v_ret_gn_b, v_w_proj_a, v_w_proj_b, v_w_out, v_norm2_g, v_w_up, v_w_down):
    weights = dict(norm1_g=norm1_g, w_in=w_in, q_norm_g=q_norm_g, k_norm_g=k_norm_g, ret_gn_g=ret_gn_g,
                   ret_gn_b=ret_gn_b, w_proj_a=w_proj_a, w_proj_b=w_proj_b, w_out=w_out, norm2_g=norm2_g,
                   w_up=w_up, w_down=w_down)
    moments_m = dict(norm1_g=m_norm1_g, w_in=m_w_in, q_norm_g=m_q_norm_g, k_norm_g=m_k_norm_g, ret_gn_g=m_ret_gn_g,
                     ret_gn_b=m_ret_gn_b, w_proj_a=m_w_proj_a, w_proj_b=m_w_proj_b, w_out=m_w_out,
                     norm2_g=m_norm2_g, w_up=m_w_up, w_down=m_w_down)
    moments_v = dict(norm1_g=v_norm1_g, w_in=v_w_in, q_norm_g=v_q_norm_g, k_norm_g=v_k_norm_g, ret_gn_g=v_ret_gn_g,
                     ret_gn_b=v_ret_gn_b, w_proj_a=v_w_proj_a, w_proj_b=v_w_proj_b, w_out=v_w_out,
                     norm2_g=v_norm2_g, w_up=v_w_up, w_down=v_w_down)

    mx, my = lax.axis_index("x"), lax.axis_index("y")
    core = lax.axis_index("c").astype(jnp.int32).reshape(1)
    chip = (2 * mx + my).astype(jnp.int32).reshape(1)
    others = jnp.stack([2 * (1 - mx) + my, 2 * mx + 1 - my, 2 * (1 - mx) + 1 - my]).astype(jnp.int32)
    shards = {n: weights[n][0].astype(CDT) for n in BIG}
    def start_gather(name, names):
        return _exchange_start(name, "gather", [shards[n] for n in names],
                               [((N_CHIPS,) + shards[n].shape, CDT) for n in names])

    i_send, i_recv, i_srcs, i_lands, i_token = start_gather("gather_w_in_start", ["w_in"])
    late = [n for n in BIG if n != "w_in"]
    l_send, l_recv, l_srcs, l_lands, l_token = start_gather("gather_late_start", late)

    def far_w_in(after):
        srcs, lands = _exchange_wait("gather_w_in_wait", "gather", i_send, i_recv, i_srcs, i_lands, after)
        return _pair_fill("pair_fill_w_in", lands[0], srcs[0], core, others, chip)

    def late_weights(after):
        srcs, lands = _exchange_wait("gather_late_wait", "gather", l_send, l_recv, l_srcs, l_lands, after)
        out = {}
        for n, mine, land in zip(late, srcs, lands):
            out[n] = _to_full(n, _pair_fill("pair_fill_%s" % n, land, mine, core, others, chip))
        return out

    pending = []

    def on_grads(group):
        names = list(group)
        red = [_pair_reduce("pair_reduce_%s" % n, g if g.ndim == 3 else _to_shard_major(n, g), core)
               for n, g in group.items()]
        wires = [wire for _, wire in red]
        send, recv, srcs, lands, token = _exchange_start(
            "scatter_start_%s" % names[0], "scatter", wires, [(wire.shape, wire.dtype) for wire in wires])
        pending.append((names, [own for own, _ in red], send, recv, srcs, lands))
        return token

    small = {n: weights[n].reshape(1, -1) for n in SMALL}

    loss, grad_x, small_g = _local_step(x[0], loss_target[0], i_srcs[0], chip, others, far_w_in, small,
                                        late_weights, on_grads, deps0=[i_token, l_token])
    loss = lax.psum(loss, ("x", "y", "c"))

    out_g, out_d, out_m, out_v = {}, {}, {}, {}
    for names, owns, send, recv, srcs, lands in pending:
        _, got = _exchange_wait("scatter_wait_%s" % names[0], "scatter", send, recv, srcs, lands, grad_x)
        for n, own, by_chip in zip(names, owns, got):
            shape = weights[n].shape
            g2 = _sum_share("sum_share_%s" % n, own, by_chip, chip, others, core)
            d, nm, nv = _adamw("adamw_%s" % n, weights[n][0], g2, moments_m[n][0], moments_v[n][0])
            out_g[n], out_d[n], out_m[n], out_v[n] = (t.reshape(shape) for t in (g2, d, nm, nv))

    packed = jnp.concatenate([small_g[n].reshape(1, -1) for n in SMALL], axis=1)
    red = _all_reduce_small(packed.reshape(-1, LANES)).reshape(1, -1)
    off = 0
    for n in SMALL:
        shape = weights[n].shape
        row = (1, weights[n].size)
        g2 = red[:, off:off + row[1]]
        off += row[1]
        d, nm, nv = _adamw("adamw_%s" % n, weights[n].reshape(row), g2, moments_m[n].reshape(row),
                           moments_v[n].reshape(row))
        out_g[n], out_d[n], out_m[n], out_v[n] = (t.reshape(shape) for t in (g2, d, nm, nv))

    return (loss, grad_x[None], *[out_g[n] for n in ALL_W], *[out_d[n] for n in ALL_W],
            *[out_m[n] for n in ALL_W], *[out_v[n] for n in ALL_W])
```

```python
import functools
import math

import jax
import jax.numpy as jnp
from jax import lax
from jax.experimental import pallas as pl
from jax.experimental.pallas import tpu as pltpu

CDT = jnp.bfloat16
F32 = jnp.float32
EPS = 1e-6

ATT_GROUPS = ((128, 1), (512, 4), (2048, 16))
ATT_HPG = 4
ATT_HEADS = 12
HD = 128
BLK = 128
ATT_W = ATT_HEADS * HD
GW = ATT_HPG * HD
RET_HEADS = 4

ADAM_LR = 0.001
ADAM_B1 = 0.9
ADAM_B2 = 0.999
ADAM_EPS = 1e-08
ADAM_WD = 0.01
ADAM_STEP = 10

VMEM_LIMIT_BYTES = 56 * 1024 * 1024
MXU_DIM = 256
MESH = pl.DeviceIdType.MESH
HBM_SPEC = pl.BlockSpec(memory_space=pltpu.HBM)
VMEM_SPEC = pl.BlockSpec(memory_space=pltpu.VMEM)


def _params(n_axes, collective_id=None):
    return pltpu.CompilerParams(dimension_semantics=("arbitrary",) * n_axes,
                                vmem_limit_bytes=VMEM_LIMIT_BYTES, collective_id=collective_id)


PAIR_FILL_ID, PAIR_REDUCE_ID, SUM_SHARE_ID = 1, 2, 3


def _sibling_barrier(first_step):
    @pl.when(first_step)
    def _():
        sem = pltpu.get_barrier_semaphore()
        x, y, c = lax.axis_index("x"), lax.axis_index("y"), lax.axis_index("c")
        pl.semaphore_signal(sem, inc=1, device_id=(x, y, 1 - c), device_id_type=pl.DeviceIdType.MESH)
        pl.semaphore_wait(sem, 1)


def _dot_nn(a, b):
    return jnp.dot(a, b, preferred_element_type=F32)


def _dot_nt(a, b):
    return lax.dot_general(a, b, (((1,), (1,)), ((), ())), preferred_element_type=F32)


def _dot_tn(a, b):
    return lax.dot_general(a, b, (((0,), (0,)), ((), ())), preferred_element_type=F32)


def _sigmoid(v):
    return 1.0 / (1.0 + jnp.exp(-v))


def _matmul(name, mode, a, b, *, tm, tn, tk, extras=(), outs, epilogue, deps=(), b_spec=None, n_cols=None,
            prefetch=(), alias_dep_to_out=None, j_outer=False):
    deps = [d for d in deps if d is not None]
    if mode == "tn":
        K, M = a.shape
    else:
        M, K = a.shape
    if b_spec is None:
        (N, K2) = b.shape if mode == "nt" else b.shape[::-1]
        assert K == K2, (name, a.shape, b.shape)
        if mode == "nt":
            b_spec = pl.BlockSpec((tn, tk), lambda i, j, k, *p: (j, k))
        else:
            b_spec = pl.BlockSpec((tk, tn), lambda i, j, k, *p: (k, j))
    else:
        N = n_cols
    assert M % tm == 0 and N % tn == 0 and K % tk == 0, (name, a.shape, b.shape)
    ni, nj, nk = M // tm, N // tn, K // tk
    if mode == "tn":
        a_spec = pl.BlockSpec((tk, tm), lambda i, j, k, *p: (k, i))
    else:
        a_spec = pl.BlockSpec((tm, tk), lambda i, j, k, *p: (i, k))
    dot = {"nn": _dot_nn, "nt": _dot_nt, "tn": _dot_tn}[mode]
    n_ex, n_out, n_dep, n_pre = len(extras), len(outs), len(deps), len(prefetch)
    grid = (ni, nj, nk)
    if j_outer:
        grid = (nj, ni, nk)

        def swapped(spec):
            return pl.BlockSpec(spec.block_shape, lambda j, i, k, *p: spec.index_map(i, j, k, *p))

        a_spec, b_spec = swapped(a_spec), swapped(b_spec)
        extras = [(e, swapped(s)) for e, s in extras]
        outs = [(o, swapped(s)) for o, s in outs]

    def body(*refs):
        refs = refs[n_pre:]
        a_ref, b_ref = refs[0], refs[1]
        ex = refs[2:2 + n_ex]
        out = refs[2 + n_ex + n_dep:2 + n_ex + n_dep + n_out]
        acc = refs[-1] if nk > 1 else None
        i = pl.program_id(1 if j_outer else 0)
        k = pl.program_id(2)
        if nk == 1:
            epilogue(dot(a_ref[...].astype(CDT), b_ref[...].astype(CDT)), ex, out, i)
            return

        @pl.when(k == 0)
        def _():
            acc[...] = jnp.zeros_like(acc)

        acc[...] += dot(a_ref[...].astype(CDT), b_ref[...].astype(CDT))

        @pl.when(k == nk - 1)
        def _():
            epilogue(acc[...], ex, out, i)

    grid_spec = pltpu.PrefetchScalarGridSpec(
        num_scalar_prefetch=n_pre, grid=grid,
        in_specs=[a_spec, b_spec] + [s for _, s in extras] + [pl.BlockSpec(memory_space=pl.ANY)] * n_dep,
        out_specs=[s for _, s in outs],
        scratch_shapes=[pltpu.VMEM((tm, tn), F32)] if nk > 1 else [])
    aliases = {}
    if alias_dep_to_out is not None:
        aliases = {n_pre + 2 + n_ex + alias_dep_to_out[0]: alias_dep_to_out[1]}
    res = pl.pallas_call(
        body, name=name, grid_spec=grid_spec, out_shape=[o for o, _ in outs], input_output_aliases=aliases,
        compiler_params=_params(3),
    )(*prefetch, a, b, *[e for e, _ in extras], *deps)
    return res


def _mn(tm, tn, col_off=0):
    return pl.BlockSpec((tm, tn), lambda i, j, k, *p: (i, j + col_off))


def _row(tn):
    return pl.BlockSpec((1, tn), lambda i, j, k, *p: (0, j))


def _ep_store(acc, ex, out, i):
    out[0][...] = acc.astype(out[0].dtype)


def _ep_resid_norm(acc, ex, out, i):
    x1 = ex[0][...] + acc
    out[0][...] = x1
    rstd = lax.rsqrt(jnp.mean(x1 * x1, axis=-1, keepdims=True) + EPS)
    out[1][...] = (x1 * rstd * ex[1][...]).astype(out[1].dtype)


def _ep_up(acc, ex, out, i):
    out[0][...] = acc.astype(out[0].dtype)
    r = jnp.maximum(acc, 0.0)
    out[1][...] = (r * r).astype(out[1].dtype)


def _ep_down_loss(acc, ex, out, i, inv_d):
    diff = (ex[0][...] + acc) - ex[1][...]
    dx2 = diff * inv_d
    out[0][...] = dx2
    out[1][...] = dx2.astype(out[1].dtype)

    @pl.when(i == 0)
    def _():
        out[2][...] = jnp.zeros_like(out[2])

    out[2][...] += jnp.sum(diff * diff, axis=0, keepdims=True)


def _ep_dh(acc, ex, out, i):
    h = ex[0][...].astype(F32)
    out[0][...] = (acc * (2.0 * jnp.maximum(h, 0.0))).astype(out[0].dtype)


def _ep_rms_bwd(acc, ex, out, i):
    x = ex[0][...]
    g = ex[1][...]
    rstd = lax.rsqrt(jnp.mean(x * x, axis=-1, keepdims=True) + EPS)
    xh = x * rstd
    dxh = acc * g
    dx = ex[2][...] + rstd * (dxh - xh * jnp.mean(dxh * xh, axis=-1, keepdims=True))
    out[0][...] = dx
    for copy in out[1:-1]:
        copy[...] = dx.astype(copy.dtype)
    dg = out[-1]

    @pl.when(i == 0)
    def _():
        dg[...] = jnp.zeros_like(dg)

    dg[...] += jnp.sum(acc * xh, axis=0, keepdims=True)


def _ep_gates(acc, ex, out, i):
    sa = _sigmoid(ex[0][...].astype(F32))
    sb = _sigmoid(ex[1][...].astype(F32))
    dpa = acc * sa
    dpb = acc * sb
    out[0][...] = dpa.astype(out[0].dtype)
    out[1][...] = dpb.astype(out[1].dtype)
    out[2][...] = (dpa * ex[2][...].astype(F32) * (1.0 - sa)).astype(out[2].dtype)
    out[3][...] = (dpb * ex[3][...].astype(F32) * (1.0 - sb)).astype(out[3].dtype)


def _sds(shape, dtype):
    return jax.ShapeDtypeStruct(shape, dtype)


def _rms_fwd(name, x, g, tm=512):
    S, D = x.shape

    def body(x_ref, g_ref, o_ref):
        xv = x_ref[...]
        rstd = lax.rsqrt(jnp.mean(xv * xv, axis=-1, keepdims=True) + EPS)
        o_ref[...] = (xv * rstd * g_ref[...]).astype(o_ref.dtype)

    return pl.pallas_call(
        body, name=name, grid=(S // tm,),
        in_specs=[pl.BlockSpec((tm, D), lambda i: (i, 0)), pl.BlockSpec((1, D), lambda i: (0, 0))],
        out_specs=pl.BlockSpec((tm, D), lambda i: (i, 0)),
        out_shape=_sds((S, D), CDT), compiler_params=_params(1))(x, g)


def _rm_shape(S, d, width):
    return (S, width) if d == 1 else (d, S // d, width)


def _rm_spec(tm, d, width):
    if d == 1:
        return pl.BlockSpec((tm, width), lambda i: (i, 0))
    return pl.BlockSpec((d, tm // d, width), lambda i: (0, i, 0))


def _rm_put(dst_ref, cols, buf_ref, d):
    if d == 1:
        dst_ref[:, cols] = buf_ref[...].astype(dst_ref.dtype)
        return
    m = buf_ref.shape[0] // d
    for r in range(d):
        dst_ref[r, :, cols] = buf_ref[pl.ds(r, m, stride=d), :].astype(dst_ref.dtype)


def _rm_reader(buf_ref, src_ref, d):
    if d == 1:
        return lambda s, rows: src_ref[rows, s * HD:(s + 1) * HD].astype(F32)
    m = buf_ref.shape[1] // d
    for s in range(buf_ref.shape[0]):
        for r in range(d):
            buf_ref.at[s][pl.ds(r, m, stride=d), :] = src_ref[r, :, s * HD:(s + 1) * HD].astype(F32)
    return lambda s, rows: buf_ref.at[s][rows, :]


def _qknorm_fwd(proj, gqk, tm=512):
    S = proj.shape[0]
    W = 2 * ATT_W
    dil = [d for _, d in ATT_GROUPS]

    def body(p_ref, g_ref, o0, o1, o2, buf):
        outs = (o0, o1, o2)
        for hd in range(3 * ATT_HEADS):
            which, head = hd // ATT_HEADS, hd % ATT_HEADS
            grp, slot = head // ATT_HPG, head % ATT_HPG
            cols = slice(hd * HD, (hd + 1) * HD)

            def chunk(rows, which=which, cols=cols):
                v = p_ref[rows, cols].astype(F32)
                if which < 2:
                    rstd = lax.rsqrt(jnp.mean(v * v, axis=-1, keepdims=True) + EPS)
                    v = v * rstd * g_ref[:, cols]
                buf[rows, :] = v

            chunk(slice(None))
            _rm_put(outs[grp], slice(which * GW + slot * HD, which * GW + (slot + 1) * HD), buf, dil[grp])

    return pl.pallas_call(
        body, name="qknorm_fwd", grid=(S // tm,),
        in_specs=[pl.BlockSpec((tm, 3 * ATT_W), lambda i: (i, 0)), pl.BlockSpec((1, W), lambda i: (0, 0))],
        out_specs=[_rm_spec(tm, d, 3 * GW) for d in dil],
        out_shape=[_sds(_rm_shape(S, d, 3 * GW), CDT) for d in dil],
        scratch_shapes=[pltpu.VMEM((tm, HD), F32)],
        compiler_params=_params(1))(proj, gqk)


def _qknorm_bwd(proj, gqk, dqs, dks, dvs, dproj, tm=256):
    S = proj.shape[0]
    W = 2 * ATT_W
    dil = [d for _, d in ATT_GROUPS]

    def body(p_ref, g_ref, *refs):
        ins = refs[0:9]
        o_ref, dg_ref = refs[10], refs[11]
        bufs = refs[12:21]
        i = pl.program_id(0)

        @pl.when(i == 0)
        def _():
            dg_ref[...] = jnp.zeros_like(dg_ref)

        nat = [_rm_reader(bufs[j], ins[j], dil[j % 3]) for j in range(9)]
        dq_get, dk_get, dv_get = nat[0:3], nat[3:6], nat[6:9]
        for hd in range(2 * ATT_HEADS):
            sl = slice(hd * HD, (hd + 1) * HD)
            head = hd % ATT_HEADS
            grp, slot = head // ATT_HPG, head % ATT_HPG
            get = (dq_get if hd < ATT_HEADS else dk_get)[grp]

            def chunk(rows, sl=sl, slot=slot, get=get):
                dn = get(slot, rows)
                v = p_ref[rows, sl].astype(F32)
                rstd = lax.rsqrt(jnp.mean(v * v, axis=-1, keepdims=True) + EPS)
                vh = v * rstd
                dg_ref[:, sl] += jnp.sum(dn * vh, axis=0, keepdims=True)
                dvh = dn * g_ref[:, sl]
                o_ref[rows, sl] = (rstd * (dvh - vh * jnp.mean(dvh * vh, axis=-1, keepdims=True))).astype(o_ref.dtype)

            chunk(slice(None))
        for head in range(ATT_HEADS):
            grp, slot = head // ATT_HPG, head % ATT_HPG
            o_ref[:, W + head * HD:W + (head + 1) * HD] = dv_get[grp](slot, slice(None)).astype(o_ref.dtype)

    return pl.pallas_call(
        body, name="qknorm_bwd", grid=(S // tm,),
        in_specs=[pl.BlockSpec((tm, W), lambda i: (i, 0)), pl.BlockSpec((1, W), lambda i: (0, 0))]
        + [_rm_spec(tm, d, GW) for d in dil] * 3 + [pl.BlockSpec(memory_space=pl.ANY)],
        out_specs=[pl.BlockSpec((tm, 3 * ATT_W), lambda i: (i, 0)), pl.BlockSpec((1, W), lambda i: (0, 0))],
        out_shape=[_sds(dproj.shape, dproj.dtype), _sds((1, W), F32)],
        scratch_shapes=[pltpu.VMEM((ATT_HPG, tm, HD), F32)] * 9,
        input_output_aliases={11: 0},
        compiler_params=_params(1))(proj, gqk, *dqs, *dks, *dvs, dproj)


def _att_mask(n):
    qi = lax.broadcasted_iota(jnp.int32, (BLK, 2 * BLK), 0)
    kj = lax.broadcasted_iota(jnp.int32, (BLK, 2 * BLK), 1)
    dist = BLK + qi - kj
    valid = (dist >= 0) & (dist <= BLK) & ((kj >= BLK) | (n > 0))
    return valid, dist.astype(F32)


def _att_slopes(grp):
    return [2.0 ** (-8.0 * (grp * ATT_HPG + hh + 1) / ATT_HEADS) for hh in range(ATT_HPG)]


def _att_spec(d, row_fn, col=0):
    if d == 1:
        return pl.BlockSpec((BLK, GW), lambda r, n: (row_fn(n), col))
    return pl.BlockSpec((None, BLK, GW), lambda r, n: (r, row_fn(n), col))


def _att_qkv_specs(d, nb):
    last = nb - 1

    def cur(n):
        return jnp.minimum(n, last)

    def prev(n):
        return jnp.maximum(jnp.minimum(n, last) - 1, 0)

    return [_att_spec(d, cur, 0), _att_spec(d, prev, 1), _att_spec(d, cur, 1), _att_spec(d, prev, 2),
            _att_spec(d, cur, 2)]


def _att_fwd(grp, S, qkv):
    _, d = ATT_GROUPS[grp]
    L = S // d
    nb = L // BLK
    slopes = _att_slopes(grp)
    scale = HD ** -0.5

    def body(q_ref, kp_ref, kc_ref, vp_ref, vc_ref, o_ref, l_ref, s_buf, p_buf, den_buf):
        n = pl.program_id(1)
        valid, distf = _att_mask(n)
        heads = [slice(hh * HD, (hh + 1) * HD) for hh in range(ATT_HPG)]
        for hh, sl in enumerate(heads):
            k = jnp.concatenate([kp_ref[:, sl], kc_ref[:, sl]], axis=0)
            s_buf[hh] = _dot_nt(q_ref[:, sl], k)
        for hh, sl in enumerate(heads):
            s = s_buf[hh] * scale + (-slopes[hh] * d) * distf
            s = jnp.where(valid, s, -1e30)
            m = jnp.max(s, axis=-1, keepdims=True)
            p = jnp.exp(s - m)
            den = jnp.sum(p, axis=-1, keepdims=True)
            p_buf[hh] = p.astype(CDT)
            den_buf[hh] = jnp.broadcast_to(den, (BLK, HD))
            l_ref[:, sl] = jnp.broadcast_to(m + jnp.log(den), (BLK, HD))
        for hh, sl in enumerate(heads):
            v = jnp.concatenate([vp_ref[:, sl], vc_ref[:, sl]], axis=0)
            o_ref[:, sl] = _dot_nn(p_buf[hh], v) / den_buf[hh]

    out_spec = _att_spec(d, lambda n: n)
    return pl.pallas_call(
        body, name="att_fwd_g%d" % grp, grid=(d, nb),
        in_specs=_att_qkv_specs(d, nb),
        out_specs=[out_spec, out_spec],
        out_shape=[_sds(_rm_shape(S, d, GW), F32)] * 2,
        scratch_shapes=[pltpu.VMEM((ATT_HPG, BLK, 2 * BLK), F32), pltpu.VMEM((ATT_HPG, BLK, 2 * BLK), CDT),
                        pltpu.VMEM((ATT_HPG, BLK, HD), F32)],
        compiler_params=_params(2),
    )(qkv, qkv, qkv, qkv, qkv)


def _att_bwd(grp, S, qkv, lse, do_g, c_g):
    _, d = ATT_GROUPS[grp]
    L = S // d
    nb = L // BLK
    slopes = _att_slopes(grp)
    scale = HD ** -0.5
    last = nb - 1

    def body(q_ref, kp_ref, kc_ref, vp_ref, vc_ref, l_ref, do_ref, c_ref, dq_ref, dk_ref, dv_ref, ck, cv,
             s_buf, dp_buf, p_buf, ds_buf):
        n = pl.program_id(1)

        @pl.when(n == 0)
        def _():
            ck[...] = jnp.zeros_like(ck)
            cv[...] = jnp.zeros_like(cv)

        @pl.when(n < nb)
        def _():
            valid, distf = _att_mask(n)
            heads = [slice(hh * HD, (hh + 1) * HD) for hh in range(ATT_HPG)]
            for hh, sl in enumerate(heads):
                k = jnp.concatenate([kp_ref[:, sl], kc_ref[:, sl]], axis=0)
                v = jnp.concatenate([vp_ref[:, sl], vc_ref[:, sl]], axis=0)
                s_buf[hh] = _dot_nt(q_ref[:, sl], k)
                dp_buf[hh] = _dot_nt(do_ref[:, sl], v)
            for hh, sl in enumerate(heads):
                s = s_buf[hh] * scale + (-slopes[hh] * d) * distf
                p = jnp.where(valid, jnp.exp(s - l_ref[:, sl][:, 0:1]), 0.0)
                p_buf[hh] = p.astype(CDT)
                ds_buf[hh] = (p * (dp_buf[hh] + c_ref[:, sl][:, 0:1]) * scale).astype(CDT)
            for hh, sl in enumerate(heads):
                k = jnp.concatenate([kp_ref[:, sl], kc_ref[:, sl]], axis=0)
                ds = ds_buf[hh]
                dq_ref[:, sl] = _dot_nn(ds, k)
                dk = _dot_tn(ds, q_ref[:, sl])
                dv = _dot_tn(p_buf[hh], do_ref[:, sl])
                dk_ref[:, sl] = ck[:, sl] + dk[0:BLK]
                dv_ref[:, sl] = cv[:, sl] + dv[0:BLK]
                ck[:, sl] = dk[BLK:2 * BLK]
                cv[:, sl] = dv[BLK:2 * BLK]

        @pl.when(n == nb)
        def _():
            dk_ref[...] = ck[...]
            dv_ref[...] = cv[...]

    blk = (BLK, GW)
    at_q = _att_spec(d, lambda n: jnp.minimum(n, last))
    behind = _att_spec(d, lambda n: jnp.maximum(n - 1, 0))
    return pl.pallas_call(
        body, name="att_bwd_g%d" % grp, grid=(d, nb + 1),
        in_specs=_att_qkv_specs(d, nb) + [at_q, at_q, at_q],
        out_specs=[at_q, behind, behind],
        out_shape=[_sds(_rm_shape(S, d, GW), F32)] * 3,
        scratch_shapes=[pltpu.VMEM(blk, F32), pltpu.VMEM(blk, F32),
                        pltpu.VMEM((ATT_HPG, BLK, 2 * BLK), F32), pltpu.VMEM((ATT_HPG, BLK, 2 * BLK), F32),
                        pltpu.VMEM((ATT_HPG, BLK, 2 * BLK), CDT), pltpu.VMEM((ATT_HPG, BLK, 2 * BLK), CDT)],
        compiler_params=_params(2),
    )(qkv, qkv, qkv, qkv, qkv, lse, do_g, c_g)


def _mix_alpha(l0, l1, l2):
    mx = jnp.maximum(jnp.maximum(l0, l1), l2)
    e = [jnp.exp(l0 - mx), jnp.exp(l1 - mx), jnp.exp(l2 - mx)]
    tot = e[0] + e[1] + e[2]
    return [ei / tot for ei in e]


def _mix_bwd(S, os_, ls_, do_a, tm=512):
    dil = [d for _, d in ATT_GROUPS]

    def body(*refs):
        d_ref, outs, bufs, tmps = refs[6], refs[7:13], refs[13:19], refs[19:25]
        get = [_rm_reader(bufs[j], refs[j], dil[j % 3]) for j in range(6)]
        for s in range(ATT_HPG):
            cols = slice(s * HD, (s + 1) * HD)

            def chunk(rows, s=s, cols=cols):
                al = _mix_alpha(*[get[3 + g](s, rows) for g in range(3)])
                dv = d_ref[rows, cols]
                o_a = al[0] * get[0](s, rows) + al[1] * get[1](s, rows) + al[2] * get[2](s, rows)
                dsum = jnp.sum(dv * o_a, axis=-1, keepdims=True)
                for g in range(3):
                    tmps[g][rows, :] = al[g] * dv
                    tmps[3 + g][rows, :] = -(al[g] * dsum)

            chunk(slice(None))
            for j in range(6):
                _rm_put(outs[j], cols, tmps[j], dil[j % 3])

    specs = [_rm_spec(tm, d, GW) for d in dil]
    res = pl.pallas_call(
        body, name="mix_bwd", grid=(S // tm,), in_specs=specs * 2 + [pl.BlockSpec((tm, GW), lambda i: (i, 0))],
        out_specs=specs * 2,
        out_shape=[_sds(_rm_shape(S, d, GW), CDT) for d in dil] + [_sds(_rm_shape(S, d, GW), F32) for d in dil],
        scratch_shapes=[pltpu.VMEM((ATT_HPG, tm, HD), F32)] * 6 + [pltpu.VMEM((tm, HD), F32)] * 6,
        compiler_params=_params(1))(*os_, *ls_, do_a)
    return res[:3], res[3:]


def _ret_tables(dk):
    H, C = RET_HEADS, BLK
    log_g = jnp.log(1.0 - 2.0 ** (-5.0 - jnp.arange(H, dtype=F32)))
    idx = jnp.arange(C, dtype=F32)
    diff = idx[:, None] - idx[None, :]
    decay = jnp.where(diff >= 0, jnp.exp(log_g[:, None, None] * jnp.maximum(diff, 0.0)), 0.0)
    xi = jnp.exp(log_g[:, None] * (idx[None, :] + 1.0))
    zeta = jnp.exp(log_g[:, None] * (C - 1.0 - idx[None, :])) * (dk ** -0.5)
    g_chunk = jnp.exp(log_g * C)
    bc = lambda t: jnp.broadcast_to(t[:, :, None], (H, C, C))
    return decay, bc(xi), bc(zeta), jnp.broadcast_to(g_chunk[:, None, None], (H, 8, C))


def _gn_fwd(o, g, b):
    mu = jnp.mean(o, axis=-1, keepdims=True)
    xc = o - mu
    rstd = lax.rsqrt(jnp.mean(xc * xc, axis=-1, keepdims=True) + EPS)
    yh = xc * rstd
    return yh, rstd, yh * g + b


def _ret_specs(dk, dv, order):
    H = RET_HEADS
    qk_w, v_w = H * dk, H * dv
    off_q = 3 * ATT_W
    off_k, off_v, off_g = off_q + qk_w, off_q + 2 * qk_w, off_q + 2 * qk_w + v_w
    assert 2 * dk == dv and all(off % dv == 0 for off in (off_q, off_k, off_v, off_g))

    def col(off, j):
        return pl.BlockSpec((BLK, dv), lambda i: (order(i), off // dv + j))

    tab = pl.BlockSpec((H, BLK, BLK), lambda i: (0, 0, 0))
    return ([col(off_q, j) for j in range(H // 2)] + [col(off_k, j) for j in range(H // 2)]
            + [col(off_v, j) for j in range(H)] + [col(off_g, j) for j in range(H)]
            + [tab, tab, tab, pl.BlockSpec((H, 8, BLK), lambda i: (0, 0, 0))])


def _ret_heads(refs, dk):
    H = RET_HEADS
    q_refs, k_refs = refs[0:H // 2], refs[H // 2:H]
    v_refs, gr_refs = refs[H:2 * H], refs[2 * H:3 * H]

    def head(h):
        cols = slice((h % 2) * dk, (h % 2 + 1) * dk)
        return q_refs[h // 2][:, cols], k_refs[h // 2][:, cols], v_refs[h][...], gr_refs[h][...]

    return head, refs[3 * H:3 * H + 4]


def _ret_fwd(proj, gn_g, gn_b, dk, dv):
    S = proj.shape[0]
    N = S // BLK
    H = RET_HEADS
    kscale = dk ** -0.5
    n_in = 3 * H + 4

    def body(*refs):
        head, (dec_ref, xi_ref, zeta_ref, gc_ref) = _ret_heads(refs, dk)
        g_ref, b_ref, opre_ref, or_ref, st_ref, state, s_buf, cross_buf = refs[n_in:n_in + 8]
        n = pl.program_id(0)

        @pl.when(n == 0)
        def _():
            state[...] = jnp.zeros_like(state)

        for h in range(H):
            q, k, v, _ = head(h)
            s_buf[h] = _dot_nt(q, k)
            st = state[h]
            st_c = st.astype(CDT)
            st_ref[h] = st_c
            cross_buf[h] = _dot_nn(q, st_c)
            kz = (k.astype(F32) * zeta_ref[h][:, 0:1]).astype(CDT)
            state[h] = st * gc_ref[h][0:1, 0:1] + _dot_tn(kz, v)
        for h in range(H):
            vs = slice(h * dv, (h + 1) * dv)
            _, _, v, gr = head(h)
            s = s_buf[h] * kscale * dec_ref[h]
            o = _dot_nn(s.astype(CDT), v) + cross_buf[h] * xi_ref[h][:, 0:1]
            opre_ref[:, vs] = o
            _, _, y = _gn_fwd(o, g_ref[:, vs], b_ref[:, vs])
            gr = gr.astype(F32)
            or_ref[:, vs] = (y * (gr * _sigmoid(gr))).astype(or_ref.dtype)

    v_w = H * dv
    row = pl.BlockSpec((1, v_w), lambda i: (0, 0))
    tile = pl.BlockSpec((BLK, v_w), lambda i: (i, 0))
    return pl.pallas_call(
        body, name="ret_fwd", grid=(N,),
        in_specs=_ret_specs(dk, dv, lambda i: i) + [row, row],
        out_specs=[tile, tile, pl.BlockSpec((None, H, dk, dv), lambda i: (i, 0, 0, 0))],
        out_shape=[_sds((S, v_w), F32), _sds((S, v_w), CDT), _sds((N, H, dk, dv), CDT)],
        scratch_shapes=[pltpu.VMEM((H, dk, dv), F32), pltpu.VMEM((H, BLK, BLK), F32), pltpu.VMEM((H, BLK, dv), F32)],
        compiler_params=_params(1),
    )(*[proj] * (3 * H), *_ret_tables(dk), gn_g, gn_b)


def _ret_bwd(proj, gn_g, gn_b, o_pre, states, d_or, dga, dgb, dk, dv):
    S, in_w = proj.shape
    N = S // BLK
    H = RET_HEADS
    qk_w, v_w = H * dk, H * dv
    kscale = dk ** -0.5
    n_in = 3 * H + 4
    out_w = 2 * qk_w + 2 * v_w
    gate_w = dga.shape[1]
    col0 = 3 * ATT_W
    assert col0 + out_w + 2 * gate_w == in_w
    rev = lambda i: N - 1 - i

    def body(*refs):
        head, (dec_ref, xi_ref, zeta_ref, gc_ref) = _ret_heads(refs, dk)
        (g_ref, b_ref, opre_ref, st_ref, dor_ref, dga_ref, dgb_ref, dproj_ref, dg_ref, db_ref, dstate, stage,
         sem, do_buf, dox_buf, a_buf, g_buf, dq_buf, dk_buf, dv_buf) = refs[n_in:n_in + 20]
        i = pl.program_id(0)
        slot = i % 2
        out_ref = stage.at[slot]

        def out_copy(s, step):
            rows = pl.ds(pl.multiple_of(rev(step) * BLK, BLK), BLK)
            return pltpu.make_async_copy(stage.at[s], dproj_ref.at[rows, pl.ds(col0, in_w - col0)], sem.at[s])

        @pl.when(i >= 2)
        def _():
            out_copy(slot, i - 2).wait()

        @pl.when(i == 0)
        def _():
            dstate[...] = jnp.zeros_like(dstate)
            dg_ref[...] = jnp.zeros_like(dg_ref)
            db_ref[...] = jnp.zeros_like(db_ref)

        out_ref[:, out_w:out_w + gate_w] = dga_ref[...]
        out_ref[:, out_w + gate_w:out_w + 2 * gate_w] = dgb_ref[...]
        for h in range(H):
            vs = slice(h * dv, (h + 1) * dv)
            _, _, _, gr = head(h)
            gr = gr.astype(F32)
            sg = _sigmoid(gr)
            gain = g_ref[:, vs]
            yh, rstd, y = _gn_fwd(opre_ref[:, vs], gain, b_ref[:, vs])
            d_or_v = dor_ref[:, vs]
            dy = d_or_v * (gr * sg)
            out_ref[:, 2 * qk_w + v_w + h * dv:2 * qk_w + v_w + (h + 1) * dv] = (
                d_or_v * y * (sg * (1.0 + gr * (1.0 - sg)))).astype(out_ref.dtype)
            dg_ref[:, vs] += jnp.sum(dy * yh, axis=0, keepdims=True)
            db_ref[:, vs] += jnp.sum(dy, axis=0, keepdims=True)
            dyh = dy * gain
            do = rstd * (dyh - jnp.mean(dyh, axis=-1, keepdims=True)
                         - yh * jnp.mean(dyh * yh, axis=-1, keepdims=True))
            do_buf[h] = do.astype(CDT)
            dox_buf[h] = (do * xi_ref[h][:, 0:1]).astype(CDT)
        for h in range(H):
            q, k, v, _ = head(h)
            dox = dox_buf[h]
            a_buf[h] = _dot_nt(q, k)
            g_buf[h] = _dot_nt(do_buf[h], v)
            dsn = dstate[h]
            dsn_c = dsn.astype(CDT)
            kz = (k.astype(F32) * zeta_ref[h][:, 0:1]).astype(CDT)
            dq_buf[h] = _dot_nt(dox, st_ref[h])
            dk_buf[h] = _dot_nt(v, dsn_c)
            dv_buf[h] = _dot_nn(kz, dsn_c)
            dstate[h] = dsn * gc_ref[h][0:1, 0:1] + _dot_tn(q, dox)
        for h in range(H):
            q, k, _, _ = head(h)
            decay = dec_ref[h]
            a_c = (a_buf[h] * kscale * decay).astype(CDT)
            g_c = (g_buf[h] * decay).astype(CDT)
            dq = _dot_nn(g_c, k) * kscale + dq_buf[h]
            dkk = _dot_tn(g_c, q) * kscale + dk_buf[h] * zeta_ref[h][:, 0:1]
            dvv = _dot_tn(a_c, do_buf[h]) + dv_buf[h]
            out_ref[:, h * dk:(h + 1) * dk] = dq.astype(out_ref.dtype)
            out_ref[:, qk_w + h * dk:qk_w + (h + 1) * dk] = dkk.astype(out_ref.dtype)
            out_ref[:, 2 * qk_w + h * dv:2 * qk_w + (h + 1) * dv] = dvv.astype(out_ref.dtype)

        cp = out_copy(slot, i)
        cp.start()

        @pl.when(i == N - 1)
        def _():
            cp.wait()
            if N >= 2:
                out_copy(1 - slot, i - 1).wait()

    row = pl.BlockSpec((1, v_w), lambda i: (0, 0))
    tile = pl.BlockSpec((BLK, v_w), lambda i: (rev(i), 0))
    gate = pl.BlockSpec((BLK, gate_w), lambda i: (rev(i), 0))
    return pl.pallas_call(
        body, name="ret_bwd", grid=(N,),
        in_specs=_ret_specs(dk, dv, rev) + [row, row, tile,
                 pl.BlockSpec((None, H, dk, dv), lambda i: (rev(i), 0, 0, 0)), tile, gate, gate],
        out_specs=[pl.BlockSpec(memory_space=pl.ANY), row, row],
        out_shape=[_sds((S, in_w), CDT), _sds((1, v_w), F32), _sds((1, v_w), F32)],
        scratch_shapes=[pltpu.VMEM((H, dk, dv), F32), pltpu.VMEM((2, BLK, in_w - col0), CDT),
                        pltpu.SemaphoreType.DMA((2,)),
                        pltpu.VMEM((H, BLK, dv), CDT), pltpu.VMEM((H, BLK, dv), CDT),
                        pltpu.VMEM((H, BLK, BLK), F32), pltpu.VMEM((H, BLK, BLK), F32),
                        pltpu.VMEM((H, BLK, dk), F32), pltpu.VMEM((H, BLK, dk), F32), pltpu.VMEM((H, BLK, dv), F32)],
        compiler_params=_params(1),
    )(*[proj] * (3 * H), *_ret_tables(dk), gn_g, gn_b, o_pre, states, d_or, dga, dgb)


def _merge_fwd(os_, ls_, o_r, wa, wb, proj, d_model, tm=512, tn=512):
    S, in_w = proj.shape
    off_a, off_b = in_w - 2 * d_model, in_w - d_model
    assert off_a % tn == 0 and off_b % tn == 0
    dil = [d for _, d in ATT_GROUPS]

    def body(*refs):
        or_ref, wa_ref, wb_ref, ga_ref, gb_ref, y_ref, pa_ref, pb_ref, oa_ref = refs[6:15]
        bufs = refs[15:21]

        @pl.when(pl.program_id(1) == 0)
        def _():
            get = [_rm_reader(bufs[j], refs[j], dil[j % 3]) for j in range(6)]
            for s in range(ATT_HPG):
                al = _mix_alpha(*[get[3 + g](s, slice(None)) for g in range(3)])
                mixed = (al[0] * get[0](s, slice(None)) + al[1] * get[1](s, slice(None))
                         + al[2] * get[2](s, slice(None)))
                oa_ref[:, s * HD:(s + 1) * HD] = mixed.astype(oa_ref.dtype)

        pa = _dot_nn(oa_ref[...], wa_ref[...])
        pb = _dot_nn(or_ref[...], wb_ref[...])
        y = _sigmoid(ga_ref[...].astype(F32)) * pa + _sigmoid(gb_ref[...].astype(F32)) * pb
        y_ref[...] = y.astype(y_ref.dtype)
        pa_ref[...] = pa.astype(pa_ref.dtype)
        pb_ref[...] = pb.astype(pb_ref.dtype)

    def rm(d):
        if d == 1:
            return pl.BlockSpec((tm, GW), lambda i, j: (i, 0))
        return pl.BlockSpec((d, tm // d, GW), lambda i, j: (0, i, 0))

    ka, kb = GW, o_r.shape[1]
    out = pl.BlockSpec((tm, tn), lambda i, j: (i, j))
    return pl.pallas_call(
        body, name="merge_fwd", grid=(S // tm, d_model // tn),
        in_specs=[rm(d) for d in dil] * 2
        + [pl.BlockSpec((tm, kb), lambda i, j: (i, 0)),
           pl.BlockSpec((ka, tn), lambda i, j: (0, j)), pl.BlockSpec((kb, tn), lambda i, j: (0, j)),
           pl.BlockSpec((tm, tn), lambda i, j: (i, off_a // tn + j)),
           pl.BlockSpec((tm, tn), lambda i, j: (i, off_b // tn + j))],
        out_specs=[out, out, out, pl.BlockSpec((tm, GW), lambda i, j: (i, 0))],
        out_shape=[_sds((S, d_model), CDT)] * 3 + [_sds((S, GW), CDT)],
        scratch_shapes=[pltpu.VMEM((ATT_HPG, tm, HD), F32)] * 6,
        compiler_params=_params(2))(*os_, *ls_, o_r, wa, wb, proj, proj)


def _local_step(x, target, w_in_mine, chip, others, far_w_in, small, late_weights, on_grads, deps0=()):
    S, D = x.shape
    ns_in = w_in_mine.shape[1]
    in_w = N_CHIPS * ns_in
    d_ff = 4 * D
    ret_v_w = 2 * D
    dv = ret_v_w // RET_HEADS
    dk = (in_w - 3 * ATT_W - 2 * ret_v_w - 2 * D) // (2 * RET_HEADS)
    gqk = jnp.concatenate([small["q_norm_g"].reshape(1, ATT_W), small["k_norm_g"].reshape(1, ATT_W)], axis=1)
    g1, g2 = small["norm1_g"], small["norm2_g"]
    gn_g, gn_b = small["ret_gn_g"], small["ret_gn_b"]

    xn = _rms_fwd("rms1_fwd", x, g1)
    proj_sds = _sds((S, in_w), CDT)
    (proj_mine,) = _matmul(
        "in_proj_mine", "nn", xn, w_in_mine, tm=512, tn=ns_in, tk=D, prefetch=[chip],
        outs=[(proj_sds, pl.BlockSpec((512, ns_in), lambda i, j, k, c: (i, c[0])))], epilogue=_ep_store, deps=deps0)
    w_in = far_w_in(proj_mine)
    (proj,) = _matmul(
        "in_proj_far", "nn", xn, w_in, tm=512, tn=ns_in, tk=D, prefetch=[others], n_cols=(N_CHIPS - 1) * ns_in,
        b_spec=pl.BlockSpec((None, D, ns_in), lambda i, j, k, o: (o[j], 0, 0)),
        outs=[(proj_sds, pl.BlockSpec((512, ns_in), lambda i, j, k, o: (i, o[j])))], epilogue=_ep_store,
        deps=[proj_mine], alias_dep_to_out=(0, 0), j_outer=True)
    qkv = _qknorm_fwd(proj, gqk)
    att = [_att_fwd(g, S, qkv[g]) for g in range(3)]
    os_, ls_ = [a[0] for a in att], [a[1] for a in att]
    o_pre, o_r, states = _ret_fwd(proj, gn_g, gn_b, dk, dv)
    w = late_weights(o_r)
    y, pa, pb, o_a = _merge_fwd(os_, ls_, o_r, w["w_proj_a"], w["w_proj_b"], proj, D)
    x1, xn2 = _matmul("out_proj", "nn", y, w["w_out"], tm=512, tn=D, tk=D,
                      extras=[(x, _mn(512, D)), (g2, _row(D))],
                      outs=[(_sds((S, D), F32), _mn(512, D)), (_sds((S, D), CDT), _mn(512, D))],
                      epilogue=_ep_resid_norm)
    hid, act = _matmul("mlp_up", "nn", xn2, w["w_up"], tm=512, tn=2048, tk=D, j_outer=True,
                       outs=[(_sds((S, d_ff), CDT), _mn(512, 2048))] * 2, epilogue=_ep_up)
    dx2, dx2c, loss_row = _matmul(
        "mlp_down_loss", "nn", act, w["w_down"], tm=512, tn=D, tk=d_ff,
        extras=[(x1, _mn(512, D)), (target, _mn(512, D))],
        outs=[(_sds((S, D), F32), _mn(512, D)), (_sds((S, D), CDT), _mn(512, D)), (_sds((1, D), F32), _row(D))],
        epilogue=functools.partial(_ep_down_loss, inv_d=1.0 / D))
    loss = 0.5 * jnp.sum(loss_row) / D

    (dh,) = _matmul("d_hidden", "nt", dx2c, w["w_down"], tm=512, tn=2048, tk=D, j_outer=True,
                    extras=[(hid, _mn(512, 2048))], outs=[(_sds((S, d_ff), CDT), _mn(512, 2048))], epilogue=_ep_dh)
    (gw_down,) = _matmul("dw_down", "tn", act, dx2c, tm=1024, tn=D, tk=1024,
                         outs=[(_sds((d_ff, D), F32), _mn(1024, D))], epilogue=_ep_store)
    (gw_up,) = _matmul("dw_up", "tn", xn2, dh, tm=D, tn=1024, tk=1024,
                       outs=[(_sds((D, d_ff), F32), _mn(D, 1024))], epilogue=_ep_store)
    tok = on_grads({"w_down": gw_down, "w_up": gw_up})
    dx1, dx1c, dg2 = _matmul(
        "d_x1", "nt", dh, w["w_up"], tm=512, tn=D, tk=d_ff,
        extras=[(x1, _mn(512, D)), (g2, _row(D)), (dx2, _mn(512, D))],
        outs=[(_sds((S, D), F32), _mn(512, D)), (_sds((S, D), CDT), _mn(512, D)), (_sds((1, D), F32), _row(D))],
        epilogue=_ep_rms_bwd, deps=[tok])

    gt = 512
    assert (in_w - 2 * D) % gt == 0
    off_a, off_b = (in_w - 2 * D) // gt, (in_w - D) // gt
    dpa, dpb, dga, dgb = _matmul(
        "d_gates", "nt", dx1c, w["w_out"], tm=512, tn=gt, tk=D,
        extras=[(proj, _mn(512, gt, off_a)), (proj, _mn(512, gt, off_b)), (pa, _mn(512, gt)), (pb, _mn(512, gt))],
        outs=[(_sds((S, D), CDT), _mn(512, gt))] * 4, epilogue=_ep_gates)
    (gw_out,) = _matmul("dw_out", "tn", y, dx1c, tm=D, tn=D, tk=1024,
                        outs=[(_sds((D, D), F32), _mn(D, D))], epilogue=_ep_store)
    (gw_pa,) = _matmul("dw_proj_a", "tn", o_a, dpa, tm=GW, tn=D, tk=1024,
                       outs=[(_sds((GW, D), F32), _mn(GW, D))], epilogue=_ep_store)
    (gw_pb,) = _matmul("dw_proj_b", "tn", o_r, dpb, tm=1024, tn=D, tk=1024,
                       outs=[(_sds((ret_v_w, D), F32), _mn(1024, D))], epilogue=_ep_store)
    (do_a,) = _matmul("d_o_a", "nt", dpa, w["w_proj_a"], tm=1024, tn=GW, tk=D,
                      outs=[(_sds((S, GW), F32), _mn(1024, GW))], epilogue=_ep_store)
    tok = on_grads({"w_out": gw_out, "w_proj_a": gw_pa, "w_proj_b": gw_pb})
    (d_or,) = _matmul("d_o_r", "nt", dpb, w["w_proj_b"], tm=512, tn=ret_v_w, tk=D,
                      outs=[(_sds((S, ret_v_w), F32), _mn(512, ret_v_w))], epilogue=_ep_store, deps=[tok])

    dproj, dgn_g, dgn_b = _ret_bwd(proj, gn_g, gn_b, o_pre, states, d_or, dga, dgb, dk, dv)
    do_gs, c_gs = _mix_bwd(S, os_, ls_, do_a)
    datt_parts = [_att_bwd(g, S, qkv[g], ls_[g], do_gs[g], c_gs[g]) for g in range(3)]
    dproj, dgqk = _qknorm_bwd(proj, gqk, [p[0] for p in datt_parts], [p[1] for p in datt_parts],
                              [p[2] for p in datt_parts], dproj)

    (gw_in,) = _matmul(
        "dw_in", "tn", xn, dproj, tm=512, tn=ns_in, tk=1024,
        outs=[(_sds((N_CHIPS, D, ns_in), F32), pl.BlockSpec((None, 512, ns_in), lambda i, j, k: (j, i, 0)))],
        epilogue=_ep_store)
    tok = on_grads({"w_in": gw_in})
    grad_x, dg1 = _matmul(
        "d_x", "nt", dproj, w_in, tm=512, tn=D, tk=ns_in, n_cols=D,
        b_spec=pl.BlockSpec((None, D, ns_in), lambda i, j, k: (k, 0, 0)),
        extras=[(x, _mn(512, D)), (g1, _row(D)), (dx1, _mn(512, D))],
        outs=[(_sds((S, D), F32), _mn(512, D)), (_sds((1, D), F32), _row(D))],
        epilogue=_ep_rms_bwd, deps=[tok])

    smallg = {"norm1_g": dg1, "q_norm_g": dgqk[:, :ATT_W], "k_norm_g": dgqk[:, ATT_W:],
              "ret_gn_g": dgn_g, "ret_gn_b": dgn_b, "norm2_g": dg2}
    return loss, grad_x, smallg


N_CHIPS = 4
N_DEV = 8


def _place():
    x, y, c = lax.axis_index("x"), lax.axis_index("y"), lax.axis_index("c")
    return x, y, c


def _other_chips(x, y):
    out = []
    for fx, fy in ((1, 0), (0, 1), (1, 1)):
        px = 1 - x if fx else x
        py = 1 - y if fy else y
        out.append(((px, py), 2 * px + py))
    return out


SEM_SPEC = pl.BlockSpec(memory_space=pltpu.SEMAPHORE)
ANY_SPEC = pl.BlockSpec(memory_space=pl.ANY)
EFFECT = pltpu.SideEffectType.DATAFLOW_SIDE_EFFECTING


def _ici_copies(kind, srcs, lands, send, recv):
    x, y, c = _place()
    me = 2 * x + y
    out = []
    for w, (s, l) in enumerate(zip(srcs, lands)):
        for j, ((px, py), pidx) in enumerate(_other_chips(x, y)):
            if kind == "gather":
                half = s.shape[0] // 2
                rows = pl.ds(c * half, half)
                src, dst_there, dst_here = s.at[rows, :], l.at[me, rows, :], l.at[pidx, rows, :]
            else:
                src, dst_there, dst_here = s.at[pidx], l.at[me], l.at[pidx]
            out.append((src, dst_there, dst_here, send.at[3 * w + j], recv.at[3 * w + j], (px, py, c)))
    return out


def _exchange_start(name, kind, srcs, land_shapes):
    n = len(srcs)

    def body(*refs):
        src_refs, land_refs = refs[:n], refs[n:2 * n]
        send, recv = refs[2 * n], refs[2 * n + 1]
        token = refs[-1]
        for src, dst, _, ss, rs, dev in _ici_copies(kind, src_refs, land_refs, send, recv):
            pltpu.make_async_remote_copy(src_ref=src, dst_ref=dst, send_sem=ss, recv_sem=rs, device_id=dev,
                                         device_id_type=MESH).start()
        token[...] = jnp.zeros_like(token)

    thru = [pltpu.HBM(s.shape, s.dtype) for s in srcs] + [pltpu.HBM(shape, dtype) for shape, dtype in land_shapes]
    res = pl.pallas_call(
        body, name=name,
        out_shape=(pltpu.SemaphoreType.DMA((3 * n,)), pltpu.SemaphoreType.DMA((3 * n,)), *thru, _sds((8, LANES), F32)),
        in_specs=[HBM_SPEC] * (2 * n), out_specs=(SEM_SPEC, SEM_SPEC, *[HBM_SPEC] * (2 * n), VMEM_SPEC),
        input_output_aliases={i: 2 + i for i in range(2 * n)},
        compiler_params=pltpu.CompilerParams(has_side_effects=EFFECT),
    )(*[pltpu.with_memory_space_constraint(s, pltpu.HBM) for s in srcs],
      *[pltpu.with_memory_space_constraint(lax.empty(shape, dtype), pltpu.HBM) for shape, dtype in land_shapes])
    return res[0], res[1], list(res[2:2 + n]), list(res[2 + n:2 + 2 * n]), res[-1]


def _exchange_wait(name, kind, send, recv, srcs, lands, after):
    n = len(srcs)

    def body(*refs):
        src_refs, land_refs = refs[:n], refs[n:2 * n]
        send_ref, recv_ref = refs[2 * n], refs[2 * n + 1]
        for src, _, dst, ss, rs, dev in _ici_copies(kind, src_refs, land_refs, send_ref, recv_ref):
            cp = pltpu.make_async_remote_copy(src_ref=src, dst_ref=dst, send_sem=ss, recv_sem=rs, device_id=dev,
                                              device_id_type=MESH)
            cp.wait_send()
            cp.wait_recv()

    thru = [pltpu.HBM(t.shape, t.dtype) for t in list(srcs) + list(lands)]
    res = pl.pallas_call(
        body, name=name, out_shape=thru,
        in_specs=[HBM_SPEC] * (2 * n) + [SEM_SPEC, SEM_SPEC, ANY_SPEC], out_specs=[HBM_SPEC] * (2 * n),
        input_output_aliases={i: i for i in range(2 * n)},
        compiler_params=pltpu.CompilerParams(has_side_effects=EFFECT),
    )(*srcs, *lands, send, recv, after)
    return list(res[:n]), list(res[n:])


PAIR_TILE_ELEMS = 1 << 19


def _pair_fill(name, gathered, mine, core, others, chip):
    k, r, C = gathered.shape
    half = r // 2
    tr = _row_tile(half, C, PAIR_TILE_ELEMS, mult=16)
    nt = half // tr
    n_far = N_CHIPS - 1

    def body(c_ref, o_ref, chip_ref, in_ref, mine_ref, out_ref, slot, send, recv):
        j = pl.program_id(0)
        _sibling_barrier((j == 0) & (pl.program_id(1) == 0))
        b = (j * nt + pl.program_id(1)) % 2
        x, y, c = _place()
        cp = pltpu.make_async_remote_copy(src_ref=in_ref, dst_ref=slot.at[b], send_sem=send.at[b],
                                          recv_sem=recv.at[b], device_id=(x, y, 1 - c), device_id_type=MESH)

        @pl.when(j < n_far)
        def _():
            cp.start()
            cp.wait_recv()
            out_ref[...] = slot[b]
            cp.wait_send()

        @pl.when(j >= n_far)
        def _():
            out_ref[...] = mine_ref[...]

    def far(j):
        return jnp.minimum(j, n_far - 1)

    grid_spec = pltpu.PrefetchScalarGridSpec(
        num_scalar_prefetch=3, grid=(n_far + 2, nt),
        in_specs=[pl.BlockSpec((tr, C), lambda j, i, c, o, m: (
                      (2 * o[far(j)] + c[0]) * nt + jnp.where(j < n_far, i, nt - 1), 0)),
                  pl.BlockSpec((tr, C), lambda j, i, c, o, m: (jnp.where(j < n_far, 0, (j - n_far) * nt + i), 0))],
        out_specs=pl.BlockSpec((tr, C), lambda j, i, c, o, m: (
            jnp.where(j < n_far, 2 * o[far(j)] + 1 - c[0], 2 * m[0] + j - n_far) * nt + i, 0)),
        scratch_shapes=[pltpu.VMEM((2, tr, C), gathered.dtype), pltpu.SemaphoreType.DMA((2,)),
                        pltpu.SemaphoreType.DMA((2,))])
    out = pl.pallas_call(body, name=name, grid_spec=grid_spec, out_shape=_sds((k * r, C), gathered.dtype),
                         input_output_aliases={3: 0}, compiler_params=_params(2, PAIR_FILL_ID))(
                             core, others, chip, gathered.reshape(k * r, C), mine)
    return out.reshape(k, r, C)


def _pair_reduce(name, g, core):
    k, R, C = g.shape
    half = R // 2
    tr = _row_tile(half, C, PAIR_TILE_ELEMS, mult=16)
    nt = half // tr

    def body(c_ref, mine_ref, give_ref, out_ref, wire_ref, stage, slot, send, recv):
        _sibling_barrier((pl.program_id(0) == 0) & (pl.program_id(1) == 0))
        b = (pl.program_id(0) * nt + pl.program_id(1)) % 2
        x, y, c = _place()
        stage[b] = give_ref[...].astype(stage.dtype)
        cp = pltpu.make_async_remote_copy(src_ref=stage.at[b], dst_ref=slot.at[b], send_sem=send.at[b],
                                          recv_sem=recv.at[b], device_id=(x, y, 1 - c), device_id_type=MESH)
        cp.start()
        cp.wait_recv()
        tot = mine_ref[...] + slot[b].astype(F32)
        out_ref[...] = tot
        wire_ref[...] = tot.astype(wire_ref.dtype)
        cp.wait_send()

    blk = (tr, C)
    out_spec = pl.BlockSpec(blk, lambda s, i, c: (s * nt + i, 0))
    grid_spec = pltpu.PrefetchScalarGridSpec(
        num_scalar_prefetch=1, grid=(k, nt),
        in_specs=[pl.BlockSpec(blk, lambda s, i, c: ((2 * s + c[0]) * nt + i, 0)),
                  pl.BlockSpec(blk, lambda s, i, c: ((2 * s + 1 - c[0]) * nt + i, 0))],
        out_specs=[out_spec, out_spec],
        scratch_shapes=[pltpu.VMEM((2, tr, C), CDT), pltpu.VMEM((2, tr, C), CDT), pltpu.SemaphoreType.DMA((2,)),
                        pltpu.SemaphoreType.DMA((2,))])
    g2 = g.reshape(k * R, C)
    out, wire = pl.pallas_call(body, name=name, grid_spec=grid_spec,
                               out_shape=[_sds((k * half, C), F32), _sds((k * half, C), CDT)],
                               compiler_params=_params(2, PAIR_REDUCE_ID))(core, g2, g2)
    return out, wire.reshape(k, half, C)


def _all_reduce_small(v):
    r, cdim = v.shape

    def body(v_ref, o_ref, buf, send, recv):
        x, y, c = _place()
        me = 4 * x + 2 * y + c
        buf[me] = v_ref[...]
        sends = []
        for m in range(1, N_DEV):
            px = 1 - x if m & 4 else x
            py = 1 - y if m & 2 else y
            pc = 1 - c if m & 1 else c
            cp = pltpu.make_async_remote_copy(src_ref=v_ref, dst_ref=buf.at[me], send_sem=send.at[m - 1],
                                              recv_sem=recv.at[m - 1], device_id=(px, py, pc), device_id_type=MESH)
            cp.start()
            sends.append((cp, 4 * px + 2 * py + pc))
        for m, (cp, pidx) in enumerate(sends):
            pltpu.make_async_remote_copy(src_ref=v_ref, dst_ref=buf.at[pidx], send_sem=send.at[m], recv_sem=recv.at[m],
                                         device_id=(x, y, c), device_id_type=MESH).wait_recv()
        for cp, _ in sends:
            cp.wait_send()
        tot = buf[0]
        for k in range(1, N_DEV):
            tot = tot + buf[k]
        o_ref[...] = tot

    return pl.pallas_call(
        body, name="all_reduce_small", in_specs=[VMEM_SPEC], out_specs=VMEM_SPEC,
        out_shape=_sds((r, cdim), F32),
        scratch_shapes=[pltpu.VMEM((N_DEV, r, cdim), F32), pltpu.SemaphoreType.DMA((N_DEV - 1,)),
                        pltpu.SemaphoreType.DMA((N_DEV - 1,))],
    )(v)


def _row_tile(rows, cols, budget_elems=1 << 18, mult=8):
    if rows % mult:
        return rows
    t = max(mult, (budget_elems // cols) // mult * mult)
    while rows % t:
        t -= mult
    return t


def _adamw_update(w, g, m, v):
    nm = ADAM_B1 * m + (1.0 - ADAM_B1) * g
    nv = ADAM_B2 * v + (1.0 - ADAM_B2) * (g * g)
    m_hat = nm / (1.0 - ADAM_B1 ** ADAM_STEP)
    v_hat = nv / (1.0 - ADAM_B2 ** ADAM_STEP)
    return -ADAM_LR * (m_hat / (jnp.sqrt(v_hat) + ADAM_EPS) + ADAM_WD * w), nm, nv


def _adamw(name, w, g, m, v):
    R, C = w.shape
    tr = _row_tile(R, C, 1 << 17)

    def body(w_ref, g_ref, m_ref, v_ref, d_ref, nm_ref, nv_ref):
        d_ref[...], nm_ref[...], nv_ref[...] = _adamw_update(w_ref[...], g_ref[...], m_ref[...], v_ref[...])

    spec = pl.BlockSpec((tr, C), lambda i: (i, 0))
    return pl.pallas_call(body, name=name, grid=(R // tr,), in_specs=[spec] * 4, out_specs=[spec] * 3,
                          out_shape=[_sds((R, C), F32)] * 3, compiler_params=_params(1))(w, g, m, v)


def _sum_share(name, own, by_chip, chip, others, core):
    k, half, C = by_chip.shape
    tr = _row_tile(half, C, mult=16)
    nt = half // tr

    def body(chip_ref, oth_ref, c_ref, own_ref, a_ref, b_ref, cc_ref, g_out, mine, slot, send, recv):
        p = pl.program_id(1)
        _sibling_barrier((pl.program_id(0) == 0) & (p == 0))
        b = pl.program_id(0) % 2
        x, y, c = _place()
        cp = pltpu.make_async_remote_copy(src_ref=mine.at[b], dst_ref=slot.at[b], send_sem=send.at[b],
                                          recv_sem=recv.at[b], device_id=(x, y, 1 - c), device_id_type=MESH)

        @pl.when(p == 0)
        def _():
            tot = ((own_ref[...] + a_ref[...].astype(F32)) + b_ref[...].astype(F32)) + cc_ref[...].astype(F32)
            mine[b] = tot
            cp.start()
            g_out[...] = tot

        @pl.when(p == 1)
        def _():
            cp.wait_recv()
            g_out[...] = slot[b]
            cp.wait_send()

    def piece(j):
        return pl.BlockSpec((tr, C), lambda i, p, chip, oth, c: (oth[j] * nt + i, 0))

    grid_spec = pltpu.PrefetchScalarGridSpec(
        num_scalar_prefetch=3, grid=(nt, 2),
        in_specs=[pl.BlockSpec((tr, C), lambda i, p, chip, oth, c: (chip[0] * nt + i, 0)),
                  piece(0), piece(1), piece(2)],
        out_specs=pl.BlockSpec((tr, C), lambda i, p, chip, oth, c: (
            jnp.where(p == 0, c[0], 1 - c[0]) * nt + i, 0)),
        scratch_shapes=[pltpu.VMEM((2, tr, C), F32), pltpu.VMEM((2, tr, C), F32), pltpu.SemaphoreType.DMA((2,)),
                        pltpu.SemaphoreType.DMA((2,))])
    by2 = by_chip.reshape(k * half, C)
    return pl.pallas_call(body, name=name, grid_spec=grid_spec, out_shape=_sds((2 * half, C), F32),
                          compiler_params=_params(2, SUM_SHARE_ID))(chip, others, core, own, by2, by2, by2)


BIG = ("w_in", "w_proj_a", "w_proj_b", "w_out", "w_up", "w_down")
COL_SHARDED = ("w_in", "w_proj_a", "w_up")
SMALL = ("norm1_g", "q_norm_g", "k_norm_g", "ret_gn_g", "ret_gn_b", "norm2_g")
ALL_W = ("norm1_g", "w_in", "q_norm_g", "k_norm_g", "ret_gn_g", "ret_gn_b", "w_proj_a", "w_proj_b", "w_out",
         "norm2_g", "w_up", "w_down")
LANES = 128


def _to_full(name, gathered):
    k, r, c = gathered.shape
    if name in COL_SHARDED:
        return gathered.transpose(1, 0, 2).reshape(r, k * c)
    return gathered.reshape(k * r, c)


def _to_shard_major(name, full):
    if name in COL_SHARDED:
        r, c4 = full.shape
        return full.reshape(r, N_CHIPS, c4 // N_CHIPS).transpose(1, 0, 2)
    r4, c = full.shape
    return full.reshape(N_CHIPS, r4 // N_CHIPS, c)


def kernel(x, norm1_g, w_in, q_norm_g, k_norm_g, ret_gn_g, ret_gn_b, w_proj_a, w_proj_b, w_out, norm2_g, w_up, w_down, loss_target, m_norm1_g, m_w_in, m_q_norm_g, m_k_norm_g, m_ret_gn_g, m_ret_gn_b, m_w_proj_a, m_w_proj_b, m_w_out, m_norm2_g, m_w_up, m_w_down, v_norm1_g, v_w_in, v_q_norm_g, v_k_norm_g, v_ret_gn_g, v_ret_gn_b, v_w_proj_a, v_w_proj_b, v_w_out, v_norm2_g, v_w_up, v_w_down):
    weights = dict(norm1_g=norm1_g, w_in=w_in, q_norm_g=q_norm_g, k_norm_g=k_norm_g, ret_gn_g=ret_gn_g,
                   ret_gn_b=ret_gn_b, w_proj_a=w_proj_a, w_proj_b=w_proj_b, w_out=w_out, norm2_g=norm2_g,
                   w_up=w_up, w_down=w_down)
    moments_m = dict(norm1_g=m_norm1_g, w_in=m_w_in, q_norm_g=m_q_norm_g, k_norm_g=m_k_norm_g, ret_gn_g=m_ret_gn_g,
                     ret_gn_b=m_ret_gn_b, w_proj_a=m_w_proj_a, w_proj_b=m_w_proj_b, w_out=m_w_out,
                     norm2_g=m_norm2_g, w_up=m_w_up, w_down=m_w_down)
    moments_v = dict(norm1_g=v_norm1_g, w_in=v_w_in, q_norm_g=v_q_norm_g, k_norm_g=v_k_norm_g, ret_gn_g=v_ret_gn_g,
                     ret_gn_b=v_ret_gn_b, w_proj_a=v_w_proj_a, w_proj_b=v_w_proj_b, w_out=v_w_out,
                     norm2_g=v_norm2_g, w_up=v_w_up, w_down=v_w_down)

    mx, my = lax.axis_index("x"), lax.axis_index("y")
    core = lax.axis_index("c").astype(jnp.int32).reshape(1)
    chip = (2 * mx + my).astype(jnp.int32).reshape(1)
    others = jnp.stack([2 * (1 - mx) + my, 2 * mx + 1 - my, 2 * (1 - mx) + 1 - my]).astype(jnp.int32)
    shards = {n: weights[n][0].astype(CDT) for n in BIG}
    def start_gather(name, names):
        return _exchange_start(name, "gather", [shards[n] for n in names],
                               [((N_CHIPS,) + shards[n].shape, CDT) for n in names])

    i_send, i_recv, i_srcs, i_lands, i_token = start_gather("gather_w_in_start", ["w_in"])
    late = [n for n in BIG if n != "w_in"]
    l_send, l_recv, l_srcs, l_lands, l_token = start_gather("gather_late_start", late)

    def far_w_in(after):
        srcs, lands = _exchange_wait("gather_w_in_wait", "gather", i_send, i_recv, i_srcs, i_lands, after)
        return _pair_fill("pair_fill_w_in", lands[0], srcs[0], core, others, chip)

    def late_weights(after):
        srcs, lands = _exchange_wait("gather_late_wait", "gather", l_send, l_recv, l_srcs, l_lands, after)
        out = {}
        for n, mine, land in zip(late, srcs, lands):
            out[n] = _to_full(n, _pair_fill("pair_fill_%s" % n, land, mine, core, others, chip))
        return out

    pending = []

    def on_grads(group):
        names = list(group)
        red = [_pair_reduce("pair_reduce_%s" % n, g if g.ndim == 3 else _to_shard_major(n, g), core)
               for n, g in group.items()]
        wires = [wire for _, wire in red]
        send, recv, srcs, lands, token = _exchange_start(
            "scatter_start_%s" % names[0], "scatter", wires, [(wire.shape, wire.dtype) for wire in wires])
        pending.append((names, [own for own, _ in red], send, recv, srcs, lands))
        return token

    small = {n: weights[n].reshape(1, -1) for n in SMALL}

    loss, grad_x, small_g = _local_step(x[0], loss_target[0], i_srcs[0], chip, others, far_w_in, small,
                                        late_weights, on_grads, deps0=[i_token, l_token])
    loss = lax.psum(loss, ("x", "y", "c"))

    out_g, out_d, out_m, out_v = {}, {}, {}, {}
    for names, owns, send, recv, srcs, lands in pending:
        _, got = _exchange_wait("scatter_wait_%s" % names[0], "scatter", send, recv, srcs, lands, grad_x)
        for n, own, by_chip in zip(names, owns, got):
            shape = weights[n].shape
            g2 = _sum_share("sum_share_%s" % n, own, by_chip, chip, others, core)
            d, nm, nv = _adamw("adamw_%s" % n, weights[n][0], g2, moments_m[n][0], moments_v[n][0])
            out_g[n], out_d[n], out_m[n], out_v[n] = (t.reshape(shape) for t in (g2, d, nm, nv))

    packed = jnp.concatenate([small_g[n].reshape(1, -1) for n in SMALL], axis=1)
    red = _all_reduce_small(packed.reshape(-1, LANES)).reshape(1, -1)
    off = 0
    for n in SMALL:
        shape = weights[n].shape
        row = (1, weights[n].size)
        g2 = red[:, off:off + row[1]]
        off += row[1]
        d, nm, nv = _adamw("adamw_%s" % n, weights[n].reshape(row), g2, moments_m[n].reshape(row),
                           moments_v[n].reshape(row))
        out_g[n], out_d[n], out_m[n], out_v[n] = (t.reshape(shape) for t in (g2, d, nm, nv))

    return (loss, grad_x[None], *[out_g[n] for n in ALL_W], *[out_d[n] for n in ALL_W],
            *[out_m[n] for n in ALL_W], *[out_v[n] for n in ALL_W])
```

```python
import functools
import math

import jax
import jax.numpy as jnp
from jax import lax
from jax.experimental import pallas as pl
from jax.experimental.pallas import tpu as pltpu

CDT = jnp.bfloat16
F32 = jnp.float32
EPS = 1e-6

ATT_GROUPS = ((128, 1), (512, 4), (2048, 16))
ATT_HPG = 4
ATT_HEADS = 12
HD = 128
BLK = 128
ATT_W = ATT_HEADS * HD
GW = ATT_HPG * HD
RET_HEADS = 4

ADAM_LR = 0.001
ADAM_B1 = 0.9
ADAM_B2 = 0.999
ADAM_EPS = 1e-08
ADAM_WD = 0.01
ADAM_STEP = 10

VMEM_LIMIT_BYTES = 56 * 1024 * 1024
MXU_DIM = 256
MESH = pl.DeviceIdType.MESH
HBM_SPEC = pl.BlockSpec(memory_space=pltpu.HBM)
VMEM_SPEC = pl.BlockSpec(memory_space=pltpu.VMEM)


def _params(n_axes, collective_id=None):
    return pltpu.CompilerParams(dimension_semantics=("arbitrary",) * n_axes,
                                vmem_limit_bytes=VMEM_LIMIT_BYTES, collective_id=collective_id)


PAIR_FILL_ID, PAIR_REDUCE_ID, SUM_SHARE_ID = 1, 2, 3


def _sibling_barrier(first_step):
    @pl.when(first_step)
    def _():
        sem = pltpu.get_barrier_semaphore()
        x, y, c = lax.axis_index("x"), lax.axis_index("y"), lax.axis_index("c")
        pl.semaphore_signal(sem, inc=1, device_id=(x, y, 1 - c), device_id_type=pl.DeviceIdType.MESH)
        pl.semaphore_wait(sem, 1)


def _dot_nn(a, b):
    return jnp.dot(a, b, preferred_element_type=F32)


def _dot_nt(a, b):
    return lax.dot_general(a, b, (((1,), (1,)), ((), ())), preferred_element_type=F32)


def _dot_tn(a, b):
    return lax.dot_general(a, b, (((0,), (0,)), ((), ())), preferred_element_type=F32)


def _sigmoid(v):
    return 1.0 / (1.0 + jnp.exp(-v))


def _matmul(name, mode, a, b, *, tm, tn, tk, extras=(), outs, epilogue, deps=(), b_spec=None, n_cols=None,
            prefetch=(), alias_dep_to_out=None, j_outer=False):
    deps = [d for d in deps if d is not None]
    if mode == "tn":
        K, M = a.shape
    else:
        M, K = a.shape
    if b_spec is None:
        (N, K2) = b.shape if mode == "nt" else b.shape[::-1]
        assert K == K2, (name, a.shape, b.shape)
        if mode == "nt":
            b_spec = pl.BlockSpec((tn, tk), lambda i, j, k, *p: (j, k))
        else:
            b_spec = pl.BlockSpec((tk, tn), lambda i, j, k, *p: (k, j))
    else:
        N = n_cols
    assert M % tm == 0 and N % tn == 0 and K % tk == 0, (name, a.shape, b.shape)
    ni, nj, nk = M // tm, N // tn, K // tk
    if mode == "tn":
        a_spec = pl.BlockSpec((tk, tm), lambda i, j, k, *p: (k, i))
    else:
        a_spec = pl.BlockSpec((tm, tk), lambda i, j, k, *p: (i, k))
    dot = {"nn": _dot_nn, "nt": _dot_nt, "tn": _dot_tn}[mode]
    n_ex, n_out, n_dep, n_pre = len(extras), len(outs), len(deps), len(prefetch)
    grid = (ni, nj, nk)
    if j_outer:
        grid = (nj, ni, nk)

        def swapped(spec):
            return pl.BlockSpec(spec.block_shape, lambda j, i, k, *p: spec.index_map(i, j, k, *p))

        a_spec, b_spec = swapped(a_spec), swapped(b_spec)
        extras = [(e, swapped(s)) for e, s in extras]
        outs = [(o, swapped(s)) for o, s in outs]

    def body(*refs):
        refs = refs[n_pre:]
        a_ref, b_ref = refs[0], refs[1]
        ex = refs[2:2 + n_ex]
        out = refs[2 + n_ex + n_dep:2 + n_ex + n_dep + n_out]
        acc = refs[-1] if nk > 1 else None
        i = pl.program_id(1 if j_outer else 0)
        k = pl.program_id(2)
        if nk == 1:
            epilogue(dot(a_ref[...].astype(CDT), b_ref[...].astype(CDT)), ex, out, i)
            return

        @pl.when(k == 0)
        def _():
            acc[...] = jnp.zeros_like(acc)

        acc[...] += dot(a_ref[...].astype(CDT), b_ref[...].astype(CDT))

        @pl.when(k == nk - 1)
        def _():
            epilogue(acc[...], ex, out, i)

    grid_spec = pltpu.PrefetchScalarGridSpec(
        num_scalar_prefetch=n_pre, grid=grid,
        in_specs=[a_spec, b_spec] + [s for _, s in extras] + [pl.BlockSpec(memory_space=pl.ANY)] * n_dep,
        out_specs=[s for _, s in outs],
        scratch_shapes=[pltpu.VMEM((tm, tn), F32)] if nk > 1 else [])
    aliases = {}
    if alias_dep_to_out is not None:
        aliases = {n_pre + 2 + n_ex + alias_dep_to_out[0]: alias_dep_to_out[1]}
    res = pl.pallas_call(
        body, name=name, grid_spec=grid_spec, out_shape=[o for o, _ in outs], input_output_aliases=aliases,
        compiler_params=_params(3),
    )(*prefetch, a, b, *[e for e, _ in extras], *deps)
    return res


def _mn(tm, tn, col_off=0):
    return pl.BlockSpec((tm, tn), lambda i, j, k, *p: (i, j + col_off))


def _row(tn):
    return pl.BlockSpec((1, tn), lambda i, j, k, *p: (0, j))


def _ep_store(acc, ex, out, i):
    out[0][...] = acc.astype(out[0].dtype)


def _ep_resid_norm(acc, ex, out, i):
    x1 = ex[0][...] + acc
    out[0][...] = x1
    rstd = lax.rsqrt(jnp.mean(x1 * x1, axis=-1, keepdims=True) + EPS)
    out[1][...] = (x1 * rstd * ex[1][...]).astype(out[1].dtype)


def _ep_up(acc, ex, out, i):
    out[0][...] = acc.astype(out[0].dtype)
    r = jnp.maximum(acc, 0.0)
    out[1][...] = (r * r).astype(out[1].dtype)


def _ep_down_loss(acc, ex, out, i, inv_d):
    diff = (ex[0][...] + acc) - ex[1][...]
    dx2 = diff * inv_d
    out[0][...] = dx2
    out[1][...] = dx2.astype(out[1].dtype)

    @pl.when(i == 0)
    def _():
        out[2][...] = jnp.zeros_like(out[2])

    out[2][...] += jnp.sum(diff * diff, axis=0, keepdims=True)


def _ep_dh(acc, ex, out, i):
    h = ex[0][...].astype(F32)
    out[0][...] = (acc * (2.0 * jnp.maximum(h, 0.0))).astype(out[0].dtype)


def _ep_rms_bwd(acc, ex, out, i):
    x = ex[0][...]
    g = ex[1][...]
    rstd = lax.rsqrt(jnp.mean(x * x, axis=-1, keepdims=True) + EPS)
    xh = x * rstd
    dxh = acc * g
    dx = ex[2][...] + rstd * (dxh - xh * jnp.mean(dxh * xh, axis=-1, keepdims=True))
    out[0][...] = dx
    for copy in out[1:-1]:
        copy[...] = dx.astype(copy.dtype)
    dg = out[-1]

    @pl.when(i == 0)
    def _():
        dg[...] = jnp.zeros_like(dg)

    dg[...] += jnp.sum(acc * xh, axis=0, keepdims=True)


def _ep_gates(acc, ex, out, i):
    sa = _sigmoid(ex[0][...].astype(F32))
    sb = _sigmoid(ex[1][...].astype(F32))
    dpa = acc * sa
    dpb = acc * sb
    out[0][...] = dpa.astype(out[0].dtype)
    out[1][...] = dpb.astype(out[1].dtype)
    out[2][...] = (dpa * ex[2][...].astype(F32) * (1.0 - sa)).astype(out[2].dtype)
    out[3][...] = (dpb * ex[3][...].astype(F32) * (1.0 - sb)).astype(out[3].dtype)


def _sds(shape, dtype):
    return jax.ShapeDtypeStruct(shape, dtype)


def _rms_fwd(name, x, g, tm=512):
    S, D = x.shape

    def body(x_ref, g_ref, o_ref):
        xv = x_ref[...]
        rstd = lax.rsqrt(jnp.mean(xv * xv, axis=-1, keepdims=True) + EPS)
        o_ref[...] = (xv * rstd * g_ref[...]).astype(o_ref.dtype)

    return pl.pallas_call(
        body, name=name, grid=(S // tm,),
        in_specs=[pl.BlockSpec((tm, D), lambda i: (i, 0)), pl.BlockSpec((1, D), lambda i: (0, 0))],
        out_specs=pl.BlockSpec((tm, D), lambda i: (i, 0)),
        out_shape=_sds((S, D), CDT), compiler_params=_params(1))(x, g)


def _rm_shape(S, d, width):
    return (S, width) if d == 1 else (d, S // d, width)


def _rm_spec(tm, d, width):
    if d == 1:
        return pl.BlockSpec((tm, width), lambda i: (i, 0))
    return pl.BlockSpec((d, tm // d, width), lambda i: (0, i, 0))


def _rm_put(dst_ref, cols, buf_ref, d):
    if d == 1:
        dst_ref[:, cols] = buf_ref[...].astype(dst_ref.dtype)
        return
    m = buf_ref.shape[0] // d
    for r in range(d):
        dst_ref[r, :, cols] = buf_ref[pl.ds(r, m, stride=d), :].astype(dst_ref.dtype)


def _rm_reader(buf_ref, src_ref, d):
    if d == 1:
        return lambda s, rows: src_ref[rows, s * HD:(s + 1) * HD].astype(F32)
    m = buf_ref.shape[1] // d
    for s in range(buf_ref.shape[0]):
        for r in range(d):
            buf_ref.at[s][pl.ds(r, m, stride=d), :] = src_ref[r, :, s * HD:(s + 1) * HD].astype(F32)
    return lambda s, rows: buf_ref.at[s][rows, :]


def _qknorm_fwd(proj, gqk, tm=512):
    S = proj.shape[0]
    W = 2 * ATT_W
    dil = [d for _, d in ATT_GROUPS]

    def body(p_ref, g_ref, o0, o1, o2, buf):
        outs = (o0, o1, o2)
        for hd in range(3 * ATT_HEADS):
            which, head = hd // ATT_HEADS, hd % ATT_HEADS
            grp, slot = head // ATT_HPG, head % ATT_HPG
            cols = slice(hd * HD, (hd + 1) * HD)

            def chunk(rows, which=which, cols=cols):
                v = p_ref[rows, cols].astype(F32)
                if which < 2:
                    rstd = lax.rsqrt(jnp.mean(v * v, axis=-1, keepdims=True) + EPS)
                    v = v * rstd * g_ref[:, cols]
                buf[rows, :] = v

            chunk(slice(None))
            _rm_put(outs[grp], slice(which * GW + slot * HD, which * GW + (slot + 1) * HD), buf, dil[grp])

    return pl.pallas_call(
        body, name="qknorm_fwd", grid=(S // tm,),
        in_specs=[pl.BlockSpec((tm, 3 * ATT_W), lambda i: (i, 0)), pl.BlockSpec((1, W), lambda i: (0, 0))],
        out_specs=[_rm_spec(tm, d, 3 * GW) for d in dil],
        out_shape=[_sds(_rm_shape(S, d, 3 * GW), CDT) for d in dil],
        scratch_shapes=[pltpu.VMEM((tm, HD), F32)],
        compiler_params=_params(1))(proj, gqk)


def _qknorm_bwd(proj, gqk, dqs, dks, dvs, dproj, tm=256):
    S = proj.shape[0]
    W = 2 * ATT_W
    dil = [d for _, d in ATT_GROUPS]

    def body(p_ref, g_ref, *refs):
        ins = refs[0:9]
        o_ref, dg_ref = refs[10], refs[11]
        bufs = refs[12:21]
        i = pl.program_id(0)

        @pl.when(i == 0)
        def _():
            dg_ref[...] = jnp.zeros_like(dg_ref)

        nat = [_rm_reader(bufs[j], ins[j], dil[j % 3]) for j in range(9)]
        dq_get, dk_get, dv_get = nat[0:3], nat[3:6], nat[6:9]
        for hd in range(2 * ATT_HEADS):
            sl = slice(hd * HD, (hd + 1) * HD)
            head = hd % ATT_HEADS
            grp, slot = head // ATT_HPG, head % ATT_HPG
            get = (dq_get if hd < ATT_HEADS else dk_get)[grp]

            def chunk(rows, sl=sl, slot=slot, get=get):
                dn = get(slot, rows)
                v = p_ref[rows, sl].astype(F32)
                rstd = lax.rsqrt(jnp.mean(v * v, axis=-1, keepdims=True) + EPS)
                vh = v * rstd
                dg_ref[:, sl] += jnp.sum(dn * vh, axis=0, keepdims=True)
                dvh = dn * g_ref[:, sl]
                o_ref[rows, sl] = (rstd * (dvh - vh * jnp.mean(dvh * vh, axis=-1, keepdims=True))).astype(o_ref.dtype)

            chunk(slice(None))
        for head in range(ATT_HEADS):
            grp, slot = head // ATT_HPG, head % ATT_HPG
            o_ref[:, W + head * HD:W + (head + 1) * HD] = dv_get[grp](slot, slice(None)).astype(o_ref.dtype)

    return pl.pallas_call(
        body, name="qknorm_bwd", grid=(S // tm,),
        in_specs=[pl.BlockSpec((tm, W), lambda i: (i, 0)), pl.BlockSpec((1, W), lambda i: (0, 0))]
        + [_rm_spec(tm, d, GW) for d in dil] * 3 + [pl.BlockSpec(memory_space=pl.ANY)],
        out_specs=[pl.BlockSpec((tm, 3 * ATT_W), lambda i: (i, 0)), pl.BlockSpec((1, W), lambda i: (0, 0))],
        out_shape=[_sds(dproj.shape, dproj.dtype), _sds((1, W), F32)],
        scratch_shapes=[pltpu.VMEM((ATT_HPG, tm, HD), F32)] * 9,
        input_output_aliases={11: 0},
        compiler_params=_params(1))(proj, gqk, *dqs, *dks, *dvs, dproj)


def _att_mask(n):
    qi = lax.broadcasted_iota(jnp.int32, (BLK, 2 * BLK), 0)
    kj = lax.broadcasted_iota(jnp.int32, (BLK, 2 * BLK), 1)
    dist = BLK + qi - kj
    valid = (dist >= 0) & (dist <= BLK) & ((kj >= BLK) | (n > 0))
    return valid, dist.astype(F32)


def _att_slopes(grp):
    return [2.0 ** (-8.0 * (grp * ATT_HPG + hh + 1) / ATT_HEADS) for hh in range(ATT_HPG)]


def _att_spec(d, row_fn, col=0):
    if d == 1:
        return pl.BlockSpec((BLK, GW), lambda r, n: (row_fn(n), col))
    return pl.BlockSpec((None, BLK, GW), lambda r, n: (r, row_fn(n), col))


def _att_qkv_specs(d, nb):
    last = nb - 1

    def cur(n):
        return jnp.minimum(n, last)

    def prev(n):
        return jnp.maximum(jnp.minimum(n, last) - 1, 0)

    return [_att_spec(d, cur, 0), _att_spec(d, prev, 1), _att_spec(d, cur, 1), _att_spec(d, prev, 2),
            _att_spec(d, cur, 2)]


def _att_fwd(grp, S, qkv):
    _, d = ATT_GROUPS[grp]
    L = S // d
    nb = L // BLK
    slopes = _att_slopes(grp)
    scale = HD ** -0.5

    def body(q_ref, kp_ref, kc_ref, vp_ref, vc_ref, o_ref, l_ref, s_buf, p_buf, den_buf):
        n = pl.program_id(1)
        valid, distf = _att_mask(n)
        heads = [slice(hh * HD, (hh + 1) * HD) for hh in range(ATT_HPG)]
        for hh, sl in enumerate(heads):
            k = jnp.concatenate([kp_ref[:, sl], kc_ref[:, sl]], axis=0)
            s_buf[hh] = _dot_nt(q_ref[:, sl], k)
        for hh, sl in enumerate(heads):
            s = s_buf[hh] * scale + (-slopes[hh] * d) * distf
            s = jnp.where(valid, s, -1e30)
            m = jnp.max(s, axis=-1, keepdims=True)
            p = jnp.exp(s - m)
            den = jnp.sum(p, axis=-1, keepdims=True)
            p_buf[hh] = p.astype(CDT)
            den_buf[hh] = jnp.broadcast_to(den, (BLK, HD))
            l_ref[:, sl] = jnp.broadcast_to(m + jnp.log(den), (BLK, HD))
        for hh, sl in enumerate(heads):
            v = jnp.concatenate([vp_ref[:, sl], vc_ref[:, sl]], axis=0)
            o_ref[:, sl] = _dot_nn(p_buf[hh], v) / den_buf[hh]

    out_spec = _att_spec(d, lambda n: n)
    return pl.pallas_call(
        body, name="att_fwd_g%d" % grp, grid=(d, nb),
        in_specs=_att_qkv_specs(d, nb),
        out_specs=[out_spec, out_spec],
        out_shape=[_sds(_rm_shape(S, d, GW), F32)] * 2,
        scratch_shapes=[pltpu.VMEM((ATT_HPG, BLK, 2 * BLK), F32), pltpu.VMEM((ATT_HPG, BLK, 2 * BLK), CDT),
                        pltpu.VMEM((ATT_HPG, BLK, HD), F32)],
        compiler_params=_params(2),
    )(qkv, qkv, qkv, qkv, qkv)


def _att_bwd(grp, S, qkv, lse, do_g, c_g):
    _, d = ATT_GROUPS[grp]
    L = S // d
    nb = L // BLK
    slopes = _att_slopes(grp)
    scale = HD ** -0.5
    last = nb - 1

    def body(q_ref, kp_ref, kc_ref, vp_ref, vc_ref, l_ref, do_ref, c_ref, dq_ref, dk_ref, dv_ref, ck, cv,
             s_buf, dp_buf, p_buf, ds_buf):
        n = pl.program_id(1)

        @pl.when(n == 0)
        def _():
            ck[...] = jnp.zeros_like(ck)
            cv[...] = jnp.zeros_like(cv)

        @pl.when(n < nb)
        def _():
            valid, distf = _att_mask(n)
            heads = [slice(hh * HD, (hh + 1) * HD) for hh in range(ATT_HPG)]
            for hh, sl in enumerate(heads):
                k = jnp.concatenate([kp_ref[:, sl], kc_ref[:, sl]], axis=0)
                v = jnp.concatenate([vp_ref[:, sl], vc_ref[:, sl]], axis=0)
                s_buf[hh] = _dot_nt(q_ref[:, sl], k)
                dp_buf[hh] = _dot_nt(do_ref[:, sl], v)
            for hh, sl in enumerate(heads):
                s = s_buf[hh] * scale + (-slopes[hh] * d) * distf
                p = jnp.where(valid, jnp.exp(s - l_ref[:, sl][:, 0:1]), 0.0)
                p_buf[hh] = p.astype(CDT)
                ds_buf[hh] = (p * (dp_buf[hh] + c_ref[:, sl][:, 0:1]) * scale).astype(CDT)
            for hh, sl in enumerate(heads):
                k = jnp.concatenate([kp_ref[:, sl], kc_ref[:, sl]], axis=0)
                ds = ds_buf[hh]
                dq_ref[:, sl] = _dot_nn(ds, k)
                dk = _dot_tn(ds, q_ref[:, sl])
                dv = _dot_tn(p_buf[hh], do_ref[:, sl])
                dk_ref[:, sl] = ck[:, sl] + dk[0:BLK]
                dv_ref[:, sl] = cv[:, sl] + dv[0:BLK]
                ck[:, sl] = dk[BLK:2 * BLK]
                cv[:, sl] = dv[BLK:2 * BLK]

        @pl.when(n == nb)
        def _():
            dk_ref[...] = ck[...]
            dv_ref[...] = cv[...]

    blk = (BLK, GW)
    at_q = _att_spec(d, lambda n: jnp.minimum(n, last))
    behind = _att_spec(d, lambda n: jnp.maximum(n - 1, 0))
    return pl.pallas_call(
        body, name="att_bwd_g%d" % grp, grid=(d, nb + 1),
        in_specs=_att_qkv_specs(d, nb) + [at_q, at_q, at_q],
        out_specs=[at_q, behind, behind],
        out_shape=[_sds(_rm_shape(S, d, GW), F32)] * 3,
        scratch_shapes=[pltpu.VMEM(blk, F32), pltpu.VMEM(blk, F32),
                        pltpu.VMEM((ATT_HPG, BLK, 2 * BLK), F32), pltpu.VMEM((ATT_HPG, BLK, 2 * BLK), F32),
                        pltpu.VMEM((ATT_HPG, BLK, 2 * BLK), CDT), pltpu.VMEM((ATT_HPG, BLK, 2 * BLK), CDT)],
        compiler_params=_params(2),
    )(qkv, qkv, qkv, qkv, qkv, lse, do_g, c_g)


def _mix_alpha(l0, l1, l2):
    mx = jnp.maximum(jnp.maximum(l0, l1), l2)
    e = [jnp.exp(l0 - mx), jnp.exp(l1 - mx), jnp.exp(l2 - mx)]
    tot = e[0] + e[1] + e[2]
    return [ei / tot for ei in e]


def _mix_fwd(S, os_, ls_, tm=512):
    dil = [d for _, d in ATT_GROUPS]

    def body(*refs):
        out, bufs = refs[6], refs[7:13]
        get = [_rm_reader(bufs[j], refs[j], dil[j % 3]) for j in range(6)]
        rows = slice(None)
        for s in range(ATT_HPG):
            al = _mix_alpha(*[get[3 + g](s, rows) for g in range(3)])
            mixed = al[0] * get[0](s, rows) + al[1] * get[1](s, rows) + al[2] * get[2](s, rows)
            out[:, s * HD:(s + 1) * HD] = mixed.astype(out.dtype)

    specs = [_rm_spec(tm, d, GW) for d in dil]
    return pl.pallas_call(
        body, name="mix_fwd", grid=(S // tm,), in_specs=specs * 2, out_specs=pl.BlockSpec((tm, GW), lambda i: (i, 0)),
        out_shape=_sds((S, GW), CDT), scratch_shapes=[pltpu.VMEM((ATT_HPG, tm, HD), F32)] * 6,
        compiler_params=_params(1))(*os_, *ls_)


def _mix_bwd(S, os_, ls_, do_a, tm=512):
    dil = [d for _, d in ATT_GROUPS]

    def body(*refs):
        d_ref, outs, bufs, tmps = refs[6], refs[7:13], refs[13:19], refs[19:25]
        get = [_rm_reader(bufs[j], refs[j], dil[j % 3]) for j in range(6)]
        for s in range(ATT_HPG):
            cols = slice(s * HD, (s + 1) * HD)

            def chunk(rows, s=s, cols=cols):
                al = _mix_alpha(*[get[3 + g](s, rows) for g in range(3)])
                dv = d_ref[rows, cols]
                o_a = al[0] * get[0](s, rows) + al[1] * get[1](s, rows) + al[2] * get[2](s, rows)
                dsum = jnp.sum(dv * o_a, axis=-1, keepdims=True)
                for g in range(3):
                    tmps[g][rows, :] = al[g] * dv
                    tmps[3 + g][rows, :] = -(al[g] * dsum)

            chunk(slice(None))
            for j in range(6):
                _rm_put(outs[j], cols, tmps[j], dil[j % 3])

    specs = [_rm_spec(tm, d, GW) for d in dil]
    res = pl.pallas_call(
        body, name="mix_bwd", grid=(S // tm,), in_specs=specs * 2 + [pl.BlockSpec((tm, GW), lambda i: (i, 0))],
        out_specs=specs * 2,
        out_shape=[_sds(_rm_shape(S, d, GW), CDT) for d in dil] + [_sds(_rm_shape(S, d, GW), F32) for d in dil],
        scratch_shapes=[pltpu.VMEM((ATT_HPG, tm, HD), F32)] * 6 + [pltpu.VMEM((tm, HD), F32)] * 6,
        compiler_params=_params(1))(*os_, *ls_, do_a)
    return res[:3], res[3:]


def _ret_tables(dk):
    H, C = RET_HEADS, BLK
    log_g = jnp.log(1.0 - 2.0 ** (-5.0 - jnp.arange(H, dtype=F32)))
    idx = jnp.arange(C, dtype=F32)
    diff = idx[:, None] - idx[None, :]
    decay = jnp.where(diff >= 0, jnp.exp(log_g[:, None, None] * jnp.maximum(diff, 0.0)), 0.0)
    xi = jnp.exp(log_g[:, None] * (idx[None, :] + 1.0))
    zeta = jnp.exp(log_g[:, None] * (C - 1.0 - idx[None, :])) * (dk ** -0.5)
    g_chunk = jnp.exp(log_g * C)
    bc = lambda t: jnp.broadcast_to(t[:, :, None], (H, C, C))
    return decay, bc(xi), bc(zeta), jnp.broadcast_to(g_chunk[:, None, None], (H, 8, C))


def _gn_fwd(o, g, b):
    mu = jnp.mean(o, axis=-1, keepdims=True)
    xc = o - mu
    rstd = lax.rsqrt(jnp.mean(xc * xc, axis=-1, keepdims=True) + EPS)
    yh = xc * rstd
    return yh, rstd, yh * g + b


def _ret_specs(dk, dv, order):
    H = RET_HEADS
    qk_w, v_w = H * dk, H * dv
    off_q = 3 * ATT_W
    off_k, off_v, off_g = off_q + qk_w, off_q + 2 * qk_w, off_q + 2 * qk_w + v_w
    assert 2 * dk == dv and all(off % dv == 0 for off in (off_q, off_k, off_v, off_g))

    def col(off, j):
        return pl.BlockSpec((BLK, dv), lambda i: (order(i), off // dv + j))

    tab = pl.BlockSpec((H, BLK, BLK), lambda i: (0, 0, 0))
    return ([col(off_q, j) for j in range(H // 2)] + [col(off_k, j) for j in range(H // 2)]
            + [col(off_v, j) for j in range(H)] + [col(off_g, j) for j in range(H)]
            + [tab, tab, tab, pl.BlockSpec((H, 8, BLK), lambda i: (0, 0, 0))])


def _ret_heads(refs, dk):
    H = RET_HEADS
    q_refs, k_refs = refs[0:H // 2], refs[H // 2:H]
    v_refs, gr_refs = refs[H:2 * H], refs[2 * H:3 * H]

    def head(h):
        cols = slice((h % 2) * dk, (h % 2 + 1) * dk)
        return q_refs[h // 2][:, cols], k_refs[h // 2][:, cols], v_refs[h][...], gr_refs[h][...]

    return head, refs[3 * H:3 * H + 4]


def _ret_fwd(proj, gn_g, gn_b, dk, dv):
    S = proj.shape[0]
    N = S // BLK
    H = RET_HEADS
    kscale = dk ** -0.5
    n_in = 3 * H + 4

    def body(*refs):
        head, (dec_ref, xi_ref, zeta_ref, gc_ref) = _ret_heads(refs, dk)
        g_ref, b_ref, opre_ref, or_ref, st_ref, state, s_buf, cross_buf = refs[n_in:n_in + 8]
        n = pl.program_id(0)

        @pl.when(n == 0)
        def _():
            state[...] = jnp.zeros_like(state)

        for h in range(H):
            q, k, v, _ = head(h)
            s_buf[h] = _dot_nt(q, k)
            st = state[h]
            st_c = st.astype(CDT)
            st_ref[h] = st_c
            cross_buf[h] = _dot_nn(q, st_c)
            kz = (k.astype(F32) * zeta_ref[h][:, 0:1]).astype(CDT)
            state[h] = st * gc_ref[h][0:1, 0:1] + _dot_tn(kz, v)
        for h in range(H):
            vs = slice(h * dv, (h + 1) * dv)
            _, _, v, gr = head(h)
            s = s_buf[h] * kscale * dec_ref[h]
            o = _dot_nn(s.astype(CDT), v) + cross_buf[h] * xi_ref[h][:, 0:1]
            opre_ref[:, vs] = o
            _, _, y = _gn_fwd(o, g_ref[:, vs], b_ref[:, vs])
            gr = gr.astype(F32)
            or_ref[:, vs] = (y * (gr * _sigmoid(gr))).astype(or_ref.dtype)

    v_w = H * dv
    row = pl.BlockSpec((1, v_w), lambda i: (0, 0))
    tile = pl.BlockSpec((BLK, v_w), lambda i: (i, 0))
    return pl.pallas_call(
        body, name="ret_fwd", grid=(N,),
        in_specs=_ret_specs(dk, dv, lambda i: i) + [row, row],
        out_specs=[tile, tile, pl.BlockSpec((None, H, dk, dv), lambda i: (i, 0, 0, 0))],
        out_shape=[_sds((S, v_w), F32), _sds((S, v_w), CDT), _sds((N, H, dk, dv), CDT)],
        scratch_shapes=[pltpu.VMEM((H, dk, dv), F32), pltpu.VMEM((H, BLK, BLK), F32), pltpu.VMEM((H, BLK, dv), F32)],
        compiler_params=_params(1),
    )(*[proj] * (3 * H), *_ret_tables(dk), gn_g, gn_b)


def _ret_bwd(proj, gn_g, gn_b, o_pre, states, d_or, dga, dgb, dk, dv):
    S, in_w = proj.shape
    N = S // BLK
    H = RET_HEADS
    qk_w, v_w = H * dk, H * dv
    kscale = dk ** -0.5
    n_in = 3 * H + 4
    out_w = 2 * qk_w + 2 * v_w
    gate_w = dga.shape[1]
    col0 = 3 * ATT_W
    assert col0 + out_w + 2 * gate_w == in_w
    rev = lambda i: N - 1 - i

    def body(*refs):
        head, (dec_ref, xi_ref, zeta_ref, gc_ref) = _ret_heads(refs, dk)
        (g_ref, b_ref, opre_ref, st_ref, dor_ref, dga_ref, dgb_ref, dproj_ref, dg_ref, db_ref, dstate, stage,
         sem, do_buf, dox_buf, a_buf, g_buf, dq_buf, dk_buf, dv_buf) = refs[n_in:n_in + 20]
        i = pl.program_id(0)
        slot = i % 2
        out_ref = stage.at[slot]

        def out_copy(s, step):
            rows = pl.ds(pl.multiple_of(rev(step) * BLK, BLK), BLK)
            return pltpu.make_async_copy(stage.at[s], dproj_ref.at[rows, pl.ds(col0, in_w - col0)], sem.at[s])

        @pl.when(i >= 2)
        def _():
            out_copy(slot, i - 2).wait()

        @pl.when(i == 0)
        def _():
            dstate[...] = jnp.zeros_like(dstate)
            dg_ref[...] = jnp.zeros_like(dg_ref)
            db_ref[...] = jnp.zeros_like(db_ref)

        out_ref[:, out_w:out_w + gate_w] = dga_ref[...]
        out_ref[:, out_w + gate_w:out_w + 2 * gate_w] = dgb_ref[...]
        for h in range(H):
            vs = slice(h * dv, (h + 1) * dv)
            _, _, _, gr = head(h)
            gr = gr.astype(F32)
            sg = _sigmoid(gr)
            gain = g_ref[:, vs]
            yh, rstd, y = _gn_fwd(opre_ref[:, vs], gain, b_ref[:, vs])
            d_or_v = dor_ref[:, vs]
            dy = d_or_v * (gr * sg)
            out_ref[:, 2 * qk_w + v_w + h * dv:2 * qk_w + v_w + (h + 1) * dv] = (
                d_or_v * y * (sg * (1.0 + gr * (1.0 - sg)))).astype(out_ref.dtype)
            dg_ref[:, vs] += jnp.sum(dy * yh, axis=0, keepdims=True)
            db_ref[:, vs] += jnp.sum(dy, axis=0, keepdims=True)
            dyh = dy * gain
            do = rstd * (dyh - jnp.mean(dyh, axis=-1, keepdims=True)
                         - yh * jnp.mean(dyh * yh, axis=-1, keepdims=True))
            do_buf[h] = do.astype(CDT)
            dox_buf[h] = (do * xi_ref[h][:, 0:1]).astype(CDT)
        for h in range(H):
            q, k, v, _ = head(h)
            dox = dox_buf[h]
            a_buf[h] = _dot_nt(q, k)
            g_buf[h] = _dot_nt(do_buf[h], v)
            dsn = dstate[h]
            dsn_c = dsn.astype(CDT)
            kz = (k.astype(F32) * zeta_ref[h][:, 0:1]).astype(CDT)
            dq_buf[h] = _dot_nt(dox, st_ref[h])
            dk_buf[h] = _dot_nt(v, dsn_c)
            dv_buf[h] = _dot_nn(kz, dsn_c)
            dstate[h] = dsn * gc_ref[h][0:1, 0:1] + _dot_tn(q, dox)
        for h in range(H):
            q, k, _, _ = head(h)
            decay = dec_ref[h]
            a_c = (a_buf[h] * kscale * decay).astype(CDT)
            g_c = (g_buf[h] * decay).astype(CDT)
            dq = _dot_nn(g_c, k) * kscale + dq_buf[h]
            dkk = _dot_tn(g_c, q) * kscale + dk_buf[h] * zeta_ref[h][:, 0:1]
            dvv = _dot_tn(a_c, do_buf[h]) + dv_buf[h]
            out_ref[:, h * dk:(h + 1) * dk] = dq.astype(out_ref.dtype)
            out_ref[:, qk_w + h * dk:qk_w + (h + 1) * dk] = dkk.astype(out_ref.dtype)
            out_ref[:, 2 * qk_w + h * dv:2 * qk_w + (h + 1) * dv] = dvv.astype(out_ref.dtype)

        cp = out_copy(slot, i)
        cp.start()

        @pl.when(i == N - 1)
        def _():
            cp.wait()
            if N >= 2:
                out_copy(1 - slot, i - 1).wait()

    row = pl.BlockSpec((1, v_w), lambda i: (0, 0))
    tile = pl.BlockSpec((BLK, v_w), lambda i: (rev(i), 0))
    gate = pl.BlockSpec((BLK, gate_w), lambda i: (rev(i), 0))
    return pl.pallas_call(
        body, name="ret_bwd", grid=(N,),
        in_specs=_ret_specs(dk, dv, rev) + [row, row, tile,
                 pl.BlockSpec((None, H, dk, dv), lambda i: (rev(i), 0, 0, 0)), tile, gate, gate],
        out_specs=[pl.BlockSpec(memory_space=pl.ANY), row, row],
        out_shape=[_sds((S, in_w), CDT), _sds((1, v_w), F32), _sds((1, v_w), F32)],
        scratch_shapes=[pltpu.VMEM((H, dk, dv), F32), pltpu.VMEM((2, BLK, in_w - col0), CDT),
                        pltpu.SemaphoreType.DMA((2,)),
                        pltpu.VMEM((H, BLK, dv), CDT), pltpu.VMEM((H, BLK, dv), CDT),
                        pltpu.VMEM((H, BLK, BLK), F32), pltpu.VMEM((H, BLK, BLK), F32),
                        pltpu.VMEM((H, BLK, dk), F32), pltpu.VMEM((H, BLK, dk), F32), pltpu.VMEM((H, BLK, dv), F32)],
        compiler_params=_params(1),
    )(*[proj] * (3 * H), *_ret_tables(dk), gn_g, gn_b, o_pre, states, d_or, dga, dgb)


def _merge_fwd(o_a, o_r, wa, wb, proj, d_model, tm=1024, tn=512):
    S, in_w = proj.shape
    off_a, off_b = in_w - 2 * d_model, in_w - d_model
    assert off_a % tn == 0 and off_b % tn == 0

    def body(oa_ref, or_ref, wa_ref, wb_ref, ga_ref, gb_ref, y_ref, pa_ref, pb_ref):
        pa = _dot_nn(oa_ref[...], wa_ref[...])
        pb = _dot_nn(or_ref[...], wb_ref[...])
        y = _sigmoid(ga_ref[...].astype(F32)) * pa + _sigmoid(gb_ref[...].astype(F32)) * pb
        y_ref[...] = y.astype(y_ref.dtype)
        pa_ref[...] = pa.astype(pa_ref.dtype)
        pb_ref[...] = pb.astype(pb_ref.dtype)

    ka, kb = o_a.shape[1], o_r.shape[1]
    out = pl.BlockSpec((tm, tn), lambda i, j: (i, j))
    return pl.pallas_call(
        body, name="merge_fwd", grid=(S // tm, d_model // tn),
        in_specs=[pl.BlockSpec((tm, ka), lambda i, j: (i, 0)), pl.BlockSpec((tm, kb), lambda i, j: (i, 0)),
                  pl.BlockSpec((ka, tn), lambda i, j: (0, j)), pl.BlockSpec((kb, tn), lambda i, j: (0, j)),
                  pl.BlockSpec((tm, tn), lambda i, j: (i, off_a // tn + j)),
                  pl.BlockSpec((tm, tn), lambda i, j: (i, off_b // tn + j))],
        out_specs=[out, out, out], out_shape=[_sds((S, d_model), CDT)] * 3,
        compiler_params=_params(2))(o_a, o_r, wa, wb, proj, proj)


def _local_step(x, target, w_in_mine, chip, others, far_w_in, small, late_weights, on_grads, deps0=()):
    S, D = x.shape
    ns_in = w_in_mine.shape[1]
    in_w = N_CHIPS * ns_in
    d_ff = 4 * D
    ret_v_w = 2 * D
    dv = ret_v_w // RET_HEADS
    dk = (in_w - 3 * ATT_W - 2 * ret_v_w - 2 * D) // (2 * RET_HEADS)
    gqk = jnp.concatenate([small["q_norm_g"].reshape(1, ATT_W), small["k_norm_g"].reshape(1, ATT_W)], axis=1)
    g1, g2 = small["norm1_g"], small["norm2_g"]
    gn_g, gn_b = small["ret_gn_g"], small["ret_gn_b"]

    xn = _rms_fwd("rms1_fwd", x, g1)
    proj_sds = _sds((S, in_w), CDT)
    (proj_mine,) = _matmul(
        "in_proj_mine", "nn", xn, w_in_mine, tm=512, tn=ns_in, tk=D, prefetch=[chip],
        outs=[(proj_sds, pl.BlockSpec((512, ns_in), lambda i, j, k, c: (i, c[0])))], epilogue=_ep_store, deps=deps0)
    w_in = far_w_in(proj_mine)
    (proj,) = _matmul(
        "in_proj_far", "nn", xn, w_in, tm=512, tn=ns_in, tk=D, prefetch=[others], n_cols=(N_CHIPS - 1) * ns_in,
        b_spec=pl.BlockSpec((None, D, ns_in), lambda i, j, k, o: (o[j], 0, 0)),
        outs=[(proj_sds, pl.BlockSpec((512, ns_in), lambda i, j, k, o: (i, o[j])))], epilogue=_ep_store,
        deps=[proj_mine], alias_dep_to_out=(0, 0), j_outer=True)
    qkv = _qknorm_fwd(proj, gqk)
    att = [_att_fwd(g, S, qkv[g]) for g in range(3)]
    os_, ls_ = [a[0] for a in att], [a[1] for a in att]
    o_a = _mix_fwd(S, os_, ls_)
    o_pre, o_r, states = _ret_fwd(proj, gn_g, gn_b, dk, dv)
    w = late_weights(o_r)
    y, pa, pb = _merge_fwd(o_a, o_r, w["w_proj_a"], w["w_proj_b"], proj, D)
    x1, xn2 = _matmul("out_proj", "nn", y, w["w_out"], tm=512, tn=D, tk=D,
                      extras=[(x, _mn(512, D)), (g2, _row(D))],
                      outs=[(_sds((S, D), F32), _mn(512, D)), (_sds((S, D), CDT), _mn(512, D))],
                      epilogue=_ep_resid_norm)
    hid, act = _matmul("mlp_up", "nn", xn2, w["w_up"], tm=512, tn=2048, tk=D, j_outer=True,
                       outs=[(_sds((S, d_ff), CDT), _mn(512, 2048))] * 2, epilogue=_ep_up)
    dx2, dx2c, loss_row = _matmul(
        "mlp_down_loss", "nn", act, w["w_down"], tm=512, tn=D, tk=d_ff,
        extras=[(x1, _mn(512, D)), (target, _mn(512, D))],
        outs=[(_sds((S, D), F32), _mn(512, D)), (_sds((S, D), CDT), _mn(512, D)), (_sds((1, D), F32), _row(D))],
        epilogue=functools.partial(_ep_down_loss, inv_d=1.0 / D))
    loss = 0.5 * jnp.sum(loss_row) / D

    (dh,) = _matmul("d_hidden", "nt", dx2c, w["w_down"], tm=512, tn=2048, tk=D, j_outer=True,
                    extras=[(hid, _mn(512, 2048))], outs=[(_sds((S, d_ff), CDT), _mn(512, 2048))], epilogue=_ep_dh)
    (gw_down,) = _matmul("dw_down", "tn", act, dx2c, tm=1024, tn=D, tk=1024,
                         outs=[(_sds((d_ff, D), F32), _mn(1024, D))], epilogue=_ep_store)
    (gw_up,) = _matmul("dw_up", "tn", xn2, dh, tm=D, tn=1024, tk=1024,
                       outs=[(_sds((D, d_ff), F32), _mn(D, 1024))], epilogue=_ep_store)
    tok = on_grads({"w_down": gw_down, "w_up": gw_up})
    dx1, dx1c, dg2 = _matmul(
        "d_x1", "nt", dh, w["w_up"], tm=512, tn=D, tk=d_ff,
        extras=[(x1, _mn(512, D)), (g2, _row(D)), (dx2, _mn(512, D))],
        outs=[(_sds((S, D), F32), _mn(512, D)), (_sds((S, D), CDT), _mn(512, D)), (_sds((1, D), F32), _row(D))],
        epilogue=_ep_rms_bwd, deps=[tok])

    gt = 512
    assert (in_w - 2 * D) % gt == 0
    off_a, off_b = (in_w - 2 * D) // gt, (in_w - D) // gt
    dpa, dpb, dga, dgb = _matmul(
        "d_gates", "nt", dx1c, w["w_out"], tm=1024, tn=gt, tk=D,
        extras=[(proj, _mn(1024, gt, off_a)), (proj, _mn(1024, gt, off_b)), (pa, _mn(1024, gt)),
                (pb, _mn(1024, gt))],
        outs=[(_sds((S, D), CDT), _mn(1024, gt))] * 4, epilogue=_ep_gates)
    (gw_out,) = _matmul("dw_out", "tn", y, dx1c, tm=D, tn=D, tk=1024,
                        outs=[(_sds((D, D), F32), _mn(D, D))], epilogue=_ep_store)
    (gw_pa,) = _matmul("dw_proj_a", "tn", o_a, dpa, tm=GW, tn=D, tk=1024,
                       outs=[(_sds((GW, D), F32), _mn(GW, D))], epilogue=_ep_store)
    (gw_pb,) = _matmul("dw_proj_b", "tn", o_r, dpb, tm=1024, tn=D, tk=1024,
                       outs=[(_sds((ret_v_w, D), F32), _mn(1024, D))], epilogue=_ep_store)
    (do_a,) = _matmul("d_o_a", "nt", dpa, w["w_proj_a"], tm=1024, tn=GW, tk=D,
                      outs=[(_sds((S, GW), F32), _mn(1024, GW))], epilogue=_ep_store)
    tok = on_grads({"w_out": gw_out, "w_proj_a": gw_pa, "w_proj_b": gw_pb})
    (d_or,) = _matmul("d_o_r", "nt", dpb, w["w_proj_b"], tm=512, tn=ret_v_w, tk=D,
                      outs=[(_sds((S, ret_v_w), F32), _mn(512, ret_v_w))], epilogue=_ep_store, deps=[tok])

    dproj, dgn_g, dgn_b = _ret_bwd(proj, gn_g, gn_b, o_pre, states, d_or, dga, dgb, dk, dv)
    do_gs, c_gs = _mix_bwd(S, os_, ls_, do_a)
    datt_parts = [_att_bwd(g, S, qkv[g], ls_[g], do_gs[g], c_gs[g]) for g in range(3)]
    dproj, dgqk = _qknorm_bwd(proj, gqk, [p[0] for p in datt_parts], [p[1] for p in datt_parts],
                              [p[2] for p in datt_parts], dproj)

    (gw_in,) = _matmul(
        "dw_in", "tn", xn, dproj, tm=512, tn=ns_in, tk=1024,
        outs=[(_sds((N_CHIPS, D, ns_in), F32), pl.BlockSpec((None, 512, ns_in), lambda i, j, k: (j, i, 0)))],
        epilogue=_ep_store)
    tok = on_grads({"w_in": gw_in})
    grad_x, dg1 = _matmul(
        "d_x", "nt", dproj, w_in, tm=512, tn=D, tk=ns_in, n_cols=D,
        b_spec=pl.BlockSpec((None, D, ns_in), lambda i, j, k: (k, 0, 0)),
        extras=[(x, _mn(512, D)), (g1, _row(D)), (dx1, _mn(512, D))],
        outs=[(_sds((S, D), F32), _mn(512, D)), (_sds((1, D), F32), _row(D))],
        epilogue=_ep_rms_bwd, deps=[tok])

    smallg = {"norm1_g": dg1, "q_norm_g": dgqk[:, :ATT_W], "k_norm_g": dgqk[:, ATT_W:],
              "ret_gn_g": dgn_g, "ret_gn_b": dgn_b, "norm2_g": dg2}
    return loss, grad_x, smallg


N_CHIPS = 4
N_DEV = 8


def _place():
    x, y, c = lax.axis_index("x"), lax.axis_index("y"), lax.axis_index("c")
    return x, y, c


def _other_chips(x, y):
    out = []
    for fx, fy in ((1, 0), (0, 1), (1, 1)):
        px = 1 - x if fx else x
        py = 1 - y if fy else y
        out.append(((px, py), 2 * px + py))
    return out


SEM_SPEC = pl.BlockSpec(memory_space=pltpu.SEMAPHORE)
ANY_SPEC = pl.BlockSpec(memory_space=pl.ANY)
EFFECT = pltpu.SideEffectType.DATAFLOW_SIDE_EFFECTING


def _ici_copies(kind, srcs, lands, send, recv):
    x, y, c = _place()
    me = 2 * x + y
    out = []
    for w, (s, l) in enumerate(zip(srcs, lands)):
        for j, ((px, py), pidx) in enumerate(_other_chips(x, y)):
            if kind == "gather":
                half = s.shape[0] // 2
                rows = pl.ds(c * half, half)
                src, dst_there, dst_here = s.at[rows, :], l.at[me, rows, :], l.at[pidx, rows, :]
            else:
                src, dst_there, dst_here = s.at[pidx], l.at[me], l.at[pidx]
            out.append((src, dst_there, dst_here, send.at[3 * w + j], recv.at[3 * w + j], (px, py, c)))
    return out


def _exchange_start(name, kind, srcs, land_shapes):
    n = len(srcs)

    def body(*refs):
        src_refs, land_refs = refs[:n], refs[n:2 * n]
        send, recv = refs[2 * n], refs[2 * n + 1]
        token = refs[-1]
        for src, dst, _, ss, rs, dev in _ici_copies(kind, src_refs, land_refs, send, recv):
            pltpu.make_async_remote_copy(src_ref=src, dst_ref=dst, send_sem=ss, recv_sem=rs, device_id=dev,
                                         device_id_type=MESH).start()
        token[...] = jnp.zeros_like(token)

    thru = [pltpu.HBM(s.shape, s.dtype) for s in srcs] + [pltpu.HBM(shape, dtype) for shape, dtype in land_shapes]
    res = pl.pallas_call(
        body, name=name,
        out_shape=(pltpu.SemaphoreType.DMA((3 * n,)), pltpu.SemaphoreType.DMA((3 * n,)), *thru, _sds((8, LANES), F32)),
        in_specs=[HBM_SPEC] * (2 * n), out_specs=(SEM_SPEC, SEM_SPEC, *[HBM_SPEC] * (2 * n), VMEM_SPEC),
        input_output_aliases={i: 2 + i for i in range(2 * n)},
        compiler_params=pltpu.CompilerParams(has_side_effects=EFFECT),
    )(*[pltpu.with_memory_space_constraint(s, pltpu.HBM) for s in srcs],
      *[pltpu.with_memory_space_constraint(lax.empty(shape, dtype), pltpu.HBM) for shape, dtype in land_shapes])
    return res[0], res[1], list(res[2:2 + n]), list(res[2 + n:2 + 2 * n]), res[-1]


def _exchange_wait(name, kind, send, recv, srcs, lands, after):
    n = len(srcs)

    def body(*refs):
        src_refs, land_refs = refs[:n], refs[n:2 * n]
        send_ref, recv_ref = refs[2 * n], refs[2 * n + 1]
        for src, _, dst, ss, rs, dev in _ici_copies(kind, src_refs, land_refs, send_ref, recv_ref):
            cp = pltpu.make_async_remote_copy(src_ref=src, dst_ref=dst, send_sem=ss, recv_sem=rs, device_id=dev,
                                              device_id_type=MESH)
            cp.wait_send()
            cp.wait_recv()

    thru = [pltpu.HBM(t.shape, t.dtype) for t in list(srcs) + list(lands)]
    res = pl.pallas_call(
        body, name=name, out_shape=thru,
        in_specs=[HBM_SPEC] * (2 * n) + [SEM_SPEC, SEM_SPEC, ANY_SPEC], out_specs=[HBM_SPEC] * (2 * n),
        input_output_aliases={i: i for i in range(2 * n)},
        compiler_params=pltpu.CompilerParams(has_side_effects=EFFECT),
    )(*srcs, *lands, send, recv, after)
    return list(res[:n]), list(res[n:])


PAIR_TILE_ELEMS = 1 << 19


def _pair_fill(name, gathered, mine, core, others, chip):
    k, r, C = gathered.shape
    half = r // 2
    tr = _row_tile(half, C, PAIR_TILE_ELEMS, mult=16)
    nt = half // tr
    n_far = N_CHIPS - 1

    def body(c_ref, o_ref, chip_ref, in_ref, mine_ref, out_ref, slot, send, recv):
        j = pl.program_id(0)
        _sibling_barrier((j == 0) & (pl.program_id(1) == 0))
        b = (j * nt + pl.program_id(1)) % 2
        x, y, c = _place()
        cp = pltpu.make_async_remote_copy(src_ref=in_ref, dst_ref=slot.at[b], send_sem=send.at[b],
                                          recv_sem=recv.at[b], device_id=(x, y, 1 - c), device_id_type=MESH)

        @pl.when(j < n_far)
        def _():
            cp.start()
            cp.wait_recv()
            out_ref[...] = slot[b]
            cp.wait_send()

        @pl.when(j >= n_far)
        def _():
            out_ref[...] = mine_ref[...]

    def far(j):
        return jnp.minimum(j, n_far - 1)

    grid_spec = pltpu.PrefetchScalarGridSpec(
        num_scalar_prefetch=3, grid=(n_far + 2, nt),
        in_specs=[pl.BlockSpec((tr, C), lambda j, i, c, o, m: (
                      (2 * o[far(j)] + c[0]) * nt + jnp.where(j < n_far, i, nt - 1), 0)),
                  pl.BlockSpec((tr, C), lambda j, i, c, o, m: (jnp.where(j < n_far, 0, (j - n_far) * nt + i), 0))],
        out_specs=pl.BlockSpec((tr, C), lambda j, i, c, o, m: (
            jnp.where(j < n_far, 2 * o[far(j)] + 1 - c[0], 2 * m[0] + j - n_far) * nt + i, 0)),
        scratch_shapes=[pltpu.VMEM((2, tr, C), gathered.dtype), pltpu.SemaphoreType.DMA((2,)),
                        pltpu.SemaphoreType.DMA((2,))])
    out = pl.pallas_call(body, name=name, grid_spec=grid_spec, out_shape=_sds((k * r, C), gathered.dtype),
                         input_output_aliases={3: 0}, compiler_params=_params(2, PAIR_FILL_ID))(
                             core, others, chip, gathered.reshape(k * r, C), mine)
    return out.reshape(k, r, C)


def _pair_reduce(name, g, core):
    k, R, C = g.shape
    half = R // 2
    tr = _row_tile(half, C, PAIR_TILE_ELEMS, mult=16)
    nt = half // tr

    def body(c_ref, mine_ref, give_ref, out_ref, wire_ref, stage, slot, send, recv):
        _sibling_barrier((pl.program_id(0) == 0) & (pl.program_id(1) == 0))
        b = (pl.program_id(0) * nt + pl.program_id(1)) % 2
        x, y, c = _place()
        stage[b] = give_ref[...].astype(stage.dtype)
        cp = pltpu.make_async_remote_copy(src_ref=stage.at[b], dst_ref=slot.at[b], send_sem=send.at[b],
                                          recv_sem=recv.at[b], device_id=(x, y, 1 - c), device_id_type=MESH)
        cp.start()
        cp.wait_recv()
        tot = mine_ref[...] + slot[b].astype(F32)
        out_ref[...] = tot
        wire_ref[...] = tot.astype(wire_ref.dtype)
        cp.wait_send()

    blk = (tr, C)
    out_spec = pl.BlockSpec(blk, lambda s, i, c: (s * nt + i, 0))
    grid_spec = pltpu.PrefetchScalarGridSpec(
        num_scalar_prefetch=1, grid=(k, nt),
        in_specs=[pl.BlockSpec(blk, lambda s, i, c: ((2 * s + c[0]) * nt + i, 0)),
                  pl.BlockSpec(blk, lambda s, i, c: ((2 * s + 1 - c[0]) * nt + i, 0))],
        out_specs=[out_spec, out_spec],
        scratch_shapes=[pltpu.VMEM((2, tr, C), CDT), pltpu.VMEM((2, tr, C), CDT), pltpu.SemaphoreType.DMA((2,)),
                        pltpu.SemaphoreType.DMA((2,))])
    g2 = g.reshape(k * R, C)
    out, wire = pl.pallas_call(body, name=name, grid_spec=grid_spec,
                               out_shape=[_sds((k * half, C), F32), _sds((k * half, C), CDT)],
                               compiler_params=_params(2, PAIR_REDUCE_ID))(core, g2, g2)
    return out, wire.reshape(k, half, C)


def _all_reduce_small(v):
    r, cdim = v.shape

    def body(v_ref, o_ref, buf, send, recv):
        x, y, c = _place()
        me = 4 * x + 2 * y + c
        buf[me] = v_ref[...]
        sends = []
        for m in range(1, N_DEV):
            px = 1 - x if m & 4 else x
            py = 1 - y if m & 2 else y
            pc = 1 - c if m & 1 else c
            cp = pltpu.make_async_remote_copy(src_ref=v_ref, dst_ref=buf.at[me], send_sem=send.at[m - 1],
                                              recv_sem=recv.at[m - 1], device_id=(px, py, pc), device_id_type=MESH)
            cp.start()
            sends.append((cp, 4 * px + 2 * py + pc))
        for m, (cp, pidx) in enumerate(sends):
            pltpu.make_async_remote_copy(src_ref=v_ref, dst_ref=buf.at[pidx], send_sem=send.at[m], recv_sem=recv.at[m],
                                         device_id=(x, y, c), device_id_type=MESH).wait_recv()
        for cp, _ in sends:
            cp.wait_send()
        tot = buf[0]
        for k in range(1, N_DEV):
            tot = tot + buf[k]
        o_ref[...] = tot

    return pl.pallas_call(
        body, name="all_reduce_small", in_specs=[VMEM_SPEC], out_specs=VMEM_SPEC,
        out_shape=_sds((r, cdim), F32),
        scratch_shapes=[pltpu.VMEM((N_DEV, r, cdim), F32), pltpu.SemaphoreType.DMA((N_DEV - 1,)),
                        pltpu.SemaphoreType.DMA((N_DEV - 1,))],
    )(v)


def _row_tile(rows, cols, budget_elems=1 << 18, mult=8):
    if rows % mult:
        return rows
    t = max(mult, (budget_elems // cols) // mult * mult)
    while rows % t:
        t -= mult
    return t


def _adamw_update(w, g, m, v):
    nm = ADAM_B1 * m + (1.0 - ADAM_B1) * g
    nv = ADAM_B2 * v + (1.0 - ADAM_B2) * (g * g)
    m_hat = nm / (1.0 - ADAM_B1 ** ADAM_STEP)
    v_hat = nv / (1.0 - ADAM_B2 ** ADAM_STEP)
    return -ADAM_LR * (m_hat / (jnp.sqrt(v_hat) + ADAM_EPS) + ADAM_WD * w), nm, nv


def _adamw(name, w, g, m, v):
    R, C = w.shape
    tr = _row_tile(R, C, 1 << 17)

    def body(w_ref, g_ref, m_ref, v_ref, d_ref, nm_ref, nv_ref):
        d_ref[...], nm_ref[...], nv_ref[...] = _adamw_update(w_ref[...], g_ref[...], m_ref[...], v_ref[...])

    spec = pl.BlockSpec((tr, C), lambda i: (i, 0))
    return pl.pallas_call(body, name=name, grid=(R // tr,), in_specs=[spec] * 4, out_specs=[spec] * 3,
                          out_shape=[_sds((R, C), F32)] * 3, compiler_params=_params(1))(w, g, m, v)


def _sum_share(name, own, by_chip, chip, others, core):
    k, half, C = by_chip.shape
    tr = _row_tile(half, C, mult=16)
    nt = half // tr

    def body(chip_ref, oth_ref, c_ref, own_ref, a_ref, b_ref, cc_ref, g_out, mine, slot, send, recv):
        p = pl.program_id(1)
        _sibling_barrier((pl.program_id(0) == 0) & (p == 0))
        b = pl.program_id(0) % 2
        x, y, c = _place()
        cp = pltpu.make_async_remote_copy(src_ref=mine.at[b], dst_ref=slot.at[b], send_sem=send.at[b],
                                          recv_sem=recv.at[b], device_id=(x, y, 1 - c), device_id_type=MESH)

        @pl.when(p == 0)
        def _():
            tot = ((own_ref[...] + a_ref[...].astype(F32)) + b_ref[...].astype(F32)) + cc_ref[...].astype(F32)
            mine[b] = tot
            cp.start()
            g_out[...] = tot

        @pl.when(p == 1)
        def _():
            cp.wait_recv()
            g_out[...] = slot[b]
            cp.wait_send()

    def piece(j):
        return pl.BlockSpec((tr, C), lambda i, p, chip, oth, c: (oth[j] * nt + i, 0))

    grid_spec = pltpu.PrefetchScalarGridSpec(
        num_scalar_prefetch=3, grid=(nt, 2),
        in_specs=[pl.BlockSpec((tr, C), lambda i, p, chip, oth, c: (chip[0] * nt + i, 0)),
                  piece(0), piece(1), piece(2)],
        out_specs=pl.BlockSpec((tr, C), lambda i, p, chip, oth, c: (
            jnp.where(p == 0, c[0], 1 - c[0]) * nt + i, 0)),
        scratch_shapes=[pltpu.VMEM((2, tr, C), F32), pltpu.VMEM((2, tr, C), F32), pltpu.SemaphoreType.DMA((2,)),
                        pltpu.SemaphoreType.DMA((2,))])
    by2 = by_chip.reshape(k * half, C)
    return pl.pallas_call(body, name=name, grid_spec=grid_spec, out_shape=_sds((2 * half, C), F32),
                          compiler_params=_params(2, SUM_SHARE_ID))(chip, others, core, own, by2, by2, by2)


BIG = ("w_in", "w_proj_a", "w_proj_b", "w_out", "w_up", "w_down")
COL_SHARDED = ("w_in", "w_proj_a", "w_up")
SMALL = ("norm1_g", "q_norm_g", "k_norm_g", "ret_gn_g", "ret_gn_b", "norm2_g")
ALL_W = ("norm1_g", "w_in", "q_norm_g", "k_norm_g", "ret_gn_g", "ret_gn_b", "w_proj_a", "w_proj_b", "w_out",
         "norm2_g", "w_up", "w_down")
LANES = 128


def _to_full(name, gathered):
    k, r, c = gathered.shape
    if name in COL_SHARDED:
        return gathered.transpose(1, 0, 2).reshape(r, k * c)
    return gathered.reshape(k * r, c)


def _to_shard_major(name, full):
    if name in COL_SHARDED:
        r, c4 = full.shape
        return full.reshape(r, N_CHIPS, c4 // N_CHIPS).transpose(1, 0, 2)
    r4, c = full.shape
    return full.reshape(N_CHIPS, r4 // N_CHIPS, c)


def kernel(x, norm1_g, w_in, q_norm_g, k_norm_g, ret_gn_g, ret_gn_b, w_proj_a, w_proj_b, w_out, norm2_g, w_up, w_down, loss_target, m_norm1_g, m_w_in, m_q_norm_g, m_k_norm_g, m_ret_gn_g, m_ret_gn_b, m_w_proj_a, m_w_proj_b, m_w_out, m_norm2_g, m_w_up, m_w_down, v_norm1_g, v_w_in, v_q_norm_g, v_k_norm_g, v_ret_gn_g, v_ret_gn_b, v_w_proj_a, v_w_proj_b, v_w_out, v_norm2_g, v_w_up, v_w_down):
    weights = dict(norm1_g=norm1_g, w_in=w_in, q_norm_g=q_norm_g, k_norm_g=k_norm_g, ret_gn_g=ret_gn_g,
                   ret_gn_b=ret_gn_b, w_proj_a=w_proj_a, w_proj_b=w_proj_b, w_out=w_out, norm2_g=norm2_g,
                   w_up=w_up, w_down=w_down)
    moments_m = dict(norm1_g=m_norm1_g, w_in=m_w_in, q_norm_g=m_q_norm_g, k_norm_g=m_k_norm_g, ret_gn_g=m_ret_gn_g,
                     ret_gn_b=m_ret_gn_b, w_proj_a=m_w_proj_a, w_proj_b=m_w_proj_b, w_out=m_w_out,
                     norm2_g=m_norm2_g, w_up=m_w_up, w_down=m_w_down)
    moments_v = dict(norm1_g=v_norm1_g, w_in=v_w_in, q_norm_g=v_q_norm_g, k_norm_g=v_k_norm_g, ret_gn_g=v_ret_gn_g,
                     ret_gn_b=v_ret_gn_b, w_proj_a=v_w_proj_a, w_proj_b=v_w_proj_b, w_out=v_w_out,
                     norm2_g=v_norm2_g, w_up=v_w_up, w_down=v_w_down)

    mx, my = lax.axis_index("x"), lax.axis_index("y")
    core = lax.axis_index("c").astype(jnp.int32).reshape(1)
    chip = (2 * mx + my).astype(jnp.int32).reshape(1)
    others = jnp.stack([2 * (1 - mx) + my, 2 * mx + 1 - my, 2 * (1 - mx) + 1 - my]).astype(jnp.int32)
    shards = {n: weights[n][0].astype(CDT) for n in BIG}
    def start_gather(name, names):
        return _exchange_start(name, "gather", [shards[n] for n in names],
                               [((N_CHIPS,) + shards[n].shape, CDT) for n in names])

    i_send, i_recv, i_srcs, i_lands, i_token = start_gather("gather_w_in_start", ["w_in"])
    late = [n for n in BIG if n != "w_in"]
    l_send, l_recv, l_srcs, l_lands, l_token = start_gather("gather_late_start", late)

    def far_w_in(after):
        srcs, lands = _exchange_wait("gather_w_in_wait", "gather", i_send, i_recv, i_srcs, i_lands, after)
        return _pair_fill("pair_fill_w_in", lands[0], srcs[0], core, others, chip)

    def late_weights(after):
        srcs, lands = _exchange_wait("gather_late_wait", "gather", l_send, l_recv, l_srcs, l_lands, after)
        out = {}
        for n, mine, land in zip(late, srcs, lands):
            out[n] = _to_full(n, _pair_fill("pair_fill_%s" % n, land, mine, core, others, chip))
        return out

    pending = []

    def on_grads(group):
        names = list(group)
        red = [_pair_reduce("pair_reduce_%s" % n, g if g.ndim == 3 else _to_shard_major(n, g), core)
               for n, g in group.items()]
        wires = [wire for _, wire in red]
        send, recv, srcs, lands, token = _exchange_start(
            "scatter_start_%s" % names[0], "scatter", wires, [(wire.shape, wire.dtype) for wire in wires])
        pending.append((names, [own for own, _ in red], send, recv, srcs, lands))
        return token

    small = {n: weights[n].reshape(1, -1) for n in SMALL}

    loss, grad_x, small_g = _local_step(x[0], loss_target[0], i_srcs[0], chip, others, far_w_in, small,
                                        late_weights, on_grads, deps0=[i_token, l_token])
    loss = lax.psum(loss, ("x", "y", "c"))

    out_g, out_d, out_m, out_v = {}, {}, {}, {}
    for names, owns, send, recv, srcs, lands in pending:
        _, got = _exchange_wait("scatter_wait_%s" % names[0], "scatter", send, recv, srcs, lands, grad_x)
        for n, own, by_chip in zip(names, owns, got):
            shape = weights[n].shape
            g2 = _sum_share("sum_share_%s" % n, own, by_chip, chip, others, core)
            d, nm, nv = _adamw("adamw_%s" % n, weights[n][0], g2, moments_m[n][0], moments_v[n][0])
            out_g[n], out_d[n], out_m[n], out_v[n] = (t.reshape(shape) for t in (g2, d, nm, nv))

    packed = jnp.concatenate([small_g[n].reshape(1, -1) for n in SMALL], axis=1)
    red = _all_reduce_small(packed.reshape(-1, LANES)).reshape(1, -1)
    off = 0
    for n in SMALL:
        shape = weights[n].shape
        row = (1, weights[n].size)
        g2 = red[:, off:off + row[1]]
        off += row[1]
        d, nm, nv = _adamw("adamw_%s" % n, weights[n].reshape(row), g2, moments_m[n].reshape(row),
                           moments_v[n].reshape(row))
        out_g[n], out_d[n], out_m[n], out_v[n] = (t.reshape(shape) for t in (g2, d, nm, nv))

    return (loss, grad_x[None], *[out_g[n] for n in ALL_W], *[out_d[n] for n in ALL_W],
            *[out_m[n] for n in ALL_W], *[out_v[n] for n in ALL_W])
```

```python
import functools
import math

import jax
import jax.numpy as jnp
from jax import lax
from jax.experimental import pallas as pl
from jax.experimental.pallas import tpu as pltpu

CDT = jnp.bfloat16
F32 = jnp.float32
EPS = 1e-6

ATT_GROUPS = ((128, 1), (512, 4), (2048, 16))
ATT_HPG = 4
ATT_HEADS = 12
HD = 128
BLK = 128
ATT_W = ATT_HEADS * HD
GW = ATT_HPG * HD
RET_HEADS = 4

ADAM_LR = 0.001
ADAM_B1 = 0.9
ADAM_B2 = 0.999
ADAM_EPS = 1e-08
ADAM_WD = 0.01
ADAM_STEP = 10

VMEM_LIMIT_BYTES = 56 * 1024 * 1024
MXU_DIM = 256
MESH = pl.DeviceIdType.MESH
HBM_SPEC = pl.BlockSpec(memory_space=pltpu.HBM)
VMEM_SPEC = pl.BlockSpec(memory_space=pltpu.VMEM)


def _params(n_axes, collective_id=None):
    return pltpu.CompilerParams(dimension_semantics=("arbitrary",) * n_axes,
                                vmem_limit_bytes=VMEM_LIMIT_BYTES, collective_id=collective_id)


PAIR_FILL_ID, PAIR_REDUCE_ID, SUM_SHARE_ID = 1, 2, 3


def _sibling_barrier(first_step):
    @pl.when(first_step)
    def _():
        sem = pltpu.get_barrier_semaphore()
        x, y, c = lax.axis_index("x"), lax.axis_index("y"), lax.axis_index("c")
        pl.semaphore_signal(sem, inc=1, device_id=(x, y, 1 - c), device_id_type=pl.DeviceIdType.MESH)
        pl.semaphore_wait(sem, 1)


def _dot_nn(a, b):
    return jnp.dot(a, b, preferred_element_type=F32)


def _dot_nt(a, b):
    return lax.dot_general(a, b, (((1,), (1,)), ((), ())), preferred_element_type=F32)


def _dot_tn(a, b):
    return lax.dot_general(a, b, (((0,), (0,)), ((), ())), preferred_element_type=F32)


def _sigmoid(v):
    return 1.0 / (1.0 + jnp.exp(-v))


def _matmul(name, mode, a, b, *, tm, tn, tk, extras=(), outs, epilogue, deps=(), b_spec=None, n_cols=None,
            prefetch=(), alias_dep_to_out=None, j_outer=False):
    deps = [d for d in deps if d is not None]
    if mode == "tn":
        K, M = a.shape
    else:
        M, K = a.shape
    if b_spec is None:
        (N, K2) = b.shape if mode == "nt" else b.shape[::-1]
        assert K == K2, (name, a.shape, b.shape)
        if mode == "nt":
            b_spec = pl.BlockSpec((tn, tk), lambda i, j, k, *p: (j, k))
        else:
            b_spec = pl.BlockSpec((tk, tn), lambda i, j, k, *p: (k, j))
    else:
        N = n_cols
    assert M % tm == 0 and N % tn == 0 and K % tk == 0, (name, a.shape, b.shape)
    ni, nj, nk = M // tm, N // tn, K // tk
    if mode == "tn":
        a_spec = pl.BlockSpec((tk, tm), lambda i, j, k, *p: (k, i))
    else:
        a_spec = pl.BlockSpec((tm, tk), lambda i, j, k, *p: (i, k))
    dot = {"nn": _dot_nn, "nt": _dot_nt, "tn": _dot_tn}[mode]
    n_ex, n_out, n_dep, n_pre = len(extras), len(outs), len(deps), len(prefetch)
    grid = (ni, nj, nk)
    if j_outer:
        grid = (nj, ni, nk)

        def swapped(spec):
            return pl.BlockSpec(spec.block_shape, lambda j, i, k, *p: spec.index_map(i, j, k, *p))

        a_spec, b_spec = swapped(a_spec), swapped(b_spec)
        extras = [(e, swapped(s)) for e, s in extras]
        outs = [(o, swapped(s)) for o, s in outs]

    def body(*refs):
        refs = refs[n_pre:]
        a_ref, b_ref = refs[0], refs[1]
        ex = refs[2:2 + n_ex]
        out = refs[2 + n_ex + n_dep:2 + n_ex + n_dep + n_out]
        acc = refs[-1] if nk > 1 else None
        i = pl.program_id(1 if j_outer else 0)
        k = pl.program_id(2)
        if nk == 1:
            epilogue(dot(a_ref[...].astype(CDT), b_ref[...].astype(CDT)), ex, out, i)
            return

        @pl.when(k == 0)
        def _():
            acc[...] = jnp.zeros_like(acc)

        acc[...] += dot(a_ref[...].astype(CDT), b_ref[...].astype(CDT))

        @pl.when(k == nk - 1)
        def _():
            epilogue(acc[...], ex, out, i)

    grid_spec = pltpu.PrefetchScalarGridSpec(
        num_scalar_prefetch=n_pre, grid=grid,
        in_specs=[a_spec, b_spec] + [s for _, s in extras] + [pl.BlockSpec(memory_space=pl.ANY)] * n_dep,
        out_specs=[s for _, s in outs],
        scratch_shapes=[pltpu.VMEM((tm, tn), F32)] if nk > 1 else [])
    aliases = {}
    if alias_dep_to_out is not None:
        aliases = {n_pre + 2 + n_ex + alias_dep_to_out[0]: alias_dep_to_out[1]}
    res = pl.pallas_call(
        body, name=name, grid_spec=grid_spec, out_shape=[o for o, _ in outs], input_output_aliases=aliases,
        compiler_params=_params(3),
    )(*prefetch, a, b, *[e for e, _ in extras], *deps)
    return res


def _mn(tm, tn, col_off=0):
    return pl.BlockSpec((tm, tn), lambda i, j, k, *p: (i, j + col_off))


def _row(tn):
    return pl.BlockSpec((1, tn), lambda i, j, k, *p: (0, j))


def _ep_store(acc, ex, out, i):
    out[0][...] = acc.astype(out[0].dtype)


def _ep_resid_norm(acc, ex, out, i):
    x1 = ex[0][...] + acc
    out[0][...] = x1
    rstd = lax.rsqrt(jnp.mean(x1 * x1, axis=-1, keepdims=True) + EPS)
    out[1][...] = (x1 * rstd * ex[1][...]).astype(out[1].dtype)


def _ep_up(acc, ex, out, i):
    out[0][...] = acc.astype(out[0].dtype)
    r = jnp.maximum(acc, 0.0)
    out[1][...] = (r * r).astype(out[1].dtype)


def _ep_down_loss(acc, ex, out, i, inv_d):
    diff = (ex[0][...] + acc) - ex[1][...]
    dx2 = diff * inv_d
    out[0][...] = dx2
    out[1][...] = dx2.astype(out[1].dtype)

    @pl.when(i == 0)
    def _():
        out[2][...] = jnp.zeros_like(out[2])

    out[2][...] += jnp.sum(diff * diff, axis=0, keepdims=True)


def _ep_dh(acc, ex, out, i):
    h = ex[0][...].astype(F32)
    out[0][...] = (acc * (2.0 * jnp.maximum(h, 0.0))).astype(out[0].dtype)


def _ep_rms_bwd(acc, ex, out, i):
    x = ex[0][...]
    g = ex[1][...]
    rstd = lax.rsqrt(jnp.mean(x * x, axis=-1, keepdims=True) + EPS)
    xh = x * rstd
    dxh = acc * g
    dx = ex[2][...] + rstd * (dxh - xh * jnp.mean(dxh * xh, axis=-1, keepdims=True))
    out[0][...] = dx
    for copy in out[1:-1]:
        copy[...] = dx.astype(copy.dtype)
    dg = out[-1]

    @pl.when(i == 0)
    def _():
        dg[...] = jnp.zeros_like(dg)

    dg[...] += jnp.sum(acc * xh, axis=0, keepdims=True)


def _ep_gates(acc, ex, out, i):
    sa = _sigmoid(ex[0][...].astype(F32))
    sb = _sigmoid(ex[1][...].astype(F32))
    dpa = acc * sa
    dpb = acc * sb
    out[0][...] = dpa.astype(out[0].dtype)
    out[1][...] = dpb.astype(out[1].dtype)
    out[2][...] = (dpa * ex[2][...].astype(F32) * (1.0 - sa)).astype(out[2].dtype)
    out[3][...] = (dpb * ex[3][...].astype(F32) * (1.0 - sb)).astype(out[3].dtype)


def _sds(shape, dtype):
    return jax.ShapeDtypeStruct(shape, dtype)


def _rms_fwd(name, x, g, tm=512):
    S, D = x.shape

    def body(x_ref, g_ref, o_ref):
        xv = x_ref[...]
        rstd = lax.rsqrt(jnp.mean(xv * xv, axis=-1, keepdims=True) + EPS)
        o_ref[...] = (xv * rstd * g_ref[...]).astype(o_ref.dtype)

    return pl.pallas_call(
        body, name=name, grid=(S // tm,),
        in_specs=[pl.BlockSpec((tm, D), lambda i: (i, 0)), pl.BlockSpec((1, D), lambda i: (0, 0))],
        out_specs=pl.BlockSpec((tm, D), lambda i: (i, 0)),
        out_shape=_sds((S, D), CDT), compiler_params=_params(1))(x, g)


def _rm_shape(S, d, width):
    return (S, width) if d == 1 else (d, S // d, width)


def _rm_spec(tm, d, width):
    if d == 1:
        return pl.BlockSpec((tm, width), lambda i: (i, 0))
    return pl.BlockSpec((d, tm // d, width), lambda i: (0, i, 0))


def _rm_put(dst_ref, cols, buf_ref, d):
    if d == 1:
        dst_ref[:, cols] = buf_ref[...].astype(dst_ref.dtype)
        return
    m = buf_ref.shape[0] // d
    for r in range(d):
        dst_ref[r, :, cols] = buf_ref[pl.ds(r, m, stride=d), :].astype(dst_ref.dtype)


def _rm_reader(buf_ref, src_ref, d):
    if d == 1:
        return lambda s, rows: src_ref[rows, s * HD:(s + 1) * HD].astype(F32)
    m = buf_ref.shape[1] // d
    for s in range(buf_ref.shape[0]):
        for r in range(d):
            buf_ref.at[s][pl.ds(r, m, stride=d), :] = src_ref[r, :, s * HD:(s + 1) * HD].astype(F32)
    return lambda s, rows: buf_ref.at[s][rows, :]


def _qknorm_fwd(proj, gqk, tm=512):
    S = proj.shape[0]
    W = 2 * ATT_W
    dil = [d for _, d in ATT_GROUPS]

    def body(p_ref, g_ref, o0, o1, o2, buf):
        outs = (o0, o1, o2)
        for hd in range(3 * ATT_HEADS):
            which, head = hd // ATT_HEADS, hd % ATT_HEADS
            grp, slot = head // ATT_HPG, head % ATT_HPG
            cols = slice(hd * HD, (hd + 1) * HD)

            def chunk(rows, which=which, cols=cols):
                v = p_ref[rows, cols].astype(F32)
                if which < 2:
                    rstd = lax.rsqrt(jnp.mean(v * v, axis=-1, keepdims=True) + EPS)
                    v = v * rstd * g_ref[:, cols]
                buf[rows, :] = v

            chunk(slice(None))
            _rm_put(outs[grp], slice(which * GW + slot * HD, which * GW + (slot + 1) * HD), buf, dil[grp])

    return pl.pallas_call(
        body, name="qknorm_fwd", grid=(S // tm,),
        in_specs=[pl.BlockSpec((tm, 3 * ATT_W), lambda i: (i, 0)), pl.BlockSpec((1, W), lambda i: (0, 0))],
        out_specs=[_rm_spec(tm, d, 3 * GW) for d in dil],
        out_shape=[_sds(_rm_shape(S, d, 3 * GW), CDT) for d in dil],
        scratch_shapes=[pltpu.VMEM((tm, HD), F32)],
        compiler_params=_params(1))(proj, gqk)


def _qknorm_bwd(proj, gqk, dqs, dks, dvs, dproj, tm=256):
    S = proj.shape[0]
    W = 2 * ATT_W
    dil = [d for _, d in ATT_GROUPS]

    def body(p_ref, g_ref, *refs):
        ins = refs[0:9]
        o_ref, dg_ref = refs[10], refs[11]
        bufs = refs[12:21]
        i = pl.program_id(0)

        @pl.when(i == 0)
        def _():
            dg_ref[...] = jnp.zeros_like(dg_ref)

        nat = [_rm_reader(bufs[j], ins[j], dil[j % 3]) for j in range(9)]
        dq_get, dk_get, dv_get = nat[0:3], nat[3:6], nat[6:9]
        for hd in range(2 * ATT_HEADS):
            sl = slice(hd * HD, (hd + 1) * HD)
            head = hd % ATT_HEADS
            grp, slot = head // ATT_HPG, head % ATT_HPG
            get = (dq_get if hd < ATT_HEADS else dk_get)[grp]

            def chunk(rows, sl=sl, slot=slot, get=get):
                dn = get(slot, rows)
                v = p_ref[rows, sl].astype(F32)
                rstd = lax.rsqrt(jnp.mean(v * v, axis=-1, keepdims=True) + EPS)
                vh = v * rstd
                dg_ref[:, sl] += jnp.sum(dn * vh, axis=0, keepdims=True)
                dvh = dn * g_ref[:, sl]
                o_ref[rows, sl] = (rstd * (dvh - vh * jnp.mean(dvh * vh, axis=-1, keepdims=True))).astype(o_ref.dtype)

            chunk(slice(None))
        for head in range(ATT_HEADS):
            grp, slot = head // ATT_HPG, head % ATT_HPG
            o_ref[:, W + head * HD:W + (head + 1) * HD] = dv_get[grp](slot, slice(None)).astype(o_ref.dtype)

    return pl.pallas_call(
        body, name="qknorm_bwd", grid=(S // tm,),
        in_specs=[pl.BlockSpec((tm, W), lambda i: (i, 0)), pl.BlockSpec((1, W), lambda i: (0, 0))]
        + [_rm_spec(tm, d, GW) for d in dil] * 3 + [pl.BlockSpec(memory_space=pl.ANY)],
        out_specs=[pl.BlockSpec((tm, 3 * ATT_W), lambda i: (i, 0)), pl.BlockSpec((1, W), lambda i: (0, 0))],
        out_shape=[_sds(dproj.shape, dproj.dtype), _sds((1, W), F32)],
        scratch_shapes=[pltpu.VMEM((ATT_HPG, tm, HD), F32)] * 9,
        input_output_aliases={11: 0},
        compiler_params=_params(1))(proj, gqk, *dqs, *dks, *dvs, dproj)


def _att_mask(n):
    qi = lax.broadcasted_iota(jnp.int32, (BLK, 2 * BLK), 0)
    kj = lax.broadcasted_iota(jnp.int32, (BLK, 2 * BLK), 1)
    dist = BLK + qi - kj
    valid = (dist >= 0) & (dist <= BLK) & ((kj >= BLK) | (n > 0))
    return valid, dist.astype(F32)


def _att_slopes(grp):
    return [2.0 ** (-8.0 * (grp * ATT_HPG + hh + 1) / ATT_HEADS) for hh in range(ATT_HPG)]


def _att_spec(d, row_fn, col=0):
    if d == 1:
        return pl.BlockSpec((BLK, GW), lambda r, n: (row_fn(n), col))
    return pl.BlockSpec((None, BLK, GW), lambda r, n: (r, row_fn(n), col))


def _att_qkv_specs(d, nb):
    last = nb - 1

    def cur(n):
        return jnp.minimum(n, last)

    def prev(n):
        return jnp.maximum(jnp.minimum(n, last) - 1, 0)

    return [_att_spec(d, cur, 0), _att_spec(d, prev, 1), _att_spec(d, cur, 1), _att_spec(d, prev, 2),
            _att_spec(d, cur, 2)]


def _att_fwd(grp, S, qkv):
    _, d = ATT_GROUPS[grp]
    L = S // d
    nb = L // BLK
    slopes = _att_slopes(grp)
    scale = HD ** -0.5

    def body(q_ref, kp_ref, kc_ref, vp_ref, vc_ref, o_ref, l_ref, s_buf, p_buf, den_buf):
        n = pl.program_id(1)
        valid, distf = _att_mask(n)
        heads = [slice(hh * HD, (hh + 1) * HD) for hh in range(ATT_HPG)]
        for hh, sl in enumerate(heads):
            k = jnp.concatenate([kp_ref[:, sl], kc_ref[:, sl]], axis=0)
            s_buf[hh] = _dot_nt(q_ref[:, sl], k)
        for hh, sl in enumerate(heads):
            s = s_buf[hh] * scale + (-slopes[hh] * d) * distf
            s = jnp.where(valid, s, -1e30)
            m = jnp.max(s, axis=-1, keepdims=True)
            p = jnp.exp(s - m)
            den = jnp.sum(p, axis=-1, keepdims=True)
            p_buf[hh] = p.astype(CDT)
            den_buf[hh] = jnp.broadcast_to(den, (BLK, HD))
            l_ref[:, sl] = jnp.broadcast_to(m + jnp.log(den), (BLK, HD))
        for hh, sl in enumerate(heads):
            v = jnp.concatenate([vp_ref[:, sl], vc_ref[:, sl]], axis=0)
            o_ref[:, sl] = _dot_nn(p_buf[hh], v) / den_buf[hh]

    out_spec = _att_spec(d, lambda n: n)
    return pl.pallas_call(
        body, name="att_fwd_g%d" % grp, grid=(d, nb),
        in_specs=_att_qkv_specs(d, nb),
        out_specs=[out_spec, out_spec],
        out_shape=[_sds(_rm_shape(S, d, GW), F32)] * 2,
        scratch_shapes=[pltpu.VMEM((ATT_HPG, BLK, 2 * BLK), F32), pltpu.VMEM((ATT_HPG, BLK, 2 * BLK), CDT),
                        pltpu.VMEM((ATT_HPG, BLK, HD), F32)],
        compiler_params=_params(2),
    )(qkv, qkv, qkv, qkv, qkv)


def _att_bwd(grp, S, qkv, lse, do_g, c_g):
    _, d = ATT_GROUPS[grp]
    L = S // d
    nb = L // BLK
    slopes = _att_slopes(grp)
    scale = HD ** -0.5
    last = nb - 1

    def body(q_ref, kp_ref, kc_ref, vp_ref, vc_ref, l_ref, do_ref, c_ref, dq_ref, dk_ref, dv_ref, ck, cv,
             s_buf, dp_buf, p_buf, ds_buf):
        n = pl.program_id(1)

        @pl.when(n == 0)
        def _():
            ck[...] = jnp.zeros_like(ck)
            cv[...] = jnp.zeros_like(cv)

        @pl.when(n < nb)
        def _():
            valid, distf = _att_mask(n)
            heads = [slice(hh * HD, (hh + 1) * HD) for hh in range(ATT_HPG)]
            for hh, sl in enumerate(heads):
                k = jnp.concatenate([kp_ref[:, sl], kc_ref[:, sl]], axis=0)
                v = jnp.concatenate([vp_ref[:, sl], vc_ref[:, sl]], axis=0)
                s_buf[hh] = _dot_nt(q_ref[:, sl], k)
                dp_buf[hh] = _dot_nt(do_ref[:, sl], v)
            for hh, sl in enumerate(heads):
                s = s_buf[hh] * scale + (-slopes[hh] * d) * distf
                p = jnp.where(valid, jnp.exp(s - l_ref[:, sl][:, 0:1]), 0.0)
                p_buf[hh] = p.astype(CDT)
                ds_buf[hh] = (p * (dp_buf[hh] + c_ref[:, sl][:, 0:1]) * scale).astype(CDT)
            for hh, sl in enumerate(heads):
                k = jnp.concatenate([kp_ref[:, sl], kc_ref[:, sl]], axis=0)
                ds = ds_buf[hh]
                dq_ref[:, sl] = _dot_nn(ds, k)
                dk = _dot_tn(ds, q_ref[:, sl])
                dv = _dot_tn(p_buf[hh], do_ref[:, sl])
                dk_ref[:, sl] = ck[:, sl] + dk[0:BLK]
                dv_ref[:, sl] = cv[:, sl] + dv[0:BLK]
                ck[:, sl] = dk[BLK:2 * BLK]
                cv[:, sl] = dv[BLK:2 * BLK]

        @pl.when(n == nb)
        def _():
            dk_ref[...] = ck[...]
            dv_ref[...] = cv[...]

    blk = (BLK, GW)
    at_q = _att_spec(d, lambda n: jnp.minimum(n, last))
    behind = _att_spec(d, lambda n: jnp.maximum(n - 1, 0))
    return pl.pallas_call(
        body, name="att_bwd_g%d" % grp, grid=(d, nb + 1),
        in_specs=_att_qkv_specs(d, nb) + [at_q, at_q, at_q],
        out_specs=[at_q, behind, behind],
        out_shape=[_sds(_rm_shape(S, d, GW), F32)] * 3,
        scratch_shapes=[pltpu.VMEM(blk, F32), pltpu.VMEM(blk, F32),
                        pltpu.VMEM((ATT_HPG, BLK, 2 * BLK), F32), pltpu.VMEM((ATT_HPG, BLK, 2 * BLK), F32),
                        pltpu.VMEM((ATT_HPG, BLK, 2 * BLK), CDT), pltpu.VMEM((ATT_HPG, BLK, 2 * BLK), CDT)],
        compiler_params=_params(2),
    )(qkv, qkv, qkv, qkv, qkv, lse, do_g, c_g)


def _mix_alpha(l0, l1, l2):
    mx = jnp.maximum(jnp.maximum(l0, l1), l2)
    e = [jnp.exp(l0 - mx), jnp.exp(l1 - mx), jnp.exp(l2 - mx)]
    tot = e[0] + e[1] + e[2]
    return [ei / tot for ei in e]


def _mix_fwd(S, os_, ls_, tm=512):
    dil = [d for _, d in ATT_GROUPS]

    def body(*refs):
        out, bufs = refs[6], refs[7:13]
        get = [_rm_reader(bufs[j], refs[j], dil[j % 3]) for j in range(6)]
        rows = slice(None)
        for s in range(ATT_HPG):
            al = _mix_alpha(*[get[3 + g](s, rows) for g in range(3)])
            mixed = al[0] * get[0](s, rows) + al[1] * get[1](s, rows) + al[2] * get[2](s, rows)
            out[:, s * HD:(s + 1) * HD] = mixed.astype(out.dtype)

    specs = [_rm_spec(tm, d, GW) for d in dil]
    return pl.pallas_call(
        body, name="mix_fwd", grid=(S // tm,), in_specs=specs * 2, out_specs=pl.BlockSpec((tm, GW), lambda i: (i, 0)),
        out_shape=_sds((S, GW), CDT), scratch_shapes=[pltpu.VMEM((ATT_HPG, tm, HD), F32)] * 6,
        compiler_params=_params(1))(*os_, *ls_)


def _mix_bwd(S, os_, ls_, do_a, tm=512):
    dil = [d for _, d in ATT_GROUPS]

    def body(*refs):
        d_ref, outs, bufs, tmps = refs[6], refs[7:13], refs[13:19], refs[19:25]
        get = [_rm_reader(bufs[j], refs[j], dil[j % 3]) for j in range(6)]
        for s in range(ATT_HPG):
            cols = slice(s * HD, (s + 1) * HD)

            def chunk(rows, s=s, cols=cols):
                al = _mix_alpha(*[get[3 + g](s, rows) for g in range(3)])
                dv = d_ref[rows, cols]
                o_a = al[0] * get[0](s, rows) + al[1] * get[1](s, rows) + al[2] * get[2](s, rows)
                dsum = jnp.sum(dv * o_a, axis=-1, keepdims=True)
                for g in range(3):
                    tmps[g][rows, :] = al[g] * dv
                    tmps[3 + g][rows, :] = -(al[g] * dsum)

            chunk(slice(None))
            for j in range(6):
                _rm_put(outs[j], cols, tmps[j], dil[j % 3])

    specs = [_rm_spec(tm, d, GW) for d in dil]
    res = pl.pallas_call(
        body, name="mix_bwd", grid=(S // tm,), in_specs=specs * 2 + [pl.BlockSpec((tm, GW), lambda i: (i, 0))],
        out_specs=specs * 2,
        out_shape=[_sds(_rm_shape(S, d, GW), CDT) for d in dil] + [_sds(_rm_shape(S, d, GW), F32) for d in dil],
        scratch_shapes=[pltpu.VMEM((ATT_HPG, tm, HD), F32)] * 6 + [pltpu.VMEM((tm, HD), F32)] * 6,
        compiler_params=_params(1))(*os_, *ls_, do_a)
    return res[:3], res[3:]


def _ret_tables(dk):
    H, C = RET_HEADS, BLK
    log_g = jnp.log(1.0 - 2.0 ** (-5.0 - jnp.arange(H, dtype=F32)))
    idx = jnp.arange(C, dtype=F32)
    diff = idx[:, None] - idx[None, :]
    decay = jnp.where(diff >= 0, jnp.exp(log_g[:, None, None] * jnp.maximum(diff, 0.0)), 0.0)
    xi = jnp.exp(log_g[:, None] * (idx[None, :] + 1.0))
    zeta = jnp.exp(log_g[:, None] * (C - 1.0 - idx[None, :])) * (dk ** -0.5)
    g_chunk = jnp.exp(log_g * C)
    bc = lambda t: jnp.broadcast_to(t[:, :, None], (H, C, C))
    return decay, bc(xi), bc(zeta), jnp.broadcast_to(g_chunk[:, None, None], (H, 8, C))


def _gn_fwd(o, g, b):
    mu = jnp.mean(o, axis=-1, keepdims=True)
    xc = o - mu
    rstd = lax.rsqrt(jnp.mean(xc * xc, axis=-1, keepdims=True) + EPS)
    yh = xc * rstd
    return yh, rstd, yh * g + b


def _ret_specs(dk, dv, order):
    H = RET_HEADS
    qk_w, v_w = H * dk, H * dv
    off_q = 3 * ATT_W
    off_k, off_v, off_g = off_q + qk_w, off_q + 2 * qk_w, off_q + 2 * qk_w + v_w
    assert 2 * dk == dv and all(off % dv == 0 for off in (off_q, off_k, off_v, off_g))

    def col(off, j):
        return pl.BlockSpec((BLK, dv), lambda i: (order(i), off // dv + j))

    tab = pl.BlockSpec((H, BLK, BLK), lambda i: (0, 0, 0))
    return ([col(off_q, j) for j in range(H // 2)] + [col(off_k, j) for j in range(H // 2)]
            + [col(off_v, j) for j in range(H)] + [col(off_g, j) for j in range(H)]
            + [tab, tab, tab, pl.BlockSpec((H, 8, BLK), lambda i: (0, 0, 0))])


def _ret_heads(refs, dk):
    H = RET_HEADS
    q_refs, k_refs = refs[0:H // 2], refs[H // 2:H]
    v_refs, gr_refs = refs[H:2 * H], refs[2 * H:3 * H]

    def head(h):
        cols = slice((h % 2) * dk, (h % 2 + 1) * dk)
        return q_refs[h // 2][:, cols], k_refs[h // 2][:, cols], v_refs[h][...], gr_refs[h][...]

    return head, refs[3 * H:3 * H + 4]


def _ret_fwd(proj, gn_g, gn_b, dk, dv):
    S = proj.shape[0]
    N = S // BLK
    H = RET_HEADS
    kscale = dk ** -0.5
    n_in = 3 * H + 4

    def body(*refs):
        head, (dec_ref, xi_ref, zeta_ref, gc_ref) = _ret_heads(refs, dk)
        g_ref, b_ref, opre_ref, or_ref, st_ref, state, s_buf, cross_buf = refs[n_in:n_in + 8]
        n = pl.program_id(0)

        @pl.when(n == 0)
        def _():
            state[...] = jnp.zeros_like(state)

        for h in range(H):
            q, k, v, _ = head(h)
            s_buf[h] = _dot_nt(q, k)
            st = state[h]
            st_c = st.astype(CDT)
            st_ref[h] = st_c
            cross_buf[h] = _dot_nn(q, st_c)
            kz = (k.astype(F32) * zeta_ref[h][:, 0:1]).astype(CDT)
            state[h] = st * gc_ref[h][0:1, 0:1] + _dot_tn(kz, v)
        for h in range(H):
            vs = slice(h * dv, (h + 1) * dv)
            _, _, v, gr = head(h)
            s = s_buf[h] * kscale * dec_ref[h]
            o = _dot_nn(s.astype(CDT), v) + cross_buf[h] * xi_ref[h][:, 0:1]
            opre_ref[:, vs] = o
            _, _, y = _gn_fwd(o, g_ref[:, vs], b_ref[:, vs])
            gr = gr.astype(F32)
            or_ref[:, vs] = (y * (gr * _sigmoid(gr))).astype(or_ref.dtype)

    v_w = H * dv
    row = pl.BlockSpec((1, v_w), lambda i: (0, 0))
    tile = pl.BlockSpec((BLK, v_w), lambda i: (i, 0))
    return pl.pallas_call(
        body, name="ret_fwd", grid=(N,),
        in_specs=_ret_specs(dk, dv, lambda i: i) + [row, row],
        out_specs=[tile, tile, pl.BlockSpec((None, H, dk, dv), lambda i: (i, 0, 0, 0))],
        out_shape=[_sds((S, v_w), F32), _sds((S, v_w), CDT), _sds((N, H, dk, dv), CDT)],
        scratch_shapes=[pltpu.VMEM((H, dk, dv), F32), pltpu.VMEM((H, BLK, BLK), F32), pltpu.VMEM((H, BLK, dv), F32)],
        compiler_params=_params(1),
    )(*[proj] * (3 * H), *_ret_tables(dk), gn_g, gn_b)


def _ret_bwd(proj, gn_g, gn_b, o_pre, states, d_or, dga, dgb, dk, dv):
    S, in_w = proj.shape
    N = S // BLK
    H = RET_HEADS
    qk_w, v_w = H * dk, H * dv
    kscale = dk ** -0.5
    n_in = 3 * H + 4
    out_w = 2 * qk_w + 2 * v_w
    gate_w = dga.shape[1]
    col0 = 3 * ATT_W
    assert col0 + out_w + 2 * gate_w == in_w
    rev = lambda i: N - 1 - i

    def body(*refs):
        head, (dec_ref, xi_ref, zeta_ref, gc_ref) = _ret_heads(refs, dk)
        (g_ref, b_ref, opre_ref, st_ref, dor_ref, dga_ref, dgb_ref, dproj_ref, dg_ref, db_ref, dstate, stage,
         sem, do_buf, dox_buf, a_buf, g_buf, dq_buf, dk_buf, dv_buf) = refs[n_in:n_in + 20]
        i = pl.program_id(0)
        slot = i % 2
        out_ref = stage.at[slot]

        def out_copy(s, step):
            rows = pl.ds(pl.multiple_of(rev(step) * BLK, BLK), BLK)
            return pltpu.make_async_copy(stage.at[s], dproj_ref.at[rows, pl.ds(col0, in_w - col0)], sem.at[s])

        @pl.when(i >= 2)
        def _():
            out_copy(slot, i - 2).wait()

        @pl.when(i == 0)
        def _():
            dstate[...] = jnp.zeros_like(dstate)
            dg_ref[...] = jnp.zeros_like(dg_ref)
            db_ref[...] = jnp.zeros_like(db_ref)

        out_ref[:, out_w:out_w + gate_w] = dga_ref[...]
        out_ref[:, out_w + gate_w:out_w + 2 * gate_w] = dgb_ref[...]
        for h in range(H):
            vs = slice(h * dv, (h + 1) * dv)
            _, _, _, gr = head(h)
            gr = gr.astype(F32)
            sg = _sigmoid(gr)
            gain = g_ref[:, vs]
            yh, rstd, y = _gn_fwd(opre_ref[:, vs], gain, b_ref[:, vs])
            d_or_v = dor_ref[:, vs]
            dy = d_or_v * (gr * sg)
            out_ref[:, 2 * qk_w + v_w + h * dv:2 * qk_w + v_w + (h + 1) * dv] = (
                d_or_v * y * (sg * (1.0 + gr * (1.0 - sg)))).astype(out_ref.dtype)
            dg_ref[:, vs] += jnp.sum(dy * yh, axis=0, keepdims=True)
            db_ref[:, vs] += jnp.sum(dy, axis=0, keepdims=True)
            dyh = dy * gain
            do = rstd * (dyh - jnp.mean(dyh, axis=-1, keepdims=True)
                         - yh * jnp.mean(dyh * yh, axis=-1, keepdims=True))
            do_buf[h] = do.astype(CDT)
            dox_buf[h] = (do * xi_ref[h][:, 0:1]).astype(CDT)
        for h in range(H):
            q, k, v, _ = head(h)
            dox = dox_buf[h]
            a_buf[h] = _dot_nt(q, k)
            g_buf[h] = _dot_nt(do_buf[h], v)
            dsn = dstate[h]
            dsn_c = dsn.astype(CDT)
            kz = (k.astype(F32) * zeta_ref[h][:, 0:1]).astype(CDT)
            dq_buf[h] = _dot_nt(dox, st_ref[h])
            dk_buf[h] = _dot_nt(v, dsn_c)
            dv_buf[h] = _dot_nn(kz, dsn_c)
            dstate[h] = dsn * gc_ref[h][0:1, 0:1] + _dot_tn(q, dox)
        for h in range(H):
            q, k, _, _ = head(h)
            decay = dec_ref[h]
            a_c = (a_buf[h] * kscale * decay).astype(CDT)
            g_c = (g_buf[h] * decay).astype(CDT)
            dq = _dot_nn(g_c, k) * kscale + dq_buf[h]
            dkk = _dot_tn(g_c, q) * kscale + dk_buf[h] * zeta_ref[h][:, 0:1]
            dvv = _dot_tn(a_c, do_buf[h]) + dv_buf[h]
            out_ref[:, h * dk:(h + 1) * dk] = dq.astype(out_ref.dtype)
            out_ref[:, qk_w + h * dk:qk_w + (h + 1) * dk] = dkk.astype(out_ref.dtype)
            out_ref[:, 2 * qk_w + h * dv:2 * qk_w + (h + 1) * dv] = dvv.astype(out_ref.dtype)

        cp = out_copy(slot, i)
        cp.start()

        @pl.when(i == N - 1)
        def _():
            cp.wait()
            if N >= 2:
                out_copy(1 - slot, i - 1).wait()

    row = pl.BlockSpec((1, v_w), lambda i: (0, 0))
    tile = pl.BlockSpec((BLK, v_w), lambda i: (rev(i), 0))
    gate = pl.BlockSpec((BLK, gate_w), lambda i: (rev(i), 0))
    return pl.pallas_call(
        body, name="ret_bwd", grid=(N,),
        in_specs=_ret_specs(dk, dv, rev) + [row, row, tile,
                 pl.BlockSpec((None, H, dk, dv), lambda i: (rev(i), 0, 0, 0)), tile, gate, gate],
        out_specs=[pl.BlockSpec(memory_space=pl.ANY), row, row],
        out_shape=[_sds((S, in_w), CDT), _sds((1, v_w), F32), _sds((1, v_w), F32)],
        scratch_shapes=[pltpu.VMEM((H, dk, dv), F32), pltpu.VMEM((2, BLK, in_w - col0), CDT),
                        pltpu.SemaphoreType.DMA((2,)),
                        pltpu.VMEM((H, BLK, dv), CDT), pltpu.VMEM((H, BLK, dv), CDT),
                        pltpu.VMEM((H, BLK, BLK), F32), pltpu.VMEM((H, BLK, BLK), F32),
                        pltpu.VMEM((H, BLK, dk), F32), pltpu.VMEM((H, BLK, dk), F32), pltpu.VMEM((H, BLK, dv), F32)],
        compiler_params=_params(1),
    )(*[proj] * (3 * H), *_ret_tables(dk), gn_g, gn_b, o_pre, states, d_or, dga, dgb)


def _merge_fwd(o_a, o_r, wa, wb, proj, d_model, tm=1024, tn=512):
    S, in_w = proj.shape
    off_a, off_b = in_w - 2 * d_model, in_w - d_model
    assert off_a % tn == 0 and off_b % tn == 0

    def body(oa_ref, or_ref, wa_ref, wb_ref, ga_ref, gb_ref, y_ref, pa_ref, pb_ref):
        pa = _dot_nn(oa_ref[...], wa_ref[...])
        pb = _dot_nn(or_ref[...], wb_ref[...])
        y = _sigmoid(ga_ref[...].astype(F32)) * pa + _sigmoid(gb_ref[...].astype(F32)) * pb
        y_ref[...] = y.astype(y_ref.dtype)
        pa_ref[...] = pa.astype(pa_ref.dtype)
        pb_ref[...] = pb.astype(pb_ref.dtype)

    ka, kb = o_a.shape[1], o_r.shape[1]
    out = pl.BlockSpec((tm, tn), lambda i, j: (i, j))
    return pl.pallas_call(
        body, name="merge_fwd", grid=(S // tm, d_model // tn),
        in_specs=[pl.BlockSpec((tm, ka), lambda i, j: (i, 0)), pl.BlockSpec((tm, kb), lambda i, j: (i, 0)),
                  pl.BlockSpec((ka, tn), lambda i, j: (0, j)), pl.BlockSpec((kb, tn), lambda i, j: (0, j)),
                  pl.BlockSpec((tm, tn), lambda i, j: (i, off_a // tn + j)),
                  pl.BlockSpec((tm, tn), lambda i, j: (i, off_b // tn + j))],
        out_specs=[out, out, out], out_shape=[_sds((S, d_model), CDT)] * 3,
        compiler_params=_params(2))(o_a, o_r, wa, wb, proj, proj)


def _local_step(x, target, w_in_mine, chip, others, far_w_in, small, late_weights, on_grads, deps0=()):
    S, D = x.shape
    ns_in = w_in_mine.shape[1]
    in_w = N_CHIPS * ns_in
    d_ff = 4 * D
    ret_v_w = 2 * D
    dv = ret_v_w // RET_HEADS
    dk = (in_w - 3 * ATT_W - 2 * ret_v_w - 2 * D) // (2 * RET_HEADS)
    gqk = jnp.concatenate([small["q_norm_g"].reshape(1, ATT_W), small["k_norm_g"].reshape(1, ATT_W)], axis=1)
    g1, g2 = small["norm1_g"], small["norm2_g"]
    gn_g, gn_b = small["ret_gn_g"], small["ret_gn_b"]

    xn = _rms_fwd("rms1_fwd", x, g1)
    proj_sds = _sds((S, in_w), CDT)
    (proj_mine,) = _matmul(
        "in_proj_mine", "nn", xn, w_in_mine, tm=512, tn=ns_in, tk=D, prefetch=[chip],
        outs=[(proj_sds, pl.BlockSpec((512, ns_in), lambda i, j, k, c: (i, c[0])))], epilogue=_ep_store, deps=deps0)
    w_in = far_w_in(proj_mine)
    (proj,) = _matmul(
        "in_proj_far", "nn", xn, w_in, tm=512, tn=ns_in, tk=D, prefetch=[others], n_cols=(N_CHIPS - 1) * ns_in,
        b_spec=pl.BlockSpec((None, D, ns_in), lambda i, j, k, o: (o[j], 0, 0)),
        outs=[(proj_sds, pl.BlockSpec((512, ns_in), lambda i, j, k, o: (i, o[j])))], epilogue=_ep_store,
        deps=[proj_mine], alias_dep_to_out=(0, 0), j_outer=True)
    qkv = _qknorm_fwd(proj, gqk)
    att = [_att_fwd(g, S, qkv[g]) for g in range(3)]
    os_, ls_ = [a[0] for a in att], [a[1] for a in att]
    o_a = _mix_fwd(S, os_, ls_)
    o_pre, o_r, states = _ret_fwd(proj, gn_g, gn_b, dk, dv)
    w = late_weights(o_r)
    y, pa, pb = _merge_fwd(o_a, o_r, w["w_proj_a"], w["w_proj_b"], proj, D)
    x1, xn2 = _matmul("out_proj", "nn", y, w["w_out"], tm=1024, tn=D, tk=D,
                      extras=[(x, _mn(1024, D)), (g2, _row(D))],
                      outs=[(_sds((S, D), F32), _mn(1024, D)), (_sds((S, D), CDT), _mn(1024, D))],
                      epilogue=_ep_resid_norm)
    hid, act = _matmul("mlp_up", "nn", xn2, w["w_up"], tm=512, tn=2048, tk=D, j_outer=True,
                       outs=[(_sds((S, d_ff), CDT), _mn(512, 2048))] * 2, epilogue=_ep_up)
    dx2, dx2c, loss_row = _matmul(
        "mlp_down_loss", "nn", act, w["w_down"], tm=512, tn=D, tk=d_ff,
        extras=[(x1, _mn(512, D)), (target, _mn(512, D))],
        outs=[(_sds((S, D), F32), _mn(512, D)), (_sds((S, D), CDT), _mn(512, D)), (_sds((1, D), F32), _row(D))],
        epilogue=functools.partial(_ep_down_loss, inv_d=1.0 / D))
    loss = 0.5 * jnp.sum(loss_row) / D

    (dh,) = _matmul("d_hidden", "nt", dx2c, w["w_down"], tm=512, tn=2048, tk=D, j_outer=True,
                    extras=[(hid, _mn(512, 2048))], outs=[(_sds((S, d_ff), CDT), _mn(512, 2048))], epilogue=_ep_dh)
    (gw_down,) = _matmul("dw_down", "tn", act, dx2c, tm=1024, tn=D, tk=1024,
                         outs=[(_sds((d_ff, D), F32), _mn(1024, D))], epilogue=_ep_store)
    (gw_up,) = _matmul("dw_up", "tn", xn2, dh, tm=D, tn=1024, tk=1024,
                       outs=[(_sds((D, d_ff), F32), _mn(D, 1024))], epilogue=_ep_store)
    tok = on_grads({"w_down": gw_down, "w_up": gw_up})
    dx1, dx1c, dg2 = _matmul(
        "d_x1", "nt", dh, w["w_up"], tm=512, tn=D, tk=d_ff,
        extras=[(x1, _mn(512, D)), (g2, _row(D)), (dx2, _mn(512, D))],
        outs=[(_sds((S, D), F32), _mn(512, D)), (_sds((S, D), CDT), _mn(512, D)), (_sds((1, D), F32), _row(D))],
        epilogue=_ep_rms_bwd, deps=[tok])

    gt = 512
    assert (in_w - 2 * D) % gt == 0
    off_a, off_b = (in_w - 2 * D) // gt, (in_w - D) // gt
    dpa, dpb, dga, dgb = _matmul(
        "d_gates", "nt", dx1c, w["w_out"], tm=1024, tn=gt, tk=D,
        extras=[(proj, _mn(1024, gt, off_a)), (proj, _mn(1024, gt, off_b)), (pa, _mn(1024, gt)),
                (pb, _mn(1024, gt))],
        outs=[(_sds((S, D), CDT), _mn(1024, gt))] * 4, epilogue=_ep_gates)
    (gw_out,) = _matmul("dw_out", "tn", y, dx1c, tm=D, tn=D, tk=1024,
                        outs=[(_sds((D, D), F32), _mn(D, D))], epilogue=_ep_store)
    (gw_pa,) = _matmul("dw_proj_a", "tn", o_a, dpa, tm=GW, tn=D, tk=1024,
                       outs=[(_sds((GW, D), F32), _mn(GW, D))], epilogue=_ep_store)
    (gw_pb,) = _matmul("dw_proj_b", "tn", o_r, dpb, tm=1024, tn=D, tk=1024,
                       outs=[(_sds((ret_v_w, D), F32), _mn(1024, D))], epilogue=_ep_store)
    (do_a,) = _matmul("d_o_a", "nt", dpa, w["w_proj_a"], tm=1024, tn=GW, tk=D,
                      outs=[(_sds((S, GW), F32), _mn(1024, GW))], epilogue=_ep_store)
    tok = on_grads({"w_out": gw_out, "w_proj_a": gw_pa, "w_proj_b": gw_pb})
    (d_or,) = _matmul("d_o_r", "nt", dpb, w["w_proj_b"], tm=512, tn=ret_v_w, tk=D,
                      outs=[(_sds((S, ret_v_w), F32), _mn(512, ret_v_w))], epilogue=_ep_store, deps=[tok])

    dproj, dgn_g, dgn_b = _ret_bwd(proj, gn_g, gn_b, o_pre, states, d_or, dga, dgb, dk, dv)
    do_gs, c_gs = _mix_bwd(S, os_, ls_, do_a)
    datt_parts = [_att_bwd(g, S, qkv[g], ls_[g], do_gs[g], c_gs[g]) for g in range(3)]
    dproj, dgqk = _qknorm_bwd(proj, gqk, [p[0] for p in datt_parts], [p[1] for p in datt_parts],
                              [p[2] for p in datt_parts], dproj)

    (gw_in,) = _matmul(
        "dw_in", "tn", xn, dproj, tm=512, tn=ns_in, tk=1024,
        outs=[(_sds((N_CHIPS, D, ns_in), F32), pl.BlockSpec((None, 512, ns_in), lambda i, j, k: (j, i, 0)))],
        epilogue=_ep_store)
    tok = on_grads({"w_in": gw_in})
    grad_x, dg1 = _matmul(
        "d_x", "nt", dproj, w_in, tm=512, tn=D, tk=ns_in, n_cols=D,
        b_spec=pl.BlockSpec((None, D, ns_in), lambda i, j, k: (k, 0, 0)),
        extras=[(x, _mn(512, D)), (g1, _row(D)), (dx1, _mn(512, D))],
        outs=[(_sds((S, D), F32), _mn(512, D)), (_sds((1, D), F32), _row(D))],
        epilogue=_ep_rms_bwd, deps=[tok])

    smallg = {"norm1_g": dg1, "q_norm_g": dgqk[:, :ATT_W], "k_norm_g": dgqk[:, ATT_W:],
              "ret_gn_g": dgn_g, "ret_gn_b": dgn_b, "norm2_g": dg2}
    return loss, grad_x, smallg


N_CHIPS = 4
N_DEV = 8


def _place():
    x, y, c = lax.axis_index("x"), lax.axis_index("y"), lax.axis_index("c")
    return x, y, c


def _other_chips(x, y):
    out = []
    for fx, fy in ((1, 0), (0, 1), (1, 1)):
        px = 1 - x if fx else x
        py = 1 - y if fy else y
        out.append(((px, py), 2 * px + py))
    return out


SEM_SPEC = pl.BlockSpec(memory_space=pltpu.SEMAPHORE)
ANY_SPEC = pl.BlockSpec(memory_space=pl.ANY)
EFFECT = pltpu.SideEffectType.DATAFLOW_SIDE_EFFECTING


def _ici_copies(kind, srcs, lands, send, recv):
    x, y, c = _place()
    me = 2 * x + y
    out = []
    for w, (s, l) in enumerate(zip(srcs, lands)):
        for j, ((px, py), pidx) in enumerate(_other_chips(x, y)):
            if kind == "gather":
                half = s.shape[0] // 2
                rows = pl.ds(c * half, half)
                src, dst_there, dst_here = s.at[rows, :], l.at[me, rows, :], l.at[pidx, rows, :]
            else:
                src, dst_there, dst_here = s.at[pidx], l.at[me], l.at[pidx]
            out.append((src, dst_there, dst_here, send.at[3 * w + j], recv.at[3 * w + j], (px, py, c)))
    return out


def _exchange_start(name, kind, srcs, land_shapes):
    n = len(srcs)

    def body(*refs):
        src_refs, land_refs = refs[:n], refs[n:2 * n]
        send, recv = refs[2 * n], refs[2 * n + 1]
        token = refs[-1]
        for src, dst, _, ss, rs, dev in _ici_copies(kind, src_refs, land_refs, send, recv):
            pltpu.make_async_remote_copy(src_ref=src, dst_ref=dst, send_sem=ss, recv_sem=rs, device_id=dev,
                                         device_id_type=MESH).start()
        token[...] = jnp.zeros_like(token)

    thru = [pltpu.HBM(s.shape, s.dtype) for s in srcs] + [pltpu.HBM(shape, dtype) for shape, dtype in land_shapes]
    res = pl.pallas_call(
        body, name=name,
        out_shape=(pltpu.SemaphoreType.DMA((3 * n,)), pltpu.SemaphoreType.DMA((3 * n,)), *thru, _sds((8, LANES), F32)),
        in_specs=[HBM_SPEC] * (2 * n), out_specs=(SEM_SPEC, SEM_SPEC, *[HBM_SPEC] * (2 * n), VMEM_SPEC),
        input_output_aliases={i: 2 + i for i in range(2 * n)},
        compiler_params=pltpu.CompilerParams(has_side_effects=EFFECT),
    )(*[pltpu.with_memory_space_constraint(s, pltpu.HBM) for s in srcs],
      *[pltpu.with_memory_space_constraint(lax.empty(shape, dtype), pltpu.HBM) for shape, dtype in land_shapes])
    return res[0], res[1], list(res[2:2 + n]), list(res[2 + n:2 + 2 * n]), res[-1]


def _exchange_wait(name, kind, send, recv, srcs, lands, after):
    n = len(srcs)

    def body(*refs):
        src_refs, land_refs = refs[:n], refs[n:2 * n]
        send_ref, recv_ref = refs[2 * n], refs[2 * n + 1]
        for src, _, dst, ss, rs, dev in _ici_copies(kind, src_refs, land_refs, send_ref, recv_ref):
            cp = pltpu.make_async_remote_copy(src_ref=src, dst_ref=dst, send_sem=ss, recv_sem=rs, device_id=dev,
                                              device_id_type=MESH)
            cp.wait_send()
            cp.wait_recv()

    thru = [pltpu.HBM(t.shape, t.dtype) for t in list(srcs) + list(lands)]
    res = pl.pallas_call(
        body, name=name, out_shape=thru,
        in_specs=[HBM_SPEC] * (2 * n) + [SEM_SPEC, SEM_SPEC, ANY_SPEC], out_specs=[HBM_SPEC] * (2 * n),
        input_output_aliases={i: i for i in range(2 * n)},
        compiler_params=pltpu.CompilerParams(has_side_effects=EFFECT),
    )(*srcs, *lands, send, recv, after)
    return list(res[:n]), list(res[n:])


PAIR_TILE_ELEMS = 1 << 19


def _pair_fill(name, gathered, mine, core, others, chip):
    k, r, C = gathered.shape
    half = r // 2
    tr = _row_tile(half, C, PAIR_TILE_ELEMS, mult=16)
    nt = half // tr
    n_far = N_CHIPS - 1

    def body(c_ref, o_ref, chip_ref, in_ref, mine_ref, out_ref, slot, send, recv):
        j = pl.program_id(0)
        _sibling_barrier((j == 0) & (pl.program_id(1) == 0))
        b = (j * nt + pl.program_id(1)) % 2
        x, y, c = _place()
        cp = pltpu.make_async_remote_copy(src_ref=in_ref, dst_ref=slot.at[b], send_sem=send.at[b],
                                          recv_sem=recv.at[b], device_id=(x, y, 1 - c), device_id_type=MESH)

        @pl.when(j < n_far)
        def _():
            cp.start()
            cp.wait_recv()
            out_ref[...] = slot[b]
            cp.wait_send()

        @pl.when(j >= n_far)
        def _():
            out_ref[...] = mine_ref[...]

    def far(j):
        return jnp.minimum(j, n_far - 1)

    grid_spec = pltpu.PrefetchScalarGridSpec(
        num_scalar_prefetch=3, grid=(n_far + 2, nt),
        in_specs=[pl.BlockSpec((tr, C), lambda j, i, c, o, m: (
                      (2 * o[far(j)] + c[0]) * nt + jnp.where(j < n_far, i, nt - 1), 0)),
                  pl.BlockSpec((tr, C), lambda j, i, c, o, m: (jnp.where(j < n_far, 0, (j - n_far) * nt + i), 0))],
        out_specs=pl.BlockSpec((tr, C), lambda j, i, c, o, m: (
            jnp.where(j < n_far, 2 * o[far(j)] + 1 - c[0], 2 * m[0] + j - n_far) * nt + i, 0)),
        scratch_shapes=[pltpu.VMEM((2, tr, C), gathered.dtype), pltpu.SemaphoreType.DMA((2,)),
                        pltpu.SemaphoreType.DMA((2,))])
    out = pl.pallas_call(body, name=name, grid_spec=grid_spec, out_shape=_sds((k * r, C), gathered.dtype),
                         input_output_aliases={3: 0}, compiler_params=_params(2, PAIR_FILL_ID))(
                             core, others, chip, gathered.reshape(k * r, C), mine)
    return out.reshape(k, r, C)


def _pair_reduce(name, g, core):
    k, R, C = g.shape
    half = R // 2
    tr = _row_tile(half, C, PAIR_TILE_ELEMS, mult=16)
    nt = half // tr

    def body(c_ref, mine_ref, give_ref, out_ref, wire_ref, stage, slot, send, recv):
        _sibling_barrier((pl.program_id(0) == 0) & (pl.program_id(1) == 0))
        b = (pl.program_id(0) * nt + pl.program_id(1)) % 2
        x, y, c = _place()
        stage[b] = give_ref[...].astype(stage.dtype)
        cp = pltpu.make_async_remote_copy(src_ref=stage.at[b], dst_ref=slot.at[b], send_sem=send.at[b],
                                          recv_sem=recv.at[b], device_id=(x, y, 1 - c), device_id_type=MESH)
        cp.start()
        cp.wait_recv()
        tot = mine_ref[...] + slot[b].astype(F32)
        out_ref[...] = tot
        wire_ref[...] = tot.astype(wire_ref.dtype)
        cp.wait_send()

    blk = (tr, C)
    out_spec = pl.BlockSpec(blk, lambda s, i, c: (s * nt + i, 0))
    grid_spec = pltpu.PrefetchScalarGridSpec(
        num_scalar_prefetch=1, grid=(k, nt),
        in_specs=[pl.BlockSpec(blk, lambda s, i, c: ((2 * s + c[0]) * nt + i, 0)),
                  pl.BlockSpec(blk, lambda s, i, c: ((2 * s + 1 - c[0]) * nt + i, 0))],
        out_specs=[out_spec, out_spec],
        scratch_shapes=[pltpu.VMEM((2, tr, C), CDT), pltpu.VMEM((2, tr, C), CDT), pltpu.SemaphoreType.DMA((2,)),
                        pltpu.SemaphoreType.DMA((2,))])
    g2 = g.reshape(k * R, C)
    out, wire = pl.pallas_call(body, name=name, grid_spec=grid_spec,
                               out_shape=[_sds((k * half, C), F32), _sds((k * half, C), CDT)],
                               compiler_params=_params(2, PAIR_REDUCE_ID))(core, g2, g2)
    return out, wire.reshape(k, half, C)


def _all_reduce_small(v):
    r, cdim = v.shape

    def body(v_ref, o_ref, buf, send, recv):
        x, y, c = _place()
        me = 4 * x + 2 * y + c
        buf[me] = v_ref[...]
        sends = []
        for m in range(1, N_DEV):
            px = 1 - x if m & 4 else x
            py = 1 - y if m & 2 else y
            pc = 1 - c if m & 1 else c
            cp = pltpu.make_async_remote_copy(src_ref=v_ref, dst_ref=buf.at[me], send_sem=send.at[m - 1],
                                              recv_sem=recv.at[m - 1], device_id=(px, py, pc), device_id_type=MESH)
            cp.start()
            sends.append((cp, 4 * px + 2 * py + pc))
        for m, (cp, pidx) in enumerate(sends):
            pltpu.make_async_remote_copy(src_ref=v_ref, dst_ref=buf.at[pidx], send_sem=send.at[m], recv_sem=recv.at[m],
                                         device_id=(x, y, c), device_id_type=MESH).wait_recv()
        for cp, _ in sends:
            cp.wait_send()
        tot = buf[0]
        for k in range(1, N_DEV):
            tot = tot + buf[k]
        o_ref[...] = tot

    return pl.pallas_call(
        body, name="all_reduce_small", in_specs=[VMEM_SPEC], out_specs=VMEM_SPEC,
        out_shape=_sds((r, cdim), F32),
        scratch_shapes=[pltpu.VMEM((N_DEV, r, cdim), F32), pltpu.SemaphoreType.DMA((N_DEV - 1,)),
                        pltpu.SemaphoreType.DMA((N_DEV - 1,))],
    )(v)


def _row_tile(rows, cols, budget_elems=1 << 18, mult=8):
    if rows % mult:
        return rows
    t = max(mult, (budget_elems // cols) // mult * mult)
    while rows % t:
        t -= mult
    return t


def _adamw_update(w, g, m, v):
    nm = ADAM_B1 * m + (1.0 - ADAM_B1) * g
    nv = ADAM_B2 * v + (1.0 - ADAM_B2) * (g * g)
    m_hat = nm / (1.0 - ADAM_B1 ** ADAM_STEP)
    v_hat = nv / (1.0 - ADAM_B2 ** ADAM_STEP)
    return -ADAM_LR * (m_hat / (jnp.sqrt(v_hat) + ADAM_EPS) + ADAM_WD * w), nm, nv


def _adamw(name, w, g, m, v):
    R, C = w.shape
    tr = _row_tile(R, C, 1 << 18)

    def body(w_ref, g_ref, m_ref, v_ref, d_ref, nm_ref, nv_ref):
        d_ref[...], nm_ref[...], nv_ref[...] = _adamw_update(w_ref[...], g_ref[...], m_ref[...], v_ref[...])

    spec = pl.BlockSpec((tr, C), lambda i: (i, 0))
    return pl.pallas_call(body, name=name, grid=(R // tr,), in_specs=[spec] * 4, out_specs=[spec] * 3,
                          out_shape=[_sds((R, C), F32)] * 3, compiler_params=_params(1))(w, g, m, v)


def _sum_share(name, own, by_chip, chip, others, core):
    k, half, C = by_chip.shape
    tr = _row_tile(half, C, mult=16)
    nt = half // tr

    def body(chip_ref, oth_ref, c_ref, own_ref, a_ref, b_ref, cc_ref, g_out, mine, slot, send, recv):
        p = pl.program_id(1)
        _sibling_barrier((pl.program_id(0) == 0) & (p == 0))
        b = pl.program_id(0) % 2
        x, y, c = _place()
        cp = pltpu.make_async_remote_copy(src_ref=mine.at[b], dst_ref=slot.at[b], send_sem=send.at[b],
                                          recv_sem=recv.at[b], device_id=(x, y, 1 - c), device_id_type=MESH)

        @pl.when(p == 0)
        def _():
            tot = ((own_ref[...] + a_ref[...].astype(F32)) + b_ref[...].astype(F32)) + cc_ref[...].astype(F32)
            mine[b] = tot
            cp.start()
            g_out[...] = tot

        @pl.when(p == 1)
        def _():
            cp.wait_recv()
            g_out[...] = slot[b]
            cp.wait_send()

    def piece(j):
        return pl.BlockSpec((tr, C), lambda i, p, chip, oth, c: (oth[j] * nt + i, 0))

    grid_spec = pltpu.PrefetchScalarGridSpec(
        num_scalar_prefetch=3, grid=(nt, 2),
        in_specs=[pl.BlockSpec((tr, C), lambda i, p, chip, oth, c: (chip[0] * nt + i, 0)),
                  piece(0), piece(1), piece(2)],
        out_specs=pl.BlockSpec((tr, C), lambda i, p, chip, oth, c: (
            jnp.where(p == 0, c[0], 1 - c[0]) * nt + i, 0)),
        scratch_shapes=[pltpu.VMEM((2, tr, C), F32), pltpu.VMEM((2, tr, C), F32), pltpu.SemaphoreType.DMA((2,)),
                        pltpu.SemaphoreType.DMA((2,))])
    by2 = by_chip.reshape(k * half, C)
    return pl.pallas_call(body, name=name, grid_spec=grid_spec, out_shape=_sds((2 * half, C), F32),
                          compiler_params=_params(2, SUM_SHARE_ID))(chip, others, core, own, by2, by2, by2)


BIG = ("w_in", "w_proj_a", "w_proj_b", "w_out", "w_up", "w_down")
COL_SHARDED = ("w_in", "w_proj_a", "w_up")
SMALL = ("norm1_g", "q_norm_g", "k_norm_g", "ret_gn_g", "ret_gn_b", "norm2_g")
ALL_W = ("norm1_g", "w_in", "q_norm_g", "k_norm_g", "ret_gn_g", "ret_gn_b", "w_proj_a", "w_proj_b", "w_out",
         "norm2_g", "w_up", "w_down")
LANES = 128


def _to_full(name, gathered):
    k, r, c = gathered.shape
    if name in COL_SHARDED:
        return gathered.transpose(1, 0, 2).reshape(r, k * c)
    return gathered.reshape(k * r, c)


def _to_shard_major(name, full):
    if name in COL_SHARDED:
        r, c4 = full.shape
        return full.reshape(r, N_CHIPS, c4 // N_CHIPS).transpose(1, 0, 2)
    r4, c = full.shape
    return full.reshape(N_CHIPS, r4 // N_CHIPS, c)


def kernel(x, norm1_g, w_in, q_norm_g, k_norm_g, ret_gn_g, ret_gn_b, w_proj_a, w_proj_b, w_out, norm2_g, w_up, w_down, loss_target, m_norm1_g, m_w_in, m_q_norm_g, m_k_norm_g, m_ret_gn_g, m_ret_gn_b, m_w_proj_a, m_w_proj_b, m_w_out, m_norm2_g, m_w_up, m_w_down, v_norm1_g, v_w_in, v_q_norm_g, v_k_norm_g, v_ret_gn_g, v_ret_gn_b, v_w_proj_a, v_w_proj_b, v_w_out, v_norm2_g, v_w_up, v_w_down):
    weights = dict(norm1_g=norm1_g, w_in=w_in, q_norm_g=q_norm_g, k_norm_g=k_norm_g, ret_gn_g=ret_gn_g,
                   ret_gn_b=ret_gn_b, w_proj_a=w_proj_a, w_proj_b=w_proj_b, w_out=w_out, norm2_g=norm2_g,
                   w_up=w_up, w_down=w_down)
    moments_m = dict(norm1_g=m_norm1_g, w_in=m_w_in, q_norm_g=m_q_norm_g, k_norm_g=m_k_norm_g, ret_gn_g=m_ret_gn_g,
                     ret_gn_b=m_ret_gn_b, w_proj_a=m_w_proj_a, w_proj_b=m_w_proj_b, w_out=m_w_out,
                     norm2_g=m_norm2_g, w_up=m_w_up, w_down=m_w_down)
    moments_v = dict(norm1_g=v_norm1_g, w_in=v_w_in, q_norm_g=v_q_norm_g, k_norm_g=v_k_norm_g, ret_gn_g=v_ret_gn_g,
                     ret_gn_b=v_ret_gn_b, w_proj_a=v_w_proj_a, w_proj_b=v_w_proj_b, w_out=v_w_out,
                     norm2_g=v_norm2_g, w_up=v_w_up, w_down=v_w_down)

    mx, my = lax.axis_index("x"), lax.axis_index("y")
    core = lax.axis_index("c").astype(jnp.int32).reshape(1)
    chip = (2 * mx + my).astype(jnp.int32).reshape(1)
    others = jnp.stack([2 * (1 - mx) + my, 2 * mx + 1 - my, 2 * (1 - mx) + 1 - my]).astype(jnp.int32)
    shards = {n: weights[n][0].astype(CDT) for n in BIG}
    def start_gather(name, names):
        return _exchange_start(name, "gather", [shards[n] for n in names],
                               [((N_CHIPS,) + shards[n].shape, CDT) for n in names])

    i_send, i_recv, i_srcs, i_lands, i_token = start_gather("gather_w_in_start", ["w_in"])
    late = [n for n in BIG if n != "w_in"]
    l_send, l_recv, l_srcs, l_lands, l_token = start_gather("gather_late_start", late)

    def far_w_in(after):
        srcs, lands = _exchange_wait("gather_w_in_wait", "gather", i_send, i_recv, i_srcs, i_lands, after)
        return _pair_fill("pair_fill_w_in", lands[0], srcs[0], core, others, chip)

    def late_weights(after):
        srcs, lands = _exchange_wait("gather_late_wait", "gather", l_send, l_recv, l_srcs, l_lands, after)
        out = {}
        for n, mine, land in zip(late, srcs, lands):
            out[n] = _to_full(n, _pair_fill("pair_fill_%s" % n, land, mine, core, others, chip))
        return out

    pending = []

    def on_grads(group):
        names = list(group)
        red = [_pair_reduce("pair_reduce_%s" % n, g if g.ndim == 3 else _to_shard_major(n, g), core)
               for n, g in group.items()]
        wires = [wire for _, wire in red]
        send, recv, srcs, lands, token = _exchange_start(
            "scatter_start_%s" % names[0], "scatter", wires, [(wire.shape, wire.dtype) for wire in wires])
        pending.append((names, [own for own, _ in red], send, recv, srcs, lands))
        return token

    small = {n: weights[n].reshape(1, -1) for n in SMALL}

    loss, grad_x, small_g = _local_step(x[0], loss_target[0], i_srcs[0], chip, others, far_w_in, small,
                                        late_weights, on_grads, deps0=[i_token, l_token])
    loss = lax.psum(loss, ("x", "y", "c"))

    out_g, out_d, out_m, out_v = {}, {}, {}, {}
    for names, owns, send, recv, srcs, lands in pending:
        _, got = _exchange_wait("scatter_wait_%s" % names[0], "scatter", send, recv, srcs, lands, grad_x)
        for n, own, by_chip in zip(names, owns, got):
            shape = weights[n].shape
            g2 = _sum_share("sum_share_%s" % n, own, by_chip, chip, others, core)
            d, nm, nv = _adamw("adamw_%s" % n, weights[n][0], g2, moments_m[n][0], moments_v[n][0])
            out_g[n], out_d[n], out_m[n], out_v[n] = (t.reshape(shape) for t in (g2, d, nm, nv))

    packed = jnp.concatenate([small_g[n].reshape(1, -1) for n in SMALL], axis=1)
    red = _all_reduce_small(packed.reshape(-1, LANES)).reshape(1, -1)
    off = 0
    for n in SMALL:
        shape = weights[n].shape
        row = (1, weights[n].size)
        g2 = red[:, off:off + row[1]]
        off += row[1]
        d, nm, nv = _adamw("adamw_%s" % n, weights[n].reshape(row), g2, moments_m[n].reshape(row),
                           moments_v[n].reshape(row))
        out_g[n], out_d[n], out_m[n], out_v[n] = (t.reshape(shape) for t in (g2, d, nm, nv))

    return (loss, grad_x[None], *[out_g[n] for n in ALL_W], *[out_d[n] for n in ALL_W],
            *[out_m[n] for n in ALL_W], *[out_v[n] for n in ALL_W])
```

```python
import functools

import jax
import jax.numpy as jnp
from jax import lax
from jax.experimental import pallas as pl
from jax.experimental.pallas import tpu as pltpu

CDT = jnp.bfloat16
F32 = jnp.float32
EPS = 1e-6

ATT_GROUPS = ((128, 1), (512, 4), (2048, 16))
ATT_HPG = 4
ATT_HEADS = 12
HD = 128
BLK = 128
ATT_W = ATT_HEADS * HD
GW = ATT_HPG * HD
RET_HEADS = 4

ADAM_LR = 0.001
ADAM_B1 = 0.9
ADAM_B2 = 0.999
ADAM_EPS = 1e-08
ADAM_WD = 0.01
ADAM_STEP = 10

VMEM_LIMIT_BYTES = 56 * 1024 * 1024
MESH = pl.DeviceIdType.MESH
HBM_SPEC = pl.BlockSpec(memory_space=pltpu.HBM)
VMEM_SPEC = pl.BlockSpec(memory_space=pltpu.VMEM)


def _params(n_axes, collective_id=None):
    return pltpu.CompilerParams(dimension_semantics=("arbitrary",) * n_axes,
                                vmem_limit_bytes=VMEM_LIMIT_BYTES, collective_id=collective_id)


PAIR_FILL_ID, PAIR_REDUCE_ID, SUM_SHARE_ID = 1, 2, 3


def _sibling_barrier(first_step):
    @pl.when(first_step)
    def _():
        sem = pltpu.get_barrier_semaphore()
        x, y, c = lax.axis_index("x"), lax.axis_index("y"), lax.axis_index("c")
        pl.semaphore_signal(sem, inc=1, device_id=(x, y, 1 - c), device_id_type=pl.DeviceIdType.MESH)
        pl.semaphore_wait(sem, 1)


def _dot_nn(a, b):
    return jnp.dot(a, b, preferred_element_type=F32)


def _dot_nt(a, b):
    return lax.dot_general(a, b, (((1,), (1,)), ((), ())), preferred_element_type=F32)


def _dot_tn(a, b):
    return lax.dot_general(a, b, (((0,), (0,)), ((), ())), preferred_element_type=F32)


def _sigmoid(v):
    return 1.0 / (1.0 + jnp.exp(-v))


def _matmul(name, mode, a, b, *, tm, tn, tk, extras=(), outs, epilogue, deps=(), b_spec=None, n_cols=None,
            prefetch=(), alias_dep_to_out=None, j_outer=False):
    deps = [d for d in deps if d is not None]
    if mode == "tn":
        K, M = a.shape
    else:
        M, K = a.shape
    if b_spec is None:
        (N, K2) = b.shape if mode == "nt" else b.shape[::-1]
        assert K == K2, (name, a.shape, b.shape)
        if mode == "nt":
            b_spec = pl.BlockSpec((tn, tk), lambda i, j, k, *p: (j, k))
        else:
            b_spec = pl.BlockSpec((tk, tn), lambda i, j, k, *p: (k, j))
    else:
        N = n_cols
    assert M % tm == 0 and N % tn == 0 and K % tk == 0, (name, a.shape, b.shape)
    ni, nj, nk = M // tm, N // tn, K // tk
    if mode == "tn":
        a_spec = pl.BlockSpec((tk, tm), lambda i, j, k, *p: (k, i))
    else:
        a_spec = pl.BlockSpec((tm, tk), lambda i, j, k, *p: (i, k))
    dot = {"nn": _dot_nn, "nt": _dot_nt, "tn": _dot_tn}[mode]
    n_ex, n_out, n_dep, n_pre = len(extras), len(outs), len(deps), len(prefetch)
    grid = (ni, nj, nk)
    if j_outer:
        grid = (nj, ni, nk)

        def swapped(spec):
            return pl.BlockSpec(spec.block_shape, lambda j, i, k, *p: spec.index_map(i, j, k, *p))

        a_spec, b_spec = swapped(a_spec), swapped(b_spec)
        extras = [(e, swapped(s)) for e, s in extras]
        outs = [(o, swapped(s)) for o, s in outs]

    def body(*refs):
        refs = refs[n_pre:]
        a_ref, b_ref = refs[0], refs[1]
        ex = refs[2:2 + n_ex]
        out = refs[2 + n_ex + n_dep:2 + n_ex + n_dep + n_out]
        acc = refs[-1] if nk > 1 else None
        i = pl.program_id(1 if j_outer else 0)
        k = pl.program_id(2)
        if nk == 1:
            epilogue(dot(a_ref[...].astype(CDT), b_ref[...].astype(CDT)), ex, out, i)
            return

        @pl.when(k == 0)
        def _():
            acc[...] = jnp.zeros_like(acc)

        acc[...] += dot(a_ref[...].astype(CDT), b_ref[...].astype(CDT))

        @pl.when(k == nk - 1)
        def _():
            epilogue(acc[...], ex, out, i)

    grid_spec = pltpu.PrefetchScalarGridSpec(
        num_scalar_prefetch=n_pre, grid=grid,
        in_specs=[a_spec, b_spec] + [s for _, s in extras] + [pl.BlockSpec(memory_space=pl.ANY)] * n_dep,
        out_specs=[s for _, s in outs],
        scratch_shapes=[pltpu.VMEM((tm, tn), F32)] if nk > 1 else [])
    aliases = {}
    if alias_dep_to_out is not None:
        aliases = {n_pre + 2 + n_ex + alias_dep_to_out[0]: alias_dep_to_out[1]}
    res = pl.pallas_call(
        body, name=name, grid_spec=grid_spec, out_shape=[o for o, _ in outs], input_output_aliases=aliases,
        compiler_params=_params(3),
    )(*prefetch, a, b, *[e for e, _ in extras], *deps)
    return res


def _mn(tm, tn, col_off=0):
    return pl.BlockSpec((tm, tn), lambda i, j, k, *p: (i, j + col_off))


def _row(tn):
    return pl.BlockSpec((1, tn), lambda i, j, k, *p: (0, j))


def _ep_store(acc, ex, out, i):
    out[0][...] = acc.astype(out[0].dtype)


def _ep_resid_norm(acc, ex, out, i):
    x1 = ex[0][...] + acc
    out[0][...] = x1
    rstd = lax.rsqrt(jnp.mean(x1 * x1, axis=-1, keepdims=True) + EPS)
    out[1][...] = (x1 * rstd * ex[1][...]).astype(out[1].dtype)


def _ep_up(acc, ex, out, i):
    out[0][...] = acc.astype(out[0].dtype)
    r = jnp.maximum(acc, 0.0)
    out[1][...] = (r * r).astype(out[1].dtype)


def _ep_down_loss(acc, ex, out, i, inv_d):
    diff = (ex[0][...] + acc) - ex[1][...]
    dx2 = diff * inv_d
    out[0][...] = dx2
    out[1][...] = dx2.astype(out[1].dtype)

    @pl.when(i == 0)
    def _():
        out[2][...] = jnp.zeros_like(out[2])

    out[2][...] += jnp.sum(diff * diff, axis=0, keepdims=True)


def _ep_dh(acc, ex, out, i):
    h = ex[0][...].astype(F32)
    out[0][...] = (acc * (2.0 * jnp.maximum(h, 0.0))).astype(out[0].dtype)


def _ep_rms_bwd(acc, ex, out, i):
    x = ex[0][...]
    g = ex[1][...]
    rstd = lax.rsqrt(jnp.mean(x * x, axis=-1, keepdims=True) + EPS)
    xh = x * rstd
    dxh = acc * g
    dx = ex[2][...] + rstd * (dxh - xh * jnp.mean(dxh * xh, axis=-1, keepdims=True))
    out[0][...] = dx
    for copy in out[1:-1]:
        copy[...] = dx.astype(copy.dtype)
    dg = out[-1]

    @pl.when(i == 0)
    def _():
        dg[...] = jnp.zeros_like(dg)

    dg[...] += jnp.sum(acc * xh, axis=0, keepdims=True)


def _ep_gates(acc, ex, out, i):
    sa = _sigmoid(ex[0][...].astype(F32))
    sb = _sigmoid(ex[1][...].astype(F32))
    dpa = acc * sa
    dpb = acc * sb
    out[0][...] = dpa.astype(out[0].dtype)
    out[1][...] = dpb.astype(out[1].dtype)
    out[2][...] = (dpa * ex[2][...].astype(F32) * (1.0 - sa)).astype(out[2].dtype)
    out[3][...] = (dpb * ex[3][...].astype(F32) * (1.0 - sb)).astype(out[3].dtype)


def _sds(shape, dtype):
    return jax.ShapeDtypeStruct(shape, dtype)


def _in_proj_mine(x, g, w_mine, chip, proj_sds, deps, tm=512):
    S, D = x.shape
    ns = w_mine.shape[1]
    deps = [d for d in deps if d is not None]

    def body(c_ref, x_ref, g_ref, w_ref, *rest):
        xn_ref, proj_ref = rest[len(deps)], rest[len(deps) + 1]
        xv = x_ref[...]
        rstd = lax.rsqrt(jnp.mean(xv * xv, axis=-1, keepdims=True) + EPS)
        xn = (xv * rstd * g_ref[...]).astype(xn_ref.dtype)
        xn_ref[...] = xn
        proj_ref[...] = _dot_nn(xn, w_ref[...]).astype(proj_ref.dtype)

    grid_spec = pltpu.PrefetchScalarGridSpec(
        num_scalar_prefetch=1, grid=(S // tm,),
        in_specs=[pl.BlockSpec((tm, D), lambda i, c: (i, 0)), pl.BlockSpec((1, D), lambda i, c: (0, 0)),
                  pl.BlockSpec((D, ns), lambda i, c: (0, 0))] + [pl.BlockSpec(memory_space=pl.ANY)] * len(deps),
        out_specs=[pl.BlockSpec((tm, D), lambda i, c: (i, 0)), pl.BlockSpec((tm, ns), lambda i, c: (i, c[0]))])
    return pl.pallas_call(body, name="in_proj_mine", grid_spec=grid_spec, out_shape=[_sds((S, D), CDT), proj_sds],
                          compiler_params=_params(1))(chip, x, g, w_mine, *deps)


def _rm_shape(S, d, width):
    return (S, width) if d == 1 else (d, S // d, width)


def _rm_spec(tm, d, width):
    if d == 1:
        return pl.BlockSpec((tm, width), lambda i: (i, 0))
    return pl.BlockSpec((d, tm // d, width), lambda i: (0, i, 0))


def _rm_put(dst_ref, cols, buf_ref, d):
    if d == 1:
        dst_ref[:, cols] = buf_ref[...].astype(dst_ref.dtype)
        return
    m = buf_ref.shape[0] // d
    for r in range(d):
        dst_ref[r, :, cols] = buf_ref[pl.ds(r, m, stride=d), :].astype(dst_ref.dtype)


def _rm_reader(buf_ref, src_ref, d):
    if d == 1:
        return lambda s, rows: src_ref[rows, s * HD:(s + 1) * HD].astype(F32)
    m = buf_ref.shape[1] // d
    for s in range(buf_ref.shape[0]):
        for r in range(d):
            buf_ref.at[s][pl.ds(r, m, stride=d), :] = src_ref[r, :, s * HD:(s + 1) * HD].astype(F32)
    return lambda s, rows: buf_ref.at[s][rows, :]


def _qknorm_fwd(proj, gqk, tm=512):
    S = proj.shape[0]
    W = 2 * ATT_W
    dil = [d for _, d in ATT_GROUPS]

    def body(p_ref, g_ref, o0, o1, o2, buf):
        outs = (o0, o1, o2)
        for hd in range(3 * ATT_HEADS):
            which, head = hd // ATT_HEADS, hd % ATT_HEADS
            grp, slot = head // ATT_HPG, head % ATT_HPG
            cols = slice(hd * HD, (hd + 1) * HD)

            def chunk(rows, which=which, cols=cols):
                v = p_ref[rows, cols].astype(F32)
                if which < 2:
                    rstd = lax.rsqrt(jnp.mean(v * v, axis=-1, keepdims=True) + EPS)
                    v = v * rstd * g_ref[:, cols]
                buf[rows, :] = v

            chunk(slice(None))
            _rm_put(outs[grp], slice(which * GW + slot * HD, which * GW + (slot + 1) * HD), buf, dil[grp])

    return pl.pallas_call(
        body, name="qknorm_fwd", grid=(S // tm,),
        in_specs=[pl.BlockSpec((tm, 3 * ATT_W), lambda i: (i, 0)), pl.BlockSpec((1, W), lambda i: (0, 0))],
        out_specs=[_rm_spec(tm, d, 3 * GW) for d in dil],
        out_shape=[_sds(_rm_shape(S, d, 3 * GW), CDT) for d in dil],
        scratch_shapes=[pltpu.VMEM((tm, HD), F32)],
        compiler_params=_params(1))(proj, gqk)


def _qknorm_bwd(proj, gqk, dqs, dks, dvs, dproj, tm=256):
    S = proj.shape[0]
    W = 2 * ATT_W
    dil = [d for _, d in ATT_GROUPS]

    def body(p_ref, g_ref, *refs):
        ins = refs[0:9]
        o_ref, dg_ref = refs[10], refs[11]
        bufs = refs[12:21]
        i = pl.program_id(0)

        @pl.when(i == 0)
        def _():
            dg_ref[...] = jnp.zeros_like(dg_ref)

        nat = [_rm_reader(bufs[j], ins[j], dil[j % 3]) for j in range(9)]
        dq_get, dk_get, dv_get = nat[0:3], nat[3:6], nat[6:9]
        for hd in range(2 * ATT_HEADS):
            sl = slice(hd * HD, (hd + 1) * HD)
            head = hd % ATT_HEADS
            grp, slot = head // ATT_HPG, head % ATT_HPG
            get = (dq_get if hd < ATT_HEADS else dk_get)[grp]

            def chunk(rows, sl=sl, slot=slot, get=get):
                dn = get(slot, rows)
                v = p_ref[rows, sl].astype(F32)
                rstd = lax.rsqrt(jnp.mean(v * v, axis=-1, keepdims=True) + EPS)
                vh = v * rstd
                dg_ref[:, sl] += jnp.sum(dn * vh, axis=0, keepdims=True)
                dvh = dn * g_ref[:, sl]
                o_ref[rows, sl] = (rstd * (dvh - vh * jnp.mean(dvh * vh, axis=-1, keepdims=True))).astype(o_ref.dtype)

            chunk(slice(None))
        for head in range(ATT_HEADS):
            grp, slot = head // ATT_HPG, head % ATT_HPG
            o_ref[:, W + head * HD:W + (head + 1) * HD] = dv_get[grp](slot, slice(None)).astype(o_ref.dtype)

    return pl.pallas_call(
        body, name="qknorm_bwd", grid=(S // tm,),
        in_specs=[pl.BlockSpec((tm, W), lambda i: (i, 0)), pl.BlockSpec((1, W), lambda i: (0, 0))]
        + [_rm_spec(tm, d, GW) for d in dil] * 3 + [pl.BlockSpec(memory_space=pl.ANY)],
        out_specs=[pl.BlockSpec((tm, 3 * ATT_W), lambda i: (i, 0)), pl.BlockSpec((1, W), lambda i: (0, 0))],
        out_shape=[_sds(dproj.shape, dproj.dtype), _sds((1, W), F32)],
        scratch_shapes=[pltpu.VMEM((ATT_HPG, tm, HD), F32)] * 9,
        input_output_aliases={11: 0},
        compiler_params=_params(1))(proj, gqk, *dqs, *dks, *dvs, dproj)


def _att_mask(n):
    qi = lax.broadcasted_iota(jnp.int32, (BLK, 2 * BLK), 0)
    kj = lax.broadcasted_iota(jnp.int32, (BLK, 2 * BLK), 1)
    dist = BLK + qi - kj
    valid = (dist >= 0) & (dist <= BLK) & ((kj >= BLK) | (n > 0))
    return valid, dist.astype(F32)


def _att_slopes(grp):
    return [2.0 ** (-8.0 * (grp * ATT_HPG + hh + 1) / ATT_HEADS) for hh in range(ATT_HPG)]


def _att_spec(d, row_fn, col=0):
    if d == 1:
        return pl.BlockSpec((BLK, GW), lambda r, n: (row_fn(n), col))
    return pl.BlockSpec((None, BLK, GW), lambda r, n: (r, row_fn(n), col))


def _att_qkv_specs(d, nb):
    last = nb - 1

    def cur(n):
        return jnp.minimum(n, last)

    def prev(n):
        return jnp.maximum(jnp.minimum(n, last) - 1, 0)

    return [_att_spec(d, cur, 0), _att_spec(d, prev, 1), _att_spec(d, cur, 1), _att_spec(d, prev, 2),
            _att_spec(d, cur, 2)]


def _att_fwd(grp, S, qkv):
    _, d = ATT_GROUPS[grp]
    L = S // d
    nb = L // BLK
    slopes = _att_slopes(grp)
    scale = HD ** -0.5

    def body(q_ref, kp_ref, kc_ref, vp_ref, vc_ref, o_ref, l_ref, s_buf, p_buf, den_buf):
        n = pl.program_id(1)
        valid, distf = _att_mask(n)
        heads = [slice(hh * HD, (hh + 1) * HD) for hh in range(ATT_HPG)]
        for hh, sl in enumerate(heads):
            k = jnp.concatenate([kp_ref[:, sl], kc_ref[:, sl]], axis=0)
            s_buf[hh] = _dot_nt(q_ref[:, sl], k)
        for hh, sl in enumerate(heads):
            s = s_buf[hh] * scale + (-slopes[hh] * d) * distf
            s = jnp.where(valid, s, -1e30)
            m = jnp.max(s, axis=-1, keepdims=True)
            p = jnp.exp(s - m)
            den = jnp.sum(p, axis=-1, keepdims=True)
            p_buf[hh] = p.astype(CDT)
            den_buf[hh] = jnp.broadcast_to(den, (BLK, HD))
            l_ref[:, sl] = jnp.broadcast_to(m + jnp.log(den), (BLK, HD))
        for hh, sl in enumerate(heads):
            v = jnp.concatenate([vp_ref[:, sl], vc_ref[:, sl]], axis=0)
            o_ref[:, sl] = _dot_nn(p_buf[hh], v) / den_buf[hh]

    out_spec = _att_spec(d, lambda n: n)
    return pl.pallas_call(
        body, name="att_fwd_g%d" % grp, grid=(d, nb),
        in_specs=_att_qkv_specs(d, nb),
        out_specs=[out_spec, out_spec],
        out_shape=[_sds(_rm_shape(S, d, GW), F32)] * 2,
        scratch_shapes=[pltpu.VMEM((ATT_HPG, BLK, 2 * BLK), F32), pltpu.VMEM((ATT_HPG, BLK, 2 * BLK), CDT),
                        pltpu.VMEM((ATT_HPG, BLK, HD), F32)],
        compiler_params=_params(2),
    )(qkv, qkv, qkv, qkv, qkv)


def _att_bwd(grp, S, qkv, lse, do_g, c_g):
    _, d = ATT_GROUPS[grp]
    L = S // d
    nb = L // BLK
    slopes = _att_slopes(grp)
    scale = HD ** -0.5
    last = nb - 1

    def body(q_ref, kp_ref, kc_ref, vp_ref, vc_ref, l_ref, do_ref, c_ref, dq_ref, dk_ref, dv_ref, ck, cv,
             s_buf, dp_buf, p_buf, ds_buf):
        n = pl.program_id(1)

        @pl.when(n == 0)
        def _():
            ck[...] = jnp.zeros_like(ck)
            cv[...] = jnp.zeros_like(cv)

        @pl.when(n < nb)
        def _():
            valid, distf = _att_mask(n)
            heads = [slice(hh * HD, (hh + 1) * HD) for hh in range(ATT_HPG)]
            for hh, sl in enumerate(heads):
                k = jnp.concatenate([kp_ref[:, sl], kc_ref[:, sl]], axis=0)
                v = jnp.concatenate([vp_ref[:, sl], vc_ref[:, sl]], axis=0)
                s_buf[hh] = _dot_nt(q_ref[:, sl], k)
                dp_buf[hh] = _dot_nt(do_ref[:, sl], v)
            for hh, sl in enumerate(heads):
                s = s_buf[hh] * scale + (-slopes[hh] * d) * distf
                p = jnp.where(valid, jnp.exp(s - l_ref[:, sl][:, 0:1]), 0.0)
                p_buf[hh] = p.astype(CDT)
                ds_buf[hh] = (p * (dp_buf[hh] + c_ref[:, sl][:, 0:1]) * scale).astype(CDT)
            for hh, sl in enumerate(heads):
                k = jnp.concatenate([kp_ref[:, sl], kc_ref[:, sl]], axis=0)
                ds = ds_buf[hh]
                dq_ref[:, sl] = _dot_nn(ds, k)
                dk = _dot_tn(ds, q_ref[:, sl])
                dv = _dot_tn(p_buf[hh], do_ref[:, sl])
                dk_ref[:, sl] = ck[:, sl] + dk[0:BLK]
                dv_ref[:, sl] = cv[:, sl] + dv[0:BLK]
                ck[:, sl] = dk[BLK:2 * BLK]
                cv[:, sl] = dv[BLK:2 * BLK]

        @pl.when(n == nb)
        def _():
            dk_ref[...] = ck[...]
            dv_ref[...] = cv[...]

    blk = (BLK, GW)
    at_q = _att_spec(d, lambda n: jnp.minimum(n, last))
    behind = _att_spec(d, lambda n: jnp.maximum(n - 1, 0))
    return pl.pallas_call(
        body, name="att_bwd_g%d" % grp, grid=(d, nb + 1),
        in_specs=_att_qkv_specs(d, nb) + [at_q, at_q, at_q],
        out_specs=[at_q, behind, behind],
        out_shape=[_sds(_rm_shape(S, d, GW), F32)] * 3,
        scratch_shapes=[pltpu.VMEM(blk, F32), pltpu.VMEM(blk, F32),
                        pltpu.VMEM((ATT_HPG, BLK, 2 * BLK), F32), pltpu.VMEM((ATT_HPG, BLK, 2 * BLK), F32),
                        pltpu.VMEM((ATT_HPG, BLK, 2 * BLK), CDT), pltpu.VMEM((ATT_HPG, BLK, 2 * BLK), CDT)],
        compiler_params=_params(2),
    )(qkv, qkv, qkv, qkv, qkv, lse, do_g, c_g)


def _mix_alpha(l0, l1, l2):
    mx = jnp.maximum(jnp.maximum(l0, l1), l2)
    e = [jnp.exp(l0 - mx), jnp.exp(l1 - mx), jnp.exp(l2 - mx)]
    tot = e[0] + e[1] + e[2]
    return [ei / tot for ei in e]


def _mix_fwd(S, os_, ls_, tm=512):
    dil = [d for _, d in ATT_GROUPS]

    def body(*refs):
        out, bufs = refs[6], refs[7:13]
        get = [_rm_reader(bufs[j], refs[j], dil[j % 3]) for j in range(6)]
        rows = slice(None)
        for s in range(ATT_HPG):
            al = _mix_alpha(*[get[3 + g](s, rows) for g in range(3)])
            mixed = al[0] * get[0](s, rows) + al[1] * get[1](s, rows) + al[2] * get[2](s, rows)
            out[:, s * HD:(s + 1) * HD] = mixed.astype(out.dtype)

    specs = [_rm_spec(tm, d, GW) for d in dil]
    return pl.pallas_call(
        body, name="mix_fwd", grid=(S // tm,), in_specs=specs * 2, out_specs=pl.BlockSpec((tm, GW), lambda i: (i, 0)),
        out_shape=_sds((S, GW), CDT), scratch_shapes=[pltpu.VMEM((ATT_HPG, tm, HD), F32)] * 6,
        compiler_params=_params(1))(*os_, *ls_)


def _mix_bwd(S, os_, ls_, do_a, tm=512):
    dil = [d for _, d in ATT_GROUPS]

    def body(*refs):
        d_ref, outs, bufs, tmps = refs[6], refs[7:13], refs[13:19], refs[19:25]
        get = [_rm_reader(bufs[j], refs[j], dil[j % 3]) for j in range(6)]
        for s in range(ATT_HPG):
            cols = slice(s * HD, (s + 1) * HD)

            def chunk(rows, s=s, cols=cols):
                al = _mix_alpha(*[get[3 + g](s, rows) for g in range(3)])
                dv = d_ref[rows, cols]
                o_a = al[0] * get[0](s, rows) + al[1] * get[1](s, rows) + al[2] * get[2](s, rows)
                dsum = jnp.sum(dv * o_a, axis=-1, keepdims=True)
                for g in range(3):
                    tmps[g][rows, :] = al[g] * dv
                    tmps[3 + g][rows, :] = -(al[g] * dsum)

            chunk(slice(None))
            for j in range(6):
                _rm_put(outs[j], cols, tmps[j], dil[j % 3])

    specs = [_rm_spec(tm, d, GW) for d in dil]
    res = pl.pallas_call(
        body, name="mix_bwd", grid=(S // tm,), in_specs=specs * 2 + [pl.BlockSpec((tm, GW), lambda i: (i, 0))],
        out_specs=specs * 2,
        out_shape=[_sds(_rm_shape(S, d, GW), CDT) for d in dil] + [_sds(_rm_shape(S, d, GW), F32) for d in dil],
        scratch_shapes=[pltpu.VMEM((ATT_HPG, tm, HD), F32)] * 6 + [pltpu.VMEM((tm, HD), F32)] * 6,
        compiler_params=_params(1))(*os_, *ls_, do_a)
    return res[:3], res[3:]


def _ret_tables(dk):
    H, C = RET_HEADS, BLK
    log_g = jnp.log(1.0 - 2.0 ** (-5.0 - jnp.arange(H, dtype=F32)))
    idx = jnp.arange(C, dtype=F32)
    diff = idx[:, None] - idx[None, :]
    decay = jnp.where(diff >= 0, jnp.exp(log_g[:, None, None] * jnp.maximum(diff, 0.0)), 0.0)
    xi = jnp.exp(log_g[:, None] * (idx[None, :] + 1.0))
    zeta = jnp.exp(log_g[:, None] * (C - 1.0 - idx[None, :])) * (dk ** -0.5)
    g_chunk = jnp.exp(log_g * C)
    bc = lambda t: jnp.broadcast_to(t[:, :, None], (H, C, C))
    return decay, bc(xi), bc(zeta), jnp.broadcast_to(g_chunk[:, None, None], (H, 8, C))


def _gn_fwd(o, g, b):
    mu = jnp.mean(o, axis=-1, keepdims=True)
    xc = o - mu
    rstd = lax.rsqrt(jnp.mean(xc * xc, axis=-1, keepdims=True) + EPS)
    yh = xc * rstd
    return yh, rstd, yh * g + b


def _ret_specs(dk, dv, order):
    H = RET_HEADS
    qk_w, v_w = H * dk, H * dv
    off_q = 3 * ATT_W
    off_k, off_v, off_g = off_q + qk_w, off_q + 2 * qk_w, off_q + 2 * qk_w + v_w
    assert 2 * dk == dv and all(off % dv == 0 for off in (off_q, off_k, off_v, off_g))

    def col(off, j):
        return pl.BlockSpec((BLK, dv), lambda i: (order(i), off // dv + j))

    tab = pl.BlockSpec((H, BLK, BLK), lambda i: (0, 0, 0))
    return ([col(off_q, j) for j in range(H // 2)] + [col(off_k, j) for j in range(H // 2)]
            + [col(off_v, j) for j in range(H)] + [col(off_g, j) for j in range(H)]
            + [tab, tab, tab, pl.BlockSpec((H, 8, BLK), lambda i: (0, 0, 0))])


def _ret_heads(refs, dk):
    H = RET_HEADS
    q_refs, k_refs = refs[0:H // 2], refs[H // 2:H]
    v_refs, gr_refs = refs[H:2 * H], refs[2 * H:3 * H]

    def head(h):
        cols = slice((h % 2) * dk, (h % 2 + 1) * dk)
        return q_refs[h // 2][:, cols], k_refs[h // 2][:, cols], v_refs[h][...], gr_refs[h][...]

    return head, refs[3 * H:3 * H + 4]


def _ret_fwd(proj, gn_g, gn_b, dk, dv):
    S = proj.shape[0]
    N = S // BLK
    H = RET_HEADS
    kscale = dk ** -0.5
    n_in = 3 * H + 4

    def body(*refs):
        head, (dec_ref, xi_ref, zeta_ref, gc_ref) = _ret_heads(refs, dk)
        g_ref, b_ref, opre_ref, or_ref, st_ref, state, s_buf, cross_buf = refs[n_in:n_in + 8]
        n = pl.program_id(0)

        @pl.when(n == 0)
        def _():
            state[...] = jnp.zeros_like(state)

        for h in range(H):
            q, k, v, _ = head(h)
            s_buf[h] = _dot_nt(q, k)
            st = state[h]
            st_c = st.astype(CDT)
            st_ref[h] = st_c
            cross_buf[h] = _dot_nn(q, st_c)
            kz = (k.astype(F32) * zeta_ref[h][:, 0:1]).astype(CDT)
            state[h] = st * gc_ref[h][0:1, 0:1] + _dot_tn(kz, v)
        for h in range(H):
            vs = slice(h * dv, (h + 1) * dv)
            _, _, v, gr = head(h)
            s = s_buf[h] * kscale * dec_ref[h]
            o = _dot_nn(s.astype(CDT), v) + cross_buf[h] * xi_ref[h][:, 0:1]
            opre_ref[:, vs] = o
            _, _, y = _gn_fwd(o, g_ref[:, vs], b_ref[:, vs])
            gr = gr.astype(F32)
            or_ref[:, vs] = (y * (gr * _sigmoid(gr))).astype(or_ref.dtype)

    v_w = H * dv
    row = pl.BlockSpec((1, v_w), lambda i: (0, 0))
    tile = pl.BlockSpec((BLK, v_w), lambda i: (i, 0))
    return pl.pallas_call(
        body, name="ret_fwd", grid=(N,),
        in_specs=_ret_specs(dk, dv, lambda i: i) + [row, row],
        out_specs=[tile, tile, pl.BlockSpec((None, H, dk, dv), lambda i: (i, 0, 0, 0))],
        out_shape=[_sds((S, v_w), F32), _sds((S, v_w), CDT), _sds((N, H, dk, dv), CDT)],
        scratch_shapes=[pltpu.VMEM((H, dk, dv), F32), pltpu.VMEM((H, BLK, BLK), F32), pltpu.VMEM((H, BLK, dv), F32)],
        compiler_params=_params(1),
    )(*[proj] * (3 * H), *_ret_tables(dk), gn_g, gn_b)


def _ret_bwd(proj, gn_g, gn_b, o_pre, states, d_or, dga, dgb, dk, dv):
    S, in_w = proj.shape
    N = S // BLK
    H = RET_HEADS
    qk_w, v_w = H * dk, H * dv
    kscale = dk ** -0.5
    n_in = 3 * H + 4
    out_w = 2 * qk_w + 2 * v_w
    gate_w = dga.shape[1]
    col0 = 3 * ATT_W
    assert col0 + out_w + 2 * gate_w == in_w
    rev = lambda i: N - 1 - i

    def body(*refs):
        head, (dec_ref, xi_ref, zeta_ref, gc_ref) = _ret_heads(refs, dk)
        (g_ref, b_ref, opre_ref, st_ref, dor_ref, dga_ref, dgb_ref, dproj_ref, dg_ref, db_ref, dstate, stage,
         sem, do_buf, dox_buf, a_buf, g_buf, dq_buf, dk_buf, dv_buf) = refs[n_in:n_in + 20]
        i = pl.program_id(0)
        slot = i % 2
        out_ref = stage.at[slot]

        def out_copy(s, step):
            rows = pl.ds(pl.multiple_of(rev(step) * BLK, BLK), BLK)
            return pltpu.make_async_copy(stage.at[s], dproj_ref.at[rows, pl.ds(col0, in_w - col0)], sem.at[s])

        @pl.when(i >= 2)
        def _():
            out_copy(slot, i - 2).wait()

        @pl.when(i == 0)
        def _():
            dstate[...] = jnp.zeros_like(dstate)
            dg_ref[...] = jnp.zeros_like(dg_ref)
            db_ref[...] = jnp.zeros_like(db_ref)

        out_ref[:, out_w:out_w + gate_w] = dga_ref[...]
        out_ref[:, out_w + gate_w:out_w + 2 * gate_w] = dgb_ref[...]
        for h in range(H):
            vs = slice(h * dv, (h + 1) * dv)
            _, _, _, gr = head(h)
            gr = gr.astype(F32)
            sg = _sigmoid(gr)
            gain = g_ref[:, vs]
            yh, rstd, y = _gn_fwd(opre_ref[:, vs], gain, b_ref[:, vs])
            d_or_v = dor_ref[:, vs]
            dy = d_or_v * (gr * sg)
            out_ref[:, 2 * qk_w + v_w + h * dv:2 * qk_w + v_w + (h + 1) * dv] = (
                d_or_v * y * (sg * (1.0 + gr * (1.0 - sg)))).astype(out_ref.dtype)
            dg_ref[:, vs] += jnp.sum(dy * yh, axis=0, keepdims=True)
            db_ref[:, vs] += jnp.sum(dy, axis=0, keepdims=True)
            dyh = dy * gain
            do = rstd * (dyh - jnp.mean(dyh, axis=-1, keepdims=True)
                         - yh * jnp.mean(dyh * yh, axis=-1, keepdims=True))
            do_buf[h] = do.astype(CDT)
            dox_buf[h] = (do * xi_ref[h][:, 0:1]).astype(CDT)
        for h in range(H):
            q, k, v, _ = head(h)
            dox = dox_buf[h]
            a_buf[h] = _dot_nt(q, k)
            g_buf[h] = _dot_nt(do_buf[h], v)
            dsn = dstate[h]
            dsn_c = dsn.astype(CDT)
            kz = (k.astype(F32) * zeta_ref[h][:, 0:1]).astype(CDT)
            dq_buf[h] = _dot_nt(dox, st_ref[h])
            dk_buf[h] = _dot_nt(v, dsn_c)
            dv_buf[h] = _dot_nn(kz, dsn_c)
            dstate[h] = dsn * gc_ref[h][0:1, 0:1] + _dot_tn(q, dox)
        for h in range(H):
            q, k, _, _ = head(h)
            decay = dec_ref[h]
            a_c = (a_buf[h] * kscale * decay).astype(CDT)
            g_c = (g_buf[h] * decay).astype(CDT)
            dq = _dot_nn(g_c, k) * kscale + dq_buf[h]
            dkk = _dot_tn(g_c, q) * kscale + dk_buf[h] * zeta_ref[h][:, 0:1]
            dvv = _dot_tn(a_c, do_buf[h]) + dv_buf[h]
            out_ref[:, h * dk:(h + 1) * dk] = dq.astype(out_ref.dtype)
            out_ref[:, qk_w + h * dk:qk_w + (h + 1) * dk] = dkk.astype(out_ref.dtype)
            out_ref[:, 2 * qk_w + h * dv:2 * qk_w + (h + 1) * dv] = dvv.astype(out_ref.dtype)

        cp = out_copy(slot, i)
        cp.start()

        @pl.when(i == N - 1)
        def _():
            cp.wait()
            if N >= 2:
                out_copy(1 - slot, i - 1).wait()

    row = pl.BlockSpec((1, v_w), lambda i: (0, 0))
    tile = pl.BlockSpec((BLK, v_w), lambda i: (rev(i), 0))
    gate = pl.BlockSpec((BLK, gate_w), lambda i: (rev(i), 0))
    return pl.pallas_call(
        body, name="ret_bwd", grid=(N,),
        in_specs=_ret_specs(dk, dv, rev) + [row, row, tile,
                 pl.BlockSpec((None, H, dk, dv), lambda i: (rev(i), 0, 0, 0)), tile, gate, gate],
        out_specs=[pl.BlockSpec(memory_space=pl.ANY), row, row],
        out_shape=[_sds((S, in_w), CDT), _sds((1, v_w), F32), _sds((1, v_w), F32)],
        scratch_shapes=[pltpu.VMEM((H, dk, dv), F32), pltpu.VMEM((2, BLK, in_w - col0), CDT),
                        pltpu.SemaphoreType.DMA((2,)),
                        pltpu.VMEM((H, BLK, dv), CDT), pltpu.VMEM((H, BLK, dv), CDT),
                        pltpu.VMEM((H, BLK, BLK), F32), pltpu.VMEM((H, BLK, BLK), F32),
                        pltpu.VMEM((H, BLK, dk), F32), pltpu.VMEM((H, BLK, dk), F32), pltpu.VMEM((H, BLK, dv), F32)],
        compiler_params=_params(1),
    )(*[proj] * (3 * H), *_ret_tables(dk), gn_g, gn_b, o_pre, states, d_or, dga, dgb)


def _merge_fwd(o_a, o_r, wa, wb, proj, d_model, tm=1024, tn=512):
    S, in_w = proj.shape
    off_a, off_b = in_w - 2 * d_model, in_w - d_model
    assert off_a % tn == 0 and off_b % tn == 0

    def body(oa_ref, or_ref, wa_ref, wb_ref, ga_ref, gb_ref, y_ref, pa_ref, pb_ref):
        pa = _dot_nn(oa_ref[...], wa_ref[...])
        pb = _dot_nn(or_ref[...], wb_ref[...])
        y = _sigmoid(ga_ref[...].astype(F32)) * pa + _sigmoid(gb_ref[...].astype(F32)) * pb
        y_ref[...] = y.astype(y_ref.dtype)
        pa_ref[...] = pa.astype(pa_ref.dtype)
        pb_ref[...] = pb.astype(pb_ref.dtype)

    ka, kb = o_a.shape[1], o_r.shape[1]
    out = pl.BlockSpec((tm, tn), lambda i, j: (i, j))
    return pl.pallas_call(
        body, name="merge_fwd", grid=(S // tm, d_model // tn),
        in_specs=[pl.BlockSpec((tm, ka), lambda i, j: (i, 0)), pl.BlockSpec((tm, kb), lambda i, j: (i, 0)),
                  pl.BlockSpec((ka, tn), lambda i, j: (0, j)), pl.BlockSpec((kb, tn), lambda i, j: (0, j)),
                  pl.BlockSpec((tm, tn), lambda i, j: (i, off_a // tn + j)),
                  pl.BlockSpec((tm, tn), lambda i, j: (i, off_b // tn + j))],
        out_specs=[out, out, out], out_shape=[_sds((S, d_model), CDT)] * 3,
        compiler_params=_params(2))(o_a, o_r, wa, wb, proj, proj)


def _local_step(x, target, w_in_mine, chip, others, far_w_in, small, late_weights, on_grads, deps0=()):
    S, D = x.shape
    ns_in = w_in_mine.shape[1]
    in_w = N_CHIPS * ns_in
    d_ff = 4 * D
    ret_v_w = 2 * D
    dv = ret_v_w // RET_HEADS
    dk = (in_w - 3 * ATT_W - 2 * ret_v_w - 2 * D) // (2 * RET_HEADS)
    gqk = jnp.concatenate([small["q_norm_g"].reshape(1, ATT_W), small["k_norm_g"].reshape(1, ATT_W)], axis=1)
    g1, g2 = small["norm1_g"], small["norm2_g"]
    gn_g, gn_b = small["ret_gn_g"], small["ret_gn_b"]

    proj_sds = _sds((S, in_w), CDT)
    xn, proj_mine = _in_proj_mine(x, g1, w_in_mine, chip, proj_sds, deps0)
    w_in = far_w_in(proj_mine)
    (proj,) = _matmul(
        "in_proj_far", "nn", xn, w_in, tm=512, tn=ns_in, tk=D, prefetch=[others], n_cols=(N_CHIPS - 1) * ns_in,
        b_spec=pl.BlockSpec((None, D, ns_in), lambda i, j, k, o: (o[j], 0, 0)),
        outs=[(proj_sds, pl.BlockSpec((512, ns_in), lambda i, j, k, o: (i, o[j])))], epilogue=_ep_store,
        deps=[proj_mine], alias_dep_to_out=(0, 0), j_outer=True)
    qkv = _qknorm_fwd(proj, gqk)
    att = [_att_fwd(g, S, qkv[g]) for g in range(3)]
    os_, ls_ = [a[0] for a in att], [a[1] for a in att]
    o_a = _mix_fwd(S, os_, ls_)
    o_pre, o_r, states = _ret_fwd(proj, gn_g, gn_b, dk, dv)
    w = late_weights(o_r)
    y, pa, pb = _merge_fwd(o_a, o_r, w["w_proj_a"], w["w_proj_b"], proj, D)
    x1, xn2 = _matmul("out_proj", "nn", y, w["w_out"], tm=1024, tn=D, tk=D,
                      extras=[(x, _mn(1024, D)), (g2, _row(D))],
                      outs=[(_sds((S, D), F32), _mn(1024, D)), (_sds((S, D), CDT), _mn(1024, D))],
                      epilogue=_ep_resid_norm)
    hid, act = _matmul("mlp_up", "nn", xn2, w["w_up"], tm=512, tn=2048, tk=D, j_outer=True,
                       outs=[(_sds((S, d_ff), CDT), _mn(512, 2048))] * 2, epilogue=_ep_up)
    dx2, dx2c, loss_row = _matmul(
        "mlp_down_loss", "nn", act, w["w_down"], tm=512, tn=D, tk=d_ff,
        extras=[(x1, _mn(512, D)), (target, _mn(512, D))],
        outs=[(_sds((S, D), F32), _mn(512, D)), (_sds((S, D), CDT), _mn(512, D)), (_sds((1, D), F32), _row(D))],
        epilogue=functools.partial(_ep_down_loss, inv_d=1.0 / D))
    loss = 0.5 * jnp.sum(loss_row) / D

    (dh,) = _matmul("d_hidden", "nt", dx2c, w["w_down"], tm=512, tn=2048, tk=D, j_outer=True,
                    extras=[(hid, _mn(512, 2048))], outs=[(_sds((S, d_ff), CDT), _mn(512, 2048))], epilogue=_ep_dh)
    (gw_down,) = _matmul("dw_down", "tn", act, dx2c, tm=1024, tn=D, tk=1024,
                         outs=[(_sds((d_ff, D), F32), _mn(1024, D))], epilogue=_ep_store)
    (gw_up,) = _matmul("dw_up", "tn", xn2, dh, tm=D, tn=1024, tk=1024,
                       outs=[(_sds((D, d_ff), F32), _mn(D, 1024))], epilogue=_ep_store)
    tok = on_grads({"w_down": gw_down, "w_up": gw_up})
    dx1, dx1c, dg2 = _matmul(
        "d_x1", "nt", dh, w["w_up"], tm=512, tn=D, tk=d_ff,
        extras=[(x1, _mn(512, D)), (g2, _row(D)), (dx2, _mn(512, D))],
        outs=[(_sds((S, D), F32), _mn(512, D)), (_sds((S, D), CDT), _mn(512, D)), (_sds((1, D), F32), _row(D))],
        epilogue=_ep_rms_bwd, deps=[tok])

    gt = 512
    assert (in_w - 2 * D) % gt == 0
    off_a, off_b = (in_w - 2 * D) // gt, (in_w - D) // gt
    dpa, dpb, dga, dgb = _matmul(
        "d_gates", "nt", dx1c, w["w_out"], tm=1024, tn=gt, tk=D,
        extras=[(proj, _mn(1024, gt, off_a)), (proj, _mn(1024, gt, off_b)), (pa, _mn(1024, gt)),
                (pb, _mn(1024, gt))],
        outs=[(_sds((S, D), CDT), _mn(1024, gt))] * 4, epilogue=_ep_gates)
    (gw_out,) = _matmul("dw_out", "tn", y, dx1c, tm=D, tn=D, tk=1024,
                        outs=[(_sds((D, D), F32), _mn(D, D))], epilogue=_ep_store)
    (gw_pa,) = _matmul("dw_proj_a", "tn", o_a, dpa, tm=GW, tn=D, tk=1024,
                       outs=[(_sds((GW, D), F32), _mn(GW, D))], epilogue=_ep_store)
    (gw_pb,) = _matmul("dw_proj_b", "tn", o_r, dpb, tm=1024, tn=D, tk=1024,
                       outs=[(_sds((ret_v_w, D), F32), _mn(1024, D))], epilogue=_ep_store)
    (do_a,) = _matmul("d_o_a", "nt", dpa, w["w_proj_a"], tm=1024, tn=GW, tk=D,
                      outs=[(_sds((S, GW), F32), _mn(1024, GW))], epilogue=_ep_store)
    tok = on_grads({"w_out": gw_out, "w_proj_a": gw_pa, "w_proj_b": gw_pb})
    (d_or,) = _matmul("d_o_r", "nt", dpb, w["w_proj_b"], tm=512, tn=ret_v_w, tk=D,
                      outs=[(_sds((S, ret_v_w), F32), _mn(512, ret_v_w))], epilogue=_ep_store, deps=[tok])

    dproj, dgn_g, dgn_b = _ret_bwd(proj, gn_g, gn_b, o_pre, states, d_or, dga, dgb, dk, dv)
    do_gs, c_gs = _mix_bwd(S, os_, ls_, do_a)
    datt_parts = [_att_bwd(g, S, qkv[g], ls_[g], do_gs[g], c_gs[g]) for g in range(3)]
    dproj, dgqk = _qknorm_bwd(proj, gqk, [p[0] for p in datt_parts], [p[1] for p in datt_parts],
                              [p[2] for p in datt_parts], dproj)

    (gw_in,) = _matmul(
        "dw_in", "tn", xn, dproj, tm=512, tn=ns_in, tk=1024,
        outs=[(_sds((N_CHIPS, D, ns_in), F32), pl.BlockSpec((None, 512, ns_in), lambda i, j, k: (j, i, 0)))],
        epilogue=_ep_store)
    tok = on_grads({"w_in": gw_in})
    grad_x, dg1 = _matmul(
        "d_x", "nt", dproj, w_in, tm=512, tn=D, tk=ns_in, n_cols=D,
        b_spec=pl.BlockSpec((None, D, ns_in), lambda i, j, k: (k, 0, 0)),
        extras=[(x, _mn(512, D)), (g1, _row(D)), (dx1, _mn(512, D))],
        outs=[(_sds((S, D), F32), _mn(512, D)), (_sds((1, D), F32), _row(D))],
        epilogue=_ep_rms_bwd, deps=[tok])

    smallg = {"norm1_g": dg1, "q_norm_g": dgqk[:, :ATT_W], "k_norm_g": dgqk[:, ATT_W:],
              "ret_gn_g": dgn_g, "ret_gn_b": dgn_b, "norm2_g": dg2}
    return loss, grad_x, smallg


N_CHIPS = 4
N_DEV = 8


def _place():
    x, y, c = lax.axis_index("x"), lax.axis_index("y"), lax.axis_index("c")
    return x, y, c


def _other_chips(x, y):
    out = []
    for fx, fy in ((1, 0), (0, 1), (1, 1)):
        px = 1 - x if fx else x
        py = 1 - y if fy else y
        out.append(((px, py), 2 * px + py))
    return out


SEM_SPEC = pl.BlockSpec(memory_space=pltpu.SEMAPHORE)
ANY_SPEC = pl.BlockSpec(memory_space=pl.ANY)
EFFECT = pltpu.SideEffectType.DATAFLOW_SIDE_EFFECTING


def _ici_copies(kind, srcs, lands, send, recv):
    x, y, c = _place()
    me = 2 * x + y
    out = []
    for w, (s, l) in enumerate(zip(srcs, lands)):
        for j, ((px, py), pidx) in enumerate(_other_chips(x, y)):
            if kind == "gather":
                half = s.shape[0] // 2
                rows = pl.ds(c * half, half)
                src, dst_there, dst_here = s.at[rows, :], l.at[me, rows, :], l.at[pidx, rows, :]
            else:
                src, dst_there, dst_here = s.at[pidx], l.at[me], l.at[pidx]
            out.append((src, dst_there, dst_here, send.at[3 * w + j], recv.at[3 * w + j], (px, py, c)))
    return out


def _exchange_start(name, kind, srcs, land_shapes):
    n = len(srcs)

    def body(*refs):
        src_refs, land_refs = refs[:n], refs[n:2 * n]
        send, recv = refs[2 * n], refs[2 * n + 1]
        token = refs[-1]
        for src, dst, _, ss, rs, dev in _ici_copies(kind, src_refs, land_refs, send, recv):
            pltpu.make_async_remote_copy(src_ref=src, dst_ref=dst, send_sem=ss, recv_sem=rs, device_id=dev,
                                         device_id_type=MESH).start()
        token[...] = jnp.zeros_like(token)

    thru = [pltpu.HBM(s.shape, s.dtype) for s in srcs] + [pltpu.HBM(shape, dtype) for shape, dtype in land_shapes]
    res = pl.pallas_call(
        body, name=name,
        out_shape=(pltpu.SemaphoreType.DMA((3 * n,)), pltpu.SemaphoreType.DMA((3 * n,)), *thru, _sds((8, LANES), F32)),
        in_specs=[HBM_SPEC] * (2 * n), out_specs=(SEM_SPEC, SEM_SPEC, *[HBM_SPEC] * (2 * n), VMEM_SPEC),
        input_output_aliases={i: 2 + i for i in range(2 * n)},
        compiler_params=pltpu.CompilerParams(has_side_effects=EFFECT),
    )(*[pltpu.with_memory_space_constraint(s, pltpu.HBM) for s in srcs],
      *[pltpu.with_memory_space_constraint(lax.empty(shape, dtype), pltpu.HBM) for shape, dtype in land_shapes])
    return res[0], res[1], list(res[2:2 + n]), list(res[2 + n:2 + 2 * n]), res[-1]


def _exchange_wait(name, kind, send, recv, srcs, lands, after):
    n = len(srcs)

    def body(*refs):
        src_refs, land_refs = refs[:n], refs[n:2 * n]
        send_ref, recv_ref = refs[2 * n], refs[2 * n + 1]
        for src, _, dst, ss, rs, dev in _ici_copies(kind, src_refs, land_refs, send_ref, recv_ref):
            cp = pltpu.make_async_remote_copy(src_ref=src, dst_ref=dst, send_sem=ss, recv_sem=rs, device_id=dev,
                                              device_id_type=MESH)
            cp.wait_send()
            cp.wait_recv()

    thru = [pltpu.HBM(t.shape, t.dtype) for t in list(srcs) + list(lands)]
    res = pl.pallas_call(
        body, name=name, out_shape=thru,
        in_specs=[HBM_SPEC] * (2 * n) + [SEM_SPEC, SEM_SPEC, ANY_SPEC], out_specs=[HBM_SPEC] * (2 * n),
        input_output_aliases={i: i for i in range(2 * n)},
        compiler_params=pltpu.CompilerParams(has_side_effects=EFFECT),
    )(*srcs, *lands, send, recv, after)
    return list(res[:n]), list(res[n:])


PAIR_TILE_ELEMS = 1 << 19


def _pair_fill(name, gathered, mine, core, others, chip):
    k, r, C = gathered.shape
    half = r // 2
    tr = _row_tile(half, C, PAIR_TILE_ELEMS, mult=16)
    nt = half // tr
    n_far = N_CHIPS - 1

    def body(c_ref, o_ref, chip_ref, in_ref, mine_ref, out_ref, slot, send, recv):
        j = pl.program_id(0)
        _sibling_barrier((j == 0) & (pl.program_id(1) == 0))
        b = (j * nt + pl.program_id(1)) % 2
        x, y, c = _place()
        cp = pltpu.make_async_remote_copy(src_ref=in_ref, dst_ref=slot.at[b], send_sem=send.at[b],
                                          recv_sem=recv.at[b], device_id=(x, y, 1 - c), device_id_type=MESH)

        @pl.when(j < n_far)
        def _():
            cp.start()
            cp.wait_recv()
            out_ref[...] = slot[b]
            cp.wait_send()

        @pl.when(j >= n_far)
        def _():
            out_ref[...] = mine_ref[...]

    def far(j):
        return jnp.minimum(j, n_far - 1)

    grid_spec = pltpu.PrefetchScalarGridSpec(
        num_scalar_prefetch=3, grid=(n_far + 2, nt),
        in_specs=[pl.BlockSpec((tr, C), lambda j, i, c, o, m: (
                      (2 * o[far(j)] + c[0]) * nt + jnp.where(j < n_far, i, nt - 1), 0)),
                  pl.BlockSpec((tr, C), lambda j, i, c, o, m: (jnp.where(j < n_far, 0, (j - n_far) * nt + i), 0))],
        out_specs=pl.BlockSpec((tr, C), lambda j, i, c, o, m: (
            jnp.where(j < n_far, 2 * o[far(j)] + 1 - c[0], 2 * m[0] + j - n_far) * nt + i, 0)),
        scratch_shapes=[pltpu.VMEM((2, tr, C), gathered.dtype), pltpu.SemaphoreType.DMA((2,)),
                        pltpu.SemaphoreType.DMA((2,))])
    out = pl.pallas_call(body, name=name, grid_spec=grid_spec, out_shape=_sds((k * r, C), gathered.dtype),
                         input_output_aliases={3: 0}, compiler_params=_params(2, PAIR_FILL_ID))(
                             core, others, chip, gathered.reshape(k * r, C), mine)
    return out.reshape(k, r, C)


def _pair_reduce(name, g, core):
    k, R, C = g.shape
    half = R // 2
    tr = _row_tile(half, C, PAIR_TILE_ELEMS, mult=16)
    nt = half // tr

    def body(c_ref, mine_ref, give_ref, out_ref, wire_ref, stage, slot, send, recv):
        _sibling_barrier((pl.program_id(0) == 0) & (pl.program_id(1) == 0))
        b = (pl.program_id(0) * nt + pl.program_id(1)) % 2
        x, y, c = _place()
        stage[b] = give_ref[...].astype(stage.dtype)
        cp = pltpu.make_async_remote_copy(src_ref=stage.at[b], dst_ref=slot.at[b], send_sem=send.at[b],
                                          recv_sem=recv.at[b], device_id=(x, y, 1 - c), device_id_type=MESH)
        cp.start()
        cp.wait_recv()
        tot = mine_ref[...] + slot[b].astype(F32)
        out_ref[...] = tot
        wire_ref[...] = tot.astype(wire_ref.dtype)
        cp.wait_send()

    blk = (tr, C)
    out_spec = pl.BlockSpec(blk, lambda s, i, c: (s * nt + i, 0))
    grid_spec = pltpu.PrefetchScalarGridSpec(
        num_scalar_prefetch=1, grid=(k, nt),
        in_specs=[pl.BlockSpec(blk, lambda s, i, c: ((2 * s + c[0]) * nt + i, 0)),
                  pl.BlockSpec(blk, lambda s, i, c: ((2 * s + 1 - c[0]) * nt + i, 0))],
        out_specs=[out_spec, out_spec],
        scratch_shapes=[pltpu.VMEM((2, tr, C), CDT), pltpu.VMEM((2, tr, C), CDT), pltpu.SemaphoreType.DMA((2,)),
                        pltpu.SemaphoreType.DMA((2,))])
    g2 = g.reshape(k * R, C)
    out, wire = pl.pallas_call(body, name=name, grid_spec=grid_spec,
                               out_shape=[_sds((k * half, C), F32), _sds((k * half, C), CDT)],
                               compiler_params=_params(2, PAIR_REDUCE_ID))(core, g2, g2)
    return out, wire.reshape(k, half, C)


def _all_reduce_small(v):
    r, cdim = v.shape

    def body(v_ref, o_ref, buf, send, recv):
        x, y, c = _place()
        me = 4 * x + 2 * y + c
        buf[me] = v_ref[...]
        sends = []
        for m in range(1, N_DEV):
            px = 1 - x if m & 4 else x
            py = 1 - y if m & 2 else y
            pc = 1 - c if m & 1 else c
            cp = pltpu.make_async_remote_copy(src_ref=v_ref, dst_ref=buf.at[me], send_sem=send.at[m - 1],
                                              recv_sem=recv.at[m - 1], device_id=(px, py, pc), device_id_type=MESH)
            cp.start()
            sends.append((cp, 4 * px + 2 * py + pc))
        for m, (cp, pidx) in enumerate(sends):
            pltpu.make_async_remote_copy(src_ref=v_ref, dst_ref=buf.at[pidx], send_sem=send.at[m], recv_sem=recv.at[m],
                                         device_id=(x, y, c), device_id_type=MESH).wait_recv()
        for cp, _ in sends:
            cp.wait_send()
        tot = buf[0]
        for k in range(1, N_DEV):
            tot = tot + buf[k]
        o_ref[...] = tot

    return pl.pallas_call(
        body, name="all_reduce_small", in_specs=[VMEM_SPEC], out_specs=VMEM_SPEC,
        out_shape=_sds((r, cdim), F32),
        scratch_shapes=[pltpu.VMEM((N_DEV, r, cdim), F32), pltpu.SemaphoreType.DMA((N_DEV - 1,)),
                        pltpu.SemaphoreType.DMA((N_DEV - 1,))],
    )(v)


def _row_tile(rows, cols, budget_elems=1 << 18, mult=8):
    if rows % mult:
        return rows
    t = max(mult, (budget_elems // cols) // mult * mult)
    while rows % t:
        t -= mult
    return t


def _adamw_update(w, g, m, v):
    nm = ADAM_B1 * m + (1.0 - ADAM_B1) * g
    nv = ADAM_B2 * v + (1.0 - ADAM_B2) * (g * g)
    m_hat = nm / (1.0 - ADAM_B1 ** ADAM_STEP)
    v_hat = nv / (1.0 - ADAM_B2 ** ADAM_STEP)
    return -ADAM_LR * (m_hat / (jnp.sqrt(v_hat) + ADAM_EPS) + ADAM_WD * w), nm, nv


def _adamw(name, w, g, m, v):
    R, C = w.shape
    tr = _row_tile(R, C, 1 << 18)

    def body(w_ref, g_ref, m_ref, v_ref, d_ref, nm_ref, nv_ref):
        d_ref[...], nm_ref[...], nv_ref[...] = _adamw_update(w_ref[...], g_ref[...], m_ref[...], v_ref[...])

    spec = pl.BlockSpec((tr, C), lambda i: (i, 0))
    return pl.pallas_call(body, name=name, grid=(R // tr,), in_specs=[spec] * 4, out_specs=[spec] * 3,
                          out_shape=[_sds((R, C), F32)] * 3, compiler_params=_params(1))(w, g, m, v)


def _sum_share(name, own, by_chip, chip, others, core):
    k, half, C = by_chip.shape
    tr = _row_tile(half, C, mult=16)
    nt = half // tr

    def body(chip_ref, oth_ref, c_ref, own_ref, a_ref, b_ref, cc_ref, g_out, mine, slot, send, recv):
        p = pl.program_id(1)
        _sibling_barrier((pl.program_id(0) == 0) & (p == 0))
        b = pl.program_id(0) % 2
        x, y, c = _place()
        cp = pltpu.make_async_remote_copy(src_ref=mine.at[b], dst_ref=slot.at[b], send_sem=send.at[b],
                                          recv_sem=recv.at[b], device_id=(x, y, 1 - c), device_id_type=MESH)

        @pl.when(p == 0)
        def _():
            tot = ((own_ref[...] + a_ref[...].astype(F32)) + b_ref[...].astype(F32)) + cc_ref[...].astype(F32)
            mine[b] = tot
            cp.start()
            g_out[...] = tot

        @pl.when(p == 1)
        def _():
            cp.wait_recv()
            g_out[...] = slot[b]
            cp.wait_send()

    def piece(j):
        return pl.BlockSpec((tr, C), lambda i, p, chip, oth, c: (oth[j] * nt + i, 0))

    grid_spec = pltpu.PrefetchScalarGridSpec(
        num_scalar_prefetch=3, grid=(nt, 2),
        in_specs=[pl.BlockSpec((tr, C), lambda i, p, chip, oth, c: (chip[0] * nt + i, 0)),
                  piece(0), piece(1), piece(2)],
        out_specs=pl.BlockSpec((tr, C), lambda i, p, chip, oth, c: (
            jnp.where(p == 0, c[0], 1 - c[0]) * nt + i, 0)),
        scratch_shapes=[pltpu.VMEM((2, tr, C), F32), pltpu.VMEM((2, tr, C), F32), pltpu.SemaphoreType.DMA((2,)),
                        pltpu.SemaphoreType.DMA((2,))])
    by2 = by_chip.reshape(k * half, C)
    return pl.pallas_call(body, name=name, grid_spec=grid_spec, out_shape=_sds((2 * half, C), F32),
                          compiler_params=_params(2, SUM_SHARE_ID))(chip, others, core, own, by2, by2, by2)


BIG = ("w_in", "w_proj_a", "w_proj_b", "w_out", "w_up", "w_down")
COL_SHARDED = ("w_in", "w_proj_a", "w_up")
SMALL = ("norm1_g", "q_norm_g", "k_norm_g", "ret_gn_g", "ret_gn_b", "norm2_g")
ALL_W = ("norm1_g", "w_in", "q_norm_g", "k_norm_g", "ret_gn_g", "ret_gn_b", "w_proj_a", "w_proj_b", "w_out",
         "norm2_g", "w_up", "w_down")
LANES = 128


def _to_full(name, gathered):
    k, r, c = gathered.shape
    if name in COL_SHARDED:
        return gathered.transpose(1, 0, 2).reshape(r, k * c)
    return gathered.reshape(k * r, c)


def _to_shard_major(name, full):
    if name in COL_SHARDED:
        r, c4 = full.shape
        return full.reshape(r, N_CHIPS, c4 // N_CHIPS).transpose(1, 0, 2)
    r4, c = full.shape
    return full.reshape(N_CHIPS, r4 // N_CHIPS, c)


def kernel(x, norm1_g, w_in, q_norm_g, k_norm_g, ret_gn_g, ret_gn_b, w_proj_a, w_proj_b, w_out, norm2_g, w_up, w_down, loss_target, m_norm1_g, m_w_in, m_q_norm_g, m_k_norm_g, m_ret_gn_g, m_ret_gn_b, m_w_proj_a, m_w_proj_b, m_w_out, m_norm2_g, m_w_up, m_w_down, v_norm1_g, v_w_in, v_q_norm_g, v_k_norm_g, v_ret_gn_g, v_ret_gn_b, v_w_proj_a, v_w_proj_b, v_w_out, v_norm2_g, v_w_up, v_w_down):
    weights = dict(norm1_g=norm1_g, w_in=w_in, q_norm_g=q_norm_g, k_norm_g=k_norm_g, ret_gn_g=ret_gn_g,
                   ret_gn_b=ret_gn_b, w_proj_a=w_proj_a, w_proj_b=w_proj_b, w_out=w_out, norm2_g=norm2_g,
                   w_up=w_up, w_down=w_down)
    moments_m = dict(norm1_g=m_norm1_g, w_in=m_w_in, q_norm_g=m_q_norm_g, k_norm_g=m_k_norm_g, ret_gn_g=m_ret_gn_g,
                     ret_gn_b=m_ret_gn_b, w_proj_a=m_w_proj_a, w_proj_b=m_w_proj_b, w_out=m_w_out,
                     norm2_g=m_norm2_g, w_up=m_w_up, w_down=m_w_down)
    moments_v = dict(norm1_g=v_norm1_g, w_in=v_w_in, q_norm_g=v_q_norm_g, k_norm_g=v_k_norm_g, ret_gn_g=v_ret_gn_g,
                     ret_gn_b=v_ret_gn_b, w_proj_a=v_w_proj_a, w_proj_b=v_w_proj_b, w_out=v_w_out,
                     norm2_g=v_norm2_g, w_up=v_w_up, w_down=v_w_down)

    mx, my = lax.axis_index("x"), lax.axis_index("y")
    core = lax.axis_index("c").astype(jnp.int32).reshape(1)
    chip = (2 * mx + my).astype(jnp.int32).reshape(1)
    others = jnp.stack([2 * (1 - mx) + my, 2 * mx + 1 - my, 2 * (1 - mx) + 1 - my]).astype(jnp.int32)
    shards = {n: weights[n][0].astype(CDT) for n in BIG}
    def start_gather(name, names):
        return _exchange_start(name, "gather", [shards[n] for n in names],
                               [((N_CHIPS,) + shards[n].shape, CDT) for n in names])

    i_send, i_recv, i_srcs, i_lands, i_token = start_gather("gather_w_in_start", ["w_in"])
    late = [n for n in BIG if n != "w_in"]
    l_send, l_recv, l_srcs, l_lands, l_token = start_gather("gather_late_start", late)

    def far_w_in(after):
        srcs, lands = _exchange_wait("gather_w_in_wait", "gather", i_send, i_recv, i_srcs, i_lands, after)
        return _pair_fill("pair_fill_w_in", lands[0], srcs[0], core, others, chip)

    def late_weights(after):
        srcs, lands = _exchange_wait("gather_late_wait", "gather", l_send, l_recv, l_srcs, l_lands, after)
        out = {}
        for n, mine, land in zip(late, srcs, lands):
            out[n] = _to_full(n, _pair_fill("pair_fill_%s" % n, land, mine, core, others, chip))
        return out

    pending = []

    def on_grads(group):
        names = list(group)
        red = [_pair_reduce("pair_reduce_%s" % n, g if g.ndim == 3 else _to_shard_major(n, g), core)
               for n, g in group.items()]
        wires = [wire for _, wire in red]
        send, recv, srcs, lands, token = _exchange_start(
            "scatter_start_%s" % names[0], "scatter", wires, [(wire.shape, wire.dtype) for wire in wires])
        pending.append((names, [own for own, _ in red], send, recv, srcs, lands))
        return token

    small = {n: weights[n].reshape(1, -1) for n in SMALL}

    loss, grad_x, small_g = _local_step(x[0], loss_target[0], i_srcs[0], chip, others, far_w_in, small,
                                        late_weights, on_grads, deps0=[i_token, l_token])
    loss = lax.psum(loss, ("x", "y", "c"))

    out_g, out_d, out_m, out_v = {}, {}, {}, {}
    for names, owns, send, recv, srcs, lands in pending:
        _, got = _exchange_wait("scatter_wait_%s" % names[0], "scatter", send, recv, srcs, lands, grad_x)
        for n, own, by_chip in zip(names, owns, got):
            shape = weights[n].shape
            g2 = _sum_share("sum_share_%s" % n, own, by_chip, chip, others, core)
            d, nm, nv = _adamw("adamw_%s" % n, weights[n][0], g2, moments_m[n][0], moments_v[n][0])
            out_g[n], out_d[n], out_m[n], out_v[n] = (t.reshape(shape) for t in (g2, d, nm, nv))

    packed = jnp.concatenate([small_g[n].reshape(1, -1) for n in SMALL], axis=1)
    red = _all_reduce_small(packed.reshape(-1, LANES)).reshape(1, -1)
    off = 0
    for n in SMALL:
        shape = weights[n].shape
        row = (1, weights[n].size)
        g2 = red[:, off:off + row[1]]
        off += row[1]
        d, nm, nv = _adamw("adamw_%s" % n, weights[n].reshape(row), g2, moments_m[n].reshape(row),
                           moments_v[n].reshape(row))
        out_g[n], out_d[n], out_m[n], out_v[n] = (t.reshape(shape) for t in (g2, d, nm, nv))

    return (loss, grad_x[None], *[out_g[n] for n in ALL_W], *[out_d[n] for n in ALL_W],
            *[out_m[n] for n in ALL_W], *[out_v[n] for n in ALL_W])
```

```python
import functools

import jax
import jax.numpy as jnp
from jax import lax
from jax.experimental import pallas as pl
from jax.experimental.pallas import tpu as pltpu

CDT = jnp.bfloat16
F32 = jnp.float32
EPS = 1e-6

ATT_GROUPS = ((128, 1), (512, 4), (2048, 16))
ATT_HPG = 4
ATT_HEADS = 12
HD = 128
BLK = 128
ATT_W = ATT_HEADS * HD
GW = ATT_HPG * HD
RET_HEADS = 4

ADAM_LR = 0.001
ADAM_B1 = 0.9
ADAM_B2 = 0.999
ADAM_EPS = 1e-08
ADAM_WD = 0.01
ADAM_STEP = 10

VMEM_LIMIT_BYTES = 56 * 1024 * 1024
MESH = pl.DeviceIdType.MESH
HBM_SPEC = pl.BlockSpec(memory_space=pltpu.HBM)
VMEM_SPEC = pl.BlockSpec(memory_space=pltpu.VMEM)


def _params(n_axes, collective_id=None):
    return pltpu.CompilerParams(dimension_semantics=("arbitrary",) * n_axes,
                                vmem_limit_bytes=VMEM_LIMIT_BYTES, collective_id=collective_id)


PAIR_FILL_ID, PAIR_REDUCE_ID, SUM_SHARE_ID = 1, 2, 3


def _sibling_barrier(first_step):
    @pl.when(first_step)
    def _():
        sem = pltpu.get_barrier_semaphore()
        x, y, c = lax.axis_index("x"), lax.axis_index("y"), lax.axis_index("c")
        pl.semaphore_signal(sem, inc=1, device_id=(x, y, 1 - c), device_id_type=pl.DeviceIdType.MESH)
        pl.semaphore_wait(sem, 1)


def _dot_nn(a, b):
    return jnp.dot(a, b, preferred_element_type=F32)


def _dot_nt(a, b):
    return lax.dot_general(a, b, (((1,), (1,)), ((), ())), preferred_element_type=F32)


def _dot_tn(a, b):
    return lax.dot_general(a, b, (((0,), (0,)), ((), ())), preferred_element_type=F32)


def _sigmoid(v):
    return 1.0 / (1.0 + jnp.exp(-v))


def _matmul(name, mode, a, b, *, tm, tn, tk, extras=(), outs, epilogue, deps=(), b_spec=None, n_cols=None,
            prefetch=(), alias_dep_to_out=None, j_outer=False):
    deps = [d for d in deps if d is not None]
    if mode == "tn":
        K, M = a.shape
    else:
        M, K = a.shape
    if b_spec is None:
        (N, K2) = b.shape if mode == "nt" else b.shape[::-1]
        assert K == K2, (name, a.shape, b.shape)
        if mode == "nt":
            b_spec = pl.BlockSpec((tn, tk), lambda i, j, k, *p: (j, k))
        else:
            b_spec = pl.BlockSpec((tk, tn), lambda i, j, k, *p: (k, j))
    else:
        N = n_cols
    assert M % tm == 0 and N % tn == 0 and K % tk == 0, (name, a.shape, b.shape)
    ni, nj, nk = M // tm, N // tn, K // tk
    if mode == "tn":
        a_spec = pl.BlockSpec((tk, tm), lambda i, j, k, *p: (k, i))
    else:
        a_spec = pl.BlockSpec((tm, tk), lambda i, j, k, *p: (i, k))
    dot = {"nn": _dot_nn, "nt": _dot_nt, "tn": _dot_tn}[mode]
    n_ex, n_out, n_dep, n_pre = len(extras), len(outs), len(deps), len(prefetch)
    grid = (ni, nj, nk)
    if j_outer:
        grid = (nj, ni, nk)

        def swapped(spec):
            return pl.BlockSpec(spec.block_shape, lambda j, i, k, *p: spec.index_map(i, j, k, *p))

        a_spec, b_spec = swapped(a_spec), swapped(b_spec)
        extras = [(e, swapped(s)) for e, s in extras]
        outs = [(o, swapped(s)) for o, s in outs]

    def body(*refs):
        refs = refs[n_pre:]
        a_ref, b_ref = refs[0], refs[1]
        ex = refs[2:2 + n_ex]
        out = refs[2 + n_ex + n_dep:2 + n_ex + n_dep + n_out]
        acc = refs[-1] if nk > 1 else None
        i = pl.program_id(1 if j_outer else 0)
        k = pl.program_id(2)
        if nk == 1:
            epilogue(dot(a_ref[...].astype(CDT), b_ref[...].astype(CDT)), ex, out, i)
            return

        @pl.when(k == 0)
        def _():
            acc[...] = jnp.zeros_like(acc)

        acc[...] += dot(a_ref[...].astype(CDT), b_ref[...].astype(CDT))

        @pl.when(k == nk - 1)
        def _():
            epilogue(acc[...], ex, out, i)

    grid_spec = pltpu.PrefetchScalarGridSpec(
        num_scalar_prefetch=n_pre, grid=grid,
        in_specs=[a_spec, b_spec] + [s for _, s in extras] + [pl.BlockSpec(memory_space=pl.ANY)] * n_dep,
        out_specs=[s for _, s in outs],
        scratch_shapes=[pltpu.VMEM((tm, tn), F32)] if nk > 1 else [])
    aliases = {}
    if alias_dep_to_out is not None:
        aliases = {n_pre + 2 + n_ex + alias_dep_to_out[0]: alias_dep_to_out[1]}
    res = pl.pallas_call(
        body, name=name, grid_spec=grid_spec, out_shape=[o for o, _ in outs], input_output_aliases=aliases,
        compiler_params=_params(3),
    )(*prefetch, a, b, *[e for e, _ in extras], *deps)
    return res


def _mn(tm, tn, col_off=0):
    return pl.BlockSpec((tm, tn), lambda i, j, k, *p: (i, j + col_off))


def _row(tn):
    return pl.BlockSpec((1, tn), lambda i, j, k, *p: (0, j))


def _ep_store(acc, ex, out, i):
    out[0][...] = acc.astype(out[0].dtype)


def _ep_resid_norm(acc, ex, out, i):
    x1 = ex[0][...] + acc
    out[0][...] = x1
    rstd = lax.rsqrt(jnp.mean(x1 * x1, axis=-1, keepdims=True) + EPS)
    out[1][...] = (x1 * rstd * ex[1][...]).astype(out[1].dtype)


def _ep_up(acc, ex, out, i):
    out[0][...] = acc.astype(out[0].dtype)
    r = jnp.maximum(acc, 0.0)
    out[1][...] = (r * r).astype(out[1].dtype)


def _ep_down_loss(acc, ex, out, i, inv_d):
    diff = (ex[0][...] + acc) - ex[1][...]
    dx2 = diff * inv_d
    out[0][...] = dx2
    out[1][...] = dx2.astype(out[1].dtype)

    @pl.when(i == 0)
    def _():
        out[2][...] = jnp.zeros_like(out[2])

    out[2][...] += jnp.sum(diff * diff, axis=0, keepdims=True)


def _ep_dh(acc, ex, out, i):
    h = ex[0][...].astype(F32)
    out[0][...] = (acc * (2.0 * jnp.maximum(h, 0.0))).astype(out[0].dtype)


def _ep_rms_bwd(acc, ex, out, i):
    x = ex[0][...]
    g = ex[1][...]
    rstd = lax.rsqrt(jnp.mean(x * x, axis=-1, keepdims=True) + EPS)
    xh = x * rstd
    dxh = acc * g
    dx = ex[2][...] + rstd * (dxh - xh * jnp.mean(dxh * xh, axis=-1, keepdims=True))
    out[0][...] = dx
    for copy in out[1:-1]:
        copy[...] = dx.astype(copy.dtype)
    dg = out[-1]

    @pl.when(i == 0)
    def _():
        dg[...] = jnp.zeros_like(dg)

    dg[...] += jnp.sum(acc * xh, axis=0, keepdims=True)


def _ep_gates(acc, ex, out, i):
    sa = _sigmoid(ex[0][...].astype(F32))
    sb = _sigmoid(ex[1][...].astype(F32))
    dpa = acc * sa
    dpb = acc * sb
    out[0][...] = dpa.astype(out[0].dtype)
    out[1][...] = dpb.astype(out[1].dtype)
    out[2][...] = (dpa * ex[2][...].astype(F32) * (1.0 - sa)).astype(out[2].dtype)
    out[3][...] = (dpb * ex[3][...].astype(F32) * (1.0 - sb)).astype(out[3].dtype)


def _sds(shape, dtype):
    return jax.ShapeDtypeStruct(shape, dtype)


def _in_proj_mine(x, g, w_mine, chip, proj_sds, deps, tm=512):
    S, D = x.shape
    ns = w_mine.shape[1]
    deps = [d for d in deps if d is not None]

    def body(c_ref, x_ref, g_ref, w_ref, *rest):
        xn_ref, proj_ref = rest[len(deps)], rest[len(deps) + 1]
        xv = x_ref[...]
        rstd = lax.rsqrt(jnp.mean(xv * xv, axis=-1, keepdims=True) + EPS)
        xn = (xv * rstd * g_ref[...]).astype(xn_ref.dtype)
        xn_ref[...] = xn
        proj_ref[...] = _dot_nn(xn, w_ref[...]).astype(proj_ref.dtype)

    grid_spec = pltpu.PrefetchScalarGridSpec(
        num_scalar_prefetch=1, grid=(S // tm,),
        in_specs=[pl.BlockSpec((tm, D), lambda i, c: (i, 0)), pl.BlockSpec((1, D), lambda i, c: (0, 0)),
                  pl.BlockSpec((D, ns), lambda i, c: (0, 0))] + [pl.BlockSpec(memory_space=pl.ANY)] * len(deps),
        out_specs=[pl.BlockSpec((tm, D), lambda i, c: (i, 0)), pl.BlockSpec((tm, ns), lambda i, c: (i, c[0]))])
    return pl.pallas_call(body, name="in_proj_mine", grid_spec=grid_spec, out_shape=[_sds((S, D), CDT), proj_sds],
                          compiler_params=_params(1))(chip, x, g, w_mine, *deps)


def _rm_shape(S, d, width):
    return (S, width) if d == 1 else (d, S // d, width)


def _rm_spec(tm, d, width):
    if d == 1:
        return pl.BlockSpec((tm, width), lambda i: (i, 0))
    return pl.BlockSpec((d, tm // d, width), lambda i: (0, i, 0))


def _rm_put(dst_ref, cols, buf_ref, d):
    if d == 1:
        dst_ref[:, cols] = buf_ref[...].astype(dst_ref.dtype)
        return
    m = buf_ref.shape[0] // d
    for r in range(d):
        dst_ref[r, :, cols] = buf_ref[pl.ds(r, m, stride=d), :].astype(dst_ref.dtype)


def _rm_reader(buf_ref, src_ref, d):
    if d == 1:
        return lambda s, rows: src_ref[rows, s * HD:(s + 1) * HD].astype(F32)
    m = buf_ref.shape[1] // d
    for s in range(buf_ref.shape[0]):
        for r in range(d):
            buf_ref.at[s][pl.ds(r, m, stride=d), :] = src_ref[r, :, s * HD:(s + 1) * HD].astype(F32)
    return lambda s, rows: buf_ref.at[s][rows, :]


def _qknorm_fwd(proj, gqk, tm=512):
    S = proj.shape[0]
    W = 2 * ATT_W
    dil = [d for _, d in ATT_GROUPS]

    def body(p_ref, g_ref, o0, o1, o2, buf):
        outs = (o0, o1, o2)
        for hd in range(3 * ATT_HEADS):
            which, head = hd // ATT_HEADS, hd % ATT_HEADS
            grp, slot = head // ATT_HPG, head % ATT_HPG
            cols = slice(hd * HD, (hd + 1) * HD)

            def chunk(rows, which=which, cols=cols):
                v = p_ref[rows, cols].astype(F32)
                if which < 2:
                    rstd = lax.rsqrt(jnp.mean(v * v, axis=-1, keepdims=True) + EPS)
                    v = v * rstd * g_ref[:, cols]
                buf[rows, :] = v

            chunk(slice(None))
            _rm_put(outs[grp], slice(which * GW + slot * HD, which * GW + (slot + 1) * HD), buf, dil[grp])

    return pl.pallas_call(
        body, name="qknorm_fwd", grid=(S // tm,),
        in_specs=[pl.BlockSpec((tm, 3 * ATT_W), lambda i: (i, 0)), pl.BlockSpec((1, W), lambda i: (0, 0))],
        out_specs=[_rm_spec(tm, d, 3 * GW) for d in dil],
        out_shape=[_sds(_rm_shape(S, d, 3 * GW), CDT) for d in dil],
        scratch_shapes=[pltpu.VMEM((tm, HD), F32)],
        compiler_params=_params(1))(proj, gqk)


def _qknorm_bwd(proj, gqk, dqs, dks, dvs, dproj, tm=256):
    S = proj.shape[0]
    W = 2 * ATT_W
    dil = [d for _, d in ATT_GROUPS]

    def body(p_ref, g_ref, *refs):
        ins = refs[0:9]
        o_ref, dg_ref = refs[10], refs[11]
        bufs = refs[12:21]
        i = pl.program_id(0)

        @pl.when(i == 0)
        def _():
            dg_ref[...] = jnp.zeros_like(dg_ref)

        nat = [_rm_reader(bufs[j], ins[j], dil[j % 3]) for j in range(9)]
        dq_get, dk_get, dv_get = nat[0:3], nat[3:6], nat[6:9]
        for hd in range(2 * ATT_HEADS):
            sl = slice(hd * HD, (hd + 1) * HD)
            head = hd % ATT_HEADS
            grp, slot = head // ATT_HPG, head % ATT_HPG
            get = (dq_get if hd < ATT_HEADS else dk_get)[grp]

            def chunk(rows, sl=sl, slot=slot, get=get):
                dn = get(slot, rows)
                v = p_ref[rows, sl].astype(F32)
                rstd = lax.rsqrt(jnp.mean(v * v, axis=-1, keepdims=True) + EPS)
                vh = v * rstd
                dg_ref[:, sl] += jnp.sum(dn * vh, axis=0, keepdims=True)
                dvh = dn * g_ref[:, sl]
                o_ref[rows, sl] = (rstd * (dvh - vh * jnp.mean(dvh * vh, axis=-1, keepdims=True))).astype(o_ref.dtype)

            chunk(slice(None))
        for head in range(ATT_HEADS):
            grp, slot = head // ATT_HPG, head % ATT_HPG
            o_ref[:, W + head * HD:W + (head + 1) * HD] = dv_get[grp](slot, slice(None)).astype(o_ref.dtype)

    return pl.pallas_call(
        body, name="qknorm_bwd", grid=(S // tm,),
        in_specs=[pl.BlockSpec((tm, W), lambda i: (i, 0)), pl.BlockSpec((1, W), lambda i: (0, 0))]
        + [_rm_spec(tm, d, GW) for d in dil] * 3 + [pl.BlockSpec(memory_space=pl.ANY)],
        out_specs=[pl.BlockSpec((tm, 3 * ATT_W), lambda i: (i, 0)), pl.BlockSpec((1, W), lambda i: (0, 0))],
        out_shape=[_sds(dproj.shape, dproj.dtype), _sds((1, W), F32)],
        scratch_shapes=[pltpu.VMEM((ATT_HPG, tm, HD), F32)] * 9,
        input_output_aliases={11: 0},
        compiler_params=_params(1))(proj, gqk, *dqs, *dks, *dvs, dproj)


def _att_mask(n):
    qi = lax.broadcasted_iota(jnp.int32, (BLK, 2 * BLK), 0)
    kj = lax.broadcasted_iota(jnp.int32, (BLK, 2 * BLK), 1)
    dist = BLK + qi - kj
    valid = (dist >= 0) & (dist <= BLK) & ((kj >= BLK) | (n > 0))
    return valid, dist.astype(F32)


def _att_slopes(grp):
    return [2.0 ** (-8.0 * (grp * ATT_HPG + hh + 1) / ATT_HEADS) for hh in range(ATT_HPG)]


def _att_spec(d, row_fn, col=0):
    if d == 1:
        return pl.BlockSpec((BLK, GW), lambda r, n: (row_fn(n), col))
    return pl.BlockSpec((None, BLK, GW), lambda r, n: (r, row_fn(n), col))


def _att_qkv_specs(d, nb):
    last = nb - 1

    def cur(n):
        return jnp.minimum(n, last)

    def prev(n):
        return jnp.maximum(jnp.minimum(n, last) - 1, 0)

    return [_att_spec(d, cur, 0), _att_spec(d, prev, 1), _att_spec(d, cur, 1), _att_spec(d, prev, 2),
            _att_spec(d, cur, 2)]


def _att_fwd(grp, S, qkv):
    _, d = ATT_GROUPS[grp]
    L = S // d
    nb = L // BLK
    slopes = _att_slopes(grp)
    scale = HD ** -0.5

    def body(q_ref, kp_ref, kc_ref, vp_ref, vc_ref, o_ref, l_ref, s_buf, p_buf, den_buf):
        n = pl.program_id(1)
        valid, distf = _att_mask(n)
        heads = [slice(hh * HD, (hh + 1) * HD) for hh in range(ATT_HPG)]
        for hh, sl in enumerate(heads):
            k = jnp.concatenate([kp_ref[:, sl], kc_ref[:, sl]], axis=0)
            s_buf[hh] = _dot_nt(q_ref[:, sl], k)
        for hh, sl in enumerate(heads):
            s = s_buf[hh] * scale + (-slopes[hh] * d) * distf
            s = jnp.where(valid, s, -1e30)
            m = jnp.max(s, axis=-1, keepdims=True)
            p = jnp.exp(s - m)
            den = jnp.sum(p, axis=-1, keepdims=True)
            p_buf[hh] = p.astype(CDT)
            den_buf[hh] = jnp.broadcast_to(den, (BLK, HD))
            l_ref[:, sl] = jnp.broadcast_to(m + jnp.log(den), (BLK, HD))
        for hh, sl in enumerate(heads):
            v = jnp.concatenate([vp_ref[:, sl], vc_ref[:, sl]], axis=0)
            o_ref[:, sl] = _dot_nn(p_buf[hh], v) / den_buf[hh]

    out_spec = _att_spec(d, lambda n: n)
    return pl.pallas_call(
        body, name="att_fwd_g%d" % grp, grid=(d, nb),
        in_specs=_att_qkv_specs(d, nb),
        out_specs=[out_spec, out_spec],
        out_shape=[_sds(_rm_shape(S, d, GW), F32)] * 2,
        scratch_shapes=[pltpu.VMEM((ATT_HPG, BLK, 2 * BLK), F32), pltpu.VMEM((ATT_HPG, BLK, 2 * BLK), CDT),
                        pltpu.VMEM((ATT_HPG, BLK, HD), F32)],
        compiler_params=_params(2),
    )(qkv, qkv, qkv, qkv, qkv)


def _att_bwd(grp, S, qkv, lse, do_g, c_g):
    _, d = ATT_GROUPS[grp]
    L = S // d
    nb = L // BLK
    slopes = _att_slopes(grp)
    scale = HD ** -0.5
    last = nb - 1

    def body(q_ref, kp_ref, kc_ref, vp_ref, vc_ref, l_ref, do_ref, c_ref, dq_ref, dk_ref, dv_ref, ck, cv,
             s_buf, dp_buf, p_buf, ds_buf):
        n = pl.program_id(1)

        @pl.when(n == 0)
        def _():
            ck[...] = jnp.zeros_like(ck)
            cv[...] = jnp.zeros_like(cv)

        @pl.when(n < nb)
        def _():
            valid, distf = _att_mask(n)
            heads = [slice(hh * HD, (hh + 1) * HD) for hh in range(ATT_HPG)]
            for hh, sl in enumerate(heads):
                k = jnp.concatenate([kp_ref[:, sl], kc_ref[:, sl]], axis=0)
                v = jnp.concatenate([vp_ref[:, sl], vc_ref[:, sl]], axis=0)
                s_buf[hh] = _dot_nt(q_ref[:, sl], k)
                dp_buf[hh] = _dot_nt(do_ref[:, sl], v)
            for hh, sl in enumerate(heads):
                s = s_buf[hh] * scale + (-slopes[hh] * d) * distf
                p = jnp.where(valid, jnp.exp(s - l_ref[:, sl][:, 0:1]), 0.0)
                p_buf[hh] = p.astype(CDT)
                ds_buf[hh] = (p * (dp_buf[hh] + c_ref[:, sl][:, 0:1]) * scale).astype(CDT)
            for hh, sl in enumerate(heads):
                k = jnp.concatenate([kp_ref[:, sl], kc_ref[:, sl]], axis=0)
                ds = ds_buf[hh]
                dq_ref[:, sl] = _dot_nn(ds, k)
                dk = _dot_tn(ds, q_ref[:, sl])
                dv = _dot_tn(p_buf[hh], do_ref[:, sl])
                dk_ref[:, sl] = ck[:, sl] + dk[0:BLK]
                dv_ref[:, sl] = cv[:, sl] + dv[0:BLK]
                ck[:, sl] = dk[BLK:2 * BLK]
                cv[:, sl] = dv[BLK:2 * BLK]

        @pl.when(n == nb)
        def _():
            dk_ref[...] = ck[...]
            dv_ref[...] = cv[...]

    blk = (BLK, GW)
    at_q = _att_spec(d, lambda n: jnp.minimum(n, last))
    behind = _att_spec(d, lambda n: jnp.maximum(n - 1, 0))
    return pl.pallas_call(
        body, name="att_bwd_g%d" % grp, grid=(d, nb + 1),
        in_specs=_att_qkv_specs(d, nb) + [at_q, at_q, at_q],
        out_specs=[at_q, behind, behind],
        out_shape=[_sds(_rm_shape(S, d, GW), F32)] * 3,
        scratch_shapes=[pltpu.VMEM(blk, F32), pltpu.VMEM(blk, F32),
                        pltpu.VMEM((ATT_HPG, BLK, 2 * BLK), F32), pltpu.VMEM((ATT_HPG, BLK, 2 * BLK), F32),
                        pltpu.VMEM((ATT_HPG, BLK, 2 * BLK), CDT), pltpu.VMEM((ATT_HPG, BLK, 2 * BLK), CDT)],
        compiler_params=_params(2),
    )(qkv, qkv, qkv, qkv, qkv, lse, do_g, c_g)


def _mix_alpha(l0, l1, l2):
    mx = jnp.maximum(jnp.maximum(l0, l1), l2)
    e = [jnp.exp(l0 - mx), jnp.exp(l1 - mx), jnp.exp(l2 - mx)]
    tot = e[0] + e[1] + e[2]
    return [ei / tot for ei in e]


def _mix_fwd(S, os_, ls_, tm=512):
    dil = [d for _, d in ATT_GROUPS]

    def body(*refs):
        out, bufs = refs[6], refs[7:13]
        get = [_rm_reader(bufs[j], refs[j], dil[j % 3]) for j in range(6)]
        rows = slice(None)
        for s in range(ATT_HPG):
            al = _mix_alpha(*[get[3 + g](s, rows) for g in range(3)])
            mixed = al[0] * get[0](s, rows) + al[1] * get[1](s, rows) + al[2] * get[2](s, rows)
            out[:, s * HD:(s + 1) * HD] = mixed.astype(out.dtype)

    specs = [_rm_spec(tm, d, GW) for d in dil]
    return pl.pallas_call(
        body, name="mix_fwd", grid=(S // tm,), in_specs=specs * 2, out_specs=pl.BlockSpec((tm, GW), lambda i: (i, 0)),
        out_shape=_sds((S, GW), CDT), scratch_shapes=[pltpu.VMEM((ATT_HPG, tm, HD), F32)] * 6,
        compiler_params=_params(1))(*os_, *ls_)


def _mix_bwd(S, os_, ls_, do_a, tm=512):
    dil = [d for _, d in ATT_GROUPS]

    def body(*refs):
        d_ref, outs, bufs, tmps = refs[6], refs[7:13], refs[13:19], refs[19:25]
        get = [_rm_reader(bufs[j], refs[j], dil[j % 3]) for j in range(6)]
        for s in range(ATT_HPG):
            cols = slice(s * HD, (s + 1) * HD)

            def chunk(rows, s=s, cols=cols):
                al = _mix_alpha(*[get[3 + g](s, rows) for g in range(3)])
                dv = d_ref[rows, cols]
                o_a = al[0] * get[0](s, rows) + al[1] * get[1](s, rows) + al[2] * get[2](s, rows)
                dsum = jnp.sum(dv * o_a, axis=-1, keepdims=True)
                for g in range(3):
                    tmps[g][rows, :] = al[g] * dv
                    tmps[3 + g][rows, :] = -(al[g] * dsum)

            chunk(slice(None))
            for j in range(6):
                _rm_put(outs[j], cols, tmps[j], dil[j % 3])

    specs = [_rm_spec(tm, d, GW) for d in dil]
    res = pl.pallas_call(
        body, name="mix_bwd", grid=(S // tm,), in_specs=specs * 2 + [pl.BlockSpec((tm, GW), lambda i: (i, 0))],
        out_specs=specs * 2,
        out_shape=[_sds(_rm_shape(S, d, GW), CDT) for d in dil] + [_sds(_rm_shape(S, d, GW), F32) for d in dil],
        scratch_shapes=[pltpu.VMEM((ATT_HPG, tm, HD), F32)] * 6 + [pltpu.VMEM((tm, HD), F32)] * 6,
        compiler_params=_params(1))(*os_, *ls_, do_a)
    return res[:3], res[3:]


def _ret_tables(dk):
    H, C = RET_HEADS, BLK
    log_g = jnp.log(1.0 - 2.0 ** (-5.0 - jnp.arange(H, dtype=F32)))
    idx = jnp.arange(C, dtype=F32)
    diff = idx[:, None] - idx[None, :]
    decay = jnp.where(diff >= 0, jnp.exp(log_g[:, None, None] * jnp.maximum(diff, 0.0)), 0.0)
    xi = jnp.exp(log_g[:, None] * (idx[None, :] + 1.0))
    zeta = jnp.exp(log_g[:, None] * (C - 1.0 - idx[None, :])) * (dk ** -0.5)
    g_chunk = jnp.exp(log_g * C)
    bc = lambda t: jnp.broadcast_to(t[:, :, None], (H, C, C))
    return decay, bc(xi), bc(zeta), jnp.broadcast_to(g_chunk[:, None, None], (H, 8, C))


def _gn_fwd(o, g, b):
    mu = jnp.mean(o, axis=-1, keepdims=True)
    xc = o - mu
    rstd = lax.rsqrt(jnp.mean(xc * xc, axis=-1, keepdims=True) + EPS)
    yh = xc * rstd
    return yh, rstd, yh * g + b


def _ret_specs(dk, dv, order):
    H = RET_HEADS
    qk_w, v_w = H * dk, H * dv
    off_q = 3 * ATT_W
    off_k, off_v, off_g = off_q + qk_w, off_q + 2 * qk_w, off_q + 2 * qk_w + v_w
    assert 2 * dk == dv and all(off % dv == 0 for off in (off_q, off_k, off_v, off_g))

    def col(off, j):
        return pl.BlockSpec((BLK, dv), lambda i: (order(i), off // dv + j))

    tab = pl.BlockSpec((H, BLK, BLK), lambda i: (0, 0, 0))
    return ([col(off_q, j) for j in range(H // 2)] + [col(off_k, j) for j in range(H // 2)]
            + [col(off_v, j) for j in range(H)] + [col(off_g, j) for j in range(H)]
            + [tab, tab, tab, pl.BlockSpec((H, 8, BLK), lambda i: (0, 0, 0))])


def _ret_heads(refs, dk):
    H = RET_HEADS
    q_refs, k_refs = refs[0:H // 2], refs[H // 2:H]
    v_refs, gr_refs = refs[H:2 * H], refs[2 * H:3 * H]

    def head(h):
        cols = slice((h % 2) * dk, (h % 2 + 1) * dk)
        return q_refs[h // 2][:, cols], k_refs[h // 2][:, cols], v_refs[h][...], gr_refs[h][...]

    return head, refs[3 * H:3 * H + 4]


def _ret_fwd(proj, gn_g, gn_b, dk, dv):
    S = proj.shape[0]
    N = S // BLK
    H = RET_HEADS
    kscale = dk ** -0.5
    n_in = 3 * H + 4

    def body(*refs):
        head, (dec_ref, xi_ref, zeta_ref, gc_ref) = _ret_heads(refs, dk)
        g_ref, b_ref, opre_ref, or_ref, st_ref, state, s_buf, cross_buf = refs[n_in:n_in + 8]
        n = pl.program_id(0)

        @pl.when(n == 0)
        def _():
            state[...] = jnp.zeros_like(state)

        for h in range(H):
            q, k, v, _ = head(h)
            s_buf[h] = _dot_nt(q, k)
            st = state[h]
            st_c = st.astype(CDT)
            st_ref[h] = st_c
            cross_buf[h] = _dot_nn(q, st_c)
            kz = (k.astype(F32) * zeta_ref[h][:, 0:1]).astype(CDT)
            state[h] = st * gc_ref[h][0:1, 0:1] + _dot_tn(kz, v)
        for h in range(H):
            vs = slice(h * dv, (h + 1) * dv)
            _, _, v, gr = head(h)
            s = s_buf[h] * kscale * dec_ref[h]
            o = _dot_nn(s.astype(CDT), v) + cross_buf[h] * xi_ref[h][:, 0:1]
            opre_ref[:, vs] = o
            _, _, y = _gn_fwd(o, g_ref[:, vs], b_ref[:, vs])
            gr = gr.astype(F32)
            or_ref[:, vs] = (y * (gr * _sigmoid(gr))).astype(or_ref.dtype)

    v_w = H * dv
    row = pl.BlockSpec((1, v_w), lambda i: (0, 0))
    tile = pl.BlockSpec((BLK, v_w), lambda i: (i, 0))
    return pl.pallas_call(
        body, name="ret_fwd", grid=(N,),
        in_specs=_ret_specs(dk, dv, lambda i: i) + [row, row],
        out_specs=[tile, tile, pl.BlockSpec((None, H, dk, dv), lambda i: (i, 0, 0, 0))],
        out_shape=[_sds((S, v_w), F32), _sds((S, v_w), CDT), _sds((N, H, dk, dv), CDT)],
        scratch_shapes=[pltpu.VMEM((H, dk, dv), F32), pltpu.VMEM((H, BLK, BLK), F32), pltpu.VMEM((H, BLK, dv), F32)],
        compiler_params=_params(1),
    )(*[proj] * (3 * H), *_ret_tables(dk), gn_g, gn_b)


def _ret_bwd(proj, gn_g, gn_b, o_pre, states, d_or, dga, dgb, dk, dv):
    S, in_w = proj.shape
    N = S // BLK
    H = RET_HEADS
    qk_w, v_w = H * dk, H * dv
    kscale = dk ** -0.5
    n_in = 3 * H + 4
    out_w = 2 * qk_w + 2 * v_w
    gate_w = dga.shape[1]
    col0 = 3 * ATT_W
    assert col0 + out_w + 2 * gate_w == in_w
    rev = lambda i: N - 1 - i

    def body(*refs):
        head, (dec_ref, xi_ref, zeta_ref, gc_ref) = _ret_heads(refs, dk)
        (g_ref, b_ref, opre_ref, st_ref, dor_ref, dga_ref, dgb_ref, dproj_ref, dg_ref, db_ref, dstate, stage,
         sem, do_buf, dox_buf, a_buf, g_buf, dq_buf, dk_buf, dv_buf) = refs[n_in:n_in + 20]
        i = pl.program_id(0)
        slot = i % 2
        out_ref = stage.at[slot]

        def out_copy(s, step):
            rows = pl.ds(pl.multiple_of(rev(step) * BLK, BLK), BLK)
            return pltpu.make_async_copy(stage.at[s], dproj_ref.at[rows, pl.ds(col0, in_w - col0)], sem.at[s])

        @pl.when(i >= 2)
        def _():
            out_copy(slot, i - 2).wait()

        @pl.when(i == 0)
        def _():
            dstate[...] = jnp.zeros_like(dstate)
            dg_ref[...] = jnp.zeros_like(dg_ref)
            db_ref[...] = jnp.zeros_like(db_ref)

        out_ref[:, out_w:out_w + gate_w] = dga_ref[...]
        out_ref[:, out_w + gate_w:out_w + 2 * gate_w] = dgb_ref[...]
        for h in range(H):
            vs = slice(h * dv, (h + 1) * dv)
            _, _, _, gr = head(h)
            gr = gr.astype(F32)
            sg = _sigmoid(gr)
            gain = g_ref[:, vs]
            yh, rstd, y = _gn_fwd(opre_ref[:, vs], gain, b_ref[:, vs])
            d_or_v = dor_ref[:, vs]
            dy = d_or_v * (gr * sg)
            out_ref[:, 2 * qk_w + v_w + h * dv:2 * qk_w + v_w + (h + 1) * dv] = (
                d_or_v * y * (sg * (1.0 + gr * (1.0 - sg)))).astype(out_ref.dtype)
            dg_ref[:, vs] += jnp.sum(dy * yh, axis=0, keepdims=True)
            db_ref[:, vs] += jnp.sum(dy, axis=0, keepdims=True)
            dyh = dy * gain
            do = rstd * (dyh - jnp.mean(dyh, axis=-1, keepdims=True)
                         - yh * jnp.mean(dyh * yh, axis=-1, keepdims=True))
            do_buf[h] = do.astype(CDT)
            dox_buf[h] = (do * xi_ref[h][:, 0:1]).astype(CDT)
        for h in range(H):
            q, k, v, _ = head(h)
            dox = dox_buf[h]
            a_buf[h] = _dot_nt(q, k)
            g_buf[h] = _dot_nt(do_buf[h], v)
            dsn = dstate[h]
            dsn_c = dsn.astype(CDT)
            kz = (k.astype(F32) * zeta_ref[h][:, 0:1]).astype(CDT)
            dq_buf[h] = _dot_nt(dox, st_ref[h])
            dk_buf[h] = _dot_nt(v, dsn_c)
            dv_buf[h] = _dot_nn(kz, dsn_c)
            dstate[h] = dsn * gc_ref[h][0:1, 0:1] + _dot_tn(q, dox)
        for h in range(H):
            q, k, _, _ = head(h)
            decay = dec_ref[h]
            a_c = (a_buf[h] * kscale * decay).astype(CDT)
            g_c = (g_buf[h] * decay).astype(CDT)
            dq = _dot_nn(g_c, k) * kscale + dq_buf[h]
            dkk = _dot_tn(g_c, q) * kscale + dk_buf[h] * zeta_ref[h][:, 0:1]
            dvv = _dot_tn(a_c, do_buf[h]) + dv_buf[h]
            out_ref[:, h * dk:(h + 1) * dk] = dq.astype(out_ref.dtype)
            out_ref[:, qk_w + h * dk:qk_w + (h + 1) * dk] = dkk.astype(out_ref.dtype)
            out_ref[:, 2 * qk_w + h * dv:2 * qk_w + (h + 1) * dv] = dvv.astype(out_ref.dtype)

        cp = out_copy(slot, i)
        cp.start()

        @pl.when(i == N - 1)
        def _():
            cp.wait()
            if N >= 2:
                out_copy(1 - slot, i - 1).wait()

    row = pl.BlockSpec((1, v_w), lambda i: (0, 0))
    tile = pl.BlockSpec((BLK, v_w), lambda i: (rev(i), 0))
    gate = pl.BlockSpec((BLK, gate_w), lambda i: (rev(i), 0))
    return pl.pallas_call(
        body, name="ret_bwd", grid=(N,),
        in_specs=_ret_specs(dk, dv, rev) + [row, row, tile,
                 pl.BlockSpec((None, H, dk, dv), lambda i: (rev(i), 0, 0, 0)), tile, gate, gate],
        out_specs=[pl.BlockSpec(memory_space=pl.ANY), row, row],
        out_shape=[_sds((S, in_w), CDT), _sds((1, v_w), F32), _sds((1, v_w), F32)],
        scratch_shapes=[pltpu.VMEM((H, dk, dv), F32), pltpu.VMEM((2, BLK, in_w - col0), CDT),
                        pltpu.SemaphoreType.DMA((2,)),
                        pltpu.VMEM((H, BLK, dv), CDT), pltpu.VMEM((H, BLK, dv), CDT),
                        pltpu.VMEM((H, BLK, BLK), F32), pltpu.VMEM((H, BLK, BLK), F32),
                        pltpu.VMEM((H, BLK, dk), F32), pltpu.VMEM((H, BLK, dk), F32), pltpu.VMEM((H, BLK, dv), F32)],
        compiler_params=_params(1),
    )(*[proj] * (3 * H), *_ret_tables(dk), gn_g, gn_b, o_pre, states, d_or, dga, dgb)


def _merge_fwd(o_a, o_r, wa, wb, proj, d_model, tm=1024, tn=512):
    S, in_w = proj.shape
    off_a, off_b = in_w - 2 * d_model, in_w - d_model
    assert off_a % tn == 0 and off_b % tn == 0

    def body(oa_ref, or_ref, wa_ref, wb_ref, ga_ref, gb_ref, y_ref, pa_ref, pb_ref):
        pa = _dot_nn(oa_ref[...], wa_ref[...])
        pb = _dot_nn(or_ref[...], wb_ref[...])
        y = _sigmoid(ga_ref[...].astype(F32)) * pa + _sigmoid(gb_ref[...].astype(F32)) * pb
        y_ref[...] = y.astype(y_ref.dtype)
        pa_ref[...] = pa.astype(pa_ref.dtype)
        pb_ref[...] = pb.astype(pb_ref.dtype)

    ka, kb = o_a.shape[1], o_r.shape[1]
    out = pl.BlockSpec((tm, tn), lambda i, j: (i, j))
    return pl.pallas_call(
        body, name="merge_fwd", grid=(S // tm, d_model // tn),
        in_specs=[pl.BlockSpec((tm, ka), lambda i, j: (i, 0)), pl.BlockSpec((tm, kb), lambda i, j: (i, 0)),
                  pl.BlockSpec((ka, tn), lambda i, j: (0, j)), pl.BlockSpec((kb, tn), lambda i, j: (0, j)),
                  pl.BlockSpec((tm, tn), lambda i, j: (i, off_a // tn + j)),
                  pl.BlockSpec((tm, tn), lambda i, j: (i, off_b // tn + j))],
        out_specs=[out, out, out], out_shape=[_sds((S, d_model), CDT)] * 3,
        compiler_params=_params(2))(o_a, o_r, wa, wb, proj, proj)


def _local_step(x, target, w_in_mine, chip, others, near_w_in, far_w_in, small, late_weights, on_grads, deps0=()):
    S, D = x.shape
    ns_in = w_in_mine.shape[1]
    in_w = N_CHIPS * ns_in
    d_ff = 4 * D
    ret_v_w = 2 * D
    dv = ret_v_w // RET_HEADS
    dk = (in_w - 3 * ATT_W - 2 * ret_v_w - 2 * D) // (2 * RET_HEADS)
    gqk = jnp.concatenate([small["q_norm_g"].reshape(1, ATT_W), small["k_norm_g"].reshape(1, ATT_W)], axis=1)
    g1, g2 = small["norm1_g"], small["norm2_g"]
    gn_g, gn_b = small["ret_gn_g"], small["ret_gn_b"]

    proj_sds = _sds((S, in_w), CDT)
    xn, proj = _in_proj_mine(x, g1, w_in_mine, chip, proj_sds, deps0)
    for stage, (get_w_in, chips) in enumerate(((near_w_in, others[:2]), (far_w_in, others[2:]))):
        w_in = get_w_in(proj)
        (proj,) = _matmul(
            "in_proj_far%d" % stage, "nn", xn, w_in, tm=512, tn=ns_in, tk=D, prefetch=[chips],
            n_cols=chips.shape[0] * ns_in, b_spec=pl.BlockSpec((None, D, ns_in), lambda i, j, k, o: (o[j], 0, 0)),
            outs=[(proj_sds, pl.BlockSpec((512, ns_in), lambda i, j, k, o: (i, o[j])))], epilogue=_ep_store,
            deps=[proj], alias_dep_to_out=(0, 0), j_outer=True)
    qkv = _qknorm_fwd(proj, gqk)
    att = [_att_fwd(g, S, qkv[g]) for g in range(3)]
    os_, ls_ = [a[0] for a in att], [a[1] for a in att]
    o_a = _mix_fwd(S, os_, ls_)
    o_pre, o_r, states = _ret_fwd(proj, gn_g, gn_b, dk, dv)
    w = late_weights(o_r)
    y, pa, pb = _merge_fwd(o_a, o_r, w["w_proj_a"], w["w_proj_b"], proj, D)
    x1, xn2 = _matmul("out_proj", "nn", y, w["w_out"], tm=1024, tn=D, tk=D,
                      extras=[(x, _mn(1024, D)), (g2, _row(D))],
                      outs=[(_sds((S, D), F32), _mn(1024, D)), (_sds((S, D), CDT), _mn(1024, D))],
                      epilogue=_ep_resid_norm)
    hid, act = _matmul("mlp_up", "nn", xn2, w["w_up"], tm=512, tn=2048, tk=D, j_outer=True,
                       outs=[(_sds((S, d_ff), CDT), _mn(512, 2048))] * 2, epilogue=_ep_up)
    dx2, dx2c, loss_row = _matmul(
        "mlp_down_loss", "nn", act, w["w_down"], tm=512, tn=D, tk=d_ff,
        extras=[(x1, _mn(512, D)), (target, _mn(512, D))],
        outs=[(_sds((S, D), F32), _mn(512, D)), (_sds((S, D), CDT), _mn(512, D)), (_sds((1, D), F32), _row(D))],
        epilogue=functools.partial(_ep_down_loss, inv_d=1.0 / D))
    loss = 0.5 * jnp.sum(loss_row) / D

    (dh,) = _matmul("d_hidden", "nt", dx2c, w["w_down"], tm=512, tn=2048, tk=D, j_outer=True,
                    extras=[(hid, _mn(512, 2048))], outs=[(_sds((S, d_ff), CDT), _mn(512, 2048))], epilogue=_ep_dh)
    (gw_down,) = _matmul("dw_down", "tn", act, dx2c, tm=1024, tn=D, tk=1024,
                         outs=[(_sds((d_ff, D), F32), _mn(1024, D))], epilogue=_ep_store)
    (gw_up,) = _matmul("dw_up", "tn", xn2, dh, tm=D, tn=1024, tk=1024,
                       outs=[(_sds((D, d_ff), F32), _mn(D, 1024))], epilogue=_ep_store)
    tok = on_grads({"w_down": gw_down, "w_up": gw_up})
    dx1, dx1c, dg2 = _matmul(
        "d_x1", "nt", dh, w["w_up"], tm=512, tn=D, tk=d_ff,
        extras=[(x1, _mn(512, D)), (g2, _row(D)), (dx2, _mn(512, D))],
        outs=[(_sds((S, D), F32), _mn(512, D)), (_sds((S, D), CDT), _mn(512, D)), (_sds((1, D), F32), _row(D))],
        epilogue=_ep_rms_bwd, deps=[tok])

    gt = 512
    assert (in_w - 2 * D) % gt == 0
    off_a, off_b = (in_w - 2 * D) // gt, (in_w - D) // gt
    dpa, dpb, dga, dgb = _matmul(
        "d_gates", "nt", dx1c, w["w_out"], tm=1024, tn=gt, tk=D,
        extras=[(proj, _mn(1024, gt, off_a)), (proj, _mn(1024, gt, off_b)), (pa, _mn(1024, gt)),
                (pb, _mn(1024, gt))],
        outs=[(_sds((S, D), CDT), _mn(1024, gt))] * 4, epilogue=_ep_gates)
    (gw_out,) = _matmul("dw_out", "tn", y, dx1c, tm=D, tn=D, tk=1024,
                        outs=[(_sds((D, D), F32), _mn(D, D))], epilogue=_ep_store)
    (gw_pa,) = _matmul("dw_proj_a", "tn", o_a, dpa, tm=GW, tn=D, tk=1024,
                       outs=[(_sds((GW, D), F32), _mn(GW, D))], epilogue=_ep_store)
    (gw_pb,) = _matmul("dw_proj_b", "tn", o_r, dpb, tm=1024, tn=D, tk=1024,
                       outs=[(_sds((ret_v_w, D), F32), _mn(1024, D))], epilogue=_ep_store)
    (do_a,) = _matmul("d_o_a", "nt", dpa, w["w_proj_a"], tm=1024, tn=GW, tk=D,
                      outs=[(_sds((S, GW), F32), _mn(1024, GW))], epilogue=_ep_store)
    tok = on_grads({"w_out": gw_out, "w_proj_a": gw_pa, "w_proj_b": gw_pb})
    (d_or,) = _matmul("d_o_r", "nt", dpb, w["w_proj_b"], tm=512, tn=ret_v_w, tk=D,
                      outs=[(_sds((S, ret_v_w), F32), _mn(512, ret_v_w))], epilogue=_ep_store, deps=[tok])

    dproj, dgn_g, dgn_b = _ret_bwd(proj, gn_g, gn_b, o_pre, states, d_or, dga, dgb, dk, dv)
    do_gs, c_gs = _mix_bwd(S, os_, ls_, do_a)
    datt_parts = [_att_bwd(g, S, qkv[g], ls_[g], do_gs[g], c_gs[g]) for g in range(3)]
    dproj, dgqk = _qknorm_bwd(proj, gqk, [p[0] for p in datt_parts], [p[1] for p in datt_parts],
                              [p[2] for p in datt_parts], dproj)

    (gw_in,) = _matmul(
        "dw_in", "tn", xn, dproj, tm=512, tn=ns_in, tk=1024,
        outs=[(_sds((N_CHIPS, D, ns_in), F32), pl.BlockSpec((None, 512, ns_in), lambda i, j, k: (j, i, 0)))],
        epilogue=_ep_store)
    tok = on_grads({"w_in": gw_in})
    grad_x, dg1 = _matmul(
        "d_x", "nt", dproj, w_in, tm=512, tn=D, tk=ns_in, n_cols=D,
        b_spec=pl.BlockSpec((None, D, ns_in), lambda i, j, k: (k, 0, 0)),
        extras=[(x, _mn(512, D)), (g1, _row(D)), (dx1, _mn(512, D))],
        outs=[(_sds((S, D), F32), _mn(512, D)), (_sds((1, D), F32), _row(D))],
        epilogue=_ep_rms_bwd, deps=[tok])

    smallg = {"norm1_g": dg1, "q_norm_g": dgqk[:, :ATT_W], "k_norm_g": dgqk[:, ATT_W:],
              "ret_gn_g": dgn_g, "ret_gn_b": dgn_b, "norm2_g": dg2}
    return loss, grad_x, smallg


N_CHIPS = 4
N_DEV = 8


def _place():
    x, y, c = lax.axis_index("x"), lax.axis_index("y"), lax.axis_index("c")
    return x, y, c


def _other_chips(x, y):
    out = []
    for fx, fy in ((1, 0), (0, 1), (1, 1)):
        px = 1 - x if fx else x
        py = 1 - y if fy else y
        out.append(((px, py), 2 * px + py))
    return out


SEM_SPEC = pl.BlockSpec(memory_space=pltpu.SEMAPHORE)
ANY_SPEC = pl.BlockSpec(memory_space=pl.ANY)
EFFECT = pltpu.SideEffectType.DATAFLOW_SIDE_EFFECTING


def _ici_copies(kind, srcs, lands, send, recv, which=(0, 1, 2)):
    x, y, c = _place()
    me = 2 * x + y
    out = []
    for w, (s, l) in enumerate(zip(srcs, lands)):
        for j, ((px, py), pidx) in enumerate(_other_chips(x, y)):
            if j not in which:
                continue
            if kind == "gather":
                half = s.shape[0] // 2
                rows = pl.ds(c * half, half)
                src, dst_there, dst_here = s.at[rows, :], l.at[me, rows, :], l.at[pidx, rows, :]
            else:
                src, dst_there, dst_here = s.at[pidx], l.at[me], l.at[pidx]
            out.append((src, dst_there, dst_here, send.at[3 * w + j], recv.at[3 * w + j], (px, py, c)))
    return out


def _exchange_start(name, kind, srcs, land_shapes, which=(0, 1, 2), lands=None):
    n = len(srcs)
    if lands is None:
        lands = [lax.empty(shape, dtype) for shape, dtype in land_shapes]

    def body(*refs):
        src_refs, land_refs = refs[:n], refs[n:2 * n]
        send, recv = refs[2 * n], refs[2 * n + 1]
        token = refs[-1]
        for src, dst, _, ss, rs, dev in _ici_copies(kind, src_refs, land_refs, send, recv, which):
            pltpu.make_async_remote_copy(src_ref=src, dst_ref=dst, send_sem=ss, recv_sem=rs, device_id=dev,
                                         device_id_type=MESH).start()
        token[...] = jnp.zeros_like(token)

    thru = [pltpu.HBM(s.shape, s.dtype) for s in srcs] + [pltpu.HBM(shape, dtype) for shape, dtype in land_shapes]
    res = pl.pallas_call(
        body, name=name,
        out_shape=(pltpu.SemaphoreType.DMA((3 * n,)), pltpu.SemaphoreType.DMA((3 * n,)), *thru, _sds((8, LANES), F32)),
        in_specs=[HBM_SPEC] * (2 * n), out_specs=(SEM_SPEC, SEM_SPEC, *[HBM_SPEC] * (2 * n), VMEM_SPEC),
        input_output_aliases={i: 2 + i for i in range(2 * n)},
        compiler_params=pltpu.CompilerParams(has_side_effects=EFFECT),
    )(*[pltpu.with_memory_space_constraint(s, pltpu.HBM) for s in srcs],
      *[pltpu.with_memory_space_constraint(l, pltpu.HBM) for l in lands])
    return res[0], res[1], list(res[2:2 + n]), list(res[2 + n:2 + 2 * n]), res[-1]


def _exchange_wait(name, kind, send, recv, srcs, lands, after, which=(0, 1, 2)):
    n = len(srcs)

    def body(*refs):
        src_refs, land_refs = refs[:n], refs[n:2 * n]
        send_ref, recv_ref = refs[2 * n], refs[2 * n + 1]
        for src, _, dst, ss, rs, dev in _ici_copies(kind, src_refs, land_refs, send_ref, recv_ref, which):
            cp = pltpu.make_async_remote_copy(src_ref=src, dst_ref=dst, send_sem=ss, recv_sem=rs, device_id=dev,
                                              device_id_type=MESH)
            cp.wait_send()
            cp.wait_recv()

    thru = [pltpu.HBM(t.shape, t.dtype) for t in list(srcs) + list(lands)]
    res = pl.pallas_call(
        body, name=name, out_shape=thru,
        in_specs=[HBM_SPEC] * (2 * n) + [SEM_SPEC, SEM_SPEC, ANY_SPEC], out_specs=[HBM_SPEC] * (2 * n),
        input_output_aliases={i: i for i in range(2 * n)},
        compiler_params=pltpu.CompilerParams(has_side_effects=EFFECT),
    )(*srcs, *lands, send, recv, after)
    return list(res[:n]), list(res[n:])


PAIR_TILE_ELEMS = 1 << 19


def _pair_fill(name, gathered, mine, core, others, chip, write_mine=True):
    k, r, C = gathered.shape
    half = r // 2
    tr = _row_tile(half, C, PAIR_TILE_ELEMS, mult=16)
    nt = half // tr
    n_far = others.shape[0]

    def body(c_ref, o_ref, chip_ref, in_ref, mine_ref, out_ref, slot, send, recv):
        j = pl.program_id(0)
        _sibling_barrier((j == 0) & (pl.program_id(1) == 0))
        b = (j * nt + pl.program_id(1)) % 2
        x, y, c = _place()
        cp = pltpu.make_async_remote_copy(src_ref=in_ref, dst_ref=slot.at[b], send_sem=send.at[b],
                                          recv_sem=recv.at[b], device_id=(x, y, 1 - c), device_id_type=MESH)

        @pl.when(j < n_far)
        def _():
            cp.start()
            cp.wait_recv()
            out_ref[...] = slot[b]
            cp.wait_send()

        @pl.when(j >= n_far)
        def _():
            out_ref[...] = mine_ref[...]

    def far(j):
        return jnp.minimum(j, n_far - 1)

    grid_spec = pltpu.PrefetchScalarGridSpec(
        num_scalar_prefetch=3, grid=(n_far + (2 if write_mine else 0), nt),
        in_specs=[pl.BlockSpec((tr, C), lambda j, i, c, o, m: (
                      (2 * o[far(j)] + c[0]) * nt + jnp.where(j < n_far, i, nt - 1), 0)),
                  pl.BlockSpec((tr, C), lambda j, i, c, o, m: (jnp.where(j < n_far, 0, (j - n_far) * nt + i), 0))],
        out_specs=pl.BlockSpec((tr, C), lambda j, i, c, o, m: (
            jnp.where(j < n_far, 2 * o[far(j)] + 1 - c[0], 2 * m[0] + j - n_far) * nt + i, 0)),
        scratch_shapes=[pltpu.VMEM((2, tr, C), gathered.dtype), pltpu.SemaphoreType.DMA((2,)),
                        pltpu.SemaphoreType.DMA((2,))])
    out = pl.pallas_call(body, name=name, grid_spec=grid_spec, out_shape=_sds((k * r, C), gathered.dtype),
                         input_output_aliases={3: 0}, compiler_params=_params(2, PAIR_FILL_ID))(
                             core, others, chip, gathered.reshape(k * r, C), mine)
    return out.reshape(k, r, C)


def _pair_reduce(name, g, core):
    k, R, C = g.shape
    half = R // 2
    tr = _row_tile(half, C, PAIR_TILE_ELEMS, mult=16)
    nt = half // tr

    def body(c_ref, mine_ref, give_ref, out_ref, wire_ref, stage, slot, send, recv):
        _sibling_barrier((pl.program_id(0) == 0) & (pl.program_id(1) == 0))
        b = (pl.program_id(0) * nt + pl.program_id(1)) % 2
        x, y, c = _place()
        stage[b] = give_ref[...].astype(stage.dtype)
        cp = pltpu.make_async_remote_copy(src_ref=stage.at[b], dst_ref=slot.at[b], send_sem=send.at[b],
                                          recv_sem=recv.at[b], device_id=(x, y, 1 - c), device_id_type=MESH)
        cp.start()
        cp.wait_recv()
        tot = mine_ref[...] + slot[b].astype(F32)
        out_ref[...] = tot
        wire_ref[...] = tot.astype(wire_ref.dtype)
        cp.wait_send()

    blk = (tr, C)
    out_spec = pl.BlockSpec(blk, lambda s, i, c: (s * nt + i, 0))
    grid_spec = pltpu.PrefetchScalarGridSpec(
        num_scalar_prefetch=1, grid=(k, nt),
        in_specs=[pl.BlockSpec(blk, lambda s, i, c: ((2 * s + c[0]) * nt + i, 0)),
                  pl.BlockSpec(blk, lambda s, i, c: ((2 * s + 1 - c[0]) * nt + i, 0))],
        out_specs=[out_spec, out_spec],
        scratch_shapes=[pltpu.VMEM((2, tr, C), CDT), pltpu.VMEM((2, tr, C), CDT), pltpu.SemaphoreType.DMA((2,)),
                        pltpu.SemaphoreType.DMA((2,))])
    g2 = g.reshape(k * R, C)
    out, wire = pl.pallas_call(body, name=name, grid_spec=grid_spec,
                               out_shape=[_sds((k * half, C), F32), _sds((k * half, C), CDT)],
                               compiler_params=_params(2, PAIR_REDUCE_ID))(core, g2, g2)
    return out, wire.reshape(k, half, C)


def _all_reduce_small(v):
    r, cdim = v.shape

    def body(v_ref, o_ref, buf, send, recv):
        x, y, c = _place()
        me = 4 * x + 2 * y + c
        buf[me] = v_ref[...]
        sends = []
        for m in range(1, N_DEV):
            px = 1 - x if m & 4 else x
            py = 1 - y if m & 2 else y
            pc = 1 - c if m & 1 else c
            cp = pltpu.make_async_remote_copy(src_ref=v_ref, dst_ref=buf.at[me], send_sem=send.at[m - 1],
                                              recv_sem=recv.at[m - 1], device_id=(px, py, pc), device_id_type=MESH)
            cp.start()
            sends.append((cp, 4 * px + 2 * py + pc))
        for m, (cp, pidx) in enumerate(sends):
            pltpu.make_async_remote_copy(src_ref=v_ref, dst_ref=buf.at[pidx], send_sem=send.at[m], recv_sem=recv.at[m],
                                         device_id=(x, y, c), device_id_type=MESH).wait_recv()
        for cp, _ in sends:
            cp.wait_send()
        tot = buf[0]
        for k in range(1, N_DEV):
            tot = tot + buf[k]
        o_ref[...] = tot

    return pl.pallas_call(
        body, name="all_reduce_small", in_specs=[VMEM_SPEC], out_specs=VMEM_SPEC,
        out_shape=_sds((r, cdim), F32),
        scratch_shapes=[pltpu.VMEM((N_DEV, r, cdim), F32), pltpu.SemaphoreType.DMA((N_DEV - 1,)),
                        pltpu.SemaphoreType.DMA((N_DEV - 1,))],
    )(v)


def _row_tile(rows, cols, budget_elems=1 << 18, mult=8):
    if rows % mult:
        return rows
    t = max(mult, (budget_elems // cols) // mult * mult)
    while rows % t:
        t -= mult
    return t


def _adamw_update(w, g, m, v):
    nm = ADAM_B1 * m + (1.0 - ADAM_B1) * g
    nv = ADAM_B2 * v + (1.0 - ADAM_B2) * (g * g)
    m_hat = nm / (1.0 - ADAM_B1 ** ADAM_STEP)
    v_hat = nv / (1.0 - ADAM_B2 ** ADAM_STEP)
    return -ADAM_LR * (m_hat / (jnp.sqrt(v_hat) + ADAM_EPS) + ADAM_WD * w), nm, nv


def _adamw(name, w, g, m, v):
    R, C = w.shape
    tr = _row_tile(R, C, 1 << 18)

    def body(w_ref, g_ref, m_ref, v_ref, d_ref, nm_ref, nv_ref):
        d_ref[...], nm_ref[...], nv_ref[...] = _adamw_update(w_ref[...], g_ref[...], m_ref[...], v_ref[...])

    spec = pl.BlockSpec((tr, C), lambda i: (i, 0))
    return pl.pallas_call(body, name=name, grid=(R // tr,), in_specs=[spec] * 4, out_specs=[spec] * 3,
                          out_shape=[_sds((R, C), F32)] * 3, compiler_params=_params(1))(w, g, m, v)


def _sum_share(name, own, by_chip, chip, others, core):
    k, half, C = by_chip.shape
    tr = _row_tile(half, C, mult=16)
    nt = half // tr

    def body(chip_ref, oth_ref, c_ref, own_ref, a_ref, b_ref, cc_ref, g_out, mine, slot, send, recv):
        p = pl.program_id(1)
        _sibling_barrier((pl.program_id(0) == 0) & (p == 0))
        b = pl.program_id(0) % 2
        x, y, c = _place()
        cp = pltpu.make_async_remote_copy(src_ref=mine.at[b], dst_ref=slot.at[b], send_sem=send.at[b],
                                          recv_sem=recv.at[b], device_id=(x, y, 1 - c), device_id_type=MESH)

        @pl.when(p == 0)
        def _():
            tot = ((own_ref[...] + a_ref[...].astype(F32)) + b_ref[...].astype(F32)) + cc_ref[...].astype(F32)
            mine[b] = tot
            cp.start()
            g_out[...] = tot

        @pl.when(p == 1)
        def _():
            cp.wait_recv()
            g_out[...] = slot[b]
            cp.wait_send()

    def piece(j):
        return pl.BlockSpec((tr, C), lambda i, p, chip, oth, c: (oth[j] * nt + i, 0))

    grid_spec = pltpu.PrefetchScalarGridSpec(
        num_scalar_prefetch=3, grid=(nt, 2),
        in_specs=[pl.BlockSpec((tr, C), lambda i, p, chip, oth, c: (chip[0] * nt + i, 0)),
                  piece(0), piece(1), piece(2)],
        out_specs=pl.BlockSpec((tr, C), lambda i, p, chip, oth, c: (
            jnp.where(p == 0, c[0], 1 - c[0]) * nt + i, 0)),
        scratch_shapes=[pltpu.VMEM((2, tr, C), F32), pltpu.VMEM((2, tr, C), F32), pltpu.SemaphoreType.DMA((2,)),
                        pltpu.SemaphoreType.DMA((2,))])
    by2 = by_chip.reshape(k * half, C)
    return pl.pallas_call(body, name=name, grid_spec=grid_spec, out_shape=_sds((2 * half, C), F32),
                          compiler_params=_params(2, SUM_SHARE_ID))(chip, others, core, own, by2, by2, by2)


BIG = ("w_in", "w_proj_a", "w_proj_b", "w_out", "w_up", "w_down")
COL_SHARDED = ("w_in", "w_proj_a", "w_up")
SMALL = ("norm1_g", "q_norm_g", "k_norm_g", "ret_gn_g", "ret_gn_b", "norm2_g")
ALL_W = ("norm1_g", "w_in", "q_norm_g", "k_norm_g", "ret_gn_g", "ret_gn_b", "w_proj_a", "w_proj_b", "w_out",
         "norm2_g", "w_up", "w_down")
LANES = 128


def _to_full(name, gathered):
    k, r, c = gathered.shape
    if name in COL_SHARDED:
        return gathered.transpose(1, 0, 2).reshape(r, k * c)
    return gathered.reshape(k * r, c)


def _to_shard_major(name, full):
    if name in COL_SHARDED:
        r, c4 = full.shape
        return full.reshape(r, N_CHIPS, c4 // N_CHIPS).transpose(1, 0, 2)
    r4, c = full.shape
    return full.reshape(N_CHIPS, r4 // N_CHIPS, c)


def kernel(x, norm1_g, w_in, q_norm_g, k_norm_g, ret_gn_g, ret_gn_b, w_proj_a, w_proj_b, w_out, norm2_g, w_up, w_down, loss_target, m_norm1_g, m_w_in, m_q_norm_g, m_k_norm_g, m_ret_gn_g, m_ret_gn_b, m_w_proj_a, m_w_proj_b, m_w_out, m_norm2_g, m_w_up, m_w_down, v_norm1_g, v_w_in, v_q_norm_g, v_k_norm_g, v_ret_gn_g, v_ret_gn_b, v_w_proj_a, v_w_proj_b, v_w_out, v_norm2_g, v_w_up, v_w_down):
    weights = dict(norm1_g=norm1_g, w_in=w_in, q_norm_g=q_norm_g, k_norm_g=k_norm_g, ret_gn_g=ret_gn_g,
                   ret_gn_b=ret_gn_b, w_proj_a=w_proj_a, w_proj_b=w_proj_b, w_out=w_out, norm2_g=norm2_g,
                   w_up=w_up, w_down=w_down)
    moments_m = dict(norm1_g=m_norm1_g, w_in=m_w_in, q_norm_g=m_q_norm_g, k_norm_g=m_k_norm_g, ret_gn_g=m_ret_gn_g,
                     ret_gn_b=m_ret_gn_b, w_proj_a=m_w_proj_a, w_proj_b=m_w_proj_b, w_out=m_w_out,
                     norm2_g=m_norm2_g, w_up=m_w_up, w_down=m_w_down)
    moments_v = dict(norm1_g=v_norm1_g, w_in=v_w_in, q_norm_g=v_q_norm_g, k_norm_g=v_k_norm_g, ret_gn_g=v_ret_gn_g,
                     ret_gn_b=v_ret_gn_b, w_proj_a=v_w_proj_a, w_proj_b=v_w_proj_b, w_out=v_w_out,
                     norm2_g=v_norm2_g, w_up=v_w_up, w_down=v_w_down)

    mx, my = lax.axis_index("x"), lax.axis_index("y")
    core = lax.axis_index("c").astype(jnp.int32).reshape(1)
    chip = (2 * mx + my).astype(jnp.int32).reshape(1)
    others = jnp.stack([2 * (1 - mx) + my, 2 * mx + 1 - my, 2 * (1 - mx) + 1 - my]).astype(jnp.int32)
    shards = {n: weights[n][0].astype(CDT) for n in BIG}
    def start_gather(name, names):
        return _exchange_start(name, "gather", [shards[n] for n in names],
                               [((N_CHIPS,) + shards[n].shape, CDT) for n in names])

    w_in_shape = [((N_CHIPS,) + shards["w_in"].shape, CDT)]
    n_send, n_recv, n_srcs, n_lands, n_token = _exchange_start(
        "gather_w_in_near_start", "gather", [shards["w_in"]], w_in_shape, which=(0, 1))
    late = [n for n in BIG if n != "w_in"]
    flight = {}

    def near_w_in(after):
        srcs, lands = _exchange_wait("gather_w_in_near_wait", "gather", n_send, n_recv, n_srcs, n_lands, after,
                                     which=(0, 1))
        d_send, d_recv, d_srcs, d_lands, _ = _exchange_start(
            "gather_w_in_diag_start", "gather", srcs, w_in_shape, which=(2,), lands=lands)
        flight["late"] = start_gather("gather_late_start", late)
        w_near = _pair_fill("pair_fill_w_in_near", d_lands[0], d_srcs[0], core, others[:2], chip)
        flight["diag"] = (d_send, d_recv, d_srcs, [w_near])
        return w_near

    def far_w_in(after):
        d_send, d_recv, d_srcs, d_lands = flight["diag"]
        srcs, lands = _exchange_wait("gather_w_in_diag_wait", "gather", d_send, d_recv, d_srcs, d_lands, after,
                                     which=(2,))
        return _pair_fill("pair_fill_w_in_diag", lands[0], srcs[0], core, others[2:], chip, write_mine=False)

    def late_weights(after):
        l_send, l_recv, l_srcs, l_lands, _ = flight["late"]
        srcs, lands = _exchange_wait("gather_late_wait", "gather", l_send, l_recv, l_srcs, l_lands, after)
        out = {}
        for n, mine, land in zip(late, srcs, lands):
            out[n] = _to_full(n, _pair_fill("pair_fill_%s" % n, land, mine, core, others, chip))
        return out

    pending = []

    def on_grads(group):
        names = list(group)
        red = [_pair_reduce("pair_reduce_%s" % n, g if g.ndim == 3 else _to_shard_major(n, g), core)
               for n, g in group.items()]
        wires = [wire for _, wire in red]
        send, recv, srcs, lands, token = _exchange_start(
            "scatter_start_%s" % names[0], "scatter", wires, [(wire.shape, wire.dtype) for wire in wires])
        pending.append((names, [own for own, _ in red], send, recv, srcs, lands))
        return token

    small = {n: weights[n].reshape(1, -1) for n in SMALL}

    loss, grad_x, small_g = _local_step(x[0], loss_target[0], n_srcs[0], chip, others, near_w_in, far_w_in, small,
                                        late_weights, on_grads, deps0=[n_token])
    loss = lax.psum(loss, ("x", "y", "c"))

    out_g, out_d, out_m, out_v = {}, {}, {}, {}
    for names, owns, send, recv, srcs, lands in pending:
        _, got = _exchange_wait("scatter_wait_%s" % names[0], "scatter", send, recv, srcs, lands, grad_x)
        for n, own, by_chip in zip(names, owns, got):
            shape = weights[n].shape
            g2 = _sum_share("sum_share_%s" % n, own, by_chip, chip, others, core)
            d, nm, nv = _adamw("adamw_%s" % n, weights[n][0], g2, moments_m[n][0], moments_v[n][0])
            out_g[n], out_d[n], out_m[n], out_v[n] = (t.reshape(shape) for t in (g2, d, nm, nv))

    packed = jnp.concatenate([small_g[n].reshape(1, -1) for n in SMALL], axis=1)
    red = _all_reduce_small(packed.reshape(-1, LANES)).reshape(1, -1)
    off = 0
    for n in SMALL:
        shape = weights[n].shape
        row = (1, weights[n].size)
        g2 = red[:, off:off + row[1]]
        off += row[1]
        d, nm, nv = _adamw("adamw_%s" % n, weights[n].reshape(row), g2, moments_m[n].reshape(row),
                           moments_v[n].reshape(row))
        out_g[n], out_d[n], out_m[n], out_v[n] = (t.reshape(shape) for t in (g2, d, nm, nv))

    return (loss, grad_x[None], *[out_g[n] for n in ALL_W], *[out_d[n] for n in ALL_W],
            *[out_m[n] for n in ALL_W], *[out_v[n] for n in ALL_W])
```

```python
import functools

import jax
import jax.numpy as jnp
from jax import lax
from jax.experimental import pallas as pl
from jax.experimental.pallas import tpu as pltpu

CDT = jnp.bfloat16
F32 = jnp.float32
EPS = 1e-6

ATT_GROUPS = ((128, 1), (512, 4), (2048, 16))
ATT_HPG = 4
ATT_HEADS = 12
HD = 128
BLK = 128
ATT_W = ATT_HEADS * HD
GW = ATT_HPG * HD
RET_HEADS = 4

ADAM_LR = 0.001
ADAM_B1 = 0.9
ADAM_B2 = 0.999
ADAM_EPS = 1e-08
ADAM_WD = 0.01
ADAM_STEP = 10

VMEM_LIMIT_BYTES = 56 * 1024 * 1024
MESH = pl.DeviceIdType.MESH
HBM_SPEC = pl.BlockSpec(memory_space=pltpu.HBM)
VMEM_SPEC = pl.BlockSpec(memory_space=pltpu.VMEM)


def _params(n_axes, collective_id=None):
    return pltpu.CompilerParams(dimension_semantics=("arbitrary",) * n_axes,
                                vmem_limit_bytes=VMEM_LIMIT_BYTES, collective_id=collective_id)


PAIR_FILL_ID, PAIR_REDUCE_ID, SUM_SHARE_ID = 1, 2, 3


def _sibling_barrier(first_step):
    @pl.when(first_step)
    def _():
        sem = pltpu.get_barrier_semaphore()
        x, y, c = lax.axis_index("x"), lax.axis_index("y"), lax.axis_index("c")
        pl.semaphore_signal(sem, inc=1, device_id=(x, y, 1 - c), device_id_type=pl.DeviceIdType.MESH)
        pl.semaphore_wait(sem, 1)


def _dot_nn(a, b):
    return jnp.dot(a, b, preferred_element_type=F32)


def _dot_nt(a, b):
    return lax.dot_general(a, b, (((1,), (1,)), ((), ())), preferred_element_type=F32)


def _dot_tn(a, b):
    return lax.dot_general(a, b, (((0,), (0,)), ((), ())), preferred_element_type=F32)


def _sigmoid(v):
    return 1.0 / (1.0 + jnp.exp(-v))


def _matmul(name, mode, a, b, *, tm, tn, tk, extras=(), outs, epilogue, deps=(), b_spec=None, n_cols=None,
            prefetch=(), alias_dep_to_out=None, j_outer=False):
    deps = [d for d in deps if d is not None]
    if mode == "tn":
        K, M = a.shape
    else:
        M, K = a.shape
    if b_spec is None:
        (N, K2) = b.shape if mode == "nt" else b.shape[::-1]
        assert K == K2, (name, a.shape, b.shape)
        if mode == "nt":
            b_spec = pl.BlockSpec((tn, tk), lambda i, j, k, *p: (j, k))
        else:
            b_spec = pl.BlockSpec((tk, tn), lambda i, j, k, *p: (k, j))
    else:
        N = n_cols
    assert M % tm == 0 and N % tn == 0 and K % tk == 0, (name, a.shape, b.shape)
    ni, nj, nk = M // tm, N // tn, K // tk
    if mode == "tn":
        a_spec = pl.BlockSpec((tk, tm), lambda i, j, k, *p: (k, i))
    else:
        a_spec = pl.BlockSpec((tm, tk), lambda i, j, k, *p: (i, k))
    dot = {"nn": _dot_nn, "nt": _dot_nt, "tn": _dot_tn}[mode]
    n_ex, n_out, n_dep, n_pre = len(extras), len(outs), len(deps), len(prefetch)
    grid = (ni, nj, nk)
    if j_outer:
        grid = (nj, ni, nk)

        def swapped(spec):
            return pl.BlockSpec(spec.block_shape, lambda j, i, k, *p: spec.index_map(i, j, k, *p))

        a_spec, b_spec = swapped(a_spec), swapped(b_spec)
        extras = [(e, swapped(s)) for e, s in extras]
        outs = [(o, swapped(s)) for o, s in outs]

    def body(*refs):
        refs = refs[n_pre:]
        a_ref, b_ref = refs[0], refs[1]
        ex = refs[2:2 + n_ex]
        out = refs[2 + n_ex + n_dep:2 + n_ex + n_dep + n_out]
        acc = refs[-1] if nk > 1 else None
        i = pl.program_id(1 if j_outer else 0)
        k = pl.program_id(2)
        if nk == 1:
            epilogue(dot(a_ref[...].astype(CDT), b_ref[...].astype(CDT)), ex, out, i)
            return

        @pl.when(k == 0)
        def _():
            acc[...] = jnp.zeros_like(acc)

        acc[...] += dot(a_ref[...].astype(CDT), b_ref[...].astype(CDT))

        @pl.when(k == nk - 1)
        def _():
            epilogue(acc[...], ex, out, i)

    grid_spec = pltpu.PrefetchScalarGridSpec(
        num_scalar_prefetch=n_pre, grid=grid,
        in_specs=[a_spec, b_spec] + [s for _, s in extras] + [pl.BlockSpec(memory_space=pl.ANY)] * n_dep,
        out_specs=[s for _, s in outs],
        scratch_shapes=[pltpu.VMEM((tm, tn), F32)] if nk > 1 else [])
    aliases = {}
    if alias_dep_to_out is not None:
        aliases = {n_pre + 2 + n_ex + alias_dep_to_out[0]: alias_dep_to_out[1]}
    res = pl.pallas_call(
        body, name=name, grid_spec=grid_spec, out_shape=[o for o, _ in outs], input_output_aliases=aliases,
        compiler_params=_params(3),
    )(*prefetch, a, b, *[e for e, _ in extras], *deps)
    return res


def _mn(tm, tn, col_off=0):
    return pl.BlockSpec((tm, tn), lambda i, j, k, *p: (i, j + col_off))


def _row(tn):
    return pl.BlockSpec((1, tn), lambda i, j, k, *p: (0, j))


def _ep_store(acc, ex, out, i):
    out[0][...] = acc.astype(out[0].dtype)


def _ep_resid_norm(acc, ex, out, i):
    x1 = ex[0][...] + acc
    out[0][...] = x1
    rstd = lax.rsqrt(jnp.mean(x1 * x1, axis=-1, keepdims=True) + EPS)
    out[1][...] = (x1 * rstd * ex[1][...]).astype(out[1].dtype)


def _ep_up(acc, ex, out, i):
    out[0][...] = acc.astype(out[0].dtype)
    r = jnp.maximum(acc, 0.0)
    out[1][...] = (r * r).astype(out[1].dtype)


def _ep_down_loss(acc, ex, out, i, inv_d):
    diff = (ex[0][...] + acc) - ex[1][...]
    dx2 = diff * inv_d
    out[0][...] = dx2
    out[1][...] = dx2.astype(out[1].dtype)

    @pl.when(i == 0)
    def _():
        out[2][...] = jnp.zeros_like(out[2])

    out[2][...] += jnp.sum(diff * diff, axis=0, keepdims=True)


def _ep_dh(acc, ex, out, i):
    h = ex[0][...].astype(F32)
    out[0][...] = (acc * (2.0 * jnp.maximum(h, 0.0))).astype(out[0].dtype)


def _ep_rms_bwd(acc, ex, out, i):
    x = ex[0][...]
    g = ex[1][...]
    rstd = lax.rsqrt(jnp.mean(x * x, axis=-1, keepdims=True) + EPS)
    xh = x * rstd
    dxh = acc * g
    dx = ex[2][...] + rstd * (dxh - xh * jnp.mean(dxh * xh, axis=-1, keepdims=True))
    out[0][...] = dx
    for copy in out[1:-1]:
        copy[...] = dx.astype(copy.dtype)
    dg = out[-1]

    @pl.when(i == 0)
    def _():
        dg[...] = jnp.zeros_like(dg)

    dg[...] += jnp.sum(acc * xh, axis=0, keepdims=True)


def _ep_gates(acc, ex, out, i):
    sa = _sigmoid(ex[0][...].astype(F32))
    sb = _sigmoid(ex[1][...].astype(F32))
    dpa = acc * sa
    dpb = acc * sb
    out[0][...] = dpa.astype(out[0].dtype)
    out[1][...] = dpb.astype(out[1].dtype)
    out[2][...] = (dpa * ex[2][...].astype(F32) * (1.0 - sa)).astype(out[2].dtype)
    out[3][...] = (dpb * ex[3][...].astype(F32) * (1.0 - sb)).astype(out[3].dtype)


def _sds(shape, dtype):
    return jax.ShapeDtypeStruct(shape, dtype)


def _in_proj_mine(x, g, w_mine, chip, proj_sds, deps, tm=512):
    S, D = x.shape
    ns = w_mine.shape[1]
    deps = [d for d in deps if d is not None]

    def body(c_ref, x_ref, g_ref, w_ref, *rest):
        xn_ref, proj_ref = rest[len(deps)], rest[len(deps) + 1]
        xv = x_ref[...]
        rstd = lax.rsqrt(jnp.mean(xv * xv, axis=-1, keepdims=True) + EPS)
        xn = (xv * rstd * g_ref[...]).astype(xn_ref.dtype)
        xn_ref[...] = xn
        proj_ref[...] = _dot_nn(xn, w_ref[...]).astype(proj_ref.dtype)

    grid_spec = pltpu.PrefetchScalarGridSpec(
        num_scalar_prefetch=1, grid=(S // tm,),
        in_specs=[pl.BlockSpec((tm, D), lambda i, c: (i, 0)), pl.BlockSpec((1, D), lambda i, c: (0, 0)),
                  pl.BlockSpec((D, ns), lambda i, c: (0, 0))] + [pl.BlockSpec(memory_space=pl.ANY)] * len(deps),
        out_specs=[pl.BlockSpec((tm, D), lambda i, c: (i, 0)), pl.BlockSpec((tm, ns), lambda i, c: (i, c[0]))])
    return pl.pallas_call(body, name="in_proj_mine", grid_spec=grid_spec, out_shape=[_sds((S, D), CDT), proj_sds],
                          compiler_params=_params(1))(chip, x, g, w_mine, *deps)


def _rm_shape(S, d, width):
    return (S, width) if d == 1 else (d, S // d, width)


def _rm_spec(tm, d, width):
    if d == 1:
        return pl.BlockSpec((tm, width), lambda i: (i, 0))
    return pl.BlockSpec((d, tm // d, width), lambda i: (0, i, 0))


def _rm_put(dst_ref, cols, buf_ref, d):
    if d == 1:
        dst_ref[:, cols] = buf_ref[...].astype(dst_ref.dtype)
        return
    m = buf_ref.shape[0] // d
    for r in range(d):
        dst_ref[r, :, cols] = buf_ref[pl.ds(r, m, stride=d), :].astype(dst_ref.dtype)


def _rm_reader(buf_ref, src_ref, d):
    if d == 1:
        return lambda s, rows: src_ref[rows, s * HD:(s + 1) * HD].astype(F32)
    m = buf_ref.shape[1] // d
    for s in range(buf_ref.shape[0]):
        for r in range(d):
            buf_ref.at[s][pl.ds(r, m, stride=d), :] = src_ref[r, :, s * HD:(s + 1) * HD].astype(F32)
    return lambda s, rows: buf_ref.at[s][rows, :]


def _qknorm_fwd(proj, gqk, tm=512):
    S = proj.shape[0]
    W = 2 * ATT_W
    dil = [d for _, d in ATT_GROUPS]

    def body(p_ref, g_ref, o0, o1, o2, buf):
        outs = (o0, o1, o2)
        for hd in range(3 * ATT_HEADS):
            which, head = hd // ATT_HEADS, hd % ATT_HEADS
            grp, slot = head // ATT_HPG, head % ATT_HPG
            cols = slice(hd * HD, (hd + 1) * HD)

            def chunk(rows, which=which, cols=cols):
                v = p_ref[rows, cols].astype(F32)
                if which < 2:
                    rstd = lax.rsqrt(jnp.mean(v * v, axis=-1, keepdims=True) + EPS)
                    v = v * rstd * g_ref[:, cols]
                buf[rows, :] = v

            chunk(slice(None))
            _rm_put(outs[grp], slice(which * GW + slot * HD, which * GW + (slot + 1) * HD), buf, dil[grp])

    return pl.pallas_call(
        body, name="qknorm_fwd", grid=(S // tm,),
        in_specs=[pl.BlockSpec((tm, 3 * ATT_W), lambda i: (i, 0)), pl.BlockSpec((1, W), lambda i: (0, 0))],
        out_specs=[_rm_spec(tm, d, 3 * GW) for d in dil],
        out_shape=[_sds(_rm_shape(S, d, 3 * GW), CDT) for d in dil],
        scratch_shapes=[pltpu.VMEM((tm, HD), F32)],
        compiler_params=_params(1))(proj, gqk)


def _qknorm_bwd(proj, gqk, dqs, dks, dvs, dproj, tm=256):
    S = proj.shape[0]
    W = 2 * ATT_W
    dil = [d for _, d in ATT_GROUPS]

    def body(p_ref, g_ref, *refs):
        ins = refs[0:9]
        o_ref, dg_ref = refs[10], refs[11]
        bufs = refs[12:21]
        i = pl.program_id(0)

        @pl.when(i == 0)
        def _():
            dg_ref[...] = jnp.zeros_like(dg_ref)

        nat = [_rm_reader(bufs[j], ins[j], dil[j % 3]) for j in range(9)]
        dq_get, dk_get, dv_get = nat[0:3], nat[3:6], nat[6:9]
        for hd in range(2 * ATT_HEADS):
            sl = slice(hd * HD, (hd + 1) * HD)
            head = hd % ATT_HEADS
            grp, slot = head // ATT_HPG, head % ATT_HPG
            get = (dq_get if hd < ATT_HEADS else dk_get)[grp]

            def chunk(rows, sl=sl, slot=slot, get=get):
                dn = get(slot, rows)
                v = p_ref[rows, sl].astype(F32)
                rstd = lax.rsqrt(jnp.mean(v * v, axis=-1, keepdims=True) + EPS)
                vh = v * rstd
                dg_ref[:, sl] += jnp.sum(dn * vh, axis=0, keepdims=True)
                dvh = dn * g_ref[:, sl]
                o_ref[rows, sl] = (rstd * (dvh - vh * jnp.mean(dvh * vh, axis=-1, keepdims=True))).astype(o_ref.dtype)

            chunk(slice(None))
        for head in range(ATT_HEADS):
            grp, slot = head // ATT_HPG, head % ATT_HPG
            o_ref[:, W + head * HD:W + (head + 1) * HD] = dv_get[grp](slot, slice(None)).astype(o_ref.dtype)

    return pl.pallas_call(
        body, name="qknorm_bwd", grid=(S // tm,),
        in_specs=[pl.BlockSpec((tm, W), lambda i: (i, 0)), pl.BlockSpec((1, W), lambda i: (0, 0))]
        + [_rm_spec(tm, d, GW) for d in dil] * 3 + [pl.BlockSpec(memory_space=pl.ANY)],
        out_specs=[pl.BlockSpec((tm, 3 * ATT_W), lambda i: (i, 0)), pl.BlockSpec((1, W), lambda i: (0, 0))],
        out_shape=[_sds(dproj.shape, dproj.dtype), _sds((1, W), F32)],
        scratch_shapes=[pltpu.VMEM((ATT_HPG, tm, HD), F32)] * 9,
        input_output_aliases={11: 0},
        compiler_params=_params(1))(proj, gqk, *dqs, *dks, *dvs, dproj)


def _att_mask(n):
    qi = lax.broadcasted_iota(jnp.int32, (BLK, 2 * BLK), 0)
    kj = lax.broadcasted_iota(jnp.int32, (BLK, 2 * BLK), 1)
    dist = BLK + qi - kj
    valid = (dist >= 0) & (dist <= BLK) & ((kj >= BLK) | (n > 0))
    return valid, dist.astype(F32)


def _att_slopes(grp):
    return [2.0 ** (-8.0 * (grp * ATT_HPG + hh + 1) / ATT_HEADS) for hh in range(ATT_HPG)]


def _att_spec(d, row_fn, col=0):
    if d == 1:
        return pl.BlockSpec((BLK, GW), lambda r, n: (row_fn(n), col))
    return pl.BlockSpec((None, BLK, GW), lambda r, n: (r, row_fn(n), col))


def _att_qkv_specs(d, nb):
    last = nb - 1

    def cur(n):
        return jnp.minimum(n, last)

    def prev(n):
        return jnp.maximum(jnp.minimum(n, last) - 1, 0)

    return [_att_spec(d, cur, 0), _att_spec(d, prev, 1), _att_spec(d, cur, 1), _att_spec(d, prev, 2),
            _att_spec(d, cur, 2)]


def _att_fwd(grp, S, qkv):
    _, d = ATT_GROUPS[grp]
    L = S // d
    nb = L // BLK
    slopes = _att_slopes(grp)
    scale = HD ** -0.5

    def body(q_ref, kp_ref, kc_ref, vp_ref, vc_ref, o_ref, l_ref, s_buf, p_buf, den_buf):
        n = pl.program_id(1)
        valid, distf = _att_mask(n)
        heads = [slice(hh * HD, (hh + 1) * HD) for hh in range(ATT_HPG)]
        for hh, sl in enumerate(heads):
            k = jnp.concatenate([kp_ref[:, sl], kc_ref[:, sl]], axis=0)
            s_buf[hh] = _dot_nt(q_ref[:, sl], k)
        for hh, sl in enumerate(heads):
            s = s_buf[hh] * scale + (-slopes[hh] * d) * distf
            s = jnp.where(valid, s, -1e30)
            m = jnp.max(s, axis=-1, keepdims=True)
            p = jnp.exp(s - m)
            den = jnp.sum(p, axis=-1, keepdims=True)
            p_buf[hh] = p.astype(CDT)
            den_buf[hh] = jnp.broadcast_to(den, (BLK, HD))
            l_ref[:, sl] = jnp.broadcast_to(m + jnp.log(den), (BLK, HD))
        for hh, sl in enumerate(heads):
            v = jnp.concatenate([vp_ref[:, sl], vc_ref[:, sl]], axis=0)
            o_ref[:, sl] = _dot_nn(p_buf[hh], v) / den_buf[hh]

    out_spec = _att_spec(d, lambda n: n)
    return pl.pallas_call(
        body, name="att_fwd_g%d" % grp, grid=(d, nb),
        in_specs=_att_qkv_specs(d, nb),
        out_specs=[out_spec, out_spec],
        out_shape=[_sds(_rm_shape(S, d, GW), F32)] * 2,
        scratch_shapes=[pltpu.VMEM((ATT_HPG, BLK, 2 * BLK), F32), pltpu.VMEM((ATT_HPG, BLK, 2 * BLK), CDT),
                        pltpu.VMEM((ATT_HPG, BLK, HD), F32)],
        compiler_params=_params(2),
    )(qkv, qkv, qkv, qkv, qkv)


def _att_bwd(grp, S, qkv, lse, do_g, c_g):
    _, d = ATT_GROUPS[grp]
    L = S // d
    nb = L // BLK
    slopes = _att_slopes(grp)
    scale = HD ** -0.5
    last = nb - 1

    def body(q_ref, kp_ref, kc_ref, vp_ref, vc_ref, l_ref, do_ref, c_ref, dq_ref, dk_ref, dv_ref, ck, cv,
             s_buf, dp_buf, p_buf, ds_buf):
        n = pl.program_id(1)

        @pl.when(n == 0)
        def _():
            ck[...] = jnp.zeros_like(ck)
            cv[...] = jnp.zeros_like(cv)

        @pl.when(n < nb)
        def _():
            valid, distf = _att_mask(n)
            heads = [slice(hh * HD, (hh + 1) * HD) for hh in range(ATT_HPG)]
            for hh, sl in enumerate(heads):
                k = jnp.concatenate([kp_ref[:, sl], kc_ref[:, sl]], axis=0)
                v = jnp.concatenate([vp_ref[:, sl], vc_ref[:, sl]], axis=0)
                s_buf[hh] = _dot_nt(q_ref[:, sl], k)
                dp_buf[hh] = _dot_nt(do_ref[:, sl], v)
            for hh, sl in enumerate(heads):
                s = s_buf[hh] * scale + (-slopes[hh] * d) * distf
                p = jnp.where(valid, jnp.exp(s - l_ref[:, sl][:, 0:1]), 0.0)
                p_buf[hh] = p.astype(CDT)
                ds_buf[hh] = (p * (dp_buf[hh] + c_ref[:, sl][:, 0:1]) * scale).astype(CDT)
            for hh, sl in enumerate(heads):
                k = jnp.concatenate([kp_ref[:, sl], kc_ref[:, sl]], axis=0)
                ds = ds_buf[hh]
                dq_ref[:, sl] = _dot_nn(ds, k)
                dk = _dot_tn(ds, q_ref[:, sl])
                dv = _dot_tn(p_buf[hh], do_ref[:, sl])
                dk_ref[:, sl] = ck[:, sl] + dk[0:BLK]
                dv_ref[:, sl] = cv[:, sl] + dv[0:BLK]
                ck[:, sl] = dk[BLK:2 * BLK]
                cv[:, sl] = dv[BLK:2 * BLK]

        @pl.when(n == nb)
        def _():
            dk_ref[...] = ck[...]
            dv_ref[...] = cv[...]

    blk = (BLK, GW)
    at_q = _att_spec(d, lambda n: jnp.minimum(n, last))
    behind = _att_spec(d, lambda n: jnp.maximum(n - 1, 0))
    return pl.pallas_call(
        body, name="att_bwd_g%d" % grp, grid=(d, nb + 1),
        in_specs=_att_qkv_specs(d, nb) + [at_q, at_q, at_q],
        out_specs=[at_q, behind, behind],
        out_shape=[_sds(_rm_shape(S, d, GW), F32)] * 3,
        scratch_shapes=[pltpu.VMEM(blk, F32), pltpu.VMEM(blk, F32),
                        pltpu.VMEM((ATT_HPG, BLK, 2 * BLK), F32), pltpu.VMEM((ATT_HPG, BLK, 2 * BLK), F32),
                        pltpu.VMEM((ATT_HPG, BLK, 2 * BLK), CDT), pltpu.VMEM((ATT_HPG, BLK, 2 * BLK), CDT)],
        compiler_params=_params(2),
    )(qkv, qkv, qkv, qkv, qkv, lse, do_g, c_g)


def _mix_alpha(l0, l1, l2):
    mx = jnp.maximum(jnp.maximum(l0, l1), l2)
    e = [jnp.exp(l0 - mx), jnp.exp(l1 - mx), jnp.exp(l2 - mx)]
    tot = e[0] + e[1] + e[2]
    return [ei / tot for ei in e]


def _mix_fwd(S, os_, ls_, tm=512):
    dil = [d for _, d in ATT_GROUPS]

    def body(*refs):
        out, bufs = refs[6], refs[7:13]
        get = [_rm_reader(bufs[j], refs[j], dil[j % 3]) for j in range(6)]
        rows = slice(None)
        for s in range(ATT_HPG):
            al = _mix_alpha(*[get[3 + g](s, rows) for g in range(3)])
            mixed = al[0] * get[0](s, rows) + al[1] * get[1](s, rows) + al[2] * get[2](s, rows)
            out[:, s * HD:(s + 1) * HD] = mixed.astype(out.dtype)

    specs = [_rm_spec(tm, d, GW) for d in dil]
    return pl.pallas_call(
        body, name="mix_fwd", grid=(S // tm,), in_specs=specs * 2, out_specs=pl.BlockSpec((tm, GW), lambda i: (i, 0)),
        out_shape=_sds((S, GW), CDT), scratch_shapes=[pltpu.VMEM((ATT_HPG, tm, HD), F32)] * 6,
        compiler_params=_params(1))(*os_, *ls_)


def _mix_bwd(S, os_, ls_, do_a, tm=512):
    dil = [d for _, d in ATT_GROUPS]

    def body(*refs):
        d_ref, outs, bufs, tmps = refs[6], refs[7:13], refs[13:19], refs[19:25]
        get = [_rm_reader(bufs[j], refs[j], dil[j % 3]) for j in range(6)]
        for s in range(ATT_HPG):
            cols = slice(s * HD, (s + 1) * HD)

            def chunk(rows, s=s, cols=cols):
                al = _mix_alpha(*[get[3 + g](s, rows) for g in range(3)])
                dv = d_ref[rows, cols]
                o_a = al[0] * get[0](s, rows) + al[1] * get[1](s, rows) + al[2] * get[2](s, rows)
                dsum = jnp.sum(dv * o_a, axis=-1, keepdims=True)
                for g in range(3):
                    tmps[g][rows, :] = al[g] * dv
                    tmps[3 + g][rows, :] = -(al[g] * dsum)

            chunk(slice(None))
            for j in range(6):
                _rm_put(outs[j], cols, tmps[j], dil[j % 3])

    specs = [_rm_spec(tm, d, GW) for d in dil]
    res = pl.pallas_call(
        body, name="mix_bwd", grid=(S // tm,), in_specs=specs * 2 + [pl.BlockSpec((tm, GW), lambda i: (i, 0))],
        out_specs=specs * 2,
        out_shape=[_sds(_rm_shape(S, d, GW), CDT) for d in dil] + [_sds(_rm_shape(S, d, GW), F32) for d in dil],
        scratch_shapes=[pltpu.VMEM((ATT_HPG, tm, HD), F32)] * 6 + [pltpu.VMEM((tm, HD), F32)] * 6,
        compiler_params=_params(1))(*os_, *ls_, do_a)
    return res[:3], res[3:]


def _ret_tables(dk):
    H, C = RET_HEADS, BLK
    log_g = jnp.log(1.0 - 2.0 ** (-5.0 - jnp.arange(H, dtype=F32)))
    idx = jnp.arange(C, dtype=F32)
    diff = idx[:, None] - idx[None, :]
    decay = jnp.where(diff >= 0, jnp.exp(log_g[:, None, None] * jnp.maximum(diff, 0.0)), 0.0)
    xi = jnp.exp(log_g[:, None] * (idx[None, :] + 1.0))
    zeta = jnp.exp(log_g[:, None] * (C - 1.0 - idx[None, :])) * (dk ** -0.5)
    g_chunk = jnp.exp(log_g * C)
    bc = lambda t: jnp.broadcast_to(t[:, :, None], (H, C, C))
    return decay, bc(xi), bc(zeta), jnp.broadcast_to(g_chunk[:, None, None], (H, 8, C))


def _gn_fwd(o, g, b):
    mu = jnp.mean(o, axis=-1, keepdims=True)
    xc = o - mu
    rstd = lax.rsqrt(jnp.mean(xc * xc, axis=-1, keepdims=True) + EPS)
    yh = xc * rstd
    return yh, rstd, yh * g + b


def _ret_specs(dk, dv, order):
    H = RET_HEADS
    qk_w, v_w = H * dk, H * dv
    off_q = 3 * ATT_W
    off_k, off_v, off_g = off_q + qk_w, off_q + 2 * qk_w, off_q + 2 * qk_w + v_w
    assert 2 * dk == dv and all(off % dv == 0 for off in (off_q, off_k, off_v, off_g))

    def col(off, j):
        return pl.BlockSpec((BLK, dv), lambda i: (order(i), off // dv + j))

    tab = pl.BlockSpec((H, BLK, BLK), lambda i: (0, 0, 0))
    return ([col(off_q, j) for j in range(H // 2)] + [col(off_k, j) for j in range(H // 2)]
            + [col(off_v, j) for j in range(H)] + [col(off_g, j) for j in range(H)]
            + [tab, tab, tab, pl.BlockSpec((H, 8, BLK), lambda i: (0, 0, 0))])


def _ret_heads(refs, dk):
    H = RET_HEADS
    q_refs, k_refs = refs[0:H // 2], refs[H // 2:H]
    v_refs, gr_refs = refs[H:2 * H], refs[2 * H:3 * H]

    def head(h):
        cols = slice((h % 2) * dk, (h % 2 + 1) * dk)
        return q_refs[h // 2][:, cols], k_refs[h // 2][:, cols], v_refs[h][...], gr_refs[h][...]

    return head, refs[3 * H:3 * H + 4]


def _ret_fwd(proj, gn_g, gn_b, dk, dv):
    S = proj.shape[0]
    N = S // BLK
    H = RET_HEADS
    kscale = dk ** -0.5
    n_in = 3 * H + 4

    def body(*refs):
        head, (dec_ref, xi_ref, zeta_ref, gc_ref) = _ret_heads(refs, dk)
        g_ref, b_ref, opre_ref, or_ref, st_ref, state, s_buf, cross_buf = refs[n_in:n_in + 8]
        n = pl.program_id(0)

        @pl.when(n == 0)
        def _():
            state[...] = jnp.zeros_like(state)

        for h in range(H):
            q, k, v, _ = head(h)
            s_buf[h] = _dot_nt(q, k)
            st = state[h]
            st_c = st.astype(CDT)
            st_ref[h] = st_c
            cross_buf[h] = _dot_nn(q, st_c)
            kz = (k.astype(F32) * zeta_ref[h][:, 0:1]).astype(CDT)
            state[h] = st * gc_ref[h][0:1, 0:1] + _dot_tn(kz, v)
        for h in range(H):
            vs = slice(h * dv, (h + 1) * dv)
            _, _, v, gr = head(h)
            s = s_buf[h] * kscale * dec_ref[h]
            o = _dot_nn(s.astype(CDT), v) + cross_buf[h] * xi_ref[h][:, 0:1]
            opre_ref[:, vs] = o
            _, _, y = _gn_fwd(o, g_ref[:, vs], b_ref[:, vs])
            gr = gr.astype(F32)
            or_ref[:, vs] = (y * (gr * _sigmoid(gr))).astype(or_ref.dtype)

    v_w = H * dv
    row = pl.BlockSpec((1, v_w), lambda i: (0, 0))
    tile = pl.BlockSpec((BLK, v_w), lambda i: (i, 0))
    return pl.pallas_call(
        body, name="ret_fwd", grid=(N,),
        in_specs=_ret_specs(dk, dv, lambda i: i) + [row, row],
        out_specs=[tile, tile, pl.BlockSpec((None, H, dk, dv), lambda i: (i, 0, 0, 0))],
        out_shape=[_sds((S, v_w), F32), _sds((S, v_w), CDT), _sds((N, H, dk, dv), CDT)],
        scratch_shapes=[pltpu.VMEM((H, dk, dv), F32), pltpu.VMEM((H, BLK, BLK), F32), pltpu.VMEM((H, BLK, dv), F32)],
        compiler_params=_params(1),
    )(*[proj] * (3 * H), *_ret_tables(dk), gn_g, gn_b)


def _ret_bwd(proj, gn_g, gn_b, o_pre, states, d_or, dga, dgb, dk, dv):
    S, in_w = proj.shape
    N = S // BLK
    H = RET_HEADS
    qk_w, v_w = H * dk, H * dv
    kscale = dk ** -0.5
    n_in = 3 * H + 4
    out_w = 2 * qk_w + 2 * v_w
    gate_w = dga.shape[1]
    col0 = 3 * ATT_W
    assert col0 + out_w + 2 * gate_w == in_w
    rev = lambda i: N - 1 - i

    def body(*refs):
        head, (dec_ref, xi_ref, zeta_ref, gc_ref) = _ret_heads(refs, dk)
        (g_ref, b_ref, opre_ref, st_ref, dor_ref, dga_ref, dgb_ref, dproj_ref, dg_ref, db_ref, dstate, stage,
         sem, do_buf, dox_buf, a_buf, g_buf, dq_buf, dk_buf, dv_buf) = refs[n_in:n_in + 20]
        i = pl.program_id(0)
        slot = i % 2
        out_ref = stage.at[slot]

        def out_copy(s, step):
            rows = pl.ds(pl.multiple_of(rev(step) * BLK, BLK), BLK)
            return pltpu.make_async_copy(stage.at[s], dproj_ref.at[rows, pl.ds(col0, in_w - col0)], sem.at[s])

        @pl.when(i >= 2)
        def _():
            out_copy(slot, i - 2).wait()

        @pl.when(i == 0)
        def _():
            dstate[...] = jnp.zeros_like(dstate)
            dg_ref[...] = jnp.zeros_like(dg_ref)
            db_ref[...] = jnp.zeros_like(db_ref)

        out_ref[:, out_w:out_w + gate_w] = dga_ref[...]
        out_ref[:, out_w + gate_w:out_w + 2 * gate_w] = dgb_ref[...]
        for h in range(H):
            vs = slice(h * dv, (h + 1) * dv)
            _, _, _, gr = head(h)
            gr = gr.astype(F32)
            sg = _sigmoid(gr)
            gain = g_ref[:, vs]
            yh, rstd, y = _gn_fwd(opre_ref[:, vs], gain, b_ref[:, vs])
            d_or_v = dor_ref[:, vs]
            dy = d_or_v * (gr * sg)
            out_ref[:, 2 * qk_w + v_w + h * dv:2 * qk_w + v_w + (h + 1) * dv] = (
                d_or_v * y * (sg * (1.0 + gr * (1.0 - sg)))).astype(out_ref.dtype)
            dg_ref[:, vs] += jnp.sum(dy * yh, axis=0, keepdims=True)
            db_ref[:, vs] += jnp.sum(dy, axis=0, keepdims=True)
            dyh = dy * gain
            do = rstd * (dyh - jnp.mean(dyh, axis=-1, keepdims=True)
                         - yh * jnp.mean(dyh * yh, axis=-1, keepdims=True))
            do_buf[h] = do.astype(CDT)
            dox_buf[h] = (do * xi_ref[h][:, 0:1]).astype(CDT)
        for h in range(H):
            q, k, v, _ = head(h)
            dox = dox_buf[h]
            a_buf[h] = _dot_nt(q, k)
            g_buf[h] = _dot_nt(do_buf[h], v)
            dsn = dstate[h]
            dsn_c = dsn.astype(CDT)
            kz = (k.astype(F32) * zeta_ref[h][:, 0:1]).astype(CDT)
            dq_buf[h] = _dot_nt(dox, st_ref[h])
            dk_buf[h] = _dot_nt(v, dsn_c)
            dv_buf[h] = _dot_nn(kz, dsn_c)
            dstate[h] = dsn * gc_ref[h][0:1, 0:1] + _dot_tn(q, dox)
        for h in range(H):
            q, k, _, _ = head(h)
            decay = dec_ref[h]
            a_c = (a_buf[h] * kscale * decay).astype(CDT)
            g_c = (g_buf[h] * decay).astype(CDT)
            dq = _dot_nn(g_c, k) * kscale + dq_buf[h]
            dkk = _dot_tn(g_c, q) * kscale + dk_buf[h] * zeta_ref[h][:, 0:1]
            dvv = _dot_tn(a_c, do_buf[h]) + dv_buf[h]
            out_ref[:, h * dk:(h + 1) * dk] = dq.astype(out_ref.dtype)
            out_ref[:, qk_w + h * dk:qk_w + (h + 1) * dk] = dkk.astype(out_ref.dtype)
            out_ref[:, 2 * qk_w + h * dv:2 * qk_w + (h + 1) * dv] = dvv.astype(out_ref.dtype)

        cp = out_copy(slot, i)
        cp.start()

        @pl.when(i == N - 1)
        def _():
            cp.wait()
            if N >= 2:
                out_copy(1 - slot, i - 1).wait()

    row = pl.BlockSpec((1, v_w), lambda i: (0, 0))
    tile = pl.BlockSpec((BLK, v_w), lambda i: (rev(i), 0))
    gate = pl.BlockSpec((BLK, gate_w), lambda i: (rev(i), 0))
    return pl.pallas_call(
        body, name="ret_bwd", grid=(N,),
        in_specs=_ret_specs(dk, dv, rev) + [row, row, tile,
                 pl.BlockSpec((None, H, dk, dv), lambda i: (rev(i), 0, 0, 0)), tile, gate, gate],
        out_specs=[pl.BlockSpec(memory_space=pl.ANY), row, row],
        out_shape=[_sds((S, in_w), CDT), _sds((1, v_w), F32), _sds((1, v_w), F32)],
        scratch_shapes=[pltpu.VMEM((H, dk, dv), F32), pltpu.VMEM((2, BLK, in_w - col0), CDT),
                        pltpu.SemaphoreType.DMA((2,)),
                        pltpu.VMEM((H, BLK, dv), CDT), pltpu.VMEM((H, BLK, dv), CDT),
                        pltpu.VMEM((H, BLK, BLK), F32), pltpu.VMEM((H, BLK, BLK), F32),
                        pltpu.VMEM((H, BLK, dk), F32), pltpu.VMEM((H, BLK, dk), F32), pltpu.VMEM((H, BLK, dv), F32)],
        compiler_params=_params(1),
    )(*[proj] * (3 * H), *_ret_tables(dk), gn_g, gn_b, o_pre, states, d_or, dga, dgb)


def _merge_fwd(o_a, o_r, wa, wb, proj, d_model, tm=1024, tn=512):
    S, in_w = proj.shape
    off_a, off_b = in_w - 2 * d_model, in_w - d_model
    assert off_a % tn == 0 and off_b % tn == 0

    def body(oa_ref, or_ref, wa_ref, wb_ref, ga_ref, gb_ref, y_ref, pa_ref, pb_ref):
        pa = _dot_nn(oa_ref[...], wa_ref[...])
        pb = _dot_nn(or_ref[...], wb_ref[...])
        y = _sigmoid(ga_ref[...].astype(F32)) * pa + _sigmoid(gb_ref[...].astype(F32)) * pb
        y_ref[...] = y.astype(y_ref.dtype)
        pa_ref[...] = pa.astype(pa_ref.dtype)
        pb_ref[...] = pb.astype(pb_ref.dtype)

    ka, kb = o_a.shape[1], o_r.shape[1]
    out = pl.BlockSpec((tm, tn), lambda i, j: (i, j))
    return pl.pallas_call(
        body, name="merge_fwd", grid=(S // tm, d_model // tn),
        in_specs=[pl.BlockSpec((tm, ka), lambda i, j: (i, 0)), pl.BlockSpec((tm, kb), lambda i, j: (i, 0)),
                  pl.BlockSpec((ka, tn), lambda i, j: (0, j)), pl.BlockSpec((kb, tn), lambda i, j: (0, j)),
                  pl.BlockSpec((tm, tn), lambda i, j: (i, off_a // tn + j)),
                  pl.BlockSpec((tm, tn), lambda i, j: (i, off_b // tn + j))],
        out_specs=[out, out, out], out_shape=[_sds((S, d_model), CDT)] * 3,
        compiler_params=_params(2))(o_a, o_r, wa, wb, proj, proj)


def _local_step(x, target, w_in_mine, chip, others, near_w_in, far_w_in, small, late_weights, on_grads, deps0=()):
    S, D = x.shape
    ns_in = w_in_mine.shape[1]
    in_w = N_CHIPS * ns_in
    d_ff = 4 * D
    ret_v_w = 2 * D
    dv = ret_v_w // RET_HEADS
    dk = (in_w - 3 * ATT_W - 2 * ret_v_w - 2 * D) // (2 * RET_HEADS)
    gqk = jnp.concatenate([small["q_norm_g"].reshape(1, ATT_W), small["k_norm_g"].reshape(1, ATT_W)], axis=1)
    g1, g2 = small["norm1_g"], small["norm2_g"]
    gn_g, gn_b = small["ret_gn_g"], small["ret_gn_b"]

    proj_sds = _sds((S, in_w), CDT)
    xn, proj = _in_proj_mine(x, g1, w_in_mine, chip, proj_sds, deps0)
    for stage, (get_w_in, chips) in enumerate(((near_w_in, others[:2]), (far_w_in, others[2:]))):
        w_in, started = get_w_in(proj)
        (proj,) = _matmul(
            "in_proj_far%d" % stage, "nn", xn, w_in, tm=512, tn=ns_in, tk=D, prefetch=[chips],
            n_cols=chips.shape[0] * ns_in, b_spec=pl.BlockSpec((None, D, ns_in), lambda i, j, k, o: (o[j], 0, 0)),
            outs=[(proj_sds, pl.BlockSpec((512, ns_in), lambda i, j, k, o: (i, o[j])))], epilogue=_ep_store,
            deps=[proj, started], alias_dep_to_out=(0, 0), j_outer=True)
    qkv = _qknorm_fwd(proj, gqk)
    att = [_att_fwd(g, S, qkv[g]) for g in range(3)]
    os_, ls_ = [a[0] for a in att], [a[1] for a in att]
    o_a = _mix_fwd(S, os_, ls_)
    o_pre, o_r, states = _ret_fwd(proj, gn_g, gn_b, dk, dv)
    w = late_weights(o_r)
    y, pa, pb = _merge_fwd(o_a, o_r, w["w_proj_a"], w["w_proj_b"], proj, D)
    x1, xn2 = _matmul("out_proj", "nn", y, w["w_out"], tm=1024, tn=D, tk=D,
                      extras=[(x, _mn(1024, D)), (g2, _row(D))],
                      outs=[(_sds((S, D), F32), _mn(1024, D)), (_sds((S, D), CDT), _mn(1024, D))],
                      epilogue=_ep_resid_norm)
    hid, act = _matmul("mlp_up", "nn", xn2, w["w_up"], tm=512, tn=2048, tk=D, j_outer=True,
                       outs=[(_sds((S, d_ff), CDT), _mn(512, 2048))] * 2, epilogue=_ep_up)
    dx2, dx2c, loss_row = _matmul(
        "mlp_down_loss", "nn", act, w["w_down"], tm=512, tn=D, tk=d_ff,
        extras=[(x1, _mn(512, D)), (target, _mn(512, D))],
        outs=[(_sds((S, D), F32), _mn(512, D)), (_sds((S, D), CDT), _mn(512, D)), (_sds((1, D), F32), _row(D))],
        epilogue=functools.partial(_ep_down_loss, inv_d=1.0 / D))
    loss = 0.5 * jnp.sum(loss_row) / D

    (dh,) = _matmul("d_hidden", "nt", dx2c, w["w_down"], tm=512, tn=2048, tk=D, j_outer=True,
                    extras=[(hid, _mn(512, 2048))], outs=[(_sds((S, d_ff), CDT), _mn(512, 2048))], epilogue=_ep_dh)
    (gw_down,) = _matmul("dw_down", "tn", act, dx2c, tm=1024, tn=D, tk=1024,
                         outs=[(_sds((d_ff, D), F32), _mn(1024, D))], epilogue=_ep_store)
    (gw_up,) = _matmul("dw_up", "tn", xn2, dh, tm=D, tn=1024, tk=1024,
                       outs=[(_sds((D, d_ff), F32), _mn(D, 1024))], epilogue=_ep_store)
    tok = on_grads({"w_down": gw_down, "w_up": gw_up})
    dx1, dx1c, dg2 = _matmul(
        "d_x1", "nt", dh, w["w_up"], tm=512, tn=D, tk=d_ff,
        extras=[(x1, _mn(512, D)), (g2, _row(D)), (dx2, _mn(512, D))],
        outs=[(_sds((S, D), F32), _mn(512, D)), (_sds((S, D), CDT), _mn(512, D)), (_sds((1, D), F32), _row(D))],
        epilogue=_ep_rms_bwd, deps=[tok])

    gt = 512
    assert (in_w - 2 * D) % gt == 0
    off_a, off_b = (in_w - 2 * D) // gt, (in_w - D) // gt
    dpa, dpb, dga, dgb = _matmul(
        "d_gates", "nt", dx1c, w["w_out"], tm=1024, tn=gt, tk=D,
        extras=[(proj, _mn(1024, gt, off_a)), (proj, _mn(1024, gt, off_b)), (pa, _mn(1024, gt)),
                (pb, _mn(1024, gt))],
        outs=[(_sds((S, D), CDT), _mn(1024, gt))] * 4, epilogue=_ep_gates)
    (gw_out,) = _matmul("dw_out", "tn", y, dx1c, tm=D, tn=D, tk=1024,
                        outs=[(_sds((D, D), F32), _mn(D, D))], epilogue=_ep_store)
    (gw_pa,) = _matmul("dw_proj_a", "tn", o_a, dpa, tm=GW, tn=D, tk=1024,
                       outs=[(_sds((GW, D), F32), _mn(GW, D))], epilogue=_ep_store)
    (gw_pb,) = _matmul("dw_proj_b", "tn", o_r, dpb, tm=1024, tn=D, tk=1024,
                       outs=[(_sds((ret_v_w, D), F32), _mn(1024, D))], epilogue=_ep_store)
    (do_a,) = _matmul("d_o_a", "nt", dpa, w["w_proj_a"], tm=1024, tn=GW, tk=D,
                      outs=[(_sds((S, GW), F32), _mn(1024, GW))], epilogue=_ep_store)
    tok = on_grads({"w_out": gw_out, "w_proj_a": gw_pa, "w_proj_b": gw_pb})
    (d_or,) = _matmul("d_o_r", "nt", dpb, w["w_proj_b"], tm=512, tn=ret_v_w, tk=D,
                      outs=[(_sds((S, ret_v_w), F32), _mn(512, ret_v_w))], epilogue=_ep_store, deps=[tok])

    dproj, dgn_g, dgn_b = _ret_bwd(proj, gn_g, gn_b, o_pre, states, d_or, dga, dgb, dk, dv)
    do_gs, c_gs = _mix_bwd(S, os_, ls_, do_a)
    datt_parts = [_att_bwd(g, S, qkv[g], ls_[g], do_gs[g], c_gs[g]) for g in range(3)]
    dproj, dgqk = _qknorm_bwd(proj, gqk, [p[0] for p in datt_parts], [p[1] for p in datt_parts],
                              [p[2] for p in datt_parts], dproj)

    (gw_in,) = _matmul(
        "dw_in", "tn", xn, dproj, tm=512, tn=ns_in, tk=1024,
        outs=[(_sds((N_CHIPS, D, ns_in), F32), pl.BlockSpec((None, 512, ns_in), lambda i, j, k: (j, i, 0)))],
        epilogue=_ep_store)
    tok = on_grads({"w_in": gw_in})
    grad_x, dg1 = _matmul(
        "d_x", "nt", dproj, w_in, tm=512, tn=D, tk=ns_in, n_cols=D,
        b_spec=pl.BlockSpec((None, D, ns_in), lambda i, j, k: (k, 0, 0)),
        extras=[(x, _mn(512, D)), (g1, _row(D)), (dx1, _mn(512, D))],
        outs=[(_sds((S, D), F32), _mn(512, D)), (_sds((1, D), F32), _row(D))],
        epilogue=_ep_rms_bwd, deps=[tok])

    smallg = {"norm1_g": dg1, "q_norm_g": dgqk[:, :ATT_W], "k_norm_g": dgqk[:, ATT_W:],
              "ret_gn_g": dgn_g, "ret_gn_b": dgn_b, "norm2_g": dg2}
    return loss, grad_x, smallg


N_CHIPS = 4
N_DEV = 8


def _place():
    x, y, c = lax.axis_index("x"), lax.axis_index("y"), lax.axis_index("c")
    return x, y, c


def _other_chips(x, y):
    out = []
    for fx, fy in ((1, 0), (0, 1), (1, 1)):
        px = 1 - x if fx else x
        py = 1 - y if fy else y
        out.append(((px, py), 2 * px + py))
    return out


SEM_SPEC = pl.BlockSpec(memory_space=pltpu.SEMAPHORE)
ANY_SPEC = pl.BlockSpec(memory_space=pl.ANY)
EFFECT = pltpu.SideEffectType.DATAFLOW_SIDE_EFFECTING


def _ici_copies(kind, srcs, lands, send, recv, which=(0, 1, 2)):
    x, y, c = _place()
    me = 2 * x + y
    out = []
    for w, (s, l) in enumerate(zip(srcs, lands)):
        for j, ((px, py), pidx) in enumerate(_other_chips(x, y)):
            if j not in which:
                continue
            if kind == "gather":
                half = s.shape[0] // 2
                rows = pl.ds(c * half, half)
                src, dst_there, dst_here = s.at[rows, :], l.at[me, rows, :], l.at[pidx, rows, :]
            else:
                src, dst_there, dst_here = s.at[pidx], l.at[me], l.at[pidx]
            out.append((src, dst_there, dst_here, send.at[3 * w + j], recv.at[3 * w + j], (px, py, c)))
    return out


def _exchange_start(name, kind, srcs, land_shapes, which=(0, 1, 2), lands=None):
    n = len(srcs)
    if lands is None:
        lands = [lax.empty(shape, dtype) for shape, dtype in land_shapes]

    def body(*refs):
        src_refs, land_refs = refs[:n], refs[n:2 * n]
        send, recv = refs[2 * n], refs[2 * n + 1]
        token = refs[-1]
        for src, dst, _, ss, rs, dev in _ici_copies(kind, src_refs, land_refs, send, recv, which):
            pltpu.make_async_remote_copy(src_ref=src, dst_ref=dst, send_sem=ss, recv_sem=rs, device_id=dev,
                                         device_id_type=MESH).start()
        token[...] = jnp.zeros_like(token)

    thru = [pltpu.HBM(s.shape, s.dtype) for s in srcs] + [pltpu.HBM(shape, dtype) for shape, dtype in land_shapes]
    res = pl.pallas_call(
        body, name=name,
        out_shape=(pltpu.SemaphoreType.DMA((3 * n,)), pltpu.SemaphoreType.DMA((3 * n,)), *thru, _sds((8, LANES), F32)),
        in_specs=[HBM_SPEC] * (2 * n), out_specs=(SEM_SPEC, SEM_SPEC, *[HBM_SPEC] * (2 * n), VMEM_SPEC),
        input_output_aliases={i: 2 + i for i in range(2 * n)},
        compiler_params=pltpu.CompilerParams(has_side_effects=EFFECT),
    )(*[pltpu.with_memory_space_constraint(s, pltpu.HBM) for s in srcs],
      *[pltpu.with_memory_space_constraint(l, pltpu.HBM) for l in lands])
    return res[0], res[1], list(res[2:2 + n]), list(res[2 + n:2 + 2 * n]), res[-1]


def _exchange_wait(name, kind, send, recv, srcs, lands, after, which=(0, 1, 2)):
    n = len(srcs)

    def body(*refs):
        src_refs, land_refs = refs[:n], refs[n:2 * n]
        send_ref, recv_ref = refs[2 * n], refs[2 * n + 1]
        for src, _, dst, ss, rs, dev in _ici_copies(kind, src_refs, land_refs, send_ref, recv_ref, which):
            cp = pltpu.make_async_remote_copy(src_ref=src, dst_ref=dst, send_sem=ss, recv_sem=rs, device_id=dev,
                                              device_id_type=MESH)
            cp.wait_send()
            cp.wait_recv()

    thru = [pltpu.HBM(t.shape, t.dtype) for t in list(srcs) + list(lands)]
    res = pl.pallas_call(
        body, name=name, out_shape=thru,
        in_specs=[HBM_SPEC] * (2 * n) + [SEM_SPEC, SEM_SPEC, ANY_SPEC], out_specs=[HBM_SPEC] * (2 * n),
        input_output_aliases={i: i for i in range(2 * n)},
        compiler_params=pltpu.CompilerParams(has_side_effects=EFFECT),
    )(*srcs, *lands, send, recv, after)
    return list(res[:n]), list(res[n:])


PAIR_TILE_ELEMS = 1 << 19


def _pair_fill(name, gathered, mine, core, others, chip, write_mine=True):
    k, r, C = gathered.shape
    half = r // 2
    tr = _row_tile(half, C, PAIR_TILE_ELEMS, mult=16)
    nt = half // tr
    n_far = others.shape[0]

    def body(c_ref, o_ref, chip_ref, in_ref, mine_ref, out_ref, slot, send, recv):
        j = pl.program_id(0)
        _sibling_barrier((j == 0) & (pl.program_id(1) == 0))
        b = (j * nt + pl.program_id(1)) % 2
        x, y, c = _place()
        cp = pltpu.make_async_remote_copy(src_ref=in_ref, dst_ref=slot.at[b], send_sem=send.at[b],
                                          recv_sem=recv.at[b], device_id=(x, y, 1 - c), device_id_type=MESH)

        @pl.when(j < n_far)
        def _():
            cp.start()
            cp.wait_recv()
            out_ref[...] = slot[b]
            cp.wait_send()

        @pl.when(j >= n_far)
        def _():
            out_ref[...] = mine_ref[...]

    def far(j):
        return jnp.minimum(j, n_far - 1)

    grid_spec = pltpu.PrefetchScalarGridSpec(
        num_scalar_prefetch=3, grid=(n_far + (2 if write_mine else 0), nt),
        in_specs=[pl.BlockSpec((tr, C), lambda j, i, c, o, m: (
                      (2 * o[far(j)] + c[0]) * nt + jnp.where(j < n_far, i, nt - 1), 0)),
                  pl.BlockSpec((tr, C), lambda j, i, c, o, m: (jnp.where(j < n_far, 0, (j - n_far) * nt + i), 0))],
        out_specs=pl.BlockSpec((tr, C), lambda j, i, c, o, m: (
            jnp.where(j < n_far, 2 * o[far(j)] + 1 - c[0], 2 * m[0] + j - n_far) * nt + i, 0)),
        scratch_shapes=[pltpu.VMEM((2, tr, C), gathered.dtype), pltpu.SemaphoreType.DMA((2,)),
                        pltpu.SemaphoreType.DMA((2,))])
    out = pl.pallas_call(body, name=name, grid_spec=grid_spec, out_shape=_sds((k * r, C), gathered.dtype),
                         input_output_aliases={3: 0}, compiler_params=_params(2, PAIR_FILL_ID))(
                             core, others, chip, gathered.reshape(k * r, C), mine)
    return out.reshape(k, r, C)


def _pair_reduce(name, g, core):
    k, R, C = g.shape
    half = R // 2
    tr = _row_tile(half, C, PAIR_TILE_ELEMS, mult=16)
    nt = half // tr

    def body(c_ref, mine_ref, give_ref, out_ref, wire_ref, stage, slot, send, recv):
        _sibling_barrier((pl.program_id(0) == 0) & (pl.program_id(1) == 0))
        b = (pl.program_id(0) * nt + pl.program_id(1)) % 2
        x, y, c = _place()
        stage[b] = give_ref[...].astype(stage.dtype)
        cp = pltpu.make_async_remote_copy(src_ref=stage.at[b], dst_ref=slot.at[b], send_sem=send.at[b],
                                          recv_sem=recv.at[b], device_id=(x, y, 1 - c), device_id_type=MESH)
        cp.start()
        cp.wait_recv()
        tot = mine_ref[...] + slot[b].astype(F32)
        out_ref[...] = tot
        wire_ref[...] = tot.astype(wire_ref.dtype)
        cp.wait_send()

    blk = (tr, C)
    out_spec = pl.BlockSpec(blk, lambda s, i, c: (s * nt + i, 0))
    grid_spec = pltpu.PrefetchScalarGridSpec(
        num_scalar_prefetch=1, grid=(k, nt),
        in_specs=[pl.BlockSpec(blk, lambda s, i, c: ((2 * s + c[0]) * nt + i, 0)),
                  pl.BlockSpec(blk, lambda s, i, c: ((2 * s + 1 - c[0]) * nt + i, 0))],
        out_specs=[out_spec, out_spec],
        scratch_shapes=[pltpu.VMEM((2, tr, C), CDT), pltpu.VMEM((2, tr, C), CDT), pltpu.SemaphoreType.DMA((2,)),
                        pltpu.SemaphoreType.DMA((2,))])
    g2 = g.reshape(k * R, C)
    out, wire = pl.pallas_call(body, name=name, grid_spec=grid_spec,
                               out_shape=[_sds((k * half, C), F32), _sds((k * half, C), CDT)],
                               compiler_params=_params(2, PAIR_REDUCE_ID))(core, g2, g2)
    return out, wire.reshape(k, half, C)


def _all_reduce_small(v):
    r, cdim = v.shape

    def body(v_ref, o_ref, buf, send, recv):
        x, y, c = _place()
        me = 4 * x + 2 * y + c
        buf[me] = v_ref[...]
        sends = []
        for m in range(1, N_DEV):
            px = 1 - x if m & 4 else x
            py = 1 - y if m & 2 else y
            pc = 1 - c if m & 1 else c
            cp = pltpu.make_async_remote_copy(src_ref=v_ref, dst_ref=buf.at[me], send_sem=send.at[m - 1],
                                              recv_sem=recv.at[m - 1], device_id=(px, py, pc), device_id_type=MESH)
            cp.start()
            sends.append((cp, 4 * px + 2 * py + pc))
        for m, (cp, pidx) in enumerate(sends):
            pltpu.make_async_remote_copy(src_ref=v_ref, dst_ref=buf.at[pidx], send_sem=send.at[m], recv_sem=recv.at[m],
                                         device_id=(x, y, c), device_id_type=MESH).wait_recv()
        for cp, _ in sends:
            cp.wait_send()
        tot = buf[0]
        for k in range(1, N_DEV):
            tot = tot + buf[k]
        o_ref[...] = tot

    return pl.pallas_call(
        body, name="all_reduce_small", in_specs=[VMEM_SPEC], out_specs=VMEM_SPEC,
        out_shape=_sds((r, cdim), F32),
        scratch_shapes=[pltpu.VMEM((N_DEV, r, cdim), F32), pltpu.SemaphoreType.DMA((N_DEV - 1,)),
                        pltpu.SemaphoreType.DMA((N_DEV - 1,))],
    )(v)


def _row_tile(rows, cols, budget_elems=1 << 18, mult=8):
    if rows % mult:
        return rows
    t = max(mult, (budget_elems // cols) // mult * mult)
    while rows % t:
        t -= mult
    return t


def _adamw_update(w, g, m, v):
    nm = ADAM_B1 * m + (1.0 - ADAM_B1) * g
    nv = ADAM_B2 * v + (1.0 - ADAM_B2) * (g * g)
    m_hat = nm / (1.0 - ADAM_B1 ** ADAM_STEP)
    v_hat = nv / (1.0 - ADAM_B2 ** ADAM_STEP)
    return -ADAM_LR * (m_hat / (jnp.sqrt(v_hat) + ADAM_EPS) + ADAM_WD * w), nm, nv


def _adamw(name, w, g, m, v):
    R, C = w.shape
    tr = _row_tile(R, C, 1 << 18)

    def body(w_ref, g_ref, m_ref, v_ref, d_ref, nm_ref, nv_ref):
        d_ref[...], nm_ref[...], nv_ref[...] = _adamw_update(w_ref[...], g_ref[...], m_ref[...], v_ref[...])

    spec = pl.BlockSpec((tr, C), lambda i: (i, 0))
    return pl.pallas_call(body, name=name, grid=(R // tr,), in_specs=[spec] * 4, out_specs=[spec] * 3,
                          out_shape=[_sds((R, C), F32)] * 3, compiler_params=_params(1))(w, g, m, v)


def _sum_share(name, own, by_chip, chip, others, core):
    k, half, C = by_chip.shape
    tr = _row_tile(half, C, mult=16)
    nt = half // tr

    def body(chip_ref, oth_ref, c_ref, own_ref, a_ref, b_ref, cc_ref, g_out, mine, slot, send, recv):
        p = pl.program_id(1)
        _sibling_barrier((pl.program_id(0) == 0) & (p == 0))
        b = pl.program_id(0) % 2
        x, y, c = _place()
        cp = pltpu.make_async_remote_copy(src_ref=mine.at[b], dst_ref=slot.at[b], send_sem=send.at[b],
                                          recv_sem=recv.at[b], device_id=(x, y, 1 - c), device_id_type=MESH)

        @pl.when(p == 0)
        def _():
            tot = ((own_ref[...] + a_ref[...].astype(F32)) + b_ref[...].astype(F32)) + cc_ref[...].astype(F32)
            mine[b] = tot
            cp.start()
            g_out[...] = tot

        @pl.when(p == 1)
        def _():
            cp.wait_recv()
            g_out[...] = slot[b]
            cp.wait_send()

    def piece(j):
        return pl.BlockSpec((tr, C), lambda i, p, chip, oth, c: (oth[j] * nt + i, 0))

    grid_spec = pltpu.PrefetchScalarGridSpec(
        num_scalar_prefetch=3, grid=(nt, 2),
        in_specs=[pl.BlockSpec((tr, C), lambda i, p, chip, oth, c: (chip[0] * nt + i, 0)),
                  piece(0), piece(1), piece(2)],
        out_specs=pl.BlockSpec((tr, C), lambda i, p, chip, oth, c: (
            jnp.where(p == 0, c[0], 1 - c[0]) * nt + i, 0)),
        scratch_shapes=[pltpu.VMEM((2, tr, C), F32), pltpu.VMEM((2, tr, C), F32), pltpu.SemaphoreType.DMA((2,)),
                        pltpu.SemaphoreType.DMA((2,))])
    by2 = by_chip.reshape(k * half, C)
    return pl.pallas_call(body, name=name, grid_spec=grid_spec, out_shape=_sds((2 * half, C), F32),
                          compiler_params=_params(2, SUM_SHARE_ID))(chip, others, core, own, by2, by2, by2)


BIG = ("w_in", "w_proj_a", "w_proj_b", "w_out", "w_up", "w_down")
COL_SHARDED = ("w_in", "w_proj_a", "w_up")
SMALL = ("norm1_g", "q_norm_g", "k_norm_g", "ret_gn_g", "ret_gn_b", "norm2_g")
ALL_W = ("norm1_g", "w_in", "q_norm_g", "k_norm_g", "ret_gn_g", "ret_gn_b", "w_proj_a", "w_proj_b", "w_out",
         "norm2_g", "w_up", "w_down")
LANES = 128


def _to_full(name, gathered):
    k, r, c = gathered.shape
    if name in COL_SHARDED:
        return gathered.transpose(1, 0, 2).reshape(r, k * c)
    return gathered.reshape(k * r, c)


def _to_shard_major(name, full):
    if name in COL_SHARDED:
        r, c4 = full.shape
        return full.reshape(r, N_CHIPS, c4 // N_CHIPS).transpose(1, 0, 2)
    r4, c = full.shape
    return full.reshape(N_CHIPS, r4 // N_CHIPS, c)


def kernel(x, norm1_g, w_in, q_norm_g, k_norm_g, ret_gn_g, ret_gn_b, w_proj_a, w_proj_b, w_out, norm2_g, w_up, w_down, loss_target, m_norm1_g, m_w_in, m_q_norm_g, m_k_norm_g, m_ret_gn_g, m_ret_gn_b, m_w_proj_a, m_w_proj_b, m_w_out, m_norm2_g, m_w_up, m_w_down, v_norm1_g, v_w_in, v_q_norm_g, v_k_norm_g, v_ret_gn_g, v_ret_gn_b, v_w_proj_a, v_w_proj_b, v_w_out, v_norm2_g, v_w_up, v_w_down):
    weights = dict(norm1_g=norm1_g, w_in=w_in, q_norm_g=q_norm_g, k_norm_g=k_norm_g, ret_gn_g=ret_gn_g,
                   ret_gn_b=ret_gn_b, w_proj_a=w_proj_a, w_proj_b=w_proj_b, w_out=w_out, norm2_g=norm2_g,
                   w_up=w_up, w_down=w_down)
    moments_m = dict(norm1_g=m_norm1_g, w_in=m_w_in, q_norm_g=m_q_norm_g, k_norm_g=m_k_norm_g, ret_gn_g=m_ret_gn_g,
                     ret_gn_b=m_ret_gn_b, w_proj_a=m_w_proj_a, w_proj_b=m_w_proj_b, w_out=m_w_out,
                     norm2_g=m_norm2_g, w_up=m_w_up, w_down=m_w_down)
    moments_v = dict(norm1_g=v_norm1_g, w_in=v_w_in, q_norm_g=v_q_norm_g, k_norm_g=v_k_norm_g, ret_gn_g=v_ret_gn_g,
                     ret_gn_b=v_ret_gn_b, w_proj_a=v_w_proj_a, w_proj_b=v_w_proj_b, w_out=v_w_out,
                     norm2_g=v_norm2_g, w_up=v_w_up, w_down=v_w_down)

    mx, my = lax.axis_index("x"), lax.axis_index("y")
    core = lax.axis_index("c").astype(jnp.int32).reshape(1)
    chip = (2 * mx + my).astype(jnp.int32).reshape(1)
    others = jnp.stack([2 * (1 - mx) + my, 2 * mx + 1 - my, 2 * (1 - mx) + 1 - my]).astype(jnp.int32)
    shards = {n: weights[n][0].astype(CDT) for n in BIG}
    def start_gather(name, names):
        return _exchange_start(name, "gather", [shards[n] for n in names],
                               [((N_CHIPS,) + shards[n].shape, CDT) for n in names])

    w_in_shape = [((N_CHIPS,) + shards["w_in"].shape, CDT)]
    n_send, n_recv, n_srcs, n_lands, n_token = _exchange_start(
        "gather_w_in_near_start", "gather", [shards["w_in"]], w_in_shape, which=(0, 1))
    late = [n for n in BIG if n != "w_in"]
    flight = {}

    def near_w_in(after):
        srcs, lands = _exchange_wait("gather_w_in_near_wait", "gather", n_send, n_recv, n_srcs, n_lands, after,
                                     which=(0, 1))
        d_send, d_recv, d_srcs, d_lands, _ = _exchange_start(
            "gather_w_in_diag_start", "gather", srcs, w_in_shape, which=(2,), lands=lands)
        flight["late"] = start_gather("gather_late_start", late)
        w_near = _pair_fill("pair_fill_w_in_near", d_lands[0], d_srcs[0], core, others[:2], chip)
        flight["diag"] = (d_send, d_recv, d_srcs, [w_near])
        return w_near, flight["late"][-1]

    def far_w_in(after):
        d_send, d_recv, d_srcs, d_lands = flight["diag"]
        srcs, lands = _exchange_wait("gather_w_in_diag_wait", "gather", d_send, d_recv, d_srcs, d_lands, after,
                                     which=(2,))
        return _pair_fill("pair_fill_w_in_diag", lands[0], srcs[0], core, others[2:], chip, write_mine=False), None

    def late_weights(after):
        l_send, l_recv, l_srcs, l_lands, _ = flight["late"]
        srcs, lands = _exchange_wait("gather_late_wait", "gather", l_send, l_recv, l_srcs, l_lands, after)
        out = {}
        for n, mine, land in zip(late, srcs, lands):
            out[n] = _to_full(n, _pair_fill("pair_fill_%s" % n, land, mine, core, others, chip))
        return out

    pending = []

    def on_grads(group):
        names = list(group)
        red = [_pair_reduce("pair_reduce_%s" % n, g if g.ndim == 3 else _to_shard_major(n, g), core)
               for n, g in group.items()]
        wires = [wire for _, wire in red]
        send, recv, srcs, lands, token = _exchange_start(
            "scatter_start_%s" % names[0], "scatter", wires, [(wire.shape, wire.dtype) for wire in wires])
        pending.append((names, [own for own, _ in red], send, recv, srcs, lands))
        return token

    small = {n: weights[n].reshape(1, -1) for n in SMALL}

    loss, grad_x, small_g = _local_step(x[0], loss_target[0], n_srcs[0], chip, others, near_w_in, far_w_in, small,
                                        late_weights, on_grads, deps0=[n_token])
    loss = lax.psum(loss, ("x", "y", "c"))

    out_g, out_d, out_m, out_v = {}, {}, {}, {}
    for names, owns, send, recv, srcs, lands in pending:
        _, got = _exchange_wait("scatter_wait_%s" % names[0], "scatter", send, recv, srcs, lands, grad_x)
        for n, own, by_chip in zip(names, owns, got):
            shape = weights[n].shape
            g2 = _sum_share("sum_share_%s" % n, own, by_chip, chip, others, core)
            d, nm, nv = _adamw("adamw_%s" % n, weights[n][0], g2, moments_m[n][0], moments_v[n][0])
            out_g[n], out_d[n], out_m[n], out_v[n] = (t.reshape(shape) for t in (g2, d, nm, nv))

    packed = jnp.concatenate([small_g[n].reshape(1, -1) for n in SMALL], axis=1)
    red = _all_reduce_small(packed.reshape(-1, LANES)).reshape(1, -1)
    off = 0
    for n in SMALL:
        shape = weights[n].shape
        row = (1, weights[n].size)
        g2 = red[:, off:off + row[1]]
        off += row[1]
        d, nm, nv = _adamw("adamw_%s" % n, weights[n].reshape(row), g2, moments_m[n].reshape(row),
                           moments_v[n].reshape(row))
        out_g[n], out_d[n], out_m[n], out_v[n] = (t.reshape(shape) for t in (g2, d, nm, nv))

    return (loss, grad_x[None], *[out_g[n] for n in ALL_W], *[out_d[n] for n in ALL_W],
            *[out_m[n] for n in ALL_W], *[out_v[n] for n in ALL_W])
```

```python
import functools

import jax
import jax.numpy as jnp
from jax import lax
from jax.experimental import pallas as pl
from jax.experimental.pallas import tpu as pltpu

CDT = jnp.bfloat16
F32 = jnp.float32
EPS = 1e-6

ATT_GROUPS = ((128, 1), (512, 4), (2048, 16))
ATT_HPG = 4
ATT_HEADS = 12
HD = 128
BLK = 128
ATT_W = ATT_HEADS * HD
GW = ATT_HPG * HD
RET_HEADS = 4

ADAM_LR = 0.001
ADAM_B1 = 0.9
ADAM_B2 = 0.999
ADAM_EPS = 1e-08
ADAM_WD = 0.01
ADAM_STEP = 10

VMEM_LIMIT_BYTES = 56 * 1024 * 1024
MESH = pl.DeviceIdType.MESH
HBM_SPEC = pl.BlockSpec(memory_space=pltpu.HBM)
VMEM_SPEC = pl.BlockSpec(memory_space=pltpu.VMEM)


def _params(n_axes, collective_id=None):
    return pltpu.CompilerParams(dimension_semantics=("arbitrary",) * n_axes,
                                vmem_limit_bytes=VMEM_LIMIT_BYTES, collective_id=collective_id)


PAIR_FILL_ID, PAIR_REDUCE_ID, SUM_SHARE_ID = 1, 2, 3


def _sibling_barrier(first_step):
    @pl.when(first_step)
    def _():
        sem = pltpu.get_barrier_semaphore()
        x, y, c = lax.axis_index("x"), lax.axis_index("y"), lax.axis_index("c")
        pl.semaphore_signal(sem, inc=1, device_id=(x, y, 1 - c), device_id_type=pl.DeviceIdType.MESH)
        pl.semaphore_wait(sem, 1)


def _dot_nn(a, b):
    return jnp.dot(a, b, preferred_element_type=F32)


def _dot_nt(a, b):
    return lax.dot_general(a, b, (((1,), (1,)), ((), ())), preferred_element_type=F32)


def _dot_tn(a, b):
    return lax.dot_general(a, b, (((0,), (0,)), ((), ())), preferred_element_type=F32)


def _sigmoid(v):
    return 1.0 / (1.0 + jnp.exp(-v))


def _matmul(name, mode, a, b, *, tm, tn, tk, extras=(), outs, epilogue, deps=(), b_spec=None, n_cols=None,
            prefetch=(), alias_dep_to_out=None, j_outer=False):
    deps = [d for d in deps if d is not None]
    if mode == "tn":
        K, M = a.shape
    else:
        M, K = a.shape
    if b_spec is None:
        (N, K2) = b.shape if mode == "nt" else b.shape[::-1]
        assert K == K2, (name, a.shape, b.shape)
        if mode == "nt":
            b_spec = pl.BlockSpec((tn, tk), lambda i, j, k, *p: (j, k))
        else:
            b_spec = pl.BlockSpec((tk, tn), lambda i, j, k, *p: (k, j))
    else:
        N = n_cols
    assert M % tm == 0 and N % tn == 0 and K % tk == 0, (name, a.shape, b.shape)
    ni, nj, nk = M // tm, N // tn, K // tk
    if mode == "tn":
        a_spec = pl.BlockSpec((tk, tm), lambda i, j, k, *p: (k, i))
    else:
        a_spec = pl.BlockSpec((tm, tk), lambda i, j, k, *p: (i, k))
    dot = {"nn": _dot_nn, "nt": _dot_nt, "tn": _dot_tn}[mode]
    n_ex, n_out, n_dep, n_pre = len(extras), len(outs), len(deps), len(prefetch)
    grid = (ni, nj, nk)
    if j_outer:
        grid = (nj, ni, nk)

        def swapped(spec):
            return pl.BlockSpec(spec.block_shape, lambda j, i, k, *p: spec.index_map(i, j, k, *p))

        a_spec, b_spec = swapped(a_spec), swapped(b_spec)
        extras = [(e, swapped(s)) for e, s in extras]
        outs = [(o, swapped(s)) for o, s in outs]

    def body(*refs):
        refs = refs[n_pre:]
        a_ref, b_ref = refs[0], refs[1]
        ex = refs[2:2 + n_ex]
        out = refs[2 + n_ex + n_dep:2 + n_ex + n_dep + n_out]
        acc = refs[-1] if nk > 1 else None
        i = pl.program_id(1 if j_outer else 0)
        k = pl.program_id(2)
        if nk == 1:
            epilogue(dot(a_ref[...].astype(CDT), b_ref[...].astype(CDT)), ex, out, i)
            return

        @pl.when(k == 0)
        def _():
            acc[...] = jnp.zeros_like(acc)

        acc[...] += dot(a_ref[...].astype(CDT), b_ref[...].astype(CDT))

        @pl.when(k == nk - 1)
        def _():
            epilogue(acc[...], ex, out, i)

    grid_spec = pltpu.PrefetchScalarGridSpec(
        num_scalar_prefetch=n_pre, grid=grid,
        in_specs=[a_spec, b_spec] + [s for _, s in extras] + [pl.BlockSpec(memory_space=pl.ANY)] * n_dep,
        out_specs=[s for _, s in outs],
        scratch_shapes=[pltpu.VMEM((tm, tn), F32)] if nk > 1 else [])
    aliases = {}
    if alias_dep_to_out is not None:
        aliases = {n_pre + 2 + n_ex + alias_dep_to_out[0]: alias_dep_to_out[1]}
    res = pl.pallas_call(
        body, name=name, grid_spec=grid_spec, out_shape=[o for o, _ in outs], input_output_aliases=aliases,
        compiler_params=_params(3),
    )(*prefetch, a, b, *[e for e, _ in extras], *deps)
    return res


def _mn(tm, tn, col_off=0):
    return pl.BlockSpec((tm, tn), lambda i, j, k, *p: (i, j + col_off))


def _row(tn):
    return pl.BlockSpec((1, tn), lambda i, j, k, *p: (0, j))


def _ep_store(acc, ex, out, i):
    out[0][...] = acc.astype(out[0].dtype)


def _ep_resid_norm(acc, ex, out, i):
    x1 = ex[0][...] + acc
    out[0][...] = x1
    rstd = lax.rsqrt(jnp.mean(x1 * x1, axis=-1, keepdims=True) + EPS)
    out[1][...] = (x1 * rstd * ex[1][...]).astype(out[1].dtype)


def _ep_up(acc, ex, out, i):
    out[0][...] = acc.astype(out[0].dtype)
    r = jnp.maximum(acc, 0.0)
    out[1][...] = (r * r).astype(out[1].dtype)


def _ep_down_loss(acc, ex, out, i, inv_d):
    diff = (ex[0][...] + acc) - ex[1][...]
    dx2 = diff * inv_d
    out[0][...] = dx2
    out[1][...] = dx2.astype(out[1].dtype)

    @pl.when(i == 0)
    def _():
        out[2][...] = jnp.zeros_like(out[2])

    out[2][...] += jnp.sum(diff * diff, axis=0, keepdims=True)


def _ep_dh(acc, ex, out, i):
    h = ex[0][...].astype(F32)
    out[0][...] = (acc * (2.0 * jnp.maximum(h, 0.0))).astype(out[0].dtype)


def _ep_rms_bwd(acc, ex, out, i):
    x = ex[0][...]
    g = ex[1][...]
    rstd = lax.rsqrt(jnp.mean(x * x, axis=-1, keepdims=True) + EPS)
    xh = x * rstd
    dxh = acc * g
    dx = ex[2][...] + rstd * (dxh - xh * jnp.mean(dxh * xh, axis=-1, keepdims=True))
    out[0][...] = dx
    for copy in out[1:-1]:
        copy[...] = dx.astype(copy.dtype)
    dg = out[-1]

    @pl.when(i == 0)
    def _():
        dg[...] = jnp.zeros_like(dg)

    dg[...] += jnp.sum(acc * xh, axis=0, keepdims=True)


def _ep_gates(acc, ex, out, i):
    sa = _sigmoid(ex[0][...].astype(F32))
    sb = _sigmoid(ex[1][...].astype(F32))
    dpa = acc * sa
    dpb = acc * sb
    out[0][...] = dpa.astype(out[0].dtype)
    out[1][...] = dpb.astype(out[1].dtype)
    out[2][...] = (dpa * ex[2][...].astype(F32) * (1.0 - sa)).astype(out[2].dtype)
    out[3][...] = (dpb * ex[3][...].astype(F32) * (1.0 - sb)).astype(out[3].dtype)


def _sds(shape, dtype):
    return jax.ShapeDtypeStruct(shape, dtype)


def _in_proj_mine(x, g, w_mine, chip, proj_sds, deps, tm=512):
    S, D = x.shape
    ns = w_mine.shape[1]
    deps = [d for d in deps if d is not None]

    def body(c_ref, x_ref, g_ref, w_ref, *rest):
        xn_ref, proj_ref = rest[len(deps)], rest[len(deps) + 1]
        xv = x_ref[...]
        rstd = lax.rsqrt(jnp.mean(xv * xv, axis=-1, keepdims=True) + EPS)
        xn = (xv * rstd * g_ref[...]).astype(xn_ref.dtype)
        xn_ref[...] = xn
        proj_ref[...] = _dot_nn(xn, w_ref[...]).astype(proj_ref.dtype)

    grid_spec = pltpu.PrefetchScalarGridSpec(
        num_scalar_prefetch=1, grid=(S // tm,),
        in_specs=[pl.BlockSpec((tm, D), lambda i, c: (i, 0)), pl.BlockSpec((1, D), lambda i, c: (0, 0)),
                  pl.BlockSpec((D, ns), lambda i, c: (0, 0))] + [pl.BlockSpec(memory_space=pl.ANY)] * len(deps),
        out_specs=[pl.BlockSpec((tm, D), lambda i, c: (i, 0)), pl.BlockSpec((tm, ns), lambda i, c: (i, c[0]))])
    return pl.pallas_call(body, name="in_proj_mine", grid_spec=grid_spec, out_shape=[_sds((S, D), CDT), proj_sds],
                          compiler_params=_params(1))(chip, x, g, w_mine, *deps)


def _rm_shape(S, d, width):
    return (S, width) if d == 1 else (d, S // d, width)


def _rm_spec(tm, d, width):
    if d == 1:
        return pl.BlockSpec((tm, width), lambda i: (i, 0))
    return pl.BlockSpec((d, tm // d, width), lambda i: (0, i, 0))


def _rm_put(dst_ref, cols, buf_ref, d):
    if d == 1:
        dst_ref[:, cols] = buf_ref[...].astype(dst_ref.dtype)
        return
    m = buf_ref.shape[0] // d
    for r in range(d):
        dst_ref[r, :, cols] = buf_ref[pl.ds(r, m, stride=d), :].astype(dst_ref.dtype)


def _rm_reader(buf_ref, src_ref, d):
    if d == 1:
        return lambda s, rows: src_ref[rows, s * HD:(s + 1) * HD].astype(F32)
    m = buf_ref.shape[1] // d
    for s in range(buf_ref.shape[0]):
        for r in range(d):
            buf_ref.at[s][pl.ds(r, m, stride=d), :] = src_ref[r, :, s * HD:(s + 1) * HD].astype(F32)
    return lambda s, rows: buf_ref.at[s][rows, :]


def _qknorm_fwd(proj, gqk, tm=512):
    S = proj.shape[0]
    W = 2 * ATT_W
    dil = [d for _, d in ATT_GROUPS]

    def body(p_ref, g_ref, o0, o1, o2, buf):
        outs = (o0, o1, o2)
        for hd in range(3 * ATT_HEADS):
            which, head = hd // ATT_HEADS, hd % ATT_HEADS
            grp, slot = head // ATT_HPG, head % ATT_HPG
            cols = slice(hd * HD, (hd + 1) * HD)

            def chunk(rows, which=which, cols=cols):
                v = p_ref[rows, cols].astype(F32)
                if which < 2:
                    rstd = lax.rsqrt(jnp.mean(v * v, axis=-1, keepdims=True) + EPS)
                    v = v * rstd * g_ref[:, cols]
                buf[rows, :] = v

            chunk(slice(None))
            _rm_put(outs[grp], slice(which * GW + slot * HD, which * GW + (slot + 1) * HD), buf, dil[grp])

    return pl.pallas_call(
        body, name="qknorm_fwd", grid=(S // tm,),
        in_specs=[pl.BlockSpec((tm, 3 * ATT_W), lambda i: (i, 0)), pl.BlockSpec((1, W), lambda i: (0, 0))],
        out_specs=[_rm_spec(tm, d, 3 * GW) for d in dil],
        out_shape=[_sds(_rm_shape(S, d, 3 * GW), CDT) for d in dil],
        scratch_shapes=[pltpu.VMEM((tm, HD), F32)],
        compiler_params=_params(1))(proj, gqk)


def _qknorm_bwd(proj, gqk, dqs, dks, dvs, dproj, tm=256):
    S = proj.shape[0]
    W = 2 * ATT_W
    dil = [d for _, d in ATT_GROUPS]

    def body(p_ref, g_ref, *refs):
        ins = refs[0:9]
        o_ref, dg_ref = refs[10], refs[11]
        bufs = refs[12:21]
        i = pl.program_id(0)

        @pl.when(i == 0)
        def _():
            dg_ref[...] = jnp.zeros_like(dg_ref)

        nat = [_rm_reader(bufs[j], ins[j], dil[j % 3]) for j in range(9)]
        dq_get, dk_get, dv_get = nat[0:3], nat[3:6], nat[6:9]
        for hd in range(2 * ATT_HEADS):
            sl = slice(hd * HD, (hd + 1) * HD)
            head = hd % ATT_HEADS
            grp, slot = head // ATT_HPG, head % ATT_HPG
            get = (dq_get if hd < ATT_HEADS else dk_get)[grp]

            def chunk(rows, sl=sl, slot=slot, get=get):
                dn = get(slot, rows)
                v = p_ref[rows, sl].astype(F32)
                rstd = lax.rsqrt(jnp.mean(v * v, axis=-1, keepdims=True) + EPS)
                vh = v * rstd
                dg_ref[:, sl] += jnp.sum(dn * vh, axis=0, keepdims=True)
                dvh = dn * g_ref[:, sl]
                o_ref[rows, sl] = (rstd * (dvh - vh * jnp.mean(dvh * vh, axis=-1, keepdims=True))).astype(o_ref.dtype)

            chunk(slice(None))
        for head in range(ATT_HEADS):
            grp, slot = head // ATT_HPG, head % ATT_HPG
            o_ref[:, W + head * HD:W + (head + 1) * HD] = dv_get[grp](slot, slice(None)).astype(o_ref.dtype)

    return pl.pallas_call(
        body, name="qknorm_bwd", grid=(S // tm,),
        in_specs=[pl.BlockSpec((tm, W), lambda i: (i, 0)), pl.BlockSpec((1, W), lambda i: (0, 0))]
        + [_rm_spec(tm, d, GW) for d in dil] * 3 + [pl.BlockSpec(memory_space=pl.ANY)],
        out_specs=[pl.BlockSpec((tm, 3 * ATT_W), lambda i: (i, 0)), pl.BlockSpec((1, W), lambda i: (0, 0))],
        out_shape=[_sds(dproj.shape, dproj.dtype), _sds((1, W), F32)],
        scratch_shapes=[pltpu.VMEM((ATT_HPG, tm, HD), F32)] * 9,
        input_output_aliases={11: 0},
        compiler_params=_params(1))(proj, gqk, *dqs, *dks, *dvs, dproj)


def _att_mask(n):
    qi = lax.broadcasted_iota(jnp.int32, (BLK, 2 * BLK), 0)
    kj = lax.broadcasted_iota(jnp.int32, (BLK, 2 * BLK), 1)
    dist = BLK + qi - kj
    valid = (dist >= 0) & (dist <= BLK) & ((kj >= BLK) | (n > 0))
    return valid, dist.astype(F32)


def _att_slopes(grp):
    return [2.0 ** (-8.0 * (grp * ATT_HPG + hh + 1) / ATT_HEADS) for hh in range(ATT_HPG)]


ATT_RES_PER_STEP = 4


def _att_res(d):
    return min(d, ATT_RES_PER_STEP)


def _att_3d(a, d):
    return a[None] if d == 1 else a


def _att_spec(d, row_fn, col=0):
    return pl.BlockSpec((_att_res(d), BLK, GW), lambda r, n: (r, row_fn(n), col))


def _att_chains(d):
    return [(rr * ATT_HPG + hh, rr, slice(hh * HD, (hh + 1) * HD), hh)
            for rr in range(_att_res(d)) for hh in range(ATT_HPG)]


def _att_qkv_specs(d, nb):
    last = nb - 1

    def cur(n):
        return jnp.minimum(n, last)

    def prev(n):
        return jnp.maximum(jnp.minimum(n, last) - 1, 0)

    return [_att_spec(d, cur, 0), _att_spec(d, prev, 1), _att_spec(d, cur, 1), _att_spec(d, prev, 2),
            _att_spec(d, cur, 2)]


def _att_fwd(grp, S, qkv):
    _, d = ATT_GROUPS[grp]
    L = S // d
    nb = L // BLK
    R = _att_res(d)
    slopes = _att_slopes(grp)
    scale = HD ** -0.5
    chains = _att_chains(d)

    def body(q_ref, kp_ref, kc_ref, vp_ref, vc_ref, o_ref, l_ref, s_buf, p_buf, den_buf):
        n = pl.program_id(1)
        valid, distf = _att_mask(n)
        for c, rr, sl, hh in chains:
            k = jnp.concatenate([kp_ref[rr, :, sl], kc_ref[rr, :, sl]], axis=0)
            s_buf[c] = _dot_nt(q_ref[rr, :, sl], k)
        for c, rr, sl, hh in chains:
            s = s_buf[c] * scale + (-slopes[hh] * d) * distf
            s = jnp.where(valid, s, -1e30)
            m = jnp.max(s, axis=-1, keepdims=True)
            p = jnp.exp(s - m)
            den = jnp.sum(p, axis=-1, keepdims=True)
            p_buf[c] = p.astype(CDT)
            den_buf[c] = jnp.broadcast_to(den, (BLK, HD))
            l_ref[rr, :, sl] = jnp.broadcast_to(m + jnp.log(den), (BLK, HD))
        for c, rr, sl, hh in chains:
            v = jnp.concatenate([vp_ref[rr, :, sl], vc_ref[rr, :, sl]], axis=0)
            o_ref[rr, :, sl] = _dot_nn(p_buf[c], v) / den_buf[c]

    out_spec = _att_spec(d, lambda n: n)
    n_ch = len(chains)
    q3 = _att_3d(qkv, d)
    o, l = pl.pallas_call(
        body, name="att_fwd_g%d" % grp, grid=(d // R, nb),
        in_specs=_att_qkv_specs(d, nb),
        out_specs=[out_spec, out_spec],
        out_shape=[_sds((d, L, GW), F32)] * 2,
        scratch_shapes=[pltpu.VMEM((n_ch, BLK, 2 * BLK), F32), pltpu.VMEM((n_ch, BLK, 2 * BLK), CDT),
                        pltpu.VMEM((n_ch, BLK, HD), F32)],
        compiler_params=_params(2),
    )(q3, q3, q3, q3, q3)
    return o.reshape(_rm_shape(S, d, GW)), l.reshape(_rm_shape(S, d, GW))


def _att_bwd(grp, S, qkv, lse, do_g, c_g):
    _, d = ATT_GROUPS[grp]
    L = S // d
    nb = L // BLK
    R = _att_res(d)
    slopes = _att_slopes(grp)
    scale = HD ** -0.5
    last = nb - 1
    chains = _att_chains(d)

    def body(q_ref, kp_ref, kc_ref, vp_ref, vc_ref, l_ref, do_ref, c_ref, dq_ref, dk_ref, dv_ref, ck, cv,
             s_buf, dp_buf, p_buf, ds_buf):
        n = pl.program_id(1)

        @pl.when(n == 0)
        def _():
            ck[...] = jnp.zeros_like(ck)
            cv[...] = jnp.zeros_like(cv)

        @pl.when(n < nb)
        def _():
            valid, distf = _att_mask(n)
            for c, rr, sl, hh in chains:
                k = jnp.concatenate([kp_ref[rr, :, sl], kc_ref[rr, :, sl]], axis=0)
                v = jnp.concatenate([vp_ref[rr, :, sl], vc_ref[rr, :, sl]], axis=0)
                s_buf[c] = _dot_nt(q_ref[rr, :, sl], k)
                dp_buf[c] = _dot_nt(do_ref[rr, :, sl], v)
            for c, rr, sl, hh in chains:
                s = s_buf[c] * scale + (-slopes[hh] * d) * distf
                p = jnp.where(valid, jnp.exp(s - l_ref[rr, :, sl][:, 0:1]), 0.0)
                p_buf[c] = p.astype(CDT)
                ds_buf[c] = (p * (dp_buf[c] + c_ref[rr, :, sl][:, 0:1]) * scale).astype(CDT)
            for c, rr, sl, hh in chains:
                k = jnp.concatenate([kp_ref[rr, :, sl], kc_ref[rr, :, sl]], axis=0)
                ds = ds_buf[c]
                dq_ref[rr, :, sl] = _dot_nn(ds, k)
                dk = _dot_tn(ds, q_ref[rr, :, sl])
                dv = _dot_tn(p_buf[c], do_ref[rr, :, sl])
                dk_ref[rr, :, sl] = ck[rr, :, sl] + dk[0:BLK]
                dv_ref[rr, :, sl] = cv[rr, :, sl] + dv[0:BLK]
                ck[rr, :, sl] = dk[BLK:2 * BLK]
                cv[rr, :, sl] = dv[BLK:2 * BLK]

        @pl.when(n == nb)
        def _():
            dk_ref[...] = ck[...]
            dv_ref[...] = cv[...]

    blk = (R, BLK, GW)
    at_q = _att_spec(d, lambda n: jnp.minimum(n, last))
    behind = _att_spec(d, lambda n: jnp.maximum(n - 1, 0))
    n_ch = len(chains)
    q3 = _att_3d(qkv, d)
    res = pl.pallas_call(
        body, name="att_bwd_g%d" % grp, grid=(d // R, nb + 1),
        in_specs=_att_qkv_specs(d, nb) + [at_q, at_q, at_q],
        out_specs=[at_q, behind, behind],
        out_shape=[_sds((d, L, GW), F32)] * 3,
        scratch_shapes=[pltpu.VMEM(blk, F32), pltpu.VMEM(blk, F32),
                        pltpu.VMEM((n_ch, BLK, 2 * BLK), F32), pltpu.VMEM((n_ch, BLK, 2 * BLK), F32),
                        pltpu.VMEM((n_ch, BLK, 2 * BLK), CDT), pltpu.VMEM((n_ch, BLK, 2 * BLK), CDT)],
        compiler_params=_params(2),
    )(q3, q3, q3, q3, q3, _att_3d(lse, d), _att_3d(do_g, d), _att_3d(c_g, d))
    return [t.reshape(_rm_shape(S, d, GW)) for t in res]


def _mix_alpha(l0, l1, l2):
    mx = jnp.maximum(jnp.maximum(l0, l1), l2)
    e = [jnp.exp(l0 - mx), jnp.exp(l1 - mx), jnp.exp(l2 - mx)]
    tot = e[0] + e[1] + e[2]
    return [ei / tot for ei in e]


def _mix_fwd(S, os_, ls_, tm=512):
    dil = [d for _, d in ATT_GROUPS]

    def body(*refs):
        out, bufs = refs[6], refs[7:13]
        get = [_rm_reader(bufs[j], refs[j], dil[j % 3]) for j in range(6)]
        rows = slice(None)
        for s in range(ATT_HPG):
            al = _mix_alpha(*[get[3 + g](s, rows) for g in range(3)])
            mixed = al[0] * get[0](s, rows) + al[1] * get[1](s, rows) + al[2] * get[2](s, rows)
            out[:, s * HD:(s + 1) * HD] = mixed.astype(out.dtype)

    specs = [_rm_spec(tm, d, GW) for d in dil]
    return pl.pallas_call(
        body, name="mix_fwd", grid=(S // tm,), in_specs=specs * 2, out_specs=pl.BlockSpec((tm, GW), lambda i: (i, 0)),
        out_shape=_sds((S, GW), CDT), scratch_shapes=[pltpu.VMEM((ATT_HPG, tm, HD), F32)] * 6,
        compiler_params=_params(1))(*os_, *ls_)


def _mix_bwd(S, os_, ls_, do_a, tm=512):
    dil = [d for _, d in ATT_GROUPS]

    def body(*refs):
        d_ref, outs, bufs, tmps = refs[6], refs[7:13], refs[13:19], refs[19:25]
        get = [_rm_reader(bufs[j], refs[j], dil[j % 3]) for j in range(6)]
        for s in range(ATT_HPG):
            cols = slice(s * HD, (s + 1) * HD)

            def chunk(rows, s=s, cols=cols):
                al = _mix_alpha(*[get[3 + g](s, rows) for g in range(3)])
                dv = d_ref[rows, cols]
                o_a = al[0] * get[0](s, rows) + al[1] * get[1](s, rows) + al[2] * get[2](s, rows)
                dsum = jnp.sum(dv * o_a, axis=-1, keepdims=True)
                for g in range(3):
                    tmps[g][rows, :] = al[g] * dv
                    tmps[3 + g][rows, :] = -(al[g] * dsum)

            chunk(slice(None))
            for j in range(6):
                _rm_put(outs[j], cols, tmps[j], dil[j % 3])

    specs = [_rm_spec(tm, d, GW) for d in dil]
    res = pl.pallas_call(
        body, name="mix_bwd", grid=(S // tm,), in_specs=specs * 2 + [pl.BlockSpec((tm, GW), lambda i: (i, 0))],
        out_specs=specs * 2,
        out_shape=[_sds(_rm_shape(S, d, GW), CDT) for d in dil] + [_sds(_rm_shape(S, d, GW), F32) for d in dil],
        scratch_shapes=[pltpu.VMEM((ATT_HPG, tm, HD), F32)] * 6 + [pltpu.VMEM((tm, HD), F32)] * 6,
        compiler_params=_params(1))(*os_, *ls_, do_a)
    return res[:3], res[3:]


def _ret_tables(dk):
    H, C = RET_HEADS, BLK
    log_g = jnp.log(1.0 - 2.0 ** (-5.0 - jnp.arange(H, dtype=F32)))
    idx = jnp.arange(C, dtype=F32)
    diff = idx[:, None] - idx[None, :]
    decay = jnp.where(diff >= 0, jnp.exp(log_g[:, None, None] * jnp.maximum(diff, 0.0)), 0.0)
    xi = jnp.exp(log_g[:, None] * (idx[None, :] + 1.0))
    zeta = jnp.exp(log_g[:, None] * (C - 1.0 - idx[None, :])) * (dk ** -0.5)
    g_chunk = jnp.exp(log_g * C)
    bc = lambda t: jnp.broadcast_to(t[:, :, None], (H, C, C))
    return decay, bc(xi), bc(zeta), jnp.broadcast_to(g_chunk[:, None, None], (H, 8, C))


def _gn_fwd(o, g, b):
    mu = jnp.mean(o, axis=-1, keepdims=True)
    xc = o - mu
    rstd = lax.rsqrt(jnp.mean(xc * xc, axis=-1, keepdims=True) + EPS)
    yh = xc * rstd
    return yh, rstd, yh * g + b


def _ret_specs(dk, dv, order):
    H = RET_HEADS
    qk_w, v_w = H * dk, H * dv
    off_q = 3 * ATT_W
    off_k, off_v, off_g = off_q + qk_w, off_q + 2 * qk_w, off_q + 2 * qk_w + v_w
    assert 2 * dk == dv and all(off % dv == 0 for off in (off_q, off_k, off_v, off_g))

    def col(off, j):
        return pl.BlockSpec((BLK, dv), lambda i: (order(i), off // dv + j))

    tab = pl.BlockSpec((H, BLK, BLK), lambda i: (0, 0, 0))
    return ([col(off_q, j) for j in range(H // 2)] + [col(off_k, j) for j in range(H // 2)]
            + [col(off_v, j) for j in range(H)] + [col(off_g, j) for j in range(H)]
            + [tab, tab, tab, pl.BlockSpec((H, 8, BLK), lambda i: (0, 0, 0))])


def _ret_heads(refs, dk):
    H = RET_HEADS
    q_refs, k_refs = refs[0:H // 2], refs[H // 2:H]
    v_refs, gr_refs = refs[H:2 * H], refs[2 * H:3 * H]

    def head(h):
        cols = slice((h % 2) * dk, (h % 2 + 1) * dk)
        return q_refs[h // 2][:, cols], k_refs[h // 2][:, cols], v_refs[h][...], gr_refs[h][...]

    return head, refs[3 * H:3 * H + 4]


def _ret_fwd(proj, gn_g, gn_b, dk, dv):
    S = proj.shape[0]
    N = S // BLK
    H = RET_HEADS
    kscale = dk ** -0.5
    n_in = 3 * H + 4

    def body(*refs):
        head, (dec_ref, xi_ref, zeta_ref, gc_ref) = _ret_heads(refs, dk)
        g_ref, b_ref, opre_ref, or_ref, st_ref, state, s_buf, cross_buf = refs[n_in:n_in + 8]
        n = pl.program_id(0)

        @pl.when(n == 0)
        def _():
            state[...] = jnp.zeros_like(state)

        for h in range(H):
            q, k, v, _ = head(h)
            s_buf[h] = _dot_nt(q, k)
            st = state[h]
            st_c = st.astype(CDT)
            st_ref[h] = st_c
            cross_buf[h] = _dot_nn(q, st_c)
            kz = (k.astype(F32) * zeta_ref[h][:, 0:1]).astype(CDT)
            state[h] = st * gc_ref[h][0:1, 0:1] + _dot_tn(kz, v)
        for h in range(H):
            vs = slice(h * dv, (h + 1) * dv)
            _, _, v, gr = head(h)
            s = s_buf[h] * kscale * dec_ref[h]
            o = _dot_nn(s.astype(CDT), v) + cross_buf[h] * xi_ref[h][:, 0:1]
            opre_ref[:, vs] = o
            _, _, y = _gn_fwd(o, g_ref[:, vs], b_ref[:, vs])
            gr = gr.astype(F32)
            or_ref[:, vs] = (y * (gr * _sigmoid(gr))).astype(or_ref.dtype)

    v_w = H * dv
    row = pl.BlockSpec((1, v_w), lambda i: (0, 0))
    tile = pl.BlockSpec((BLK, v_w), lambda i: (i, 0))
    return pl.pallas_call(
        body, name="ret_fwd", grid=(N,),
        in_specs=_ret_specs(dk, dv, lambda i: i) + [row, row],
        out_specs=[tile, tile, pl.BlockSpec((None, H, dk, dv), lambda i: (i, 0, 0, 0))],
        out_shape=[_sds((S, v_w), F32), _sds((S, v_w), CDT), _sds((N, H, dk, dv), CDT)],
        scratch_shapes=[pltpu.VMEM((H, dk, dv), F32), pltpu.VMEM((H, BLK, BLK), F32), pltpu.VMEM((H, BLK, dv), F32)],
        compiler_params=_params(1),
    )(*[proj] * (3 * H), *_ret_tables(dk), gn_g, gn_b)


def _ret_bwd(proj, gn_g, gn_b, o_pre, states, d_or, dga, dgb, dk, dv):
    S, in_w = proj.shape
    N = S // BLK
    H = RET_HEADS
    qk_w, v_w = H * dk, H * dv
    kscale = dk ** -0.5
    n_in = 3 * H + 4
    out_w = 2 * qk_w + 2 * v_w
    gate_w = dga.shape[1]
    col0 = 3 * ATT_W
    assert col0 + out_w + 2 * gate_w == in_w
    rev = lambda i: N - 1 - i

    def body(*refs):
        head, (dec_ref, xi_ref, zeta_ref, gc_ref) = _ret_heads(refs, dk)
        (g_ref, b_ref, opre_ref, st_ref, dor_ref, dga_ref, dgb_ref, dproj_ref, dg_ref, db_ref, dstate, stage,
         sem, do_buf, dox_buf, a_buf, g_buf, dq_buf, dk_buf, dv_buf) = refs[n_in:n_in + 20]
        i = pl.program_id(0)
        slot = i % 2
        out_ref = stage.at[slot]

        def out_copy(s, step):
            rows = pl.ds(pl.multiple_of(rev(step) * BLK, BLK), BLK)
            return pltpu.make_async_copy(stage.at[s], dproj_ref.at[rows, pl.ds(col0, in_w - col0)], sem.at[s])

        @pl.when(i >= 2)
        def _():
            out_copy(slot, i - 2).wait()

        @pl.when(i == 0)
        def _():
            dstate[...] = jnp.zeros_like(dstate)
            dg_ref[...] = jnp.zeros_like(dg_ref)
            db_ref[...] = jnp.zeros_like(db_ref)

        out_ref[:, out_w:out_w + gate_w] = dga_ref[...]
        out_ref[:, out_w + gate_w:out_w + 2 * gate_w] = dgb_ref[...]
        for h in range(H):
            vs = slice(h * dv, (h + 1) * dv)
            _, _, _, gr = head(h)
            gr = gr.astype(F32)
            sg = _sigmoid(gr)
            gain = g_ref[:, vs]
            yh, rstd, y = _gn_fwd(opre_ref[:, vs], gain, b_ref[:, vs])
            d_or_v = dor_ref[:, vs]
            dy = d_or_v * (gr * sg)
            out_ref[:, 2 * qk_w + v_w + h * dv:2 * qk_w + v_w + (h + 1) * dv] = (
                d_or_v * y * (sg * (1.0 + gr * (1.0 - sg)))).astype(out_ref.dtype)
            dg_ref[:, vs] += jnp.sum(dy * yh, axis=0, keepdims=True)
            db_ref[:, vs] += jnp.sum(dy, axis=0, keepdims=True)
            dyh = dy * gain
            do = rstd * (dyh - jnp.mean(dyh, axis=-1, keepdims=True)
                         - yh * jnp.mean(dyh * yh, axis=-1, keepdims=True))
            do_buf[h] = do.astype(CDT)
            dox_buf[h] = (do * xi_ref[h][:, 0:1]).astype(CDT)
        for h in range(H):
            q, k, v, _ = head(h)
            dox = dox_buf[h]
            a_buf[h] = _dot_nt(q, k)
            g_buf[h] = _dot_nt(do_buf[h], v)
            dsn = dstate[h]
            dsn_c = dsn.astype(CDT)
            kz = (k.astype(F32) * zeta_ref[h][:, 0:1]).astype(CDT)
            dq_buf[h] = _dot_nt(dox, st_ref[h])
            dk_buf[h] = _dot_nt(v, dsn_c)
            dv_buf[h] = _dot_nn(kz, dsn_c)
            dstate[h] = dsn * gc_ref[h][0:1, 0:1] + _dot_tn(q, dox)
        for h in range(H):
            q, k, _, _ = head(h)
            decay = dec_ref[h]
            a_c = (a_buf[h] * kscale * decay).astype(CDT)
            g_c = (g_buf[h] * decay).astype(CDT)
            dq = _dot_nn(g_c, k) * kscale + dq_buf[h]
            dkk = _dot_tn(g_c, q) * kscale + dk_buf[h] * zeta_ref[h][:, 0:1]
            dvv = _dot_tn(a_c, do_buf[h]) + dv_buf[h]
            out_ref[:, h * dk:(h + 1) * dk] = dq.astype(out_ref.dtype)
            out_ref[:, qk_w + h * dk:qk_w + (h + 1) * dk] = dkk.astype(out_ref.dtype)
            out_ref[:, 2 * qk_w + h * dv:2 * qk_w + (h + 1) * dv] = dvv.astype(out_ref.dtype)

        cp = out_copy(slot, i)
        cp.start()

        @pl.when(i == N - 1)
        def _():
            cp.wait()
            if N >= 2:
                out_copy(1 - slot, i - 1).wait()

    row = pl.BlockSpec((1, v_w), lambda i: (0, 0))
    tile = pl.BlockSpec((BLK, v_w), lambda i: (rev(i), 0))
    gate = pl.BlockSpec((BLK, gate_w), lambda i: (rev(i), 0))
    return pl.pallas_call(
        body, name="ret_bwd", grid=(N,),
        in_specs=_ret_specs(dk, dv, rev) + [row, row, tile,
                 pl.BlockSpec((None, H, dk, dv), lambda i: (rev(i), 0, 0, 0)), tile, gate, gate],
        out_specs=[pl.BlockSpec(memory_space=pl.ANY), row, row],
        out_shape=[_sds((S, in_w), CDT), _sds((1, v_w), F32), _sds((1, v_w), F32)],
        scratch_shapes=[pltpu.VMEM((H, dk, dv), F32), pltpu.VMEM((2, BLK, in_w - col0), CDT),
                        pltpu.SemaphoreType.DMA((2,)),
                        pltpu.VMEM((H, BLK, dv), CDT), pltpu.VMEM((H, BLK, dv), CDT),
                        pltpu.VMEM((H, BLK, BLK), F32), pltpu.VMEM((H, BLK, BLK), F32),
                        pltpu.VMEM((H, BLK, dk), F32), pltpu.VMEM((H, BLK, dk), F32), pltpu.VMEM((H, BLK, dv), F32)],
        compiler_params=_params(1),
    )(*[proj] * (3 * H), *_ret_tables(dk), gn_g, gn_b, o_pre, states, d_or, dga, dgb)


def _merge_fwd(o_a, o_r, wa, wb, proj, d_model, tm=1024, tn=512):
    S, in_w = proj.shape
    off_a, off_b = in_w - 2 * d_model, in_w - d_model
    assert off_a % tn == 0 and off_b % tn == 0

    def body(oa_ref, or_ref, wa_ref, wb_ref, ga_ref, gb_ref, y_ref, pa_ref, pb_ref):
        pa = _dot_nn(oa_ref[...], wa_ref[...])
        pb = _dot_nn(or_ref[...], wb_ref[...])
        y = _sigmoid(ga_ref[...].astype(F32)) * pa + _sigmoid(gb_ref[...].astype(F32)) * pb
        y_ref[...] = y.astype(y_ref.dtype)
        pa_ref[...] = pa.astype(pa_ref.dtype)
        pb_ref[...] = pb.astype(pb_ref.dtype)

    ka, kb = o_a.shape[1], o_r.shape[1]
    out = pl.BlockSpec((tm, tn), lambda i, j: (i, j))
    return pl.pallas_call(
        body, name="merge_fwd", grid=(S // tm, d_model // tn),
        in_specs=[pl.BlockSpec((tm, ka), lambda i, j: (i, 0)), pl.BlockSpec((tm, kb), lambda i, j: (i, 0)),
                  pl.BlockSpec((ka, tn), lambda i, j: (0, j)), pl.BlockSpec((kb, tn), lambda i, j: (0, j)),
                  pl.BlockSpec((tm, tn), lambda i, j: (i, off_a // tn + j)),
                  pl.BlockSpec((tm, tn), lambda i, j: (i, off_b // tn + j))],
        out_specs=[out, out, out], out_shape=[_sds((S, d_model), CDT)] * 3,
        compiler_params=_params(2))(o_a, o_r, wa, wb, proj, proj)


def _local_step(x, target, w_in_mine, chip, others, near_w_in, far_w_in, small, late_weights, on_grads, deps0=()):
    S, D = x.shape
    ns_in = w_in_mine.shape[1]
    in_w = N_CHIPS * ns_in
    d_ff = 4 * D
    ret_v_w = 2 * D
    dv = ret_v_w // RET_HEADS
    dk = (in_w - 3 * ATT_W - 2 * ret_v_w - 2 * D) // (2 * RET_HEADS)
    gqk = jnp.concatenate([small["q_norm_g"].reshape(1, ATT_W), small["k_norm_g"].reshape(1, ATT_W)], axis=1)
    g1, g2 = small["norm1_g"], small["norm2_g"]
    gn_g, gn_b = small["ret_gn_g"], small["ret_gn_b"]

    proj_sds = _sds((S, in_w), CDT)
    xn, proj = _in_proj_mine(x, g1, w_in_mine, chip, proj_sds, deps0)
    for stage, (get_w_in, chips) in enumerate(((near_w_in, others[:2]), (far_w_in, others[2:]))):
        w_in, started = get_w_in(proj)
        (proj,) = _matmul(
            "in_proj_far%d" % stage, "nn", xn, w_in, tm=512, tn=ns_in, tk=D, prefetch=[chips],
            n_cols=chips.shape[0] * ns_in, b_spec=pl.BlockSpec((None, D, ns_in), lambda i, j, k, o: (o[j], 0, 0)),
            outs=[(proj_sds, pl.BlockSpec((512, ns_in), lambda i, j, k, o: (i, o[j])))], epilogue=_ep_store,
            deps=[proj, started], alias_dep_to_out=(0, 0), j_outer=True)
    qkv = _qknorm_fwd(proj, gqk)
    att = [_att_fwd(g, S, qkv[g]) for g in range(3)]
    os_, ls_ = [a[0] for a in att], [a[1] for a in att]
    o_a = _mix_fwd(S, os_, ls_)
    o_pre, o_r, states = _ret_fwd(proj, gn_g, gn_b, dk, dv)
    w = late_weights(o_r)
    y, pa, pb = _merge_fwd(o_a, o_r, w["w_proj_a"], w["w_proj_b"], proj, D)
    x1, xn2 = _matmul("out_proj", "nn", y, w["w_out"], tm=1024, tn=D, tk=D,
                      extras=[(x, _mn(1024, D)), (g2, _row(D))],
                      outs=[(_sds((S, D), F32), _mn(1024, D)), (_sds((S, D), CDT), _mn(1024, D))],
                      epilogue=_ep_resid_norm)
    hid, act = _matmul("mlp_up", "nn", xn2, w["w_up"], tm=512, tn=2048, tk=D, j_outer=True,
                       outs=[(_sds((S, d_ff), CDT), _mn(512, 2048))] * 2, epilogue=_ep_up)
    dx2, dx2c, loss_row = _matmul(
        "mlp_down_loss", "nn", act, w["w_down"], tm=512, tn=D, tk=d_ff,
        extras=[(x1, _mn(512, D)), (target, _mn(512, D))],
        outs=[(_sds((S, D), F32), _mn(512, D)), (_sds((S, D), CDT), _mn(512, D)), (_sds((1, D), F32), _row(D))],
        epilogue=functools.partial(_ep_down_loss, inv_d=1.0 / D))
    loss = 0.5 * jnp.sum(loss_row) / D

    (dh,) = _matmul("d_hidden", "nt", dx2c, w["w_down"], tm=512, tn=2048, tk=D, j_outer=True,
                    extras=[(hid, _mn(512, 2048))], outs=[(_sds((S, d_ff), CDT), _mn(512, 2048))], epilogue=_ep_dh)
    (gw_down,) = _matmul("dw_down", "tn", act, dx2c, tm=1024, tn=D, tk=1024,
                         outs=[(_sds((d_ff, D), F32), _mn(1024, D))], epilogue=_ep_store)
    (gw_up,) = _matmul("dw_up", "tn", xn2, dh, tm=D, tn=1024, tk=1024,
                       outs=[(_sds((D, d_ff), F32), _mn(D, 1024))], epilogue=_ep_store)
    tok = on_grads({"w_down": gw_down, "w_up": gw_up})
    dx1, dx1c, dg2 = _matmul(
        "d_x1", "nt", dh, w["w_up"], tm=512, tn=D, tk=d_ff,
        extras=[(x1, _mn(512, D)), (g2, _row(D)), (dx2, _mn(512, D))],
        outs=[(_sds((S, D), F32), _mn(512, D)), (_sds((S, D), CDT), _mn(512, D)), (_sds((1, D), F32), _row(D))],
        epilogue=_ep_rms_bwd, deps=[tok])

    gt = 512
    assert (in_w - 2 * D) % gt == 0
    off_a, off_b = (in_w - 2 * D) // gt, (in_w - D) // gt
    dpa, dpb, dga, dgb = _matmul(
        "d_gates", "nt", dx1c, w["w_out"], tm=1024, tn=gt, tk=D,
        extras=[(proj, _mn(1024, gt, off_a)), (proj, _mn(1024, gt, off_b)), (pa, _mn(1024, gt)),
                (pb, _mn(1024, gt))],
        outs=[(_sds((S, D), CDT), _mn(1024, gt))] * 4, epilogue=_ep_gates)
    (gw_out,) = _matmul("dw_out", "tn", y, dx1c, tm=D, tn=D, tk=1024,
                        outs=[(_sds((D, D), F32), _mn(D, D))], epilogue=_ep_store)
    (gw_pa,) = _matmul("dw_proj_a", "tn", o_a, dpa, tm=GW, tn=D, tk=1024,
                       outs=[(_sds((GW, D), F32), _mn(GW, D))], epilogue=_ep_store)
    (gw_pb,) = _matmul("dw_proj_b", "tn", o_r, dpb, tm=1024, tn=D, tk=1024,
                       outs=[(_sds((ret_v_w, D), F32), _mn(1024, D))], epilogue=_ep_store)
    (do_a,) = _matmul("d_o_a", "nt", dpa, w["w_proj_a"], tm=1024, tn=GW, tk=D,
                      outs=[(_sds((S, GW), F32), _mn(1024, GW))], epilogue=_ep_store)
    tok = on_grads({"w_out": gw_out, "w_proj_a": gw_pa, "w_proj_b": gw_pb})
    (d_or,) = _matmul("d_o_r", "nt", dpb, w["w_proj_b"], tm=512, tn=ret_v_w, tk=D,
                      outs=[(_sds((S, ret_v_w), F32), _mn(512, ret_v_w))], epilogue=_ep_store, deps=[tok])

    dproj, dgn_g, dgn_b = _ret_bwd(proj, gn_g, gn_b, o_pre, states, d_or, dga, dgb, dk, dv)
    do_gs, c_gs = _mix_bwd(S, os_, ls_, do_a)
    datt_parts = [_att_bwd(g, S, qkv[g], ls_[g], do_gs[g], c_gs[g]) for g in range(3)]
    dproj, dgqk = _qknorm_bwd(proj, gqk, [p[0] for p in datt_parts], [p[1] for p in datt_parts],
                              [p[2] for p in datt_parts], dproj)

    (gw_in,) = _matmul(
        "dw_in", "tn", xn, dproj, tm=512, tn=ns_in, tk=1024,
        outs=[(_sds((N_CHIPS, D, ns_in), F32), pl.BlockSpec((None, 512, ns_in), lambda i, j, k: (j, i, 0)))],
        epilogue=_ep_store)
    tok = on_grads({"w_in": gw_in})
    grad_x, dg1 = _matmul(
        "d_x", "nt", dproj, w_in, tm=512, tn=D, tk=ns_in, n_cols=D,
        b_spec=pl.BlockSpec((None, D, ns_in), lambda i, j, k: (k, 0, 0)),
        extras=[(x, _mn(512, D)), (g1, _row(D)), (dx1, _mn(512, D))],
        outs=[(_sds((S, D), F32), _mn(512, D)), (_sds((1, D), F32), _row(D))],
        epilogue=_ep_rms_bwd, deps=[tok])

    smallg = {"norm1_g": dg1, "q_norm_g": dgqk[:, :ATT_W], "k_norm_g": dgqk[:, ATT_W:],
              "ret_gn_g": dgn_g, "ret_gn_b": dgn_b, "norm2_g": dg2}
    return loss, grad_x, smallg


N_CHIPS = 4
N_DEV = 8


def _place():
    x, y, c = lax.axis_index("x"), lax.axis_index("y"), lax.axis_index("c")
    return x, y, c


def _other_chips(x, y):
    out = []
    for fx, fy in ((1, 0), (0, 1), (1, 1)):
        px = 1 - x if fx else x
        py = 1 - y if fy else y
        out.append(((px, py), 2 * px + py))
    return out


SEM_SPEC = pl.BlockSpec(memory_space=pltpu.SEMAPHORE)
ANY_SPEC = pl.BlockSpec(memory_space=pl.ANY)
EFFECT = pltpu.SideEffectType.DATAFLOW_SIDE_EFFECTING


def _ici_copies(kind, srcs, lands, send, recv, which=(0, 1, 2)):
    x, y, c = _place()
    me = 2 * x + y
    out = []
    for w, (s, l) in enumerate(zip(srcs, lands)):
        for j, ((px, py), pidx) in enumerate(_other_chips(x, y)):
            if j not in which:
                continue
            if kind == "gather":
                half = s.shape[0] // 2
                rows = pl.ds(c * half, half)
                src, dst_there, dst_here = s.at[rows, :], l.at[me, rows, :], l.at[pidx, rows, :]
            else:
                src, dst_there, dst_here = s.at[pidx], l.at[me], l.at[pidx]
            out.append((src, dst_there, dst_here, send.at[3 * w + j], recv.at[3 * w + j], (px, py, c)))
    return out


def _exchange_start(name, kind, srcs, land_shapes, which=(0, 1, 2), lands=None):
    n = len(srcs)
    if lands is None:
        lands = [lax.empty(shape, dtype) for shape, dtype in land_shapes]

    def body(*refs):
        src_refs, land_refs = refs[:n], refs[n:2 * n]
        send, recv = refs[2 * n], refs[2 * n + 1]
        token = refs[-1]
        for src, dst, _, ss, rs, dev in _ici_copies(kind, src_refs, land_refs, send, recv, which):
            pltpu.make_async_remote_copy(src_ref=src, dst_ref=dst, send_sem=ss, recv_sem=rs, device_id=dev,
                                         device_id_type=MESH).start()
        token[...] = jnp.zeros_like(token)

    thru = [pltpu.HBM(s.shape, s.dtype) for s in srcs] + [pltpu.HBM(shape, dtype) for shape, dtype in land_shapes]
    res = pl.pallas_call(
        body, name=name,
        out_shape=(pltpu.SemaphoreType.DMA((3 * n,)), pltpu.SemaphoreType.DMA((3 * n,)), *thru, _sds((8, LANES), F32)),
        in_specs=[HBM_SPEC] * (2 * n), out_specs=(SEM_SPEC, SEM_SPEC, *[HBM_SPEC] * (2 * n), VMEM_SPEC),
        input_output_aliases={i: 2 + i for i in range(2 * n)},
        compiler_params=pltpu.CompilerParams(has_side_effects=EFFECT),
    )(*[pltpu.with_memory_space_constraint(s, pltpu.HBM) for s in srcs],
      *[pltpu.with_memory_space_constraint(l, pltpu.HBM) for l in lands])
    return res[0], res[1], list(res[2:2 + n]), list(res[2 + n:2 + 2 * n]), res[-1]


def _exchange_wait(name, kind, send, recv, srcs, lands, after, which=(0, 1, 2)):
    n = len(srcs)

    def body(*refs):
        src_refs, land_refs = refs[:n], refs[n:2 * n]
        send_ref, recv_ref = refs[2 * n], refs[2 * n + 1]
        for src, _, dst, ss, rs, dev in _ici_copies(kind, src_refs, land_refs, send_ref, recv_ref, which):
            cp = pltpu.make_async_remote_copy(src_ref=src, dst_ref=dst, send_sem=ss, recv_sem=rs, device_id=dev,
                                              device_id_type=MESH)
            cp.wait_send()
            cp.wait_recv()

    thru = [pltpu.HBM(t.shape, t.dtype) for t in list(srcs) + list(lands)]
    res = pl.pallas_call(
        body, name=name, out_shape=thru,
        in_specs=[HBM_SPEC] * (2 * n) + [SEM_SPEC, SEM_SPEC, ANY_SPEC], out_specs=[HBM_SPEC] * (2 * n),
        input_output_aliases={i: i for i in range(2 * n)},
        compiler_params=pltpu.CompilerParams(has_side_effects=EFFECT),
    )(*srcs, *lands, send, recv, after)
    return list(res[:n]), list(res[n:])


PAIR_TILE_ELEMS = 1 << 19


def _pair_fill(name, gathered, mine, core, others, chip, write_mine=True):
    k, r, C = gathered.shape
    half = r // 2
    tr = _row_tile(half, C, PAIR_TILE_ELEMS, mult=16)
    nt = half // tr
    n_far = others.shape[0]

    def body(c_ref, o_ref, chip_ref, in_ref, mine_ref, out_ref, slot, send, recv):
        j = pl.program_id(0)
        _sibling_barrier((j == 0) & (pl.program_id(1) == 0))
        b = (j * nt + pl.program_id(1)) % 2
        x, y, c = _place()
        cp = pltpu.make_async_remote_copy(src_ref=in_ref, dst_ref=slot.at[b], send_sem=send.at[b],
                                          recv_sem=recv.at[b], device_id=(x, y, 1 - c), device_id_type=MESH)

        @pl.when(j < n_far)
        def _():
            cp.start()
            cp.wait_recv()
            out_ref[...] = slot[b]
            cp.wait_send()

        @pl.when(j >= n_far)
        def _():
            out_ref[...] = mine_ref[...]

    def far(j):
        return jnp.minimum(j, n_far - 1)

    grid_spec = pltpu.PrefetchScalarGridSpec(
        num_scalar_prefetch=3, grid=(n_far + (2 if write_mine else 0), nt),
        in_specs=[pl.BlockSpec((tr, C), lambda j, i, c, o, m: (
                      (2 * o[far(j)] + c[0]) * nt + jnp.where(j < n_far, i, nt - 1), 0)),
                  pl.BlockSpec((tr, C), lambda j, i, c, o, m: (jnp.where(j < n_far, 0, (j - n_far) * nt + i), 0))],
        out_specs=pl.BlockSpec((tr, C), lambda j, i, c, o, m: (
            jnp.where(j < n_far, 2 * o[far(j)] + 1 - c[0], 2 * m[0] + j - n_far) * nt + i, 0)),
        scratch_shapes=[pltpu.VMEM((2, tr, C), gathered.dtype), pltpu.SemaphoreType.DMA((2,)),
                        pltpu.SemaphoreType.DMA((2,))])
    out = pl.pallas_call(body, name=name, grid_spec=grid_spec, out_shape=_sds((k * r, C), gathered.dtype),
                         input_output_aliases={3: 0}, compiler_params=_params(2, PAIR_FILL_ID))(
                             core, others, chip, gathered.reshape(k * r, C), mine)
    return out.reshape(k, r, C)


def _pair_reduce(name, g, core):
    k, R, C = g.shape
    half = R // 2
    tr = _row_tile(half, C, PAIR_TILE_ELEMS, mult=16)
    nt = half // tr

    def body(c_ref, mine_ref, give_ref, out_ref, wire_ref, stage, slot, send, recv):
        _sibling_barrier((pl.program_id(0) == 0) & (pl.program_id(1) == 0))
        b = (pl.program_id(0) * nt + pl.program_id(1)) % 2
        x, y, c = _place()
        stage[b] = give_ref[...].astype(stage.dtype)
        cp = pltpu.make_async_remote_copy(src_ref=stage.at[b], dst_ref=slot.at[b], send_sem=send.at[b],
                                          recv_sem=recv.at[b], device_id=(x, y, 1 - c), device_id_type=MESH)
        cp.start()
        cp.wait_recv()
        tot = mine_ref[...] + slot[b].astype(F32)
        out_ref[...] = tot
        wire_ref[...] = tot.astype(wire_ref.dtype)
        cp.wait_send()

    blk = (tr, C)
    out_spec = pl.BlockSpec(blk, lambda s, i, c: (s * nt + i, 0))
    grid_spec = pltpu.PrefetchScalarGridSpec(
        num_scalar_prefetch=1, grid=(k, nt),
        in_specs=[pl.BlockSpec(blk, lambda s, i, c: ((2 * s + c[0]) * nt + i, 0)),
                  pl.BlockSpec(blk, lambda s, i, c: ((2 * s + 1 - c[0]) * nt + i, 0))],
        out_specs=[out_spec, out_spec],
        scratch_shapes=[pltpu.VMEM((2, tr, C), CDT), pltpu.VMEM((2, tr, C), CDT), pltpu.SemaphoreType.DMA((2,)),
                        pltpu.SemaphoreType.DMA((2,))])
    g2 = g.reshape(k * R, C)
    out, wire = pl.pallas_call(body, name=name, grid_spec=grid_spec,
                               out_shape=[_sds((k * half, C), F32), _sds((k * half, C), CDT)],
                               compiler_params=_params(2, PAIR_REDUCE_ID))(core, g2, g2)
    return out, wire.reshape(k, half, C)


def _all_reduce_small(v):
    r, cdim = v.shape

    def body(v_ref, o_ref, buf, send, recv):
        x, y, c = _place()
        me = 4 * x + 2 * y + c
        buf[me] = v_ref[...]
        sends = []
        for m in range(1, N_DEV):
            px = 1 - x if m & 4 else x
            py = 1 - y if m & 2 else y
            pc = 1 - c if m & 1 else c
            cp = pltpu.make_async_remote_copy(src_ref=v_ref, dst_ref=buf.at[me], send_sem=send.at[m - 1],
                                              recv_sem=recv.at[m - 1], device_id=(px, py, pc), device_id_type=MESH)
            cp.start()
            sends.append((cp, 4 * px + 2 * py + pc))
        for m, (cp, pidx) in enumerate(sends):
            pltpu.make_async_remote_copy(src_ref=v_ref, dst_ref=buf.at[pidx], send_sem=send.at[m], recv_sem=recv.at[m],
                                         device_id=(x, y, c), device_id_type=MESH).wait_recv()
        for cp, _ in sends:
            cp.wait_send()
        tot = buf[0]
        for k in range(1, N_DEV):
            tot = tot + buf[k]
        o_ref[...] = tot

    return pl.pallas_call(
        body, name="all_reduce_small", in_specs=[VMEM_SPEC], out_specs=VMEM_SPEC,
        out_shape=_sds((r, cdim), F32),
        scratch_shapes=[pltpu.VMEM((N_DEV, r, cdim), F32), pltpu.SemaphoreType.DMA((N_DEV - 1,)),
                        pltpu.SemaphoreType.DMA((N_DEV - 1,))],
    )(v)


def _row_tile(rows, cols, budget_elems=1 << 18, mult=8):
    if rows % mult:
        return rows
    t = max(mult, (budget_elems // cols) // mult * mult)
    while rows % t:
        t -= mult
    return t


def _adamw_update(w, g, m, v):
    nm = ADAM_B1 * m + (1.0 - ADAM_B1) * g
    nv = ADAM_B2 * v + (1.0 - ADAM_B2) * (g * g)
    m_hat = nm / (1.0 - ADAM_B1 ** ADAM_STEP)
    v_hat = nv / (1.0 - ADAM_B2 ** ADAM_STEP)
    return -ADAM_LR * (m_hat / (jnp.sqrt(v_hat) + ADAM_EPS) + ADAM_WD * w), nm, nv


def _adamw(name, w, g, m, v):
    R, C = w.shape
    tr = _row_tile(R, C, 1 << 18)

    def body(w_ref, g_ref, m_ref, v_ref, d_ref, nm_ref, nv_ref):
        d_ref[...], nm_ref[...], nv_ref[...] = _adamw_update(w_ref[...], g_ref[...], m_ref[...], v_ref[...])

    spec = pl.BlockSpec((tr, C), lambda i: (i, 0))
    return pl.pallas_call(body, name=name, grid=(R // tr,), in_specs=[spec] * 4, out_specs=[spec] * 3,
                          out_shape=[_sds((R, C), F32)] * 3, compiler_params=_params(1))(w, g, m, v)


def _sum_share(name, own, by_chip, chip, others, core):
    k, half, C = by_chip.shape
    tr = _row_tile(half, C, mult=16)
    nt = half // tr

    def body(chip_ref, oth_ref, c_ref, own_ref, a_ref, b_ref, cc_ref, g_out, mine, slot, send, recv):
        p = pl.program_id(1)
        _sibling_barrier((pl.program_id(0) == 0) & (p == 0))
        b = pl.program_id(0) % 2
        x, y, c = _place()
        cp = pltpu.make_async_remote_copy(src_ref=mine.at[b], dst_ref=slot.at[b], send_sem=send.at[b],
                                          recv_sem=recv.at[b], device_id=(x, y, 1 - c), device_id_type=MESH)

        @pl.when(p == 0)
        def _():
            tot = ((own_ref[...] + a_ref[...].astype(F32)) + b_ref[...].astype(F32)) + cc_ref[...].astype(F32)
            mine[b] = tot
            cp.start()
            g_out[...] = tot

        @pl.when(p == 1)
        def _():
            cp.wait_recv()
            g_out[...] = slot[b]
            cp.wait_send()

    def piece(j):
        return pl.BlockSpec((tr, C), lambda i, p, chip, oth, c: (oth[j] * nt + i, 0))

    grid_spec = pltpu.PrefetchScalarGridSpec(
        num_scalar_prefetch=3, grid=(nt, 2),
        in_specs=[pl.BlockSpec((tr, C), lambda i, p, chip, oth, c: (chip[0] * nt + i, 0)),
                  piece(0), piece(1), piece(2)],
        out_specs=pl.BlockSpec((tr, C), lambda i, p, chip, oth, c: (
            jnp.where(p == 0, c[0], 1 - c[0]) * nt + i, 0)),
        scratch_shapes=[pltpu.VMEM((2, tr, C), F32), pltpu.VMEM((2, tr, C), F32), pltpu.SemaphoreType.DMA((2,)),
                        pltpu.SemaphoreType.DMA((2,))])
    by2 = by_chip.reshape(k * half, C)
    return pl.pallas_call(body, name=name, grid_spec=grid_spec, out_shape=_sds((2 * half, C), F32),
                          compiler_params=_params(2, SUM_SHARE_ID))(chip, others, core, own, by2, by2, by2)


BIG = ("w_in", "w_proj_a", "w_proj_b", "w_out", "w_up", "w_down")
COL_SHARDED = ("w_in", "w_proj_a", "w_up")
SMALL = ("norm1_g", "q_norm_g", "k_norm_g", "ret_gn_g", "ret_gn_b", "norm2_g")
ALL_W = ("norm1_g", "w_in", "q_norm_g", "k_norm_g", "ret_gn_g", "ret_gn_b", "w_proj_a", "w_proj_b", "w_out",
         "norm2_g", "w_up", "w_down")
LANES = 128


def _to_full(name, gathered):
    k, r, c = gathered.shape
    if name in COL_SHARDED:
        return gathered.transpose(1, 0, 2).reshape(r, k * c)
    return gathered.reshape(k * r, c)


def _to_shard_major(name, full):
    if name in COL_SHARDED:
        r, c4 = full.shape
        return full.reshape(r, N_CHIPS, c4 // N_CHIPS).transpose(1, 0, 2)
    r4, c = full.shape
    return full.reshape(N_CHIPS, r4 // N_CHIPS, c)


def kernel(x, norm1_g, w_in, q_norm_g, k_norm_g, ret_gn_g, ret_gn_b, w_proj_a, w_proj_b, w_out, norm2_g, w_up, w_down, loss_target, m_norm1_g, m_w_in, m_q_norm_g, m_k_norm_g, m_ret_gn_g, m_ret_gn_b, m_w_proj_a, m_w_proj_b, m_w_out, m_norm2_g, m_w_up, m_w_down, v_norm1_g, v_w_in, v_q_norm_g, v_k_norm_g, v_ret_gn_g, v_ret_gn_b, v_w_proj_a, v_w_proj_b, v_w_out, v_norm2_g, v_w_up, v_w_down):
    weights = dict(norm1_g=norm1_g, w_in=w_in, q_norm_g=q_norm_g, k_norm_g=k_norm_g, ret_gn_g=ret_gn_g,
                   ret_gn_b=ret_gn_b, w_proj_a=w_proj_a, w_proj_b=w_proj_b, w_out=w_out, norm2_g=norm2_g,
                   w_up=w_up, w_down=w_down)
    moments_m = dict(norm1_g=m_norm1_g, w_in=m_w_in, q_norm_g=m_q_norm_g, k_norm_g=m_k_norm_g, ret_gn_g=m_ret_gn_g,
                     ret_gn_b=m_ret_gn_b, w_proj_a=m_w_proj_a, w_proj_b=m_w_proj_b, w_out=m_w_out,
                     norm2_g=m_norm2_g, w_up=m_w_up, w_down=m_w_down)
    moments_v = dict(norm1_g=v_norm1_g, w_in=v_w_in, q_norm_g=v_q_norm_g, k_norm_g=v_k_norm_g, ret_gn_g=v_ret_gn_g,
                     ret_gn_b=v_ret_gn_b, w_proj_a=v_w_proj_a, w_proj_b=v_w_proj_b, w_out=v_w_out,
                     norm2_g=v_norm2_g, w_up=v_w_up, w_down=v_w_down)

    mx, my = lax.axis_index("x"), lax.axis_index("y")
    core = lax.axis_index("c").astype(jnp.int32).reshape(1)
    chip = (2 * mx + my).astype(jnp.int32).reshape(1)
    others = jnp.stack([2 * (1 - mx) + my, 2 * mx + 1 - my, 2 * (1 - mx) + 1 - my]).astype(jnp.int32)
    shards = {n: weights[n][0].astype(CDT) for n in BIG}
    def start_gather(name, names):
        return _exchange_start(name, "gather", [shards[n] for n in names],
                               [((N_CHIPS,) + shards[n].shape, CDT) for n in names])

    w_in_shape = [((N_CHIPS,) + shards["w_in"].shape, CDT)]
    n_send, n_recv, n_srcs, n_lands, n_token = _exchange_start(
        "gather_w_in_near_start", "gather", [shards["w_in"]], w_in_shape, which=(0, 1))
    late = [n for n in BIG if n != "w_in"]
    flight = {}

    def near_w_in(after):
        srcs, lands = _exchange_wait("gather_w_in_near_wait", "gather", n_send, n_recv, n_srcs, n_lands, after,
                                     which=(0, 1))
        d_send, d_recv, d_srcs, d_lands, _ = _exchange_start(
            "gather_w_in_diag_start", "gather", srcs, w_in_shape, which=(2,), lands=lands)
        flight["late"] = start_gather("gather_late_start", late)
        w_near = _pair_fill("pair_fill_w_in_near", d_lands[0], d_srcs[0], core, others[:2], chip)
        flight["diag"] = (d_send, d_recv, d_srcs, [w_near])
        return w_near, flight["late"][-1]

    def far_w_in(after):
        d_send, d_recv, d_srcs, d_lands = flight["diag"]
        srcs, lands = _exchange_wait("gather_w_in_diag_wait", "gather", d_send, d_recv, d_srcs, d_lands, after,
                                     which=(2,))
        return _pair_fill("pair_fill_w_in_diag", lands[0], srcs[0], core, others[2:], chip, write_mine=False), None

    def late_weights(after):
        l_send, l_recv, l_srcs, l_lands, _ = flight["late"]
        srcs, lands = _exchange_wait("gather_late_wait", "gather", l_send, l_recv, l_srcs, l_lands, after)
        out = {}
        for n, mine, land in zip(late, srcs, lands):
            out[n] = _to_full(n, _pair_fill("pair_fill_%s" % n, land, mine, core, others, chip))
        return out

    pending = []

    def on_grads(group):
        names = list(group)
        red = [_pair_reduce("pair_reduce_%s" % n, g if g.ndim == 3 else _to_shard_major(n, g), core)
               for n, g in group.items()]
        wires = [wire for _, wire in red]
        send, recv, srcs, lands, token = _exchange_start(
            "scatter_start_%s" % names[0], "scatter", wires, [(wire.shape, wire.dtype) for wire in wires])
        pending.append((names, [own for own, _ in red], send, recv, srcs, lands))
        return token

    small = {n: weights[n].reshape(1, -1) for n in SMALL}

    loss, grad_x, small_g = _local_step(x[0], loss_target[0], n_srcs[0], chip, others, near_w_in, far_w_in, small,
                                        late_weights, on_grads, deps0=[n_token])
    loss = lax.psum(loss, ("x", "y", "c"))

    out_g, out_d, out_m, out_v = {}, {}, {}, {}
    for names, owns, send, recv, srcs, lands in pending:
        _, got = _exchange_wait("scatter_wait_%s" % names[0], "scatter", send, recv, srcs, lands, grad_x)
        for n, own, by_chip in zip(names, owns, got):
            shape = weights[n].shape
            g2 = _sum_share("sum_share_%s" % n, own, by_chip, chip, others, core)
            d, nm, nv = _adamw("adamw_%s" % n, weights[n][0], g2, moments_m[n][0], moments_v[n][0])
            out_g[n], out_d[n], out_m[n], out_v[n] = (t.reshape(shape) for t in (g2, d, nm, nv))

    packed = jnp.concatenate([small_g[n].reshape(1, -1) for n in SMALL], axis=1)
    red = _all_reduce_small(packed.reshape(-1, LANES)).reshape(1, -1)
    off = 0
    for n in SMALL:
        shape = weights[n].shape
        row = (1, weights[n].size)
        g2 = red[:, off:off + row[1]]
        off += row[1]
        d, nm, nv = _adamw("adamw_%s" % n, weights[n].reshape(row), g2, moments_m[n].reshape(row),
                           moments_v[n].reshape(row))
        out_g[n], out_d[n], out_m[n], out_v[n] = (t.reshape(shape) for t in (g2, d, nm, nv))

    return (loss, grad_x[None], *[out_g[n] for n in ALL_W], *[out_d[n] for n in ALL_W],
            *[out_m[n] for n in ALL_W], *[out_v[n] for n in ALL_W])
```

```python
import functools

import jax
import jax.numpy as jnp
from jax import lax
from jax.experimental import pallas as pl
from jax.experimental.pallas import tpu as pltpu

CDT = jnp.bfloat16
F32 = jnp.float32
EPS = 1e-6

ATT_GROUPS = ((128, 1), (512, 4), (2048, 16))
ATT_HPG = 4
ATT_HEADS = 12
HD = 128
BLK = 128
ATT_W = ATT_HEADS * HD
GW = ATT_HPG * HD
RET_HEADS = 4

ADAM_LR = 0.001
ADAM_B1 = 0.9
ADAM_B2 = 0.999
ADAM_EPS = 1e-08
ADAM_WD = 0.01
ADAM_STEP = 10

VMEM_LIMIT_BYTES = 56 * 1024 * 1024
MESH = pl.DeviceIdType.MESH
HBM_SPEC = pl.BlockSpec(memory_space=pltpu.HBM)
VMEM_SPEC = pl.BlockSpec(memory_space=pltpu.VMEM)


def _params(n_axes, collective_id=None):
    return pltpu.CompilerParams(dimension_semantics=("arbitrary",) * n_axes,
                                vmem_limit_bytes=VMEM_LIMIT_BYTES, collective_id=collective_id)


PAIR_FILL_ID, PAIR_REDUCE_ID, SUM_SHARE_ID = 1, 2, 3


def _sibling_barrier(first_step):
    @pl.when(first_step)
    def _():
        sem = pltpu.get_barrier_semaphore()
        x, y, c = lax.axis_index("x"), lax.axis_index("y"), lax.axis_index("c")
        pl.semaphore_signal(sem, inc=1, device_id=(x, y, 1 - c), device_id_type=pl.DeviceIdType.MESH)
        pl.semaphore_wait(sem, 1)


def _dot_nn(a, b):
    return jnp.dot(a, b, preferred_element_type=F32)


def _dot_nt(a, b):
    return lax.dot_general(a, b, (((1,), (1,)), ((), ())), preferred_element_type=F32)


def _dot_tn(a, b):
    return lax.dot_general(a, b, (((0,), (0,)), ((), ())), preferred_element_type=F32)


def _sigmoid(v):
    return 1.0 / (1.0 + jnp.exp(-v))


def _matmul(name, mode, a, b, *, tm, tn, tk, extras=(), outs, epilogue, deps=(), b_spec=None, n_cols=None,
            prefetch=(), alias_dep_to_out=None, j_outer=False):
    deps = [d for d in deps if d is not None]
    if mode == "tn":
        K, M = a.shape
    else:
        M, K = a.shape
    if b_spec is None:
        (N, K2) = b.shape if mode == "nt" else b.shape[::-1]
        assert K == K2, (name, a.shape, b.shape)
        if mode == "nt":
            b_spec = pl.BlockSpec((tn, tk), lambda i, j, k, *p: (j, k))
        else:
            b_spec = pl.BlockSpec((tk, tn), lambda i, j, k, *p: (k, j))
    else:
        N = n_cols
    assert M % tm == 0 and N % tn == 0 and K % tk == 0, (name, a.shape, b.shape)
    ni, nj, nk = M // tm, N // tn, K // tk
    if mode == "tn":
        a_spec = pl.BlockSpec((tk, tm), lambda i, j, k, *p: (k, i))
    else:
        a_spec = pl.BlockSpec((tm, tk), lambda i, j, k, *p: (i, k))
    dot = {"nn": _dot_nn, "nt": _dot_nt, "tn": _dot_tn}[mode]
    n_ex, n_out, n_dep, n_pre = len(extras), len(outs), len(deps), len(prefetch)
    grid = (ni, nj, nk)
    if j_outer:
        grid = (nj, ni, nk)

        def swapped(spec):
            return pl.BlockSpec(spec.block_shape, lambda j, i, k, *p: spec.index_map(i, j, k, *p))

        a_spec, b_spec = swapped(a_spec), swapped(b_spec)
        extras = [(e, swapped(s)) for e, s in extras]
        outs = [(o, swapped(s)) for o, s in outs]

    def body(*refs):
        refs = refs[n_pre:]
        a_ref, b_ref = refs[0], refs[1]
        ex = refs[2:2 + n_ex]
        out = refs[2 + n_ex + n_dep:2 + n_ex + n_dep + n_out]
        acc = refs[-1] if nk > 1 else None
        i = pl.program_id(1 if j_outer else 0)
        k = pl.program_id(2)
        if nk == 1:
            epilogue(dot(a_ref[...].astype(CDT), b_ref[...].astype(CDT)), ex, out, i)
            return

        @pl.when(k == 0)
        def _():
            acc[...] = jnp.zeros_like(acc)

        acc[...] += dot(a_ref[...].astype(CDT), b_ref[...].astype(CDT))

        @pl.when(k == nk - 1)
        def _():
            epilogue(acc[...], ex, out, i)

    grid_spec = pltpu.PrefetchScalarGridSpec(
        num_scalar_prefetch=n_pre, grid=grid,
        in_specs=[a_spec, b_spec] + [s for _, s in extras] + [pl.BlockSpec(memory_space=pl.ANY)] * n_dep,
        out_specs=[s for _, s in outs],
        scratch_shapes=[pltpu.VMEM((tm, tn), F32)] if nk > 1 else [])
    aliases = {}
    if alias_dep_to_out is not None:
        aliases = {n_pre + 2 + n_ex + alias_dep_to_out[0]: alias_dep_to_out[1]}
    res = pl.pallas_call(
        body, name=name, grid_spec=grid_spec, out_shape=[o for o, _ in outs], input_output_aliases=aliases,
        compiler_params=_params(3),
    )(*prefetch, a, b, *[e for e, _ in extras], *deps)
    return res


def _mn(tm, tn, col_off=0):
    return pl.BlockSpec((tm, tn), lambda i, j, k, *p: (i, j + col_off))


def _row(tn):
    return pl.BlockSpec((1, tn), lambda i, j, k, *p: (0, j))


def _ep_store(acc, ex, out, i):
    out[0][...] = acc.astype(out[0].dtype)


def _ep_resid_norm(acc, ex, out, i):
    x1 = ex[0][...] + acc
    out[0][...] = x1
    rstd = lax.rsqrt(jnp.mean(x1 * x1, axis=-1, keepdims=True) + EPS)
    out[1][...] = (x1 * rstd * ex[1][...]).astype(out[1].dtype)


def _ep_up(acc, ex, out, i):
    out[0][...] = acc.astype(out[0].dtype)
    r = jnp.maximum(acc, 0.0)
    out[1][...] = (r * r).astype(out[1].dtype)


def _ep_down_loss(acc, ex, out, i, inv_d):
    diff = (ex[0][...] + acc) - ex[1][...]
    dx2 = diff * inv_d
    out[0][...] = dx2
    out[1][...] = dx2.astype(out[1].dtype)

    @pl.when(i == 0)
    def _():
        out[2][...] = jnp.zeros_like(out[2])

    out[2][...] += jnp.sum(diff * diff, axis=0, keepdims=True)


def _ep_dh(acc, ex, out, i):
    h = ex[0][...].astype(F32)
    out[0][...] = (acc * (2.0 * jnp.maximum(h, 0.0))).astype(out[0].dtype)


def _ep_rms_bwd(acc, ex, out, i):
    x = ex[0][...]
    g = ex[1][...]
    rstd = lax.rsqrt(jnp.mean(x * x, axis=-1, keepdims=True) + EPS)
    xh = x * rstd
    dxh = acc * g
    dx = ex[2][...] + rstd * (dxh - xh * jnp.mean(dxh * xh, axis=-1, keepdims=True))
    out[0][...] = dx
    for copy in out[1:-1]:
        copy[...] = dx.astype(copy.dtype)
    dg = out[-1]

    @pl.when(i == 0)
    def _():
        dg[...] = jnp.zeros_like(dg)

    dg[...] += jnp.sum(acc * xh, axis=0, keepdims=True)


def _ep_gates(acc, ex, out, i):
    sa = _sigmoid(ex[0][...].astype(F32))
    sb = _sigmoid(ex[1][...].astype(F32))
    dpa = acc * sa
    dpb = acc * sb
    out[0][...] = dpa.astype(out[0].dtype)
    out[1][...] = dpb.astype(out[1].dtype)
    out[2][...] = (dpa * ex[2][...].astype(F32) * (1.0 - sa)).astype(out[2].dtype)
    out[3][...] = (dpb * ex[3][...].astype(F32) * (1.0 - sb)).astype(out[3].dtype)


def _sds(shape, dtype):
    return jax.ShapeDtypeStruct(shape, dtype)


def _in_proj_mine(x, g, w_mine, chip, proj_sds, deps, tm=512):
    S, D = x.shape
    ns = w_mine.shape[1]
    deps = [d for d in deps if d is not None]

    def body(c_ref, x_ref, g_ref, w_ref, *rest):
        xn_ref, proj_ref = rest[len(deps)], rest[len(deps) + 1]
        xv = x_ref[...]
        rstd = lax.rsqrt(jnp.mean(xv * xv, axis=-1, keepdims=True) + EPS)
        xn = (xv * rstd * g_ref[...]).astype(xn_ref.dtype)
        xn_ref[...] = xn
        proj_ref[...] = _dot_nn(xn, w_ref[...]).astype(proj_ref.dtype)

    grid_spec = pltpu.PrefetchScalarGridSpec(
        num_scalar_prefetch=1, grid=(S // tm,),
        in_specs=[pl.BlockSpec((tm, D), lambda i, c: (i, 0)), pl.BlockSpec((1, D), lambda i, c: (0, 0)),
                  pl.BlockSpec((D, ns), lambda i, c: (0, 0))] + [pl.BlockSpec(memory_space=pl.ANY)] * len(deps),
        out_specs=[pl.BlockSpec((tm, D), lambda i, c: (i, 0)), pl.BlockSpec((tm, ns), lambda i, c: (i, c[0]))])
    return pl.pallas_call(body, name="in_proj_mine", grid_spec=grid_spec, out_shape=[_sds((S, D), CDT), proj_sds],
                          compiler_params=_params(1))(chip, x, g, w_mine, *deps)


def _rm_shape(S, d, width):
    return (S, width) if d == 1 else (d, S // d, width)


def _rm_spec(tm, d, width):
    if d == 1:
        return pl.BlockSpec((tm, width), lambda i: (i, 0))
    return pl.BlockSpec((d, tm // d, width), lambda i: (0, i, 0))


def _rm_put(dst_ref, cols, buf_ref, d):
    if d == 1:
        dst_ref[:, cols] = buf_ref[...].astype(dst_ref.dtype)
        return
    m = buf_ref.shape[0] // d
    for r in range(d):
        dst_ref[r, :, cols] = buf_ref[pl.ds(r, m, stride=d), :].astype(dst_ref.dtype)


def _rm_reader(buf_ref, src_ref, d):
    if d == 1:
        return lambda s, rows: src_ref[rows, s * HD:(s + 1) * HD].astype(F32)
    m = buf_ref.shape[1] // d
    for s in range(buf_ref.shape[0]):
        for r in range(d):
            buf_ref.at[s][pl.ds(r, m, stride=d), :] = src_ref[r, :, s * HD:(s + 1) * HD].astype(F32)
    return lambda s, rows: buf_ref.at[s][rows, :]


def _qknorm_fwd(proj, gqk, tm=512):
    S = proj.shape[0]
    W = 2 * ATT_W
    dil = [d for _, d in ATT_GROUPS]

    def body(p_ref, g_ref, o0, o1, o2, buf):
        outs = (o0, o1, o2)
        for hd in range(3 * ATT_HEADS):
            which, head = hd // ATT_HEADS, hd % ATT_HEADS
            grp, slot = head // ATT_HPG, head % ATT_HPG
            cols = slice(hd * HD, (hd + 1) * HD)

            def chunk(rows, which=which, cols=cols):
                v = p_ref[rows, cols].astype(F32)
                if which < 2:
                    rstd = lax.rsqrt(jnp.mean(v * v, axis=-1, keepdims=True) + EPS)
                    v = v * rstd * g_ref[:, cols]
                buf[rows, :] = v

            chunk(slice(None))
            _rm_put(outs[grp], slice(which * GW + slot * HD, which * GW + (slot + 1) * HD), buf, dil[grp])

    return pl.pallas_call(
        body, name="qknorm_fwd", grid=(S // tm,),
        in_specs=[pl.BlockSpec((tm, 3 * ATT_W), lambda i: (i, 0)), pl.BlockSpec((1, W), lambda i: (0, 0))],
        out_specs=[_rm_spec(tm, d, 3 * GW) for d in dil],
        out_shape=[_sds(_rm_shape(S, d, 3 * GW), CDT) for d in dil],
        scratch_shapes=[pltpu.VMEM((tm, HD), F32)],
        compiler_params=_params(1))(proj, gqk)


def _qknorm_bwd(proj, gqk, dqs, dks, dvs, dproj, tm=256):
    S = proj.shape[0]
    W = 2 * ATT_W
    dil = [d for _, d in ATT_GROUPS]

    def body(p_ref, g_ref, *refs):
        ins = refs[0:9]
        o_ref, dg_ref = refs[10], refs[11]
        bufs = refs[12:21]
        i = pl.program_id(0)

        @pl.when(i == 0)
        def _():
            dg_ref[...] = jnp.zeros_like(dg_ref)

        nat = [_rm_reader(bufs[j], ins[j], dil[j % 3]) for j in range(9)]
        dq_get, dk_get, dv_get = nat[0:3], nat[3:6], nat[6:9]
        for hd in range(2 * ATT_HEADS):
            sl = slice(hd * HD, (hd + 1) * HD)
            head = hd % ATT_HEADS
            grp, slot = head // ATT_HPG, head % ATT_HPG
            get = (dq_get if hd < ATT_HEADS else dk_get)[grp]

            def chunk(rows, sl=sl, slot=slot, get=get):
                dn = get(slot, rows)
                v = p_ref[rows, sl].astype(F32)
                rstd = lax.rsqrt(jnp.mean(v * v, axis=-1, keepdims=True) + EPS)
                vh = v * rstd
                dg_ref[:, sl] += jnp.sum(dn * vh, axis=0, keepdims=True)
                dvh = dn * g_ref[:, sl]
                o_ref[rows, sl] = (rstd * (dvh - vh * jnp.mean(dvh * vh, axis=-1, keepdims=True))).astype(o_ref.dtype)

            chunk(slice(None))
        for head in range(ATT_HEADS):
            grp, slot = head // ATT_HPG, head % ATT_HPG
            o_ref[:, W + head * HD:W + (head + 1) * HD] = dv_get[grp](slot, slice(None)).astype(o_ref.dtype)

    return pl.pallas_call(
        body, name="qknorm_bwd", grid=(S // tm,),
        in_specs=[pl.BlockSpec((tm, W), lambda i: (i, 0)), pl.BlockSpec((1, W), lambda i: (0, 0))]
        + [_rm_spec(tm, d, GW) for d in dil] * 3 + [pl.BlockSpec(memory_space=pl.ANY)],
        out_specs=[pl.BlockSpec((tm, 3 * ATT_W), lambda i: (i, 0)), pl.BlockSpec((1, W), lambda i: (0, 0))],
        out_shape=[_sds(dproj.shape, dproj.dtype), _sds((1, W), F32)],
        scratch_shapes=[pltpu.VMEM((ATT_HPG, tm, HD), F32)] * 9,
        input_output_aliases={11: 0},
        compiler_params=_params(1))(proj, gqk, *dqs, *dks, *dvs, dproj)


def _att_mask(n):
    qi = lax.broadcasted_iota(jnp.int32, (BLK, 2 * BLK), 0)
    kj = lax.broadcasted_iota(jnp.int32, (BLK, 2 * BLK), 1)
    dist = BLK + qi - kj
    valid_all = (dist >= 0) & (dist <= BLK)
    return valid_all & ((kj >= BLK) | (n > 0)), valid_all, dist.astype(F32)


def _att_slopes(grp):
    return [2.0 ** (-8.0 * (grp * ATT_HPG + hh + 1) / ATT_HEADS) for hh in range(ATT_HPG)]


ATT_PLANES_PER_STEP = 4


def _att_planes(d):
    return d if d > 1 else ATT_PLANES_PER_STEP


def _att_3d(a, d):
    return a.reshape(ATT_PLANES_PER_STEP, a.shape[0] // ATT_PLANES_PER_STEP, a.shape[1]) if d == 1 else a


def _att_spec(row_fn, col=0):
    return pl.BlockSpec((ATT_PLANES_PER_STEP, BLK, GW), lambda r, n: (r, row_fn(n), col))


def _att_chains():
    return [(rr * ATT_HPG + hh, rr, slice(hh * HD, (hh + 1) * HD), hh)
            for rr in range(ATT_PLANES_PER_STEP) for hh in range(ATT_HPG)]


def _att_qkv_specs(nb):
    last = nb - 1

    def cur(n):
        return jnp.minimum(n, last)

    def prev(n):
        return jnp.maximum(jnp.minimum(n, last) - 1, 0)

    return [_att_spec(cur, 0), _att_spec(prev, 1), _att_spec(cur, 1), _att_spec(prev, 2), _att_spec(cur, 2),
            _att_spec(lambda n: last, 1), _att_spec(lambda n: last, 2)]


def _att_prev(seg, n, prev_ref, last_ref, rr, sl):
    t = prev_ref[rr, :, sl]
    if seg and rr > 0:
        t = jnp.where(n == 0, last_ref[rr - 1, :, sl], t)
    return t


def _att_fwd(grp, S, qkv):
    _, d = ATT_GROUPS[grp]
    seg = d == 1
    P = _att_planes(d)
    L = S // P
    nb = L // BLK
    assert P % ATT_PLANES_PER_STEP == 0 and (not seg or P == ATT_PLANES_PER_STEP)
    slopes = _att_slopes(grp)
    scale = HD ** -0.5
    chains = _att_chains()

    def body(q_ref, kp_ref, kc_ref, vp_ref, vc_ref, kl_ref, vl_ref, o_ref, l_ref, s_buf, p_buf, den_buf):
        n = pl.program_id(1)
        valid, valid_all, distf = _att_mask(n)
        for c, rr, sl, hh in chains:
            k = jnp.concatenate([_att_prev(seg, n, kp_ref, kl_ref, rr, sl), kc_ref[rr, :, sl]], axis=0)
            s_buf[c] = _dot_nt(q_ref[rr, :, sl], k)
        for c, rr, sl, hh in chains:
            s = s_buf[c] * scale + (-slopes[hh] * d) * distf
            s = jnp.where(valid_all if seg and rr > 0 else valid, s, -1e30)
            m = jnp.max(s, axis=-1, keepdims=True)
            p = jnp.exp(s - m)
            den = jnp.sum(p, axis=-1, keepdims=True)
            p_buf[c] = p.astype(CDT)
            den_buf[c] = jnp.broadcast_to(den, (BLK, HD))
            l_ref[rr, :, sl] = jnp.broadcast_to(m + jnp.log(den), (BLK, HD))
        for c, rr, sl, hh in chains:
            v = jnp.concatenate([_att_prev(seg, n, vp_ref, vl_ref, rr, sl), vc_ref[rr, :, sl]], axis=0)
            o_ref[rr, :, sl] = _dot_nn(p_buf[c], v) / den_buf[c]

    out_spec = _att_spec(lambda n: n)
    n_ch = len(chains)
    q3 = _att_3d(qkv, d)
    o, l = pl.pallas_call(
        body, name="att_fwd_g%d" % grp, grid=(P // ATT_PLANES_PER_STEP, nb),
        in_specs=_att_qkv_specs(nb),
        out_specs=[out_spec, out_spec],
        out_shape=[_sds((P, L, GW), F32)] * 2,
        scratch_shapes=[pltpu.VMEM((n_ch, BLK, 2 * BLK), F32), pltpu.VMEM((n_ch, BLK, 2 * BLK), CDT),
                        pltpu.VMEM((n_ch, BLK, HD), F32)],
        compiler_params=_params(2),
    )(*[q3] * 7)
    return o.reshape(_rm_shape(S, d, GW)), l.reshape(_rm_shape(S, d, GW))


def _att_bwd(grp, S, qkv, lse, do_g, c_g):
    _, d = ATT_GROUPS[grp]
    seg = d == 1
    P = _att_planes(d)
    L = S // P
    nb = L // BLK
    assert P % ATT_PLANES_PER_STEP == 0 and (not seg or P == ATT_PLANES_PER_STEP)
    slopes = _att_slopes(grp)
    scale = HD ** -0.5
    last = nb - 1
    chains = _att_chains()

    def body(q_ref, kp_ref, kc_ref, vp_ref, vc_ref, kl_ref, vl_ref, l_ref, do_ref, c_ref, dq_ref, dk_ref, dv_ref,
             ck, cv, fk, fv, s_buf, dp_buf, p_buf, ds_buf):
        n = pl.program_id(1)

        @pl.when(n == 0)
        def _():
            for buf in (ck, cv, fk, fv):
                buf[...] = jnp.zeros_like(buf)

        @pl.when(n < nb)
        def _():
            valid, valid_all, distf = _att_mask(n)
            for c, rr, sl, hh in chains:
                k = jnp.concatenate([_att_prev(seg, n, kp_ref, kl_ref, rr, sl), kc_ref[rr, :, sl]], axis=0)
                v = jnp.concatenate([_att_prev(seg, n, vp_ref, vl_ref, rr, sl), vc_ref[rr, :, sl]], axis=0)
                s_buf[c] = _dot_nt(q_ref[rr, :, sl], k)
                dp_buf[c] = _dot_nt(do_ref[rr, :, sl], v)
            for c, rr, sl, hh in chains:
                s = s_buf[c] * scale + (-slopes[hh] * d) * distf
                p = jnp.where(valid_all if seg and rr > 0 else valid, jnp.exp(s - l_ref[rr, :, sl][:, 0:1]), 0.0)
                p_buf[c] = p.astype(CDT)
                ds_buf[c] = (p * (dp_buf[c] + c_ref[rr, :, sl][:, 0:1]) * scale).astype(CDT)
            for c, rr, sl, hh in chains:
                k = jnp.concatenate([_att_prev(seg, n, kp_ref, kl_ref, rr, sl), kc_ref[rr, :, sl]], axis=0)
                ds = ds_buf[c]
                dq_ref[rr, :, sl] = _dot_nn(ds, k)
                dk = _dot_tn(ds, q_ref[rr, :, sl])
                dv = _dot_tn(p_buf[c], do_ref[rr, :, sl])
                dk_ref[rr, :, sl] = ck[rr, :, sl] + dk[0:BLK]
                dv_ref[rr, :, sl] = cv[rr, :, sl] + dv[0:BLK]
                ck[rr, :, sl] = dk[BLK:2 * BLK]
                cv[rr, :, sl] = dv[BLK:2 * BLK]
                if seg and rr > 0:
                    @pl.when(n == 0)
                    def _(rr=rr, sl=sl, dk=dk, dv=dv):
                        fk[rr - 1, :, sl] = dk[0:BLK]
                        fv[rr - 1, :, sl] = dv[0:BLK]

        @pl.when(n == nb)
        def _():
            dk_ref[...] = ck[...] + fk[...]
            dv_ref[...] = cv[...] + fv[...]

    blk = (ATT_PLANES_PER_STEP, BLK, GW)
    at_q = _att_spec(lambda n: jnp.minimum(n, last))
    behind = _att_spec(lambda n: jnp.maximum(n - 1, 0))
    n_ch = len(chains)
    q3 = _att_3d(qkv, d)
    res = pl.pallas_call(
        body, name="att_bwd_g%d" % grp, grid=(P // ATT_PLANES_PER_STEP, nb + 1),
        in_specs=_att_qkv_specs(nb) + [at_q, at_q, at_q],
        out_specs=[at_q, behind, behind],
        out_shape=[_sds((P, L, GW), F32)] * 3,
        scratch_shapes=[pltpu.VMEM(blk, F32)] * 4
        + [pltpu.VMEM((n_ch, BLK, 2 * BLK), F32), pltpu.VMEM((n_ch, BLK, 2 * BLK), F32),
           pltpu.VMEM((n_ch, BLK, 2 * BLK), CDT), pltpu.VMEM((n_ch, BLK, 2 * BLK), CDT)],
        compiler_params=_params(2),
    )(*[q3] * 7, _att_3d(lse, d), _att_3d(do_g, d), _att_3d(c_g, d))
    return [t.reshape(_rm_shape(S, d, GW)) for t in res]


def _mix_alpha(l0, l1, l2):
    mx = jnp.maximum(jnp.maximum(l0, l1), l2)
    e = [jnp.exp(l0 - mx), jnp.exp(l1 - mx), jnp.exp(l2 - mx)]
    tot = e[0] + e[1] + e[2]
    return [ei / tot for ei in e]


def _mix_fwd(S, os_, ls_, tm=512):
    dil = [d for _, d in ATT_GROUPS]

    def body(*refs):
        out, bufs = refs[6], refs[7:13]
        get = [_rm_reader(bufs[j], refs[j], dil[j % 3]) for j in range(6)]
        rows = slice(None)
        for s in range(ATT_HPG):
            al = _mix_alpha(*[get[3 + g](s, rows) for g in range(3)])
            mixed = al[0] * get[0](s, rows) + al[1] * get[1](s, rows) + al[2] * get[2](s, rows)
            out[:, s * HD:(s + 1) * HD] = mixed.astype(out.dtype)

    specs = [_rm_spec(tm, d, GW) for d in dil]
    return pl.pallas_call(
        body, name="mix_fwd", grid=(S // tm,), in_specs=specs * 2, out_specs=pl.BlockSpec((tm, GW), lambda i: (i, 0)),
        out_shape=_sds((S, GW), CDT), scratch_shapes=[pltpu.VMEM((ATT_HPG, tm, HD), F32)] * 6,
        compiler_params=_params(1))(*os_, *ls_)


def _mix_bwd(S, os_, ls_, do_a, tm=512):
    dil = [d for _, d in ATT_GROUPS]

    def body(*refs):
        d_ref, outs, bufs, tmps = refs[6], refs[7:13], refs[13:19], refs[19:25]
        get = [_rm_reader(bufs[j], refs[j], dil[j % 3]) for j in range(6)]
        for s in range(ATT_HPG):
            cols = slice(s * HD, (s + 1) * HD)

            def chunk(rows, s=s, cols=cols):
                al = _mix_alpha(*[get[3 + g](s, rows) for g in range(3)])
                dv = d_ref[rows, cols]
                o_a = al[0] * get[0](s, rows) + al[1] * get[1](s, rows) + al[2] * get[2](s, rows)
                dsum = jnp.sum(dv * o_a, axis=-1, keepdims=True)
                for g in range(3):
                    tmps[g][rows, :] = al[g] * dv
                    tmps[3 + g][rows, :] = -(al[g] * dsum)

            chunk(slice(None))
            for j in range(6):
                _rm_put(outs[j], cols, tmps[j], dil[j % 3])

    specs = [_rm_spec(tm, d, GW) for d in dil]
    res = pl.pallas_call(
        body, name="mix_bwd", grid=(S // tm,), in_specs=specs * 2 + [pl.BlockSpec((tm, GW), lambda i: (i, 0))],
        out_specs=specs * 2,
        out_shape=[_sds(_rm_shape(S, d, GW), CDT) for d in dil] + [_sds(_rm_shape(S, d, GW), F32) for d in dil],
        scratch_shapes=[pltpu.VMEM((ATT_HPG, tm, HD), F32)] * 6 + [pltpu.VMEM((tm, HD), F32)] * 6,
        compiler_params=_params(1))(*os_, *ls_, do_a)
    return res[:3], res[3:]


def _ret_tables(dk):
    H, C = RET_HEADS, BLK
    log_g = jnp.log(1.0 - 2.0 ** (-5.0 - jnp.arange(H, dtype=F32)))
    idx = jnp.arange(C, dtype=F32)
    diff = idx[:, None] - idx[None, :]
    decay = jnp.where(diff >= 0, jnp.exp(log_g[:, None, None] * jnp.maximum(diff, 0.0)), 0.0)
    xi = jnp.exp(log_g[:, None] * (idx[None, :] + 1.0))
    zeta = jnp.exp(log_g[:, None] * (C - 1.0 - idx[None, :])) * (dk ** -0.5)
    g_chunk = jnp.exp(log_g * C)
    bc = lambda t: jnp.broadcast_to(t[:, :, None], (H, C, C))
    return decay, bc(xi), bc(zeta), jnp.broadcast_to(g_chunk[:, None, None], (H, 8, C))


def _gn_fwd(o, g, b):
    mu = jnp.mean(o, axis=-1, keepdims=True)
    xc = o - mu
    rstd = lax.rsqrt(jnp.mean(xc * xc, axis=-1, keepdims=True) + EPS)
    yh = xc * rstd
    return yh, rstd, yh * g + b


def _ret_specs(dk, dv, order):
    H = RET_HEADS
    qk_w, v_w = H * dk, H * dv
    off_q = 3 * ATT_W
    off_k, off_v, off_g = off_q + qk_w, off_q + 2 * qk_w, off_q + 2 * qk_w + v_w
    assert 2 * dk == dv and all(off % dv == 0 for off in (off_q, off_k, off_v, off_g))

    def col(off, j):
        return pl.BlockSpec((BLK, dv), lambda i: (order(i), off // dv + j))

    tab = pl.BlockSpec((H, BLK, BLK), lambda i: (0, 0, 0))
    return ([col(off_q, j) for j in range(H // 2)] + [col(off_k, j) for j in range(H // 2)]
            + [col(off_v, j) for j in range(H)] + [col(off_g, j) for j in range(H)]
            + [tab, tab, tab, pl.BlockSpec((H, 8, BLK), lambda i: (0, 0, 0))])


def _ret_heads(refs, dk):
    H = RET_HEADS
    q_refs, k_refs = refs[0:H // 2], refs[H // 2:H]
    v_refs, gr_refs = refs[H:2 * H], refs[2 * H:3 * H]

    def head(h):
        cols = slice((h % 2) * dk, (h % 2 + 1) * dk)
        return q_refs[h // 2][:, cols], k_refs[h // 2][:, cols], v_refs[h][...], gr_refs[h][...]

    return head, refs[3 * H:3 * H + 4]


def _ret_fwd(proj, gn_g, gn_b, dk, dv):
    S = proj.shape[0]
    N = S // BLK
    H = RET_HEADS
    kscale = dk ** -0.5
    n_in = 3 * H + 4

    def body(*refs):
        head, (dec_ref, xi_ref, zeta_ref, gc_ref) = _ret_heads(refs, dk)
        g_ref, b_ref, opre_ref, or_ref, st_ref, state, s_buf, cross_buf = refs[n_in:n_in + 8]
        n = pl.program_id(0)

        @pl.when(n == 0)
        def _():
            state[...] = jnp.zeros_like(state)

        for h in range(H):
            q, k, v, _ = head(h)
            s_buf[h] = _dot_nt(q, k)
            st = state[h]
            st_c = st.astype(CDT)
            st_ref[h] = st_c
            cross_buf[h] = _dot_nn(q, st_c)
            kz = (k.astype(F32) * zeta_ref[h][:, 0:1]).astype(CDT)
            state[h] = st * gc_ref[h][0:1, 0:1] + _dot_tn(kz, v)
        for h in range(H):
            vs = slice(h * dv, (h + 1) * dv)
            _, _, v, gr = head(h)
            s = s_buf[h] * kscale * dec_ref[h]
            o = _dot_nn(s.astype(CDT), v) + cross_buf[h] * xi_ref[h][:, 0:1]
            opre_ref[:, vs] = o
            _, _, y = _gn_fwd(o, g_ref[:, vs], b_ref[:, vs])
            gr = gr.astype(F32)
            or_ref[:, vs] = (y * (gr * _sigmoid(gr))).astype(or_ref.dtype)

    v_w = H * dv
    row = pl.BlockSpec((1, v_w), lambda i: (0, 0))
    tile = pl.BlockSpec((BLK, v_w), lambda i: (i, 0))
    return pl.pallas_call(
        body, name="ret_fwd", grid=(N,),
        in_specs=_ret_specs(dk, dv, lambda i: i) + [row, row],
        out_specs=[tile, tile, pl.BlockSpec((None, H, dk, dv), lambda i: (i, 0, 0, 0))],
        out_shape=[_sds((S, v_w), F32), _sds((S, v_w), CDT), _sds((N, H, dk, dv), CDT)],
        scratch_shapes=[pltpu.VMEM((H, dk, dv), F32), pltpu.VMEM((H, BLK, BLK), F32), pltpu.VMEM((H, BLK, dv), F32)],
        compiler_params=_params(1),
    )(*[proj] * (3 * H), *_ret_tables(dk), gn_g, gn_b)


def _ret_bwd(proj, gn_g, gn_b, o_pre, states, d_or, dga, dgb, dk, dv):
    S, in_w = proj.shape
    N = S // BLK
    H = RET_HEADS
    qk_w, v_w = H * dk, H * dv
    kscale = dk ** -0.5
    n_in = 3 * H + 4
    out_w = 2 * qk_w + 2 * v_w
    gate_w = dga.shape[1]
    col0 = 3 * ATT_W
    assert col0 + out_w + 2 * gate_w == in_w
    rev = lambda i: N - 1 - i

    def body(*refs):
        head, (dec_ref, xi_ref, zeta_ref, gc_ref) = _ret_heads(refs, dk)
        (g_ref, b_ref, opre_ref, st_ref, dor_ref, dga_ref, dgb_ref, dproj_ref, dg_ref, db_ref, dstate, stage,
         sem, do_buf, dox_buf, a_buf, g_buf, dq_buf, dk_buf, dv_buf) = refs[n_in:n_in + 20]
        i = pl.program_id(0)
        slot = i % 2
        out_ref = stage.at[slot]

        def out_copy(s, step):
            rows = pl.ds(pl.multiple_of(rev(step) * BLK, BLK), BLK)
            return pltpu.make_async_copy(stage.at[s], dproj_ref.at[rows, pl.ds(col0, in_w - col0)], sem.at[s])

        @pl.when(i >= 2)
        def _():
            out_copy(slot, i - 2).wait()

        @pl.when(i == 0)
        def _():
            dstate[...] = jnp.zeros_like(dstate)
            dg_ref[...] = jnp.zeros_like(dg_ref)
            db_ref[...] = jnp.zeros_like(db_ref)

        out_ref[:, out_w:out_w + gate_w] = dga_ref[...]
        out_ref[:, out_w + gate_w:out_w + 2 * gate_w] = dgb_ref[...]
        for h in range(H):
            vs = slice(h * dv, (h + 1) * dv)
            _, _, _, gr = head(h)
            gr = gr.astype(F32)
            sg = _sigmoid(gr)
            gain = g_ref[:, vs]
            yh, rstd, y = _gn_fwd(opre_ref[:, vs], gain, b_ref[:, vs])
            d_or_v = dor_ref[:, vs]
            dy = d_or_v * (gr * sg)
            out_ref[:, 2 * qk_w + v_w + h * dv:2 * qk_w + v_w + (h + 1) * dv] = (
                d_or_v * y * (sg * (1.0 + gr * (1.0 - sg)))).astype(out_ref.dtype)
            dg_ref[:, vs] += jnp.sum(dy * yh, axis=0, keepdims=True)
            db_ref[:, vs] += jnp.sum(dy, axis=0, keepdims=True)
            dyh = dy * gain
            do = rstd * (dyh - jnp.mean(dyh, axis=-1, keepdims=True)
                         - yh * jnp.mean(dyh * yh, axis=-1, keepdims=True))
            do_buf[h] = do.astype(CDT)
            dox_buf[h] = (do * xi_ref[h][:, 0:1]).astype(CDT)
        for h in range(H):
            q, k, v, _ = head(h)
            dox = dox_buf[h]
            a_buf[h] = _dot_nt(q, k)
            g_buf[h] = _dot_nt(do_buf[h], v)
            dsn = dstate[h]
            dsn_c = dsn.astype(CDT)
            kz = (k.astype(F32) * zeta_ref[h][:, 0:1]).astype(CDT)
            dq_buf[h] = _dot_nt(dox, st_ref[h])
            dk_buf[h] = _dot_nt(v, dsn_c)
            dv_buf[h] = _dot_nn(kz, dsn_c)
            dstate[h] = dsn * gc_ref[h][0:1, 0:1] + _dot_tn(q, dox)
        for h in range(H):
            q, k, _, _ = head(h)
            decay = dec_ref[h]
            a_c = (a_buf[h] * kscale * decay).astype(CDT)
            g_c = (g_buf[h] * decay).astype(CDT)
            dq = _dot_nn(g_c, k) * kscale + dq_buf[h]
            dkk = _dot_tn(g_c, q) * kscale + dk_buf[h] * zeta_ref[h][:, 0:1]
            dvv = _dot_tn(a_c, do_buf[h]) + dv_buf[h]
            out_ref[:, h * dk:(h + 1) * dk] = dq.astype(out_ref.dtype)
            out_ref[:, qk_w + h * dk:qk_w + (h + 1) * dk] = dkk.astype(out_ref.dtype)
            out_ref[:, 2 * qk_w + h * dv:2 * qk_w + (h + 1) * dv] = dvv.astype(out_ref.dtype)

        cp = out_copy(slot, i)
        cp.start()

        @pl.when(i == N - 1)
        def _():
            cp.wait()
            if N >= 2:
                out_copy(1 - slot, i - 1).wait()

    row = pl.BlockSpec((1, v_w), lambda i: (0, 0))
    tile = pl.BlockSpec((BLK, v_w), lambda i: (rev(i), 0))
    gate = pl.BlockSpec((BLK, gate_w), lambda i: (rev(i), 0))
    return pl.pallas_call(
        body, name="ret_bwd", grid=(N,),
        in_specs=_ret_specs(dk, dv, rev) + [row, row, tile,
                 pl.BlockSpec((None, H, dk, dv), lambda i: (rev(i), 0, 0, 0)), tile, gate, gate],
        out_specs=[pl.BlockSpec(memory_space=pl.ANY), row, row],
        out_shape=[_sds((S, in_w), CDT), _sds((1, v_w), F32), _sds((1, v_w), F32)],
        scratch_shapes=[pltpu.VMEM((H, dk, dv), F32), pltpu.VMEM((2, BLK, in_w - col0), CDT),
                        pltpu.SemaphoreType.DMA((2,)),
                        pltpu.VMEM((H, BLK, dv), CDT), pltpu.VMEM((H, BLK, dv), CDT),
                        pltpu.VMEM((H, BLK, BLK), F32), pltpu.VMEM((H, BLK, BLK), F32),
                        pltpu.VMEM((H, BLK, dk), F32), pltpu.VMEM((H, BLK, dk), F32), pltpu.VMEM((H, BLK, dv), F32)],
        compiler_params=_params(1),
    )(*[proj] * (3 * H), *_ret_tables(dk), gn_g, gn_b, o_pre, states, d_or, dga, dgb)


def _merge_fwd(o_a, o_r, wa, wb, proj, d_model, tm=1024, tn=512):
    S, in_w = proj.shape
    off_a, off_b = in_w - 2 * d_model, in_w - d_model
    assert off_a % tn == 0 and off_b % tn == 0

    def body(oa_ref, or_ref, wa_ref, wb_ref, ga_ref, gb_ref, y_ref, pa_ref, pb_ref):
        pa = _dot_nn(oa_ref[...], wa_ref[...])
        pb = _dot_nn(or_ref[...], wb_ref[...])
        y = _sigmoid(ga_ref[...].astype(F32)) * pa + _sigmoid(gb_ref[...].astype(F32)) * pb
        y_ref[...] = y.astype(y_ref.dtype)
        pa_ref[...] = pa.astype(pa_ref.dtype)
        pb_ref[...] = pb.astype(pb_ref.dtype)

    ka, kb = o_a.shape[1], o_r.shape[1]
    out = pl.BlockSpec((tm, tn), lambda i, j: (i, j))
    return pl.pallas_call(
        body, name="merge_fwd", grid=(S // tm, d_model // tn),
        in_specs=[pl.BlockSpec((tm, ka), lambda i, j: (i, 0)), pl.BlockSpec((tm, kb), lambda i, j: (i, 0)),
                  pl.BlockSpec((ka, tn), lambda i, j: (0, j)), pl.BlockSpec((kb, tn), lambda i, j: (0, j)),
                  pl.BlockSpec((tm, tn), lambda i, j: (i, off_a // tn + j)),
                  pl.BlockSpec((tm, tn), lambda i, j: (i, off_b // tn + j))],
        out_specs=[out, out, out], out_shape=[_sds((S, d_model), CDT)] * 3,
        compiler_params=_params(2))(o_a, o_r, wa, wb, proj, proj)


def _local_step(x, target, w_in_mine, chip, others, near_w_in, far_w_in, small, late_weights, on_grads, deps0=()):
    S, D = x.shape
    ns_in = w_in_mine.shape[1]
    in_w = N_CHIPS * ns_in
    d_ff = 4 * D
    ret_v_w = 2 * D
    dv = ret_v_w // RET_HEADS
    dk = (in_w - 3 * ATT_W - 2 * ret_v_w - 2 * D) // (2 * RET_HEADS)
    gqk = jnp.concatenate([small["q_norm_g"].reshape(1, ATT_W), small["k_norm_g"].reshape(1, ATT_W)], axis=1)
    g1, g2 = small["norm1_g"], small["norm2_g"]
    gn_g, gn_b = small["ret_gn_g"], small["ret_gn_b"]

    proj_sds = _sds((S, in_w), CDT)
    xn, proj = _in_proj_mine(x, g1, w_in_mine, chip, proj_sds, deps0)
    for stage, (get_w_in, chips) in enumerate(((near_w_in, others[:2]), (far_w_in, others[2:]))):
        w_in, started = get_w_in(proj)
        (proj,) = _matmul(
            "in_proj_far%d" % stage, "nn", xn, w_in, tm=512, tn=ns_in, tk=D, prefetch=[chips],
            n_cols=chips.shape[0] * ns_in, b_spec=pl.BlockSpec((None, D, ns_in), lambda i, j, k, o: (o[j], 0, 0)),
            outs=[(proj_sds, pl.BlockSpec((512, ns_in), lambda i, j, k, o: (i, o[j])))], epilogue=_ep_store,
            deps=[proj, started], alias_dep_to_out=(0, 0), j_outer=True)
    qkv = _qknorm_fwd(proj, gqk)
    att = [_att_fwd(g, S, qkv[g]) for g in range(3)]
    os_, ls_ = [a[0] for a in att], [a[1] for a in att]
    o_a = _mix_fwd(S, os_, ls_)
    o_pre, o_r, states = _ret_fwd(proj, gn_g, gn_b, dk, dv)
    w = late_weights(o_r)
    y, pa, pb = _merge_fwd(o_a, o_r, w["w_proj_a"], w["w_proj_b"], proj, D)
    x1, xn2 = _matmul("out_proj", "nn", y, w["w_out"], tm=1024, tn=D, tk=D,
                      extras=[(x, _mn(1024, D)), (g2, _row(D))],
                      outs=[(_sds((S, D), F32), _mn(1024, D)), (_sds((S, D), CDT), _mn(1024, D))],
                      epilogue=_ep_resid_norm)
    hid, act = _matmul("mlp_up", "nn", xn2, w["w_up"], tm=512, tn=2048, tk=D, j_outer=True,
                       outs=[(_sds((S, d_ff), CDT), _mn(512, 2048))] * 2, epilogue=_ep_up)
    dx2, dx2c, loss_row = _matmul(
        "mlp_down_loss", "nn", act, w["w_down"], tm=512, tn=D, tk=d_ff,
        extras=[(x1, _mn(512, D)), (target, _mn(512, D))],
        outs=[(_sds((S, D), F32), _mn(512, D)), (_sds((S, D), CDT), _mn(512, D)), (_sds((1, D), F32), _row(D))],
        epilogue=functools.partial(_ep_down_loss, inv_d=1.0 / D))
    loss = 0.5 * jnp.sum(loss_row) / D

    (dh,) = _matmul("d_hidden", "nt", dx2c, w["w_down"], tm=512, tn=2048, tk=D, j_outer=True,
                    extras=[(hid, _mn(512, 2048))], outs=[(_sds((S, d_ff), CDT), _mn(512, 2048))], epilogue=_ep_dh)
    (gw_down,) = _matmul("dw_down", "tn", act, dx2c, tm=1024, tn=D, tk=1024,
                         outs=[(_sds((d_ff, D), F32), _mn(1024, D))], epilogue=_ep_store)
    (gw_up,) = _matmul("dw_up", "tn", xn2, dh, tm=D, tn=1024, tk=1024,
                       outs=[(_sds((D, d_ff), F32), _mn(D, 1024))], epilogue=_ep_store)
    tok = on_grads({"w_down": gw_down, "w_up": gw_up})
    dx1, dx1c, dg2 = _matmul(
        "d_x1", "nt", dh, w["w_up"], tm=512, tn=D, tk=d_ff,
        extras=[(x1, _mn(512, D)), (g2, _row(D)), (dx2, _mn(512, D))],
        outs=[(_sds((S, D), F32), _mn(512, D)), (_sds((S, D), CDT), _mn(512, D)), (_sds((1, D), F32), _row(D))],
        epilogue=_ep_rms_bwd, deps=[tok])

    gt = 512
    assert (in_w - 2 * D) % gt == 0
    off_a, off_b = (in_w - 2 * D) // gt, (in_w - D) // gt
    dpa, dpb, dga, dgb = _matmul(
        "d_gates", "nt", dx1c, w["w_out"], tm=1024, tn=gt, tk=D,
        extras=[(proj, _mn(1024, gt, off_a)), (proj, _mn(1024, gt, off_b)), (pa, _mn(1024, gt)),
                (pb, _mn(1024, gt))],
        outs=[(_sds((S, D), CDT), _mn(1024, gt))] * 4, epilogue=_ep_gates)
    (gw_out,) = _matmul("dw_out", "tn", y, dx1c, tm=D, tn=D, tk=1024,
                        outs=[(_sds((D, D), F32), _mn(D, D))], epilogue=_ep_store)
    (gw_pa,) = _matmul("dw_proj_a", "tn", o_a, dpa, tm=GW, tn=D, tk=1024,
                       outs=[(_sds((GW, D), F32), _mn(GW, D))], epilogue=_ep_store)
    (gw_pb,) = _matmul("dw_proj_b", "tn", o_r, dpb, tm=1024, tn=D, tk=1024,
                       outs=[(_sds((ret_v_w, D), F32), _mn(1024, D))], epilogue=_ep_store)
    (do_a,) = _matmul("d_o_a", "nt", dpa, w["w_proj_a"], tm=1024, tn=GW, tk=D,
                      outs=[(_sds((S, GW), F32), _mn(1024, GW))], epilogue=_ep_store)
    tok = on_grads({"w_out": gw_out, "w_proj_a": gw_pa, "w_proj_b": gw_pb})
    (d_or,) = _matmul("d_o_r", "nt", dpb, w["w_proj_b"], tm=512, tn=ret_v_w, tk=D,
                      outs=[(_sds((S, ret_v_w), F32), _mn(512, ret_v_w))], epilogue=_ep_store, deps=[tok])

    dproj, dgn_g, dgn_b = _ret_bwd(proj, gn_g, gn_b, o_pre, states, d_or, dga, dgb, dk, dv)
    do_gs, c_gs = _mix_bwd(S, os_, ls_, do_a)
    datt_parts = [_att_bwd(g, S, qkv[g], ls_[g], do_gs[g], c_gs[g]) for g in range(3)]
    dproj, dgqk = _qknorm_bwd(proj, gqk, [p[0] for p in datt_parts], [p[1] for p in datt_parts],
                              [p[2] for p in datt_parts], dproj)

    (gw_in,) = _matmul(
        "dw_in", "tn", xn, dproj, tm=512, tn=ns_in, tk=1024,
        outs=[(_sds((N_CHIPS, D, ns_in), F32), pl.BlockSpec((None, 512, ns_in), lambda i, j, k: (j, i, 0)))],
        epilogue=_ep_store)
    tok = on_grads({"w_in": gw_in})
    grad_x, dg1 = _matmul(
        "d_x", "nt", dproj, w_in, tm=512, tn=D, tk=ns_in, n_cols=D,
        b_spec=pl.BlockSpec((None, D, ns_in), lambda i, j, k: (k, 0, 0)),
        extras=[(x, _mn(512, D)), (g1, _row(D)), (dx1, _mn(512, D))],
        outs=[(_sds((S, D), F32), _mn(512, D)), (_sds((1, D), F32), _row(D))],
        epilogue=_ep_rms_bwd, deps=[tok])

    smallg = {"norm1_g": dg1, "q_norm_g": dgqk[:, :ATT_W], "k_norm_g": dgqk[:, ATT_W:],
              "ret_gn_g": dgn_g, "ret_gn_b": dgn_b, "norm2_g": dg2}
    return loss, grad_x, smallg


N_CHIPS = 4
N_DEV = 8


def _place():
    x, y, c = lax.axis_index("x"), lax.axis_index("y"), lax.axis_index("c")
    return x, y, c


def _other_chips(x, y):
    out = []
    for fx, fy in ((1, 0), (0, 1), (1, 1)):
        px = 1 - x if fx else x
        py = 1 - y if fy else y
        out.append(((px, py), 2 * px + py))
    return out


SEM_SPEC = pl.BlockSpec(memory_space=pltpu.SEMAPHORE)
ANY_SPEC = pl.BlockSpec(memory_space=pl.ANY)
EFFECT = pltpu.SideEffectType.DATAFLOW_SIDE_EFFECTING


def _ici_copies(kind, srcs, lands, send, recv, which=(0, 1, 2)):
    x, y, c = _place()
    me = 2 * x + y
    out = []
    for w, (s, l) in enumerate(zip(srcs, lands)):
        for j, ((px, py), pidx) in enumerate(_other_chips(x, y)):
            if j not in which:
                continue
            if kind == "gather":
                half = s.shape[0] // 2
                rows = pl.ds(c * half, half)
                src, dst_there, dst_here = s.at[rows, :], l.at[me, rows, :], l.at[pidx, rows, :]
            else:
                src, dst_there, dst_here = s.at[pidx], l.at[me], l.at[pidx]
            out.append((src, dst_there, dst_here, send.at[3 * w + j], recv.at[3 * w + j], (px, py, c)))
    return out


def _exchange_start(name, kind, srcs, land_shapes, which=(0, 1, 2), lands=None):
    n = len(srcs)
    if lands is None:
        lands = [lax.empty(shape, dtype) for shape, dtype in land_shapes]

    def body(*refs):
        src_refs, land_refs = refs[:n], refs[n:2 * n]
        send, recv = refs[2 * n], refs[2 * n + 1]
        token = refs[-1]
        for src, dst, _, ss, rs, dev in _ici_copies(kind, src_refs, land_refs, send, recv, which):
            pltpu.make_async_remote_copy(src_ref=src, dst_ref=dst, send_sem=ss, recv_sem=rs, device_id=dev,
                                         device_id_type=MESH).start()
        token[...] = jnp.zeros_like(token)

    thru = [pltpu.HBM(s.shape, s.dtype) for s in srcs] + [pltpu.HBM(shape, dtype) for shape, dtype in land_shapes]
    res = pl.pallas_call(
        body, name=name,
        out_shape=(pltpu.SemaphoreType.DMA((3 * n,)), pltpu.SemaphoreType.DMA((3 * n,)), *thru, _sds((8, LANES), F32)),
        in_specs=[HBM_SPEC] * (2 * n), out_specs=(SEM_SPEC, SEM_SPEC, *[HBM_SPEC] * (2 * n), VMEM_SPEC),
        input_output_aliases={i: 2 + i for i in range(2 * n)},
        compiler_params=pltpu.CompilerParams(has_side_effects=EFFECT),
    )(*[pltpu.with_memory_space_constraint(s, pltpu.HBM) for s in srcs],
      *[pltpu.with_memory_space_constraint(l, pltpu.HBM) for l in lands])
    return res[0], res[1], list(res[2:2 + n]), list(res[2 + n:2 + 2 * n]), res[-1]


def _exchange_wait(name, kind, send, recv, srcs, lands, after, which=(0, 1, 2)):
    n = len(srcs)

    def body(*refs):
        src_refs, land_refs = refs[:n], refs[n:2 * n]
        send_ref, recv_ref = refs[2 * n], refs[2 * n + 1]
        for src, _, dst, ss, rs, dev in _ici_copies(kind, src_refs, land_refs, send_ref, recv_ref, which):
            cp = pltpu.make_async_remote_copy(src_ref=src, dst_ref=dst, send_sem=ss, recv_sem=rs, device_id=dev,
                                              device_id_type=MESH)
            cp.wait_send()
            cp.wait_recv()

    thru = [pltpu.HBM(t.shape, t.dtype) for t in list(srcs) + list(lands)]
    res = pl.pallas_call(
        body, name=name, out_shape=thru,
        in_specs=[HBM_SPEC] * (2 * n) + [SEM_SPEC, SEM_SPEC, ANY_SPEC], out_specs=[HBM_SPEC] * (2 * n),
        input_output_aliases={i: i for i in range(2 * n)},
        compiler_params=pltpu.CompilerParams(has_side_effects=EFFECT),
    )(*srcs, *lands, send, recv, after)
    return list(res[:n]), list(res[n:])


PAIR_TILE_ELEMS = 1 << 19


def _pair_fill(name, gathered, mine, core, others, chip, write_mine=True):
    k, r, C = gathered.shape
    half = r // 2
    tr = _row_tile(half, C, PAIR_TILE_ELEMS, mult=16)
    nt = half // tr
    n_far = others.shape[0]

    def body(c_ref, o_ref, chip_ref, in_ref, mine_ref, out_ref, slot, send, recv):
        j = pl.program_id(0)
        _sibling_barrier((j == 0) & (pl.program_id(1) == 0))
        b = (j * nt + pl.program_id(1)) % 2
        x, y, c = _place()
        cp = pltpu.make_async_remote_copy(src_ref=in_ref, dst_ref=slot.at[b], send_sem=send.at[b],
                                          recv_sem=recv.at[b], device_id=(x, y, 1 - c), device_id_type=MESH)

        @pl.when(j < n_far)
        def _():
            cp.start()
            cp.wait_recv()
            out_ref[...] = slot[b]
            cp.wait_send()

        @pl.when(j >= n_far)
        def _():
            out_ref[...] = mine_ref[...]

    def far(j):
        return jnp.minimum(j, n_far - 1)

    grid_spec = pltpu.PrefetchScalarGridSpec(
        num_scalar_prefetch=3, grid=(n_far + (2 if write_mine else 0), nt),
        in_specs=[pl.BlockSpec((tr, C), lambda j, i, c, o, m: (
                      (2 * o[far(j)] + c[0]) * nt + jnp.where(j < n_far, i, nt - 1), 0)),
                  pl.BlockSpec((tr, C), lambda j, i, c, o, m: (jnp.where(j < n_far, 0, (j - n_far) * nt + i), 0))],
        out_specs=pl.BlockSpec((tr, C), lambda j, i, c, o, m: (
            jnp.where(j < n_far, 2 * o[far(j)] + 1 - c[0], 2 * m[0] + j - n_far) * nt + i, 0)),
        scratch_shapes=[pltpu.VMEM((2, tr, C), gathered.dtype), pltpu.SemaphoreType.DMA((2,)),
                        pltpu.SemaphoreType.DMA((2,))])
    out = pl.pallas_call(body, name=name, grid_spec=grid_spec, out_shape=_sds((k * r, C), gathered.dtype),
                         input_output_aliases={3: 0}, compiler_params=_params(2, PAIR_FILL_ID))(
                             core, others, chip, gathered.reshape(k * r, C), mine)
    return out.reshape(k, r, C)


def _pair_reduce(name, g, core):
    k, R, C = g.shape
    half = R // 2
    tr = _row_tile(half, C, PAIR_TILE_ELEMS, mult=16)
    nt = half // tr

    def body(c_ref, mine_ref, give_ref, out_ref, wire_ref, stage, slot, send, recv):
        _sibling_barrier((pl.program_id(0) == 0) & (pl.program_id(1) == 0))
        b = (pl.program_id(0) * nt + pl.program_id(1)) % 2
        x, y, c = _place()
        stage[b] = give_ref[...].astype(stage.dtype)
        cp = pltpu.make_async_remote_copy(src_ref=stage.at[b], dst_ref=slot.at[b], send_sem=send.at[b],
                                          recv_sem=recv.at[b], device_id=(x, y, 1 - c), device_id_type=MESH)
        cp.start()
        cp.wait_recv()
        tot = mine_ref[...] + slot[b].astype(F32)
        out_ref[...] = tot
        wire_ref[...] = tot.astype(wire_ref.dtype)
        cp.wait_send()

    blk = (tr, C)
    out_spec = pl.BlockSpec(blk, lambda s, i, c: (s * nt + i, 0))
    grid_spec = pltpu.PrefetchScalarGridSpec(
        num_scalar_prefetch=1, grid=(k, nt),
        in_specs=[pl.BlockSpec(blk, lambda s, i, c: ((2 * s + c[0]) * nt + i, 0)),
                  pl.BlockSpec(blk, lambda s, i, c: ((2 * s + 1 - c[0]) * nt + i, 0))],
        out_specs=[out_spec, out_spec],
        scratch_shapes=[pltpu.VMEM((2, tr, C), CDT), pltpu.VMEM((2, tr, C), CDT), pltpu.SemaphoreType.DMA((2,)),
                        pltpu.SemaphoreType.DMA((2,))])
    g2 = g.reshape(k * R, C)
    out, wire = pl.pallas_call(body, name=name, grid_spec=grid_spec,
                               out_shape=[_sds((k * half, C), F32), _sds((k * half, C), CDT)],
                               compiler_params=_params(2, PAIR_REDUCE_ID))(core, g2, g2)
    return out, wire.reshape(k, half, C)


def _all_reduce_small(v):
    r, cdim = v.shape

    def body(v_ref, o_ref, buf, send, recv):
        x, y, c = _place()
        me = 4 * x + 2 * y + c
        buf[me] = v_ref[...]
        sends = []
        for m in range(1, N_DEV):
            px = 1 - x if m & 4 else x
            py = 1 - y if m & 2 else y
            pc = 1 - c if m & 1 else c
            cp = pltpu.make_async_remote_copy(src_ref=v_ref, dst_ref=buf.at[me], send_sem=send.at[m - 1],
                                              recv_sem=recv.at[m - 1], device_id=(px, py, pc), device_id_type=MESH)
            cp.start()
            sends.append((cp, 4 * px + 2 * py + pc))
        for m, (cp, pidx) in enumerate(sends):
            pltpu.make_async_remote_copy(src_ref=v_ref, dst_ref=buf.at[pidx], send_sem=send.at[m], recv_sem=recv.at[m],
                                         device_id=(x, y, c), device_id_type=MESH).wait_recv()
        for cp, _ in sends:
            cp.wait_send()
        tot = buf[0]
        for k in range(1, N_DEV):
            tot = tot + buf[k]
        o_ref[...] = tot

    return pl.pallas_call(
        body, name="all_reduce_small", in_specs=[VMEM_SPEC], out_specs=VMEM_SPEC,
        out_shape=_sds((r, cdim), F32),
        scratch_shapes=[pltpu.VMEM((N_DEV, r, cdim), F32), pltpu.SemaphoreType.DMA((N_DEV - 1,)),
                        pltpu.SemaphoreType.DMA((N_DEV - 1,))],
    )(v)


def _row_tile(rows, cols, budget_elems=1 << 18, mult=8):
    if rows % mult:
        return rows
    t = max(mult, (budget_elems // cols) // mult * mult)
    while rows % t:
        t -= mult
    return t


def _adamw_update(w, g, m, v):
    nm = ADAM_B1 * m + (1.0 - ADAM_B1) * g
    nv = ADAM_B2 * v + (1.0 - ADAM_B2) * (g * g)
    m_hat = nm / (1.0 - ADAM_B1 ** ADAM_STEP)
    v_hat = nv / (1.0 - ADAM_B2 ** ADAM_STEP)
    return -ADAM_LR * (m_hat / (jnp.sqrt(v_hat) + ADAM_EPS) + ADAM_WD * w), nm, nv


def _adamw(name, w, g, m, v):
    R, C = w.shape
    tr = _row_tile(R, C, 1 << 18)

    def body(w_ref, g_ref, m_ref, v_ref, d_ref, nm_ref, nv_ref):
        d_ref[...], nm_ref[...], nv_ref[...] = _adamw_update(w_ref[...], g_ref[...], m_ref[...], v_ref[...])

    spec = pl.BlockSpec((tr, C), lambda i: (i, 0))
    return pl.pallas_call(body, name=name, grid=(R // tr,), in_specs=[spec] * 4, out_specs=[spec] * 3,
                          out_shape=[_sds((R, C), F32)] * 3, compiler_params=_params(1))(w, g, m, v)


def _sum_share(name, own, by_chip, chip, others, core):
    k, half, C = by_chip.shape
    tr = _row_tile(half, C, mult=16)
    nt = half // tr

    def body(chip_ref, oth_ref, c_ref, own_ref, a_ref, b_ref, cc_ref, g_out, mine, slot, send, recv):
        p = pl.program_id(1)
        _sibling_barrier((pl.program_id(0) == 0) & (p == 0))
        b = pl.program_id(0) % 2
        x, y, c = _place()
        cp = pltpu.make_async_remote_copy(src_ref=mine.at[b], dst_ref=slot.at[b], send_sem=send.at[b],
                                          recv_sem=recv.at[b], device_id=(x, y, 1 - c), device_id_type=MESH)

        @pl.when(p == 0)
        def _():
            tot = ((own_ref[...] + a_ref[...].astype(F32)) + b_ref[...].astype(F32)) + cc_ref[...].astype(F32)
            mine[b] = tot
            cp.start()
            g_out[...] = tot

        @pl.when(p == 1)
        def _():
            cp.wait_recv()
            g_out[...] = slot[b]
            cp.wait_send()

    def piece(j):
        return pl.BlockSpec((tr, C), lambda i, p, chip, oth, c: (oth[j] * nt + i, 0))

    grid_spec = pltpu.PrefetchScalarGridSpec(
        num_scalar_prefetch=3, grid=(nt, 2),
        in_specs=[pl.BlockSpec((tr, C), lambda i, p, chip, oth, c: (chip[0] * nt + i, 0)),
                  piece(0), piece(1), piece(2)],
        out_specs=pl.BlockSpec((tr, C), lambda i, p, chip, oth, c: (
            jnp.where(p == 0, c[0], 1 - c[0]) * nt + i, 0)),
        scratch_shapes=[pltpu.VMEM((2, tr, C), F32), pltpu.VMEM((2, tr, C), F32), pltpu.SemaphoreType.DMA((2,)),
                        pltpu.SemaphoreType.DMA((2,))])
    by2 = by_chip.reshape(k * half, C)
    return pl.pallas_call(body, name=name, grid_spec=grid_spec, out_shape=_sds((2 * half, C), F32),
                          compiler_params=_params(2, SUM_SHARE_ID))(chip, others, core, own, by2, by2, by2)


BIG = ("w_in", "w_proj_a", "w_proj_b", "w_out", "w_up", "w_down")
COL_SHARDED = ("w_in", "w_proj_a", "w_up")
SMALL = ("norm1_g", "q_norm_g", "k_norm_g", "ret_gn_g", "ret_gn_b", "norm2_g")
ALL_W = ("norm1_g", "w_in", "q_norm_g", "k_norm_g", "ret_gn_g", "ret_gn_b", "w_proj_a", "w_proj_b", "w_out",
         "norm2_g", "w_up", "w_down")
LANES = 128


def _to_full(name, gathered):
    k, r, c = gathered.shape
    if name in COL_SHARDED:
        return gathered.transpose(1, 0, 2).reshape(r, k * c)
    return gathered.reshape(k * r, c)


def _to_shard_major(name, full):
    if name in COL_SHARDED:
        r, c4 = full.shape
        return full.reshape(r, N_CHIPS, c4 // N_CHIPS).transpose(1, 0, 2)
    r4, c = full.shape
    return full.reshape(N_CHIPS, r4 // N_CHIPS, c)


def kernel(x, norm1_g, w_in, q_norm_g, k_norm_g, ret_gn_g, ret_gn_b, w_proj_a, w_proj_b, w_out, norm2_g, w_up, w_down, loss_target, m_norm1_g, m_w_in, m_q_norm_g, m_k_norm_g, m_ret_gn_g, m_ret_gn_b, m_w_proj_a, m_w_proj_b, m_w_out, m_norm2_g, m_w_up, m_w_down, v_norm1_g, v_w_in, v_q_norm_g, v_k_norm_g, v_ret_gn_g, v_ret_gn_b, v_w_proj_a, v_w_proj_b, v_w_out, v_norm2_g, v_w_up, v_w_down):
    weights = dict(norm1_g=norm1_g, w_in=w_in, q_norm_g=q_norm_g, k_norm_g=k_norm_g, ret_gn_g=ret_gn_g,
                   ret_gn_b=ret_gn_b, w_proj_a=w_proj_a, w_proj_b=w_proj_b, w_out=w_out, norm2_g=norm2_g,
                   w_up=w_up, w_down=w_down)
    moments_m = dict(norm1_g=m_norm1_g, w_in=m_w_in, q_norm_g=m_q_norm_g, k_norm_g=m_k_norm_g, ret_gn_g=m_ret_gn_g,
                     ret_gn_b=m_ret_gn_b, w_proj_a=m_w_proj_a, w_proj_b=m_w_proj_b, w_out=m_w_out,
                     norm2_g=m_norm2_g, w_up=m_w_up, w_down=m_w_down)
    moments_v = dict(norm1_g=v_norm1_g, w_in=v_w_in, q_norm_g=v_q_norm_g, k_norm_g=v_k_norm_g, ret_gn_g=v_ret_gn_g,
                     ret_gn_b=v_ret_gn_b, w_proj_a=v_w_proj_a, w_proj_b=v_w_proj_b, w_out=v_w_out,
                     norm2_g=v_norm2_g, w_up=v_w_up, w_down=v_w_down)

    mx, my = lax.axis_index("x"), lax.axis_index("y")
    core = lax.axis_index("c").astype(jnp.int32).reshape(1)
    chip = (2 * mx + my).astype(jnp.int32).reshape(1)
    others = jnp.stack([2 * (1 - mx) + my, 2 * mx + 1 - my, 2 * (1 - mx) + 1 - my]).astype(jnp.int32)
    shards = {n: weights[n][0].astype(CDT) for n in BIG}
    def start_gather(name, names):
        return _exchange_start(name, "gather", [shards[n] for n in names],
                               [((N_CHIPS,) + shards[n].shape, CDT) for n in names])

    w_in_shape = [((N_CHIPS,) + shards["w_in"].shape, CDT)]
    n_send, n_recv, n_srcs, n_lands, n_token = _exchange_start(
        "gather_w_in_near_start", "gather", [shards["w_in"]], w_in_shape, which=(0, 1))
    late = [n for n in BIG if n != "w_in"]
    flight = {}

    def near_w_in(after):
        srcs, lands = _exchange_wait("gather_w_in_near_wait", "gather", n_send, n_recv, n_srcs, n_lands, after,
                                     which=(0, 1))
        d_send, d_recv, d_srcs, d_lands, _ = _exchange_start(
            "gather_w_in_diag_start", "gather", srcs, w_in_shape, which=(2,), lands=lands)
        flight["late"] = start_gather("gather_late_start", late)
        w_near = _pair_fill("pair_fill_w_in_near", d_lands[0], d_srcs[0], core, others[:2], chip)
        flight["diag"] = (d_send, d_recv, d_srcs, [w_near])
        return w_near, flight["late"][-1]

    def far_w_in(after):
        d_send, d_recv, d_srcs, d_lands = flight["diag"]
        srcs, lands = _exchange_wait("gather_w_in_diag_wait", "gather", d_send, d_recv, d_srcs, d_lands, after,
                                     which=(2,))
        return _pair_fill("pair_fill_w_in_diag", lands[0], srcs[0], core, others[2:], chip, write_mine=False), None

    def late_weights(after):
        l_send, l_recv, l_srcs, l_lands, _ = flight["late"]
        srcs, lands = _exchange_wait("gather_late_wait", "gather", l_send, l_recv, l_srcs, l_lands, after)
        out = {}
        for n, mine, land in zip(late, srcs, lands):
            out[n] = _to_full(n, _pair_fill("pair_fill_%s" % n, land, mine, core, others, chip))
        return out

    pending = []

    def on_grads(group):
        names = list(group)
        red = [_pair_reduce("pair_reduce_%s" % n, g if g.ndim == 3 else _to_shard_major(n, g), core)
               for n, g in group.items()]
        wires = [wire for _, wire in red]
        send, recv, srcs, lands, token = _exchange_start(
            "scatter_start_%s" % names[0], "scatter", wires, [(wire.shape, wire.dtype) for wire in wires])
        pending.append((names, [own for own, _ in red], send, recv, srcs, lands))
        return token

    small = {n: weights[n].reshape(1, -1) for n in SMALL}

    loss, grad_x, small_g = _local_step(x[0], loss_target[0], n_srcs[0], chip, others, near_w_in, far_w_in, small,
                                        late_weights, on_grads, deps0=[n_token])
    loss = lax.psum(loss, ("x", "y", "c"))

    out_g, out_d, out_m, out_v = {}, {}, {}, {}
    for names, owns, send, recv, srcs, lands in pending:
        _, got = _exchange_wait("scatter_wait_%s" % names[0], "scatter", send, recv, srcs, lands, grad_x)
        for n, own, by_chip in zip(names, owns, got):
            shape = weights[n].shape
            g2 = _sum_share("sum_share_%s" % n, own, by_chip, chip, others, core)
            d, nm, nv = _adamw("adamw_%s" % n, weights[n][0], g2, moments_m[n][0], moments_v[n][0])
            out_g[n], out_d[n], out_m[n], out_v[n] = (t.reshape(shape) for t in (g2, d, nm, nv))

    packed = jnp.concatenate([small_g[n].reshape(1, -1) for n in SMALL], axis=1)
    red = _all_reduce_small(packed.reshape(-1, LANES)).reshape(1, -1)
    off = 0
    for n in SMALL:
        shape = weights[n].shape
        row = (1, weights[n].size)
        g2 = red[:, off:off + row[1]]
        off += row[1]
        d, nm, nv = _adamw("adamw_%s" % n, weights[n].reshape(row), g2, moments_m[n].reshape(row),
                           moments_v[n].reshape(row))
        out_g[n], out_d[n], out_m[n], out_v[n] = (t.reshape(shape) for t in (g2, d, nm, nv))

    return (loss, grad_x[None], *[out_g[n] for n in ALL_W], *[out_d[n] for n in ALL_W],
            *[out_m[n] for n in ALL_W], *[out_v[n] for n in ALL_W])
```

```python
import functools

import jax
import jax.numpy as jnp
from jax import lax
from jax.experimental import pallas as pl
from jax.experimental.pallas import tpu as pltpu

CDT = jnp.bfloat16
F32 = jnp.float32
EPS = 1e-6

ATT_GROUPS = ((128, 1), (512, 4), (2048, 16))
ATT_HPG = 4
ATT_HEADS = 12
HD = 128
BLK = 128
ATT_W = ATT_HEADS * HD
GW = ATT_HPG * HD
RET_HEADS = 4

ADAM_LR = 0.001
ADAM_B1 = 0.9
ADAM_B2 = 0.999
ADAM_EPS = 1e-08
ADAM_WD = 0.01
ADAM_STEP = 10

VMEM_LIMIT_BYTES = 56 * 1024 * 1024
MESH = pl.DeviceIdType.MESH
HBM_SPEC = pl.BlockSpec(memory_space=pltpu.HBM)
VMEM_SPEC = pl.BlockSpec(memory_space=pltpu.VMEM)


def _params(n_axes, collective_id=None):
    return pltpu.CompilerParams(dimension_semantics=("arbitrary",) * n_axes,
                                vmem_limit_bytes=VMEM_LIMIT_BYTES, collective_id=collective_id)


PAIR_FILL_ID, PAIR_REDUCE_ID, SUM_SHARE_ID = 1, 2, 3


def _sibling_barrier(first_step):
    @pl.when(first_step)
    def _():
        sem = pltpu.get_barrier_semaphore()
        x, y, c = lax.axis_index("x"), lax.axis_index("y"), lax.axis_index("c")
        pl.semaphore_signal(sem, inc=1, device_id=(x, y, 1 - c), device_id_type=pl.DeviceIdType.MESH)
        pl.semaphore_wait(sem, 1)


def _dot_nn(a, b):
    return jnp.dot(a, b, preferred_element_type=F32)


def _dot_nt(a, b):
    return lax.dot_general(a, b, (((1,), (1,)), ((), ())), preferred_element_type=F32)


def _dot_tn(a, b):
    return lax.dot_general(a, b, (((0,), (0,)), ((), ())), preferred_element_type=F32)


def _sigmoid(v):
    return 1.0 / (1.0 + jnp.exp(-v))


def _matmul(name, mode, a, b, *, tm, tn, tk, extras=(), outs, epilogue, deps=(), b_spec=None, n_cols=None,
            prefetch=(), alias_dep_to_out=None, j_outer=False):
    deps = [d for d in deps if d is not None]
    if mode == "tn":
        K, M = a.shape
    else:
        M, K = a.shape
    if b_spec is None:
        (N, K2) = b.shape if mode == "nt" else b.shape[::-1]
        assert K == K2, (name, a.shape, b.shape)
        if mode == "nt":
            b_spec = pl.BlockSpec((tn, tk), lambda i, j, k, *p: (j, k))
        else:
            b_spec = pl.BlockSpec((tk, tn), lambda i, j, k, *p: (k, j))
    else:
        N = n_cols
    assert M % tm == 0 and N % tn == 0 and K % tk == 0, (name, a.shape, b.shape)
    ni, nj, nk = M // tm, N // tn, K // tk
    if mode == "tn":
        a_spec = pl.BlockSpec((tk, tm), lambda i, j, k, *p: (k, i))
    else:
        a_spec = pl.BlockSpec((tm, tk), lambda i, j, k, *p: (i, k))
    dot = {"nn": _dot_nn, "nt": _dot_nt, "tn": _dot_tn}[mode]
    n_ex, n_out, n_dep, n_pre = len(extras), len(outs), len(deps), len(prefetch)
    grid = (ni, nj, nk)
    if j_outer:
        grid = (nj, ni, nk)

        def swapped(spec):
            return pl.BlockSpec(spec.block_shape, lambda j, i, k, *p: spec.index_map(i, j, k, *p))

        a_spec, b_spec = swapped(a_spec), swapped(b_spec)
        extras = [(e, swapped(s)) for e, s in extras]
        outs = [(o, swapped(s)) for o, s in outs]

    def body(*refs):
        refs = refs[n_pre:]
        a_ref, b_ref = refs[0], refs[1]
        ex = refs[2:2 + n_ex]
        out = refs[2 + n_ex + n_dep:2 + n_ex + n_dep + n_out]
        acc = refs[-1] if nk > 1 else None
        i = pl.program_id(1 if j_outer else 0)
        k = pl.program_id(2)
        if nk == 1:
            epilogue(dot(a_ref[...].astype(CDT), b_ref[...].astype(CDT)), ex, out, i)
            return

        @pl.when(k == 0)
        def _():
            acc[...] = jnp.zeros_like(acc)

        acc[...] += dot(a_ref[...].astype(CDT), b_ref[...].astype(CDT))

        @pl.when(k == nk - 1)
        def _():
            epilogue(acc[...], ex, out, i)

    grid_spec = pltpu.PrefetchScalarGridSpec(
        num_scalar_prefetch=n_pre, grid=grid,
        in_specs=[a_spec, b_spec] + [s for _, s in extras] + [pl.BlockSpec(memory_space=pl.ANY)] * n_dep,
        out_specs=[s for _, s in outs],
        scratch_shapes=[pltpu.VMEM((tm, tn), F32)] if nk > 1 else [])
    aliases = {}
    if alias_dep_to_out is not None:
        aliases = {n_pre + 2 + n_ex + alias_dep_to_out[0]: alias_dep_to_out[1]}
    res = pl.pallas_call(
        body, name=name, grid_spec=grid_spec, out_shape=[o for o, _ in outs], input_output_aliases=aliases,
        compiler_params=_params(3),
    )(*prefetch, a, b, *[e for e, _ in extras], *deps)
    return res


def _mn(tm, tn, col_off=0):
    return pl.BlockSpec((tm, tn), lambda i, j, k, *p: (i, j + col_off))


def _row(tn):
    return pl.BlockSpec((1, tn), lambda i, j, k, *p: (0, j))


def _ep_store(acc, ex, out, i):
    out[0][...] = acc.astype(out[0].dtype)


def _ep_resid_norm(acc, ex, out, i):
    x1 = ex[0][...] + acc
    out[0][...] = x1
    rstd = lax.rsqrt(jnp.mean(x1 * x1, axis=-1, keepdims=True) + EPS)
    out[1][...] = (x1 * rstd * ex[1][...]).astype(out[1].dtype)


def _ep_up(acc, ex, out, i):
    out[0][...] = acc.astype(out[0].dtype)
    r = jnp.maximum(acc, 0.0)
    out[1][...] = (r * r).astype(out[1].dtype)


def _ep_down_loss(acc, ex, out, i, inv_d):
    diff = (ex[0][...] + acc) - ex[1][...]
    dx2 = diff * inv_d
    out[0][...] = dx2
    out[1][...] = dx2.astype(out[1].dtype)

    @pl.when(i == 0)
    def _():
        out[2][...] = jnp.zeros_like(out[2])

    out[2][...] += jnp.sum(diff * diff, axis=0, keepdims=True)


def _ep_dh(acc, ex, out, i):
    h = ex[0][...].astype(F32)
    out[0][...] = (acc * (2.0 * jnp.maximum(h, 0.0))).astype(out[0].dtype)


def _ep_rms_bwd(acc, ex, out, i):
    x = ex[0][...]
    g = ex[1][...]
    rstd = lax.rsqrt(jnp.mean(x * x, axis=-1, keepdims=True) + EPS)
    xh = x * rstd
    dxh = acc * g
    dx = ex[2][...] + rstd * (dxh - xh * jnp.mean(dxh * xh, axis=-1, keepdims=True))
    out[0][...] = dx
    for copy in out[1:-1]:
        copy[...] = dx.astype(copy.dtype)
    dg = out[-1]

    @pl.when(i == 0)
    def _():
        dg[...] = jnp.zeros_like(dg)

    dg[...] += jnp.sum(acc * xh, axis=0, keepdims=True)


def _ep_gates(acc, ex, out, i):
    sa = _sigmoid(ex[0][...].astype(F32))
    sb = _sigmoid(ex[1][...].astype(F32))
    dpa = acc * sa
    dpb = acc * sb
    out[0][...] = dpa.astype(out[0].dtype)
    out[1][...] = dpb.astype(out[1].dtype)
    out[2][...] = (dpa * ex[2][...].astype(F32) * (1.0 - sa)).astype(out[2].dtype)
    out[3][...] = (dpb * ex[3][...].astype(F32) * (1.0 - sb)).astype(out[3].dtype)


def _sds(shape, dtype):
    return jax.ShapeDtypeStruct(shape, dtype)


def _in_proj_mine(x, g, w_mine, chip, proj_sds, deps, tm=512):
    S, D = x.shape
    ns = w_mine.shape[1]
    deps = [d for d in deps if d is not None]

    def body(c_ref, x_ref, g_ref, w_ref, *rest):
        xn_ref, proj_ref = rest[len(deps)], rest[len(deps) + 1]
        xv = x_ref[...]
        rstd = lax.rsqrt(jnp.mean(xv * xv, axis=-1, keepdims=True) + EPS)
        xn = (xv * rstd * g_ref[...]).astype(xn_ref.dtype)
        xn_ref[...] = xn
        proj_ref[...] = _dot_nn(xn, w_ref[...]).astype(proj_ref.dtype)

    grid_spec = pltpu.PrefetchScalarGridSpec(
        num_scalar_prefetch=1, grid=(S // tm,),
        in_specs=[pl.BlockSpec((tm, D), lambda i, c: (i, 0)), pl.BlockSpec((1, D), lambda i, c: (0, 0)),
                  pl.BlockSpec((D, ns), lambda i, c: (0, 0))] + [pl.BlockSpec(memory_space=pl.ANY)] * len(deps),
        out_specs=[pl.BlockSpec((tm, D), lambda i, c: (i, 0)), pl.BlockSpec((tm, ns), lambda i, c: (i, c[0]))])
    return pl.pallas_call(body, name="in_proj_mine", grid_spec=grid_spec, out_shape=[_sds((S, D), CDT), proj_sds],
                          compiler_params=_params(1))(chip, x, g, w_mine, *deps)


def _rm_shape(S, d, width):
    return (S, width) if d == 1 else (d, S // d, width)


def _rm_spec(tm, d, width):
    if d == 1:
        return pl.BlockSpec((tm, width), lambda i: (i, 0))
    return pl.BlockSpec((d, tm // d, width), lambda i: (0, i, 0))


def _rm_put(dst_ref, cols, buf_ref, d):
    if d == 1:
        dst_ref[:, cols] = buf_ref[...].astype(dst_ref.dtype)
        return
    m = buf_ref.shape[0] // d
    for r in range(d):
        dst_ref[r, :, cols] = buf_ref[pl.ds(r, m, stride=d), :].astype(dst_ref.dtype)


def _rm_reader(buf_ref, src_ref, d):
    if d == 1:
        return lambda s, rows: src_ref[rows, s * HD:(s + 1) * HD].astype(F32)
    m = buf_ref.shape[1] // d
    for s in range(buf_ref.shape[0]):
        for r in range(d):
            buf_ref.at[s][pl.ds(r, m, stride=d), :] = src_ref[r, :, s * HD:(s + 1) * HD].astype(F32)
    return lambda s, rows: buf_ref.at[s][rows, :]


def _qknorm_fwd(proj, gqk, tm=512):
    S = proj.shape[0]
    W = 2 * ATT_W
    dil = [d for _, d in ATT_GROUPS]

    def body(p_ref, g_ref, o0, o1, o2, buf):
        outs = (o0, o1, o2)
        for hd in range(3 * ATT_HEADS):
            which, head = hd // ATT_HEADS, hd % ATT_HEADS
            grp, slot = head // ATT_HPG, head % ATT_HPG
            cols = slice(hd * HD, (hd + 1) * HD)

            def chunk(rows, which=which, cols=cols):
                v = p_ref[rows, cols].astype(F32)
                if which < 2:
                    rstd = lax.rsqrt(jnp.mean(v * v, axis=-1, keepdims=True) + EPS)
                    v = v * rstd * g_ref[:, cols]
                buf[rows, :] = v

            chunk(slice(None))
            _rm_put(outs[grp], slice(which * GW + slot * HD, which * GW + (slot + 1) * HD), buf, dil[grp])

    return pl.pallas_call(
        body, name="qknorm_fwd", grid=(S // tm,),
        in_specs=[pl.BlockSpec((tm, 3 * ATT_W), lambda i: (i, 0)), pl.BlockSpec((1, W), lambda i: (0, 0))],
        out_specs=[_rm_spec(tm, d, 3 * GW) for d in dil],
        out_shape=[_sds(_rm_shape(S, d, 3 * GW), CDT) for d in dil],
        scratch_shapes=[pltpu.VMEM((tm, HD), F32)],
        compiler_params=_params(1))(proj, gqk)


def _qknorm_bwd(proj, gqk, dqs, dks, dvs, dproj, tm=256):
    S = proj.shape[0]
    W = 2 * ATT_W
    dil = [d for _, d in ATT_GROUPS]

    def body(p_ref, g_ref, *refs):
        ins = refs[0:9]
        o_ref, dg_ref = refs[10], refs[11]
        bufs = refs[12:21]
        i = pl.program_id(0)

        @pl.when(i == 0)
        def _():
            dg_ref[...] = jnp.zeros_like(dg_ref)

        nat = [_rm_reader(bufs[j], ins[j], dil[j % 3]) for j in range(9)]
        dq_get, dk_get, dv_get = nat[0:3], nat[3:6], nat[6:9]
        for hd in range(2 * ATT_HEADS):
            sl = slice(hd * HD, (hd + 1) * HD)
            head = hd % ATT_HEADS
            grp, slot = head // ATT_HPG, head % ATT_HPG
            get = (dq_get if hd < ATT_HEADS else dk_get)[grp]

            def chunk(rows, sl=sl, slot=slot, get=get):
                dn = get(slot, rows)
                v = p_ref[rows, sl].astype(F32)
                rstd = lax.rsqrt(jnp.mean(v * v, axis=-1, keepdims=True) + EPS)
                vh = v * rstd
                dg_ref[:, sl] += jnp.sum(dn * vh, axis=0, keepdims=True)
                dvh = dn * g_ref[:, sl]
                o_ref[rows, sl] = (rstd * (dvh - vh * jnp.mean(dvh * vh, axis=-1, keepdims=True))).astype(o_ref.dtype)

            chunk(slice(None))
        for head in range(ATT_HEADS):
            grp, slot = head // ATT_HPG, head % ATT_HPG
            o_ref[:, W + head * HD:W + (head + 1) * HD] = dv_get[grp](slot, slice(None)).astype(o_ref.dtype)

    return pl.pallas_call(
        body, name="qknorm_bwd", grid=(S // tm,),
        in_specs=[pl.BlockSpec((tm, W), lambda i: (i, 0)), pl.BlockSpec((1, W), lambda i: (0, 0))]
        + [_rm_spec(tm, d, GW) for d in dil] * 3 + [pl.BlockSpec(memory_space=pl.ANY)],
        out_specs=[pl.BlockSpec((tm, 3 * ATT_W), lambda i: (i, 0)), pl.BlockSpec((1, W), lambda i: (0, 0))],
        out_shape=[_sds(dproj.shape, dproj.dtype), _sds((1, W), F32)],
        scratch_shapes=[pltpu.VMEM((ATT_HPG, tm, HD), F32)] * 9,
        input_output_aliases={11: 0},
        compiler_params=_params(1))(proj, gqk, *dqs, *dks, *dvs, dproj)


def _att_mask(n):
    qi = lax.broadcasted_iota(jnp.int32, (BLK, 2 * BLK), 0)
    kj = lax.broadcasted_iota(jnp.int32, (BLK, 2 * BLK), 1)
    dist = BLK + qi - kj
    valid_all = (dist >= 0) & (dist <= BLK)
    return valid_all & ((kj >= BLK) | (n > 0)), valid_all, dist.astype(F32)


def _att_slopes(grp):
    return [2.0 ** (-8.0 * (grp * ATT_HPG + hh + 1) / ATT_HEADS) for hh in range(ATT_HPG)]


ATT_PLANES_PER_STEP = 4


def _att_planes(d):
    return d if d > 1 else ATT_PLANES_PER_STEP


def _att_3d(a, d):
    return a.reshape(ATT_PLANES_PER_STEP, a.shape[0] // ATT_PLANES_PER_STEP, a.shape[1]) if d == 1 else a


def _att_spec(row_fn, col=0):
    return pl.BlockSpec((ATT_PLANES_PER_STEP, BLK, GW), lambda r, n: (r, row_fn(n), col))


def _att_chains():
    return [(rr * ATT_HPG + hh, rr, slice(hh * HD, (hh + 1) * HD), hh)
            for rr in range(ATT_PLANES_PER_STEP) for hh in range(ATT_HPG)]


def _att_qkv_specs(nb):
    last = nb - 1

    def cur(n):
        return jnp.minimum(n, last)

    def prev(n):
        return jnp.maximum(jnp.minimum(n, last) - 1, 0)

    return [_att_spec(cur, 0), _att_spec(prev, 1), _att_spec(cur, 1), _att_spec(prev, 2), _att_spec(cur, 2),
            _att_spec(lambda n: last, 1), _att_spec(lambda n: last, 2)]


def _att_prev(seg, n, prev_ref, last_ref, rr, sl):
    t = prev_ref[rr, :, sl]
    if seg and rr > 0:
        t = jnp.where(n == 0, last_ref[rr - 1, :, sl], t)
    return t


def _att_fwd(grp, S, qkv):
    _, d = ATT_GROUPS[grp]
    seg = d == 1
    P = _att_planes(d)
    L = S // P
    nb = L // BLK
    assert P % ATT_PLANES_PER_STEP == 0 and (not seg or P == ATT_PLANES_PER_STEP)
    slopes = _att_slopes(grp)
    scale = HD ** -0.5
    chains = _att_chains()

    def body(q_ref, kp_ref, kc_ref, vp_ref, vc_ref, kl_ref, vl_ref, o_ref, l_ref, s_buf, p_buf, den_buf):
        n = pl.program_id(1)
        valid, valid_all, distf = _att_mask(n)
        for c, rr, sl, hh in chains:
            k = jnp.concatenate([_att_prev(seg, n, kp_ref, kl_ref, rr, sl), kc_ref[rr, :, sl]], axis=0)
            s_buf[c] = _dot_nt(q_ref[rr, :, sl], k)
        for c, rr, sl, hh in chains:
            s = s_buf[c] * scale + (-slopes[hh] * d) * distf
            s = jnp.where(valid_all if seg and rr > 0 else valid, s, -1e30)
            m = jnp.max(s, axis=-1, keepdims=True)
            p = jnp.exp(s - m)
            den = jnp.sum(p, axis=-1, keepdims=True)
            p_buf[c] = p.astype(CDT)
            den_buf[c] = jnp.broadcast_to(den, (BLK, HD))
            l_ref[rr, :, sl] = jnp.broadcast_to(m + jnp.log(den), (BLK, HD))
        for c, rr, sl, hh in chains:
            v = jnp.concatenate([_att_prev(seg, n, vp_ref, vl_ref, rr, sl), vc_ref[rr, :, sl]], axis=0)
            o_ref[rr, :, sl] = _dot_nn(p_buf[c], v) / den_buf[c]

    out_spec = _att_spec(lambda n: n)
    n_ch = len(chains)
    q3 = _att_3d(qkv, d)
    o, l = pl.pallas_call(
        body, name="att_fwd_g%d" % grp, grid=(P // ATT_PLANES_PER_STEP, nb),
        in_specs=_att_qkv_specs(nb),
        out_specs=[out_spec, out_spec],
        out_shape=[_sds((P, L, GW), F32)] * 2,
        scratch_shapes=[pltpu.VMEM((n_ch, BLK, 2 * BLK), F32), pltpu.VMEM((n_ch, BLK, 2 * BLK), CDT),
                        pltpu.VMEM((n_ch, BLK, HD), F32)],
        compiler_params=_params(2),
    )(*[q3] * 7)
    return o.reshape(_rm_shape(S, d, GW)), l.reshape(_rm_shape(S, d, GW))


def _att_bwd(grp, S, qkv, lse, do_g, c_g):
    _, d = ATT_GROUPS[grp]
    seg = d == 1
    P = _att_planes(d)
    L = S // P
    nb = L // BLK
    assert P % ATT_PLANES_PER_STEP == 0 and (not seg or P == ATT_PLANES_PER_STEP)
    slopes = _att_slopes(grp)
    scale = HD ** -0.5
    last = nb - 1
    chains = _att_chains()

    def body(q_ref, kp_ref, kc_ref, vp_ref, vc_ref, kl_ref, vl_ref, l_ref, do_ref, c_ref, dq_ref, dk_ref, dv_ref,
             ck, cv, fk, fv, s_buf, dp_buf, p_buf, ds_buf):
        n = pl.program_id(1)

        @pl.when(n == 0)
        def _():
            for buf in (ck, cv, fk, fv):
                buf[...] = jnp.zeros_like(buf)

        @pl.when(n < nb)
        def _():
            valid, valid_all, distf = _att_mask(n)
            for c, rr, sl, hh in chains:
                k = jnp.concatenate([_att_prev(seg, n, kp_ref, kl_ref, rr, sl), kc_ref[rr, :, sl]], axis=0)
                v = jnp.concatenate([_att_prev(seg, n, vp_ref, vl_ref, rr, sl), vc_ref[rr, :, sl]], axis=0)
                s_buf[c] = _dot_nt(q_ref[rr, :, sl], k)
                dp_buf[c] = _dot_nt(do_ref[rr, :, sl], v)
            for c, rr, sl, hh in chains:
                s = s_buf[c] * scale + (-slopes[hh] * d) * distf
                p = jnp.where(valid_all if seg and rr > 0 else valid, jnp.exp(s - l_ref[rr, :, sl][:, 0:1]), 0.0)
                p_buf[c] = p.astype(CDT)
                ds_buf[c] = (p * (dp_buf[c] + c_ref[rr, :, sl][:, 0:1]) * scale).astype(CDT)
            for c, rr, sl, hh in chains:
                k = jnp.concatenate([_att_prev(seg, n, kp_ref, kl_ref, rr, sl), kc_ref[rr, :, sl]], axis=0)
                ds = ds_buf[c]
                dq_ref[rr, :, sl] = _dot_nn(ds, k)
                dk = _dot_tn(ds, q_ref[rr, :, sl])
                dv = _dot_tn(p_buf[c], do_ref[rr, :, sl])
                dk_ref[rr, :, sl] = ck[rr, :, sl] + dk[0:BLK]
                dv_ref[rr, :, sl] = cv[rr, :, sl] + dv[0:BLK]
                ck[rr, :, sl] = dk[BLK:2 * BLK]
                cv[rr, :, sl] = dv[BLK:2 * BLK]
                if seg and rr > 0:
                    @pl.when(n == 0)
                    def _(rr=rr, sl=sl, dk=dk, dv=dv):
                        fk[rr - 1, :, sl] = dk[0:BLK]
                        fv[rr - 1, :, sl] = dv[0:BLK]

        @pl.when(n == nb)
        def _():
            dk_ref[...] = ck[...] + fk[...]
            dv_ref[...] = cv[...] + fv[...]

    blk = (ATT_PLANES_PER_STEP, BLK, GW)
    at_q = _att_spec(lambda n: jnp.minimum(n, last))
    behind = _att_spec(lambda n: jnp.maximum(n - 1, 0))
    n_ch = len(chains)
    q3 = _att_3d(qkv, d)
    res = pl.pallas_call(
        body, name="att_bwd_g%d" % grp, grid=(P // ATT_PLANES_PER_STEP, nb + 1),
        in_specs=_att_qkv_specs(nb) + [at_q, at_q, at_q],
        out_specs=[at_q, behind, behind],
        out_shape=[_sds((P, L, GW), F32)] * 3,
        scratch_shapes=[pltpu.VMEM(blk, F32)] * 4
        + [pltpu.VMEM((n_ch, BLK, 2 * BLK), F32), pltpu.VMEM((n_ch, BLK, 2 * BLK), F32),
           pltpu.VMEM((n_ch, BLK, 2 * BLK), CDT), pltpu.VMEM((n_ch, BLK, 2 * BLK), CDT)],
        compiler_params=_params(2),
    )(*[q3] * 7, _att_3d(lse, d), _att_3d(do_g, d), _att_3d(c_g, d))
    return [t.reshape(_rm_shape(S, d, GW)) for t in res]


def _mix_alpha(l0, l1, l2):
    mx = jnp.maximum(jnp.maximum(l0, l1), l2)
    e = [jnp.exp(l0 - mx), jnp.exp(l1 - mx), jnp.exp(l2 - mx)]
    tot = e[0] + e[1] + e[2]
    return [ei / tot for ei in e]


def _mix_fwd(S, os_, ls_, tm=512):
    dil = [d for _, d in ATT_GROUPS]

    def body(*refs):
        out, bufs = refs[6], refs[7:13]
        get = [_rm_reader(bufs[j], refs[j], dil[j % 3]) for j in range(6)]
        rows = slice(None)
        for s in range(ATT_HPG):
            al = _mix_alpha(*[get[3 + g](s, rows) for g in range(3)])
            mixed = al[0] * get[0](s, rows) + al[1] * get[1](s, rows) + al[2] * get[2](s, rows)
            out[:, s * HD:(s + 1) * HD] = mixed.astype(out.dtype)

    specs = [_rm_spec(tm, d, GW) for d in dil]
    return pl.pallas_call(
        body, name="mix_fwd", grid=(S // tm,), in_specs=specs * 2, out_specs=pl.BlockSpec((tm, GW), lambda i: (i, 0)),
        out_shape=_sds((S, GW), CDT), scratch_shapes=[pltpu.VMEM((ATT_HPG, tm, HD), F32)] * 6,
        compiler_params=_params(1))(*os_, *ls_)


def _mix_bwd(S, os_, ls_, do_a, tm=512):
    dil = [d for _, d in ATT_GROUPS]

    def body(*refs):
        d_ref, outs, bufs, tmps = refs[6], refs[7:13], refs[13:19], refs[19:25]
        get = [_rm_reader(bufs[j], refs[j], dil[j % 3]) for j in range(6)]
        for s in range(ATT_HPG):
            cols = slice(s * HD, (s + 1) * HD)

            def chunk(rows, s=s, cols=cols):
                al = _mix_alpha(*[get[3 + g](s, rows) for g in range(3)])
                dv = d_ref[rows, cols]
                o_a = al[0] * get[0](s, rows) + al[1] * get[1](s, rows) + al[2] * get[2](s, rows)
                dsum = jnp.sum(dv * o_a, axis=-1, keepdims=True)
                for g in range(3):
                    tmps[g][rows, :] = al[g] * dv
                    tmps[3 + g][rows, :] = -(al[g] * dsum)

            chunk(slice(None))
            for j in range(6):
                _rm_put(outs[j], cols, tmps[j], dil[j % 3])

    specs = [_rm_spec(tm, d, GW) for d in dil]
    res = pl.pallas_call(
        body, name="mix_bwd", grid=(S // tm,), in_specs=specs * 2 + [pl.BlockSpec((tm, GW), lambda i: (i, 0))],
        out_specs=specs * 2,
        out_shape=[_sds(_rm_shape(S, d, GW), CDT) for d in dil] + [_sds(_rm_shape(S, d, GW), F32) for d in dil],
        scratch_shapes=[pltpu.VMEM((ATT_HPG, tm, HD), F32)] * 6 + [pltpu.VMEM((tm, HD), F32)] * 6,
        compiler_params=_params(1))(*os_, *ls_, do_a)
    return res[:3], res[3:]


def _ret_tables(dk):
    H, C = RET_HEADS, BLK
    log_g = jnp.log(1.0 - 2.0 ** (-5.0 - jnp.arange(H, dtype=F32)))
    idx = jnp.arange(C, dtype=F32)
    diff = idx[:, None] - idx[None, :]
    decay = jnp.where(diff >= 0, jnp.exp(log_g[:, None, None] * jnp.maximum(diff, 0.0)), 0.0)
    xi = jnp.exp(log_g[:, None] * (idx[None, :] + 1.0))
    zeta = jnp.exp(log_g[:, None] * (C - 1.0 - idx[None, :])) * (dk ** -0.5)
    g_chunk = jnp.exp(log_g * C)
    bc = lambda t: jnp.broadcast_to(t[:, :, None], (H, C, C))
    return decay, bc(xi), bc(zeta), jnp.broadcast_to(g_chunk[:, None, None], (H, 8, C))


def _gn_fwd(o, g, b):
    mu = jnp.mean(o, axis=-1, keepdims=True)
    xc = o - mu
    rstd = lax.rsqrt(jnp.mean(xc * xc, axis=-1, keepdims=True) + EPS)
    yh = xc * rstd
    return yh, rstd, yh * g + b


def _ret_specs(dk, dv, order):
    H = RET_HEADS
    qk_w, v_w = H * dk, H * dv
    off_q = 3 * ATT_W
    off_k, off_v, off_g = off_q + qk_w, off_q + 2 * qk_w, off_q + 2 * qk_w + v_w
    assert 2 * dk == dv and all(off % dv == 0 for off in (off_q, off_k, off_v, off_g))

    def col(off, j):
        return pl.BlockSpec((BLK, dv), lambda i: (order(i), off // dv + j))

    tab = pl.BlockSpec((H, BLK, BLK), lambda i: (0, 0, 0))
    return ([col(off_q, j) for j in range(H // 2)] + [col(off_k, j) for j in range(H // 2)]
            + [col(off_v, j) for j in range(H)] + [col(off_g, j) for j in range(H)]
            + [tab, tab, tab, pl.BlockSpec((H, 8, BLK), lambda i: (0, 0, 0))])


def _ret_heads(refs, dk):
    H = RET_HEADS
    q_refs, k_refs = refs[0:H // 2], refs[H // 2:H]
    v_refs, gr_refs = refs[H:2 * H], refs[2 * H:3 * H]

    def head(h):
        cols = slice((h % 2) * dk, (h % 2 + 1) * dk)
        return q_refs[h // 2][:, cols], k_refs[h // 2][:, cols], v_refs[h][...], gr_refs[h][...]

    return head, refs[3 * H:3 * H + 4]


def _ret_fwd(proj, gn_g, gn_b, dk, dv):
    S = proj.shape[0]
    N = S // BLK
    H = RET_HEADS
    kscale = dk ** -0.5
    n_in = 3 * H + 4

    def body(*refs):
        head, (dec_ref, xi_ref, zeta_ref, gc_ref) = _ret_heads(refs, dk)
        g_ref, b_ref, opre_ref, or_ref, st_ref, state, s_buf, cross_buf = refs[n_in:n_in + 8]
        n = pl.program_id(0)

        @pl.when(n == 0)
        def _():
            state[...] = jnp.zeros_like(state)

        for h in range(H):
            q, k, v, _ = head(h)
            s_buf[h] = _dot_nt(q, k)
            st = state[h]
            st_c = st.astype(CDT)
            st_ref[h] = st_c
            cross_buf[h] = _dot_nn(q, st_c)
            kz = (k.astype(F32) * zeta_ref[h][:, 0:1]).astype(CDT)
            state[h] = st * gc_ref[h][0:1, 0:1] + _dot_tn(kz, v)
        for h in range(H):
            vs = slice(h * dv, (h + 1) * dv)
            _, _, v, gr = head(h)
            s = s_buf[h] * kscale * dec_ref[h]
            o = _dot_nn(s.astype(CDT), v) + cross_buf[h] * xi_ref[h][:, 0:1]
            opre_ref[:, vs] = o
            _, _, y = _gn_fwd(o, g_ref[:, vs], b_ref[:, vs])
            gr = gr.astype(F32)
            or_ref[:, vs] = (y * (gr * _sigmoid(gr))).astype(or_ref.dtype)

    v_w = H * dv
    row = pl.BlockSpec((1, v_w), lambda i: (0, 0))
    tile = pl.BlockSpec((BLK, v_w), lambda i: (i, 0))
    return pl.pallas_call(
        body, name="ret_fwd", grid=(N,),
        in_specs=_ret_specs(dk, dv, lambda i: i) + [row, row],
        out_specs=[tile, tile, pl.BlockSpec((None, H, dk, dv), lambda i: (i, 0, 0, 0))],
        out_shape=[_sds((S, v_w), F32), _sds((S, v_w), CDT), _sds((N, H, dk, dv), CDT)],
        scratch_shapes=[pltpu.VMEM((H, dk, dv), F32), pltpu.VMEM((H, BLK, BLK), F32), pltpu.VMEM((H, BLK, dv), F32)],
        compiler_params=_params(1),
    )(*[proj] * (3 * H), *_ret_tables(dk), gn_g, gn_b)


def _ret_bwd(proj, gn_g, gn_b, o_pre, states, d_or, dga, dgb, dk, dv):
    S, in_w = proj.shape
    N = S // BLK
    H = RET_HEADS
    qk_w, v_w = H * dk, H * dv
    kscale = dk ** -0.5
    n_in = 3 * H + 4
    out_w = 2 * qk_w + 2 * v_w
    gate_w = dga.shape[1]
    col0 = 3 * ATT_W
    assert col0 + out_w + 2 * gate_w == in_w
    rev = lambda i: N - 1 - i

    def body(*refs):
        head, (dec_ref, xi_ref, zeta_ref, gc_ref) = _ret_heads(refs, dk)
        (g_ref, b_ref, opre_ref, st_ref, dor_ref, dga_ref, dgb_ref, dproj_ref, dg_ref, db_ref, dstate, stage,
         sem, do_buf, dox_buf, a_buf, g_buf, dq_buf, dk_buf, dv_buf) = refs[n_in:n_in + 20]
        i = pl.program_id(0)
        slot = i % 2
        out_ref = stage.at[slot]

        def out_copy(s, step):
            rows = pl.ds(pl.multiple_of(rev(step) * BLK, BLK), BLK)
            return pltpu.make_async_copy(stage.at[s], dproj_ref.at[rows, pl.ds(col0, in_w - col0)], sem.at[s])

        @pl.when(i >= 2)
        def _():
            out_copy(slot, i - 2).wait()

        @pl.when(i == 0)
        def _():
            dstate[...] = jnp.zeros_like(dstate)
            dg_ref[...] = jnp.zeros_like(dg_ref)
            db_ref[...] = jnp.zeros_like(db_ref)

        out_ref[:, out_w:out_w + gate_w] = dga_ref[...]
        out_ref[:, out_w + gate_w:out_w + 2 * gate_w] = dgb_ref[...]
        for h in range(H):
            vs = slice(h * dv, (h + 1) * dv)
            _, _, _, gr = head(h)
            gr = gr.astype(F32)
            sg = _sigmoid(gr)
            gain = g_ref[:, vs]
            yh, rstd, y = _gn_fwd(opre_ref[:, vs], gain, b_ref[:, vs])
            d_or_v = dor_ref[:, vs]
            dy = d_or_v * (gr * sg)
            out_ref[:, 2 * qk_w + v_w + h * dv:2 * qk_w + v_w + (h + 1) * dv] = (
                d_or_v * y * (sg * (1.0 + gr * (1.0 - sg)))).astype(out_ref.dtype)
            dg_ref[:, vs] += jnp.sum(dy * yh, axis=0, keepdims=True)
            db_ref[:, vs] += jnp.sum(dy, axis=0, keepdims=True)
            dyh = dy * gain
            do = rstd * (dyh - jnp.mean(dyh, axis=-1, keepdims=True)
                         - yh * jnp.mean(dyh * yh, axis=-1, keepdims=True))
            do_buf[h] = do.astype(CDT)
            dox_buf[h] = (do * xi_ref[h][:, 0:1]).astype(CDT)
        for h in range(H):
            q, k, v, _ = head(h)
            dox = dox_buf[h]
            a_buf[h] = _dot_nt(q, k)
            g_buf[h] = _dot_nt(do_buf[h], v)
            dsn = dstate[h]
            dsn_c = dsn.astype(CDT)
            kz = (k.astype(F32) * zeta_ref[h][:, 0:1]).astype(CDT)
            dq_buf[h] = _dot_nt(dox, st_ref[h])
            dk_buf[h] = _dot_nt(v, dsn_c)
            dv_buf[h] = _dot_nn(kz, dsn_c)
            dstate[h] = dsn * gc_ref[h][0:1, 0:1] + _dot_tn(q, dox)
        for h in range(H):
            q, k, _, _ = head(h)
            decay = dec_ref[h]
            a_c = (a_buf[h] * kscale * decay).astype(CDT)
            g_c = (g_buf[h] * decay).astype(CDT)
            dq = _dot_nn(g_c, k) * kscale + dq_buf[h]
            dkk = _dot_tn(g_c, q) * kscale + dk_buf[h] * zeta_ref[h][:, 0:1]
            dvv = _dot_tn(a_c, do_buf[h]) + dv_buf[h]
            out_ref[:, h * dk:(h + 1) * dk] = dq.astype(out_ref.dtype)
            out_ref[:, qk_w + h * dk:qk_w + (h + 1) * dk] = dkk.astype(out_ref.dtype)
            out_ref[:, 2 * qk_w + h * dv:2 * qk_w + (h + 1) * dv] = dvv.astype(out_ref.dtype)

        cp = out_copy(slot, i)
        cp.start()

        @pl.when(i == N - 1)
        def _():
            cp.wait()
            if N >= 2:
                out_copy(1 - slot, i - 1).wait()

    row = pl.BlockSpec((1, v_w), lambda i: (0, 0))
    tile = pl.BlockSpec((BLK, v_w), lambda i: (rev(i), 0))
    gate = pl.BlockSpec((BLK, gate_w), lambda i: (rev(i), 0))
    return pl.pallas_call(
        body, name="ret_bwd", grid=(N,),
        in_specs=_ret_specs(dk, dv, rev) + [row, row, tile,
                 pl.BlockSpec((None, H, dk, dv), lambda i: (rev(i), 0, 0, 0)), tile, gate, gate],
        out_specs=[pl.BlockSpec(memory_space=pl.ANY), row, row],
        out_shape=[_sds((S, in_w), CDT), _sds((1, v_w), F32), _sds((1, v_w), F32)],
        scratch_shapes=[pltpu.VMEM((H, dk, dv), F32), pltpu.VMEM((2, BLK, in_w - col0), CDT),
                        pltpu.SemaphoreType.DMA((2,)),
                        pltpu.VMEM((H, BLK, dv), CDT), pltpu.VMEM((H, BLK, dv), CDT),
                        pltpu.VMEM((H, BLK, BLK), F32), pltpu.VMEM((H, BLK, BLK), F32),
                        pltpu.VMEM((H, BLK, dk), F32), pltpu.VMEM((H, BLK, dk), F32), pltpu.VMEM((H, BLK, dv), F32)],
        compiler_params=_params(1),
    )(*[proj] * (3 * H), *_ret_tables(dk), gn_g, gn_b, o_pre, states, d_or, dga, dgb)


def _merge_fwd(o_a, o_r, wa, wb, proj, d_model, tm=1024, tn=512):
    S, in_w = proj.shape
    off_a, off_b = in_w - 2 * d_model, in_w - d_model
    assert off_a % tn == 0 and off_b % tn == 0

    def body(oa_ref, or_ref, wa_ref, wb_ref, ga_ref, gb_ref, y_ref, pa_ref, pb_ref):
        pa = _dot_nn(oa_ref[...], wa_ref[...])
        pb = _dot_nn(or_ref[...], wb_ref[...])
        y = _sigmoid(ga_ref[...].astype(F32)) * pa + _sigmoid(gb_ref[...].astype(F32)) * pb
        y_ref[...] = y.astype(y_ref.dtype)
        pa_ref[...] = pa.astype(pa_ref.dtype)
        pb_ref[...] = pb.astype(pb_ref.dtype)

    ka, kb = o_a.shape[1], o_r.shape[1]
    out = pl.BlockSpec((tm, tn), lambda i, j: (i, j))
    return pl.pallas_call(
        body, name="merge_fwd", grid=(S // tm, d_model // tn),
        in_specs=[pl.BlockSpec((tm, ka), lambda i, j: (i, 0)), pl.BlockSpec((tm, kb), lambda i, j: (i, 0)),
                  pl.BlockSpec((ka, tn), lambda i, j: (0, j)), pl.BlockSpec((kb, tn), lambda i, j: (0, j)),
                  pl.BlockSpec((tm, tn), lambda i, j: (i, off_a // tn + j)),
                  pl.BlockSpec((tm, tn), lambda i, j: (i, off_b // tn + j))],
        out_specs=[out, out, out], out_shape=[_sds((S, d_model), CDT)] * 3,
        compiler_params=_params(2))(o_a, o_r, wa, wb, proj, proj)


def _local_step(x, target, w_in_mine, chip, others, near_w_in, far_w_in, small, late_weights, on_grads, deps0=()):
    S, D = x.shape
    ns_in = w_in_mine.shape[1]
    in_w = N_CHIPS * ns_in
    d_ff = 4 * D
    ret_v_w = 2 * D
    dv = ret_v_w // RET_HEADS
    dk = (in_w - 3 * ATT_W - 2 * ret_v_w - 2 * D) // (2 * RET_HEADS)
    gqk = jnp.concatenate([small["q_norm_g"].reshape(1, ATT_W), small["k_norm_g"].reshape(1, ATT_W)], axis=1)
    g1, g2 = small["norm1_g"], small["norm2_g"]
    gn_g, gn_b = small["ret_gn_g"], small["ret_gn_b"]

    proj_sds = _sds((S, in_w), CDT)
    xn, proj = _in_proj_mine(x, g1, w_in_mine, chip, proj_sds, deps0)
    for stage, (get_w_in, chips) in enumerate(((near_w_in, others[:2]), (far_w_in, others[2:]))):
        w_in, started = get_w_in(proj)
        (proj,) = _matmul(
            "in_proj_far%d" % stage, "nn", xn, w_in, tm=512, tn=ns_in, tk=D, prefetch=[chips],
            n_cols=chips.shape[0] * ns_in, b_spec=pl.BlockSpec((None, D, ns_in), lambda i, j, k, o: (o[j], 0, 0)),
            outs=[(proj_sds, pl.BlockSpec((512, ns_in), lambda i, j, k, o: (i, o[j])))], epilogue=_ep_store,
            deps=[proj, started], alias_dep_to_out=(0, 0), j_outer=True)
    qkv = _qknorm_fwd(proj, gqk)
    att = [_att_fwd(g, S, qkv[g]) for g in range(3)]
    os_, ls_ = [a[0] for a in att], [a[1] for a in att]
    o_a = _mix_fwd(S, os_, ls_)
    o_pre, o_r, states = _ret_fwd(proj, gn_g, gn_b, dk, dv)
    w = late_weights(o_r)
    y, pa, pb = _merge_fwd(o_a, o_r, w["w_proj_a"], w["w_proj_b"], proj, D)
    x1, xn2 = _matmul("out_proj", "nn", y, w["w_out"], tm=1024, tn=D, tk=D,
                      extras=[(x, _mn(1024, D)), (g2, _row(D))],
                      outs=[(_sds((S, D), F32), _mn(1024, D)), (_sds((S, D), CDT), _mn(1024, D))],
                      epilogue=_ep_resid_norm)
    hid, act = _matmul("mlp_up", "nn", xn2, w["w_up"], tm=512, tn=2048, tk=D, j_outer=True,
                       outs=[(_sds((S, d_ff), CDT), _mn(512, 2048))] * 2, epilogue=_ep_up)
    dx2, dx2c, loss_row = _matmul(
        "mlp_down_loss", "nn", act, w["w_down"], tm=512, tn=D, tk=d_ff,
        extras=[(x1, _mn(512, D)), (target, _mn(512, D))],
        outs=[(_sds((S, D), F32), _mn(512, D)), (_sds((S, D), CDT), _mn(512, D)), (_sds((1, D), F32), _row(D))],
        epilogue=functools.partial(_ep_down_loss, inv_d=1.0 / D))
    loss = 0.5 * jnp.sum(loss_row) / D

    (dh,) = _matmul("d_hidden", "nt", dx2c, w["w_down"], tm=512, tn=2048, tk=D, j_outer=True,
                    extras=[(hid, _mn(512, 2048))], outs=[(_sds((S, d_ff), CDT), _mn(512, 2048))], epilogue=_ep_dh)
    (gw_down,) = _matmul("dw_down", "tn", act, dx2c, tm=1024, tn=D, tk=1024,
                         outs=[(_sds((d_ff, D), F32), _mn(1024, D))], epilogue=_ep_store)
    (gw_up,) = _matmul("dw_up", "tn", xn2, dh, tm=D, tn=1024, tk=1024,
                       outs=[(_sds((D, d_ff), F32), _mn(D, 1024))], epilogue=_ep_store)
    tok = on_grads({"w_down": gw_down, "w_up": gw_up})
    dx1, dx1c, dg2 = _matmul(
        "d_x1", "nt", dh, w["w_up"], tm=512, tn=D, tk=d_ff,
        extras=[(x1, _mn(512, D)), (g2, _row(D)), (dx2, _mn(512, D))],
        outs=[(_sds((S, D), F32), _mn(512, D)), (_sds((S, D), CDT), _mn(512, D)), (_sds((1, D), F32), _row(D))],
        epilogue=_ep_rms_bwd, deps=[tok])

    gt = 512
    assert (in_w - 2 * D) % gt == 0
    off_a, off_b = (in_w - 2 * D) // gt, (in_w - D) // gt
    dpa, dpb, dga, dgb = _matmul(
        "d_gates", "nt", dx1c, w["w_out"], tm=1024, tn=gt, tk=D,
        extras=[(proj, _mn(1024, gt, off_a)), (proj, _mn(1024, gt, off_b)), (pa, _mn(1024, gt)),
                (pb, _mn(1024, gt))],
        outs=[(_sds((S, D), CDT), _mn(1024, gt))] * 4, epilogue=_ep_gates)
    (gw_out,) = _matmul("dw_out", "tn", y, dx1c, tm=D, tn=D, tk=1024,
                        outs=[(_sds((D, D), F32), _mn(D, D))], epilogue=_ep_store)
    (gw_pa,) = _matmul("dw_proj_a", "tn", o_a, dpa, tm=GW, tn=D, tk=1024,
                       outs=[(_sds((GW, D), F32), _mn(GW, D))], epilogue=_ep_store)
    (gw_pb,) = _matmul("dw_proj_b", "tn", o_r, dpb, tm=1024, tn=D, tk=1024,
                       outs=[(_sds((ret_v_w, D), F32), _mn(1024, D))], epilogue=_ep_store)
    (do_a,) = _matmul("d_o_a", "nt", dpa, w["w_proj_a"], tm=1024, tn=GW, tk=D,
                      outs=[(_sds((S, GW), F32), _mn(1024, GW))], epilogue=_ep_store)
    tok = on_grads({"w_out": gw_out, "w_proj_a": gw_pa, "w_proj_b": gw_pb})
    (d_or,) = _matmul("d_o_r", "nt", dpb, w["w_proj_b"], tm=512, tn=ret_v_w, tk=D,
                      outs=[(_sds((S, ret_v_w), F32), _mn(512, ret_v_w))], epilogue=_ep_store, deps=[tok])

    dproj, dgn_g, dgn_b = _ret_bwd(proj, gn_g, gn_b, o_pre, states, d_or, dga, dgb, dk, dv)
    do_gs, c_gs = _mix_bwd(S, os_, ls_, do_a)
    datt_parts = [_att_bwd(g, S, qkv[g], ls_[g], do_gs[g], c_gs[g]) for g in range(3)]
    dproj, dgqk = _qknorm_bwd(proj, gqk, [p[0] for p in datt_parts], [p[1] for p in datt_parts],
                              [p[2] for p in datt_parts], dproj)

    (gw_in,) = _matmul(
        "dw_in", "tn", xn, dproj, tm=512, tn=ns_in, tk=1024,
        outs=[(_sds((N_CHIPS, D, ns_in), F32), pl.BlockSpec((None, 512, ns_in), lambda i, j, k: (j, i, 0)))],
        epilogue=_ep_store)
    tok = on_grads({"w_in": gw_in})
    grad_x, dg1 = _matmul(
        "d_x", "nt", dproj, w_in, tm=512, tn=D, tk=ns_in, n_cols=D,
        b_spec=pl.BlockSpec((None, D, ns_in), lambda i, j, k: (k, 0, 0)),
        extras=[(x, _mn(512, D)), (g1, _row(D)), (dx1, _mn(512, D))],
        outs=[(_sds((S, D), F32), _mn(512, D)), (_sds((1, D), F32), _row(D))],
        epilogue=_ep_rms_bwd, deps=[tok])

    smallg = {"norm1_g": dg1, "q_norm_g": dgqk[:, :ATT_W], "k_norm_g": dgqk[:, ATT_W:],
              "ret_gn_g": dgn_g, "ret_gn_b": dgn_b, "norm2_g": dg2}
    return loss, grad_x, smallg


N_CHIPS = 4
N_DEV = 8


def _place():
    x, y, c = lax.axis_index("x"), lax.axis_index("y"), lax.axis_index("c")
    return x, y, c


def _other_chips(x, y):
    out = []
    for fx, fy in ((1, 0), (0, 1), (1, 1)):
        px = 1 - x if fx else x
        py = 1 - y if fy else y
        out.append(((px, py), 2 * px + py))
    return out


SEM_SPEC = pl.BlockSpec(memory_space=pltpu.SEMAPHORE)
ANY_SPEC = pl.BlockSpec(memory_space=pl.ANY)
EFFECT = pltpu.SideEffectType.DATAFLOW_SIDE_EFFECTING


def _ici_copies(kind, srcs, lands, send, recv, which=(0, 1, 2)):
    x, y, c = _place()
    me = 2 * x + y
    out = []
    for w, (s, l) in enumerate(zip(srcs, lands)):
        for j, ((px, py), pidx) in enumerate(_other_chips(x, y)):
            if j not in which:
                continue
            if kind == "gather":
                half = s.shape[0] // 2
                rows = pl.ds(c * half, half)
                src, dst_there, dst_here = s.at[rows, :], l.at[me, rows, :], l.at[pidx, rows, :]
            else:
                src, dst_there, dst_here = s.at[pidx], l.at[me], l.at[pidx]
            out.append((src, dst_there, dst_here, send.at[3 * w + j], recv.at[3 * w + j], (px, py, c)))
    return out


def _exchange_start(name, kind, srcs, land_shapes, which=(0, 1, 2), lands=None):
    n = len(srcs)
    if lands is None:
        lands = [lax.empty(shape, dtype) for shape, dtype in land_shapes]

    def body(*refs):
        src_refs, land_refs = refs[:n], refs[n:2 * n]
        send, recv = refs[2 * n], refs[2 * n + 1]
        token = refs[-1]
        for src, dst, _, ss, rs, dev in _ici_copies(kind, src_refs, land_refs, send, recv, which):
            pltpu.make_async_remote_copy(src_ref=src, dst_ref=dst, send_sem=ss, recv_sem=rs, device_id=dev,
                                         device_id_type=MESH).start()
        token[...] = jnp.zeros_like(token)

    thru = [pltpu.HBM(s.shape, s.dtype) for s in srcs] + [pltpu.HBM(shape, dtype) for shape, dtype in land_shapes]
    res = pl.pallas_call(
        body, name=name,
        out_shape=(pltpu.SemaphoreType.DMA((3 * n,)), pltpu.SemaphoreType.DMA((3 * n,)), *thru, _sds((8, LANES), F32)),
        in_specs=[HBM_SPEC] * (2 * n), out_specs=(SEM_SPEC, SEM_SPEC, *[HBM_SPEC] * (2 * n), VMEM_SPEC),
        input_output_aliases={i: 2 + i for i in range(2 * n)},
        compiler_params=pltpu.CompilerParams(has_side_effects=EFFECT),
    )(*[pltpu.with_memory_space_constraint(s, pltpu.HBM) for s in srcs],
      *[pltpu.with_memory_space_constraint(l, pltpu.HBM) for l in lands])
    return res[0], res[1], list(res[2:2 + n]), list(res[2 + n:2 + 2 * n]), res[-1]


def _exchange_wait(name, kind, send, recv, srcs, lands, after, which=(0, 1, 2)):
    n = len(srcs)

    def body(*refs):
        src_refs, land_refs = refs[:n], refs[n:2 * n]
        send_ref, recv_ref = refs[2 * n], refs[2 * n + 1]
        for src, _, dst, ss, rs, dev in _ici_copies(kind, src_refs, land_refs, send_ref, recv_ref, which):
            cp = pltpu.make_async_remote_copy(src_ref=src, dst_ref=dst, send_sem=ss, recv_sem=rs, device_id=dev,
                                              device_id_type=MESH)
            cp.wait_send()
            cp.wait_recv()

    thru = [pltpu.HBM(t.shape, t.dtype) for t in list(srcs) + list(lands)]
    res = pl.pallas_call(
        body, name=name, out_shape=thru,
        in_specs=[HBM_SPEC] * (2 * n) + [SEM_SPEC, SEM_SPEC, ANY_SPEC], out_specs=[HBM_SPEC] * (2 * n),
        input_output_aliases={i: i for i in range(2 * n)},
        compiler_params=pltpu.CompilerParams(has_side_effects=EFFECT),
    )(*srcs, *lands, send, recv, after)
    return list(res[:n]), list(res[n:])


PAIR_TILE_ELEMS = 1 << 20


def _pair_fill(name, gathered, mine, core, others, chip, write_mine=True):
    k, r, C = gathered.shape
    half = r // 2
    tr = _row_tile(half, C, PAIR_TILE_ELEMS, mult=16)
    nt = half // tr
    n_far = others.shape[0]

    def body(c_ref, o_ref, chip_ref, in_ref, mine_ref, out_ref, slot, send, recv):
        j = pl.program_id(0)
        _sibling_barrier((j == 0) & (pl.program_id(1) == 0))
        b = (j * nt + pl.program_id(1)) % 2
        x, y, c = _place()
        cp = pltpu.make_async_remote_copy(src_ref=in_ref, dst_ref=slot.at[b], send_sem=send.at[b],
                                          recv_sem=recv.at[b], device_id=(x, y, 1 - c), device_id_type=MESH)

        @pl.when(j < n_far)
        def _():
            cp.start()
            cp.wait_recv()
            out_ref[...] = slot[b]
            cp.wait_send()

        @pl.when(j >= n_far)
        def _():
            out_ref[...] = mine_ref[...]

    def far(j):
        return jnp.minimum(j, n_far - 1)

    grid_spec = pltpu.PrefetchScalarGridSpec(
        num_scalar_prefetch=3, grid=(n_far + (2 if write_mine else 0), nt),
        in_specs=[pl.BlockSpec((tr, C), lambda j, i, c, o, m: (
                      (2 * o[far(j)] + c[0]) * nt + jnp.where(j < n_far, i, nt - 1), 0)),
                  pl.BlockSpec((tr, C), lambda j, i, c, o, m: (jnp.where(j < n_far, 0, (j - n_far) * nt + i), 0))],
        out_specs=pl.BlockSpec((tr, C), lambda j, i, c, o, m: (
            jnp.where(j < n_far, 2 * o[far(j)] + 1 - c[0], 2 * m[0] + j - n_far) * nt + i, 0)),
        scratch_shapes=[pltpu.VMEM((2, tr, C), gathered.dtype), pltpu.SemaphoreType.DMA((2,)),
                        pltpu.SemaphoreType.DMA((2,))])
    out = pl.pallas_call(body, name=name, grid_spec=grid_spec, out_shape=_sds((k * r, C), gathered.dtype),
                         input_output_aliases={3: 0}, compiler_params=_params(2, PAIR_FILL_ID))(
                             core, others, chip, gathered.reshape(k * r, C), mine)
    return out.reshape(k, r, C)


def _pair_reduce(name, g, core):
    k, R, C = g.shape
    half = R // 2
    tr = _row_tile(half, C, PAIR_TILE_ELEMS, mult=16)
    nt = half // tr

    def body(c_ref, mine_ref, give_ref, out_ref, wire_ref, stage, slot, send, recv):
        _sibling_barrier((pl.program_id(0) == 0) & (pl.program_id(1) == 0))
        b = (pl.program_id(0) * nt + pl.program_id(1)) % 2
        x, y, c = _place()
        stage[b] = give_ref[...].astype(stage.dtype)
        cp = pltpu.make_async_remote_copy(src_ref=stage.at[b], dst_ref=slot.at[b], send_sem=send.at[b],
                                          recv_sem=recv.at[b], device_id=(x, y, 1 - c), device_id_type=MESH)
        cp.start()
        cp.wait_recv()
        tot = mine_ref[...] + slot[b].astype(F32)
        out_ref[...] = tot
        wire_ref[...] = tot.astype(wire_ref.dtype)
        cp.wait_send()

    blk = (tr, C)
    out_spec = pl.BlockSpec(blk, lambda s, i, c: (s * nt + i, 0))
    grid_spec = pltpu.PrefetchScalarGridSpec(
        num_scalar_prefetch=1, grid=(k, nt),
        in_specs=[pl.BlockSpec(blk, lambda s, i, c: ((2 * s + c[0]) * nt + i, 0)),
                  pl.BlockSpec(blk, lambda s, i, c: ((2 * s + 1 - c[0]) * nt + i, 0))],
        out_specs=[out_spec, out_spec],
        scratch_shapes=[pltpu.VMEM((2, tr, C), CDT), pltpu.VMEM((2, tr, C), CDT), pltpu.SemaphoreType.DMA((2,)),
                        pltpu.SemaphoreType.DMA((2,))])
    g2 = g.reshape(k * R, C)
    out, wire = pl.pallas_call(body, name=name, grid_spec=grid_spec,
                               out_shape=[_sds((k * half, C), F32), _sds((k * half, C), CDT)],
                               compiler_params=_params(2, PAIR_REDUCE_ID))(core, g2, g2)
    return out, wire.reshape(k, half, C)


def _all_reduce_small(v):
    r, cdim = v.shape

    def body(v_ref, o_ref, buf, send, recv):
        x, y, c = _place()
        me = 4 * x + 2 * y + c
        buf[me] = v_ref[...]
        sends = []
        for m in range(1, N_DEV):
            px = 1 - x if m & 4 else x
            py = 1 - y if m & 2 else y
            pc = 1 - c if m & 1 else c
            cp = pltpu.make_async_remote_copy(src_ref=v_ref, dst_ref=buf.at[me], send_sem=send.at[m - 1],
                                              recv_sem=recv.at[m - 1], device_id=(px, py, pc), device_id_type=MESH)
            cp.start()
            sends.append((cp, 4 * px + 2 * py + pc))
        for m, (cp, pidx) in enumerate(sends):
            pltpu.make_async_remote_copy(src_ref=v_ref, dst_ref=buf.at[pidx], send_sem=send.at[m], recv_sem=recv.at[m],
                                         device_id=(x, y, c), device_id_type=MESH).wait_recv()
        for cp, _ in sends:
            cp.wait_send()
        tot = buf[0]
        for k in range(1, N_DEV):
            tot = tot + buf[k]
        o_ref[...] = tot

    return pl.pallas_call(
        body, name="all_reduce_small", in_specs=[VMEM_SPEC], out_specs=VMEM_SPEC,
        out_shape=_sds((r, cdim), F32),
        scratch_shapes=[pltpu.VMEM((N_DEV, r, cdim), F32), pltpu.SemaphoreType.DMA((N_DEV - 1,)),
                        pltpu.SemaphoreType.DMA((N_DEV - 1,))],
    )(v)


def _row_tile(rows, cols, budget_elems=1 << 18, mult=8):
    if rows % mult:
        return rows
    t = max(mult, (budget_elems // cols) // mult * mult)
    while rows % t:
        t -= mult
    return t


def _adamw_update(w, g, m, v):
    nm = ADAM_B1 * m + (1.0 - ADAM_B1) * g
    nv = ADAM_B2 * v + (1.0 - ADAM_B2) * (g * g)
    m_hat = nm / (1.0 - ADAM_B1 ** ADAM_STEP)
    v_hat = nv / (1.0 - ADAM_B2 ** ADAM_STEP)
    return -ADAM_LR * (m_hat / (jnp.sqrt(v_hat) + ADAM_EPS) + ADAM_WD * w), nm, nv


def _adamw(name, w, g, m, v):
    R, C = w.shape
    tr = _row_tile(R, C, 1 << 18)

    def body(w_ref, g_ref, m_ref, v_ref, d_ref, nm_ref, nv_ref):
        d_ref[...], nm_ref[...], nv_ref[...] = _adamw_update(w_ref[...], g_ref[...], m_ref[...], v_ref[...])

    spec = pl.BlockSpec((tr, C), lambda i: (i, 0))
    return pl.pallas_call(body, name=name, grid=(R // tr,), in_specs=[spec] * 4, out_specs=[spec] * 3,
                          out_shape=[_sds((R, C), F32)] * 3, compiler_params=_params(1))(w, g, m, v)


def _sum_share(name, own, by_chip, chip, others, core):
    k, half, C = by_chip.shape
    tr = _row_tile(half, C, PAIR_TILE_ELEMS // 2, mult=16)
    nt = half // tr

    def body(chip_ref, oth_ref, c_ref, own_ref, a_ref, b_ref, cc_ref, g_out, mine, slot, send, recv):
        p = pl.program_id(1)
        _sibling_barrier((pl.program_id(0) == 0) & (p == 0))
        b = pl.program_id(0) % 2
        x, y, c = _place()
        cp = pltpu.make_async_remote_copy(src_ref=mine.at[b], dst_ref=slot.at[b], send_sem=send.at[b],
                                          recv_sem=recv.at[b], device_id=(x, y, 1 - c), device_id_type=MESH)

        @pl.when(p == 0)
        def _():
            tot = ((own_ref[...] + a_ref[...].astype(F32)) + b_ref[...].astype(F32)) + cc_ref[...].astype(F32)
            mine[b] = tot
            cp.start()
            g_out[...] = tot

        @pl.when(p == 1)
        def _():
            cp.wait_recv()
            g_out[...] = slot[b]
            cp.wait_send()

    def piece(j):
        return pl.BlockSpec((tr, C), lambda i, p, chip, oth, c: (oth[j] * nt + i, 0))

    grid_spec = pltpu.PrefetchScalarGridSpec(
        num_scalar_prefetch=3, grid=(nt, 2),
        in_specs=[pl.BlockSpec((tr, C), lambda i, p, chip, oth, c: (chip[0] * nt + i, 0)),
                  piece(0), piece(1), piece(2)],
        out_specs=pl.BlockSpec((tr, C), lambda i, p, chip, oth, c: (
            jnp.where(p == 0, c[0], 1 - c[0]) * nt + i, 0)),
        scratch_shapes=[pltpu.VMEM((2, tr, C), F32), pltpu.VMEM((2, tr, C), F32), pltpu.SemaphoreType.DMA((2,)),
                        pltpu.SemaphoreType.DMA((2,))])
    by2 = by_chip.reshape(k * half, C)
    return pl.pallas_call(body, name=name, grid_spec=grid_spec, out_shape=_sds((2 * half, C), F32),
                          compiler_params=_params(2, SUM_SHARE_ID))(chip, others, core, own, by2, by2, by2)


BIG = ("w_in", "w_proj_a", "w_proj_b", "w_out", "w_up", "w_down")
COL_SHARDED = ("w_in", "w_proj_a", "w_up")
SMALL = ("norm1_g", "q_norm_g", "k_norm_g", "ret_gn_g", "ret_gn_b", "norm2_g")
ALL_W = ("norm1_g", "w_in", "q_norm_g", "k_norm_g", "ret_gn_g", "ret_gn_b", "w_proj_a", "w_proj_b", "w_out",
         "norm2_g", "w_up", "w_down")
LANES = 128


def _to_full(name, gathered):
    k, r, c = gathered.shape
    if name in COL_SHARDED:
        return gathered.transpose(1, 0, 2).reshape(r, k * c)
    return gathered.reshape(k * r, c)


def _to_shard_major(name, full):
    if name in COL_SHARDED:
        r, c4 = full.shape
        return full.reshape(r, N_CHIPS, c4 // N_CHIPS).transpose(1, 0, 2)
    r4, c = full.shape
    return full.reshape(N_CHIPS, r4 // N_CHIPS, c)


def kernel(x, norm1_g, w_in, q_norm_g, k_norm_g, ret_gn_g, ret_gn_b, w_proj_a, w_proj_b, w_out, norm2_g, w_up, w_down, loss_target, m_norm1_g, m_w_in, m_q_norm_g, m_k_norm_g, m_ret_gn_g, m_ret_gn_b, m_w_proj_a, m_w_proj_b, m_w_out, m_norm2_g, m_w_up, m_w_down, v_norm1_g, v_w_in, v_q_norm_g, v_k_norm_g, v_ret_gn_g, v_ret_gn_b, v_w_proj_a, v_w_proj_b, v_w_out, v_norm2_g, v_w_up, v_w_down):
    weights = dict(norm1_g=norm1_g, w_in=w_in, q_norm_g=q_norm_g, k_norm_g=k_norm_g, ret_gn_g=ret_gn_g,
                   ret_gn_b=ret_gn_b, w_proj_a=w_proj_a, w_proj_b=w_proj_b, w_out=w_out, norm2_g=norm2_g,
                   w_up=w_up, w_down=w_down)
    moments_m = dict(norm1_g=m_norm1_g, w_in=m_w_in, q_norm_g=m_q_norm_g, k_norm_g=m_k_norm_g, ret_gn_g=m_ret_gn_g,
                     ret_gn_b=m_ret_gn_b, w_proj_a=m_w_proj_a, w_proj_b=m_w_proj_b, w_out=m_w_out,
                     norm2_g=m_norm2_g, w_up=m_w_up, w_down=m_w_down)
    moments_v = dict(norm1_g=v_norm1_g, w_in=v_w_in, q_norm_g=v_q_norm_g, k_norm_g=v_k_norm_g, ret_gn_g=v_ret_gn_g,
                     ret_gn_b=v_ret_gn_b, w_proj_a=v_w_proj_a, w_proj_b=v_w_proj_b, w_out=v_w_out,
                     norm2_g=v_norm2_g, w_up=v_w_up, w_down=v_w_down)

    mx, my = lax.axis_index("x"), lax.axis_index("y")
    core = lax.axis_index("c").astype(jnp.int32).reshape(1)
    chip = (2 * mx + my).astype(jnp.int32).reshape(1)
    others = jnp.stack([2 * (1 - mx) + my, 2 * mx + 1 - my, 2 * (1 - mx) + 1 - my]).astype(jnp.int32)
    shards = {n: weights[n][0].astype(CDT) for n in BIG}
    def start_gather(name, names):
        return _exchange_start(name, "gather", [shards[n] for n in names],
                               [((N_CHIPS,) + shards[n].shape, CDT) for n in names])

    w_in_shape = [((N_CHIPS,) + shards["w_in"].shape, CDT)]
    n_send, n_recv, n_srcs, n_lands, n_token = _exchange_start(
        "gather_w_in_near_start", "gather", [shards["w_in"]], w_in_shape, which=(0, 1))
    late = [n for n in BIG if n != "w_in"]
    flight = {}

    def near_w_in(after):
        srcs, lands = _exchange_wait("gather_w_in_near_wait", "gather", n_send, n_recv, n_srcs, n_lands, after,
                                     which=(0, 1))
        d_send, d_recv, d_srcs, d_lands, _ = _exchange_start(
            "gather_w_in_diag_start", "gather", srcs, w_in_shape, which=(2,), lands=lands)
        flight["late"] = start_gather("gather_late_start", late)
        w_near = _pair_fill("pair_fill_w_in_near", d_lands[0], d_srcs[0], core, others[:2], chip)
        flight["diag"] = (d_send, d_recv, d_srcs, [w_near])
        return w_near, flight["late"][-1]

    def far_w_in(after):
        d_send, d_recv, d_srcs, d_lands = flight["diag"]
        srcs, lands = _exchange_wait("gather_w_in_diag_wait", "gather", d_send, d_recv, d_srcs, d_lands, after,
                                     which=(2,))
        return _pair_fill("pair_fill_w_in_diag", lands[0], srcs[0], core, others[2:], chip, write_mine=False), None

    def late_weights(after):
        l_send, l_recv, l_srcs, l_lands, _ = flight["late"]
        srcs, lands = _exchange_wait("gather_late_wait", "gather", l_send, l_recv, l_srcs, l_lands, after)
        out = {}
        for n, mine, land in zip(late, srcs, lands):
            out[n] = _to_full(n, _pair_fill("pair_fill_%s" % n, land, mine, core, others, chip))
        return out

    pending = []

    def on_grads(group):
        names = list(group)
        red = [_pair_reduce("pair_reduce_%s" % n, g if g.ndim == 3 else _to_shard_major(n, g), core)
               for n, g in group.items()]
        wires = [wire for _, wire in red]
        send, recv, srcs, lands, token = _exchange_start(
            "scatter_start_%s" % names[0], "scatter", wires, [(wire.shape, wire.dtype) for wire in wires])
        pending.append((names, [own for own, _ in red], send, recv, srcs, lands))
        return token

    small = {n: weights[n].reshape(1, -1) for n in SMALL}

    loss, grad_x, small_g = _local_step(x[0], loss_target[0], n_srcs[0], chip, others, near_w_in, far_w_in, small,
                                        late_weights, on_grads, deps0=[n_token])
    loss = lax.psum(loss, ("x", "y", "c"))

    out_g, out_d, out_m, out_v = {}, {}, {}, {}
    for names, owns, send, recv, srcs, lands in pending:
        _, got = _exchange_wait("scatter_wait_%s" % names[0], "scatter", send, recv, srcs, lands, grad_x)
        for n, own, by_chip in zip(names, owns, got):
            shape = weights[n].shape
            g2 = _sum_share("sum_share_%s" % n, own, by_chip, chip, others, core)
            d, nm, nv = _adamw("adamw_%s" % n, weights[n][0], g2, moments_m[n][0], moments_v[n][0])
            out_g[n], out_d[n], out_m[n], out_v[n] = (t.reshape(shape) for t in (g2, d, nm, nv))

    packed = jnp.concatenate([small_g[n].reshape(1, -1) for n in SMALL], axis=1)
    red = _all_reduce_small(packed.reshape(-1, LANES)).reshape(1, -1)
    off = 0
    for n in SMALL:
        shape = weights[n].shape
        row = (1, weights[n].size)
        g2 = red[:, off:off + row[1]]
        off += row[1]
        d, nm, nv = _adamw("adamw_%s" % n, weights[n].reshape(row), g2, moments_m[n].reshape(row),
                           moments_v[n].reshape(row))
        out_g[n], out_d[n], out_m[n], out_v[n] = (t.reshape(shape) for t in (g2, d, nm, nv))

    return (loss, grad_x[None], *[out_g[n] for n in ALL_W], *[out_d[n] for n in ALL_W],
            *[out_m[n] for n in ALL_W], *[out_v[n] for n in ALL_W])
```

```python
import functools

import jax
import jax.numpy as jnp
from jax import lax
from jax.experimental import pallas as pl
from jax.experimental.pallas import tpu as pltpu

CDT = jnp.bfloat16
F32 = jnp.float32
EPS = 1e-6

ATT_GROUPS = ((128, 1), (512, 4), (2048, 16))
ATT_HPG = 4
ATT_HEADS = 12
HD = 128
BLK = 128
ATT_W = ATT_HEADS * HD
GW = ATT_HPG * HD
RET_HEADS = 4

ADAM_LR = 0.001
ADAM_B1 = 0.9
ADAM_B2 = 0.999
ADAM_EPS = 1e-08
ADAM_WD = 0.01
ADAM_STEP = 10

VMEM_LIMIT_BYTES = 56 * 1024 * 1024
MESH = pl.DeviceIdType.MESH
HBM_SPEC = pl.BlockSpec(memory_space=pltpu.HBM)
VMEM_SPEC = pl.BlockSpec(memory_space=pltpu.VMEM)


def _params(n_axes, collective_id=None):
    return pltpu.CompilerParams(dimension_semantics=("arbitrary",) * n_axes,
                                vmem_limit_bytes=VMEM_LIMIT_BYTES, collective_id=collective_id)


PAIR_FILL_ID, PAIR_REDUCE_ID, SUM_SHARE_ID = 1, 2, 3


def _sibling_barrier(first_step):
    @pl.when(first_step)
    def _():
        sem = pltpu.get_barrier_semaphore()
        x, y, c = lax.axis_index("x"), lax.axis_index("y"), lax.axis_index("c")
        pl.semaphore_signal(sem, inc=1, device_id=(x, y, 1 - c), device_id_type=pl.DeviceIdType.MESH)
        pl.semaphore_wait(sem, 1)


def _dot_nn(a, b):
    return jnp.dot(a, b, preferred_element_type=F32)


def _dot_nt(a, b):
    return lax.dot_general(a, b, (((1,), (1,)), ((), ())), preferred_element_type=F32)


def _dot_tn(a, b):
    return lax.dot_general(a, b, (((0,), (0,)), ((), ())), preferred_element_type=F32)


def _sigmoid(v):
    return 1.0 / (1.0 + jnp.exp(-v))


def _matmul(name, mode, a, b, *, tm, tn, tk, extras=(), outs, epilogue, deps=(), b_spec=None, n_cols=None,
            prefetch=(), alias_dep_to_out=None, j_outer=False):
    deps = [d for d in deps if d is not None]
    if mode == "tn":
        K, M = a.shape
    else:
        M, K = a.shape
    if b_spec is None:
        (N, K2) = b.shape if mode == "nt" else b.shape[::-1]
        assert K == K2, (name, a.shape, b.shape)
        if mode == "nt":
            b_spec = pl.BlockSpec((tn, tk), lambda i, j, k, *p: (j, k))
        else:
            b_spec = pl.BlockSpec((tk, tn), lambda i, j, k, *p: (k, j))
    else:
        N = n_cols
    assert M % tm == 0 and N % tn == 0 and K % tk == 0, (name, a.shape, b.shape)
    ni, nj, nk = M // tm, N // tn, K // tk
    if mode == "tn":
        a_spec = pl.BlockSpec((tk, tm), lambda i, j, k, *p: (k, i))
    else:
        a_spec = pl.BlockSpec((tm, tk), lambda i, j, k, *p: (i, k))
    dot = {"nn": _dot_nn, "nt": _dot_nt, "tn": _dot_tn}[mode]
    n_ex, n_out, n_dep, n_pre = len(extras), len(outs), len(deps), len(prefetch)
    grid = (ni, nj, nk)
    if j_outer:
        grid = (nj, ni, nk)

        def swapped(spec):
            return pl.BlockSpec(spec.block_shape, lambda j, i, k, *p: spec.index_map(i, j, k, *p))

        a_spec, b_spec = swapped(a_spec), swapped(b_spec)
        extras = [(e, swapped(s)) for e, s in extras]
        outs = [(o, swapped(s)) for o, s in outs]

    def body(*refs):
        refs = refs[n_pre:]
        a_ref, b_ref = refs[0], refs[1]
        ex = refs[2:2 + n_ex]
        out = refs[2 + n_ex + n_dep:2 + n_ex + n_dep + n_out]
        acc = refs[-1] if nk > 1 else None
        i = pl.program_id(1 if j_outer else 0)
        k = pl.program_id(2)
        if nk == 1:
            epilogue(dot(a_ref[...].astype(CDT), b_ref[...].astype(CDT)), ex, out, i)
            return

        @pl.when(k == 0)
        def _():
            acc[...] = jnp.zeros_like(acc)

        acc[...] += dot(a_ref[...].astype(CDT), b_ref[...].astype(CDT))

        @pl.when(k == nk - 1)
        def _():
            epilogue(acc[...], ex, out, i)

    grid_spec = pltpu.PrefetchScalarGridSpec(
        num_scalar_prefetch=n_pre, grid=grid,
        in_specs=[a_spec, b_spec] + [s for _, s in extras] + [pl.BlockSpec(memory_space=pl.ANY)] * n_dep,
        out_specs=[s for _, s in outs],
        scratch_shapes=[pltpu.VMEM((tm, tn), F32)] if nk > 1 else [])
    aliases = {}
    if alias_dep_to_out is not None:
        aliases = {n_pre + 2 + n_ex + alias_dep_to_out[0]: alias_dep_to_out[1]}
    res = pl.pallas_call(
        body, name=name, grid_spec=grid_spec, out_shape=[o for o, _ in outs], input_output_aliases=aliases,
        compiler_params=_params(3),
    )(*prefetch, a, b, *[e for e, _ in extras], *deps)
    return res


def _mn(tm, tn, col_off=0):
    return pl.BlockSpec((tm, tn), lambda i, j, k, *p: (i, j + col_off))


def _row(tn):
    return pl.BlockSpec((1, tn), lambda i, j, k, *p: (0, j))


def _ep_store(acc, ex, out, i):
    out[0][...] = acc.astype(out[0].dtype)


def _ep_resid_norm(acc, ex, out, i):
    x1 = ex[0][...] + acc
    out[0][...] = x1
    rstd = lax.rsqrt(jnp.mean(x1 * x1, axis=-1, keepdims=True) + EPS)
    out[1][...] = (x1 * rstd * ex[1][...]).astype(out[1].dtype)


def _ep_up(acc, ex, out, i):
    out[0][...] = acc.astype(out[0].dtype)
    r = jnp.maximum(acc, 0.0)
    out[1][...] = (r * r).astype(out[1].dtype)


def _ep_down_loss(acc, ex, out, i, inv_d):
    diff = (ex[0][...] + acc) - ex[1][...]
    dx2 = diff * inv_d
    out[0][...] = dx2
    out[1][...] = dx2.astype(out[1].dtype)

    @pl.when(i == 0)
    def _():
        out[2][...] = jnp.zeros_like(out[2])

    out[2][...] += jnp.sum(diff * diff, axis=0, keepdims=True)


def _ep_dh(acc, ex, out, i):
    h = ex[0][...].astype(F32)
    out[0][...] = (acc * (2.0 * jnp.maximum(h, 0.0))).astype(out[0].dtype)


def _ep_rms_bwd(acc, ex, out, i):
    x = ex[0][...]
    g = ex[1][...]
    rstd = lax.rsqrt(jnp.mean(x * x, axis=-1, keepdims=True) + EPS)
    xh = x * rstd
    dxh = acc * g
    dx = ex[2][...] + rstd * (dxh - xh * jnp.mean(dxh * xh, axis=-1, keepdims=True))
    out[0][...] = dx
    for copy in out[1:-1]:
        copy[...] = dx.astype(copy.dtype)
    dg = out[-1]

    @pl.when(i == 0)
    def _():
        dg[...] = jnp.zeros_like(dg)

    dg[...] += jnp.sum(acc * xh, axis=0, keepdims=True)


def _ep_gates(acc, ex, out, i):
    sa = _sigmoid(ex[0][...].astype(F32))
    sb = _sigmoid(ex[1][...].astype(F32))
    dpa = acc * sa
    dpb = acc * sb
    out[0][...] = dpa.astype(out[0].dtype)
    out[1][...] = dpb.astype(out[1].dtype)
    out[2][...] = (dpa * ex[2][...].astype(F32) * (1.0 - sa)).astype(out[2].dtype)
    out[3][...] = (dpb * ex[3][...].astype(F32) * (1.0 - sb)).astype(out[3].dtype)


def _sds(shape, dtype):
    return jax.ShapeDtypeStruct(shape, dtype)


def _in_proj_mine(x, g, w_mine, chip, proj_sds, deps, tm=512):
    S, D = x.shape
    ns = w_mine.shape[1]
    deps = [d for d in deps if d is not None]

    def body(c_ref, x_ref, g_ref, w_ref, *rest):
        xn_ref, proj_ref = rest[len(deps)], rest[len(deps) + 1]
        xv = x_ref[...]
        rstd = lax.rsqrt(jnp.mean(xv * xv, axis=-1, keepdims=True) + EPS)
        xn = (xv * rstd * g_ref[...]).astype(xn_ref.dtype)
        xn_ref[...] = xn
        proj_ref[...] = _dot_nn(xn, w_ref[...]).astype(proj_ref.dtype)

    grid_spec = pltpu.PrefetchScalarGridSpec(
        num_scalar_prefetch=1, grid=(S // tm,),
        in_specs=[pl.BlockSpec((tm, D), lambda i, c: (i, 0)), pl.BlockSpec((1, D), lambda i, c: (0, 0)),
                  pl.BlockSpec((D, ns), lambda i, c: (0, 0))] + [pl.BlockSpec(memory_space=pl.ANY)] * len(deps),
        out_specs=[pl.BlockSpec((tm, D), lambda i, c: (i, 0)), pl.BlockSpec((tm, ns), lambda i, c: (i, c[0]))])
    return pl.pallas_call(body, name="in_proj_mine", grid_spec=grid_spec, out_shape=[_sds((S, D), CDT), proj_sds],
                          compiler_params=_params(1))(chip, x, g, w_mine, *deps)


def _rm_shape(S, d, width):
    return (S, width) if d == 1 else (d, S // d, width)


def _rm_spec(tm, d, width):
    if d == 1:
        return pl.BlockSpec((tm, width), lambda i: (i, 0))
    return pl.BlockSpec((d, tm // d, width), lambda i: (0, i, 0))


def _rm_put(dst_ref, cols, buf_ref, d):
    if d == 1:
        dst_ref[:, cols] = buf_ref[...].astype(dst_ref.dtype)
        return
    m = buf_ref.shape[0] // d
    for r in range(d):
        dst_ref[r, :, cols] = buf_ref[pl.ds(r, m, stride=d), :].astype(dst_ref.dtype)


def _rm_reader(buf_ref, src_ref, d):
    if d == 1:
        return lambda s, rows: src_ref[rows, s * HD:(s + 1) * HD].astype(F32)
    m = buf_ref.shape[1] // d
    for s in range(buf_ref.shape[0]):
        for r in range(d):
            buf_ref.at[s][pl.ds(r, m, stride=d), :] = src_ref[r, :, s * HD:(s + 1) * HD].astype(F32)
    return lambda s, rows: buf_ref.at[s][rows, :]


def _qknorm_fwd(proj, gqk, tm=512):
    S = proj.shape[0]
    W = 2 * ATT_W
    dil = [d for _, d in ATT_GROUPS]

    def body(p_ref, g_ref, o0, o1, o2, buf):
        outs = (o0, o1, o2)
        for hd in range(3 * ATT_HEADS):
            which, head = hd // ATT_HEADS, hd % ATT_HEADS
            grp, slot = head // ATT_HPG, head % ATT_HPG
            cols = slice(hd * HD, (hd + 1) * HD)

            def chunk(rows, which=which, cols=cols):
                v = p_ref[rows, cols].astype(F32)
                if which < 2:
                    rstd = lax.rsqrt(jnp.mean(v * v, axis=-1, keepdims=True) + EPS)
                    v = v * rstd * g_ref[:, cols]
                buf[rows, :] = v

            chunk(slice(None))
            _rm_put(outs[grp], slice(which * GW + slot * HD, which * GW + (slot + 1) * HD), buf, dil[grp])

    return pl.pallas_call(
        body, name="qknorm_fwd", grid=(S // tm,),
        in_specs=[pl.BlockSpec((tm, 3 * ATT_W), lambda i: (i, 0)), pl.BlockSpec((1, W), lambda i: (0, 0))],
        out_specs=[_rm_spec(tm, d, 3 * GW) for d in dil],
        out_shape=[_sds(_rm_shape(S, d, 3 * GW), CDT) for d in dil],
        scratch_shapes=[pltpu.VMEM((tm, HD), F32)],
        compiler_params=_params(1))(proj, gqk)


def _qknorm_bwd(proj, gqk, dqs, dks, dvs, dproj, tm=256):
    S = proj.shape[0]
    W = 2 * ATT_W
    dil = [d for _, d in ATT_GROUPS]

    def body(p_ref, g_ref, *refs):
        ins = refs[0:9]
        o_ref, dg_ref = refs[10], refs[11]
        bufs = refs[12:21]
        i = pl.program_id(0)

        @pl.when(i == 0)
        def _():
            dg_ref[...] = jnp.zeros_like(dg_ref)

        nat = [_rm_reader(bufs[j], ins[j], dil[j % 3]) for j in range(9)]
        dq_get, dk_get, dv_get = nat[0:3], nat[3:6], nat[6:9]
        for hd in range(2 * ATT_HEADS):
            sl = slice(hd * HD, (hd + 1) * HD)
            head = hd % ATT_HEADS
            grp, slot = head // ATT_HPG, head % ATT_HPG
            get = (dq_get if hd < ATT_HEADS else dk_get)[grp]

            def chunk(rows, sl=sl, slot=slot, get=get):
                dn = get(slot, rows)
                v = p_ref[rows, sl].astype(F32)
                rstd = lax.rsqrt(jnp.mean(v * v, axis=-1, keepdims=True) + EPS)
                vh = v * rstd
                dg_ref[:, sl] += jnp.sum(dn * vh, axis=0, keepdims=True)
                dvh = dn * g_ref[:, sl]
                o_ref[rows, sl] = (rstd * (dvh - vh * jnp.mean(dvh * vh, axis=-1, keepdims=True))).astype(o_ref.dtype)

            chunk(slice(None))
        for head in range(ATT_HEADS):
            grp, slot = head // ATT_HPG, head % ATT_HPG
            o_ref[:, W + head * HD:W + (head + 1) * HD] = dv_get[grp](slot, slice(None)).astype(o_ref.dtype)

    return pl.pallas_call(
        body, name="qknorm_bwd", grid=(S // tm,),
        in_specs=[pl.BlockSpec((tm, W), lambda i: (i, 0)), pl.BlockSpec((1, W), lambda i: (0, 0))]
        + [_rm_spec(tm, d, GW) for d in dil] * 3 + [pl.BlockSpec(memory_space=pl.ANY)],
        out_specs=[pl.BlockSpec((tm, 3 * ATT_W), lambda i: (i, 0)), pl.BlockSpec((1, W), lambda i: (0, 0))],
        out_shape=[_sds(dproj.shape, dproj.dtype), _sds((1, W), F32)],
        scratch_shapes=[pltpu.VMEM((ATT_HPG, tm, HD), F32)] * 9,
        input_output_aliases={11: 0},
        compiler_params=_params(1))(proj, gqk, *dqs, *dks, *dvs, dproj)


def _att_mask(n):
    qi = lax.broadcasted_iota(jnp.int32, (BLK, 2 * BLK), 0)
    kj = lax.broadcasted_iota(jnp.int32, (BLK, 2 * BLK), 1)
    dist = BLK + qi - kj
    valid_all = (dist >= 0) & (dist <= BLK)
    return valid_all & ((kj >= BLK) | (n > 0)), valid_all, dist.astype(F32)


def _att_slopes(grp):
    return [2.0 ** (-8.0 * (grp * ATT_HPG + hh + 1) / ATT_HEADS) for hh in range(ATT_HPG)]


ATT_PLANES_PER_STEP = 4


def _att_planes(d):
    return d if d > 1 else ATT_PLANES_PER_STEP


def _att_3d(a, d):
    return a.reshape(ATT_PLANES_PER_STEP, a.shape[0] // ATT_PLANES_PER_STEP, a.shape[1]) if d == 1 else a


def _att_spec(row_fn, col=0):
    return pl.BlockSpec((ATT_PLANES_PER_STEP, BLK, GW), lambda r, n: (r, row_fn(n), col))


def _att_chains():
    return [(rr * ATT_HPG + hh, rr, slice(hh * HD, (hh + 1) * HD), hh)
            for rr in range(ATT_PLANES_PER_STEP) for hh in range(ATT_HPG)]


def _att_qkv_specs(nb):
    last = nb - 1

    def cur(n):
        return jnp.minimum(n, last)

    def prev(n):
        return jnp.maximum(jnp.minimum(n, last) - 1, 0)

    return [_att_spec(cur, 0), _att_spec(prev, 1), _att_spec(cur, 1), _att_spec(prev, 2), _att_spec(cur, 2),
            _att_spec(lambda n: last, 1), _att_spec(lambda n: last, 2)]


def _att_prev(seg, n, prev_ref, last_ref, rr, sl):
    t = prev_ref[rr, :, sl]
    if seg and rr > 0:
        t = jnp.where(n == 0, last_ref[rr - 1, :, sl], t)
    return t


def _att_fwd(grp, S, qkv):
    _, d = ATT_GROUPS[grp]
    seg = d == 1
    P = _att_planes(d)
    L = S // P
    nb = L // BLK
    assert P % ATT_PLANES_PER_STEP == 0 and (not seg or P == ATT_PLANES_PER_STEP)
    slopes = _att_slopes(grp)
    scale = HD ** -0.5
    chains = _att_chains()

    def body(q_ref, kp_ref, kc_ref, vp_ref, vc_ref, kl_ref, vl_ref, o_ref, l_ref, s_buf, p_buf, den_buf):
        n = pl.program_id(1)
        valid, valid_all, distf = _att_mask(n)
        for c, rr, sl, hh in chains:
            k = jnp.concatenate([_att_prev(seg, n, kp_ref, kl_ref, rr, sl), kc_ref[rr, :, sl]], axis=0)
            s_buf[c] = _dot_nt(q_ref[rr, :, sl], k)
        for c, rr, sl, hh in chains:
            s = s_buf[c] * scale + (-slopes[hh] * d) * distf
            s = jnp.where(valid_all if seg and rr > 0 else valid, s, -1e30)
            m = jnp.max(s, axis=-1, keepdims=True)
            p = jnp.exp(s - m)
            den = jnp.sum(p, axis=-1, keepdims=True)
            p_buf[c] = p.astype(CDT)
            den_buf[c] = jnp.broadcast_to(den, (BLK, HD))
            l_ref[rr, :, sl] = jnp.broadcast_to(m + jnp.log(den), (BLK, HD))
        for c, rr, sl, hh in chains:
            v = jnp.concatenate([_att_prev(seg, n, vp_ref, vl_ref, rr, sl), vc_ref[rr, :, sl]], axis=0)
            o_ref[rr, :, sl] = _dot_nn(p_buf[c], v) / den_buf[c]

    out_spec = _att_spec(lambda n: n)
    n_ch = len(chains)
    q3 = _att_3d(qkv, d)
    o, l = pl.pallas_call(
        body, name="att_fwd_g%d" % grp, grid=(P // ATT_PLANES_PER_STEP, nb),
        in_specs=_att_qkv_specs(nb),
        out_specs=[out_spec, out_spec],
        out_shape=[_sds((P, L, GW), F32)] * 2,
        scratch_shapes=[pltpu.VMEM((n_ch, BLK, 2 * BLK), F32), pltpu.VMEM((n_ch, BLK, 2 * BLK), CDT),
                        pltpu.VMEM((n_ch, BLK, HD), F32)],
        compiler_params=_params(2),
    )(*[q3] * 7)
    return o.reshape(_rm_shape(S, d, GW)), l.reshape(_rm_shape(S, d, GW))


def _att_bwd(grp, S, qkv, lse, do_g, c_g):
    _, d = ATT_GROUPS[grp]
    seg = d == 1
    P = _att_planes(d)
    L = S // P
    nb = L // BLK
    assert P % ATT_PLANES_PER_STEP == 0 and (not seg or P == ATT_PLANES_PER_STEP)
    slopes = _att_slopes(grp)
    scale = HD ** -0.5
    last = nb - 1
    chains = _att_chains()

    def body(q_ref, kp_ref, kc_ref, vp_ref, vc_ref, kl_ref, vl_ref, l_ref, do_ref, c_ref, dq_ref, dk_ref, dv_ref,
             ck, cv, fk, fv, s_buf, dp_buf, p_buf, ds_buf):
        n = pl.program_id(1)

        @pl.when(n == 0)
        def _():
            for buf in (ck, cv, fk, fv):
                buf[...] = jnp.zeros_like(buf)

        @pl.when(n < nb)
        def _():
            valid, valid_all, distf = _att_mask(n)
            for c, rr, sl, hh in chains:
                k = jnp.concatenate([_att_prev(seg, n, kp_ref, kl_ref, rr, sl), kc_ref[rr, :, sl]], axis=0)
                v = jnp.concatenate([_att_prev(seg, n, vp_ref, vl_ref, rr, sl), vc_ref[rr, :, sl]], axis=0)
                s_buf[c] = _dot_nt(q_ref[rr, :, sl], k)
                dp_buf[c] = _dot_nt(do_ref[rr, :, sl], v)
            for c, rr, sl, hh in chains:
                s = s_buf[c] * scale + (-slopes[hh] * d) * distf
                p = jnp.where(valid_all if seg and rr > 0 else valid, jnp.exp(s - l_ref[rr, :, sl][:, 0:1]), 0.0)
                p_buf[c] = p.astype(CDT)
                ds_buf[c] = (p * (dp_buf[c] + c_ref[rr, :, sl][:, 0:1]) * scale).astype(CDT)
            for c, rr, sl, hh in chains:
                k = jnp.concatenate([_att_prev(seg, n, kp_ref, kl_ref, rr, sl), kc_ref[rr, :, sl]], axis=0)
                ds = ds_buf[c]
                dq_ref[rr, :, sl] = _dot_nn(ds, k)
                dk = _dot_tn(ds, q_ref[rr, :, sl])
                dv = _dot_tn(p_buf[c], do_ref[rr, :, sl])
                dk_ref[rr, :, sl] = ck[rr, :, sl] + dk[0:BLK]
                dv_ref[rr, :, sl] = cv[rr, :, sl] + dv[0:BLK]
                ck[rr, :, sl] = dk[BLK:2 * BLK]
                cv[rr, :, sl] = dv[BLK:2 * BLK]
                if seg and rr > 0:
                    @pl.when(n == 0)
                    def _(rr=rr, sl=sl, dk=dk, dv=dv):
                        fk[rr - 1, :, sl] = dk[0:BLK]
                        fv[rr - 1, :, sl] = dv[0:BLK]

        @pl.when(n == nb)
        def _():
            dk_ref[...] = ck[...] + fk[...]
            dv_ref[...] = cv[...] + fv[...]

    blk = (ATT_PLANES_PER_STEP, BLK, GW)
    at_q = _att_spec(lambda n: jnp.minimum(n, last))
    behind = _att_spec(lambda n: jnp.maximum(n - 1, 0))
    n_ch = len(chains)
    q3 = _att_3d(qkv, d)
    res = pl.pallas_call(
        body, name="att_bwd_g%d" % grp, grid=(P // ATT_PLANES_PER_STEP, nb + 1),
        in_specs=_att_qkv_specs(nb) + [at_q, at_q, at_q],
        out_specs=[at_q, behind, behind],
        out_shape=[_sds((P, L, GW), F32)] * 3,
        scratch_shapes=[pltpu.VMEM(blk, F32)] * 4
        + [pltpu.VMEM((n_ch, BLK, 2 * BLK), F32), pltpu.VMEM((n_ch, BLK, 2 * BLK), F32),
           pltpu.VMEM((n_ch, BLK, 2 * BLK), CDT), pltpu.VMEM((n_ch, BLK, 2 * BLK), CDT)],
        compiler_params=_params(2),
    )(*[q3] * 7, _att_3d(lse, d), _att_3d(do_g, d), _att_3d(c_g, d))
    return [t.reshape(_rm_shape(S, d, GW)) for t in res]


def _mix_alpha(l0, l1, l2):
    mx = jnp.maximum(jnp.maximum(l0, l1), l2)
    e = [jnp.exp(l0 - mx), jnp.exp(l1 - mx), jnp.exp(l2 - mx)]
    tot = e[0] + e[1] + e[2]
    return [ei / tot for ei in e]


def _mix_fwd(S, os_, ls_, tm=512):
    dil = [d for _, d in ATT_GROUPS]

    def body(*refs):
        out, bufs = refs[6], refs[7:13]
        get = [_rm_reader(bufs[j], refs[j], dil[j % 3]) for j in range(6)]
        rows = slice(None)
        for s in range(ATT_HPG):
            al = _mix_alpha(*[get[3 + g](s, rows) for g in range(3)])
            mixed = al[0] * get[0](s, rows) + al[1] * get[1](s, rows) + al[2] * get[2](s, rows)
            out[:, s * HD:(s + 1) * HD] = mixed.astype(out.dtype)

    specs = [_rm_spec(tm, d, GW) for d in dil]
    return pl.pallas_call(
        body, name="mix_fwd", grid=(S // tm,), in_specs=specs * 2, out_specs=pl.BlockSpec((tm, GW), lambda i: (i, 0)),
        out_shape=_sds((S, GW), CDT), scratch_shapes=[pltpu.VMEM((ATT_HPG, tm, HD), F32)] * 6,
        compiler_params=_params(1))(*os_, *ls_)


def _mix_bwd(S, os_, ls_, do_a, tm=512):
    dil = [d for _, d in ATT_GROUPS]

    def body(*refs):
        d_ref, outs, bufs, tmps = refs[6], refs[7:13], refs[13:19], refs[19:25]
        get = [_rm_reader(bufs[j], refs[j], dil[j % 3]) for j in range(6)]
        for s in range(ATT_HPG):
            cols = slice(s * HD, (s + 1) * HD)

            def chunk(rows, s=s, cols=cols):
                al = _mix_alpha(*[get[3 + g](s, rows) for g in range(3)])
                dv = d_ref[rows, cols]
                o_a = al[0] * get[0](s, rows) + al[1] * get[1](s, rows) + al[2] * get[2](s, rows)
                dsum = jnp.sum(dv * o_a, axis=-1, keepdims=True)
                for g in range(3):
                    tmps[g][rows, :] = al[g] * dv
                    tmps[3 + g][rows, :] = -(al[g] * dsum)

            chunk(slice(None))
            for j in range(6):
                _rm_put(outs[j], cols, tmps[j], dil[j % 3])

    specs = [_rm_spec(tm, d, GW) for d in dil]
    res = pl.pallas_call(
        body, name="mix_bwd", grid=(S // tm,), in_specs=specs * 2 + [pl.BlockSpec((tm, GW), lambda i: (i, 0))],
        out_specs=specs * 2,
        out_shape=[_sds(_rm_shape(S, d, GW), CDT) for d in dil] + [_sds(_rm_shape(S, d, GW), F32) for d in dil],
        scratch_shapes=[pltpu.VMEM((ATT_HPG, tm, HD), F32)] * 6 + [pltpu.VMEM((tm, HD), F32)] * 6,
        compiler_params=_params(1))(*os_, *ls_, do_a)
    return res[:3], res[3:]


def _ret_tables(dk):
    H, C = RET_HEADS, BLK
    log_g = jnp.log(1.0 - 2.0 ** (-5.0 - jnp.arange(H, dtype=F32)))
    idx = jnp.arange(C, dtype=F32)
    diff = idx[:, None] - idx[None, :]
    decay = jnp.where(diff >= 0, jnp.exp(log_g[:, None, None] * jnp.maximum(diff, 0.0)), 0.0)
    xi = jnp.exp(log_g[:, None] * (idx[None, :] + 1.0))
    zeta = jnp.exp(log_g[:, None] * (C - 1.0 - idx[None, :])) * (dk ** -0.5)
    g_chunk = jnp.exp(log_g * C)
    bc = lambda t: jnp.broadcast_to(t[:, :, None], (H, C, C))
    return decay, bc(xi), bc(zeta), jnp.broadcast_to(g_chunk[:, None, None], (H, 8, C))


def _gn_fwd(o, g, b):
    mu = jnp.mean(o, axis=-1, keepdims=True)
    xc = o - mu
    rstd = lax.rsqrt(jnp.mean(xc * xc, axis=-1, keepdims=True) + EPS)
    yh = xc * rstd
    return yh, rstd, yh * g + b


def _ret_specs(dk, dv, order):
    H = RET_HEADS
    qk_w, v_w = H * dk, H * dv
    off_q = 3 * ATT_W
    off_k, off_v, off_g = off_q + qk_w, off_q + 2 * qk_w, off_q + 2 * qk_w + v_w
    assert 2 * dk == dv and all(off % dv == 0 for off in (off_q, off_k, off_v, off_g))

    def col(off, j):
        return pl.BlockSpec((BLK, dv), lambda i: (order(i), off // dv + j))

    tab = pl.BlockSpec((H, BLK, BLK), lambda i: (0, 0, 0))
    return ([col(off_q, j) for j in range(H // 2)] + [col(off_k, j) for j in range(H // 2)]
            + [col(off_v, j) for j in range(H)] + [col(off_g, j) for j in range(H)]
            + [tab, tab, tab, pl.BlockSpec((H, 8, BLK), lambda i: (0, 0, 0))])


def _ret_heads(refs, dk):
    H = RET_HEADS
    q_refs, k_refs = refs[0:H // 2], refs[H // 2:H]
    v_refs, gr_refs = refs[H:2 * H], refs[2 * H:3 * H]

    def head(h):
        cols = slice((h % 2) * dk, (h % 2 + 1) * dk)
        return q_refs[h // 2][:, cols], k_refs[h // 2][:, cols], v_refs[h][...], gr_refs[h][...]

    return head, refs[3 * H:3 * H + 4]


def _ret_fwd(proj, gn_g, gn_b, dk, dv):
    S = proj.shape[0]
    N = S // BLK
    H = RET_HEADS
    kscale = dk ** -0.5
    n_in = 3 * H + 4

    def body(*refs):
        head, (dec_ref, xi_ref, zeta_ref, gc_ref) = _ret_heads(refs, dk)
        g_ref, b_ref, opre_ref, or_ref, st_ref, state, s_buf, cross_buf = refs[n_in:n_in + 8]
        n = pl.program_id(0)

        @pl.when(n == 0)
        def _():
            state[...] = jnp.zeros_like(state)

        for h in range(H):
            q, k, v, _ = head(h)
            s_buf[h] = _dot_nt(q, k)
            st = state[h]
            st_c = st.astype(CDT)
            st_ref[h] = st_c
            cross_buf[h] = _dot_nn(q, st_c)
            kz = (k.astype(F32) * zeta_ref[h][:, 0:1]).astype(CDT)
            state[h] = st * gc_ref[h][0:1, 0:1] + _dot_tn(kz, v)
        for h in range(H):
            vs = slice(h * dv, (h + 1) * dv)
            _, _, v, gr = head(h)
            s = s_buf[h] * kscale * dec_ref[h]
            o = _dot_nn(s.astype(CDT), v) + cross_buf[h] * xi_ref[h][:, 0:1]
            opre_ref[:, vs] = o
            _, _, y = _gn_fwd(o, g_ref[:, vs], b_ref[:, vs])
            gr = gr.astype(F32)
            or_ref[:, vs] = (y * (gr * _sigmoid(gr))).astype(or_ref.dtype)

    v_w = H * dv
    row = pl.BlockSpec((1, v_w), lambda i: (0, 0))
    tile = pl.BlockSpec((BLK, v_w), lambda i: (i, 0))
    return pl.pallas_call(
        body, name="ret_fwd", grid=(N,),
        in_specs=_ret_specs(dk, dv, lambda i: i) + [row, row],
        out_specs=[tile, tile, pl.BlockSpec((None, H, dk, dv), lambda i: (i, 0, 0, 0))],
        out_shape=[_sds((S, v_w), F32), _sds((S, v_w), CDT), _sds((N, H, dk, dv), CDT)],
        scratch_shapes=[pltpu.VMEM((H, dk, dv), F32), pltpu.VMEM((H, BLK, BLK), F32), pltpu.VMEM((H, BLK, dv), F32)],
        compiler_params=_params(1),
    )(*[proj] * (3 * H), *_ret_tables(dk), gn_g, gn_b)


def _ret_bwd(proj, gn_g, gn_b, o_pre, states, d_or, dga, dgb, dk, dv):
    S, in_w = proj.shape
    N = S // BLK
    H = RET_HEADS
    qk_w, v_w = H * dk, H * dv
    kscale = dk ** -0.5
    n_in = 3 * H + 4
    out_w = 2 * qk_w + 2 * v_w
    gate_w = dga.shape[1]
    col0 = 3 * ATT_W
    assert col0 + out_w + 2 * gate_w == in_w
    rev = lambda i: N - 1 - i

    def body(*refs):
        head, (dec_ref, xi_ref, zeta_ref, gc_ref) = _ret_heads(refs, dk)
        (g_ref, b_ref, opre_ref, st_ref, dor_ref, dga_ref, dgb_ref, dproj_ref, dg_ref, db_ref, dstate, stage,
         sem, do_buf, dox_buf, a_buf, g_buf, dq_buf, dk_buf, dv_buf) = refs[n_in:n_in + 20]
        i = pl.program_id(0)
        slot = i % 2
        out_ref = stage.at[slot]

        def out_copy(s, step):
            rows = pl.ds(pl.multiple_of(rev(step) * BLK, BLK), BLK)
            return pltpu.make_async_copy(stage.at[s], dproj_ref.at[rows, pl.ds(col0, in_w - col0)], sem.at[s])

        @pl.when(i >= 2)
        def _():
            out_copy(slot, i - 2).wait()

        @pl.when(i == 0)
        def _():
            dstate[...] = jnp.zeros_like(dstate)
            dg_ref[...] = jnp.zeros_like(dg_ref)
            db_ref[...] = jnp.zeros_like(db_ref)

        out_ref[:, out_w:out_w + gate_w] = dga_ref[...]
        out_ref[:, out_w + gate_w:out_w + 2 * gate_w] = dgb_ref[...]
        for h in range(H):
            vs = slice(h * dv, (h + 1) * dv)
            _, _, _, gr = head(h)
            gr = gr.astype(F32)
            sg = _sigmoid(gr)
            gain = g_ref[:, vs]
            yh, rstd, y = _gn_fwd(opre_ref[:, vs], gain, b_ref[:, vs])
            d_or_v = dor_ref[:, vs]
            dy = d_or_v * (gr * sg)
            out_ref[:, 2 * qk_w + v_w + h * dv:2 * qk_w + v_w + (h + 1) * dv] = (
                d_or_v * y * (sg * (1.0 + gr * (1.0 - sg)))).astype(out_ref.dtype)
            dg_ref[:, vs] += jnp.sum(dy * yh, axis=0, keepdims=True)
            db_ref[:, vs] += jnp.sum(dy, axis=0, keepdims=True)
            dyh = dy * gain
            do = rstd * (dyh - jnp.mean(dyh, axis=-1, keepdims=True)
                         - yh * jnp.mean(dyh * yh, axis=-1, keepdims=True))
            do_buf[h] = do.astype(CDT)
            dox_buf[h] = (do * xi_ref[h][:, 0:1]).astype(CDT)
        for h in range(H):
            q, k, v, _ = head(h)
            dox = dox_buf[h]
            a_buf[h] = _dot_nt(q, k)
            g_buf[h] = _dot_nt(do_buf[h], v)
            dsn = dstate[h]
            dsn_c = dsn.astype(CDT)
            kz = (k.astype(F32) * zeta_ref[h][:, 0:1]).astype(CDT)
            dq_buf[h] = _dot_nt(dox, st_ref[h])
            dk_buf[h] = _dot_nt(v, dsn_c)
            dv_buf[h] = _dot_nn(kz, dsn_c)
            dstate[h] = dsn * gc_ref[h][0:1, 0:1] + _dot_tn(q, dox)
        for h in range(H):
            q, k, _, _ = head(h)
            decay = dec_ref[h]
            a_c = (a_buf[h] * kscale * decay).astype(CDT)
            g_c = (g_buf[h] * decay).astype(CDT)
            dq = _dot_nn(g_c, k) * kscale + dq_buf[h]
            dkk = _dot_tn(g_c, q) * kscale + dk_buf[h] * zeta_ref[h][:, 0:1]
            dvv = _dot_tn(a_c, do_buf[h]) + dv_buf[h]
            out_ref[:, h * dk:(h + 1) * dk] = dq.astype(out_ref.dtype)
            out_ref[:, qk_w + h * dk:qk_w + (h + 1) * dk] = dkk.astype(out_ref.dtype)
            out_ref[:, 2 * qk_w + h * dv:2 * qk_w + (h + 1) * dv] = dvv.astype(out_ref.dtype)

        cp = out_copy(slot, i)
        cp.start()

        @pl.when(i == N - 1)
        def _():
            cp.wait()
            if N >= 2:
                out_copy(1 - slot, i - 1).wait()

    row = pl.BlockSpec((1, v_w), lambda i: (0, 0))
    tile = pl.BlockSpec((BLK, v_w), lambda i: (rev(i), 0))
    gate = pl.BlockSpec((BLK, gate_w), lambda i: (rev(i), 0))
    return pl.pallas_call(
        body, name="ret_bwd", grid=(N,),
        in_specs=_ret_specs(dk, dv, rev) + [row, row, tile,
                 pl.BlockSpec((None, H, dk, dv), lambda i: (rev(i), 0, 0, 0)), tile, gate, gate],
        out_specs=[pl.BlockSpec(memory_space=pl.ANY), row, row],
        out_shape=[_sds((S, in_w), CDT), _sds((1, v_w), F32), _sds((1, v_w), F32)],
        scratch_shapes=[pltpu.VMEM((H, dk, dv), F32), pltpu.VMEM((2, BLK, in_w - col0), CDT),
                        pltpu.SemaphoreType.DMA((2,)),
                        pltpu.VMEM((H, BLK, dv), CDT), pltpu.VMEM((H, BLK, dv), CDT),
                        pltpu.VMEM((H, BLK, BLK), F32), pltpu.VMEM((H, BLK, BLK), F32),
                        pltpu.VMEM((H, BLK, dk), F32), pltpu.VMEM((H, BLK, dk), F32), pltpu.VMEM((H, BLK, dv), F32)],
        compiler_params=_params(1),
    )(*[proj] * (3 * H), *_ret_tables(dk), gn_g, gn_b, o_pre, states, d_or, dga, dgb)


def _merge_fwd(o_a, o_r, wa, wb, proj, d_model, tm=1024, tn=512):
    S, in_w = proj.shape
    off_a, off_b = in_w - 2 * d_model, in_w - d_model
    assert off_a % tn == 0 and off_b % tn == 0

    def body(oa_ref, or_ref, wa_ref, wb_ref, ga_ref, gb_ref, y_ref, pa_ref, pb_ref):
        pa = _dot_nn(oa_ref[...], wa_ref[...])
        pb = _dot_nn(or_ref[...], wb_ref[...])
        y = _sigmoid(ga_ref[...].astype(F32)) * pa + _sigmoid(gb_ref[...].astype(F32)) * pb
        y_ref[...] = y.astype(y_ref.dtype)
        pa_ref[...] = pa.astype(pa_ref.dtype)
        pb_ref[...] = pb.astype(pb_ref.dtype)

    ka, kb = o_a.shape[1], o_r.shape[1]
    out = pl.BlockSpec((tm, tn), lambda i, j: (i, j))
    return pl.pallas_call(
        body, name="merge_fwd", grid=(S // tm, d_model // tn),
        in_specs=[pl.BlockSpec((tm, ka), lambda i, j: (i, 0)), pl.BlockSpec((tm, kb), lambda i, j: (i, 0)),
                  pl.BlockSpec((ka, tn), lambda i, j: (0, j)), pl.BlockSpec((kb, tn), lambda i, j: (0, j)),
                  pl.BlockSpec((tm, tn), lambda i, j: (i, off_a // tn + j)),
                  pl.BlockSpec((tm, tn), lambda i, j: (i, off_b // tn + j))],
        out_specs=[out, out, out], out_shape=[_sds((S, d_model), CDT)] * 3,
        compiler_params=_params(2))(o_a, o_r, wa, wb, proj, proj)


def _d_x(dproj, w_in, x, g, dx1, dep, tm=512, row_groups=2):
    S, D = x.shape
    n_sh, _, ns = w_in.shape
    nt = S // tm // row_groups
    deps = [d for d in (dep,) if d is not None]

    def body(a_ref, b_ref, x_ref, g_ref, r_ref, *rest):
        dx_ref, dg_ref, acc = rest[len(deps):]
        h, k, i = pl.program_id(0), pl.program_id(1), pl.program_id(2)

        @pl.when(k == 0)
        def _():
            acc[i] = jnp.zeros((tm, D), F32)

        acc[i] += _dot_nt(a_ref[...], b_ref[...])

        @pl.when(k == n_sh - 1)
        def _():
            _ep_rms_bwd(acc[i], (x_ref, g_ref, r_ref), (dx_ref, dg_ref), h * nt + i)

    def last_only(h, k, i):
        return (h * nt + jnp.where(k == n_sh - 1, i, 0), 0)

    return pl.pallas_call(
        body, name="d_x", grid=(row_groups, n_sh, nt),
        in_specs=[pl.BlockSpec((tm, ns), lambda h, k, i: (h * nt + i, k)),
                  pl.BlockSpec((None, D, ns), lambda h, k, i: (k, 0, 0)),
                  pl.BlockSpec((tm, D), last_only), pl.BlockSpec((1, D), lambda h, k, i: (0, 0)),
                  pl.BlockSpec((tm, D), last_only)] + [pl.BlockSpec(memory_space=pl.ANY)] * len(deps),
        out_specs=[pl.BlockSpec((tm, D), last_only), pl.BlockSpec((1, D), lambda h, k, i: (0, 0))],
        out_shape=[_sds((S, D), F32), _sds((1, D), F32)],
        scratch_shapes=[pltpu.VMEM((nt, tm, D), F32)],
        compiler_params=_params(3))(dproj, w_in, x, g, dx1, *deps)


def _local_step(x, target, w_in_mine, chip, others, near_w_in, far_w_in, small, late_weights, on_grads, deps0=()):
    S, D = x.shape
    ns_in = w_in_mine.shape[1]
    in_w = N_CHIPS * ns_in
    d_ff = 4 * D
    ret_v_w = 2 * D
    dv = ret_v_w // RET_HEADS
    dk = (in_w - 3 * ATT_W - 2 * ret_v_w - 2 * D) // (2 * RET_HEADS)
    gqk = jnp.concatenate([small["q_norm_g"].reshape(1, ATT_W), small["k_norm_g"].reshape(1, ATT_W)], axis=1)
    g1, g2 = small["norm1_g"], small["norm2_g"]
    gn_g, gn_b = small["ret_gn_g"], small["ret_gn_b"]

    proj_sds = _sds((S, in_w), CDT)
    xn, proj = _in_proj_mine(x, g1, w_in_mine, chip, proj_sds, deps0)
    for stage, (get_w_in, chips) in enumerate(((near_w_in, others[:2]), (far_w_in, others[2:]))):
        w_in, started = get_w_in(proj)
        (proj,) = _matmul(
            "in_proj_far%d" % stage, "nn", xn, w_in, tm=512, tn=ns_in, tk=D, prefetch=[chips],
            n_cols=chips.shape[0] * ns_in, b_spec=pl.BlockSpec((None, D, ns_in), lambda i, j, k, o: (o[j], 0, 0)),
            outs=[(proj_sds, pl.BlockSpec((512, ns_in), lambda i, j, k, o: (i, o[j])))], epilogue=_ep_store,
            deps=[proj, started], alias_dep_to_out=(0, 0), j_outer=True)
    qkv = _qknorm_fwd(proj, gqk)
    att = [_att_fwd(g, S, qkv[g]) for g in range(3)]
    os_, ls_ = [a[0] for a in att], [a[1] for a in att]
    o_a = _mix_fwd(S, os_, ls_)
    o_pre, o_r, states = _ret_fwd(proj, gn_g, gn_b, dk, dv)
    w = late_weights(o_r)
    y, pa, pb = _merge_fwd(o_a, o_r, w["w_proj_a"], w["w_proj_b"], proj, D)
    x1, xn2 = _matmul("out_proj", "nn", y, w["w_out"], tm=1024, tn=D, tk=D,
                      extras=[(x, _mn(1024, D)), (g2, _row(D))],
                      outs=[(_sds((S, D), F32), _mn(1024, D)), (_sds((S, D), CDT), _mn(1024, D))],
                      epilogue=_ep_resid_norm)
    hid, act = _matmul("mlp_up", "nn", xn2, w["w_up"], tm=512, tn=2048, tk=D, j_outer=True,
                       outs=[(_sds((S, d_ff), CDT), _mn(512, 2048))] * 2, epilogue=_ep_up)
    dx2, dx2c, loss_row = _matmul(
        "mlp_down_loss", "nn", act, w["w_down"], tm=512, tn=D, tk=d_ff,
        extras=[(x1, _mn(512, D)), (target, _mn(512, D))],
        outs=[(_sds((S, D), F32), _mn(512, D)), (_sds((S, D), CDT), _mn(512, D)), (_sds((1, D), F32), _row(D))],
        epilogue=functools.partial(_ep_down_loss, inv_d=1.0 / D))
    loss = 0.5 * jnp.sum(loss_row) / D

    (dh,) = _matmul("d_hidden", "nt", dx2c, w["w_down"], tm=512, tn=2048, tk=D, j_outer=True,
                    extras=[(hid, _mn(512, 2048))], outs=[(_sds((S, d_ff), CDT), _mn(512, 2048))], epilogue=_ep_dh)
    (gw_down,) = _matmul("dw_down", "tn", act, dx2c, tm=1024, tn=D, tk=1024,
                         outs=[(_sds((d_ff, D), F32), _mn(1024, D))], epilogue=_ep_store)
    (gw_up,) = _matmul("dw_up", "tn", xn2, dh, tm=D, tn=1024, tk=1024,
                       outs=[(_sds((D, d_ff), F32), _mn(D, 1024))], epilogue=_ep_store)
    tok = on_grads({"w_down": gw_down, "w_up": gw_up})
    dx1, dx1c, dg2 = _matmul(
        "d_x1", "nt", dh, w["w_up"], tm=512, tn=D, tk=d_ff,
        extras=[(x1, _mn(512, D)), (g2, _row(D)), (dx2, _mn(512, D))],
        outs=[(_sds((S, D), F32), _mn(512, D)), (_sds((S, D), CDT), _mn(512, D)), (_sds((1, D), F32), _row(D))],
        epilogue=_ep_rms_bwd, deps=[tok])

    gt = 512
    assert (in_w - 2 * D) % gt == 0
    off_a, off_b = (in_w - 2 * D) // gt, (in_w - D) // gt
    dpa, dpb, dga, dgb = _matmul(
        "d_gates", "nt", dx1c, w["w_out"], tm=1024, tn=gt, tk=D,
        extras=[(proj, _mn(1024, gt, off_a)), (proj, _mn(1024, gt, off_b)), (pa, _mn(1024, gt)),
                (pb, _mn(1024, gt))],
        outs=[(_sds((S, D), CDT), _mn(1024, gt))] * 4, epilogue=_ep_gates)
    (gw_out,) = _matmul("dw_out", "tn", y, dx1c, tm=D, tn=D, tk=1024,
                        outs=[(_sds((D, D), F32), _mn(D, D))], epilogue=_ep_store)
    (gw_pa,) = _matmul("dw_proj_a", "tn", o_a, dpa, tm=GW, tn=D, tk=1024,
                       outs=[(_sds((GW, D), F32), _mn(GW, D))], epilogue=_ep_store)
    (gw_pb,) = _matmul("dw_proj_b", "tn", o_r, dpb, tm=1024, tn=D, tk=1024,
                       outs=[(_sds((ret_v_w, D), F32), _mn(1024, D))], epilogue=_ep_store)
    (do_a,) = _matmul("d_o_a", "nt", dpa, w["w_proj_a"], tm=1024, tn=GW, tk=D,
                      outs=[(_sds((S, GW), F32), _mn(1024, GW))], epilogue=_ep_store)
    tok = on_grads({"w_out": gw_out, "w_proj_a": gw_pa, "w_proj_b": gw_pb})
    (d_or,) = _matmul("d_o_r", "nt", dpb, w["w_proj_b"], tm=512, tn=ret_v_w, tk=D,
                      outs=[(_sds((S, ret_v_w), F32), _mn(512, ret_v_w))], epilogue=_ep_store, deps=[tok])

    dproj, dgn_g, dgn_b = _ret_bwd(proj, gn_g, gn_b, o_pre, states, d_or, dga, dgb, dk, dv)
    do_gs, c_gs = _mix_bwd(S, os_, ls_, do_a)
    datt_parts = [_att_bwd(g, S, qkv[g], ls_[g], do_gs[g], c_gs[g]) for g in range(3)]
    dproj, dgqk = _qknorm_bwd(proj, gqk, [p[0] for p in datt_parts], [p[1] for p in datt_parts],
                              [p[2] for p in datt_parts], dproj)

    (gw_in,) = _matmul(
        "dw_in", "tn", xn, dproj, tm=512, tn=ns_in, tk=1024,
        outs=[(_sds((N_CHIPS, D, ns_in), F32), pl.BlockSpec((None, 512, ns_in), lambda i, j, k: (j, i, 0)))],
        epilogue=_ep_store)
    tok = on_grads({"w_in": gw_in})
    grad_x, dg1 = _d_x(dproj, w_in, x, g1, dx1, tok)

    smallg = {"norm1_g": dg1, "q_norm_g": dgqk[:, :ATT_W], "k_norm_g": dgqk[:, ATT_W:],
              "ret_gn_g": dgn_g, "ret_gn_b": dgn_b, "norm2_g": dg2}
    return loss, grad_x, smallg


N_CHIPS = 4
N_DEV = 8


def _place():
    x, y, c = lax.axis_index("x"), lax.axis_index("y"), lax.axis_index("c")
    return x, y, c


def _other_chips(x, y):
    out = []
    for fx, fy in ((1, 0), (0, 1), (1, 1)):
        px = 1 - x if fx else x
        py = 1 - y if fy else y
        out.append(((px, py), 2 * px + py))
    return out


SEM_SPEC = pl.BlockSpec(memory_space=pltpu.SEMAPHORE)
ANY_SPEC = pl.BlockSpec(memory_space=pl.ANY)
EFFECT = pltpu.SideEffectType.DATAFLOW_SIDE_EFFECTING


def _ici_copies(kind, srcs, lands, send, recv, which=(0, 1, 2)):
    x, y, c = _place()
    me = 2 * x + y
    out = []
    for w, (s, l) in enumerate(zip(srcs, lands)):
        for j, ((px, py), pidx) in enumerate(_other_chips(x, y)):
            if j not in which:
                continue
            if kind == "gather":
                half = s.shape[0] // 2
                rows = pl.ds(c * half, half)
                src, dst_there, dst_here = s.at[rows, :], l.at[me, rows, :], l.at[pidx, rows, :]
            else:
                src, dst_there, dst_here = s.at[pidx], l.at[me], l.at[pidx]
            out.append((src, dst_there, dst_here, send.at[3 * w + j], recv.at[3 * w + j], (px, py, c)))
    return out


def _exchange_start(name, kind, srcs, land_shapes, which=(0, 1, 2), lands=None):
    n = len(srcs)
    if lands is None:
        lands = [lax.empty(shape, dtype) for shape, dtype in land_shapes]

    def body(*refs):
        src_refs, land_refs = refs[:n], refs[n:2 * n]
        send, recv = refs[2 * n], refs[2 * n + 1]
        token = refs[-1]
        for src, dst, _, ss, rs, dev in _ici_copies(kind, src_refs, land_refs, send, recv, which):
            pltpu.make_async_remote_copy(src_ref=src, dst_ref=dst, send_sem=ss, recv_sem=rs, device_id=dev,
                                         device_id_type=MESH).start()
        token[...] = jnp.zeros_like(token)

    thru = [pltpu.HBM(s.shape, s.dtype) for s in srcs] + [pltpu.HBM(shape, dtype) for shape, dtype in land_shapes]
    res = pl.pallas_call(
        body, name=name,
        out_shape=(pltpu.SemaphoreType.DMA((3 * n,)), pltpu.SemaphoreType.DMA((3 * n,)), *thru, _sds((8, LANES), F32)),
        in_specs=[HBM_SPEC] * (2 * n), out_specs=(SEM_SPEC, SEM_SPEC, *[HBM_SPEC] * (2 * n), VMEM_SPEC),
        input_output_aliases={i: 2 + i for i in range(2 * n)},
        compiler_params=pltpu.CompilerParams(has_side_effects=EFFECT),
    )(*[pltpu.with_memory_space_constraint(s, pltpu.HBM) for s in srcs],
      *[pltpu.with_memory_space_constraint(l, pltpu.HBM) for l in lands])
    return res[0], res[1], list(res[2:2 + n]), list(res[2 + n:2 + 2 * n]), res[-1]


def _exchange_wait(name, kind, send, recv, srcs, lands, after, which=(0, 1, 2)):
    n = len(srcs)

    def body(*refs):
        src_refs, land_refs = refs[:n], refs[n:2 * n]
        send_ref, recv_ref = refs[2 * n], refs[2 * n + 1]
        for src, _, dst, ss, rs, dev in _ici_copies(kind, src_refs, land_refs, send_ref, recv_ref, which):
            cp = pltpu.make_async_remote_copy(src_ref=src, dst_ref=dst, send_sem=ss, recv_sem=rs, device_id=dev,
                                              device_id_type=MESH)
            cp.wait_send()
            cp.wait_recv()

    thru = [pltpu.HBM(t.shape, t.dtype) for t in list(srcs) + list(lands)]
    res = pl.pallas_call(
        body, name=name, out_shape=thru,
        in_specs=[HBM_SPEC] * (2 * n) + [SEM_SPEC, SEM_SPEC, ANY_SPEC], out_specs=[HBM_SPEC] * (2 * n),
        input_output_aliases={i: i for i in range(2 * n)},
        compiler_params=pltpu.CompilerParams(has_side_effects=EFFECT),
    )(*srcs, *lands, send, recv, after)
    return list(res[:n]), list(res[n:])


PAIR_TILE_ELEMS = 1 << 20


def _pair_fill(name, gathered, mine, core, others, chip, write_mine=True):
    k, r, C = gathered.shape
    half = r // 2
    tr = _row_tile(half, C, PAIR_TILE_ELEMS, mult=16)
    nt = half // tr
    n_far = others.shape[0]

    def body(c_ref, o_ref, chip_ref, in_ref, mine_ref, out_ref, slot, send, recv):
        j = pl.program_id(0)
        _sibling_barrier((j == 0) & (pl.program_id(1) == 0))
        b = (j * nt + pl.program_id(1)) % 2
        x, y, c = _place()
        cp = pltpu.make_async_remote_copy(src_ref=in_ref, dst_ref=slot.at[b], send_sem=send.at[b],
                                          recv_sem=recv.at[b], device_id=(x, y, 1 - c), device_id_type=MESH)

        @pl.when(j < n_far)
        def _():
            cp.start()
            cp.wait_recv()
            out_ref[...] = slot[b]
            cp.wait_send()

        @pl.when(j >= n_far)
        def _():
            out_ref[...] = mine_ref[...]

    def far(j):
        return jnp.minimum(j, n_far - 1)

    grid_spec = pltpu.PrefetchScalarGridSpec(
        num_scalar_prefetch=3, grid=(n_far + (2 if write_mine else 0), nt),
        in_specs=[pl.BlockSpec((tr, C), lambda j, i, c, o, m: (
                      (2 * o[far(j)] + c[0]) * nt + jnp.where(j < n_far, i, nt - 1), 0)),
                  pl.BlockSpec((tr, C), lambda j, i, c, o, m: (jnp.where(j < n_far, 0, (j - n_far) * nt + i), 0))],
        out_specs=pl.BlockSpec((tr, C), lambda j, i, c, o, m: (
            jnp.where(j < n_far, 2 * o[far(j)] + 1 - c[0], 2 * m[0] + j - n_far) * nt + i, 0)),
        scratch_shapes=[pltpu.VMEM((2, tr, C), gathered.dtype), pltpu.SemaphoreType.DMA((2,)),
                        pltpu.SemaphoreType.DMA((2,))])
    out = pl.pallas_call(body, name=name, grid_spec=grid_spec, out_shape=_sds((k * r, C), gathered.dtype),
                         input_output_aliases={3: 0}, compiler_params=_params(2, PAIR_FILL_ID))(
                             core, others, chip, gathered.reshape(k * r, C), mine)
    return out.reshape(k, r, C)


def _pair_reduce(name, g, core):
    k, R, C = g.shape
    half = R // 2
    tr = _row_tile(half, C, PAIR_TILE_ELEMS, mult=16)
    nt = half // tr

    def body(c_ref, mine_ref, give_ref, out_ref, wire_ref, stage, slot, send, recv):
        _sibling_barrier((pl.program_id(0) == 0) & (pl.program_id(1) == 0))
        b = (pl.program_id(0) * nt + pl.program_id(1)) % 2
        x, y, c = _place()
        stage[b] = give_ref[...].astype(stage.dtype)
        cp = pltpu.make_async_remote_copy(src_ref=stage.at[b], dst_ref=slot.at[b], send_sem=send.at[b],
                                          recv_sem=recv.at[b], device_id=(x, y, 1 - c), device_id_type=MESH)
        cp.start()
        cp.wait_recv()
        tot = mine_ref[...] + slot[b].astype(F32)
        out_ref[...] = tot
        wire_ref[...] = tot.astype(wire_ref.dtype)
        cp.wait_send()

    blk = (tr, C)
    out_spec = pl.BlockSpec(blk, lambda s, i, c: (s * nt + i, 0))
    grid_spec = pltpu.PrefetchScalarGridSpec(
        num_scalar_prefetch=1, grid=(k, nt),
        in_specs=[pl.BlockSpec(blk, lambda s, i, c: ((2 * s + c[0]) * nt + i, 0)),
                  pl.BlockSpec(blk, lambda s, i, c: ((2 * s + 1 - c[0]) * nt + i, 0))],
        out_specs=[out_spec, out_spec],
        scratch_shapes=[pltpu.VMEM((2, tr, C), CDT), pltpu.VMEM((2, tr, C), CDT), pltpu.SemaphoreType.DMA((2,)),
                        pltpu.SemaphoreType.DMA((2,))])
    g2 = g.reshape(k * R, C)
    out, wire = pl.pallas_call(body, name=name, grid_spec=grid_spec,
                               out_shape=[_sds((k * half, C), F32), _sds((k * half, C), CDT)],
                               compiler_params=_params(2, PAIR_REDUCE_ID))(core, g2, g2)
    return out, wire.reshape(k, half, C)


def _all_reduce_small(v):
    r, cdim = v.shape

    def body(v_ref, o_ref, buf, send, recv):
        x, y, c = _place()
        me = 4 * x + 2 * y + c
        buf[me] = v_ref[...]
        sends = []
        for m in range(1, N_DEV):
            px = 1 - x if m & 4 else x
            py = 1 - y if m & 2 else y
            pc = 1 - c if m & 1 else c
            cp = pltpu.make_async_remote_copy(src_ref=v_ref, dst_ref=buf.at[me], send_sem=send.at[m - 1],
                                              recv_sem=recv.at[m - 1], device_id=(px, py, pc), device_id_type=MESH)
            cp.start()
            sends.append((cp, 4 * px + 2 * py + pc))
        for m, (cp, pidx) in enumerate(sends):
            pltpu.make_async_remote_copy(src_ref=v_ref, dst_ref=buf.at[pidx], send_sem=send.at[m], recv_sem=recv.at[m],
                                         device_id=(x, y, c), device_id_type=MESH).wait_recv()
        for cp, _ in sends:
            cp.wait_send()
        tot = buf[0]
        for k in range(1, N_DEV):
            tot = tot + buf[k]
        o_ref[...] = tot

    return pl.pallas_call(
        body, name="all_reduce_small", in_specs=[VMEM_SPEC], out_specs=VMEM_SPEC,
        out_shape=_sds((r, cdim), F32),
        scratch_shapes=[pltpu.VMEM((N_DEV, r, cdim), F32), pltpu.SemaphoreType.DMA((N_DEV - 1,)),
                        pltpu.SemaphoreType.DMA((N_DEV - 1,))],
    )(v)


def _row_tile(rows, cols, budget_elems=1 << 18, mult=8):
    if rows % mult:
        return rows
    t = max(mult, (budget_elems // cols) // mult * mult)
    while rows % t:
        t -= mult
    return t


def _adamw_update(w, g, m, v):
    nm = ADAM_B1 * m + (1.0 - ADAM_B1) * g
    nv = ADAM_B2 * v + (1.0 - ADAM_B2) * (g * g)
    m_hat = nm / (1.0 - ADAM_B1 ** ADAM_STEP)
    v_hat = nv / (1.0 - ADAM_B2 ** ADAM_STEP)
    return -ADAM_LR * (m_hat / (jnp.sqrt(v_hat) + ADAM_EPS) + ADAM_WD * w), nm, nv


def _adamw(name, w, g, m, v):
    R, C = w.shape
    tr = _row_tile(R, C, 1 << 18)

    def body(w_ref, g_ref, m_ref, v_ref, d_ref, nm_ref, nv_ref):
        d_ref[...], nm_ref[...], nv_ref[...] = _adamw_update(w_ref[...], g_ref[...], m_ref[...], v_ref[...])

    spec = pl.BlockSpec((tr, C), lambda i: (i, 0))
    return pl.pallas_call(body, name=name, grid=(R // tr,), in_specs=[spec] * 4, out_specs=[spec] * 3,
                          out_shape=[_sds((R, C), F32)] * 3, compiler_params=_params(1))(w, g, m, v)


def _sum_share(name, own, by_chip, chip, others, core):
    k, half, C = by_chip.shape
    tr = _row_tile(half, C, PAIR_TILE_ELEMS // 2, mult=16)
    nt = half // tr

    def body(chip_ref, oth_ref, c_ref, own_ref, a_ref, b_ref, cc_ref, g_out, mine, slot, send, recv):
        p = pl.program_id(1)
        _sibling_barrier((pl.program_id(0) == 0) & (p == 0))
        b = pl.program_id(0) % 2
        x, y, c = _place()
        cp = pltpu.make_async_remote_copy(src_ref=mine.at[b], dst_ref=slot.at[b], send_sem=send.at[b],
                                          recv_sem=recv.at[b], device_id=(x, y, 1 - c), device_id_type=MESH)

        @pl.when(p == 0)
        def _():
            tot = ((own_ref[...] + a_ref[...].astype(F32)) + b_ref[...].astype(F32)) + cc_ref[...].astype(F32)
            mine[b] = tot
            cp.start()
            g_out[...] = tot

        @pl.when(p == 1)
        def _():
            cp.wait_recv()
            g_out[...] = slot[b]
            cp.wait_send()

    def piece(j):
        return pl.BlockSpec((tr, C), lambda i, p, chip, oth, c: (oth[j] * nt + i, 0))

    grid_spec = pltpu.PrefetchScalarGridSpec(
        num_scalar_prefetch=3, grid=(nt, 2),
        in_specs=[pl.BlockSpec((tr, C), lambda i, p, chip, oth, c: (chip[0] * nt + i, 0)),
                  piece(0), piece(1), piece(2)],
        out_specs=pl.BlockSpec((tr, C), lambda i, p, chip, oth, c: (
            jnp.where(p == 0, c[0], 1 - c[0]) * nt + i, 0)),
        scratch_shapes=[pltpu.VMEM((2, tr, C), F32), pltpu.VMEM((2, tr, C), F32), pltpu.SemaphoreType.DMA((2,)),
                        pltpu.SemaphoreType.DMA((2,))])
    by2 = by_chip.reshape(k * half, C)
    return pl.pallas_call(body, name=name, grid_spec=grid_spec, out_shape=_sds((2 * half, C), F32),
                          compiler_params=_params(2, SUM_SHARE_ID))(chip, others, core, own, by2, by2, by2)


BIG = ("w_in", "w_proj_a", "w_proj_b", "w_out", "w_up", "w_down")
COL_SHARDED = ("w_in", "w_proj_a", "w_up")
SMALL = ("norm1_g", "q_norm_g", "k_norm_g", "ret_gn_g", "ret_gn_b", "norm2_g")
ALL_W = ("norm1_g", "w_in", "q_norm_g", "k_norm_g", "ret_gn_g", "ret_gn_b", "w_proj_a", "w_proj_b", "w_out",
         "norm2_g", "w_up", "w_down")
LANES = 128


def _to_full(name, gathered):
    k, r, c = gathered.shape
    if name in COL_SHARDED:
        return gathered.transpose(1, 0, 2).reshape(r, k * c)
    return gathered.reshape(k * r, c)


def _to_shard_major(name, full):
    if name in COL_SHARDED:
        r, c4 = full.shape
        return full.reshape(r, N_CHIPS, c4 // N_CHIPS).transpose(1, 0, 2)
    r4, c = full.shape
    return full.reshape(N_CHIPS, r4 // N_CHIPS, c)


def kernel(x, norm1_g, w_in, q_norm_g, k_norm_g, ret_gn_g, ret_gn_b, w_proj_a, w_proj_b, w_out, norm2_g, w_up, w_down, loss_target, m_norm1_g, m_w_in, m_q_norm_g, m_k_norm_g, m_ret_gn_g, m_ret_gn_b, m_w_proj_a, m_w_proj_b, m_w_out, m_norm2_g, m_w_up, m_w_down, v_norm1_g, v_w_in, v_q_norm_g, v_k_norm_g, v_ret_gn_g, v_ret_gn_b, v_w_proj_a, v_w_proj_b, v_w_out, v_norm2_g, v_w_up, v_w_down):
    weights = dict(norm1_g=norm1_g, w_in=w_in, q_norm_g=q_norm_g, k_norm_g=k_norm_g, ret_gn_g=ret_gn_g,
                   ret_gn_b=ret_gn_b, w_proj_a=w_proj_a, w_proj_b=w_proj_b, w_out=w_out, norm2_g=norm2_g,
                   w_up=w_up, w_down=w_down)
    moments_m = dict(norm1_g=m_norm1_g, w_in=m_w_in, q_norm_g=m_q_norm_g, k_norm_g=m_k_norm_g, ret_gn_g=m_ret_gn_g,
                     ret_gn_b=m_ret_gn_b, w_proj_a=m_w_proj_a, w_proj_b=m_w_proj_b, w_out=m_w_out,
                     norm2_g=m_norm2_g, w_up=m_w_up, w_down=m_w_down)
    moments_v = dict(norm1_g=v_norm1_g, w_in=v_w_in, q_norm_g=v_q_norm_g, k_norm_g=v_k_norm_g, ret_gn_g=v_ret_gn_g,
                     ret_gn_b=v_ret_gn_b, w_proj_a=v_w_proj_a, w_proj_b=v_w_proj_b, w_out=v_w_out,
                     norm2_g=v_norm2_g, w_up=v_w_up, w_down=v_w_down)

    mx, my = lax.axis_index("x"), lax.axis_index("y")
    core = lax.axis_index("c").astype(jnp.int32).reshape(1)
    chip = (2 * mx + my).astype(jnp.int32).reshape(1)
    others = jnp.stack([2 * (1 - mx) + my, 2 * mx + 1 - my, 2 * (1 - mx) + 1 - my]).astype(jnp.int32)
    shards = {n: weights[n][0].astype(CDT) for n in BIG}
    def start_gather(name, names):
        return _exchange_start(name, "gather", [shards[n] for n in names],
                               [((N_CHIPS,) + shards[n].shape, CDT) for n in names])

    w_in_shape = [((N_CHIPS,) + shards["w_in"].shape, CDT)]
    n_send, n_recv, n_srcs, n_lands, n_token = _exchange_start(
        "gather_w_in_near_start", "gather", [shards["w_in"]], w_in_shape, which=(0, 1))
    late = [n for n in BIG if n != "w_in"]
    flight = {}

    def near_w_in(after):
        srcs, lands = _exchange_wait("gather_w_in_near_wait", "gather", n_send, n_recv, n_srcs, n_lands, after,
                                     which=(0, 1))
        d_send, d_recv, d_srcs, d_lands, _ = _exchange_start(
            "gather_w_in_diag_start", "gather", srcs, w_in_shape, which=(2,), lands=lands)
        flight["late"] = start_gather("gather_late_start", late)
        w_near = _pair_fill("pair_fill_w_in_near", d_lands[0], d_srcs[0], core, others[:2], chip)
        flight["diag"] = (d_send, d_recv, d_srcs, [w_near])
        return w_near, flight["late"][-1]

    def far_w_in(after):
        d_send, d_recv, d_srcs, d_lands = flight["diag"]
        srcs, lands = _exchange_wait("gather_w_in_diag_wait", "gather", d_send, d_recv, d_srcs, d_lands, after,
                                     which=(2,))
        return _pair_fill("pair_fill_w_in_diag", lands[0], srcs[0], core, others[2:], chip, write_mine=False), None

    def late_weights(after):
        l_send, l_recv, l_srcs, l_lands, _ = flight["late"]
        srcs, lands = _exchange_wait("gather_late_wait", "gather", l_send, l_recv, l_srcs, l_lands, after)
        out = {}
        for n, mine, land in zip(late, srcs, lands):
            out[n] = _to_full(n, _pair_fill("pair_fill_%s" % n, land, mine, core, others, chip))
        return out

    pending = []

    def on_grads(group):
        names = list(group)
        red = [_pair_reduce("pair_reduce_%s" % n, g if g.ndim == 3 else _to_shard_major(n, g), core)
               for n, g in group.items()]
        wires = [wire for _, wire in red]
        send, recv, srcs, lands, token = _exchange_start(
            "scatter_start_%s" % names[0], "scatter", wires, [(wire.shape, wire.dtype) for wire in wires])
        pending.append((names, [own for own, _ in red], send, recv, srcs, lands))
        return token

    small = {n: weights[n].reshape(1, -1) for n in SMALL}

    loss, grad_x, small_g = _local_step(x[0], loss_target[0], n_srcs[0], chip, others, near_w_in, far_w_in, small,
                                        late_weights, on_grads, deps0=[n_token])
    loss = lax.psum(loss, ("x", "y", "c"))

    out_g, out_d, out_m, out_v = {}, {}, {}, {}
    for names, owns, send, recv, srcs, lands in pending:
        _, got = _exchange_wait("scatter_wait_%s" % names[0], "scatter", send, recv, srcs, lands, grad_x)
        for n, own, by_chip in zip(names, owns, got):
            shape = weights[n].shape
            g2 = _sum_share("sum_share_%s" % n, own, by_chip, chip, others, core)
            d, nm, nv = _adamw("adamw_%s" % n, weights[n][0], g2, moments_m[n][0], moments_v[n][0])
            out_g[n], out_d[n], out_m[n], out_v[n] = (t.reshape(shape) for t in (g2, d, nm, nv))

    packed = jnp.concatenate([small_g[n].reshape(1, -1) for n in SMALL], axis=1)
    red = _all_reduce_small(packed.reshape(-1, LANES)).reshape(1, -1)
    off = 0
    for n in SMALL:
        shape = weights[n].shape
        row = (1, weights[n].size)
        g2 = red[:, off:off + row[1]]
        off += row[1]
        d, nm, nv = _adamw("adamw_%s" % n, weights[n].reshape(row), g2, moments_m[n].reshape(row),
                           moments_v[n].reshape(row))
        out_g[n], out_d[n], out_m[n], out_v[n] = (t.reshape(shape) for t in (g2, d, nm, nv))

    return (loss, grad_x[None], *[out_g[n] for n in ALL_W], *[out_d[n] for n in ALL_W],
            *[out_m[n] for n in ALL_W], *[out_v[n] for n in ALL_W])
```

```python
import functools

import jax
import jax.numpy as jnp
from jax import lax
from jax.experimental import pallas as pl
from jax.experimental.pallas import tpu as pltpu

CDT = jnp.bfloat16
F32 = jnp.float32
EPS = 1e-6

ATT_GROUPS = ((128, 1), (512, 4), (2048, 16))
ATT_HPG = 4
ATT_HEADS = 12
HD = 128
BLK = 128
ATT_W = ATT_HEADS * HD
GW = ATT_HPG * HD
RET_HEADS = 4

ADAM_LR = 0.001
ADAM_B1 = 0.9
ADAM_B2 = 0.999
ADAM_EPS = 1e-08
ADAM_WD = 0.01
ADAM_STEP = 10

VMEM_LIMIT_BYTES = 56 * 1024 * 1024
MESH = pl.DeviceIdType.MESH
HBM_SPEC = pl.BlockSpec(memory_space=pltpu.HBM)
VMEM_SPEC = pl.BlockSpec(memory_space=pltpu.VMEM)


def _params(n_axes, collective_id=None):
    return pltpu.CompilerParams(dimension_semantics=("arbitrary",) * n_axes,
                                vmem_limit_bytes=VMEM_LIMIT_BYTES, collective_id=collective_id)


PAIR_FILL_ID, PAIR_REDUCE_ID, SUM_SHARE_ID = 1, 2, 3


def _sibling_barrier(first_step):
    @pl.when(first_step)
    def _():
        sem = pltpu.get_barrier_semaphore()
        x, y, c = lax.axis_index("x"), lax.axis_index("y"), lax.axis_index("c")
        pl.semaphore_signal(sem, inc=1, device_id=(x, y, 1 - c), device_id_type=pl.DeviceIdType.MESH)
        pl.semaphore_wait(sem, 1)


def _dot_nn(a, b):
    return jnp.dot(a, b, preferred_element_type=F32)


def _dot_nt(a, b):
    return lax.dot_general(a, b, (((1,), (1,)), ((), ())), preferred_element_type=F32)


def _dot_tn(a, b):
    return lax.dot_general(a, b, (((0,), (0,)), ((), ())), preferred_element_type=F32)


def _sigmoid(v):
    return 1.0 / (1.0 + jnp.exp(-v))


def _matmul(name, mode, a, b, *, tm, tn, tk, extras=(), outs, epilogue, deps=(), b_spec=None, n_cols=None,
            prefetch=(), alias_dep_to_out=None, j_outer=False):
    deps = [d for d in deps if d is not None]
    if mode == "tn":
        K, M = a.shape
    else:
        M, K = a.shape
    if b_spec is None:
        (N, K2) = b.shape if mode == "nt" else b.shape[::-1]
        assert K == K2, (name, a.shape, b.shape)
        if mode == "nt":
            b_spec = pl.BlockSpec((tn, tk), lambda i, j, k, *p: (j, k))
        else:
            b_spec = pl.BlockSpec((tk, tn), lambda i, j, k, *p: (k, j))
    else:
        N = n_cols
    assert M % tm == 0 and N % tn == 0 and K % tk == 0, (name, a.shape, b.shape)
    ni, nj, nk = M // tm, N // tn, K // tk
    if mode == "tn":
        a_spec = pl.BlockSpec((tk, tm), lambda i, j, k, *p: (k, i))
    else:
        a_spec = pl.BlockSpec((tm, tk), lambda i, j, k, *p: (i, k))
    dot = {"nn": _dot_nn, "nt": _dot_nt, "tn": _dot_tn}[mode]
    n_ex, n_out, n_dep, n_pre = len(extras), len(outs), len(deps), len(prefetch)
    grid = (ni, nj, nk)
    if j_outer:
        grid = (nj, ni, nk)

        def swapped(spec):
            return pl.BlockSpec(spec.block_shape, lambda j, i, k, *p: spec.index_map(i, j, k, *p))

        a_spec, b_spec = swapped(a_spec), swapped(b_spec)
        extras = [(e, swapped(s)) for e, s in extras]
        outs = [(o, swapped(s)) for o, s in outs]

    def body(*refs):
        refs = refs[n_pre:]
        a_ref, b_ref = refs[0], refs[1]
        ex = refs[2:2 + n_ex]
        out = refs[2 + n_ex + n_dep:2 + n_ex + n_dep + n_out]
        acc = refs[-1] if nk > 1 else None
        i = pl.program_id(1 if j_outer else 0)
        k = pl.program_id(2)
        if nk == 1:
            epilogue(dot(a_ref[...].astype(CDT), b_ref[...].astype(CDT)), ex, out, i)
            return

        @pl.when(k == 0)
        def _():
            acc[...] = jnp.zeros_like(acc)

        acc[...] += dot(a_ref[...].astype(CDT), b_ref[...].astype(CDT))

        @pl.when(k == nk - 1)
        def _():
            epilogue(acc[...], ex, out, i)

    grid_spec = pltpu.PrefetchScalarGridSpec(
        num_scalar_prefetch=n_pre, grid=grid,
        in_specs=[a_spec, b_spec] + [s for _, s in extras] + [pl.BlockSpec(memory_space=pl.ANY)] * n_dep,
        out_specs=[s for _, s in outs],
        scratch_shapes=[pltpu.VMEM((tm, tn), F32)] if nk > 1 else [])
    aliases = {}
    if alias_dep_to_out is not None:
        aliases = {n_pre + 2 + n_ex + alias_dep_to_out[0]: alias_dep_to_out[1]}
    res = pl.pallas_call(
        body, name=name, grid_spec=grid_spec, out_shape=[o for o, _ in outs], input_output_aliases=aliases,
        compiler_params=_params(3),
    )(*prefetch, a, b, *[e for e, _ in extras], *deps)
    return res


def _mn(tm, tn, col_off=0):
    return pl.BlockSpec((tm, tn), lambda i, j, k, *p: (i, j + col_off))


def _row(tn):
    return pl.BlockSpec((1, tn), lambda i, j, k, *p: (0, j))


def _ep_store(acc, ex, out, i):
    out[0][...] = acc.astype(out[0].dtype)


def _ep_resid_norm(acc, ex, out, i):
    x1 = ex[0][...] + acc
    out[0][...] = x1
    rstd = lax.rsqrt(jnp.mean(x1 * x1, axis=-1, keepdims=True) + EPS)
    out[1][...] = (x1 * rstd * ex[1][...]).astype(out[1].dtype)


def _ep_up(acc, ex, out, i):
    out[0][...] = acc.astype(out[0].dtype)
    r = jnp.maximum(acc, 0.0)
    out[1][...] = (r * r).astype(out[1].dtype)


def _ep_down_loss(acc, ex, out, i, inv_d):
    diff = (ex[0][...] + acc) - ex[1][...]
    dx2 = diff * inv_d
    out[0][...] = dx2
    out[1][...] = dx2.astype(out[1].dtype)

    @pl.when(i == 0)
    def _():
        out[2][...] = jnp.zeros_like(out[2])

    out[2][...] += jnp.sum(diff * diff, axis=0, keepdims=True)


def _ep_dh(acc, ex, out, i):
    h = ex[0][...].astype(F32)
    out[0][...] = (acc * (2.0 * jnp.maximum(h, 0.0))).astype(out[0].dtype)


def _ep_rms_bwd(acc, ex, out, i):
    x = ex[0][...]
    g = ex[1][...]
    rstd = lax.rsqrt(jnp.mean(x * x, axis=-1, keepdims=True) + EPS)
    xh = x * rstd
    dxh = acc * g
    dx = ex[2][...] + rstd * (dxh - xh * jnp.mean(dxh * xh, axis=-1, keepdims=True))
    out[0][...] = dx
    for copy in out[1:-1]:
        copy[...] = dx.astype(copy.dtype)
    dg = out[-1]

    @pl.when(i == 0)
    def _():
        dg[...] = jnp.zeros_like(dg)

    dg[...] += jnp.sum(acc * xh, axis=0, keepdims=True)


def _ep_gates(acc, ex, out, i):
    sa = _sigmoid(ex[0][...].astype(F32))
    sb = _sigmoid(ex[1][...].astype(F32))
    dpa = acc * sa
    dpb = acc * sb
    out[0][...] = dpa.astype(out[0].dtype)
    out[1][...] = dpb.astype(out[1].dtype)
    out[2][...] = (dpa * ex[2][...].astype(F32) * (1.0 - sa)).astype(out[2].dtype)
    out[3][...] = (dpb * ex[3][...].astype(F32) * (1.0 - sb)).astype(out[3].dtype)


def _sds(shape, dtype):
    return jax.ShapeDtypeStruct(shape, dtype)


def _in_proj_mine(x, g, w_mine, chip, proj_sds, deps, tm=512):
    S, D = x.shape
    ns = w_mine.shape[1]
    deps = [d for d in deps if d is not None]

    def body(c_ref, x_ref, g_ref, w_ref, *rest):
        xn_ref, proj_ref = rest[len(deps)], rest[len(deps) + 1]
        xv = x_ref[...]
        rstd = lax.rsqrt(jnp.mean(xv * xv, axis=-1, keepdims=True) + EPS)
        xn = (xv * rstd * g_ref[...]).astype(xn_ref.dtype)
        xn_ref[...] = xn
        proj_ref[...] = _dot_nn(xn, w_ref[...]).astype(proj_ref.dtype)

    grid_spec = pltpu.PrefetchScalarGridSpec(
        num_scalar_prefetch=1, grid=(S // tm,),
        in_specs=[pl.BlockSpec((tm, D), lambda i, c: (i, 0)), pl.BlockSpec((1, D), lambda i, c: (0, 0)),
                  pl.BlockSpec((D, ns), lambda i, c: (0, 0))] + [pl.BlockSpec(memory_space=pl.ANY)] * len(deps),
        out_specs=[pl.BlockSpec((tm, D), lambda i, c: (i, 0)), pl.BlockSpec((tm, ns), lambda i, c: (i, c[0]))])
    return pl.pallas_call(body, name="in_proj_mine", grid_spec=grid_spec, out_shape=[_sds((S, D), CDT), proj_sds],
                          compiler_params=_params(1))(chip, x, g, w_mine, *deps)


def _rm_shape(S, d, width):
    return (S, width) if d == 1 else (d, S // d, width)


def _rm_spec(tm, d, width):
    if d == 1:
        return pl.BlockSpec((tm, width), lambda i: (i, 0))
    return pl.BlockSpec((d, tm // d, width), lambda i: (0, i, 0))


def _rm_put(dst_ref, cols, buf_ref, d):
    if d == 1:
        dst_ref[:, cols] = buf_ref[...].astype(dst_ref.dtype)
        return
    m = buf_ref.shape[0] // d
    for r in range(d):
        dst_ref[r, :, cols] = buf_ref[pl.ds(r, m, stride=d), :].astype(dst_ref.dtype)


def _rm_reader(buf_ref, src_ref, d):
    if d == 1:
        return lambda s, rows: src_ref[rows, s * HD:(s + 1) * HD].astype(F32)
    m = buf_ref.shape[1] // d
    for s in range(buf_ref.shape[0]):
        for r in range(d):
            buf_ref.at[s][pl.ds(r, m, stride=d), :] = src_ref[r, :, s * HD:(s + 1) * HD].astype(F32)
    return lambda s, rows: buf_ref.at[s][rows, :]


def _head_mean(v):
    return jnp.dot(v, jnp.full((HD, HD), 1.0 / HD, F32), precision=lax.Precision.HIGHEST,
                   preferred_element_type=F32)


def _qknorm_fwd(proj, gqk, tm=512):
    S = proj.shape[0]
    W = 2 * ATT_W
    dil = [d for _, d in ATT_GROUPS]

    def body(p_ref, g_ref, o0, o1, o2, buf):
        outs = (o0, o1, o2)
        for hd in range(3 * ATT_HEADS):
            which, head = hd // ATT_HEADS, hd % ATT_HEADS
            grp, slot = head // ATT_HPG, head % ATT_HPG
            cols = slice(hd * HD, (hd + 1) * HD)

            def chunk(rows, which=which, cols=cols):
                v = p_ref[rows, cols].astype(F32)
                if which < 2:
                    rstd = lax.rsqrt(_head_mean(v * v) + EPS)
                    v = v * rstd * g_ref[:, cols]
                buf[rows, :] = v

            chunk(slice(None))
            _rm_put(outs[grp], slice(which * GW + slot * HD, which * GW + (slot + 1) * HD), buf, dil[grp])

    return pl.pallas_call(
        body, name="qknorm_fwd", grid=(S // tm,),
        in_specs=[pl.BlockSpec((tm, 3 * ATT_W), lambda i: (i, 0)), pl.BlockSpec((1, W), lambda i: (0, 0))],
        out_specs=[_rm_spec(tm, d, 3 * GW) for d in dil],
        out_shape=[_sds(_rm_shape(S, d, 3 * GW), CDT) for d in dil],
        scratch_shapes=[pltpu.VMEM((tm, HD), F32)],
        compiler_params=_params(1))(proj, gqk)


def _qknorm_bwd(proj, gqk, dqs, dks, dvs, dproj, tm=256):
    S = proj.shape[0]
    W = 2 * ATT_W
    dil = [d for _, d in ATT_GROUPS]

    def body(p_ref, g_ref, *refs):
        ins = refs[0:9]
        o_ref, dg_ref = refs[10], refs[11]
        bufs = refs[12:21]
        i = pl.program_id(0)

        @pl.when(i == 0)
        def _():
            dg_ref[...] = jnp.zeros_like(dg_ref)

        nat = [_rm_reader(bufs[j], ins[j], dil[j % 3]) for j in range(9)]
        dq_get, dk_get, dv_get = nat[0:3], nat[3:6], nat[6:9]
        for hd in range(2 * ATT_HEADS):
            sl = slice(hd * HD, (hd + 1) * HD)
            head = hd % ATT_HEADS
            grp, slot = head // ATT_HPG, head % ATT_HPG
            get = (dq_get if hd < ATT_HEADS else dk_get)[grp]

            def chunk(rows, sl=sl, slot=slot, get=get):
                dn = get(slot, rows)
                v = p_ref[rows, sl].astype(F32)
                rstd = lax.rsqrt(_head_mean(v * v) + EPS)
                vh = v * rstd
                dg_ref[:, sl] += jnp.sum(dn * vh, axis=0, keepdims=True)
                dvh = dn * g_ref[:, sl]
                o_ref[rows, sl] = (rstd * (dvh - vh * _head_mean(dvh * vh))).astype(o_ref.dtype)

            chunk(slice(None))
        for head in range(ATT_HEADS):
            grp, slot = head // ATT_HPG, head % ATT_HPG
            o_ref[:, W + head * HD:W + (head + 1) * HD] = dv_get[grp](slot, slice(None)).astype(o_ref.dtype)

    return pl.pallas_call(
        body, name="qknorm_bwd", grid=(S // tm,),
        in_specs=[pl.BlockSpec((tm, W), lambda i: (i, 0)), pl.BlockSpec((1, W), lambda i: (0, 0))]
        + [_rm_spec(tm, d, GW) for d in dil] * 3 + [pl.BlockSpec(memory_space=pl.ANY)],
        out_specs=[pl.BlockSpec((tm, 3 * ATT_W), lambda i: (i, 0)), pl.BlockSpec((1, W), lambda i: (0, 0))],
        out_shape=[_sds(dproj.shape, dproj.dtype), _sds((1, W), F32)],
        scratch_shapes=[pltpu.VMEM((ATT_HPG, tm, HD), F32)] * 9,
        input_output_aliases={11: 0},
        compiler_params=_params(1))(proj, gqk, *dqs, *dks, *dvs, dproj)


def _att_mask(n):
    qi = lax.broadcasted_iota(jnp.int32, (BLK, 2 * BLK), 0)
    kj = lax.broadcasted_iota(jnp.int32, (BLK, 2 * BLK), 1)
    dist = BLK + qi - kj
    valid_all = (dist >= 0) & (dist <= BLK)
    return valid_all & ((kj >= BLK) | (n > 0)), valid_all, dist.astype(F32)


def _att_slopes(grp):
    return [2.0 ** (-8.0 * (grp * ATT_HPG + hh + 1) / ATT_HEADS) for hh in range(ATT_HPG)]


ATT_PLANES_PER_STEP = 4


def _att_planes(d):
    return d if d > 1 else ATT_PLANES_PER_STEP


def _att_3d(a, d):
    return a.reshape(ATT_PLANES_PER_STEP, a.shape[0] // ATT_PLANES_PER_STEP, a.shape[1]) if d == 1 else a


def _att_spec(row_fn, col=0):
    return pl.BlockSpec((ATT_PLANES_PER_STEP, BLK, GW), lambda r, n: (r, row_fn(n), col))


def _att_chains():
    return [(rr * ATT_HPG + hh, rr, slice(hh * HD, (hh + 1) * HD), hh)
            for rr in range(ATT_PLANES_PER_STEP) for hh in range(ATT_HPG)]


def _att_qkv_specs(nb):
    last = nb - 1

    def cur(n):
        return jnp.minimum(n, last)

    def prev(n):
        return jnp.maximum(jnp.minimum(n, last) - 1, 0)

    return [_att_spec(cur, 0), _att_spec(prev, 1), _att_spec(cur, 1), _att_spec(prev, 2), _att_spec(cur, 2),
            _att_spec(lambda n: last, 1), _att_spec(lambda n: last, 2)]


def _att_prev(seg, n, prev_ref, last_ref, rr, sl):
    t = prev_ref[rr, :, sl]
    if seg and rr > 0:
        t = jnp.where(n == 0, last_ref[rr - 1, :, sl], t)
    return t


def _att_fwd(grp, S, qkv):
    _, d = ATT_GROUPS[grp]
    seg = d == 1
    P = _att_planes(d)
    L = S // P
    nb = L // BLK
    assert P % ATT_PLANES_PER_STEP == 0 and (not seg or P == ATT_PLANES_PER_STEP)
    slopes = _att_slopes(grp)
    scale = HD ** -0.5
    chains = _att_chains()

    def body(q_ref, kp_ref, kc_ref, vp_ref, vc_ref, kl_ref, vl_ref, o_ref, l_ref, s_buf, p_buf, den_buf):
        n = pl.program_id(1)
        valid, valid_all, distf = _att_mask(n)
        for c, rr, sl, hh in chains:
            k = jnp.concatenate([_att_prev(seg, n, kp_ref, kl_ref, rr, sl), kc_ref[rr, :, sl]], axis=0)
            s_buf[c] = _dot_nt(q_ref[rr, :, sl], k)
        for c, rr, sl, hh in chains:
            s = s_buf[c] * scale + (-slopes[hh] * d) * distf
            s = jnp.where(valid_all if seg and rr > 0 else valid, s, -1e30)
            m = jnp.max(s, axis=-1, keepdims=True)
            p = jnp.exp(s - m)
            den = jnp.sum(p, axis=-1, keepdims=True)
            p_buf[c] = p.astype(CDT)
            den_buf[c] = jnp.broadcast_to(den, (BLK, HD))
            l_ref[rr, :, sl] = jnp.broadcast_to(m + jnp.log(den), (BLK, HD))
        for c, rr, sl, hh in chains:
            v = jnp.concatenate([_att_prev(seg, n, vp_ref, vl_ref, rr, sl), vc_ref[rr, :, sl]], axis=0)
            o_ref[rr, :, sl] = _dot_nn(p_buf[c], v) / den_buf[c]

    out_spec = _att_spec(lambda n: n)
    n_ch = len(chains)
    q3 = _att_3d(qkv, d)
    o, l = pl.pallas_call(
        body, name="att_fwd_g%d" % grp, grid=(P // ATT_PLANES_PER_STEP, nb),
        in_specs=_att_qkv_specs(nb),
        out_specs=[out_spec, out_spec],
        out_shape=[_sds((P, L, GW), F32)] * 2,
        scratch_shapes=[pltpu.VMEM((n_ch, BLK, 2 * BLK), F32), pltpu.VMEM((n_ch, BLK, 2 * BLK), CDT),
                        pltpu.VMEM((n_ch, BLK, HD), F32)],
        compiler_params=_params(2),
    )(*[q3] * 7)
    return o.reshape(_rm_shape(S, d, GW)), l.reshape(_rm_shape(S, d, GW))


def _att_bwd(grp, S, qkv, lse, do_g, c_g):
    _, d = ATT_GROUPS[grp]
    seg = d == 1
    P = _att_planes(d)
    L = S // P
    nb = L // BLK
    assert P % ATT_PLANES_PER_STEP == 0 and (not seg or P == ATT_PLANES_PER_STEP)
    slopes = _att_slopes(grp)
    scale = HD ** -0.5
    last = nb - 1
    chains = _att_chains()

    def body(q_ref, kp_ref, kc_ref, vp_ref, vc_ref, kl_ref, vl_ref, l_ref, do_ref, c_ref, dq_ref, dk_ref, dv_ref,
             ck, cv, fk, fv, s_buf, dp_buf, p_buf, ds_buf):
        n = pl.program_id(1)

        @pl.when(n == 0)
        def _():
            for buf in (ck, cv, fk, fv):
                buf[...] = jnp.zeros_like(buf)

        @pl.when(n < nb)
        def _():
            valid, valid_all, distf = _att_mask(n)
            for c, rr, sl, hh in chains:
                k = jnp.concatenate([_att_prev(seg, n, kp_ref, kl_ref, rr, sl), kc_ref[rr, :, sl]], axis=0)
                v = jnp.concatenate([_att_prev(seg, n, vp_ref, vl_ref, rr, sl), vc_ref[rr, :, sl]], axis=0)
                s_buf[c] = _dot_nt(q_ref[rr, :, sl], k)
                dp_buf[c] = _dot_nt(do_ref[rr, :, sl], v)
            for c, rr, sl, hh in chains:
                s = s_buf[c] * scale + (-slopes[hh] * d) * distf
                p = jnp.where(valid_all if seg and rr > 0 else valid, jnp.exp(s - l_ref[rr, :, sl][:, 0:1]), 0.0)
                p_buf[c] = p.astype(CDT)
                ds_buf[c] = (p * (dp_buf[c] + c_ref[rr, :, sl][:, 0:1]) * scale).astype(CDT)
            for c, rr, sl, hh in chains:
                k = jnp.concatenate([_att_prev(seg, n, kp_ref, kl_ref, rr, sl), kc_ref[rr, :, sl]], axis=0)
                ds = ds_buf[c]
                dq_ref[rr, :, sl] = _dot_nn(ds, k)
                dk = _dot_tn(ds, q_ref[rr, :, sl])
                dv = _dot_tn(p_buf[c], do_ref[rr, :, sl])
                dk_ref[rr, :, sl] = ck[rr, :, sl] + dk[0:BLK]
                dv_ref[rr, :, sl] = cv[rr, :, sl] + dv[0:BLK]
                ck[rr, :, sl] = dk[BLK:2 * BLK]
                cv[rr, :, sl] = dv[BLK:2 * BLK]
                if seg and rr > 0:
                    @pl.when(n == 0)
                    def _(rr=rr, sl=sl, dk=dk, dv=dv):
                        fk[rr - 1, :, sl] = dk[0:BLK]
                        fv[rr - 1, :, sl] = dv[0:BLK]

        @pl.when(n == nb)
        def _():
            dk_ref[...] = ck[...] + fk[...]
            dv_ref[...] = cv[...] + fv[...]

    blk = (ATT_PLANES_PER_STEP, BLK, GW)
    at_q = _att_spec(lambda n: jnp.minimum(n, last))
    behind = _att_spec(lambda n: jnp.maximum(n - 1, 0))
    n_ch = len(chains)
    q3 = _att_3d(qkv, d)
    res = pl.pallas_call(
        body, name="att_bwd_g%d" % grp, grid=(P // ATT_PLANES_PER_STEP, nb + 1),
        in_specs=_att_qkv_specs(nb) + [at_q, at_q, at_q],
        out_specs=[at_q, behind, behind],
        out_shape=[_sds((P, L, GW), F32)] * 3,
        scratch_shapes=[pltpu.VMEM(blk, F32)] * 4
        + [pltpu.VMEM((n_ch, BLK, 2 * BLK), F32), pltpu.VMEM((n_ch, BLK, 2 * BLK), F32),
           pltpu.VMEM((n_ch, BLK, 2 * BLK), CDT), pltpu.VMEM((n_ch, BLK, 2 * BLK), CDT)],
        compiler_params=_params(2),
    )(*[q3] * 7, _att_3d(lse, d), _att_3d(do_g, d), _att_3d(c_g, d))
    return [t.reshape(_rm_shape(S, d, GW)) for t in res]


def _mix_alpha(l0, l1, l2):
    mx = jnp.maximum(jnp.maximum(l0, l1), l2)
    e = [jnp.exp(l0 - mx), jnp.exp(l1 - mx), jnp.exp(l2 - mx)]
    tot = e[0] + e[1] + e[2]
    return [ei / tot for ei in e]


def _mix_fwd(S, os_, ls_, tm=512):
    dil = [d for _, d in ATT_GROUPS]

    def body(*refs):
        out, bufs = refs[6], refs[7:13]
        get = [_rm_reader(bufs[j], refs[j], dil[j % 3]) for j in range(6)]
        rows = slice(None)
        for s in range(ATT_HPG):
            al = _mix_alpha(*[get[3 + g](s, rows) for g in range(3)])
            mixed = al[0] * get[0](s, rows) + al[1] * get[1](s, rows) + al[2] * get[2](s, rows)
            out[:, s * HD:(s + 1) * HD] = mixed.astype(out.dtype)

    specs = [_rm_spec(tm, d, GW) for d in dil]
    return pl.pallas_call(
        body, name="mix_fwd", grid=(S // tm,), in_specs=specs * 2, out_specs=pl.BlockSpec((tm, GW), lambda i: (i, 0)),
        out_shape=_sds((S, GW), CDT), scratch_shapes=[pltpu.VMEM((ATT_HPG, tm, HD), F32)] * 6,
        compiler_params=_params(1))(*os_, *ls_)


def _mix_bwd(S, os_, ls_, do_a, tm=512):
    dil = [d for _, d in ATT_GROUPS]

    def body(*refs):
        d_ref, outs, bufs, tmps = refs[6], refs[7:13], refs[13:19], refs[19:25]
        get = [_rm_reader(bufs[j], refs[j], dil[j % 3]) for j in range(6)]
        for s in range(ATT_HPG):
            cols = slice(s * HD, (s + 1) * HD)

            def chunk(rows, s=s, cols=cols):
                al = _mix_alpha(*[get[3 + g](s, rows) for g in range(3)])
                dv = d_ref[rows, cols]
                o_a = al[0] * get[0](s, rows) + al[1] * get[1](s, rows) + al[2] * get[2](s, rows)
                dsum = jnp.sum(dv * o_a, axis=-1, keepdims=True)
                for g in range(3):
                    tmps[g][rows, :] = al[g] * dv
                    tmps[3 + g][rows, :] = -(al[g] * dsum)

            chunk(slice(None))
            for j in range(6):
                _rm_put(outs[j], cols, tmps[j], dil[j % 3])

    specs = [_rm_spec(tm, d, GW) for d in dil]
    res = pl.pallas_call(
        body, name="mix_bwd", grid=(S // tm,), in_specs=specs * 2 + [pl.BlockSpec((tm, GW), lambda i: (i, 0))],
        out_specs=specs * 2,
        out_shape=[_sds(_rm_shape(S, d, GW), CDT) for d in dil] + [_sds(_rm_shape(S, d, GW), F32) for d in dil],
        scratch_shapes=[pltpu.VMEM((ATT_HPG, tm, HD), F32)] * 6 + [pltpu.VMEM((tm, HD), F32)] * 6,
        compiler_params=_params(1))(*os_, *ls_, do_a)
    return res[:3], res[3:]


def _ret_tables(dk):
    H, C = RET_HEADS, BLK
    log_g = jnp.log(1.0 - 2.0 ** (-5.0 - jnp.arange(H, dtype=F32)))
    idx = jnp.arange(C, dtype=F32)
    diff = idx[:, None] - idx[None, :]
    decay = jnp.where(diff >= 0, jnp.exp(log_g[:, None, None] * jnp.maximum(diff, 0.0)), 0.0)
    xi = jnp.exp(log_g[:, None] * (idx[None, :] + 1.0))
    zeta = jnp.exp(log_g[:, None] * (C - 1.0 - idx[None, :])) * (dk ** -0.5)
    g_chunk = jnp.exp(log_g * C)
    bc = lambda t: jnp.broadcast_to(t[:, :, None], (H, C, C))
    return decay, bc(xi), bc(zeta), jnp.broadcast_to(g_chunk[:, None, None], (H, 8, C))


def _gn_fwd(o, g, b):
    mu = jnp.mean(o, axis=-1, keepdims=True)
    xc = o - mu
    rstd = lax.rsqrt(jnp.mean(xc * xc, axis=-1, keepdims=True) + EPS)
    yh = xc * rstd
    return yh, rstd, yh * g + b


def _ret_specs(dk, dv, order):
    H = RET_HEADS
    qk_w, v_w = H * dk, H * dv
    off_q = 3 * ATT_W
    off_k, off_v, off_g = off_q + qk_w, off_q + 2 * qk_w, off_q + 2 * qk_w + v_w
    assert 2 * dk == dv and all(off % dv == 0 for off in (off_q, off_k, off_v, off_g))

    def col(off, j):
        return pl.BlockSpec((BLK, dv), lambda i: (order(i), off // dv + j))

    tab = pl.BlockSpec((H, BLK, BLK), lambda i: (0, 0, 0))
    return ([col(off_q, j) for j in range(H // 2)] + [col(off_k, j) for j in range(H // 2)]
            + [col(off_v, j) for j in range(H)] + [col(off_g, j) for j in range(H)]
            + [tab, tab, tab, pl.BlockSpec((H, 8, BLK), lambda i: (0, 0, 0))])


def _ret_heads(refs, dk):
    H = RET_HEADS
    q_refs, k_refs = refs[0:H // 2], refs[H // 2:H]
    v_refs, gr_refs = refs[H:2 * H], refs[2 * H:3 * H]

    def head(h):
        cols = slice((h % 2) * dk, (h % 2 + 1) * dk)
        return q_refs[h // 2][:, cols], k_refs[h // 2][:, cols], v_refs[h][...], gr_refs[h][...]

    return head, refs[3 * H:3 * H + 4]


def _ret_fwd(proj, gn_g, gn_b, dk, dv):
    S = proj.shape[0]
    N = S // BLK
    H = RET_HEADS
    kscale = dk ** -0.5
    n_in = 3 * H + 4

    def body(*refs):
        head, (dec_ref, xi_ref, zeta_ref, gc_ref) = _ret_heads(refs, dk)
        g_ref, b_ref, opre_ref, or_ref, st_ref, state, s_buf, cross_buf = refs[n_in:n_in + 8]
        n = pl.program_id(0)

        @pl.when(n == 0)
        def _():
            state[...] = jnp.zeros_like(state)

        for h in range(H):
            q, k, v, _ = head(h)
            s_buf[h] = _dot_nt(q, k)
            st = state[h]
            st_c = st.astype(CDT)
            st_ref[h] = st_c
            cross_buf[h] = _dot_nn(q, st_c)
            kz = (k.astype(F32) * zeta_ref[h][:, 0:1]).astype(CDT)
            state[h] = st * gc_ref[h][0:1, 0:1] + _dot_tn(kz, v)
        for h in range(H):
            vs = slice(h * dv, (h + 1) * dv)
            _, _, v, gr = head(h)
            s = s_buf[h] * kscale * dec_ref[h]
            o = _dot_nn(s.astype(CDT), v) + cross_buf[h] * xi_ref[h][:, 0:1]
            opre_ref[:, vs] = o
            _, _, y = _gn_fwd(o, g_ref[:, vs], b_ref[:, vs])
            gr = gr.astype(F32)
            or_ref[:, vs] = (y * (gr * _sigmoid(gr))).astype(or_ref.dtype)

    v_w = H * dv
    row = pl.BlockSpec((1, v_w), lambda i: (0, 0))
    tile = pl.BlockSpec((BLK, v_w), lambda i: (i, 0))
    return pl.pallas_call(
        body, name="ret_fwd", grid=(N,),
        in_specs=_ret_specs(dk, dv, lambda i: i) + [row, row],
        out_specs=[tile, tile, pl.BlockSpec((None, H, dk, dv), lambda i: (i, 0, 0, 0))],
        out_shape=[_sds((S, v_w), F32), _sds((S, v_w), CDT), _sds((N, H, dk, dv), CDT)],
        scratch_shapes=[pltpu.VMEM((H, dk, dv), F32), pltpu.VMEM((H, BLK, BLK), F32), pltpu.VMEM((H, BLK, dv), F32)],
        compiler_params=_params(1),
    )(*[proj] * (3 * H), *_ret_tables(dk), gn_g, gn_b)


def _ret_bwd(proj, gn_g, gn_b, o_pre, states, d_or, dga, dgb, dk, dv):
    S, in_w = proj.shape
    N = S // BLK
    H = RET_HEADS
    qk_w, v_w = H * dk, H * dv
    kscale = dk ** -0.5
    n_in = 3 * H + 4
    out_w = 2 * qk_w + 2 * v_w
    gate_w = dga.shape[1]
    col0 = 3 * ATT_W
    assert col0 + out_w + 2 * gate_w == in_w
    rev = lambda i: N - 1 - i

    def body(*refs):
        head, (dec_ref, xi_ref, zeta_ref, gc_ref) = _ret_heads(refs, dk)
        (g_ref, b_ref, opre_ref, st_ref, dor_ref, dga_ref, dgb_ref, dproj_ref, dg_ref, db_ref, dstate, stage,
         sem, do_buf, dox_buf, a_buf, g_buf, dq_buf, dk_buf, dv_buf) = refs[n_in:n_in + 20]
        i = pl.program_id(0)
        slot = i % 2
        out_ref = stage.at[slot]

        def out_copy(s, step):
            rows = pl.ds(pl.multiple_of(rev(step) * BLK, BLK), BLK)
            return pltpu.make_async_copy(stage.at[s], dproj_ref.at[rows, pl.ds(col0, in_w - col0)], sem.at[s])

        @pl.when(i >= 2)
        def _():
            out_copy(slot, i - 2).wait()

        @pl.when(i == 0)
        def _():
            dstate[...] = jnp.zeros_like(dstate)
            dg_ref[...] = jnp.zeros_like(dg_ref)
            db_ref[...] = jnp.zeros_like(db_ref)

        out_ref[:, out_w:out_w + gate_w] = dga_ref[...]
        out_ref[:, out_w + gate_w:out_w + 2 * gate_w] = dgb_ref[...]
        for h in range(H):
            vs = slice(h * dv, (h + 1) * dv)
            _, _, _, gr = head(h)
            gr = gr.astype(F32)
            sg = _sigmoid(gr)
            gain = g_ref[:, vs]
            yh, rstd, y = _gn_fwd(opre_ref[:, vs], gain, b_ref[:, vs])
            d_or_v = dor_ref[:, vs]
            dy = d_or_v * (gr * sg)
            out_ref[:, 2 * qk_w + v_w + h * dv:2 * qk_w + v_w + (h + 1) * dv] = (
                d_or_v * y * (sg * (1.0 + gr * (1.0 - sg)))).astype(out_ref.dtype)
            dg_ref[:, vs] += jnp.sum(dy * yh, axis=0, keepdims=True)
            db_ref[:, vs] += jnp.sum(dy, axis=0, keepdims=True)
            dyh = dy * gain
            do = rstd * (dyh - jnp.mean(dyh, axis=-1, keepdims=True)
                         - yh * jnp.mean(dyh * yh, axis=-1, keepdims=True))
            do_buf[h] = do.astype(CDT)
            dox_buf[h] = (do * xi_ref[h][:, 0:1]).astype(CDT)
        for h in range(H):
            q, k, v, _ = head(h)
            dox = dox_buf[h]
            a_buf[h] = _dot_nt(q, k)
            g_buf[h] = _dot_nt(do_buf[h], v)
            dsn = dstate[h]
            dsn_c = dsn.astype(CDT)
            kz = (k.astype(F32) * zeta_ref[h][:, 0:1]).astype(CDT)
            dq_buf[h] = _dot_nt(dox, st_ref[h])
            dk_buf[h] = _dot_nt(v, dsn_c)
            dv_buf[h] = _dot_nn(kz, dsn_c)
            dstate[h] = dsn * gc_ref[h][0:1, 0:1] + _dot_tn(q, dox)
        for h in range(H):
            q, k, _, _ = head(h)
            decay = dec_ref[h]
            a_c = (a_buf[h] * kscale * decay).astype(CDT)
            g_c = (g_buf[h] * decay).astype(CDT)
            dq = _dot_nn(g_c, k) * kscale + dq_buf[h]
            dkk = _dot_tn(g_c, q) * kscale + dk_buf[h] * zeta_ref[h][:, 0:1]
            dvv = _dot_tn(a_c, do_buf[h]) + dv_buf[h]
            out_ref[:, h * dk:(h + 1) * dk] = dq.astype(out_ref.dtype)
            out_ref[:, qk_w + h * dk:qk_w + (h + 1) * dk] = dkk.astype(out_ref.dtype)
            out_ref[:, 2 * qk_w + h * dv:2 * qk_w + (h + 1) * dv] = dvv.astype(out_ref.dtype)

        cp = out_copy(slot, i)
        cp.start()

        @pl.when(i == N - 1)
        def _():
            cp.wait()
            if N >= 2:
                out_copy(1 - slot, i - 1).wait()

    row = pl.BlockSpec((1, v_w), lambda i: (0, 0))
    tile = pl.BlockSpec((BLK, v_w), lambda i: (rev(i), 0))
    gate = pl.BlockSpec((BLK, gate_w), lambda i: (rev(i), 0))
    return pl.pallas_call(
        body, name="ret_bwd", grid=(N,),
        in_specs=_ret_specs(dk, dv, rev) + [row, row, tile,
                 pl.BlockSpec((None, H, dk, dv), lambda i: (rev(i), 0, 0, 0)), tile, gate, gate],
        out_specs=[pl.BlockSpec(memory_space=pl.ANY), row, row],
        out_shape=[_sds((S, in_w), CDT), _sds((1, v_w), F32), _sds((1, v_w), F32)],
        scratch_shapes=[pltpu.VMEM((H, dk, dv), F32), pltpu.VMEM((2, BLK, in_w - col0), CDT),
                        pltpu.SemaphoreType.DMA((2,)),
                        pltpu.VMEM((H, BLK, dv), CDT), pltpu.VMEM((H, BLK, dv), CDT),
                        pltpu.VMEM((H, BLK, BLK), F32), pltpu.VMEM((H, BLK, BLK), F32),
                        pltpu.VMEM((H, BLK, dk), F32), pltpu.VMEM((H, BLK, dk), F32), pltpu.VMEM((H, BLK, dv), F32)],
        compiler_params=_params(1),
    )(*[proj] * (3 * H), *_ret_tables(dk), gn_g, gn_b, o_pre, states, d_or, dga, dgb)


def _merge_fwd(o_a, o_r, wa, wb, proj, d_model, tm=1024, tn=512):
    S, in_w = proj.shape
    off_a, off_b = in_w - 2 * d_model, in_w - d_model
    assert off_a % tn == 0 and off_b % tn == 0

    def body(oa_ref, or_ref, wa_ref, wb_ref, ga_ref, gb_ref, y_ref, pa_ref, pb_ref):
        pa = _dot_nn(oa_ref[...], wa_ref[...])
        pb = _dot_nn(or_ref[...], wb_ref[...])
        y = _sigmoid(ga_ref[...].astype(F32)) * pa + _sigmoid(gb_ref[...].astype(F32)) * pb
        y_ref[...] = y.astype(y_ref.dtype)
        pa_ref[...] = pa.astype(pa_ref.dtype)
        pb_ref[...] = pb.astype(pb_ref.dtype)

    ka, kb = o_a.shape[1], o_r.shape[1]
    out = pl.BlockSpec((tm, tn), lambda i, j: (i, j))
    return pl.pallas_call(
        body, name="merge_fwd", grid=(S // tm, d_model // tn),
        in_specs=[pl.BlockSpec((tm, ka), lambda i, j: (i, 0)), pl.BlockSpec((tm, kb), lambda i, j: (i, 0)),
                  pl.BlockSpec((ka, tn), lambda i, j: (0, j)), pl.BlockSpec((kb, tn), lambda i, j: (0, j)),
                  pl.BlockSpec((tm, tn), lambda i, j: (i, off_a // tn + j)),
                  pl.BlockSpec((tm, tn), lambda i, j: (i, off_b // tn + j))],
        out_specs=[out, out, out], out_shape=[_sds((S, d_model), CDT)] * 3,
        compiler_params=_params(2))(o_a, o_r, wa, wb, proj, proj)


def _d_x(dproj, w_in, x, g, dx1, dep, tm=512, row_groups=2):
    S, D = x.shape
    n_sh, _, ns = w_in.shape
    nt = S // tm // row_groups
    deps = [d for d in (dep,) if d is not None]

    def body(a_ref, b_ref, x_ref, g_ref, r_ref, *rest):
        dx_ref, dg_ref, acc = rest[len(deps):]
        h, k, i = pl.program_id(0), pl.program_id(1), pl.program_id(2)

        @pl.when(k == 0)
        def _():
            acc[i] = jnp.zeros((tm, D), F32)

        acc[i] += _dot_nt(a_ref[...], b_ref[...])

        @pl.when(k == n_sh - 1)
        def _():
            _ep_rms_bwd(acc[i], (x_ref, g_ref, r_ref), (dx_ref, dg_ref), h * nt + i)

    def last_only(h, k, i):
        return (h * nt + jnp.where(k == n_sh - 1, i, 0), 0)

    return pl.pallas_call(
        body, name="d_x", grid=(row_groups, n_sh, nt),
        in_specs=[pl.BlockSpec((tm, ns), lambda h, k, i: (h * nt + i, k)),
                  pl.BlockSpec((None, D, ns), lambda h, k, i: (k, 0, 0)),
                  pl.BlockSpec((tm, D), last_only), pl.BlockSpec((1, D), lambda h, k, i: (0, 0)),
                  pl.BlockSpec((tm, D), last_only)] + [pl.BlockSpec(memory_space=pl.ANY)] * len(deps),
        out_specs=[pl.BlockSpec((tm, D), last_only), pl.BlockSpec((1, D), lambda h, k, i: (0, 0))],
        out_shape=[_sds((S, D), F32), _sds((1, D), F32)],
        scratch_shapes=[pltpu.VMEM((nt, tm, D), F32)],
        compiler_params=_params(3))(dproj, w_in, x, g, dx1, *deps)


def _local_step(x, target, w_in_mine, chip, others, near_w_in, far_w_in, small, late_weights, on_grads, deps0=()):
    S, D = x.shape
    ns_in = w_in_mine.shape[1]
    in_w = N_CHIPS * ns_in
    d_ff = 4 * D
    ret_v_w = 2 * D
    dv = ret_v_w // RET_HEADS
    dk = (in_w - 3 * ATT_W - 2 * ret_v_w - 2 * D) // (2 * RET_HEADS)
    gqk = jnp.concatenate([small["q_norm_g"].reshape(1, ATT_W), small["k_norm_g"].reshape(1, ATT_W)], axis=1)
    g1, g2 = small["norm1_g"], small["norm2_g"]
    gn_g, gn_b = small["ret_gn_g"], small["ret_gn_b"]

    proj_sds = _sds((S, in_w), CDT)
    xn, proj = _in_proj_mine(x, g1, w_in_mine, chip, proj_sds, deps0)
    for stage, (get_w_in, chips) in enumerate(((near_w_in, others[:2]), (far_w_in, others[2:]))):
        w_in, started = get_w_in(proj)
        (proj,) = _matmul(
            "in_proj_far%d" % stage, "nn", xn, w_in, tm=512, tn=ns_in, tk=D, prefetch=[chips],
            n_cols=chips.shape[0] * ns_in, b_spec=pl.BlockSpec((None, D, ns_in), lambda i, j, k, o: (o[j], 0, 0)),
            outs=[(proj_sds, pl.BlockSpec((512, ns_in), lambda i, j, k, o: (i, o[j])))], epilogue=_ep_store,
            deps=[proj, started], alias_dep_to_out=(0, 0), j_outer=True)
    qkv = _qknorm_fwd(proj, gqk)
    att = [_att_fwd(g, S, qkv[g]) for g in range(3)]
    os_, ls_ = [a[0] for a in att], [a[1] for a in att]
    o_a = _mix_fwd(S, os_, ls_)
    o_pre, o_r, states = _ret_fwd(proj, gn_g, gn_b, dk, dv)
    w = late_weights(o_r)
    y, pa, pb = _merge_fwd(o_a, o_r, w["w_proj_a"], w["w_proj_b"], proj, D)
    x1, xn2 = _matmul("out_proj", "nn", y, w["w_out"], tm=1024, tn=D, tk=D,
                      extras=[(x, _mn(1024, D)), (g2, _row(D))],
                      outs=[(_sds((S, D), F32), _mn(1024, D)), (_sds((S, D), CDT), _mn(1024, D))],
                      epilogue=_ep_resid_norm)
    hid, act = _matmul("mlp_up", "nn", xn2, w["w_up"], tm=512, tn=2048, tk=D, j_outer=True,
                       outs=[(_sds((S, d_ff), CDT), _mn(512, 2048))] * 2, epilogue=_ep_up)
    dx2, dx2c, loss_row = _matmul(
        "mlp_down_loss", "nn", act, w["w_down"], tm=512, tn=D, tk=d_ff,
        extras=[(x1, _mn(512, D)), (target, _mn(512, D))],
        outs=[(_sds((S, D), F32), _mn(512, D)), (_sds((S, D), CDT), _mn(512, D)), (_sds((1, D), F32), _row(D))],
        epilogue=functools.partial(_ep_down_loss, inv_d=1.0 / D))
    loss = 0.5 * jnp.sum(loss_row) / D

    (dh,) = _matmul("d_hidden", "nt", dx2c, w["w_down"], tm=512, tn=2048, tk=D, j_outer=True,
                    extras=[(hid, _mn(512, 2048))], outs=[(_sds((S, d_ff), CDT), _mn(512, 2048))], epilogue=_ep_dh)
    (gw_down,) = _matmul("dw_down", "tn", act, dx2c, tm=1024, tn=D, tk=1024,
                         outs=[(_sds((d_ff, D), F32), _mn(1024, D))], epilogue=_ep_store)
    (gw_up,) = _matmul("dw_up", "tn", xn2, dh, tm=D, tn=1024, tk=1024,
                       outs=[(_sds((D, d_ff), F32), _mn(D, 1024))], epilogue=_ep_store)
    tok = on_grads({"w_down": gw_down, "w_up": gw_up})
    dx1, dx1c, dg2 = _matmul(
        "d_x1", "nt", dh, w["w_up"], tm=512, tn=D, tk=d_ff,
        extras=[(x1, _mn(512, D)), (g2, _row(D)), (dx2, _mn(512, D))],
        outs=[(_sds((S, D), F32), _mn(512, D)), (_sds((S, D), CDT), _mn(512, D)), (_sds((1, D), F32), _row(D))],
        epilogue=_ep_rms_bwd, deps=[tok])

    gt = 512
    assert (in_w - 2 * D) % gt == 0
    off_a, off_b = (in_w - 2 * D) // gt, (in_w - D) // gt
    dpa, dpb, dga, dgb = _matmul(
        "d_gates", "nt", dx1c, w["w_out"], tm=1024, tn=gt, tk=D,
        extras=[(proj, _mn(1024, gt, off_a)), (proj, _mn(1024, gt, off_b)), (pa, _mn(1024, gt)),
                (pb, _mn(1024, gt))],
        outs=[(_sds((S, D), CDT), _mn(1024, gt))] * 4, epilogue=_ep_gates)
    (gw_out,) = _matmul("dw_out", "tn", y, dx1c, tm=D, tn=D, tk=1024,
                        outs=[(_sds((D, D), F32), _mn(D, D))], epilogue=_ep_store)
    (gw_pa,) = _matmul("dw_proj_a", "tn", o_a, dpa, tm=GW, tn=D, tk=1024,
                       outs=[(_sds((GW, D), F32), _mn(GW, D))], epilogue=_ep_store)
    (gw_pb,) = _matmul("dw_proj_b", "tn", o_r, dpb, tm=1024, tn=D, tk=1024,
                       outs=[(_sds((ret_v_w, D), F32), _mn(1024, D))], epilogue=_ep_store)
    (do_a,) = _matmul("d_o_a", "nt", dpa, w["w_proj_a"], tm=1024, tn=GW, tk=D,
                      outs=[(_sds((S, GW), F32), _mn(1024, GW))], epilogue=_ep_store)
    tok = on_grads({"w_out": gw_out, "w_proj_a": gw_pa, "w_proj_b": gw_pb})
    (d_or,) = _matmul("d_o_r", "nt", dpb, w["w_proj_b"], tm=512, tn=ret_v_w, tk=D,
                      outs=[(_sds((S, ret_v_w), F32), _mn(512, ret_v_w))], epilogue=_ep_store, deps=[tok])

    dproj, dgn_g, dgn_b = _ret_bwd(proj, gn_g, gn_b, o_pre, states, d_or, dga, dgb, dk, dv)
    do_gs, c_gs = _mix_bwd(S, os_, ls_, do_a)
    datt_parts = [_att_bwd(g, S, qkv[g], ls_[g], do_gs[g], c_gs[g]) for g in range(3)]
    dproj, dgqk = _qknorm_bwd(proj, gqk, [p[0] for p in datt_parts], [p[1] for p in datt_parts],
                              [p[2] for p in datt_parts], dproj)

    (gw_in,) = _matmul(
        "dw_in", "tn", xn, dproj, tm=512, tn=ns_in, tk=1024,
        outs=[(_sds((N_CHIPS, D, ns_in), F32), pl.BlockSpec((None, 512, ns_in), lambda i, j, k: (j, i, 0)))],
        epilogue=_ep_store)
    tok = on_grads({"w_in": gw_in})
    grad_x, dg1 = _d_x(dproj, w_in, x, g1, dx1, tok)

    smallg = {"norm1_g": dg1, "q_norm_g": dgqk[:, :ATT_W], "k_norm_g": dgqk[:, ATT_W:],
              "ret_gn_g": dgn_g, "ret_gn_b": dgn_b, "norm2_g": dg2}
    return loss, grad_x, smallg


N_CHIPS = 4
N_DEV = 8


def _place():
    x, y, c = lax.axis_index("x"), lax.axis_index("y"), lax.axis_index("c")
    return x, y, c


def _other_chips(x, y):
    out = []
    for fx, fy in ((1, 0), (0, 1), (1, 1)):
        px = 1 - x if fx else x
        py = 1 - y if fy else y
        out.append(((px, py), 2 * px + py))
    return out


SEM_SPEC = pl.BlockSpec(memory_space=pltpu.SEMAPHORE)
ANY_SPEC = pl.BlockSpec(memory_space=pl.ANY)
EFFECT = pltpu.SideEffectType.DATAFLOW_SIDE_EFFECTING


def _ici_copies(kind, srcs, lands, send, recv, which=(0, 1, 2)):
    x, y, c = _place()
    me = 2 * x + y
    out = []
    for w, (s, l) in enumerate(zip(srcs, lands)):
        for j, ((px, py), pidx) in enumerate(_other_chips(x, y)):
            if j not in which:
                continue
            if kind == "gather":
                half = s.shape[0] // 2
                rows = pl.ds(c * half, half)
                src, dst_there, dst_here = s.at[rows, :], l.at[me, rows, :], l.at[pidx, rows, :]
            else:
                src, dst_there, dst_here = s.at[pidx], l.at[me], l.at[pidx]
            out.append((src, dst_there, dst_here, send.at[3 * w + j], recv.at[3 * w + j], (px, py, c)))
    return out


def _exchange_start(name, kind, srcs, land_shapes, which=(0, 1, 2), lands=None):
    n = len(srcs)
    if lands is None:
        lands = [lax.empty(shape, dtype) for shape, dtype in land_shapes]

    def body(*refs):
        src_refs, land_refs = refs[:n], refs[n:2 * n]
        send, recv = refs[2 * n], refs[2 * n + 1]
        token = refs[-1]
        for src, dst, _, ss, rs, dev in _ici_copies(kind, src_refs, land_refs, send, recv, which):
            pltpu.make_async_remote_copy(src_ref=src, dst_ref=dst, send_sem=ss, recv_sem=rs, device_id=dev,
                                         device_id_type=MESH).start()
        token[...] = jnp.zeros_like(token)

    thru = [pltpu.HBM(s.shape, s.dtype) for s in srcs] + [pltpu.HBM(shape, dtype) for shape, dtype in land_shapes]
    res = pl.pallas_call(
        body, name=name,
        out_shape=(pltpu.SemaphoreType.DMA((3 * n,)), pltpu.SemaphoreType.DMA((3 * n,)), *thru, _sds((8, LANES), F32)),
        in_specs=[HBM_SPEC] * (2 * n), out_specs=(SEM_SPEC, SEM_SPEC, *[HBM_SPEC] * (2 * n), VMEM_SPEC),
        input_output_aliases={i: 2 + i for i in range(2 * n)},
        compiler_params=pltpu.CompilerParams(has_side_effects=EFFECT),
    )(*[pltpu.with_memory_space_constraint(s, pltpu.HBM) for s in srcs],
      *[pltpu.with_memory_space_constraint(l, pltpu.HBM) for l in lands])
    return res[0], res[1], list(res[2:2 + n]), list(res[2 + n:2 + 2 * n]), res[-1]


def _exchange_wait(name, kind, send, recv, srcs, lands, after, which=(0, 1, 2)):
    n = len(srcs)

    def body(*refs):
        src_refs, land_refs = refs[:n], refs[n:2 * n]
        send_ref, recv_ref = refs[2 * n], refs[2 * n + 1]
        for src, _, dst, ss, rs, dev in _ici_copies(kind, src_refs, land_refs, send_ref, recv_ref, which):
            cp = pltpu.make_async_remote_copy(src_ref=src, dst_ref=dst, send_sem=ss, recv_sem=rs, device_id=dev,
                                              device_id_type=MESH)
            cp.wait_send()
            cp.wait_recv()

    thru = [pltpu.HBM(t.shape, t.dtype) for t in list(srcs) + list(lands)]
    res = pl.pallas_call(
        body, name=name, out_shape=thru,
        in_specs=[HBM_SPEC] * (2 * n) + [SEM_SPEC, SEM_SPEC, ANY_SPEC], out_specs=[HBM_SPEC] * (2 * n),
        input_output_aliases={i: i for i in range(2 * n)},
        compiler_params=pltpu.CompilerParams(has_side_effects=EFFECT),
    )(*srcs, *lands, send, recv, after)
    return list(res[:n]), list(res[n:])


PAIR_TILE_ELEMS = 1 << 20


def _pair_fill(name, gathered, mine, core, others, chip, write_mine=True):
    k, r, C = gathered.shape
    half = r // 2
    tr = _row_tile(half, C, PAIR_TILE_ELEMS, mult=16)
    nt = half // tr
    n_far = others.shape[0]

    def body(c_ref, o_ref, chip_ref, in_ref, mine_ref, out_ref, slot, send, recv):
        j = pl.program_id(0)
        _sibling_barrier((j == 0) & (pl.program_id(1) == 0))
        b = (j * nt + pl.program_id(1)) % 2
        x, y, c = _place()
        cp = pltpu.make_async_remote_copy(src_ref=in_ref, dst_ref=slot.at[b], send_sem=send.at[b],
                                          recv_sem=recv.at[b], device_id=(x, y, 1 - c), device_id_type=MESH)

        @pl.when(j < n_far)
        def _():
            cp.start()
            cp.wait_recv()
            out_ref[...] = slot[b]
            cp.wait_send()

        @pl.when(j >= n_far)
        def _():
            out_ref[...] = mine_ref[...]

    def far(j):
        return jnp.minimum(j, n_far - 1)

    grid_spec = pltpu.PrefetchScalarGridSpec(
        num_scalar_prefetch=3, grid=(n_far + (2 if write_mine else 0), nt),
        in_specs=[pl.BlockSpec((tr, C), lambda j, i, c, o, m: (
                      (2 * o[far(j)] + c[0]) * nt + jnp.where(j < n_far, i, nt - 1), 0)),
                  pl.BlockSpec((tr, C), lambda j, i, c, o, m: (jnp.where(j < n_far, 0, (j - n_far) * nt + i), 0))],
        out_specs=pl.BlockSpec((tr, C), lambda j, i, c, o, m: (
            jnp.where(j < n_far, 2 * o[far(j)] + 1 - c[0], 2 * m[0] + j - n_far) * nt + i, 0)),
        scratch_shapes=[pltpu.VMEM((2, tr, C), gathered.dtype), pltpu.SemaphoreType.DMA((2,)),
                        pltpu.SemaphoreType.DMA((2,))])
    out = pl.pallas_call(body, name=name, grid_spec=grid_spec, out_shape=_sds((k * r, C), gathered.dtype),
                         input_output_aliases={3: 0}, compiler_params=_params(2, PAIR_FILL_ID))(
                             core, others, chip, gathered.reshape(k * r, C), mine)
    return out.reshape(k, r, C)


def _pair_reduce(name, g, core):
    k, R, C = g.shape
    half = R // 2
    tr = _row_tile(half, C, PAIR_TILE_ELEMS, mult=16)
    nt = half // tr

    def body(c_ref, mine_ref, give_ref, out_ref, wire_ref, stage, slot, send, recv):
        _sibling_barrier((pl.program_id(0) == 0) & (pl.program_id(1) == 0))
        b = (pl.program_id(0) * nt + pl.program_id(1)) % 2
        x, y, c = _place()
        stage[b] = give_ref[...].astype(stage.dtype)
        cp = pltpu.make_async_remote_copy(src_ref=stage.at[b], dst_ref=slot.at[b], send_sem=send.at[b],
                                          recv_sem=recv.at[b], device_id=(x, y, 1 - c), device_id_type=MESH)
        cp.start()
        cp.wait_recv()
        tot = mine_ref[...] + slot[b].astype(F32)
        out_ref[...] = tot
        wire_ref[...] = tot.astype(wire_ref.dtype)
        cp.wait_send()

    blk = (tr, C)
    out_spec = pl.BlockSpec(blk, lambda s, i, c: (s * nt + i, 0))
    grid_spec = pltpu.PrefetchScalarGridSpec(
        num_scalar_prefetch=1, grid=(k, nt),
        in_specs=[pl.BlockSpec(blk, lambda s, i, c: ((2 * s + c[0]) * nt + i, 0)),
                  pl.BlockSpec(blk, lambda s, i, c: ((2 * s + 1 - c[0]) * nt + i, 0))],
        out_specs=[out_spec, out_spec],
        scratch_shapes=[pltpu.VMEM((2, tr, C), CDT), pltpu.VMEM((2, tr, C), CDT), pltpu.SemaphoreType.DMA((2,)),
                        pltpu.SemaphoreType.DMA((2,))])
    g2 = g.reshape(k * R, C)
    out, wire = pl.pallas_call(body, name=name, grid_spec=grid_spec,
                               out_shape=[_sds((k * half, C), F32), _sds((k * half, C), CDT)],
                               compiler_params=_params(2, PAIR_REDUCE_ID))(core, g2, g2)
    return out, wire.reshape(k, half, C)


def _all_reduce_small(v):
    r, cdim = v.shape

    def body(v_ref, o_ref, buf, send, recv):
        x, y, c = _place()
        me = 4 * x + 2 * y + c
        buf[me] = v_ref[...]
        sends = []
        for m in range(1, N_DEV):
            px = 1 - x if m & 4 else x
            py = 1 - y if m & 2 else y
            pc = 1 - c if m & 1 else c
            cp = pltpu.make_async_remote_copy(src_ref=v_ref, dst_ref=buf.at[me], send_sem=send.at[m - 1],
                                              recv_sem=recv.at[m - 1], device_id=(px, py, pc), device_id_type=MESH)
            cp.start()
            sends.append((cp, 4 * px + 2 * py + pc))
        for m, (cp, pidx) in enumerate(sends):
            pltpu.make_async_remote_copy(src_ref=v_ref, dst_ref=buf.at[pidx], send_sem=send.at[m], recv_sem=recv.at[m],
                                         device_id=(x, y, c), device_id_type=MESH).wait_recv()
        for cp, _ in sends:
            cp.wait_send()
        tot = buf[0]
        for k in range(1, N_DEV):
            tot = tot + buf[k]
        o_ref[...] = tot

    return pl.pallas_call(
        body, name="all_reduce_small", in_specs=[VMEM_SPEC], out_specs=VMEM_SPEC,
        out_shape=_sds((r, cdim), F32),
        scratch_shapes=[pltpu.VMEM((N_DEV, r, cdim), F32), pltpu.SemaphoreType.DMA((N_DEV - 1,)),
                        pltpu.SemaphoreType.DMA((N_DEV - 1,))],
    )(v)


def _row_tile(rows, cols, budget_elems=1 << 18, mult=8):
    if rows % mult:
        return rows
    t = max(mult, (budget_elems // cols) // mult * mult)
    while rows % t:
        t -= mult
    return t


def _adamw_update(w, g, m, v):
    nm = ADAM_B1 * m + (1.0 - ADAM_B1) * g
    nv = ADAM_B2 * v + (1.0 - ADAM_B2) * (g * g)
    m_hat = nm / (1.0 - ADAM_B1 ** ADAM_STEP)
    v_hat = nv / (1.0 - ADAM_B2 ** ADAM_STEP)
    return -ADAM_LR * (m_hat / (jnp.sqrt(v_hat) + ADAM_EPS) + ADAM_WD * w), nm, nv


def _adamw(name, w, g, m, v):
    R, C = w.shape
    tr = _row_tile(R, C, 1 << 18)

    def body(w_ref, g_ref, m_ref, v_ref, d_ref, nm_ref, nv_ref):
        d_ref[...], nm_ref[...], nv_ref[...] = _adamw_update(w_ref[...], g_ref[...], m_ref[...], v_ref[...])

    spec = pl.BlockSpec((tr, C), lambda i: (i, 0))
    return pl.pallas_call(body, name=name, grid=(R // tr,), in_specs=[spec] * 4, out_specs=[spec] * 3,
                          out_shape=[_sds((R, C), F32)] * 3, compiler_params=_params(1))(w, g, m, v)


def _sum_share(name, own, by_chip, chip, others, core):
    k, half, C = by_chip.shape
    tr = _row_tile(half, C, PAIR_TILE_ELEMS // 2, mult=16)
    nt = half // tr

    def body(chip_ref, oth_ref, c_ref, own_ref, a_ref, b_ref, cc_ref, g_out, mine, slot, send, recv):
        p = pl.program_id(1)
        _sibling_barrier((pl.program_id(0) == 0) & (p == 0))
        b = pl.program_id(0) % 2
        x, y, c = _place()
        cp = pltpu.make_async_remote_copy(src_ref=mine.at[b], dst_ref=slot.at[b], send_sem=send.at[b],
                                          recv_sem=recv.at[b], device_id=(x, y, 1 - c), device_id_type=MESH)

        @pl.when(p == 0)
        def _():
            tot = ((own_ref[...] + a_ref[...].astype(F32)) + b_ref[...].astype(F32)) + cc_ref[...].astype(F32)
            mine[b] = tot
            cp.start()
            g_out[...] = tot

        @pl.when(p == 1)
        def _():
            cp.wait_recv()
            g_out[...] = slot[b]
            cp.wait_send()

    def piece(j):
        return pl.BlockSpec((tr, C), lambda i, p, chip, oth, c: (oth[j] * nt + i, 0))

    grid_spec = pltpu.PrefetchScalarGridSpec(
        num_scalar_prefetch=3, grid=(nt, 2),
        in_specs=[pl.BlockSpec((tr, C), lambda i, p, chip, oth, c: (chip[0] * nt + i, 0)),
                  piece(0), piece(1), piece(2)],
        out_specs=pl.BlockSpec((tr, C), lambda i, p, chip, oth, c: (
            jnp.where(p == 0, c[0], 1 - c[0]) * nt + i, 0)),
        scratch_shapes=[pltpu.VMEM((2, tr, C), F32), pltpu.VMEM((2, tr, C), F32), pltpu.SemaphoreType.DMA((2,)),
                        pltpu.SemaphoreType.DMA((2,))])
    by2 = by_chip.reshape(k * half, C)
    return pl.pallas_call(body, name=name, grid_spec=grid_spec, out_shape=_sds((2 * half, C), F32),
                          compiler_params=_params(2, SUM_SHARE_ID))(chip, others, core, own, by2, by2, by2)


BIG = ("w_in", "w_proj_a", "w_proj_b", "w_out", "w_up", "w_down")
COL_SHARDED = ("w_in", "w_proj_a", "w_up")
SMALL = ("norm1_g", "q_norm_g", "k_norm_g", "ret_gn_g", "ret_gn_b", "norm2_g")
ALL_W = ("norm1_g", "w_in", "q_norm_g", "k_norm_g", "ret_gn_g", "ret_gn_b", "w_proj_a", "w_proj_b", "w_out",
         "norm2_g", "w_up", "w_down")
LANES = 128


def _to_full(name, gathered):
    k, r, c = gathered.shape
    if name in COL_SHARDED:
        return gathered.transpose(1, 0, 2).reshape(r, k * c)
    return gathered.reshape(k * r, c)


def _to_shard_major(name, full):
    if name in COL_SHARDED:
        r, c4 = full.shape
        return full.reshape(r, N_CHIPS, c4 // N_CHIPS).transpose(1, 0, 2)
    r4, c = full.shape
    return full.reshape(N_CHIPS, r4 // N_CHIPS, c)


def kernel(x, norm1_g, w_in, q_norm_g, k_norm_g, ret_gn_g, ret_gn_b, w_proj_a, w_proj_b, w_out, norm2_g, w_up, w_down, loss_target, m_norm1_g, m_w_in, m_q_norm_g, m_k_norm_g, m_ret_gn_g, m_ret_gn_b, m_w_proj_a, m_w_proj_b, m_w_out, m_norm2_g, m_w_up, m_w_down, v_norm1_g, v_w_in, v_q_norm_g, v_k_norm_g, v_ret_gn_g, v_ret_gn_b, v_w_proj_a, v_w_proj_b, v_w_out, v_norm2_g, v_w_up, v_w_down):
    weights = dict(norm1_g=norm1_g, w_in=w_in, q_norm_g=q_norm_g, k_norm_g=k_norm_g, ret_gn_g=ret_gn_g,
                   ret_gn_b=ret_gn_b, w_proj_a=w_proj_a, w_proj_b=w_proj_b, w_out=w_out, norm2_g=norm2_g,
                   w_up=w_up, w_down=w_down)
    moments_m = dict(norm1_g=m_norm1_g, w_in=m_w_in, q_norm_g=m_q_norm_g, k_norm_g=m_k_norm_g, ret_gn_g=m_ret_gn_g,
                     ret_gn_b=m_ret_gn_b, w_proj_a=m_w_proj_a, w_proj_b=m_w_proj_b, w_out=m_w_out,
                     norm2_g=m_norm2_g, w_up=m_w_up, w_down=m_w_down)
    moments_v = dict(norm1_g=v_norm1_g, w_in=v_w_in, q_norm_g=v_q_norm_g, k_norm_g=v_k_norm_g, ret_gn_g=v_ret_gn_g,
                     ret_gn_b=v_ret_gn_b, w_proj_a=v_w_proj_a, w_proj_b=v_w_proj_b, w_out=v_w_out,
                     norm2_g=v_norm2_g, w_up=v_w_up, w_down=v_w_down)

    mx, my = lax.axis_index("x"), lax.axis_index("y")
    core = lax.axis_index("c").astype(jnp.int32).reshape(1)
    chip = (2 * mx + my).astype(jnp.int32).reshape(1)
    others = jnp.stack([2 * (1 - mx) + my, 2 * mx + 1 - my, 2 * (1 - mx) + 1 - my]).astype(jnp.int32)
    shards = {n: weights[n][0].astype(CDT) for n in BIG}
    def start_gather(name, names):
        return _exchange_start(name, "gather", [shards[n] for n in names],
                               [((N_CHIPS,) + shards[n].shape, CDT) for n in names])

    w_in_shape = [((N_CHIPS,) + shards["w_in"].shape, CDT)]
    n_send, n_recv, n_srcs, n_lands, n_token = _exchange_start(
        "gather_w_in_near_start", "gather", [shards["w_in"]], w_in_shape, which=(0, 1))
    late = [n for n in BIG if n != "w_in"]
    flight = {}

    def near_w_in(after):
        srcs, lands = _exchange_wait("gather_w_in_near_wait", "gather", n_send, n_recv, n_srcs, n_lands, after,
                                     which=(0, 1))
        d_send, d_recv, d_srcs, d_lands, _ = _exchange_start(
            "gather_w_in_diag_start", "gather", srcs, w_in_shape, which=(2,), lands=lands)
        flight["late"] = start_gather("gather_late_start", late)
        w_near = _pair_fill("pair_fill_w_in_near", d_lands[0], d_srcs[0], core, others[:2], chip)
        flight["diag"] = (d_send, d_recv, d_srcs, [w_near])
        return w_near, flight["late"][-1]

    def far_w_in(after):
        d_send, d_recv, d_srcs, d_lands = flight["diag"]
        srcs, lands = _exchange_wait("gather_w_in_diag_wait", "gather", d_send, d_recv, d_srcs, d_lands, after,
                                     which=(2,))
        return _pair_fill("pair_fill_w_in_diag", lands[0], srcs[0], core, others[2:], chip, write_mine=False), None

    def late_weights(after):
        l_send, l_recv, l_srcs, l_lands, _ = flight["late"]
        srcs, lands = _exchange_wait("gather_late_wait", "gather", l_send, l_recv, l_srcs, l_lands, after)
        out = {}
        for n, mine, land in zip(late, srcs, lands):
            out[n] = _to_full(n, _pair_fill("pair_fill_%s" % n, land, mine, core, others, chip))
        return out

    pending = []

    def on_grads(group):
        names = list(group)
        red = [_pair_reduce("pair_reduce_%s" % n, g if g.ndim == 3 else _to_shard_major(n, g), core)
               for n, g in group.items()]
        wires = [wire for _, wire in red]
        send, recv, srcs, lands, token = _exchange_start(
            "scatter_start_%s" % names[0], "scatter", wires, [(wire.shape, wire.dtype) for wire in wires])
        pending.append((names, [own for own, _ in red], send, recv, srcs, lands))
        return token

    small = {n: weights[n].reshape(1, -1) for n in SMALL}

    loss, grad_x, small_g = _local_step(x[0], loss_target[0], n_srcs[0], chip, others, near_w_in, far_w_in, small,
                                        late_weights, on_grads, deps0=[n_token])

    out_g, out_d, out_m, out_v = {}, {}, {}, {}
    for names, owns, send, recv, srcs, lands in pending:
        _, got = _exchange_wait("scatter_wait_%s" % names[0], "scatter", send, recv, srcs, lands, grad_x)
        for n, own, by_chip in zip(names, owns, got):
            shape = weights[n].shape
            g2 = _sum_share("sum_share_%s" % n, own, by_chip, chip, others, core)
            d, nm, nv = _adamw("adamw_%s" % n, weights[n][0], g2, moments_m[n][0], moments_v[n][0])
            out_g[n], out_d[n], out_m[n], out_v[n] = (t.reshape(shape) for t in (g2, d, nm, nv))

    packed = jnp.concatenate([small_g[n].reshape(1, -1) for n in SMALL], axis=1).reshape(-1, LANES)
    loss_tile = jnp.zeros((8, LANES), F32).at[0, 0].set(loss)
    red = _all_reduce_small(jnp.concatenate([packed, loss_tile], axis=0))
    loss = red[packed.shape[0], 0]
    red = red[:packed.shape[0]].reshape(1, -1)
    off = 0
    for n in SMALL:
        shape = weights[n].shape
        row = (1, weights[n].size)
        g2 = red[:, off:off + row[1]]
        off += row[1]
        d, nm, nv = _adamw("adamw_%s" % n, weights[n].reshape(row), g2, moments_m[n].reshape(row),
                           moments_v[n].reshape(row))
        out_g[n], out_d[n], out_m[n], out_v[n] = (t.reshape(shape) for t in (g2, d, nm, nv))

    return (loss, grad_x[None], *[out_g[n] for n in ALL_W], *[out_d[n] for n in ALL_W],
            *[out_m[n] for n in ALL_W], *[out_v[n] for n in ALL_W])
```

```python
import functools

import jax
import jax.numpy as jnp
from jax import lax
from jax.experimental import pallas as pl
from jax.experimental.pallas import tpu as pltpu

CDT = jnp.bfloat16
F32 = jnp.float32
EPS = 1e-6

ATT_GROUPS = ((128, 1), (512, 4), (2048, 16))
ATT_HPG = 4
ATT_HEADS = 12
HD = 128
BLK = 128
ATT_W = ATT_HEADS * HD
GW = ATT_HPG * HD
RET_HEADS = 4

ADAM_LR = 0.001
ADAM_B1 = 0.9
ADAM_B2 = 0.999
ADAM_EPS = 1e-08
ADAM_WD = 0.01
ADAM_STEP = 10

VMEM_LIMIT_BYTES = 56 * 1024 * 1024
MESH = pl.DeviceIdType.MESH
HBM_SPEC = pl.BlockSpec(memory_space=pltpu.HBM)
VMEM_SPEC = pl.BlockSpec(memory_space=pltpu.VMEM)


def _params(n_axes, collective_id=None):
    return pltpu.CompilerParams(dimension_semantics=("arbitrary",) * n_axes,
                                vmem_limit_bytes=VMEM_LIMIT_BYTES, collective_id=collective_id)


PAIR_FILL_ID, PAIR_REDUCE_ID, SUM_SHARE_ID = 1, 2, 3


def _sibling_barrier(first_step):
    @pl.when(first_step)
    def _():
        sem = pltpu.get_barrier_semaphore()
        x, y, c = lax.axis_index("x"), lax.axis_index("y"), lax.axis_index("c")
        pl.semaphore_signal(sem, inc=1, device_id=(x, y, 1 - c), device_id_type=pl.DeviceIdType.MESH)
        pl.semaphore_wait(sem, 1)


def _dot_nn(a, b):
    return jnp.dot(a, b, preferred_element_type=F32)


def _dot_nt(a, b):
    return lax.dot_general(a, b, (((1,), (1,)), ((), ())), preferred_element_type=F32)


def _dot_tn(a, b):
    return lax.dot_general(a, b, (((0,), (0,)), ((), ())), preferred_element_type=F32)


def _sigmoid(v):
    return 1.0 / (1.0 + jnp.exp(-v))


def _matmul(name, mode, a, b, *, tm, tn, tk, extras=(), outs, epilogue, deps=(), b_spec=None, n_cols=None,
            prefetch=(), alias_dep_to_out=None, j_outer=False):
    deps = [d for d in deps if d is not None]
    if mode == "tn":
        K, M = a.shape
    else:
        M, K = a.shape
    if b_spec is None:
        (N, K2) = b.shape if mode == "nt" else b.shape[::-1]
        assert K == K2, (name, a.shape, b.shape)
        if mode == "nt":
            b_spec = pl.BlockSpec((tn, tk), lambda i, j, k, *p: (j, k))
        else:
            b_spec = pl.BlockSpec((tk, tn), lambda i, j, k, *p: (k, j))
    else:
        N = n_cols
    assert M % tm == 0 and N % tn == 0 and K % tk == 0, (name, a.shape, b.shape)
    ni, nj, nk = M // tm, N // tn, K // tk
    if mode == "tn":
        a_spec = pl.BlockSpec((tk, tm), lambda i, j, k, *p: (k, i))
    else:
        a_spec = pl.BlockSpec((tm, tk), lambda i, j, k, *p: (i, k))
    dot = {"nn": _dot_nn, "nt": _dot_nt, "tn": _dot_tn}[mode]
    n_ex, n_out, n_dep, n_pre = len(extras), len(outs), len(deps), len(prefetch)
    grid = (ni, nj, nk)
    if j_outer:
        grid = (nj, ni, nk)

        def swapped(spec):
            return pl.BlockSpec(spec.block_shape, lambda j, i, k, *p: spec.index_map(i, j, k, *p))

        a_spec, b_spec = swapped(a_spec), swapped(b_spec)
        extras = [(e, swapped(s)) for e, s in extras]
        outs = [(o, swapped(s)) for o, s in outs]

    def body(*refs):
        refs = refs[n_pre:]
        a_ref, b_ref = refs[0], refs[1]
        ex = refs[2:2 + n_ex]
        out = refs[2 + n_ex + n_dep:2 + n_ex + n_dep + n_out]
        acc = refs[-1] if nk > 1 else None
        i = pl.program_id(1 if j_outer else 0)
        k = pl.program_id(2)
        if nk == 1:
            epilogue(dot(a_ref[...].astype(CDT), b_ref[...].astype(CDT)), ex, out, i)
            return

        @pl.when(k == 0)
        def _():
            acc[...] = jnp.zeros_like(acc)

        acc[...] += dot(a_ref[...].astype(CDT), b_ref[...].astype(CDT))

        @pl.when(k == nk - 1)
        def _():
            epilogue(acc[...], ex, out, i)

    grid_spec = pltpu.PrefetchScalarGridSpec(
        num_scalar_prefetch=n_pre, grid=grid,
        in_specs=[a_spec, b_spec] + [s for _, s in extras] + [pl.BlockSpec(memory_space=pl.ANY)] * n_dep,
        out_specs=[s for _, s in outs],
        scratch_shapes=[pltpu.VMEM((tm, tn), F32)] if nk > 1 else [])
    aliases = {}
    if alias_dep_to_out is not None:
        aliases = {n_pre + 2 + n_ex + alias_dep_to_out[0]: alias_dep_to_out[1]}
    res = pl.pallas_call(
        body, name=name, grid_spec=grid_spec, out_shape=[o for o, _ in outs], input_output_aliases=aliases,
        compiler_params=_params(3),
    )(*prefetch, a, b, *[e for e, _ in extras], *deps)
    return res


def _mn(tm, tn, col_off=0):
    return pl.BlockSpec((tm, tn), lambda i, j, k, *p: (i, j + col_off))


def _row(tn):
    return pl.BlockSpec((1, tn), lambda i, j, k, *p: (0, j))


def _ep_store(acc, ex, out, i):
    out[0][...] = acc.astype(out[0].dtype)


def _ep_resid_norm(acc, ex, out, i):
    x1 = ex[0][...] + acc
    out[0][...] = x1
    rstd = lax.rsqrt(jnp.mean(x1 * x1, axis=-1, keepdims=True) + EPS)
    out[1][...] = (x1 * rstd * ex[1][...]).astype(out[1].dtype)


def _ep_up(acc, ex, out, i):
    out[0][...] = acc.astype(out[0].dtype)
    r = jnp.maximum(acc, 0.0)
    out[1][...] = (r * r).astype(out[1].dtype)


def _ep_down_loss(acc, ex, out, i, inv_d):
    diff = (ex[0][...] + acc) - ex[1][...]
    dx2 = diff * inv_d
    out[0][...] = dx2
    out[1][...] = dx2.astype(out[1].dtype)

    @pl.when(i == 0)
    def _():
        out[2][...] = jnp.zeros_like(out[2])

    out[2][...] += jnp.sum(diff * diff, axis=0, keepdims=True)


def _ep_dh(acc, ex, out, i):
    h = ex[0][...].astype(F32)
    out[0][...] = (acc * (2.0 * jnp.maximum(h, 0.0))).astype(out[0].dtype)


def _ep_rms_bwd(acc, ex, out, i):
    x = ex[0][...]
    g = ex[1][...]
    rstd = lax.rsqrt(jnp.mean(x * x, axis=-1, keepdims=True) + EPS)
    xh = x * rstd
    dxh = acc * g
    dx = ex[2][...] + rstd * (dxh - xh * jnp.mean(dxh * xh, axis=-1, keepdims=True))
    out[0][...] = dx
    for copy in out[1:-1]:
        copy[...] = dx.astype(copy.dtype)
    dg = out[-1]

    @pl.when(i == 0)
    def _():
        dg[...] = jnp.zeros_like(dg)

    dg[...] += jnp.sum(acc * xh, axis=0, keepdims=True)


def _ep_gates(acc, ex, out, i):
    sa = _sigmoid(ex[0][...].astype(F32))
    sb = _sigmoid(ex[1][...].astype(F32))
    dpa = acc * sa
    dpb = acc * sb
    out[0][...] = dpa.astype(out[0].dtype)
    out[1][...] = dpb.astype(out[1].dtype)
    out[2][...] = (dpa * ex[2][...].astype(F32) * (1.0 - sa)).astype(out[2].dtype)
    out[3][...] = (dpb * ex[3][...].astype(F32) * (1.0 - sb)).astype(out[3].dtype)


def _sds(shape, dtype):
    return jax.ShapeDtypeStruct(shape, dtype)


def _in_proj_mine(x, g, w_mine, chip, proj_sds, deps, tm=512):
    S, D = x.shape
    ns = w_mine.shape[1]
    deps = [d for d in deps if d is not None]

    def body(c_ref, x_ref, g_ref, w_ref, *rest):
        xn_ref, proj_ref = rest[len(deps)], rest[len(deps) + 1]
        xv = x_ref[...]
        rstd = lax.rsqrt(jnp.mean(xv * xv, axis=-1, keepdims=True) + EPS)
        xn = (xv * rstd * g_ref[...]).astype(xn_ref.dtype)
        xn_ref[...] = xn
        proj_ref[...] = _dot_nn(xn, w_ref[...]).astype(proj_ref.dtype)

    grid_spec = pltpu.PrefetchScalarGridSpec(
        num_scalar_prefetch=1, grid=(S // tm,),
        in_specs=[pl.BlockSpec((tm, D), lambda i, c: (i, 0)), pl.BlockSpec((1, D), lambda i, c: (0, 0)),
                  pl.BlockSpec((D, ns), lambda i, c: (0, 0))] + [pl.BlockSpec(memory_space=pl.ANY)] * len(deps),
        out_specs=[pl.BlockSpec((tm, D), lambda i, c: (i, 0)), pl.BlockSpec((tm, ns), lambda i, c: (i, c[0]))])
    return pl.pallas_call(body, name="in_proj_mine", grid_spec=grid_spec, out_shape=[_sds((S, D), CDT), proj_sds],
                          compiler_params=_params(1))(chip, x, g, w_mine, *deps)


def _rm_shape(S, d, width):
    return (S, width) if d == 1 else (d, S // d, width)


def _rm_spec(tm, d, width):
    if d == 1:
        return pl.BlockSpec((tm, width), lambda i: (i, 0))
    return pl.BlockSpec((d, tm // d, width), lambda i: (0, i, 0))


def _rm_put(dst_ref, cols, buf_ref, d):
    if d == 1:
        dst_ref[:, cols] = buf_ref[...].astype(dst_ref.dtype)
        return
    m = buf_ref.shape[0] // d
    for r in range(d):
        dst_ref[r, :, cols] = buf_ref[pl.ds(r, m, stride=d), :].astype(dst_ref.dtype)


def _rm_reader(buf_ref, src_ref, d):
    if d == 1:
        return lambda s, rows: src_ref[rows, s * HD:(s + 1) * HD].astype(F32)
    m = buf_ref.shape[1] // d
    for s in range(buf_ref.shape[0]):
        for r in range(d):
            buf_ref.at[s][pl.ds(r, m, stride=d), :] = src_ref[r, :, s * HD:(s + 1) * HD].astype(F32)
    return lambda s, rows: buf_ref.at[s][rows, :]


def _qknorm_fwd(proj, gqk, tm=512):
    S = proj.shape[0]
    W = 2 * ATT_W
    dil = [d for _, d in ATT_GROUPS]

    def body(p_ref, g_ref, o0, o1, o2, buf):
        outs = (o0, o1, o2)
        for hd in range(3 * ATT_HEADS):
            which, head = hd // ATT_HEADS, hd % ATT_HEADS
            grp, slot = head // ATT_HPG, head % ATT_HPG
            cols = slice(hd * HD, (hd + 1) * HD)

            def chunk(rows, which=which, cols=cols):
                v = p_ref[rows, cols].astype(F32)
                if which < 2:
                    rstd = lax.rsqrt(jnp.mean(v * v, axis=-1, keepdims=True) + EPS)
                    v = v * rstd * g_ref[:, cols]
                buf[rows, :] = v

            chunk(slice(None))
            _rm_put(outs[grp], slice(which * GW + slot * HD, which * GW + (slot + 1) * HD), buf, dil[grp])

    return pl.pallas_call(
        body, name="qknorm_fwd", grid=(S // tm,),
        in_specs=[pl.BlockSpec((tm, 3 * ATT_W), lambda i: (i, 0)), pl.BlockSpec((1, W), lambda i: (0, 0))],
        out_specs=[_rm_spec(tm, d, 3 * GW) for d in dil],
        out_shape=[_sds(_rm_shape(S, d, 3 * GW), CDT) for d in dil],
        scratch_shapes=[pltpu.VMEM((tm, HD), F32)],
        compiler_params=_params(1))(proj, gqk)


def _qknorm_bwd(proj, gqk, dqs, dks, dvs, dproj, tm=256):
    S = proj.shape[0]
    W = 2 * ATT_W
    dil = [d for _, d in ATT_GROUPS]

    def body(p_ref, g_ref, *refs):
        ins = refs[0:9]
        o_ref, dg_ref = refs[10], refs[11]
        bufs = refs[12:21]
        i = pl.program_id(0)

        @pl.when(i == 0)
        def _():
            dg_ref[...] = jnp.zeros_like(dg_ref)

        nat = [_rm_reader(bufs[j], ins[j], dil[j % 3]) for j in range(9)]
        dq_get, dk_get, dv_get = nat[0:3], nat[3:6], nat[6:9]
        for hd in range(2 * ATT_HEADS):
            sl = slice(hd * HD, (hd + 1) * HD)
            head = hd % ATT_HEADS
            grp, slot = head // ATT_HPG, head % ATT_HPG
            get = (dq_get if hd < ATT_HEADS else dk_get)[grp]

            def chunk(rows, sl=sl, slot=slot, get=get):
                dn = get(slot, rows)
                v = p_ref[rows, sl].astype(F32)
                rstd = lax.rsqrt(jnp.mean(v * v, axis=-1, keepdims=True) + EPS)
                vh = v * rstd
                dg_ref[:, sl] += jnp.sum(dn * vh, axis=0, keepdims=True)
                dvh = dn * g_ref[:, sl]
                o_ref[rows, sl] = (rstd * (dvh - vh * jnp.mean(dvh * vh, axis=-1, keepdims=True))).astype(o_ref.dtype)

            chunk(slice(None))
        for head in range(ATT_HEADS):
            grp, slot = head // ATT_HPG, head % ATT_HPG
            o_ref[:, W + head * HD:W + (head + 1) * HD] = dv_get[grp](slot, slice(None)).astype(o_ref.dtype)

    return pl.pallas_call(
        body, name="qknorm_bwd", grid=(S // tm,),
        in_specs=[pl.BlockSpec((tm, W), lambda i: (i, 0)), pl.BlockSpec((1, W), lambda i: (0, 0))]
        + [_rm_spec(tm, d, GW) for d in dil] * 3 + [pl.BlockSpec(memory_space=pl.ANY)],
        out_specs=[pl.BlockSpec((tm, 3 * ATT_W), lambda i: (i, 0)), pl.BlockSpec((1, W), lambda i: (0, 0))],
        out_shape=[_sds(dproj.shape, dproj.dtype), _sds((1, W), F32)],
        scratch_shapes=[pltpu.VMEM((ATT_HPG, tm, HD), F32)] * 9,
        input_output_aliases={11: 0},
        compiler_params=_params(1))(proj, gqk, *dqs, *dks, *dvs, dproj)


def _att_mask(n):
    qi = lax.broadcasted_iota(jnp.int32, (BLK, 2 * BLK), 0)
    kj = lax.broadcasted_iota(jnp.int32, (BLK, 2 * BLK), 1)
    dist = BLK + qi - kj
    valid_all = (dist >= 0) & (dist <= BLK)
    return valid_all & ((kj >= BLK) | (n > 0)), valid_all, dist.astype(F32)


def _att_slopes(grp):
    return [2.0 ** (-8.0 * (grp * ATT_HPG + hh + 1) / ATT_HEADS) for hh in range(ATT_HPG)]


ATT_PLANES_PER_STEP = 4


def _att_planes(d):
    return d if d > 1 else ATT_PLANES_PER_STEP


def _att_3d(a, d):
    return a.reshape(ATT_PLANES_PER_STEP, a.shape[0] // ATT_PLANES_PER_STEP, a.shape[1]) if d == 1 else a


def _att_spec(row_fn, col=0):
    return pl.BlockSpec((ATT_PLANES_PER_STEP, BLK, GW), lambda r, n: (r, row_fn(n), col))


def _att_chains():
    return [(rr * ATT_HPG + hh, rr, slice(hh * HD, (hh + 1) * HD), hh)
            for rr in range(ATT_PLANES_PER_STEP) for hh in range(ATT_HPG)]


def _att_qkv_specs(nb):
    last = nb - 1

    def cur(n):
        return jnp.minimum(n, last)

    def prev(n):
        return jnp.maximum(jnp.minimum(n, last) - 1, 0)

    return [_att_spec(cur, 0), _att_spec(prev, 1), _att_spec(cur, 1), _att_spec(prev, 2), _att_spec(cur, 2),
            _att_spec(lambda n: last, 1), _att_spec(lambda n: last, 2)]


def _att_prev(seg, n, prev_ref, last_ref, rr, sl):
    t = prev_ref[rr, :, sl]
    if seg and rr > 0:
        t = jnp.where(n == 0, last_ref[rr - 1, :, sl], t)
    return t


def _att_fwd(grp, S, qkv):
    _, d = ATT_GROUPS[grp]
    seg = d == 1
    P = _att_planes(d)
    L = S // P
    nb = L // BLK
    assert P % ATT_PLANES_PER_STEP == 0 and (not seg or P == ATT_PLANES_PER_STEP)
    slopes = _att_slopes(grp)
    scale = HD ** -0.5
    chains = _att_chains()

    def body(q_ref, kp_ref, kc_ref, vp_ref, vc_ref, kl_ref, vl_ref, o_ref, l_ref, s_buf, p_buf, den_buf):
        n = pl.program_id(1)
        valid, valid_all, distf = _att_mask(n)
        for c, rr, sl, hh in chains:
            k = jnp.concatenate([_att_prev(seg, n, kp_ref, kl_ref, rr, sl), kc_ref[rr, :, sl]], axis=0)
            s_buf[c] = _dot_nt(q_ref[rr, :, sl], k)
        for c, rr, sl, hh in chains:
            s = s_buf[c] * scale + (-slopes[hh] * d) * distf
            s = jnp.where(valid_all if seg and rr > 0 else valid, s, -1e30)
            m = jnp.max(s, axis=-1, keepdims=True)
            p = jnp.exp(s - m)
            den = jnp.sum(p, axis=-1, keepdims=True)
            p_buf[c] = p.astype(CDT)
            den_buf[c] = jnp.broadcast_to(den, (BLK, HD))
            l_ref[rr, :, sl] = jnp.broadcast_to(m + jnp.log(den), (BLK, HD))
        for c, rr, sl, hh in chains:
            v = jnp.concatenate([_att_prev(seg, n, vp_ref, vl_ref, rr, sl), vc_ref[rr, :, sl]], axis=0)
            o_ref[rr, :, sl] = _dot_nn(p_buf[c], v) / den_buf[c]

    out_spec = _att_spec(lambda n: n)
    n_ch = len(chains)
    q3 = _att_3d(qkv, d)
    o, l = pl.pallas_call(
        body, name="att_fwd_g%d" % grp, grid=(P // ATT_PLANES_PER_STEP, nb),
        in_specs=_att_qkv_specs(nb),
        out_specs=[out_spec, out_spec],
        out_shape=[_sds((P, L, GW), F32)] * 2,
        scratch_shapes=[pltpu.VMEM((n_ch, BLK, 2 * BLK), F32), pltpu.VMEM((n_ch, BLK, 2 * BLK), CDT),
                        pltpu.VMEM((n_ch, BLK, HD), F32)],
        compiler_params=_params(2),
    )(*[q3] * 7)
    return o.reshape(_rm_shape(S, d, GW)), l.reshape(_rm_shape(S, d, GW))


def _att_bwd(grp, S, qkv, lse, do_g, c_g):
    _, d = ATT_GROUPS[grp]
    seg = d == 1
    P = _att_planes(d)
    L = S // P
    nb = L // BLK
    assert P % ATT_PLANES_PER_STEP == 0 and (not seg or P == ATT_PLANES_PER_STEP)
    slopes = _att_slopes(grp)
    scale = HD ** -0.5
    last = nb - 1
    chains = _att_chains()

    def body(q_ref, kp_ref, kc_ref, vp_ref, vc_ref, kl_ref, vl_ref, l_ref, do_ref, c_ref, dq_ref, dk_ref, dv_ref,
             ck, cv, fk, fv, s_buf, dp_buf, p_buf, ds_buf, k_buf):
        n = pl.program_id(1)

        @pl.when(n == 0)
        def _():
            for buf in (ck, cv, fk, fv):
                buf[...] = jnp.zeros_like(buf)

        @pl.when(n < nb)
        def _():
            valid, valid_all, distf = _att_mask(n)
            for c, rr, sl, hh in chains:
                k = jnp.concatenate([_att_prev(seg, n, kp_ref, kl_ref, rr, sl), kc_ref[rr, :, sl]], axis=0)
                v = jnp.concatenate([_att_prev(seg, n, vp_ref, vl_ref, rr, sl), vc_ref[rr, :, sl]], axis=0)
                k_buf[c] = k
                s_buf[c] = _dot_nt(q_ref[rr, :, sl], k)
                dp_buf[c] = _dot_nt(do_ref[rr, :, sl], v)
            for c, rr, sl, hh in chains:
                s = s_buf[c] * scale + (-slopes[hh] * d) * distf
                p = jnp.where(valid_all if seg and rr > 0 else valid, jnp.exp(s - l_ref[rr, :, sl][:, 0:1]), 0.0)
                p_buf[c] = p.astype(CDT)
                ds_buf[c] = (p * (dp_buf[c] + c_ref[rr, :, sl][:, 0:1]) * scale).astype(CDT)
            for c, rr, sl, hh in chains:
                ds = ds_buf[c]
                dq_ref[rr, :, sl] = _dot_nn(ds, k_buf[c])
                dk = _dot_tn(ds, q_ref[rr, :, sl])
                dv = _dot_tn(p_buf[c], do_ref[rr, :, sl])
                dk_ref[rr, :, sl] = ck[rr, :, sl] + dk[0:BLK]
                dv_ref[rr, :, sl] = cv[rr, :, sl] + dv[0:BLK]
                ck[rr, :, sl] = dk[BLK:2 * BLK]
                cv[rr, :, sl] = dv[BLK:2 * BLK]
                if seg and rr > 0:
                    @pl.when(n == 0)
                    def _(rr=rr, sl=sl, dk=dk, dv=dv):
                        fk[rr - 1, :, sl] = dk[0:BLK]
                        fv[rr - 1, :, sl] = dv[0:BLK]

        @pl.when(n == nb)
        def _():
            dk_ref[...] = ck[...] + fk[...]
            dv_ref[...] = cv[...] + fv[...]

    blk = (ATT_PLANES_PER_STEP, BLK, GW)
    at_q = _att_spec(lambda n: jnp.minimum(n, last))
    behind = _att_spec(lambda n: jnp.maximum(n - 1, 0))
    n_ch = len(chains)
    q3 = _att_3d(qkv, d)
    res = pl.pallas_call(
        body, name="att_bwd_g%d" % grp, grid=(P // ATT_PLANES_PER_STEP, nb + 1),
        in_specs=_att_qkv_specs(nb) + [at_q, at_q, at_q],
        out_specs=[at_q, behind, behind],
        out_shape=[_sds((P, L, GW), F32)] * 3,
        scratch_shapes=[pltpu.VMEM(blk, F32)] * 4
        + [pltpu.VMEM((n_ch, BLK, 2 * BLK), F32), pltpu.VMEM((n_ch, BLK, 2 * BLK), F32),
           pltpu.VMEM((n_ch, BLK, 2 * BLK), CDT), pltpu.VMEM((n_ch, BLK, 2 * BLK), CDT),
           pltpu.VMEM((n_ch, 2 * BLK, HD), CDT)],
        compiler_params=_params(2),
    )(*[q3] * 7, _att_3d(lse, d), _att_3d(do_g, d), _att_3d(c_g, d))
    return [t.reshape(_rm_shape(S, d, GW)) for t in res]


def _mix_alpha(l0, l1, l2):
    mx = jnp.maximum(jnp.maximum(l0, l1), l2)
    e = [jnp.exp(l0 - mx), jnp.exp(l1 - mx), jnp.exp(l2 - mx)]
    tot = e[0] + e[1] + e[2]
    return [ei / tot for ei in e]


def _mix_fwd(S, os_, ls_, tm=512):
    dil = [d for _, d in ATT_GROUPS]

    def body(*refs):
        out, bufs = refs[6], refs[7:13]
        get = [_rm_reader(bufs[j], refs[j], dil[j % 3]) for j in range(6)]
        rows = slice(None)
        for s in range(ATT_HPG):
            al = _mix_alpha(*[get[3 + g](s, rows) for g in range(3)])
            mixed = al[0] * get[0](s, rows) + al[1] * get[1](s, rows) + al[2] * get[2](s, rows)
            out[:, s * HD:(s + 1) * HD] = mixed.astype(out.dtype)

    specs = [_rm_spec(tm, d, GW) for d in dil]
    return pl.pallas_call(
        body, name="mix_fwd", grid=(S // tm,), in_specs=specs * 2, out_specs=pl.BlockSpec((tm, GW), lambda i: (i, 0)),
        out_shape=_sds((S, GW), CDT), scratch_shapes=[pltpu.VMEM((ATT_HPG, tm, HD), F32)] * 6,
        compiler_params=_params(1))(*os_, *ls_)


def _mix_bwd(S, os_, ls_, do_a, tm=512):
    dil = [d for _, d in ATT_GROUPS]

    def body(*refs):
        d_ref, outs, bufs, tmps = refs[6], refs[7:13], refs[13:19], refs[19:25]
        get = [_rm_reader(bufs[j], refs[j], dil[j % 3]) for j in range(6)]
        for s in range(ATT_HPG):
            cols = slice(s * HD, (s + 1) * HD)

            def chunk(rows, s=s, cols=cols):
                al = _mix_alpha(*[get[3 + g](s, rows) for g in range(3)])
                dv = d_ref[rows, cols]
                o_a = al[0] * get[0](s, rows) + al[1] * get[1](s, rows) + al[2] * get[2](s, rows)
                dsum = jnp.sum(dv * o_a, axis=-1, keepdims=True)
                for g in range(3):
                    tmps[g][rows, :] = al[g] * dv
                    tmps[3 + g][rows, :] = -(al[g] * dsum)

            chunk(slice(None))
            for j in range(6):
                _rm_put(outs[j], cols, tmps[j], dil[j % 3])

    specs = [_rm_spec(tm, d, GW) for d in dil]
    res = pl.pallas_call(
        body, name="mix_bwd", grid=(S // tm,), in_specs=specs * 2 + [pl.BlockSpec((tm, GW), lambda i: (i, 0))],
        out_specs=specs * 2,
        out_shape=[_sds(_rm_shape(S, d, GW), CDT) for d in dil] + [_sds(_rm_shape(S, d, GW), F32) for d in dil],
        scratch_shapes=[pltpu.VMEM((ATT_HPG, tm, HD), F32)] * 6 + [pltpu.VMEM((tm, HD), F32)] * 6,
        compiler_params=_params(1))(*os_, *ls_, do_a)
    return res[:3], res[3:]


def _ret_tables(dk):
    H, C = RET_HEADS, BLK
    log_g = jnp.log(1.0 - 2.0 ** (-5.0 - jnp.arange(H, dtype=F32)))
    idx = jnp.arange(C, dtype=F32)
    diff = idx[:, None] - idx[None, :]
    decay = jnp.where(diff >= 0, jnp.exp(log_g[:, None, None] * jnp.maximum(diff, 0.0)), 0.0)
    xi = jnp.exp(log_g[:, None] * (idx[None, :] + 1.0))
    zeta = jnp.exp(log_g[:, None] * (C - 1.0 - idx[None, :])) * (dk ** -0.5)
    g_chunk = jnp.exp(log_g * C)
    bc = lambda t: jnp.broadcast_to(t[:, :, None], (H, C, C))
    return decay, bc(xi), bc(zeta), jnp.broadcast_to(g_chunk[:, None, None], (H, 8, C))


def _gn_fwd(o, g, b):
    mu = jnp.mean(o, axis=-1, keepdims=True)
    xc = o - mu
    rstd = lax.rsqrt(jnp.mean(xc * xc, axis=-1, keepdims=True) + EPS)
    yh = xc * rstd
    return yh, rstd, yh * g + b


def _ret_specs(dk, dv, order):
    H = RET_HEADS
    qk_w, v_w = H * dk, H * dv
    off_q = 3 * ATT_W
    off_k, off_v, off_g = off_q + qk_w, off_q + 2 * qk_w, off_q + 2 * qk_w + v_w
    assert 2 * dk == dv and all(off % dv == 0 for off in (off_q, off_k, off_v, off_g))

    def col(off, j):
        return pl.BlockSpec((BLK, dv), lambda i: (order(i), off // dv + j))

    tab = pl.BlockSpec((H, BLK, BLK), lambda i: (0, 0, 0))
    return ([col(off_q, j) for j in range(H // 2)] + [col(off_k, j) for j in range(H // 2)]
            + [col(off_v, j) for j in range(H)] + [col(off_g, j) for j in range(H)]
            + [tab, tab, tab, pl.BlockSpec((H, 8, BLK), lambda i: (0, 0, 0))])


def _ret_heads(refs, dk):
    H = RET_HEADS
    q_refs, k_refs = refs[0:H // 2], refs[H // 2:H]
    v_refs, gr_refs = refs[H:2 * H], refs[2 * H:3 * H]

    def head(h):
        cols = slice((h % 2) * dk, (h % 2 + 1) * dk)
        return q_refs[h // 2][:, cols], k_refs[h // 2][:, cols], v_refs[h][...], gr_refs[h][...]

    return head, refs[3 * H:3 * H + 4]


def _ret_fwd(proj, gn_g, gn_b, dk, dv):
    S = proj.shape[0]
    N = S // BLK
    H = RET_HEADS
    kscale = dk ** -0.5
    n_in = 3 * H + 4

    def body(*refs):
        head, (dec_ref, xi_ref, zeta_ref, gc_ref) = _ret_heads(refs, dk)
        g_ref, b_ref, opre_ref, or_ref, st_ref, state, s_buf, cross_buf = refs[n_in:n_in + 8]
        n = pl.program_id(0)

        @pl.when(n == 0)
        def _():
            state[...] = jnp.zeros_like(state)

        for h in range(H):
            q, k, v, _ = head(h)
            s_buf[h] = _dot_nt(q, k)
            st = state[h]
            st_c = st.astype(CDT)
            st_ref[h] = st_c
            cross_buf[h] = _dot_nn(q, st_c)
            kz = (k.astype(F32) * zeta_ref[h][:, 0:1]).astype(CDT)
            state[h] = st * gc_ref[h][0:1, 0:1] + _dot_tn(kz, v)
        for h in range(H):
            vs = slice(h * dv, (h + 1) * dv)
            _, _, v, gr = head(h)
            s = s_buf[h] * kscale * dec_ref[h]
            o = _dot_nn(s.astype(CDT), v) + cross_buf[h] * xi_ref[h][:, 0:1]
            opre_ref[:, vs] = o
            _, _, y = _gn_fwd(o, g_ref[:, vs], b_ref[:, vs])
            gr = gr.astype(F32)
            or_ref[:, vs] = (y * (gr * _sigmoid(gr))).astype(or_ref.dtype)

    v_w = H * dv
    row = pl.BlockSpec((1, v_w), lambda i: (0, 0))
    tile = pl.BlockSpec((BLK, v_w), lambda i: (i, 0))
    return pl.pallas_call(
        body, name="ret_fwd", grid=(N,),
        in_specs=_ret_specs(dk, dv, lambda i: i) + [row, row],
        out_specs=[tile, tile, pl.BlockSpec((None, H, dk, dv), lambda i: (i, 0, 0, 0))],
        out_shape=[_sds((S, v_w), F32), _sds((S, v_w), CDT), _sds((N, H, dk, dv), CDT)],
        scratch_shapes=[pltpu.VMEM((H, dk, dv), F32), pltpu.VMEM((H, BLK, BLK), F32), pltpu.VMEM((H, BLK, dv), F32)],
        compiler_params=_params(1),
    )(*[proj] * (3 * H), *_ret_tables(dk), gn_g, gn_b)


def _ret_bwd(proj, gn_g, gn_b, o_pre, states, d_or, dga, dgb, dk, dv):
    S, in_w = proj.shape
    N = S // BLK
    H = RET_HEADS
    qk_w, v_w = H * dk, H * dv
    kscale = dk ** -0.5
    n_in = 3 * H + 4
    out_w = 2 * qk_w + 2 * v_w
    gate_w = dga.shape[1]
    col0 = 3 * ATT_W
    assert col0 + out_w + 2 * gate_w == in_w
    rev = lambda i: N - 1 - i

    def body(*refs):
        head, (dec_ref, xi_ref, zeta_ref, gc_ref) = _ret_heads(refs, dk)
        (g_ref, b_ref, opre_ref, st_ref, dor_ref, dga_ref, dgb_ref, dproj_ref, dg_ref, db_ref, dstate, stage,
         sem, do_buf, dox_buf, a_buf, g_buf, dq_buf, dk_buf, dv_buf) = refs[n_in:n_in + 20]
        i = pl.program_id(0)
        slot = i % 2
        out_ref = stage.at[slot]

        def out_copy(s, step):
            rows = pl.ds(pl.multiple_of(rev(step) * BLK, BLK), BLK)
            return pltpu.make_async_copy(stage.at[s], dproj_ref.at[rows, pl.ds(col0, in_w - col0)], sem.at[s])

        @pl.when(i >= 2)
        def _():
            out_copy(slot, i - 2).wait()

        @pl.when(i == 0)
        def _():
            dstate[...] = jnp.zeros_like(dstate)
            dg_ref[...] = jnp.zeros_like(dg_ref)
            db_ref[...] = jnp.zeros_like(db_ref)

        out_ref[:, out_w:out_w + gate_w] = dga_ref[...]
        out_ref[:, out_w + gate_w:out_w + 2 * gate_w] = dgb_ref[...]
        for h in range(H):
            vs = slice(h * dv, (h + 1) * dv)
            _, _, _, gr = head(h)
            gr = gr.astype(F32)
            sg = _sigmoid(gr)
            gain = g_ref[:, vs]
            yh, rstd, y = _gn_fwd(opre_ref[:, vs], gain, b_ref[:, vs])
            d_or_v = dor_ref[:, vs]
            dy = d_or_v * (gr * sg)
            out_ref[:, 2 * qk_w + v_w + h * dv:2 * qk_w + v_w + (h + 1) * dv] = (
                d_or_v * y * (sg * (1.0 + gr * (1.0 - sg)))).astype(out_ref.dtype)
            dg_ref[:, vs] += jnp.sum(dy * yh, axis=0, keepdims=True)
            db_ref[:, vs] += jnp.sum(dy, axis=0, keepdims=True)
            dyh = dy * gain
            do = rstd * (dyh - jnp.mean(dyh, axis=-1, keepdims=True)
                         - yh * jnp.mean(dyh * yh, axis=-1, keepdims=True))
            do_buf[h] = do.astype(CDT)
            dox_buf[h] = (do * xi_ref[h][:, 0:1]).astype(CDT)
        for h in range(H):
            q, k, v, _ = head(h)
            dox = dox_buf[h]
            a_buf[h] = _dot_nt(q, k)
            g_buf[h] = _dot_nt(do_buf[h], v)
            dsn = dstate[h]
            dsn_c = dsn.astype(CDT)
            kz = (k.astype(F32) * zeta_ref[h][:, 0:1]).astype(CDT)
            dq_buf[h] = _dot_nt(dox, st_ref[h])
            dk_buf[h] = _dot_nt(v, dsn_c)
            dv_buf[h] = _dot_nn(kz, dsn_c)
            dstate[h] = dsn * gc_ref[h][0:1, 0:1] + _dot_tn(q, dox)
        for h in range(H):
            q, k, _, _ = head(h)
            decay = dec_ref[h]
            a_c = (a_buf[h] * kscale * decay).astype(CDT)
            g_c = (g_buf[h] * decay).astype(CDT)
            dq = _dot_nn(g_c, k) * kscale + dq_buf[h]
            dkk = _dot_tn(g_c, q) * kscale + dk_buf[h] * zeta_ref[h][:, 0:1]
            dvv = _dot_tn(a_c, do_buf[h]) + dv_buf[h]
            out_ref[:, h * dk:(h + 1) * dk] = dq.astype(out_ref.dtype)
            out_ref[:, qk_w + h * dk:qk_w + (h + 1) * dk] = dkk.astype(out_ref.dtype)
            out_ref[:, 2 * qk_w + h * dv:2 * qk_w + (h + 1) * dv] = dvv.astype(out_ref.dtype)

        cp = out_copy(slot, i)
        cp.start()

        @pl.when(i == N - 1)
        def _():
            cp.wait()
            if N >= 2:
                out_copy(1 - slot, i - 1).wait()

    row = pl.BlockSpec((1, v_w), lambda i: (0, 0))
    tile = pl.BlockSpec((BLK, v_w), lambda i: (rev(i), 0))
    gate = pl.BlockSpec((BLK, gate_w), lambda i: (rev(i), 0))
    return pl.pallas_call(
        body, name="ret_bwd", grid=(N,),
        in_specs=_ret_specs(dk, dv, rev) + [row, row, tile,
                 pl.BlockSpec((None, H, dk, dv), lambda i: (rev(i), 0, 0, 0)), tile, gate, gate],
        out_specs=[pl.BlockSpec(memory_space=pl.ANY), row, row],
        out_shape=[_sds((S, in_w), CDT), _sds((1, v_w), F32), _sds((1, v_w), F32)],
        scratch_shapes=[pltpu.VMEM((H, dk, dv), F32), pltpu.VMEM((2, BLK, in_w - col0), CDT),
                        pltpu.SemaphoreType.DMA((2,)),
                        pltpu.VMEM((H, BLK, dv), CDT), pltpu.VMEM((H, BLK, dv), CDT),
                        pltpu.VMEM((H, BLK, BLK), F32), pltpu.VMEM((H, BLK, BLK), F32),
                        pltpu.VMEM((H, BLK, dk), F32), pltpu.VMEM((H, BLK, dk), F32), pltpu.VMEM((H, BLK, dv), F32)],
        compiler_params=_params(1),
    )(*[proj] * (3 * H), *_ret_tables(dk), gn_g, gn_b, o_pre, states, d_or, dga, dgb)


def _merge_fwd(o_a, o_r, wa, wb, proj, d_model, tm=1024, tn=512):
    S, in_w = proj.shape
    off_a, off_b = in_w - 2 * d_model, in_w - d_model
    assert off_a % tn == 0 and off_b % tn == 0

    def body(oa_ref, or_ref, wa_ref, wb_ref, ga_ref, gb_ref, y_ref, pa_ref, pb_ref):
        pa = _dot_nn(oa_ref[...], wa_ref[...])
        pb = _dot_nn(or_ref[...], wb_ref[...])
        y = _sigmoid(ga_ref[...].astype(F32)) * pa + _sigmoid(gb_ref[...].astype(F32)) * pb
        y_ref[...] = y.astype(y_ref.dtype)
        pa_ref[...] = pa.astype(pa_ref.dtype)
        pb_ref[...] = pb.astype(pb_ref.dtype)

    ka, kb = o_a.shape[1], o_r.shape[1]
    out = pl.BlockSpec((tm, tn), lambda i, j: (i, j))
    return pl.pallas_call(
        body, name="merge_fwd", grid=(S // tm, d_model // tn),
        in_specs=[pl.BlockSpec((tm, ka), lambda i, j: (i, 0)), pl.BlockSpec((tm, kb), lambda i, j: (i, 0)),
                  pl.BlockSpec((ka, tn), lambda i, j: (0, j)), pl.BlockSpec((kb, tn), lambda i, j: (0, j)),
                  pl.BlockSpec((tm, tn), lambda i, j: (i, off_a // tn + j)),
                  pl.BlockSpec((tm, tn), lambda i, j: (i, off_b // tn + j))],
        out_specs=[out, out, out], out_shape=[_sds((S, d_model), CDT)] * 3,
        compiler_params=_params(2))(o_a, o_r, wa, wb, proj, proj)


def _d_x(dproj, w_in, x, g, dx1, dep, tm=512, row_groups=2):
    S, D = x.shape
    n_sh, _, ns = w_in.shape
    nt = S // tm // row_groups
    deps = [d for d in (dep,) if d is not None]

    def body(a_ref, b_ref, x_ref, g_ref, r_ref, *rest):
        dx_ref, dg_ref, acc = rest[len(deps):]
        h, k, i = pl.program_id(0), pl.program_id(1), pl.program_id(2)

        @pl.when(k == 0)
        def _():
            acc[i] = jnp.zeros((tm, D), F32)

        acc[i] += _dot_nt(a_ref[...], b_ref[...])

        @pl.when(k == n_sh - 1)
        def _():
            _ep_rms_bwd(acc[i], (x_ref, g_ref, r_ref), (dx_ref, dg_ref), h * nt + i)

    def last_only(h, k, i):
        return (h * nt + jnp.where(k == n_sh - 1, i, 0), 0)

    return pl.pallas_call(
        body, name="d_x", grid=(row_groups, n_sh, nt),
        in_specs=[pl.BlockSpec((tm, ns), lambda h, k, i: (h * nt + i, k)),
                  pl.BlockSpec((None, D, ns), lambda h, k, i: (k, 0, 0)),
                  pl.BlockSpec((tm, D), last_only), pl.BlockSpec((1, D), lambda h, k, i: (0, 0)),
                  pl.BlockSpec((tm, D), last_only)] + [pl.BlockSpec(memory_space=pl.ANY)] * len(deps),
        out_specs=[pl.BlockSpec((tm, D), last_only), pl.BlockSpec((1, D), lambda h, k, i: (0, 0))],
        out_shape=[_sds((S, D), F32), _sds((1, D), F32)],
        scratch_shapes=[pltpu.VMEM((nt, tm, D), F32)],
        compiler_params=_params(3))(dproj, w_in, x, g, dx1, *deps)


def _local_step(x, target, w_in_mine, chip, others, near_w_in, far_w_in, small, late_weights, on_grads, deps0=()):
    S, D = x.shape
    ns_in = w_in_mine.shape[1]
    in_w = N_CHIPS * ns_in
    d_ff = 4 * D
    ret_v_w = 2 * D
    dv = ret_v_w // RET_HEADS
    dk = (in_w - 3 * ATT_W - 2 * ret_v_w - 2 * D) // (2 * RET_HEADS)
    gqk = jnp.concatenate([small["q_norm_g"].reshape(1, ATT_W), small["k_norm_g"].reshape(1, ATT_W)], axis=1)
    g1, g2 = small["norm1_g"], small["norm2_g"]
    gn_g, gn_b = small["ret_gn_g"], small["ret_gn_b"]

    proj_sds = _sds((S, in_w), CDT)
    xn, proj = _in_proj_mine(x, g1, w_in_mine, chip, proj_sds, deps0)
    for stage, (get_w_in, chips) in enumerate(((near_w_in, others[:2]), (far_w_in, others[2:]))):
        w_in, started = get_w_in(proj)
        (proj,) = _matmul(
            "in_proj_far%d" % stage, "nn", xn, w_in, tm=512, tn=ns_in, tk=D, prefetch=[chips],
            n_cols=chips.shape[0] * ns_in, b_spec=pl.BlockSpec((None, D, ns_in), lambda i, j, k, o: (o[j], 0, 0)),
            outs=[(proj_sds, pl.BlockSpec((512, ns_in), lambda i, j, k, o: (i, o[j])))], epilogue=_ep_store,
            deps=[proj, started], alias_dep_to_out=(0, 0), j_outer=True)
    qkv = _qknorm_fwd(proj, gqk)
    att = [_att_fwd(g, S, qkv[g]) for g in range(3)]
    os_, ls_ = [a[0] for a in att], [a[1] for a in att]
    o_a = _mix_fwd(S, os_, ls_)
    o_pre, o_r, states = _ret_fwd(proj, gn_g, gn_b, dk, dv)
    w = late_weights(o_r)
    y, pa, pb = _merge_fwd(o_a, o_r, w["w_proj_a"], w["w_proj_b"], proj, D)
    x1, xn2 = _matmul("out_proj", "nn", y, w["w_out"], tm=1024, tn=D, tk=D,
                      extras=[(x, _mn(1024, D)), (g2, _row(D))],
                      outs=[(_sds((S, D), F32), _mn(1024, D)), (_sds((S, D), CDT), _mn(1024, D))],
                      epilogue=_ep_resid_norm)
    hid, act = _matmul("mlp_up", "nn", xn2, w["w_up"], tm=512, tn=2048, tk=D, j_outer=True,
                       outs=[(_sds((S, d_ff), CDT), _mn(512, 2048))] * 2, epilogue=_ep_up)
    dx2, dx2c, loss_row = _matmul(
        "mlp_down_loss", "nn", act, w["w_down"], tm=512, tn=D, tk=d_ff,
        extras=[(x1, _mn(512, D)), (target, _mn(512, D))],
        outs=[(_sds((S, D), F32), _mn(512, D)), (_sds((S, D), CDT), _mn(512, D)), (_sds((1, D), F32), _row(D))],
        epilogue=functools.partial(_ep_down_loss, inv_d=1.0 / D))
    loss = 0.5 * jnp.sum(loss_row) / D

    (dh,) = _matmul("d_hidden", "nt", dx2c, w["w_down"], tm=512, tn=2048, tk=D, j_outer=True,
                    extras=[(hid, _mn(512, 2048))], outs=[(_sds((S, d_ff), CDT), _mn(512, 2048))], epilogue=_ep_dh)
    (gw_down,) = _matmul("dw_down", "tn", act, dx2c, tm=1024, tn=D, tk=1024,
                         outs=[(_sds((d_ff, D), F32), _mn(1024, D))], epilogue=_ep_store)
    (gw_up,) = _matmul("dw_up", "tn", xn2, dh, tm=D, tn=1024, tk=1024,
                       outs=[(_sds((D, d_ff), F32), _mn(D, 1024))], epilogue=_ep_store)
    tok = on_grads({"w_down": gw_down, "w_up": gw_up})
    dx1, dx1c, dg2 = _matmul(
        "d_x1", "nt", dh, w["w_up"], tm=512, tn=D, tk=d_ff,
        extras=[(x1, _mn(512, D)), (g2, _row(D)), (dx2, _mn(512, D))],
        outs=[(_sds((S, D), F32), _mn(512, D)), (_sds((S, D), CDT), _mn(512, D)), (_sds((1, D), F32), _row(D))],
        epilogue=_ep_rms_bwd, deps=[tok])

    gt = 512
    assert (in_w - 2 * D) % gt == 0
    off_a, off_b = (in_w - 2 * D) // gt, (in_w - D) // gt
    dpa, dpb, dga, dgb = _matmul(
        "d_gates", "nt", dx1c, w["w_out"], tm=1024, tn=gt, tk=D,
        extras=[(proj, _mn(1024, gt, off_a)), (proj, _mn(1024, gt, off_b)), (pa, _mn(1024, gt)),
                (pb, _mn(1024, gt))],
        outs=[(_sds((S, D), CDT), _mn(1024, gt))] * 4, epilogue=_ep_gates)
    (gw_out,) = _matmul("dw_out", "tn", y, dx1c, tm=D, tn=D, tk=1024,
                        outs=[(_sds((D, D), F32), _mn(D, D))], epilogue=_ep_store)
    (gw_pa,) = _matmul("dw_proj_a", "tn", o_a, dpa, tm=GW, tn=D, tk=1024,
                       outs=[(_sds((GW, D), F32), _mn(GW, D))], epilogue=_ep_store)
    (gw_pb,) = _matmul("dw_proj_b", "tn", o_r, dpb, tm=1024, tn=D, tk=1024,
                       outs=[(_sds((ret_v_w, D), F32), _mn(1024, D))], epilogue=_ep_store)
    (do_a,) = _matmul("d_o_a", "nt", dpa, w["w_proj_a"], tm=1024, tn=GW, tk=D,
                      outs=[(_sds((S, GW), F32), _mn(1024, GW))], epilogue=_ep_store)
    tok = on_grads({"w_out": gw_out, "w_proj_a": gw_pa, "w_proj_b": gw_pb})
    (d_or,) = _matmul("d_o_r", "nt", dpb, w["w_proj_b"], tm=512, tn=ret_v_w, tk=D,
                      outs=[(_sds((S, ret_v_w), F32), _mn(512, ret_v_w))], epilogue=_ep_store, deps=[tok])

    dproj, dgn_g, dgn_b = _ret_bwd(proj, gn_g, gn_b, o_pre, states, d_or, dga, dgb, dk, dv)
    do_gs, c_gs = _mix_bwd(S, os_, ls_, do_a)
    datt_parts = [_att_bwd(g, S, qkv[g], ls_[g], do_gs[g], c_gs[g]) for g in range(3)]
    dproj, dgqk = _qknorm_bwd(proj, gqk, [p[0] for p in datt_parts], [p[1] for p in datt_parts],
                              [p[2] for p in datt_parts], dproj)

    (gw_in,) = _matmul(
        "dw_in", "tn", xn, dproj, tm=512, tn=ns_in, tk=1024,
        outs=[(_sds((N_CHIPS, D, ns_in), F32), pl.BlockSpec((None, 512, ns_in), lambda i, j, k: (j, i, 0)))],
        epilogue=_ep_store)
    tok = on_grads({"w_in": gw_in})
    grad_x, dg1 = _d_x(dproj, w_in, x, g1, dx1, tok)

    smallg = {"norm1_g": dg1, "q_norm_g": dgqk[:, :ATT_W], "k_norm_g": dgqk[:, ATT_W:],
              "ret_gn_g": dgn_g, "ret_gn_b": dgn_b, "norm2_g": dg2}
    return loss, grad_x, smallg


N_CHIPS = 4
N_DEV = 8


def _place():
    x, y, c = lax.axis_index("x"), lax.axis_index("y"), lax.axis_index("c")
    return x, y, c


def _other_chips(x, y):
    out = []
    for fx, fy in ((1, 0), (0, 1), (1, 1)):
        px = 1 - x if fx else x
        py = 1 - y if fy else y
        out.append(((px, py), 2 * px + py))
    return out


SEM_SPEC = pl.BlockSpec(memory_space=pltpu.SEMAPHORE)
ANY_SPEC = pl.BlockSpec(memory_space=pl.ANY)
EFFECT = pltpu.SideEffectType.DATAFLOW_SIDE_EFFECTING


def _ici_copies(kind, srcs, lands, send, recv, which=(0, 1, 2)):
    x, y, c = _place()
    me = 2 * x + y
    out = []
    for w, (s, l) in enumerate(zip(srcs, lands)):
        for j, ((px, py), pidx) in enumerate(_other_chips(x, y)):
            if j not in which:
                continue
            if kind == "gather":
                half = s.shape[0] // 2
                rows = pl.ds(c * half, half)
                src, dst_there, dst_here = s.at[rows, :], l.at[me, rows, :], l.at[pidx, rows, :]
            else:
                src, dst_there, dst_here = s.at[pidx], l.at[me], l.at[pidx]
            out.append((src, dst_there, dst_here, send.at[3 * w + j], recv.at[3 * w + j], (px, py, c)))
    return out


def _exchange_start(name, kind, srcs, land_shapes, which=(0, 1, 2), lands=None):
    n = len(srcs)
    if lands is None:
        lands = [lax.empty(shape, dtype) for shape, dtype in land_shapes]

    def body(*refs):
        src_refs, land_refs = refs[:n], refs[n:2 * n]
        send, recv = refs[2 * n], refs[2 * n + 1]
        token = refs[-1]
        for src, dst, _, ss, rs, dev in _ici_copies(kind, src_refs, land_refs, send, recv, which):
            pltpu.make_async_remote_copy(src_ref=src, dst_ref=dst, send_sem=ss, recv_sem=rs, device_id=dev,
                                         device_id_type=MESH).start()
        token[...] = jnp.zeros_like(token)

    thru = [pltpu.HBM(s.shape, s.dtype) for s in srcs] + [pltpu.HBM(shape, dtype) for shape, dtype in land_shapes]
    res = pl.pallas_call(
        body, name=name,
        out_shape=(pltpu.SemaphoreType.DMA((3 * n,)), pltpu.SemaphoreType.DMA((3 * n,)), *thru, _sds((8, LANES), F32)),
        in_specs=[HBM_SPEC] * (2 * n), out_specs=(SEM_SPEC, SEM_SPEC, *[HBM_SPEC] * (2 * n), VMEM_SPEC),
        input_output_aliases={i: 2 + i for i in range(2 * n)},
        compiler_params=pltpu.CompilerParams(has_side_effects=EFFECT),
    )(*[pltpu.with_memory_space_constraint(s, pltpu.HBM) for s in srcs],
      *[pltpu.with_memory_space_constraint(l, pltpu.HBM) for l in lands])
    return res[0], res[1], list(res[2:2 + n]), list(res[2 + n:2 + 2 * n]), res[-1]


def _exchange_wait(name, kind, send, recv, srcs, lands, after, which=(0, 1, 2)):
    n = len(srcs)

    def body(*refs):
        src_refs, land_refs = refs[:n], refs[n:2 * n]
        send_ref, recv_ref = refs[2 * n], refs[2 * n + 1]
        for src, _, dst, ss, rs, dev in _ici_copies(kind, src_refs, land_refs, send_ref, recv_ref, which):
            cp = pltpu.make_async_remote_copy(src_ref=src, dst_ref=dst, send_sem=ss, recv_sem=rs, device_id=dev,
                                              device_id_type=MESH)
            cp.wait_send()
            cp.wait_recv()

    thru = [pltpu.HBM(t.shape, t.dtype) for t in list(srcs) + list(lands)]
    res = pl.pallas_call(
        body, name=name, out_shape=thru,
        in_specs=[HBM_SPEC] * (2 * n) + [SEM_SPEC, SEM_SPEC, ANY_SPEC], out_specs=[HBM_SPEC] * (2 * n),
        input_output_aliases={i: i for i in range(2 * n)},
        compiler_params=pltpu.CompilerParams(has_side_effects=EFFECT),
    )(*srcs, *lands, send, recv, after)
    return list(res[:n]), list(res[n:])


PAIR_TILE_ELEMS = 1 << 20


def _pair_fill(name, gathered, mine, core, others, chip, write_mine=True):
    k, r, C = gathered.shape
    half = r // 2
    tr = _row_tile(half, C, PAIR_TILE_ELEMS, mult=16)
    nt = half // tr
    n_far = others.shape[0]

    def body(c_ref, o_ref, chip_ref, in_ref, mine_ref, out_ref, slot, send, recv):
        j = pl.program_id(0)
        _sibling_barrier((j == 0) & (pl.program_id(1) == 0))
        b = (j * nt + pl.program_id(1)) % 2
        x, y, c = _place()
        cp = pltpu.make_async_remote_copy(src_ref=in_ref, dst_ref=slot.at[b], send_sem=send.at[b],
                                          recv_sem=recv.at[b], device_id=(x, y, 1 - c), device_id_type=MESH)

        @pl.when(j < n_far)
        def _():
            cp.start()
            cp.wait_recv()
            out_ref[...] = slot[b]
            cp.wait_send()

        @pl.when(j >= n_far)
        def _():
            out_ref[...] = mine_ref[...]

    def far(j):
        return jnp.minimum(j, n_far - 1)

    grid_spec = pltpu.PrefetchScalarGridSpec(
        num_scalar_prefetch=3, grid=(n_far + (2 if write_mine else 0), nt),
        in_specs=[pl.BlockSpec((tr, C), lambda j, i, c, o, m: (
                      (2 * o[far(j)] + c[0]) * nt + jnp.where(j < n_far, i, nt - 1), 0)),
                  pl.BlockSpec((tr, C), lambda j, i, c, o, m: (jnp.where(j < n_far, 0, (j - n_far) * nt + i), 0))],
        out_specs=pl.BlockSpec((tr, C), lambda j, i, c, o, m: (
            jnp.where(j < n_far, 2 * o[far(j)] + 1 - c[0], 2 * m[0] + j - n_far) * nt + i, 0)),
        scratch_shapes=[pltpu.VMEM((2, tr, C), gathered.dtype), pltpu.SemaphoreType.DMA((2,)),
                        pltpu.SemaphoreType.DMA((2,))])
    out = pl.pallas_call(body, name=name, grid_spec=grid_spec, out_shape=_sds((k * r, C), gathered.dtype),
                         input_output_aliases={3: 0}, compiler_params=_params(2, PAIR_FILL_ID))(
                             core, others, chip, gathered.reshape(k * r, C), mine)
    return out.reshape(k, r, C)


def _pair_reduce(name, g, core):
    k, R, C = g.shape
    half = R // 2
    tr = _row_tile(half, C, PAIR_TILE_ELEMS, mult=16)
    nt = half // tr

    def body(c_ref, mine_ref, give_ref, out_ref, wire_ref, stage, slot, send, recv):
        _sibling_barrier((pl.program_id(0) == 0) & (pl.program_id(1) == 0))
        b = (pl.program_id(0) * nt + pl.program_id(1)) % 2
        x, y, c = _place()
        stage[b] = give_ref[...].astype(stage.dtype)
        cp = pltpu.make_async_remote_copy(src_ref=stage.at[b], dst_ref=slot.at[b], send_sem=send.at[b],
                                          recv_sem=recv.at[b], device_id=(x, y, 1 - c), device_id_type=MESH)
        cp.start()
        cp.wait_recv()
        tot = mine_ref[...] + slot[b].astype(F32)
        out_ref[...] = tot
        wire_ref[...] = tot.astype(wire_ref.dtype)
        cp.wait_send()

    blk = (tr, C)
    out_spec = pl.BlockSpec(blk, lambda s, i, c: (s * nt + i, 0))
    grid_spec = pltpu.PrefetchScalarGridSpec(
        num_scalar_prefetch=1, grid=(k, nt),
        in_specs=[pl.BlockSpec(blk, lambda s, i, c: ((2 * s + c[0]) * nt + i, 0)),
                  pl.BlockSpec(blk, lambda s, i, c: ((2 * s + 1 - c[0]) * nt + i, 0))],
        out_specs=[out_spec, out_spec],
        scratch_shapes=[pltpu.VMEM((2, tr, C), CDT), pltpu.VMEM((2, tr, C), CDT), pltpu.SemaphoreType.DMA((2,)),
                        pltpu.SemaphoreType.DMA((2,))])
    g2 = g.reshape(k * R, C)
    out, wire = pl.pallas_call(body, name=name, grid_spec=grid_spec,
                               out_shape=[_sds((k * half, C), F32), _sds((k * half, C), CDT)],
                               compiler_params=_params(2, PAIR_REDUCE_ID))(core, g2, g2)
    return out, wire.reshape(k, half, C)


def _all_reduce_small(v):
    r, cdim = v.shape

    def body(v_ref, o_ref, buf, send, recv):
        x, y, c = _place()
        me = 4 * x + 2 * y + c
        buf[me] = v_ref[...]
        sends = []
        for m in range(1, N_DEV):
            px = 1 - x if m & 4 else x
            py = 1 - y if m & 2 else y
            pc = 1 - c if m & 1 else c
            cp = pltpu.make_async_remote_copy(src_ref=v_ref, dst_ref=buf.at[me], send_sem=send.at[m - 1],
                                              recv_sem=recv.at[m - 1], device_id=(px, py, pc), device_id_type=MESH)
            cp.start()
            sends.append((cp, 4 * px + 2 * py + pc))
        for m, (cp, pidx) in enumerate(sends):
            pltpu.make_async_remote_copy(src_ref=v_ref, dst_ref=buf.at[pidx], send_sem=send.at[m], recv_sem=recv.at[m],
                                         device_id=(x, y, c), device_id_type=MESH).wait_recv()
        for cp, _ in sends:
            cp.wait_send()
        tot = buf[0]
        for k in range(1, N_DEV):
            tot = tot + buf[k]
        o_ref[...] = tot

    return pl.pallas_call(
        body, name="all_reduce_small", in_specs=[VMEM_SPEC], out_specs=VMEM_SPEC,
        out_shape=_sds((r, cdim), F32),
        scratch_shapes=[pltpu.VMEM((N_DEV, r, cdim), F32), pltpu.SemaphoreType.DMA((N_DEV - 1,)),
                        pltpu.SemaphoreType.DMA((N_DEV - 1,))],
    )(v)


def _row_tile(rows, cols, budget_elems=1 << 18, mult=8):
    if rows % mult:
        return rows
    t = max(mult, (budget_elems // cols) // mult * mult)
    while rows % t:
        t -= mult
    return t


def _adamw_update(w, g, m, v):
    nm = ADAM_B1 * m + (1.0 - ADAM_B1) * g
    nv = ADAM_B2 * v + (1.0 - ADAM_B2) * (g * g)
    m_hat = nm / (1.0 - ADAM_B1 ** ADAM_STEP)
    v_hat = nv / (1.0 - ADAM_B2 ** ADAM_STEP)
    return -ADAM_LR * (m_hat / (jnp.sqrt(v_hat) + ADAM_EPS) + ADAM_WD * w), nm, nv


def _adamw(name, w, g, m, v):
    R, C = w.shape
    tr = _row_tile(R, C, 1 << 18)

    def body(w_ref, g_ref, m_ref, v_ref, d_ref, nm_ref, nv_ref):
        d_ref[...], nm_ref[...], nv_ref[...] = _adamw_update(w_ref[...], g_ref[...], m_ref[...], v_ref[...])

    spec = pl.BlockSpec((tr, C), lambda i: (i, 0))
    return pl.pallas_call(body, name=name, grid=(R // tr,), in_specs=[spec] * 4, out_specs=[spec] * 3,
                          out_shape=[_sds((R, C), F32)] * 3, compiler_params=_params(1))(w, g, m, v)


def _sum_share(name, own, by_chip, chip, others, core):
    k, half, C = by_chip.shape
    tr = _row_tile(half, C, PAIR_TILE_ELEMS // 2, mult=16)
    nt = half // tr

    def body(chip_ref, oth_ref, c_ref, own_ref, a_ref, b_ref, cc_ref, g_out, mine, slot, send, recv):
        p = pl.program_id(1)
        _sibling_barrier((pl.program_id(0) == 0) & (p == 0))
        b = pl.program_id(0) % 2
        x, y, c = _place()
        cp = pltpu.make_async_remote_copy(src_ref=mine.at[b], dst_ref=slot.at[b], send_sem=send.at[b],
                                          recv_sem=recv.at[b], device_id=(x, y, 1 - c), device_id_type=MESH)

        @pl.when(p == 0)
        def _():
            tot = ((own_ref[...] + a_ref[...].astype(F32)) + b_ref[...].astype(F32)) + cc_ref[...].astype(F32)
            mine[b] = tot
            cp.start()
            g_out[...] = tot

        @pl.when(p == 1)
        def _():
            cp.wait_recv()
            g_out[...] = slot[b]
            cp.wait_send()

    def piece(j):
        return pl.BlockSpec((tr, C), lambda i, p, chip, oth, c: (oth[j] * nt + i, 0))

    grid_spec = pltpu.PrefetchScalarGridSpec(
        num_scalar_prefetch=3, grid=(nt, 2),
        in_specs=[pl.BlockSpec((tr, C), lambda i, p, chip, oth, c: (chip[0] * nt + i, 0)),
                  piece(0), piece(1), piece(2)],
        out_specs=pl.BlockSpec((tr, C), lambda i, p, chip, oth, c: (
            jnp.where(p == 0, c[0], 1 - c[0]) * nt + i, 0)),
        scratch_shapes=[pltpu.VMEM((2, tr, C), F32), pltpu.VMEM((2, tr, C), F32), pltpu.SemaphoreType.DMA((2,)),
                        pltpu.SemaphoreType.DMA((2,))])
    by2 = by_chip.reshape(k * half, C)
    return pl.pallas_call(body, name=name, grid_spec=grid_spec, out_shape=_sds((2 * half, C), F32),
                          compiler_params=_params(2, SUM_SHARE_ID))(chip, others, core, own, by2, by2, by2)


BIG = ("w_in", "w_proj_a", "w_proj_b", "w_out", "w_up", "w_down")
COL_SHARDED = ("w_in", "w_proj_a", "w_up")
SMALL = ("norm1_g", "q_norm_g", "k_norm_g", "ret_gn_g", "ret_gn_b", "norm2_g")
ALL_W = ("norm1_g", "w_in", "q_norm_g", "k_norm_g", "ret_gn_g", "ret_gn_b", "w_proj_a", "w_proj_b", "w_out",
         "norm2_g", "w_up", "w_down")
LANES = 128


def _to_full(name, gathered):
    k, r, c = gathered.shape
    if name in COL_SHARDED:
        return gathered.transpose(1, 0, 2).reshape(r, k * c)
    return gathered.reshape(k * r, c)


def _to_shard_major(name, full):
    if name in COL_SHARDED:
        r, c4 = full.shape
        return full.reshape(r, N_CHIPS, c4 // N_CHIPS).transpose(1, 0, 2)
    r4, c = full.shape
    return full.reshape(N_CHIPS, r4 // N_CHIPS, c)


def kernel(x, norm1_g, w_in, q_norm_g, k_norm_g, ret_gn_g, ret_gn_b, w_proj_a, w_proj_b, w_out, norm2_g, w_up, w_down, loss_target, m_norm1_g, m_w_in, m_q_norm_g, m_k_norm_g, m_ret_gn_g, m_ret_gn_b, m_w_proj_a, m_w_proj_b, m_w_out, m_norm2_g, m_w_up, m_w_down, v_norm1_g, v_w_in, v_q_norm_g, v_k_norm_g, v_ret_gn_g, v_ret_gn_b, v_w_proj_a, v_w_proj_b, v_w_out, v_norm2_g, v_w_up, v_w_down):
    weights = dict(norm1_g=norm1_g, w_in=w_in, q_norm_g=q_norm_g, k_norm_g=k_norm_g, ret_gn_g=ret_gn_g,
                   ret_gn_b=ret_gn_b, w_proj_a=w_proj_a, w_proj_b=w_proj_b, w_out=w_out, norm2_g=norm2_g,
                   w_up=w_up, w_down=w_down)
    moments_m = dict(norm1_g=m_norm1_g, w_in=m_w_in, q_norm_g=m_q_norm_g, k_norm_g=m_k_norm_g, ret_gn_g=m_ret_gn_g,
                     ret_gn_b=m_ret_gn_b, w_proj_a=m_w_proj_a, w_proj_b=m_w_proj_b, w_out=m_w_out,
                     norm2_g=m_norm2_g, w_up=m_w_up, w_down=m_w_down)
    moments_v = dict(norm1_g=v_norm1_g, w_in=v_w_in, q_norm_g=v_q_norm_g, k_norm_g=v_k_norm_g, ret_gn_g=v_ret_gn_g,
                     ret_gn_b=v_ret_gn_b, w_proj_a=v_w_proj_a, w_proj_b=v_w_proj_b, w_out=v_w_out,
                     norm2_g=v_norm2_g, w_up=v_w_up, w_down=v_w_down)

    mx, my = lax.axis_index("x"), lax.axis_index("y")
    core = lax.axis_index("c").astype(jnp.int32).reshape(1)
    chip = (2 * mx + my).astype(jnp.int32).reshape(1)
    others = jnp.stack([2 * (1 - mx) + my, 2 * mx + 1 - my, 2 * (1 - mx) + 1 - my]).astype(jnp.int32)
    shards = {n: weights[n][0].astype(CDT) for n in BIG}
    def start_gather(name, names):
        return _exchange_start(name, "gather", [shards[n] for n in names],
                               [((N_CHIPS,) + shards[n].shape, CDT) for n in names])

    w_in_shape = [((N_CHIPS,) + shards["w_in"].shape, CDT)]
    n_send, n_recv, n_srcs, n_lands, n_token = _exchange_start(
        "gather_w_in_near_start", "gather", [shards["w_in"]], w_in_shape, which=(0, 1))
    late = [n for n in BIG if n != "w_in"]
    flight = {}

    def near_w_in(after):
        srcs, lands = _exchange_wait("gather_w_in_near_wait", "gather", n_send, n_recv, n_srcs, n_lands, after,
                                     which=(0, 1))
        d_send, d_recv, d_srcs, d_lands, _ = _exchange_start(
            "gather_w_in_diag_start", "gather", srcs, w_in_shape, which=(2,), lands=lands)
        flight["late"] = start_gather("gather_late_start", late)
        w_near = _pair_fill("pair_fill_w_in_near", d_lands[0], d_srcs[0], core, others[:2], chip)
        flight["diag"] = (d_send, d_recv, d_srcs, [w_near])
        return w_near, flight["late"][-1]

    def far_w_in(after):
        d_send, d_recv, d_srcs, d_lands = flight["diag"]
        srcs, lands = _exchange_wait("gather_w_in_diag_wait", "gather", d_send, d_recv, d_srcs, d_lands, after,
                                     which=(2,))
        return _pair_fill("pair_fill_w_in_diag", lands[0], srcs[0], core, others[2:], chip, write_mine=False), None

    def late_weights(after):
        l_send, l_recv, l_srcs, l_lands, _ = flight["late"]
        srcs, lands = _exchange_wait("gather_late_wait", "gather", l_send, l_recv, l_srcs, l_lands, after)
        out = {}
        for n, mine, land in zip(late, srcs, lands):
            out[n] = _to_full(n, _pair_fill("pair_fill_%s" % n, land, mine, core, others, chip))
        return out

    pending = []

    def on_grads(group):
        names = list(group)
        red = [_pair_reduce("pair_reduce_%s" % n, g if g.ndim == 3 else _to_shard_major(n, g), core)
               for n, g in group.items()]
        wires = [wire for _, wire in red]
        send, recv, srcs, lands, token = _exchange_start(
            "scatter_start_%s" % names[0], "scatter", wires, [(wire.shape, wire.dtype) for wire in wires])
        pending.append((names, [own for own, _ in red], send, recv, srcs, lands))
        return token

    small = {n: weights[n].reshape(1, -1) for n in SMALL}

    loss, grad_x, small_g = _local_step(x[0], loss_target[0], n_srcs[0], chip, others, near_w_in, far_w_in, small,
                                        late_weights, on_grads, deps0=[n_token])

    out_g, out_d, out_m, out_v = {}, {}, {}, {}
    for names, owns, send, recv, srcs, lands in pending:
        _, got = _exchange_wait("scatter_wait_%s" % names[0], "scatter", send, recv, srcs, lands, grad_x)
        for n, own, by_chip in zip(names, owns, got):
            shape = weights[n].shape
            g2 = _sum_share("sum_share_%s" % n, own, by_chip, chip, others, core)
            d, nm, nv = _adamw("adamw_%s" % n, weights[n][0], g2, moments_m[n][0], moments_v[n][0])
            out_g[n], out_d[n], out_m[n], out_v[n] = (t.reshape(shape) for t in (g2, d, nm, nv))

    packed = jnp.concatenate([small_g[n].reshape(1, -1) for n in SMALL], axis=1).reshape(-1, LANES)
    loss_tile = jnp.zeros((8, LANES), F32).at[0, 0].set(loss)
    red = _all_reduce_small(jnp.concatenate([packed, loss_tile], axis=0))
    loss = red[packed.shape[0], 0]
    red = red[:packed.shape[0]].reshape(1, -1)
    off = 0
    for n in SMALL:
        shape = weights[n].shape
        row = (1, weights[n].size)
        g2 = red[:, off:off + row[1]]
        off += row[1]
        d, nm, nv = _adamw("adamw_%s" % n, weights[n].reshape(row), g2, moments_m[n].reshape(row),
                           moments_v[n].reshape(row))
        out_g[n], out_d[n], out_m[n], out_v[n] = (t.reshape(shape) for t in (g2, d, nm, nv))

    return (loss, grad_x[None], *[out_g[n] for n in ALL_W], *[out_d[n] for n in ALL_W],
            *[out_m[n] for n in ALL_W], *[out_v[n] for n in ALL_W])
```

```python
import functools

import jax
import jax.numpy as jnp
from jax import lax
from jax.experimental import pallas as pl
from jax.experimental.pallas import tpu as pltpu

CDT = jnp.bfloat16
F32 = jnp.float32
EPS = 1e-6

ATT_GROUPS = ((128, 1), (512, 4), (2048, 16))
ATT_HPG = 4
ATT_HEADS = 12
HD = 128
BLK = 128
ATT_W = ATT_HEADS * HD
GW = ATT_HPG * HD
RET_HEADS = 4

ADAM_LR = 0.001
ADAM_B1 = 0.9
ADAM_B2 = 0.999
ADAM_EPS = 1e-08
ADAM_WD = 0.01
ADAM_STEP = 10

VMEM_LIMIT_BYTES = 56 * 1024 * 1024
MESH = pl.DeviceIdType.MESH
HBM_SPEC = pl.BlockSpec(memory_space=pltpu.HBM)
VMEM_SPEC = pl.BlockSpec(memory_space=pltpu.VMEM)


def _params(n_axes, collective_id=None):
    return pltpu.CompilerParams(dimension_semantics=("arbitrary",) * n_axes,
                                vmem_limit_bytes=VMEM_LIMIT_BYTES, collective_id=collective_id)


PAIR_FILL_ID, PAIR_REDUCE_ID, SUM_SHARE_ID = 1, 2, 3


def _sibling_barrier(first_step):
    @pl.when(first_step)
    def _():
        sem = pltpu.get_barrier_semaphore()
        x, y, c = lax.axis_index("x"), lax.axis_index("y"), lax.axis_index("c")
        pl.semaphore_signal(sem, inc=1, device_id=(x, y, 1 - c), device_id_type=pl.DeviceIdType.MESH)
        pl.semaphore_wait(sem, 1)


def _dot_nn(a, b):
    return jnp.dot(a, b, preferred_element_type=F32)


def _dot_nt(a, b):
    return lax.dot_general(a, b, (((1,), (1,)), ((), ())), preferred_element_type=F32)


def _dot_tn(a, b):
    return lax.dot_general(a, b, (((0,), (0,)), ((), ())), preferred_element_type=F32)


def _sigmoid(v):
    return 1.0 / (1.0 + jnp.exp(-v))


def _matmul(name, mode, a, b, *, tm, tn, tk, extras=(), outs, epilogue, deps=(), b_spec=None, n_cols=None,
            prefetch=(), alias_dep_to_out=None, j_outer=False):
    deps = [d for d in deps if d is not None]
    if mode == "tn":
        K, M = a.shape
    else:
        M, K = a.shape
    if b_spec is None:
        (N, K2) = b.shape if mode == "nt" else b.shape[::-1]
        assert K == K2, (name, a.shape, b.shape)
        if mode == "nt":
            b_spec = pl.BlockSpec((tn, tk), lambda i, j, k, *p: (j, k))
        else:
            b_spec = pl.BlockSpec((tk, tn), lambda i, j, k, *p: (k, j))
    else:
        N = n_cols
    assert M % tm == 0 and N % tn == 0 and K % tk == 0, (name, a.shape, b.shape)
    ni, nj, nk = M // tm, N // tn, K // tk
    if mode == "tn":
        a_spec = pl.BlockSpec((tk, tm), lambda i, j, k, *p: (k, i))
    else:
        a_spec = pl.BlockSpec((tm, tk), lambda i, j, k, *p: (i, k))
    dot = {"nn": _dot_nn, "nt": _dot_nt, "tn": _dot_tn}[mode]
    n_ex, n_out, n_dep, n_pre = len(extras), len(outs), len(deps), len(prefetch)
    grid = (ni, nj, nk)
    if j_outer:
        grid = (nj, ni, nk)

        def swapped(spec):
            return pl.BlockSpec(spec.block_shape, lambda j, i, k, *p: spec.index_map(i, j, k, *p))

        a_spec, b_spec = swapped(a_spec), swapped(b_spec)
        extras = [(e, swapped(s)) for e, s in extras]
        outs = [(o, swapped(s)) for o, s in outs]

    def body(*refs):
        refs = refs[n_pre:]
        a_ref, b_ref = refs[0], refs[1]
        ex = refs[2:2 + n_ex]
        out = refs[2 + n_ex + n_dep:2 + n_ex + n_dep + n_out]
        acc = refs[-1] if nk > 1 else None
        i = pl.program_id(1 if j_outer else 0)
        k = pl.program_id(2)
        if nk == 1:
            epilogue(dot(a_ref[...].astype(CDT), b_ref[...].astype(CDT)), ex, out, i)
            return

        @pl.when(k == 0)
        def _():
            acc[...] = jnp.zeros_like(acc)

        acc[...] += dot(a_ref[...].astype(CDT), b_ref[...].astype(CDT))

        @pl.when(k == nk - 1)
        def _():
            epilogue(acc[...], ex, out, i)

    grid_spec = pltpu.PrefetchScalarGridSpec(
        num_scalar_prefetch=n_pre, grid=grid,
        in_specs=[a_spec, b_spec] + [s for _, s in extras] + [pl.BlockSpec(memory_space=pl.ANY)] * n_dep,
        out_specs=[s for _, s in outs],
        scratch_shapes=[pltpu.VMEM((tm, tn), F32)] if nk > 1 else [])
    aliases = {}
    if alias_dep_to_out is not None:
        aliases = {n_pre + 2 + n_ex + alias_dep_to_out[0]: alias_dep_to_out[1]}
    res = pl.pallas_call(
        body, name=name, grid_spec=grid_spec, out_shape=[o for o, _ in outs], input_output_aliases=aliases,
        compiler_params=_params(3),
    )(*prefetch, a, b, *[e for e, _ in extras], *deps)
    return res


def _mn(tm, tn, col_off=0):
    return pl.BlockSpec((tm, tn), lambda i, j, k, *p: (i, j + col_off))


def _row(tn):
    return pl.BlockSpec((1, tn), lambda i, j, k, *p: (0, j))


def _ep_store(acc, ex, out, i):
    out[0][...] = acc.astype(out[0].dtype)


def _ep_resid_norm(acc, ex, out, i):
    x1 = ex[0][...] + acc
    out[0][...] = x1
    rstd = lax.rsqrt(jnp.mean(x1 * x1, axis=-1, keepdims=True) + EPS)
    out[1][...] = (x1 * rstd * ex[1][...]).astype(out[1].dtype)


def _ep_up(acc, ex, out, i):
    out[0][...] = acc.astype(out[0].dtype)
    r = jnp.maximum(acc, 0.0)
    out[1][...] = (r * r).astype(out[1].dtype)


def _ep_down_loss(acc, ex, out, i, inv_d):
    diff = (ex[0][...] + acc) - ex[1][...]
    dx2 = diff * inv_d
    out[0][...] = dx2
    out[1][...] = dx2.astype(out[1].dtype)

    @pl.when(i == 0)
    def _():
        out[2][...] = jnp.zeros_like(out[2])

    out[2][...] += jnp.sum(diff * diff, axis=0, keepdims=True)


def _ep_dh(acc, ex, out, i):
    h = ex[0][...].astype(F32)
    out[0][...] = (acc * (2.0 * jnp.maximum(h, 0.0))).astype(out[0].dtype)


def _ep_rms_bwd(acc, ex, out, i):
    x = ex[0][...]
    g = ex[1][...]
    rstd = lax.rsqrt(jnp.mean(x * x, axis=-1, keepdims=True) + EPS)
    xh = x * rstd
    dxh = acc * g
    dx = ex[2][...] + rstd * (dxh - xh * jnp.mean(dxh * xh, axis=-1, keepdims=True))
    out[0][...] = dx
    for copy in out[1:-1]:
        copy[...] = dx.astype(copy.dtype)
    dg = out[-1]

    @pl.when(i == 0)
    def _():
        dg[...] = jnp.zeros_like(dg)

    dg[...] += jnp.sum(acc * xh, axis=0, keepdims=True)


def _ep_gates(acc, ex, out, i):
    sa = _sigmoid(ex[0][...].astype(F32))
    sb = _sigmoid(ex[1][...].astype(F32))
    dpa = acc * sa
    dpb = acc * sb
    out[0][...] = dpa.astype(out[0].dtype)
    out[1][...] = dpb.astype(out[1].dtype)
    out[2][...] = (dpa * ex[2][...].astype(F32) * (1.0 - sa)).astype(out[2].dtype)
    out[3][...] = (dpb * ex[3][...].astype(F32) * (1.0 - sb)).astype(out[3].dtype)


def _sds(shape, dtype):
    return jax.ShapeDtypeStruct(shape, dtype)


def _in_proj_mine(x, g, w_mine, chip, proj_sds, deps, tm=512):
    S, D = x.shape
    ns = w_mine.shape[1]
    deps = [d for d in deps if d is not None]

    def body(c_ref, x_ref, g_ref, w_ref, *rest):
        xn_ref, proj_ref = rest[len(deps)], rest[len(deps) + 1]
        xv = x_ref[...]
        rstd = lax.rsqrt(jnp.mean(xv * xv, axis=-1, keepdims=True) + EPS)
        xn = (xv * rstd * g_ref[...]).astype(xn_ref.dtype)
        xn_ref[...] = xn
        proj_ref[...] = _dot_nn(xn, w_ref[...]).astype(proj_ref.dtype)

    grid_spec = pltpu.PrefetchScalarGridSpec(
        num_scalar_prefetch=1, grid=(S // tm,),
        in_specs=[pl.BlockSpec((tm, D), lambda i, c: (i, 0)), pl.BlockSpec((1, D), lambda i, c: (0, 0)),
                  pl.BlockSpec((D, ns), lambda i, c: (0, 0))] + [pl.BlockSpec(memory_space=pl.ANY)] * len(deps),
        out_specs=[pl.BlockSpec((tm, D), lambda i, c: (i, 0)), pl.BlockSpec((tm, ns), lambda i, c: (i, c[0]))])
    return pl.pallas_call(body, name="in_proj_mine", grid_spec=grid_spec, out_shape=[_sds((S, D), CDT), proj_sds],
                          compiler_params=_params(1))(chip, x, g, w_mine, *deps)


def _rm_shape(S, d, width):
    return (S, width) if d == 1 else (d, S // d, width)


def _rm_spec(tm, d, width):
    if d == 1:
        return pl.BlockSpec((tm, width), lambda i: (i, 0))
    return pl.BlockSpec((d, tm // d, width), lambda i: (0, i, 0))


def _rm_put(dst_ref, cols, buf_ref, d):
    if d == 1:
        dst_ref[:, cols] = buf_ref[...].astype(dst_ref.dtype)
        return
    m = buf_ref.shape[0] // d
    for r in range(d):
        dst_ref[r, :, cols] = buf_ref[pl.ds(r, m, stride=d), :].astype(dst_ref.dtype)


def _rm_reader(buf_ref, src_ref, d):
    if d == 1:
        return lambda s, rows: src_ref[rows, s * HD:(s + 1) * HD].astype(F32)
    m = buf_ref.shape[1] // d
    for s in range(buf_ref.shape[0]):
        for r in range(d):
            buf_ref.at[s][pl.ds(r, m, stride=d), :] = src_ref[r, :, s * HD:(s + 1) * HD].astype(F32)
    return lambda s, rows: buf_ref.at[s][rows, :]


def _qknorm_fwd(proj, gqk, tm=512):
    S = proj.shape[0]
    W = 2 * ATT_W
    dil = [d for _, d in ATT_GROUPS]

    def body(p_ref, g_ref, o0, o1, o2, buf):
        outs = (o0, o1, o2)
        for hd in range(3 * ATT_HEADS):
            which, head = hd // ATT_HEADS, hd % ATT_HEADS
            grp, slot = head // ATT_HPG, head % ATT_HPG
            cols = slice(hd * HD, (hd + 1) * HD)

            def chunk(rows, which=which, cols=cols):
                v = p_ref[rows, cols].astype(F32)
                if which < 2:
                    rstd = lax.rsqrt(jnp.mean(v * v, axis=-1, keepdims=True) + EPS)
                    v = v * rstd * g_ref[:, cols]
                buf[rows, :] = v

            chunk(slice(None))
            _rm_put(outs[grp], slice(which * GW + slot * HD, which * GW + (slot + 1) * HD), buf, dil[grp])

    return pl.pallas_call(
        body, name="qknorm_fwd", grid=(S // tm,),
        in_specs=[pl.BlockSpec((tm, 3 * ATT_W), lambda i: (i, 0)), pl.BlockSpec((1, W), lambda i: (0, 0))],
        out_specs=[_rm_spec(tm, d, 3 * GW) for d in dil],
        out_shape=[_sds(_rm_shape(S, d, 3 * GW), CDT) for d in dil],
        scratch_shapes=[pltpu.VMEM((tm, HD), F32)],
        compiler_params=_params(1))(proj, gqk)


def _qknorm_bwd(proj, gqk, dqs, dks, dvs, dproj, tm=256):
    S = proj.shape[0]
    W = 2 * ATT_W
    dil = [d for _, d in ATT_GROUPS]

    def body(p_ref, g_ref, *refs):
        ins = refs[0:9]
        o_ref, dg_ref = refs[10], refs[11]
        bufs = refs[12:21]
        i = pl.program_id(0)

        @pl.when(i == 0)
        def _():
            dg_ref[...] = jnp.zeros_like(dg_ref)

        nat = [_rm_reader(bufs[j], ins[j], dil[j % 3]) for j in range(9)]
        dq_get, dk_get, dv_get = nat[0:3], nat[3:6], nat[6:9]
        for hd in range(2 * ATT_HEADS):
            sl = slice(hd * HD, (hd + 1) * HD)
            head = hd % ATT_HEADS
            grp, slot = head // ATT_HPG, head % ATT_HPG
            get = (dq_get if hd < ATT_HEADS else dk_get)[grp]

            def chunk(rows, sl=sl, slot=slot, get=get):
                dn = get(slot, rows)
                v = p_ref[rows, sl].astype(F32)
                rstd = lax.rsqrt(jnp.mean(v * v, axis=-1, keepdims=True) + EPS)
                vh = v * rstd
                dg_ref[:, sl] += jnp.sum(dn * vh, axis=0, keepdims=True)
                dvh = dn * g_ref[:, sl]
                o_ref[rows, sl] = (rstd * (dvh - vh * jnp.mean(dvh * vh, axis=-1, keepdims=True))).astype(o_ref.dtype)

            chunk(slice(None))
        for head in range(ATT_HEADS):
            grp, slot = head // ATT_HPG, head % ATT_HPG
            o_ref[:, W + head * HD:W + (head + 1) * HD] = dv_get[grp](slot, slice(None)).astype(o_ref.dtype)

    return pl.pallas_call(
        body, name="qknorm_bwd", grid=(S // tm,),
        in_specs=[pl.BlockSpec((tm, W), lambda i: (i, 0)), pl.BlockSpec((1, W), lambda i: (0, 0))]
        + [_rm_spec(tm, d, GW) for d in dil] * 3 + [pl.BlockSpec(memory_space=pl.ANY)],
        out_specs=[pl.BlockSpec((tm, 3 * ATT_W), lambda i: (i, 0)), pl.BlockSpec((1, W), lambda i: (0, 0))],
        out_shape=[_sds(dproj.shape, dproj.dtype), _sds((1, W), F32)],
        scratch_shapes=[pltpu.VMEM((ATT_HPG, tm, HD), F32)] * 9,
        input_output_aliases={11: 0},
        compiler_params=_params(1))(proj, gqk, *dqs, *dks, *dvs, dproj)


def _att_mask(n):
    qi = lax.broadcasted_iota(jnp.int32, (BLK, 2 * BLK), 0)
    kj = lax.broadcasted_iota(jnp.int32, (BLK, 2 * BLK), 1)
    dist = BLK + qi - kj
    valid_all = (dist >= 0) & (dist <= BLK)
    return valid_all & ((kj >= BLK) | (n > 0)), valid_all, dist.astype(F32)


def _att_slopes(grp):
    return [2.0 ** (-8.0 * (grp * ATT_HPG + hh + 1) / ATT_HEADS) for hh in range(ATT_HPG)]


ATT_PLANES_PER_STEP = 4


def _att_planes(d):
    return d if d > 1 else ATT_PLANES_PER_STEP


def _att_3d(a, d):
    return a.reshape(ATT_PLANES_PER_STEP, a.shape[0] // ATT_PLANES_PER_STEP, a.shape[1]) if d == 1 else a


def _att_spec(row_fn, col=0):
    return pl.BlockSpec((ATT_PLANES_PER_STEP, BLK, GW), lambda r, n: (r, row_fn(n), col))


def _att_chains():
    return [(rr * ATT_HPG + hh, rr, slice(hh * HD, (hh + 1) * HD), hh)
            for rr in range(ATT_PLANES_PER_STEP) for hh in range(ATT_HPG)]


def _att_qkv_specs(nb):
    last = nb - 1

    def cur(n):
        return jnp.minimum(n, last)

    def prev(n):
        return jnp.maximum(jnp.minimum(n, last) - 1, 0)

    return [_att_spec(cur, 0), _att_spec(prev, 1), _att_spec(cur, 1), _att_spec(prev, 2), _att_spec(cur, 2),
            _att_spec(lambda n: last, 1), _att_spec(lambda n: last, 2)]


def _att_prev(seg, n, prev_ref, last_ref, rr, sl):
    t = prev_ref[rr, :, sl]
    if seg and rr > 0:
        t = jnp.where(n == 0, last_ref[rr - 1, :, sl], t)
    return t


def _att_fwd(grp, S, qkv):
    _, d = ATT_GROUPS[grp]
    seg = d == 1
    P = _att_planes(d)
    L = S // P
    nb = L // BLK
    assert P % ATT_PLANES_PER_STEP == 0 and (not seg or P == ATT_PLANES_PER_STEP)
    slopes = _att_slopes(grp)
    scale = HD ** -0.5
    chains = _att_chains()

    def body(q_ref, kp_ref, kc_ref, vp_ref, vc_ref, kl_ref, vl_ref, o_ref, l_ref, s_buf, p_buf, den_buf):
        n = pl.program_id(1)
        valid, valid_all, distf = _att_mask(n)
        for c, rr, sl, hh in chains:
            k = jnp.concatenate([_att_prev(seg, n, kp_ref, kl_ref, rr, sl), kc_ref[rr, :, sl]], axis=0)
            s_buf[c] = _dot_nt(q_ref[rr, :, sl], k)
        for c, rr, sl, hh in chains:
            s = s_buf[c] * scale + (-slopes[hh] * d) * distf
            s = jnp.where(valid_all if seg and rr > 0 else valid, s, -1e30)
            m = jnp.max(s, axis=-1, keepdims=True)
            p = jnp.exp(s - m)
            den = jnp.sum(p, axis=-1, keepdims=True)
            p_buf[c] = p.astype(CDT)
            den_buf[c] = jnp.broadcast_to(den, (BLK, HD))
            l_ref[rr, :, sl] = jnp.broadcast_to(m + jnp.log(den), (BLK, HD))
        for c, rr, sl, hh in chains:
            v = jnp.concatenate([_att_prev(seg, n, vp_ref, vl_ref, rr, sl), vc_ref[rr, :, sl]], axis=0)
            o_ref[rr, :, sl] = _dot_nn(p_buf[c], v) / den_buf[c]

    out_spec = _att_spec(lambda n: n)
    n_ch = len(chains)
    q3 = _att_3d(qkv, d)
    o, l = pl.pallas_call(
        body, name="att_fwd_g%d" % grp, grid=(P // ATT_PLANES_PER_STEP, nb),
        in_specs=_att_qkv_specs(nb),
        out_specs=[out_spec, out_spec],
        out_shape=[_sds((P, L, GW), F32)] * 2,
        scratch_shapes=[pltpu.VMEM((n_ch, BLK, 2 * BLK), F32), pltpu.VMEM((n_ch, BLK, 2 * BLK), CDT),
                        pltpu.VMEM((n_ch, BLK, HD), F32)],
        compiler_params=_params(2),
    )(*[q3] * 7)
    return o.reshape(_rm_shape(S, d, GW)), l.reshape(_rm_shape(S, d, GW))


def _att_bwd(grp, S, qkv, lse, do_g, c_g):
    _, d = ATT_GROUPS[grp]
    seg = d == 1
    P = _att_planes(d)
    L = S // P
    nb = L // BLK
    assert P % ATT_PLANES_PER_STEP == 0 and (not seg or P == ATT_PLANES_PER_STEP)
    slopes = _att_slopes(grp)
    scale = HD ** -0.5
    last = nb - 1
    chains = _att_chains()

    def body(q_ref, kp_ref, kc_ref, vp_ref, vc_ref, kl_ref, vl_ref, l_ref, do_ref, c_ref, dq_ref, dk_ref, dv_ref,
             ck, cv, fk, fv, s_buf, dp_buf, p_buf, ds_buf):
        n = pl.program_id(1)

        @pl.when(n == 0)
        def _():
            for buf in (ck, cv, fk, fv):
                buf[...] = jnp.zeros_like(buf)

        @pl.when(n < nb)
        def _():
            valid, valid_all, distf = _att_mask(n)
            for c, rr, sl, hh in chains:
                k = jnp.concatenate([_att_prev(seg, n, kp_ref, kl_ref, rr, sl), kc_ref[rr, :, sl]], axis=0)
                v = jnp.concatenate([_att_prev(seg, n, vp_ref, vl_ref, rr, sl), vc_ref[rr, :, sl]], axis=0)
                s_buf[c] = _dot_nt(q_ref[rr, :, sl], k)
                dp_buf[c] = _dot_nt(do_ref[rr, :, sl], v)
            for c, rr, sl, hh in chains:
                s = s_buf[c] * scale + (-slopes[hh] * d) * distf
                p = jnp.where(valid_all if seg and rr > 0 else valid, jnp.exp(s - l_ref[rr, :, sl][:, 0:1]), 0.0)
                p_buf[c] = p.astype(CDT)
                ds_buf[c] = (p * (dp_buf[c] + c_ref[rr, :, sl][:, 0:1]) * scale).astype(CDT)
            for c, rr, sl, hh in chains:
                k = jnp.concatenate([_att_prev(seg, n, kp_ref, kl_ref, rr, sl), kc_ref[rr, :, sl]], axis=0)
                ds = ds_buf[c]
                dq_ref[rr, :, sl] = _dot_nn(ds, k)
                dk = _dot_tn(ds, q_ref[rr, :, sl])
                dv = _dot_tn(p_buf[c], do_ref[rr, :, sl])
                dk_ref[rr, :, sl] = ck[rr, :, sl] + dk[0:BLK]
                dv_ref[rr, :, sl] = cv[rr, :, sl] + dv[0:BLK]
                ck[rr, :, sl] = dk[BLK:2 * BLK]
                cv[rr, :, sl] = dv[BLK:2 * BLK]
                if seg and rr > 0:
                    @pl.when(n == 0)
                    def _(rr=rr, sl=sl, dk=dk, dv=dv):
                        fk[rr - 1, :, sl] = dk[0:BLK]
                        fv[rr - 1, :, sl] = dv[0:BLK]

        @pl.when(n == nb)
        def _():
            dk_ref[...] = ck[...] + fk[...]
            dv_ref[...] = cv[...] + fv[...]

    blk = (ATT_PLANES_PER_STEP, BLK, GW)
    at_q = _att_spec(lambda n: jnp.minimum(n, last))
    behind = _att_spec(lambda n: jnp.maximum(n - 1, 0))
    n_ch = len(chains)
    q3 = _att_3d(qkv, d)
    res = pl.pallas_call(
        body, name="att_bwd_g%d" % grp, grid=(P // ATT_PLANES_PER_STEP, nb + 1),
        in_specs=_att_qkv_specs(nb) + [at_q, at_q, at_q],
        out_specs=[at_q, behind, behind],
        out_shape=[_sds((P, L, GW), F32)] * 3,
        scratch_shapes=[pltpu.VMEM(blk, F32)] * 4
        + [pltpu.VMEM((n_ch, BLK, 2 * BLK), F32), pltpu.VMEM((n_ch, BLK, 2 * BLK), F32),
           pltpu.VMEM((n_ch, BLK, 2 * BLK), CDT), pltpu.VMEM((n_ch, BLK, 2 * BLK), CDT)],
        compiler_params=_params(2),
    )(*[q3] * 7, _att_3d(lse, d), _att_3d(do_g, d), _att_3d(c_g, d))
    return [t.reshape(_rm_shape(S, d, GW)) for t in res]


def _mix_alpha(l0, l1, l2):
    mx = jnp.maximum(jnp.maximum(l0, l1), l2)
    e = [jnp.exp(l0 - mx), jnp.exp(l1 - mx), jnp.exp(l2 - mx)]
    tot = e[0] + e[1] + e[2]
    return [ei / tot for ei in e]


def _mix_fwd(S, os_, ls_, tm=512):
    dil = [d for _, d in ATT_GROUPS]

    def body(*refs):
        out, bufs = refs[6], refs[7:13]
        get = [_rm_reader(bufs[j], refs[j], dil[j % 3]) for j in range(6)]
        rows = slice(None)
        for s in range(ATT_HPG):
            al = _mix_alpha(*[get[3 + g](s, rows) for g in range(3)])
            mixed = al[0] * get[0](s, rows) + al[1] * get[1](s, rows) + al[2] * get[2](s, rows)
            out[:, s * HD:(s + 1) * HD] = mixed.astype(out.dtype)

    specs = [_rm_spec(tm, d, GW) for d in dil]
    return pl.pallas_call(
        body, name="mix_fwd", grid=(S // tm,), in_specs=specs * 2, out_specs=pl.BlockSpec((tm, GW), lambda i: (i, 0)),
        out_shape=_sds((S, GW), CDT), scratch_shapes=[pltpu.VMEM((ATT_HPG, tm, HD), F32)] * 6,
        compiler_params=_params(1))(*os_, *ls_)


def _mix_bwd(S, os_, ls_, do_a, tm=512):
    dil = [d for _, d in ATT_GROUPS]

    def body(*refs):
        d_ref, outs, bufs, tmps = refs[6], refs[7:13], refs[13:19], refs[19:25]
        get = [_rm_reader(bufs[j], refs[j], dil[j % 3]) for j in range(6)]
        for s in range(ATT_HPG):
            cols = slice(s * HD, (s + 1) * HD)

            def chunk(rows, s=s, cols=cols):
                al = _mix_alpha(*[get[3 + g](s, rows) for g in range(3)])
                dv = d_ref[rows, cols]
                o_a = al[0] * get[0](s, rows) + al[1] * get[1](s, rows) + al[2] * get[2](s, rows)
                dsum = jnp.sum(dv * o_a, axis=-1, keepdims=True)
                for g in range(3):
                    tmps[g][rows, :] = al[g] * dv
                    tmps[3 + g][rows, :] = -(al[g] * dsum)

            chunk(slice(None))
            for j in range(6):
                _rm_put(outs[j], cols, tmps[j], dil[j % 3])

    specs = [_rm_spec(tm, d, GW) for d in dil]
    res = pl.pallas_call(
        body, name="mix_bwd", grid=(S // tm,), in_specs=specs * 2 + [pl.BlockSpec((tm, GW), lambda i: (i, 0))],
        out_specs=specs * 2,
        out_shape=[_sds(_rm_shape(S, d, GW), CDT) for d in dil] + [_sds(_rm_shape(S, d, GW), F32) for d in dil],
        scratch_shapes=[pltpu.VMEM((ATT_HPG, tm, HD), F32)] * 6 + [pltpu.VMEM((tm, HD), F32)] * 6,
        compiler_params=_params(1))(*os_, *ls_, do_a)
    return res[:3], res[3:]


def _ret_tables(dk):
    H, C = RET_HEADS, BLK
    log_g = jnp.log(1.0 - 2.0 ** (-5.0 - jnp.arange(H, dtype=F32)))
    idx = jnp.arange(C, dtype=F32)
    diff = idx[:, None] - idx[None, :]
    decay = jnp.where(diff >= 0, jnp.exp(log_g[:, None, None] * jnp.maximum(diff, 0.0)), 0.0)
    xi = jnp.exp(log_g[:, None] * (idx[None, :] + 1.0))
    zeta = jnp.exp(log_g[:, None] * (C - 1.0 - idx[None, :])) * (dk ** -0.5)
    g_chunk = jnp.exp(log_g * C)
    bc = lambda t: jnp.broadcast_to(t[:, :, None], (H, C, C))
    return decay, bc(xi), bc(zeta), jnp.broadcast_to(g_chunk[:, None, None], (H, 8, C))


def _gn_fwd(o, g, b):
    mu = jnp.mean(o, axis=-1, keepdims=True)
    xc = o - mu
    rstd = lax.rsqrt(jnp.mean(xc * xc, axis=-1, keepdims=True) + EPS)
    yh = xc * rstd
    return yh, rstd, yh * g + b


def _ret_specs(dk, dv, order):
    H = RET_HEADS
    qk_w, v_w = H * dk, H * dv
    off_q = 3 * ATT_W
    off_k, off_v, off_g = off_q + qk_w, off_q + 2 * qk_w, off_q + 2 * qk_w + v_w
    assert 2 * dk == dv and all(off % dv == 0 for off in (off_q, off_k, off_v, off_g))

    def col(off, j):
        return pl.BlockSpec((BLK, dv), lambda i: (order(i), off // dv + j))

    tab = pl.BlockSpec((H, BLK, BLK), lambda i: (0, 0, 0))
    return ([col(off_q, j) for j in range(H // 2)] + [col(off_k, j) for j in range(H // 2)]
            + [col(off_v, j) for j in range(H)] + [col(off_g, j) for j in range(H)]
            + [tab, tab, tab, pl.BlockSpec((H, 8, BLK), lambda i: (0, 0, 0))])


def _ret_heads(refs, dk):
    H = RET_HEADS
    q_refs, k_refs = refs[0:H // 2], refs[H // 2:H]
    v_refs, gr_refs = refs[H:2 * H], refs[2 * H:3 * H]

    def head(h):
        cols = slice((h % 2) * dk, (h % 2 + 1) * dk)
        return q_refs[h // 2][:, cols], k_refs[h // 2][:, cols], v_refs[h][...], gr_refs[h][...]

    return head, refs[3 * H:3 * H + 4]


def _ret_fwd(proj, gn_g, gn_b, dk, dv):
    S = proj.shape[0]
    N = S // BLK
    H = RET_HEADS
    kscale = dk ** -0.5
    n_in = 3 * H + 4

    def body(*refs):
        head, (dec_ref, xi_ref, zeta_ref, gc_ref) = _ret_heads(refs, dk)
        g_ref, b_ref, opre_ref, or_ref, st_ref, state, s_buf, cross_buf = refs[n_in:n_in + 8]
        n = pl.program_id(0)

        @pl.when(n == 0)
        def _():
            state[...] = jnp.zeros_like(state)

        for h in range(H):
            q, k, v, _ = head(h)
            s_buf[h] = _dot_nt(q, k)
            st = state[h]
            st_c = st.astype(CDT)
            st_ref[h] = st_c
            cross_buf[h] = _dot_nn(q, st_c)
            kz = (k.astype(F32) * zeta_ref[h][:, 0:1]).astype(CDT)
            state[h] = st * gc_ref[h][0:1, 0:1] + _dot_tn(kz, v)
        for h in range(H):
            vs = slice(h * dv, (h + 1) * dv)
            _, _, v, gr = head(h)
            s = s_buf[h] * kscale * dec_ref[h]
            o = _dot_nn(s.astype(CDT), v) + cross_buf[h] * xi_ref[h][:, 0:1]
            opre_ref[:, vs] = o
            _, _, y = _gn_fwd(o, g_ref[:, vs], b_ref[:, vs])
            gr = gr.astype(F32)
            or_ref[:, vs] = (y * (gr * _sigmoid(gr))).astype(or_ref.dtype)

    v_w = H * dv
    row = pl.BlockSpec((1, v_w), lambda i: (0, 0))
    tile = pl.BlockSpec((BLK, v_w), lambda i: (i, 0))
    return pl.pallas_call(
        body, name="ret_fwd", grid=(N,),
        in_specs=_ret_specs(dk, dv, lambda i: i) + [row, row],
        out_specs=[tile, tile, pl.BlockSpec((None, H, dk, dv), lambda i: (i, 0, 0, 0))],
        out_shape=[_sds((S, v_w), F32), _sds((S, v_w), CDT), _sds((N, H, dk, dv), CDT)],
        scratch_shapes=[pltpu.VMEM((H, dk, dv), F32), pltpu.VMEM((H, BLK, BLK), F32), pltpu.VMEM((H, BLK, dv), F32)],
        compiler_params=_params(1),
    )(*[proj] * (3 * H), *_ret_tables(dk), gn_g, gn_b)


def _ret_bwd(proj, gn_g, gn_b, o_pre, states, d_or, dga, dgb, dk, dv):
    S, in_w = proj.shape
    N = S // BLK
    H = RET_HEADS
    qk_w, v_w = H * dk, H * dv
    kscale = dk ** -0.5
    n_in = 3 * H + 4
    out_w = 2 * qk_w + 2 * v_w
    gate_w = dga.shape[1]
    col0 = 3 * ATT_W
    assert col0 + out_w + 2 * gate_w == in_w
    rev = lambda i: N - 1 - i

    def body(*refs):
        head, (dec_ref, xi_ref, zeta_ref, gc_ref) = _ret_heads(refs, dk)
        (g_ref, b_ref, opre_ref, st_ref, dor_ref, dga_ref, dgb_ref, dproj_ref, dg_ref, db_ref, dstate, stage,
         sem, do_buf, dox_buf, a_buf, g_buf, dq_buf, dk_buf, dv_buf) = refs[n_in:n_in + 20]
        i = pl.program_id(0)
        slot = i % 2
        out_ref = stage.at[slot]

        def out_copy(s, step):
            rows = pl.ds(pl.multiple_of(rev(step) * BLK, BLK), BLK)
            return pltpu.make_async_copy(stage.at[s], dproj_ref.at[rows, pl.ds(col0, in_w - col0)], sem.at[s])

        @pl.when(i >= 2)
        def _():
            out_copy(slot, i - 2).wait()

        @pl.when(i == 0)
        def _():
            dstate[...] = jnp.zeros_like(dstate)
            dg_ref[...] = jnp.zeros_like(dg_ref)
            db_ref[...] = jnp.zeros_like(db_ref)

        out_ref[:, out_w:out_w + gate_w] = dga_ref[...]
        out_ref[:, out_w + gate_w:out_w + 2 * gate_w] = dgb_ref[...]
        for h in range(H):
            vs = slice(h * dv, (h + 1) * dv)
            _, _, _, gr = head(h)
            gr = gr.astype(F32)
            sg = _sigmoid(gr)
            gain = g_ref[:, vs]
            yh, rstd, y = _gn_fwd(opre_ref[:, vs], gain, b_ref[:, vs])
            d_or_v = dor_ref[:, vs]
            dy = d_or_v * (gr * sg)
            out_ref[:, 2 * qk_w + v_w + h * dv:2 * qk_w + v_w + (h + 1) * dv] = (
                d_or_v * y * (sg * (1.0 + gr * (1.0 - sg)))).astype(out_ref.dtype)
            dg_ref[:, vs] += jnp.sum(dy * yh, axis=0, keepdims=True)
            db_ref[:, vs] += jnp.sum(dy, axis=0, keepdims=True)
            dyh = dy * gain
            do = rstd * (dyh - jnp.mean(dyh, axis=-1, keepdims=True)
                         - yh * jnp.mean(dyh * yh, axis=-1, keepdims=True))
            do_buf[h] = do.astype(CDT)
            dox_buf[h] = (do * xi_ref[h][:, 0:1]).astype(CDT)
        for h in range(H):
            q, k, v, _ = head(h)
            dox = dox_buf[h]
            a_buf[h] = _dot_nt(q, k)
            g_buf[h] = _dot_nt(do_buf[h], v)
            dsn = dstate[h]
            dsn_c = dsn.astype(CDT)
            kz = (k.astype(F32) * zeta_ref[h][:, 0:1]).astype(CDT)
            dq_buf[h] = _dot_nt(dox, st_ref[h])
            dk_buf[h] = _dot_nt(v, dsn_c)
            dv_buf[h] = _dot_nn(kz, dsn_c)
            dstate[h] = dsn * gc_ref[h][0:1, 0:1] + _dot_tn(q, dox)
        for h in range(H):
            q, k, _, _ = head(h)
            decay = dec_ref[h]
            a_c = (a_buf[h] * kscale * decay).astype(CDT)
            g_c = (g_buf[h] * decay).astype(CDT)
            dq = _dot_nn(g_c, k) * kscale + dq_buf[h]
            dkk = _dot_tn(g_c, q) * kscale + dk_buf[h] * zeta_ref[h][:, 0:1]
            dvv = _dot_tn(a_c, do_buf[h]) + dv_buf[h]
            out_ref[:, h * dk:(h + 1) * dk] = dq.astype(out_ref.dtype)
            out_ref[:, qk_w + h * dk:qk_w + (h + 1) * dk] = dkk.astype(out_ref.dtype)
            out_ref[:, 2 * qk_w + h * dv:2 * qk_w + (h + 1) * dv] = dvv.astype(out_ref.dtype)

        cp = out_copy(slot, i)
        cp.start()

        @pl.when(i == N - 1)
        def _():
            cp.wait()
            if N >= 2:
                out_copy(1 - slot, i - 1).wait()

    row = pl.BlockSpec((1, v_w), lambda i: (0, 0))
    tile = pl.BlockSpec((BLK, v_w), lambda i: (rev(i), 0))
    gate = pl.BlockSpec((BLK, gate_w), lambda i: (rev(i), 0))
    return pl.pallas_call(
        body, name="ret_bwd", grid=(N,),
        in_specs=_ret_specs(dk, dv, rev) + [row, row, tile,
                 pl.BlockSpec((None, H, dk, dv), lambda i: (rev(i), 0, 0, 0)), tile, gate, gate],
        out_specs=[pl.BlockSpec(memory_space=pl.ANY), row, row],
        out_shape=[_sds((S, in_w), CDT), _sds((1, v_w), F32), _sds((1, v_w), F32)],
        scratch_shapes=[pltpu.VMEM((H, dk, dv), F32), pltpu.VMEM((2, BLK, in_w - col0), CDT),
                        pltpu.SemaphoreType.DMA((2,)),
                        pltpu.VMEM((H, BLK, dv), CDT), pltpu.VMEM((H, BLK, dv), CDT),
                        pltpu.VMEM((H, BLK, BLK), F32), pltpu.VMEM((H, BLK, BLK), F32),
                        pltpu.VMEM((H, BLK, dk), F32), pltpu.VMEM((H, BLK, dk), F32), pltpu.VMEM((H, BLK, dv), F32)],
        compiler_params=_params(1),
    )(*[proj] * (3 * H), *_ret_tables(dk), gn_g, gn_b, o_pre, states, d_or, dga, dgb)


def _merge_fwd(o_a, o_r, wa, wb, proj, d_model, tm=1024, tn=512):
    S, in_w = proj.shape
    off_a, off_b = in_w - 2 * d_model, in_w - d_model
    assert off_a % tn == 0 and off_b % tn == 0

    def body(oa_ref, or_ref, wa_ref, wb_ref, ga_ref, gb_ref, y_ref, pa_ref, pb_ref):
        pa = _dot_nn(oa_ref[...], wa_ref[...])
        pb = _dot_nn(or_ref[...], wb_ref[...])
        y = _sigmoid(ga_ref[...].astype(F32)) * pa + _sigmoid(gb_ref[...].astype(F32)) * pb
        y_ref[...] = y.astype(y_ref.dtype)
        pa_ref[...] = pa.astype(pa_ref.dtype)
        pb_ref[...] = pb.astype(pb_ref.dtype)

    ka, kb = o_a.shape[1], o_r.shape[1]
    out = pl.BlockSpec((tm, tn), lambda i, j: (i, j))
    return pl.pallas_call(
        body, name="merge_fwd", grid=(S // tm, d_model // tn),
        in_specs=[pl.BlockSpec((tm, ka), lambda i, j: (i, 0)), pl.BlockSpec((tm, kb), lambda i, j: (i, 0)),
                  pl.BlockSpec((ka, tn), lambda i, j: (0, j)), pl.BlockSpec((kb, tn), lambda i, j: (0, j)),
                  pl.BlockSpec((tm, tn), lambda i, j: (i, off_a // tn + j)),
                  pl.BlockSpec((tm, tn), lambda i, j: (i, off_b // tn + j))],
        out_specs=[out, out, out], out_shape=[_sds((S, d_model), CDT)] * 3,
        compiler_params=_params(2))(o_a, o_r, wa, wb, proj, proj)


def _d_x(dproj, w_in, x, g, dx1, dep, tm=512, row_groups=2):
    S, D = x.shape
    n_sh, _, ns = w_in.shape
    nt = S // tm // row_groups
    deps = [d for d in (dep,) if d is not None]

    def body(a_ref, b_ref, x_ref, g_ref, r_ref, *rest):
        dx_ref, dg_ref, acc = rest[len(deps):]
        h, k, i = pl.program_id(0), pl.program_id(1), pl.program_id(2)

        @pl.when(k == 0)
        def _():
            acc[i] = jnp.zeros((tm, D), F32)

        acc[i] += _dot_nt(a_ref[...], b_ref[...])

        @pl.when(k == n_sh - 1)
        def _():
            _ep_rms_bwd(acc[i], (x_ref, g_ref, r_ref), (dx_ref, dg_ref), h * nt + i)

    def last_only(h, k, i):
        return (h * nt + jnp.where(k == n_sh - 1, i, 0), 0)

    return pl.pallas_call(
        body, name="d_x", grid=(row_groups, n_sh, nt),
        in_specs=[pl.BlockSpec((tm, ns), lambda h, k, i: (h * nt + i, k)),
                  pl.BlockSpec((None, D, ns), lambda h, k, i: (k, 0, 0)),
                  pl.BlockSpec((tm, D), last_only), pl.BlockSpec((1, D), lambda h, k, i: (0, 0)),
                  pl.BlockSpec((tm, D), last_only)] + [pl.BlockSpec(memory_space=pl.ANY)] * len(deps),
        out_specs=[pl.BlockSpec((tm, D), last_only), pl.BlockSpec((1, D), lambda h, k, i: (0, 0))],
        out_shape=[_sds((S, D), F32), _sds((1, D), F32)],
        scratch_shapes=[pltpu.VMEM((nt, tm, D), F32)],
        compiler_params=_params(3))(dproj, w_in, x, g, dx1, *deps)


def _local_step(x, target, w_in_mine, chip, others, near_w_in, far_w_in, small, late_weights, on_grads, deps0=()):
    S, D = x.shape
    ns_in = w_in_mine.shape[1]
    in_w = N_CHIPS * ns_in
    d_ff = 4 * D
    ret_v_w = 2 * D
    dv = ret_v_w // RET_HEADS
    dk = (in_w - 3 * ATT_W - 2 * ret_v_w - 2 * D) // (2 * RET_HEADS)
    gqk = jnp.concatenate([small["q_norm_g"].reshape(1, ATT_W), small["k_norm_g"].reshape(1, ATT_W)], axis=1)
    g1, g2 = small["norm1_g"], small["norm2_g"]
    gn_g, gn_b = small["ret_gn_g"], small["ret_gn_b"]

    proj_sds = _sds((S, in_w), CDT)
    xn, proj = _in_proj_mine(x, g1, w_in_mine, chip, proj_sds, deps0)
    for stage, (get_w_in, chips) in enumerate(((near_w_in, others[:2]), (far_w_in, others[2:]))):
        w_in, started = get_w_in(proj)
        (proj,) = _matmul(
            "in_proj_far%d" % stage, "nn", xn, w_in, tm=512, tn=ns_in, tk=D, prefetch=[chips],
            n_cols=chips.shape[0] * ns_in, b_spec=pl.BlockSpec((None, D, ns_in), lambda i, j, k, o: (o[j], 0, 0)),
            outs=[(proj_sds, pl.BlockSpec((512, ns_in), lambda i, j, k, o: (i, o[j])))], epilogue=_ep_store,
            deps=[proj, started], alias_dep_to_out=(0, 0), j_outer=True)
    qkv = _qknorm_fwd(proj, gqk)
    att = [_att_fwd(g, S, qkv[g]) for g in range(3)]
    os_, ls_ = [a[0] for a in att], [a[1] for a in att]
    o_a = _mix_fwd(S, os_, ls_)
    o_pre, o_r, states = _ret_fwd(proj, gn_g, gn_b, dk, dv)
    w = late_weights(o_r)
    y, pa, pb = _merge_fwd(o_a, o_r, w["w_proj_a"], w["w_proj_b"], proj, D)
    x1, xn2 = _matmul("out_proj", "nn", y, w["w_out"], tm=1024, tn=D, tk=D,
                      extras=[(x, _mn(1024, D)), (g2, _row(D))],
                      outs=[(_sds((S, D), F32), _mn(1024, D)), (_sds((S, D), CDT), _mn(1024, D))],
                      epilogue=_ep_resid_norm)
    hid, act = _matmul("mlp_up", "nn", xn2, w["w_up"], tm=1024, tn=2048, tk=D, j_outer=True,
                       outs=[(_sds((S, d_ff), CDT), _mn(1024, 2048))] * 2, epilogue=_ep_up)
    dx2, dx2c, loss_row = _matmul(
        "mlp_down_loss", "nn", act, w["w_down"], tm=512, tn=D, tk=d_ff,
        extras=[(x1, _mn(512, D)), (target, _mn(512, D))],
        outs=[(_sds((S, D), F32), _mn(512, D)), (_sds((S, D), CDT), _mn(512, D)), (_sds((1, D), F32), _row(D))],
        epilogue=functools.partial(_ep_down_loss, inv_d=1.0 / D))
    loss = 0.5 * jnp.sum(loss_row) / D

    (dh,) = _matmul("d_hidden", "nt", dx2c, w["w_down"], tm=1024, tn=2048, tk=D, j_outer=True,
                    extras=[(hid, _mn(1024, 2048))], outs=[(_sds((S, d_ff), CDT), _mn(1024, 2048))],
                    epilogue=_ep_dh)
    (gw_down,) = _matmul("dw_down", "tn", act, dx2c, tm=1024, tn=D, tk=2048,
                         outs=[(_sds((d_ff, D), F32), _mn(1024, D))], epilogue=_ep_store)
    (gw_up,) = _matmul("dw_up", "tn", xn2, dh, tm=D, tn=1024, tk=2048,
                       outs=[(_sds((D, d_ff), F32), _mn(D, 1024))], epilogue=_ep_store)
    tok = on_grads({"w_down": gw_down, "w_up": gw_up})
    dx1, dx1c, dg2 = _matmul(
        "d_x1", "nt", dh, w["w_up"], tm=512, tn=D, tk=d_ff,
        extras=[(x1, _mn(512, D)), (g2, _row(D)), (dx2, _mn(512, D))],
        outs=[(_sds((S, D), F32), _mn(512, D)), (_sds((S, D), CDT), _mn(512, D)), (_sds((1, D), F32), _row(D))],
        epilogue=_ep_rms_bwd, deps=[tok])

    gt = 512
    assert (in_w - 2 * D) % gt == 0
    off_a, off_b = (in_w - 2 * D) // gt, (in_w - D) // gt
    dpa, dpb, dga, dgb = _matmul(
        "d_gates", "nt", dx1c, w["w_out"], tm=1024, tn=gt, tk=D,
        extras=[(proj, _mn(1024, gt, off_a)), (proj, _mn(1024, gt, off_b)), (pa, _mn(1024, gt)),
                (pb, _mn(1024, gt))],
        outs=[(_sds((S, D), CDT), _mn(1024, gt))] * 4, epilogue=_ep_gates)
    (gw_out,) = _matmul("dw_out", "tn", y, dx1c, tm=D, tn=D, tk=1024,
                        outs=[(_sds((D, D), F32), _mn(D, D))], epilogue=_ep_store)
    (gw_pa,) = _matmul("dw_proj_a", "tn", o_a, dpa, tm=GW, tn=D, tk=1024,
                       outs=[(_sds((GW, D), F32), _mn(GW, D))], epilogue=_ep_store)
    (gw_pb,) = _matmul("dw_proj_b", "tn", o_r, dpb, tm=1024, tn=D, tk=2048,
                       outs=[(_sds((ret_v_w, D), F32), _mn(1024, D))], epilogue=_ep_store)
    (do_a,) = _matmul("d_o_a", "nt", dpa, w["w_proj_a"], tm=1024, tn=GW, tk=D,
                      outs=[(_sds((S, GW), F32), _mn(1024, GW))], epilogue=_ep_store)
    tok = on_grads({"w_out": gw_out, "w_proj_a": gw_pa, "w_proj_b": gw_pb})
    (d_or,) = _matmul("d_o_r", "nt", dpb, w["w_proj_b"], tm=512, tn=ret_v_w, tk=D,
                      outs=[(_sds((S, ret_v_w), F32), _mn(512, ret_v_w))], epilogue=_ep_store, deps=[tok])

    dproj, dgn_g, dgn_b = _ret_bwd(proj, gn_g, gn_b, o_pre, states, d_or, dga, dgb, dk, dv)
    do_gs, c_gs = _mix_bwd(S, os_, ls_, do_a)
    datt_parts = [_att_bwd(g, S, qkv[g], ls_[g], do_gs[g], c_gs[g]) for g in range(3)]
    dproj, dgqk = _qknorm_bwd(proj, gqk, [p[0] for p in datt_parts], [p[1] for p in datt_parts],
                              [p[2] for p in datt_parts], dproj)

    (gw_in,) = _matmul(
        "dw_in", "tn", xn, dproj, tm=512, tn=ns_in, tk=1024,
        outs=[(_sds((N_CHIPS, D, ns_in), F32), pl.BlockSpec((None, 512, ns_in), lambda i, j, k: (j, i, 0)))],
        epilogue=_ep_store)
    tok = on_grads({"w_in": gw_in})
    grad_x, dg1 = _d_x(dproj, w_in, x, g1, dx1, tok)

    smallg = {"norm1_g": dg1, "q_norm_g": dgqk[:, :ATT_W], "k_norm_g": dgqk[:, ATT_W:],
              "ret_gn_g": dgn_g, "ret_gn_b": dgn_b, "norm2_g": dg2}
    return loss, grad_x, smallg


N_CHIPS = 4
N_DEV = 8


def _place():
    x, y, c = lax.axis_index("x"), lax.axis_index("y"), lax.axis_index("c")
    return x, y, c


def _other_chips(x, y):
    out = []
    for fx, fy in ((1, 0), (0, 1), (1, 1)):
        px = 1 - x if fx else x
        py = 1 - y if fy else y
        out.append(((px, py), 2 * px + py))
    return out


SEM_SPEC = pl.BlockSpec(memory_space=pltpu.SEMAPHORE)
ANY_SPEC = pl.BlockSpec(memory_space=pl.ANY)
EFFECT = pltpu.SideEffectType.DATAFLOW_SIDE_EFFECTING


def _ici_copies(kind, srcs, lands, send, recv, which=(0, 1, 2)):
    x, y, c = _place()
    me = 2 * x + y
    out = []
    for w, (s, l) in enumerate(zip(srcs, lands)):
        for j, ((px, py), pidx) in enumerate(_other_chips(x, y)):
            if j not in which:
                continue
            if kind == "gather":
                half = s.shape[0] // 2
                rows = pl.ds(c * half, half)
                src, dst_there, dst_here = s.at[rows, :], l.at[me, rows, :], l.at[pidx, rows, :]
            else:
                src, dst_there, dst_here = s.at[pidx], l.at[me], l.at[pidx]
            out.append((src, dst_there, dst_here, send.at[3 * w + j], recv.at[3 * w + j], (px, py, c)))
    return out


def _exchange_start(name, kind, srcs, land_shapes, which=(0, 1, 2), lands=None):
    n = len(srcs)
    if lands is None:
        lands = [lax.empty(shape, dtype) for shape, dtype in land_shapes]

    def body(*refs):
        src_refs, land_refs = refs[:n], refs[n:2 * n]
        send, recv = refs[2 * n], refs[2 * n + 1]
        token = refs[-1]
        for src, dst, _, ss, rs, dev in _ici_copies(kind, src_refs, land_refs, send, recv, which):
            pltpu.make_async_remote_copy(src_ref=src, dst_ref=dst, send_sem=ss, recv_sem=rs, device_id=dev,
                                         device_id_type=MESH).start()
        token[...] = jnp.zeros_like(token)

    thru = [pltpu.HBM(s.shape, s.dtype) for s in srcs] + [pltpu.HBM(shape, dtype) for shape, dtype in land_shapes]
    res = pl.pallas_call(
        body, name=name,
        out_shape=(pltpu.SemaphoreType.DMA((3 * n,)), pltpu.SemaphoreType.DMA((3 * n,)), *thru, _sds((8, LANES), F32)),
        in_specs=[HBM_SPEC] * (2 * n), out_specs=(SEM_SPEC, SEM_SPEC, *[HBM_SPEC] * (2 * n), VMEM_SPEC),
        input_output_aliases={i: 2 + i for i in range(2 * n)},
        compiler_params=pltpu.CompilerParams(has_side_effects=EFFECT),
    )(*[pltpu.with_memory_space_constraint(s, pltpu.HBM) for s in srcs],
      *[pltpu.with_memory_space_constraint(l, pltpu.HBM) for l in lands])
    return res[0], res[1], list(res[2:2 + n]), list(res[2 + n:2 + 2 * n]), res[-1]


def _exchange_wait(name, kind, send, recv, srcs, lands, after, which=(0, 1, 2)):
    n = len(srcs)

    def body(*refs):
        src_refs, land_refs = refs[:n], refs[n:2 * n]
        send_ref, recv_ref = refs[2 * n], refs[2 * n + 1]
        for src, _, dst, ss, rs, dev in _ici_copies(kind, src_refs, land_refs, send_ref, recv_ref, which):
            cp = pltpu.make_async_remote_copy(src_ref=src, dst_ref=dst, send_sem=ss, recv_sem=rs, device_id=dev,
                                              device_id_type=MESH)
            cp.wait_send()
            cp.wait_recv()

    thru = [pltpu.HBM(t.shape, t.dtype) for t in list(srcs) + list(lands)]
    res = pl.pallas_call(
        body, name=name, out_shape=thru,
        in_specs=[HBM_SPEC] * (2 * n) + [SEM_SPEC, SEM_SPEC, ANY_SPEC], out_specs=[HBM_SPEC] * (2 * n),
        input_output_aliases={i: i for i in range(2 * n)},
        compiler_params=pltpu.CompilerParams(has_side_effects=EFFECT),
    )(*srcs, *lands, send, recv, after)
    return list(res[:n]), list(res[n:])


PAIR_TILE_ELEMS = 1 << 20


def _pair_fill(name, gathered, mine, core, others, chip, write_mine=True):
    k, r, C = gathered.shape
    half = r // 2
    tr = _row_tile(half, C, PAIR_TILE_ELEMS, mult=16)
    nt = half // tr
    n_far = others.shape[0]

    def body(c_ref, o_ref, chip_ref, in_ref, mine_ref, out_ref, slot, send, recv):
        j = pl.program_id(0)
        _sibling_barrier((j == 0) & (pl.program_id(1) == 0))
        b = (j * nt + pl.program_id(1)) % 2
        x, y, c = _place()
        cp = pltpu.make_async_remote_copy(src_ref=in_ref, dst_ref=slot.at[b], send_sem=send.at[b],
                                          recv_sem=recv.at[b], device_id=(x, y, 1 - c), device_id_type=MESH)

        @pl.when(j < n_far)
        def _():
            cp.start()
            cp.wait_recv()
            out_ref[...] = slot[b]
            cp.wait_send()

        @pl.when(j >= n_far)
        def _():
            out_ref[...] = mine_ref[...]

    def far(j):
        return jnp.minimum(j, n_far - 1)

    grid_spec = pltpu.PrefetchScalarGridSpec(
        num_scalar_prefetch=3, grid=(n_far + (2 if write_mine else 0), nt),
        in_specs=[pl.BlockSpec((tr, C), lambda j, i, c, o, m: (
                      (2 * o[far(j)] + c[0]) * nt + jnp.where(j < n_far, i, nt - 1), 0)),
                  pl.BlockSpec((tr, C), lambda j, i, c, o, m: (jnp.where(j < n_far, 0, (j - n_far) * nt + i), 0))],
        out_specs=pl.BlockSpec((tr, C), lambda j, i, c, o, m: (
            jnp.where(j < n_far, 2 * o[far(j)] + 1 - c[0], 2 * m[0] + j - n_far) * nt + i, 0)),
        scratch_shapes=[pltpu.VMEM((2, tr, C), gathered.dtype), pltpu.SemaphoreType.DMA((2,)),
                        pltpu.SemaphoreType.DMA((2,))])
    out = pl.pallas_call(body, name=name, grid_spec=grid_spec, out_shape=_sds((k * r, C), gathered.dtype),
                         input_output_aliases={3: 0}, compiler_params=_params(2, PAIR_FILL_ID))(
                             core, others, chip, gathered.reshape(k * r, C), mine)
    return out.reshape(k, r, C)


def _pair_reduce(name, g, core):
    k, R, C = g.shape
    half = R // 2
    tr = _row_tile(half, C, PAIR_TILE_ELEMS, mult=16)
    nt = half // tr

    def body(c_ref, mine_ref, give_ref, out_ref, wire_ref, stage, slot, send, recv):
        _sibling_barrier((pl.program_id(0) == 0) & (pl.program_id(1) == 0))
        b = (pl.program_id(0) * nt + pl.program_id(1)) % 2
        x, y, c = _place()
        stage[b] = give_ref[...].astype(stage.dtype)
        cp = pltpu.make_async_remote_copy(src_ref=stage.at[b], dst_ref=slot.at[b], send_sem=send.at[b],
                                          recv_sem=recv.at[b], device_id=(x, y, 1 - c), device_id_type=MESH)
        cp.start()
        cp.wait_recv()
        tot = mine_ref[...] + slot[b].astype(F32)
        out_ref[...] = tot
        wire_ref[...] = tot.astype(wire_ref.dtype)
        cp.wait_send()

    blk = (tr, C)
    out_spec = pl.BlockSpec(blk, lambda s, i, c: (s * nt + i, 0))
    grid_spec = pltpu.PrefetchScalarGridSpec(
        num_scalar_prefetch=1, grid=(k, nt),
        in_specs=[pl.BlockSpec(blk, lambda s, i, c: ((2 * s + c[0]) * nt + i, 0)),
                  pl.BlockSpec(blk, lambda s, i, c: ((2 * s + 1 - c[0]) * nt + i, 0))],
        out_specs=[out_spec, out_spec],
        scratch_shapes=[pltpu.VMEM((2, tr, C), CDT), pltpu.VMEM((2, tr, C), CDT), pltpu.SemaphoreType.DMA((2,)),
                        pltpu.SemaphoreType.DMA((2,))])
    g2 = g.reshape(k * R, C)
    out, wire = pl.pallas_call(body, name=name, grid_spec=grid_spec,
                               out_shape=[_sds((k * half, C), F32), _sds((k * half, C), CDT)],
                               compiler_params=_params(2, PAIR_REDUCE_ID))(core, g2, g2)
    return out, wire.reshape(k, half, C)


def _all_reduce_small(v):
    r, cdim = v.shape

    def body(v_ref, o_ref, buf, send, recv):
        x, y, c = _place()
        me = 4 * x + 2 * y + c
        buf[me] = v_ref[...]
        sends = []
        for m in range(1, N_DEV):
            px = 1 - x if m & 4 else x
            py = 1 - y if m & 2 else y
            pc = 1 - c if m & 1 else c
            cp = pltpu.make_async_remote_copy(src_ref=v_ref, dst_ref=buf.at[me], send_sem=send.at[m - 1],
                                              recv_sem=recv.at[m - 1], device_id=(px, py, pc), device_id_type=MESH)
            cp.start()
            sends.append((cp, 4 * px + 2 * py + pc))
        for m, (cp, pidx) in enumerate(sends):
            pltpu.make_async_remote_copy(src_ref=v_ref, dst_ref=buf.at[pidx], send_sem=send.at[m], recv_sem=recv.at[m],
                                         device_id=(x, y, c), device_id_type=MESH).wait_recv()
        for cp, _ in sends:
            cp.wait_send()
        tot = buf[0]
        for k in range(1, N_DEV):
            tot = tot + buf[k]
        o_ref[...] = tot

    return pl.pallas_call(
        body, name="all_reduce_small", in_specs=[VMEM_SPEC], out_specs=VMEM_SPEC,
        out_shape=_sds((r, cdim), F32),
        scratch_shapes=[pltpu.VMEM((N_DEV, r, cdim), F32), pltpu.SemaphoreType.DMA((N_DEV - 1,)),
                        pltpu.SemaphoreType.DMA((N_DEV - 1,))],
    )(v)


def _row_tile(rows, cols, budget_elems=1 << 18, mult=8):
    if rows % mult:
        return rows
    t = max(mult, (budget_elems // cols) // mult * mult)
    while rows % t:
        t -= mult
    return t


def _adamw_update(w, g, m, v):
    nm = ADAM_B1 * m + (1.0 - ADAM_B1) * g
    nv = ADAM_B2 * v + (1.0 - ADAM_B2) * (g * g)
    m_hat = nm / (1.0 - ADAM_B1 ** ADAM_STEP)
    v_hat = nv / (1.0 - ADAM_B2 ** ADAM_STEP)
    return -ADAM_LR * (m_hat / (jnp.sqrt(v_hat) + ADAM_EPS) + ADAM_WD * w), nm, nv


def _adamw(name, w, g, m, v):
    R, C = w.shape
    tr = _row_tile(R, C, 1 << 18)

    def body(w_ref, g_ref, m_ref, v_ref, d_ref, nm_ref, nv_ref):
        d_ref[...], nm_ref[...], nv_ref[...] = _adamw_update(w_ref[...], g_ref[...], m_ref[...], v_ref[...])

    spec = pl.BlockSpec((tr, C), lambda i: (i, 0))
    return pl.pallas_call(body, name=name, grid=(R // tr,), in_specs=[spec] * 4, out_specs=[spec] * 3,
                          out_shape=[_sds((R, C), F32)] * 3, compiler_params=_params(1))(w, g, m, v)


def _sum_share(name, own, by_chip, chip, others, core):
    k, half, C = by_chip.shape
    tr = _row_tile(half, C, PAIR_TILE_ELEMS // 2, mult=16)
    nt = half // tr

    def body(chip_ref, oth_ref, c_ref, own_ref, a_ref, b_ref, cc_ref, g_out, mine, slot, send, recv):
        p = pl.program_id(1)
        _sibling_barrier((pl.program_id(0) == 0) & (p == 0))
        b = pl.program_id(0) % 2
        x, y, c = _place()
        cp = pltpu.make_async_remote_copy(src_ref=mine.at[b], dst_ref=slot.at[b], send_sem=send.at[b],
                                          recv_sem=recv.at[b], device_id=(x, y, 1 - c), device_id_type=MESH)

        @pl.when(p == 0)
        def _():
            tot = ((own_ref[...] + a_ref[...].astype(F32)) + b_ref[...].astype(F32)) + cc_ref[...].astype(F32)
            mine[b] = tot
            cp.start()
            g_out[...] = tot

        @pl.when(p == 1)
        def _():
            cp.wait_recv()
            g_out[...] = slot[b]
            cp.wait_send()

    def piece(j):
        return pl.BlockSpec((tr, C), lambda i, p, chip, oth, c: (oth[j] * nt + i, 0))

    grid_spec = pltpu.PrefetchScalarGridSpec(
        num_scalar_prefetch=3, grid=(nt, 2),
        in_specs=[pl.BlockSpec((tr, C), lambda i, p, chip, oth, c: (chip[0] * nt + i, 0)),
                  piece(0), piece(1), piece(2)],
        out_specs=pl.BlockSpec((tr, C), lambda i, p, chip, oth, c: (
            jnp.where(p == 0, c[0], 1 - c[0]) * nt + i, 0)),
        scratch_shapes=[pltpu.VMEM((2, tr, C), F32), pltpu.VMEM((2, tr, C), F32), pltpu.SemaphoreType.DMA((2,)),
                        pltpu.SemaphoreType.DMA((2,))])
    by2 = by_chip.reshape(k * half, C)
    return pl.pallas_call(body, name=name, grid_spec=grid_spec, out_shape=_sds((2 * half, C), F32),
                          compiler_params=_params(2, SUM_SHARE_ID))(chip, others, core, own, by2, by2, by2)


BIG = ("w_in", "w_proj_a", "w_proj_b", "w_out", "w_up", "w_down")
COL_SHARDED = ("w_in", "w_proj_a", "w_up")
SMALL = ("norm1_g", "q_norm_g", "k_norm_g", "ret_gn_g", "ret_gn_b", "norm2_g")
ALL_W = ("norm1_g", "w_in", "q_norm_g", "k_norm_g", "ret_gn_g", "ret_gn_b", "w_proj_a", "w_proj_b", "w_out",
         "norm2_g", "w_up", "w_down")
LANES = 128


def _to_full(name, gathered):
    k, r, c = gathered.shape
    if name in COL_SHARDED:
        return gathered.transpose(1, 0, 2).reshape(r, k * c)
    return gathered.reshape(k * r, c)


def _to_shard_major(name, full):
    if name in COL_SHARDED:
        r, c4 = full.shape
        return full.reshape(r, N_CHIPS, c4 // N_CHIPS).transpose(1, 0, 2)
    r4, c = full.shape
    return full.reshape(N_CHIPS, r4 // N_CHIPS, c)


def kernel(x, norm1_g, w_in, q_norm_g, k_norm_g, ret_gn_g, ret_gn_b, w_proj_a, w_proj_b, w_out, norm2_g, w_up, w_down, loss_target, m_norm1_g, m_w_in, m_q_norm_g, m_k_norm_g, m_ret_gn_g, m_ret_gn_b, m_w_proj_a, m_w_proj_b, m_w_out, m_norm2_g, m_w_up, m_w_down, v_norm1_g, v_w_in, v_q_norm_g, v_k_norm_g, v_ret_gn_g, v_ret_gn_b, v_w_proj_a, v_w_proj_b, v_w_out, v_norm2_g, v_w_up, v_w_down):
    weights = dict(norm1_g=norm1_g, w_in=w_in, q_norm_g=q_norm_g, k_norm_g=k_norm_g, ret_gn_g=ret_gn_g,
                   ret_gn_b=ret_gn_b, w_proj_a=w_proj_a, w_proj_b=w_proj_b, w_out=w_out, norm2_g=norm2_g,
                   w_up=w_up, w_down=w_down)
    moments_m = dict(norm1_g=m_norm1_g, w_in=m_w_in, q_norm_g=m_q_norm_g, k_norm_g=m_k_norm_g, ret_gn_g=m_ret_gn_g,
                     ret_gn_b=m_ret_gn_b, w_proj_a=m_w_proj_a, w_proj_b=m_w_proj_b, w_out=m_w_out,
                     norm2_g=m_norm2_g, w_up=m_w_up, w_down=m_w_down)
    moments_v = dict(norm1_g=v_norm1_g, w_in=v_w_in, q_norm_g=v_q_norm_g, k_norm_g=v_k_norm_g, ret_gn_g=v_ret_gn_g,
                     ret_gn_b=v_ret_gn_b, w_proj_a=v_w_proj_a, w_proj_b=v_w_proj_b, w_out=v_w_out,
                     norm2_g=v_norm2_g, w_up=v_w_up, w_down=v_w_down)

    mx, my = lax.axis_index("x"), lax.axis_index("y")
    core = lax.axis_index("c").astype(jnp.int32).reshape(1)
    chip = (2 * mx + my).astype(jnp.int32).reshape(1)
    others = jnp.stack([2 * (1 - mx) + my, 2 * mx + 1 - my, 2 * (1 - mx) + 1 - my]).astype(jnp.int32)
    shards = {n: weights[n][0].astype(CDT) for n in BIG}
    def start_gather(name, names):
        return _exchange_start(name, "gather", [shards[n] for n in names],
                               [((N_CHIPS,) + shards[n].shape, CDT) for n in names])

    w_in_shape = [((N_CHIPS,) + shards["w_in"].shape, CDT)]
    n_send, n_recv, n_srcs, n_lands, n_token = _exchange_start(
        "gather_w_in_near_start", "gather", [shards["w_in"]], w_in_shape, which=(0, 1))
    late = [n for n in BIG if n != "w_in"]
    flight = {}

    def near_w_in(after):
        srcs, lands = _exchange_wait("gather_w_in_near_wait", "gather", n_send, n_recv, n_srcs, n_lands, after,
                                     which=(0, 1))
        d_send, d_recv, d_srcs, d_lands, _ = _exchange_start(
            "gather_w_in_diag_start", "gather", srcs, w_in_shape, which=(2,), lands=lands)
        flight["late"] = start_gather("gather_late_start", late)
        w_near = _pair_fill("pair_fill_w_in_near", d_lands[0], d_srcs[0], core, others[:2], chip)
        flight["diag"] = (d_send, d_recv, d_srcs, [w_near])
        return w_near, flight["late"][-1]

    def far_w_in(after):
        d_send, d_recv, d_srcs, d_lands = flight["diag"]
        srcs, lands = _exchange_wait("gather_w_in_diag_wait", "gather", d_send, d_recv, d_srcs, d_lands, after,
                                     which=(2,))
        return _pair_fill("pair_fill_w_in_diag", lands[0], srcs[0], core, others[2:], chip, write_mine=False), None

    def late_weights(after):
        l_send, l_recv, l_srcs, l_lands, _ = flight["late"]
        srcs, lands = _exchange_wait("gather_late_wait", "gather", l_send, l_recv, l_srcs, l_lands, after)
        out = {}
        for n, mine, land in zip(late, srcs, lands):
            out[n] = _to_full(n, _pair_fill("pair_fill_%s" % n, land, mine, core, others, chip))
        return out

    pending = []

    def on_grads(group):
        names = list(group)
        red = [_pair_reduce("pair_reduce_%s" % n, g if g.ndim == 3 else _to_shard_major(n, g), core)
               for n, g in group.items()]
        wires = [wire for _, wire in red]
        send, recv, srcs, lands, token = _exchange_start(
            "scatter_start_%s" % names[0], "scatter", wires, [(wire.shape, wire.dtype) for wire in wires])
        pending.append((names, [own for own, _ in red], send, recv, srcs, lands))
        return token

    small = {n: weights[n].reshape(1, -1) for n in SMALL}

    loss, grad_x, small_g = _local_step(x[0], loss_target[0], n_srcs[0], chip, others, near_w_in, far_w_in, small,
                                        late_weights, on_grads, deps0=[n_token])

    out_g, out_d, out_m, out_v = {}, {}, {}, {}
    for names, owns, send, recv, srcs, lands in pending:
        _, got = _exchange_wait("scatter_wait_%s" % names[0], "scatter", send, recv, srcs, lands, grad_x)
        for n, own, by_chip in zip(names, owns, got):
            shape = weights[n].shape
            g2 = _sum_share("sum_share_%s" % n, own, by_chip, chip, others, core)
            d, nm, nv = _adamw("adamw_%s" % n, weights[n][0], g2, moments_m[n][0], moments_v[n][0])
            out_g[n], out_d[n], out_m[n], out_v[n] = (t.reshape(shape) for t in (g2, d, nm, nv))

    packed = jnp.concatenate([small_g[n].reshape(1, -1) for n in SMALL], axis=1).reshape(-1, LANES)
    loss_tile = jnp.zeros((8, LANES), F32).at[0, 0].set(loss)
    red = _all_reduce_small(jnp.concatenate([packed, loss_tile], axis=0))
    loss = red[packed.shape[0], 0]
    red = red[:packed.shape[0]].reshape(1, -1)
    off = 0
    for n in SMALL:
        shape = weights[n].shape
        row = (1, weights[n].size)
        g2 = red[:, off:off + row[1]]
        off += row[1]
        d, nm, nv = _adamw("adamw_%s" % n, weights[n].reshape(row), g2, moments_m[n].reshape(row),
                           moments_v[n].reshape(row))
        out_g[n], out_d[n], out_m[n], out_v[n] = (t.reshape(shape) for t in (g2, d, nm, nv))

    return (loss, grad_x[None], *[out_g[n] for n in ALL_W], *[out_d[n] for n in ALL_W],
            *[out_m[n] for n in ALL_W], *[out_v[n] for n in ALL_W])
```

```python
import functools

import jax
import jax.numpy as jnp
from jax import lax
from jax.experimental import pallas as pl
from jax.experimental.pallas import tpu as pltpu

CDT = jnp.bfloat16
F32 = jnp.float32
EPS = 1e-6

ATT_GROUPS = ((128, 1), (512, 4), (2048, 16))
ATT_HPG = 4
ATT_HEADS = 12
HD = 128
BLK = 128
ATT_W = ATT_HEADS * HD
GW = ATT_HPG * HD
RET_HEADS = 4

ADAM_LR = 0.001
ADAM_B1 = 0.9
ADAM_B2 = 0.999
ADAM_EPS = 1e-08
ADAM_WD = 0.01
ADAM_STEP = 10

VMEM_LIMIT_BYTES = 56 * 1024 * 1024
MESH = pl.DeviceIdType.MESH
HBM_SPEC = pl.BlockSpec(memory_space=pltpu.HBM)
VMEM_SPEC = pl.BlockSpec(memory_space=pltpu.VMEM)


def _params(n_axes, collective_id=None):
    return pltpu.CompilerParams(dimension_semantics=("arbitrary",) * n_axes,
                                vmem_limit_bytes=VMEM_LIMIT_BYTES, collective_id=collective_id)


PAIR_FILL_ID, PAIR_REDUCE_ID, SUM_SHARE_ID = 1, 2, 3


def _sibling_barrier(first_step):
    @pl.when(first_step)
    def _():
        sem = pltpu.get_barrier_semaphore()
        x, y, c = lax.axis_index("x"), lax.axis_index("y"), lax.axis_index("c")
        pl.semaphore_signal(sem, inc=1, device_id=(x, y, 1 - c), device_id_type=pl.DeviceIdType.MESH)
        pl.semaphore_wait(sem, 1)


def _dot_nn(a, b):
    return jnp.dot(a, b, preferred_element_type=F32)


def _dot_nt(a, b):
    return lax.dot_general(a, b, (((1,), (1,)), ((), ())), preferred_element_type=F32)


def _dot_tn(a, b):
    return lax.dot_general(a, b, (((0,), (0,)), ((), ())), preferred_element_type=F32)


def _sigmoid(v):
    return 1.0 / (1.0 + jnp.exp(-v))


def _matmul(name, mode, a, b, *, tm, tn, tk, extras=(), outs, epilogue, deps=(), b_spec=None, n_cols=None,
            prefetch=(), alias_dep_to_out=None, j_outer=False):
    deps = [d for d in deps if d is not None]
    if mode == "tn":
        K, M = a.shape
    else:
        M, K = a.shape
    if b_spec is None:
        (N, K2) = b.shape if mode == "nt" else b.shape[::-1]
        assert K == K2, (name, a.shape, b.shape)
        if mode == "nt":
            b_spec = pl.BlockSpec((tn, tk), lambda i, j, k, *p: (j, k))
        else:
            b_spec = pl.BlockSpec((tk, tn), lambda i, j, k, *p: (k, j))
    else:
        N = n_cols
    assert M % tm == 0 and N % tn == 0 and K % tk == 0, (name, a.shape, b.shape)
    ni, nj, nk = M // tm, N // tn, K // tk
    if mode == "tn":
        a_spec = pl.BlockSpec((tk, tm), lambda i, j, k, *p: (k, i))
    else:
        a_spec = pl.BlockSpec((tm, tk), lambda i, j, k, *p: (i, k))
    dot = {"nn": _dot_nn, "nt": _dot_nt, "tn": _dot_tn}[mode]
    n_ex, n_out, n_dep, n_pre = len(extras), len(outs), len(deps), len(prefetch)
    grid = (ni, nj, nk)
    if j_outer:
        grid = (nj, ni, nk)

        def swapped(spec):
            return pl.BlockSpec(spec.block_shape, lambda j, i, k, *p: spec.index_map(i, j, k, *p))

        a_spec, b_spec = swapped(a_spec), swapped(b_spec)
        extras = [(e, swapped(s)) for e, s in extras]
        outs = [(o, swapped(s)) for o, s in outs]

    def body(*refs):
        refs = refs[n_pre:]
        a_ref, b_ref = refs[0], refs[1]
        ex = refs[2:2 + n_ex]
        out = refs[2 + n_ex + n_dep:2 + n_ex + n_dep + n_out]
        acc = refs[-1] if nk > 1 else None
        i = pl.program_id(1 if j_outer else 0)
        k = pl.program_id(2)
        if nk == 1:
            epilogue(dot(a_ref[...].astype(CDT), b_ref[...].astype(CDT)), ex, out, i)
            return

        @pl.when(k == 0)
        def _():
            acc[...] = jnp.zeros_like(acc)

        acc[...] += dot(a_ref[...].astype(CDT), b_ref[...].astype(CDT))

        @pl.when(k == nk - 1)
        def _():
            epilogue(acc[...], ex, out, i)

    grid_spec = pltpu.PrefetchScalarGridSpec(
        num_scalar_prefetch=n_pre, grid=grid,
        in_specs=[a_spec, b_spec] + [s for _, s in extras] + [pl.BlockSpec(memory_space=pl.ANY)] * n_dep,
        out_specs=[s for _, s in outs],
        scratch_shapes=[pltpu.VMEM((tm, tn), F32)] if nk > 1 else [])
    aliases = {}
    if alias_dep_to_out is not None:
        aliases = {n_pre + 2 + n_ex + alias_dep_to_out[0]: alias_dep_to_out[1]}
    res = pl.pallas_call(
        body, name=name, grid_spec=grid_spec, out_shape=[o for o, _ in outs], input_output_aliases=aliases,
        compiler_params=_params(3),
    )(*prefetch, a, b, *[e for e, _ in extras], *deps)
    return res


def _mn(tm, tn, col_off=0):
    return pl.BlockSpec((tm, tn), lambda i, j, k, *p: (i, j + col_off))


def _row(tn):
    return pl.BlockSpec((1, tn), lambda i, j, k, *p: (0, j))


def _ep_store(acc, ex, out, i):
    out[0][...] = acc.astype(out[0].dtype)


def _ep_resid_norm(acc, ex, out, i):
    x1 = ex[0][...] + acc
    out[0][...] = x1
    rstd = lax.rsqrt(jnp.mean(x1 * x1, axis=-1, keepdims=True) + EPS)
    out[1][...] = (x1 * rstd * ex[1][...]).astype(out[1].dtype)


def _ep_up(acc, ex, out, i):
    out[0][...] = acc.astype(out[0].dtype)
    r = jnp.maximum(acc, 0.0)
    out[1][...] = (r * r).astype(out[1].dtype)


def _ep_down_loss(acc, ex, out, i, inv_d):
    diff = (ex[0][...] + acc) - ex[1][...]
    dx2 = diff * inv_d
    out[0][...] = dx2
    out[1][...] = dx2.astype(out[1].dtype)

    @pl.when(i == 0)
    def _():
        out[2][...] = jnp.zeros_like(out[2])

    out[2][...] += jnp.sum(diff * diff, axis=0, keepdims=True)


def _ep_dh(acc, ex, out, i):
    h = ex[0][...].astype(F32)
    out[0][...] = (acc * (2.0 * jnp.maximum(h, 0.0))).astype(out[0].dtype)


def _ep_rms_bwd(acc, ex, out, i):
    x = ex[0][...]
    g = ex[1][...]
    rstd = lax.rsqrt(jnp.mean(x * x, axis=-1, keepdims=True) + EPS)
    xh = x * rstd
    dxh = acc * g
    dx = ex[2][...] + rstd * (dxh - xh * jnp.mean(dxh * xh, axis=-1, keepdims=True))
    out[0][...] = dx
    for copy in out[1:-1]:
        copy[...] = dx.astype(copy.dtype)
    dg = out[-1]

    @pl.when(i == 0)
    def _():
        dg[...] = jnp.zeros_like(dg)

    dg[...] += jnp.sum(acc * xh, axis=0, keepdims=True)


def _ep_gates(acc, ex, out, i):
    sa = _sigmoid(ex[0][...].astype(F32))
    sb = _sigmoid(ex[1][...].astype(F32))
    dpa = acc * sa
    dpb = acc * sb
    out[0][...] = dpa.astype(out[0].dtype)
    out[1][...] = dpb.astype(out[1].dtype)
    out[2][...] = (dpa * ex[2][...].astype(F32) * (1.0 - sa)).astype(out[2].dtype)
    out[3][...] = (dpb * ex[3][...].astype(F32) * (1.0 - sb)).astype(out[3].dtype)


def _sds(shape, dtype):
    return jax.ShapeDtypeStruct(shape, dtype)


def _in_proj_mine(x, g, w_mine, chip, proj_sds, deps, tm=512):
    S, D = x.shape
    ns = w_mine.shape[1]
    deps = [d for d in deps if d is not None]

    def body(c_ref, x_ref, g_ref, w_ref, *rest):
        xn_ref, proj_ref = rest[len(deps)], rest[len(deps) + 1]
        xv = x_ref[...]
        rstd = lax.rsqrt(jnp.mean(xv * xv, axis=-1, keepdims=True) + EPS)
        xn = (xv * rstd * g_ref[...]).astype(xn_ref.dtype)
        xn_ref[...] = xn
        proj_ref[...] = _dot_nn(xn, w_ref[...]).astype(proj_ref.dtype)

    grid_spec = pltpu.PrefetchScalarGridSpec(
        num_scalar_prefetch=1, grid=(S // tm,),
        in_specs=[pl.BlockSpec((tm, D), lambda i, c: (i, 0)), pl.BlockSpec((1, D), lambda i, c: (0, 0)),
                  pl.BlockSpec((D, ns), lambda i, c: (0, 0))] + [pl.BlockSpec(memory_space=pl.ANY)] * len(deps),
        out_specs=[pl.BlockSpec((tm, D), lambda i, c: (i, 0)), pl.BlockSpec((tm, ns), lambda i, c: (i, c[0]))])
    return pl.pallas_call(body, name="in_proj_mine", grid_spec=grid_spec, out_shape=[_sds((S, D), CDT), proj_sds],
                          compiler_params=_params(1))(chip, x, g, w_mine, *deps)


def _rm_shape(S, d, width):
    return (S, width) if d == 1 else (d, S // d, width)


def _rm_spec(tm, d, width):
    if d == 1:
        return pl.BlockSpec((tm, width), lambda i: (i, 0))
    return pl.BlockSpec((d, tm // d, width), lambda i: (0, i, 0))


def _rm_put(dst_ref, cols, buf_ref, d):
    if d == 1:
        dst_ref[:, cols] = buf_ref[...].astype(dst_ref.dtype)
        return
    m = buf_ref.shape[0] // d
    for r in range(d):
        dst_ref[r, :, cols] = buf_ref[pl.ds(r, m, stride=d), :].astype(dst_ref.dtype)


def _rm_reader(buf_ref, src_ref, d):
    if d == 1:
        return lambda s, rows: src_ref[rows, s * HD:(s + 1) * HD].astype(F32)
    m = buf_ref.shape[1] // d
    for s in range(buf_ref.shape[0]):
        for r in range(d):
            buf_ref.at[s][pl.ds(r, m, stride=d), :] = src_ref[r, :, s * HD:(s + 1) * HD].astype(F32)
    return lambda s, rows: buf_ref.at[s][rows, :]


def _qknorm_fwd(proj, gqk, tm=512):
    S = proj.shape[0]
    W = 2 * ATT_W
    dil = [d for _, d in ATT_GROUPS]

    def body(p_ref, g_ref, o0, o1, o2, buf):
        outs = (o0, o1, o2)
        for hd in range(3 * ATT_HEADS):
            which, head = hd // ATT_HEADS, hd % ATT_HEADS
            grp, slot = head // ATT_HPG, head % ATT_HPG
            cols = slice(hd * HD, (hd + 1) * HD)

            def chunk(rows, which=which, cols=cols):
                v = p_ref[rows, cols].astype(F32)
                if which < 2:
                    rstd = lax.rsqrt(jnp.mean(v * v, axis=-1, keepdims=True) + EPS)
                    v = v * rstd * g_ref[:, cols]
                buf[rows, :] = v

            chunk(slice(None))
            _rm_put(outs[grp], slice(which * GW + slot * HD, which * GW + (slot + 1) * HD), buf, dil[grp])

    return pl.pallas_call(
        body, name="qknorm_fwd", grid=(S // tm,),
        in_specs=[pl.BlockSpec((tm, 3 * ATT_W), lambda i: (i, 0)), pl.BlockSpec((1, W), lambda i: (0, 0))],
        out_specs=[_rm_spec(tm, d, 3 * GW) for d in dil],
        out_shape=[_sds(_rm_shape(S, d, 3 * GW), CDT) for d in dil],
        scratch_shapes=[pltpu.VMEM((tm, HD), F32)],
        compiler_params=_params(1))(proj, gqk)


def _qknorm_bwd(proj, gqk, dqs, dks, dvs, dproj, tm=256):
    S = proj.shape[0]
    W = 2 * ATT_W
    dil = [d for _, d in ATT_GROUPS]

    def body(p_ref, g_ref, *refs):
        ins = refs[0:9]
        o_ref, dg_ref = refs[10], refs[11]
        bufs = refs[12:21]
        i = pl.program_id(0)

        @pl.when(i == 0)
        def _():
            dg_ref[...] = jnp.zeros_like(dg_ref)

        nat = [_rm_reader(bufs[j], ins[j], dil[j % 3]) for j in range(9)]
        dq_get, dk_get, dv_get = nat[0:3], nat[3:6], nat[6:9]
        for hd in range(2 * ATT_HEADS):
            sl = slice(hd * HD, (hd + 1) * HD)
            head = hd % ATT_HEADS
            grp, slot = head // ATT_HPG, head % ATT_HPG
            get = (dq_get if hd < ATT_HEADS else dk_get)[grp]

            def chunk(rows, sl=sl, slot=slot, get=get):
                dn = get(slot, rows)
                v = p_ref[rows, sl].astype(F32)
                rstd = lax.rsqrt(jnp.mean(v * v, axis=-1, keepdims=True) + EPS)
                vh = v * rstd
                dg_ref[:, sl] += jnp.sum(dn * vh, axis=0, keepdims=True)
                dvh = dn * g_ref[:, sl]
                o_ref[rows, sl] = (rstd * (dvh - vh * jnp.mean(dvh * vh, axis=-1, keepdims=True))).astype(o_ref.dtype)

            chunk(slice(None))
        for head in range(ATT_HEADS):
            grp, slot = head // ATT_HPG, head % ATT_HPG
            o_ref[:, W + head * HD:W + (head + 1) * HD] = dv_get[grp](slot, slice(None)).astype(o_ref.dtype)

    return pl.pallas_call(
        body, name="qknorm_bwd", grid=(S // tm,),
        in_specs=[pl.BlockSpec((tm, W), lambda i: (i, 0)), pl.BlockSpec((1, W), lambda i: (0, 0))]
        + [_rm_spec(tm, d, GW) for d in dil] * 3 + [pl.BlockSpec(memory_space=pl.ANY)],
        out_specs=[pl.BlockSpec((tm, 3 * ATT_W), lambda i: (i, 0)), pl.BlockSpec((1, W), lambda i: (0, 0))],
        out_shape=[_sds(dproj.shape, dproj.dtype), _sds((1, W), F32)],
        scratch_shapes=[pltpu.VMEM((ATT_HPG, tm, HD), F32)] * 9,
        input_output_aliases={11: 0},
        compiler_params=_params(1))(proj, gqk, *dqs, *dks, *dvs, dproj)


def _att_mask(n):
    qi = lax.broadcasted_iota(jnp.int32, (BLK, 2 * BLK), 0)
    kj = lax.broadcasted_iota(jnp.int32, (BLK, 2 * BLK), 1)
    dist = BLK + qi - kj
    valid_all = (dist >= 0) & (dist <= BLK)
    return valid_all & ((kj >= BLK) | (n > 0)), valid_all, dist.astype(F32)


def _att_slopes(grp):
    return [2.0 ** (-8.0 * (grp * ATT_HPG + hh + 1) / ATT_HEADS) for hh in range(ATT_HPG)]


ATT_PLANES_PER_STEP = 4


def _att_planes(d):
    return d if d > 1 else ATT_PLANES_PER_STEP


def _att_3d(a, d):
    return a.reshape(ATT_PLANES_PER_STEP, a.shape[0] // ATT_PLANES_PER_STEP, a.shape[1]) if d == 1 else a


def _att_spec(row_fn, col=0):
    return pl.BlockSpec((ATT_PLANES_PER_STEP, BLK, GW), lambda r, n: (r, row_fn(n), col))


def _att_chains():
    return [(rr * ATT_HPG + hh, rr, slice(hh * HD, (hh + 1) * HD), hh)
            for rr in range(ATT_PLANES_PER_STEP) for hh in range(ATT_HPG)]


def _att_qkv_specs(nb):
    last = nb - 1

    def cur(n):
        return jnp.minimum(n, last)

    def prev(n):
        return jnp.maximum(jnp.minimum(n, last) - 1, 0)

    return [_att_spec(cur, 0), _att_spec(prev, 1), _att_spec(cur, 1), _att_spec(prev, 2), _att_spec(cur, 2),
            _att_spec(lambda n: last, 1), _att_spec(lambda n: last, 2)]


def _att_prev(seg, n, prev_ref, last_ref, rr, sl):
    t = prev_ref[rr, :, sl]
    if seg and rr > 0:
        t = jnp.where(n == 0, last_ref[rr - 1, :, sl], t)
    return t


def _att_fwd(grp, S, qkv):
    _, d = ATT_GROUPS[grp]
    seg = d == 1
    P = _att_planes(d)
    L = S // P
    nb = L // BLK
    assert P % ATT_PLANES_PER_STEP == 0 and (not seg or P == ATT_PLANES_PER_STEP)
    slopes = _att_slopes(grp)
    scale = HD ** -0.5
    chains = _att_chains()

    def body(q_ref, kp_ref, kc_ref, vp_ref, vc_ref, kl_ref, vl_ref, o_ref, l_ref, s_buf, p_buf, den_buf):
        n = pl.program_id(1)
        valid, valid_all, distf = _att_mask(n)
        for c, rr, sl, hh in chains:
            k = jnp.concatenate([_att_prev(seg, n, kp_ref, kl_ref, rr, sl), kc_ref[rr, :, sl]], axis=0)
            s_buf[c] = _dot_nt(q_ref[rr, :, sl], k)
        for c, rr, sl, hh in chains:
            s = s_buf[c] * scale + (-slopes[hh] * d) * distf
            s = jnp.where(valid_all if seg and rr > 0 else valid, s, -1e30)
            m = jnp.max(s, axis=-1, keepdims=True)
            p = jnp.exp(s - m)
            den = jnp.sum(p, axis=-1, keepdims=True)
            p_buf[c] = p.astype(CDT)
            den_buf[c] = jnp.broadcast_to(den, (BLK, HD))
            l_ref[rr, :, sl] = jnp.broadcast_to(m + jnp.log(den), (BLK, HD))
        for c, rr, sl, hh in chains:
            v = jnp.concatenate([_att_prev(seg, n, vp_ref, vl_ref, rr, sl), vc_ref[rr, :, sl]], axis=0)
            o_ref[rr, :, sl] = _dot_nn(p_buf[c], v) / den_buf[c]

    out_spec = _att_spec(lambda n: n)
    n_ch = len(chains)
    q3 = _att_3d(qkv, d)
    o, l = pl.pallas_call(
        body, name="att_fwd_g%d" % grp, grid=(P // ATT_PLANES_PER_STEP, nb),
        in_specs=_att_qkv_specs(nb),
        out_specs=[out_spec, out_spec],
        out_shape=[_sds((P, L, GW), F32)] * 2,
        scratch_shapes=[pltpu.VMEM((n_ch, BLK, 2 * BLK), F32), pltpu.VMEM((n_ch, BLK, 2 * BLK), CDT),
                        pltpu.VMEM((n_ch, BLK, HD), F32)],
        compiler_params=_params(2),
    )(*[q3] * 7)
    return o.reshape(_rm_shape(S, d, GW)), l.reshape(_rm_shape(S, d, GW))


def _att_bwd(grp, S, qkv, lse, do_g, c_g):
    _, d = ATT_GROUPS[grp]
    seg = d == 1
    P = _att_planes(d)
    L = S // P
    nb = L // BLK
    assert P % ATT_PLANES_PER_STEP == 0 and (not seg or P == ATT_PLANES_PER_STEP)
    slopes = _att_slopes(grp)
    scale = HD ** -0.5
    last = nb - 1
    chains = _att_chains()

    def body(q_ref, kp_ref, kc_ref, vp_ref, vc_ref, kl_ref, vl_ref, l_ref, do_ref, c_ref, dq_ref, dk_ref, dv_ref,
             ck, cv, fk, fv, s_buf, dp_buf, p_buf, ds_buf):
        n = pl.program_id(1)

        @pl.when(n == 0)
        def _():
            for buf in (ck, cv, fk, fv):
                buf[...] = jnp.zeros_like(buf)

        @pl.when(n < nb)
        def _():
            valid, valid_all, distf = _att_mask(n)
            for c, rr, sl, hh in chains:
                k = jnp.concatenate([_att_prev(seg, n, kp_ref, kl_ref, rr, sl), kc_ref[rr, :, sl]], axis=0)
                v = jnp.concatenate([_att_prev(seg, n, vp_ref, vl_ref, rr, sl), vc_ref[rr, :, sl]], axis=0)
                s_buf[c] = _dot_nt(q_ref[rr, :, sl], k)
                dp_buf[c] = _dot_nt(do_ref[rr, :, sl], v)
            for c, rr, sl, hh in chains:
                s = s_buf[c] * scale + (-slopes[hh] * d) * distf
                p = jnp.where(valid_all if seg and rr > 0 else valid, jnp.exp(s - l_ref[rr, :, sl][:, 0:1]), 0.0)
                p_buf[c] = p.astype(CDT)
                ds_buf[c] = (p * (dp_buf[c] + c_ref[rr, :, sl][:, 0:1]) * scale).astype(CDT)
            for c, rr, sl, hh in chains:
                k = jnp.concatenate([_att_prev(seg, n, kp_ref, kl_ref, rr, sl), kc_ref[rr, :, sl]], axis=0)
                ds = ds_buf[c]
                dq_ref[rr, :, sl] = _dot_nn(ds, k)
                dk = _dot_tn(ds, q_ref[rr, :, sl])
                dv = _dot_tn(p_buf[c], do_ref[rr, :, sl])
                dk_ref[rr, :, sl] = ck[rr, :, sl] + dk[0:BLK]
                dv_ref[rr, :, sl] = cv[rr, :, sl] + dv[0:BLK]
                ck[rr, :, sl] = dk[BLK:2 * BLK]
                cv[rr, :, sl] = dv[BLK:2 * BLK]
                if seg and rr > 0:
                    @pl.when(n == 0)
                    def _(rr=rr, sl=sl, dk=dk, dv=dv):
                        fk[rr - 1, :, sl] = dk[0:BLK]
                        fv[rr - 1, :, sl] = dv[0:BLK]

        @pl.when(n == nb)
        def _():
            dk_ref[...] = ck[...] + fk[...]
            dv_ref[...] = cv[...] + fv[...]

    blk = (ATT_PLANES_PER_STEP, BLK, GW)
    at_q = _att_spec(lambda n: jnp.minimum(n, last))
    behind = _att_spec(lambda n: jnp.maximum(n - 1, 0))
    n_ch = len(chains)
    q3 = _att_3d(qkv, d)
    res = pl.pallas_call(
        body, name="att_bwd_g%d" % grp, grid=(P // ATT_PLANES_PER_STEP, nb + 1),
        in_specs=_att_qkv_specs(nb) + [at_q, at_q, at_q],
        out_specs=[at_q, behind, behind],
        out_shape=[_sds((P, L, GW), F32)] * 3,
        scratch_shapes=[pltpu.VMEM(blk, F32)] * 4
        + [pltpu.VMEM((n_ch, BLK, 2 * BLK), F32), pltpu.VMEM((n_ch, BLK, 2 * BLK), F32),
           pltpu.VMEM((n_ch, BLK, 2 * BLK), CDT), pltpu.VMEM((n_ch, BLK, 2 * BLK), CDT)],
        compiler_params=_params(2),
    )(*[q3] * 7, _att_3d(lse, d), _att_3d(do_g, d), _att_3d(c_g, d))
    return [t.reshape(_rm_shape(S, d, GW)) for t in res]


def _mix_alpha(l0, l1, l2):
    mx = jnp.maximum(jnp.maximum(l0, l1), l2)
    e = [jnp.exp(l0 - mx), jnp.exp(l1 - mx), jnp.exp(l2 - mx)]
    tot = e[0] + e[1] + e[2]
    return [ei / tot for ei in e]


def _mix_fwd(S, os_, ls_, tm=512):
    dil = [d for _, d in ATT_GROUPS]

    def body(*refs):
        out, bufs = refs[6], refs[7:13]
        get = [_rm_reader(bufs[j], refs[j], dil[j % 3]) for j in range(6)]
        rows = slice(None)
        for s in range(ATT_HPG):
            al = _mix_alpha(*[get[3 + g](s, rows) for g in range(3)])
            mixed = al[0] * get[0](s, rows) + al[1] * get[1](s, rows) + al[2] * get[2](s, rows)
            out[:, s * HD:(s + 1) * HD] = mixed.astype(out.dtype)

    specs = [_rm_spec(tm, d, GW) for d in dil]
    return pl.pallas_call(
        body, name="mix_fwd", grid=(S // tm,), in_specs=specs * 2, out_specs=pl.BlockSpec((tm, GW), lambda i: (i, 0)),
        out_shape=_sds((S, GW), CDT), scratch_shapes=[pltpu.VMEM((ATT_HPG, tm, HD), F32)] * 6,
        compiler_params=_params(1))(*os_, *ls_)


def _mix_bwd(S, os_, ls_, do_a, tm=512):
    dil = [d for _, d in ATT_GROUPS]

    def body(*refs):
        d_ref, outs, bufs, tmps = refs[6], refs[7:13], refs[13:19], refs[19:25]
        get = [_rm_reader(bufs[j], refs[j], dil[j % 3]) for j in range(6)]
        for s in range(ATT_HPG):
            cols = slice(s * HD, (s + 1) * HD)

            def chunk(rows, s=s, cols=cols):
                al = _mix_alpha(*[get[3 + g](s, rows) for g in range(3)])
                dv = d_ref[rows, cols]
                o_a = al[0] * get[0](s, rows) + al[1] * get[1](s, rows) + al[2] * get[2](s, rows)
                dsum = jnp.sum(dv * o_a, axis=-1, keepdims=True)
                for g in range(3):
                    tmps[g][rows, :] = al[g] * dv
                    tmps[3 + g][rows, :] = -(al[g] * dsum)

            chunk(slice(None))
            for j in range(6):
                _rm_put(outs[j], cols, tmps[j], dil[j % 3])

    specs = [_rm_spec(tm, d, GW) for d in dil]
    res = pl.pallas_call(
        body, name="mix_bwd", grid=(S // tm,), in_specs=specs * 2 + [pl.BlockSpec((tm, GW), lambda i: (i, 0))],
        out_specs=specs * 2,
        out_shape=[_sds(_rm_shape(S, d, GW), CDT) for d in dil] + [_sds(_rm_shape(S, d, GW), F32) for d in dil],
        scratch_shapes=[pltpu.VMEM((ATT_HPG, tm, HD), F32)] * 6 + [pltpu.VMEM((tm, HD), F32)] * 6,
        compiler_params=_params(1))(*os_, *ls_, do_a)
    return res[:3], res[3:]


def _ret_tables(dk):
    H, C = RET_HEADS, BLK
    log_g = jnp.log(1.0 - 2.0 ** (-5.0 - jnp.arange(H, dtype=F32)))
    idx = jnp.arange(C, dtype=F32)
    diff = idx[:, None] - idx[None, :]
    decay = jnp.where(diff >= 0, jnp.exp(log_g[:, None, None] * jnp.maximum(diff, 0.0)), 0.0)
    xi = jnp.exp(log_g[:, None] * (idx[None, :] + 1.0))
    zeta = jnp.exp(log_g[:, None] * (C - 1.0 - idx[None, :])) * (dk ** -0.5)
    g_chunk = jnp.exp(log_g * C)
    bc = lambda t: jnp.broadcast_to(t[:, :, None], (H, C, C))
    return decay, bc(xi), bc(zeta), jnp.broadcast_to(g_chunk[:, None, None], (H, 8, C))


def _gn_fwd(o, g, b):
    mu = jnp.mean(o, axis=-1, keepdims=True)
    xc = o - mu
    rstd = lax.rsqrt(jnp.mean(xc * xc, axis=-1, keepdims=True) + EPS)
    yh = xc * rstd
    return yh, rstd, yh * g + b


def _ret_specs(dk, dv, order):
    H = RET_HEADS
    qk_w, v_w = H * dk, H * dv
    off_q = 3 * ATT_W
    off_k, off_v, off_g = off_q + qk_w, off_q + 2 * qk_w, off_q + 2 * qk_w + v_w
    assert 2 * dk == dv and all(off % dv == 0 for off in (off_q, off_k, off_v, off_g))

    def col(off, j):
        return pl.BlockSpec((BLK, dv), lambda i: (order(i), off // dv + j))

    tab = pl.BlockSpec((H, BLK, BLK), lambda i: (0, 0, 0))
    return ([col(off_q, j) for j in range(H // 2)] + [col(off_k, j) for j in range(H // 2)]
            + [col(off_v, j) for j in range(H)] + [col(off_g, j) for j in range(H)]
            + [tab, tab, tab, pl.BlockSpec((H, 8, BLK), lambda i: (0, 0, 0))])


def _ret_heads(refs, dk):
    H = RET_HEADS
    q_refs, k_refs = refs[0:H // 2], refs[H // 2:H]
    v_refs, gr_refs = refs[H:2 * H], refs[2 * H:3 * H]

    def head(h):
        cols = slice((h % 2) * dk, (h % 2 + 1) * dk)
        return q_refs[h // 2][:, cols], k_refs[h // 2][:, cols], v_refs[h][...], gr_refs[h][...]

    return head, refs[3 * H:3 * H + 4]


def _ret_fwd(proj, gn_g, gn_b, dk, dv):
    S = proj.shape[0]
    N = S // BLK
    H = RET_HEADS
    kscale = dk ** -0.5
    n_in = 3 * H + 4

    def body(*refs):
        head, (dec_ref, xi_ref, zeta_ref, gc_ref) = _ret_heads(refs, dk)
        g_ref, b_ref, opre_ref, or_ref, st_ref, state, s_buf, cross_buf = refs[n_in:n_in + 8]
        n = pl.program_id(0)

        @pl.when(n == 0)
        def _():
            state[...] = jnp.zeros_like(state)

        for h in range(H):
            q, k, v, _ = head(h)
            s_buf[h] = _dot_nt(q, k)
            st = state[h]
            st_c = st.astype(CDT)
            st_ref[h] = st_c
            cross_buf[h] = _dot_nn(q, st_c)
            kz = (k.astype(F32) * zeta_ref[h][:, 0:1]).astype(CDT)
            state[h] = st * gc_ref[h][0:1, 0:1] + _dot_tn(kz, v)
        for h in range(H):
            vs = slice(h * dv, (h + 1) * dv)
            _, _, v, gr = head(h)
            s = s_buf[h] * kscale * dec_ref[h]
            o = _dot_nn(s.astype(CDT), v) + cross_buf[h] * xi_ref[h][:, 0:1]
            opre_ref[:, vs] = o
            _, _, y = _gn_fwd(o, g_ref[:, vs], b_ref[:, vs])
            gr = gr.astype(F32)
            or_ref[:, vs] = (y * (gr * _sigmoid(gr))).astype(or_ref.dtype)

    v_w = H * dv
    row = pl.BlockSpec((1, v_w), lambda i: (0, 0))
    tile = pl.BlockSpec((BLK, v_w), lambda i: (i, 0))
    return pl.pallas_call(
        body, name="ret_fwd", grid=(N,),
        in_specs=_ret_specs(dk, dv, lambda i: i) + [row, row],
        out_specs=[tile, tile, pl.BlockSpec((None, H, dk, dv), lambda i: (i, 0, 0, 0))],
        out_shape=[_sds((S, v_w), F32), _sds((S, v_w), CDT), _sds((N, H, dk, dv), CDT)],
        scratch_shapes=[pltpu.VMEM((H, dk, dv), F32), pltpu.VMEM((H, BLK, BLK), F32), pltpu.VMEM((H, BLK, dv), F32)],
        compiler_params=_params(1),
    )(*[proj] * (3 * H), *_ret_tables(dk), gn_g, gn_b)


def _ret_bwd(proj, gn_g, gn_b, o_pre, states, d_or, dga, dgb, dk, dv):
    S, in_w = proj.shape
    N = S // BLK
    H = RET_HEADS
    qk_w, v_w = H * dk, H * dv
    kscale = dk ** -0.5
    n_in = 3 * H + 4
    out_w = 2 * qk_w + 2 * v_w
    gate_w = dga.shape[1]
    col0 = 3 * ATT_W
    assert col0 + out_w + 2 * gate_w == in_w
    rev = lambda i: N - 1 - i

    def body(*refs):
        head, (dec_ref, xi_ref, zeta_ref, gc_ref) = _ret_heads(refs, dk)
        (g_ref, b_ref, opre_ref, st_ref, dor_ref, dga_ref, dgb_ref, dproj_ref, dg_ref, db_ref, dstate, stage,
         sem, do_buf, dox_buf, a_buf, g_buf, dq_buf, dk_buf, dv_buf) = refs[n_in:n_in + 20]
        i = pl.program_id(0)
        slot = i % 2
        out_ref = stage.at[slot]

        def out_copy(s, step):
            rows = pl.ds(pl.multiple_of(rev(step) * BLK, BLK), BLK)
            return pltpu.make_async_copy(stage.at[s], dproj_ref.at[rows, pl.ds(col0, in_w - col0)], sem.at[s])

        @pl.when(i >= 2)
        def _():
            out_copy(slot, i - 2).wait()

        @pl.when(i == 0)
        def _():
            dstate[...] = jnp.zeros_like(dstate)
            dg_ref[...] = jnp.zeros_like(dg_ref)
            db_ref[...] = jnp.zeros_like(db_ref)

        out_ref[:, out_w:out_w + gate_w] = dga_ref[...]
        out_ref[:, out_w + gate_w:out_w + 2 * gate_w] = dgb_ref[...]
        for h in range(H):
            vs = slice(h * dv, (h + 1) * dv)
            _, _, _, gr = head(h)
            gr = gr.astype(F32)
            sg = _sigmoid(gr)
            gain = g_ref[:, vs]
            yh, rstd, y = _gn_fwd(opre_ref[:, vs], gain, b_ref[:, vs])
            d_or_v = dor_ref[:, vs]
            dy = d_or_v * (gr * sg)
            out_ref[:, 2 * qk_w + v_w + h * dv:2 * qk_w + v_w + (h + 1) * dv] = (
                d_or_v * y * (sg * (1.0 + gr * (1.0 - sg)))).astype(out_ref.dtype)
            dg_ref[:, vs] += jnp.sum(dy * yh, axis=0, keepdims=True)
            db_ref[:, vs] += jnp.sum(dy, axis=0, keepdims=True)
            dyh = dy * gain
            do = rstd * (dyh - jnp.mean(dyh, axis=-1, keepdims=True)
                         - yh * jnp.mean(dyh * yh, axis=-1, keepdims=True))
            do_buf[h] = do.astype(CDT)
            dox_buf[h] = (do * xi_ref[h][:, 0:1]).astype(CDT)
        for h in range(H):
            q, k, v, _ = head(h)
            dox = dox_buf[h]
            a_buf[h] = _dot_nt(q, k)
            g_buf[h] = _dot_nt(do_buf[h], v)
            dsn = dstate[h]
            dsn_c = dsn.astype(CDT)
            kz = (k.astype(F32) * zeta_ref[h][:, 0:1]).astype(CDT)
            dq_buf[h] = _dot_nt(dox, st_ref[h])
            dk_buf[h] = _dot_nt(v, dsn_c)
            dv_buf[h] = _dot_nn(kz, dsn_c)
            dstate[h] = dsn * gc_ref[h][0:1, 0:1] + _dot_tn(q, dox)
        for h in range(H):
            q, k, _, _ = head(h)
            decay = dec_ref[h]
            a_c = (a_buf[h] * kscale * decay).astype(CDT)
            g_c = (g_buf[h] * decay).astype(CDT)
            dq = _dot_nn(g_c, k) * kscale + dq_buf[h]
            dkk = _dot_tn(g_c, q) * kscale + dk_buf[h] * zeta_ref[h][:, 0:1]
            dvv = _dot_tn(a_c, do_buf[h]) + dv_buf[h]
            out_ref[:, h * dk:(h + 1) * dk] = dq.astype(out_ref.dtype)
            out_ref[:, qk_w + h * dk:qk_w + (h + 1) * dk] = dkk.astype(out_ref.dtype)
            out_ref[:, 2 * qk_w + h * dv:2 * qk_w + (h + 1) * dv] = dvv.astype(out_ref.dtype)

        cp = out_copy(slot, i)
        cp.start()

        @pl.when(i == N - 1)
        def _():
            cp.wait()
            if N >= 2:
                out_copy(1 - slot, i - 1).wait()

    row = pl.BlockSpec((1, v_w), lambda i: (0, 0))
    tile = pl.BlockSpec((BLK, v_w), lambda i: (rev(i), 0))
    gate = pl.BlockSpec((BLK, gate_w), lambda i: (rev(i), 0))
    return pl.pallas_call(
        body, name="ret_bwd", grid=(N,),
        in_specs=_ret_specs(dk, dv, rev) + [row, row, tile,
                 pl.BlockSpec((None, H, dk, dv), lambda i: (rev(i), 0, 0, 0)), tile, gate, gate],
        out_specs=[pl.BlockSpec(memory_space=pl.ANY), row, row],
        out_shape=[_sds((S, in_w), CDT), _sds((1, v_w), F32), _sds((1, v_w), F32)],
        scratch_shapes=[pltpu.VMEM((H, dk, dv), F32), pltpu.VMEM((2, BLK, in_w - col0), CDT),
                        pltpu.SemaphoreType.DMA((2,)),
                        pltpu.VMEM((H, BLK, dv), CDT), pltpu.VMEM((H, BLK, dv), CDT),
                        pltpu.VMEM((H, BLK, BLK), F32), pltpu.VMEM((H, BLK, BLK), F32),
                        pltpu.VMEM((H, BLK, dk), F32), pltpu.VMEM((H, BLK, dk), F32), pltpu.VMEM((H, BLK, dv), F32)],
        compiler_params=_params(1),
    )(*[proj] * (3 * H), *_ret_tables(dk), gn_g, gn_b, o_pre, states, d_or, dga, dgb)


def _merge_fwd(o_a, o_r, wa, wb, proj, d_model, tm=1024, tn=512):
    S, in_w = proj.shape
    off_a, off_b = in_w - 2 * d_model, in_w - d_model
    assert off_a % tn == 0 and off_b % tn == 0

    def body(oa_ref, or_ref, wa_ref, wb_ref, ga_ref, gb_ref, y_ref, pa_ref, pb_ref):
        pa = _dot_nn(oa_ref[...], wa_ref[...])
        pb = _dot_nn(or_ref[...], wb_ref[...])
        y = _sigmoid(ga_ref[...].astype(F32)) * pa + _sigmoid(gb_ref[...].astype(F32)) * pb
        y_ref[...] = y.astype(y_ref.dtype)
        pa_ref[...] = pa.astype(pa_ref.dtype)
        pb_ref[...] = pb.astype(pb_ref.dtype)

    ka, kb = o_a.shape[1], o_r.shape[1]
    out = pl.BlockSpec((tm, tn), lambda i, j: (i, j))
    return pl.pallas_call(
        body, name="merge_fwd", grid=(S // tm, d_model // tn),
        in_specs=[pl.BlockSpec((tm, ka), lambda i, j: (i, 0)), pl.BlockSpec((tm, kb), lambda i, j: (i, 0)),
                  pl.BlockSpec((ka, tn), lambda i, j: (0, j)), pl.BlockSpec((kb, tn), lambda i, j: (0, j)),
                  pl.BlockSpec((tm, tn), lambda i, j: (i, off_a // tn + j)),
                  pl.BlockSpec((tm, tn), lambda i, j: (i, off_b // tn + j))],
        out_specs=[out, out, out], out_shape=[_sds((S, d_model), CDT)] * 3,
        compiler_params=_params(2))(o_a, o_r, wa, wb, proj, proj)


def _d_x(dproj, w_in, x, g, dx1, dep, tm=512, row_groups=2):
    S, D = x.shape
    n_sh, _, ns = w_in.shape
    nt = S // tm // row_groups
    deps = [d for d in (dep,) if d is not None]

    def body(a_ref, b_ref, x_ref, g_ref, r_ref, *rest):
        dx_ref, dg_ref, acc = rest[len(deps):]
        h, k, i = pl.program_id(0), pl.program_id(1), pl.program_id(2)

        @pl.when(k == 0)
        def _():
            acc[i] = jnp.zeros((tm, D), F32)

        acc[i] += _dot_nt(a_ref[...], b_ref[...])

        @pl.when(k == n_sh - 1)
        def _():
            _ep_rms_bwd(acc[i], (x_ref, g_ref, r_ref), (dx_ref, dg_ref), h * nt + i)

    def last_only(h, k, i):
        return (h * nt + jnp.where(k == n_sh - 1, i, 0), 0)

    return pl.pallas_call(
        body, name="d_x", grid=(row_groups, n_sh, nt),
        in_specs=[pl.BlockSpec((tm, ns), lambda h, k, i: (h * nt + i, k)),
                  pl.BlockSpec((None, D, ns), lambda h, k, i: (k, 0, 0)),
                  pl.BlockSpec((tm, D), last_only), pl.BlockSpec((1, D), lambda h, k, i: (0, 0)),
                  pl.BlockSpec((tm, D), last_only)] + [pl.BlockSpec(memory_space=pl.ANY)] * len(deps),
        out_specs=[pl.BlockSpec((tm, D), last_only), pl.BlockSpec((1, D), lambda h, k, i: (0, 0))],
        out_shape=[_sds((S, D), F32), _sds((1, D), F32)],
        scratch_shapes=[pltpu.VMEM((nt, tm, D), F32)],
        compiler_params=_params(3))(dproj, w_in, x, g, dx1, *deps)


def _local_step(x, target, w_in_mine, chip, others, near_w_in, far_w_in, small, late_weights, on_grads, deps0=()):
    S, D = x.shape
    ns_in = w_in_mine.shape[1]
    in_w = N_CHIPS * ns_in
    d_ff = 4 * D
    ret_v_w = 2 * D
    dv = ret_v_w // RET_HEADS
    dk = (in_w - 3 * ATT_W - 2 * ret_v_w - 2 * D) // (2 * RET_HEADS)
    gqk = jnp.concatenate([small["q_norm_g"].reshape(1, ATT_W), small["k_norm_g"].reshape(1, ATT_W)], axis=1)
    g1, g2 = small["norm1_g"], small["norm2_g"]
    gn_g, gn_b = small["ret_gn_g"], small["ret_gn_b"]

    proj_sds = _sds((S, in_w), CDT)
    xn, proj = _in_proj_mine(x, g1, w_in_mine, chip, proj_sds, deps0)
    for stage, (get_w_in, chips) in enumerate(((near_w_in, others[:2]), (far_w_in, others[2:]))):
        w_in, started = get_w_in(proj)
        (proj,) = _matmul(
            "in_proj_far%d" % stage, "nn", xn, w_in, tm=1024, tn=ns_in, tk=D, prefetch=[chips],
            n_cols=chips.shape[0] * ns_in, b_spec=pl.BlockSpec((None, D, ns_in), lambda i, j, k, o: (o[j], 0, 0)),
            outs=[(proj_sds, pl.BlockSpec((1024, ns_in), lambda i, j, k, o: (i, o[j])))], epilogue=_ep_store,
            deps=[proj, started], alias_dep_to_out=(0, 0), j_outer=True)
    qkv = _qknorm_fwd(proj, gqk)
    att = [_att_fwd(g, S, qkv[g]) for g in range(3)]
    os_, ls_ = [a[0] for a in att], [a[1] for a in att]
    o_a = _mix_fwd(S, os_, ls_)
    o_pre, o_r, states = _ret_fwd(proj, gn_g, gn_b, dk, dv)
    w = late_weights(o_r)
    y, pa, pb = _merge_fwd(o_a, o_r, w["w_proj_a"], w["w_proj_b"], proj, D)
    x1, xn2 = _matmul("out_proj", "nn", y, w["w_out"], tm=1024, tn=D, tk=D,
                      extras=[(x, _mn(1024, D)), (g2, _row(D))],
                      outs=[(_sds((S, D), F32), _mn(1024, D)), (_sds((S, D), CDT), _mn(1024, D))],
                      epilogue=_ep_resid_norm)
    hid, act = _matmul("mlp_up", "nn", xn2, w["w_up"], tm=1024, tn=2048, tk=D, j_outer=True,
                       outs=[(_sds((S, d_ff), CDT), _mn(1024, 2048))] * 2, epilogue=_ep_up)
    dx2, dx2c, loss_row = _matmul(
        "mlp_down_loss", "nn", act, w["w_down"], tm=512, tn=D, tk=d_ff,
        extras=[(x1, _mn(512, D)), (target, _mn(512, D))],
        outs=[(_sds((S, D), F32), _mn(512, D)), (_sds((S, D), CDT), _mn(512, D)), (_sds((1, D), F32), _row(D))],
        epilogue=functools.partial(_ep_down_loss, inv_d=1.0 / D))
    loss = 0.5 * jnp.sum(loss_row) / D

    (dh,) = _matmul("d_hidden", "nt", dx2c, w["w_down"], tm=1024, tn=2048, tk=D, j_outer=True,
                    extras=[(hid, _mn(1024, 2048))], outs=[(_sds((S, d_ff), CDT), _mn(1024, 2048))],
                    epilogue=_ep_dh)
    (gw_down,) = _matmul("dw_down", "tn", act, dx2c, tm=1024, tn=D, tk=2048,
                         outs=[(_sds((d_ff, D), F32), _mn(1024, D))], epilogue=_ep_store)
    (gw_up,) = _matmul("dw_up", "tn", xn2, dh, tm=D, tn=1024, tk=2048,
                       outs=[(_sds((D, d_ff), F32), _mn(D, 1024))], epilogue=_ep_store)
    tok = on_grads({"w_down": gw_down, "w_up": gw_up})
    dx1, dx1c, dg2 = _matmul(
        "d_x1", "nt", dh, w["w_up"], tm=512, tn=D, tk=d_ff,
        extras=[(x1, _mn(512, D)), (g2, _row(D)), (dx2, _mn(512, D))],
        outs=[(_sds((S, D), F32), _mn(512, D)), (_sds((S, D), CDT), _mn(512, D)), (_sds((1, D), F32), _row(D))],
        epilogue=_ep_rms_bwd, deps=[tok])

    gt = 512
    assert (in_w - 2 * D) % gt == 0
    off_a, off_b = (in_w - 2 * D) // gt, (in_w - D) // gt
    dpa, dpb, dga, dgb = _matmul(
        "d_gates", "nt", dx1c, w["w_out"], tm=1024, tn=gt, tk=D,
        extras=[(proj, _mn(1024, gt, off_a)), (proj, _mn(1024, gt, off_b)), (pa, _mn(1024, gt)),
                (pb, _mn(1024, gt))],
        outs=[(_sds((S, D), CDT), _mn(1024, gt))] * 4, epilogue=_ep_gates)
    (gw_out,) = _matmul("dw_out", "tn", y, dx1c, tm=D, tn=D, tk=1024,
                        outs=[(_sds((D, D), F32), _mn(D, D))], epilogue=_ep_store)
    (gw_pa,) = _matmul("dw_proj_a", "tn", o_a, dpa, tm=GW, tn=D, tk=1024,
                       outs=[(_sds((GW, D), F32), _mn(GW, D))], epilogue=_ep_store)
    (gw_pb,) = _matmul("dw_proj_b", "tn", o_r, dpb, tm=1024, tn=D, tk=2048,
                       outs=[(_sds((ret_v_w, D), F32), _mn(1024, D))], epilogue=_ep_store)
    (do_a,) = _matmul("d_o_a", "nt", dpa, w["w_proj_a"], tm=1024, tn=GW, tk=D,
                      outs=[(_sds((S, GW), F32), _mn(1024, GW))], epilogue=_ep_store)
    tok = on_grads({"w_out": gw_out, "w_proj_a": gw_pa, "w_proj_b": gw_pb})
    (d_or,) = _matmul("d_o_r", "nt", dpb, w["w_proj_b"], tm=1024, tn=ret_v_w, tk=D,
                      outs=[(_sds((S, ret_v_w), F32), _mn(1024, ret_v_w))], epilogue=_ep_store, deps=[tok])

    dproj, dgn_g, dgn_b = _ret_bwd(proj, gn_g, gn_b, o_pre, states, d_or, dga, dgb, dk, dv)
    do_gs, c_gs = _mix_bwd(S, os_, ls_, do_a)
    datt_parts = [_att_bwd(g, S, qkv[g], ls_[g], do_gs[g], c_gs[g]) for g in range(3)]
    dproj, dgqk = _qknorm_bwd(proj, gqk, [p[0] for p in datt_parts], [p[1] for p in datt_parts],
                              [p[2] for p in datt_parts], dproj)

    (gw_in,) = _matmul(
        "dw_in", "tn", xn, dproj, tm=512, tn=ns_in, tk=1024,
        outs=[(_sds((N_CHIPS, D, ns_in), F32), pl.BlockSpec((None, 512, ns_in), lambda i, j, k: (j, i, 0)))],
        epilogue=_ep_store)
    tok = on_grads({"w_in": gw_in})
    grad_x, dg1 = _d_x(dproj, w_in, x, g1, dx1, tok)

    smallg = {"norm1_g": dg1, "q_norm_g": dgqk[:, :ATT_W], "k_norm_g": dgqk[:, ATT_W:],
              "ret_gn_g": dgn_g, "ret_gn_b": dgn_b, "norm2_g": dg2}
    return loss, grad_x, smallg


N_CHIPS = 4
N_DEV = 8


def _place():
    x, y, c = lax.axis_index("x"), lax.axis_index("y"), lax.axis_index("c")
    return x, y, c


def _other_chips(x, y):
    out = []
    for fx, fy in ((1, 0), (0, 1), (1, 1)):
        px = 1 - x if fx else x
        py = 1 - y if fy else y
        out.append(((px, py), 2 * px + py))
    return out


SEM_SPEC = pl.BlockSpec(memory_space=pltpu.SEMAPHORE)
ANY_SPEC = pl.BlockSpec(memory_space=pl.ANY)
EFFECT = pltpu.SideEffectType.DATAFLOW_SIDE_EFFECTING


def _ici_copies(kind, srcs, lands, send, recv, which=(0, 1, 2)):
    x, y, c = _place()
    me = 2 * x + y
    out = []
    for w, (s, l) in enumerate(zip(srcs, lands)):
        for j, ((px, py), pidx) in enumerate(_other_chips(x, y)):
            if j not in which:
                continue
            if kind == "gather":
                half = s.shape[0] // 2
                rows = pl.ds(c * half, half)
                src, dst_there, dst_here = s.at[rows, :], l.at[me, rows, :], l.at[pidx, rows, :]
            else:
                src, dst_there, dst_here = s.at[pidx], l.at[me], l.at[pidx]
            out.append((src, dst_there, dst_here, send.at[3 * w + j], recv.at[3 * w + j], (px, py, c)))
    return out


def _exchange_start(name, kind, srcs, land_shapes, which=(0, 1, 2), lands=None):
    n = len(srcs)
    if lands is None:
        lands = [lax.empty(shape, dtype) for shape, dtype in land_shapes]

    def body(*refs):
        src_refs, land_refs = refs[:n], refs[n:2 * n]
        send, recv = refs[2 * n], refs[2 * n + 1]
        token = refs[-1]
        for src, dst, _, ss, rs, dev in _ici_copies(kind, src_refs, land_refs, send, recv, which):
            pltpu.make_async_remote_copy(src_ref=src, dst_ref=dst, send_sem=ss, recv_sem=rs, device_id=dev,
                                         device_id_type=MESH).start()
        token[...] = jnp.zeros_like(token)

    thru = [pltpu.HBM(s.shape, s.dtype) for s in srcs] + [pltpu.HBM(shape, dtype) for shape, dtype in land_shapes]
    res = pl.pallas_call(
        body, name=name,
        out_shape=(pltpu.SemaphoreType.DMA((3 * n,)), pltpu.SemaphoreType.DMA((3 * n,)), *thru, _sds((8, LANES), F32)),
        in_specs=[HBM_SPEC] * (2 * n), out_specs=(SEM_SPEC, SEM_SPEC, *[HBM_SPEC] * (2 * n), VMEM_SPEC),
        input_output_aliases={i: 2 + i for i in range(2 * n)},
        compiler_params=pltpu.CompilerParams(has_side_effects=EFFECT),
    )(*[pltpu.with_memory_space_constraint(s, pltpu.HBM) for s in srcs],
      *[pltpu.with_memory_space_constraint(l, pltpu.HBM) for l in lands])
    return res[0], res[1], list(res[2:2 + n]), list(res[2 + n:2 + 2 * n]), res[-1]


def _exchange_wait(name, kind, send, recv, srcs, lands, after, which=(0, 1, 2)):
    n = len(srcs)

    def body(*refs):
        src_refs, land_refs = refs[:n], refs[n:2 * n]
        send_ref, recv_ref = refs[2 * n], refs[2 * n + 1]
        for src, _, dst, ss, rs, dev in _ici_copies(kind, src_refs, land_refs, send_ref, recv_ref, which):
            cp = pltpu.make_async_remote_copy(src_ref=src, dst_ref=dst, send_sem=ss, recv_sem=rs, device_id=dev,
                                              device_id_type=MESH)
            cp.wait_send()
            cp.wait_recv()

    thru = [pltpu.HBM(t.shape, t.dtype) for t in list(srcs) + list(lands)]
    res = pl.pallas_call(
        body, name=name, out_shape=thru,
        in_specs=[HBM_SPEC] * (2 * n) + [SEM_SPEC, SEM_SPEC, ANY_SPEC], out_specs=[HBM_SPEC] * (2 * n),
        input_output_aliases={i: i for i in range(2 * n)},
        compiler_params=pltpu.CompilerParams(has_side_effects=EFFECT),
    )(*srcs, *lands, send, recv, after)
    return list(res[:n]), list(res[n:])


PAIR_TILE_ELEMS = 1 << 20


def _pair_fill(name, gathered, mine, core, others, chip, write_mine=True):
    k, r, C = gathered.shape
    half = r // 2
    tr = _row_tile(half, C, PAIR_TILE_ELEMS, mult=16)
    nt = half // tr
    n_far = others.shape[0]

    def body(c_ref, o_ref, chip_ref, in_ref, mine_ref, out_ref, slot, send, recv):
        j = pl.program_id(0)
        _sibling_barrier((j == 0) & (pl.program_id(1) == 0))
        b = (j * nt + pl.program_id(1)) % 2
        x, y, c = _place()
        cp = pltpu.make_async_remote_copy(src_ref=in_ref, dst_ref=slot.at[b], send_sem=send.at[b],
                                          recv_sem=recv.at[b], device_id=(x, y, 1 - c), device_id_type=MESH)

        @pl.when(j < n_far)
        def _():
            cp.start()
            cp.wait_recv()
            out_ref[...] = slot[b]
            cp.wait_send()

        @pl.when(j >= n_far)
        def _():
            out_ref[...] = mine_ref[...]

    def far(j):
        return jnp.minimum(j, n_far - 1)

    grid_spec = pltpu.PrefetchScalarGridSpec(
        num_scalar_prefetch=3, grid=(n_far + (2 if write_mine else 0), nt),
        in_specs=[pl.BlockSpec((tr, C), lambda j, i, c, o, m: (
                      (2 * o[far(j)] + c[0]) * nt + jnp.where(j < n_far, i, nt - 1), 0)),
                  pl.BlockSpec((tr, C), lambda j, i, c, o, m: (jnp.where(j < n_far, 0, (j - n_far) * nt + i), 0))],
        out_specs=pl.BlockSpec((tr, C), lambda j, i, c, o, m: (
            jnp.where(j < n_far, 2 * o[far(j)] + 1 - c[0], 2 * m[0] + j - n_far) * nt + i, 0)),
        scratch_shapes=[pltpu.VMEM((2, tr, C), gathered.dtype), pltpu.SemaphoreType.DMA((2,)),
                        pltpu.SemaphoreType.DMA((2,))])
    out = pl.pallas_call(body, name=name, grid_spec=grid_spec, out_shape=_sds((k * r, C), gathered.dtype),
                         input_output_aliases={3: 0}, compiler_params=_params(2, PAIR_FILL_ID))(
                             core, others, chip, gathered.reshape(k * r, C), mine)
    return out.reshape(k, r, C)


def _pair_reduce(name, g, core):
    k, R, C = g.shape
    half = R // 2
    tr = _row_tile(half, C, PAIR_TILE_ELEMS, mult=16)
    nt = half // tr

    def body(c_ref, mine_ref, give_ref, out_ref, wire_ref, stage, slot, send, recv):
        _sibling_barrier((pl.program_id(0) == 0) & (pl.program_id(1) == 0))
        b = (pl.program_id(0) * nt + pl.program_id(1)) % 2
        x, y, c = _place()
        stage[b] = give_ref[...].astype(stage.dtype)
        cp = pltpu.make_async_remote_copy(src_ref=stage.at[b], dst_ref=slot.at[b], send_sem=send.at[b],
                                          recv_sem=recv.at[b], device_id=(x, y, 1 - c), device_id_type=MESH)
        cp.start()
        cp.wait_recv()
        tot = mine_ref[...] + slot[b].astype(F32)
        out_ref[...] = tot
        wire_ref[...] = tot.astype(wire_ref.dtype)
        cp.wait_send()

    blk = (tr, C)
    out_spec = pl.BlockSpec(blk, lambda s, i, c: (s * nt + i, 0))
    grid_spec = pltpu.PrefetchScalarGridSpec(
        num_scalar_prefetch=1, grid=(k, nt),
        in_specs=[pl.BlockSpec(blk, lambda s, i, c: ((2 * s + c[0]) * nt + i, 0)),
                  pl.BlockSpec(blk, lambda s, i, c: ((2 * s + 1 - c[0]) * nt + i, 0))],
        out_specs=[out_spec, out_spec],
        scratch_shapes=[pltpu.VMEM((2, tr, C), CDT), pltpu.VMEM((2, tr, C), CDT), pltpu.SemaphoreType.DMA((2,)),
                        pltpu.SemaphoreType.DMA((2,))])
    g2 = g.reshape(k * R, C)
    out, wire = pl.pallas_call(body, name=name, grid_spec=grid_spec,
                               out_shape=[_sds((k * half, C), F32), _sds((k * half, C), CDT)],
                               compiler_params=_params(2, PAIR_REDUCE_ID))(core, g2, g2)
    return out, wire.reshape(k, half, C)


def _all_reduce_small(v):
    r, cdim = v.shape

    def body(v_ref, o_ref, buf, send, recv):
        x, y, c = _place()
        me = 4 * x + 2 * y + c
        buf[me] = v_ref[...]
        sends = []
        for m in range(1, N_DEV):
            px = 1 - x if m & 4 else x
            py = 1 - y if m & 2 else y
            pc = 1 - c if m & 1 else c
            cp = pltpu.make_async_remote_copy(src_ref=v_ref, dst_ref=buf.at[me], send_sem=send.at[m - 1],
                                              recv_sem=recv.at[m - 1], device_id=(px, py, pc), device_id_type=MESH)
            cp.start()
            sends.append((cp, 4 * px + 2 * py + pc))
        for m, (cp, pidx) in enumerate(sends):
            pltpu.make_async_remote_copy(src_ref=v_ref, dst_ref=buf.at[pidx], send_sem=send.at[m], recv_sem=recv.at[m],
                                         device_id=(x, y, c), device_id_type=MESH).wait_recv()
        for cp, _ in sends:
            cp.wait_send()
        tot = buf[0]
        for k in range(1, N_DEV):
            tot = tot + buf[k]
        o_ref[...] = tot

    return pl.pallas_call(
        body, name="all_reduce_small", in_specs=[VMEM_SPEC], out_specs=VMEM_SPEC,
        out_shape=_sds((r, cdim), F32),
        scratch_shapes=[pltpu.VMEM((N_DEV, r, cdim), F32), pltpu.SemaphoreType.DMA((N_DEV - 1,)),
                        pltpu.SemaphoreType.DMA((N_DEV - 1,))],
    )(v)


def _row_tile(rows, cols, budget_elems=1 << 18, mult=8):
    if rows % mult:
        return rows
    t = max(mult, (budget_elems // cols) // mult * mult)
    while rows % t:
        t -= mult
    return t


def _adamw_update(w, g, m, v):
    nm = ADAM_B1 * m + (1.0 - ADAM_B1) * g
    nv = ADAM_B2 * v + (1.0 - ADAM_B2) * (g * g)
    m_hat = nm / (1.0 - ADAM_B1 ** ADAM_STEP)
    v_hat = nv / (1.0 - ADAM_B2 ** ADAM_STEP)
    return -ADAM_LR * (m_hat / (jnp.sqrt(v_hat) + ADAM_EPS) + ADAM_WD * w), nm, nv


def _adamw(name, w, g, m, v):
    R, C = w.shape
    tr = _row_tile(R, C, 1 << 19)

    def body(w_ref, g_ref, m_ref, v_ref, d_ref, nm_ref, nv_ref):
        d_ref[...], nm_ref[...], nv_ref[...] = _adamw_update(w_ref[...], g_ref[...], m_ref[...], v_ref[...])

    spec = pl.BlockSpec((tr, C), lambda i: (i, 0))
    return pl.pallas_call(body, name=name, grid=(R // tr,), in_specs=[spec] * 4, out_specs=[spec] * 3,
                          out_shape=[_sds((R, C), F32)] * 3, compiler_params=_params(1))(w, g, m, v)


def _sum_share(name, own, by_chip, chip, others, core):
    k, half, C = by_chip.shape
    tr = _row_tile(half, C, PAIR_TILE_ELEMS // 2, mult=16)
    nt = half // tr

    def body(chip_ref, oth_ref, c_ref, own_ref, a_ref, b_ref, cc_ref, g_out, mine, slot, send, recv):
        p = pl.program_id(1)
        _sibling_barrier((pl.program_id(0) == 0) & (p == 0))
        b = pl.program_id(0) % 2
        x, y, c = _place()
        cp = pltpu.make_async_remote_copy(src_ref=mine.at[b], dst_ref=slot.at[b], send_sem=send.at[b],
                                          recv_sem=recv.at[b], device_id=(x, y, 1 - c), device_id_type=MESH)

        @pl.when(p == 0)
        def _():
            tot = ((own_ref[...] + a_ref[...].astype(F32)) + b_ref[...].astype(F32)) + cc_ref[...].astype(F32)
            mine[b] = tot
            cp.start()
            g_out[...] = tot

        @pl.when(p == 1)
        def _():
            cp.wait_recv()
            g_out[...] = slot[b]
            cp.wait_send()

    def piece(j):
        return pl.BlockSpec((tr, C), lambda i, p, chip, oth, c: (oth[j] * nt + i, 0))

    grid_spec = pltpu.PrefetchScalarGridSpec(
        num_scalar_prefetch=3, grid=(nt, 2),
        in_specs=[pl.BlockSpec((tr, C), lambda i, p, chip, oth, c: (chip[0] * nt + i, 0)),
                  piece(0), piece(1), piece(2)],
        out_specs=pl.BlockSpec((tr, C), lambda i, p, chip, oth, c: (
            jnp.where(p == 0, c[0], 1 - c[0]) * nt + i, 0)),
        scratch_shapes=[pltpu.VMEM((2, tr, C), F32), pltpu.VMEM((2, tr, C), F32), pltpu.SemaphoreType.DMA((2,)),
                        pltpu.SemaphoreType.DMA((2,))])
    by2 = by_chip.reshape(k * half, C)
    return pl.pallas_call(body, name=name, grid_spec=grid_spec, out_shape=_sds((2 * half, C), F32),
                          compiler_params=_params(2, SUM_SHARE_ID))(chip, others, core, own, by2, by2, by2)


BIG = ("w_in", "w_proj_a", "w_proj_b", "w_out", "w_up", "w_down")
COL_SHARDED = ("w_in", "w_proj_a", "w_up")
SMALL = ("norm1_g", "q_norm_g", "k_norm_g", "ret_gn_g", "ret_gn_b", "norm2_g")
ALL_W = ("norm1_g", "w_in", "q_norm_g", "k_norm_g", "ret_gn_g", "ret_gn_b", "w_proj_a", "w_proj_b", "w_out",
         "norm2_g", "w_up", "w_down")
LANES = 128


def _to_full(name, gathered):
    k, r, c = gathered.shape
    if name in COL_SHARDED:
        return gathered.transpose(1, 0, 2).reshape(r, k * c)
    return gathered.reshape(k * r, c)


def _to_shard_major(name, full):
    if name in COL_SHARDED:
        r, c4 = full.shape
        return full.reshape(r, N_CHIPS, c4 // N_CHIPS).transpose(1, 0, 2)
    r4, c = full.shape
    return full.reshape(N_CHIPS, r4 // N_CHIPS, c)


def kernel(x, norm1_g, w_in, q_norm_g, k_norm_g, ret_gn_g, ret_gn_b, w_proj_a, w_proj_b, w_out, norm2_g, w_up, w_down, loss_target, m_norm1_g, m_w_in, m_q_norm_g, m_k_norm_g, m_ret_gn_g, m_ret_gn_b, m_w_proj_a, m_w_proj_b, m_w_out, m_norm2_g, m_w_up, m_w_down, v_norm1_g, v_w_in, v_q_norm_g, v_k_norm_g, v_ret_gn_g, v_ret_gn_b, v_w_proj_a, v_w_proj_b, v_w_out, v_norm2_g, v_w_up, v_w_down):
    weights = dict(norm1_g=norm1_g, w_in=w_in, q_norm_g=q_norm_g, k_norm_g=k_norm_g, ret_gn_g=ret_gn_g,
                   ret_gn_b=ret_gn_b, w_proj_a=w_proj_a, w_proj_b=w_proj_b, w_out=w_out, norm2_g=norm2_g,
                   w_up=w_up, w_down=w_down)
    moments_m = dict(norm1_g=m_norm1_g, w_in=m_w_in, q_norm_g=m_q_norm_g, k_norm_g=m_k_norm_g, ret_gn_g=m_ret_gn_g,
                     ret_gn_b=m_ret_gn_b, w_proj_a=m_w_proj_a, w_proj_b=m_w_proj_b, w_out=m_w_out,
                     norm2_g=m_norm2_g, w_up=m_w_up, w_down=m_w_down)
    moments_v = dict(norm1_g=v_norm1_g, w_in=v_w_in, q_norm_g=v_q_norm_g, k_norm_g=v_k_norm_g, ret_gn_g=v_ret_gn_g,
                     ret_gn_b=v_ret_gn_b, w_proj_a=v_w_proj_a, w_proj_b=v_w_proj_b, w_out=v_w_out,
                     norm2_g=v_norm2_g, w_up=v_w_up, w_down=v_w_down)

    mx, my = lax.axis_index("x"), lax.axis_index("y")
    core = lax.axis_index("c").astype(jnp.int32).reshape(1)
    chip = (2 * mx + my).astype(jnp.int32).reshape(1)
    others = jnp.stack([2 * (1 - mx) + my, 2 * mx + 1 - my, 2 * (1 - mx) + 1 - my]).astype(jnp.int32)
    shards = {n: weights[n][0].astype(CDT) for n in BIG}
    def start_gather(name, names):
        return _exchange_start(name, "gather", [shards[n] for n in names],
                               [((N_CHIPS,) + shards[n].shape, CDT) for n in names])

    w_in_shape = [((N_CHIPS,) + shards["w_in"].shape, CDT)]
    n_send, n_recv, n_srcs, n_lands, n_token = _exchange_start(
        "gather_w_in_near_start", "gather", [shards["w_in"]], w_in_shape, which=(0, 1))
    late = [n for n in BIG if n != "w_in"]
    flight = {}

    def near_w_in(after):
        srcs, lands = _exchange_wait("gather_w_in_near_wait", "gather", n_send, n_recv, n_srcs, n_lands, after,
                                     which=(0, 1))
        d_send, d_recv, d_srcs, d_lands, _ = _exchange_start(
            "gather_w_in_diag_start", "gather", srcs, w_in_shape, which=(2,), lands=lands)
        flight["late"] = start_gather("gather_late_start", late)
        w_near = _pair_fill("pair_fill_w_in_near", d_lands[0], d_srcs[0], core, others[:2], chip)
        flight["diag"] = (d_send, d_recv, d_srcs, [w_near])
        return w_near, flight["late"][-1]

    def far_w_in(after):
        d_send, d_recv, d_srcs, d_lands = flight["diag"]
        srcs, lands = _exchange_wait("gather_w_in_diag_wait", "gather", d_send, d_recv, d_srcs, d_lands, after,
                                     which=(2,))
        return _pair_fill("pair_fill_w_in_diag", lands[0], srcs[0], core, others[2:], chip, write_mine=False), None

    def late_weights(after):
        l_send, l_recv, l_srcs, l_lands, _ = flight["late"]
        srcs, lands = _exchange_wait("gather_late_wait", "gather", l_send, l_recv, l_srcs, l_lands, after)
        out = {}
        for n, mine, land in zip(late, srcs, lands):
            out[n] = _to_full(n, _pair_fill("pair_fill_%s" % n, land, mine, core, others, chip))
        return out

    pending = []

    def on_grads(group):
        names = list(group)
        red = [_pair_reduce("pair_reduce_%s" % n, g if g.ndim == 3 else _to_shard_major(n, g), core)
               for n, g in group.items()]
        wires = [wire for _, wire in red]
        send, recv, srcs, lands, token = _exchange_start(
            "scatter_start_%s" % names[0], "scatter", wires, [(wire.shape, wire.dtype) for wire in wires])
        pending.append((names, [own for own, _ in red], send, recv, srcs, lands))
        return token

    small = {n: weights[n].reshape(1, -1) for n in SMALL}

    loss, grad_x, small_g = _local_step(x[0], loss_target[0], n_srcs[0], chip, others, near_w_in, far_w_in, small,
                                        late_weights, on_grads, deps0=[n_token])

    out_g, out_d, out_m, out_v = {}, {}, {}, {}
    for names, owns, send, recv, srcs, lands in pending:
        _, got = _exchange_wait("scatter_wait_%s" % names[0], "scatter", send, recv, srcs, lands, grad_x)
        for n, own, by_chip in zip(names, owns, got):
            shape = weights[n].shape
            g2 = _sum_share("sum_share_%s" % n, own, by_chip, chip, others, core)
            d, nm, nv = _adamw("adamw_%s" % n, weights[n][0], g2, moments_m[n][0], moments_v[n][0])
            out_g[n], out_d[n], out_m[n], out_v[n] = (t.reshape(shape) for t in (g2, d, nm, nv))

    packed = jnp.concatenate([small_g[n].reshape(1, -1) for n in SMALL], axis=1).reshape(-1, LANES)
    loss_tile = jnp.zeros((8, LANES), F32).at[0, 0].set(loss)
    red = _all_reduce_small(jnp.concatenate([packed, loss_tile], axis=0))
    loss = red[packed.shape[0], 0]
    red = red[:packed.shape[0]].reshape(1, -1)
    off = 0
    for n in SMALL:
        shape = weights[n].shape
        row = (1, weights[n].size)
        g2 = red[:, off:off + row[1]]
        off += row[1]
        d, nm, nv = _adamw("adamw_%s" % n, weights[n].reshape(row), g2, moments_m[n].reshape(row),
                           moments_v[n].reshape(row))
        out_g[n], out_d[n], out_m[n], out_v[n] = (t.reshape(shape) for t in (g2, d, nm, nv))

    return (loss, grad_x[None], *[out_g[n] for n in ALL_W], *[out_d[n] for n in ALL_W],
            *[out_m[n] for n in ALL_W], *[out_v[n] for n in ALL_W])
```

```python
import functools

import jax
import jax.numpy as jnp
from jax import lax
from jax.experimental import pallas as pl
from jax.experimental.pallas import tpu as pltpu

CDT = jnp.bfloat16
F32 = jnp.float32
EPS = 1e-6

ATT_GROUPS = ((128, 1), (512, 4), (2048, 16))
ATT_HPG = 4
ATT_HEADS = 12
HD = 128
BLK = 128
ATT_W = ATT_HEADS * HD
GW = ATT_HPG * HD
RET_HEADS = 4

ADAM_LR = 0.001
ADAM_B1 = 0.9
ADAM_B2 = 0.999
ADAM_EPS = 1e-08
ADAM_WD = 0.01
ADAM_STEP = 10

VMEM_LIMIT_BYTES = 56 * 1024 * 1024
MESH = pl.DeviceIdType.MESH
HBM_SPEC = pl.BlockSpec(memory_space=pltpu.HBM)
VMEM_SPEC = pl.BlockSpec(memory_space=pltpu.VMEM)


def _params(n_axes, collective_id=None):
    return pltpu.CompilerParams(dimension_semantics=("arbitrary",) * n_axes,
                                vmem_limit_bytes=VMEM_LIMIT_BYTES, collective_id=collective_id)


PAIR_FILL_ID, PAIR_REDUCE_ID, SUM_SHARE_ID = 1, 2, 3


def _sibling_barrier(first_step):
    @pl.when(first_step)
    def _():
        sem = pltpu.get_barrier_semaphore()
        x, y, c = lax.axis_index("x"), lax.axis_index("y"), lax.axis_index("c")
        pl.semaphore_signal(sem, inc=1, device_id=(x, y, 1 - c), device_id_type=pl.DeviceIdType.MESH)
        pl.semaphore_wait(sem, 1)


def _dot_nn(a, b):
    return jnp.dot(a, b, preferred_element_type=F32)


def _dot_nt(a, b):
    return lax.dot_general(a, b, (((1,), (1,)), ((), ())), preferred_element_type=F32)


def _dot_tn(a, b):
    return lax.dot_general(a, b, (((0,), (0,)), ((), ())), preferred_element_type=F32)


def _sigmoid(v):
    return 1.0 / (1.0 + jnp.exp(-v))


def _matmul(name, mode, a, b, *, tm, tn, tk, extras=(), outs, epilogue, deps=(), b_spec=None, n_cols=None,
            prefetch=(), alias_dep_to_out=None, j_outer=False):
    deps = [d for d in deps if d is not None]
    if mode == "tn":
        K, M = a.shape
    else:
        M, K = a.shape
    if b_spec is None:
        (N, K2) = b.shape if mode == "nt" else b.shape[::-1]
        assert K == K2, (name, a.shape, b.shape)
        if mode == "nt":
            b_spec = pl.BlockSpec((tn, tk), lambda i, j, k, *p: (j, k))
        else:
            b_spec = pl.BlockSpec((tk, tn), lambda i, j, k, *p: (k, j))
    else:
        N = n_cols
    assert M % tm == 0 and N % tn == 0 and K % tk == 0, (name, a.shape, b.shape)
    ni, nj, nk = M // tm, N // tn, K // tk
    if mode == "tn":
        a_spec = pl.BlockSpec((tk, tm), lambda i, j, k, *p: (k, i))
    else:
        a_spec = pl.BlockSpec((tm, tk), lambda i, j, k, *p: (i, k))
    dot = {"nn": _dot_nn, "nt": _dot_nt, "tn": _dot_tn}[mode]
    n_ex, n_out, n_dep, n_pre = len(extras), len(outs), len(deps), len(prefetch)
    grid = (ni, nj, nk)
    if j_outer:
        grid = (nj, ni, nk)

        def swapped(spec):
            return pl.BlockSpec(spec.block_shape, lambda j, i, k, *p: spec.index_map(i, j, k, *p))

        a_spec, b_spec = swapped(a_spec), swapped(b_spec)
        extras = [(e, swapped(s)) for e, s in extras]
        outs = [(o, swapped(s)) for o, s in outs]

    def body(*refs):
        refs = refs[n_pre:]
        a_ref, b_ref = refs[0], refs[1]
        ex = refs[2:2 + n_ex]
        out = refs[2 + n_ex + n_dep:2 + n_ex + n_dep + n_out]
        acc = refs[-1] if nk > 1 else None
        i = pl.program_id(1 if j_outer else 0)
        k = pl.program_id(2)
        if nk == 1:
            epilogue(dot(a_ref[...].astype(CDT), b_ref[...].astype(CDT)), ex, out, i)
            return

        @pl.when(k == 0)
        def _():
            acc[...] = jnp.zeros_like(acc)

        acc[...] += dot(a_ref[...].astype(CDT), b_ref[...].astype(CDT))

        @pl.when(k == nk - 1)
        def _():
            epilogue(acc[...], ex, out, i)

    grid_spec = pltpu.PrefetchScalarGridSpec(
        num_scalar_prefetch=n_pre, grid=grid,
        in_specs=[a_spec, b_spec] + [s for _, s in extras] + [pl.BlockSpec(memory_space=pl.ANY)] * n_dep,
        out_specs=[s for _, s in outs],
        scratch_shapes=[pltpu.VMEM((tm, tn), F32)] if nk > 1 else [])
    aliases = {}
    if alias_dep_to_out is not None:
        aliases = {n_pre + 2 + n_ex + alias_dep_to_out[0]: alias_dep_to_out[1]}
    res = pl.pallas_call(
        body, name=name, grid_spec=grid_spec, out_shape=[o for o, _ in outs], input_output_aliases=aliases,
        compiler_params=_params(3),
    )(*prefetch, a, b, *[e for e, _ in extras], *deps)
    return res


def _mn(tm, tn, col_off=0):
    return pl.BlockSpec((tm, tn), lambda i, j, k, *p: (i, j + col_off))


def _row(tn):
    return pl.BlockSpec((1, tn), lambda i, j, k, *p: (0, j))


def _ep_store(acc, ex, out, i):
    out[0][...] = acc.astype(out[0].dtype)


def _ep_resid_norm(acc, ex, out, i):
    x1 = ex[0][...] + acc
    out[0][...] = x1
    rstd = lax.rsqrt(jnp.mean(x1 * x1, axis=-1, keepdims=True) + EPS)
    out[1][...] = (x1 * rstd * ex[1][...]).astype(out[1].dtype)


def _ep_up(acc, ex, out, i):
    out[0][...] = acc.astype(out[0].dtype)
    r = jnp.maximum(acc, 0.0)
    out[1][...] = (r * r).astype(out[1].dtype)


def _ep_down_loss(acc, ex, out, i, inv_d):
    diff = (ex[0][...] + acc) - ex[1][...]
    dx2 = diff * inv_d
    out[0][...] = dx2
    out[1][...] = dx2.astype(out[1].dtype)

    @pl.when(i == 0)
    def _():
        out[2][...] = jnp.zeros_like(out[2])

    out[2][...] += jnp.sum(diff * diff, axis=0, keepdims=True)


def _ep_dh(acc, ex, out, i):
    h = ex[0][...].astype(F32)
    out[0][...] = (acc * (2.0 * jnp.maximum(h, 0.0))).astype(out[0].dtype)


def _ep_rms_bwd(acc, ex, out, i):
    x = ex[0][...]
    g = ex[1][...]
    rstd = lax.rsqrt(jnp.mean(x * x, axis=-1, keepdims=True) + EPS)
    xh = x * rstd
    dxh = acc * g
    dx = ex[2][...] + rstd * (dxh - xh * jnp.mean(dxh * xh, axis=-1, keepdims=True))
    out[0][...] = dx
    for copy in out[1:-1]:
        copy[...] = dx.astype(copy.dtype)
    dg = out[-1]

    @pl.when(i == 0)
    def _():
        dg[...] = jnp.zeros_like(dg)

    dg[...] += jnp.sum(acc * xh, axis=0, keepdims=True)


def _ep_gates(acc, ex, out, i):
    sa = _sigmoid(ex[0][...].astype(F32))
    sb = _sigmoid(ex[1][...].astype(F32))
    dpa = acc * sa
    dpb = acc * sb
    out[0][...] = dpa.astype(out[0].dtype)
    out[1][...] = dpb.astype(out[1].dtype)
    out[2][...] = (dpa * ex[2][...].astype(F32) * (1.0 - sa)).astype(out[2].dtype)
    out[3][...] = (dpb * ex[3][...].astype(F32) * (1.0 - sb)).astype(out[3].dtype)


def _sds(shape, dtype):
    return jax.ShapeDtypeStruct(shape, dtype)


def _in_proj_mine(x, g, w_mine, chip, proj_sds, deps, tm=512):
    S, D = x.shape
    ns = w_mine.shape[1]
    deps = [d for d in deps if d is not None]

    def body(c_ref, x_ref, g_ref, w_ref, *rest):
        xn_ref, proj_ref = rest[len(deps)], rest[len(deps) + 1]
        xv = x_ref[...]
        rstd = lax.rsqrt(jnp.mean(xv * xv, axis=-1, keepdims=True) + EPS)
        xn = (xv * rstd * g_ref[...]).astype(xn_ref.dtype)
        xn_ref[...] = xn
        proj_ref[...] = _dot_nn(xn, w_ref[...]).astype(proj_ref.dtype)

    grid_spec = pltpu.PrefetchScalarGridSpec(
        num_scalar_prefetch=1, grid=(S // tm,),
        in_specs=[pl.BlockSpec((tm, D), lambda i, c: (i, 0)), pl.BlockSpec((1, D), lambda i, c: (0, 0)),
                  pl.BlockSpec((D, ns), lambda i, c: (0, 0))] + [pl.BlockSpec(memory_space=pl.ANY)] * len(deps),
        out_specs=[pl.BlockSpec((tm, D), lambda i, c: (i, 0)), pl.BlockSpec((tm, ns), lambda i, c: (i, c[0]))])
    return pl.pallas_call(body, name="in_proj_mine", grid_spec=grid_spec, out_shape=[_sds((S, D), CDT), proj_sds],
                          compiler_params=_params(1))(chip, x, g, w_mine, *deps)


def _rm_shape(S, d, width):
    return (S, width) if d == 1 else (d, S // d, width)


def _rm_spec(tm, d, width):
    if d == 1:
        return pl.BlockSpec((tm, width), lambda i: (i, 0))
    return pl.BlockSpec((d, tm // d, width), lambda i: (0, i, 0))


def _rm_put(dst_ref, cols, buf_ref, d):
    if d == 1:
        dst_ref[:, cols] = buf_ref[...].astype(dst_ref.dtype)
        return
    m = buf_ref.shape[0] // d
    for r in range(d):
        dst_ref[r, :, cols] = buf_ref[pl.ds(r, m, stride=d), :].astype(dst_ref.dtype)


def _rm_reader(buf_ref, src_ref, d):
    if d == 1:
        return lambda s, rows: src_ref[rows, s * HD:(s + 1) * HD].astype(F32)
    m = buf_ref.shape[1] // d
    for s in range(buf_ref.shape[0]):
        for r in range(d):
            buf_ref.at[s][pl.ds(r, m, stride=d), :] = src_ref[r, :, s * HD:(s + 1) * HD].astype(F32)
    return lambda s, rows: buf_ref.at[s][rows, :]


def _qknorm_fwd(proj, gqk, tm=512):
    S = proj.shape[0]
    W = 2 * ATT_W
    dil = [d for _, d in ATT_GROUPS]

    def body(p_ref, g_ref, o0, o1, o2, buf):
        outs = (o0, o1, o2)
        for hd in range(3 * ATT_HEADS):
            which, head = hd // ATT_HEADS, hd % ATT_HEADS
            grp, slot = head // ATT_HPG, head % ATT_HPG
            cols = slice(hd * HD, (hd + 1) * HD)

            def chunk(rows, which=which, cols=cols):
                v = p_ref[rows, cols].astype(F32)
                if which < 2:
                    rstd = lax.rsqrt(jnp.mean(v * v, axis=-1, keepdims=True) + EPS)
                    v = v * rstd * g_ref[:, cols]
                buf[rows, :] = v

            chunk(slice(None))
            _rm_put(outs[grp], slice(which * GW + slot * HD, which * GW + (slot + 1) * HD), buf, dil[grp])

    return pl.pallas_call(
        body, name="qknorm_fwd", grid=(S // tm,),
        in_specs=[pl.BlockSpec((tm, 3 * ATT_W), lambda i: (i, 0)), pl.BlockSpec((1, W), lambda i: (0, 0))],
        out_specs=[_rm_spec(tm, d, 3 * GW) for d in dil],
        out_shape=[_sds(_rm_shape(S, d, 3 * GW), CDT) for d in dil],
        scratch_shapes=[pltpu.VMEM((tm, HD), F32)],
        compiler_params=_params(1))(proj, gqk)


def _qknorm_bwd(proj, gqk, dqs, dks, dvs, dproj, tm=256):
    S = proj.shape[0]
    W = 2 * ATT_W
    dil = [d for _, d in ATT_GROUPS]

    def body(p_ref, g_ref, *refs):
        ins = refs[0:9]
        o_ref, dg_ref = refs[10], refs[11]
        bufs = refs[12:21]
        i = pl.program_id(0)

        @pl.when(i == 0)
        def _():
            dg_ref[...] = jnp.zeros_like(dg_ref)

        nat = [_rm_reader(bufs[j], ins[j], dil[j % 3]) for j in range(9)]
        dq_get, dk_get, dv_get = nat[0:3], nat[3:6], nat[6:9]
        for hd in range(2 * ATT_HEADS):
            sl = slice(hd * HD, (hd + 1) * HD)
            head = hd % ATT_HEADS
            grp, slot = head // ATT_HPG, head % ATT_HPG
            get = (dq_get if hd < ATT_HEADS else dk_get)[grp]

            def chunk(rows, sl=sl, slot=slot, get=get):
                dn = get(slot, rows)
                v = p_ref[rows, sl].astype(F32)
                rstd = lax.rsqrt(jnp.mean(v * v, axis=-1, keepdims=True) + EPS)
                vh = v * rstd
                dg_ref[:, sl] += jnp.sum(dn * vh, axis=0, keepdims=True)
                dvh = dn * g_ref[:, sl]
                o_ref[rows, sl] = (rstd * (dvh - vh * jnp.mean(dvh * vh, axis=-1, keepdims=True))).astype(o_ref.dtype)

            chunk(slice(None))
        for head in range(ATT_HEADS):
            grp, slot = head // ATT_HPG, head % ATT_HPG
            o_ref[:, W + head * HD:W + (head + 1) * HD] = dv_get[grp](slot, slice(None)).astype(o_ref.dtype)

    return pl.pallas_call(
        body, name="qknorm_bwd", grid=(S // tm,),
        in_specs=[pl.BlockSpec((tm, W), lambda i: (i, 0)), pl.BlockSpec((1, W), lambda i: (0, 0))]
        + [_rm_spec(tm, d, GW) for d in dil] * 3 + [pl.BlockSpec(memory_space=pl.ANY)],
        out_specs=[pl.BlockSpec((tm, 3 * ATT_W), lambda i: (i, 0)), pl.BlockSpec((1, W), lambda i: (0, 0))],
        out_shape=[_sds(dproj.shape, dproj.dtype), _sds((1, W), F32)],
        scratch_shapes=[pltpu.VMEM((ATT_HPG, tm, HD), F32)] * 9,
        input_output_aliases={11: 0},
        compiler_params=_params(1))(proj, gqk, *dqs, *dks, *dvs, dproj)


def _att_mask(n):
    qi = lax.broadcasted_iota(jnp.int32, (BLK, 2 * BLK), 0)
    kj = lax.broadcasted_iota(jnp.int32, (BLK, 2 * BLK), 1)
    dist = BLK + qi - kj
    valid_all = (dist >= 0) & (dist <= BLK)
    return valid_all & ((kj >= BLK) | (n > 0)), valid_all, dist.astype(F32)


def _att_slopes(grp):
    return [2.0 ** (-8.0 * (grp * ATT_HPG + hh + 1) / ATT_HEADS) for hh in range(ATT_HPG)]


ATT_PLANES_PER_STEP = 8


def _att_planes(d):
    return d if d > 1 else ATT_PLANES_PER_STEP


def _att_step_planes(d):
    return min(_att_planes(d), ATT_PLANES_PER_STEP)


def _att_3d(a, d):
    return a.reshape(ATT_PLANES_PER_STEP, a.shape[0] // ATT_PLANES_PER_STEP, a.shape[1]) if d == 1 else a


def _att_spec(R, row_fn, col=0):
    return pl.BlockSpec((R, BLK, GW), lambda r, n: (r, row_fn(n), col))


def _att_chains(R):
    return [(rr * ATT_HPG + hh, rr, slice(hh * HD, (hh + 1) * HD), hh) for rr in range(R) for hh in range(ATT_HPG)]


def _att_qkv_specs(R, nb):
    last = nb - 1

    def cur(n):
        return jnp.minimum(n, last)

    def prev(n):
        return jnp.maximum(jnp.minimum(n, last) - 1, 0)

    return [_att_spec(R, cur, 0), _att_spec(R, prev, 1), _att_spec(R, cur, 1), _att_spec(R, prev, 2),
            _att_spec(R, cur, 2), _att_spec(R, lambda n: last, 1), _att_spec(R, lambda n: last, 2)]


def _att_prev(seg, n, prev_ref, last_ref, rr, sl):
    t = prev_ref[rr, :, sl]
    if seg and rr > 0:
        t = jnp.where(n == 0, last_ref[rr - 1, :, sl], t)
    return t


def _att_fwd(grp, S, qkv):
    _, d = ATT_GROUPS[grp]
    seg = d == 1
    P = _att_planes(d)
    L = S // P
    nb = L // BLK
    R = _att_step_planes(d)
    assert P % R == 0 and (not seg or P == R)
    slopes = _att_slopes(grp)
    scale = HD ** -0.5
    chains = _att_chains(R)

    def body(q_ref, kp_ref, kc_ref, vp_ref, vc_ref, kl_ref, vl_ref, o_ref, l_ref, s_buf, p_buf, den_buf):
        n = pl.program_id(1)
        valid, valid_all, distf = _att_mask(n)
        for c, rr, sl, hh in chains:
            k = jnp.concatenate([_att_prev(seg, n, kp_ref, kl_ref, rr, sl), kc_ref[rr, :, sl]], axis=0)
            s_buf[c] = _dot_nt(q_ref[rr, :, sl], k)
        for c, rr, sl, hh in chains:
            s = s_buf[c] * scale + (-slopes[hh] * d) * distf
            s = jnp.where(valid_all if seg and rr > 0 else valid, s, -1e30)
            m = jnp.max(s, axis=-1, keepdims=True)
            p = jnp.exp(s - m)
            den = jnp.sum(p, axis=-1, keepdims=True)
            p_buf[c] = p.astype(CDT)
            den_buf[c] = jnp.broadcast_to(den, (BLK, HD))
            l_ref[rr, :, sl] = jnp.broadcast_to(m + jnp.log(den), (BLK, HD))
        for c, rr, sl, hh in chains:
            v = jnp.concatenate([_att_prev(seg, n, vp_ref, vl_ref, rr, sl), vc_ref[rr, :, sl]], axis=0)
            o_ref[rr, :, sl] = _dot_nn(p_buf[c], v) / den_buf[c]

    out_spec = _att_spec(R, lambda n: n)
    n_ch = len(chains)
    q3 = _att_3d(qkv, d)
    o, l = pl.pallas_call(
        body, name="att_fwd_g%d" % grp, grid=(P // R, nb),
        in_specs=_att_qkv_specs(R, nb),
        out_specs=[out_spec, out_spec],
        out_shape=[_sds((P, L, GW), F32)] * 2,
        scratch_shapes=[pltpu.VMEM((n_ch, BLK, 2 * BLK), F32), pltpu.VMEM((n_ch, BLK, 2 * BLK), CDT),
                        pltpu.VMEM((n_ch, BLK, HD), F32)],
        compiler_params=_params(2),
    )(*[q3] * 7)
    return o.reshape(_rm_shape(S, d, GW)), l.reshape(_rm_shape(S, d, GW))


def _att_bwd(grp, S, qkv, lse, do_g, c_g):
    _, d = ATT_GROUPS[grp]
    seg = d == 1
    P = _att_planes(d)
    L = S // P
    nb = L // BLK
    R = _att_step_planes(d)
    assert P % R == 0 and (not seg or P == R)
    slopes = _att_slopes(grp)
    scale = HD ** -0.5
    last = nb - 1
    chains = _att_chains(R)

    def body(q_ref, kp_ref, kc_ref, vp_ref, vc_ref, kl_ref, vl_ref, l_ref, do_ref, c_ref, dq_ref, dk_ref, dv_ref,
             ck, cv, fk, fv, s_buf, dp_buf, p_buf, ds_buf):
        n = pl.program_id(1)

        @pl.when(n == 0)
        def _():
            for buf in (ck, cv, fk, fv):
                buf[...] = jnp.zeros_like(buf)

        @pl.when(n < nb)
        def _():
            valid, valid_all, distf = _att_mask(n)
            for c, rr, sl, hh in chains:
                k = jnp.concatenate([_att_prev(seg, n, kp_ref, kl_ref, rr, sl), kc_ref[rr, :, sl]], axis=0)
                v = jnp.concatenate([_att_prev(seg, n, vp_ref, vl_ref, rr, sl), vc_ref[rr, :, sl]], axis=0)
                s_buf[c] = _dot_nt(q_ref[rr, :, sl], k)
                dp_buf[c] = _dot_nt(do_ref[rr, :, sl], v)
            for c, rr, sl, hh in chains:
                s = s_buf[c] * scale + (-slopes[hh] * d) * distf
                p = jnp.where(valid_all if seg and rr > 0 else valid, jnp.exp(s - l_ref[rr, :, sl][:, 0:1]), 0.0)
                p_buf[c] = p.astype(CDT)
                ds_buf[c] = (p * (dp_buf[c] + c_ref[rr, :, sl][:, 0:1]) * scale).astype(CDT)
            for c, rr, sl, hh in chains:
                k = jnp.concatenate([_att_prev(seg, n, kp_ref, kl_ref, rr, sl), kc_ref[rr, :, sl]], axis=0)
                ds = ds_buf[c]
                dq_ref[rr, :, sl] = _dot_nn(ds, k)
                dk = _dot_tn(ds, q_ref[rr, :, sl])
                dv = _dot_tn(p_buf[c], do_ref[rr, :, sl])
                dk_ref[rr, :, sl] = ck[rr, :, sl] + dk[0:BLK]
                dv_ref[rr, :, sl] = cv[rr, :, sl] + dv[0:BLK]
                ck[rr, :, sl] = dk[BLK:2 * BLK]
                cv[rr, :, sl] = dv[BLK:2 * BLK]
                if seg and rr > 0:
                    @pl.when(n == 0)
                    def _(rr=rr, sl=sl, dk=dk, dv=dv):
                        fk[rr - 1, :, sl] = dk[0:BLK]
                        fv[rr - 1, :, sl] = dv[0:BLK]

        @pl.when(n == nb)
        def _():
            dk_ref[...] = ck[...] + fk[...]
            dv_ref[...] = cv[...] + fv[...]

    blk = (R, BLK, GW)
    at_q = _att_spec(R, lambda n: jnp.minimum(n, last))
    behind = _att_spec(R, lambda n: jnp.maximum(n - 1, 0))
    n_ch = len(chains)
    q3 = _att_3d(qkv, d)
    res = pl.pallas_call(
        body, name="att_bwd_g%d" % grp, grid=(P // R, nb + 1),
        in_specs=_att_qkv_specs(R, nb) + [at_q, at_q, at_q],
        out_specs=[at_q, behind, behind],
        out_shape=[_sds((P, L, GW), F32)] * 3,
        scratch_shapes=[pltpu.VMEM(blk, F32)] * 4
        + [pltpu.VMEM((n_ch, BLK, 2 * BLK), F32), pltpu.VMEM((n_ch, BLK, 2 * BLK), F32),
           pltpu.VMEM((n_ch, BLK, 2 * BLK), CDT), pltpu.VMEM((n_ch, BLK, 2 * BLK), CDT)],
        compiler_params=_params(2),
    )(*[q3] * 7, _att_3d(lse, d), _att_3d(do_g, d), _att_3d(c_g, d))
    return [t.reshape(_rm_shape(S, d, GW)) for t in res]


def _mix_alpha(l0, l1, l2):
    mx = jnp.maximum(jnp.maximum(l0, l1), l2)
    e = [jnp.exp(l0 - mx), jnp.exp(l1 - mx), jnp.exp(l2 - mx)]
    tot = e[0] + e[1] + e[2]
    return [ei / tot for ei in e]


def _mix_fwd(S, os_, ls_, tm=512):
    dil = [d for _, d in ATT_GROUPS]

    def body(*refs):
        out, bufs = refs[6], refs[7:13]
        get = [_rm_reader(bufs[j], refs[j], dil[j % 3]) for j in range(6)]
        rows = slice(None)
        for s in range(ATT_HPG):
            al = _mix_alpha(*[get[3 + g](s, rows) for g in range(3)])
            mixed = al[0] * get[0](s, rows) + al[1] * get[1](s, rows) + al[2] * get[2](s, rows)
            out[:, s * HD:(s + 1) * HD] = mixed.astype(out.dtype)

    specs = [_rm_spec(tm, d, GW) for d in dil]
    return pl.pallas_call(
        body, name="mix_fwd", grid=(S // tm,), in_specs=specs * 2, out_specs=pl.BlockSpec((tm, GW), lambda i: (i, 0)),
        out_shape=_sds((S, GW), CDT), scratch_shapes=[pltpu.VMEM((ATT_HPG, tm, HD), F32)] * 6,
        compiler_params=_params(1))(*os_, *ls_)


def _mix_bwd(S, os_, ls_, do_a, tm=512):
    dil = [d for _, d in ATT_GROUPS]

    def body(*refs):
        d_ref, outs, bufs, tmps = refs[6], refs[7:13], refs[13:19], refs[19:25]
        get = [_rm_reader(bufs[j], refs[j], dil[j % 3]) for j in range(6)]
        for s in range(ATT_HPG):
            cols = slice(s * HD, (s + 1) * HD)

            def chunk(rows, s=s, cols=cols):
                al = _mix_alpha(*[get[3 + g](s, rows) for g in range(3)])
                dv = d_ref[rows, cols]
                o_a = al[0] * get[0](s, rows) + al[1] * get[1](s, rows) + al[2] * get[2](s, rows)
                dsum = jnp.sum(dv * o_a, axis=-1, keepdims=True)
                for g in range(3):
                    tmps[g][rows, :] = al[g] * dv
                    tmps[3 + g][rows, :] = -(al[g] * dsum)

            chunk(slice(None))
            for j in range(6):
                _rm_put(outs[j], cols, tmps[j], dil[j % 3])

    specs = [_rm_spec(tm, d, GW) for d in dil]
    res = pl.pallas_call(
        body, name="mix_bwd", grid=(S // tm,), in_specs=specs * 2 + [pl.BlockSpec((tm, GW), lambda i: (i, 0))],
        out_specs=specs * 2,
        out_shape=[_sds(_rm_shape(S, d, GW), CDT) for d in dil] + [_sds(_rm_shape(S, d, GW), F32) for d in dil],
        scratch_shapes=[pltpu.VMEM((ATT_HPG, tm, HD), F32)] * 6 + [pltpu.VMEM((tm, HD), F32)] * 6,
        compiler_params=_params(1))(*os_, *ls_, do_a)
    return res[:3], res[3:]


def _ret_tables(dk):
    H, C = RET_HEADS, BLK
    log_g = jnp.log(1.0 - 2.0 ** (-5.0 - jnp.arange(H, dtype=F32)))
    idx = jnp.arange(C, dtype=F32)
    diff = idx[:, None] - idx[None, :]
    decay = jnp.where(diff >= 0, jnp.exp(log_g[:, None, None] * jnp.maximum(diff, 0.0)), 0.0)
    xi = jnp.exp(log_g[:, None] * (idx[None, :] + 1.0))
    zeta = jnp.exp(log_g[:, None] * (C - 1.0 - idx[None, :])) * (dk ** -0.5)
    g_chunk = jnp.exp(log_g * C)
    bc = lambda t: jnp.broadcast_to(t[:, :, None], (H, C, C))
    return decay, bc(xi), bc(zeta), jnp.broadcast_to(g_chunk[:, None, None], (H, 8, C))


def _gn_fwd(o, g, b):
    mu = jnp.mean(o, axis=-1, keepdims=True)
    xc = o - mu
    rstd = lax.rsqrt(jnp.mean(xc * xc, axis=-1, keepdims=True) + EPS)
    yh = xc * rstd
    return yh, rstd, yh * g + b


def _ret_specs(dk, dv, order):
    H = RET_HEADS
    qk_w, v_w = H * dk, H * dv
    off_q = 3 * ATT_W
    off_k, off_v, off_g = off_q + qk_w, off_q + 2 * qk_w, off_q + 2 * qk_w + v_w
    assert 2 * dk == dv and all(off % dv == 0 for off in (off_q, off_k, off_v, off_g))

    def col(off, j):
        return pl.BlockSpec((BLK, dv), lambda i: (order(i), off // dv + j))

    tab = pl.BlockSpec((H, BLK, BLK), lambda i: (0, 0, 0))
    return ([col(off_q, j) for j in range(H // 2)] + [col(off_k, j) for j in range(H // 2)]
            + [col(off_v, j) for j in range(H)] + [col(off_g, j) for j in range(H)]
            + [tab, tab, tab, pl.BlockSpec((H, 8, BLK), lambda i: (0, 0, 0))])


def _ret_heads(refs, dk):
    H = RET_HEADS
    q_refs, k_refs = refs[0:H // 2], refs[H // 2:H]
    v_refs, gr_refs = refs[H:2 * H], refs[2 * H:3 * H]

    def head(h):
        cols = slice((h % 2) * dk, (h % 2 + 1) * dk)
        return q_refs[h // 2][:, cols], k_refs[h // 2][:, cols], v_refs[h][...], gr_refs[h][...]

    return head, refs[3 * H:3 * H + 4]


def _ret_fwd(proj, gn_g, gn_b, dk, dv):
    S = proj.shape[0]
    N = S // BLK
    H = RET_HEADS
    kscale = dk ** -0.5
    n_in = 3 * H + 4

    def body(*refs):
        head, (dec_ref, xi_ref, zeta_ref, gc_ref) = _ret_heads(refs, dk)
        g_ref, b_ref, opre_ref, or_ref, st_ref, state, s_buf, cross_buf = refs[n_in:n_in + 8]
        n = pl.program_id(0)

        @pl.when(n == 0)
        def _():
            state[...] = jnp.zeros_like(state)

        for h in range(H):
            q, k, v, _ = head(h)
            s_buf[h] = _dot_nt(q, k)
            st = state[h]
            st_c = st.astype(CDT)
            st_ref[h] = st_c
            cross_buf[h] = _dot_nn(q, st_c)
            kz = (k.astype(F32) * zeta_ref[h][:, 0:1]).astype(CDT)
            state[h] = st * gc_ref[h][0:1, 0:1] + _dot_tn(kz, v)
        for h in range(H):
            vs = slice(h * dv, (h + 1) * dv)
            _, _, v, gr = head(h)
            s = s_buf[h] * kscale * dec_ref[h]
            o = _dot_nn(s.astype(CDT), v) + cross_buf[h] * xi_ref[h][:, 0:1]
            opre_ref[:, vs] = o
            _, _, y = _gn_fwd(o, g_ref[:, vs], b_ref[:, vs])
            gr = gr.astype(F32)
            or_ref[:, vs] = (y * (gr * _sigmoid(gr))).astype(or_ref.dtype)

    v_w = H * dv
    row = pl.BlockSpec((1, v_w), lambda i: (0, 0))
    tile = pl.BlockSpec((BLK, v_w), lambda i: (i, 0))
    return pl.pallas_call(
        body, name="ret_fwd", grid=(N,),
        in_specs=_ret_specs(dk, dv, lambda i: i) + [row, row],
        out_specs=[tile, tile, pl.BlockSpec((None, H, dk, dv), lambda i: (i, 0, 0, 0))],
        out_shape=[_sds((S, v_w), F32), _sds((S, v_w), CDT), _sds((N, H, dk, dv), CDT)],
        scratch_shapes=[pltpu.VMEM((H, dk, dv), F32), pltpu.VMEM((H, BLK, BLK), F32), pltpu.VMEM((H, BLK, dv), F32)],
        compiler_params=_params(1),
    )(*[proj] * (3 * H), *_ret_tables(dk), gn_g, gn_b)


def _ret_bwd(proj, gn_g, gn_b, o_pre, states, d_or, dga, dgb, dk, dv):
    S, in_w = proj.shape
    N = S // BLK
    H = RET_HEADS
    qk_w, v_w = H * dk, H * dv
    kscale = dk ** -0.5
    n_in = 3 * H + 4
    out_w = 2 * qk_w + 2 * v_w
    gate_w = dga.shape[1]
    col0 = 3 * ATT_W
    assert col0 + out_w + 2 * gate_w == in_w
    rev = lambda i: N - 1 - i

    def body(*refs):
        head, (dec_ref, xi_ref, zeta_ref, gc_ref) = _ret_heads(refs, dk)
        (g_ref, b_ref, opre_ref, st_ref, dor_ref, dga_ref, dgb_ref, dproj_ref, dg_ref, db_ref, dstate, stage,
         sem, do_buf, dox_buf, a_buf, g_buf, dq_buf, dk_buf, dv_buf) = refs[n_in:n_in + 20]
        i = pl.program_id(0)
        slot = i % 2
        out_ref = stage.at[slot]

        def out_copy(s, step):
            rows = pl.ds(pl.multiple_of(rev(step) * BLK, BLK), BLK)
            return pltpu.make_async_copy(stage.at[s], dproj_ref.at[rows, pl.ds(col0, in_w - col0)], sem.at[s])

        @pl.when(i >= 2)
        def _():
            out_copy(slot, i - 2).wait()

        @pl.when(i == 0)
        def _():
            dstate[...] = jnp.zeros_like(dstate)
            dg_ref[...] = jnp.zeros_like(dg_ref)
            db_ref[...] = jnp.zeros_like(db_ref)

        out_ref[:, out_w:out_w + gate_w] = dga_ref[...]
        out_ref[:, out_w + gate_w:out_w + 2 * gate_w] = dgb_ref[...]
        for h in range(H):
            vs = slice(h * dv, (h + 1) * dv)
            _, _, _, gr = head(h)
            gr = gr.astype(F32)
            sg = _sigmoid(gr)
            gain = g_ref[:, vs]
            yh, rstd, y = _gn_fwd(opre_ref[:, vs], gain, b_ref[:, vs])
            d_or_v = dor_ref[:, vs]
            dy = d_or_v * (gr * sg)
            out_ref[:, 2 * qk_w + v_w + h * dv:2 * qk_w + v_w + (h + 1) * dv] = (
                d_or_v * y * (sg * (1.0 + gr * (1.0 - sg)))).astype(out_ref.dtype)
            dg_ref[:, vs] += jnp.sum(dy * yh, axis=0, keepdims=True)
            db_ref[:, vs] += jnp.sum(dy, axis=0, keepdims=True)
            dyh = dy * gain
            do = rstd * (dyh - jnp.mean(dyh, axis=-1, keepdims=True)
                         - yh * jnp.mean(dyh * yh, axis=-1, keepdims=True))
            do_buf[h] = do.astype(CDT)
            dox_buf[h] = (do * xi_ref[h][:, 0:1]).astype(CDT)
        for h in range(H):
            q, k, v, _ = head(h)
            dox = dox_buf[h]
            a_buf[h] = _dot_nt(q, k)
            g_buf[h] = _dot_nt(do_buf[h], v)
            dsn = dstate[h]
            dsn_c = dsn.astype(CDT)
            kz = (k.astype(F32) * zeta_ref[h][:, 0:1]).astype(CDT)
            dq_buf[h] = _dot_nt(dox, st_ref[h])
            dk_buf[h] = _dot_nt(v, dsn_c)
            dv_buf[h] = _dot_nn(kz, dsn_c)
            dstate[h] = dsn * gc_ref[h][0:1, 0:1] + _dot_tn(q, dox)
        for h in range(H):
            q, k, _, _ = head(h)
            decay = dec_ref[h]
            a_c = (a_buf[h] * kscale * decay).astype(CDT)
            g_c = (g_buf[h] * decay).astype(CDT)
            dq = _dot_nn(g_c, k) * kscale + dq_buf[h]
            dkk = _dot_tn(g_c, q) * kscale + dk_buf[h] * zeta_ref[h][:, 0:1]
            dvv = _dot_tn(a_c, do_buf[h]) + dv_buf[h]
            out_ref[:, h * dk:(h + 1) * dk] = dq.astype(out_ref.dtype)
            out_ref[:, qk_w + h * dk:qk_w + (h + 1) * dk] = dkk.astype(out_ref.dtype)
            out_ref[:, 2 * qk_w + h * dv:2 * qk_w + (h + 1) * dv] = dvv.astype(out_ref.dtype)

        cp = out_copy(slot, i)
        cp.start()

        @pl.when(i == N - 1)
        def _():
            cp.wait()
            if N >= 2:
                out_copy(1 - slot, i - 1).wait()

    row = pl.BlockSpec((1, v_w), lambda i: (0, 0))
    tile = pl.BlockSpec((BLK, v_w), lambda i: (rev(i), 0))
    gate = pl.BlockSpec((BLK, gate_w), lambda i: (rev(i), 0))
    return pl.pallas_call(
        body, name="ret_bwd", grid=(N,),
        in_specs=_ret_specs(dk, dv, rev) + [row, row, tile,
                 pl.BlockSpec((None, H, dk, dv), lambda i: (rev(i), 0, 0, 0)), tile, gate, gate],
        out_specs=[pl.BlockSpec(memory_space=pl.ANY), row, row],
        out_shape=[_sds((S, in_w), CDT), _sds((1, v_w), F32), _sds((1, v_w), F32)],
        scratch_shapes=[pltpu.VMEM((H, dk, dv), F32), pltpu.VMEM((2, BLK, in_w - col0), CDT),
                        pltpu.SemaphoreType.DMA((2,)),
                        pltpu.VMEM((H, BLK, dv), CDT), pltpu.VMEM((H, BLK, dv), CDT),
                        pltpu.VMEM((H, BLK, BLK), F32), pltpu.VMEM((H, BLK, BLK), F32),
                        pltpu.VMEM((H, BLK, dk), F32), pltpu.VMEM((H, BLK, dk), F32), pltpu.VMEM((H, BLK, dv), F32)],
        compiler_params=_params(1),
    )(*[proj] * (3 * H), *_ret_tables(dk), gn_g, gn_b, o_pre, states, d_or, dga, dgb)


def _merge_fwd(o_a, o_r, wa, wb, proj, d_model, tm=1024, tn=512):
    S, in_w = proj.shape
    off_a, off_b = in_w - 2 * d_model, in_w - d_model
    assert off_a % tn == 0 and off_b % tn == 0

    def body(oa_ref, or_ref, wa_ref, wb_ref, ga_ref, gb_ref, y_ref, pa_ref, pb_ref):
        pa = _dot_nn(oa_ref[...], wa_ref[...])
        pb = _dot_nn(or_ref[...], wb_ref[...])
        y = _sigmoid(ga_ref[...].astype(F32)) * pa + _sigmoid(gb_ref[...].astype(F32)) * pb
        y_ref[...] = y.astype(y_ref.dtype)
        pa_ref[...] = pa.astype(pa_ref.dtype)
        pb_ref[...] = pb.astype(pb_ref.dtype)

    ka, kb = o_a.shape[1], o_r.shape[1]
    out = pl.BlockSpec((tm, tn), lambda i, j: (i, j))
    return pl.pallas_call(
        body, name="merge_fwd", grid=(S // tm, d_model // tn),
        in_specs=[pl.BlockSpec((tm, ka), lambda i, j: (i, 0)), pl.BlockSpec((tm, kb), lambda i, j: (i, 0)),
                  pl.BlockSpec((ka, tn), lambda i, j: (0, j)), pl.BlockSpec((kb, tn), lambda i, j: (0, j)),
                  pl.BlockSpec((tm, tn), lambda i, j: (i, off_a // tn + j)),
                  pl.BlockSpec((tm, tn), lambda i, j: (i, off_b // tn + j))],
        out_specs=[out, out, out], out_shape=[_sds((S, d_model), CDT)] * 3,
        compiler_params=_params(2))(o_a, o_r, wa, wb, proj, proj)


def _d_x(dproj, w_in, x, g, dx1, dep, tm=512, row_groups=2):
    S, D = x.shape
    n_sh, _, ns = w_in.shape
    nt = S // tm // row_groups
    deps = [d for d in (dep,) if d is not None]

    def body(a_ref, b_ref, x_ref, g_ref, r_ref, *rest):
        dx_ref, dg_ref, acc = rest[len(deps):]
        h, k, i = pl.program_id(0), pl.program_id(1), pl.program_id(2)

        @pl.when(k == 0)
        def _():
            acc[i] = jnp.zeros((tm, D), F32)

        acc[i] += _dot_nt(a_ref[...], b_ref[...])

        @pl.when(k == n_sh - 1)
        def _():
            _ep_rms_bwd(acc[i], (x_ref, g_ref, r_ref), (dx_ref, dg_ref), h * nt + i)

    def last_only(h, k, i):
        return (h * nt + jnp.where(k == n_sh - 1, i, 0), 0)

    return pl.pallas_call(
        body, name="d_x", grid=(row_groups, n_sh, nt),
        in_specs=[pl.BlockSpec((tm, ns), lambda h, k, i: (h * nt + i, k)),
                  pl.BlockSpec((None, D, ns), lambda h, k, i: (k, 0, 0)),
                  pl.BlockSpec((tm, D), last_only), pl.BlockSpec((1, D), lambda h, k, i: (0, 0)),
                  pl.BlockSpec((tm, D), last_only)] + [pl.BlockSpec(memory_space=pl.ANY)] * len(deps),
        out_specs=[pl.BlockSpec((tm, D), last_only), pl.BlockSpec((1, D), lambda h, k, i: (0, 0))],
        out_shape=[_sds((S, D), F32), _sds((1, D), F32)],
        scratch_shapes=[pltpu.VMEM((nt, tm, D), F32)],
        compiler_params=_params(3))(dproj, w_in, x, g, dx1, *deps)


def _local_step(x, target, w_in_mine, chip, others, near_w_in, far_w_in, small, late_weights, on_grads, deps0=()):
    S, D = x.shape
    ns_in = w_in_mine.shape[1]
    in_w = N_CHIPS * ns_in
    d_ff = 4 * D
    ret_v_w = 2 * D
    dv = ret_v_w // RET_HEADS
    dk = (in_w - 3 * ATT_W - 2 * ret_v_w - 2 * D) // (2 * RET_HEADS)
    gqk = jnp.concatenate([small["q_norm_g"].reshape(1, ATT_W), small["k_norm_g"].reshape(1, ATT_W)], axis=1)
    g1, g2 = small["norm1_g"], small["norm2_g"]
    gn_g, gn_b = small["ret_gn_g"], small["ret_gn_b"]

    proj_sds = _sds((S, in_w), CDT)
    xn, proj = _in_proj_mine(x, g1, w_in_mine, chip, proj_sds, deps0)
    for stage, (get_w_in, chips) in enumerate(((near_w_in, others[:2]), (far_w_in, others[2:]))):
        w_in, started = get_w_in(proj)
        (proj,) = _matmul(
            "in_proj_far%d" % stage, "nn", xn, w_in, tm=1024, tn=ns_in, tk=D, prefetch=[chips],
            n_cols=chips.shape[0] * ns_in, b_spec=pl.BlockSpec((None, D, ns_in), lambda i, j, k, o: (o[j], 0, 0)),
            outs=[(proj_sds, pl.BlockSpec((1024, ns_in), lambda i, j, k, o: (i, o[j])))], epilogue=_ep_store,
            deps=[proj, started], alias_dep_to_out=(0, 0), j_outer=True)
    qkv = _qknorm_fwd(proj, gqk)
    att = [_att_fwd(g, S, qkv[g]) for g in range(3)]
    os_, ls_ = [a[0] for a in att], [a[1] for a in att]
    o_a = _mix_fwd(S, os_, ls_)
    o_pre, o_r, states = _ret_fwd(proj, gn_g, gn_b, dk, dv)
    w = late_weights(o_r)
    y, pa, pb = _merge_fwd(o_a, o_r, w["w_proj_a"], w["w_proj_b"], proj, D)
    x1, xn2 = _matmul("out_proj", "nn", y, w["w_out"], tm=1024, tn=D, tk=D,
                      extras=[(x, _mn(1024, D)), (g2, _row(D))],
                      outs=[(_sds((S, D), F32), _mn(1024, D)), (_sds((S, D), CDT), _mn(1024, D))],
                      epilogue=_ep_resid_norm)
    hid, act = _matmul("mlp_up", "nn", xn2, w["w_up"], tm=1024, tn=2048, tk=D, j_outer=True,
                       outs=[(_sds((S, d_ff), CDT), _mn(1024, 2048))] * 2, epilogue=_ep_up)
    dx2, dx2c, loss_row = _matmul(
        "mlp_down_loss", "nn", act, w["w_down"], tm=512, tn=D, tk=d_ff,
        extras=[(x1, _mn(512, D)), (target, _mn(512, D))],
        outs=[(_sds((S, D), F32), _mn(512, D)), (_sds((S, D), CDT), _mn(512, D)), (_sds((1, D), F32), _row(D))],
        epilogue=functools.partial(_ep_down_loss, inv_d=1.0 / D))
    loss = 0.5 * jnp.sum(loss_row) / D

    (dh,) = _matmul("d_hidden", "nt", dx2c, w["w_down"], tm=1024, tn=2048, tk=D, j_outer=True,
                    extras=[(hid, _mn(1024, 2048))], outs=[(_sds((S, d_ff), CDT), _mn(1024, 2048))],
                    epilogue=_ep_dh)
    (gw_down,) = _matmul("dw_down", "tn", act, dx2c, tm=1024, tn=D, tk=2048,
                         outs=[(_sds((d_ff, D), F32), _mn(1024, D))], epilogue=_ep_store)
    (gw_up,) = _matmul("dw_up", "tn", xn2, dh, tm=D, tn=1024, tk=2048,
                       outs=[(_sds((D, d_ff), F32), _mn(D, 1024))], epilogue=_ep_store)
    tok = on_grads({"w_down": gw_down, "w_up": gw_up})
    dx1, dx1c, dg2 = _matmul(
        "d_x1", "nt", dh, w["w_up"], tm=512, tn=D, tk=d_ff,
        extras=[(x1, _mn(512, D)), (g2, _row(D)), (dx2, _mn(512, D))],
        outs=[(_sds((S, D), F32), _mn(512, D)), (_sds((S, D), CDT), _mn(512, D)), (_sds((1, D), F32), _row(D))],
        epilogue=_ep_rms_bwd, deps=[tok])

    gt = 512
    assert (in_w - 2 * D) % gt == 0
    off_a, off_b = (in_w - 2 * D) // gt, (in_w - D) // gt
    dpa, dpb, dga, dgb = _matmul(
        "d_gates", "nt", dx1c, w["w_out"], tm=1024, tn=gt, tk=D,
        extras=[(proj, _mn(1024, gt, off_a)), (proj, _mn(1024, gt, off_b)), (pa, _mn(1024, gt)),
                (pb, _mn(1024, gt))],
        outs=[(_sds((S, D), CDT), _mn(1024, gt))] * 4, epilogue=_ep_gates)
    (gw_out,) = _matmul("dw_out", "tn", y, dx1c, tm=D, tn=D, tk=1024,
                        outs=[(_sds((D, D), F32), _mn(D, D))], epilogue=_ep_store)
    (gw_pa,) = _matmul("dw_proj_a", "tn", o_a, dpa, tm=GW, tn=D, tk=1024,
                       outs=[(_sds((GW, D), F32), _mn(GW, D))], epilogue=_ep_store)
    (gw_pb,) = _matmul("dw_proj_b", "tn", o_r, dpb, tm=1024, tn=D, tk=2048,
                       outs=[(_sds((ret_v_w, D), F32), _mn(1024, D))], epilogue=_ep_store)
    (do_a,) = _matmul("d_o_a", "nt", dpa, w["w_proj_a"], tm=1024, tn=GW, tk=D,
                      outs=[(_sds((S, GW), F32), _mn(1024, GW))], epilogue=_ep_store)
    tok = on_grads({"w_out": gw_out, "w_proj_a": gw_pa, "w_proj_b": gw_pb})
    (d_or,) = _matmul("d_o_r", "nt", dpb, w["w_proj_b"], tm=1024, tn=ret_v_w, tk=D,
                      outs=[(_sds((S, ret_v_w), F32), _mn(1024, ret_v_w))], epilogue=_ep_store, deps=[tok])

    dproj, dgn_g, dgn_b = _ret_bwd(proj, gn_g, gn_b, o_pre, states, d_or, dga, dgb, dk, dv)
    do_gs, c_gs = _mix_bwd(S, os_, ls_, do_a)
    datt_parts = [_att_bwd(g, S, qkv[g], ls_[g], do_gs[g], c_gs[g]) for g in range(3)]
    dproj, dgqk = _qknorm_bwd(proj, gqk, [p[0] for p in datt_parts], [p[1] for p in datt_parts],
                              [p[2] for p in datt_parts], dproj)

    (gw_in,) = _matmul(
        "dw_in", "tn", xn, dproj, tm=512, tn=ns_in, tk=1024,
        outs=[(_sds((N_CHIPS, D, ns_in), F32), pl.BlockSpec((None, 512, ns_in), lambda i, j, k: (j, i, 0)))],
        epilogue=_ep_store)
    tok = on_grads({"w_in": gw_in})
    grad_x, dg1 = _d_x(dproj, w_in, x, g1, dx1, tok)

    smallg = {"norm1_g": dg1, "q_norm_g": dgqk[:, :ATT_W], "k_norm_g": dgqk[:, ATT_W:],
              "ret_gn_g": dgn_g, "ret_gn_b": dgn_b, "norm2_g": dg2}
    return loss, grad_x, smallg


N_CHIPS = 4
N_DEV = 8


def _place():
    x, y, c = lax.axis_index("x"), lax.axis_index("y"), lax.axis_index("c")
    return x, y, c


def _other_chips(x, y):
    out = []
    for fx, fy in ((1, 0), (0, 1), (1, 1)):
        px = 1 - x if fx else x
        py = 1 - y if fy else y
        out.append(((px, py), 2 * px + py))
    return out


SEM_SPEC = pl.BlockSpec(memory_space=pltpu.SEMAPHORE)
ANY_SPEC = pl.BlockSpec(memory_space=pl.ANY)
EFFECT = pltpu.SideEffectType.DATAFLOW_SIDE_EFFECTING


def _ici_copies(kind, srcs, lands, send, recv, which=(0, 1, 2)):
    x, y, c = _place()
    me = 2 * x + y
    out = []
    for w, (s, l) in enumerate(zip(srcs, lands)):
        for j, ((px, py), pidx) in enumerate(_other_chips(x, y)):
            if j not in which:
                continue
            if kind == "gather":
                half = s.shape[0] // 2
                rows = pl.ds(c * half, half)
                src, dst_there, dst_here = s.at[rows, :], l.at[me, rows, :], l.at[pidx, rows, :]
            else:
                src, dst_there, dst_here = s.at[pidx], l.at[me], l.at[pidx]
            out.append((src, dst_there, dst_here, send.at[3 * w + j], recv.at[3 * w + j], (px, py, c)))
    return out


def _exchange_start(name, kind, srcs, land_shapes, which=(0, 1, 2), lands=None):
    n = len(srcs)
    if lands is None:
        lands = [lax.empty(shape, dtype) for shape, dtype in land_shapes]

    def body(*refs):
        src_refs, land_refs = refs[:n], refs[n:2 * n]
        send, recv = refs[2 * n], refs[2 * n + 1]
        token = refs[-1]
        for src, dst, _, ss, rs, dev in _ici_copies(kind, src_refs, land_refs, send, recv, which):
            pltpu.make_async_remote_copy(src_ref=src, dst_ref=dst, send_sem=ss, recv_sem=rs, device_id=dev,
                                         device_id_type=MESH).start()
        token[...] = jnp.zeros_like(token)

    thru = [pltpu.HBM(s.shape, s.dtype) for s in srcs] + [pltpu.HBM(shape, dtype) for shape, dtype in land_shapes]
    res = pl.pallas_call(
        body, name=name,
        out_shape=(pltpu.SemaphoreType.DMA((3 * n,)), pltpu.SemaphoreType.DMA((3 * n,)), *thru, _sds((8, LANES), F32)),
        in_specs=[HBM_SPEC] * (2 * n), out_specs=(SEM_SPEC, SEM_SPEC, *[HBM_SPEC] * (2 * n), VMEM_SPEC),
        input_output_aliases={i: 2 + i for i in range(2 * n)},
        compiler_params=pltpu.CompilerParams(has_side_effects=EFFECT),
    )(*[pltpu.with_memory_space_constraint(s, pltpu.HBM) for s in srcs],
      *[pltpu.with_memory_space_constraint(l, pltpu.HBM) for l in lands])
    return res[0], res[1], list(res[2:2 + n]), list(res[2 + n:2 + 2 * n]), res[-1]


def _exchange_wait(name, kind, send, recv, srcs, lands, after, which=(0, 1, 2)):
    n = len(srcs)

    def body(*refs):
        src_refs, land_refs = refs[:n], refs[n:2 * n]
        send_ref, recv_ref = refs[2 * n], refs[2 * n + 1]
        for src, _, dst, ss, rs, dev in _ici_copies(kind, src_refs, land_refs, send_ref, recv_ref, which):
            cp = pltpu.make_async_remote_copy(src_ref=src, dst_ref=dst, send_sem=ss, recv_sem=rs, device_id=dev,
                                              device_id_type=MESH)
            cp.wait_send()
            cp.wait_recv()

    thru = [pltpu.HBM(t.shape, t.dtype) for t in list(srcs) + list(lands)]
    res = pl.pallas_call(
        body, name=name, out_shape=thru,
        in_specs=[HBM_SPEC] * (2 * n) + [SEM_SPEC, SEM_SPEC, ANY_SPEC], out_specs=[HBM_SPEC] * (2 * n),
        input_output_aliases={i: i for i in range(2 * n)},
        compiler_params=pltpu.CompilerParams(has_side_effects=EFFECT),
    )(*srcs, *lands, send, recv, after)
    return list(res[:n]), list(res[n:])


PAIR_TILE_ELEMS = 1 << 20


def _pair_fill(name, gathered, mine, core, others, chip, write_mine=True):
    k, r, C = gathered.shape
    half = r // 2
    tr = _row_tile(half, C, PAIR_TILE_ELEMS, mult=16)
    nt = half // tr
    n_far = others.shape[0]

    def body(c_ref, o_ref, chip_ref, in_ref, mine_ref, out_ref, slot, send, recv):
        j = pl.program_id(0)
        _sibling_barrier((j == 0) & (pl.program_id(1) == 0))
        b = (j * nt + pl.program_id(1)) % 2
        x, y, c = _place()
        cp = pltpu.make_async_remote_copy(src_ref=in_ref, dst_ref=slot.at[b], send_sem=send.at[b],
                                          recv_sem=recv.at[b], device_id=(x, y, 1 - c), device_id_type=MESH)

        @pl.when(j < n_far)
        def _():
            cp.start()
            cp.wait_recv()
            out_ref[...] = slot[b]
            cp.wait_send()

        @pl.when(j >= n_far)
        def _():
            out_ref[...] = mine_ref[...]

    def far(j):
        return jnp.minimum(j, n_far - 1)

    grid_spec = pltpu.PrefetchScalarGridSpec(
        num_scalar_prefetch=3, grid=(n_far + (2 if write_mine else 0), nt),
        in_specs=[pl.BlockSpec((tr, C), lambda j, i, c, o, m: (
                      (2 * o[far(j)] + c[0]) * nt + jnp.where(j < n_far, i, nt - 1), 0)),
                  pl.BlockSpec((tr, C), lambda j, i, c, o, m: (jnp.where(j < n_far, 0, (j - n_far) * nt + i), 0))],
        out_specs=pl.BlockSpec((tr, C), lambda j, i, c, o, m: (
            jnp.where(j < n_far, 2 * o[far(j)] + 1 - c[0], 2 * m[0] + j - n_far) * nt + i, 0)),
        scratch_shapes=[pltpu.VMEM((2, tr, C), gathered.dtype), pltpu.SemaphoreType.DMA((2,)),
                        pltpu.SemaphoreType.DMA((2,))])
    out = pl.pallas_call(body, name=name, grid_spec=grid_spec, out_shape=_sds((k * r, C), gathered.dtype),
                         input_output_aliases={3: 0}, compiler_params=_params(2, PAIR_FILL_ID))(
                             core, others, chip, gathered.reshape(k * r, C), mine)
    return out.reshape(k, r, C)


def _pair_reduce(name, g, core):
    k, R, C = g.shape
    half = R // 2
    tr = _row_tile(half, C, PAIR_TILE_ELEMS, mult=16)
    nt = half // tr

    def body(c_ref, mine_ref, give_ref, out_ref, wire_ref, stage, slot, send, recv):
        _sibling_barrier((pl.program_id(0) == 0) & (pl.program_id(1) == 0))
        b = (pl.program_id(0) * nt + pl.program_id(1)) % 2
        x, y, c = _place()
        stage[b] = give_ref[...].astype(stage.dtype)
        cp = pltpu.make_async_remote_copy(src_ref=stage.at[b], dst_ref=slot.at[b], send_sem=send.at[b],
                                          recv_sem=recv.at[b], device_id=(x, y, 1 - c), device_id_type=MESH)
        cp.start()
        cp.wait_recv()
        tot = mine_ref[...] + slot[b].astype(F32)
        out_ref[...] = tot
        wire_ref[...] = tot.astype(wire_ref.dtype)
        cp.wait_send()

    blk = (tr, C)
    out_spec = pl.BlockSpec(blk, lambda s, i, c: (s * nt + i, 0))
    grid_spec = pltpu.PrefetchScalarGridSpec(
        num_scalar_prefetch=1, grid=(k, nt),
        in_specs=[pl.BlockSpec(blk, lambda s, i, c: ((2 * s + c[0]) * nt + i, 0)),
                  pl.BlockSpec(blk, lambda s, i, c: ((2 * s + 1 - c[0]) * nt + i, 0))],
        out_specs=[out_spec, out_spec],
        scratch_shapes=[pltpu.VMEM((2, tr, C), CDT), pltpu.VMEM((2, tr, C), CDT), pltpu.SemaphoreType.DMA((2,)),
                        pltpu.SemaphoreType.DMA((2,))])
    g2 = g.reshape(k * R, C)
    out, wire = pl.pallas_call(body, name=name, grid_spec=grid_spec,
                               out_shape=[_sds((k * half, C), F32), _sds((k * half, C), CDT)],
                               compiler_params=_params(2, PAIR_REDUCE_ID))(core, g2, g2)
    return out, wire.reshape(k, half, C)


def _all_reduce_small(v):
    r, cdim = v.shape

    def body(v_ref, o_ref, buf, send, recv):
        x, y, c = _place()
        me = 4 * x + 2 * y + c
        buf[me] = v_ref[...]
        sends = []
        for m in range(1, N_DEV):
            px = 1 - x if m & 4 else x
            py = 1 - y if m & 2 else y
            pc = 1 - c if m & 1 else c
            cp = pltpu.make_async_remote_copy(src_ref=v_ref, dst_ref=buf.at[me], send_sem=send.at[m - 1],
                                              recv_sem=recv.at[m - 1], device_id=(px, py, pc), device_id_type=MESH)
            cp.start()
            sends.append((cp, 4 * px + 2 * py + pc))
        for m, (cp, pidx) in enumerate(sends):
            pltpu.make_async_remote_copy(src_ref=v_ref, dst_ref=buf.at[pidx], send_sem=send.at[m], recv_sem=recv.at[m],
                                         device_id=(x, y, c), device_id_type=MESH).wait_recv()
        for cp, _ in sends:
            cp.wait_send()
        tot = buf[0]
        for k in range(1, N_DEV):
            tot = tot + buf[k]
        o_ref[...] = tot

    return pl.pallas_call(
        body, name="all_reduce_small", in_specs=[VMEM_SPEC], out_specs=VMEM_SPEC,
        out_shape=_sds((r, cdim), F32),
        scratch_shapes=[pltpu.VMEM((N_DEV, r, cdim), F32), pltpu.SemaphoreType.DMA((N_DEV - 1,)),
                        pltpu.SemaphoreType.DMA((N_DEV - 1,))],
    )(v)


def _row_tile(rows, cols, budget_elems=1 << 18, mult=8):
    if rows % mult:
        return rows
    t = max(mult, (budget_elems // cols) // mult * mult)
    while rows % t:
        t -= mult
    return t


def _adamw_update(w, g, m, v):
    nm = ADAM_B1 * m + (1.0 - ADAM_B1) * g
    nv = ADAM_B2 * v + (1.0 - ADAM_B2) * (g * g)
    m_hat = nm / (1.0 - ADAM_B1 ** ADAM_STEP)
    v_hat = nv / (1.0 - ADAM_B2 ** ADAM_STEP)
    return -ADAM_LR * (m_hat / (jnp.sqrt(v_hat) + ADAM_EPS) + ADAM_WD * w), nm, nv


def _adamw(name, w, g, m, v):
    R, C = w.shape
    tr = _row_tile(R, C, 1 << 19)

    def body(w_ref, g_ref, m_ref, v_ref, d_ref, nm_ref, nv_ref):
        d_ref[...], nm_ref[...], nv_ref[...] = _adamw_update(w_ref[...], g_ref[...], m_ref[...], v_ref[...])

    spec = pl.BlockSpec((tr, C), lambda i: (i, 0))
    return pl.pallas_call(body, name=name, grid=(R // tr,), in_specs=[spec] * 4, out_specs=[spec] * 3,
                          out_shape=[_sds((R, C), F32)] * 3, compiler_params=_params(1))(w, g, m, v)


def _sum_share(name, own, by_chip, chip, others, core):
    k, half, C = by_chip.shape
    tr = _row_tile(half, C, PAIR_TILE_ELEMS // 2, mult=16)
    nt = half // tr

    def body(chip_ref, oth_ref, c_ref, own_ref, a_ref, b_ref, cc_ref, g_out, mine, slot, send, recv):
        p = pl.program_id(1)
        _sibling_barrier((pl.program_id(0) == 0) & (p == 0))
        b = pl.program_id(0) % 2
        x, y, c = _place()
        cp = pltpu.make_async_remote_copy(src_ref=mine.at[b], dst_ref=slot.at[b], send_sem=send.at[b],
                                          recv_sem=recv.at[b], device_id=(x, y, 1 - c), device_id_type=MESH)

        @pl.when(p == 0)
        def _():
            tot = ((own_ref[...] + a_ref[...].astype(F32)) + b_ref[...].astype(F32)) + cc_ref[...].astype(F32)
            mine[b] = tot
            cp.start()
            g_out[...] = tot

        @pl.when(p == 1)
        def _():
            cp.wait_recv()
            g_out[...] = slot[b]
            cp.wait_send()

    def piece(j):
        return pl.BlockSpec((tr, C), lambda i, p, chip, oth, c: (oth[j] * nt + i, 0))

    grid_spec = pltpu.PrefetchScalarGridSpec(
        num_scalar_prefetch=3, grid=(nt, 2),
        in_specs=[pl.BlockSpec((tr, C), lambda i, p, chip, oth, c: (chip[0] * nt + i, 0)),
                  piece(0), piece(1), piece(2)],
        out_specs=pl.BlockSpec((tr, C), lambda i, p, chip, oth, c: (
            jnp.where(p == 0, c[0], 1 - c[0]) * nt + i, 0)),
        scratch_shapes=[pltpu.VMEM((2, tr, C), F32), pltpu.VMEM((2, tr, C), F32), pltpu.SemaphoreType.DMA((2,)),
                        pltpu.SemaphoreType.DMA((2,))])
    by2 = by_chip.reshape(k * half, C)
    return pl.pallas_call(body, name=name, grid_spec=grid_spec, out_shape=_sds((2 * half, C), F32),
                          compiler_params=_params(2, SUM_SHARE_ID))(chip, others, core, own, by2, by2, by2)


BIG = ("w_in", "w_proj_a", "w_proj_b", "w_out", "w_up", "w_down")
COL_SHARDED = ("w_in", "w_proj_a", "w_up")
SMALL = ("norm1_g", "q_norm_g", "k_norm_g", "ret_gn_g", "ret_gn_b", "norm2_g")
ALL_W = ("norm1_g", "w_in", "q_norm_g", "k_norm_g", "ret_gn_g", "ret_gn_b", "w_proj_a", "w_proj_b", "w_out",
         "norm2_g", "w_up", "w_down")
LANES = 128


def _to_full(name, gathered):
    k, r, c = gathered.shape
    if name in COL_SHARDED:
        return gathered.transpose(1, 0, 2).reshape(r, k * c)
    return gathered.reshape(k * r, c)


def _to_shard_major(name, full):
    if name in COL_SHARDED:
        r, c4 = full.shape
        return full.reshape(r, N_CHIPS, c4 // N_CHIPS).transpose(1, 0, 2)
    r4, c = full.shape
    return full.reshape(N_CHIPS, r4 // N_CHIPS, c)


def kernel(x, norm1_g, w_in, q_norm_g, k_norm_g, ret_gn_g, ret_gn_b, w_proj_a, w_proj_b, w_out, norm2_g, w_up, w_down, loss_target, m_norm1_g, m_w_in, m_q_norm_g, m_k_norm_g, m_ret_gn_g, m_ret_gn_b, m_w_proj_a, m_w_proj_b, m_w_out, m_norm2_g, m_w_up, m_w_down, v_norm1_g, v_w_in, v_q_norm_g, v_k_norm_g, v_ret_gn_g, v_ret_gn_b, v_w_proj_a, v_w_proj_b, v_w_out, v_norm2_g, v_w_up, v_w_down):
    weights = dict(norm1_g=norm1_g, w_in=w_in, q_norm_g=q_norm_g, k_norm_g=k_norm_g, ret_gn_g=ret_gn_g,
                   ret_gn_b=ret_gn_b, w_proj_a=w_proj_a, w_proj_b=w_proj_b, w_out=w_out, norm2_g=norm2_g,
                   w_up=w_up, w_down=w_down)
    moments_m = dict(norm1_g=m_norm1_g, w_in=m_w_in, q_norm_g=m_q_norm_g, k_norm_g=m_k_norm_g, ret_gn_g=m_ret_gn_g,
                     ret_gn_b=m_ret_gn_b, w_proj_a=m_w_proj_a, w_proj_b=m_w_proj_b, w_out=m_w_out,
                     norm2_g=m_norm2_g, w_up=m_w_up, w_down=m_w_down)
    moments_v = dict(norm1_g=v_norm1_g, w_in=v_w_in, q_norm_g=v_q_norm_g, k_norm_g=v_k_norm_g, ret_gn_g=v_ret_gn_g,
                     ret_gn_b=v_ret_gn_b, w_proj_a=v_w_proj_a, w_proj_b=v_w_proj_b, w_out=v_w_out,
                     norm2_g=v_norm2_g, w_up=v_w_up, w_down=v_w_down)

    mx, my = lax.axis_index("x"), lax.axis_index("y")
    core = lax.axis_index("c").astype(jnp.int32).reshape(1)
    chip = (2 * mx + my).astype(jnp.int32).reshape(1)
    others = jnp.stack([2 * (1 - mx) + my, 2 * mx + 1 - my, 2 * (1 - mx) + 1 - my]).astype(jnp.int32)
    shards = {n: weights[n][0].astype(CDT) for n in BIG}
    def start_gather(name, names):
        return _exchange_start(name, "gather", [shards[n] for n in names],
                               [((N_CHIPS,) + shards[n].shape, CDT) for n in names])

    w_in_shape = [((N_CHIPS,) + shards["w_in"].shape, CDT)]
    n_send, n_recv, n_srcs, n_lands, n_token = _exchange_start(
        "gather_w_in_near_start", "gather", [shards["w_in"]], w_in_shape, which=(0, 1))
    late = [n for n in BIG if n != "w_in"]
    flight = {}

    def near_w_in(after):
        srcs, lands = _exchange_wait("gather_w_in_near_wait", "gather", n_send, n_recv, n_srcs, n_lands, after,
                                     which=(0, 1))
        d_send, d_recv, d_srcs, d_lands, _ = _exchange_start(
            "gather_w_in_diag_start", "gather", srcs, w_in_shape, which=(2,), lands=lands)
        flight["late"] = start_gather("gather_late_start", late)
        w_near = _pair_fill("pair_fill_w_in_near", d_lands[0], d_srcs[0], core, others[:2], chip)
        flight["diag"] = (d_send, d_recv, d_srcs, [w_near])
        return w_near, flight["late"][-1]

    def far_w_in(after):
        d_send, d_recv, d_srcs, d_lands = flight["diag"]
        srcs, lands = _exchange_wait("gather_w_in_diag_wait", "gather", d_send, d_recv, d_srcs, d_lands, after,
                                     which=(2,))
        return _pair_fill("pair_fill_w_in_diag", lands[0], srcs[0], core, others[2:], chip, write_mine=False), None

    def late_weights(after):
        l_send, l_recv, l_srcs, l_lands, _ = flight["late"]
        srcs, lands = _exchange_wait("gather_late_wait", "gather", l_send, l_recv, l_srcs, l_lands, after)
        out = {}
        for n, mine, land in zip(late, srcs, lands):
            out[n] = _to_full(n, _pair_fill("pair_fill_%s" % n, land, mine, core, others, chip))
        return out

    pending = []

    def on_grads(group):
        names = list(group)
        red = [_pair_reduce("pair_reduce_%s" % n, g if g.ndim == 3 else _to_shard_major(n, g), core)
               for n, g in group.items()]
        wires = [wire for _, wire in red]
        send, recv, srcs, lands, token = _exchange_start(
            "scatter_start_%s" % names[0], "scatter", wires, [(wire.shape, wire.dtype) for wire in wires])
        pending.append((names, [own for own, _ in red], send, recv, srcs, lands))
        return token

    small = {n: weights[n].reshape(1, -1) for n in SMALL}

    loss, grad_x, small_g = _local_step(x[0], loss_target[0], n_srcs[0], chip, others, near_w_in, far_w_in, small,
                                        late_weights, on_grads, deps0=[n_token])

    out_g, out_d, out_m, out_v = {}, {}, {}, {}
    for names, owns, send, recv, srcs, lands in pending:
        _, got = _exchange_wait("scatter_wait_%s" % names[0], "scatter", send, recv, srcs, lands, grad_x)
        for n, own, by_chip in zip(names, owns, got):
            shape = weights[n].shape
            g2 = _sum_share("sum_share_%s" % n, own, by_chip, chip, others, core)
            d, nm, nv = _adamw("adamw_%s" % n, weights[n][0], g2, moments_m[n][0], moments_v[n][0])
            out_g[n], out_d[n], out_m[n], out_v[n] = (t.reshape(shape) for t in (g2, d, nm, nv))

    packed = jnp.concatenate([small_g[n].reshape(1, -1) for n in SMALL], axis=1).reshape(-1, LANES)
    loss_tile = jnp.zeros((8, LANES), F32).at[0, 0].set(loss)
    red = _all_reduce_small(jnp.concatenate([packed, loss_tile], axis=0))
    loss = red[packed.shape[0], 0]
    red = red[:packed.shape[0]].reshape(1, -1)
    off = 0
    for n in SMALL:
        shape = weights[n].shape
        row = (1, weights[n].size)
        g2 = red[:, off:off + row[1]]
        off += row[1]
        d, nm, nv = _adamw("adamw_%s" % n, weights[n].reshape(row), g2, moments_m[n].reshape(row),
                           moments_v[n].reshape(row))
        out_g[n], out_d[n], out_m[n], out_v[n] = (t.reshape(shape) for t in (g2, d, nm, nv))

    return (loss, grad_x[None], *[out_g[n] for n in ALL_W], *[out_d[n] for n in ALL_W],
            *[out_m[n] for n in ALL_W], *[out_v[n] for n in ALL_W])
```

```python
import functools

import jax
import jax.numpy as jnp
from jax import lax
from jax.experimental import pallas as pl
from jax.experimental.pallas import tpu as pltpu

CDT = jnp.bfloat16
F32 = jnp.float32
EPS = 1e-6

ATT_GROUPS = ((128, 1), (512, 4), (2048, 16))
ATT_HPG = 4
ATT_HEADS = 12
HD = 128
BLK = 128
ATT_W = ATT_HEADS * HD
GW = ATT_HPG * HD
RET_HEADS = 4

ADAM_LR = 0.001
ADAM_B1 = 0.9
ADAM_B2 = 0.999
ADAM_EPS = 1e-08
ADAM_WD = 0.01
ADAM_STEP = 10

VMEM_LIMIT_BYTES = 56 * 1024 * 1024
MESH = pl.DeviceIdType.MESH
HBM_SPEC = pl.BlockSpec(memory_space=pltpu.HBM)
VMEM_SPEC = pl.BlockSpec(memory_space=pltpu.VMEM)


def _params(n_axes, collective_id=None):
    return pltpu.CompilerParams(dimension_semantics=("arbitrary",) * n_axes,
                                vmem_limit_bytes=VMEM_LIMIT_BYTES, collective_id=collective_id)


PAIR_FILL_ID, PAIR_REDUCE_ID, SUM_SHARE_ID = 1, 2, 3


def _sibling_barrier(first_step):
    @pl.when(first_step)
    def _():
        sem = pltpu.get_barrier_semaphore()
        x, y, c = lax.axis_index("x"), lax.axis_index("y"), lax.axis_index("c")
        pl.semaphore_signal(sem, inc=1, device_id=(x, y, 1 - c), device_id_type=pl.DeviceIdType.MESH)
        pl.semaphore_wait(sem, 1)


def _dot_nn(a, b):
    return jnp.dot(a, b, preferred_element_type=F32)


def _dot_nt(a, b):
    return lax.dot_general(a, b, (((1,), (1,)), ((), ())), preferred_element_type=F32)


def _dot_tn(a, b):
    return lax.dot_general(a, b, (((0,), (0,)), ((), ())), preferred_element_type=F32)


def _sigmoid(v):
    return 1.0 / (1.0 + jnp.exp(-v))


def _matmul(name, mode, a, b, *, tm, tn, tk, extras=(), outs, epilogue, deps=(), b_spec=None, n_cols=None,
            prefetch=(), alias_dep_to_out=None, j_outer=False):
    deps = [d for d in deps if d is not None]
    if mode == "tn":
        K, M = a.shape
    else:
        M, K = a.shape
    if b_spec is None:
        (N, K2) = b.shape if mode == "nt" else b.shape[::-1]
        assert K == K2, (name, a.shape, b.shape)
        if mode == "nt":
            b_spec = pl.BlockSpec((tn, tk), lambda i, j, k, *p: (j, k))
        else:
            b_spec = pl.BlockSpec((tk, tn), lambda i, j, k, *p: (k, j))
    else:
        N = n_cols
    assert M % tm == 0 and N % tn == 0 and K % tk == 0, (name, a.shape, b.shape)
    ni, nj, nk = M // tm, N // tn, K // tk
    if mode == "tn":
        a_spec = pl.BlockSpec((tk, tm), lambda i, j, k, *p: (k, i))
    else:
        a_spec = pl.BlockSpec((tm, tk), lambda i, j, k, *p: (i, k))
    dot = {"nn": _dot_nn, "nt": _dot_nt, "tn": _dot_tn}[mode]
    n_ex, n_out, n_dep, n_pre = len(extras), len(outs), len(deps), len(prefetch)
    grid = (ni, nj, nk)
    if j_outer:
        grid = (nj, ni, nk)

        def swapped(spec):
            return pl.BlockSpec(spec.block_shape, lambda j, i, k, *p: spec.index_map(i, j, k, *p))

        a_spec, b_spec = swapped(a_spec), swapped(b_spec)
        extras = [(e, swapped(s)) for e, s in extras]
        outs = [(o, swapped(s)) for o, s in outs]

    def body(*refs):
        refs = refs[n_pre:]
        a_ref, b_ref = refs[0], refs[1]
        ex = refs[2:2 + n_ex]
        out = refs[2 + n_ex + n_dep:2 + n_ex + n_dep + n_out]
        acc = refs[-1] if nk > 1 else None
        i = pl.program_id(1 if j_outer else 0)
        k = pl.program_id(2)
        if nk == 1:
            epilogue(dot(a_ref[...].astype(CDT), b_ref[...].astype(CDT)), ex, out, i)
            return

        @pl.when(k == 0)
        def _():
            acc[...] = jnp.zeros_like(acc)

        acc[...] += dot(a_ref[...].astype(CDT), b_ref[...].astype(CDT))

        @pl.when(k == nk - 1)
        def _():
            epilogue(acc[...], ex, out, i)

    grid_spec = pltpu.PrefetchScalarGridSpec(
        num_scalar_prefetch=n_pre, grid=grid,
        in_specs=[a_spec, b_spec] + [s for _, s in extras] + [pl.BlockSpec(memory_space=pl.ANY)] * n_dep,
        out_specs=[s for _, s in outs],
        scratch_shapes=[pltpu.VMEM((tm, tn), F32)] if nk > 1 else [])
    aliases = {}
    if alias_dep_to_out is not None:
        aliases = {n_pre + 2 + n_ex + alias_dep_to_out[0]: alias_dep_to_out[1]}
    res = pl.pallas_call(
        body, name=name, grid_spec=grid_spec, out_shape=[o for o, _ in outs], input_output_aliases=aliases,
        compiler_params=_params(3),
    )(*prefetch, a, b, *[e for e, _ in extras], *deps)
    return res


def _mn(tm, tn, col_off=0):
    return pl.BlockSpec((tm, tn), lambda i, j, k, *p: (i, j + col_off))


def _row(tn):
    return pl.BlockSpec((1, tn), lambda i, j, k, *p: (0, j))


def _ep_store(acc, ex, out, i):
    out[0][...] = acc.astype(out[0].dtype)


def _ep_resid_norm(acc, ex, out, i):
    x1 = ex[0][...] + acc
    out[0][...] = x1
    rstd = lax.rsqrt(jnp.mean(x1 * x1, axis=-1, keepdims=True) + EPS)
    out[1][...] = (x1 * rstd * ex[1][...]).astype(out[1].dtype)


def _ep_up(acc, ex, out, i):
    out[0][...] = acc.astype(out[0].dtype)
    r = jnp.maximum(acc, 0.0)
    out[1][...] = (r * r).astype(out[1].dtype)


def _ep_down_loss(acc, ex, out, i, inv_d):
    diff = (ex[0][...] + acc) - ex[1][...]
    dx2 = diff * inv_d
    out[0][...] = dx2
    out[1][...] = dx2.astype(out[1].dtype)

    @pl.when(i == 0)
    def _():
        out[2][...] = jnp.zeros_like(out[2])

    out[2][...] += jnp.sum(diff * diff, axis=0, keepdims=True)


def _ep_dh(acc, ex, out, i):
    h = ex[0][...].astype(F32)
    out[0][...] = (acc * (2.0 * jnp.maximum(h, 0.0))).astype(out[0].dtype)


def _ep_rms_bwd(acc, ex, out, i):
    x = ex[0][...]
    g = ex[1][...]
    rstd = lax.rsqrt(jnp.mean(x * x, axis=-1, keepdims=True) + EPS)
    xh = x * rstd
    dxh = acc * g
    dx = ex[2][...] + rstd * (dxh - xh * jnp.mean(dxh * xh, axis=-1, keepdims=True))
    out[0][...] = dx
    for copy in out[1:-1]:
        copy[...] = dx.astype(copy.dtype)
    dg = out[-1]

    @pl.when(i == 0)
    def _():
        dg[...] = jnp.zeros_like(dg)

    dg[...] += jnp.sum(acc * xh, axis=0, keepdims=True)


def _ep_gates(acc, ex, out, i):
    sa = _sigmoid(ex[0][...].astype(F32))
    sb = _sigmoid(ex[1][...].astype(F32))
    dpa = acc * sa
    dpb = acc * sb
    out[0][...] = dpa.astype(out[0].dtype)
    out[1][...] = dpb.astype(out[1].dtype)
    out[2][...] = (dpa * ex[2][...].astype(F32) * (1.0 - sa)).astype(out[2].dtype)
    out[3][...] = (dpb * ex[3][...].astype(F32) * (1.0 - sb)).astype(out[3].dtype)


def _sds(shape, dtype):
    return jax.ShapeDtypeStruct(shape, dtype)


def _in_proj_mine(x, g, w_mine, chip, proj_sds, deps, tm=512):
    S, D = x.shape
    ns = w_mine.shape[1]
    deps = [d for d in deps if d is not None]

    def body(c_ref, x_ref, g_ref, w_ref, *rest):
        xn_ref, proj_ref = rest[len(deps)], rest[len(deps) + 1]
        xv = x_ref[...]
        rstd = lax.rsqrt(jnp.mean(xv * xv, axis=-1, keepdims=True) + EPS)
        xn = (xv * rstd * g_ref[...]).astype(xn_ref.dtype)
        xn_ref[...] = xn
        proj_ref[...] = _dot_nn(xn, w_ref[...]).astype(proj_ref.dtype)

    grid_spec = pltpu.PrefetchScalarGridSpec(
        num_scalar_prefetch=1, grid=(S // tm,),
        in_specs=[pl.BlockSpec((tm, D), lambda i, c: (i, 0)), pl.BlockSpec((1, D), lambda i, c: (0, 0)),
                  pl.BlockSpec((D, ns), lambda i, c: (0, 0))] + [pl.BlockSpec(memory_space=pl.ANY)] * len(deps),
        out_specs=[pl.BlockSpec((tm, D), lambda i, c: (i, 0)), pl.BlockSpec((tm, ns), lambda i, c: (i, c[0]))])
    return pl.pallas_call(body, name="in_proj_mine", grid_spec=grid_spec, out_shape=[_sds((S, D), CDT), proj_sds],
                          compiler_params=_params(1))(chip, x, g, w_mine, *deps)


def _rm_shape(S, d, width):
    return (S, width) if d == 1 else (d, S // d, width)


def _rm_spec(tm, d, width):
    if d == 1:
        return pl.BlockSpec((tm, width), lambda i: (i, 0))
    return pl.BlockSpec((d, tm // d, width), lambda i: (0, i, 0))


def _rm_put(dst_ref, cols, buf_ref, d):
    if d == 1:
        dst_ref[:, cols] = buf_ref[...].astype(dst_ref.dtype)
        return
    m = buf_ref.shape[0] // d
    for r in range(d):
        dst_ref[r, :, cols] = buf_ref[pl.ds(r, m, stride=d), :].astype(dst_ref.dtype)


def _rm_reader(buf_ref, src_ref, d):
    if d == 1:
        return lambda s, rows: src_ref[rows, s * HD:(s + 1) * HD].astype(F32)
    m = buf_ref.shape[1] // d
    for s in range(buf_ref.shape[0]):
        for r in range(d):
            buf_ref.at[s][pl.ds(r, m, stride=d), :] = src_ref[r, :, s * HD:(s + 1) * HD].astype(F32)
    return lambda s, rows: buf_ref.at[s][rows, :]


def _qknorm_fwd(proj, gqk, tm=512):
    S = proj.shape[0]
    W = 2 * ATT_W
    dil = [d for _, d in ATT_GROUPS]

    def body(p_ref, g_ref, o0, o1, o2, buf):
        outs = (o0, o1, o2)
        for hd in range(3 * ATT_HEADS):
            which, head = hd // ATT_HEADS, hd % ATT_HEADS
            grp, slot = head // ATT_HPG, head % ATT_HPG
            cols = slice(hd * HD, (hd + 1) * HD)

            def chunk(rows, which=which, cols=cols):
                v = p_ref[rows, cols].astype(F32)
                if which < 2:
                    rstd = lax.rsqrt(jnp.mean(v * v, axis=-1, keepdims=True) + EPS)
                    v = v * rstd * g_ref[:, cols]
                buf[rows, :] = v

            chunk(slice(None))
            _rm_put(outs[grp], slice(which * GW + slot * HD, which * GW + (slot + 1) * HD), buf, dil[grp])

    return pl.pallas_call(
        body, name="qknorm_fwd", grid=(S // tm,),
        in_specs=[pl.BlockSpec((tm, 3 * ATT_W), lambda i: (i, 0)), pl.BlockSpec((1, W), lambda i: (0, 0))],
        out_specs=[_rm_spec(tm, d, 3 * GW) for d in dil],
        out_shape=[_sds(_rm_shape(S, d, 3 * GW), CDT) for d in dil],
        scratch_shapes=[pltpu.VMEM((tm, HD), F32)],
        compiler_params=_params(1))(proj, gqk)


def _qknorm_bwd(proj, gqk, dqs, dks, dvs, dproj, tm=256):
    S = proj.shape[0]
    W = 2 * ATT_W
    dil = [d for _, d in ATT_GROUPS]

    def body(p_ref, g_ref, *refs):
        ins = refs[0:9]
        o_ref, dg_ref = refs[10], refs[11]
        bufs = refs[12:21]
        i = pl.program_id(0)

        @pl.when(i == 0)
        def _():
            dg_ref[...] = jnp.zeros_like(dg_ref)

        nat = [_rm_reader(bufs[j], ins[j], dil[j % 3]) for j in range(9)]
        dq_get, dk_get, dv_get = nat[0:3], nat[3:6], nat[6:9]
        for hd in range(2 * ATT_HEADS):
            sl = slice(hd * HD, (hd + 1) * HD)
            head = hd % ATT_HEADS
            grp, slot = head // ATT_HPG, head % ATT_HPG
            get = (dq_get if hd < ATT_HEADS else dk_get)[grp]

            def chunk(rows, sl=sl, slot=slot, get=get):
                dn = get(slot, rows)
                v = p_ref[rows, sl].astype(F32)
                rstd = lax.rsqrt(jnp.mean(v * v, axis=-1, keepdims=True) + EPS)
                vh = v * rstd
                dg_ref[:, sl] += jnp.sum(dn * vh, axis=0, keepdims=True)
                dvh = dn * g_ref[:, sl]
                o_ref[rows, sl] = (rstd * (dvh - vh * jnp.mean(dvh * vh, axis=-1, keepdims=True))).astype(o_ref.dtype)

            chunk(slice(None))
        for head in range(ATT_HEADS):
            grp, slot = head // ATT_HPG, head % ATT_HPG
            o_ref[:, W + head * HD:W + (head + 1) * HD] = dv_get[grp](slot, slice(None)).astype(o_ref.dtype)

    return pl.pallas_call(
        body, name="qknorm_bwd", grid=(S // tm,),
        in_specs=[pl.BlockSpec((tm, W), lambda i: (i, 0)), pl.BlockSpec((1, W), lambda i: (0, 0))]
        + [_rm_spec(tm, d, GW) for d in dil] * 3 + [pl.BlockSpec(memory_space=pl.ANY)],
        out_specs=[pl.BlockSpec((tm, 3 * ATT_W), lambda i: (i, 0)), pl.BlockSpec((1, W), lambda i: (0, 0))],
        out_shape=[_sds(dproj.shape, dproj.dtype), _sds((1, W), F32)],
        scratch_shapes=[pltpu.VMEM((ATT_HPG, tm, HD), F32)] * 9,
        input_output_aliases={11: 0},
        compiler_params=_params(1))(proj, gqk, *dqs, *dks, *dvs, dproj)


def _att_mask(n):
    qi = lax.broadcasted_iota(jnp.int32, (BLK, 2 * BLK), 0)
    kj = lax.broadcasted_iota(jnp.int32, (BLK, 2 * BLK), 1)
    dist = BLK + qi - kj
    valid_all = (dist >= 0) & (dist <= BLK)
    return valid_all & ((kj >= BLK) | (n > 0)), valid_all, dist.astype(F32)


def _att_slopes(grp):
    return [2.0 ** (-8.0 * (grp * ATT_HPG + hh + 1) / ATT_HEADS) for hh in range(ATT_HPG)]


ATT_WIDTH = 8
ATT_BWD_SEGMENTS = 4


def _att_planes(d, width):
    return d if d > 1 else width


def _att_step_planes(d, width):
    return min(_att_planes(d, width), width)


def _att_3d(a, d, width):
    return a.reshape(width, a.shape[0] // width, a.shape[1]) if d == 1 else a


def _att_spec(R, row_fn, col=0):
    return pl.BlockSpec((R, BLK, GW), lambda r, n: (r, row_fn(n), col))


def _att_chains(R):
    return [(rr * ATT_HPG + hh, rr, slice(hh * HD, (hh + 1) * HD), hh) for rr in range(R) for hh in range(ATT_HPG)]


def _att_qkv_specs(R, nb):
    last = nb - 1

    def cur(n):
        return jnp.minimum(n, last)

    def prev(n):
        return jnp.maximum(jnp.minimum(n, last) - 1, 0)

    return [_att_spec(R, cur, 0), _att_spec(R, prev, 1), _att_spec(R, cur, 1), _att_spec(R, prev, 2),
            _att_spec(R, cur, 2), _att_spec(R, lambda n: last, 1), _att_spec(R, lambda n: last, 2)]


def _att_prev(seg, n, prev_ref, last_ref, rr, sl):
    t = prev_ref[rr, :, sl]
    if seg and rr > 0:
        t = jnp.where(n == 0, last_ref[rr - 1, :, sl], t)
    return t


def _att_fwd(grp, S, qkv):
    _, d = ATT_GROUPS[grp]
    seg = d == 1
    width = ATT_WIDTH
    P = _att_planes(d, width)
    L = S // P
    nb = L // BLK
    R = _att_step_planes(d, width)
    assert P % R == 0 and (not seg or P == R)
    slopes = _att_slopes(grp)
    scale = HD ** -0.5
    chains = _att_chains(R)

    def body(q_ref, kp_ref, kc_ref, vp_ref, vc_ref, kl_ref, vl_ref, o_ref, l_ref, s_buf, p_buf, den_buf):
        n = pl.program_id(1)
        valid, valid_all, distf = _att_mask(n)
        for c, rr, sl, hh in chains:
            k = jnp.concatenate([_att_prev(seg, n, kp_ref, kl_ref, rr, sl), kc_ref[rr, :, sl]], axis=0)
            s_buf[c] = _dot_nt(q_ref[rr, :, sl], k)
        for c, rr, sl, hh in chains:
            s = s_buf[c] * scale + (-slopes[hh] * d) * distf
            s = jnp.where(valid_all if seg and rr > 0 else valid, s, -1e30)
            m = jnp.max(s, axis=-1, keepdims=True)
            p = jnp.exp(s - m)
            den = jnp.sum(p, axis=-1, keepdims=True)
            p_buf[c] = p.astype(CDT)
            den_buf[c] = jnp.broadcast_to(den, (BLK, HD))
            l_ref[rr, :, sl] = jnp.broadcast_to(m + jnp.log(den), (BLK, HD))
        for c, rr, sl, hh in chains:
            v = jnp.concatenate([_att_prev(seg, n, vp_ref, vl_ref, rr, sl), vc_ref[rr, :, sl]], axis=0)
            o_ref[rr, :, sl] = _dot_nn(p_buf[c], v) / den_buf[c]

    out_spec = _att_spec(R, lambda n: n)
    n_ch = len(chains)
    q3 = _att_3d(qkv, d, width)
    o, l = pl.pallas_call(
        body, name="att_fwd_g%d" % grp, grid=(P // R, nb),
        in_specs=_att_qkv_specs(R, nb),
        out_specs=[out_spec, out_spec],
        out_shape=[_sds((P, L, GW), F32)] * 2,
        scratch_shapes=[pltpu.VMEM((n_ch, BLK, 2 * BLK), F32), pltpu.VMEM((n_ch, BLK, 2 * BLK), CDT),
                        pltpu.VMEM((n_ch, BLK, HD), F32)],
        compiler_params=_params(2),
    )(*[q3] * 7)
    return o.reshape(_rm_shape(S, d, GW)), l.reshape(_rm_shape(S, d, GW))


def _att_bwd(grp, S, qkv, lse, do_g, c_g):
    _, d = ATT_GROUPS[grp]
    seg = d == 1
    width = ATT_BWD_SEGMENTS if seg else ATT_WIDTH
    P = _att_planes(d, width)
    L = S // P
    nb = L // BLK
    R = _att_step_planes(d, width)
    assert P % R == 0 and (not seg or P == R)
    slopes = _att_slopes(grp)
    scale = HD ** -0.5
    last = nb - 1
    chains = _att_chains(R)

    def body(q_ref, kp_ref, kc_ref, vp_ref, vc_ref, kl_ref, vl_ref, l_ref, do_ref, c_ref, dq_ref, dk_ref, dv_ref,
             ck, cv, fk, fv, s_buf, dp_buf, p_buf, ds_buf):
        n = pl.program_id(1)

        @pl.when(n == 0)
        def _():
            for buf in (ck, cv, fk, fv):
                buf[...] = jnp.zeros_like(buf)

        @pl.when(n < nb)
        def _():
            valid, valid_all, distf = _att_mask(n)
            for c, rr, sl, hh in chains:
                k = jnp.concatenate([_att_prev(seg, n, kp_ref, kl_ref, rr, sl), kc_ref[rr, :, sl]], axis=0)
                v = jnp.concatenate([_att_prev(seg, n, vp_ref, vl_ref, rr, sl), vc_ref[rr, :, sl]], axis=0)
                s_buf[c] = _dot_nt(q_ref[rr, :, sl], k)
                dp_buf[c] = _dot_nt(do_ref[rr, :, sl], v)
            for c, rr, sl, hh in chains:
                s = s_buf[c] * scale + (-slopes[hh] * d) * distf
                p = jnp.where(valid_all if seg and rr > 0 else valid, jnp.exp(s - l_ref[rr, :, sl][:, 0:1]), 0.0)
                p_buf[c] = p.astype(CDT)
                ds_buf[c] = (p * (dp_buf[c] + c_ref[rr, :, sl][:, 0:1]) * scale).astype(CDT)
            for c, rr, sl, hh in chains:
                k = jnp.concatenate([_att_prev(seg, n, kp_ref, kl_ref, rr, sl), kc_ref[rr, :, sl]], axis=0)
                ds = ds_buf[c]
                dq_ref[rr, :, sl] = _dot_nn(ds, k)
                dk = _dot_tn(ds, q_ref[rr, :, sl])
                dv = _dot_tn(p_buf[c], do_ref[rr, :, sl])
                dk_ref[rr, :, sl] = ck[rr, :, sl] + dk[0:BLK]
                dv_ref[rr, :, sl] = cv[rr, :, sl] + dv[0:BLK]
                ck[rr, :, sl] = dk[BLK:2 * BLK]
                cv[rr, :, sl] = dv[BLK:2 * BLK]
                if seg and rr > 0:
                    @pl.when(n == 0)
                    def _(rr=rr, sl=sl, dk=dk, dv=dv):
                        fk[rr - 1, :, sl] = dk[0:BLK]
                        fv[rr - 1, :, sl] = dv[0:BLK]

        @pl.when(n == nb)
        def _():
            dk_ref[...] = ck[...] + fk[...]
            dv_ref[...] = cv[...] + fv[...]

    blk = (R, BLK, GW)
    at_q = _att_spec(R, lambda n: jnp.minimum(n, last))
    behind = _att_spec(R, lambda n: jnp.maximum(n - 1, 0))
    n_ch = len(chains)
    q3 = _att_3d(qkv, d, width)
    res = pl.pallas_call(
        body, name="att_bwd_g%d" % grp, grid=(P // R, nb + 1),
        in_specs=_att_qkv_specs(R, nb) + [at_q, at_q, at_q],
        out_specs=[at_q, behind, behind],
        out_shape=[_sds((P, L, GW), F32)] * 3,
        scratch_shapes=[pltpu.VMEM(blk, F32)] * 4
        + [pltpu.VMEM((n_ch, BLK, 2 * BLK), F32), pltpu.VMEM((n_ch, BLK, 2 * BLK), F32),
           pltpu.VMEM((n_ch, BLK, 2 * BLK), CDT), pltpu.VMEM((n_ch, BLK, 2 * BLK), CDT)],
        compiler_params=_params(2),
    )(*[q3] * 7, _att_3d(lse, d, width), _att_3d(do_g, d, width), _att_3d(c_g, d, width))
    return [t.reshape(_rm_shape(S, d, GW)) for t in res]


def _mix_alpha(l0, l1, l2):
    mx = jnp.maximum(jnp.maximum(l0, l1), l2)
    e = [jnp.exp(l0 - mx), jnp.exp(l1 - mx), jnp.exp(l2 - mx)]
    tot = e[0] + e[1] + e[2]
    return [ei / tot for ei in e]


def _mix_fwd(S, os_, ls_, tm=512):
    dil = [d for _, d in ATT_GROUPS]

    def body(*refs):
        out, bufs = refs[6], refs[7:13]
        get = [_rm_reader(bufs[j], refs[j], dil[j % 3]) for j in range(6)]
        rows = slice(None)
        for s in range(ATT_HPG):
            al = _mix_alpha(*[get[3 + g](s, rows) for g in range(3)])
            mixed = al[0] * get[0](s, rows) + al[1] * get[1](s, rows) + al[2] * get[2](s, rows)
            out[:, s * HD:(s + 1) * HD] = mixed.astype(out.dtype)

    specs = [_rm_spec(tm, d, GW) for d in dil]
    return pl.pallas_call(
        body, name="mix_fwd", grid=(S // tm,), in_specs=specs * 2, out_specs=pl.BlockSpec((tm, GW), lambda i: (i, 0)),
        out_shape=_sds((S, GW), CDT), scratch_shapes=[pltpu.VMEM((ATT_HPG, tm, HD), F32)] * 6,
        compiler_params=_params(1))(*os_, *ls_)


def _mix_bwd(S, os_, ls_, do_a, tm=512):
    dil = [d for _, d in ATT_GROUPS]

    def body(*refs):
        d_ref, outs, bufs, tmps = refs[6], refs[7:13], refs[13:19], refs[19:25]
        get = [_rm_reader(bufs[j], refs[j], dil[j % 3]) for j in range(6)]
        for s in range(ATT_HPG):
            cols = slice(s * HD, (s + 1) * HD)

            def chunk(rows, s=s, cols=cols):
                al = _mix_alpha(*[get[3 + g](s, rows) for g in range(3)])
                dv = d_ref[rows, cols]
                o_a = al[0] * get[0](s, rows) + al[1] * get[1](s, rows) + al[2] * get[2](s, rows)
                dsum = jnp.sum(dv * o_a, axis=-1, keepdims=True)
                for g in range(3):
                    tmps[g][rows, :] = al[g] * dv
                    tmps[3 + g][rows, :] = -(al[g] * dsum)

            chunk(slice(None))
            for j in range(6):
                _rm_put(outs[j], cols, tmps[j], dil[j % 3])

    specs = [_rm_spec(tm, d, GW) for d in dil]
    res = pl.pallas_call(
        body, name="mix_bwd", grid=(S // tm,), in_specs=specs * 2 + [pl.BlockSpec((tm, GW), lambda i: (i, 0))],
        out_specs=specs * 2,
        out_shape=[_sds(_rm_shape(S, d, GW), CDT) for d in dil] + [_sds(_rm_shape(S, d, GW), F32) for d in dil],
        scratch_shapes=[pltpu.VMEM((ATT_HPG, tm, HD), F32)] * 6 + [pltpu.VMEM((tm, HD), F32)] * 6,
        compiler_params=_params(1))(*os_, *ls_, do_a)
    return res[:3], res[3:]


def _ret_tables(dk):
    H, C = RET_HEADS, BLK
    log_g = jnp.log(1.0 - 2.0 ** (-5.0 - jnp.arange(H, dtype=F32)))
    idx = jnp.arange(C, dtype=F32)
    diff = idx[:, None] - idx[None, :]
    decay = jnp.where(diff >= 0, jnp.exp(log_g[:, None, None] * jnp.maximum(diff, 0.0)), 0.0)
    xi = jnp.exp(log_g[:, None] * (idx[None, :] + 1.0))
    zeta = jnp.exp(log_g[:, None] * (C - 1.0 - idx[None, :])) * (dk ** -0.5)
    g_chunk = jnp.exp(log_g * C)
    bc = lambda t: jnp.broadcast_to(t[:, :, None], (H, C, C))
    return decay, bc(xi), bc(zeta), jnp.broadcast_to(g_chunk[:, None, None], (H, 8, C))


def _gn_fwd(o, g, b):
    mu = jnp.mean(o, axis=-1, keepdims=True)
    xc = o - mu
    rstd = lax.rsqrt(jnp.mean(xc * xc, axis=-1, keepdims=True) + EPS)
    yh = xc * rstd
    return yh, rstd, yh * g + b


def _ret_specs(dk, dv, order):
    H = RET_HEADS
    qk_w, v_w = H * dk, H * dv
    off_q = 3 * ATT_W
    off_k, off_v, off_g = off_q + qk_w, off_q + 2 * qk_w, off_q + 2 * qk_w + v_w
    assert 2 * dk == dv and all(off % dv == 0 for off in (off_q, off_k, off_v, off_g))

    def col(off, j):
        return pl.BlockSpec((BLK, dv), lambda i: (order(i), off // dv + j))

    tab = pl.BlockSpec((H, BLK, BLK), lambda i: (0, 0, 0))
    return ([col(off_q, j) for j in range(H // 2)] + [col(off_k, j) for j in range(H // 2)]
            + [col(off_v, j) for j in range(H)] + [col(off_g, j) for j in range(H)]
            + [tab, tab, tab, pl.BlockSpec((H, 8, BLK), lambda i: (0, 0, 0))])


def _ret_heads(refs, dk):
    H = RET_HEADS
    q_refs, k_refs = refs[0:H // 2], refs[H // 2:H]
    v_refs, gr_refs = refs[H:2 * H], refs[2 * H:3 * H]

    def head(h):
        cols = slice((h % 2) * dk, (h % 2 + 1) * dk)
        return q_refs[h // 2][:, cols], k_refs[h // 2][:, cols], v_refs[h][...], gr_refs[h][...]

    return head, refs[3 * H:3 * H + 4]


def _ret_fwd(proj, gn_g, gn_b, dk, dv):
    S = proj.shape[0]
    N = S // BLK
    H = RET_HEADS
    kscale = dk ** -0.5
    n_in = 3 * H + 4

    def body(*refs):
        head, (dec_ref, xi_ref, zeta_ref, gc_ref) = _ret_heads(refs, dk)
        g_ref, b_ref, opre_ref, or_ref, st_ref, state, s_buf, cross_buf = refs[n_in:n_in + 8]
        n = pl.program_id(0)

        @pl.when(n == 0)
        def _():
            state[...] = jnp.zeros_like(state)

        for h in range(H):
            q, k, v, _ = head(h)
            s_buf[h] = _dot_nt(q, k)
            st = state[h]
            st_c = st.astype(CDT)
            st_ref[h] = st_c
            cross_buf[h] = _dot_nn(q, st_c)
            kz = (k.astype(F32) * zeta_ref[h][:, 0:1]).astype(CDT)
            state[h] = st * gc_ref[h][0:1, 0:1] + _dot_tn(kz, v)
        for h in range(H):
            vs = slice(h * dv, (h + 1) * dv)
            _, _, v, gr = head(h)
            s = s_buf[h] * kscale * dec_ref[h]
            o = _dot_nn(s.astype(CDT), v) + cross_buf[h] * xi_ref[h][:, 0:1]
            opre_ref[:, vs] = o
            _, _, y = _gn_fwd(o, g_ref[:, vs], b_ref[:, vs])
            gr = gr.astype(F32)
            or_ref[:, vs] = (y * (gr * _sigmoid(gr))).astype(or_ref.dtype)

    v_w = H * dv
    row = pl.BlockSpec((1, v_w), lambda i: (0, 0))
    tile = pl.BlockSpec((BLK, v_w), lambda i: (i, 0))
    return pl.pallas_call(
        body, name="ret_fwd", grid=(N,),
        in_specs=_ret_specs(dk, dv, lambda i: i) + [row, row],
        out_specs=[tile, tile, pl.BlockSpec((None, H, dk, dv), lambda i: (i, 0, 0, 0))],
        out_shape=[_sds((S, v_w), F32), _sds((S, v_w), CDT), _sds((N, H, dk, dv), CDT)],
        scratch_shapes=[pltpu.VMEM((H, dk, dv), F32), pltpu.VMEM((H, BLK, BLK), F32), pltpu.VMEM((H, BLK, dv), F32)],
        compiler_params=_params(1),
    )(*[proj] * (3 * H), *_ret_tables(dk), gn_g, gn_b)


def _ret_bwd(proj, gn_g, gn_b, o_pre, states, d_or, dga, dgb, dk, dv):
    S, in_w = proj.shape
    N = S // BLK
    H = RET_HEADS
    qk_w, v_w = H * dk, H * dv
    kscale = dk ** -0.5
    n_in = 3 * H + 4
    out_w = 2 * qk_w + 2 * v_w
    gate_w = dga.shape[1]
    col0 = 3 * ATT_W
    assert col0 + out_w + 2 * gate_w == in_w
    rev = lambda i: N - 1 - i

    def body(*refs):
        head, (dec_ref, xi_ref, zeta_ref, gc_ref) = _ret_heads(refs, dk)
        (g_ref, b_ref, opre_ref, st_ref, dor_ref, dga_ref, dgb_ref, dproj_ref, dg_ref, db_ref, dstate, stage,
         sem, do_buf, dox_buf, a_buf, g_buf, dq_buf, dk_buf, dv_buf) = refs[n_in:n_in + 20]
        i = pl.program_id(0)
        slot = i % 2
        out_ref = stage.at[slot]

        def out_copy(s, step):
            rows = pl.ds(pl.multiple_of(rev(step) * BLK, BLK), BLK)
            return pltpu.make_async_copy(stage.at[s], dproj_ref.at[rows, pl.ds(col0, in_w - col0)], sem.at[s])

        @pl.when(i >= 2)
        def _():
            out_copy(slot, i - 2).wait()

        @pl.when(i == 0)
        def _():
            dstate[...] = jnp.zeros_like(dstate)
            dg_ref[...] = jnp.zeros_like(dg_ref)
            db_ref[...] = jnp.zeros_like(db_ref)

        out_ref[:, out_w:out_w + gate_w] = dga_ref[...]
        out_ref[:, out_w + gate_w:out_w + 2 * gate_w] = dgb_ref[...]
        for h in range(H):
            vs = slice(h * dv, (h + 1) * dv)
            _, _, _, gr = head(h)
            gr = gr.astype(F32)
            sg = _sigmoid(gr)
            gain = g_ref[:, vs]
            yh, rstd, y = _gn_fwd(opre_ref[:, vs], gain, b_ref[:, vs])
            d_or_v = dor_ref[:, vs]
            dy = d_or_v * (gr * sg)
            out_ref[:, 2 * qk_w + v_w + h * dv:2 * qk_w + v_w + (h + 1) * dv] = (
                d_or_v * y * (sg * (1.0 + gr * (1.0 - sg)))).astype(out_ref.dtype)
            dg_ref[:, vs] += jnp.sum(dy * yh, axis=0, keepdims=True)
            db_ref[:, vs] += jnp.sum(dy, axis=0, keepdims=True)
            dyh = dy * gain
            do = rstd * (dyh - jnp.mean(dyh, axis=-1, keepdims=True)
                         - yh * jnp.mean(dyh * yh, axis=-1, keepdims=True))
            do_buf[h] = do.astype(CDT)
            dox_buf[h] = (do * xi_ref[h][:, 0:1]).astype(CDT)
        for h in range(H):
            q, k, v, _ = head(h)
            dox = dox_buf[h]
            a_buf[h] = _dot_nt(q, k)
            g_buf[h] = _dot_nt(do_buf[h], v)
            dsn = dstate[h]
            dsn_c = dsn.astype(CDT)
            kz = (k.astype(F32) * zeta_ref[h][:, 0:1]).astype(CDT)
            dq_buf[h] = _dot_nt(dox, st_ref[h])
            dk_buf[h] = _dot_nt(v, dsn_c)
            dv_buf[h] = _dot_nn(kz, dsn_c)
            dstate[h] = dsn * gc_ref[h][0:1, 0:1] + _dot_tn(q, dox)
        for h in range(H):
            q, k, _, _ = head(h)
            decay = dec_ref[h]
            a_c = (a_buf[h] * kscale * decay).astype(CDT)
            g_c = (g_buf[h] * decay).astype(CDT)
            dq = _dot_nn(g_c, k) * kscale + dq_buf[h]
            dkk = _dot_tn(g_c, q) * kscale + dk_buf[h] * zeta_ref[h][:, 0:1]
            dvv = _dot_tn(a_c, do_buf[h]) + dv_buf[h]
            out_ref[:, h * dk:(h + 1) * dk] = dq.astype(out_ref.dtype)
            out_ref[:, qk_w + h * dk:qk_w + (h + 1) * dk] = dkk.astype(out_ref.dtype)
            out_ref[:, 2 * qk_w + h * dv:2 * qk_w + (h + 1) * dv] = dvv.astype(out_ref.dtype)

        cp = out_copy(slot, i)
        cp.start()

        @pl.when(i == N - 1)
        def _():
            cp.wait()
            if N >= 2:
                out_copy(1 - slot, i - 1).wait()

    row = pl.BlockSpec((1, v_w), lambda i: (0, 0))
    tile = pl.BlockSpec((BLK, v_w), lambda i: (rev(i), 0))
    gate = pl.BlockSpec((BLK, gate_w), lambda i: (rev(i), 0))
    return pl.pallas_call(
        body, name="ret_bwd", grid=(N,),
        in_specs=_ret_specs(dk, dv, rev) + [row, row, tile,
                 pl.BlockSpec((None, H, dk, dv), lambda i: (rev(i), 0, 0, 0)), tile, gate, gate],
        out_specs=[pl.BlockSpec(memory_space=pl.ANY), row, row],
        out_shape=[_sds((S, in_w), CDT), _sds((1, v_w), F32), _sds((1, v_w), F32)],
        scratch_shapes=[pltpu.VMEM((H, dk, dv), F32), pltpu.VMEM((2, BLK, in_w - col0), CDT),
                        pltpu.SemaphoreType.DMA((2,)),
                        pltpu.VMEM((H, BLK, dv), CDT), pltpu.VMEM((H, BLK, dv), CDT),
                        pltpu.VMEM((H, BLK, BLK), F32), pltpu.VMEM((H, BLK, BLK), F32),
                        pltpu.VMEM((H, BLK, dk), F32), pltpu.VMEM((H, BLK, dk), F32), pltpu.VMEM((H, BLK, dv), F32)],
        compiler_params=_params(1),
    )(*[proj] * (3 * H), *_ret_tables(dk), gn_g, gn_b, o_pre, states, d_or, dga, dgb)


def _merge_fwd(o_a, o_r, wa, wb, proj, d_model, tm=1024, tn=512):
    S, in_w = proj.shape
    off_a, off_b = in_w - 2 * d_model, in_w - d_model
    assert off_a % tn == 0 and off_b % tn == 0

    def body(oa_ref, or_ref, wa_ref, wb_ref, ga_ref, gb_ref, y_ref, pa_ref, pb_ref):
        pa = _dot_nn(oa_ref[...], wa_ref[...])
        pb = _dot_nn(or_ref[...], wb_ref[...])
        y = _sigmoid(ga_ref[...].astype(F32)) * pa + _sigmoid(gb_ref[...].astype(F32)) * pb
        y_ref[...] = y.astype(y_ref.dtype)
        pa_ref[...] = pa.astype(pa_ref.dtype)
        pb_ref[...] = pb.astype(pb_ref.dtype)

    ka, kb = o_a.shape[1], o_r.shape[1]
    out = pl.BlockSpec((tm, tn), lambda i, j: (i, j))
    return pl.pallas_call(
        body, name="merge_fwd", grid=(S // tm, d_model // tn),
        in_specs=[pl.BlockSpec((tm, ka), lambda i, j: (i, 0)), pl.BlockSpec((tm, kb), lambda i, j: (i, 0)),
                  pl.BlockSpec((ka, tn), lambda i, j: (0, j)), pl.BlockSpec((kb, tn), lambda i, j: (0, j)),
                  pl.BlockSpec((tm, tn), lambda i, j: (i, off_a // tn + j)),
                  pl.BlockSpec((tm, tn), lambda i, j: (i, off_b // tn + j))],
        out_specs=[out, out, out], out_shape=[_sds((S, d_model), CDT)] * 3,
        compiler_params=_params(2))(o_a, o_r, wa, wb, proj, proj)


def _d_x(dproj, w_in, x, g, dx1, dep, tm=512, row_groups=2):
    S, D = x.shape
    n_sh, _, ns = w_in.shape
    nt = S // tm // row_groups
    deps = [d for d in (dep,) if d is not None]

    def body(a_ref, b_ref, x_ref, g_ref, r_ref, *rest):
        dx_ref, dg_ref, acc = rest[len(deps):]
        h, k, i = pl.program_id(0), pl.program_id(1), pl.program_id(2)

        @pl.when(k == 0)
        def _():
            acc[i] = jnp.zeros((tm, D), F32)

        acc[i] += _dot_nt(a_ref[...], b_ref[...])

        @pl.when(k == n_sh - 1)
        def _():
            _ep_rms_bwd(acc[i], (x_ref, g_ref, r_ref), (dx_ref, dg_ref), h * nt + i)

    def last_only(h, k, i):
        return (h * nt + jnp.where(k == n_sh - 1, i, 0), 0)

    return pl.pallas_call(
        body, name="d_x", grid=(row_groups, n_sh, nt),
        in_specs=[pl.BlockSpec((tm, ns), lambda h, k, i: (h * nt + i, k)),
                  pl.BlockSpec((None, D, ns), lambda h, k, i: (k, 0, 0)),
                  pl.BlockSpec((tm, D), last_only), pl.BlockSpec((1, D), lambda h, k, i: (0, 0)),
                  pl.BlockSpec((tm, D), last_only)] + [pl.BlockSpec(memory_space=pl.ANY)] * len(deps),
        out_specs=[pl.BlockSpec((tm, D), last_only), pl.BlockSpec((1, D), lambda h, k, i: (0, 0))],
        out_shape=[_sds((S, D), F32), _sds((1, D), F32)],
        scratch_shapes=[pltpu.VMEM((nt, tm, D), F32)],
        compiler_params=_params(3))(dproj, w_in, x, g, dx1, *deps)


def _local_step(x, target, w_in_mine, chip, others, near_w_in, far_w_in, small, late_weights, on_grads, deps0=()):
    S, D = x.shape
    ns_in = w_in_mine.shape[1]
    in_w = N_CHIPS * ns_in
    d_ff = 4 * D
    ret_v_w = 2 * D
    dv = ret_v_w // RET_HEADS
    dk = (in_w - 3 * ATT_W - 2 * ret_v_w - 2 * D) // (2 * RET_HEADS)
    gqk = jnp.concatenate([small["q_norm_g"].reshape(1, ATT_W), small["k_norm_g"].reshape(1, ATT_W)], axis=1)
    g1, g2 = small["norm1_g"], small["norm2_g"]
    gn_g, gn_b = small["ret_gn_g"], small["ret_gn_b"]

    proj_sds = _sds((S, in_w), CDT)
    xn, proj = _in_proj_mine(x, g1, w_in_mine, chip, proj_sds, deps0)
    for stage, (get_w_in, chips) in enumerate(((near_w_in, others[:2]), (far_w_in, others[2:]))):
        w_in, started = get_w_in(proj)
        (proj,) = _matmul(
            "in_proj_far%d" % stage, "nn", xn, w_in, tm=1024, tn=ns_in, tk=D, prefetch=[chips],
            n_cols=chips.shape[0] * ns_in, b_spec=pl.BlockSpec((None, D, ns_in), lambda i, j, k, o: (o[j], 0, 0)),
            outs=[(proj_sds, pl.BlockSpec((1024, ns_in), lambda i, j, k, o: (i, o[j])))], epilogue=_ep_store,
            deps=[proj, started], alias_dep_to_out=(0, 0), j_outer=True)
    qkv = _qknorm_fwd(proj, gqk)
    att = [_att_fwd(g, S, qkv[g]) for g in range(3)]
    os_, ls_ = [a[0] for a in att], [a[1] for a in att]
    o_a = _mix_fwd(S, os_, ls_)
    o_pre, o_r, states = _ret_fwd(proj, gn_g, gn_b, dk, dv)
    w = late_weights(o_r)
    y, pa, pb = _merge_fwd(o_a, o_r, w["w_proj_a"], w["w_proj_b"], proj, D)
    x1, xn2 = _matmul("out_proj", "nn", y, w["w_out"], tm=1024, tn=D, tk=D,
                      extras=[(x, _mn(1024, D)), (g2, _row(D))],
                      outs=[(_sds((S, D), F32), _mn(1024, D)), (_sds((S, D), CDT), _mn(1024, D))],
                      epilogue=_ep_resid_norm)
    hid, act = _matmul("mlp_up", "nn", xn2, w["w_up"], tm=1024, tn=2048, tk=D, j_outer=True,
                       outs=[(_sds((S, d_ff), CDT), _mn(1024, 2048))] * 2, epilogue=_ep_up)
    dx2, dx2c, loss_row = _matmul(
        "mlp_down_loss", "nn", act, w["w_down"], tm=512, tn=D, tk=d_ff,
        extras=[(x1, _mn(512, D)), (target, _mn(512, D))],
        outs=[(_sds((S, D), F32), _mn(512, D)), (_sds((S, D), CDT), _mn(512, D)), (_sds((1, D), F32), _row(D))],
        epilogue=functools.partial(_ep_down_loss, inv_d=1.0 / D))
    loss = 0.5 * jnp.sum(loss_row) / D

    (dh,) = _matmul("d_hidden", "nt", dx2c, w["w_down"], tm=1024, tn=2048, tk=D, j_outer=True,
                    extras=[(hid, _mn(1024, 2048))], outs=[(_sds((S, d_ff), CDT), _mn(1024, 2048))],
                    epilogue=_ep_dh)
    (gw_down,) = _matmul("dw_down", "tn", act, dx2c, tm=1024, tn=D, tk=2048,
                         outs=[(_sds((d_ff, D), F32), _mn(1024, D))], epilogue=_ep_store)
    (gw_up,) = _matmul("dw_up", "tn", xn2, dh, tm=D, tn=1024, tk=2048,
                       outs=[(_sds((D, d_ff), F32), _mn(D, 1024))], epilogue=_ep_store)
    tok = on_grads({"w_down": gw_down, "w_up": gw_up})
    dx1, dx1c, dg2 = _matmul(
        "d_x1", "nt", dh, w["w_up"], tm=512, tn=D, tk=d_ff,
        extras=[(x1, _mn(512, D)), (g2, _row(D)), (dx2, _mn(512, D))],
        outs=[(_sds((S, D), F32), _mn(512, D)), (_sds((S, D), CDT), _mn(512, D)), (_sds((1, D), F32), _row(D))],
        epilogue=_ep_rms_bwd, deps=[tok])

    gt = 512
    assert (in_w - 2 * D) % gt == 0
    off_a, off_b = (in_w - 2 * D) // gt, (in_w - D) // gt
    dpa, dpb, dga, dgb = _matmul(
        "d_gates", "nt", dx1c, w["w_out"], tm=1024, tn=gt, tk=D,
        extras=[(proj, _mn(1024, gt, off_a)), (proj, _mn(1024, gt, off_b)), (pa, _mn(1024, gt)),
                (pb, _mn(1024, gt))],
        outs=[(_sds((S, D), CDT), _mn(1024, gt))] * 4, epilogue=_ep_gates)
    (gw_out,) = _matmul("dw_out", "tn", y, dx1c, tm=D, tn=D, tk=1024,
                        outs=[(_sds((D, D), F32), _mn(D, D))], epilogue=_ep_store)
    (gw_pa,) = _matmul("dw_proj_a", "tn", o_a, dpa, tm=GW, tn=D, tk=1024,
                       outs=[(_sds((GW, D), F32), _mn(GW, D))], epilogue=_ep_store)
    (gw_pb,) = _matmul("dw_proj_b", "tn", o_r, dpb, tm=1024, tn=D, tk=2048,
                       outs=[(_sds((ret_v_w, D), F32), _mn(1024, D))], epilogue=_ep_store)
    (do_a,) = _matmul("d_o_a", "nt", dpa, w["w_proj_a"], tm=1024, tn=GW, tk=D,
                      outs=[(_sds((S, GW), F32), _mn(1024, GW))], epilogue=_ep_store)
    tok = on_grads({"w_out": gw_out, "w_proj_a": gw_pa, "w_proj_b": gw_pb})
    (d_or,) = _matmul("d_o_r", "nt", dpb, w["w_proj_b"], tm=1024, tn=ret_v_w, tk=D,
                      outs=[(_sds((S, ret_v_w), F32), _mn(1024, ret_v_w))], epilogue=_ep_store, deps=[tok])

    dproj, dgn_g, dgn_b = _ret_bwd(proj, gn_g, gn_b, o_pre, states, d_or, dga, dgb, dk, dv)
    do_gs, c_gs = _mix_bwd(S, os_, ls_, do_a)
    datt_parts = [_att_bwd(g, S, qkv[g], ls_[g], do_gs[g], c_gs[g]) for g in range(3)]
    dproj, dgqk = _qknorm_bwd(proj, gqk, [p[0] for p in datt_parts], [p[1] for p in datt_parts],
                              [p[2] for p in datt_parts], dproj)

    (gw_in,) = _matmul(
        "dw_in", "tn", xn, dproj, tm=512, tn=ns_in, tk=1024,
        outs=[(_sds((N_CHIPS, D, ns_in), F32), pl.BlockSpec((None, 512, ns_in), lambda i, j, k: (j, i, 0)))],
        epilogue=_ep_store)
    tok = on_grads({"w_in": gw_in})
    grad_x, dg1 = _d_x(dproj, w_in, x, g1, dx1, tok)

    smallg = {"norm1_g": dg1, "q_norm_g": dgqk[:, :ATT_W], "k_norm_g": dgqk[:, ATT_W:],
              "ret_gn_g": dgn_g, "ret_gn_b": dgn_b, "norm2_g": dg2}
    return loss, grad_x, smallg


N_CHIPS = 4
N_DEV = 8


def _place():
    x, y, c = lax.axis_index("x"), lax.axis_index("y"), lax.axis_index("c")
    return x, y, c


def _other_chips(x, y):
    out = []
    for fx, fy in ((1, 0), (0, 1), (1, 1)):
        px = 1 - x if fx else x
        py = 1 - y if fy else y
        out.append(((px, py), 2 * px + py))
    return out


SEM_SPEC = pl.BlockSpec(memory_space=pltpu.SEMAPHORE)
ANY_SPEC = pl.BlockSpec(memory_space=pl.ANY)
EFFECT = pltpu.SideEffectType.DATAFLOW_SIDE_EFFECTING


def _ici_copies(kind, srcs, lands, send, recv, which=(0, 1, 2)):
    x, y, c = _place()
    me = 2 * x + y
    out = []
    for w, (s, l) in enumerate(zip(srcs, lands)):
        for j, ((px, py), pidx) in enumerate(_other_chips(x, y)):
            if j not in which:
                continue
            if kind == "gather":
                half = s.shape[0] // 2
                rows = pl.ds(c * half, half)
                src, dst_there, dst_here = s.at[rows, :], l.at[me, rows, :], l.at[pidx, rows, :]
            else:
                src, dst_there, dst_here = s.at[pidx], l.at[me], l.at[pidx]
            out.append((src, dst_there, dst_here, send.at[3 * w + j], recv.at[3 * w + j], (px, py, c)))
    return out


def _exchange_start(name, kind, srcs, land_shapes, which=(0, 1, 2), lands=None):
    n = len(srcs)
    if lands is None:
        lands = [lax.empty(shape, dtype) for shape, dtype in land_shapes]

    def body(*refs):
        src_refs, land_refs = refs[:n], refs[n:2 * n]
        send, recv = refs[2 * n], refs[2 * n + 1]
        token = refs[-1]
        for src, dst, _, ss, rs, dev in _ici_copies(kind, src_refs, land_refs, send, recv, which):
            pltpu.make_async_remote_copy(src_ref=src, dst_ref=dst, send_sem=ss, recv_sem=rs, device_id=dev,
                                         device_id_type=MESH).start()
        token[...] = jnp.zeros_like(token)

    thru = [pltpu.HBM(s.shape, s.dtype) for s in srcs] + [pltpu.HBM(shape, dtype) for shape, dtype in land_shapes]
    res = pl.pallas_call(
        body, name=name,
        out_shape=(pltpu.SemaphoreType.DMA((3 * n,)), pltpu.SemaphoreType.DMA((3 * n,)), *thru, _sds((8, LANES), F32)),
        in_specs=[HBM_SPEC] * (2 * n), out_specs=(SEM_SPEC, SEM_SPEC, *[HBM_SPEC] * (2 * n), VMEM_SPEC),
        input_output_aliases={i: 2 + i for i in range(2 * n)},
        compiler_params=pltpu.CompilerParams(has_side_effects=EFFECT),
    )(*[pltpu.with_memory_space_constraint(s, pltpu.HBM) for s in srcs],
      *[pltpu.with_memory_space_constraint(l, pltpu.HBM) for l in lands])
    return res[0], res[1], list(res[2:2 + n]), list(res[2 + n:2 + 2 * n]), res[-1]


def _exchange_wait(name, kind, send, recv, srcs, lands, after, which=(0, 1, 2)):
    n = len(srcs)

    def body(*refs):
        src_refs, land_refs = refs[:n], refs[n:2 * n]
        send_ref, recv_ref = refs[2 * n], refs[2 * n + 1]
        for src, _, dst, ss, rs, dev in _ici_copies(kind, src_refs, land_refs, send_ref, recv_ref, which):
            cp = pltpu.make_async_remote_copy(src_ref=src, dst_ref=dst, send_sem=ss, recv_sem=rs, device_id=dev,
                                              device_id_type=MESH)
            cp.wait_send()
            cp.wait_recv()

    thru = [pltpu.HBM(t.shape, t.dtype) for t in list(srcs) + list(lands)]
    res = pl.pallas_call(
        body, name=name, out_shape=thru,
        in_specs=[HBM_SPEC] * (2 * n) + [SEM_SPEC, SEM_SPEC, ANY_SPEC], out_specs=[HBM_SPEC] * (2 * n),
        input_output_aliases={i: i for i in range(2 * n)},
        compiler_params=pltpu.CompilerParams(has_side_effects=EFFECT),
    )(*srcs, *lands, send, recv, after)
    return list(res[:n]), list(res[n:])


PAIR_TILE_ELEMS = 1 << 20


def _pair_fill(name, gathered, mine, core, others, chip, write_mine=True):
    k, r, C = gathered.shape
    half = r // 2
    tr = _row_tile(half, C, PAIR_TILE_ELEMS, mult=16)
    nt = half // tr
    n_far = others.shape[0]

    def body(c_ref, o_ref, chip_ref, in_ref, mine_ref, out_ref, slot, send, recv):
        j = pl.program_id(0)
        _sibling_barrier((j == 0) & (pl.program_id(1) == 0))
        b = (j * nt + pl.program_id(1)) % 2
        x, y, c = _place()
        cp = pltpu.make_async_remote_copy(src_ref=in_ref, dst_ref=slot.at[b], send_sem=send.at[b],
                                          recv_sem=recv.at[b], device_id=(x, y, 1 - c), device_id_type=MESH)

        @pl.when(j < n_far)
        def _():
            cp.start()
            cp.wait_recv()
            out_ref[...] = slot[b]
            cp.wait_send()

        @pl.when(j >= n_far)
        def _():
            out_ref[...] = mine_ref[...]

    def far(j):
        return jnp.minimum(j, n_far - 1)

    grid_spec = pltpu.PrefetchScalarGridSpec(
        num_scalar_prefetch=3, grid=(n_far + (2 if write_mine else 0), nt),
        in_specs=[pl.BlockSpec((tr, C), lambda j, i, c, o, m: (
                      (2 * o[far(j)] + c[0]) * nt + jnp.where(j < n_far, i, nt - 1), 0)),
                  pl.BlockSpec((tr, C), lambda j, i, c, o, m: (jnp.where(j < n_far, 0, (j - n_far) * nt + i), 0))],
        out_specs=pl.BlockSpec((tr, C), lambda j, i, c, o, m: (
            jnp.where(j < n_far, 2 * o[far(j)] + 1 - c[0], 2 * m[0] + j - n_far) * nt + i, 0)),
        scratch_shapes=[pltpu.VMEM((2, tr, C), gathered.dtype), pltpu.SemaphoreType.DMA((2,)),
                        pltpu.SemaphoreType.DMA((2,))])
    out = pl.pallas_call(body, name=name, grid_spec=grid_spec, out_shape=_sds((k * r, C), gathered.dtype),
                         input_output_aliases={3: 0}, compiler_params=_params(2, PAIR_FILL_ID))(
                             core, others, chip, gathered.reshape(k * r, C), mine)
    return out.reshape(k, r, C)


def _pair_reduce(name, g, core):
    k, R, C = g.shape
    half = R // 2
    tr = _row_tile(half, C, PAIR_TILE_ELEMS, mult=16)
    nt = half // tr

    def body(c_ref, mine_ref, give_ref, out_ref, wire_ref, stage, slot, send, recv):
        _sibling_barrier((pl.program_id(0) == 0) & (pl.program_id(1) == 0))
        b = (pl.program_id(0) * nt + pl.program_id(1)) % 2
        x, y, c = _place()
        stage[b] = give_ref[...].astype(stage.dtype)
        cp = pltpu.make_async_remote_copy(src_ref=stage.at[b], dst_ref=slot.at[b], send_sem=send.at[b],
                                          recv_sem=recv.at[b], device_id=(x, y, 1 - c), device_id_type=MESH)
        cp.start()
        cp.wait_recv()
        tot = mine_ref[...] + slot[b].astype(F32)
        out_ref[...] = tot
        wire_ref[...] = tot.astype(wire_ref.dtype)
        cp.wait_send()

    blk = (tr, C)
    out_spec = pl.BlockSpec(blk, lambda s, i, c: (s * nt + i, 0))
    grid_spec = pltpu.PrefetchScalarGridSpec(
        num_scalar_prefetch=1, grid=(k, nt),
        in_specs=[pl.BlockSpec(blk, lambda s, i, c: ((2 * s + c[0]) * nt + i, 0)),
                  pl.BlockSpec(blk, lambda s, i, c: ((2 * s + 1 - c[0]) * nt + i, 0))],
        out_specs=[out_spec, out_spec],
        scratch_shapes=[pltpu.VMEM((2, tr, C), CDT), pltpu.VMEM((2, tr, C), CDT), pltpu.SemaphoreType.DMA((2,)),
                        pltpu.SemaphoreType.DMA((2,))])
    g2 = g.reshape(k * R, C)
    out, wire = pl.pallas_call(body, name=name, grid_spec=grid_spec,
                               out_shape=[_sds((k * half, C), F32), _sds((k * half, C), CDT)],
                               compiler_params=_params(2, PAIR_REDUCE_ID))(core, g2, g2)
    return out, wire.reshape(k, half, C)


def _all_reduce_small(v):
    r, cdim = v.shape

    def body(v_ref, o_ref, buf, send, recv):
        x, y, c = _place()
        me = 4 * x + 2 * y + c
        buf[me] = v_ref[...]
        sends = []
        for m in range(1, N_DEV):
            px = 1 - x if m & 4 else x
            py = 1 - y if m & 2 else y
            pc = 1 - c if m & 1 else c
            cp = pltpu.make_async_remote_copy(src_ref=v_ref, dst_ref=buf.at[me], send_sem=send.at[m - 1],
                                              recv_sem=recv.at[m - 1], device_id=(px, py, pc), device_id_type=MESH)
            cp.start()
            sends.append((cp, 4 * px + 2 * py + pc))
        for m, (cp, pidx) in enumerate(sends):
            pltpu.make_async_remote_copy(src_ref=v_ref, dst_ref=buf.at[pidx], send_sem=send.at[m], recv_sem=recv.at[m],
                                         device_id=(x, y, c), device_id_type=MESH).wait_recv()
        for cp, _ in sends:
            cp.wait_send()
        tot = buf[0]
        for k in range(1, N_DEV):
            tot = tot + buf[k]
        o_ref[...] = tot

    return pl.pallas_call(
        body, name="all_reduce_small", in_specs=[VMEM_SPEC], out_specs=VMEM_SPEC,
        out_shape=_sds((r, cdim), F32),
        scratch_shapes=[pltpu.VMEM((N_DEV, r, cdim), F32), pltpu.SemaphoreType.DMA((N_DEV - 1,)),
                        pltpu.SemaphoreType.DMA((N_DEV - 1,))],
    )(v)


def _row_tile(rows, cols, budget_elems=1 << 18, mult=8):
    if rows % mult:
        return rows
    t = max(mult, (budget_elems // cols) // mult * mult)
    while rows % t:
        t -= mult
    return t


def _adamw_update(w, g, m, v):
    nm = ADAM_B1 * m + (1.0 - ADAM_B1) * g
    nv = ADAM_B2 * v + (1.0 - ADAM_B2) * (g * g)
    m_hat = nm / (1.0 - ADAM_B1 ** ADAM_STEP)
    v_hat = nv / (1.0 - ADAM_B2 ** ADAM_STEP)
    return -ADAM_LR * (m_hat / (jnp.sqrt(v_hat) + ADAM_EPS) + ADAM_WD * w), nm, nv


def _adamw(name, w, g, m, v):
    R, C = w.shape
    tr = _row_tile(R, C, 1 << 19)

    def body(w_ref, g_ref, m_ref, v_ref, d_ref, nm_ref, nv_ref):
        d_ref[...], nm_ref[...], nv_ref[...] = _adamw_update(w_ref[...], g_ref[...], m_ref[...], v_ref[...])

    spec = pl.BlockSpec((tr, C), lambda i: (i, 0))
    return pl.pallas_call(body, name=name, grid=(R // tr,), in_specs=[spec] * 4, out_specs=[spec] * 3,
                          out_shape=[_sds((R, C), F32)] * 3, compiler_params=_params(1))(w, g, m, v)


def _sum_share(name, own, by_chip, chip, others, core):
    k, half, C = by_chip.shape
    tr = _row_tile(half, C, PAIR_TILE_ELEMS // 2, mult=16)
    nt = half // tr

    def body(chip_ref, oth_ref, c_ref, own_ref, a_ref, b_ref, cc_ref, g_out, mine, slot, send, recv):
        p = pl.program_id(1)
        _sibling_barrier((pl.program_id(0) == 0) & (p == 0))
        b = pl.program_id(0) % 2
        x, y, c = _place()
        cp = pltpu.make_async_remote_copy(src_ref=mine.at[b], dst_ref=slot.at[b], send_sem=send.at[b],
                                          recv_sem=recv.at[b], device_id=(x, y, 1 - c), device_id_type=MESH)

        @pl.when(p == 0)
        def _():
            tot = ((own_ref[...] + a_ref[...].astype(F32)) + b_ref[...].astype(F32)) + cc_ref[...].astype(F32)
            mine[b] = tot
            cp.start()
            g_out[...] = tot

        @pl.when(p == 1)
        def _():
            cp.wait_recv()
            g_out[...] = slot[b]
            cp.wait_send()

    def piece(j):
        return pl.BlockSpec((tr, C), lambda i, p, chip, oth, c: (oth[j] * nt + i, 0))

    grid_spec = pltpu.PrefetchScalarGridSpec(
        num_scalar_prefetch=3, grid=(nt, 2),
        in_specs=[pl.BlockSpec((tr, C), lambda i, p, chip, oth, c: (chip[0] * nt + i, 0)),
                  piece(0), piece(1), piece(2)],
        out_specs=pl.BlockSpec((tr, C), lambda i, p, chip, oth, c: (
            jnp.where(p == 0, c[0], 1 - c[0]) * nt + i, 0)),
        scratch_shapes=[pltpu.VMEM((2, tr, C), F32), pltpu.VMEM((2, tr, C), F32), pltpu.SemaphoreType.DMA((2,)),
                        pltpu.SemaphoreType.DMA((2,))])
    by2 = by_chip.reshape(k * half, C)
    return pl.pallas_call(body, name=name, grid_spec=grid_spec, out_shape=_sds((2 * half, C), F32),
                          compiler_params=_params(2, SUM_SHARE_ID))(chip, others, core, own, by2, by2, by2)


BIG = ("w_in", "w_proj_a", "w_proj_b", "w_out", "w_up", "w_down")
COL_SHARDED = ("w_in", "w_proj_a", "w_up")
SMALL = ("norm1_g", "q_norm_g", "k_norm_g", "ret_gn_g", "ret_gn_b", "norm2_g")
ALL_W = ("norm1_g", "w_in", "q_norm_g", "k_norm_g", "ret_gn_g", "ret_gn_b", "w_proj_a", "w_proj_b", "w_out",
         "norm2_g", "w_up", "w_down")
LANES = 128


def _to_full(name, gathered):
    k, r, c = gathered.shape
    if name in COL_SHARDED:
        return gathered.transpose(1, 0, 2).reshape(r, k * c)
    return gathered.reshape(k * r, c)


def _to_shard_major(name, full):
    if name in COL_SHARDED:
        r, c4 = full.shape
        return full.reshape(r, N_CHIPS, c4 // N_CHIPS).transpose(1, 0, 2)
    r4, c = full.shape
    return full.reshape(N_CHIPS, r4 // N_CHIPS, c)


def kernel(x, norm1_g, w_in, q_norm_g, k_norm_g, ret_gn_g, ret_gn_b, w_proj_a, w_proj_b, w_out, norm2_g, w_up, w_down, loss_target, m_norm1_g, m_w_in, m_q_norm_g, m_k_norm_g, m_ret_gn_g, m_ret_gn_b, m_w_proj_a, m_w_proj_b, m_w_out, m_norm2_g, m_w_up, m_w_down, v_norm1_g, v_w_in, v_q_norm_g, v_k_norm_g, v_ret_gn_g, v_ret_gn_b, v_w_proj_a, v_w_proj_b, v_w_out, v_norm2_g, v_w_up, v_w_down):
    weights = dict(norm1_g=norm1_g, w_in=w_in, q_norm_g=q_norm_g, k_norm_g=k_norm_g, ret_gn_g=ret_gn_g,
                   ret_gn_b=ret_gn_b, w_proj_a=w_proj_a, w_proj_b=w_proj_b, w_out=w_out, norm2_g=norm2_g,
                   w_up=w_up, w_down=w_down)
    moments_m = dict(norm1_g=m_norm1_g, w_in=m_w_in, q_norm_g=m_q_norm_g, k_norm_g=m_k_norm_g, ret_gn_g=m_ret_gn_g,
                     ret_gn_b=m_ret_gn_b, w_proj_a=m_w_proj_a, w_proj_b=m_w_proj_b, w_out=m_w_out,
                     norm2_g=m_norm2_g, w_up=m_w_up, w_down=m_w_down)
    moments_v = dict(norm1_g=v_norm1_g, w_in=v_w_in, q_norm_g=v_q_norm_g, k_norm_g=v_k_norm_g, ret_gn_g=v_ret_gn_g,
                     ret_gn_b=v_ret_gn_b, w_proj_a=v_w_proj_a, w_proj_b=v_w_proj_b, w_out=v_w_out,
                     norm2_g=v_norm2_g, w_up=v_w_up, w_down=v_w_down)

    mx, my = lax.axis_index("x"), lax.axis_index("y")
    core = lax.axis_index("c").astype(jnp.int32).reshape(1)
    chip = (2 * mx + my).astype(jnp.int32).reshape(1)
    others = jnp.stack([2 * (1 - mx) + my, 2 * mx + 1 - my, 2 * (1 - mx) + 1 - my]).astype(jnp.int32)
    shards = {n: weights[n][0].astype(CDT) for n in BIG}
    def start_gather(name, names):
        return _exchange_start(name, "gather", [shards[n] for n in names],
                               [((N_CHIPS,) + shards[n].shape, CDT) for n in names])

    w_in_shape = [((N_CHIPS,) + shards["w_in"].shape, CDT)]
    n_send, n_recv, n_srcs, n_lands, n_token = _exchange_start(
        "gather_w_in_near_start", "gather", [shards["w_in"]], w_in_shape, which=(0, 1))
    late = [n for n in BIG if n != "w_in"]
    flight = {}

    def near_w_in(after):
        srcs, lands = _exchange_wait("gather_w_in_near_wait", "gather", n_send, n_recv, n_srcs, n_lands, after,
                                     which=(0, 1))
        d_send, d_recv, d_srcs, d_lands, _ = _exchange_start(
            "gather_w_in_diag_start", "gather", srcs, w_in_shape, which=(2,), lands=lands)
        flight["late"] = start_gather("gather_late_start", late)
        w_near = _pair_fill("pair_fill_w_in_near", d_lands[0], d_srcs[0], core, others[:2], chip)
        flight["diag"] = (d_send, d_recv, d_srcs, [w_near])
        return w_near, flight["late"][-1]

    def far_w_in(after):
        d_send, d_recv, d_srcs, d_lands = flight["diag"]
        srcs, lands = _exchange_wait("gather_w_in_diag_wait", "gather", d_send, d_recv, d_srcs, d_lands, after,
                                     which=(2,))
        return _pair_fill("pair_fill_w_in_diag", lands[0], srcs[0], core, others[2:], chip, write_mine=False), None

    def late_weights(after):
        l_send, l_recv, l_srcs, l_lands, _ = flight["late"]
        srcs, lands = _exchange_wait("gather_late_wait", "gather", l_send, l_recv, l_srcs, l_lands, after)
        out = {}
        for n, mine, land in zip(late, srcs, lands):
            out[n] = _to_full(n, _pair_fill("pair_fill_%s" % n, land, mine, core, others, chip))
        return out

    pending = []

    def on_grads(group):
        names = list(group)
        red = [_pair_reduce("pair_reduce_%s" % n, g if g.ndim == 3 else _to_shard_major(n, g), core)
               for n, g in group.items()]
        wires = [wire for _, wire in red]
        send, recv, srcs, lands, token = _exchange_start(
            "scatter_start_%s" % names[0], "scatter", wires, [(wire.shape, wire.dtype) for wire in wires])
        pending.append((names, [own for own, _ in red], send, recv, srcs, lands))
        return token

    small = {n: weights[n].reshape(1, -1) for n in SMALL}

    loss, grad_x, small_g = _local_step(x[0], loss_target[0], n_srcs[0], chip, others, near_w_in, far_w_in, small,
                                        late_weights, on_grads, deps0=[n_token])

    out_g, out_d, out_m, out_v = {}, {}, {}, {}
    for names, owns, send, recv, srcs, lands in pending:
        _, got = _exchange_wait("scatter_wait_%s" % names[0], "scatter", send, recv, srcs, lands, grad_x)
        for n, own, by_chip in zip(names, owns, got):
            shape = weights[n].shape
            g2 = _sum_share("sum_share_%s" % n, own, by_chip, chip, others, core)
            d, nm, nv = _adamw("adamw_%s" % n, weights[n][0], g2, moments_m[n][0], moments_v[n][0])
            out_g[n], out_d[n], out_m[n], out_v[n] = (t.reshape(shape) for t in (g2, d, nm, nv))

    packed = jnp.concatenate([small_g[n].reshape(1, -1) for n in SMALL], axis=1).reshape(-1, LANES)
    loss_tile = jnp.zeros((8, LANES), F32).at[0, 0].set(loss)
    red = _all_reduce_small(jnp.concatenate([packed, loss_tile], axis=0))
    loss = red[packed.shape[0], 0]
    red = red[:packed.shape[0]].reshape(1, -1)
    off = 0
    for n in SMALL:
        shape = weights[n].shape
        row = (1, weights[n].size)
        g2 = red[:, off:off + row[1]]
        off += row[1]
        d, nm, nv = _adamw("adamw_%s" % n, weights[n].reshape(row), g2, moments_m[n].reshape(row),
                           moments_v[n].reshape(row))
        out_g[n], out_d[n], out_m[n], out_v[n] = (t.reshape(shape) for t in (g2, d, nm, nv))

    return (loss, grad_x[None], *[out_g[n] for n in ALL_W], *[out_d[n] for n in ALL_W],
            *[out_m[n] for n in ALL_W], *[out_v[n] for n in ALL_W])
```

```python
import functools

import jax
import jax.numpy as jnp
from jax import lax
from jax.experimental import pallas as pl
from jax.experimental.pallas import tpu as pltpu

CDT = jnp.bfloat16
F32 = jnp.float32
EPS = 1e-6

ATT_GROUPS = ((128, 1), (512, 4), (2048, 16))
ATT_HPG = 4
ATT_HEADS = 12
HD = 128
BLK = 128
ATT_W = ATT_HEADS * HD
GW = ATT_HPG * HD
RET_HEADS = 4

ADAM_LR = 0.001
ADAM_B1 = 0.9
ADAM_B2 = 0.999
ADAM_EPS = 1e-08
ADAM_WD = 0.01
ADAM_STEP = 10

VMEM_LIMIT_BYTES = 56 * 1024 * 1024
MESH = pl.DeviceIdType.MESH
HBM_SPEC = pl.BlockSpec(memory_space=pltpu.HBM)
VMEM_SPEC = pl.BlockSpec(memory_space=pltpu.VMEM)


def _params(n_axes, collective_id=None):
    return pltpu.CompilerParams(dimension_semantics=("arbitrary",) * n_axes,
                                vmem_limit_bytes=VMEM_LIMIT_BYTES, collective_id=collective_id)


PAIR_FILL_ID, PAIR_REDUCE_ID, SUM_SHARE_ID = 1, 2, 3


def _sibling_barrier(first_step):
    @pl.when(first_step)
    def _():
        sem = pltpu.get_barrier_semaphore()
        x, y, c = lax.axis_index("x"), lax.axis_index("y"), lax.axis_index("c")
        pl.semaphore_signal(sem, inc=1, device_id=(x, y, 1 - c), device_id_type=pl.DeviceIdType.MESH)
        pl.semaphore_wait(sem, 1)


def _dot_nn(a, b):
    return jnp.dot(a, b, preferred_element_type=F32)


def _dot_nt(a, b):
    return lax.dot_general(a, b, (((1,), (1,)), ((), ())), preferred_element_type=F32)


def _dot_tn(a, b):
    return lax.dot_general(a, b, (((0,), (0,)), ((), ())), preferred_element_type=F32)


def _sigmoid(v):
    return 1.0 / (1.0 + jnp.exp(-v))


def _matmul(name, mode, a, b, *, tm, tn, tk, extras=(), outs, epilogue, deps=(), b_spec=None, n_cols=None,
            prefetch=(), alias_dep_to_out=None, j_outer=False):
    deps = [d for d in deps if d is not None]
    if mode == "tn":
        K, M = a.shape
    else:
        M, K = a.shape
    if b_spec is None:
        (N, K2) = b.shape if mode == "nt" else b.shape[::-1]
        assert K == K2, (name, a.shape, b.shape)
        if mode == "nt":
            b_spec = pl.BlockSpec((tn, tk), lambda i, j, k, *p: (j, k))
        else:
            b_spec = pl.BlockSpec((tk, tn), lambda i, j, k, *p: (k, j))
    else:
        N = n_cols
    assert M % tm == 0 and N % tn == 0 and K % tk == 0, (name, a.shape, b.shape)
    ni, nj, nk = M // tm, N // tn, K // tk
    if mode == "tn":
        a_spec = pl.BlockSpec((tk, tm), lambda i, j, k, *p: (k, i))
    else:
        a_spec = pl.BlockSpec((tm, tk), lambda i, j, k, *p: (i, k))
    dot = {"nn": _dot_nn, "nt": _dot_nt, "tn": _dot_tn}[mode]
    n_ex, n_out, n_dep, n_pre = len(extras), len(outs), len(deps), len(prefetch)
    grid = (ni, nj, nk)
    if j_outer:
        grid = (nj, ni, nk)

        def swapped(spec):
            return pl.BlockSpec(spec.block_shape, lambda j, i, k, *p: spec.index_map(i, j, k, *p))

        a_spec, b_spec = swapped(a_spec), swapped(b_spec)
        extras = [(e, swapped(s)) for e, s in extras]
        outs = [(o, swapped(s)) for o, s in outs]

    def body(*refs):
        refs = refs[n_pre:]
        a_ref, b_ref = refs[0], refs[1]
        ex = refs[2:2 + n_ex]
        out = refs[2 + n_ex + n_dep:2 + n_ex + n_dep + n_out]
        acc = refs[-1] if nk > 1 else None
        i = pl.program_id(1 if j_outer else 0)
        k = pl.program_id(2)
        if nk == 1:
            epilogue(dot(a_ref[...].astype(CDT), b_ref[...].astype(CDT)), ex, out, i)
            return

        @pl.when(k == 0)
        def _():
            acc[...] = jnp.zeros_like(acc)

        acc[...] += dot(a_ref[...].astype(CDT), b_ref[...].astype(CDT))

        @pl.when(k == nk - 1)
        def _():
            epilogue(acc[...], ex, out, i)

    grid_spec = pltpu.PrefetchScalarGridSpec(
        num_scalar_prefetch=n_pre, grid=grid,
        in_specs=[a_spec, b_spec] + [s for _, s in extras] + [pl.BlockSpec(memory_space=pl.ANY)] * n_dep,
        out_specs=[s for _, s in outs],
        scratch_shapes=[pltpu.VMEM((tm, tn), F32)] if nk > 1 else [])
    aliases = {}
    if alias_dep_to_out is not None:
        aliases = {n_pre + 2 + n_ex + alias_dep_to_out[0]: alias_dep_to_out[1]}
    res = pl.pallas_call(
        body, name=name, grid_spec=grid_spec, out_shape=[o for o, _ in outs], input_output_aliases=aliases,
        compiler_params=_params(3),
    )(*prefetch, a, b, *[e for e, _ in extras], *deps)
    return res


def _mn(tm, tn, col_off=0):
    return pl.BlockSpec((tm, tn), lambda i, j, k, *p: (i, j + col_off))


def _row(tn):
    return pl.BlockSpec((1, tn), lambda i, j, k, *p: (0, j))


def _ep_store(acc, ex, out, i):
    out[0][...] = acc.astype(out[0].dtype)


def _ep_resid_norm(acc, ex, out, i):
    x1 = ex[0][...] + acc
    out[0][...] = x1
    rstd = lax.rsqrt(jnp.mean(x1 * x1, axis=-1, keepdims=True) + EPS)
    out[1][...] = (x1 * rstd * ex[1][...]).astype(out[1].dtype)


def _ep_up(acc, ex, out, i):
    out[0][...] = acc.astype(out[0].dtype)
    r = jnp.maximum(acc, 0.0)
    out[1][...] = (r * r).astype(out[1].dtype)


def _ep_down_loss(acc, ex, out, i, inv_d):
    diff = (ex[0][...] + acc) - ex[1][...]
    dx2 = diff * inv_d
    out[0][...] = dx2
    out[1][...] = dx2.astype(out[1].dtype)

    @pl.when(i == 0)
    def _():
        out[2][...] = jnp.zeros_like(out[2])

    out[2][...] += jnp.sum(diff * diff, axis=0, keepdims=True)


def _ep_dh(acc, ex, out, i):
    h = ex[0][...].astype(F32)
    out[0][...] = (acc * (2.0 * jnp.maximum(h, 0.0))).astype(out[0].dtype)


def _ep_rms_bwd(acc, ex, out, i):
    x = ex[0][...]
    g = ex[1][...]
    rstd = lax.rsqrt(jnp.mean(x * x, axis=-1, keepdims=True) + EPS)
    xh = x * rstd
    dxh = acc * g
    dx = ex[2][...] + rstd * (dxh - xh * jnp.mean(dxh * xh, axis=-1, keepdims=True))
    out[0][...] = dx
    for copy in out[1:-1]:
        copy[...] = dx.astype(copy.dtype)
    dg = out[-1]

    @pl.when(i == 0)
    def _():
        dg[...] = jnp.zeros_like(dg)

    dg[...] += jnp.sum(acc * xh, axis=0, keepdims=True)


def _ep_gates(acc, ex, out, i):
    sa = _sigmoid(ex[0][...].astype(F32))
    sb = _sigmoid(ex[1][...].astype(F32))
    dpa = acc * sa
    dpb = acc * sb
    out[0][...] = dpa.astype(out[0].dtype)
    out[1][...] = dpb.astype(out[1].dtype)
    out[2][...] = (dpa * ex[2][...].astype(F32) * (1.0 - sa)).astype(out[2].dtype)
    out[3][...] = (dpb * ex[3][...].astype(F32) * (1.0 - sb)).astype(out[3].dtype)


def _sds(shape, dtype):
    return jax.ShapeDtypeStruct(shape, dtype)


def _in_proj_mine(x, g, w_mine, chip, proj_sds, deps, tm=512):
    S, D = x.shape
    ns = w_mine.shape[1]
    deps = [d for d in deps if d is not None]

    def body(c_ref, x_ref, g_ref, w_ref, *rest):
        xn_ref, proj_ref = rest[len(deps)], rest[len(deps) + 1]
        xv = x_ref[...]
        rstd = lax.rsqrt(jnp.mean(xv * xv, axis=-1, keepdims=True) + EPS)
        xn = (xv * rstd * g_ref[...]).astype(xn_ref.dtype)
        xn_ref[...] = xn
        proj_ref[...] = _dot_nn(xn, w_ref[...]).astype(proj_ref.dtype)

    grid_spec = pltpu.PrefetchScalarGridSpec(
        num_scalar_prefetch=1, grid=(S // tm,),
        in_specs=[pl.BlockSpec((tm, D), lambda i, c: (i, 0)), pl.BlockSpec((1, D), lambda i, c: (0, 0)),
                  pl.BlockSpec((D, ns), lambda i, c: (0, 0))] + [pl.BlockSpec(memory_space=pl.ANY)] * len(deps),
        out_specs=[pl.BlockSpec((tm, D), lambda i, c: (i, 0)), pl.BlockSpec((tm, ns), lambda i, c: (i, c[0]))])
    return pl.pallas_call(body, name="in_proj_mine", grid_spec=grid_spec, out_shape=[_sds((S, D), CDT), proj_sds],
                          compiler_params=_params(1))(chip, x, g, w_mine, *deps)


def _rm_shape(S, d, width):
    return (S, width) if d == 1 else (d, S // d, width)


def _rm_spec(tm, d, width):
    if d == 1:
        return pl.BlockSpec((tm, width), lambda i: (i, 0))
    return pl.BlockSpec((d, tm // d, width), lambda i: (0, i, 0))


def _rm_put(dst_ref, cols, buf_ref, d):
    if d == 1:
        dst_ref[:, cols] = buf_ref[...].astype(dst_ref.dtype)
        return
    m = buf_ref.shape[0] // d
    for r in range(d):
        dst_ref[r, :, cols] = buf_ref[pl.ds(r, m, stride=d), :].astype(dst_ref.dtype)


def _rm_reader(buf_ref, src_ref, d):
    if d == 1:
        return lambda s, rows: src_ref[rows, s * HD:(s + 1) * HD].astype(F32)
    m = buf_ref.shape[1] // d
    for s in range(buf_ref.shape[0]):
        for r in range(d):
            buf_ref.at[s][pl.ds(r, m, stride=d), :] = src_ref[r, :, s * HD:(s + 1) * HD].astype(F32)
    return lambda s, rows: buf_ref.at[s][rows, :]


def _qknorm_fwd(proj, gqk, tm=512):
    S = proj.shape[0]
    W = 2 * ATT_W
    dil = [d for _, d in ATT_GROUPS]

    def body(p_ref, g_ref, o0, o1, o2, buf):
        outs = (o0, o1, o2)
        for hd in range(3 * ATT_HEADS):
            which, head = hd // ATT_HEADS, hd % ATT_HEADS
            grp, slot = head // ATT_HPG, head % ATT_HPG
            cols = slice(hd * HD, (hd + 1) * HD)

            def chunk(rows, which=which, cols=cols):
                v = p_ref[rows, cols].astype(F32)
                if which < 2:
                    rstd = lax.rsqrt(jnp.mean(v * v, axis=-1, keepdims=True) + EPS)
                    v = v * rstd * g_ref[:, cols]
                buf[rows, :] = v

            chunk(slice(None))
            _rm_put(outs[grp], slice(which * GW + slot * HD, which * GW + (slot + 1) * HD), buf, dil[grp])

    return pl.pallas_call(
        body, name="qknorm_fwd", grid=(S // tm,),
        in_specs=[pl.BlockSpec((tm, 3 * ATT_W), lambda i: (i, 0)), pl.BlockSpec((1, W), lambda i: (0, 0))],
        out_specs=[_rm_spec(tm, d, 3 * GW) for d in dil],
        out_shape=[_sds(_rm_shape(S, d, 3 * GW), CDT) for d in dil],
        scratch_shapes=[pltpu.VMEM((tm, HD), F32)],
        compiler_params=_params(1))(proj, gqk)


def _qknorm_bwd(proj, gqk, dqs, dks, dvs, dproj, tm=512):
    S = proj.shape[0]
    W = 2 * ATT_W
    dil = [d for _, d in ATT_GROUPS]

    def body(p_ref, g_ref, *refs):
        ins = refs[0:9]
        o_ref, dg_ref = refs[10], refs[11]
        bufs = refs[12:21]
        i = pl.program_id(0)

        @pl.when(i == 0)
        def _():
            dg_ref[...] = jnp.zeros_like(dg_ref)

        nat = [_rm_reader(bufs[j], ins[j], dil[j % 3]) for j in range(9)]
        dq_get, dk_get, dv_get = nat[0:3], nat[3:6], nat[6:9]
        for hd in range(2 * ATT_HEADS):
            sl = slice(hd * HD, (hd + 1) * HD)
            head = hd % ATT_HEADS
            grp, slot = head // ATT_HPG, head % ATT_HPG
            get = (dq_get if hd < ATT_HEADS else dk_get)[grp]

            def chunk(rows, sl=sl, slot=slot, get=get):
                dn = get(slot, rows)
                v = p_ref[rows, sl].astype(F32)
                rstd = lax.rsqrt(jnp.mean(v * v, axis=-1, keepdims=True) + EPS)
                vh = v * rstd
                dg_ref[:, sl] += jnp.sum(dn * vh, axis=0, keepdims=True)
                dvh = dn * g_ref[:, sl]
                o_ref[rows, sl] = (rstd * (dvh - vh * jnp.mean(dvh * vh, axis=-1, keepdims=True))).astype(o_ref.dtype)

            chunk(slice(None))
        for head in range(ATT_HEADS):
            grp, slot = head // ATT_HPG, head % ATT_HPG
            o_ref[:, W + head * HD:W + (head + 1) * HD] = dv_get[grp](slot, slice(None)).astype(o_ref.dtype)

    return pl.pallas_call(
        body, name="qknorm_bwd", grid=(S // tm,),
        in_specs=[pl.BlockSpec((tm, W), lambda i: (i, 0)), pl.BlockSpec((1, W), lambda i: (0, 0))]
        + [_rm_spec(tm, d, GW) for d in dil] * 3 + [pl.BlockSpec(memory_space=pl.ANY)],
        out_specs=[pl.BlockSpec((tm, 3 * ATT_W), lambda i: (i, 0)), pl.BlockSpec((1, W), lambda i: (0, 0))],
        out_shape=[_sds(dproj.shape, dproj.dtype), _sds((1, W), F32)],
        scratch_shapes=[pltpu.VMEM((ATT_HPG, tm, HD), F32)] * 9,
        input_output_aliases={11: 0},
        compiler_params=_params(1))(proj, gqk, *dqs, *dks, *dvs, dproj)


def _att_mask(n):
    qi = lax.broadcasted_iota(jnp.int32, (BLK, 2 * BLK), 0)
    kj = lax.broadcasted_iota(jnp.int32, (BLK, 2 * BLK), 1)
    dist = BLK + qi - kj
    valid_all = (dist >= 0) & (dist <= BLK)
    return valid_all & ((kj >= BLK) | (n > 0)), valid_all, dist.astype(F32)


def _att_slopes(grp):
    return [2.0 ** (-8.0 * (grp * ATT_HPG + hh + 1) / ATT_HEADS) for hh in range(ATT_HPG)]


ATT_WIDTH = 8
ATT_BWD_SEGMENTS = 4


def _att_planes(d, width):
    return d if d > 1 else width


def _att_step_planes(d, width):
    return min(_att_planes(d, width), width)


def _att_3d(a, d, width):
    return a.reshape(width, a.shape[0] // width, a.shape[1]) if d == 1 else a


def _att_spec(R, row_fn, col=0):
    return pl.BlockSpec((R, BLK, GW), lambda r, n: (r, row_fn(n), col))


def _att_chains(R):
    return [(rr * ATT_HPG + hh, rr, slice(hh * HD, (hh + 1) * HD), hh) for rr in range(R) for hh in range(ATT_HPG)]


def _att_qkv_specs(R, nb):
    last = nb - 1

    def cur(n):
        return jnp.minimum(n, last)

    def prev(n):
        return jnp.maximum(jnp.minimum(n, last) - 1, 0)

    return [_att_spec(R, cur, 0), _att_spec(R, prev, 1), _att_spec(R, cur, 1), _att_spec(R, prev, 2),
            _att_spec(R, cur, 2), _att_spec(R, lambda n: last, 1), _att_spec(R, lambda n: last, 2)]


def _att_prev(seg, n, prev_ref, last_ref, rr, sl):
    t = prev_ref[rr, :, sl]
    if seg and rr > 0:
        t = jnp.where(n == 0, last_ref[rr - 1, :, sl], t)
    return t


def _att_fwd(grp, S, qkv):
    _, d = ATT_GROUPS[grp]
    seg = d == 1
    width = ATT_WIDTH
    P = _att_planes(d, width)
    L = S // P
    nb = L // BLK
    R = _att_step_planes(d, width)
    assert P % R == 0 and (not seg or P == R)
    slopes = _att_slopes(grp)
    scale = HD ** -0.5
    chains = _att_chains(R)

    def body(q_ref, kp_ref, kc_ref, vp_ref, vc_ref, kl_ref, vl_ref, o_ref, l_ref, s_buf, p_buf, den_buf):
        n = pl.program_id(1)
        valid, valid_all, distf = _att_mask(n)
        for c, rr, sl, hh in chains:
            k = jnp.concatenate([_att_prev(seg, n, kp_ref, kl_ref, rr, sl), kc_ref[rr, :, sl]], axis=0)
            s_buf[c] = _dot_nt(q_ref[rr, :, sl], k)
        for c, rr, sl, hh in chains:
            s = s_buf[c] * scale + (-slopes[hh] * d) * distf
            s = jnp.where(valid_all if seg and rr > 0 else valid, s, -1e30)
            m = jnp.max(s, axis=-1, keepdims=True)
            p = jnp.exp(s - m)
            den = jnp.sum(p, axis=-1, keepdims=True)
            p_buf[c] = p.astype(CDT)
            den_buf[c] = jnp.broadcast_to(den, (BLK, HD))
            l_ref[rr, :, sl] = jnp.broadcast_to(m + jnp.log(den), (BLK, HD))
        for c, rr, sl, hh in chains:
            v = jnp.concatenate([_att_prev(seg, n, vp_ref, vl_ref, rr, sl), vc_ref[rr, :, sl]], axis=0)
            o_ref[rr, :, sl] = _dot_nn(p_buf[c], v) / den_buf[c]

    out_spec = _att_spec(R, lambda n: n)
    n_ch = len(chains)
    q3 = _att_3d(qkv, d, width)
    o, l = pl.pallas_call(
        body, name="att_fwd_g%d" % grp, grid=(P // R, nb),
        in_specs=_att_qkv_specs(R, nb),
        out_specs=[out_spec, out_spec],
        out_shape=[_sds((P, L, GW), F32)] * 2,
        scratch_shapes=[pltpu.VMEM((n_ch, BLK, 2 * BLK), F32), pltpu.VMEM((n_ch, BLK, 2 * BLK), CDT),
                        pltpu.VMEM((n_ch, BLK, HD), F32)],
        compiler_params=_params(2),
    )(*[q3] * 7)
    return o.reshape(_rm_shape(S, d, GW)), l.reshape(_rm_shape(S, d, GW))


def _att_bwd(grp, S, qkv, lse, do_g, c_g):
    _, d = ATT_GROUPS[grp]
    seg = d == 1
    width = ATT_BWD_SEGMENTS if seg else ATT_WIDTH
    P = _att_planes(d, width)
    L = S // P
    nb = L // BLK
    R = _att_step_planes(d, width)
    assert P % R == 0 and (not seg or P == R)
    slopes = _att_slopes(grp)
    scale = HD ** -0.5
    last = nb - 1
    chains = _att_chains(R)

    def body(q_ref, kp_ref, kc_ref, vp_ref, vc_ref, kl_ref, vl_ref, l_ref, do_ref, c_ref, dq_ref, dk_ref, dv_ref,
             ck, cv, fk, fv, s_buf, dp_buf, p_buf, ds_buf):
        n = pl.program_id(1)

        @pl.when(n == 0)
        def _():
            for buf in (ck, cv, fk, fv):
                buf[...] = jnp.zeros_like(buf)

        @pl.when(n < nb)
        def _():
            valid, valid_all, distf = _att_mask(n)
            for c, rr, sl, hh in chains:
                k = jnp.concatenate([_att_prev(seg, n, kp_ref, kl_ref, rr, sl), kc_ref[rr, :, sl]], axis=0)
                v = jnp.concatenate([_att_prev(seg, n, vp_ref, vl_ref, rr, sl), vc_ref[rr, :, sl]], axis=0)
                s_buf[c] = _dot_nt(q_ref[rr, :, sl], k)
                dp_buf[c] = _dot_nt(do_ref[rr, :, sl], v)
            for c, rr, sl, hh in chains:
                s = s_buf[c] * scale + (-slopes[hh] * d) * distf
                p = jnp.where(valid_all if seg and rr > 0 else valid, jnp.exp(s - l_ref[rr, :, sl][:, 0:1]), 0.0)
                p_buf[c] = p.astype(CDT)
                ds_buf[c] = (p * (dp_buf[c] + c_ref[rr, :, sl][:, 0:1]) * scale).astype(CDT)
            for c, rr, sl, hh in chains:
                k = jnp.concatenate([_att_prev(seg, n, kp_ref, kl_ref, rr, sl), kc_ref[rr, :, sl]], axis=0)
                ds = ds_buf[c]
                dq_ref[rr, :, sl] = _dot_nn(ds, k)
                dk = _dot_tn(ds, q_ref[rr, :, sl])
                dv = _dot_tn(p_buf[c], do_ref[rr, :, sl])
                dk_ref[rr, :, sl] = ck[rr, :, sl] + dk[0:BLK]
                dv_ref[rr, :, sl] = cv[rr, :, sl] + dv[0:BLK]
                ck[rr, :, sl] = dk[BLK:2 * BLK]
                cv[rr, :, sl] = dv[BLK:2 * BLK]
                if seg and rr > 0:
                    @pl.when(n == 0)
                    def _(rr=rr, sl=sl, dk=dk, dv=dv):
                        fk[rr - 1, :, sl] = dk[0:BLK]
                        fv[rr - 1, :, sl] = dv[0:BLK]

        @pl.when(n == nb)
        def _():
            dk_ref[...] = ck[...] + fk[...]
            dv_ref[...] = cv[...] + fv[...]

    blk = (R, BLK, GW)
    at_q = _att_spec(R, lambda n: jnp.minimum(n, last))
    behind = _att_spec(R, lambda n: jnp.maximum(n - 1, 0))
    n_ch = len(chains)
    q3 = _att_3d(qkv, d, width)
    res = pl.pallas_call(
        body, name="att_bwd_g%d" % grp, grid=(P // R, nb + 1),
        in_specs=_att_qkv_specs(R, nb) + [at_q, at_q, at_q],
        out_specs=[at_q, behind, behind],
        out_shape=[_sds((P, L, GW), F32)] * 3,
        scratch_shapes=[pltpu.VMEM(blk, F32)] * 4
        + [pltpu.VMEM((n_ch, BLK, 2 * BLK), F32), pltpu.VMEM((n_ch, BLK, 2 * BLK), F32),
           pltpu.VMEM((n_ch, BLK, 2 * BLK), CDT), pltpu.VMEM((n_ch, BLK, 2 * BLK), CDT)],
        compiler_params=_params(2),
    )(*[q3] * 7, _att_3d(lse, d, width), _att_3d(do_g, d, width), _att_3d(c_g, d, width))
    return [t.reshape(_rm_shape(S, d, GW)) for t in res]


def _mix_alpha(l0, l1, l2):
    mx = jnp.maximum(jnp.maximum(l0, l1), l2)
    e = [jnp.exp(l0 - mx), jnp.exp(l1 - mx), jnp.exp(l2 - mx)]
    tot = e[0] + e[1] + e[2]
    return [ei / tot for ei in e]


def _mix_fwd(S, os_, ls_, tm=512):
    dil = [d for _, d in ATT_GROUPS]

    def body(*refs):
        out, bufs = refs[6], refs[7:13]
        get = [_rm_reader(bufs[j], refs[j], dil[j % 3]) for j in range(6)]
        rows = slice(None)
        for s in range(ATT_HPG):
            al = _mix_alpha(*[get[3 + g](s, rows) for g in range(3)])
            mixed = al[0] * get[0](s, rows) + al[1] * get[1](s, rows) + al[2] * get[2](s, rows)
            out[:, s * HD:(s + 1) * HD] = mixed.astype(out.dtype)

    specs = [_rm_spec(tm, d, GW) for d in dil]
    return pl.pallas_call(
        body, name="mix_fwd", grid=(S // tm,), in_specs=specs * 2, out_specs=pl.BlockSpec((tm, GW), lambda i: (i, 0)),
        out_shape=_sds((S, GW), CDT), scratch_shapes=[pltpu.VMEM((ATT_HPG, tm, HD), F32)] * 6,
        compiler_params=_params(1))(*os_, *ls_)


def _mix_bwd(S, os_, ls_, do_a, tm=512):
    dil = [d for _, d in ATT_GROUPS]

    def body(*refs):
        d_ref, outs, bufs, tmps = refs[6], refs[7:13], refs[13:19], refs[19:25]
        get = [_rm_reader(bufs[j], refs[j], dil[j % 3]) for j in range(6)]
        for s in range(ATT_HPG):
            cols = slice(s * HD, (s + 1) * HD)

            def chunk(rows, s=s, cols=cols):
                al = _mix_alpha(*[get[3 + g](s, rows) for g in range(3)])
                dv = d_ref[rows, cols]
                o_a = al[0] * get[0](s, rows) + al[1] * get[1](s, rows) + al[2] * get[2](s, rows)
                dsum = jnp.sum(dv * o_a, axis=-1, keepdims=True)
                for g in range(3):
                    tmps[g][rows, :] = al[g] * dv
                    tmps[3 + g][rows, :] = -(al[g] * dsum)

            chunk(slice(None))
            for j in range(6):
                _rm_put(outs[j], cols, tmps[j], dil[j % 3])

    specs = [_rm_spec(tm, d, GW) for d in dil]
    res = pl.pallas_call(
        body, name="mix_bwd", grid=(S // tm,), in_specs=specs * 2 + [pl.BlockSpec((tm, GW), lambda i: (i, 0))],
        out_specs=specs * 2,
        out_shape=[_sds(_rm_shape(S, d, GW), CDT) for d in dil] + [_sds(_rm_shape(S, d, GW), F32) for d in dil],
        scratch_shapes=[pltpu.VMEM((ATT_HPG, tm, HD), F32)] * 6 + [pltpu.VMEM((tm, HD), F32)] * 6,
        compiler_params=_params(1))(*os_, *ls_, do_a)
    return res[:3], res[3:]


def _ret_tables(dk):
    H, C = RET_HEADS, BLK
    log_g = jnp.log(1.0 - 2.0 ** (-5.0 - jnp.arange(H, dtype=F32)))
    idx = jnp.arange(C, dtype=F32)
    diff = idx[:, None] - idx[None, :]
    decay = jnp.where(diff >= 0, jnp.exp(log_g[:, None, None] * jnp.maximum(diff, 0.0)), 0.0)
    xi = jnp.exp(log_g[:, None] * (idx[None, :] + 1.0))
    zeta = jnp.exp(log_g[:, None] * (C - 1.0 - idx[None, :])) * (dk ** -0.5)
    g_chunk = jnp.exp(log_g * C)
    bc = lambda t: jnp.broadcast_to(t[:, :, None], (H, C, C))
    return decay, bc(xi), bc(zeta), jnp.broadcast_to(g_chunk[:, None, None], (H, 8, C))


def _gn_fwd(o, g, b):
    mu = jnp.mean(o, axis=-1, keepdims=True)
    xc = o - mu
    rstd = lax.rsqrt(jnp.mean(xc * xc, axis=-1, keepdims=True) + EPS)
    yh = xc * rstd
    return yh, rstd, yh * g + b


def _ret_specs(dk, dv, order):
    H = RET_HEADS
    qk_w, v_w = H * dk, H * dv
    off_q = 3 * ATT_W
    off_k, off_v, off_g = off_q + qk_w, off_q + 2 * qk_w, off_q + 2 * qk_w + v_w
    assert 2 * dk == dv and all(off % dv == 0 for off in (off_q, off_k, off_v, off_g))

    def col(off, j):
        return pl.BlockSpec((BLK, dv), lambda i: (order(i), off // dv + j))

    tab = pl.BlockSpec((H, BLK, BLK), lambda i: (0, 0, 0))
    return ([col(off_q, j) for j in range(H // 2)] + [col(off_k, j) for j in range(H // 2)]
            + [col(off_v, j) for j in range(H)] + [col(off_g, j) for j in range(H)]
            + [tab, tab, tab, pl.BlockSpec((H, 8, BLK), lambda i: (0, 0, 0))])


def _ret_heads(refs, dk):
    H = RET_HEADS
    q_refs, k_refs = refs[0:H // 2], refs[H // 2:H]
    v_refs, gr_refs = refs[H:2 * H], refs[2 * H:3 * H]

    def head(h):
        cols = slice((h % 2) * dk, (h % 2 + 1) * dk)
        return q_refs[h // 2][:, cols], k_refs[h // 2][:, cols], v_refs[h][...], gr_refs[h][...]

    return head, refs[3 * H:3 * H + 4]


def _ret_fwd(proj, gn_g, gn_b, dk, dv):
    S = proj.shape[0]
    N = S // BLK
    H = RET_HEADS
    kscale = dk ** -0.5
    n_in = 3 * H + 4

    def body(*refs):
        head, (dec_ref, xi_ref, zeta_ref, gc_ref) = _ret_heads(refs, dk)
        g_ref, b_ref, opre_ref, or_ref, st_ref, state, s_buf, cross_buf = refs[n_in:n_in + 8]
        n = pl.program_id(0)

        @pl.when(n == 0)
        def _():
            state[...] = jnp.zeros_like(state)

        for h in range(H):
            q, k, v, _ = head(h)
            s_buf[h] = _dot_nt(q, k)
            st = state[h]
            st_c = st.astype(CDT)
            st_ref[h] = st_c
            cross_buf[h] = _dot_nn(q, st_c)
            kz = (k.astype(F32) * zeta_ref[h][:, 0:1]).astype(CDT)
            state[h] = st * gc_ref[h][0:1, 0:1] + _dot_tn(kz, v)
        for h in range(H):
            vs = slice(h * dv, (h + 1) * dv)
            _, _, v, gr = head(h)
            s = s_buf[h] * kscale * dec_ref[h]
            o = _dot_nn(s.astype(CDT), v) + cross_buf[h] * xi_ref[h][:, 0:1]
            opre_ref[:, vs] = o
            _, _, y = _gn_fwd(o, g_ref[:, vs], b_ref[:, vs])
            gr = gr.astype(F32)
            or_ref[:, vs] = (y * (gr * _sigmoid(gr))).astype(or_ref.dtype)

    v_w = H * dv
    row = pl.BlockSpec((1, v_w), lambda i: (0, 0))
    tile = pl.BlockSpec((BLK, v_w), lambda i: (i, 0))
    return pl.pallas_call(
        body, name="ret_fwd", grid=(N,),
        in_specs=_ret_specs(dk, dv, lambda i: i) + [row, row],
        out_specs=[tile, tile, pl.BlockSpec((None, H, dk, dv), lambda i: (i, 0, 0, 0))],
        out_shape=[_sds((S, v_w), F32), _sds((S, v_w), CDT), _sds((N, H, dk, dv), CDT)],
        scratch_shapes=[pltpu.VMEM((H, dk, dv), F32), pltpu.VMEM((H, BLK, BLK), F32), pltpu.VMEM((H, BLK, dv), F32)],
        compiler_params=_params(1),
    )(*[proj] * (3 * H), *_ret_tables(dk), gn_g, gn_b)


def _ret_bwd(proj, gn_g, gn_b, o_pre, states, d_or, dga, dgb, dk, dv):
    S, in_w = proj.shape
    N = S // BLK
    H = RET_HEADS
    qk_w, v_w = H * dk, H * dv
    kscale = dk ** -0.5
    n_in = 3 * H + 4
    out_w = 2 * qk_w + 2 * v_w
    gate_w = dga.shape[1]
    col0 = 3 * ATT_W
    assert col0 + out_w + 2 * gate_w == in_w
    rev = lambda i: N - 1 - i

    def body(*refs):
        head, (dec_ref, xi_ref, zeta_ref, gc_ref) = _ret_heads(refs, dk)
        (g_ref, b_ref, opre_ref, st_ref, dor_ref, dga_ref, dgb_ref, dproj_ref, dg_ref, db_ref, dstate, stage,
         sem, do_buf, dox_buf, a_buf, g_buf, dq_buf, dk_buf, dv_buf) = refs[n_in:n_in + 20]
        i = pl.program_id(0)
        slot = i % 2
        out_ref = stage.at[slot]

        def out_copy(s, step):
            rows = pl.ds(pl.multiple_of(rev(step) * BLK, BLK), BLK)
            return pltpu.make_async_copy(stage.at[s], dproj_ref.at[rows, pl.ds(col0, in_w - col0)], sem.at[s])

        @pl.when(i >= 2)
        def _():
            out_copy(slot, i - 2).wait()

        @pl.when(i == 0)
        def _():
            dstate[...] = jnp.zeros_like(dstate)
            dg_ref[...] = jnp.zeros_like(dg_ref)
            db_ref[...] = jnp.zeros_like(db_ref)

        out_ref[:, out_w:out_w + gate_w] = dga_ref[...]
        out_ref[:, out_w + gate_w:out_w + 2 * gate_w] = dgb_ref[...]
        for h in range(H):
            vs = slice(h * dv, (h + 1) * dv)
            _, _, _, gr = head(h)
            gr = gr.astype(F32)
            sg = _sigmoid(gr)
            gain = g_ref[:, vs]
            yh, rstd, y = _gn_fwd(opre_ref[:, vs], gain, b_ref[:, vs])
            d_or_v = dor_ref[:, vs]
            dy = d_or_v * (gr * sg)
            out_ref[:, 2 * qk_w + v_w + h * dv:2 * qk_w + v_w + (h + 1) * dv] = (
                d_or_v * y * (sg * (1.0 + gr * (1.0 - sg)))).astype(out_ref.dtype)
            dg_ref[:, vs] += jnp.sum(dy * yh, axis=0, keepdims=True)
            db_ref[:, vs] += jnp.sum(dy, axis=0, keepdims=True)
            dyh = dy * gain
            do = rstd * (dyh - jnp.mean(dyh, axis=-1, keepdims=True)
                         - yh * jnp.mean(dyh * yh, axis=-1, keepdims=True))
            do_buf[h] = do.astype(CDT)
            dox_buf[h] = (do * xi_ref[h][:, 0:1]).astype(CDT)
        for h in range(H):
            q, k, v, _ = head(h)
            dox = dox_buf[h]
            a_buf[h] = _dot_nt(q, k)
            g_buf[h] = _dot_nt(do_buf[h], v)
            dsn = dstate[h]
            dsn_c = dsn.astype(CDT)
            kz = (k.astype(F32) * zeta_ref[h][:, 0:1]).astype(CDT)
            dq_buf[h] = _dot_nt(dox, st_ref[h])
            dk_buf[h] = _dot_nt(v, dsn_c)
            dv_buf[h] = _dot_nn(kz, dsn_c)
            dstate[h] = dsn * gc_ref[h][0:1, 0:1] + _dot_tn(q, dox)
        for h in range(H):
            q, k, _, _ = head(h)
            decay = dec_ref[h]
            a_c = (a_buf[h] * kscale * decay).astype(CDT)
            g_c = (g_buf[h] * decay).astype(CDT)
            dq = _dot_nn(g_c, k) * kscale + dq_buf[h]
            dkk = _dot_tn(g_c, q) * kscale + dk_buf[h] * zeta_ref[h][:, 0:1]
            dvv = _dot_tn(a_c, do_buf[h]) + dv_buf[h]
            out_ref[:, h * dk:(h + 1) * dk] = dq.astype(out_ref.dtype)
            out_ref[:, qk_w + h * dk:qk_w + (h + 1) * dk] = dkk.astype(out_ref.dtype)
            out_ref[:, 2 * qk_w + h * dv:2 * qk_w + (h + 1) * dv] = dvv.astype(out_ref.dtype)

        cp = out_copy(slot, i)
        cp.start()

        @pl.when(i == N - 1)
        def _():
            cp.wait()
            if N >= 2:
                out_copy(1 - slot, i - 1).wait()

    row = pl.BlockSpec((1, v_w), lambda i: (0, 0))
    tile = pl.BlockSpec((BLK, v_w), lambda i: (rev(i), 0))
    gate = pl.BlockSpec((BLK, gate_w), lambda i: (rev(i), 0))
    return pl.pallas_call(
        body, name="ret_bwd", grid=(N,),
        in_specs=_ret_specs(dk, dv, rev) + [row, row, tile,
                 pl.BlockSpec((None, H, dk, dv), lambda i: (rev(i), 0, 0, 0)), tile, gate, gate],
        out_specs=[pl.BlockSpec(memory_space=pl.ANY), row, row],
        out_shape=[_sds((S, in_w), CDT), _sds((1, v_w), F32), _sds((1, v_w), F32)],
        scratch_shapes=[pltpu.VMEM((H, dk, dv), F32), pltpu.VMEM((2, BLK, in_w - col0), CDT),
                        pltpu.SemaphoreType.DMA((2,)),
                        pltpu.VMEM((H, BLK, dv), CDT), pltpu.VMEM((H, BLK, dv), CDT),
                        pltpu.VMEM((H, BLK, BLK), F32), pltpu.VMEM((H, BLK, BLK), F32),
                        pltpu.VMEM((H, BLK, dk), F32), pltpu.VMEM((H, BLK, dk), F32), pltpu.VMEM((H, BLK, dv), F32)],
        compiler_params=_params(1),
    )(*[proj] * (3 * H), *_ret_tables(dk), gn_g, gn_b, o_pre, states, d_or, dga, dgb)


def _merge_fwd(o_a, o_r, wa, wb, proj, d_model, tm=1024, tn=512):
    S, in_w = proj.shape
    off_a, off_b = in_w - 2 * d_model, in_w - d_model
    assert off_a % tn == 0 and off_b % tn == 0

    def body(oa_ref, or_ref, wa_ref, wb_ref, ga_ref, gb_ref, y_ref, pa_ref, pb_ref):
        pa = _dot_nn(oa_ref[...], wa_ref[...])
        pb = _dot_nn(or_ref[...], wb_ref[...])
        y = _sigmoid(ga_ref[...].astype(F32)) * pa + _sigmoid(gb_ref[...].astype(F32)) * pb
        y_ref[...] = y.astype(y_ref.dtype)
        pa_ref[...] = pa.astype(pa_ref.dtype)
        pb_ref[...] = pb.astype(pb_ref.dtype)

    ka, kb = o_a.shape[1], o_r.shape[1]
    out = pl.BlockSpec((tm, tn), lambda i, j: (i, j))
    return pl.pallas_call(
        body, name="merge_fwd", grid=(S // tm, d_model // tn),
        in_specs=[pl.BlockSpec((tm, ka), lambda i, j: (i, 0)), pl.BlockSpec((tm, kb), lambda i, j: (i, 0)),
                  pl.BlockSpec((ka, tn), lambda i, j: (0, j)), pl.BlockSpec((kb, tn), lambda i, j: (0, j)),
                  pl.BlockSpec((tm, tn), lambda i, j: (i, off_a // tn + j)),
                  pl.BlockSpec((tm, tn), lambda i, j: (i, off_b // tn + j))],
        out_specs=[out, out, out], out_shape=[_sds((S, d_model), CDT)] * 3,
        compiler_params=_params(2))(o_a, o_r, wa, wb, proj, proj)


def _d_x(dproj, w_in, x, g, dx1, dep, tm=512, row_groups=2):
    S, D = x.shape
    n_sh, _, ns = w_in.shape
    nt = S // tm // row_groups
    deps = [d for d in (dep,) if d is not None]

    def body(a_ref, b_ref, x_ref, g_ref, r_ref, *rest):
        dx_ref, dg_ref, acc = rest[len(deps):]
        h, k, i = pl.program_id(0), pl.program_id(1), pl.program_id(2)

        @pl.when(k == 0)
        def _():
            acc[i] = jnp.zeros((tm, D), F32)

        acc[i] += _dot_nt(a_ref[...], b_ref[...])

        @pl.when(k == n_sh - 1)
        def _():
            _ep_rms_bwd(acc[i], (x_ref, g_ref, r_ref), (dx_ref, dg_ref), h * nt + i)

    def last_only(h, k, i):
        return (h * nt + jnp.where(k == n_sh - 1, i, 0), 0)

    return pl.pallas_call(
        body, name="d_x", grid=(row_groups, n_sh, nt),
        in_specs=[pl.BlockSpec((tm, ns), lambda h, k, i: (h * nt + i, k)),
                  pl.BlockSpec((None, D, ns), lambda h, k, i: (k, 0, 0)),
                  pl.BlockSpec((tm, D), last_only), pl.BlockSpec((1, D), lambda h, k, i: (0, 0)),
                  pl.BlockSpec((tm, D), last_only)] + [pl.BlockSpec(memory_space=pl.ANY)] * len(deps),
        out_specs=[pl.BlockSpec((tm, D), last_only), pl.BlockSpec((1, D), lambda h, k, i: (0, 0))],
        out_shape=[_sds((S, D), F32), _sds((1, D), F32)],
        scratch_shapes=[pltpu.VMEM((nt, tm, D), F32)],
        compiler_params=_params(3))(dproj, w_in, x, g, dx1, *deps)


def _local_step(x, target, w_in_mine, chip, others, near_w_in, far_w_in, small, late_weights, on_grads, deps0=()):
    S, D = x.shape
    ns_in = w_in_mine.shape[1]
    in_w = N_CHIPS * ns_in
    d_ff = 4 * D
    ret_v_w = 2 * D
    dv = ret_v_w // RET_HEADS
    dk = (in_w - 3 * ATT_W - 2 * ret_v_w - 2 * D) // (2 * RET_HEADS)
    gqk = jnp.concatenate([small["q_norm_g"].reshape(1, ATT_W), small["k_norm_g"].reshape(1, ATT_W)], axis=1)
    g1, g2 = small["norm1_g"], small["norm2_g"]
    gn_g, gn_b = small["ret_gn_g"], small["ret_gn_b"]

    proj_sds = _sds((S, in_w), CDT)
    xn, proj = _in_proj_mine(x, g1, w_in_mine, chip, proj_sds, deps0)
    for stage, (get_w_in, chips) in enumerate(((near_w_in, others[:2]), (far_w_in, others[2:]))):
        w_in, started = get_w_in(proj)
        (proj,) = _matmul(
            "in_proj_far%d" % stage, "nn", xn, w_in, tm=1024, tn=ns_in, tk=D, prefetch=[chips],
            n_cols=chips.shape[0] * ns_in, b_spec=pl.BlockSpec((None, D, ns_in), lambda i, j, k, o: (o[j], 0, 0)),
            outs=[(proj_sds, pl.BlockSpec((1024, ns_in), lambda i, j, k, o: (i, o[j])))], epilogue=_ep_store,
            deps=[proj, started], alias_dep_to_out=(0, 0), j_outer=True)
    qkv = _qknorm_fwd(proj, gqk)
    att = [_att_fwd(g, S, qkv[g]) for g in range(3)]
    os_, ls_ = [a[0] for a in att], [a[1] for a in att]
    o_a = _mix_fwd(S, os_, ls_)
    o_pre, o_r, states = _ret_fwd(proj, gn_g, gn_b, dk, dv)
    w = late_weights(o_r)
    y, pa, pb = _merge_fwd(o_a, o_r, w["w_proj_a"], w["w_proj_b"], proj, D)
    x1, xn2 = _matmul("out_proj", "nn", y, w["w_out"], tm=1024, tn=D, tk=D,
                      extras=[(x, _mn(1024, D)), (g2, _row(D))],
                      outs=[(_sds((S, D), F32), _mn(1024, D)), (_sds((S, D), CDT), _mn(1024, D))],
                      epilogue=_ep_resid_norm)
    hid, act = _matmul("mlp_up", "nn", xn2, w["w_up"], tm=1024, tn=2048, tk=D, j_outer=True,
                       outs=[(_sds((S, d_ff), CDT), _mn(1024, 2048))] * 2, epilogue=_ep_up)
    dx2, dx2c, loss_row = _matmul(
        "mlp_down_loss", "nn", act, w["w_down"], tm=512, tn=D, tk=d_ff,
        extras=[(x1, _mn(512, D)), (target, _mn(512, D))],
        outs=[(_sds((S, D), F32), _mn(512, D)), (_sds((S, D), CDT), _mn(512, D)), (_sds((1, D), F32), _row(D))],
        epilogue=functools.partial(_ep_down_loss, inv_d=1.0 / D))
    loss = 0.5 * jnp.sum(loss_row) / D

    (dh,) = _matmul("d_hidden", "nt", dx2c, w["w_down"], tm=1024, tn=2048, tk=D, j_outer=True,
                    extras=[(hid, _mn(1024, 2048))], outs=[(_sds((S, d_ff), CDT), _mn(1024, 2048))],
                    epilogue=_ep_dh)
    (gw_down,) = _matmul("dw_down", "tn", act, dx2c, tm=1024, tn=D, tk=2048,
                         outs=[(_sds((d_ff, D), F32), _mn(1024, D))], epilogue=_ep_store)
    (gw_up,) = _matmul("dw_up", "tn", xn2, dh, tm=D, tn=1024, tk=2048,
                       outs=[(_sds((D, d_ff), F32), _mn(D, 1024))], epilogue=_ep_store)
    tok = on_grads({"w_down": gw_down, "w_up": gw_up})
    dx1, dx1c, dg2 = _matmul(
        "d_x1", "nt", dh, w["w_up"], tm=512, tn=D, tk=d_ff,
        extras=[(x1, _mn(512, D)), (g2, _row(D)), (dx2, _mn(512, D))],
        outs=[(_sds((S, D), F32), _mn(512, D)), (_sds((S, D), CDT), _mn(512, D)), (_sds((1, D), F32), _row(D))],
        epilogue=_ep_rms_bwd, deps=[tok])

    gt = 512
    assert (in_w - 2 * D) % gt == 0
    off_a, off_b = (in_w - 2 * D) // gt, (in_w - D) // gt
    dpa, dpb, dga, dgb = _matmul(
        "d_gates", "nt", dx1c, w["w_out"], tm=1024, tn=gt, tk=D,
        extras=[(proj, _mn(1024, gt, off_a)), (proj, _mn(1024, gt, off_b)), (pa, _mn(1024, gt)),
                (pb, _mn(1024, gt))],
        outs=[(_sds((S, D), CDT), _mn(1024, gt))] * 4, epilogue=_ep_gates)
    (gw_out,) = _matmul("dw_out", "tn", y, dx1c, tm=D, tn=D, tk=1024,
                        outs=[(_sds((D, D), F32), _mn(D, D))], epilogue=_ep_store)
    (gw_pa,) = _matmul("dw_proj_a", "tn", o_a, dpa, tm=GW, tn=D, tk=1024,
                       outs=[(_sds((GW, D), F32), _mn(GW, D))], epilogue=_ep_store)
    (gw_pb,) = _matmul("dw_proj_b", "tn", o_r, dpb, tm=1024, tn=D, tk=2048,
                       outs=[(_sds((ret_v_w, D), F32), _mn(1024, D))], epilogue=_ep_store)
    (do_a,) = _matmul("d_o_a", "nt", dpa, w["w_proj_a"], tm=1024, tn=GW, tk=D,
                      outs=[(_sds((S, GW), F32), _mn(1024, GW))], epilogue=_ep_store)
    tok = on_grads({"w_out": gw_out, "w_proj_a": gw_pa, "w_proj_b": gw_pb})
    (d_or,) = _matmul("d_o_r", "nt", dpb, w["w_proj_b"], tm=1024, tn=ret_v_w, tk=D,
                      outs=[(_sds((S, ret_v_w), F32), _mn(1024, ret_v_w))], epilogue=_ep_store, deps=[tok])

    dproj, dgn_g, dgn_b = _ret_bwd(proj, gn_g, gn_b, o_pre, states, d_or, dga, dgb, dk, dv)
    do_gs, c_gs = _mix_bwd(S, os_, ls_, do_a)
    datt_parts = [_att_bwd(g, S, qkv[g], ls_[g], do_gs[g], c_gs[g]) for g in range(3)]
    dproj, dgqk = _qknorm_bwd(proj, gqk, [p[0] for p in datt_parts], [p[1] for p in datt_parts],
                              [p[2] for p in datt_parts], dproj)

    (gw_in,) = _matmul(
        "dw_in", "tn", xn, dproj, tm=512, tn=ns_in, tk=1024,
        outs=[(_sds((N_CHIPS, D, ns_in), F32), pl.BlockSpec((None, 512, ns_in), lambda i, j, k: (j, i, 0)))],
        epilogue=_ep_store)
    tok = on_grads({"w_in": gw_in})
    grad_x, dg1 = _d_x(dproj, w_in, x, g1, dx1, tok)

    smallg = {"norm1_g": dg1, "q_norm_g": dgqk[:, :ATT_W], "k_norm_g": dgqk[:, ATT_W:],
              "ret_gn_g": dgn_g, "ret_gn_b": dgn_b, "norm2_g": dg2}
    return loss, grad_x, smallg


N_CHIPS = 4
N_DEV = 8


def _place():
    x, y, c = lax.axis_index("x"), lax.axis_index("y"), lax.axis_index("c")
    return x, y, c


def _other_chips(x, y):
    out = []
    for fx, fy in ((1, 0), (0, 1), (1, 1)):
        px = 1 - x if fx else x
        py = 1 - y if fy else y
        out.append(((px, py), 2 * px + py))
    return out


SEM_SPEC = pl.BlockSpec(memory_space=pltpu.SEMAPHORE)
ANY_SPEC = pl.BlockSpec(memory_space=pl.ANY)
EFFECT = pltpu.SideEffectType.DATAFLOW_SIDE_EFFECTING


def _ici_copies(kind, srcs, lands, send, recv, which=(0, 1, 2)):
    x, y, c = _place()
    me = 2 * x + y
    out = []
    for w, (s, l) in enumerate(zip(srcs, lands)):
        for j, ((px, py), pidx) in enumerate(_other_chips(x, y)):
            if j not in which:
                continue
            if kind == "gather":
                half = s.shape[0] // 2
                rows = pl.ds(c * half, half)
                src, dst_there, dst_here = s.at[rows, :], l.at[me, rows, :], l.at[pidx, rows, :]
            else:
                src, dst_there, dst_here = s.at[pidx], l.at[me], l.at[pidx]
            out.append((src, dst_there, dst_here, send.at[3 * w + j], recv.at[3 * w + j], (px, py, c)))
    return out


def _exchange_start(name, kind, srcs, land_shapes, which=(0, 1, 2), lands=None):
    n = len(srcs)
    if lands is None:
        lands = [lax.empty(shape, dtype) for shape, dtype in land_shapes]

    def body(*refs):
        src_refs, land_refs = refs[:n], refs[n:2 * n]
        send, recv = refs[2 * n], refs[2 * n + 1]
        token = refs[-1]
        for src, dst, _, ss, rs, dev in _ici_copies(kind, src_refs, land_refs, send, recv, which):
            pltpu.make_async_remote_copy(src_ref=src, dst_ref=dst, send_sem=ss, recv_sem=rs, device_id=dev,
                                         device_id_type=MESH).start()
        token[...] = jnp.zeros_like(token)

    thru = [pltpu.HBM(s.shape, s.dtype) for s in srcs] + [pltpu.HBM(shape, dtype) for shape, dtype in land_shapes]
    res = pl.pallas_call(
        body, name=name,
        out_shape=(pltpu.SemaphoreType.DMA((3 * n,)), pltpu.SemaphoreType.DMA((3 * n,)), *thru, _sds((8, LANES), F32)),
        in_specs=[HBM_SPEC] * (2 * n), out_specs=(SEM_SPEC, SEM_SPEC, *[HBM_SPEC] * (2 * n), VMEM_SPEC),
        input_output_aliases={i: 2 + i for i in range(2 * n)},
        compiler_params=pltpu.CompilerParams(has_side_effects=EFFECT),
    )(*[pltpu.with_memory_space_constraint(s, pltpu.HBM) for s in srcs],
      *[pltpu.with_memory_space_constraint(l, pltpu.HBM) for l in lands])
    return res[0], res[1], list(res[2:2 + n]), list(res[2 + n:2 + 2 * n]), res[-1]


def _exchange_wait(name, kind, send, recv, srcs, lands, after, which=(0, 1, 2)):
    n = len(srcs)

    def body(*refs):
        src_refs, land_refs = refs[:n], refs[n:2 * n]
        send_ref, recv_ref = refs[2 * n], refs[2 * n + 1]
        for src, _, dst, ss, rs, dev in _ici_copies(kind, src_refs, land_refs, send_ref, recv_ref, which):
            cp = pltpu.make_async_remote_copy(src_ref=src, dst_ref=dst, send_sem=ss, recv_sem=rs, device_id=dev,
                                              device_id_type=MESH)
            cp.wait_send()
            cp.wait_recv()

    thru = [pltpu.HBM(t.shape, t.dtype) for t in list(srcs) + list(lands)]
    res = pl.pallas_call(
        body, name=name, out_shape=thru,
        in_specs=[HBM_SPEC] * (2 * n) + [SEM_SPEC, SEM_SPEC, ANY_SPEC], out_specs=[HBM_SPEC] * (2 * n),
        input_output_aliases={i: i for i in range(2 * n)},
        compiler_params=pltpu.CompilerParams(has_side_effects=EFFECT),
    )(*srcs, *lands, send, recv, after)
    return list(res[:n]), list(res[n:])


PAIR_TILE_ELEMS = 1 << 20


def _pair_fill(name, gathered, mine, core, others, chip, write_mine=True):
    k, r, C = gathered.shape
    half = r // 2
    tr = _row_tile(half, C, PAIR_TILE_ELEMS, mult=16)
    nt = half // tr
    n_far = others.shape[0]

    def body(c_ref, o_ref, chip_ref, in_ref, mine_ref, out_ref, slot, send, recv):
        j = pl.program_id(0)
        _sibling_barrier((j == 0) & (pl.program_id(1) == 0))
        b = (j * nt + pl.program_id(1)) % 2
        x, y, c = _place()
        cp = pltpu.make_async_remote_copy(src_ref=in_ref, dst_ref=slot.at[b], send_sem=send.at[b],
                                          recv_sem=recv.at[b], device_id=(x, y, 1 - c), device_id_type=MESH)

        @pl.when(j < n_far)
        def _():
            cp.start()
            cp.wait_recv()
            out_ref[...] = slot[b]
            cp.wait_send()

        @pl.when(j >= n_far)
        def _():
            out_ref[...] = mine_ref[...]

    def far(j):
        return jnp.minimum(j, n_far - 1)

    grid_spec = pltpu.PrefetchScalarGridSpec(
        num_scalar_prefetch=3, grid=(n_far + (2 if write_mine else 0), nt),
        in_specs=[pl.BlockSpec((tr, C), lambda j, i, c, o, m: (
                      (2 * o[far(j)] + c[0]) * nt + jnp.where(j < n_far, i, nt - 1), 0)),
                  pl.BlockSpec((tr, C), lambda j, i, c, o, m: (jnp.where(j < n_far, 0, (j - n_far) * nt + i), 0))],
        out_specs=pl.BlockSpec((tr, C), lambda j, i, c, o, m: (
            jnp.where(j < n_far, 2 * o[far(j)] + 1 - c[0], 2 * m[0] + j - n_far) * nt + i, 0)),
        scratch_shapes=[pltpu.VMEM((2, tr, C), gathered.dtype), pltpu.SemaphoreType.DMA((2,)),
                        pltpu.SemaphoreType.DMA((2,))])
    out = pl.pallas_call(body, name=name, grid_spec=grid_spec, out_shape=_sds((k * r, C), gathered.dtype),
                         input_output_aliases={3: 0}, compiler_params=_params(2, PAIR_FILL_ID))(
                             core, others, chip, gathered.reshape(k * r, C), mine)
    return out.reshape(k, r, C)


def _pair_reduce(name, g, core):
    k, R, C = g.shape
    half = R // 2
    tr = _row_tile(half, C, PAIR_TILE_ELEMS, mult=16)
    nt = half // tr

    def body(c_ref, mine_ref, give_ref, out_ref, wire_ref, stage, slot, send, recv):
        _sibling_barrier((pl.program_id(0) == 0) & (pl.program_id(1) == 0))
        b = (pl.program_id(0) * nt + pl.program_id(1)) % 2
        x, y, c = _place()
        stage[b] = give_ref[...].astype(stage.dtype)
        cp = pltpu.make_async_remote_copy(src_ref=stage.at[b], dst_ref=slot.at[b], send_sem=send.at[b],
                                          recv_sem=recv.at[b], device_id=(x, y, 1 - c), device_id_type=MESH)
        cp.start()
        cp.wait_recv()
        tot = mine_ref[...] + slot[b].astype(F32)
        out_ref[...] = tot
        wire_ref[...] = tot.astype(wire_ref.dtype)
        cp.wait_send()

    blk = (tr, C)
    out_spec = pl.BlockSpec(blk, lambda s, i, c: (s * nt + i, 0))
    grid_spec = pltpu.PrefetchScalarGridSpec(
        num_scalar_prefetch=1, grid=(k, nt),
        in_specs=[pl.BlockSpec(blk, lambda s, i, c: ((2 * s + c[0]) * nt + i, 0)),
                  pl.BlockSpec(blk, lambda s, i, c: ((2 * s + 1 - c[0]) * nt + i, 0))],
        out_specs=[out_spec, out_spec],
        scratch_shapes=[pltpu.VMEM((2, tr, C), CDT), pltpu.VMEM((2, tr, C), CDT), pltpu.SemaphoreType.DMA((2,)),
                        pltpu.SemaphoreType.DMA((2,))])
    g2 = g.reshape(k * R, C)
    out, wire = pl.pallas_call(body, name=name, grid_spec=grid_spec,
                               out_shape=[_sds((k * half, C), F32), _sds((k * half, C), CDT)],
                               compiler_params=_params(2, PAIR_REDUCE_ID))(core, g2, g2)
    return out, wire.reshape(k, half, C)


def _all_reduce_small(v):
    r, cdim = v.shape

    def body(v_ref, o_ref, buf, send, recv):
        x, y, c = _place()
        me = 4 * x + 2 * y + c
        buf[me] = v_ref[...]
        sends = []
        for m in range(1, N_DEV):
            px = 1 - x if m & 4 else x
            py = 1 - y if m & 2 else y
            pc = 1 - c if m & 1 else c
            cp = pltpu.make_async_remote_copy(src_ref=v_ref, dst_ref=buf.at[me], send_sem=send.at[m - 1],
                                              recv_sem=recv.at[m - 1], device_id=(px, py, pc), device_id_type=MESH)
            cp.start()
            sends.append((cp, 4 * px + 2 * py + pc))
        for m, (cp, pidx) in enumerate(sends):
            pltpu.make_async_remote_copy(src_ref=v_ref, dst_ref=buf.at[pidx], send_sem=send.at[m], recv_sem=recv.at[m],
                                         device_id=(x, y, c), device_id_type=MESH).wait_recv()
        for cp, _ in sends:
            cp.wait_send()
        tot = buf[0]
        for k in range(1, N_DEV):
            tot = tot + buf[k]
        o_ref[...] = tot

    return pl.pallas_call(
        body, name="all_reduce_small", in_specs=[VMEM_SPEC], out_specs=VMEM_SPEC,
        out_shape=_sds((r, cdim), F32),
        scratch_shapes=[pltpu.VMEM((N_DEV, r, cdim), F32), pltpu.SemaphoreType.DMA((N_DEV - 1,)),
                        pltpu.SemaphoreType.DMA((N_DEV - 1,))],
    )(v)


def _row_tile(rows, cols, budget_elems=1 << 18, mult=8):
    if rows % mult:
        return rows
    t = max(mult, (budget_elems // cols) // mult * mult)
    while rows % t:
        t -= mult
    return t


def _adamw_update(w, g, m, v):
    nm = ADAM_B1 * m + (1.0 - ADAM_B1) * g
    nv = ADAM_B2 * v + (1.0 - ADAM_B2) * (g * g)
    m_hat = nm / (1.0 - ADAM_B1 ** ADAM_STEP)
    v_hat = nv / (1.0 - ADAM_B2 ** ADAM_STEP)
    return -ADAM_LR * (m_hat / (jnp.sqrt(v_hat) + ADAM_EPS) + ADAM_WD * w), nm, nv


def _adamw(name, w, g, m, v):
    R, C = w.shape
    tr = _row_tile(R, C, 1 << 19)

    def body(w_ref, g_ref, m_ref, v_ref, d_ref, nm_ref, nv_ref):
        d_ref[...], nm_ref[...], nv_ref[...] = _adamw_update(w_ref[...], g_ref[...], m_ref[...], v_ref[...])

    spec = pl.BlockSpec((tr, C), lambda i: (i, 0))
    return pl.pallas_call(body, name=name, grid=(R // tr,), in_specs=[spec] * 4, out_specs=[spec] * 3,
                          out_shape=[_sds((R, C), F32)] * 3, compiler_params=_params(1))(w, g, m, v)


def _sum_share(name, own, by_chip, chip, others, core):
    k, half, C = by_chip.shape
    tr = _row_tile(half, C, PAIR_TILE_ELEMS // 2, mult=16)
    nt = half // tr

    def body(chip_ref, oth_ref, c_ref, own_ref, a_ref, b_ref, cc_ref, g_out, mine, slot, send, recv):
        p = pl.program_id(1)
        _sibling_barrier((pl.program_id(0) == 0) & (p == 0))
        b = pl.program_id(0) % 2
        x, y, c = _place()
        cp = pltpu.make_async_remote_copy(src_ref=mine.at[b], dst_ref=slot.at[b], send_sem=send.at[b],
                                          recv_sem=recv.at[b], device_id=(x, y, 1 - c), device_id_type=MESH)

        @pl.when(p == 0)
        def _():
            tot = ((own_ref[...] + a_ref[...].astype(F32)) + b_ref[...].astype(F32)) + cc_ref[...].astype(F32)
            mine[b] = tot
            cp.start()
            g_out[...] = tot

        @pl.when(p == 1)
        def _():
            cp.wait_recv()
            g_out[...] = slot[b]
            cp.wait_send()

    def piece(j):
        return pl.BlockSpec((tr, C), lambda i, p, chip, oth, c: (oth[j] * nt + i, 0))

    grid_spec = pltpu.PrefetchScalarGridSpec(
        num_scalar_prefetch=3, grid=(nt, 2),
        in_specs=[pl.BlockSpec((tr, C), lambda i, p, chip, oth, c: (chip[0] * nt + i, 0)),
                  piece(0), piece(1), piece(2)],
        out_specs=pl.BlockSpec((tr, C), lambda i, p, chip, oth, c: (
            jnp.where(p == 0, c[0], 1 - c[0]) * nt + i, 0)),
        scratch_shapes=[pltpu.VMEM((2, tr, C), F32), pltpu.VMEM((2, tr, C), F32), pltpu.SemaphoreType.DMA((2,)),
                        pltpu.SemaphoreType.DMA((2,))])
    by2 = by_chip.reshape(k * half, C)
    return pl.pallas_call(body, name=name, grid_spec=grid_spec, out_shape=_sds((2 * half, C), F32),
                          compiler_params=_params(2, SUM_SHARE_ID))(chip, others, core, own, by2, by2, by2)


BIG = ("w_in", "w_proj_a", "w_proj_b", "w_out", "w_up", "w_down")
COL_SHARDED = ("w_in", "w_proj_a", "w_up")
SMALL = ("norm1_g", "q_norm_g", "k_norm_g", "ret_gn_g", "ret_gn_b", "norm2_g")
ALL_W = ("norm1_g", "w_in", "q_norm_g", "k_norm_g", "ret_gn_g", "ret_gn_b", "w_proj_a", "w_proj_b", "w_out",
         "norm2_g", "w_up", "w_down")
LANES = 128


def _to_full(name, gathered):
    k, r, c = gathered.shape
    if name in COL_SHARDED:
        return gathered.transpose(1, 0, 2).reshape(r, k * c)
    return gathered.reshape(k * r, c)


def _to_shard_major(name, full):
    if name in COL_SHARDED:
        r, c4 = full.shape
        return full.reshape(r, N_CHIPS, c4 // N_CHIPS).transpose(1, 0, 2)
    r4, c = full.shape
    return full.reshape(N_CHIPS, r4 // N_CHIPS, c)


def kernel(x, norm1_g, w_in, q_norm_g, k_norm_g, ret_gn_g, ret_gn_b, w_proj_a, w_proj_b, w_out, norm2_g, w_up, w_down, loss_target, m_norm1_g, m_w_in, m_q_norm_g, m_k_norm_g, m_ret_gn_g, m_ret_gn_b, m_w_proj_a, m_w_proj_b, m_w_out, m_norm2_g, m_w_up, m_w_down, v_norm1_g, v_w_in, v_q_norm_g, v_k_norm_g, v_ret_gn_g, v_ret_gn_b, v_w_proj_a, v_w_proj_b, v_w_out, v_norm2_g, v_w_up, v_w_down):
    weights = dict(norm1_g=norm1_g, w_in=w_in, q_norm_g=q_norm_g, k_norm_g=k_norm_g, ret_gn_g=ret_gn_g,
                   ret_gn_b=ret_gn_b, w_proj_a=w_proj_a, w_proj_b=w_proj_b, w_out=w_out, norm2_g=norm2_g,
                   w_up=w_up, w_down=w_down)
    moments_m = dict(norm1_g=m_norm1_g, w_in=m_w_in, q_norm_g=m_q_norm_g, k_norm_g=m_k_norm_g, ret_gn_g=m_ret_gn_g,
                     ret_gn_b=m_ret_gn_b, w_proj_a=m_w_proj_a, w_proj_b=m_w_proj_b, w_out=m_w_out,
                     norm2_g=m_norm2_g, w_up=m_w_up, w_down=m_w_down)
    moments_v = dict(norm1_g=v_norm1_g, w_in=v_w_in, q_norm_g=v_q_norm_g, k_norm_g=v_k_norm_g, ret_gn_g=v_ret_gn_g,
                     ret_gn_b=v_ret_gn_b, w_proj_a=v_w_proj_a, w_proj_b=v_w_proj_b, w_out=v_w_out,
                     norm2_g=v_norm2_g, w_up=v_w_up, w_down=v_w_down)

    mx, my = lax.axis_index("x"), lax.axis_index("y")
    core = lax.axis_index("c").astype(jnp.int32).reshape(1)
    chip = (2 * mx + my).astype(jnp.int32).reshape(1)
    others = jnp.stack([2 * (1 - mx) + my, 2 * mx + 1 - my, 2 * (1 - mx) + 1 - my]).astype(jnp.int32)
    shards = {n: weights[n][0].astype(CDT) for n in BIG}
    def start_gather(name, names):
        return _exchange_start(name, "gather", [shards[n] for n in names],
                               [((N_CHIPS,) + shards[n].shape, CDT) for n in names])

    w_in_shape = [((N_CHIPS,) + shards["w_in"].shape, CDT)]
    n_send, n_recv, n_srcs, n_lands, n_token = _exchange_start(
        "gather_w_in_near_start", "gather", [shards["w_in"]], w_in_shape, which=(0, 1))
    late = [n for n in BIG if n != "w_in"]
    flight = {}

    def near_w_in(after):
        srcs, lands = _exchange_wait("gather_w_in_near_wait", "gather", n_send, n_recv, n_srcs, n_lands, after,
                                     which=(0, 1))
        d_send, d_recv, d_srcs, d_lands, _ = _exchange_start(
            "gather_w_in_diag_start", "gather", srcs, w_in_shape, which=(2,), lands=lands)
        flight["late"] = start_gather("gather_late_start", late)
        w_near = _pair_fill("pair_fill_w_in_near", d_lands[0], d_srcs[0], core, others[:2], chip)
        flight["diag"] = (d_send, d_recv, d_srcs, [w_near])
        return w_near, flight["late"][-1]

    def far_w_in(after):
        d_send, d_recv, d_srcs, d_lands = flight["diag"]
        srcs, lands = _exchange_wait("gather_w_in_diag_wait", "gather", d_send, d_recv, d_srcs, d_lands, after,
                                     which=(2,))
        return _pair_fill("pair_fill_w_in_diag", lands[0], srcs[0], core, others[2:], chip, write_mine=False), None

    def late_weights(after):
        l_send, l_recv, l_srcs, l_lands, _ = flight["late"]
        srcs, lands = _exchange_wait("gather_late_wait", "gather", l_send, l_recv, l_srcs, l_lands, after)
        out = {}
        for n, mine, land in zip(late, srcs, lands):
            out[n] = _to_full(n, _pair_fill("pair_fill_%s" % n, land, mine, core, others, chip))
        return out

    pending = []

    def on_grads(group):
        names = list(group)
        red = [_pair_reduce("pair_reduce_%s" % n, g if g.ndim == 3 else _to_shard_major(n, g), core)
               for n, g in group.items()]
        wires = [wire for _, wire in red]
        send, recv, srcs, lands, token = _exchange_start(
            "scatter_start_%s" % names[0], "scatter", wires, [(wire.shape, wire.dtype) for wire in wires])
        pending.append((names, [own for own, _ in red], send, recv, srcs, lands))
        return token

    small = {n: weights[n].reshape(1, -1) for n in SMALL}

    loss, grad_x, small_g = _local_step(x[0], loss_target[0], n_srcs[0], chip, others, near_w_in, far_w_in, small,
                                        late_weights, on_grads, deps0=[n_token])

    out_g, out_d, out_m, out_v = {}, {}, {}, {}
    for names, owns, send, recv, srcs, lands in pending:
        _, got = _exchange_wait("scatter_wait_%s" % names[0], "scatter", send, recv, srcs, lands, grad_x)
        for n, own, by_chip in zip(names, owns, got):
            shape = weights[n].shape
            g2 = _sum_share("sum_share_%s" % n, own, by_chip, chip, others, core)
            d, nm, nv = _adamw("adamw_%s" % n, weights[n][0], g2, moments_m[n][0], moments_v[n][0])
            out_g[n], out_d[n], out_m[n], out_v[n] = (t.reshape(shape) for t in (g2, d, nm, nv))

    packed = jnp.concatenate([small_g[n].reshape(1, -1) for n in SMALL], axis=1).reshape(-1, LANES)
    loss_tile = jnp.zeros((8, LANES), F32).at[0, 0].set(loss)
    red = _all_reduce_small(jnp.concatenate([packed, loss_tile], axis=0))
    loss = red[packed.shape[0], 0]
    red = red[:packed.shape[0]].reshape(1, -1)
    off = 0
    for n in SMALL:
        shape = weights[n].shape
        row = (1, weights[n].size)
        g2 = red[:, off:off + row[1]]
        off += row[1]
        d, nm, nv = _adamw("adamw_%s" % n, weights[n].reshape(row), g2, moments_m[n].reshape(row),
                           moments_v[n].reshape(row))
        out_g[n], out_d[n], out_m[n], out_v[n] = (t.reshape(shape) for t in (g2, d, nm, nv))

    return (loss, grad_x[None], *[out_g[n] for n in ALL_W], *[out_d[n] for n in ALL_W],
            *[out_m[n] for n in ALL_W], *[out_v[n] for n in ALL_W])
```

```python
import functools

import jax
import jax.numpy as jnp
from jax import lax
from jax.experimental import pallas as pl
from jax.experimental.pallas import tpu as pltpu

CDT = jnp.bfloat16
F32 = jnp.float32
EPS = 1e-6

ATT_GROUPS = ((128, 1), (512, 4), (2048, 16))
ATT_HPG = 4
ATT_HEADS = 12
HD = 128
BLK = 128
ATT_W = ATT_HEADS * HD
GW = ATT_HPG * HD
RET_HEADS = 4

ADAM_LR = 0.001
ADAM_B1 = 0.9
ADAM_B2 = 0.999
ADAM_EPS = 1e-08
ADAM_WD = 0.01
ADAM_STEP = 10

VMEM_LIMIT_BYTES = 56 * 1024 * 1024
MESH = pl.DeviceIdType.MESH
HBM_SPEC = pl.BlockSpec(memory_space=pltpu.HBM)
VMEM_SPEC = pl.BlockSpec(memory_space=pltpu.VMEM)


def _params(n_axes, collective_id=None):
    return pltpu.CompilerParams(dimension_semantics=("arbitrary",) * n_axes,
                                vmem_limit_bytes=VMEM_LIMIT_BYTES, collective_id=collective_id)


PAIR_FILL_ID, PAIR_REDUCE_ID, SUM_SHARE_ID = 1, 2, 3


def _sibling_barrier(first_step):
    @pl.when(first_step)
    def _():
        sem = pltpu.get_barrier_semaphore()
        x, y, c = lax.axis_index("x"), lax.axis_index("y"), lax.axis_index("c")
        pl.semaphore_signal(sem, inc=1, device_id=(x, y, 1 - c), device_id_type=pl.DeviceIdType.MESH)
        pl.semaphore_wait(sem, 1)


def _dot_nn(a, b):
    return jnp.dot(a, b, preferred_element_type=F32)


def _dot_nt(a, b):
    return lax.dot_general(a, b, (((1,), (1,)), ((), ())), preferred_element_type=F32)


def _dot_tn(a, b):
    return lax.dot_general(a, b, (((0,), (0,)), ((), ())), preferred_element_type=F32)


def _sigmoid(v):
    return 1.0 / (1.0 + jnp.exp(-v))


def _matmul(name, mode, a, b, *, tm, tn, tk, extras=(), outs, epilogue, deps=(), b_spec=None, n_cols=None,
            prefetch=(), alias_dep_to_out=None, j_outer=False):
    deps = [d for d in deps if d is not None]
    if mode == "tn":
        K, M = a.shape
    else:
        M, K = a.shape
    if b_spec is None:
        (N, K2) = b.shape if mode == "nt" else b.shape[::-1]
        assert K == K2, (name, a.shape, b.shape)
        if mode == "nt":
            b_spec = pl.BlockSpec((tn, tk), lambda i, j, k, *p: (j, k))
        else:
            b_spec = pl.BlockSpec((tk, tn), lambda i, j, k, *p: (k, j))
    else:
        N = n_cols
    assert M % tm == 0 and N % tn == 0 and K % tk == 0, (name, a.shape, b.shape)
    ni, nj, nk = M // tm, N // tn, K // tk
    if mode == "tn":
        a_spec = pl.BlockSpec((tk, tm), lambda i, j, k, *p: (k, i))
    else:
        a_spec = pl.BlockSpec((tm, tk), lambda i, j, k, *p: (i, k))
    dot = {"nn": _dot_nn, "nt": _dot_nt, "tn": _dot_tn}[mode]
    n_ex, n_out, n_dep, n_pre = len(extras), len(outs), len(deps), len(prefetch)
    grid = (ni, nj, nk)
    if j_outer:
        grid = (nj, ni, nk)

        def swapped(spec):
            return pl.BlockSpec(spec.block_shape, lambda j, i, k, *p: spec.index_map(i, j, k, *p))

        a_spec, b_spec = swapped(a_spec), swapped(b_spec)
        extras = [(e, swapped(s)) for e, s in extras]
        outs = [(o, swapped(s)) for o, s in outs]

    def body(*refs):
        refs = refs[n_pre:]
        a_ref, b_ref = refs[0], refs[1]
        ex = refs[2:2 + n_ex]
        out = refs[2 + n_ex + n_dep:2 + n_ex + n_dep + n_out]
        acc = refs[-1] if nk > 1 else None
        i = pl.program_id(1 if j_outer else 0)
        k = pl.program_id(2)
        if nk == 1:
            epilogue(dot(a_ref[...].astype(CDT), b_ref[...].astype(CDT)), ex, out, i)
            return

        @pl.when(k == 0)
        def _():
            acc[...] = jnp.zeros_like(acc)

        acc[...] += dot(a_ref[...].astype(CDT), b_ref[...].astype(CDT))

        @pl.when(k == nk - 1)
        def _():
            epilogue(acc[...], ex, out, i)

    grid_spec = pltpu.PrefetchScalarGridSpec(
        num_scalar_prefetch=n_pre, grid=grid,
        in_specs=[a_spec, b_spec] + [s for _, s in extras] + [pl.BlockSpec(memory_space=pl.ANY)] * n_dep,
        out_specs=[s for _, s in outs],
        scratch_shapes=[pltpu.VMEM((tm, tn), F32)] if nk > 1 else [])
    aliases = {}
    if alias_dep_to_out is not None:
        aliases = {n_pre + 2 + n_ex + alias_dep_to_out[0]: alias_dep_to_out[1]}
    res = pl.pallas_call(
        body, name=name, grid_spec=grid_spec, out_shape=[o for o, _ in outs], input_output_aliases=aliases,
        compiler_params=_params(3),
    )(*prefetch, a, b, *[e for e, _ in extras], *deps)
    return res


def _mn(tm, tn, col_off=0):
    return pl.BlockSpec((tm, tn), lambda i, j, k, *p: (i, j + col_off))


def _row(tn):
    return pl.BlockSpec((1, tn), lambda i, j, k, *p: (0, j))


def _ep_store(acc, ex, out, i):
    out[0][...] = acc.astype(out[0].dtype)


def _ep_resid_norm(acc, ex, out, i):
    x1 = ex[0][...] + acc
    out[0][...] = x1
    rstd = lax.rsqrt(jnp.mean(x1 * x1, axis=-1, keepdims=True) + EPS)
    out[1][...] = (x1 * rstd * ex[1][...]).astype(out[1].dtype)


def _ep_up(acc, ex, out, i):
    out[0][...] = acc.astype(out[0].dtype)
    r = jnp.maximum(acc, 0.0)
    out[1][...] = (r * r).astype(out[1].dtype)


def _ep_down_loss(acc, ex, out, i, inv_d):
    diff = (ex[0][...] + acc) - ex[1][...]
    dx2 = diff * inv_d
    out[0][...] = dx2
    out[1][...] = dx2.astype(out[1].dtype)

    @pl.when(i == 0)
    def _():
        out[2][...] = jnp.zeros_like(out[2])

    out[2][...] += jnp.sum(diff * diff, axis=0, keepdims=True)


def _ep_dh(acc, ex, out, i):
    h = ex[0][...].astype(F32)
    out[0][...] = (acc * (2.0 * jnp.maximum(h, 0.0))).astype(out[0].dtype)


def _ep_rms_bwd(acc, ex, out, i):
    x = ex[0][...]
    g = ex[1][...]
    rstd = lax.rsqrt(jnp.mean(x * x, axis=-1, keepdims=True) + EPS)
    xh = x * rstd
    dxh = acc * g
    dx = ex[2][...] + rstd * (dxh - xh * jnp.mean(dxh * xh, axis=-1, keepdims=True))
    out[0][...] = dx
    for copy in out[1:-1]:
        copy[...] = dx.astype(copy.dtype)
    dg = out[-1]

    @pl.when(i == 0)
    def _():
        dg[...] = jnp.zeros_like(dg)

    dg[...] += jnp.sum(acc * xh, axis=0, keepdims=True)


def _ep_gates(acc, ex, out, i):
    sa = _sigmoid(ex[0][...].astype(F32))
    sb = _sigmoid(ex[1][...].astype(F32))
    dpa = acc * sa
    dpb = acc * sb
    out[0][...] = dpa.astype(out[0].dtype)
    out[1][...] = dpb.astype(out[1].dtype)
    out[2][...] = (dpa * ex[2][...].astype(F32) * (1.0 - sa)).astype(out[2].dtype)
    out[3][...] = (dpb * ex[3][...].astype(F32) * (1.0 - sb)).astype(out[3].dtype)


def _sds(shape, dtype):
    return jax.ShapeDtypeStruct(shape, dtype)


def _in_proj_mine(x, g, w_mine, chip, proj_sds, deps, tm=512):
    S, D = x.shape
    ns = w_mine.shape[1]
    deps = [d for d in deps if d is not None]

    def body(c_ref, x_ref, g_ref, w_ref, *rest):
        xn_ref, proj_ref = rest[len(deps)], rest[len(deps) + 1]
        xv = x_ref[...]
        rstd = lax.rsqrt(jnp.mean(xv * xv, axis=-1, keepdims=True) + EPS)
        xn = (xv * rstd * g_ref[...]).astype(xn_ref.dtype)
        xn_ref[...] = xn
        proj_ref[...] = _dot_nn(xn, w_ref[...]).astype(proj_ref.dtype)

    grid_spec = pltpu.PrefetchScalarGridSpec(
        num_scalar_prefetch=1, grid=(S // tm,),
        in_specs=[pl.BlockSpec((tm, D), lambda i, c: (i, 0)), pl.BlockSpec((1, D), lambda i, c: (0, 0)),
                  pl.BlockSpec((D, ns), lambda i, c: (0, 0))] + [pl.BlockSpec(memory_space=pl.ANY)] * len(deps),
        out_specs=[pl.BlockSpec((tm, D), lambda i, c: (i, 0)), pl.BlockSpec((tm, ns), lambda i, c: (i, c[0]))])
    return pl.pallas_call(body, name="in_proj_mine", grid_spec=grid_spec, out_shape=[_sds((S, D), CDT), proj_sds],
                          compiler_params=_params(1))(chip, x, g, w_mine, *deps)


def _rm_shape(S, d, width):
    return (S, width) if d == 1 else (d, S // d, width)


def _rm_spec(tm, d, width):
    if d == 1:
        return pl.BlockSpec((tm, width), lambda i: (i, 0))
    return pl.BlockSpec((d, tm // d, width), lambda i: (0, i, 0))


def _rm_put(dst_ref, cols, buf_ref, d):
    if d == 1:
        dst_ref[:, cols] = buf_ref[...].astype(dst_ref.dtype)
        return
    m = buf_ref.shape[0] // d
    for r in range(d):
        dst_ref[r, :, cols] = buf_ref[pl.ds(r, m, stride=d), :].astype(dst_ref.dtype)


def _rm_reader(buf_ref, src_ref, d):
    if d == 1:
        return lambda s, rows: src_ref[rows, s * HD:(s + 1) * HD].astype(F32)
    m = buf_ref.shape[1] // d
    for s in range(buf_ref.shape[0]):
        for r in range(d):
            buf_ref.at[s][pl.ds(r, m, stride=d), :] = src_ref[r, :, s * HD:(s + 1) * HD].astype(F32)
    return lambda s, rows: buf_ref.at[s][rows, :]


def _qknorm_fwd(proj, gqk, tm=1024):
    S = proj.shape[0]
    W = 2 * ATT_W
    dil = [d for _, d in ATT_GROUPS]

    def body(p_ref, g_ref, o0, o1, o2, buf):
        outs = (o0, o1, o2)
        for hd in range(3 * ATT_HEADS):
            which, head = hd // ATT_HEADS, hd % ATT_HEADS
            grp, slot = head // ATT_HPG, head % ATT_HPG
            cols = slice(hd * HD, (hd + 1) * HD)

            def chunk(rows, which=which, cols=cols):
                v = p_ref[rows, cols].astype(F32)
                if which < 2:
                    rstd = lax.rsqrt(jnp.mean(v * v, axis=-1, keepdims=True) + EPS)
                    v = v * rstd * g_ref[:, cols]
                buf[rows, :] = v

            chunk(slice(None))
            _rm_put(outs[grp], slice(which * GW + slot * HD, which * GW + (slot + 1) * HD), buf, dil[grp])

    return pl.pallas_call(
        body, name="qknorm_fwd", grid=(S // tm,),
        in_specs=[pl.BlockSpec((tm, 3 * ATT_W), lambda i: (i, 0)), pl.BlockSpec((1, W), lambda i: (0, 0))],
        out_specs=[_rm_spec(tm, d, 3 * GW) for d in dil],
        out_shape=[_sds(_rm_shape(S, d, 3 * GW), CDT) for d in dil],
        scratch_shapes=[pltpu.VMEM((tm, HD), F32)],
        compiler_params=_params(1))(proj, gqk)


def _qknorm_bwd(proj, gqk, dqs, dks, dvs, dproj, tm=512):
    S = proj.shape[0]
    W = 2 * ATT_W
    dil = [d for _, d in ATT_GROUPS]

    def body(p_ref, g_ref, *refs):
        ins = refs[0:9]
        o_ref, dg_ref = refs[10], refs[11]
        bufs = refs[12:21]
        i = pl.program_id(0)

        @pl.when(i == 0)
        def _():
            dg_ref[...] = jnp.zeros_like(dg_ref)

        nat = [_rm_reader(bufs[j], ins[j], dil[j % 3]) for j in range(9)]
        dq_get, dk_get, dv_get = nat[0:3], nat[3:6], nat[6:9]
        for hd in range(2 * ATT_HEADS):
            sl = slice(hd * HD, (hd + 1) * HD)
            head = hd % ATT_HEADS
            grp, slot = head // ATT_HPG, head % ATT_HPG
            get = (dq_get if hd < ATT_HEADS else dk_get)[grp]

            def chunk(rows, sl=sl, slot=slot, get=get):
                dn = get(slot, rows)
                v = p_ref[rows, sl].astype(F32)
                rstd = lax.rsqrt(jnp.mean(v * v, axis=-1, keepdims=True) + EPS)
                vh = v * rstd
                dg_ref[:, sl] += jnp.sum(dn * vh, axis=0, keepdims=True)
                dvh = dn * g_ref[:, sl]
                o_ref[rows, sl] = (rstd * (dvh - vh * jnp.mean(dvh * vh, axis=-1, keepdims=True))).astype(o_ref.dtype)

            chunk(slice(None))
        for head in range(ATT_HEADS):
            grp, slot = head // ATT_HPG, head % ATT_HPG
            o_ref[:, W + head * HD:W + (head + 1) * HD] = dv_get[grp](slot, slice(None)).astype(o_ref.dtype)

    return pl.pallas_call(
        body, name="qknorm_bwd", grid=(S // tm,),
        in_specs=[pl.BlockSpec((tm, W), lambda i: (i, 0)), pl.BlockSpec((1, W), lambda i: (0, 0))]
        + [_rm_spec(tm, d, GW) for d in dil] * 3 + [pl.BlockSpec(memory_space=pl.ANY)],
        out_specs=[pl.BlockSpec((tm, 3 * ATT_W), lambda i: (i, 0)), pl.BlockSpec((1, W), lambda i: (0, 0))],
        out_shape=[_sds(dproj.shape, dproj.dtype), _sds((1, W), F32)],
        scratch_shapes=[pltpu.VMEM((ATT_HPG, tm, HD), F32)] * 9,
        input_output_aliases={11: 0},
        compiler_params=_params(1))(proj, gqk, *dqs, *dks, *dvs, dproj)


def _att_mask(n):
    qi = lax.broadcasted_iota(jnp.int32, (BLK, 2 * BLK), 0)
    kj = lax.broadcasted_iota(jnp.int32, (BLK, 2 * BLK), 1)
    dist = BLK + qi - kj
    valid_all = (dist >= 0) & (dist <= BLK)
    return valid_all & ((kj >= BLK) | (n > 0)), valid_all, dist.astype(F32)


def _att_slopes(grp):
    return [2.0 ** (-8.0 * (grp * ATT_HPG + hh + 1) / ATT_HEADS) for hh in range(ATT_HPG)]


ATT_WIDTH = 8
ATT_BWD_SEGMENTS = 4


def _att_planes(d, width):
    return d if d > 1 else width


def _att_step_planes(d, width):
    return min(_att_planes(d, width), width)


def _att_3d(a, d, width):
    return a.reshape(width, a.shape[0] // width, a.shape[1]) if d == 1 else a


def _att_spec(R, row_fn, col=0):
    return pl.BlockSpec((R, BLK, GW), lambda r, n: (r, row_fn(n), col))


def _att_chains(R):
    return [(rr * ATT_HPG + hh, rr, slice(hh * HD, (hh + 1) * HD), hh) for rr in range(R) for hh in range(ATT_HPG)]


def _att_qkv_specs(R, nb):
    last = nb - 1

    def cur(n):
        return jnp.minimum(n, last)

    def prev(n):
        return jnp.maximum(jnp.minimum(n, last) - 1, 0)

    return [_att_spec(R, cur, 0), _att_spec(R, prev, 1), _att_spec(R, cur, 1), _att_spec(R, prev, 2),
            _att_spec(R, cur, 2), _att_spec(R, lambda n: last, 1), _att_spec(R, lambda n: last, 2)]


def _att_prev(seg, n, prev_ref, last_ref, rr, sl):
    t = prev_ref[rr, :, sl]
    if seg and rr > 0:
        t = jnp.where(n == 0, last_ref[rr - 1, :, sl], t)
    return t


def _att_fwd(grp, S, qkv):
    _, d = ATT_GROUPS[grp]
    seg = d == 1
    width = ATT_WIDTH
    P = _att_planes(d, width)
    L = S // P
    nb = L // BLK
    R = _att_step_planes(d, width)
    assert P % R == 0 and (not seg or P == R)
    slopes = _att_slopes(grp)
    scale = HD ** -0.5
    chains = _att_chains(R)

    def body(q_ref, kp_ref, kc_ref, vp_ref, vc_ref, kl_ref, vl_ref, o_ref, l_ref, s_buf, p_buf, den_buf):
        n = pl.program_id(1)
        valid, valid_all, distf = _att_mask(n)
        for c, rr, sl, hh in chains:
            k = jnp.concatenate([_att_prev(seg, n, kp_ref, kl_ref, rr, sl), kc_ref[rr, :, sl]], axis=0)
            s_buf[c] = _dot_nt(q_ref[rr, :, sl], k)
        for c, rr, sl, hh in chains:
            s = s_buf[c] * scale + (-slopes[hh] * d) * distf
            s = jnp.where(valid_all if seg and rr > 0 else valid, s, -1e30)
            m = jnp.max(s, axis=-1, keepdims=True)
            p = jnp.exp(s - m)
            den = jnp.sum(p, axis=-1, keepdims=True)
            p_buf[c] = p.astype(CDT)
            den_buf[c] = jnp.broadcast_to(den, (BLK, HD))
            l_ref[rr, :, sl] = jnp.broadcast_to(m + jnp.log(den), (BLK, HD))
        for c, rr, sl, hh in chains:
            v = jnp.concatenate([_att_prev(seg, n, vp_ref, vl_ref, rr, sl), vc_ref[rr, :, sl]], axis=0)
            o_ref[rr, :, sl] = _dot_nn(p_buf[c], v) / den_buf[c]

    out_spec = _att_spec(R, lambda n: n)
    n_ch = len(chains)
    q3 = _att_3d(qkv, d, width)
    o, l = pl.pallas_call(
        body, name="att_fwd_g%d" % grp, grid=(P // R, nb),
        in_specs=_att_qkv_specs(R, nb),
        out_specs=[out_spec, out_spec],
        out_shape=[_sds((P, L, GW), F32)] * 2,
        scratch_shapes=[pltpu.VMEM((n_ch, BLK, 2 * BLK), F32), pltpu.VMEM((n_ch, BLK, 2 * BLK), CDT),
                        pltpu.VMEM((n_ch, BLK, HD), F32)],
        compiler_params=_params(2),
    )(*[q3] * 7)
    return o.reshape(_rm_shape(S, d, GW)), l.reshape(_rm_shape(S, d, GW))


def _att_bwd(grp, S, qkv, lse, do_g, c_g):
    _, d = ATT_GROUPS[grp]
    seg = d == 1
    width = ATT_BWD_SEGMENTS if seg else ATT_WIDTH
    P = _att_planes(d, width)
    L = S // P
    nb = L // BLK
    R = _att_step_planes(d, width)
    assert P % R == 0 and (not seg or P == R)
    slopes = _att_slopes(grp)
    scale = HD ** -0.5
    last = nb - 1
    chains = _att_chains(R)

    def body(q_ref, kp_ref, kc_ref, vp_ref, vc_ref, kl_ref, vl_ref, l_ref, do_ref, c_ref, dq_ref, dk_ref, dv_ref,
             ck, cv, fk, fv, s_buf, dp_buf, p_buf, ds_buf):
        n = pl.program_id(1)

        @pl.when(n == 0)
        def _():
            for buf in (ck, cv, fk, fv):
                buf[...] = jnp.zeros_like(buf)

        @pl.when(n < nb)
        def _():
            valid, valid_all, distf = _att_mask(n)
            for c, rr, sl, hh in chains:
                k = jnp.concatenate([_att_prev(seg, n, kp_ref, kl_ref, rr, sl), kc_ref[rr, :, sl]], axis=0)
                v = jnp.concatenate([_att_prev(seg, n, vp_ref, vl_ref, rr, sl), vc_ref[rr, :, sl]], axis=0)
                s_buf[c] = _dot_nt(q_ref[rr, :, sl], k)
                dp_buf[c] = _dot_nt(do_ref[rr, :, sl], v)
            for c, rr, sl, hh in chains:
                s = s_buf[c] * scale + (-slopes[hh] * d) * distf
                p = jnp.where(valid_all if seg and rr > 0 else valid, jnp.exp(s - l_ref[rr, :, sl][:, 0:1]), 0.0)
                p_buf[c] = p.astype(CDT)
                ds_buf[c] = (p * (dp_buf[c] + c_ref[rr, :, sl][:, 0:1]) * scale).astype(CDT)
            for c, rr, sl, hh in chains:
                k = jnp.concatenate([_att_prev(seg, n, kp_ref, kl_ref, rr, sl), kc_ref[rr, :, sl]], axis=0)
                ds = ds_buf[c]
                dq_ref[rr, :, sl] = _dot_nn(ds, k)
                dk = _dot_tn(ds, q_ref[rr, :, sl])
                dv = _dot_tn(p_buf[c], do_ref[rr, :, sl])
                dk_ref[rr, :, sl] = ck[rr, :, sl] + dk[0:BLK]
                dv_ref[rr, :, sl] = cv[rr, :, sl] + dv[0:BLK]
                ck[rr, :, sl] = dk[BLK:2 * BLK]
                cv[rr, :, sl] = dv[BLK:2 * BLK]
                if seg and rr > 0:
                    @pl.when(n == 0)
                    def _(rr=rr, sl=sl, dk=dk, dv=dv):
                        fk[rr - 1, :, sl] = dk[0:BLK]
                        fv[rr - 1, :, sl] = dv[0:BLK]

        @pl.when(n == nb)
        def _():
            dk_ref[...] = ck[...] + fk[...]
            dv_ref[...] = cv[...] + fv[...]

    blk = (R, BLK, GW)
    at_q = _att_spec(R, lambda n: jnp.minimum(n, last))
    behind = _att_spec(R, lambda n: jnp.maximum(n - 1, 0))
    n_ch = len(chains)
    q3 = _att_3d(qkv, d, width)
    res = pl.pallas_call(
        body, name="att_bwd_g%d" % grp, grid=(P // R, nb + 1),
        in_specs=_att_qkv_specs(R, nb) + [at_q, at_q, at_q],
        out_specs=[at_q, behind, behind],
        out_shape=[_sds((P, L, GW), F32)] * 3,
        scratch_shapes=[pltpu.VMEM(blk, F32)] * 4
        + [pltpu.VMEM((n_ch, BLK, 2 * BLK), F32), pltpu.VMEM((n_ch, BLK, 2 * BLK), F32),
           pltpu.VMEM((n_ch, BLK, 2 * BLK), CDT), pltpu.VMEM((n_ch, BLK, 2 * BLK), CDT)],
        compiler_params=_params(2),
    )(*[q3] * 7, _att_3d(lse, d, width), _att_3d(do_g, d, width), _att_3d(c_g, d, width))
    return [t.reshape(_rm_shape(S, d, GW)) for t in res]


def _mix_alpha(l0, l1, l2):
    mx = jnp.maximum(jnp.maximum(l0, l1), l2)
    e = [jnp.exp(l0 - mx), jnp.exp(l1 - mx), jnp.exp(l2 - mx)]
    tot = e[0] + e[1] + e[2]
    return [ei / tot for ei in e]


def _mix_fwd(S, os_, ls_, tm=1024):
    dil = [d for _, d in ATT_GROUPS]

    def body(*refs):
        out, bufs = refs[6], refs[7:13]
        get = [_rm_reader(bufs[j], refs[j], dil[j % 3]) for j in range(6)]
        rows = slice(None)
        for s in range(ATT_HPG):
            al = _mix_alpha(*[get[3 + g](s, rows) for g in range(3)])
            mixed = al[0] * get[0](s, rows) + al[1] * get[1](s, rows) + al[2] * get[2](s, rows)
            out[:, s * HD:(s + 1) * HD] = mixed.astype(out.dtype)

    specs = [_rm_spec(tm, d, GW) for d in dil]
    return pl.pallas_call(
        body, name="mix_fwd", grid=(S // tm,), in_specs=specs * 2, out_specs=pl.BlockSpec((tm, GW), lambda i: (i, 0)),
        out_shape=_sds((S, GW), CDT), scratch_shapes=[pltpu.VMEM((ATT_HPG, tm, HD), F32)] * 6,
        compiler_params=_params(1))(*os_, *ls_)


def _mix_bwd(S, os_, ls_, do_a, tm=512):
    dil = [d for _, d in ATT_GROUPS]

    def body(*refs):
        d_ref, outs, bufs, tmps = refs[6], refs[7:13], refs[13:19], refs[19:25]
        get = [_rm_reader(bufs[j], refs[j], dil[j % 3]) for j in range(6)]
        for s in range(ATT_HPG):
            cols = slice(s * HD, (s + 1) * HD)

            def chunk(rows, s=s, cols=cols):
                al = _mix_alpha(*[get[3 + g](s, rows) for g in range(3)])
                dv = d_ref[rows, cols]
                o_a = al[0] * get[0](s, rows) + al[1] * get[1](s, rows) + al[2] * get[2](s, rows)
                dsum = jnp.sum(dv * o_a, axis=-1, keepdims=True)
                for g in range(3):
                    tmps[g][rows, :] = al[g] * dv
                    tmps[3 + g][rows, :] = -(al[g] * dsum)

            chunk(slice(None))
            for j in range(6):
                _rm_put(outs[j], cols, tmps[j], dil[j % 3])

    specs = [_rm_spec(tm, d, GW) for d in dil]
    res = pl.pallas_call(
        body, name="mix_bwd", grid=(S // tm,), in_specs=specs * 2 + [pl.BlockSpec((tm, GW), lambda i: (i, 0))],
        out_specs=specs * 2,
        out_shape=[_sds(_rm_shape(S, d, GW), CDT) for d in dil] + [_sds(_rm_shape(S, d, GW), F32) for d in dil],
        scratch_shapes=[pltpu.VMEM((ATT_HPG, tm, HD), F32)] * 6 + [pltpu.VMEM((tm, HD), F32)] * 6,
        compiler_params=_params(1))(*os_, *ls_, do_a)
    return res[:3], res[3:]


def _ret_tables(dk):
    H, C = RET_HEADS, BLK
    log_g = jnp.log(1.0 - 2.0 ** (-5.0 - jnp.arange(H, dtype=F32)))
    idx = jnp.arange(C, dtype=F32)
    diff = idx[:, None] - idx[None, :]
    decay = jnp.where(diff >= 0, jnp.exp(log_g[:, None, None] * jnp.maximum(diff, 0.0)), 0.0)
    xi = jnp.exp(log_g[:, None] * (idx[None, :] + 1.0))
    zeta = jnp.exp(log_g[:, None] * (C - 1.0 - idx[None, :])) * (dk ** -0.5)
    g_chunk = jnp.exp(log_g * C)
    bc = lambda t: jnp.broadcast_to(t[:, :, None], (H, C, C))
    return decay, bc(xi), bc(zeta), jnp.broadcast_to(g_chunk[:, None, None], (H, 8, C))


def _gn_fwd(o, g, b):
    mu = jnp.mean(o, axis=-1, keepdims=True)
    xc = o - mu
    rstd = lax.rsqrt(jnp.mean(xc * xc, axis=-1, keepdims=True) + EPS)
    yh = xc * rstd
    return yh, rstd, yh * g + b


def _ret_specs(dk, dv, order):
    H = RET_HEADS
    qk_w, v_w = H * dk, H * dv
    off_q = 3 * ATT_W
    off_k, off_v, off_g = off_q + qk_w, off_q + 2 * qk_w, off_q + 2 * qk_w + v_w
    assert 2 * dk == dv and all(off % dv == 0 for off in (off_q, off_k, off_v, off_g))

    def col(off, j):
        return pl.BlockSpec((BLK, dv), lambda i: (order(i), off // dv + j))

    tab = pl.BlockSpec((H, BLK, BLK), lambda i: (0, 0, 0))
    return ([col(off_q, j) for j in range(H // 2)] + [col(off_k, j) for j in range(H // 2)]
            + [col(off_v, j) for j in range(H)] + [col(off_g, j) for j in range(H)]
            + [tab, tab, tab, pl.BlockSpec((H, 8, BLK), lambda i: (0, 0, 0))])


def _ret_heads(refs, dk):
    H = RET_HEADS
    q_refs, k_refs = refs[0:H // 2], refs[H // 2:H]
    v_refs, gr_refs = refs[H:2 * H], refs[2 * H:3 * H]

    def head(h):
        cols = slice((h % 2) * dk, (h % 2 + 1) * dk)
        return q_refs[h // 2][:, cols], k_refs[h // 2][:, cols], v_refs[h][...], gr_refs[h][...]

    return head, refs[3 * H:3 * H + 4]


def _ret_fwd(proj, gn_g, gn_b, dk, dv):
    S = proj.shape[0]
    N = S // BLK
    H = RET_HEADS
    kscale = dk ** -0.5
    n_in = 3 * H + 4

    def body(*refs):
        head, (dec_ref, xi_ref, zeta_ref, gc_ref) = _ret_heads(refs, dk)
        g_ref, b_ref, opre_ref, or_ref, st_ref, state, s_buf, cross_buf = refs[n_in:n_in + 8]
        n = pl.program_id(0)

        @pl.when(n == 0)
        def _():
            state[...] = jnp.zeros_like(state)

        for h in range(H):
            q, k, v, _ = head(h)
            s_buf[h] = _dot_nt(q, k)
            st = state[h]
            st_c = st.astype(CDT)
            st_ref[h] = st_c
            cross_buf[h] = _dot_nn(q, st_c)
            kz = (k.astype(F32) * zeta_ref[h][:, 0:1]).astype(CDT)
            state[h] = st * gc_ref[h][0:1, 0:1] + _dot_tn(kz, v)
        for h in range(H):
            vs = slice(h * dv, (h + 1) * dv)
            _, _, v, gr = head(h)
            s = s_buf[h] * kscale * dec_ref[h]
            o = _dot_nn(s.astype(CDT), v) + cross_buf[h] * xi_ref[h][:, 0:1]
            opre_ref[:, vs] = o
            _, _, y = _gn_fwd(o, g_ref[:, vs], b_ref[:, vs])
            gr = gr.astype(F32)
            or_ref[:, vs] = (y * (gr * _sigmoid(gr))).astype(or_ref.dtype)

    v_w = H * dv
    row = pl.BlockSpec((1, v_w), lambda i: (0, 0))
    tile = pl.BlockSpec((BLK, v_w), lambda i: (i, 0))
    return pl.pallas_call(
        body, name="ret_fwd", grid=(N,),
        in_specs=_ret_specs(dk, dv, lambda i: i) + [row, row],
        out_specs=[tile, tile, pl.BlockSpec((None, H, dk, dv), lambda i: (i, 0, 0, 0))],
        out_shape=[_sds((S, v_w), F32), _sds((S, v_w), CDT), _sds((N, H, dk, dv), CDT)],
        scratch_shapes=[pltpu.VMEM((H, dk, dv), F32), pltpu.VMEM((H, BLK, BLK), F32), pltpu.VMEM((H, BLK, dv), F32)],
        compiler_params=_params(1),
    )(*[proj] * (3 * H), *_ret_tables(dk), gn_g, gn_b)


def _ret_bwd(proj, gn_g, gn_b, o_pre, states, d_or, dga, dgb, dk, dv):
    S, in_w = proj.shape
    N = S // BLK
    H = RET_HEADS
    qk_w, v_w = H * dk, H * dv
    kscale = dk ** -0.5
    n_in = 3 * H + 4
    out_w = 2 * qk_w + 2 * v_w
    gate_w = dga.shape[1]
    col0 = 3 * ATT_W
    assert col0 + out_w + 2 * gate_w == in_w
    rev = lambda i: N - 1 - i

    def body(*refs):
        head, (dec_ref, xi_ref, zeta_ref, gc_ref) = _ret_heads(refs, dk)
        (g_ref, b_ref, opre_ref, st_ref, dor_ref, dga_ref, dgb_ref, dproj_ref, dg_ref, db_ref, dstate, stage,
         sem, do_buf, dox_buf, a_buf, g_buf, dq_buf, dk_buf, dv_buf) = refs[n_in:n_in + 20]
        i = pl.program_id(0)
        slot = i % 2
        out_ref = stage.at[slot]

        def out_copy(s, step):
            rows = pl.ds(pl.multiple_of(rev(step) * BLK, BLK), BLK)
            return pltpu.make_async_copy(stage.at[s], dproj_ref.at[rows, pl.ds(col0, in_w - col0)], sem.at[s])

        @pl.when(i >= 2)
        def _():
            out_copy(slot, i - 2).wait()

        @pl.when(i == 0)
        def _():
            dstate[...] = jnp.zeros_like(dstate)
            dg_ref[...] = jnp.zeros_like(dg_ref)
            db_ref[...] = jnp.zeros_like(db_ref)

        out_ref[:, out_w:out_w + gate_w] = dga_ref[...]
        out_ref[:, out_w + gate_w:out_w + 2 * gate_w] = dgb_ref[...]
        for h in range(H):
            vs = slice(h * dv, (h + 1) * dv)
            _, _, _, gr = head(h)
            gr = gr.astype(F32)
            sg = _sigmoid(gr)
            gain = g_ref[:, vs]
            yh, rstd, y = _gn_fwd(opre_ref[:, vs], gain, b_ref[:, vs])
            d_or_v = dor_ref[:, vs]
            dy = d_or_v * (gr * sg)
            out_ref[:, 2 * qk_w + v_w + h * dv:2 * qk_w + v_w + (h + 1) * dv] = (
                d_or_v * y * (sg * (1.0 + gr * (1.0 - sg)))).astype(out_ref.dtype)
            dg_ref[:, vs] += jnp.sum(dy * yh, axis=0, keepdims=True)
            db_ref[:, vs] += jnp.sum(dy, axis=0, keepdims=True)
            dyh = dy * gain
            do = rstd * (dyh - jnp.mean(dyh, axis=-1, keepdims=True)
                         - yh * jnp.mean(dyh * yh, axis=-1, keepdims=True))
            do_buf[h] = do.astype(CDT)
            dox_buf[h] = (do * xi_ref[h][:, 0:1]).astype(CDT)
        for h in range(H):
            q, k, v, _ = head(h)
            dox = dox_buf[h]
            a_buf[h] = _dot_nt(q, k)
            g_buf[h] = _dot_nt(do_buf[h], v)
            dsn = dstate[h]
            dsn_c = dsn.astype(CDT)
            kz = (k.astype(F32) * zeta_ref[h][:, 0:1]).astype(CDT)
            dq_buf[h] = _dot_nt(dox, st_ref[h])
            dk_buf[h] = _dot_nt(v, dsn_c)
            dv_buf[h] = _dot_nn(kz, dsn_c)
            dstate[h] = dsn * gc_ref[h][0:1, 0:1] + _dot_tn(q, dox)
        for h in range(H):
            q, k, _, _ = head(h)
            decay = dec_ref[h]
            a_c = (a_buf[h] * kscale * decay).astype(CDT)
            g_c = (g_buf[h] * decay).astype(CDT)
            dq = _dot_nn(g_c, k) * kscale + dq_buf[h]
            dkk = _dot_tn(g_c, q) * kscale + dk_buf[h] * zeta_ref[h][:, 0:1]
            dvv = _dot_tn(a_c, do_buf[h]) + dv_buf[h]
            out_ref[:, h * dk:(h + 1) * dk] = dq.astype(out_ref.dtype)
            out_ref[:, qk_w + h * dk:qk_w + (h + 1) * dk] = dkk.astype(out_ref.dtype)
            out_ref[:, 2 * qk_w + h * dv:2 * qk_w + (h + 1) * dv] = dvv.astype(out_ref.dtype)

        cp = out_copy(slot, i)
        cp.start()

        @pl.when(i == N - 1)
        def _():
            cp.wait()
            if N >= 2:
                out_copy(1 - slot, i - 1).wait()

    row = pl.BlockSpec((1, v_w), lambda i: (0, 0))
    tile = pl.BlockSpec((BLK, v_w), lambda i: (rev(i), 0))
    gate = pl.BlockSpec((BLK, gate_w), lambda i: (rev(i), 0))
    return pl.pallas_call(
        body, name="ret_bwd", grid=(N,),
        in_specs=_ret_specs(dk, dv, rev) + [row, row, tile,
                 pl.BlockSpec((None, H, dk, dv), lambda i: (rev(i), 0, 0, 0)), tile, gate, gate],
        out_specs=[pl.BlockSpec(memory_space=pl.ANY), row, row],
        out_shape=[_sds((S, in_w), CDT), _sds((1, v_w), F32), _sds((1, v_w), F32)],
        scratch_shapes=[pltpu.VMEM((H, dk, dv), F32), pltpu.VMEM((2, BLK, in_w - col0), CDT),
                        pltpu.SemaphoreType.DMA((2,)),
                        pltpu.VMEM((H, BLK, dv), CDT), pltpu.VMEM((H, BLK, dv), CDT),
                        pltpu.VMEM((H, BLK, BLK), F32), pltpu.VMEM((H, BLK, BLK), F32),
                        pltpu.VMEM((H, BLK, dk), F32), pltpu.VMEM((H, BLK, dk), F32), pltpu.VMEM((H, BLK, dv), F32)],
        compiler_params=_params(1),
    )(*[proj] * (3 * H), *_ret_tables(dk), gn_g, gn_b, o_pre, states, d_or, dga, dgb)


def _merge_fwd(o_a, o_r, wa, wb, proj, d_model, tm=1024, tn=512):
    S, in_w = proj.shape
    off_a, off_b = in_w - 2 * d_model, in_w - d_model
    assert off_a % tn == 0 and off_b % tn == 0

    def body(oa_ref, or_ref, wa_ref, wb_ref, ga_ref, gb_ref, y_ref, pa_ref, pb_ref):
        pa = _dot_nn(oa_ref[...], wa_ref[...])
        pb = _dot_nn(or_ref[...], wb_ref[...])
        y = _sigmoid(ga_ref[...].astype(F32)) * pa + _sigmoid(gb_ref[...].astype(F32)) * pb
        y_ref[...] = y.astype(y_ref.dtype)
        pa_ref[...] = pa.astype(pa_ref.dtype)
        pb_ref[...] = pb.astype(pb_ref.dtype)

    ka, kb = o_a.shape[1], o_r.shape[1]
    out = pl.BlockSpec((tm, tn), lambda i, j: (i, j))
    return pl.pallas_call(
        body, name="merge_fwd", grid=(S // tm, d_model // tn),
        in_specs=[pl.BlockSpec((tm, ka), lambda i, j: (i, 0)), pl.BlockSpec((tm, kb), lambda i, j: (i, 0)),
                  pl.BlockSpec((ka, tn), lambda i, j: (0, j)), pl.BlockSpec((kb, tn), lambda i, j: (0, j)),
                  pl.BlockSpec((tm, tn), lambda i, j: (i, off_a // tn + j)),
                  pl.BlockSpec((tm, tn), lambda i, j: (i, off_b // tn + j))],
        out_specs=[out, out, out], out_shape=[_sds((S, d_model), CDT)] * 3,
        compiler_params=_params(2))(o_a, o_r, wa, wb, proj, proj)


def _d_x(dproj, w_in, x, g, dx1, dep, tm=512, row_groups=2):
    S, D = x.shape
    n_sh, _, ns = w_in.shape
    nt = S // tm // row_groups
    deps = [d for d in (dep,) if d is not None]

    def body(a_ref, b_ref, x_ref, g_ref, r_ref, *rest):
        dx_ref, dg_ref, acc = rest[len(deps):]
        h, k, i = pl.program_id(0), pl.program_id(1), pl.program_id(2)

        @pl.when(k == 0)
        def _():
            acc[i] = jnp.zeros((tm, D), F32)

        acc[i] += _dot_nt(a_ref[...], b_ref[...])

        @pl.when(k == n_sh - 1)
        def _():
            _ep_rms_bwd(acc[i], (x_ref, g_ref, r_ref), (dx_ref, dg_ref), h * nt + i)

    def last_only(h, k, i):
        return (h * nt + jnp.where(k == n_sh - 1, i, 0), 0)

    return pl.pallas_call(
        body, name="d_x", grid=(row_groups, n_sh, nt),
        in_specs=[pl.BlockSpec((tm, ns), lambda h, k, i: (h * nt + i, k)),
                  pl.BlockSpec((None, D, ns), lambda h, k, i: (k, 0, 0)),
                  pl.BlockSpec((tm, D), last_only), pl.BlockSpec((1, D), lambda h, k, i: (0, 0)),
                  pl.BlockSpec((tm, D), last_only)] + [pl.BlockSpec(memory_space=pl.ANY)] * len(deps),
        out_specs=[pl.BlockSpec((tm, D), last_only), pl.BlockSpec((1, D), lambda h, k, i: (0, 0))],
        out_shape=[_sds((S, D), F32), _sds((1, D), F32)],
        scratch_shapes=[pltpu.VMEM((nt, tm, D), F32)],
        compiler_params=_params(3))(dproj, w_in, x, g, dx1, *deps)


def _local_step(x, target, w_in_mine, chip, others, near_w_in, far_w_in, small, late_weights, on_grads, deps0=()):
    S, D = x.shape
    ns_in = w_in_mine.shape[1]
    in_w = N_CHIPS * ns_in
    d_ff = 4 * D
    ret_v_w = 2 * D
    dv = ret_v_w // RET_HEADS
    dk = (in_w - 3 * ATT_W - 2 * ret_v_w - 2 * D) // (2 * RET_HEADS)
    gqk = jnp.concatenate([small["q_norm_g"].reshape(1, ATT_W), small["k_norm_g"].reshape(1, ATT_W)], axis=1)
    g1, g2 = small["norm1_g"], small["norm2_g"]
    gn_g, gn_b = small["ret_gn_g"], small["ret_gn_b"]

    proj_sds = _sds((S, in_w), CDT)
    xn, proj = _in_proj_mine(x, g1, w_in_mine, chip, proj_sds, deps0)
    for stage, (get_w_in, chips) in enumerate(((near_w_in, others[:2]), (far_w_in, others[2:]))):
        w_in, started = get_w_in(proj)
        (proj,) = _matmul(
            "in_proj_far%d" % stage, "nn", xn, w_in, tm=1024, tn=ns_in, tk=D, prefetch=[chips],
            n_cols=chips.shape[0] * ns_in, b_spec=pl.BlockSpec((None, D, ns_in), lambda i, j, k, o: (o[j], 0, 0)),
            outs=[(proj_sds, pl.BlockSpec((1024, ns_in), lambda i, j, k, o: (i, o[j])))], epilogue=_ep_store,
            deps=[proj, started], alias_dep_to_out=(0, 0), j_outer=True)
    qkv = _qknorm_fwd(proj, gqk)
    att = [_att_fwd(g, S, qkv[g]) for g in range(3)]
    os_, ls_ = [a[0] for a in att], [a[1] for a in att]
    o_a = _mix_fwd(S, os_, ls_)
    o_pre, o_r, states = _ret_fwd(proj, gn_g, gn_b, dk, dv)
    w = late_weights(o_r)
    y, pa, pb = _merge_fwd(o_a, o_r, w["w_proj_a"], w["w_proj_b"], proj, D)
    x1, xn2 = _matmul("out_proj", "nn", y, w["w_out"], tm=1024, tn=D, tk=D,
                      extras=[(x, _mn(1024, D)), (g2, _row(D))],
                      outs=[(_sds((S, D), F32), _mn(1024, D)), (_sds((S, D), CDT), _mn(1024, D))],
                      epilogue=_ep_resid_norm)
    hid, act = _matmul("mlp_up", "nn", xn2, w["w_up"], tm=1024, tn=2048, tk=D, j_outer=True,
                       outs=[(_sds((S, d_ff), CDT), _mn(1024, 2048))] * 2, epilogue=_ep_up)
    dx2, dx2c, loss_row = _matmul(
        "mlp_down_loss", "nn", act, w["w_down"], tm=512, tn=D, tk=d_ff,
        extras=[(x1, _mn(512, D)), (target, _mn(512, D))],
        outs=[(_sds((S, D), F32), _mn(512, D)), (_sds((S, D), CDT), _mn(512, D)), (_sds((1, D), F32), _row(D))],
        epilogue=functools.partial(_ep_down_loss, inv_d=1.0 / D))
    loss = 0.5 * jnp.sum(loss_row) / D

    (dh,) = _matmul("d_hidden", "nt", dx2c, w["w_down"], tm=1024, tn=2048, tk=D, j_outer=True,
                    extras=[(hid, _mn(1024, 2048))], outs=[(_sds((S, d_ff), CDT), _mn(1024, 2048))],
                    epilogue=_ep_dh)
    (gw_down,) = _matmul("dw_down", "tn", act, dx2c, tm=1024, tn=D, tk=2048,
                         outs=[(_sds((d_ff, D), F32), _mn(1024, D))], epilogue=_ep_store)
    (gw_up,) = _matmul("dw_up", "tn", xn2, dh, tm=D, tn=1024, tk=2048,
                       outs=[(_sds((D, d_ff), F32), _mn(D, 1024))], epilogue=_ep_store)
    tok = on_grads({"w_down": gw_down, "w_up": gw_up})
    dx1, dx1c, dg2 = _matmul(
        "d_x1", "nt", dh, w["w_up"], tm=512, tn=D, tk=d_ff,
        extras=[(x1, _mn(512, D)), (g2, _row(D)), (dx2, _mn(512, D))],
        outs=[(_sds((S, D), F32), _mn(512, D)), (_sds((S, D), CDT), _mn(512, D)), (_sds((1, D), F32), _row(D))],
        epilogue=_ep_rms_bwd, deps=[tok])

    gt = 512
    assert (in_w - 2 * D) % gt == 0
    off_a, off_b = (in_w - 2 * D) // gt, (in_w - D) // gt
    dpa, dpb, dga, dgb = _matmul(
        "d_gates", "nt", dx1c, w["w_out"], tm=1024, tn=gt, tk=D,
        extras=[(proj, _mn(1024, gt, off_a)), (proj, _mn(1024, gt, off_b)), (pa, _mn(1024, gt)),
                (pb, _mn(1024, gt))],
        outs=[(_sds((S, D), CDT), _mn(1024, gt))] * 4, epilogue=_ep_gates)
    (gw_out,) = _matmul("dw_out", "tn", y, dx1c, tm=D, tn=D, tk=1024,
                        outs=[(_sds((D, D), F32), _mn(D, D))], epilogue=_ep_store)
    (gw_pa,) = _matmul("dw_proj_a", "tn", o_a, dpa, tm=GW, tn=D, tk=1024,
                       outs=[(_sds((GW, D), F32), _mn(GW, D))], epilogue=_ep_store)
    (gw_pb,) = _matmul("dw_proj_b", "tn", o_r, dpb, tm=1024, tn=D, tk=2048,
                       outs=[(_sds((ret_v_w, D), F32), _mn(1024, D))], epilogue=_ep_store)
    (do_a,) = _matmul("d_o_a", "nt", dpa, w["w_proj_a"], tm=1024, tn=GW, tk=D,
                      outs=[(_sds((S, GW), F32), _mn(1024, GW))], epilogue=_ep_store)
    tok = on_grads({"w_out": gw_out, "w_proj_a": gw_pa, "w_proj_b": gw_pb})
    (d_or,) = _matmul("d_o_r", "nt", dpb, w["w_proj_b"], tm=1024, tn=ret_v_w, tk=D,
                      outs=[(_sds((S, ret_v_w), F32), _mn(1024, ret_v_w))], epilogue=_ep_store, deps=[tok])

    dproj, dgn_g, dgn_b = _ret_bwd(proj, gn_g, gn_b, o_pre, states, d_or, dga, dgb, dk, dv)
    do_gs, c_gs = _mix_bwd(S, os_, ls_, do_a)
    datt_parts = [_att_bwd(g, S, qkv[g], ls_[g], do_gs[g], c_gs[g]) for g in range(3)]
    dproj, dgqk = _qknorm_bwd(proj, gqk, [p[0] for p in datt_parts], [p[1] for p in datt_parts],
                              [p[2] for p in datt_parts], dproj)

    (gw_in,) = _matmul(
        "dw_in", "tn", xn, dproj, tm=512, tn=ns_in, tk=1024,
        outs=[(_sds((N_CHIPS, D, ns_in), F32), pl.BlockSpec((None, 512, ns_in), lambda i, j, k: (j, i, 0)))],
        epilogue=_ep_store)
    tok = on_grads({"w_in": gw_in})
    grad_x, dg1 = _d_x(dproj, w_in, x, g1, dx1, tok)

    smallg = {"norm1_g": dg1, "q_norm_g": dgqk[:, :ATT_W], "k_norm_g": dgqk[:, ATT_W:],
              "ret_gn_g": dgn_g, "ret_gn_b": dgn_b, "norm2_g": dg2}
    return loss, grad_x, smallg


N_CHIPS = 4
N_DEV = 8


def _place():
    x, y, c = lax.axis_index("x"), lax.axis_index("y"), lax.axis_index("c")
    return x, y, c


def _other_chips(x, y):
    out = []
    for fx, fy in ((1, 0), (0, 1), (1, 1)):
        px = 1 - x if fx else x
        py = 1 - y if fy else y
        out.append(((px, py), 2 * px + py))
    return out


SEM_SPEC = pl.BlockSpec(memory_space=pltpu.SEMAPHORE)
ANY_SPEC = pl.BlockSpec(memory_space=pl.ANY)
EFFECT = pltpu.SideEffectType.DATAFLOW_SIDE_EFFECTING


def _ici_copies(kind, srcs, lands, send, recv, which=(0, 1, 2)):
    x, y, c = _place()
    me = 2 * x + y
    out = []
    for w, (s, l) in enumerate(zip(srcs, lands)):
        for j, ((px, py), pidx) in enumerate(_other_chips(x, y)):
            if j not in which:
                continue
            if kind == "gather":
                half = s.shape[0] // 2
                rows = pl.ds(c * half, half)
                src, dst_there, dst_here = s.at[rows, :], l.at[me, rows, :], l.at[pidx, rows, :]
            else:
                src, dst_there, dst_here = s.at[pidx], l.at[me], l.at[pidx]
            out.append((src, dst_there, dst_here, send.at[3 * w + j], recv.at[3 * w + j], (px, py, c)))
    return out


def _exchange_start(name, kind, srcs, land_shapes, which=(0, 1, 2), lands=None):
    n = len(srcs)
    if lands is None:
        lands = [lax.empty(shape, dtype) for shape, dtype in land_shapes]

    def body(*refs):
        src_refs, land_refs = refs[:n], refs[n:2 * n]
        send, recv = refs[2 * n], refs[2 * n + 1]
        token = refs[-1]
        for src, dst, _, ss, rs, dev in _ici_copies(kind, src_refs, land_refs, send, recv, which):
            pltpu.make_async_remote_copy(src_ref=src, dst_ref=dst, send_sem=ss, recv_sem=rs, device_id=dev,
                                         device_id_type=MESH).start()
        token[...] = jnp.zeros_like(token)

    thru = [pltpu.HBM(s.shape, s.dtype) for s in srcs] + [pltpu.HBM(shape, dtype) for shape, dtype in land_shapes]
    res = pl.pallas_call(
        body, name=name,
        out_shape=(pltpu.SemaphoreType.DMA((3 * n,)), pltpu.SemaphoreType.DMA((3 * n,)), *thru, _sds((8, LANES), F32)),
        in_specs=[HBM_SPEC] * (2 * n), out_specs=(SEM_SPEC, SEM_SPEC, *[HBM_SPEC] * (2 * n), VMEM_SPEC),
        input_output_aliases={i: 2 + i for i in range(2 * n)},
        compiler_params=pltpu.CompilerParams(has_side_effects=EFFECT),
    )(*[pltpu.with_memory_space_constraint(s, pltpu.HBM) for s in srcs],
      *[pltpu.with_memory_space_constraint(l, pltpu.HBM) for l in lands])
    return res[0], res[1], list(res[2:2 + n]), list(res[2 + n:2 + 2 * n]), res[-1]


def _exchange_wait(name, kind, send, recv, srcs, lands, after, which=(0, 1, 2)):
    n = len(srcs)

    def body(*refs):
        src_refs, land_refs = refs[:n], refs[n:2 * n]
        send_ref, recv_ref = refs[2 * n], refs[2 * n + 1]
        for src, _, dst, ss, rs, dev in _ici_copies(kind, src_refs, land_refs, send_ref, recv_ref, which):
            cp = pltpu.make_async_remote_copy(src_ref=src, dst_ref=dst, send_sem=ss, recv_sem=rs, device_id=dev,
                                              device_id_type=MESH)
            cp.wait_send()
            cp.wait_recv()

    thru = [pltpu.HBM(t.shape, t.dtype) for t in list(srcs) + list(lands)]
    res = pl.pallas_call(
        body, name=name, out_shape=thru,
        in_specs=[HBM_SPEC] * (2 * n) + [SEM_SPEC, SEM_SPEC, ANY_SPEC], out_specs=[HBM_SPEC] * (2 * n),
        input_output_aliases={i: i for i in range(2 * n)},
        compiler_params=pltpu.CompilerParams(has_side_effects=EFFECT),
    )(*srcs, *lands, send, recv, after)
    return list(res[:n]), list(res[n:])


PAIR_TILE_ELEMS = 1 << 20


def _pair_fill(name, gathered, mine, core, others, chip, write_mine=True):
    k, r, C = gathered.shape
    half = r // 2
    tr = _row_tile(half, C, PAIR_TILE_ELEMS, mult=16)
    nt = half // tr
    n_far = others.shape[0]

    def body(c_ref, o_ref, chip_ref, in_ref, mine_ref, out_ref, slot, send, recv):
        j = pl.program_id(0)
        _sibling_barrier((j == 0) & (pl.program_id(1) == 0))
        b = (j * nt + pl.program_id(1)) % 2
        x, y, c = _place()
        cp = pltpu.make_async_remote_copy(src_ref=in_ref, dst_ref=slot.at[b], send_sem=send.at[b],
                                          recv_sem=recv.at[b], device_id=(x, y, 1 - c), device_id_type=MESH)

        @pl.when(j < n_far)
        def _():
            cp.start()
            cp.wait_recv()
            out_ref[...] = slot[b]
            cp.wait_send()

        @pl.when(j >= n_far)
        def _():
            out_ref[...] = mine_ref[...]

    def far(j):
        return jnp.minimum(j, n_far - 1)

    grid_spec = pltpu.PrefetchScalarGridSpec(
        num_scalar_prefetch=3, grid=(n_far + (2 if write_mine else 0), nt),
        in_specs=[pl.BlockSpec((tr, C), lambda j, i, c, o, m: (
                      (2 * o[far(j)] + c[0]) * nt + jnp.where(j < n_far, i, nt - 1), 0)),
                  pl.BlockSpec((tr, C), lambda j, i, c, o, m: (jnp.where(j < n_far, 0, (j - n_far) * nt + i), 0))],
        out_specs=pl.BlockSpec((tr, C), lambda j, i, c, o, m: (
            jnp.where(j < n_far, 2 * o[far(j)] + 1 - c[0], 2 * m[0] + j - n_far) * nt + i, 0)),
        scratch_shapes=[pltpu.VMEM((2, tr, C), gathered.dtype), pltpu.SemaphoreType.DMA((2,)),
                        pltpu.SemaphoreType.DMA((2,))])
    out = pl.pallas_call(body, name=name, grid_spec=grid_spec, out_shape=_sds((k * r, C), gathered.dtype),
                         input_output_aliases={3: 0}, compiler_params=_params(2, PAIR_FILL_ID))(
                             core, others, chip, gathered.reshape(k * r, C), mine)
    return out.reshape(k, r, C)


def _pair_reduce(name, g, core):
    k, R, C = g.shape
    half = R // 2
    tr = _row_tile(half, C, PAIR_TILE_ELEMS, mult=16)
    nt = half // tr

    def body(c_ref, mine_ref, give_ref, out_ref, wire_ref, stage, slot, send, recv):
        _sibling_barrier((pl.program_id(0) == 0) & (pl.program_id(1) == 0))
        b = (pl.program_id(0) * nt + pl.program_id(1)) % 2
        x, y, c = _place()
        stage[b] = give_ref[...].astype(stage.dtype)
        cp = pltpu.make_async_remote_copy(src_ref=stage.at[b], dst_ref=slot.at[b], send_sem=send.at[b],
                                          recv_sem=recv.at[b], device_id=(x, y, 1 - c), device_id_type=MESH)
        cp.start()
        cp.wait_recv()
        tot = mine_ref[...] + slot[b].astype(F32)
        out_ref[...] = tot
        wire_ref[...] = tot.astype(wire_ref.dtype)
        cp.wait_send()

    blk = (tr, C)
    out_spec = pl.BlockSpec(blk, lambda s, i, c: (s * nt + i, 0))
    grid_spec = pltpu.PrefetchScalarGridSpec(
        num_scalar_prefetch=1, grid=(k, nt),
        in_specs=[pl.BlockSpec(blk, lambda s, i, c: ((2 * s + c[0]) * nt + i, 0)),
                  pl.BlockSpec(blk, lambda s, i, c: ((2 * s + 1 - c[0]) * nt + i, 0))],
        out_specs=[out_spec, out_spec],
        scratch_shapes=[pltpu.VMEM((2, tr, C), CDT), pltpu.VMEM((2, tr, C), CDT), pltpu.SemaphoreType.DMA((2,)),
                        pltpu.SemaphoreType.DMA((2,))])
    g2 = g.reshape(k * R, C)
    out, wire = pl.pallas_call(body, name=name, grid_spec=grid_spec,
                               out_shape=[_sds((k * half, C), F32), _sds((k * half, C), CDT)],
                               compiler_params=_params(2, PAIR_REDUCE_ID))(core, g2, g2)
    return out, wire.reshape(k, half, C)


def _all_reduce_small(v):
    r, cdim = v.shape

    def body(v_ref, o_ref, buf, send, recv):
        x, y, c = _place()
        me = 4 * x + 2 * y + c
        buf[me] = v_ref[...]
        sends = []
        for m in range(1, N_DEV):
            px = 1 - x if m & 4 else x
            py = 1 - y if m & 2 else y
            pc = 1 - c if m & 1 else c
            cp = pltpu.make_async_remote_copy(src_ref=v_ref, dst_ref=buf.at[me], send_sem=send.at[m - 1],
                                              recv_sem=recv.at[m - 1], device_id=(px, py, pc), device_id_type=MESH)
            cp.start()
            sends.append((cp, 4 * px + 2 * py + pc))
        for m, (cp, pidx) in enumerate(sends):
            pltpu.make_async_remote_copy(src_ref=v_ref, dst_ref=buf.at[pidx], send_sem=send.at[m], recv_sem=recv.at[m],
                                         device_id=(x, y, c), device_id_type=MESH).wait_recv()
        for cp, _ in sends:
            cp.wait_send()
        tot = buf[0]
        for k in range(1, N_DEV):
            tot = tot + buf[k]
        o_ref[...] = tot

    return pl.pallas_call(
        body, name="all_reduce_small", in_specs=[VMEM_SPEC], out_specs=VMEM_SPEC,
        out_shape=_sds((r, cdim), F32),
        scratch_shapes=[pltpu.VMEM((N_DEV, r, cdim), F32), pltpu.SemaphoreType.DMA((N_DEV - 1,)),
                        pltpu.SemaphoreType.DMA((N_DEV - 1,))],
    )(v)


def _row_tile(rows, cols, budget_elems=1 << 18, mult=8):
    if rows % mult:
        return rows
    t = max(mult, (budget_elems // cols) // mult * mult)
    while rows % t:
        t -= mult
    return t


def _adamw_update(w, g, m, v):
    nm = ADAM_B1 * m + (1.0 - ADAM_B1) * g
    nv = ADAM_B2 * v + (1.0 - ADAM_B2) * (g * g)
    m_hat = nm / (1.0 - ADAM_B1 ** ADAM_STEP)
    v_hat = nv / (1.0 - ADAM_B2 ** ADAM_STEP)
    return -ADAM_LR * (m_hat / (jnp.sqrt(v_hat) + ADAM_EPS) + ADAM_WD * w), nm, nv


def _adamw(name, w, g, m, v):
    R, C = w.shape
    tr = _row_tile(R, C, 1 << 19)

    def body(w_ref, g_ref, m_ref, v_ref, d_ref, nm_ref, nv_ref):
        d_ref[...], nm_ref[...], nv_ref[...] = _adamw_update(w_ref[...], g_ref[...], m_ref[...], v_ref[...])

    spec = pl.BlockSpec((tr, C), lambda i: (i, 0))
    return pl.pallas_call(body, name=name, grid=(R // tr,), in_specs=[spec] * 4, out_specs=[spec] * 3,
                          out_shape=[_sds((R, C), F32)] * 3, compiler_params=_params(1))(w, g, m, v)


def _sum_share(name, own, by_chip, chip, others, core):
    k, half, C = by_chip.shape
    tr = _row_tile(half, C, PAIR_TILE_ELEMS // 2, mult=16)
    nt = half // tr

    def body(chip_ref, oth_ref, c_ref, own_ref, a_ref, b_ref, cc_ref, g_out, mine, slot, send, recv):
        p = pl.program_id(1)
        _sibling_barrier((pl.program_id(0) == 0) & (p == 0))
        b = pl.program_id(0) % 2
        x, y, c = _place()
        cp = pltpu.make_async_remote_copy(src_ref=mine.at[b], dst_ref=slot.at[b], send_sem=send.at[b],
                                          recv_sem=recv.at[b], device_id=(x, y, 1 - c), device_id_type=MESH)

        @pl.when(p == 0)
        def _():
            tot = ((own_ref[...] + a_ref[...].astype(F32)) + b_ref[...].astype(F32)) + cc_ref[...].astype(F32)
            mine[b] = tot
            cp.start()
            g_out[...] = tot

        @pl.when(p == 1)
        def _():
            cp.wait_recv()
            g_out[...] = slot[b]
            cp.wait_send()

    def piece(j):
        return pl.BlockSpec((tr, C), lambda i, p, chip, oth, c: (oth[j] * nt + i, 0))

    grid_spec = pltpu.PrefetchScalarGridSpec(
        num_scalar_prefetch=3, grid=(nt, 2),
        in_specs=[pl.BlockSpec((tr, C), lambda i, p, chip, oth, c: (chip[0] * nt + i, 0)),
                  piece(0), piece(1), piece(2)],
        out_specs=pl.BlockSpec((tr, C), lambda i, p, chip, oth, c: (
            jnp.where(p == 0, c[0], 1 - c[0]) * nt + i, 0)),
        scratch_shapes=[pltpu.VMEM((2, tr, C), F32), pltpu.VMEM((2, tr, C), F32), pltpu.SemaphoreType.DMA((2,)),
                        pltpu.SemaphoreType.DMA((2,))])
    by2 = by_chip.reshape(k * half, C)
    return pl.pallas_call(body, name=name, grid_spec=grid_spec, out_shape=_sds((2 * half, C), F32),
                          compiler_params=_params(2, SUM_SHARE_ID))(chip, others, core, own, by2, by2, by2)


BIG = ("w_in", "w_proj_a", "w_proj_b", "w_out", "w_up", "w_down")
COL_SHARDED = ("w_in", "w_proj_a", "w_up")
SMALL = ("norm1_g", "q_norm_g", "k_norm_g", "ret_gn_g", "ret_gn_b", "norm2_g")
ALL_W = ("norm1_g", "w_in", "q_norm_g", "k_norm_g", "ret_gn_g", "ret_gn_b", "w_proj_a", "w_proj_b", "w_out",
         "norm2_g", "w_up", "w_down")
LANES = 128


def _to_full(name, gathered):
    k, r, c = gathered.shape
    if name in COL_SHARDED:
        return gathered.transpose(1, 0, 2).reshape(r, k * c)
    return gathered.reshape(k * r, c)


def _to_shard_major(name, full):
    if name in COL_SHARDED:
        r, c4 = full.shape
        return full.reshape(r, N_CHIPS, c4 // N_CHIPS).transpose(1, 0, 2)
    r4, c = full.shape
    return full.reshape(N_CHIPS, r4 // N_CHIPS, c)


def kernel(x, norm1_g, w_in, q_norm_g, k_norm_g, ret_gn_g, ret_gn_b, w_proj_a, w_proj_b, w_out, norm2_g, w_up, w_down, loss_target, m_norm1_g, m_w_in, m_q_norm_g, m_k_norm_g, m_ret_gn_g, m_ret_gn_b, m_w_proj_a, m_w_proj_b, m_w_out, m_norm2_g, m_w_up, m_w_down, v_norm1_g, v_w_in, v_q_norm_g, v_k_norm_g, v_ret_gn_g, v_ret_gn_b, v_w_proj_a, v_w_proj_b, v_w_out, v_norm2_g, v_w_up, v_w_down):
    weights = dict(norm1_g=norm1_g, w_in=w_in, q_norm_g=q_norm_g, k_norm_g=k_norm_g, ret_gn_g=ret_gn_g,
                   ret_gn_b=ret_gn_b, w_proj_a=w_proj_a, w_proj_b=w_proj_b, w_out=w_out, norm2_g=norm2_g,
                   w_up=w_up, w_down=w_down)
    moments_m = dict(norm1_g=m_norm1_g, w_in=m_w_in, q_norm_g=m_q_norm_g, k_norm_g=m_k_norm_g, ret_gn_g=m_ret_gn_g,
                     ret_gn_b=m_ret_gn_b, w_proj_a=m_w_proj_a, w_proj_b=m_w_proj_b, w_out=m_w_out,
                     norm2_g=m_norm2_g, w_up=m_w_up, w_down=m_w_down)
    moments_v = dict(norm1_g=v_norm1_g, w_in=v_w_in, q_norm_g=v_q_norm_g, k_norm_g=v_k_norm_g, ret_gn_g=v_ret_gn_g,
                     ret_gn_b=v_ret_gn_b, w_proj_a=v_w_proj_a, w_proj_b=v_w_proj_b, w_out=v_w_out,
                     norm2_g=v_norm2_g, w_up=v_w_up, w_down=v_w_down)

    mx, my = lax.axis_index("x"), lax.axis_index("y")
    core = lax.axis_index("c").astype(jnp.int32).reshape(1)
    chip = (2 * mx + my).astype(jnp.int32).reshape(1)
    others = jnp.stack([2 * (1 - mx) + my, 2 * mx + 1 - my, 2 * (1 - mx) + 1 - my]).astype(jnp.int32)
    shards = {n: weights[n][0].astype(CDT) for n in BIG}
    def start_gather(name, names):
        return _exchange_start(name, "gather", [shards[n] for n in names],
                               [((N_CHIPS,) + shards[n].shape, CDT) for n in names])

    w_in_shape = [((N_CHIPS,) + shards["w_in"].shape, CDT)]
    n_send, n_recv, n_srcs, n_lands, n_token = _exchange_start(
        "gather_w_in_near_start", "gather", [shards["w_in"]], w_in_shape, which=(0, 1))
    late = [n for n in BIG if n != "w_in"]
    flight = {}

    def near_w_in(after):
        srcs, lands = _exchange_wait("gather_w_in_near_wait", "gather", n_send, n_recv, n_srcs, n_lands, after,
                                     which=(0, 1))
        d_send, d_recv, d_srcs, d_lands, _ = _exchange_start(
            "gather_w_in_diag_start", "gather", srcs, w_in_shape, which=(2,), lands=lands)
        flight["late"] = start_gather("gather_late_start", late)
        w_near = _pair_fill("pair_fill_w_in_near", d_lands[0], d_srcs[0], core, others[:2], chip)
        flight["diag"] = (d_send, d_recv, d_srcs, [w_near])
        return w_near, flight["late"][-1]

    def far_w_in(after):
        d_send, d_recv, d_srcs, d_lands = flight["diag"]
        srcs, lands = _exchange_wait("gather_w_in_diag_wait", "gather", d_send, d_recv, d_srcs, d_lands, after,
                                     which=(2,))
        return _pair_fill("pair_fill_w_in_diag", lands[0], srcs[0], core, others[2:], chip, write_mine=False), None

    def late_weights(after):
        l_send, l_recv, l_srcs, l_lands, _ = flight["late"]
        srcs, lands = _exchange_wait("gather_late_wait", "gather", l_send, l_recv, l_srcs, l_lands, after)
        out = {}
        for n, mine, land in zip(late, srcs, lands):
            out[n] = _to_full(n, _pair_fill("pair_fill_%s" % n, land, mine, core, others, chip))
        return out

    pending = []

    def on_grads(group):
        names = list(group)
        red = [_pair_reduce("pair_reduce_%s" % n, g if g.ndim == 3 else _to_shard_major(n, g), core)
               for n, g in group.items()]
        wires = [wire for _, wire in red]
        send, recv, srcs, lands, token = _exchange_start(
            "scatter_start_%s" % names[0], "scatter", wires, [(wire.shape, wire.dtype) for wire in wires])
        pending.append((names, [own for own, _ in red], send, recv, srcs, lands))
        return token

    small = {n: weights[n].reshape(1, -1) for n in SMALL}

    loss, grad_x, small_g = _local_step(x[0], loss_target[0], n_srcs[0], chip, others, near_w_in, far_w_in, small,
                                        late_weights, on_grads, deps0=[n_token])

    out_g, out_d, out_m, out_v = {}, {}, {}, {}
    for names, owns, send, recv, srcs, lands in pending:
        _, got = _exchange_wait("scatter_wait_%s" % names[0], "scatter", send, recv, srcs, lands, grad_x)
        for n, own, by_chip in zip(names, owns, got):
            shape = weights[n].shape
            g2 = _sum_share("sum_share_%s" % n, own, by_chip, chip, others, core)
            d, nm, nv = _adamw("adamw_%s" % n, weights[n][0], g2, moments_m[n][0], moments_v[n][0])
            out_g[n], out_d[n], out_m[n], out_v[n] = (t.reshape(shape) for t in (g2, d, nm, nv))

    packed = jnp.concatenate([small_g[n].reshape(1, -1) for n in SMALL], axis=1).reshape(-1, LANES)
    loss_tile = jnp.zeros((8, LANES), F32).at[0, 0].set(loss)
    red = _all_reduce_small(jnp.concatenate([packed, loss_tile], axis=0))
    loss = red[packed.shape[0], 0]
    red = red[:packed.shape[0]].reshape(1, -1)
    off = 0
    for n in SMALL:
        shape = weights[n].shape
        row = (1, weights[n].size)
        g2 = red[:, off:off + row[1]]
        off += row[1]
        d, nm, nv = _adamw("adamw_%s" % n, weights[n].reshape(row), g2, moments_m[n].reshape(row),
                           moments_v[n].reshape(row))
        out_g[n], out_d[n], out_m[n], out_v[n] = (t.reshape(shape) for t in (g2, d, nm, nv))

    return (loss, grad_x[None], *[out_g[n] for n in ALL_W], *[out_d[n] for n in ALL_W],
            *[out_m[n] for n in ALL_W], *[out_v[n] for n in ALL_W])
```

```python
import functools

import jax
import jax.numpy as jnp
from jax import lax
from jax.experimental import pallas as pl
from jax.experimental.pallas import tpu as pltpu

CDT = jnp.bfloat16
F32 = jnp.float32
EPS = 1e-6

ATT_GROUPS = ((128, 1), (512, 4), (2048, 16))
ATT_HPG = 4
ATT_HEADS = 12
HD = 128
BLK = 128
ATT_W = ATT_HEADS * HD
GW = ATT_HPG * HD
RET_HEADS = 4

ADAM_LR = 0.001
ADAM_B1 = 0.9
ADAM_B2 = 0.999
ADAM_EPS = 1e-08
ADAM_WD = 0.01
ADAM_STEP = 10

VMEM_LIMIT_BYTES = 56 * 1024 * 1024
MESH = pl.DeviceIdType.MESH
HBM_SPEC = pl.BlockSpec(memory_space=pltpu.HBM)
VMEM_SPEC = pl.BlockSpec(memory_space=pltpu.VMEM)


def _params(n_axes, collective_id=None):
    return pltpu.CompilerParams(dimension_semantics=("arbitrary",) * n_axes,
                                vmem_limit_bytes=VMEM_LIMIT_BYTES, collective_id=collective_id)


PAIR_FILL_ID, PAIR_REDUCE_ID, SUM_SHARE_ID = 1, 2, 3


def _sibling_barrier(first_step):
    @pl.when(first_step)
    def _():
        sem = pltpu.get_barrier_semaphore()
        x, y, c = lax.axis_index("x"), lax.axis_index("y"), lax.axis_index("c")
        pl.semaphore_signal(sem, inc=1, device_id=(x, y, 1 - c), device_id_type=pl.DeviceIdType.MESH)
        pl.semaphore_wait(sem, 1)


def _dot_nn(a, b):
    return jnp.dot(a, b, preferred_element_type=F32)


def _dot_nt(a, b):
    return lax.dot_general(a, b, (((1,), (1,)), ((), ())), preferred_element_type=F32)


def _dot_tn(a, b):
    return lax.dot_general(a, b, (((0,), (0,)), ((), ())), preferred_element_type=F32)


def _sigmoid(v):
    return 1.0 / (1.0 + jnp.exp(-v))


def _matmul(name, mode, a, b, *, tm, tn, tk, extras=(), outs, epilogue, deps=(), b_spec=None, n_cols=None,
            prefetch=(), alias_dep_to_out=None, j_outer=False):
    deps = [d for d in deps if d is not None]
    if mode == "tn":
        K, M = a.shape
    else:
        M, K = a.shape
    if b_spec is None:
        (N, K2) = b.shape if mode == "nt" else b.shape[::-1]
        assert K == K2, (name, a.shape, b.shape)
        if mode == "nt":
            b_spec = pl.BlockSpec((tn, tk), lambda i, j, k, *p: (j, k))
        else:
            b_spec = pl.BlockSpec((tk, tn), lambda i, j, k, *p: (k, j))
    else:
        N = n_cols
    assert M % tm == 0 and N % tn == 0 and K % tk == 0, (name, a.shape, b.shape)
    ni, nj, nk = M // tm, N // tn, K // tk
    if mode == "tn":
        a_spec = pl.BlockSpec((tk, tm), lambda i, j, k, *p: (k, i))
    else:
        a_spec = pl.BlockSpec((tm, tk), lambda i, j, k, *p: (i, k))
    dot = {"nn": _dot_nn, "nt": _dot_nt, "tn": _dot_tn}[mode]
    n_ex, n_out, n_dep, n_pre = len(extras), len(outs), len(deps), len(prefetch)
    grid = (ni, nj, nk)
    if j_outer:
        grid = (nj, ni, nk)

        def swapped(spec):
            return pl.BlockSpec(spec.block_shape, lambda j, i, k, *p: spec.index_map(i, j, k, *p))

        a_spec, b_spec = swapped(a_spec), swapped(b_spec)
        extras = [(e, swapped(s)) for e, s in extras]
        outs = [(o, swapped(s)) for o, s in outs]

    def body(*refs):
        refs = refs[n_pre:]
        a_ref, b_ref = refs[0], refs[1]
        ex = refs[2:2 + n_ex]
        out = refs[2 + n_ex + n_dep:2 + n_ex + n_dep + n_out]
        acc = refs[-1] if nk > 1 else None
        i = pl.program_id(1 if j_outer else 0)
        k = pl.program_id(2)
        if nk == 1:
            epilogue(dot(a_ref[...].astype(CDT), b_ref[...].astype(CDT)), ex, out, i)
            return

        @pl.when(k == 0)
        def _():
            acc[...] = jnp.zeros_like(acc)

        acc[...] += dot(a_ref[...].astype(CDT), b_ref[...].astype(CDT))

        @pl.when(k == nk - 1)
        def _():
            epilogue(acc[...], ex, out, i)

    grid_spec = pltpu.PrefetchScalarGridSpec(
        num_scalar_prefetch=n_pre, grid=grid,
        in_specs=[a_spec, b_spec] + [s for _, s in extras] + [pl.BlockSpec(memory_space=pl.ANY)] * n_dep,
        out_specs=[s for _, s in outs],
        scratch_shapes=[pltpu.VMEM((tm, tn), F32)] if nk > 1 else [])
    aliases = {}
    if alias_dep_to_out is not None:
        aliases = {n_pre + 2 + n_ex + alias_dep_to_out[0]: alias_dep_to_out[1]}
    res = pl.pallas_call(
        body, name=name, grid_spec=grid_spec, out_shape=[o for o, _ in outs], input_output_aliases=aliases,
        compiler_params=_params(3),
    )(*prefetch, a, b, *[e for e, _ in extras], *deps)
    return res


def _mn(tm, tn, col_off=0):
    return pl.BlockSpec((tm, tn), lambda i, j, k, *p: (i, j + col_off))


def _row(tn):
    return pl.BlockSpec((1, tn), lambda i, j, k, *p: (0, j))


def _ep_store(acc, ex, out, i):
    out[0][...] = acc.astype(out[0].dtype)


def _ep_resid_norm(acc, ex, out, i):
    x1 = ex[0][...] + acc
    out[0][...] = x1
    rstd = lax.rsqrt(jnp.mean(x1 * x1, axis=-1, keepdims=True) + EPS)
    out[1][...] = (x1 * rstd * ex[1][...]).astype(out[1].dtype)


def _ep_up(acc, ex, out, i):
    out[0][...] = acc.astype(out[0].dtype)
    r = jnp.maximum(acc, 0.0)
    out[1][...] = (r * r).astype(out[1].dtype)


def _ep_down_loss(acc, ex, out, i, inv_d):
    diff = (ex[0][...] + acc) - ex[1][...]
    dx2 = diff * inv_d
    out[0][...] = dx2
    out[1][...] = dx2.astype(out[1].dtype)

    @pl.when(i == 0)
    def _():
        out[2][...] = jnp.zeros_like(out[2])

    out[2][...] += jnp.sum(diff * diff, axis=0, keepdims=True)


def _ep_dh(acc, ex, out, i):
    h = ex[0][...].astype(F32)
    out[0][...] = (acc * (2.0 * jnp.maximum(h, 0.0))).astype(out[0].dtype)


def _ep_rms_bwd(acc, ex, out, i):
    x = ex[0][...]
    g = ex[1][...]
    rstd = lax.rsqrt(jnp.mean(x * x, axis=-1, keepdims=True) + EPS)
    xh = x * rstd
    dxh = acc * g
    dx = ex[2][...] + rstd * (dxh - xh * jnp.mean(dxh * xh, axis=-1, keepdims=True))
    out[0][...] = dx
    for copy in out[1:-1]:
        copy[...] = dx.astype(copy.dtype)
    dg = out[-1]

    @pl.when(i == 0)
    def _():
        dg[...] = jnp.zeros_like(dg)

    dg[...] += jnp.sum(acc * xh, axis=0, keepdims=True)


def _ep_gates(acc, ex, out, i):
    sa = _sigmoid(ex[0][...].astype(F32))
    sb = _sigmoid(ex[1][...].astype(F32))
    dpa = acc * sa
    dpb = acc * sb
    out[0][...] = dpa.astype(out[0].dtype)
    out[1][...] = dpb.astype(out[1].dtype)
    out[2][...] = (dpa * ex[2][...].astype(F32) * (1.0 - sa)).astype(out[2].dtype)
    out[3][...] = (dpb * ex[3][...].astype(F32) * (1.0 - sb)).astype(out[3].dtype)


def _sds(shape, dtype):
    return jax.ShapeDtypeStruct(shape, dtype)


def _in_proj_mine(x, g, w_mine, chip, proj_sds, deps, tm=512):
    S, D = x.shape
    ns = w_mine.shape[1]
    deps = [d for d in deps if d is not None]

    def body(c_ref, x_ref, g_ref, w_ref, *rest):
        xn_ref, proj_ref = rest[len(deps)], rest[len(deps) + 1]
        xv = x_ref[...]
        rstd = lax.rsqrt(jnp.mean(xv * xv, axis=-1, keepdims=True) + EPS)
        xn = (xv * rstd * g_ref[...]).astype(xn_ref.dtype)
        xn_ref[...] = xn
        proj_ref[...] = _dot_nn(xn, w_ref[...]).astype(proj_ref.dtype)

    grid_spec = pltpu.PrefetchScalarGridSpec(
        num_scalar_prefetch=1, grid=(S // tm,),
        in_specs=[pl.BlockSpec((tm, D), lambda i, c: (i, 0)), pl.BlockSpec((1, D), lambda i, c: (0, 0)),
                  pl.BlockSpec((D, ns), lambda i, c: (0, 0))] + [pl.BlockSpec(memory_space=pl.ANY)] * len(deps),
        out_specs=[pl.BlockSpec((tm, D), lambda i, c: (i, 0)), pl.BlockSpec((tm, ns), lambda i, c: (i, c[0]))])
    return pl.pallas_call(body, name="in_proj_mine", grid_spec=grid_spec, out_shape=[_sds((S, D), CDT), proj_sds],
                          compiler_params=_params(1))(chip, x, g, w_mine, *deps)


def _rm_shape(S, d, width):
    return (S, width) if d == 1 else (d, S // d, width)


def _rm_spec(tm, d, width):
    if d == 1:
        return pl.BlockSpec((tm, width), lambda i: (i, 0))
    return pl.BlockSpec((d, tm // d, width), lambda i: (0, i, 0))


def _rm_put(dst_ref, cols, buf_ref, d):
    if d == 1:
        dst_ref[:, cols] = buf_ref[...].astype(dst_ref.dtype)
        return
    m = buf_ref.shape[0] // d
    for r in range(d):
        dst_ref[r, :, cols] = buf_ref[pl.ds(r, m, stride=d), :].astype(dst_ref.dtype)


def _rm_reader(buf_ref, src_ref, d):
    if d == 1:
        return lambda s, rows: src_ref[rows, s * HD:(s + 1) * HD].astype(F32)
    m = buf_ref.shape[1] // d
    for s in range(buf_ref.shape[0]):
        for r in range(d):
            buf_ref.at[s][pl.ds(r, m, stride=d), :] = src_ref[r, :, s * HD:(s + 1) * HD].astype(F32)
    return lambda s, rows: buf_ref.at[s][rows, :]


def _qknorm_fwd(proj, gqk, tm=1024):
    S = proj.shape[0]
    W = 2 * ATT_W
    dil = [d for _, d in ATT_GROUPS]

    def body(p_ref, g_ref, o0, o1, o2, buf):
        outs = (o0, o1, o2)
        for hd in range(3 * ATT_HEADS):
            which, head = hd // ATT_HEADS, hd % ATT_HEADS
            grp, slot = head // ATT_HPG, head % ATT_HPG
            cols = slice(hd * HD, (hd + 1) * HD)

            def chunk(rows, which=which, cols=cols):
                v = p_ref[rows, cols].astype(F32)
                if which < 2:
                    rstd = lax.rsqrt(jnp.mean(v * v, axis=-1, keepdims=True) + EPS)
                    v = v * rstd * g_ref[:, cols]
                buf[rows, :] = v

            chunk(slice(None))
            _rm_put(outs[grp], slice(which * GW + slot * HD, which * GW + (slot + 1) * HD), buf, dil[grp])

    return pl.pallas_call(
        body, name="qknorm_fwd", grid=(S // tm,),
        in_specs=[pl.BlockSpec((tm, 3 * ATT_W), lambda i: (i, 0)), pl.BlockSpec((1, W), lambda i: (0, 0))],
        out_specs=[_rm_spec(tm, d, 3 * GW) for d in dil],
        out_shape=[_sds(_rm_shape(S, d, 3 * GW), CDT) for d in dil],
        scratch_shapes=[pltpu.VMEM((tm, HD), F32)],
        compiler_params=_params(1))(proj, gqk)


def _qknorm_bwd(proj, gqk, dqs, dks, dvs, dproj, tm=512):
    S = proj.shape[0]
    W = 2 * ATT_W
    dil = [d for _, d in ATT_GROUPS]

    def body(p_ref, g_ref, *refs):
        ins = refs[0:9]
        o_ref, dg_ref = refs[10], refs[11]
        bufs = refs[12:21]
        i = pl.program_id(0)

        @pl.when(i == 0)
        def _():
            dg_ref[...] = jnp.zeros_like(dg_ref)

        nat = [_rm_reader(bufs[j], ins[j], dil[j % 3]) for j in range(9)]
        dq_get, dk_get, dv_get = nat[0:3], nat[3:6], nat[6:9]
        for hd in range(2 * ATT_HEADS):
            sl = slice(hd * HD, (hd + 1) * HD)
            head = hd % ATT_HEADS
            grp, slot = head // ATT_HPG, head % ATT_HPG
            get = (dq_get if hd < ATT_HEADS else dk_get)[grp]

            def chunk(rows, sl=sl, slot=slot, get=get):
                dn = get(slot, rows)
                v = p_ref[rows, sl].astype(F32)
                rstd = lax.rsqrt(jnp.mean(v * v, axis=-1, keepdims=True) + EPS)
                vh = v * rstd
                dg_ref[:, sl] += jnp.sum(dn * vh, axis=0, keepdims=True)
                dvh = dn * g_ref[:, sl]
                o_ref[rows, sl] = (rstd * (dvh - vh * jnp.mean(dvh * vh, axis=-1, keepdims=True))).astype(o_ref.dtype)

            chunk(slice(None))
        for head in range(ATT_HEADS):
            grp, slot = head // ATT_HPG, head % ATT_HPG
            o_ref[:, W + head * HD:W + (head + 1) * HD] = dv_get[grp](slot, slice(None)).astype(o_ref.dtype)

    return pl.pallas_call(
        body, name="qknorm_bwd", grid=(S // tm,),
        in_specs=[pl.BlockSpec((tm, W), lambda i: (i, 0)), pl.BlockSpec((1, W), lambda i: (0, 0))]
        + [_rm_spec(tm, d, GW) for d in dil] * 3 + [pl.BlockSpec(memory_space=pl.ANY)],
        out_specs=[pl.BlockSpec((tm, 3 * ATT_W), lambda i: (i, 0)), pl.BlockSpec((1, W), lambda i: (0, 0))],
        out_shape=[_sds(dproj.shape, dproj.dtype), _sds((1, W), F32)],
        scratch_shapes=[pltpu.VMEM((ATT_HPG, tm, HD), F32)] * 9,
        input_output_aliases={11: 0},
        compiler_params=_params(1))(proj, gqk, *dqs, *dks, *dvs, dproj)


def _att_mask(n):
    qi = lax.broadcasted_iota(jnp.int32, (BLK, 2 * BLK), 0)
    kj = lax.broadcasted_iota(jnp.int32, (BLK, 2 * BLK), 1)
    dist = BLK + qi - kj
    valid_all = (dist >= 0) & (dist <= BLK)
    return valid_all & ((kj >= BLK) | (n > 0)), valid_all, dist.astype(F32)


def _att_slopes(grp):
    return [2.0 ** (-8.0 * (grp * ATT_HPG + hh + 1) / ATT_HEADS) for hh in range(ATT_HPG)]


ATT_WIDTH = 8
ATT_BWD_SEGMENTS = 4


def _att_planes(d, width):
    return d if d > 1 else width


def _att_step_planes(d, width):
    return min(_att_planes(d, width), width)


def _att_3d(a, d, width):
    return a.reshape(width, a.shape[0] // width, a.shape[1]) if d == 1 else a


def _att_spec(R, row_fn, col=0):
    return pl.BlockSpec((R, BLK, GW), lambda r, n: (r, row_fn(n), col))


def _att_chains(R):
    return [(rr * ATT_HPG + hh, rr, slice(hh * HD, (hh + 1) * HD), hh) for rr in range(R) for hh in range(ATT_HPG)]


def _att_qkv_specs(R, nb):
    last = nb - 1

    def cur(n):
        return jnp.minimum(n, last)

    def prev(n):
        return jnp.maximum(jnp.minimum(n, last) - 1, 0)

    return [_att_spec(R, cur, 0), _att_spec(R, prev, 1), _att_spec(R, cur, 1), _att_spec(R, prev, 2),
            _att_spec(R, cur, 2), _att_spec(R, lambda n: last, 1), _att_spec(R, lambda n: last, 2)]


def _att_prev(seg, n, prev_ref, last_ref, rr, sl):
    t = prev_ref[rr, :, sl]
    if seg and rr > 0:
        t = jnp.where(n == 0, last_ref[rr - 1, :, sl], t)
    return t


def _att_fwd(grp, S, qkv):
    _, d = ATT_GROUPS[grp]
    seg = d == 1
    width = ATT_WIDTH
    P = _att_planes(d, width)
    L = S // P
    nb = L // BLK
    R = _att_step_planes(d, width)
    assert P % R == 0 and (not seg or P == R)
    slopes = _att_slopes(grp)
    scale = HD ** -0.5
    chains = _att_chains(R)

    def body(q_ref, kp_ref, kc_ref, vp_ref, vc_ref, kl_ref, vl_ref, o_ref, l_ref, s_buf, p_buf, den_buf):
        n = pl.program_id(1)
        valid, valid_all, distf = _att_mask(n)
        for c, rr, sl, hh in chains:
            k = jnp.concatenate([_att_prev(seg, n, kp_ref, kl_ref, rr, sl), kc_ref[rr, :, sl]], axis=0)
            s_buf[c] = _dot_nt(q_ref[rr, :, sl], k)
        for c, rr, sl, hh in chains:
            s = s_buf[c] * scale + (-slopes[hh] * d) * distf
            s = jnp.where(valid_all if seg and rr > 0 else valid, s, -1e30)
            m = jnp.max(s, axis=-1, keepdims=True)
            p = jnp.exp(s - m)
            den = jnp.sum(p, axis=-1, keepdims=True)
            p_buf[c] = p.astype(CDT)
            den_buf[c] = jnp.broadcast_to(den, (BLK, HD))
            l_ref[rr, :, sl] = jnp.broadcast_to(m + jnp.log(den), (BLK, HD))
        for c, rr, sl, hh in chains:
            v = jnp.concatenate([_att_prev(seg, n, vp_ref, vl_ref, rr, sl), vc_ref[rr, :, sl]], axis=0)
            o_ref[rr, :, sl] = _dot_nn(p_buf[c], v) / den_buf[c]

    out_spec = _att_spec(R, lambda n: n)
    n_ch = len(chains)
    q3 = _att_3d(qkv, d, width)
    o, l = pl.pallas_call(
        body, name="att_fwd_g%d" % grp, grid=(P // R, nb),
        in_specs=_att_qkv_specs(R, nb),
        out_specs=[out_spec, out_spec],
        out_shape=[_sds((P, L, GW), F32)] * 2,
        scratch_shapes=[pltpu.VMEM((n_ch, BLK, 2 * BLK), F32), pltpu.VMEM((n_ch, BLK, 2 * BLK), CDT),
                        pltpu.VMEM((n_ch, BLK, HD), F32)],
        compiler_params=_params(2),
    )(*[q3] * 7)
    return o.reshape(_rm_shape(S, d, GW)), l.reshape(_rm_shape(S, d, GW))


def _att_bwd(grp, S, qkv, lse, do_g, c_g):
    _, d = ATT_GROUPS[grp]
    seg = d == 1
    width = ATT_BWD_SEGMENTS if seg else ATT_WIDTH
    P = _att_planes(d, width)
    L = S // P
    nb = L // BLK
    R = _att_step_planes(d, width)
    assert P % R == 0 and (not seg or P == R)
    slopes = _att_slopes(grp)
    scale = HD ** -0.5
    last = nb - 1
    chains = _att_chains(R)

    def body(q_ref, kp_ref, kc_ref, vp_ref, vc_ref, kl_ref, vl_ref, l_ref, do_ref, c_ref, dq_ref, dk_ref, dv_ref,
             ck, cv, fk, fv, s_buf, dp_buf, p_buf, ds_buf):
        n = pl.program_id(1)

        @pl.when(n == 0)
        def _():
            for buf in (ck, cv, fk, fv):
                buf[...] = jnp.zeros_like(buf)

        @pl.when(n < nb)
        def _():
            valid, valid_all, distf = _att_mask(n)
            for c, rr, sl, hh in chains:
                k = jnp.concatenate([_att_prev(seg, n, kp_ref, kl_ref, rr, sl), kc_ref[rr, :, sl]], axis=0)
                v = jnp.concatenate([_att_prev(seg, n, vp_ref, vl_ref, rr, sl), vc_ref[rr, :, sl]], axis=0)
                s_buf[c] = _dot_nt(q_ref[rr, :, sl], k)
                dp_buf[c] = _dot_nt(do_ref[rr, :, sl], v)
            for c, rr, sl, hh in chains:
                s = s_buf[c] * scale + (-slopes[hh] * d) * distf
                p = jnp.where(valid_all if seg and rr > 0 else valid, jnp.exp(s - l_ref[rr, :, sl][:, 0:1]), 0.0)
                p_buf[c] = p.astype(CDT)
                ds_buf[c] = (p * (dp_buf[c] + c_ref[rr, :, sl][:, 0:1]) * scale).astype(CDT)
            for c, rr, sl, hh in chains:
                k = jnp.concatenate([_att_prev(seg, n, kp_ref, kl_ref, rr, sl), kc_ref[rr, :, sl]], axis=0)
                ds = ds_buf[c]
                dq_ref[rr, :, sl] = _dot_nn(ds, k)
                dk = _dot_tn(ds, q_ref[rr, :, sl])
                dv = _dot_tn(p_buf[c], do_ref[rr, :, sl])
                dk_ref[rr, :, sl] = ck[rr, :, sl] + dk[0:BLK]
                dv_ref[rr, :, sl] = cv[rr, :, sl] + dv[0:BLK]
                ck[rr, :, sl] = dk[BLK:2 * BLK]
                cv[rr, :, sl] = dv[BLK:2 * BLK]
                if seg and rr > 0:
                    @pl.when(n == 0)
                    def _(rr=rr, sl=sl, dk=dk, dv=dv):
                        fk[rr - 1, :, sl] = dk[0:BLK]
                        fv[rr - 1, :, sl] = dv[0:BLK]

        @pl.when(n == nb)
        def _():
            dk_ref[...] = ck[...] + fk[...]
            dv_ref[...] = cv[...] + fv[...]

    blk = (R, BLK, GW)
    at_q = _att_spec(R, lambda n: jnp.minimum(n, last))
    behind = _att_spec(R, lambda n: jnp.maximum(n - 1, 0))
    n_ch = len(chains)
    q3 = _att_3d(qkv, d, width)
    res = pl.pallas_call(
        body, name="att_bwd_g%d" % grp, grid=(P // R, nb + 1),
        in_specs=_att_qkv_specs(R, nb) + [at_q, at_q, at_q],
        out_specs=[at_q, behind, behind],
        out_shape=[_sds((P, L, GW), F32)] * 3,
        scratch_shapes=[pltpu.VMEM(blk, F32)] * 4
        + [pltpu.VMEM((n_ch, BLK, 2 * BLK), F32), pltpu.VMEM((n_ch, BLK, 2 * BLK), F32),
           pltpu.VMEM((n_ch, BLK, 2 * BLK), CDT), pltpu.VMEM((n_ch, BLK, 2 * BLK), CDT)],
        compiler_params=_params(2),
    )(*[q3] * 7, _att_3d(lse, d, width), _att_3d(do_g, d, width), _att_3d(c_g, d, width))
    return [t.reshape(_rm_shape(S, d, GW)) for t in res]


def _mix_alpha(l0, l1, l2):
    mx = jnp.maximum(jnp.maximum(l0, l1), l2)
    e = [jnp.exp(l0 - mx), jnp.exp(l1 - mx), jnp.exp(l2 - mx)]
    tot = e[0] + e[1] + e[2]
    return [ei / tot for ei in e]


def _mix_fwd(S, os_, ls_, tm=512):
    dil = [d for _, d in ATT_GROUPS]

    def body(*refs):
        out, bufs = refs[6], refs[7:13]
        get = [_rm_reader(bufs[j], refs[j], dil[j % 3]) for j in range(6)]
        rows = slice(None)
        for s in range(ATT_HPG):
            al = _mix_alpha(*[get[3 + g](s, rows) for g in range(3)])
            mixed = al[0] * get[0](s, rows) + al[1] * get[1](s, rows) + al[2] * get[2](s, rows)
            out[:, s * HD:(s + 1) * HD] = mixed.astype(out.dtype)

    specs = [_rm_spec(tm, d, GW) for d in dil]
    return pl.pallas_call(
        body, name="mix_fwd", grid=(S // tm,), in_specs=specs * 2, out_specs=pl.BlockSpec((tm, GW), lambda i: (i, 0)),
        out_shape=_sds((S, GW), CDT), scratch_shapes=[pltpu.VMEM((ATT_HPG, tm, HD), F32)] * 6,
        compiler_params=_params(1))(*os_, *ls_)


def _mix_bwd(S, os_, ls_, do_a, tm=512):
    dil = [d for _, d in ATT_GROUPS]

    def body(*refs):
        d_ref, outs, bufs, tmps = refs[6], refs[7:13], refs[13:19], refs[19:25]
        get = [_rm_reader(bufs[j], refs[j], dil[j % 3]) for j in range(6)]
        for s in range(ATT_HPG):
            cols = slice(s * HD, (s + 1) * HD)

            def chunk(rows, s=s, cols=cols):
                al = _mix_alpha(*[get[3 + g](s, rows) for g in range(3)])
                dv = d_ref[rows, cols]
                o_a = al[0] * get[0](s, rows) + al[1] * get[1](s, rows) + al[2] * get[2](s, rows)
                dsum = jnp.sum(dv * o_a, axis=-1, keepdims=True)
                for g in range(3):
                    tmps[g][rows, :] = al[g] * dv
                    tmps[3 + g][rows, :] = -(al[g] * dsum)

            chunk(slice(None))
            for j in range(6):
                _rm_put(outs[j], cols, tmps[j], dil[j % 3])

    specs = [_rm_spec(tm, d, GW) for d in dil]
    res = pl.pallas_call(
        body, name="mix_bwd", grid=(S // tm,), in_specs=specs * 2 + [pl.BlockSpec((tm, GW), lambda i: (i, 0))],
        out_specs=specs * 2,
        out_shape=[_sds(_rm_shape(S, d, GW), CDT) for d in dil] + [_sds(_rm_shape(S, d, GW), F32) for d in dil],
        scratch_shapes=[pltpu.VMEM((ATT_HPG, tm, HD), F32)] * 6 + [pltpu.VMEM((tm, HD), F32)] * 6,
        compiler_params=_params(1))(*os_, *ls_, do_a)
    return res[:3], res[3:]


def _ret_tables(dk):
    H, C = RET_HEADS, BLK
    log_g = jnp.log(1.0 - 2.0 ** (-5.0 - jnp.arange(H, dtype=F32)))
    idx = jnp.arange(C, dtype=F32)
    diff = idx[:, None] - idx[None, :]
    decay = jnp.where(diff >= 0, jnp.exp(log_g[:, None, None] * jnp.maximum(diff, 0.0)), 0.0)
    xi = jnp.exp(log_g[:, None] * (idx[None, :] + 1.0))
    zeta = jnp.exp(log_g[:, None] * (C - 1.0 - idx[None, :])) * (dk ** -0.5)
    g_chunk = jnp.exp(log_g * C)
    bc = lambda t: jnp.broadcast_to(t[:, :, None], (H, C, C))
    return decay, bc(xi), bc(zeta), jnp.broadcast_to(g_chunk[:, None, None], (H, 8, C))


def _gn_fwd(o, g, b):
    mu = jnp.mean(o, axis=-1, keepdims=True)
    xc = o - mu
    rstd = lax.rsqrt(jnp.mean(xc * xc, axis=-1, keepdims=True) + EPS)
    yh = xc * rstd
    return yh, rstd, yh * g + b


def _ret_specs(dk, dv, order):
    H = RET_HEADS
    qk_w, v_w = H * dk, H * dv
    off_q = 3 * ATT_W
    off_k, off_v, off_g = off_q + qk_w, off_q + 2 * qk_w, off_q + 2 * qk_w + v_w
    assert 2 * dk == dv and all(off % dv == 0 for off in (off_q, off_k, off_v, off_g))

    def col(off, j):
        return pl.BlockSpec((BLK, dv), lambda i: (order(i), off // dv + j))

    tab = pl.BlockSpec((H, BLK, BLK), lambda i: (0, 0, 0))
    return ([col(off_q, j) for j in range(H // 2)] + [col(off_k, j) for j in range(H // 2)]
            + [col(off_v, j) for j in range(H)] + [col(off_g, j) for j in range(H)]
            + [tab, tab, tab, pl.BlockSpec((H, 8, BLK), lambda i: (0, 0, 0))])


def _ret_heads(refs, dk):
    H = RET_HEADS
    q_refs, k_refs = refs[0:H // 2], refs[H // 2:H]
    v_refs, gr_refs = refs[H:2 * H], refs[2 * H:3 * H]

    def head(h):
        cols = slice((h % 2) * dk, (h % 2 + 1) * dk)
        return q_refs[h // 2][:, cols], k_refs[h // 2][:, cols], v_refs[h][...], gr_refs[h][...]

    return head, refs[3 * H:3 * H + 4]


def _ret_fwd(proj, gn_g, gn_b, dk, dv):
    S = proj.shape[0]
    N = S // BLK
    H = RET_HEADS
    kscale = dk ** -0.5
    n_in = 3 * H + 4

    def body(*refs):
        head, (dec_ref, xi_ref, zeta_ref, gc_ref) = _ret_heads(refs, dk)
        g_ref, b_ref, opre_ref, or_ref, st_ref, state, s_buf, cross_buf = refs[n_in:n_in + 8]
        n = pl.program_id(0)

        @pl.when(n == 0)
        def _():
            state[...] = jnp.zeros_like(state)

        for h in range(H):
            q, k, v, _ = head(h)
            s_buf[h] = _dot_nt(q, k)
            st = state[h]
            st_c = st.astype(CDT)
            st_ref[h] = st_c
            cross_buf[h] = _dot_nn(q, st_c)
            kz = (k.astype(F32) * zeta_ref[h][:, 0:1]).astype(CDT)
            state[h] = st * gc_ref[h][0:1, 0:1] + _dot_tn(kz, v)
        for h in range(H):
            vs = slice(h * dv, (h + 1) * dv)
            _, _, v, gr = head(h)
            s = s_buf[h] * kscale * dec_ref[h]
            o = _dot_nn(s.astype(CDT), v) + cross_buf[h] * xi_ref[h][:, 0:1]
            opre_ref[:, vs] = o
            _, _, y = _gn_fwd(o, g_ref[:, vs], b_ref[:, vs])
            gr = gr.astype(F32)
            or_ref[:, vs] = (y * (gr * _sigmoid(gr))).astype(or_ref.dtype)

    v_w = H * dv
    row = pl.BlockSpec((1, v_w), lambda i: (0, 0))
    tile = pl.BlockSpec((BLK, v_w), lambda i: (i, 0))
    return pl.pallas_call(
        body, name="ret_fwd", grid=(N,),
        in_specs=_ret_specs(dk, dv, lambda i: i) + [row, row],
        out_specs=[tile, tile, pl.BlockSpec((None, H, dk, dv), lambda i: (i, 0, 0, 0))],
        out_shape=[_sds((S, v_w), F32), _sds((S, v_w), CDT), _sds((N, H, dk, dv), CDT)],
        scratch_shapes=[pltpu.VMEM((H, dk, dv), F32), pltpu.VMEM((H, BLK, BLK), F32), pltpu.VMEM((H, BLK, dv), F32)],
        compiler_params=_params(1),
    )(*[proj] * (3 * H), *_ret_tables(dk), gn_g, gn_b)


def _ret_bwd(proj, gn_g, gn_b, o_pre, states, d_or, dga, dgb, dk, dv):
    S, in_w = proj.shape
    N = S // BLK
    H = RET_HEADS
    qk_w, v_w = H * dk, H * dv
    kscale = dk ** -0.5
    n_in = 3 * H + 4
    out_w = 2 * qk_w + 2 * v_w
    gate_w = dga.shape[1]
    col0 = 3 * ATT_W
    assert col0 + out_w + 2 * gate_w == in_w
    rev = lambda i: N - 1 - i

    def body(*refs):
        head, (dec_ref, xi_ref, zeta_ref, gc_ref) = _ret_heads(refs, dk)
        (g_ref, b_ref, opre_ref, st_ref, dor_ref, dga_ref, dgb_ref, dproj_ref, dg_ref, db_ref, dstate, stage,
         sem, do_buf, dox_buf, a_buf, g_buf, dq_buf, dk_buf, dv_buf) = refs[n_in:n_in + 20]
        i = pl.program_id(0)
        slot = i % 2
        out_ref = stage.at[slot]

        def out_copy(s, step):
            rows = pl.ds(pl.multiple_of(rev(step) * BLK, BLK), BLK)
            return pltpu.make_async_copy(stage.at[s], dproj_ref.at[rows, pl.ds(col0, in_w - col0)], sem.at[s])

        @pl.when(i >= 2)
        def _():
            out_copy(slot, i - 2).wait()

        @pl.when(i == 0)
        def _():
            dstate[...] = jnp.zeros_like(dstate)
            dg_ref[...] = jnp.zeros_like(dg_ref)
            db_ref[...] = jnp.zeros_like(db_ref)

        out_ref[:, out_w:out_w + gate_w] = dga_ref[...]
        out_ref[:, out_w + gate_w:out_w + 2 * gate_w] = dgb_ref[...]
        for h in range(H):
            vs = slice(h * dv, (h + 1) * dv)
            _, _, _, gr = head(h)
            gr = gr.astype(F32)
            sg = _sigmoid(gr)
            gain = g_ref[:, vs]
            yh, rstd, y = _gn_fwd(opre_ref[:, vs], gain, b_ref[:, vs])
            d_or_v = dor_ref[:, vs]
            dy = d_or_v * (gr * sg)
            out_ref[:, 2 * qk_w + v_w + h * dv:2 * qk_w + v_w + (h + 1) * dv] = (
                d_or_v * y * (sg * (1.0 + gr * (1.0 - sg)))).astype(out_ref.dtype)
            dg_ref[:, vs] += jnp.sum(dy * yh, axis=0, keepdims=True)
            db_ref[:, vs] += jnp.sum(dy, axis=0, keepdims=True)
            dyh = dy * gain
            do = rstd * (dyh - jnp.mean(dyh, axis=-1, keepdims=True)
                         - yh * jnp.mean(dyh * yh, axis=-1, keepdims=True))
            do_buf[h] = do.astype(CDT)
            dox_buf[h] = (do * xi_ref[h][:, 0:1]).astype(CDT)
        for h in range(H):
            q, k, v, _ = head(h)
            dox = dox_buf[h]
            a_buf[h] = _dot_nt(q, k)
            g_buf[h] = _dot_nt(do_buf[h], v)
            dsn = dstate[h]
            dsn_c = dsn.astype(CDT)
            kz = (k.astype(F32) * zeta_ref[h][:, 0:1]).astype(CDT)
            dq_buf[h] = _dot_nt(dox, st_ref[h])
            dk_buf[h] = _dot_nt(v, dsn_c)
            dv_buf[h] = _dot_nn(kz, dsn_c)
            dstate[h] = dsn * gc_ref[h][0:1, 0:1] + _dot_tn(q, dox)
        for h in range(H):
            q, k, _, _ = head(h)
            decay = dec_ref[h]
            a_c = (a_buf[h] * kscale * decay).astype(CDT)
            g_c = (g_buf[h] * decay).astype(CDT)
            dq = _dot_nn(g_c, k) * kscale + dq_buf[h]
            dkk = _dot_tn(g_c, q) * kscale + dk_buf[h] * zeta_ref[h][:, 0:1]
            dvv = _dot_tn(a_c, do_buf[h]) + dv_buf[h]
            out_ref[:, h * dk:(h + 1) * dk] = dq.astype(out_ref.dtype)
            out_ref[:, qk_w + h * dk:qk_w + (h + 1) * dk] = dkk.astype(out_ref.dtype)
            out_ref[:, 2 * qk_w + h * dv:2 * qk_w + (h + 1) * dv] = dvv.astype(out_ref.dtype)

        cp = out_copy(slot, i)
        cp.start()

        @pl.when(i == N - 1)
        def _():
            cp.wait()
            if N >= 2:
                out_copy(1 - slot, i - 1).wait()

    row = pl.BlockSpec((1, v_w), lambda i: (0, 0))
    tile = pl.BlockSpec((BLK, v_w), lambda i: (rev(i), 0))
    gate = pl.BlockSpec((BLK, gate_w), lambda i: (rev(i), 0))
    return pl.pallas_call(
        body, name="ret_bwd", grid=(N,),
        in_specs=_ret_specs(dk, dv, rev) + [row, row, tile,
                 pl.BlockSpec((None, H, dk, dv), lambda i: (rev(i), 0, 0, 0)), tile, gate, gate],
        out_specs=[pl.BlockSpec(memory_space=pl.ANY), row, row],
        out_shape=[_sds((S, in_w), CDT), _sds((1, v_w), F32), _sds((1, v_w), F32)],
        scratch_shapes=[pltpu.VMEM((H, dk, dv), F32), pltpu.VMEM((2, BLK, in_w - col0), CDT),
                        pltpu.SemaphoreType.DMA((2,)),
                        pltpu.VMEM((H, BLK, dv), CDT), pltpu.VMEM((H, BLK, dv), CDT),
                        pltpu.VMEM((H, BLK, BLK), F32), pltpu.VMEM((H, BLK, BLK), F32),
                        pltpu.VMEM((H, BLK, dk), F32), pltpu.VMEM((H, BLK, dk), F32), pltpu.VMEM((H, BLK, dv), F32)],
        compiler_params=_params(1),
    )(*[proj] * (3 * H), *_ret_tables(dk), gn_g, gn_b, o_pre, states, d_or, dga, dgb)


def _merge_fwd(o_a, o_r, wa, wb, proj, d_model, tm=1024, tn=512):
    S, in_w = proj.shape
    off_a, off_b = in_w - 2 * d_model, in_w - d_model
    assert off_a % tn == 0 and off_b % tn == 0

    def body(oa_ref, or_ref, wa_ref, wb_ref, ga_ref, gb_ref, y_ref, pa_ref, pb_ref):
        pa = _dot_nn(oa_ref[...], wa_ref[...])
        pb = _dot_nn(or_ref[...], wb_ref[...])
        y = _sigmoid(ga_ref[...].astype(F32)) * pa + _sigmoid(gb_ref[...].astype(F32)) * pb
        y_ref[...] = y.astype(y_ref.dtype)
        pa_ref[...] = pa.astype(pa_ref.dtype)
        pb_ref[...] = pb.astype(pb_ref.dtype)

    ka, kb = o_a.shape[1], o_r.shape[1]
    out = pl.BlockSpec((tm, tn), lambda i, j: (i, j))
    return pl.pallas_call(
        body, name="merge_fwd", grid=(S // tm, d_model // tn),
        in_specs=[pl.BlockSpec((tm, ka), lambda i, j: (i, 0)), pl.BlockSpec((tm, kb), lambda i, j: (i, 0)),
                  pl.BlockSpec((ka, tn), lambda i, j: (0, j)), pl.BlockSpec((kb, tn), lambda i, j: (0, j)),
                  pl.BlockSpec((tm, tn), lambda i, j: (i, off_a // tn + j)),
                  pl.BlockSpec((tm, tn), lambda i, j: (i, off_b // tn + j))],
        out_specs=[out, out, out], out_shape=[_sds((S, d_model), CDT)] * 3,
        compiler_params=_params(2))(o_a, o_r, wa, wb, proj, proj)


def _d_x(dproj, w_in, x, g, dx1, dep, tm=512, row_groups=2):
    S, D = x.shape
    n_sh, _, ns = w_in.shape
    nt = S // tm // row_groups
    deps = [d for d in (dep,) if d is not None]

    def body(a_ref, b_ref, x_ref, g_ref, r_ref, *rest):
        dx_ref, dg_ref, acc = rest[len(deps):]
        h, k, i = pl.program_id(0), pl.program_id(1), pl.program_id(2)

        @pl.when(k == 0)
        def _():
            acc[i] = jnp.zeros((tm, D), F32)

        acc[i] += _dot_nt(a_ref[...], b_ref[...])

        @pl.when(k == n_sh - 1)
        def _():
            _ep_rms_bwd(acc[i], (x_ref, g_ref, r_ref), (dx_ref, dg_ref), h * nt + i)

    def last_only(h, k, i):
        return (h * nt + jnp.where(k == n_sh - 1, i, 0), 0)

    return pl.pallas_call(
        body, name="d_x", grid=(row_groups, n_sh, nt),
        in_specs=[pl.BlockSpec((tm, ns), lambda h, k, i: (h * nt + i, k)),
                  pl.BlockSpec((None, D, ns), lambda h, k, i: (k, 0, 0)),
                  pl.BlockSpec((tm, D), last_only), pl.BlockSpec((1, D), lambda h, k, i: (0, 0)),
                  pl.BlockSpec((tm, D), last_only)] + [pl.BlockSpec(memory_space=pl.ANY)] * len(deps),
        out_specs=[pl.BlockSpec((tm, D), last_only), pl.BlockSpec((1, D), lambda h, k, i: (0, 0))],
        out_shape=[_sds((S, D), F32), _sds((1, D), F32)],
        scratch_shapes=[pltpu.VMEM((nt, tm, D), F32)],
        compiler_params=_params(3))(dproj, w_in, x, g, dx1, *deps)


def _local_step(x, target, w_in_mine, chip, others, near_w_in, far_w_in, small, late_weights, on_grads, deps0=()):
    S, D = x.shape
    ns_in = w_in_mine.shape[1]
    in_w = N_CHIPS * ns_in
    d_ff = 4 * D
    ret_v_w = 2 * D
    dv = ret_v_w // RET_HEADS
    dk = (in_w - 3 * ATT_W - 2 * ret_v_w - 2 * D) // (2 * RET_HEADS)
    gqk = jnp.concatenate([small["q_norm_g"].reshape(1, ATT_W), small["k_norm_g"].reshape(1, ATT_W)], axis=1)
    g1, g2 = small["norm1_g"], small["norm2_g"]
    gn_g, gn_b = small["ret_gn_g"], small["ret_gn_b"]

    proj_sds = _sds((S, in_w), CDT)
    xn, proj = _in_proj_mine(x, g1, w_in_mine, chip, proj_sds, deps0)
    for stage, (get_w_in, chips) in enumerate(((near_w_in, others[:2]), (far_w_in, others[2:]))):
        w_in, started = get_w_in(proj)
        (proj,) = _matmul(
            "in_proj_far%d" % stage, "nn", xn, w_in, tm=1024, tn=ns_in, tk=D, prefetch=[chips],
            n_cols=chips.shape[0] * ns_in, b_spec=pl.BlockSpec((None, D, ns_in), lambda i, j, k, o: (o[j], 0, 0)),
            outs=[(proj_sds, pl.BlockSpec((1024, ns_in), lambda i, j, k, o: (i, o[j])))], epilogue=_ep_store,
            deps=[proj, started], alias_dep_to_out=(0, 0), j_outer=True)
    qkv = _qknorm_fwd(proj, gqk)
    att = [_att_fwd(g, S, qkv[g]) for g in range(3)]
    os_, ls_ = [a[0] for a in att], [a[1] for a in att]
    o_a = _mix_fwd(S, os_, ls_)
    o_pre, o_r, states = _ret_fwd(proj, gn_g, gn_b, dk, dv)
    w = late_weights(o_r)
    y, pa, pb = _merge_fwd(o_a, o_r, w["w_proj_a"], w["w_proj_b"], proj, D)
    x1, xn2 = _matmul("out_proj", "nn", y, w["w_out"], tm=1024, tn=D, tk=D,
                      extras=[(x, _mn(1024, D)), (g2, _row(D))],
                      outs=[(_sds((S, D), F32), _mn(1024, D)), (_sds((S, D), CDT), _mn(1024, D))],
                      epilogue=_ep_resid_norm)
    hid, act = _matmul("mlp_up", "nn", xn2, w["w_up"], tm=1024, tn=2048, tk=D, j_outer=True,
                       outs=[(_sds((S, d_ff), CDT), _mn(1024, 2048))] * 2, epilogue=_ep_up)
    dx2, dx2c, loss_row = _matmul(
        "mlp_down_loss", "nn", act, w["w_down"], tm=512, tn=D, tk=d_ff,
        extras=[(x1, _mn(512, D)), (target, _mn(512, D))],
        outs=[(_sds((S, D), F32), _mn(512, D)), (_sds((S, D), CDT), _mn(512, D)), (_sds((1, D), F32), _row(D))],
        epilogue=functools.partial(_ep_down_loss, inv_d=1.0 / D))
    loss = 0.5 * jnp.sum(loss_row) / D

    (dh,) = _matmul("d_hidden", "nt", dx2c, w["w_down"], tm=1024, tn=2048, tk=D, j_outer=True,
                    extras=[(hid, _mn(1024, 2048))], outs=[(_sds((S, d_ff), CDT), _mn(1024, 2048))],
                    epilogue=_ep_dh)
    (gw_down,) = _matmul("dw_down", "tn", act, dx2c, tm=1024, tn=D, tk=2048,
                         outs=[(_sds((d_ff, D), F32), _mn(1024, D))], epilogue=_ep_store)
    (gw_up,) = _matmul("dw_up", "tn", xn2, dh, tm=D, tn=1024, tk=2048,
                       outs=[(_sds((D, d_ff), F32), _mn(D, 1024))], epilogue=_ep_store)
    tok = on_grads({"w_down": gw_down, "w_up": gw_up})
    dx1, dx1c, dg2 = _matmul(
        "d_x1", "nt", dh, w["w_up"], tm=512, tn=D, tk=d_ff,
        extras=[(x1, _mn(512, D)), (g2, _row(D)), (dx2, _mn(512, D))],
        outs=[(_sds((S, D), F32), _mn(512, D)), (_sds((S, D), CDT), _mn(512, D)), (_sds((1, D), F32), _row(D))],
        epilogue=_ep_rms_bwd, deps=[tok])

    gt = 512
    assert (in_w - 2 * D) % gt == 0
    off_a, off_b = (in_w - 2 * D) // gt, (in_w - D) // gt
    dpa, dpb, dga, dgb = _matmul(
        "d_gates", "nt", dx1c, w["w_out"], tm=1024, tn=gt, tk=D,
        extras=[(proj, _mn(1024, gt, off_a)), (proj, _mn(1024, gt, off_b)), (pa, _mn(1024, gt)),
                (pb, _mn(1024, gt))],
        outs=[(_sds((S, D), CDT), _mn(1024, gt))] * 4, epilogue=_ep_gates)
    (gw_out,) = _matmul("dw_out", "tn", y, dx1c, tm=D, tn=D, tk=1024,
                        outs=[(_sds((D, D), F32), _mn(D, D))], epilogue=_ep_store)
    (gw_pa,) = _matmul("dw_proj_a", "tn", o_a, dpa, tm=GW, tn=D, tk=1024,
                       outs=[(_sds((GW, D), F32), _mn(GW, D))], epilogue=_ep_store)
    (gw_pb,) = _matmul("dw_proj_b", "tn", o_r, dpb, tm=1024, tn=D, tk=2048,
                       outs=[(_sds((ret_v_w, D), F32), _mn(1024, D))], epilogue=_ep_store)
    (do_a,) = _matmul("d_o_a", "nt", dpa, w["w_proj_a"], tm=1024, tn=GW, tk=D,
                      outs=[(_sds((S, GW), F32), _mn(1024, GW))], epilogue=_ep_store)
    tok = on_grads({"w_out": gw_out, "w_proj_a": gw_pa, "w_proj_b": gw_pb})
    (d_or,) = _matmul("d_o_r", "nt", dpb, w["w_proj_b"], tm=1024, tn=ret_v_w, tk=D,
                      outs=[(_sds((S, ret_v_w), F32), _mn(1024, ret_v_w))], epilogue=_ep_store, deps=[tok])

    dproj, dgn_g, dgn_b = _ret_bwd(proj, gn_g, gn_b, o_pre, states, d_or, dga, dgb, dk, dv)
    do_gs, c_gs = _mix_bwd(S, os_, ls_, do_a)
    datt_parts = [_att_bwd(g, S, qkv[g], ls_[g], do_gs[g], c_gs[g]) for g in range(3)]
    dproj, dgqk = _qknorm_bwd(proj, gqk, [p[0] for p in datt_parts], [p[1] for p in datt_parts],
                              [p[2] for p in datt_parts], dproj)

    (gw_in,) = _matmul(
        "dw_in", "tn", xn, dproj, tm=512, tn=ns_in, tk=1024,
        outs=[(_sds((N_CHIPS, D, ns_in), F32), pl.BlockSpec((None, 512, ns_in), lambda i, j, k: (j, i, 0)))],
        epilogue=_ep_store)
    tok = on_grads({"w_in": gw_in})
    grad_x, dg1 = _d_x(dproj, w_in, x, g1, dx1, tok)

    smallg = {"norm1_g": dg1, "q_norm_g": dgqk[:, :ATT_W], "k_norm_g": dgqk[:, ATT_W:],
              "ret_gn_g": dgn_g, "ret_gn_b": dgn_b, "norm2_g": dg2}
    return loss, grad_x, smallg


N_CHIPS = 4
N_DEV = 8


def _place():
    x, y, c = lax.axis_index("x"), lax.axis_index("y"), lax.axis_index("c")
    return x, y, c


def _other_chips(x, y):
    out = []
    for fx, fy in ((1, 0), (0, 1), (1, 1)):
        px = 1 - x if fx else x
        py = 1 - y if fy else y
        out.append(((px, py), 2 * px + py))
    return out


SEM_SPEC = pl.BlockSpec(memory_space=pltpu.SEMAPHORE)
ANY_SPEC = pl.BlockSpec(memory_space=pl.ANY)
EFFECT = pltpu.SideEffectType.DATAFLOW_SIDE_EFFECTING


def _ici_copies(kind, srcs, lands, send, recv, which=(0, 1, 2)):
    x, y, c = _place()
    me = 2 * x + y
    out = []
    for w, (s, l) in enumerate(zip(srcs, lands)):
        for j, ((px, py), pidx) in enumerate(_other_chips(x, y)):
            if j not in which:
                continue
            if kind == "gather":
                half = s.shape[0] // 2
                rows = pl.ds(c * half, half)
                src, dst_there, dst_here = s.at[rows, :], l.at[me, rows, :], l.at[pidx, rows, :]
            else:
                src, dst_there, dst_here = s.at[pidx], l.at[me], l.at[pidx]
            out.append((src, dst_there, dst_here, send.at[3 * w + j], recv.at[3 * w + j], (px, py, c)))
    return out


def _exchange_start(name, kind, srcs, land_shapes, which=(0, 1, 2), lands=None):
    n = len(srcs)
    if lands is None:
        lands = [lax.empty(shape, dtype) for shape, dtype in land_shapes]

    def body(*refs):
        src_refs, land_refs = refs[:n], refs[n:2 * n]
        send, recv = refs[2 * n], refs[2 * n + 1]
        token = refs[-1]
        for src, dst, _, ss, rs, dev in _ici_copies(kind, src_refs, land_refs, send, recv, which):
            pltpu.make_async_remote_copy(src_ref=src, dst_ref=dst, send_sem=ss, recv_sem=rs, device_id=dev,
                                         device_id_type=MESH).start()
        token[...] = jnp.zeros_like(token)

    thru = [pltpu.HBM(s.shape, s.dtype) for s in srcs] + [pltpu.HBM(shape, dtype) for shape, dtype in land_shapes]
    res = pl.pallas_call(
        body, name=name,
        out_shape=(pltpu.SemaphoreType.DMA((3 * n,)), pltpu.SemaphoreType.DMA((3 * n,)), *thru, _sds((8, LANES), F32)),
        in_specs=[HBM_SPEC] * (2 * n), out_specs=(SEM_SPEC, SEM_SPEC, *[HBM_SPEC] * (2 * n), VMEM_SPEC),
        input_output_aliases={i: 2 + i for i in range(2 * n)},
        compiler_params=pltpu.CompilerParams(has_side_effects=EFFECT),
    )(*[pltpu.with_memory_space_constraint(s, pltpu.HBM) for s in srcs],
      *[pltpu.with_memory_space_constraint(l, pltpu.HBM) for l in lands])
    return res[0], res[1], list(res[2:2 + n]), list(res[2 + n:2 + 2 * n]), res[-1]


def _exchange_wait(name, kind, send, recv, srcs, lands, after, which=(0, 1, 2)):
    n = len(srcs)

    def body(*refs):
        src_refs, land_refs = refs[:n], refs[n:2 * n]
        send_ref, recv_ref = refs[2 * n], refs[2 * n + 1]
        for src, _, dst, ss, rs, dev in _ici_copies(kind, src_refs, land_refs, send_ref, recv_ref, which):
            cp = pltpu.make_async_remote_copy(src_ref=src, dst_ref=dst, send_sem=ss, recv_sem=rs, device_id=dev,
                                              device_id_type=MESH)
            cp.wait_send()
            cp.wait_recv()

    thru = [pltpu.HBM(t.shape, t.dtype) for t in list(srcs) + list(lands)]
    res = pl.pallas_call(
        body, name=name, out_shape=thru,
        in_specs=[HBM_SPEC] * (2 * n) + [SEM_SPEC, SEM_SPEC, ANY_SPEC], out_specs=[HBM_SPEC] * (2 * n),
        input_output_aliases={i: i for i in range(2 * n)},
        compiler_params=pltpu.CompilerParams(has_side_effects=EFFECT),
    )(*srcs, *lands, send, recv, after)
    return list(res[:n]), list(res[n:])


PAIR_TILE_ELEMS = 1 << 20


def _pair_fill(name, gathered, mine, core, others, chip, write_mine=True):
    k, r, C = gathered.shape
    half = r // 2
    tr = _row_tile(half, C, PAIR_TILE_ELEMS, mult=16)
    nt = half // tr
    n_far = others.shape[0]

    def body(c_ref, o_ref, chip_ref, in_ref, mine_ref, out_ref, slot, send, recv):
        j = pl.program_id(0)
        _sibling_barrier((j == 0) & (pl.program_id(1) == 0))
        b = (j * nt + pl.program_id(1)) % 2
        x, y, c = _place()
        cp = pltpu.make_async_remote_copy(src_ref=in_ref, dst_ref=slot.at[b], send_sem=send.at[b],
                                          recv_sem=recv.at[b], device_id=(x, y, 1 - c), device_id_type=MESH)

        @pl.when(j < n_far)
        def _():
            cp.start()
            cp.wait_recv()
            out_ref[...] = slot[b]
            cp.wait_send()

        @pl.when(j >= n_far)
        def _():
            out_ref[...] = mine_ref[...]

    def far(j):
        return jnp.minimum(j, n_far - 1)

    grid_spec = pltpu.PrefetchScalarGridSpec(
        num_scalar_prefetch=3, grid=(n_far + (2 if write_mine else 0), nt),
        in_specs=[pl.BlockSpec((tr, C), lambda j, i, c, o, m: (
                      (2 * o[far(j)] + c[0]) * nt + jnp.where(j < n_far, i, nt - 1), 0)),
                  pl.BlockSpec((tr, C), lambda j, i, c, o, m: (jnp.where(j < n_far, 0, (j - n_far) * nt + i), 0))],
        out_specs=pl.BlockSpec((tr, C), lambda j, i, c, o, m: (
            jnp.where(j < n_far, 2 * o[far(j)] + 1 - c[0], 2 * m[0] + j - n_far) * nt + i, 0)),
        scratch_shapes=[pltpu.VMEM((2, tr, C), gathered.dtype), pltpu.SemaphoreType.DMA((2,)),
                        pltpu.SemaphoreType.DMA((2,))])
    out = pl.pallas_call(body, name=name, grid_spec=grid_spec, out_shape=_sds((k * r, C), gathered.dtype),
                         input_output_aliases={3: 0}, compiler_params=_params(2, PAIR_FILL_ID))(
                             core, others, chip, gathered.reshape(k * r, C), mine)
    return out.reshape(k, r, C)


def _pair_reduce(name, g, core):
    k, R, C = g.shape
    half = R // 2
    tr = _row_tile(half, C, PAIR_TILE_ELEMS, mult=16)
    nt = half // tr

    def body(c_ref, mine_ref, give_ref, out_ref, wire_ref, stage, slot, send, recv):
        _sibling_barrier((pl.program_id(0) == 0) & (pl.program_id(1) == 0))
        b = (pl.program_id(0) * nt + pl.program_id(1)) % 2
        x, y, c = _place()
        stage[b] = give_ref[...].astype(stage.dtype)
        cp = pltpu.make_async_remote_copy(src_ref=stage.at[b], dst_ref=slot.at[b], send_sem=send.at[b],
                                          recv_sem=recv.at[b], device_id=(x, y, 1 - c), device_id_type=MESH)
        cp.start()
        cp.wait_recv()
        tot = mine_ref[...] + slot[b].astype(F32)
        out_ref[...] = tot
        wire_ref[...] = tot.astype(wire_ref.dtype)
        cp.wait_send()

    blk = (tr, C)
    out_spec = pl.BlockSpec(blk, lambda s, i, c: (s * nt + i, 0))
    grid_spec = pltpu.PrefetchScalarGridSpec(
        num_scalar_prefetch=1, grid=(k, nt),
        in_specs=[pl.BlockSpec(blk, lambda s, i, c: ((2 * s + c[0]) * nt + i, 0)),
                  pl.BlockSpec(blk, lambda s, i, c: ((2 * s + 1 - c[0]) * nt + i, 0))],
        out_specs=[out_spec, out_spec],
        scratch_shapes=[pltpu.VMEM((2, tr, C), CDT), pltpu.VMEM((2, tr, C), CDT), pltpu.SemaphoreType.DMA((2,)),
                        pltpu.SemaphoreType.DMA((2,))])
    g2 = g.reshape(k * R, C)
    out, wire = pl.pallas_call(body, name=name, grid_spec=grid_spec,
                               out_shape=[_sds((k * half, C), F32), _sds((k * half, C), CDT)],
                               compiler_params=_params(2, PAIR_REDUCE_ID))(core, g2, g2)
    return out, wire.reshape(k, half, C)


def _all_reduce_small(v):
    r, cdim = v.shape

    def body(v_ref, o_ref, buf, send, recv):
        x, y, c = _place()
        me = 4 * x + 2 * y + c
        buf[me] = v_ref[...]
        sends = []
        for m in range(1, N_DEV):
            px = 1 - x if m & 4 else x
            py = 1 - y if m & 2 else y
            pc = 1 - c if m & 1 else c
            cp = pltpu.make_async_remote_copy(src_ref=v_ref, dst_ref=buf.at[me], send_sem=send.at[m - 1],
                                              recv_sem=recv.at[m - 1], device_id=(px, py, pc), device_id_type=MESH)
            cp.start()
            sends.append((cp, 4 * px + 2 * py + pc))
        for m, (cp, pidx) in enumerate(sends):
            pltpu.make_async_remote_copy(src_ref=v_ref, dst_ref=buf.at[pidx], send_sem=send.at[m], recv_sem=recv.at[m],
                                         device_id=(x, y, c), device_id_type=MESH).wait_recv()
        for cp, _ in sends:
            cp.wait_send()
        tot = buf[0]
        for k in range(1, N_DEV):
            tot = tot + buf[k]
        o_ref[...] = tot

    return pl.pallas_call(
        body, name="all_reduce_small", in_specs=[VMEM_SPEC], out_specs=VMEM_SPEC,
        out_shape=_sds((r, cdim), F32),
        scratch_shapes=[pltpu.VMEM((N_DEV, r, cdim), F32), pltpu.SemaphoreType.DMA((N_DEV - 1,)),
                        pltpu.SemaphoreType.DMA((N_DEV - 1,))],
    )(v)


def _row_tile(rows, cols, budget_elems=1 << 18, mult=8):
    if rows % mult:
        return rows
    t = max(mult, (budget_elems // cols) // mult * mult)
    while rows % t:
        t -= mult
    return t


def _adamw_update(w, g, m, v):
    nm = ADAM_B1 * m + (1.0 - ADAM_B1) * g
    nv = ADAM_B2 * v + (1.0 - ADAM_B2) * (g * g)
    m_hat = nm / (1.0 - ADAM_B1 ** ADAM_STEP)
    v_hat = nv / (1.0 - ADAM_B2 ** ADAM_STEP)
    return -ADAM_LR * (m_hat / (jnp.sqrt(v_hat) + ADAM_EPS) + ADAM_WD * w), nm, nv


def _adamw(name, w, g, m, v):
    R, C = w.shape
    tr = _row_tile(R, C, 1 << 19)

    def body(w_ref, g_ref, m_ref, v_ref, d_ref, nm_ref, nv_ref):
        d_ref[...], nm_ref[...], nv_ref[...] = _adamw_update(w_ref[...], g_ref[...], m_ref[...], v_ref[...])

    spec = pl.BlockSpec((tr, C), lambda i: (i, 0))
    return pl.pallas_call(body, name=name, grid=(R // tr,), in_specs=[spec] * 4, out_specs=[spec] * 3,
                          out_shape=[_sds((R, C), F32)] * 3, compiler_params=_params(1))(w, g, m, v)


def _sum_share(name, own, by_chip, chip, others, core):
    k, half, C = by_chip.shape
    tr = _row_tile(half, C, PAIR_TILE_ELEMS // 2, mult=16)
    nt = half // tr

    def body(chip_ref, oth_ref, c_ref, own_ref, a_ref, b_ref, cc_ref, g_out, mine, slot, send, recv):
        p = pl.program_id(1)
        _sibling_barrier((pl.program_id(0) == 0) & (p == 0))
        b = pl.program_id(0) % 2
        x, y, c = _place()
        cp = pltpu.make_async_remote_copy(src_ref=mine.at[b], dst_ref=slot.at[b], send_sem=send.at[b],
                                          recv_sem=recv.at[b], device_id=(x, y, 1 - c), device_id_type=MESH)

        @pl.when(p == 0)
        def _():
            tot = ((own_ref[...] + a_ref[...].astype(F32)) + b_ref[...].astype(F32)) + cc_ref[...].astype(F32)
            mine[b] = tot
            cp.start()
            g_out[...] = tot

        @pl.when(p == 1)
        def _():
            cp.wait_recv()
            g_out[...] = slot[b]
            cp.wait_send()

    def piece(j):
        return pl.BlockSpec((tr, C), lambda i, p, chip, oth, c: (oth[j] * nt + i, 0))

    grid_spec = pltpu.PrefetchScalarGridSpec(
        num_scalar_prefetch=3, grid=(nt, 2),
        in_specs=[pl.BlockSpec((tr, C), lambda i, p, chip, oth, c: (chip[0] * nt + i, 0)),
                  piece(0), piece(1), piece(2)],
        out_specs=pl.BlockSpec((tr, C), lambda i, p, chip, oth, c: (
            jnp.where(p == 0, c[0], 1 - c[0]) * nt + i, 0)),
        scratch_shapes=[pltpu.VMEM((2, tr, C), F32), pltpu.VMEM((2, tr, C), F32), pltpu.SemaphoreType.DMA((2,)),
                        pltpu.SemaphoreType.DMA((2,))])
    by2 = by_chip.reshape(k * half, C)
    return pl.pallas_call(body, name=name, grid_spec=grid_spec, out_shape=_sds((2 * half, C), F32),
                          compiler_params=_params(2, SUM_SHARE_ID))(chip, others, core, own, by2, by2, by2)


BIG = ("w_in", "w_proj_a", "w_proj_b", "w_out", "w_up", "w_down")
COL_SHARDED = ("w_in", "w_proj_a", "w_up")
SMALL = ("norm1_g", "q_norm_g", "k_norm_g", "ret_gn_g", "ret_gn_b", "norm2_g")
ALL_W = ("norm1_g", "w_in", "q_norm_g", "k_norm_g", "ret_gn_g", "ret_gn_b", "w_proj_a", "w_proj_b", "w_out",
         "norm2_g", "w_up", "w_down")
LANES = 128


def _to_full(name, gathered):
    k, r, c = gathered.shape
    if name in COL_SHARDED:
        return gathered.transpose(1, 0, 2).reshape(r, k * c)
    return gathered.reshape(k * r, c)


def _to_shard_major(name, full):
    if name in COL_SHARDED:
        r, c4 = full.shape
        return full.reshape(r, N_CHIPS, c4 // N_CHIPS).transpose(1, 0, 2)
    r4, c = full.shape
    return full.reshape(N_CHIPS, r4 // N_CHIPS, c)


def kernel(x, norm1_g, w_in, q_norm_g, k_norm_g, ret_gn_g, ret_gn_b, w_proj_a, w_proj_b, w_out, norm2_g, w_up, w_down, loss_target, m_norm1_g, m_w_in, m_q_norm_g, m_k_norm_g, m_ret_gn_g, m_ret_gn_b, m_w_proj_a, m_w_proj_b, m_w_out, m_norm2_g, m_w_up, m_w_down, v_norm1_g, v_w_in, v_q_norm_g, v_k_norm_g, v_ret_gn_g, v_ret_gn_b, v_w_proj_a, v_w_proj_b, v_w_out, v_norm2_g, v_w_up, v_w_down):
    weights = dict(norm1_g=norm1_g, w_in=w_in, q_norm_g=q_norm_g, k_norm_g=k_norm_g, ret_gn_g=ret_gn_g,
                   ret_gn_b=ret_gn_b, w_proj_a=w_proj_a, w_proj_b=w_proj_b, w_out=w_out, norm2_g=norm2_g,
                   w_up=w_up, w_down=w_down)
    moments_m = dict(norm1_g=m_norm1_g, w_in=m_w_in, q_norm_g=m_q_norm_g, k_norm_g=m_k_norm_g, ret_gn_g=m_ret_gn_g,
                     ret_gn_b=m_ret_gn_b, w_proj_a=m_w_proj_a, w_proj_b=m_w_proj_b, w_out=m_w_out,
                     norm2_g=m_norm2_g, w_up=m_w_up, w_down=m_w_down)
    moments_v = dict(norm1_g=v_norm1_g, w_in=v_w_in, q_norm_g=v_q_norm_g, k_norm_g=v_k_norm_g, ret_gn_g=v_ret_gn_g,
                     ret_gn_b=v_ret_gn_b, w_proj_a=v_w_proj_a, w_proj_b=v_w_proj_b, w_out=v_w_out,
                     norm2_g=v_norm2_g, w_up=v_w_up, w_down=v_w_down)

    mx, my = lax.axis_index("x"), lax.axis_index("y")
    core = lax.axis_index("c").astype(jnp.int32).reshape(1)
    chip = (2 * mx + my).astype(jnp.int32).reshape(1)
    others = jnp.stack([2 * (1 - mx) + my, 2 * mx + 1 - my, 2 * (1 - mx) + 1 - my]).astype(jnp.int32)
    shards = {n: weights[n][0].astype(CDT) for n in BIG}
    def start_gather(name, names):
        return _exchange_start(name, "gather", [shards[n] for n in names],
                               [((N_CHIPS,) + shards[n].shape, CDT) for n in names])

    w_in_shape = [((N_CHIPS,) + shards["w_in"].shape, CDT)]
    n_send, n_recv, n_srcs, n_lands, n_token = _exchange_start(
        "gather_w_in_near_start", "gather", [shards["w_in"]], w_in_shape, which=(0, 1))
    late = [n for n in BIG if n != "w_in"]
    flight = {}

    def near_w_in(after):
        srcs, lands = _exchange_wait("gather_w_in_near_wait", "gather", n_send, n_recv, n_srcs, n_lands, after,
                                     which=(0, 1))
        d_send, d_recv, d_srcs, d_lands, _ = _exchange_start(
            "gather_w_in_diag_start", "gather", srcs, w_in_shape, which=(2,), lands=lands)
        flight["late"] = start_gather("gather_late_start", late)
        w_near = _pair_fill("pair_fill_w_in_near", d_lands[0], d_srcs[0], core, others[:2], chip)
        flight["diag"] = (d_send, d_recv, d_srcs, [w_near])
        return w_near, flight["late"][-1]

    def far_w_in(after):
        d_send, d_recv, d_srcs, d_lands = flight["diag"]
        srcs, lands = _exchange_wait("gather_w_in_diag_wait", "gather", d_send, d_recv, d_srcs, d_lands, after,
                                     which=(2,))
        return _pair_fill("pair_fill_w_in_diag", lands[0], srcs[0], core, others[2:], chip, write_mine=False), None

    def late_weights(after):
        l_send, l_recv, l_srcs, l_lands, _ = flight["late"]
        srcs, lands = _exchange_wait("gather_late_wait", "gather", l_send, l_recv, l_srcs, l_lands, after)
        out = {}
        for n, mine, land in zip(late, srcs, lands):
            out[n] = _to_full(n, _pair_fill("pair_fill_%s" % n, land, mine, core, others, chip))
        return out

    pending = []

    def on_grads(group):
        names = list(group)
        red = [_pair_reduce("pair_reduce_%s" % n, g if g.ndim == 3 else _to_shard_major(n, g), core)
               for n, g in group.items()]
        wires = [wire for _, wire in red]
        send, recv, srcs, lands, token = _exchange_start(
            "scatter_start_%s" % names[0], "scatter", wires, [(wire.shape, wire.dtype) for wire in wires])
        pending.append((names, [own for own, _ in red], send, recv, srcs, lands))
        return token

    small = {n: weights[n].reshape(1, -1) for n in SMALL}

    loss, grad_x, small_g = _local_step(x[0], loss_target[0], n_srcs[0], chip, others, near_w_in, far_w_in, small,
                                        late_weights, on_grads, deps0=[n_token])

    out_g, out_d, out_m, out_v = {}, {}, {}, {}
    for names, owns, send, recv, srcs, lands in pending:
        _, got = _exchange_wait("scatter_wait_%s" % names[0], "scatter", send, recv, srcs, lands, grad_x)
        for n, own, by_chip in zip(names, owns, got):
            shape = weights[n].shape
            g2 = _sum_share("sum_share_%s" % n, own, by_chip, chip, others, core)
            d, nm, nv = _adamw("adamw_%s" % n, weights[n][0], g2, moments_m[n][0], moments_v[n][0])
            out_g[n], out_d[n], out_m[n], out_v[n] = (t.reshape(shape) for t in (g2, d, nm, nv))

    packed = jnp.concatenate([small_g[n].reshape(1, -1) for n in SMALL], axis=1).reshape(-1, LANES)
    loss_tile = jnp.zeros((8, LANES), F32).at[0, 0].set(loss)
    red = _all_reduce_small(jnp.concatenate([packed, loss_tile], axis=0))
    loss = red[packed.shape[0], 0]
    red = red[:packed.shape[0]].reshape(1, -1)
    off = 0
    for n in SMALL:
        shape = weights[n].shape
        row = (1, weights[n].size)
        g2 = red[:, off:off + row[1]]
        off += row[1]
        d, nm, nv = _adamw("adamw_%s" % n, weights[n].reshape(row), g2, moments_m[n].reshape(row),
                           moments_v[n].reshape(row))
        out_g[n], out_d[n], out_m[n], out_v[n] = (t.reshape(shape) for t in (g2, d, nm, nv))

    return (loss, grad_x[None], *[out_g[n] for n in ALL_W], *[out_d[n] for n in ALL_W],
            *[out_m[n] for n in ALL_W], *[out_v[n] for n in ALL_W])
```

```python
import functools

import jax
import jax.numpy as jnp
from jax import lax
from jax.experimental import pallas as pl
from jax.experimental.pallas import tpu as pltpu

CDT = jnp.bfloat16
F32 = jnp.float32
EPS = 1e-6

ATT_GROUPS = ((128, 1), (512, 4), (2048, 16))
ATT_HPG = 4
ATT_HEADS = 12
HD = 128
BLK = 128
ATT_W = ATT_HEADS * HD
GW = ATT_HPG * HD
RET_HEADS = 4

ADAM_LR = 0.001
ADAM_B1 = 0.9
ADAM_B2 = 0.999
ADAM_EPS = 1e-08
ADAM_WD = 0.01
ADAM_STEP = 10

VMEM_LIMIT_BYTES = 56 * 1024 * 1024
MESH = pl.DeviceIdType.MESH
HBM_SPEC = pl.BlockSpec(memory_space=pltpu.HBM)
VMEM_SPEC = pl.BlockSpec(memory_space=pltpu.VMEM)


def _params(n_axes, collective_id=None):
    return pltpu.CompilerParams(dimension_semantics=("arbitrary",) * n_axes,
                                vmem_limit_bytes=VMEM_LIMIT_BYTES, collective_id=collective_id)


PAIR_FILL_ID, PAIR_REDUCE_ID, SUM_SHARE_ID = 1, 2, 3


def _sibling_barrier(first_step):
    @pl.when(first_step)
    def _():
        sem = pltpu.get_barrier_semaphore()
        x, y, c = lax.axis_index("x"), lax.axis_index("y"), lax.axis_index("c")
        pl.semaphore_signal(sem, inc=1, device_id=(x, y, 1 - c), device_id_type=pl.DeviceIdType.MESH)
        pl.semaphore_wait(sem, 1)


def _dot_nn(a, b):
    return jnp.dot(a, b, preferred_element_type=F32)


def _dot_nt(a, b):
    return lax.dot_general(a, b, (((1,), (1,)), ((), ())), preferred_element_type=F32)


def _dot_tn(a, b):
    return lax.dot_general(a, b, (((0,), (0,)), ((), ())), preferred_element_type=F32)


def _sigmoid(v):
    return 1.0 / (1.0 + jnp.exp(-v))


def _matmul(name, mode, a, b, *, tm, tn, tk, extras=(), outs, epilogue, deps=(), b_spec=None, n_cols=None,
            prefetch=(), alias_dep_to_out=None, j_outer=False):
    deps = [d for d in deps if d is not None]
    if mode == "tn":
        K, M = a.shape
    else:
        M, K = a.shape
    if b_spec is None:
        (N, K2) = b.shape if mode == "nt" else b.shape[::-1]
        assert K == K2, (name, a.shape, b.shape)
        if mode == "nt":
            b_spec = pl.BlockSpec((tn, tk), lambda i, j, k, *p: (j, k))
        else:
            b_spec = pl.BlockSpec((tk, tn), lambda i, j, k, *p: (k, j))
    else:
        N = n_cols
    assert M % tm == 0 and N % tn == 0 and K % tk == 0, (name, a.shape, b.shape)
    ni, nj, nk = M // tm, N // tn, K // tk
    if mode == "tn":
        a_spec = pl.BlockSpec((tk, tm), lambda i, j, k, *p: (k, i))
    else:
        a_spec = pl.BlockSpec((tm, tk), lambda i, j, k, *p: (i, k))
    dot = {"nn": _dot_nn, "nt": _dot_nt, "tn": _dot_tn}[mode]
    n_ex, n_out, n_dep, n_pre = len(extras), len(outs), len(deps), len(prefetch)
    grid = (ni, nj, nk)
    if j_outer:
        grid = (nj, ni, nk)

        def swapped(spec):
            return pl.BlockSpec(spec.block_shape, lambda j, i, k, *p: spec.index_map(i, j, k, *p))

        a_spec, b_spec = swapped(a_spec), swapped(b_spec)
        extras = [(e, swapped(s)) for e, s in extras]
        outs = [(o, swapped(s)) for o, s in outs]

    def body(*refs):
        refs = refs[n_pre:]
        a_ref, b_ref = refs[0], refs[1]
        ex = refs[2:2 + n_ex]
        out = refs[2 + n_ex + n_dep:2 + n_ex + n_dep + n_out]
        acc = refs[-1] if nk > 1 else None
        i = pl.program_id(1 if j_outer else 0)
        k = pl.program_id(2)
        if nk == 1:
            epilogue(dot(a_ref[...].astype(CDT), b_ref[...].astype(CDT)), ex, out, i)
            return

        @pl.when(k == 0)
        def _():
            acc[...] = jnp.zeros_like(acc)

        acc[...] += dot(a_ref[...].astype(CDT), b_ref[...].astype(CDT))

        @pl.when(k == nk - 1)
        def _():
            epilogue(acc[...], ex, out, i)

    grid_spec = pltpu.PrefetchScalarGridSpec(
        num_scalar_prefetch=n_pre, grid=grid,
        in_specs=[a_spec, b_spec] + [s for _, s in extras] + [pl.BlockSpec(memory_space=pl.ANY)] * n_dep,
        out_specs=[s for _, s in outs],
        scratch_shapes=[pltpu.VMEM((tm, tn), F32)] if nk > 1 else [])
    aliases = {}
    if alias_dep_to_out is not None:
        aliases = {n_pre + 2 + n_ex + alias_dep_to_out[0]: alias_dep_to_out[1]}
    res = pl.pallas_call(
        body, name=name, grid_spec=grid_spec, out_shape=[o for o, _ in outs], input_output_aliases=aliases,
        compiler_params=_params(3),
    )(*prefetch, a, b, *[e for e, _ in extras], *deps)
    return res


def _mn(tm, tn, col_off=0):
    return pl.BlockSpec((tm, tn), lambda i, j, k, *p: (i, j + col_off))


def _row(tn):
    return pl.BlockSpec((1, tn), lambda i, j, k, *p: (0, j))


def _ep_store(acc, ex, out, i):
    out[0][...] = acc.astype(out[0].dtype)


def _ep_resid_norm(acc, ex, out, i):
    x1 = ex[0][...] + acc
    out[0][...] = x1
    rstd = lax.rsqrt(jnp.mean(x1 * x1, axis=-1, keepdims=True) + EPS)
    out[1][...] = (x1 * rstd * ex[1][...]).astype(out[1].dtype)


def _ep_up(acc, ex, out, i):
    out[0][...] = acc.astype(out[0].dtype)
    r = jnp.maximum(acc, 0.0)
    out[1][...] = (r * r).astype(out[1].dtype)


def _ep_down_loss(acc, ex, out, i, inv_d):
    diff = (ex[0][...] + acc) - ex[1][...]
    dx2 = diff * inv_d
    out[0][...] = dx2
    out[1][...] = dx2.astype(out[1].dtype)

    @pl.when(i == 0)
    def _():
        out[2][...] = jnp.zeros_like(out[2])

    out[2][...] += jnp.sum(diff * diff, axis=0, keepdims=True)


def _ep_dh(acc, ex, out, i):
    h = ex[0][...].astype(F32)
    out[0][...] = (acc * (2.0 * jnp.maximum(h, 0.0))).astype(out[0].dtype)


def _ep_rms_bwd(acc, ex, out, i):
    x = ex[0][...]
    g = ex[1][...]
    rstd = lax.rsqrt(jnp.mean(x * x, axis=-1, keepdims=True) + EPS)
    xh = x * rstd
    dxh = acc * g
    dx = ex[2][...] + rstd * (dxh - xh * jnp.mean(dxh * xh, axis=-1, keepdims=True))
    out[0][...] = dx
    for copy in out[1:-1]:
        copy[...] = dx.astype(copy.dtype)
    dg = out[-1]

    @pl.when(i == 0)
    def _():
        dg[...] = jnp.zeros_like(dg)

    dg[...] += jnp.sum(acc * xh, axis=0, keepdims=True)


def _ep_gates(acc, ex, out, i):
    sa = _sigmoid(ex[0][...].astype(F32))
    sb = _sigmoid(ex[1][...].astype(F32))
    dpa = acc * sa
    dpb = acc * sb
    out[0][...] = dpa.astype(out[0].dtype)
    out[1][...] = dpb.astype(out[1].dtype)
    out[2][...] = (dpa * ex[2][...].astype(F32) * (1.0 - sa)).astype(out[2].dtype)
    out[3][...] = (dpb * ex[3][...].astype(F32) * (1.0 - sb)).astype(out[3].dtype)


def _sds(shape, dtype):
    return jax.ShapeDtypeStruct(shape, dtype)


def _in_proj_mine(x, g, w_mine, chip, proj_sds, deps, tm=512):
    S, D = x.shape
    ns = w_mine.shape[1]
    deps = [d for d in deps if d is not None]

    def body(c_ref, x_ref, g_ref, w_ref, *rest):
        xn_ref, proj_ref = rest[len(deps)], rest[len(deps) + 1]
        xv = x_ref[...]
        rstd = lax.rsqrt(jnp.mean(xv * xv, axis=-1, keepdims=True) + EPS)
        xn = (xv * rstd * g_ref[...]).astype(xn_ref.dtype)
        xn_ref[...] = xn
        proj_ref[...] = _dot_nn(xn, w_ref[...]).astype(proj_ref.dtype)

    grid_spec = pltpu.PrefetchScalarGridSpec(
        num_scalar_prefetch=1, grid=(S // tm,),
        in_specs=[pl.BlockSpec((tm, D), lambda i, c: (i, 0)), pl.BlockSpec((1, D), lambda i, c: (0, 0)),
                  pl.BlockSpec((D, ns), lambda i, c: (0, 0))] + [pl.BlockSpec(memory_space=pl.ANY)] * len(deps),
        out_specs=[pl.BlockSpec((tm, D), lambda i, c: (i, 0)), pl.BlockSpec((tm, ns), lambda i, c: (i, c[0]))])
    return pl.pallas_call(body, name="in_proj_mine", grid_spec=grid_spec, out_shape=[_sds((S, D), CDT), proj_sds],
                          compiler_params=_params(1))(chip, x, g, w_mine, *deps)


def _rm_shape(S, d, width):
    return (S, width) if d == 1 else (d, S // d, width)


def _rm_spec(tm, d, width):
    if d == 1:
        return pl.BlockSpec((tm, width), lambda i: (i, 0))
    return pl.BlockSpec((d, tm // d, width), lambda i: (0, i, 0))


def _rm_put(dst_ref, cols, buf_ref, d):
    if d == 1:
        dst_ref[:, cols] = buf_ref[...].astype(dst_ref.dtype)
        return
    m = buf_ref.shape[0] // d
    for r in range(d):
        dst_ref[r, :, cols] = buf_ref[pl.ds(r, m, stride=d), :].astype(dst_ref.dtype)


def _rm_reader(buf_ref, src_ref, d):
    if d == 1:
        return lambda s, rows: src_ref[rows, s * HD:(s + 1) * HD].astype(F32)
    m = buf_ref.shape[1] // d
    for s in range(buf_ref.shape[0]):
        for r in range(d):
            buf_ref.at[s][pl.ds(r, m, stride=d), :] = src_ref[r, :, s * HD:(s + 1) * HD].astype(F32)
    return lambda s, rows: buf_ref.at[s][rows, :]


def _qknorm_fwd(proj, gqk, tm=1024):
    S = proj.shape[0]
    W = 2 * ATT_W
    dil = [d for _, d in ATT_GROUPS]

    def body(p_ref, g_ref, o0, o1, o2, buf):
        outs = (o0, o1, o2)
        for hd in range(3 * ATT_HEADS):
            which, head = hd // ATT_HEADS, hd % ATT_HEADS
            grp, slot = head // ATT_HPG, head % ATT_HPG
            cols = slice(hd * HD, (hd + 1) * HD)

            def chunk(rows, which=which, cols=cols):
                v = p_ref[rows, cols].astype(F32)
                if which < 2:
                    rstd = lax.rsqrt(jnp.mean(v * v, axis=-1, keepdims=True) + EPS)
                    v = v * rstd * g_ref[:, cols]
                buf[rows, :] = v

            chunk(slice(None))
            _rm_put(outs[grp], slice(which * GW + slot * HD, which * GW + (slot + 1) * HD), buf, dil[grp])

    return pl.pallas_call(
        body, name="qknorm_fwd", grid=(S // tm,),
        in_specs=[pl.BlockSpec((tm, 3 * ATT_W), lambda i: (i, 0)), pl.BlockSpec((1, W), lambda i: (0, 0))],
        out_specs=[_rm_spec(tm, d, 3 * GW) for d in dil],
        out_shape=[_sds(_rm_shape(S, d, 3 * GW), CDT) for d in dil],
        scratch_shapes=[pltpu.VMEM((tm, HD), F32)],
        compiler_params=_params(1))(proj, gqk)


def _qknorm_bwd(proj, gqk, dqs, dks, dvs, dproj, tm=512):
    S = proj.shape[0]
    W = 2 * ATT_W
    dil = [d for _, d in ATT_GROUPS]

    def body(p_ref, g_ref, *refs):
        ins = refs[0:9]
        o_ref, dg_ref = refs[10], refs[11]
        bufs = refs[12:21]
        i = pl.program_id(0)

        @pl.when(i == 0)
        def _():
            dg_ref[...] = jnp.zeros_like(dg_ref)

        nat = [_rm_reader(bufs[j], ins[j], dil[j % 3]) for j in range(9)]
        dq_get, dk_get, dv_get = nat[0:3], nat[3:6], nat[6:9]
        for hd in range(2 * ATT_HEADS):
            sl = slice(hd * HD, (hd + 1) * HD)
            head = hd % ATT_HEADS
            grp, slot = head // ATT_HPG, head % ATT_HPG
            get = (dq_get if hd < ATT_HEADS else dk_get)[grp]

            def chunk(rows, sl=sl, slot=slot, get=get):
                dn = get(slot, rows)
                v = p_ref[rows, sl].astype(F32)
                rstd = lax.rsqrt(jnp.mean(v * v, axis=-1, keepdims=True) + EPS)
                vh = v * rstd
                dg_ref[:, sl] += jnp.sum(dn * vh, axis=0, keepdims=True)
                dvh = dn * g_ref[:, sl]
                o_ref[rows, sl] = (rstd * (dvh - vh * jnp.mean(dvh * vh, axis=-1, keepdims=True))).astype(o_ref.dtype)

            chunk(slice(None))
        for head in range(ATT_HEADS):
            grp, slot = head // ATT_HPG, head % ATT_HPG
            o_ref[:, W + head * HD:W + (head + 1) * HD] = dv_get[grp](slot, slice(None)).astype(o_ref.dtype)

    return pl.pallas_call(
        body, name="qknorm_bwd", grid=(S // tm,),
        in_specs=[pl.BlockSpec((tm, W), lambda i: (i, 0)), pl.BlockSpec((1, W), lambda i: (0, 0))]
        + [_rm_spec(tm, d, GW) for d in dil] * 3 + [pl.BlockSpec(memory_space=pl.ANY)],
        out_specs=[pl.BlockSpec((tm, 3 * ATT_W), lambda i: (i, 0)), pl.BlockSpec((1, W), lambda i: (0, 0))],
        out_shape=[_sds(dproj.shape, dproj.dtype), _sds((1, W), F32)],
        scratch_shapes=[pltpu.VMEM((ATT_HPG, tm, HD), F32)] * 9,
        input_output_aliases={11: 0},
        compiler_params=_params(1))(proj, gqk, *dqs, *dks, *dvs, dproj)


def _att_mask(n):
    qi = lax.broadcasted_iota(jnp.int32, (BLK, 2 * BLK), 0)
    kj = lax.broadcasted_iota(jnp.int32, (BLK, 2 * BLK), 1)
    dist = BLK + qi - kj
    valid_all = (dist >= 0) & (dist <= BLK)
    return valid_all & ((kj >= BLK) | (n > 0)), valid_all, dist.astype(F32)


def _att_slopes(grp):
    return [2.0 ** (-8.0 * (grp * ATT_HPG + hh + 1) / ATT_HEADS) for hh in range(ATT_HPG)]


ATT_WIDTH = 8
ATT_BWD_SEGMENTS = 4


def _att_planes(d, width):
    return d if d > 1 else width


def _att_step_planes(d, width):
    return min(_att_planes(d, width), width)


def _att_3d(a, d, width):
    return a.reshape(width, a.shape[0] // width, a.shape[1]) if d == 1 else a


def _att_spec(R, row_fn, col=0):
    return pl.BlockSpec((R, BLK, GW), lambda r, n: (r, row_fn(n), col))


def _att_chains(R):
    return [(rr * ATT_HPG + hh, rr, slice(hh * HD, (hh + 1) * HD), hh) for rr in range(R) for hh in range(ATT_HPG)]


def _att_qkv_specs(R, nb):
    last = nb - 1

    def cur(n):
        return jnp.minimum(n, last)

    def prev(n):
        return jnp.maximum(jnp.minimum(n, last) - 1, 0)

    return [_att_spec(R, cur, 0), _att_spec(R, prev, 1), _att_spec(R, cur, 1), _att_spec(R, prev, 2),
            _att_spec(R, cur, 2), _att_spec(R, lambda n: last, 1), _att_spec(R, lambda n: last, 2)]


def _att_prev(seg, n, prev_ref, last_ref, rr, sl):
    t = prev_ref[rr, :, sl]
    if seg and rr > 0:
        t = jnp.where(n == 0, last_ref[rr - 1, :, sl], t)
    return t


def _att_fwd(grp, S, qkv):
    _, d = ATT_GROUPS[grp]
    seg = d == 1
    width = ATT_WIDTH
    P = _att_planes(d, width)
    L = S // P
    nb = L // BLK
    R = _att_step_planes(d, width)
    assert P % R == 0 and (not seg or P == R)
    slopes = _att_slopes(grp)
    scale = HD ** -0.5
    chains = _att_chains(R)

    def body(q_ref, kp_ref, kc_ref, vp_ref, vc_ref, kl_ref, vl_ref, o_ref, l_ref, s_buf, p_buf, den_buf):
        n = pl.program_id(1)
        valid, valid_all, distf = _att_mask(n)
        for c, rr, sl, hh in chains:
            k = jnp.concatenate([_att_prev(seg, n, kp_ref, kl_ref, rr, sl), kc_ref[rr, :, sl]], axis=0)
            s_buf[c] = _dot_nt(q_ref[rr, :, sl], k)
        for c, rr, sl, hh in chains:
            s = s_buf[c] * scale + (-slopes[hh] * d) * distf
            s = jnp.where(valid_all if seg and rr > 0 else valid, s, -1e30)
            m = jnp.max(s, axis=-1, keepdims=True)
            p = jnp.exp(s - m)
            den = jnp.sum(p, axis=-1, keepdims=True)
            p_buf[c] = p.astype(CDT)
            den_buf[c] = jnp.broadcast_to(den, (BLK, HD))
            l_ref[rr, :, sl] = jnp.broadcast_to(m + jnp.log(den), (BLK, HD))
        for c, rr, sl, hh in chains:
            v = jnp.concatenate([_att_prev(seg, n, vp_ref, vl_ref, rr, sl), vc_ref[rr, :, sl]], axis=0)
            o_ref[rr, :, sl] = _dot_nn(p_buf[c], v) / den_buf[c]

    out_spec = _att_spec(R, lambda n: n)
    n_ch = len(chains)
    q3 = _att_3d(qkv, d, width)
    o, l = pl.pallas_call(
        body, name="att_fwd_g%d" % grp, grid=(P // R, nb),
        in_specs=_att_qkv_specs(R, nb),
        out_specs=[out_spec, out_spec],
        out_shape=[_sds((P, L, GW), F32)] * 2,
        scratch_shapes=[pltpu.VMEM((n_ch, BLK, 2 * BLK), F32), pltpu.VMEM((n_ch, BLK, 2 * BLK), CDT),
                        pltpu.VMEM((n_ch, BLK, HD), F32)],
        compiler_params=_params(2),
    )(*[q3] * 7)
    return o.reshape(_rm_shape(S, d, GW)), l.reshape(_rm_shape(S, d, GW))


def _att_bwd(grp, S, qkv, lse, do_g, c_g):
    _, d = ATT_GROUPS[grp]
    seg = d == 1
    width = ATT_BWD_SEGMENTS if seg else ATT_WIDTH
    P = _att_planes(d, width)
    L = S // P
    nb = L // BLK
    R = _att_step_planes(d, width)
    assert P % R == 0 and (not seg or P == R)
    slopes = _att_slopes(grp)
    scale = HD ** -0.5
    last = nb - 1
    chains = _att_chains(R)

    def body(q_ref, kp_ref, kc_ref, vp_ref, vc_ref, kl_ref, vl_ref, l_ref, do_ref, c_ref, dq_ref, dk_ref, dv_ref,
             ck, cv, fk, fv, s_buf, dp_buf, p_buf, ds_buf):
        n = pl.program_id(1)

        @pl.when(n == 0)
        def _():
            for buf in (ck, cv, fk, fv):
                buf[...] = jnp.zeros_like(buf)

        @pl.when(n < nb)
        def _():
            valid, valid_all, distf = _att_mask(n)
            for c, rr, sl, hh in chains:
                k = jnp.concatenate([_att_prev(seg, n, kp_ref, kl_ref, rr, sl), kc_ref[rr, :, sl]], axis=0)
                v = jnp.concatenate([_att_prev(seg, n, vp_ref, vl_ref, rr, sl), vc_ref[rr, :, sl]], axis=0)
                s_buf[c] = _dot_nt(q_ref[rr, :, sl], k)
                dp_buf[c] = _dot_nt(do_ref[rr, :, sl], v)
            for c, rr, sl, hh in chains:
                s = s_buf[c] * scale + (-slopes[hh] * d) * distf
                p = jnp.where(valid_all if seg and rr > 0 else valid, jnp.exp(s - l_ref[rr, :, sl][:, 0:1]), 0.0)
                p_buf[c] = p.astype(CDT)
                ds_buf[c] = (p * (dp_buf[c] + c_ref[rr, :, sl][:, 0:1]) * scale).astype(CDT)
            for c, rr, sl, hh in chains:
                k = jnp.concatenate([_att_prev(seg, n, kp_ref, kl_ref, rr, sl), kc_ref[rr, :, sl]], axis=0)
                ds = ds_buf[c]
                dq_ref[rr, :, sl] = _dot_nn(ds, k)
                dk = _dot_tn(ds, q_ref[rr, :, sl])
                dv = _dot_tn(p_buf[c], do_ref[rr, :, sl])
                dk_ref[rr, :, sl] = ck[rr, :, sl] + dk[0:BLK]
                dv_ref[rr, :, sl] = cv[rr, :, sl] + dv[0:BLK]
                ck[rr, :, sl] = dk[BLK:2 * BLK]
                cv[rr, :, sl] = dv[BLK:2 * BLK]
                if seg and rr > 0:
                    @pl.when(n == 0)
                    def _(rr=rr, sl=sl, dk=dk, dv=dv):
                        fk[rr - 1, :, sl] = dk[0:BLK]
                        fv[rr - 1, :, sl] = dv[0:BLK]

        @pl.when(n == nb)
        def _():
            dk_ref[...] = ck[...] + fk[...]
            dv_ref[...] = cv[...] + fv[...]

    blk = (R, BLK, GW)
    at_q = _att_spec(R, lambda n: jnp.minimum(n, last))
    behind = _att_spec(R, lambda n: jnp.maximum(n - 1, 0))
    n_ch = len(chains)
    q3 = _att_3d(qkv, d, width)
    res = pl.pallas_call(
        body, name="att_bwd_g%d" % grp, grid=(P // R, nb + 1),
        in_specs=_att_qkv_specs(R, nb) + [at_q, at_q, at_q],
        out_specs=[at_q, behind, behind],
        out_shape=[_sds((P, L, GW), F32)] * 3,
        scratch_shapes=[pltpu.VMEM(blk, F32)] * 4
        + [pltpu.VMEM((n_ch, BLK, 2 * BLK), F32), pltpu.VMEM((n_ch, BLK, 2 * BLK), F32),
           pltpu.VMEM((n_ch, BLK, 2 * BLK), CDT), pltpu.VMEM((n_ch, BLK, 2 * BLK), CDT)],
        compiler_params=_params(2),
    )(*[q3] * 7, _att_3d(lse, d, width), _att_3d(do_g, d, width), _att_3d(c_g, d, width))
    return [t.reshape(_rm_shape(S, d, GW)) for t in res]


def _mix_alpha(l0, l1, l2):
    mx = jnp.maximum(jnp.maximum(l0, l1), l2)
    e = [jnp.exp(l0 - mx), jnp.exp(l1 - mx), jnp.exp(l2 - mx)]
    tot = e[0] + e[1] + e[2]
    return [ei / tot for ei in e]


def _mix_fwd(S, os_, ls_, tm=512):
    dil = [d for _, d in ATT_GROUPS]

    def body(*refs):
        out, bufs = refs[6], refs[7:13]
        get = [_rm_reader(bufs[j], refs[j], dil[j % 3]) for j in range(6)]
        rows = slice(None)
        for s in range(ATT_HPG):
            al = _mix_alpha(*[get[3 + g](s, rows) for g in range(3)])
            mixed = al[0] * get[0](s, rows) + al[1] * get[1](s, rows) + al[2] * get[2](s, rows)
            out[:, s * HD:(s + 1) * HD] = mixed.astype(out.dtype)

    specs = [_rm_spec(tm, d, GW) for d in dil]
    return pl.pallas_call(
        body, name="mix_fwd", grid=(S // tm,), in_specs=specs * 2, out_specs=pl.BlockSpec((tm, GW), lambda i: (i, 0)),
        out_shape=_sds((S, GW), CDT), scratch_shapes=[pltpu.VMEM((ATT_HPG, tm, HD), F32)] * 6,
        compiler_params=_params(1))(*os_, *ls_)


def _mix_bwd(S, os_, ls_, do_a, tm=512):
    dil = [d for _, d in ATT_GROUPS]

    def body(*refs):
        d_ref, outs, bufs, tmps = refs[6], refs[7:13], refs[13:19], refs[19:25]
        get = [_rm_reader(bufs[j], refs[j], dil[j % 3]) for j in range(6)]
        for s in range(ATT_HPG):
            cols = slice(s * HD, (s + 1) * HD)

            def chunk(rows, s=s, cols=cols):
                al = _mix_alpha(*[get[3 + g](s, rows) for g in range(3)])
                dv = d_ref[rows, cols]
                o_a = al[0] * get[0](s, rows) + al[1] * get[1](s, rows) + al[2] * get[2](s, rows)
                dsum = jnp.sum(dv * o_a, axis=-1, keepdims=True)
                for g in range(3):
                    tmps[g][rows, :] = al[g] * dv
                    tmps[3 + g][rows, :] = -(al[g] * dsum)

            chunk(slice(None))
            for j in range(6):
                _rm_put(outs[j], cols, tmps[j], dil[j % 3])

    specs = [_rm_spec(tm, d, GW) for d in dil]
    res = pl.pallas_call(
        body, name="mix_bwd", grid=(S // tm,), in_specs=specs * 2 + [pl.BlockSpec((tm, GW), lambda i: (i, 0))],
        out_specs=specs * 2,
        out_shape=[_sds(_rm_shape(S, d, GW), CDT) for d in dil] + [_sds(_rm_shape(S, d, GW), F32) for d in dil],
        scratch_shapes=[pltpu.VMEM((ATT_HPG, tm, HD), F32)] * 6 + [pltpu.VMEM((tm, HD), F32)] * 6,
        compiler_params=_params(1))(*os_, *ls_, do_a)
    return res[:3], res[3:]


RET_FWD_CHUNKS = 2


def _ret_tables(dk):
    H, C = RET_HEADS, BLK
    log_g = jnp.log(1.0 - 2.0 ** (-5.0 - jnp.arange(H, dtype=F32)))
    idx = jnp.arange(C, dtype=F32)
    diff = idx[:, None] - idx[None, :]
    decay = jnp.where(diff >= 0, jnp.exp(log_g[:, None, None] * jnp.maximum(diff, 0.0)), 0.0)
    xi = jnp.exp(log_g[:, None] * (idx[None, :] + 1.0))
    zeta = jnp.exp(log_g[:, None] * (C - 1.0 - idx[None, :])) * (dk ** -0.5)
    g_chunk = jnp.exp(log_g * C)
    bc = lambda t: jnp.broadcast_to(t[:, :, None], (H, C, C))
    return decay, bc(xi), bc(zeta), jnp.broadcast_to(g_chunk[:, None, None], (H, 8, C))


def _gn_fwd(o, g, b):
    mu = jnp.mean(o, axis=-1, keepdims=True)
    xc = o - mu
    rstd = lax.rsqrt(jnp.mean(xc * xc, axis=-1, keepdims=True) + EPS)
    yh = xc * rstd
    return yh, rstd, yh * g + b


def _ret_specs(dk, dv, order, rows=BLK):
    H = RET_HEADS
    qk_w, v_w = H * dk, H * dv
    off_q = 3 * ATT_W
    off_k, off_v, off_g = off_q + qk_w, off_q + 2 * qk_w, off_q + 2 * qk_w + v_w
    assert 2 * dk == dv and all(off % dv == 0 for off in (off_q, off_k, off_v, off_g))

    def col(off, j):
        return pl.BlockSpec((rows, dv), lambda i: (order(i), off // dv + j))

    tab = pl.BlockSpec((H, BLK, BLK), lambda i: (0, 0, 0))
    return ([col(off_q, j) for j in range(H // 2)] + [col(off_k, j) for j in range(H // 2)]
            + [col(off_v, j) for j in range(H)] + [col(off_g, j) for j in range(H)]
            + [tab, tab, tab, pl.BlockSpec((H, 8, BLK), lambda i: (0, 0, 0))])


def _ret_heads(refs, dk):
    H = RET_HEADS
    q_refs, k_refs = refs[0:H // 2], refs[H // 2:H]
    v_refs, gr_refs = refs[H:2 * H], refs[2 * H:3 * H]

    def head(h, rows=slice(None)):
        cols = slice((h % 2) * dk, (h % 2 + 1) * dk)
        return q_refs[h // 2][rows, cols], k_refs[h // 2][rows, cols], v_refs[h][rows, :], gr_refs[h][rows, :]

    return head, refs[3 * H:3 * H + 4]


def _ret_fwd(proj, gn_g, gn_b, dk, dv):
    S = proj.shape[0]
    N = S // BLK
    H = RET_HEADS
    per = RET_FWD_CHUNKS
    assert N % per == 0
    kscale = dk ** -0.5
    n_in = 3 * H + 4

    def body(*refs):
        head, (dec_ref, xi_ref, zeta_ref, gc_ref) = _ret_heads(refs, dk)
        g_ref, b_ref, opre_ref, or_ref, st_ref, state, s_buf, cross_buf = refs[n_in:n_in + 8]
        n = pl.program_id(0)

        @pl.when(n == 0)
        def _():
            state[...] = jnp.zeros_like(state)

        for c in range(per):
            rows = slice(c * BLK, (c + 1) * BLK)
            for h in range(H):
                q, k, v, _ = head(h, rows)
                s_buf[h] = _dot_nt(q, k)
                st = state[h]
                st_c = st.astype(CDT)
                st_ref[c, h] = st_c
                cross_buf[h] = _dot_nn(q, st_c)
                kz = (k.astype(F32) * zeta_ref[h][:, 0:1]).astype(CDT)
                state[h] = st * gc_ref[h][0:1, 0:1] + _dot_tn(kz, v)
            for h in range(H):
                vs = slice(h * dv, (h + 1) * dv)
                _, _, v, gr = head(h, rows)
                s = s_buf[h] * kscale * dec_ref[h]
                o = _dot_nn(s.astype(CDT), v) + cross_buf[h] * xi_ref[h][:, 0:1]
                opre_ref[rows, vs] = o
                _, _, y = _gn_fwd(o, g_ref[:, vs], b_ref[:, vs])
                gr = gr.astype(F32)
                or_ref[rows, vs] = (y * (gr * _sigmoid(gr))).astype(or_ref.dtype)

    v_w = H * dv
    row = pl.BlockSpec((1, v_w), lambda i: (0, 0))
    tile = pl.BlockSpec((per * BLK, v_w), lambda i: (i, 0))
    return pl.pallas_call(
        body, name="ret_fwd", grid=(N // per,),
        in_specs=_ret_specs(dk, dv, lambda i: i, per * BLK) + [row, row],
        out_specs=[tile, tile, pl.BlockSpec((per, H, dk, dv), lambda i: (i, 0, 0, 0))],
        out_shape=[_sds((S, v_w), F32), _sds((S, v_w), CDT), _sds((N, H, dk, dv), CDT)],
        scratch_shapes=[pltpu.VMEM((H, dk, dv), F32), pltpu.VMEM((H, BLK, BLK), F32), pltpu.VMEM((H, BLK, dv), F32)],
        compiler_params=_params(1),
    )(*[proj] * (3 * H), *_ret_tables(dk), gn_g, gn_b)


def _ret_bwd(proj, gn_g, gn_b, o_pre, states, d_or, dga, dgb, dk, dv):
    S, in_w = proj.shape
    N = S // BLK
    H = RET_HEADS
    qk_w, v_w = H * dk, H * dv
    kscale = dk ** -0.5
    n_in = 3 * H + 4
    out_w = 2 * qk_w + 2 * v_w
    gate_w = dga.shape[1]
    col0 = 3 * ATT_W
    assert col0 + out_w + 2 * gate_w == in_w
    rev = lambda i: N - 1 - i

    def body(*refs):
        head, (dec_ref, xi_ref, zeta_ref, gc_ref) = _ret_heads(refs, dk)
        (g_ref, b_ref, opre_ref, st_ref, dor_ref, dga_ref, dgb_ref, dproj_ref, dg_ref, db_ref, dstate, stage,
         sem, do_buf, dox_buf, a_buf, g_buf, dq_buf, dk_buf, dv_buf) = refs[n_in:n_in + 20]
        i = pl.program_id(0)
        slot = i % 2
        out_ref = stage.at[slot]

        def out_copy(s, step):
            rows = pl.ds(pl.multiple_of(rev(step) * BLK, BLK), BLK)
            return pltpu.make_async_copy(stage.at[s], dproj_ref.at[rows, pl.ds(col0, in_w - col0)], sem.at[s])

        @pl.when(i >= 2)
        def _():
            out_copy(slot, i - 2).wait()

        @pl.when(i == 0)
        def _():
            dstate[...] = jnp.zeros_like(dstate)
            dg_ref[...] = jnp.zeros_like(dg_ref)
            db_ref[...] = jnp.zeros_like(db_ref)

        out_ref[:, out_w:out_w + gate_w] = dga_ref[...]
        out_ref[:, out_w + gate_w:out_w + 2 * gate_w] = dgb_ref[...]
        for h in range(H):
            vs = slice(h * dv, (h + 1) * dv)
            _, _, _, gr = head(h)
            gr = gr.astype(F32)
            sg = _sigmoid(gr)
            gain = g_ref[:, vs]
            yh, rstd, y = _gn_fwd(opre_ref[:, vs], gain, b_ref[:, vs])
            d_or_v = dor_ref[:, vs]
            dy = d_or_v * (gr * sg)
            out_ref[:, 2 * qk_w + v_w + h * dv:2 * qk_w + v_w + (h + 1) * dv] = (
                d_or_v * y * (sg * (1.0 + gr * (1.0 - sg)))).astype(out_ref.dtype)
            dg_ref[:, vs] += jnp.sum(dy * yh, axis=0, keepdims=True)
            db_ref[:, vs] += jnp.sum(dy, axis=0, keepdims=True)
            dyh = dy * gain
            do = rstd * (dyh - jnp.mean(dyh, axis=-1, keepdims=True)
                         - yh * jnp.mean(dyh * yh, axis=-1, keepdims=True))
            do_buf[h] = do.astype(CDT)
            dox_buf[h] = (do * xi_ref[h][:, 0:1]).astype(CDT)
        for h in range(H):
            q, k, v, _ = head(h)
            dox = dox_buf[h]
            a_buf[h] = _dot_nt(q, k)
            g_buf[h] = _dot_nt(do_buf[h], v)
            dsn = dstate[h]
            dsn_c = dsn.astype(CDT)
            kz = (k.astype(F32) * zeta_ref[h][:, 0:1]).astype(CDT)
            dq_buf[h] = _dot_nt(dox, st_ref[h])
            dk_buf[h] = _dot_nt(v, dsn_c)
            dv_buf[h] = _dot_nn(kz, dsn_c)
            dstate[h] = dsn * gc_ref[h][0:1, 0:1] + _dot_tn(q, dox)
        for h in range(H):
            q, k, _, _ = head(h)
            decay = dec_ref[h]
            a_c = (a_buf[h] * kscale * decay).astype(CDT)
            g_c = (g_buf[h] * decay).astype(CDT)
            dq = _dot_nn(g_c, k) * kscale + dq_buf[h]
            dkk = _dot_tn(g_c, q) * kscale + dk_buf[h] * zeta_ref[h][:, 0:1]
            dvv = _dot_tn(a_c, do_buf[h]) + dv_buf[h]
            out_ref[:, h * dk:(h + 1) * dk] = dq.astype(out_ref.dtype)
            out_ref[:, qk_w + h * dk:qk_w + (h + 1) * dk] = dkk.astype(out_ref.dtype)
            out_ref[:, 2 * qk_w + h * dv:2 * qk_w + (h + 1) * dv] = dvv.astype(out_ref.dtype)

        cp = out_copy(slot, i)
        cp.start()

        @pl.when(i == N - 1)
        def _():
            cp.wait()
            if N >= 2:
                out_copy(1 - slot, i - 1).wait()

    row = pl.BlockSpec((1, v_w), lambda i: (0, 0))
    tile = pl.BlockSpec((BLK, v_w), lambda i: (rev(i), 0))
    gate = pl.BlockSpec((BLK, gate_w), lambda i: (rev(i), 0))
    return pl.pallas_call(
        body, name="ret_bwd", grid=(N,),
        in_specs=_ret_specs(dk, dv, rev) + [row, row, tile,
                 pl.BlockSpec((None, H, dk, dv), lambda i: (rev(i), 0, 0, 0)), tile, gate, gate],
        out_specs=[pl.BlockSpec(memory_space=pl.ANY), row, row],
        out_shape=[_sds((S, in_w), CDT), _sds((1, v_w), F32), _sds((1, v_w), F32)],
        scratch_shapes=[pltpu.VMEM((H, dk, dv), F32), pltpu.VMEM((2, BLK, in_w - col0), CDT),
                        pltpu.SemaphoreType.DMA((2,)),
                        pltpu.VMEM((H, BLK, dv), CDT), pltpu.VMEM((H, BLK, dv), CDT),
                        pltpu.VMEM((H, BLK, BLK), F32), pltpu.VMEM((H, BLK, BLK), F32),
                        pltpu.VMEM((H, BLK, dk), F32), pltpu.VMEM((H, BLK, dk), F32), pltpu.VMEM((H, BLK, dv), F32)],
        compiler_params=_params(1),
    )(*[proj] * (3 * H), *_ret_tables(dk), gn_g, gn_b, o_pre, states, d_or, dga, dgb)


def _merge_fwd(o_a, o_r, wa, wb, proj, d_model, tm=1024, tn=512):
    S, in_w = proj.shape
    off_a, off_b = in_w - 2 * d_model, in_w - d_model
    assert off_a % tn == 0 and off_b % tn == 0

    def body(oa_ref, or_ref, wa_ref, wb_ref, ga_ref, gb_ref, y_ref, pa_ref, pb_ref):
        pa = _dot_nn(oa_ref[...], wa_ref[...])
        pb = _dot_nn(or_ref[...], wb_ref[...])
        y = _sigmoid(ga_ref[...].astype(F32)) * pa + _sigmoid(gb_ref[...].astype(F32)) * pb
        y_ref[...] = y.astype(y_ref.dtype)
        pa_ref[...] = pa.astype(pa_ref.dtype)
        pb_ref[...] = pb.astype(pb_ref.dtype)

    ka, kb = o_a.shape[1], o_r.shape[1]
    out = pl.BlockSpec((tm, tn), lambda i, j: (i, j))
    return pl.pallas_call(
        body, name="merge_fwd", grid=(S // tm, d_model // tn),
        in_specs=[pl.BlockSpec((tm, ka), lambda i, j: (i, 0)), pl.BlockSpec((tm, kb), lambda i, j: (i, 0)),
                  pl.BlockSpec((ka, tn), lambda i, j: (0, j)), pl.BlockSpec((kb, tn), lambda i, j: (0, j)),
                  pl.BlockSpec((tm, tn), lambda i, j: (i, off_a // tn + j)),
                  pl.BlockSpec((tm, tn), lambda i, j: (i, off_b // tn + j))],
        out_specs=[out, out, out], out_shape=[_sds((S, d_model), CDT)] * 3,
        compiler_params=_params(2))(o_a, o_r, wa, wb, proj, proj)


def _d_x(dproj, w_in, x, g, dx1, dep, tm=512, row_groups=2):
    S, D = x.shape
    n_sh, _, ns = w_in.shape
    nt = S // tm // row_groups
    deps = [d for d in (dep,) if d is not None]

    def body(a_ref, b_ref, x_ref, g_ref, r_ref, *rest):
        dx_ref, dg_ref, acc = rest[len(deps):]
        h, k, i = pl.program_id(0), pl.program_id(1), pl.program_id(2)

        @pl.when(k == 0)
        def _():
            acc[i] = jnp.zeros((tm, D), F32)

        acc[i] += _dot_nt(a_ref[...], b_ref[...])

        @pl.when(k == n_sh - 1)
        def _():
            _ep_rms_bwd(acc[i], (x_ref, g_ref, r_ref), (dx_ref, dg_ref), h * nt + i)

    def last_only(h, k, i):
        return (h * nt + jnp.where(k == n_sh - 1, i, 0), 0)

    return pl.pallas_call(
        body, name="d_x", grid=(row_groups, n_sh, nt),
        in_specs=[pl.BlockSpec((tm, ns), lambda h, k, i: (h * nt + i, k)),
                  pl.BlockSpec((None, D, ns), lambda h, k, i: (k, 0, 0)),
                  pl.BlockSpec((tm, D), last_only), pl.BlockSpec((1, D), lambda h, k, i: (0, 0)),
                  pl.BlockSpec((tm, D), last_only)] + [pl.BlockSpec(memory_space=pl.ANY)] * len(deps),
        out_specs=[pl.BlockSpec((tm, D), last_only), pl.BlockSpec((1, D), lambda h, k, i: (0, 0))],
        out_shape=[_sds((S, D), F32), _sds((1, D), F32)],
        scratch_shapes=[pltpu.VMEM((nt, tm, D), F32)],
        compiler_params=_params(3))(dproj, w_in, x, g, dx1, *deps)


def _local_step(x, target, w_in_mine, chip, others, near_w_in, far_w_in, small, late_weights, on_grads, deps0=()):
    S, D = x.shape
    ns_in = w_in_mine.shape[1]
    in_w = N_CHIPS * ns_in
    d_ff = 4 * D
    ret_v_w = 2 * D
    dv = ret_v_w // RET_HEADS
    dk = (in_w - 3 * ATT_W - 2 * ret_v_w - 2 * D) // (2 * RET_HEADS)
    gqk = jnp.concatenate([small["q_norm_g"].reshape(1, ATT_W), small["k_norm_g"].reshape(1, ATT_W)], axis=1)
    g1, g2 = small["norm1_g"], small["norm2_g"]
    gn_g, gn_b = small["ret_gn_g"], small["ret_gn_b"]

    proj_sds = _sds((S, in_w), CDT)
    xn, proj = _in_proj_mine(x, g1, w_in_mine, chip, proj_sds, deps0)
    for stage, (get_w_in, chips) in enumerate(((near_w_in, others[:2]), (far_w_in, others[2:]))):
        w_in, started = get_w_in(proj)
        (proj,) = _matmul(
            "in_proj_far%d" % stage, "nn", xn, w_in, tm=1024, tn=ns_in, tk=D, prefetch=[chips],
            n_cols=chips.shape[0] * ns_in, b_spec=pl.BlockSpec((None, D, ns_in), lambda i, j, k, o: (o[j], 0, 0)),
            outs=[(proj_sds, pl.BlockSpec((1024, ns_in), lambda i, j, k, o: (i, o[j])))], epilogue=_ep_store,
            deps=[proj, started], alias_dep_to_out=(0, 0), j_outer=True)
    qkv = _qknorm_fwd(proj, gqk)
    att = [_att_fwd(g, S, qkv[g]) for g in range(3)]
    os_, ls_ = [a[0] for a in att], [a[1] for a in att]
    o_a = _mix_fwd(S, os_, ls_)
    o_pre, o_r, states = _ret_fwd(proj, gn_g, gn_b, dk, dv)
    w = late_weights(o_r)
    y, pa, pb = _merge_fwd(o_a, o_r, w["w_proj_a"], w["w_proj_b"], proj, D)
    x1, xn2 = _matmul("out_proj", "nn", y, w["w_out"], tm=1024, tn=D, tk=D,
                      extras=[(x, _mn(1024, D)), (g2, _row(D))],
                      outs=[(_sds((S, D), F32), _mn(1024, D)), (_sds((S, D), CDT), _mn(1024, D))],
                      epilogue=_ep_resid_norm)
    hid, act = _matmul("mlp_up", "nn", xn2, w["w_up"], tm=1024, tn=2048, tk=D, j_outer=True,
                       outs=[(_sds((S, d_ff), CDT), _mn(1024, 2048))] * 2, epilogue=_ep_up)
    dx2, dx2c, loss_row = _matmul(
        "mlp_down_loss", "nn", act, w["w_down"], tm=512, tn=D, tk=d_ff,
        extras=[(x1, _mn(512, D)), (target, _mn(512, D))],
        outs=[(_sds((S, D), F32), _mn(512, D)), (_sds((S, D), CDT), _mn(512, D)), (_sds((1, D), F32), _row(D))],
        epilogue=functools.partial(_ep_down_loss, inv_d=1.0 / D))
    loss = 0.5 * jnp.sum(loss_row) / D

    (dh,) = _matmul("d_hidden", "nt", dx2c, w["w_down"], tm=1024, tn=2048, tk=D, j_outer=True,
                    extras=[(hid, _mn(1024, 2048))], outs=[(_sds((S, d_ff), CDT), _mn(1024, 2048))],
                    epilogue=_ep_dh)
    (gw_down,) = _matmul("dw_down", "tn", act, dx2c, tm=1024, tn=D, tk=2048,
                         outs=[(_sds((d_ff, D), F32), _mn(1024, D))], epilogue=_ep_store)
    (gw_up,) = _matmul("dw_up", "tn", xn2, dh, tm=D, tn=1024, tk=2048,
                       outs=[(_sds((D, d_ff), F32), _mn(D, 1024))], epilogue=_ep_store)
    tok = on_grads({"w_down": gw_down, "w_up": gw_up})
    dx1, dx1c, dg2 = _matmul(
        "d_x1", "nt", dh, w["w_up"], tm=512, tn=D, tk=d_ff,
        extras=[(x1, _mn(512, D)), (g2, _row(D)), (dx2, _mn(512, D))],
        outs=[(_sds((S, D), F32), _mn(512, D)), (_sds((S, D), CDT), _mn(512, D)), (_sds((1, D), F32), _row(D))],
        epilogue=_ep_rms_bwd, deps=[tok])

    gt = 512
    assert (in_w - 2 * D) % gt == 0
    off_a, off_b = (in_w - 2 * D) // gt, (in_w - D) // gt
    dpa, dpb, dga, dgb = _matmul(
        "d_gates", "nt", dx1c, w["w_out"], tm=1024, tn=gt, tk=D,
        extras=[(proj, _mn(1024, gt, off_a)), (proj, _mn(1024, gt, off_b)), (pa, _mn(1024, gt)),
                (pb, _mn(1024, gt))],
        outs=[(_sds((S, D), CDT), _mn(1024, gt))] * 4, epilogue=_ep_gates)
    (gw_out,) = _matmul("dw_out", "tn", y, dx1c, tm=D, tn=D, tk=1024,
                        outs=[(_sds((D, D), F32), _mn(D, D))], epilogue=_ep_store)
    (gw_pa,) = _matmul("dw_proj_a", "tn", o_a, dpa, tm=GW, tn=D, tk=1024,
                       outs=[(_sds((GW, D), F32), _mn(GW, D))], epilogue=_ep_store)
    (gw_pb,) = _matmul("dw_proj_b", "tn", o_r, dpb, tm=1024, tn=D, tk=2048,
                       outs=[(_sds((ret_v_w, D), F32), _mn(1024, D))], epilogue=_ep_store)
    (do_a,) = _matmul("d_o_a", "nt", dpa, w["w_proj_a"], tm=1024, tn=GW, tk=D,
                      outs=[(_sds((S, GW), F32), _mn(1024, GW))], epilogue=_ep_store)
    tok = on_grads({"w_out": gw_out, "w_proj_a": gw_pa, "w_proj_b": gw_pb})
    (d_or,) = _matmul("d_o_r", "nt", dpb, w["w_proj_b"], tm=1024, tn=ret_v_w, tk=D,
                      outs=[(_sds((S, ret_v_w), F32), _mn(1024, ret_v_w))], epilogue=_ep_store, deps=[tok])

    dproj, dgn_g, dgn_b = _ret_bwd(proj, gn_g, gn_b, o_pre, states, d_or, dga, dgb, dk, dv)
    do_gs, c_gs = _mix_bwd(S, os_, ls_, do_a)
    datt_parts = [_att_bwd(g, S, qkv[g], ls_[g], do_gs[g], c_gs[g]) for g in range(3)]
    dproj, dgqk = _qknorm_bwd(proj, gqk, [p[0] for p in datt_parts], [p[1] for p in datt_parts],
                              [p[2] for p in datt_parts], dproj)

    (gw_in,) = _matmul(
        "dw_in", "tn", xn, dproj, tm=512, tn=ns_in, tk=1024,
        outs=[(_sds((N_CHIPS, D, ns_in), F32), pl.BlockSpec((None, 512, ns_in), lambda i, j, k: (j, i, 0)))],
        epilogue=_ep_store)
    tok = on_grads({"w_in": gw_in})
    grad_x, dg1 = _d_x(dproj, w_in, x, g1, dx1, tok)

    smallg = {"norm1_g": dg1, "q_norm_g": dgqk[:, :ATT_W], "k_norm_g": dgqk[:, ATT_W:],
              "ret_gn_g": dgn_g, "ret_gn_b": dgn_b, "norm2_g": dg2}
    return loss, grad_x, smallg


N_CHIPS = 4
N_DEV = 8


def _place():
    x, y, c = lax.axis_index("x"), lax.axis_index("y"), lax.axis_index("c")
    return x, y, c


def _other_chips(x, y):
    out = []
    for fx, fy in ((1, 0), (0, 1), (1, 1)):
        px = 1 - x if fx else x
        py = 1 - y if fy else y
        out.append(((px, py), 2 * px + py))
    return out


SEM_SPEC = pl.BlockSpec(memory_space=pltpu.SEMAPHORE)
ANY_SPEC = pl.BlockSpec(memory_space=pl.ANY)
EFFECT = pltpu.SideEffectType.DATAFLOW_SIDE_EFFECTING


def _ici_copies(kind, srcs, lands, send, recv, which=(0, 1, 2)):
    x, y, c = _place()
    me = 2 * x + y
    out = []
    for w, (s, l) in enumerate(zip(srcs, lands)):
        for j, ((px, py), pidx) in enumerate(_other_chips(x, y)):
            if j not in which:
                continue
            if kind == "gather":
                half = s.shape[0] // 2
                rows = pl.ds(c * half, half)
                src, dst_there, dst_here = s.at[rows, :], l.at[me, rows, :], l.at[pidx, rows, :]
            else:
                src, dst_there, dst_here = s.at[pidx], l.at[me], l.at[pidx]
            out.append((src, dst_there, dst_here, send.at[3 * w + j], recv.at[3 * w + j], (px, py, c)))
    return out


def _exchange_start(name, kind, srcs, land_shapes, which=(0, 1, 2), lands=None):
    n = len(srcs)
    if lands is None:
        lands = [lax.empty(shape, dtype) for shape, dtype in land_shapes]

    def body(*refs):
        src_refs, land_refs = refs[:n], refs[n:2 * n]
        send, recv = refs[2 * n], refs[2 * n + 1]
        token = refs[-1]
        for src, dst, _, ss, rs, dev in _ici_copies(kind, src_refs, land_refs, send, recv, which):
            pltpu.make_async_remote_copy(src_ref=src, dst_ref=dst, send_sem=ss, recv_sem=rs, device_id=dev,
                                         device_id_type=MESH).start()
        token[...] = jnp.zeros_like(token)

    thru = [pltpu.HBM(s.shape, s.dtype) for s in srcs] + [pltpu.HBM(shape, dtype) for shape, dtype in land_shapes]
    res = pl.pallas_call(
        body, name=name,
        out_shape=(pltpu.SemaphoreType.DMA((3 * n,)), pltpu.SemaphoreType.DMA((3 * n,)), *thru, _sds((8, LANES), F32)),
        in_specs=[HBM_SPEC] * (2 * n), out_specs=(SEM_SPEC, SEM_SPEC, *[HBM_SPEC] * (2 * n), VMEM_SPEC),
        input_output_aliases={i: 2 + i for i in range(2 * n)},
        compiler_params=pltpu.CompilerParams(has_side_effects=EFFECT),
    )(*[pltpu.with_memory_space_constraint(s, pltpu.HBM) for s in srcs],
      *[pltpu.with_memory_space_constraint(l, pltpu.HBM) for l in lands])
    return res[0], res[1], list(res[2:2 + n]), list(res[2 + n:2 + 2 * n]), res[-1]


def _exchange_wait(name, kind, send, recv, srcs, lands, after, which=(0, 1, 2)):
    n = len(srcs)

    def body(*refs):
        src_refs, land_refs = refs[:n], refs[n:2 * n]
        send_ref, recv_ref = refs[2 * n], refs[2 * n + 1]
        for src, _, dst, ss, rs, dev in _ici_copies(kind, src_refs, land_refs, send_ref, recv_ref, which):
            cp = pltpu.make_async_remote_copy(src_ref=src, dst_ref=dst, send_sem=ss, recv_sem=rs, device_id=dev,
                                              device_id_type=MESH)
            cp.wait_send()
            cp.wait_recv()

    thru = [pltpu.HBM(t.shape, t.dtype) for t in list(srcs) + list(lands)]
    res = pl.pallas_call(
        body, name=name, out_shape=thru,
        in_specs=[HBM_SPEC] * (2 * n) + [SEM_SPEC, SEM_SPEC, ANY_SPEC], out_specs=[HBM_SPEC] * (2 * n),
        input_output_aliases={i: i for i in range(2 * n)},
        compiler_params=pltpu.CompilerParams(has_side_effects=EFFECT),
    )(*srcs, *lands, send, recv, after)
    return list(res[:n]), list(res[n:])


PAIR_TILE_ELEMS = 1 << 20


def _pair_fill(name, gathered, mine, core, others, chip, write_mine=True):
    k, r, C = gathered.shape
    half = r // 2
    tr = _row_tile(half, C, PAIR_TILE_ELEMS, mult=16)
    nt = half // tr
    n_far = others.shape[0]

    def body(c_ref, o_ref, chip_ref, in_ref, mine_ref, out_ref, slot, send, recv):
        j = pl.program_id(0)
        _sibling_barrier((j == 0) & (pl.program_id(1) == 0))
        b = (j * nt + pl.program_id(1)) % 2
        x, y, c = _place()
        cp = pltpu.make_async_remote_copy(src_ref=in_ref, dst_ref=slot.at[b], send_sem=send.at[b],
                                          recv_sem=recv.at[b], device_id=(x, y, 1 - c), device_id_type=MESH)

        @pl.when(j < n_far)
        def _():
            cp.start()
            cp.wait_recv()
            out_ref[...] = slot[b]
            cp.wait_send()

        @pl.when(j >= n_far)
        def _():
            out_ref[...] = mine_ref[...]

    def far(j):
        return jnp.minimum(j, n_far - 1)

    grid_spec = pltpu.PrefetchScalarGridSpec(
        num_scalar_prefetch=3, grid=(n_far + (2 if write_mine else 0), nt),
        in_specs=[pl.BlockSpec((tr, C), lambda j, i, c, o, m: (
                      (2 * o[far(j)] + c[0]) * nt + jnp.where(j < n_far, i, nt - 1), 0)),
                  pl.BlockSpec((tr, C), lambda j, i, c, o, m: (jnp.where(j < n_far, 0, (j - n_far) * nt + i), 0))],
        out_specs=pl.BlockSpec((tr, C), lambda j, i, c, o, m: (
            jnp.where(j < n_far, 2 * o[far(j)] + 1 - c[0], 2 * m[0] + j - n_far) * nt + i, 0)),
        scratch_shapes=[pltpu.VMEM((2, tr, C), gathered.dtype), pltpu.SemaphoreType.DMA((2,)),
                        pltpu.SemaphoreType.DMA((2,))])
    out = pl.pallas_call(body, name=name, grid_spec=grid_spec, out_shape=_sds((k * r, C), gathered.dtype),
                         input_output_aliases={3: 0}, compiler_params=_params(2, PAIR_FILL_ID))(
                             core, others, chip, gathered.reshape(k * r, C), mine)
    return out.reshape(k, r, C)


def _pair_reduce(name, g, core):
    k, R, C = g.shape
    half = R // 2
    tr = _row_tile(half, C, PAIR_TILE_ELEMS, mult=16)
    nt = half // tr

    def body(c_ref, mine_ref, give_ref, out_ref, wire_ref, stage, slot, send, recv):
        _sibling_barrier((pl.program_id(0) == 0) & (pl.program_id(1) == 0))
        b = (pl.program_id(0) * nt + pl.program_id(1)) % 2
        x, y, c = _place()
        stage[b] = give_ref[...].astype(stage.dtype)
        cp = pltpu.make_async_remote_copy(src_ref=stage.at[b], dst_ref=slot.at[b], send_sem=send.at[b],
                                          recv_sem=recv.at[b], device_id=(x, y, 1 - c), device_id_type=MESH)
        cp.start()
        cp.wait_recv()
        tot = mine_ref[...] + slot[b].astype(F32)
        out_ref[...] = tot
        wire_ref[...] = tot.astype(wire_ref.dtype)
        cp.wait_send()

    blk = (tr, C)
    out_spec = pl.BlockSpec(blk, lambda s, i, c: (s * nt + i, 0))
    grid_spec = pltpu.PrefetchScalarGridSpec(
        num_scalar_prefetch=1, grid=(k, nt),
        in_specs=[pl.BlockSpec(blk, lambda s, i, c: ((2 * s + c[0]) * nt + i, 0)),
                  pl.BlockSpec(blk, lambda s, i, c: ((2 * s + 1 - c[0]) * nt + i, 0))],
        out_specs=[out_spec, out_spec],
        scratch_shapes=[pltpu.VMEM((2, tr, C), CDT), pltpu.VMEM((2, tr, C), CDT), pltpu.SemaphoreType.DMA((2,)),
                        pltpu.SemaphoreType.DMA((2,))])
    g2 = g.reshape(k * R, C)
    out, wire = pl.pallas_call(body, name=name, grid_spec=grid_spec,
                               out_shape=[_sds((k * half, C), F32), _sds((k * half, C), CDT)],
                               compiler_params=_params(2, PAIR_REDUCE_ID))(core, g2, g2)
    return out, wire.reshape(k, half, C)


def _all_reduce_small(v):
    r, cdim = v.shape

    def body(v_ref, o_ref, buf, send, recv):
        x, y, c = _place()
        me = 4 * x + 2 * y + c
        buf[me] = v_ref[...]
        sends = []
        for m in range(1, N_DEV):
            px = 1 - x if m & 4 else x
            py = 1 - y if m & 2 else y
            pc = 1 - c if m & 1 else c
            cp = pltpu.make_async_remote_copy(src_ref=v_ref, dst_ref=buf.at[me], send_sem=send.at[m - 1],
                                              recv_sem=recv.at[m - 1], device_id=(px, py, pc), device_id_type=MESH)
            cp.start()
            sends.append((cp, 4 * px + 2 * py + pc))
        for m, (cp, pidx) in enumerate(sends):
            pltpu.make_async_remote_copy(src_ref=v_ref, dst_ref=buf.at[pidx], send_sem=send.at[m], recv_sem=recv.at[m],
                                         device_id=(x, y, c), device_id_type=MESH).wait_recv()
        for cp, _ in sends:
            cp.wait_send()
        tot = buf[0]
        for k in range(1, N_DEV):
            tot = tot + buf[k]
        o_ref[...] = tot

    return pl.pallas_call(
        body, name="all_reduce_small", in_specs=[VMEM_SPEC], out_specs=VMEM_SPEC,
        out_shape=_sds((r, cdim), F32),
        scratch_shapes=[pltpu.VMEM((N_DEV, r, cdim), F32), pltpu.SemaphoreType.DMA((N_DEV - 1,)),
                        pltpu.SemaphoreType.DMA((N_DEV - 1,))],
    )(v)


def _row_tile(rows, cols, budget_elems=1 << 18, mult=8):
    if rows % mult:
        return rows
    t = max(mult, (budget_elems // cols) // mult * mult)
    while rows % t:
        t -= mult
    return t


def _adamw_update(w, g, m, v):
    nm = ADAM_B1 * m + (1.0 - ADAM_B1) * g
    nv = ADAM_B2 * v + (1.0 - ADAM_B2) * (g * g)
    m_hat = nm / (1.0 - ADAM_B1 ** ADAM_STEP)
    v_hat = nv / (1.0 - ADAM_B2 ** ADAM_STEP)
    return -ADAM_LR * (m_hat / (jnp.sqrt(v_hat) + ADAM_EPS) + ADAM_WD * w), nm, nv


def _adamw(name, w, g, m, v):
    R, C = w.shape
    tr = _row_tile(R, C, 1 << 19)

    def body(w_ref, g_ref, m_ref, v_ref, d_ref, nm_ref, nv_ref):
        d_ref[...], nm_ref[...], nv_ref[...] = _adamw_update(w_ref[...], g_ref[...], m_ref[...], v_ref[...])

    spec = pl.BlockSpec((tr, C), lambda i: (i, 0))
    return pl.pallas_call(body, name=name, grid=(R // tr,), in_specs=[spec] * 4, out_specs=[spec] * 3,
                          out_shape=[_sds((R, C), F32)] * 3, compiler_params=_params(1))(w, g, m, v)


def _sum_share(name, own, by_chip, chip, others, core):
    k, half, C = by_chip.shape
    tr = _row_tile(half, C, PAIR_TILE_ELEMS // 2, mult=16)
    nt = half // tr

    def body(chip_ref, oth_ref, c_ref, own_ref, a_ref, b_ref, cc_ref, g_out, mine, slot, send, recv):
        p = pl.program_id(1)
        _sibling_barrier((pl.program_id(0) == 0) & (p == 0))
        b = pl.program_id(0) % 2
        x, y, c = _place()
        cp = pltpu.make_async_remote_copy(src_ref=mine.at[b], dst_ref=slot.at[b], send_sem=send.at[b],
                                          recv_sem=recv.at[b], device_id=(x, y, 1 - c), device_id_type=MESH)

        @pl.when(p == 0)
        def _():
            tot = ((own_ref[...] + a_ref[...].astype(F32)) + b_ref[...].astype(F32)) + cc_ref[...].astype(F32)
            mine[b] = tot
            cp.start()
            g_out[...] = tot

        @pl.when(p == 1)
        def _():
            cp.wait_recv()
            g_out[...] = slot[b]
            cp.wait_send()

    def piece(j):
        return pl.BlockSpec((tr, C), lambda i, p, chip, oth, c: (oth[j] * nt + i, 0))

    grid_spec = pltpu.PrefetchScalarGridSpec(
        num_scalar_prefetch=3, grid=(nt, 2),
        in_specs=[pl.BlockSpec((tr, C), lambda i, p, chip, oth, c: (chip[0] * nt + i, 0)),
                  piece(0), piece(1), piece(2)],
        out_specs=pl.BlockSpec((tr, C), lambda i, p, chip, oth, c: (
            jnp.where(p == 0, c[0], 1 - c[0]) * nt + i, 0)),
        scratch_shapes=[pltpu.VMEM((2, tr, C), F32), pltpu.VMEM((2, tr, C), F32), pltpu.SemaphoreType.DMA((2,)),
                        pltpu.SemaphoreType.DMA((2,))])
    by2 = by_chip.reshape(k * half, C)
    return pl.pallas_call(body, name=name, grid_spec=grid_spec, out_shape=_sds((2 * half, C), F32),
                          compiler_params=_params(2, SUM_SHARE_ID))(chip, others, core, own, by2, by2, by2)


BIG = ("w_in", "w_proj_a", "w_proj_b", "w_out", "w_up", "w_down")
COL_SHARDED = ("w_in", "w_proj_a", "w_up")
SMALL = ("norm1_g", "q_norm_g", "k_norm_g", "ret_gn_g", "ret_gn_b", "norm2_g")
ALL_W = ("norm1_g", "w_in", "q_norm_g", "k_norm_g", "ret_gn_g", "ret_gn_b", "w_proj_a", "w_proj_b", "w_out",
         "norm2_g", "w_up", "w_down")
LANES = 128


def _to_full(name, gathered):
    k, r, c = gathered.shape
    if name in COL_SHARDED:
        return gathered.transpose(1, 0, 2).reshape(r, k * c)
    return gathered.reshape(k * r, c)


def _to_shard_major(name, full):
    if name in COL_SHARDED:
        r, c4 = full.shape
        return full.reshape(r, N_CHIPS, c4 // N_CHIPS).transpose(1, 0, 2)
    r4, c = full.shape
    return full.reshape(N_CHIPS, r4 // N_CHIPS, c)


def kernel(x, norm1_g, w_in, q_norm_g, k_norm_g, ret_gn_g, ret_gn_b, w_proj_a, w_proj_b, w_out, norm2_g, w_up, w_down, loss_target, m_norm1_g, m_w_in, m_q_norm_g, m_k_norm_g, m_ret_gn_g, m_ret_gn_b, m_w_proj_a, m_w_proj_b, m_w_out, m_norm2_g, m_w_up, m_w_down, v_norm1_g, v_w_in, v_q_norm_g, v_k_norm_g, v_ret_gn_g, v_ret_gn_b, v_w_proj_a, v_w_proj_b, v_w_out, v_norm2_g, v_w_up, v_w_down):
    weights = dict(norm1_g=norm1_g, w_in=w_in, q_norm_g=q_norm_g, k_norm_g=k_norm_g, ret_gn_g=ret_gn_g,
                   ret_gn_b=ret_gn_b, w_proj_a=w_proj_a, w_proj_b=w_proj_b, w_out=w_out, norm2_g=norm2_g,
                   w_up=w_up, w_down=w_down)
    moments_m = dict(norm1_g=m_norm1_g, w_in=m_w_in, q_norm_g=m_q_norm_g, k_norm_g=m_k_norm_g, ret_gn_g=m_ret_gn_g,
                     ret_gn_b=m_ret_gn_b, w_proj_a=m_w_proj_a, w_proj_b=m_w_proj_b, w_out=m_w_out,
                     norm2_g=m_norm2_g, w_up=m_w_up, w_down=m_w_down)
    moments_v = dict(norm1_g=v_norm1_g, w_in=v_w_in, q_norm_g=v_q_norm_g, k_norm_g=v_k_norm_g, ret_gn_g=v_ret_gn_g,
                     ret_gn_b=v_ret_gn_b, w_proj_a=v_w_proj_a, w_proj_b=v_w_proj_b, w_out=v_w_out,
                     norm2_g=v_norm2_g, w_up=v_w_up, w_down=v_w_down)

    mx, my = lax.axis_index("x"), lax.axis_index("y")
    core = lax.axis_index("c").astype(jnp.int32).reshape(1)
    chip = (2 * mx + my).astype(jnp.int32).reshape(1)
    others = jnp.stack([2 * (1 - mx) + my, 2 * mx + 1 - my, 2 * (1 - mx) + 1 - my]).astype(jnp.int32)
    shards = {n: weights[n][0].astype(CDT) for n in BIG}
    def start_gather(name, names):
        return _exchange_start(name, "gather", [shards[n] for n in names],
                               [((N_CHIPS,) + shards[n].shape, CDT) for n in names])

    w_in_shape = [((N_CHIPS,) + shards["w_in"].shape, CDT)]
    n_send, n_recv, n_srcs, n_lands, n_token = _exchange_start(
        "gather_w_in_near_start", "gather", [shards["w_in"]], w_in_shape, which=(0, 1))
    late = [n for n in BIG if n != "w_in"]
    flight = {}

    def near_w_in(after):
        srcs, lands = _exchange_wait("gather_w_in_near_wait", "gather", n_send, n_recv, n_srcs, n_lands, after,
                                     which=(0, 1))
        d_send, d_recv, d_srcs, d_lands, _ = _exchange_start(
            "gather_w_in_diag_start", "gather", srcs, w_in_shape, which=(2,), lands=lands)
        flight["late"] = start_gather("gather_late_start", late)
        w_near = _pair_fill("pair_fill_w_in_near", d_lands[0], d_srcs[0], core, others[:2], chip)
        flight["diag"] = (d_send, d_recv, d_srcs, [w_near])
        return w_near, flight["late"][-1]

    def far_w_in(after):
        d_send, d_recv, d_srcs, d_lands = flight["diag"]
        srcs, lands = _exchange_wait("gather_w_in_diag_wait", "gather", d_send, d_recv, d_srcs, d_lands, after,
                                     which=(2,))
        return _pair_fill("pair_fill_w_in_diag", lands[0], srcs[0], core, others[2:], chip, write_mine=False), None

    def late_weights(after):
        l_send, l_recv, l_srcs, l_lands, _ = flight["late"]
        srcs, lands = _exchange_wait("gather_late_wait", "gather", l_send, l_recv, l_srcs, l_lands, after)
        out = {}
        for n, mine, land in zip(late, srcs, lands):
            out[n] = _to_full(n, _pair_fill("pair_fill_%s" % n, land, mine, core, others, chip))
        return out

    pending = []

    def on_grads(group):
        names = list(group)
        red = [_pair_reduce("pair_reduce_%s" % n, g if g.ndim == 3 else _to_shard_major(n, g), core)
               for n, g in group.items()]
        wires = [wire for _, wire in red]
        send, recv, srcs, lands, token = _exchange_start(
            "scatter_start_%s" % names[0], "scatter", wires, [(wire.shape, wire.dtype) for wire in wires])
        pending.append((names, [own for own, _ in red], send, recv, srcs, lands))
        return token

    small = {n: weights[n].reshape(1, -1) for n in SMALL}

    loss, grad_x, small_g = _local_step(x[0], loss_target[0], n_srcs[0], chip, others, near_w_in, far_w_in, small,
                                        late_weights, on_grads, deps0=[n_token])

    out_g, out_d, out_m, out_v = {}, {}, {}, {}
    for names, owns, send, recv, srcs, lands in pending:
        _, got = _exchange_wait("scatter_wait_%s" % names[0], "scatter", send, recv, srcs, lands, grad_x)
        for n, own, by_chip in zip(names, owns, got):
            shape = weights[n].shape
            g2 = _sum_share("sum_share_%s" % n, own, by_chip, chip, others, core)
            d, nm, nv = _adamw("adamw_%s" % n, weights[n][0], g2, moments_m[n][0], moments_v[n][0])
            out_g[n], out_d[n], out_m[n], out_v[n] = (t.reshape(shape) for t in (g2, d, nm, nv))

    packed = jnp.concatenate([small_g[n].reshape(1, -1) for n in SMALL], axis=1).reshape(-1, LANES)
    loss_tile = jnp.zeros((8, LANES), F32).at[0, 0].set(loss)
    red = _all_reduce_small(jnp.concatenate([packed, loss_tile], axis=0))
    loss = red[packed.shape[0], 0]
    red = red[:packed.shape[0]]

    def pack(by_name):
        return jnp.concatenate([by_name[n].reshape(1, -1) for n in SMALL], axis=1).reshape(-1, LANES)

    small_out = (red, *_adamw("adamw_small", pack(weights), red, pack(moments_m), pack(moments_v)))
    off = 0
    for n in SMALL:
        shape, size = weights[n].shape, weights[n].size
        out_g[n], out_d[n], out_m[n], out_v[n] = (
            t.reshape(1, -1)[:, off:off + size].reshape(shape) for t in small_out)
        off += size

    return (loss, grad_x[None], *[out_g[n] for n in ALL_W], *[out_d[n] for n in ALL_W],
            *[out_m[n] for n in ALL_W], *[out_v[n] for n in ALL_W])
```

```python
import functools

import jax
import jax.numpy as jnp
from jax import lax
from jax.experimental import pallas as pl
from jax.experimental.pallas import tpu as pltpu

CDT = jnp.bfloat16
F32 = jnp.float32
EPS = 1e-6

ATT_GROUPS = ((128, 1), (512, 4), (2048, 16))
ATT_HPG = 4
ATT_HEADS = 12
HD = 128
BLK = 128
ATT_W = ATT_HEADS * HD
GW = ATT_HPG * HD
RET_HEADS = 4

ADAM_LR = 0.001
ADAM_B1 = 0.9
ADAM_B2 = 0.999
ADAM_EPS = 1e-08
ADAM_WD = 0.01
ADAM_STEP = 10

VMEM_LIMIT_BYTES = 56 * 1024 * 1024
MESH = pl.DeviceIdType.MESH
HBM_SPEC = pl.BlockSpec(memory_space=pltpu.HBM)
VMEM_SPEC = pl.BlockSpec(memory_space=pltpu.VMEM)


def _params(n_axes, collective_id=None):
    return pltpu.CompilerParams(dimension_semantics=("arbitrary",) * n_axes,
                                vmem_limit_bytes=VMEM_LIMIT_BYTES, collective_id=collective_id)


PAIR_FILL_ID, PAIR_REDUCE_ID, SUM_SHARE_ID = 1, 2, 3


def _sibling_barrier(first_step):
    @pl.when(first_step)
    def _():
        sem = pltpu.get_barrier_semaphore()
        x, y, c = lax.axis_index("x"), lax.axis_index("y"), lax.axis_index("c")
        pl.semaphore_signal(sem, inc=1, device_id=(x, y, 1 - c), device_id_type=pl.DeviceIdType.MESH)
        pl.semaphore_wait(sem, 1)


def _dot_nn(a, b):
    return jnp.dot(a, b, preferred_element_type=F32)


def _dot_nt(a, b):
    return lax.dot_general(a, b, (((1,), (1,)), ((), ())), preferred_element_type=F32)


def _dot_tn(a, b):
    return lax.dot_general(a, b, (((0,), (0,)), ((), ())), preferred_element_type=F32)


def _sigmoid(v):
    return 1.0 / (1.0 + jnp.exp(-v))


def _matmul(name, mode, a, b, *, tm, tn, tk, extras=(), outs, epilogue, deps=(), b_spec=None, n_cols=None,
            prefetch=(), alias_dep_to_out=None, j_outer=False):
    deps = [d for d in deps if d is not None]
    if mode == "tn":
        K, M = a.shape
    else:
        M, K = a.shape
    if b_spec is None:
        (N, K2) = b.shape if mode == "nt" else b.shape[::-1]
        assert K == K2, (name, a.shape, b.shape)
        if mode == "nt":
            b_spec = pl.BlockSpec((tn, tk), lambda i, j, k, *p: (j, k))
        else:
            b_spec = pl.BlockSpec((tk, tn), lambda i, j, k, *p: (k, j))
    else:
        N = n_cols
    assert M % tm == 0 and N % tn == 0 and K % tk == 0, (name, a.shape, b.shape)
    ni, nj, nk = M // tm, N // tn, K // tk
    if mode == "tn":
        a_spec = pl.BlockSpec((tk, tm), lambda i, j, k, *p: (k, i))
    else:
        a_spec = pl.BlockSpec((tm, tk), lambda i, j, k, *p: (i, k))
    dot = {"nn": _dot_nn, "nt": _dot_nt, "tn": _dot_tn}[mode]
    n_ex, n_out, n_dep, n_pre = len(extras), len(outs), len(deps), len(prefetch)
    grid = (ni, nj, nk)
    if j_outer:
        grid = (nj, ni, nk)

        def swapped(spec):
            return pl.BlockSpec(spec.block_shape, lambda j, i, k, *p: spec.index_map(i, j, k, *p))

        a_spec, b_spec = swapped(a_spec), swapped(b_spec)
        extras = [(e, swapped(s)) for e, s in extras]
        outs = [(o, swapped(s)) for o, s in outs]

    def body(*refs):
        refs = refs[n_pre:]
        a_ref, b_ref = refs[0], refs[1]
        ex = refs[2:2 + n_ex]
        out = refs[2 + n_ex + n_dep:2 + n_ex + n_dep + n_out]
        acc = refs[-1] if nk > 1 else None
        i = pl.program_id(1 if j_outer else 0)
        k = pl.program_id(2)
        if nk == 1:
            epilogue(dot(a_ref[...].astype(CDT), b_ref[...].astype(CDT)), ex, out, i)
            return

        @pl.when(k == 0)
        def _():
            acc[...] = jnp.zeros_like(acc)

        acc[...] += dot(a_ref[...].astype(CDT), b_ref[...].astype(CDT))

        @pl.when(k == nk - 1)
        def _():
            epilogue(acc[...], ex, out, i)

    grid_spec = pltpu.PrefetchScalarGridSpec(
        num_scalar_prefetch=n_pre, grid=grid,
        in_specs=[a_spec, b_spec] + [s for _, s in extras] + [pl.BlockSpec(memory_space=pl.ANY)] * n_dep,
        out_specs=[s for _, s in outs],
        scratch_shapes=[pltpu.VMEM((tm, tn), F32)] if nk > 1 else [])
    aliases = {}
    if alias_dep_to_out is not None:
        aliases = {n_pre + 2 + n_ex + alias_dep_to_out[0]: alias_dep_to_out[1]}
    res = pl.pallas_call(
        body, name=name, grid_spec=grid_spec, out_shape=[o for o, _ in outs], input_output_aliases=aliases,
        compiler_params=_params(3),
    )(*prefetch, a, b, *[e for e, _ in extras], *deps)
    return res


def _mn(tm, tn, col_off=0):
    return pl.BlockSpec((tm, tn), lambda i, j, k, *p: (i, j + col_off))


def _row(tn):
    return pl.BlockSpec((1, tn), lambda i, j, k, *p: (0, j))


def _ep_store(acc, ex, out, i):
    out[0][...] = acc.astype(out[0].dtype)


def _ep_resid_norm(acc, ex, out, i):
    x1 = ex[0][...] + acc
    out[0][...] = x1
    rstd = lax.rsqrt(jnp.mean(x1 * x1, axis=-1, keepdims=True) + EPS)
    out[1][...] = (x1 * rstd * ex[1][...]).astype(out[1].dtype)


def _ep_up(acc, ex, out, i):
    out[0][...] = acc.astype(out[0].dtype)
    r = jnp.maximum(acc, 0.0)
    out[1][...] = (r * r).astype(out[1].dtype)


def _ep_down_loss(acc, ex, out, i, inv_d):
    diff = (ex[0][...] + acc) - ex[1][...]
    dx2 = diff * inv_d
    out[0][...] = dx2
    out[1][...] = dx2.astype(out[1].dtype)

    @pl.when(i == 0)
    def _():
        out[2][...] = jnp.zeros_like(out[2])

    out[2][...] += jnp.sum(diff * diff, axis=0, keepdims=True)


def _ep_dh(acc, ex, out, i):
    h = ex[0][...].astype(F32)
    out[0][...] = (acc * (2.0 * jnp.maximum(h, 0.0))).astype(out[0].dtype)


def _ep_rms_bwd(acc, ex, out, i):
    x = ex[0][...]
    g = ex[1][...]
    rstd = lax.rsqrt(jnp.mean(x * x, axis=-1, keepdims=True) + EPS)
    xh = x * rstd
    dxh = acc * g
    dx = ex[2][...] + rstd * (dxh - xh * jnp.mean(dxh * xh, axis=-1, keepdims=True))
    out[0][...] = dx
    for copy in out[1:-1]:
        copy[...] = dx.astype(copy.dtype)
    dg = out[-1]

    @pl.when(i == 0)
    def _():
        dg[...] = jnp.zeros_like(dg)

    dg[...] += jnp.sum(acc * xh, axis=0, keepdims=True)


def _ep_gates(acc, ex, out, i):
    sa = _sigmoid(ex[0][...].astype(F32))
    sb = _sigmoid(ex[1][...].astype(F32))
    dpa = acc * sa
    dpb = acc * sb
    out[0][...] = dpa.astype(out[0].dtype)
    out[1][...] = dpb.astype(out[1].dtype)
    out[2][...] = (dpa * ex[2][...].astype(F32) * (1.0 - sa)).astype(out[2].dtype)
    out[3][...] = (dpb * ex[3][...].astype(F32) * (1.0 - sb)).astype(out[3].dtype)


def _sds(shape, dtype):
    return jax.ShapeDtypeStruct(shape, dtype)


def _in_proj_mine(x, g, w_mine, chip, proj_sds, deps, tm=512):
    S, D = x.shape
    ns = w_mine.shape[1]
    deps = [d for d in deps if d is not None]

    def body(c_ref, x_ref, g_ref, w_ref, *rest):
        xn_ref, proj_ref = rest[len(deps)], rest[len(deps) + 1]
        xv = x_ref[...]
        rstd = lax.rsqrt(jnp.mean(xv * xv, axis=-1, keepdims=True) + EPS)
        xn = (xv * rstd * g_ref[...]).astype(xn_ref.dtype)
        xn_ref[...] = xn
        proj_ref[...] = _dot_nn(xn, w_ref[...]).astype(proj_ref.dtype)

    grid_spec = pltpu.PrefetchScalarGridSpec(
        num_scalar_prefetch=1, grid=(S // tm,),
        in_specs=[pl.BlockSpec((tm, D), lambda i, c: (i, 0)), pl.BlockSpec((1, D), lambda i, c: (0, 0)),
                  pl.BlockSpec((D, ns), lambda i, c: (0, 0))] + [pl.BlockSpec(memory_space=pl.ANY)] * len(deps),
        out_specs=[pl.BlockSpec((tm, D), lambda i, c: (i, 0)), pl.BlockSpec((tm, ns), lambda i, c: (i, c[0]))])
    return pl.pallas_call(body, name="in_proj_mine", grid_spec=grid_spec, out_shape=[_sds((S, D), CDT), proj_sds],
                          compiler_params=_params(1))(chip, x, g, w_mine, *deps)


def _rm_shape(S, d, width):
    return (S, width) if d == 1 else (d, S // d, width)


def _rm_spec(tm, d, width):
    if d == 1:
        return pl.BlockSpec((tm, width), lambda i: (i, 0))
    return pl.BlockSpec((d, tm // d, width), lambda i: (0, i, 0))


def _rm_put(dst_ref, cols, buf_ref, d):
    if d == 1:
        dst_ref[:, cols] = buf_ref[...].astype(dst_ref.dtype)
        return
    m = buf_ref.shape[0] // d
    for r in range(d):
        dst_ref[r, :, cols] = buf_ref[pl.ds(r, m, stride=d), :].astype(dst_ref.dtype)


def _rm_reader(buf_ref, src_ref, d):
    if d == 1:
        return lambda s, rows: src_ref[rows, s * HD:(s + 1) * HD].astype(F32)
    m = buf_ref.shape[1] // d
    for s in range(buf_ref.shape[0]):
        for r in range(d):
            buf_ref.at[s][pl.ds(r, m, stride=d), :] = src_ref[r, :, s * HD:(s + 1) * HD].astype(F32)
    return lambda s, rows: buf_ref.at[s][rows, :]


def _qknorm_fwd(proj, gqk, tm=1024):
    S = proj.shape[0]
    W = 2 * ATT_W
    dil = [d for _, d in ATT_GROUPS]

    def body(p_ref, g_ref, o0, o1, o2, buf):
        outs = (o0, o1, o2)
        for hd in range(3 * ATT_HEADS):
            which, head = hd // ATT_HEADS, hd % ATT_HEADS
            grp, slot = head // ATT_HPG, head % ATT_HPG
            cols = slice(hd * HD, (hd + 1) * HD)

            def chunk(rows, which=which, cols=cols):
                v = p_ref[rows, cols].astype(F32)
                if which < 2:
                    rstd = lax.rsqrt(jnp.mean(v * v, axis=-1, keepdims=True) + EPS)
                    v = v * rstd * g_ref[:, cols]
                buf[rows, :] = v

            chunk(slice(None))
            _rm_put(outs[grp], slice(which * GW + slot * HD, which * GW + (slot + 1) * HD), buf, dil[grp])

    return pl.pallas_call(
        body, name="qknorm_fwd", grid=(S // tm,),
        in_specs=[pl.BlockSpec((tm, 3 * ATT_W), lambda i: (i, 0)), pl.BlockSpec((1, W), lambda i: (0, 0))],
        out_specs=[_rm_spec(tm, d, 3 * GW) for d in dil],
        out_shape=[_sds(_rm_shape(S, d, 3 * GW), CDT) for d in dil],
        scratch_shapes=[pltpu.VMEM((tm, HD), F32)],
        compiler_params=_params(1))(proj, gqk)


def _qknorm_bwd(proj, gqk, dqs, dks, dvs, dproj, tm=512):
    S = proj.shape[0]
    W = 2 * ATT_W
    dil = [d for _, d in ATT_GROUPS]

    def body(p_ref, g_ref, *refs):
        ins = refs[0:9]
        o_ref, dg_ref = refs[10], refs[11]
        bufs = refs[12:21]
        i = pl.program_id(0)

        @pl.when(i == 0)
        def _():
            dg_ref[...] = jnp.zeros_like(dg_ref)

        nat = [_rm_reader(bufs[j], ins[j], dil[j % 3]) for j in range(9)]
        dq_get, dk_get, dv_get = nat[0:3], nat[3:6], nat[6:9]
        for hd in range(2 * ATT_HEADS):
            sl = slice(hd * HD, (hd + 1) * HD)
            head = hd % ATT_HEADS
            grp, slot = head // ATT_HPG, head % ATT_HPG
            get = (dq_get if hd < ATT_HEADS else dk_get)[grp]

            def chunk(rows, sl=sl, slot=slot, get=get):
                dn = get(slot, rows)
                v = p_ref[rows, sl].astype(F32)
                rstd = lax.rsqrt(jnp.mean(v * v, axis=-1, keepdims=True) + EPS)
                vh = v * rstd
                dg_ref[:, sl] += jnp.sum(dn * vh, axis=0, keepdims=True)
                dvh = dn * g_ref[:, sl]
                o_ref[rows, sl] = (rstd * (dvh - vh * jnp.mean(dvh * vh, axis=-1, keepdims=True))).astype(o_ref.dtype)

            chunk(slice(None))
        for head in range(ATT_HEADS):
            grp, slot = head // ATT_HPG, head % ATT_HPG
            o_ref[:, W + head * HD:W + (head + 1) * HD] = dv_get[grp](slot, slice(None)).astype(o_ref.dtype)

    return pl.pallas_call(
        body, name="qknorm_bwd", grid=(S // tm,),
        in_specs=[pl.BlockSpec((tm, W), lambda i: (i, 0)), pl.BlockSpec((1, W), lambda i: (0, 0))]
        + [_rm_spec(tm, d, GW) for d in dil] * 3 + [pl.BlockSpec(memory_space=pl.ANY)],
        out_specs=[pl.BlockSpec((tm, 3 * ATT_W), lambda i: (i, 0)), pl.BlockSpec((1, W), lambda i: (0, 0))],
        out_shape=[_sds(dproj.shape, dproj.dtype), _sds((1, W), F32)],
        scratch_shapes=[pltpu.VMEM((ATT_HPG, tm, HD), F32)] * 9,
        input_output_aliases={11: 0},
        compiler_params=_params(1))(proj, gqk, *dqs, *dks, *dvs, dproj)


def _att_mask(n):
    qi = lax.broadcasted_iota(jnp.int32, (BLK, 2 * BLK), 0)
    kj = lax.broadcasted_iota(jnp.int32, (BLK, 2 * BLK), 1)
    dist = BLK + qi - kj
    valid_all = (dist >= 0) & (dist <= BLK)
    return valid_all & ((kj >= BLK) | (n > 0)), valid_all, dist.astype(F32)


def _att_slopes(grp):
    return [2.0 ** (-8.0 * (grp * ATT_HPG + hh + 1) / ATT_HEADS) for hh in range(ATT_HPG)]


ATT_WIDTH = 8
ATT_BWD_SEGMENTS = 4


def _att_planes(d, width):
    return d if d > 1 else width


def _att_step_planes(d, width):
    return min(_att_planes(d, width), width)


def _att_3d(a, d, width):
    return a.reshape(width, a.shape[0] // width, a.shape[1]) if d == 1 else a


def _att_spec(R, row_fn, col=0):
    return pl.BlockSpec((R, BLK, GW), lambda r, n: (r, row_fn(n), col))


def _att_chains(R):
    return [(rr * ATT_HPG + hh, rr, slice(hh * HD, (hh + 1) * HD), hh) for rr in range(R) for hh in range(ATT_HPG)]


def _att_qkv_specs(R, nb):
    last = nb - 1

    def cur(n):
        return jnp.minimum(n, last)

    def prev(n):
        return jnp.maximum(jnp.minimum(n, last) - 1, 0)

    return [_att_spec(R, cur, 0), _att_spec(R, prev, 1), _att_spec(R, cur, 1), _att_spec(R, prev, 2),
            _att_spec(R, cur, 2), _att_spec(R, lambda n: last, 1), _att_spec(R, lambda n: last, 2)]


def _att_prev(seg, n, prev_ref, last_ref, rr, sl):
    t = prev_ref[rr, :, sl]
    if seg and rr > 0:
        t = jnp.where(n == 0, last_ref[rr - 1, :, sl], t)
    return t


def _att_fwd(grp, S, qkv):
    _, d = ATT_GROUPS[grp]
    seg = d == 1
    width = ATT_WIDTH
    P = _att_planes(d, width)
    L = S // P
    nb = L // BLK
    R = _att_step_planes(d, width)
    assert P % R == 0 and (not seg or P == R)
    slopes = _att_slopes(grp)
    scale = HD ** -0.5
    chains = _att_chains(R)

    def body(q_ref, kp_ref, kc_ref, vp_ref, vc_ref, kl_ref, vl_ref, o_ref, l_ref, s_buf, p_buf, den_buf):
        n = pl.program_id(1)
        valid, valid_all, distf = _att_mask(n)
        for c, rr, sl, hh in chains:
            k = jnp.concatenate([_att_prev(seg, n, kp_ref, kl_ref, rr, sl), kc_ref[rr, :, sl]], axis=0)
            s_buf[c] = _dot_nt(q_ref[rr, :, sl], k)
        for c, rr, sl, hh in chains:
            s = s_buf[c] * scale + (-slopes[hh] * d) * distf
            s = jnp.where(valid_all if seg and rr > 0 else valid, s, -1e30)
            m = jnp.max(s, axis=-1, keepdims=True)
            p = jnp.exp(s - m)
            den = jnp.sum(p, axis=-1, keepdims=True)
            p_buf[c] = p.astype(CDT)
            den_buf[c] = jnp.broadcast_to(den, (BLK, HD))
            l_ref[rr, :, sl] = jnp.broadcast_to(m + jnp.log(den), (BLK, HD))
        for c, rr, sl, hh in chains:
            v = jnp.concatenate([_att_prev(seg, n, vp_ref, vl_ref, rr, sl), vc_ref[rr, :, sl]], axis=0)
            o_ref[rr, :, sl] = _dot_nn(p_buf[c], v) / den_buf[c]

    out_spec = _att_spec(R, lambda n: n)
    n_ch = len(chains)
    q3 = _att_3d(qkv, d, width)
    o, l = pl.pallas_call(
        body, name="att_fwd_g%d" % grp, grid=(P // R, nb),
        in_specs=_att_qkv_specs(R, nb),
        out_specs=[out_spec, out_spec],
        out_shape=[_sds((P, L, GW), F32)] * 2,
        scratch_shapes=[pltpu.VMEM((n_ch, BLK, 2 * BLK), F32), pltpu.VMEM((n_ch, BLK, 2 * BLK), CDT),
                        pltpu.VMEM((n_ch, BLK, HD), F32)],
        compiler_params=_params(2),
    )(*[q3] * 7)
    return o.reshape(_rm_shape(S, d, GW)), l.reshape(_rm_shape(S, d, GW))


def _att_bwd(grp, S, qkv, lse, do_g, c_g):
    _, d = ATT_GROUPS[grp]
    seg = d == 1
    width = ATT_BWD_SEGMENTS if seg else ATT_WIDTH
    P = _att_planes(d, width)
    L = S // P
    nb = L // BLK
    R = _att_step_planes(d, width)
    assert P % R == 0 and (not seg or P == R)
    slopes = _att_slopes(grp)
    scale = HD ** -0.5
    last = nb - 1
    chains = _att_chains(R)

    def body(q_ref, kp_ref, kc_ref, vp_ref, vc_ref, kl_ref, vl_ref, l_ref, do_ref, c_ref, dq_ref, dk_ref, dv_ref,
             ck, cv, fk, fv, s_buf, dp_buf, p_buf, ds_buf):
        n = pl.program_id(1)

        @pl.when(n == 0)
        def _():
            for buf in (ck, cv, fk, fv):
                buf[...] = jnp.zeros_like(buf)

        @pl.when(n < nb)
        def _():
            valid, valid_all, distf = _att_mask(n)
            for c, rr, sl, hh in chains:
                k = jnp.concatenate([_att_prev(seg, n, kp_ref, kl_ref, rr, sl), kc_ref[rr, :, sl]], axis=0)
                v = jnp.concatenate([_att_prev(seg, n, vp_ref, vl_ref, rr, sl), vc_ref[rr, :, sl]], axis=0)
                s_buf[c] = _dot_nt(q_ref[rr, :, sl], k)
                dp_buf[c] = _dot_nt(do_ref[rr, :, sl], v)
            for c, rr, sl, hh in chains:
                s = s_buf[c] * scale + (-slopes[hh] * d) * distf
                p = jnp.where(valid_all if seg and rr > 0 else valid, jnp.exp(s - l_ref[rr, :, sl][:, 0:1]), 0.0)
                p_buf[c] = p.astype(CDT)
                ds_buf[c] = (p * (dp_buf[c] + c_ref[rr, :, sl][:, 0:1]) * scale).astype(CDT)
            for c, rr, sl, hh in chains:
                k = jnp.concatenate([_att_prev(seg, n, kp_ref, kl_ref, rr, sl), kc_ref[rr, :, sl]], axis=0)
                ds = ds_buf[c]
                dq_ref[rr, :, sl] = _dot_nn(ds, k)
                dk = _dot_tn(ds, q_ref[rr, :, sl])
                dv = _dot_tn(p_buf[c], do_ref[rr, :, sl])
                dk_ref[rr, :, sl] = ck[rr, :, sl] + dk[0:BLK]
                dv_ref[rr, :, sl] = cv[rr, :, sl] + dv[0:BLK]
                ck[rr, :, sl] = dk[BLK:2 * BLK]
                cv[rr, :, sl] = dv[BLK:2 * BLK]
                if seg and rr > 0:
                    @pl.when(n == 0)
                    def _(rr=rr, sl=sl, dk=dk, dv=dv):
                        fk[rr - 1, :, sl] = dk[0:BLK]
                        fv[rr - 1, :, sl] = dv[0:BLK]

        @pl.when(n == nb)
        def _():
            dk_ref[...] = ck[...] + fk[...]
            dv_ref[...] = cv[...] + fv[...]

    blk = (R, BLK, GW)
    at_q = _att_spec(R, lambda n: jnp.minimum(n, last))
    behind = _att_spec(R, lambda n: jnp.maximum(n - 1, 0))
    n_ch = len(chains)
    q3 = _att_3d(qkv, d, width)
    res = pl.pallas_call(
        body, name="att_bwd_g%d" % grp, grid=(P // R, nb + 1),
        in_specs=_att_qkv_specs(R, nb) + [at_q, at_q, at_q],
        out_specs=[at_q, behind, behind],
        out_shape=[_sds((P, L, GW), F32)] * 3,
        scratch_shapes=[pltpu.VMEM(blk, F32)] * 4
        + [pltpu.VMEM((n_ch, BLK, 2 * BLK), F32), pltpu.VMEM((n_ch, BLK, 2 * BLK), F32),
           pltpu.VMEM((n_ch, BLK, 2 * BLK), CDT), pltpu.VMEM((n_ch, BLK, 2 * BLK), CDT)],
        compiler_params=_params(2),
    )(*[q3] * 7, _att_3d(lse, d, width), _att_3d(do_g, d, width), _att_3d(c_g, d, width))
    return [t.reshape(_rm_shape(S, d, GW)) for t in res]


def _mix_alpha(l0, l1, l2):
    mx = jnp.maximum(jnp.maximum(l0, l1), l2)
    e = [jnp.exp(l0 - mx), jnp.exp(l1 - mx), jnp.exp(l2 - mx)]
    tot = e[0] + e[1] + e[2]
    return [ei / tot for ei in e]


def _mix_fwd(S, os_, ls_, tm=512):
    dil = [d for _, d in ATT_GROUPS]

    def body(*refs):
        out, bufs = refs[6], refs[7:13]
        get = [_rm_reader(bufs[j], refs[j], dil[j % 3]) for j in range(6)]
        rows = slice(None)
        for s in range(ATT_HPG):
            al = _mix_alpha(*[get[3 + g](s, rows) for g in range(3)])
            mixed = al[0] * get[0](s, rows) + al[1] * get[1](s, rows) + al[2] * get[2](s, rows)
            out[:, s * HD:(s + 1) * HD] = mixed.astype(out.dtype)

    specs = [_rm_spec(tm, d, GW) for d in dil]
    return pl.pallas_call(
        body, name="mix_fwd", grid=(S // tm,), in_specs=specs * 2, out_specs=pl.BlockSpec((tm, GW), lambda i: (i, 0)),
        out_shape=_sds((S, GW), CDT), scratch_shapes=[pltpu.VMEM((ATT_HPG, tm, HD), F32)] * 6,
        compiler_params=_params(1))(*os_, *ls_)


def _mix_bwd(S, os_, ls_, do_a, tm=512):
    dil = [d for _, d in ATT_GROUPS]

    def body(*refs):
        d_ref, outs, bufs, tmps = refs[6], refs[7:13], refs[13:19], refs[19:25]
        get = [_rm_reader(bufs[j], refs[j], dil[j % 3]) for j in range(6)]
        for s in range(ATT_HPG):
            cols = slice(s * HD, (s + 1) * HD)

            def chunk(rows, s=s, cols=cols):
                al = _mix_alpha(*[get[3 + g](s, rows) for g in range(3)])
                dv = d_ref[rows, cols]
                o_a = al[0] * get[0](s, rows) + al[1] * get[1](s, rows) + al[2] * get[2](s, rows)
                dsum = jnp.sum(dv * o_a, axis=-1, keepdims=True)
                for g in range(3):
                    tmps[g][rows, :] = al[g] * dv
                    tmps[3 + g][rows, :] = -(al[g] * dsum)

            chunk(slice(None))
            for j in range(6):
                _rm_put(outs[j], cols, tmps[j], dil[j % 3])

    specs = [_rm_spec(tm, d, GW) for d in dil]
    res = pl.pallas_call(
        body, name="mix_bwd", grid=(S // tm,), in_specs=specs * 2 + [pl.BlockSpec((tm, GW), lambda i: (i, 0))],
        out_specs=specs * 2,
        out_shape=[_sds(_rm_shape(S, d, GW), CDT) for d in dil] + [_sds(_rm_shape(S, d, GW), F32) for d in dil],
        scratch_shapes=[pltpu.VMEM((ATT_HPG, tm, HD), F32)] * 6 + [pltpu.VMEM((tm, HD), F32)] * 6,
        compiler_params=_params(1))(*os_, *ls_, do_a)
    return res[:3], res[3:]


RET_FWD_CHUNKS = 2


def _ret_tables(dk):
    H, C = RET_HEADS, BLK
    log_g = jnp.log(1.0 - 2.0 ** (-5.0 - jnp.arange(H, dtype=F32)))
    idx = jnp.arange(C, dtype=F32)
    diff = idx[:, None] - idx[None, :]
    decay = jnp.where(diff >= 0, jnp.exp(log_g[:, None, None] * jnp.maximum(diff, 0.0)), 0.0)
    xi = jnp.exp(log_g[:, None] * (idx[None, :] + 1.0))
    zeta = jnp.exp(log_g[:, None] * (C - 1.0 - idx[None, :])) * (dk ** -0.5)
    g_chunk = jnp.exp(log_g * C)
    bc = lambda t: jnp.broadcast_to(t[:, :, None], (H, C, C))
    return decay, bc(xi), bc(zeta), jnp.broadcast_to(g_chunk[:, None, None], (H, 8, C))


def _gn_fwd(o, g, b):
    mu = jnp.mean(o, axis=-1, keepdims=True)
    xc = o - mu
    rstd = lax.rsqrt(jnp.mean(xc * xc, axis=-1, keepdims=True) + EPS)
    yh = xc * rstd
    return yh, rstd, yh * g + b


def _ret_specs(dk, dv, order, rows=BLK):
    H = RET_HEADS
    qk_w, v_w = H * dk, H * dv
    off_q = 3 * ATT_W
    off_k, off_v, off_g = off_q + qk_w, off_q + 2 * qk_w, off_q + 2 * qk_w + v_w
    assert 2 * dk == dv and all(off % dv == 0 for off in (off_q, off_k, off_v, off_g))

    def col(off, j):
        return pl.BlockSpec((rows, dv), lambda i: (order(i), off // dv + j))

    tab = pl.BlockSpec((H, BLK, BLK), lambda i: (0, 0, 0))
    return ([col(off_q, j) for j in range(H // 2)] + [col(off_k, j) for j in range(H // 2)]
            + [col(off_v, j) for j in range(H)] + [col(off_g, j) for j in range(H)]
            + [tab, tab, tab, pl.BlockSpec((H, 8, BLK), lambda i: (0, 0, 0))])


def _ret_heads(refs, dk):
    H = RET_HEADS
    q_refs, k_refs = refs[0:H // 2], refs[H // 2:H]
    v_refs, gr_refs = refs[H:2 * H], refs[2 * H:3 * H]

    def head(h, rows=slice(None)):
        cols = slice((h % 2) * dk, (h % 2 + 1) * dk)
        return q_refs[h // 2][rows, cols], k_refs[h // 2][rows, cols], v_refs[h][rows, :], gr_refs[h][rows, :]

    return head, refs[3 * H:3 * H + 4]


def _ret_fwd(proj, gn_g, gn_b, dk, dv):
    S = proj.shape[0]
    N = S // BLK
    H = RET_HEADS
    per = RET_FWD_CHUNKS
    assert N % per == 0
    kscale = dk ** -0.5
    n_in = 3 * H + 4

    def body(*refs):
        head, (dec_ref, xi_ref, zeta_ref, gc_ref) = _ret_heads(refs, dk)
        g_ref, b_ref, opre_ref, or_ref, st_ref, state, s_buf, cross_buf = refs[n_in:n_in + 8]
        n = pl.program_id(0)

        @pl.when(n == 0)
        def _():
            state[...] = jnp.zeros_like(state)

        for c in range(per):
            rows = slice(c * BLK, (c + 1) * BLK)
            for h in range(H):
                q, k, v, _ = head(h, rows)
                s_buf[h] = _dot_nt(q, k)
                st = state[h]
                st_c = st.astype(CDT)
                st_ref[c, h] = st_c
                cross_buf[h] = _dot_nn(q, st_c)
                kz = (k.astype(F32) * zeta_ref[h][:, 0:1]).astype(CDT)
                state[h] = st * gc_ref[h][0:1, 0:1] + _dot_tn(kz, v)
            for h in range(H):
                vs = slice(h * dv, (h + 1) * dv)
                _, _, v, gr = head(h, rows)
                s = s_buf[h] * kscale * dec_ref[h]
                o = _dot_nn(s.astype(CDT), v) + cross_buf[h] * xi_ref[h][:, 0:1]
                opre_ref[rows, vs] = o
                _, _, y = _gn_fwd(o, g_ref[:, vs], b_ref[:, vs])
                gr = gr.astype(F32)
                or_ref[rows, vs] = (y * (gr * _sigmoid(gr))).astype(or_ref.dtype)

    v_w = H * dv
    row = pl.BlockSpec((1, v_w), lambda i: (0, 0))
    tile = pl.BlockSpec((per * BLK, v_w), lambda i: (i, 0))
    return pl.pallas_call(
        body, name="ret_fwd", grid=(N // per,),
        in_specs=_ret_specs(dk, dv, lambda i: i, per * BLK) + [row, row],
        out_specs=[tile, tile, pl.BlockSpec((per, H, dk, dv), lambda i: (i, 0, 0, 0))],
        out_shape=[_sds((S, v_w), F32), _sds((S, v_w), CDT), _sds((N, H, dk, dv), CDT)],
        scratch_shapes=[pltpu.VMEM((H, dk, dv), F32), pltpu.VMEM((H, BLK, BLK), F32), pltpu.VMEM((H, BLK, dv), F32)],
        compiler_params=_params(1),
    )(*[proj] * (3 * H), *_ret_tables(dk), gn_g, gn_b)


def _ret_bwd(proj, gn_g, gn_b, o_pre, states, d_or, dga, dgb, dk, dv):
    S, in_w = proj.shape
    N = S // BLK
    H = RET_HEADS
    qk_w, v_w = H * dk, H * dv
    kscale = dk ** -0.5
    n_in = 3 * H + 4
    out_w = 2 * qk_w + 2 * v_w
    gate_w = dga.shape[1]
    col0 = 3 * ATT_W
    assert col0 + out_w + 2 * gate_w == in_w
    rev = lambda i: N - 1 - i

    def body(*refs):
        head, (dec_ref, xi_ref, zeta_ref, gc_ref) = _ret_heads(refs, dk)
        (g_ref, b_ref, opre_ref, st_ref, dor_ref, dga_ref, dgb_ref, dproj_ref, dg_ref, db_ref, dstate, stage,
         sem, do_buf, dox_buf, a_buf, g_buf, dq_buf, dk_buf, dv_buf) = refs[n_in:n_in + 20]
        i = pl.program_id(0)
        slot = i % 2
        out_ref = stage.at[slot]

        def out_copy(s, step):
            rows = pl.ds(pl.multiple_of(rev(step) * BLK, BLK), BLK)
            return pltpu.make_async_copy(stage.at[s], dproj_ref.at[rows, pl.ds(col0, in_w - col0)], sem.at[s])

        @pl.when(i >= 2)
        def _():
            out_copy(slot, i - 2).wait()

        @pl.when(i == 0)
        def _():
            dstate[...] = jnp.zeros_like(dstate)
            dg_ref[...] = jnp.zeros_like(dg_ref)
            db_ref[...] = jnp.zeros_like(db_ref)

        out_ref[:, out_w:out_w + gate_w] = dga_ref[...]
        out_ref[:, out_w + gate_w:out_w + 2 * gate_w] = dgb_ref[...]
        for h in range(H):
            vs = slice(h * dv, (h + 1) * dv)
            _, _, _, gr = head(h)
            gr = gr.astype(F32)
            sg = _sigmoid(gr)
            gain = g_ref[:, vs]
            yh, rstd, y = _gn_fwd(opre_ref[:, vs], gain, b_ref[:, vs])
            d_or_v = dor_ref[:, vs]
            dy = d_or_v * (gr * sg)
            out_ref[:, 2 * qk_w + v_w + h * dv:2 * qk_w + v_w + (h + 1) * dv] = (
                d_or_v * y * (sg * (1.0 + gr * (1.0 - sg)))).astype(out_ref.dtype)
            dg_ref[:, vs] += jnp.sum(dy * yh, axis=0, keepdims=True)
            db_ref[:, vs] += jnp.sum(dy, axis=0, keepdims=True)
            dyh = dy * gain
            do = rstd * (dyh - jnp.mean(dyh, axis=-1, keepdims=True)
                         - yh * jnp.mean(dyh * yh, axis=-1, keepdims=True))
            do_buf[h] = do.astype(CDT)
            dox_buf[h] = (do * xi_ref[h][:, 0:1]).astype(CDT)
        for h in range(H):
            q, k, v, _ = head(h)
            dox = dox_buf[h]
            a_buf[h] = _dot_nt(q, k)
            g_buf[h] = _dot_nt(do_buf[h], v)
            dsn = dstate[h]
            dsn_c = dsn.astype(CDT)
            kz = (k.astype(F32) * zeta_ref[h][:, 0:1]).astype(CDT)
            dq_buf[h] = _dot_nt(dox, st_ref[h])
            dk_buf[h] = _dot_nt(v, dsn_c)
            dv_buf[h] = _dot_nn(kz, dsn_c)
            dstate[h] = dsn * gc_ref[h][0:1, 0:1] + _dot_tn(q, dox)
        for h in range(H):
            q, k, _, _ = head(h)
            decay = dec_ref[h]
            a_c = (a_buf[h] * kscale * decay).astype(CDT)
            g_c = (g_buf[h] * decay).astype(CDT)
            dq = _dot_nn(g_c, k) * kscale + dq_buf[h]
            dkk = _dot_tn(g_c, q) * kscale + dk_buf[h] * zeta_ref[h][:, 0:1]
            dvv = _dot_tn(a_c, do_buf[h]) + dv_buf[h]
            out_ref[:, h * dk:(h + 1) * dk] = dq.astype(out_ref.dtype)
            out_ref[:, qk_w + h * dk:qk_w + (h + 1) * dk] = dkk.astype(out_ref.dtype)
            out_ref[:, 2 * qk_w + h * dv:2 * qk_w + (h + 1) * dv] = dvv.astype(out_ref.dtype)

        cp = out_copy(slot, i)
        cp.start()

        @pl.when(i == N - 1)
        def _():
            cp.wait()
            if N >= 2:
                out_copy(1 - slot, i - 1).wait()

    row = pl.BlockSpec((1, v_w), lambda i: (0, 0))
    tile = pl.BlockSpec((BLK, v_w), lambda i: (rev(i), 0))
    gate = pl.BlockSpec((BLK, gate_w), lambda i: (rev(i), 0))
    return pl.pallas_call(
        body, name="ret_bwd", grid=(N,),
        in_specs=_ret_specs(dk, dv, rev) + [row, row, tile,
                 pl.BlockSpec((None, H, dk, dv), lambda i: (rev(i), 0, 0, 0)), tile, gate, gate],
        out_specs=[pl.BlockSpec(memory_space=pl.ANY), row, row],
        out_shape=[_sds((S, in_w), CDT), _sds((1, v_w), F32), _sds((1, v_w), F32)],
        scratch_shapes=[pltpu.VMEM((H, dk, dv), F32), pltpu.VMEM((2, BLK, in_w - col0), CDT),
                        pltpu.SemaphoreType.DMA((2,)),
                        pltpu.VMEM((H, BLK, dv), CDT), pltpu.VMEM((H, BLK, dv), CDT),
                        pltpu.VMEM((H, BLK, BLK), F32), pltpu.VMEM((H, BLK, BLK), F32),
                        pltpu.VMEM((H, BLK, dk), F32), pltpu.VMEM((H, BLK, dk), F32), pltpu.VMEM((H, BLK, dv), F32)],
        compiler_params=_params(1),
    )(*[proj] * (3 * H), *_ret_tables(dk), gn_g, gn_b, o_pre, states, d_or, dga, dgb)


def _merge_fwd(o_a, o_r, wa, wb, proj, d_model, tm=1024, tn=512):
    S, in_w = proj.shape
    off_a, off_b = in_w - 2 * d_model, in_w - d_model
    assert off_a % tn == 0 and off_b % tn == 0

    def body(oa_ref, or_ref, wa_ref, wb_ref, ga_ref, gb_ref, y_ref, pa_ref, pb_ref):
        pa = _dot_nn(oa_ref[...], wa_ref[...])
        pb = _dot_nn(or_ref[...], wb_ref[...])
        y = _sigmoid(ga_ref[...].astype(F32)) * pa + _sigmoid(gb_ref[...].astype(F32)) * pb
        y_ref[...] = y.astype(y_ref.dtype)
        pa_ref[...] = pa.astype(pa_ref.dtype)
        pb_ref[...] = pb.astype(pb_ref.dtype)

    ka, kb = o_a.shape[1], o_r.shape[1]
    out = pl.BlockSpec((tm, tn), lambda i, j: (i, j))
    return pl.pallas_call(
        body, name="merge_fwd", grid=(S // tm, d_model // tn),
        in_specs=[pl.BlockSpec((tm, ka), lambda i, j: (i, 0)), pl.BlockSpec((tm, kb), lambda i, j: (i, 0)),
                  pl.BlockSpec((ka, tn), lambda i, j: (0, j)), pl.BlockSpec((kb, tn), lambda i, j: (0, j)),
                  pl.BlockSpec((tm, tn), lambda i, j: (i, off_a // tn + j)),
                  pl.BlockSpec((tm, tn), lambda i, j: (i, off_b // tn + j))],
        out_specs=[out, out, out], out_shape=[_sds((S, d_model), CDT)] * 3,
        compiler_params=_params(2))(o_a, o_r, wa, wb, proj, proj)


def _d_x(dproj, w_in, x, g, dx1, dep, tm=512, row_groups=2):
    S, D = x.shape
    n_sh, _, ns = w_in.shape
    nt = S // tm // row_groups
    deps = [d for d in (dep,) if d is not None]

    def body(a_ref, b_ref, x_ref, g_ref, r_ref, *rest):
        dx_ref, dg_ref, acc = rest[len(deps):]
        h, k, i = pl.program_id(0), pl.program_id(1), pl.program_id(2)

        @pl.when(k == 0)
        def _():
            acc[i] = jnp.zeros((tm, D), F32)

        acc[i] += _dot_nt(a_ref[...], b_ref[...])

        @pl.when(k == n_sh - 1)
        def _():
            _ep_rms_bwd(acc[i], (x_ref, g_ref, r_ref), (dx_ref, dg_ref), h * nt + i)

    def last_only(h, k, i):
        return (h * nt + jnp.where(k == n_sh - 1, i, 0), 0)

    return pl.pallas_call(
        body, name="d_x", grid=(row_groups, n_sh, nt),
        in_specs=[pl.BlockSpec((tm, ns), lambda h, k, i: (h * nt + i, k)),
                  pl.BlockSpec((None, D, ns), lambda h, k, i: (k, 0, 0)),
                  pl.BlockSpec((tm, D), last_only), pl.BlockSpec((1, D), lambda h, k, i: (0, 0)),
                  pl.BlockSpec((tm, D), last_only)] + [pl.BlockSpec(memory_space=pl.ANY)] * len(deps),
        out_specs=[pl.BlockSpec((tm, D), last_only), pl.BlockSpec((1, D), lambda h, k, i: (0, 0))],
        out_shape=[_sds((S, D), F32), _sds((1, D), F32)],
        scratch_shapes=[pltpu.VMEM((nt, tm, D), F32)],
        compiler_params=_params(3))(dproj, w_in, x, g, dx1, *deps)


def _local_step(x, target, w_in_mine, chip, others, near_w_in, far_w_in, small, late_weights, on_grads, deps0=()):
    S, D = x.shape
    ns_in = w_in_mine.shape[1]
    in_w = N_CHIPS * ns_in
    d_ff = 4 * D
    ret_v_w = 2 * D
    dv = ret_v_w // RET_HEADS
    dk = (in_w - 3 * ATT_W - 2 * ret_v_w - 2 * D) // (2 * RET_HEADS)
    gqk = jnp.concatenate([small["q_norm_g"].reshape(1, ATT_W), small["k_norm_g"].reshape(1, ATT_W)], axis=1)
    g1, g2 = small["norm1_g"], small["norm2_g"]
    gn_g, gn_b = small["ret_gn_g"], small["ret_gn_b"]

    proj_sds = _sds((S, in_w), CDT)
    xn, proj = _in_proj_mine(x, g1, w_in_mine, chip, proj_sds, deps0)
    for stage, (get_w_in, chips) in enumerate(((near_w_in, others[:2]), (far_w_in, others[2:]))):
        w_in, started = get_w_in(proj)
        (proj,) = _matmul(
            "in_proj_far%d" % stage, "nn", xn, w_in, tm=1024, tn=ns_in, tk=D, prefetch=[chips],
            n_cols=chips.shape[0] * ns_in, b_spec=pl.BlockSpec((None, D, ns_in), lambda i, j, k, o: (o[j], 0, 0)),
            outs=[(proj_sds, pl.BlockSpec((1024, ns_in), lambda i, j, k, o: (i, o[j])))], epilogue=_ep_store,
            deps=[proj, started], alias_dep_to_out=(0, 0), j_outer=True)
    qkv = _qknorm_fwd(proj, gqk)
    att = [_att_fwd(g, S, qkv[g]) for g in range(3)]
    os_, ls_ = [a[0] for a in att], [a[1] for a in att]
    o_a = _mix_fwd(S, os_, ls_)
    o_pre, o_r, states = _ret_fwd(proj, gn_g, gn_b, dk, dv)
    w = late_weights(o_r)
    y, pa, pb = _merge_fwd(o_a, o_r, w["w_proj_a"], w["w_proj_b"], proj, D)
    x1, xn2 = _matmul("out_proj", "nn", y, w["w_out"], tm=1024, tn=D, tk=D,
                      extras=[(x, _mn(1024, D)), (g2, _row(D))],
                      outs=[(_sds((S, D), F32), _mn(1024, D)), (_sds((S, D), CDT), _mn(1024, D))],
                      epilogue=_ep_resid_norm)
    hid, act = _matmul("mlp_up", "nn", xn2, w["w_up"], tm=1024, tn=2048, tk=D, j_outer=True,
                       outs=[(_sds((S, d_ff), CDT), _mn(1024, 2048))] * 2, epilogue=_ep_up)
    dx2, dx2c, loss_row = _matmul(
        "mlp_down_loss", "nn", act, w["w_down"], tm=512, tn=D, tk=d_ff,
        extras=[(x1, _mn(512, D)), (target, _mn(512, D))],
        outs=[(_sds((S, D), F32), _mn(512, D)), (_sds((S, D), CDT), _mn(512, D)), (_sds((1, D), F32), _row(D))],
        epilogue=functools.partial(_ep_down_loss, inv_d=1.0 / D))
    loss = 0.5 * jnp.sum(loss_row) / D

    (dh,) = _matmul("d_hidden", "nt", dx2c, w["w_down"], tm=1024, tn=2048, tk=D, j_outer=True,
                    extras=[(hid, _mn(1024, 2048))], outs=[(_sds((S, d_ff), CDT), _mn(1024, 2048))],
                    epilogue=_ep_dh)
    (gw_down,) = _matmul("dw_down", "tn", act, dx2c, tm=1024, tn=D, tk=2048,
                         outs=[(_sds((d_ff, D), F32), _mn(1024, D))], epilogue=_ep_store)
    (gw_up,) = _matmul("dw_up", "tn", xn2, dh, tm=D, tn=1024, tk=2048,
                       outs=[(_sds((D, d_ff), F32), _mn(D, 1024))], epilogue=_ep_store)
    tok = on_grads({"w_down": gw_down, "w_up": gw_up})
    dx1, dx1c, dg2 = _matmul(
        "d_x1", "nt", dh, w["w_up"], tm=512, tn=D, tk=d_ff,
        extras=[(x1, _mn(512, D)), (g2, _row(D)), (dx2, _mn(512, D))],
        outs=[(_sds((S, D), F32), _mn(512, D)), (_sds((S, D), CDT), _mn(512, D)), (_sds((1, D), F32), _row(D))],
        epilogue=_ep_rms_bwd, deps=[tok])

    gt = 512
    assert (in_w - 2 * D) % gt == 0
    off_a, off_b = (in_w - 2 * D) // gt, (in_w - D) // gt
    dpa, dpb, dga, dgb = _matmul(
        "d_gates", "nt", dx1c, w["w_out"], tm=1024, tn=gt, tk=D,
        extras=[(proj, _mn(1024, gt, off_a)), (proj, _mn(1024, gt, off_b)), (pa, _mn(1024, gt)),
                (pb, _mn(1024, gt))],
        outs=[(_sds((S, D), CDT), _mn(1024, gt))] * 4, epilogue=_ep_gates)
    (gw_out,) = _matmul("dw_out", "tn", y, dx1c, tm=D, tn=D, tk=1024,
                        outs=[(_sds((D, D), F32), _mn(D, D))], epilogue=_ep_store)
    (gw_pa,) = _matmul("dw_proj_a", "tn", o_a, dpa, tm=GW, tn=D, tk=1024,
                       outs=[(_sds((GW, D), F32), _mn(GW, D))], epilogue=_ep_store)
    (gw_pb,) = _matmul("dw_proj_b", "tn", o_r, dpb, tm=1024, tn=D, tk=2048,
                       outs=[(_sds((ret_v_w, D), F32), _mn(1024, D))], epilogue=_ep_store)
    (do_a,) = _matmul("d_o_a", "nt", dpa, w["w_proj_a"], tm=1024, tn=GW, tk=D,
                      outs=[(_sds((S, GW), F32), _mn(1024, GW))], epilogue=_ep_store)
    tok = on_grads({"w_out": gw_out, "w_proj_a": gw_pa, "w_proj_b": gw_pb})
    (d_or,) = _matmul("d_o_r", "nt", dpb, w["w_proj_b"], tm=1024, tn=ret_v_w, tk=D,
                      outs=[(_sds((S, ret_v_w), F32), _mn(1024, ret_v_w))], epilogue=_ep_store, deps=[tok])

    dproj, dgn_g, dgn_b = _ret_bwd(proj, gn_g, gn_b, o_pre, states, d_or, dga, dgb, dk, dv)
    do_gs, c_gs = _mix_bwd(S, os_, ls_, do_a)
    datt_parts = [_att_bwd(g, S, qkv[g], ls_[g], do_gs[g], c_gs[g]) for g in range(3)]
    dproj, dgqk = _qknorm_bwd(proj, gqk, [p[0] for p in datt_parts], [p[1] for p in datt_parts],
                              [p[2] for p in datt_parts], dproj)

    (gw_in,) = _matmul(
        "dw_in", "tn", xn, dproj, tm=512, tn=ns_in, tk=1024,
        outs=[(_sds((N_CHIPS, D, ns_in), F32), pl.BlockSpec((None, 512, ns_in), lambda i, j, k: (j, i, 0)))],
        epilogue=_ep_store)
    tok = on_grads({"w_in": gw_in})
    grad_x, dg1 = _d_x(dproj, w_in, x, g1, dx1, tok)

    smallg = {"norm1_g": dg1, "q_norm_g": dgqk[:, :ATT_W], "k_norm_g": dgqk[:, ATT_W:],
              "ret_gn_g": dgn_g, "ret_gn_b": dgn_b, "norm2_g": dg2}
    return loss, grad_x, smallg


N_CHIPS = 4
N_DEV = 8


def _place():
    x, y, c = lax.axis_index("x"), lax.axis_index("y"), lax.axis_index("c")
    return x, y, c


def _other_chips(x, y):
    out = []
    for fx, fy in ((1, 0), (0, 1), (1, 1)):
        px = 1 - x if fx else x
        py = 1 - y if fy else y
        out.append(((px, py), 2 * px + py))
    return out


SEM_SPEC = pl.BlockSpec(memory_space=pltpu.SEMAPHORE)
ANY_SPEC = pl.BlockSpec(memory_space=pl.ANY)
EFFECT = pltpu.SideEffectType.DATAFLOW_SIDE_EFFECTING


def _ici_copies(kind, srcs, lands, send, recv, which=(0, 1, 2)):
    x, y, c = _place()
    me = 2 * x + y
    out = []
    for w, (s, l) in enumerate(zip(srcs, lands)):
        for j, ((px, py), pidx) in enumerate(_other_chips(x, y)):
            if j not in which:
                continue
            if kind == "gather":
                half = s.shape[0] // 2
                rows = pl.ds(c * half, half)
                src, dst_there, dst_here = s.at[rows, :], l.at[me, rows, :], l.at[pidx, rows, :]
            else:
                src, dst_there, dst_here = s.at[pidx], l.at[me], l.at[pidx]
            out.append((src, dst_there, dst_here, send.at[3 * w + j], recv.at[3 * w + j], (px, py, c)))
    return out


def _exchange_start(name, kind, srcs, land_shapes, which=(0, 1, 2), lands=None):
    n = len(srcs)
    if lands is None:
        lands = [lax.empty(shape, dtype) for shape, dtype in land_shapes]

    def body(*refs):
        src_refs, land_refs = refs[:n], refs[n:2 * n]
        send, recv = refs[2 * n], refs[2 * n + 1]
        token = refs[-1]
        for src, dst, _, ss, rs, dev in _ici_copies(kind, src_refs, land_refs, send, recv, which):
            pltpu.make_async_remote_copy(src_ref=src, dst_ref=dst, send_sem=ss, recv_sem=rs, device_id=dev,
                                         device_id_type=MESH).start()
        token[...] = jnp.zeros_like(token)

    thru = [pltpu.HBM(s.shape, s.dtype) for s in srcs] + [pltpu.HBM(shape, dtype) for shape, dtype in land_shapes]
    res = pl.pallas_call(
        body, name=name,
        out_shape=(pltpu.SemaphoreType.DMA((3 * n,)), pltpu.SemaphoreType.DMA((3 * n,)), *thru, _sds((8, LANES), F32)),
        in_specs=[HBM_SPEC] * (2 * n), out_specs=(SEM_SPEC, SEM_SPEC, *[HBM_SPEC] * (2 * n), VMEM_SPEC),
        input_output_aliases={i: 2 + i for i in range(2 * n)},
        compiler_params=pltpu.CompilerParams(has_side_effects=EFFECT),
    )(*[pltpu.with_memory_space_constraint(s, pltpu.HBM) for s in srcs],
      *[pltpu.with_memory_space_constraint(l, pltpu.HBM) for l in lands])
    return res[0], res[1], list(res[2:2 + n]), list(res[2 + n:2 + 2 * n]), res[-1]


def _exchange_wait(name, kind, send, recv, srcs, lands, after, which=(0, 1, 2)):
    n = len(srcs)

    def body(*refs):
        src_refs, land_refs = refs[:n], refs[n:2 * n]
        send_ref, recv_ref = refs[2 * n], refs[2 * n + 1]
        for src, _, dst, ss, rs, dev in _ici_copies(kind, src_refs, land_refs, send_ref, recv_ref, which):
            cp = pltpu.make_async_remote_copy(src_ref=src, dst_ref=dst, send_sem=ss, recv_sem=rs, device_id=dev,
                                              device_id_type=MESH)
            cp.wait_send()
            cp.wait_recv()

    thru = [pltpu.HBM(t.shape, t.dtype) for t in list(srcs) + list(lands)]
    res = pl.pallas_call(
        body, name=name, out_shape=thru,
        in_specs=[HBM_SPEC] * (2 * n) + [SEM_SPEC, SEM_SPEC, ANY_SPEC], out_specs=[HBM_SPEC] * (2 * n),
        input_output_aliases={i: i for i in range(2 * n)},
        compiler_params=pltpu.CompilerParams(has_side_effects=EFFECT),
    )(*srcs, *lands, send, recv, after)
    return list(res[:n]), list(res[n:])


PAIR_TILE_ELEMS = 1 << 20


def _pair_fill(name, gathered, mine, core, others, chip, write_mine=True):
    k, r, C = gathered.shape
    half = r // 2
    tr = _row_tile(half, C, PAIR_TILE_ELEMS, mult=16)
    nt = half // tr
    n_far = others.shape[0]

    def body(c_ref, o_ref, chip_ref, in_ref, mine_ref, out_ref, slot, send, recv):
        j = pl.program_id(0)
        _sibling_barrier((j == 0) & (pl.program_id(1) == 0))
        b = (j * nt + pl.program_id(1)) % 2
        x, y, c = _place()
        cp = pltpu.make_async_remote_copy(src_ref=in_ref, dst_ref=slot.at[b], send_sem=send.at[b],
                                          recv_sem=recv.at[b], device_id=(x, y, 1 - c), device_id_type=MESH)

        @pl.when(j < n_far)
        def _():
            cp.start()
            cp.wait_recv()
            out_ref[...] = slot[b]
            cp.wait_send()

        @pl.when(j >= n_far)
        def _():
            out_ref[...] = mine_ref[...]

    def far(j):
        return jnp.minimum(j, n_far - 1)

    grid_spec = pltpu.PrefetchScalarGridSpec(
        num_scalar_prefetch=3, grid=(n_far + (2 if write_mine else 0), nt),
        in_specs=[pl.BlockSpec((tr, C), lambda j, i, c, o, m: (
                      (2 * o[far(j)] + c[0]) * nt + jnp.where(j < n_far, i, nt - 1), 0)),
                  pl.BlockSpec((tr, C), lambda j, i, c, o, m: (jnp.where(j < n_far, 0, (j - n_far) * nt + i), 0))],
        out_specs=pl.BlockSpec((tr, C), lambda j, i, c, o, m: (
            jnp.where(j < n_far, 2 * o[far(j)] + 1 - c[0], 2 * m[0] + j - n_far) * nt + i, 0)),
        scratch_shapes=[pltpu.VMEM((2, tr, C), gathered.dtype), pltpu.SemaphoreType.DMA((2,)),
                        pltpu.SemaphoreType.DMA((2,))])
    out = pl.pallas_call(body, name=name, grid_spec=grid_spec, out_shape=_sds((k * r, C), gathered.dtype),
                         input_output_aliases={3: 0}, compiler_params=_params(2, PAIR_FILL_ID))(
                             core, others, chip, gathered.reshape(k * r, C), mine)
    return out.reshape(k, r, C)


def _pair_reduce(name, g, core):
    k, R, C = g.shape
    half = R // 2
    tr = _row_tile(half, C, PAIR_TILE_ELEMS, mult=16)
    nt = half // tr

    def body(c_ref, mine_ref, give_ref, out_ref, wire_ref, stage, slot, send, recv):
        _sibling_barrier((pl.program_id(0) == 0) & (pl.program_id(1) == 0))
        b = (pl.program_id(0) * nt + pl.program_id(1)) % 2
        x, y, c = _place()
        stage[b] = give_ref[...].astype(stage.dtype)
        cp = pltpu.make_async_remote_copy(src_ref=stage.at[b], dst_ref=slot.at[b], send_sem=send.at[b],
                                          recv_sem=recv.at[b], device_id=(x, y, 1 - c), device_id_type=MESH)
        cp.start()
        cp.wait_recv()
        tot = mine_ref[...] + slot[b].astype(F32)
        out_ref[...] = tot
        wire_ref[...] = tot.astype(wire_ref.dtype)
        cp.wait_send()

    blk = (tr, C)
    out_spec = pl.BlockSpec(blk, lambda s, i, c: (s * nt + i, 0))
    grid_spec = pltpu.PrefetchScalarGridSpec(
        num_scalar_prefetch=1, grid=(k, nt),
        in_specs=[pl.BlockSpec(blk, lambda s, i, c: ((2 * s + c[0]) * nt + i, 0)),
                  pl.BlockSpec(blk, lambda s, i, c: ((2 * s + 1 - c[0]) * nt + i, 0))],
        out_specs=[out_spec, out_spec],
        scratch_shapes=[pltpu.VMEM((2, tr, C), CDT), pltpu.VMEM((2, tr, C), CDT), pltpu.SemaphoreType.DMA((2,)),
                        pltpu.SemaphoreType.DMA((2,))])
    g2 = g.reshape(k * R, C)
    out, wire = pl.pallas_call(body, name=name, grid_spec=grid_spec,
                               out_shape=[_sds((k * half, C), F32), _sds((k * half, C), CDT)],
                               compiler_params=_params(2, PAIR_REDUCE_ID))(core, g2, g2)
    return out, wire.reshape(k, half, C)


def _all_reduce_small(v):
    r, cdim = v.shape

    def body(v_ref, o_ref, buf, send, recv):
        x, y, c = _place()
        me = 4 * x + 2 * y + c
        buf[me] = v_ref[...]
        sends = []
        for m in range(1, N_DEV):
            px = 1 - x if m & 4 else x
            py = 1 - y if m & 2 else y
            pc = 1 - c if m & 1 else c
            cp = pltpu.make_async_remote_copy(src_ref=v_ref, dst_ref=buf.at[me], send_sem=send.at[m - 1],
                                              recv_sem=recv.at[m - 1], device_id=(px, py, pc), device_id_type=MESH)
            cp.start()
            sends.append((cp, 4 * px + 2 * py + pc))
        for m, (cp, pidx) in enumerate(sends):
            pltpu.make_async_remote_copy(src_ref=v_ref, dst_ref=buf.at[pidx], send_sem=send.at[m], recv_sem=recv.at[m],
                                         device_id=(x, y, c), device_id_type=MESH).wait_recv()
        for cp, _ in sends:
            cp.wait_send()
        tot = buf[0]
        for k in range(1, N_DEV):
            tot = tot + buf[k]
        o_ref[...] = tot

    return pl.pallas_call(
        body, name="all_reduce_small", in_specs=[VMEM_SPEC], out_specs=VMEM_SPEC,
        out_shape=_sds((r, cdim), F32),
        scratch_shapes=[pltpu.VMEM((N_DEV, r, cdim), F32), pltpu.SemaphoreType.DMA((N_DEV - 1,)),
                        pltpu.SemaphoreType.DMA((N_DEV - 1,))],
    )(v)


def _row_tile(rows, cols, budget_elems=1 << 18, mult=8):
    if rows % mult:
        return rows
    t = max(mult, (budget_elems // cols) // mult * mult)
    while rows % t:
        t -= mult
    return t


def _adamw_update(w, g, m, v):
    nm = ADAM_B1 * m + (1.0 - ADAM_B1) * g
    nv = ADAM_B2 * v + (1.0 - ADAM_B2) * (g * g)
    m_hat = nm / (1.0 - ADAM_B1 ** ADAM_STEP)
    v_hat = nv / (1.0 - ADAM_B2 ** ADAM_STEP)
    return -ADAM_LR * (m_hat / (jnp.sqrt(v_hat) + ADAM_EPS) + ADAM_WD * w), nm, nv


def _adamw(name, w, g, m, v):
    R, C = w.shape
    tr = _row_tile(R, C, 1 << 19)

    def body(w_ref, g_ref, m_ref, v_ref, d_ref, nm_ref, nv_ref):
        d_ref[...], nm_ref[...], nv_ref[...] = _adamw_update(w_ref[...], g_ref[...], m_ref[...], v_ref[...])

    spec = pl.BlockSpec((tr, C), lambda i: (i, 0))
    return pl.pallas_call(body, name=name, grid=(R // tr,), in_specs=[spec] * 4, out_specs=[spec] * 3,
                          out_shape=[_sds((R, C), F32)] * 3, compiler_params=_params(1))(w, g, m, v)


def _sum_share(name, own, by_chip, chip, others, core):
    k, half, C = by_chip.shape
    tr = _row_tile(half, C, PAIR_TILE_ELEMS // 2, mult=16)
    nt = half // tr

    def body(chip_ref, oth_ref, c_ref, own_ref, a_ref, b_ref, cc_ref, g_out, mine, slot, send, recv):
        p = pl.program_id(1)
        _sibling_barrier((pl.program_id(0) == 0) & (p == 0))
        b = pl.program_id(0) % 2
        x, y, c = _place()
        cp = pltpu.make_async_remote_copy(src_ref=mine.at[b], dst_ref=slot.at[b], send_sem=send.at[b],
                                          recv_sem=recv.at[b], device_id=(x, y, 1 - c), device_id_type=MESH)

        @pl.when(p == 0)
        def _():
            tot = ((own_ref[...] + a_ref[...].astype(F32)) + b_ref[...].astype(F32)) + cc_ref[...].astype(F32)
            mine[b] = tot
            cp.start()
            g_out[...] = tot

        @pl.when(p == 1)
        def _():
            cp.wait_recv()
            g_out[...] = slot[b]
            cp.wait_send()

    def piece(j):
        return pl.BlockSpec((tr, C), lambda i, p, chip, oth, c: (oth[j] * nt + i, 0))

    grid_spec = pltpu.PrefetchScalarGridSpec(
        num_scalar_prefetch=3, grid=(nt, 2),
        in_specs=[pl.BlockSpec((tr, C), lambda i, p, chip, oth, c: (chip[0] * nt + i, 0)),
                  piece(0), piece(1), piece(2)],
        out_specs=pl.BlockSpec((tr, C), lambda i, p, chip, oth, c: (
            jnp.where(p == 0, c[0], 1 - c[0]) * nt + i, 0)),
        scratch_shapes=[pltpu.VMEM((2, tr, C), F32), pltpu.VMEM((2, tr, C), F32), pltpu.SemaphoreType.DMA((2,)),
                        pltpu.SemaphoreType.DMA((2,))])
    by2 = by_chip.reshape(k * half, C)
    return pl.pallas_call(body, name=name, grid_spec=grid_spec, out_shape=_sds((2 * half, C), F32),
                          compiler_params=_params(2, SUM_SHARE_ID))(chip, others, core, own, by2, by2, by2)


BIG = ("w_in", "w_proj_a", "w_proj_b", "w_out", "w_up", "w_down")
COL_SHARDED = ("w_in", "w_proj_a", "w_up")
SMALL = ("norm1_g", "q_norm_g", "k_norm_g", "ret_gn_g", "ret_gn_b", "norm2_g")
ALL_W = ("norm1_g", "w_in", "q_norm_g", "k_norm_g", "ret_gn_g", "ret_gn_b", "w_proj_a", "w_proj_b", "w_out",
         "norm2_g", "w_up", "w_down")
LANES = 128


def _to_full(name, gathered):
    k, r, c = gathered.shape
    if name in COL_SHARDED:
        return gathered.transpose(1, 0, 2).reshape(r, k * c)
    return gathered.reshape(k * r, c)


def _to_shard_major(name, full):
    if name in COL_SHARDED:
        r, c4 = full.shape
        return full.reshape(r, N_CHIPS, c4 // N_CHIPS).transpose(1, 0, 2)
    r4, c = full.shape
    return full.reshape(N_CHIPS, r4 // N_CHIPS, c)


def kernel(x, norm1_g, w_in, q_norm_g, k_norm_g, ret_gn_g, ret_gn_b, w_proj_a, w_proj_b, w_out, norm2_g, w_up, w_down, loss_target, m_norm1_g, m_w_in, m_q_norm_g, m_k_norm_g, m_ret_gn_g, m_ret_gn_b, m_w_proj_a, m_w_proj_b, m_w_out, m_norm2_g, m_w_up, m_w_down, v_norm1_g, v_w_in, v_q_norm_g, v_k_norm_g, v_ret_gn_g, v_ret_gn_b, v_w_proj_a, v_w_proj_b, v_w_out, v_norm2_g, v_w_up, v_w_down):
    weights = dict(norm1_g=norm1_g, w_in=w_in, q_norm_g=q_norm_g, k_norm_g=k_norm_g, ret_gn_g=ret_gn_g,
                   ret_gn_b=ret_gn_b, w_proj_a=w_proj_a, w_proj_b=w_proj_b, w_out=w_out, norm2_g=norm2_g,
                   w_up=w_up, w_down=w_down)
    moments_m = dict(norm1_g=m_norm1_g, w_in=m_w_in, q_norm_g=m_q_norm_g, k_norm_g=m_k_norm_g, ret_gn_g=m_ret_gn_g,
                     ret_gn_b=m_ret_gn_b, w_proj_a=m_w_proj_a, w_proj_b=m_w_proj_b, w_out=m_w_out,
                     norm2_g=m_norm2_g, w_up=m_w_up, w_down=m_w_down)
    moments_v = dict(norm1_g=v_norm1_g, w_in=v_w_in, q_norm_g=v_q_norm_g, k_norm_g=v_k_norm_g, ret_gn_g=v_ret_gn_g,
                     ret_gn_b=v_ret_gn_b, w_proj_a=v_w_proj_a, w_proj_b=v_w_proj_b, w_out=v_w_out,
                     norm2_g=v_norm2_g, w_up=v_w_up, w_down=v_w_down)

    mx, my = lax.axis_index("x"), lax.axis_index("y")
    core = lax.axis_index("c").astype(jnp.int32).reshape(1)
    chip = (2 * mx + my).astype(jnp.int32).reshape(1)
    others = jnp.stack([2 * (1 - mx) + my, 2 * mx + 1 - my, 2 * (1 - mx) + 1 - my]).astype(jnp.int32)
    shards = {n: weights[n][0].astype(CDT) for n in BIG}
    def start_gather(name, names):
        return _exchange_start(name, "gather", [shards[n] for n in names],
                               [((N_CHIPS,) + shards[n].shape, CDT) for n in names])

    w_in_shape = [((N_CHIPS,) + shards["w_in"].shape, CDT)]
    n_send, n_recv, n_srcs, n_lands, n_token = _exchange_start(
        "gather_w_in_near_start", "gather", [shards["w_in"]], w_in_shape, which=(0, 1))
    late = [n for n in BIG if n != "w_in"]
    flight = {}

    def near_w_in(after):
        srcs, lands = _exchange_wait("gather_w_in_near_wait", "gather", n_send, n_recv, n_srcs, n_lands, after,
                                     which=(0, 1))
        d_send, d_recv, d_srcs, d_lands, _ = _exchange_start(
            "gather_w_in_diag_start", "gather", srcs, w_in_shape, which=(2,), lands=lands)
        flight["late"] = start_gather("gather_late_start", late)
        w_near = _pair_fill("pair_fill_w_in_near", d_lands[0], d_srcs[0], core, others[:2], chip)
        flight["diag"] = (d_send, d_recv, d_srcs, [w_near])
        return w_near, flight["late"][-1]

    def far_w_in(after):
        d_send, d_recv, d_srcs, d_lands = flight["diag"]
        srcs, lands = _exchange_wait("gather_w_in_diag_wait", "gather", d_send, d_recv, d_srcs, d_lands, after,
                                     which=(2,))
        return _pair_fill("pair_fill_w_in_diag", lands[0], srcs[0], core, others[2:], chip, write_mine=False), None

    def late_weights(after):
        l_send, l_recv, l_srcs, l_lands, _ = flight["late"]
        srcs, lands = _exchange_wait("gather_late_wait", "gather", l_send, l_recv, l_srcs, l_lands, after)
        out = {}
        for n, mine, land in zip(late, srcs, lands):
            out[n] = _to_full(n, _pair_fill("pair_fill_%s" % n, land, mine, core, others, chip))
        return out

    pending = []

    def on_grads(group):
        names = list(group)
        red = [_pair_reduce("pair_reduce_%s" % n, g if g.ndim == 3 else _to_shard_major(n, g), core)
               for n, g in group.items()]
        wires = [wire for _, wire in red]
        send, recv, srcs, lands, token = _exchange_start(
            "scatter_start_%s" % names[0], "scatter", wires, [(wire.shape, wire.dtype) for wire in wires])
        pending.append((names, [own for own, _ in red], send, recv, srcs, lands))
        return token

    small = {n: weights[n].reshape(1, -1) for n in SMALL}

    loss, grad_x, small_g = _local_step(x[0], loss_target[0], n_srcs[0], chip, others, near_w_in, far_w_in, small,
                                        late_weights, on_grads, deps0=[n_token])

    out_g, out_d, out_m, out_v = {}, {}, {}, {}
    for names, owns, send, recv, srcs, lands in pending:
        _, got = _exchange_wait("scatter_wait_%s" % names[0], "scatter", send, recv, srcs, lands, grad_x)
        for n, own, by_chip in zip(names, owns, got):
            shape = weights[n].shape
            g2 = _sum_share("sum_share_%s" % n, own, by_chip, chip, others, core)
            d, nm, nv = _adamw("adamw_%s" % n, weights[n][0], g2, moments_m[n][0], moments_v[n][0])
            out_g[n], out_d[n], out_m[n], out_v[n] = (t.reshape(shape) for t in (g2, d, nm, nv))

    packed = jnp.concatenate([small_g[n].reshape(1, -1) for n in SMALL], axis=1).reshape(-1, LANES)
    loss_tile = jnp.zeros((8, LANES), F32).at[0, 0].set(loss)
    red = _all_reduce_small(jnp.concatenate([packed, loss_tile], axis=0))
    loss = red[packed.shape[0], 0]
    red = red[:packed.shape[0]].reshape(1, -1)
    off = 0
    for n in SMALL:
        shape = weights[n].shape
        row = (1, weights[n].size)
        g2 = red[:, off:off + row[1]]
        off += row[1]
        d, nm, nv = _adamw("adamw_%s" % n, weights[n].reshape(row), g2, moments_m[n].reshape(row),
                           moments_v[n].reshape(row))
        out_g[n], out_d[n], out_m[n], out_v[n] = (t.reshape(shape) for t in (g2, d, nm, nv))

    return (loss, grad_x[None], *[out_g[n] for n in ALL_W], *[out_d[n] for n in ALL_W],
            *[out_m[n] for n in ALL_W], *[out_v[n] for n in ALL_W])
```

```python
import functools

import jax
import jax.numpy as jnp
from jax import lax
from jax.experimental import pallas as pl
from jax.experimental.pallas import tpu as pltpu

CDT = jnp.bfloat16
F32 = jnp.float32
EPS = 1e-6

ATT_GROUPS = ((128, 1), (512, 4), (2048, 16))
ATT_HPG = 4
ATT_HEADS = 12
HD = 128
BLK = 128
ATT_W = ATT_HEADS * HD
GW = ATT_HPG * HD
RET_HEADS = 4

ADAM_LR = 0.001
ADAM_B1 = 0.9
ADAM_B2 = 0.999
ADAM_EPS = 1e-08
ADAM_WD = 0.01
ADAM_STEP = 10

VMEM_LIMIT_BYTES = 56 * 1024 * 1024
MESH = pl.DeviceIdType.MESH
HBM_SPEC = pl.BlockSpec(memory_space=pltpu.HBM)
VMEM_SPEC = pl.BlockSpec(memory_space=pltpu.VMEM)


def _params(n_axes, collective_id=None):
    return pltpu.CompilerParams(dimension_semantics=("arbitrary",) * n_axes,
                                vmem_limit_bytes=VMEM_LIMIT_BYTES, collective_id=collective_id)


PAIR_FILL_ID, PAIR_REDUCE_ID, SUM_SHARE_ID = 1, 2, 3


def _sibling_barrier(first_step):
    @pl.when(first_step)
    def _():
        sem = pltpu.get_barrier_semaphore()
        x, y, c = lax.axis_index("x"), lax.axis_index("y"), lax.axis_index("c")
        pl.semaphore_signal(sem, inc=1, device_id=(x, y, 1 - c), device_id_type=pl.DeviceIdType.MESH)
        pl.semaphore_wait(sem, 1)


def _dot_nn(a, b):
    return jnp.dot(a, b, preferred_element_type=F32)


def _dot_nt(a, b):
    return lax.dot_general(a, b, (((1,), (1,)), ((), ())), preferred_element_type=F32)


def _dot_tn(a, b):
    return lax.dot_general(a, b, (((0,), (0,)), ((), ())), preferred_element_type=F32)


def _sigmoid(v):
    return 1.0 / (1.0 + jnp.exp(-v))


def _matmul(name, mode, a, b, *, tm, tn, tk, extras=(), outs, epilogue, deps=(), b_spec=None, n_cols=None,
            prefetch=(), alias_dep_to_out=None, j_outer=False):
    deps = [d for d in deps if d is not None]
    if mode == "tn":
        K, M = a.shape
    else:
        M, K = a.shape
    if b_spec is None:
        (N, K2) = b.shape if mode == "nt" else b.shape[::-1]
        assert K == K2, (name, a.shape, b.shape)
        if mode == "nt":
            b_spec = pl.BlockSpec((tn, tk), lambda i, j, k, *p: (j, k))
        else:
            b_spec = pl.BlockSpec((tk, tn), lambda i, j, k, *p: (k, j))
    else:
        N = n_cols
    assert M % tm == 0 and N % tn == 0 and K % tk == 0, (name, a.shape, b.shape)
    ni, nj, nk = M // tm, N // tn, K // tk
    if mode == "tn":
        a_spec = pl.BlockSpec((tk, tm), lambda i, j, k, *p: (k, i))
    else:
        a_spec = pl.BlockSpec((tm, tk), lambda i, j, k, *p: (i, k))
    dot = {"nn": _dot_nn, "nt": _dot_nt, "tn": _dot_tn}[mode]
    n_ex, n_out, n_dep, n_pre = len(extras), len(outs), len(deps), len(prefetch)
    grid = (ni, nj, nk)
    if j_outer:
        grid = (nj, ni, nk)

        def swapped(spec):
            return pl.BlockSpec(spec.block_shape, lambda j, i, k, *p: spec.index_map(i, j, k, *p))

        a_spec, b_spec = swapped(a_spec), swapped(b_spec)
        extras = [(e, swapped(s)) for e, s in extras]
        outs = [(o, swapped(s)) for o, s in outs]

    def body(*refs):
        refs = refs[n_pre:]
        a_ref, b_ref = refs[0], refs[1]
        ex = refs[2:2 + n_ex]
        out = refs[2 + n_ex + n_dep:2 + n_ex + n_dep + n_out]
        acc = refs[-1] if nk > 1 else None
        i = pl.program_id(1 if j_outer else 0)
        k = pl.program_id(2)
        if nk == 1:
            epilogue(dot(a_ref[...].astype(CDT), b_ref[...].astype(CDT)), ex, out, i)
            return

        @pl.when(k == 0)
        def _():
            acc[...] = jnp.zeros_like(acc)

        acc[...] += dot(a_ref[...].astype(CDT), b_ref[...].astype(CDT))

        @pl.when(k == nk - 1)
        def _():
            epilogue(acc[...], ex, out, i)

    grid_spec = pltpu.PrefetchScalarGridSpec(
        num_scalar_prefetch=n_pre, grid=grid,
        in_specs=[a_spec, b_spec] + [s for _, s in extras] + [pl.BlockSpec(memory_space=pl.ANY)] * n_dep,
        out_specs=[s for _, s in outs],
        scratch_shapes=[pltpu.VMEM((tm, tn), F32)] if nk > 1 else [])
    aliases = {}
    if alias_dep_to_out is not None:
        aliases = {n_pre + 2 + n_ex + alias_dep_to_out[0]: alias_dep_to_out[1]}
    res = pl.pallas_call(
        body, name=name, grid_spec=grid_spec, out_shape=[o for o, _ in outs], input_output_aliases=aliases,
        compiler_params=_params(3),
    )(*prefetch, a, b, *[e for e, _ in extras], *deps)
    return res


def _mn(tm, tn, col_off=0):
    return pl.BlockSpec((tm, tn), lambda i, j, k, *p: (i, j + col_off))


def _row(tn):
    return pl.BlockSpec((1, tn), lambda i, j, k, *p: (0, j))


def _ep_store(acc, ex, out, i):
    out[0][...] = acc.astype(out[0].dtype)


def _ep_resid_norm(acc, ex, out, i):
    x1 = ex[0][...] + acc
    out[0][...] = x1
    rstd = lax.rsqrt(jnp.mean(x1 * x1, axis=-1, keepdims=True) + EPS)
    out[1][...] = (x1 * rstd * ex[1][...]).astype(out[1].dtype)


def _ep_up(acc, ex, out, i):
    out[0][...] = acc.astype(out[0].dtype)
    r = jnp.maximum(acc, 0.0)
    out[1][...] = (r * r).astype(out[1].dtype)


def _ep_down_loss(acc, ex, out, i, inv_d):
    diff = (ex[0][...] + acc) - ex[1][...]
    dx2 = diff * inv_d
    out[0][...] = dx2
    out[1][...] = dx2.astype(out[1].dtype)

    @pl.when(i == 0)
    def _():
        out[2][...] = jnp.zeros_like(out[2])

    out[2][...] += jnp.sum(diff * diff, axis=0, keepdims=True)


def _ep_dh(acc, ex, out, i):
    h = ex[0][...].astype(F32)
    out[0][...] = (acc * (2.0 * jnp.maximum(h, 0.0))).astype(out[0].dtype)


def _ep_rms_bwd(acc, ex, out, i):
    x = ex[0][...]
    g = ex[1][...]
    rstd = lax.rsqrt(jnp.mean(x * x, axis=-1, keepdims=True) + EPS)
    xh = x * rstd
    dxh = acc * g
    dx = ex[2][...] + rstd * (dxh - xh * jnp.mean(dxh * xh, axis=-1, keepdims=True))
    out[0][...] = dx
    for copy in out[1:-1]:
        copy[...] = dx.astype(copy.dtype)
    dg = out[-1]

    @pl.when(i == 0)
    def _():
        dg[...] = jnp.zeros_like(dg)

    dg[...] += jnp.sum(acc * xh, axis=0, keepdims=True)


def _ep_gates(acc, ex, out, i):
    sa = _sigmoid(ex[0][...].astype(F32))
    sb = _sigmoid(ex[1][...].astype(F32))
    dpa = acc * sa
    dpb = acc * sb
    out[0][...] = dpa.astype(out[0].dtype)
    out[1][...] = dpb.astype(out[1].dtype)
    out[2][...] = (dpa * ex[2][...].astype(F32) * (1.0 - sa)).astype(out[2].dtype)
    out[3][...] = (dpb * ex[3][...].astype(F32) * (1.0 - sb)).astype(out[3].dtype)


def _sds(shape, dtype):
    return jax.ShapeDtypeStruct(shape, dtype)


def _in_proj_mine(x, g, w_mine, chip, proj_sds, deps, tm=512):
    S, D = x.shape
    ns = w_mine.shape[1]
    deps = [d for d in deps if d is not None]

    def body(c_ref, x_ref, g_ref, w_ref, *rest):
        xn_ref, proj_ref = rest[len(deps)], rest[len(deps) + 1]
        xv = x_ref[...]
        rstd = lax.rsqrt(jnp.mean(xv * xv, axis=-1, keepdims=True) + EPS)
        xn = (xv * rstd * g_ref[...]).astype(xn_ref.dtype)
        xn_ref[...] = xn
        proj_ref[...] = _dot_nn(xn, w_ref[...]).astype(proj_ref.dtype)

    grid_spec = pltpu.PrefetchScalarGridSpec(
        num_scalar_prefetch=1, grid=(S // tm,),
        in_specs=[pl.BlockSpec((tm, D), lambda i, c: (i, 0)), pl.BlockSpec((1, D), lambda i, c: (0, 0)),
                  pl.BlockSpec((D, ns), lambda i, c: (0, 0))] + [pl.BlockSpec(memory_space=pl.ANY)] * len(deps),
        out_specs=[pl.BlockSpec((tm, D), lambda i, c: (i, 0)), pl.BlockSpec((tm, ns), lambda i, c: (i, c[0]))])
    return pl.pallas_call(body, name="in_proj_mine", grid_spec=grid_spec, out_shape=[_sds((S, D), CDT), proj_sds],
                          compiler_params=_params(1))(chip, x, g, w_mine, *deps)


def _rm_shape(S, d, width):
    return (S, width) if d == 1 else (d, S // d, width)


def _rm_spec(tm, d, width):
    if d == 1:
        return pl.BlockSpec((tm, width), lambda i: (i, 0))
    return pl.BlockSpec((d, tm // d, width), lambda i: (0, i, 0))


def _rm_put(dst_ref, cols, buf_ref, d):
    if d == 1:
        dst_ref[:, cols] = buf_ref[...].astype(dst_ref.dtype)
        return
    m = buf_ref.shape[0] // d
    for r in range(d):
        dst_ref[r, :, cols] = buf_ref[pl.ds(r, m, stride=d), :].astype(dst_ref.dtype)


def _rm_reader(buf_ref, src_ref, d):
    if d == 1:
        return lambda s, rows: src_ref[rows, s * HD:(s + 1) * HD].astype(F32)
    m = buf_ref.shape[1] // d
    for s in range(buf_ref.shape[0]):
        for r in range(d):
            buf_ref.at[s][pl.ds(r, m, stride=d), :] = src_ref[r, :, s * HD:(s + 1) * HD].astype(F32)
    return lambda s, rows: buf_ref.at[s][rows, :]


def _qknorm_fwd(proj, gqk, tm=1024):
    S = proj.shape[0]
    W = 2 * ATT_W
    dil = [d for _, d in ATT_GROUPS]

    def body(p_ref, g_ref, o0, o1, o2, buf):
        outs = (o0, o1, o2)
        for hd in range(3 * ATT_HEADS):
            which, head = hd // ATT_HEADS, hd % ATT_HEADS
            grp, slot = head // ATT_HPG, head % ATT_HPG
            cols = slice(hd * HD, (hd + 1) * HD)

            def chunk(rows, which=which, cols=cols):
                v = p_ref[rows, cols].astype(F32)
                if which < 2:
                    rstd = lax.rsqrt(jnp.mean(v * v, axis=-1, keepdims=True) + EPS)
                    v = v * rstd * g_ref[:, cols]
                buf[rows, :] = v

            chunk(slice(None))
            _rm_put(outs[grp], slice(which * GW + slot * HD, which * GW + (slot + 1) * HD), buf, dil[grp])

    return pl.pallas_call(
        body, name="qknorm_fwd", grid=(S // tm,),
        in_specs=[pl.BlockSpec((tm, 3 * ATT_W), lambda i: (i, 0)), pl.BlockSpec((1, W), lambda i: (0, 0))],
        out_specs=[_rm_spec(tm, d, 3 * GW) for d in dil],
        out_shape=[_sds(_rm_shape(S, d, 3 * GW), CDT) for d in dil],
        scratch_shapes=[pltpu.VMEM((tm, HD), F32)],
        compiler_params=_params(1))(proj, gqk)


def _qknorm_bwd(proj, gqk, dqs, dks, dvs, dproj, tm=512):
    S = proj.shape[0]
    W = 2 * ATT_W
    dil = [d for _, d in ATT_GROUPS]

    def body(p_ref, g_ref, *refs):
        ins = refs[0:9]
        o_ref, dg_ref = refs[10], refs[11]
        bufs = refs[12:21]
        i = pl.program_id(0)

        @pl.when(i == 0)
        def _():
            dg_ref[...] = jnp.zeros_like(dg_ref)

        nat = [_rm_reader(bufs[j], ins[j], dil[j % 3]) for j in range(9)]
        dq_get, dk_get, dv_get = nat[0:3], nat[3:6], nat[6:9]
        for hd in range(2 * ATT_HEADS):
            sl = slice(hd * HD, (hd + 1) * HD)
            head = hd % ATT_HEADS
            grp, slot = head // ATT_HPG, head % ATT_HPG
            get = (dq_get if hd < ATT_HEADS else dk_get)[grp]

            def chunk(rows, sl=sl, slot=slot, get=get):
                dn = get(slot, rows)
                v = p_ref[rows, sl].astype(F32)
                rstd = lax.rsqrt(jnp.mean(v * v, axis=-1, keepdims=True) + EPS)
                vh = v * rstd
                dg_ref[:, sl] += jnp.sum(dn * vh, axis=0, keepdims=True)
                dvh = dn * g_ref[:, sl]
                o_ref[rows, sl] = (rstd * (dvh - vh * jnp.mean(dvh * vh, axis=-1, keepdims=True))).astype(o_ref.dtype)

            chunk(slice(None))
        for head in range(ATT_HEADS):
            grp, slot = head // ATT_HPG, head % ATT_HPG
            o_ref[:, W + head * HD:W + (head + 1) * HD] = dv_get[grp](slot, slice(None)).astype(o_ref.dtype)

    return pl.pallas_call(
        body, name="qknorm_bwd", grid=(S // tm,),
        in_specs=[pl.BlockSpec((tm, W), lambda i: (i, 0)), pl.BlockSpec((1, W), lambda i: (0, 0))]
        + [_rm_spec(tm, d, GW) for d in dil] * 3 + [pl.BlockSpec(memory_space=pl.ANY)],
        out_specs=[pl.BlockSpec((tm, 3 * ATT_W), lambda i: (i, 0)), pl.BlockSpec((1, W), lambda i: (0, 0))],
        out_shape=[_sds(dproj.shape, dproj.dtype), _sds((1, W), F32)],
        scratch_shapes=[pltpu.VMEM((ATT_HPG, tm, HD), F32)] * 9,
        input_output_aliases={11: 0},
        compiler_params=_params(1))(proj, gqk, *dqs, *dks, *dvs, dproj)


def _att_mask(n):
    qi = lax.broadcasted_iota(jnp.int32, (BLK, 2 * BLK), 0)
    kj = lax.broadcasted_iota(jnp.int32, (BLK, 2 * BLK), 1)
    dist = BLK + qi - kj
    valid_all = (dist >= 0) & (dist <= BLK)
    return valid_all & ((kj >= BLK) | (n > 0)), valid_all, dist.astype(F32)


def _att_slopes(grp):
    return [2.0 ** (-8.0 * (grp * ATT_HPG + hh + 1) / ATT_HEADS) for hh in range(ATT_HPG)]


ATT_WIDTH = 8
ATT_BWD_SEGMENTS = 4


def _att_planes(d, width):
    return d if d > 1 else width


def _att_step_planes(d, width):
    return min(_att_planes(d, width), width)


def _att_3d(a, d, width):
    return a.reshape(width, a.shape[0] // width, a.shape[1]) if d == 1 else a


def _att_spec(R, row_fn, col=0):
    return pl.BlockSpec((R, BLK, GW), lambda r, n: (r, row_fn(n), col))


def _att_chains(R):
    return [(rr * ATT_HPG + hh, rr, slice(hh * HD, (hh + 1) * HD), hh) for rr in range(R) for hh in range(ATT_HPG)]


def _att_qkv_specs(R, nb):
    last = nb - 1

    def cur(n):
        return jnp.minimum(n, last)

    def prev(n):
        return jnp.maximum(jnp.minimum(n, last) - 1, 0)

    return [_att_spec(R, cur, 0), _att_spec(R, prev, 1), _att_spec(R, cur, 1), _att_spec(R, prev, 2),
            _att_spec(R, cur, 2), _att_spec(R, lambda n: last, 1), _att_spec(R, lambda n: last, 2)]


def _att_prev(seg, n, prev_ref, last_ref, rr, sl):
    t = prev_ref[rr, :, sl]
    if seg and rr > 0:
        t = jnp.where(n == 0, last_ref[rr - 1, :, sl], t)
    return t


def _att_fwd(grp, S, qkv):
    _, d = ATT_GROUPS[grp]
    seg = d == 1
    width = ATT_WIDTH
    P = _att_planes(d, width)
    L = S // P
    nb = L // BLK
    R = _att_step_planes(d, width)
    assert P % R == 0 and (not seg or P == R)
    slopes = _att_slopes(grp)
    scale = HD ** -0.5
    chains = _att_chains(R)

    def body(q_ref, kp_ref, kc_ref, vp_ref, vc_ref, kl_ref, vl_ref, o_ref, l_ref, s_buf, p_buf, den_buf):
        n = pl.program_id(1)
        valid, valid_all, distf = _att_mask(n)
        for c, rr, sl, hh in chains:
            k = jnp.concatenate([_att_prev(seg, n, kp_ref, kl_ref, rr, sl), kc_ref[rr, :, sl]], axis=0)
            s_buf[c] = _dot_nt(q_ref[rr, :, sl], k)
        for c, rr, sl, hh in chains:
            s = s_buf[c] * scale + (-slopes[hh] * d) * distf
            s = jnp.where(valid_all if seg and rr > 0 else valid, s, -1e30)
            m = jnp.max(s, axis=-1, keepdims=True)
            p = jnp.exp(s - m)
            den = jnp.sum(p, axis=-1, keepdims=True)
            p_buf[c] = p.astype(CDT)
            den_buf[c] = jnp.broadcast_to(den, (BLK, HD))
            l_ref[rr, :, sl] = jnp.broadcast_to(m + jnp.log(den), (BLK, HD))
        for c, rr, sl, hh in chains:
            v = jnp.concatenate([_att_prev(seg, n, vp_ref, vl_ref, rr, sl), vc_ref[rr, :, sl]], axis=0)
            o_ref[rr, :, sl] = _dot_nn(p_buf[c], v) / den_buf[c]

    out_spec = _att_spec(R, lambda n: n)
    n_ch = len(chains)
    q3 = _att_3d(qkv, d, width)
    o, l = pl.pallas_call(
        body, name="att_fwd_g%d" % grp, grid=(P // R, nb),
        in_specs=_att_qkv_specs(R, nb),
        out_specs=[out_spec, out_spec],
        out_shape=[_sds((P, L, GW), F32)] * 2,
        scratch_shapes=[pltpu.VMEM((n_ch, BLK, 2 * BLK), F32), pltpu.VMEM((n_ch, BLK, 2 * BLK), CDT),
                        pltpu.VMEM((n_ch, BLK, HD), F32)],
        compiler_params=_params(2),
    )(*[q3] * 7)
    return o.reshape(_rm_shape(S, d, GW)), l.reshape(_rm_shape(S, d, GW))


def _att_bwd(grp, S, qkv, lse, do_g, c_g):
    _, d = ATT_GROUPS[grp]
    seg = d == 1
    width = ATT_BWD_SEGMENTS if seg else ATT_WIDTH
    P = _att_planes(d, width)
    L = S // P
    nb = L // BLK
    R = _att_step_planes(d, width)
    assert P % R == 0 and (not seg or P == R)
    slopes = _att_slopes(grp)
    scale = HD ** -0.5
    last = nb - 1
    chains = _att_chains(R)

    def body(q_ref, kp_ref, kc_ref, vp_ref, vc_ref, kl_ref, vl_ref, l_ref, do_ref, c_ref, dq_ref, dk_ref, dv_ref,
             ck, cv, fk, fv, s_buf, dp_buf, p_buf, ds_buf):
        n = pl.program_id(1)

        @pl.when(n == 0)
        def _():
            for buf in (ck, cv, fk, fv):
                buf[...] = jnp.zeros_like(buf)

        @pl.when(n < nb)
        def _():
            valid, valid_all, distf = _att_mask(n)
            for c, rr, sl, hh in chains:
                k = jnp.concatenate([_att_prev(seg, n, kp_ref, kl_ref, rr, sl), kc_ref[rr, :, sl]], axis=0)
                v = jnp.concatenate([_att_prev(seg, n, vp_ref, vl_ref, rr, sl), vc_ref[rr, :, sl]], axis=0)
                s_buf[c] = _dot_nt(q_ref[rr, :, sl], k)
                dp_buf[c] = _dot_nt(do_ref[rr, :, sl], v)
            for c, rr, sl, hh in chains:
                s = s_buf[c] * scale + (-slopes[hh] * d) * distf
                p = jnp.where(valid_all if seg and rr > 0 else valid, jnp.exp(s - l_ref[rr, :, sl][:, 0:1]), 0.0)
                p_buf[c] = p.astype(CDT)
                ds_buf[c] = (p * (dp_buf[c] + c_ref[rr, :, sl][:, 0:1]) * scale).astype(CDT)
            for c, rr, sl, hh in chains:
                k = jnp.concatenate([_att_prev(seg, n, kp_ref, kl_ref, rr, sl), kc_ref[rr, :, sl]], axis=0)
                ds = ds_buf[c]
                dq_ref[rr, :, sl] = _dot_nn(ds, k)
                dk = _dot_tn(ds, q_ref[rr, :, sl])
                dv = _dot_tn(p_buf[c], do_ref[rr, :, sl])
                dk_ref[rr, :, sl] = ck[rr, :, sl] + dk[0:BLK]
                dv_ref[rr, :, sl] = cv[rr, :, sl] + dv[0:BLK]
                ck[rr, :, sl] = dk[BLK:2 * BLK]
                cv[rr, :, sl] = dv[BLK:2 * BLK]
                if seg and rr > 0:
                    @pl.when(n == 0)
                    def _(rr=rr, sl=sl, dk=dk, dv=dv):
                        fk[rr - 1, :, sl] = dk[0:BLK]
                        fv[rr - 1, :, sl] = dv[0:BLK]

        @pl.when(n == nb)
        def _():
            dk_ref[...] = ck[...] + fk[...]
            dv_ref[...] = cv[...] + fv[...]

    blk = (R, BLK, GW)
    at_q = _att_spec(R, lambda n: jnp.minimum(n, last))
    behind = _att_spec(R, lambda n: jnp.maximum(n - 1, 0))
    n_ch = len(chains)
    q3 = _att_3d(qkv, d, width)
    res = pl.pallas_call(
        body, name="att_bwd_g%d" % grp, grid=(P // R, nb + 1),
        in_specs=_att_qkv_specs(R, nb) + [at_q, at_q, at_q],
        out_specs=[at_q, behind, behind],
        out_shape=[_sds((P, L, GW), F32)] * 3,
        scratch_shapes=[pltpu.VMEM(blk, F32)] * 4
        + [pltpu.VMEM((n_ch, BLK, 2 * BLK), F32), pltpu.VMEM((n_ch, BLK, 2 * BLK), F32),
           pltpu.VMEM((n_ch, BLK, 2 * BLK), CDT), pltpu.VMEM((n_ch, BLK, 2 * BLK), CDT)],
        compiler_params=_params(2),
    )(*[q3] * 7, _att_3d(lse, d, width), _att_3d(do_g, d, width), _att_3d(c_g, d, width))
    return [t.reshape(_rm_shape(S, d, GW)) for t in res]


def _mix_alpha(l0, l1, l2):
    mx = jnp.maximum(jnp.maximum(l0, l1), l2)
    e = [jnp.exp(l0 - mx), jnp.exp(l1 - mx), jnp.exp(l2 - mx)]
    tot = e[0] + e[1] + e[2]
    return [ei / tot for ei in e]


def _mix_fwd(S, os_, ls_, tm=512):
    dil = [d for _, d in ATT_GROUPS]

    def body(*refs):
        out, bufs = refs[6], refs[7:13]
        get = [_rm_reader(bufs[j], refs[j], dil[j % 3]) for j in range(6)]
        rows = slice(None)
        for s in range(ATT_HPG):
            al = _mix_alpha(*[get[3 + g](s, rows) for g in range(3)])
            mixed = al[0] * get[0](s, rows) + al[1] * get[1](s, rows) + al[2] * get[2](s, rows)
            out[:, s * HD:(s + 1) * HD] = mixed.astype(out.dtype)

    specs = [_rm_spec(tm, d, GW) for d in dil]
    return pl.pallas_call(
        body, name="mix_fwd", grid=(S // tm,), in_specs=specs * 2, out_specs=pl.BlockSpec((tm, GW), lambda i: (i, 0)),
        out_shape=_sds((S, GW), CDT), scratch_shapes=[pltpu.VMEM((ATT_HPG, tm, HD), F32)] * 6,
        compiler_params=_params(1))(*os_, *ls_)


def _mix_bwd(S, os_, ls_, do_a, tm=512):
    dil = [d for _, d in ATT_GROUPS]

    def body(*refs):
        d_ref, outs, bufs, tmps = refs[6], refs[7:13], refs[13:19], refs[19:25]
        get = [_rm_reader(bufs[j], refs[j], dil[j % 3]) for j in range(6)]
        for s in range(ATT_HPG):
            cols = slice(s * HD, (s + 1) * HD)

            def chunk(rows, s=s, cols=cols):
                al = _mix_alpha(*[get[3 + g](s, rows) for g in range(3)])
                dv = d_ref[rows, cols]
                o_a = al[0] * get[0](s, rows) + al[1] * get[1](s, rows) + al[2] * get[2](s, rows)
                dsum = jnp.sum(dv * o_a, axis=-1, keepdims=True)
                for g in range(3):
                    tmps[g][rows, :] = al[g] * dv
                    tmps[3 + g][rows, :] = -(al[g] * dsum)

            chunk(slice(None))
            for j in range(6):
                _rm_put(outs[j], cols, tmps[j], dil[j % 3])

    specs = [_rm_spec(tm, d, GW) for d in dil]
    res = pl.pallas_call(
        body, name="mix_bwd", grid=(S // tm,), in_specs=specs * 2 + [pl.BlockSpec((tm, GW), lambda i: (i, 0))],
        out_specs=specs * 2,
        out_shape=[_sds(_rm_shape(S, d, GW), CDT) for d in dil] + [_sds(_rm_shape(S, d, GW), F32) for d in dil],
        scratch_shapes=[pltpu.VMEM((ATT_HPG, tm, HD), F32)] * 6 + [pltpu.VMEM((tm, HD), F32)] * 6,
        compiler_params=_params(1))(*os_, *ls_, do_a)
    return res[:3], res[3:]


RET_FWD_CHUNKS = 4


def _ret_tables(dk):
    H, C = RET_HEADS, BLK
    log_g = jnp.log(1.0 - 2.0 ** (-5.0 - jnp.arange(H, dtype=F32)))
    idx = jnp.arange(C, dtype=F32)
    diff = idx[:, None] - idx[None, :]
    decay = jnp.where(diff >= 0, jnp.exp(log_g[:, None, None] * jnp.maximum(diff, 0.0)), 0.0)
    xi = jnp.exp(log_g[:, None] * (idx[None, :] + 1.0))
    zeta = jnp.exp(log_g[:, None] * (C - 1.0 - idx[None, :])) * (dk ** -0.5)
    g_chunk = jnp.exp(log_g * C)
    bc = lambda t: jnp.broadcast_to(t[:, :, None], (H, C, C))
    return decay, bc(xi), bc(zeta), jnp.broadcast_to(g_chunk[:, None, None], (H, 8, C))


def _gn_fwd(o, g, b):
    mu = jnp.mean(o, axis=-1, keepdims=True)
    xc = o - mu
    rstd = lax.rsqrt(jnp.mean(xc * xc, axis=-1, keepdims=True) + EPS)
    yh = xc * rstd
    return yh, rstd, yh * g + b


def _ret_specs(dk, dv, order, rows=BLK):
    H = RET_HEADS
    qk_w, v_w = H * dk, H * dv
    off_q = 3 * ATT_W
    off_k, off_v, off_g = off_q + qk_w, off_q + 2 * qk_w, off_q + 2 * qk_w + v_w
    assert 2 * dk == dv and all(off % dv == 0 for off in (off_q, off_k, off_v, off_g))

    def col(off, j):
        return pl.BlockSpec((rows, dv), lambda i: (order(i), off // dv + j))

    tab = pl.BlockSpec((H, BLK, BLK), lambda i: (0, 0, 0))
    return ([col(off_q, j) for j in range(H // 2)] + [col(off_k, j) for j in range(H // 2)]
            + [col(off_v, j) for j in range(H)] + [col(off_g, j) for j in range(H)]
            + [tab, tab, tab, pl.BlockSpec((H, 8, BLK), lambda i: (0, 0, 0))])


def _ret_heads(refs, dk):
    H = RET_HEADS
    q_refs, k_refs = refs[0:H // 2], refs[H // 2:H]
    v_refs, gr_refs = refs[H:2 * H], refs[2 * H:3 * H]

    def head(h, rows=slice(None)):
        cols = slice((h % 2) * dk, (h % 2 + 1) * dk)
        return q_refs[h // 2][rows, cols], k_refs[h // 2][rows, cols], v_refs[h][rows, :], gr_refs[h][rows, :]

    return head, refs[3 * H:3 * H + 4]


def _ret_fwd(proj, gn_g, gn_b, dk, dv):
    S = proj.shape[0]
    N = S // BLK
    H = RET_HEADS
    per = RET_FWD_CHUNKS
    assert N % per == 0
    kscale = dk ** -0.5
    n_in = 3 * H + 4

    def body(*refs):
        head, (dec_ref, xi_ref, zeta_ref, gc_ref) = _ret_heads(refs, dk)
        g_ref, b_ref, opre_ref, or_ref, st_ref, state, s_buf, cross_buf = refs[n_in:n_in + 8]
        n = pl.program_id(0)

        @pl.when(n == 0)
        def _():
            state[...] = jnp.zeros_like(state)

        for c in range(per):
            rows = slice(c * BLK, (c + 1) * BLK)
            for h in range(H):
                q, k, v, _ = head(h, rows)
                s_buf[h] = _dot_nt(q, k)
                st = state[h]
                st_c = st.astype(CDT)
                st_ref[c, h] = st_c
                cross_buf[h] = _dot_nn(q, st_c)
                kz = (k.astype(F32) * zeta_ref[h][:, 0:1]).astype(CDT)
                state[h] = st * gc_ref[h][0:1, 0:1] + _dot_tn(kz, v)
            for h in range(H):
                vs = slice(h * dv, (h + 1) * dv)
                _, _, v, gr = head(h, rows)
                s = s_buf[h] * kscale * dec_ref[h]
                o = _dot_nn(s.astype(CDT), v) + cross_buf[h] * xi_ref[h][:, 0:1]
                opre_ref[rows, vs] = o
                _, _, y = _gn_fwd(o, g_ref[:, vs], b_ref[:, vs])
                gr = gr.astype(F32)
                or_ref[rows, vs] = (y * (gr * _sigmoid(gr))).astype(or_ref.dtype)

    v_w = H * dv
    row = pl.BlockSpec((1, v_w), lambda i: (0, 0))
    tile = pl.BlockSpec((per * BLK, v_w), lambda i: (i, 0))
    return pl.pallas_call(
        body, name="ret_fwd", grid=(N // per,),
        in_specs=_ret_specs(dk, dv, lambda i: i, per * BLK) + [row, row],
        out_specs=[tile, tile, pl.BlockSpec((per, H, dk, dv), lambda i: (i, 0, 0, 0))],
        out_shape=[_sds((S, v_w), F32), _sds((S, v_w), CDT), _sds((N, H, dk, dv), CDT)],
        scratch_shapes=[pltpu.VMEM((H, dk, dv), F32), pltpu.VMEM((H, BLK, BLK), F32), pltpu.VMEM((H, BLK, dv), F32)],
        compiler_params=_params(1),
    )(*[proj] * (3 * H), *_ret_tables(dk), gn_g, gn_b)


def _ret_bwd(proj, gn_g, gn_b, o_pre, states, d_or, dga, dgb, dk, dv):
    S, in_w = proj.shape
    N = S // BLK
    H = RET_HEADS
    qk_w, v_w = H * dk, H * dv
    kscale = dk ** -0.5
    n_in = 3 * H + 4
    out_w = 2 * qk_w + 2 * v_w
    gate_w = dga.shape[1]
    col0 = 3 * ATT_W
    assert col0 + out_w + 2 * gate_w == in_w
    rev = lambda i: N - 1 - i

    def body(*refs):
        head, (dec_ref, xi_ref, zeta_ref, gc_ref) = _ret_heads(refs, dk)
        (g_ref, b_ref, opre_ref, st_ref, dor_ref, dga_ref, dgb_ref, dproj_ref, dg_ref, db_ref, dstate, stage,
         sem, do_buf, dox_buf, a_buf, g_buf, dq_buf, dk_buf, dv_buf) = refs[n_in:n_in + 20]
        i = pl.program_id(0)
        slot = i % 2
        out_ref = stage.at[slot]

        def out_copy(s, step):
            rows = pl.ds(pl.multiple_of(rev(step) * BLK, BLK), BLK)
            return pltpu.make_async_copy(stage.at[s], dproj_ref.at[rows, pl.ds(col0, in_w - col0)], sem.at[s])

        @pl.when(i >= 2)
        def _():
            out_copy(slot, i - 2).wait()

        @pl.when(i == 0)
        def _():
            dstate[...] = jnp.zeros_like(dstate)
            dg_ref[...] = jnp.zeros_like(dg_ref)
            db_ref[...] = jnp.zeros_like(db_ref)

        out_ref[:, out_w:out_w + gate_w] = dga_ref[...]
        out_ref[:, out_w + gate_w:out_w + 2 * gate_w] = dgb_ref[...]
        for h in range(H):
            vs = slice(h * dv, (h + 1) * dv)
            _, _, _, gr = head(h)
            gr = gr.astype(F32)
            sg = _sigmoid(gr)
            gain = g_ref[:, vs]
            yh, rstd, y = _gn_fwd(opre_ref[:, vs], gain, b_ref[:, vs])
            d_or_v = dor_ref[:, vs]
            dy = d_or_v * (gr * sg)
            out_ref[:, 2 * qk_w + v_w + h * dv:2 * qk_w + v_w + (h + 1) * dv] = (
                d_or_v * y * (sg * (1.0 + gr * (1.0 - sg)))).astype(out_ref.dtype)
            dg_ref[:, vs] += jnp.sum(dy * yh, axis=0, keepdims=True)
            db_ref[:, vs] += jnp.sum(dy, axis=0, keepdims=True)
            dyh = dy * gain
            do = rstd * (dyh - jnp.mean(dyh, axis=-1, keepdims=True)
                         - yh * jnp.mean(dyh * yh, axis=-1, keepdims=True))
            do_buf[h] = do.astype(CDT)
            dox_buf[h] = (do * xi_ref[h][:, 0:1]).astype(CDT)
        for h in range(H):
            q, k, v, _ = head(h)
            dox = dox_buf[h]
            a_buf[h] = _dot_nt(q, k)
            g_buf[h] = _dot_nt(do_buf[h], v)
            dsn = dstate[h]
            dsn_c = dsn.astype(CDT)
            kz = (k.astype(F32) * zeta_ref[h][:, 0:1]).astype(CDT)
            dq_buf[h] = _dot_nt(dox, st_ref[h])
            dk_buf[h] = _dot_nt(v, dsn_c)
            dv_buf[h] = _dot_nn(kz, dsn_c)
            dstate[h] = dsn * gc_ref[h][0:1, 0:1] + _dot_tn(q, dox)
        for h in range(H):
            q, k, _, _ = head(h)
            decay = dec_ref[h]
            a_c = (a_buf[h] * kscale * decay).astype(CDT)
            g_c = (g_buf[h] * decay).astype(CDT)
            dq = _dot_nn(g_c, k) * kscale + dq_buf[h]
            dkk = _dot_tn(g_c, q) * kscale + dk_buf[h] * zeta_ref[h][:, 0:1]
            dvv = _dot_tn(a_c, do_buf[h]) + dv_buf[h]
            out_ref[:, h * dk:(h + 1) * dk] = dq.astype(out_ref.dtype)
            out_ref[:, qk_w + h * dk:qk_w + (h + 1) * dk] = dkk.astype(out_ref.dtype)
            out_ref[:, 2 * qk_w + h * dv:2 * qk_w + (h + 1) * dv] = dvv.astype(out_ref.dtype)

        cp = out_copy(slot, i)
        cp.start()

        @pl.when(i == N - 1)
        def _():
            cp.wait()
            if N >= 2:
                out_copy(1 - slot, i - 1).wait()

    row = pl.BlockSpec((1, v_w), lambda i: (0, 0))
    tile = pl.BlockSpec((BLK, v_w), lambda i: (rev(i), 0))
    gate = pl.BlockSpec((BLK, gate_w), lambda i: (rev(i), 0))
    return pl.pallas_call(
        body, name="ret_bwd", grid=(N,),
        in_specs=_ret_specs(dk, dv, rev) + [row, row, tile,
                 pl.BlockSpec((None, H, dk, dv), lambda i: (rev(i), 0, 0, 0)), tile, gate, gate],
        out_specs=[pl.BlockSpec(memory_space=pl.ANY), row, row],
        out_shape=[_sds((S, in_w), CDT), _sds((1, v_w), F32), _sds((1, v_w), F32)],
        scratch_shapes=[pltpu.VMEM((H, dk, dv), F32), pltpu.VMEM((2, BLK, in_w - col0), CDT),
                        pltpu.SemaphoreType.DMA((2,)),
                        pltpu.VMEM((H, BLK, dv), CDT), pltpu.VMEM((H, BLK, dv), CDT),
                        pltpu.VMEM((H, BLK, BLK), F32), pltpu.VMEM((H, BLK, BLK), F32),
                        pltpu.VMEM((H, BLK, dk), F32), pltpu.VMEM((H, BLK, dk), F32), pltpu.VMEM((H, BLK, dv), F32)],
        compiler_params=_params(1),
    )(*[proj] * (3 * H), *_ret_tables(dk), gn_g, gn_b, o_pre, states, d_or, dga, dgb)


def _merge_fwd(o_a, o_r, wa, wb, proj, d_model, tm=1024, tn=512):
    S, in_w = proj.shape
    off_a, off_b = in_w - 2 * d_model, in_w - d_model
    assert off_a % tn == 0 and off_b % tn == 0

    def body(oa_ref, or_ref, wa_ref, wb_ref, ga_ref, gb_ref, y_ref, pa_ref, pb_ref):
        pa = _dot_nn(oa_ref[...], wa_ref[...])
        pb = _dot_nn(or_ref[...], wb_ref[...])
        y = _sigmoid(ga_ref[...].astype(F32)) * pa + _sigmoid(gb_ref[...].astype(F32)) * pb
        y_ref[...] = y.astype(y_ref.dtype)
        pa_ref[...] = pa.astype(pa_ref.dtype)
        pb_ref[...] = pb.astype(pb_ref.dtype)

    ka, kb = o_a.shape[1], o_r.shape[1]
    out = pl.BlockSpec((tm, tn), lambda i, j: (i, j))
    return pl.pallas_call(
        body, name="merge_fwd", grid=(S // tm, d_model // tn),
        in_specs=[pl.BlockSpec((tm, ka), lambda i, j: (i, 0)), pl.BlockSpec((tm, kb), lambda i, j: (i, 0)),
                  pl.BlockSpec((ka, tn), lambda i, j: (0, j)), pl.BlockSpec((kb, tn), lambda i, j: (0, j)),
                  pl.BlockSpec((tm, tn), lambda i, j: (i, off_a // tn + j)),
                  pl.BlockSpec((tm, tn), lambda i, j: (i, off_b // tn + j))],
        out_specs=[out, out, out], out_shape=[_sds((S, d_model), CDT)] * 3,
        compiler_params=_params(2))(o_a, o_r, wa, wb, proj, proj)


def _d_x(dproj, w_in, x, g, dx1, dep, tm=512, row_groups=2):
    S, D = x.shape
    n_sh, _, ns = w_in.shape
    nt = S // tm // row_groups
    deps = [d for d in (dep,) if d is not None]

    def body(a_ref, b_ref, x_ref, g_ref, r_ref, *rest):
        dx_ref, dg_ref, acc = rest[len(deps):]
        h, k, i = pl.program_id(0), pl.program_id(1), pl.program_id(2)

        @pl.when(k == 0)
        def _():
            acc[i] = jnp.zeros((tm, D), F32)

        acc[i] += _dot_nt(a_ref[...], b_ref[...])

        @pl.when(k == n_sh - 1)
        def _():
            _ep_rms_bwd(acc[i], (x_ref, g_ref, r_ref), (dx_ref, dg_ref), h * nt + i)

    def last_only(h, k, i):
        return (h * nt + jnp.where(k == n_sh - 1, i, 0), 0)

    return pl.pallas_call(
        body, name="d_x", grid=(row_groups, n_sh, nt),
        in_specs=[pl.BlockSpec((tm, ns), lambda h, k, i: (h * nt + i, k)),
                  pl.BlockSpec((None, D, ns), lambda h, k, i: (k, 0, 0)),
                  pl.BlockSpec((tm, D), last_only), pl.BlockSpec((1, D), lambda h, k, i: (0, 0)),
                  pl.BlockSpec((tm, D), last_only)] + [pl.BlockSpec(memory_space=pl.ANY)] * len(deps),
        out_specs=[pl.BlockSpec((tm, D), last_only), pl.BlockSpec((1, D), lambda h, k, i: (0, 0))],
        out_shape=[_sds((S, D), F32), _sds((1, D), F32)],
        scratch_shapes=[pltpu.VMEM((nt, tm, D), F32)],
        compiler_params=_params(3))(dproj, w_in, x, g, dx1, *deps)


def _local_step(x, target, w_in_mine, chip, others, near_w_in, far_w_in, small, late_weights, on_grads, deps0=()):
    S, D = x.shape
    ns_in = w_in_mine.shape[1]
    in_w = N_CHIPS * ns_in
    d_ff = 4 * D
    ret_v_w = 2 * D
    dv = ret_v_w // RET_HEADS
    dk = (in_w - 3 * ATT_W - 2 * ret_v_w - 2 * D) // (2 * RET_HEADS)
    gqk = jnp.concatenate([small["q_norm_g"].reshape(1, ATT_W), small["k_norm_g"].reshape(1, ATT_W)], axis=1)
    g1, g2 = small["norm1_g"], small["norm2_g"]
    gn_g, gn_b = small["ret_gn_g"], small["ret_gn_b"]

    proj_sds = _sds((S, in_w), CDT)
    xn, proj = _in_proj_mine(x, g1, w_in_mine, chip, proj_sds, deps0)
    for stage, (get_w_in, chips) in enumerate(((near_w_in, others[:2]), (far_w_in, others[2:]))):
        w_in, started = get_w_in(proj)
        (proj,) = _matmul(
            "in_proj_far%d" % stage, "nn", xn, w_in, tm=1024, tn=ns_in, tk=D, prefetch=[chips],
            n_cols=chips.shape[0] * ns_in, b_spec=pl.BlockSpec((None, D, ns_in), lambda i, j, k, o: (o[j], 0, 0)),
            outs=[(proj_sds, pl.BlockSpec((1024, ns_in), lambda i, j, k, o: (i, o[j])))], epilogue=_ep_store,
            deps=[proj, started], alias_dep_to_out=(0, 0), j_outer=True)
    qkv = _qknorm_fwd(proj, gqk)
    att = [_att_fwd(g, S, qkv[g]) for g in range(3)]
    os_, ls_ = [a[0] for a in att], [a[1] for a in att]
    o_a = _mix_fwd(S, os_, ls_)
    o_pre, o_r, states = _ret_fwd(proj, gn_g, gn_b, dk, dv)
    w = late_weights(o_r)
    y, pa, pb = _merge_fwd(o_a, o_r, w["w_proj_a"], w["w_proj_b"], proj, D)
    x1, xn2 = _matmul("out_proj", "nn", y, w["w_out"], tm=1024, tn=D, tk=D,
                      extras=[(x, _mn(1024, D)), (g2, _row(D))],
                      outs=[(_sds((S, D), F32), _mn(1024, D)), (_sds((S, D), CDT), _mn(1024, D))],
                      epilogue=_ep_resid_norm)
    hid, act = _matmul("mlp_up", "nn", xn2, w["w_up"], tm=1024, tn=2048, tk=D, j_outer=True,
                       outs=[(_sds((S, d_ff), CDT), _mn(1024, 2048))] * 2, epilogue=_ep_up)
    dx2, dx2c, loss_row = _matmul(
        "mlp_down_loss", "nn", act, w["w_down"], tm=512, tn=D, tk=d_ff,
        extras=[(x1, _mn(512, D)), (target, _mn(512, D))],
        outs=[(_sds((S, D), F32), _mn(512, D)), (_sds((S, D), CDT), _mn(512, D)), (_sds((1, D), F32), _row(D))],
        epilogue=functools.partial(_ep_down_loss, inv_d=1.0 / D))
    loss = 0.5 * jnp.sum(loss_row) / D

    (dh,) = _matmul("d_hidden", "nt", dx2c, w["w_down"], tm=1024, tn=2048, tk=D, j_outer=True,
                    extras=[(hid, _mn(1024, 2048))], outs=[(_sds((S, d_ff), CDT), _mn(1024, 2048))],
                    epilogue=_ep_dh)
    (gw_down,) = _matmul("dw_down", "tn", act, dx2c, tm=1024, tn=D, tk=2048,
                         outs=[(_sds((d_ff, D), F32), _mn(1024, D))], epilogue=_ep_store)
    (gw_up,) = _matmul("dw_up", "tn", xn2, dh, tm=D, tn=1024, tk=2048,
                       outs=[(_sds((D, d_ff), F32), _mn(D, 1024))], epilogue=_ep_store)
    tok = on_grads({"w_down": gw_down, "w_up": gw_up})
    dx1, dx1c, dg2 = _matmul(
        "d_x1", "nt", dh, w["w_up"], tm=512, tn=D, tk=d_ff,
        extras=[(x1, _mn(512, D)), (g2, _row(D)), (dx2, _mn(512, D))],
        outs=[(_sds((S, D), F32), _mn(512, D)), (_sds((S, D), CDT), _mn(512, D)), (_sds((1, D), F32), _row(D))],
        epilogue=_ep_rms_bwd, deps=[tok])

    gt = 512
    assert (in_w - 2 * D) % gt == 0
    off_a, off_b = (in_w - 2 * D) // gt, (in_w - D) // gt
    dpa, dpb, dga, dgb = _matmul(
        "d_gates", "nt", dx1c, w["w_out"], tm=1024, tn=gt, tk=D,
        extras=[(proj, _mn(1024, gt, off_a)), (proj, _mn(1024, gt, off_b)), (pa, _mn(1024, gt)),
                (pb, _mn(1024, gt))],
        outs=[(_sds((S, D), CDT), _mn(1024, gt))] * 4, epilogue=_ep_gates)
    (gw_out,) = _matmul("dw_out", "tn", y, dx1c, tm=D, tn=D, tk=1024,
                        outs=[(_sds((D, D), F32), _mn(D, D))], epilogue=_ep_store)
    (gw_pa,) = _matmul("dw_proj_a", "tn", o_a, dpa, tm=GW, tn=D, tk=1024,
                       outs=[(_sds((GW, D), F32), _mn(GW, D))], epilogue=_ep_store)
    (gw_pb,) = _matmul("dw_proj_b", "tn", o_r, dpb, tm=1024, tn=D, tk=2048,
                       outs=[(_sds((ret_v_w, D), F32), _mn(1024, D))], epilogue=_ep_store)
    (do_a,) = _matmul("d_o_a", "nt", dpa, w["w_proj_a"], tm=1024, tn=GW, tk=D,
                      outs=[(_sds((S, GW), F32), _mn(1024, GW))], epilogue=_ep_store)
    tok = on_grads({"w_out": gw_out, "w_proj_a": gw_pa, "w_proj_b": gw_pb})
    (d_or,) = _matmul("d_o_r", "nt", dpb, w["w_proj_b"], tm=1024, tn=ret_v_w, tk=D,
                      outs=[(_sds((S, ret_v_w), F32), _mn(1024, ret_v_w))], epilogue=_ep_store, deps=[tok])

    dproj, dgn_g, dgn_b = _ret_bwd(proj, gn_g, gn_b, o_pre, states, d_or, dga, dgb, dk, dv)
    do_gs, c_gs = _mix_bwd(S, os_, ls_, do_a)
    datt_parts = [_att_bwd(g, S, qkv[g], ls_[g], do_gs[g], c_gs[g]) for g in range(3)]
    dproj, dgqk = _qknorm_bwd(proj, gqk, [p[0] for p in datt_parts], [p[1] for p in datt_parts],
                              [p[2] for p in datt_parts], dproj)

    (gw_in,) = _matmul(
        "dw_in", "tn", xn, dproj, tm=512, tn=ns_in, tk=1024,
        outs=[(_sds((N_CHIPS, D, ns_in), F32), pl.BlockSpec((None, 512, ns_in), lambda i, j, k: (j, i, 0)))],
        epilogue=_ep_store)
    tok = on_grads({"w_in": gw_in})
    grad_x, dg1 = _d_x(dproj, w_in, x, g1, dx1, tok)

    smallg = {"norm1_g": dg1, "q_norm_g": dgqk[:, :ATT_W], "k_norm_g": dgqk[:, ATT_W:],
              "ret_gn_g": dgn_g, "ret_gn_b": dgn_b, "norm2_g": dg2}
    return loss, grad_x, smallg


N_CHIPS = 4
N_DEV = 8


def _place():
    x, y, c = lax.axis_index("x"), lax.axis_index("y"), lax.axis_index("c")
    return x, y, c


def _other_chips(x, y):
    out = []
    for fx, fy in ((1, 0), (0, 1), (1, 1)):
        px = 1 - x if fx else x
        py = 1 - y if fy else y
        out.append(((px, py), 2 * px + py))
    return out


SEM_SPEC = pl.BlockSpec(memory_space=pltpu.SEMAPHORE)
ANY_SPEC = pl.BlockSpec(memory_space=pl.ANY)
EFFECT = pltpu.SideEffectType.DATAFLOW_SIDE_EFFECTING


def _ici_copies(kind, srcs, lands, send, recv, which=(0, 1, 2)):
    x, y, c = _place()
    me = 2 * x + y
    out = []
    for w, (s, l) in enumerate(zip(srcs, lands)):
        for j, ((px, py), pidx) in enumerate(_other_chips(x, y)):
            if j not in which:
                continue
            if kind == "gather":
                half = s.shape[0] // 2
                rows = pl.ds(c * half, half)
                src, dst_there, dst_here = s.at[rows, :], l.at[me, rows, :], l.at[pidx, rows, :]
            else:
                src, dst_there, dst_here = s.at[pidx], l.at[me], l.at[pidx]
            out.append((src, dst_there, dst_here, send.at[3 * w + j], recv.at[3 * w + j], (px, py, c)))
    return out


def _exchange_start(name, kind, srcs, land_shapes, which=(0, 1, 2), lands=None):
    n = len(srcs)
    if lands is None:
        lands = [lax.empty(shape, dtype) for shape, dtype in land_shapes]

    def body(*refs):
        src_refs, land_refs = refs[:n], refs[n:2 * n]
        send, recv = refs[2 * n], refs[2 * n + 1]
        token = refs[-1]
        for src, dst, _, ss, rs, dev in _ici_copies(kind, src_refs, land_refs, send, recv, which):
            pltpu.make_async_remote_copy(src_ref=src, dst_ref=dst, send_sem=ss, recv_sem=rs, device_id=dev,
                                         device_id_type=MESH).start()
        token[...] = jnp.zeros_like(token)

    thru = [pltpu.HBM(s.shape, s.dtype) for s in srcs] + [pltpu.HBM(shape, dtype) for shape, dtype in land_shapes]
    res = pl.pallas_call(
        body, name=name,
        out_shape=(pltpu.SemaphoreType.DMA((3 * n,)), pltpu.SemaphoreType.DMA((3 * n,)), *thru, _sds((8, LANES), F32)),
        in_specs=[HBM_SPEC] * (2 * n), out_specs=(SEM_SPEC, SEM_SPEC, *[HBM_SPEC] * (2 * n), VMEM_SPEC),
        input_output_aliases={i: 2 + i for i in range(2 * n)},
        compiler_params=pltpu.CompilerParams(has_side_effects=EFFECT),
    )(*[pltpu.with_memory_space_constraint(s, pltpu.HBM) for s in srcs],
      *[pltpu.with_memory_space_constraint(l, pltpu.HBM) for l in lands])
    return res[0], res[1], list(res[2:2 + n]), list(res[2 + n:2 + 2 * n]), res[-1]


def _exchange_wait(name, kind, send, recv, srcs, lands, after, which=(0, 1, 2)):
    n = len(srcs)

    def body(*refs):
        src_refs, land_refs = refs[:n], refs[n:2 * n]
        send_ref, recv_ref = refs[2 * n], refs[2 * n + 1]
        for src, _, dst, ss, rs, dev in _ici_copies(kind, src_refs, land_refs, send_ref, recv_ref, which):
            cp = pltpu.make_async_remote_copy(src_ref=src, dst_ref=dst, send_sem=ss, recv_sem=rs, device_id=dev,
                                              device_id_type=MESH)
            cp.wait_send()
            cp.wait_recv()

    thru = [pltpu.HBM(t.shape, t.dtype) for t in list(srcs) + list(lands)]
    res = pl.pallas_call(
        body, name=name, out_shape=thru,
        in_specs=[HBM_SPEC] * (2 * n) + [SEM_SPEC, SEM_SPEC, ANY_SPEC], out_specs=[HBM_SPEC] * (2 * n),
        input_output_aliases={i: i for i in range(2 * n)},
        compiler_params=pltpu.CompilerParams(has_side_effects=EFFECT),
    )(*srcs, *lands, send, recv, after)
    return list(res[:n]), list(res[n:])


PAIR_TILE_ELEMS = 1 << 20


def _pair_fill(name, gathered, mine, core, others, chip, write_mine=True):
    k, r, C = gathered.shape
    half = r // 2
    tr = _row_tile(half, C, PAIR_TILE_ELEMS, mult=16)
    nt = half // tr
    n_far = others.shape[0]

    def body(c_ref, o_ref, chip_ref, in_ref, mine_ref, out_ref, slot, send, recv):
        j = pl.program_id(0)
        _sibling_barrier((j == 0) & (pl.program_id(1) == 0))
        b = (j * nt + pl.program_id(1)) % 2
        x, y, c = _place()
        cp = pltpu.make_async_remote_copy(src_ref=in_ref, dst_ref=slot.at[b], send_sem=send.at[b],
                                          recv_sem=recv.at[b], device_id=(x, y, 1 - c), device_id_type=MESH)

        @pl.when(j < n_far)
        def _():
            cp.start()
            cp.wait_recv()
            out_ref[...] = slot[b]
            cp.wait_send()

        @pl.when(j >= n_far)
        def _():
            out_ref[...] = mine_ref[...]

    def far(j):
        return jnp.minimum(j, n_far - 1)

    grid_spec = pltpu.PrefetchScalarGridSpec(
        num_scalar_prefetch=3, grid=(n_far + (2 if write_mine else 0), nt),
        in_specs=[pl.BlockSpec((tr, C), lambda j, i, c, o, m: (
                      (2 * o[far(j)] + c[0]) * nt + jnp.where(j < n_far, i, nt - 1), 0)),
                  pl.BlockSpec((tr, C), lambda j, i, c, o, m: (jnp.where(j < n_far, 0, (j - n_far) * nt + i), 0))],
        out_specs=pl.BlockSpec((tr, C), lambda j, i, c, o, m: (
            jnp.where(j < n_far, 2 * o[far(j)] + 1 - c[0], 2 * m[0] + j - n_far) * nt + i, 0)),
        scratch_shapes=[pltpu.VMEM((2, tr, C), gathered.dtype), pltpu.SemaphoreType.DMA((2,)),
                        pltpu.SemaphoreType.DMA((2,))])
    out = pl.pallas_call(body, name=name, grid_spec=grid_spec, out_shape=_sds((k * r, C), gathered.dtype),
                         input_output_aliases={3: 0}, compiler_params=_params(2, PAIR_FILL_ID))(
                             core, others, chip, gathered.reshape(k * r, C), mine)
    return out.reshape(k, r, C)


def _pair_reduce(name, g, core):
    k, R, C = g.shape
    half = R // 2
    tr = _row_tile(half, C, PAIR_TILE_ELEMS, mult=16)
    nt = half // tr

    def body(c_ref, mine_ref, give_ref, out_ref, wire_ref, stage, slot, send, recv):
        _sibling_barrier((pl.program_id(0) == 0) & (pl.program_id(1) == 0))
        b = (pl.program_id(0) * nt + pl.program_id(1)) % 2
        x, y, c = _place()
        stage[b] = give_ref[...].astype(stage.dtype)
        cp = pltpu.make_async_remote_copy(src_ref=stage.at[b], dst_ref=slot.at[b], send_sem=send.at[b],
                                          recv_sem=recv.at[b], device_id=(x, y, 1 - c), device_id_type=MESH)
        cp.start()
        cp.wait_recv()
        tot = mine_ref[...] + slot[b].astype(F32)
        out_ref[...] = tot
        wire_ref[...] = tot.astype(wire_ref.dtype)
        cp.wait_send()

    blk = (tr, C)
    out_spec = pl.BlockSpec(blk, lambda s, i, c: (s * nt + i, 0))
    grid_spec = pltpu.PrefetchScalarGridSpec(
        num_scalar_prefetch=1, grid=(k, nt),
        in_specs=[pl.BlockSpec(blk, lambda s, i, c: ((2 * s + c[0]) * nt + i, 0)),
                  pl.BlockSpec(blk, lambda s, i, c: ((2 * s + 1 - c[0]) * nt + i, 0))],
        out_specs=[out_spec, out_spec],
        scratch_shapes=[pltpu.VMEM((2, tr, C), CDT), pltpu.VMEM((2, tr, C), CDT), pltpu.SemaphoreType.DMA((2,)),
                        pltpu.SemaphoreType.DMA((2,))])
    g2 = g.reshape(k * R, C)
    out, wire = pl.pallas_call(body, name=name, grid_spec=grid_spec,
                               out_shape=[_sds((k * half, C), F32), _sds((k * half, C), CDT)],
                               compiler_params=_params(2, PAIR_REDUCE_ID))(core, g2, g2)
    return out, wire.reshape(k, half, C)


def _all_reduce_small(v):
    r, cdim = v.shape

    def body(v_ref, o_ref, buf, send, recv):
        x, y, c = _place()
        me = 4 * x + 2 * y + c
        buf[me] = v_ref[...]
        sends = []
        for m in range(1, N_DEV):
            px = 1 - x if m & 4 else x
            py = 1 - y if m & 2 else y
            pc = 1 - c if m & 1 else c
            cp = pltpu.make_async_remote_copy(src_ref=v_ref, dst_ref=buf.at[me], send_sem=send.at[m - 1],
                                              recv_sem=recv.at[m - 1], device_id=(px, py, pc), device_id_type=MESH)
            cp.start()
            sends.append((cp, 4 * px + 2 * py + pc))
        for m, (cp, pidx) in enumerate(sends):
            pltpu.make_async_remote_copy(src_ref=v_ref, dst_ref=buf.at[pidx], send_sem=send.at[m], recv_sem=recv.at[m],
                                         device_id=(x, y, c), device_id_type=MESH).wait_recv()
        for cp, _ in sends:
            cp.wait_send()
        tot = buf[0]
        for k in range(1, N_DEV):
            tot = tot + buf[k]
        o_ref[...] = tot

    return pl.pallas_call(
        body, name="all_reduce_small", in_specs=[VMEM_SPEC], out_specs=VMEM_SPEC,
        out_shape=_sds((r, cdim), F32),
        scratch_shapes=[pltpu.VMEM((N_DEV, r, cdim), F32), pltpu.SemaphoreType.DMA((N_DEV - 1,)),
                        pltpu.SemaphoreType.DMA((N_DEV - 1,))],
    )(v)


def _row_tile(rows, cols, budget_elems=1 << 18, mult=8):
    if rows % mult:
        return rows
    t = max(mult, (budget_elems // cols) // mult * mult)
    while rows % t:
        t -= mult
    return t


def _adamw_update(w, g, m, v):
    nm = ADAM_B1 * m + (1.0 - ADAM_B1) * g
    nv = ADAM_B2 * v + (1.0 - ADAM_B2) * (g * g)
    m_hat = nm / (1.0 - ADAM_B1 ** ADAM_STEP)
    v_hat = nv / (1.0 - ADAM_B2 ** ADAM_STEP)
    return -ADAM_LR * (m_hat / (jnp.sqrt(v_hat) + ADAM_EPS) + ADAM_WD * w), nm, nv


def _adamw(name, w, g, m, v):
    R, C = w.shape
    tr = _row_tile(R, C, 1 << 19)

    def body(w_ref, g_ref, m_ref, v_ref, d_ref, nm_ref, nv_ref):
        d_ref[...], nm_ref[...], nv_ref[...] = _adamw_update(w_ref[...], g_ref[...], m_ref[...], v_ref[...])

    spec = pl.BlockSpec((tr, C), lambda i: (i, 0))
    return pl.pallas_call(body, name=name, grid=(R // tr,), in_specs=[spec] * 4, out_specs=[spec] * 3,
                          out_shape=[_sds((R, C), F32)] * 3, compiler_params=_params(1))(w, g, m, v)


def _sum_share(name, own, by_chip, chip, others, core):
    k, half, C = by_chip.shape
    tr = _row_tile(half, C, PAIR_TILE_ELEMS // 2, mult=16)
    nt = half // tr

    def body(chip_ref, oth_ref, c_ref, own_ref, a_ref, b_ref, cc_ref, g_out, mine, slot, send, recv):
        p = pl.program_id(1)
        _sibling_barrier((pl.program_id(0) == 0) & (p == 0))
        b = pl.program_id(0) % 2
        x, y, c = _place()
        cp = pltpu.make_async_remote_copy(src_ref=mine.at[b], dst_ref=slot.at[b], send_sem=send.at[b],
                                          recv_sem=recv.at[b], device_id=(x, y, 1 - c), device_id_type=MESH)

        @pl.when(p == 0)
        def _():
            tot = ((own_ref[...] + a_ref[...].astype(F32)) + b_ref[...].astype(F32)) + cc_ref[...].astype(F32)
            mine[b] = tot
            cp.start()
            g_out[...] = tot

        @pl.when(p == 1)
        def _():
            cp.wait_recv()
            g_out[...] = slot[b]
            cp.wait_send()

    def piece(j):
        return pl.BlockSpec((tr, C), lambda i, p, chip, oth, c: (oth[j] * nt + i, 0))

    grid_spec = pltpu.PrefetchScalarGridSpec(
        num_scalar_prefetch=3, grid=(nt, 2),
        in_specs=[pl.BlockSpec((tr, C), lambda i, p, chip, oth, c: (chip[0] * nt + i, 0)),
                  piece(0), piece(1), piece(2)],
        out_specs=pl.BlockSpec((tr, C), lambda i, p, chip, oth, c: (
            jnp.where(p == 0, c[0], 1 - c[0]) * nt + i, 0)),
        scratch_shapes=[pltpu.VMEM((2, tr, C), F32), pltpu.VMEM((2, tr, C), F32), pltpu.SemaphoreType.DMA((2,)),
                        pltpu.SemaphoreType.DMA((2,))])
    by2 = by_chip.reshape(k * half, C)
    return pl.pallas_call(body, name=name, grid_spec=grid_spec, out_shape=_sds((2 * half, C), F32),
                          compiler_params=_params(2, SUM_SHARE_ID))(chip, others, core, own, by2, by2, by2)


BIG = ("w_in", "w_proj_a", "w_proj_b", "w_out", "w_up", "w_down")
COL_SHARDED = ("w_in", "w_proj_a", "w_up")
SMALL = ("norm1_g", "q_norm_g", "k_norm_g", "ret_gn_g", "ret_gn_b", "norm2_g")
ALL_W = ("norm1_g", "w_in", "q_norm_g", "k_norm_g", "ret_gn_g", "ret_gn_b", "w_proj_a", "w_proj_b", "w_out",
         "norm2_g", "w_up", "w_down")
LANES = 128


def _to_full(name, gathered):
    k, r, c = gathered.shape
    if name in COL_SHARDED:
        return gathered.transpose(1, 0, 2).reshape(r, k * c)
    return gathered.reshape(k * r, c)


def _to_shard_major(name, full):
    if name in COL_SHARDED:
        r, c4 = full.shape
        return full.reshape(r, N_CHIPS, c4 // N_CHIPS).transpose(1, 0, 2)
    r4, c = full.shape
    return full.reshape(N_CHIPS, r4 // N_CHIPS, c)


def kernel(x, norm1_g, w_in, q_norm_g, k_norm_g, ret_gn_g, ret_gn_b, w_proj_a, w_proj_b, w_out, norm2_g, w_up, w_down, loss_target, m_norm1_g, m_w_in, m_q_norm_g, m_k_norm_g, m_ret_gn_g, m_ret_gn_b, m_w_proj_a, m_w_proj_b, m_w_out, m_norm2_g, m_w_up, m_w_down, v_norm1_g, v_w_in, v_q_norm_g, v_k_norm_g, v_ret_gn_g, v_ret_gn_b, v_w_proj_a, v_w_proj_b, v_w_out, v_norm2_g, v_w_up, v_w_down):
    weights = dict(norm1_g=norm1_g, w_in=w_in, q_norm_g=q_norm_g, k_norm_g=k_norm_g, ret_gn_g=ret_gn_g,
                   ret_gn_b=ret_gn_b, w_proj_a=w_proj_a, w_proj_b=w_proj_b, w_out=w_out, norm2_g=norm2_g,
                   w_up=w_up, w_down=w_down)
    moments_m = dict(norm1_g=m_norm1_g, w_in=m_w_in, q_norm_g=m_q_norm_g, k_norm_g=m_k_norm_g, ret_gn_g=m_ret_gn_g,
                     ret_gn_b=m_ret_gn_b, w_proj_a=m_w_proj_a, w_proj_b=m_w_proj_b, w_out=m_w_out,
                     norm2_g=m_norm2_g, w_up=m_w_up, w_down=m_w_down)
    moments_v = dict(norm1_g=v_norm1_g, w_in=v_w_in, q_norm_g=v_q_norm_g, k_norm_g=v_k_norm_g, ret_gn_g=v_ret_gn_g,
                     ret_gn_b=v_ret_gn_b, w_proj_a=v_w_proj_a, w_proj_b=v_w_proj_b, w_out=v_w_out,
                     norm2_g=v_norm2_g, w_up=v_w_up, w_down=v_w_down)

    mx, my = lax.axis_index("x"), lax.axis_index("y")
    core = lax.axis_index("c").astype(jnp.int32).reshape(1)
    chip = (2 * mx + my).astype(jnp.int32).reshape(1)
    others = jnp.stack([2 * (1 - mx) + my, 2 * mx + 1 - my, 2 * (1 - mx) + 1 - my]).astype(jnp.int32)
    shards = {n: weights[n][0].astype(CDT) for n in BIG}
    def start_gather(name, names):
        return _exchange_start(name, "gather", [shards[n] for n in names],
                               [((N_CHIPS,) + shards[n].shape, CDT) for n in names])

    w_in_shape = [((N_CHIPS,) + shards["w_in"].shape, CDT)]
    n_send, n_recv, n_srcs, n_lands, n_token = _exchange_start(
        "gather_w_in_near_start", "gather", [shards["w_in"]], w_in_shape, which=(0, 1))
    late = [n for n in BIG if n != "w_in"]
    flight = {}

    def near_w_in(after):
        srcs, lands = _exchange_wait("gather_w_in_near_wait", "gather", n_send, n_recv, n_srcs, n_lands, after,
                                     which=(0, 1))
        d_send, d_recv, d_srcs, d_lands, _ = _exchange_start(
            "gather_w_in_diag_start", "gather", srcs, w_in_shape, which=(2,), lands=lands)
        flight["late"] = start_gather("gather_late_start", late)
        w_near = _pair_fill("pair_fill_w_in_near", d_lands[0], d_srcs[0], core, others[:2], chip)
        flight["diag"] = (d_send, d_recv, d_srcs, [w_near])
        return w_near, flight["late"][-1]

    def far_w_in(after):
        d_send, d_recv, d_srcs, d_lands = flight["diag"]
        srcs, lands = _exchange_wait("gather_w_in_diag_wait", "gather", d_send, d_recv, d_srcs, d_lands, after,
                                     which=(2,))
        return _pair_fill("pair_fill_w_in_diag", lands[0], srcs[0], core, others[2:], chip, write_mine=False), None

    def late_weights(after):
        l_send, l_recv, l_srcs, l_lands, _ = flight["late"]
        srcs, lands = _exchange_wait("gather_late_wait", "gather", l_send, l_recv, l_srcs, l_lands, after)
        out = {}
        for n, mine, land in zip(late, srcs, lands):
            out[n] = _to_full(n, _pair_fill("pair_fill_%s" % n, land, mine, core, others, chip))
        return out

    pending = []

    def on_grads(group):
        names = list(group)
        red = [_pair_reduce("pair_reduce_%s" % n, g if g.ndim == 3 else _to_shard_major(n, g), core)
               for n, g in group.items()]
        wires = [wire for _, wire in red]
        send, recv, srcs, lands, token = _exchange_start(
            "scatter_start_%s" % names[0], "scatter", wires, [(wire.shape, wire.dtype) for wire in wires])
        pending.append((names, [own for own, _ in red], send, recv, srcs, lands))
        return token

    small = {n: weights[n].reshape(1, -1) for n in SMALL}

    loss, grad_x, small_g = _local_step(x[0], loss_target[0], n_srcs[0], chip, others, near_w_in, far_w_in, small,
                                        late_weights, on_grads, deps0=[n_token])

    out_g, out_d, out_m, out_v = {}, {}, {}, {}
    for names, owns, send, recv, srcs, lands in pending:
        _, got = _exchange_wait("scatter_wait_%s" % names[0], "scatter", send, recv, srcs, lands, grad_x)
        for n, own, by_chip in zip(names, owns, got):
            shape = weights[n].shape
            g2 = _sum_share("sum_share_%s" % n, own, by_chip, chip, others, core)
            d, nm, nv = _adamw("adamw_%s" % n, weights[n][0], g2, moments_m[n][0], moments_v[n][0])
            out_g[n], out_d[n], out_m[n], out_v[n] = (t.reshape(shape) for t in (g2, d, nm, nv))

    packed = jnp.concatenate([small_g[n].reshape(1, -1) for n in SMALL], axis=1).reshape(-1, LANES)
    loss_tile = jnp.zeros((8, LANES), F32).at[0, 0].set(loss)
    red = _all_reduce_small(jnp.concatenate([packed, loss_tile], axis=0))
    loss = red[packed.shape[0], 0]
    red = red[:packed.shape[0]].reshape(1, -1)
    off = 0
    for n in SMALL:
        shape = weights[n].shape
        row = (1, weights[n].size)
        g2 = red[:, off:off + row[1]]
        off += row[1]
        d, nm, nv = _adamw("adamw_%s" % n, weights[n].reshape(row), g2, moments_m[n].reshape(row),
                           moments_v[n].reshape(row))
        out_g[n], out_d[n], out_m[n], out_v[n] = (t.reshape(shape) for t in (g2, d, nm, nv))

    return (loss, grad_x[None], *[out_g[n] for n in ALL_W], *[out_d[n] for n in ALL_W],
            *[out_m[n] for n in ALL_W], *[out_v[n] for n in ALL_W])
```
